```python
import jax, jax.numpy as jnp
from jax import lax
import numpy as np

D_MODEL = 2048
BATCH = 8
SEQ = 4096
DEPTH = 1

A_HEAD_DIM = 128
A_WIDTH = D_MODEL // 2
A_HEADS = A_WIDTH // A_HEAD_DIM
A_CHUNK = 64
B_HEAD_DIM = 64
B_WIDTH = D_MODEL // 2
B_Q_HEADS = B_WIDTH // B_HEAD_DIM
B_GROUP = 4
B_KV_HEADS = B_Q_HEADS // B_GROUP
B_KV_WIDTH = B_KV_HEADS * B_HEAD_DIM
WINDOW = 128
BLOCK = 128
MLP_HIDDEN = 4 * D_MODEL
N_MOD = 6
EPS = 1e-6

SPLIT_SIZES = (A_WIDTH, A_WIDTH, A_WIDTH, A_WIDTH,
               B_WIDTH, B_KV_WIDTH, B_KV_WIDTH,
               D_MODEL, D_MODEL)
IN_WIDTH = 4 * A_WIDTH + B_WIDTH + 2 * B_KV_WIDTH + 2 * D_MODEL

kernel_name = "hybrid_hgrn2_swa_sink_gated_block"


def split_columns(t):
    idx, acc = [], 0
    for s in SPLIT_SIZES[:-1]:
        acc += s
        idx.append(acc)
    return jnp.split(t, idx, axis=-1)


def rms_norm(x, gain):
    xf = x.astype(jnp.float32)
    y = xf * lax.rsqrt(jnp.mean(xf * xf, axis=-1, keepdims=True) + EPS)
    return (y * gain.astype(jnp.float32)).astype(x.dtype)


def head_rms(t, gain):
    return t * lax.rsqrt(jnp.mean(t * t, axis=-1, keepdims=True) + EPS) * gain.astype(jnp.float32)


def hgrn2_mixer(q, f_logit, i, g, lb, o_gain):
    f32 = jnp.float32
    bsz, seq, _ = q.shape
    H, K, C = A_HEADS, A_HEAD_DIM, A_CHUNK
    n = seq // C
    lbf = lb.astype(f32)
    f = lbf + (1.0 - lbf) * jax.nn.sigmoid(f_logit.astype(f32))
    log_f = jnp.log(f)
    k = 1.0 - f
    qf = jax.nn.silu(q.astype(f32))

    def to_chunks(t):
        return t.reshape(bsz, n, C, H, K).transpose(0, 3, 1, 2, 4)

    qc, kc, vc, lfc = (to_chunks(t) for t in (qf, k, i.astype(f32), log_f))
    b = jnp.cumsum(lfc, axis=3)
    b_mid = b[:, :, :, C // 2 - 1:C // 2, :]
    b_last = b[:, :, :, C - 1:C, :]
    q_dec = qc * jnp.exp(b - b_mid)
    k_dec = kc * jnp.exp(b_mid - b)
    causal = jnp.tril(jnp.ones((C, C), dtype=bool))
    scores = jnp.where(causal, jnp.einsum('bhntk,bhnsk->bhnts', q_dec, k_dec), 0.0)
    o_intra = jnp.einsum('bhnts,bhnsv->bhntv', scores, vc)
    d_state = jnp.einsum('bhnsk,bhnsv->bhnkv', kc * jnp.exp(b_last - b), vc)
    chunk_decay = jnp.exp(b_last[:, :, :, 0, :])

    def step(state, inp):
        ds, dec = inp
        return dec[..., None] * state + ds, state

    s0 = jnp.zeros((bsz, H, K, K), f32)
    _, s_prev = lax.scan(step, s0, (jnp.moveaxis(d_state, 2, 0), jnp.moveaxis(chunk_decay, 2, 0)))
    s_prev = jnp.moveaxis(s_prev, 0, 2)
    o_inter = jnp.einsum('bhntk,bhnkv->bhntv', qc * jnp.exp(b), s_prev)
    o = (o_intra + o_inter).transpose(0, 2, 3, 1, 4).reshape(bsz, seq, H, K)
    o = head_rms(o, o_gain.reshape(H, K)).reshape(bsz, seq, A_WIDTH)
    o = o * jax.nn.silu(g.astype(f32))
    return o.astype(q.dtype)


def swa_sink_attention(q, k, v, q_gain, k_gain, sinks):
    f32 = jnp.float32
    bsz, seq, _ = q.shape
    nb = seq // BLOCK
    qh = head_rms(q.astype(f32).reshape(bsz, seq, B_KV_HEADS, B_GROUP, B_HEAD_DIM), q_gain)
    kh = head_rms(k.astype(f32).reshape(bsz, seq, B_KV_HEADS, B_HEAD_DIM), k_gain)
    vh = v.astype(f32).reshape(bsz, seq, B_KV_HEADS, B_HEAD_DIM)
    qb = qh.reshape(bsz, nb, BLOCK, B_KV_HEADS, B_GROUP, B_HEAD_DIM)
    kb = kh.reshape(bsz, nb, BLOCK, B_KV_HEADS, B_HEAD_DIM)
    vb = vh.reshape(bsz, nb, BLOCK, B_KV_HEADS, B_HEAD_DIM)

    def with_prev(t):
        prev = jnp.concatenate([jnp.zeros_like(t[:, :1]), t[:, :-1]], axis=1)
        return jnp.concatenate([prev, t], axis=2)

    kw, vw = with_prev(kb), with_prev(vb)
    scale = B_HEAD_DIM ** -0.5
    scores = jnp.einsum('bnqhgd,bnkhd->bnhgqk', qb, kw) * scale
    qi = jnp.arange(BLOCK)[:, None] + BLOCK
    ki = jnp.arange(2 * BLOCK)[None, :]
    rel = qi - ki
    band = (rel >= 0) & (rel < WINDOW)
    has_key = (jnp.arange(nb) > 0)[:, None, None] | (ki >= BLOCK)[None]
    mask = band[None] & has_key
    scores = jnp.where(mask[None, :, None, None], scores, -jnp.inf)
    sink = jnp.broadcast_to(sinks.astype(f32).reshape(B_KV_HEADS, B_GROUP)[None, None, :, :, None, None],
                            scores.shape[:-1] + (1,))
    probs = jax.nn.softmax(jnp.concatenate([scores, sink], axis=-1), axis=-1)[..., :-1]
    out = jnp.einsum('bnhgqk,bnkhd->bnqhgd', probs, vw)
    return out.reshape(bsz, seq, B_WIDTH).astype(q.dtype)


def _fwd_setup_inputs(seed: int = 0) -> dict:
    key = jax.random.key(seed)
    ks = jax.random.split(key, 20)
    f32 = jnp.float32

    def w(k, shape, fan_in):
        return jax.random.normal(k, shape, f32) * (fan_in ** -0.5)

    return {
        "x": jax.random.normal(ks[0], (BATCH, SEQ, D_MODEL), f32),
        "c": jax.random.normal(ks[1], (BATCH, D_MODEL), f32),
        "w_ada": w(ks[2], (DEPTH, D_MODEL, N_MOD * D_MODEL), D_MODEL),
        "b_ada": 0.02 * jax.random.normal(ks[3], (DEPTH, N_MOD * D_MODEL), f32),
        "norm1_gain": 1.0 + 0.02 * jax.random.normal(ks[4], (DEPTH, D_MODEL), f32),
        "w_in": w(ks[5], (DEPTH, D_MODEL, IN_WIDTH), D_MODEL),
        "lb_logits": 0.5 * jax.random.normal(ks[6], (DEPTH + 1, A_WIDTH), f32),
        "hgrn_o_gain": 1.0 + 0.02 * jax.random.normal(ks[7], (DEPTH, A_WIDTH), f32),
        "q_norm_gain": 1.0 + 0.02 * jax.random.normal(ks[8], (DEPTH, B_HEAD_DIM), f32),
        "k_norm_gain": 1.0 + 0.02 * jax.random.normal(ks[9], (DEPTH, B_HEAD_DIM), f32),
        "sinks": 0.5 * jax.random.normal(ks[10], (DEPTH, B_Q_HEADS), f32),
        "w_branch_a": w(ks[11], (DEPTH, A_WIDTH, D_MODEL), A_WIDTH),
        "w_branch_b": w(ks[12], (DEPTH, B_WIDTH, D_MODEL), B_WIDTH),
        "w_out": w(ks[13], (DEPTH, D_MODEL, D_MODEL), D_MODEL),
        "norm2_gain": 1.0 + 0.02 * jax.random.normal(ks[14], (DEPTH, D_MODEL), f32),
        "w_mlp_in": w(ks[15], (DEPTH, D_MODEL, MLP_HIDDEN), D_MODEL),
        "w_mlp_out": w(ks[16], (DEPTH, MLP_HIDDEN, D_MODEL), MLP_HIDDEN),
    }


def _fwd_reference(x, c, w_ada, b_ada, norm1_gain, w_in, lb_logits, hgrn_o_gain, q_norm_gain,
              k_norm_gain, sinks, w_branch_a, w_branch_b, w_out, norm2_gain, w_mlp_in, w_mlp_out):
    lb_all = jnp.cumsum(jax.nn.softmax(lb_logits.astype(jnp.float32), axis=0), axis=0)
    for l in range(DEPTH):
        mod = jax.nn.silu(c) @ w_ada[l] + b_ada[l]
        sh1, sc1, gt1, sh2, sc2, gt2 = (m[:, None, :] for m in jnp.split(mod, N_MOD, axis=-1))
        h = rms_norm(x, norm1_gain[l]) * (1.0 + sc1) + sh1
        qa, fa, ia, ga, qb, kb, vb, gate_a, gate_b = split_columns(h @ w_in[l])
        ya = hgrn2_mixer(qa, fa, ia, ga, lb_all[l], hgrn_o_gain[l]) @ w_branch_a[l]
        yb = swa_sink_attention(qb, kb, vb, q_norm_gain[l], k_norm_gain[l], sinks[l]) @ w_branch_b[l]
        merged = jax.nn.sigmoid(gate_a) * ya + jax.nn.sigmoid(gate_b) * yb
        x = x + gt1 * (merged @ w_out[l])
        h2 = rms_norm(x, norm2_gain[l]) * (1.0 + sc2) + sh2
        x = x + gt2 * (jnp.square(jax.nn.relu(h2 @ w_mlp_in[l])) @ w_mlp_out[l])
    return x


import jax as _jax
import jax.numpy as _jnp

TWIN_FORMAT = 'train_step'
FWD_PARAMS = ['x', 'c', 'w_ada', 'b_ada', 'norm1_gain', 'w_in', 'lb_logits', 'hgrn_o_gain', 'q_norm_gain', 'k_norm_gain', 'sinks', 'w_branch_a', 'w_branch_b', 'w_out', 'norm2_gain', 'w_mlp_in', 'w_mlp_out']
TWIN_WEIGHTS = ['w_ada', 'b_ada', 'norm1_gain', 'w_in', 'lb_logits', 'hgrn_o_gain', 'q_norm_gain', 'k_norm_gain', 'sinks', 'w_branch_a', 'w_branch_b', 'w_out', 'norm2_gain', 'w_mlp_in', 'w_mlp_out']
TWIN_DIFF_INPUT = 'x'
TWIN_INPUTS = ['x', 'c', 'w_ada', 'b_ada', 'norm1_gain', 'w_in', 'lb_logits', 'hgrn_o_gain', 'q_norm_gain', 'k_norm_gain', 'sinks', 'w_branch_a', 'w_branch_b', 'w_out', 'norm2_gain', 'w_mlp_in', 'w_mlp_out', 'loss_target', 'm_w_ada', 'm_b_ada', 'm_norm1_gain', 'm_w_in', 'm_lb_logits', 'm_hgrn_o_gain', 'm_q_norm_gain', 'm_k_norm_gain', 'm_sinks', 'm_w_branch_a', 'm_w_branch_b', 'm_w_out', 'm_norm2_gain', 'm_w_mlp_in', 'm_w_mlp_out', 'v_w_ada', 'v_b_ada', 'v_norm1_gain', 'v_w_in', 'v_lb_logits', 'v_hgrn_o_gain', 'v_q_norm_gain', 'v_k_norm_gain', 'v_sinks', 'v_w_branch_a', 'v_w_branch_b', 'v_w_out', 'v_norm2_gain', 'v_w_mlp_in', 'v_w_mlp_out']
TWIN_OUTPUTS = ['loss', 'grad_x', 'grad_w_ada', 'grad_b_ada', 'grad_norm1_gain', 'grad_w_in', 'grad_lb_logits', 'grad_hgrn_o_gain', 'grad_q_norm_gain', 'grad_k_norm_gain', 'grad_sinks', 'grad_w_branch_a', 'grad_w_branch_b', 'grad_w_out', 'grad_norm2_gain', 'grad_w_mlp_in', 'grad_w_mlp_out', 'delta_w_ada', 'delta_b_ada', 'delta_norm1_gain', 'delta_w_in', 'delta_lb_logits', 'delta_hgrn_o_gain', 'delta_q_norm_gain', 'delta_k_norm_gain', 'delta_sinks', 'delta_w_branch_a', 'delta_w_branch_b', 'delta_w_out', 'delta_norm2_gain', 'delta_w_mlp_in', 'delta_w_mlp_out', 'new_m_w_ada', 'new_m_b_ada', 'new_m_norm1_gain', 'new_m_w_in', 'new_m_lb_logits', 'new_m_hgrn_o_gain', 'new_m_q_norm_gain', 'new_m_k_norm_gain', 'new_m_sinks', 'new_m_w_branch_a', 'new_m_w_branch_b', 'new_m_w_out', 'new_m_norm2_gain', 'new_m_w_mlp_in', 'new_m_w_mlp_out', 'new_v_w_ada', 'new_v_b_ada', 'new_v_norm1_gain', 'new_v_w_in', 'new_v_lb_logits', 'new_v_hgrn_o_gain', 'new_v_q_norm_gain', 'new_v_k_norm_gain', 'new_v_sinks', 'new_v_w_branch_a', 'new_v_w_branch_b', 'new_v_w_out', 'new_v_norm2_gain', 'new_v_w_mlp_in', 'new_v_w_mlp_out']
TWIN_LEAF_KINDS = {'loss': 'loss', 'grad_x': 'grad_x', 'grad_w_ada': 'grad_w', 'grad_b_ada': 'grad_w', 'grad_norm1_gain': 'grad_w', 'grad_w_in': 'grad_w', 'grad_lb_logits': 'grad_w', 'grad_hgrn_o_gain': 'grad_w', 'grad_q_norm_gain': 'grad_w', 'grad_k_norm_gain': 'grad_w', 'grad_sinks': 'grad_w', 'grad_w_branch_a': 'grad_w', 'grad_w_branch_b': 'grad_w', 'grad_w_out': 'grad_w', 'grad_norm2_gain': 'grad_w', 'grad_w_mlp_in': 'grad_w', 'grad_w_mlp_out': 'grad_w', 'delta_w_ada': 'delta_w', 'delta_b_ada': 'delta_w', 'delta_norm1_gain': 'delta_w', 'delta_w_in': 'delta_w', 'delta_lb_logits': 'delta_w', 'delta_hgrn_o_gain': 'delta_w', 'delta_q_norm_gain': 'delta_w', 'delta_k_norm_gain': 'delta_w', 'delta_sinks': 'delta_w', 'delta_w_branch_a': 'delta_w', 'delta_w_branch_b': 'delta_w', 'delta_w_out': 'delta_w', 'delta_norm2_gain': 'delta_w', 'delta_w_mlp_in': 'delta_w', 'delta_w_mlp_out': 'delta_w', 'new_m_w_ada': 'new_m', 'new_m_b_ada': 'new_m', 'new_m_norm1_gain': 'new_m', 'new_m_w_in': 'new_m', 'new_m_lb_logits': 'new_m', 'new_m_hgrn_o_gain': 'new_m', 'new_m_q_norm_gain': 'new_m', 'new_m_k_norm_gain': 'new_m', 'new_m_sinks': 'new_m', 'new_m_w_branch_a': 'new_m', 'new_m_w_branch_b': 'new_m', 'new_m_w_out': 'new_m', 'new_m_norm2_gain': 'new_m', 'new_m_w_mlp_in': 'new_m', 'new_m_w_mlp_out': 'new_m', 'new_v_w_ada': 'new_v', 'new_v_b_ada': 'new_v', 'new_v_norm1_gain': 'new_v', 'new_v_w_in': 'new_v', 'new_v_lb_logits': 'new_v', 'new_v_hgrn_o_gain': 'new_v', 'new_v_q_norm_gain': 'new_v', 'new_v_k_norm_gain': 'new_v', 'new_v_sinks': 'new_v', 'new_v_w_branch_a': 'new_v', 'new_v_w_branch_b': 'new_v', 'new_v_w_out': 'new_v', 'new_v_norm2_gain': 'new_v', 'new_v_w_mlp_in': 'new_v', 'new_v_w_mlp_out': 'new_v'}


def _forward(args):
    return _fwd_reference(*[args[k] for k in FWD_PARAMS])


def _output_shape():
    def fwd():
        inp = _fwd_setup_inputs(0)
        return _fwd_reference(*[inp[k] for k in FWD_PARAMS])
    out = _jax.eval_shape(fwd)
    return out.shape, out.dtype

N_MICROBATCH = 1
ADAM_LR = 0.001
ADAM_B1 = 0.9
ADAM_B2 = 0.999
ADAM_EPS = 1e-08
ADAM_WD = 0.01
ADAM_STEP = 10
PER_EXAMPLE_BATCH_AXIS = {'x': 0, 'c': 0, 'loss_target': 0}
SHARED_INPUTS = []
_WEIGHT_DTYPES = {'w_ada': _jnp.float32, 'b_ada': _jnp.float32, 'norm1_gain': _jnp.float32, 'w_in': _jnp.float32, 'lb_logits': _jnp.float32, 'hgrn_o_gain': _jnp.float32, 'q_norm_gain': _jnp.float32, 'k_norm_gain': _jnp.float32, 'sinks': _jnp.float32, 'w_branch_a': _jnp.float32, 'w_branch_b': _jnp.float32, 'w_out': _jnp.float32, 'norm2_gain': _jnp.float32, 'w_mlp_in': _jnp.float32, 'w_mlp_out': _jnp.float32}
MOMENT_SCALE = {'w_ada': 6.820097e+00, 'b_ada': 1.778998e+01, 'norm1_gain': 1.442549e+00, 'w_in': 1.492683e+00, 'lb_logits': 2.260412e-02, 'hgrn_o_gain': 3.534554e+00, 'q_norm_gain': 1.049345e+00, 'k_norm_gain': 1.051135e+00, 'sinks': 5.519988e-01, 'w_branch_a': 1.470387e+00, 'w_branch_b': 2.528922e+00, 'w_out': 2.898611e+00, 'norm2_gain': 4.618427e+01, 'w_mlp_in': 2.807984e+00, 'w_mlp_out': 6.552319e+00}


def _to_microbatches(a, axis):
    t = _jnp.moveaxis(a, axis, 0)
    t = t.reshape((N_MICROBATCH, t.shape[0] // N_MICROBATCH) + t.shape[1:])
    return _jnp.moveaxis(t, 1, axis + 1)


def setup_inputs(seed: int = 0) -> dict:
    inp = _fwd_setup_inputs(seed)
    key = _jax.random.fold_in(_jax.random.key(seed), 7919)
    shape, _ = _output_shape()
    out = dict(inp)
    out["loss_target"] = _jax.random.normal(_jax.random.fold_in(key, 0), shape, _jnp.float32)
    for i, name in enumerate(TWIN_WEIGHTS):
        w = inp[name].astype(_jnp.float32)
        if MOMENT_SCALE is None:
            s = _jnp.sqrt(_jnp.mean(_jnp.square(w)) + 1e-30)
        else:
            s = MOMENT_SCALE[name]
        km, kv = _jax.random.split(_jax.random.fold_in(key, i + 1))
        out[name] = w
        out["m_" + name] = s * _jax.random.normal(km, w.shape, _jnp.float32)
        out["v_" + name] = (s * s) * _jax.random.uniform(kv, w.shape, _jnp.float32, 0.5, 1.5)
    if N_MICROBATCH > 1:
        for name, axis in PER_EXAMPLE_BATCH_AXIS.items():
            out[name] = _to_microbatches(out[name], axis)
    return {'x': out['x'], 'c': out['c'], 'w_ada': out['w_ada'], 'b_ada': out['b_ada'], 'norm1_gain': out['norm1_gain'], 'w_in': out['w_in'], 'lb_logits': out['lb_logits'], 'hgrn_o_gain': out['hgrn_o_gain'], 'q_norm_gain': out['q_norm_gain'], 'k_norm_gain': out['k_norm_gain'], 'sinks': out['sinks'], 'w_branch_a': out['w_branch_a'], 'w_branch_b': out['w_branch_b'], 'w_out': out['w_out'], 'norm2_gain': out['norm2_gain'], 'w_mlp_in': out['w_mlp_in'], 'w_mlp_out': out['w_mlp_out'], 'loss_target': out['loss_target'], 'm_w_ada': out['m_w_ada'], 'm_b_ada': out['m_b_ada'], 'm_norm1_gain': out['m_norm1_gain'], 'm_w_in': out['m_w_in'], 'm_lb_logits': out['m_lb_logits'], 'm_hgrn_o_gain': out['m_hgrn_o_gain'], 'm_q_norm_gain': out['m_q_norm_gain'], 'm_k_norm_gain': out['m_k_norm_gain'], 'm_sinks': out['m_sinks'], 'm_w_branch_a': out['m_w_branch_a'], 'm_w_branch_b': out['m_w_branch_b'], 'm_w_out': out['m_w_out'], 'm_norm2_gain': out['m_norm2_gain'], 'm_w_mlp_in': out['m_w_mlp_in'], 'm_w_mlp_out': out['m_w_mlp_out'], 'v_w_ada': out['v_w_ada'], 'v_b_ada': out['v_b_ada'], 'v_norm1_gain': out['v_norm1_gain'], 'v_w_in': out['v_w_in'], 'v_lb_logits': out['v_lb_logits'], 'v_hgrn_o_gain': out['v_hgrn_o_gain'], 'v_q_norm_gain': out['v_q_norm_gain'], 'v_k_norm_gain': out['v_k_norm_gain'], 'v_sinks': out['v_sinks'], 'v_w_branch_a': out['v_w_branch_a'], 'v_w_branch_b': out['v_w_branch_b'], 'v_w_out': out['v_w_out'], 'v_norm2_gain': out['v_norm2_gain'], 'v_w_mlp_in': out['v_w_mlp_in'], 'v_w_mlp_out': out['v_w_mlp_out']}


def _loss(weights, diff, rest, loss_target):
    with _jax.named_scope("forward"):
        args = {**rest, TWIN_DIFF_INPUT: diff, **{k: w.astype(_WEIGHT_DTYPES[k]) for k, w in weights.items()}}
        y = _forward(args)
    with _jax.named_scope("loss_head"):
        err = _jnp.square(y.astype(_jnp.float32) - loss_target)
        return 0.5 * _jnp.sum(_jnp.mean(err, axis=-1)) if err.ndim else 0.5 * err


def _adamw(w, g, m, v):
    m = ADAM_B1 * m + (1.0 - ADAM_B1) * g
    v = ADAM_B2 * v + (1.0 - ADAM_B2) * _jnp.square(g)
    m_hat = m / (1.0 - ADAM_B1 ** ADAM_STEP)
    v_hat = v / (1.0 - ADAM_B2 ** ADAM_STEP)
    delta = -ADAM_LR * (m_hat / (_jnp.sqrt(v_hat) + ADAM_EPS) + ADAM_WD * w)
    return delta, m, v


def reference(x, c, w_ada, b_ada, norm1_gain, w_in, lb_logits, hgrn_o_gain, q_norm_gain, k_norm_gain, sinks, w_branch_a, w_branch_b, w_out, norm2_gain, w_mlp_in, w_mlp_out, loss_target, m_w_ada, m_b_ada, m_norm1_gain, m_w_in, m_lb_logits, m_hgrn_o_gain, m_q_norm_gain, m_k_norm_gain, m_sinks, m_w_branch_a, m_w_branch_b, m_w_out, m_norm2_gain, m_w_mlp_in, m_w_mlp_out, v_w_ada, v_b_ada, v_norm1_gain, v_w_in, v_lb_logits, v_hgrn_o_gain, v_q_norm_gain, v_k_norm_gain, v_sinks, v_w_branch_a, v_w_branch_b, v_w_out, v_norm2_gain, v_w_mlp_in, v_w_mlp_out):
    given = dict(x=x, c=c, w_ada=w_ada, b_ada=b_ada, norm1_gain=norm1_gain, w_in=w_in, lb_logits=lb_logits, hgrn_o_gain=hgrn_o_gain, q_norm_gain=q_norm_gain, k_norm_gain=k_norm_gain, sinks=sinks, w_branch_a=w_branch_a, w_branch_b=w_branch_b, w_out=w_out, norm2_gain=norm2_gain, w_mlp_in=w_mlp_in, w_mlp_out=w_mlp_out, loss_target=loss_target, m_w_ada=m_w_ada, m_b_ada=m_b_ada, m_norm1_gain=m_norm1_gain, m_w_in=m_w_in, m_lb_logits=m_lb_logits, m_hgrn_o_gain=m_hgrn_o_gain, m_q_norm_gain=m_q_norm_gain, m_k_norm_gain=m_k_norm_gain, m_sinks=m_sinks, m_w_branch_a=m_w_branch_a, m_w_branch_b=m_w_branch_b, m_w_out=m_w_out, m_norm2_gain=m_norm2_gain, m_w_mlp_in=m_w_mlp_in, m_w_mlp_out=m_w_mlp_out, v_w_ada=v_w_ada, v_b_ada=v_b_ada, v_norm1_gain=v_norm1_gain, v_w_in=v_w_in, v_lb_logits=v_lb_logits, v_hgrn_o_gain=v_hgrn_o_gain, v_q_norm_gain=v_q_norm_gain, v_k_norm_gain=v_k_norm_gain, v_sinks=v_sinks, v_w_branch_a=v_w_branch_a, v_w_branch_b=v_w_branch_b, v_w_out=v_w_out, v_norm2_gain=v_norm2_gain, v_w_mlp_in=v_w_mlp_in, v_w_mlp_out=v_w_mlp_out)
    weights = {n: given[n] for n in TWIN_WEIGHTS}
    shared = {n: given[n] for n in SHARED_INPUTS}
    per_example = {n: given[n] for n in ['x', 'c']}
    grad_fn = _jax.value_and_grad(_loss, argnums=(0, 1))

    def one_microbatch(ex, loss_target):
        ex = dict(ex)
        diff = ex.pop(TWIN_DIFF_INPUT)
        return grad_fn(weights, diff, {**shared, **ex}, loss_target)

    if N_MICROBATCH == 1:
        loss, (grad_w, grad_x) = one_microbatch(per_example, given["loss_target"])
    else:
        def body(carry, xs):
            loss_sum, grad_sum = carry
            l_k, (gw_k, gx_k) = one_microbatch(xs[0], xs[1])
            with _jax.named_scope("update"):
                return (loss_sum + l_k, _jax.tree.map(_jnp.add, grad_sum, gw_k)), gx_k

        init = (_jnp.zeros((), _jnp.float32), _jax.tree.map(_jnp.zeros_like, weights))
        (loss, grad_w), grad_x = _jax.lax.scan(body, init, (per_example, given["loss_target"]))
    with _jax.named_scope("update"):
        delta_w, new_m, new_v = {}, {}, {}
        for n in TWIN_WEIGHTS:
            delta_w[n], new_m[n], new_v[n] = _adamw(weights[n], grad_w[n], given["m_" + n], given["v_" + n])
    return (loss, grad_x, *[grad_w[n] for n in TWIN_WEIGHTS], *[delta_w[n] for n in TWIN_WEIGHTS],
            *[new_m[n] for n in TWIN_WEIGHTS], *[new_v[n] for n in TWIN_WEIGHTS])
```

```python
import functools

import jax
import jax.numpy as jnp
from jax import lax
from jax.experimental import pallas as pl
from jax.experimental.pallas import tpu as pltpu

F32 = jnp.float32
BF16 = jnp.bfloat16
MESH = pl.DeviceIdType.MESH

N_DEV = 8
D = 2048
A_HEADS, A_HD, CHUNK = 8, 128, 64
AW = A_HEADS * A_HD
Q_HEADS, KV_HEADS, GROUP, B_HD, BLK = 16, 4, 4, 64, 128
BW = Q_HEADS * B_HD
KVW = KV_HEADS * B_HD
HID = 4 * D
IN_W = 4 * AW + BW + 2 * KVW + 2 * D
OFF_QA, OFF_FA, OFF_IA, OFF_GA = 0, AW, 2 * AW, 3 * AW
OFF_QB = 4 * AW
OFF_KB = OFF_QB + BW
OFF_VB = OFF_KB + KVW
OFF_GTA = OFF_VB + KVW
OFF_GTB = OFF_GTA + D
N_MOD = 6
EPS = 1e-6
LR, B1, B2, ADAM_EPS, WD, STEP = 1e-3, 0.9, 0.999, 1e-8, 0.01, 10
NEG = -1e30

VMEM_LIMIT = 56 * 1024 * 1024

NN = (((1,), (0,)), ((), ()))
NT = (((1,), (1,)), ((), ()))
TN = (((0,), (0,)), ((), ()))


def _dot(a, b, dims=NN):
    return lax.dot_general(a.astype(BF16), b.astype(BF16), dims, preferred_element_type=F32)


def _params(sem):
    return pltpu.CompilerParams(dimension_semantics=sem, vmem_limit_bytes=VMEM_LIMIT)


def _sigmoid(x):
    return 1.0 / (1.0 + jnp.exp(-x))


def _fold8(v):
    r, n = v.shape
    return jnp.sum(v.reshape(r // 8, 8, n), axis=0)


def _mm(name, form, a_list, b, M, N, K, tm, tn, tk, extras, outs, epi):
    nI, nJ, nK = M // tm, N // tn, K // tk
    assert nI * tm == M and nJ * tn == N and nK * tk == K
    dims = {"nn": NN, "nt": NT, "tn": TN}[form]
    split = tm if form == "tn" else tk
    starts, s = [], 0
    for _, w in a_list:
        assert w % split == 0
        starts.append(s // split)
        s += w
    counts = [w // split for _, w in a_list]
    assert s == (M if form == "tn" else K)
    nP = len(a_list)

    def a_spec(p):
        st, cn = starts[p], counts[p]
        if form == "tn":
            return pl.BlockSpec((tk, tm), lambda i, j, k: (k, jnp.clip(i - st, 0, cn - 1)))
        return pl.BlockSpec((tm, tk), lambda i, j, k: (i, jnp.clip(k - st, 0, cn - 1)))

    if form == "nn":
        b_spec = pl.BlockSpec((tk, tn), lambda i, j, k: (k, j))
    elif form == "nt":
        b_spec = pl.BlockSpec((tn, tk), lambda i, j, k: (j, k))
    else:
        b_spec = pl.BlockSpec((tk, tn), lambda i, j, k: (k, j))
    in_specs = [a_spec(p) for p in range(nP)] + [b_spec] + [pl.BlockSpec(bs, im) for _, bs, im in extras]
    out_shape = [jax.ShapeDtypeStruct(s_, d_) for s_, d_, _, _ in outs]
    out_specs = [pl.BlockSpec(bs, im) for _, _, bs, im in outs]
    nE, nO = len(extras), len(outs)

    def body(*refs):
        a_refs, b_ref = refs[:nP], refs[nP]
        ex, ou = refs[nP + 1:nP + 1 + nE], refs[nP + 1 + nE:nP + 1 + nE + nO]
        sel = pl.program_id(0) if form == "tn" else pl.program_id(2)

        def partial_of(p):
            return lax.dot_general(a_refs[p][...], b_ref[...], dims, preferred_element_type=F32)

        if nK == 1 and nP == 1:
            epi(partial_of(0), ex, ou)
            return
        acc = refs[-1]
        k = pl.program_id(2)
        for p in range(nP):
            inside = (sel >= starts[p]) & (sel < starts[p] + counts[p]) if nP > 1 else None

            def first(p=p):
                acc[...] = partial_of(p)

            def later(p=p):
                acc[...] += partial_of(p)

            if nP == 1:
                pl.when(k == 0)(first)
                pl.when(k > 0)(later)
            else:
                pl.when(inside & (k == 0))(first)
                pl.when(inside & (k > 0))(later)

        @pl.when(k == nK - 1)
        def _():
            epi(acc[...], ex, ou)

    scratch = [] if (nK == 1 and nP == 1) else [pltpu.VMEM((tm, tn), F32)]
    return pl.pallas_call(
        body, grid=(nI, nJ, nK), in_specs=in_specs, out_specs=out_specs, out_shape=out_shape,
        scratch_shapes=scratch, name=name, compiler_params=_params(("parallel", "parallel", "arbitrary")),
    )(*[a for a, _ in a_list], b, *[e for e, _, _ in extras])


def _tile(shape_dtype, tm, tn, col0=0):
    return (tm, tn), (lambda i, j, k: (i, col0 + j))


def _rms_mod_fwd(name, x, gain, sc, sh, tr):
    T = x.shape[0]

    def body(x_ref, g_ref, sc_ref, sh_ref, h_ref):
        xv = x_ref[...]
        rstd = lax.rsqrt(jnp.mean(xv * xv, axis=-1, keepdims=True) + EPS)
        h_ref[...] = ((xv * rstd * g_ref[...]) * (1.0 + sc_ref[...]) + sh_ref[...]).astype(BF16)

    row = pl.BlockSpec((tr, D), lambda i: (i, 0))
    vec = pl.BlockSpec((1, D), lambda i: (0, 0))
    return pl.pallas_call(
        body, grid=(T // tr,), in_specs=[row, vec, vec, vec], out_specs=row,
        out_shape=jax.ShapeDtypeStruct((T, D), BF16), name=name, compiler_params=_params(("parallel",)),
    )(x, gain, sc, sh)


def _rms_mod_bwd(name, dh, x, gain, sc, dres, tr, gate=None, mo=None):
    T = x.shape[0]
    nR = T // tr
    with_gate = gate is not None

    def body(*refs):
        if with_gate:
            dh_ref, x_ref, g_ref, sc_ref, dr_ref, gt_ref, mo_ref, dx_ref, psh, psc, pg, dmo_ref, pgt = refs
        else:
            dh_ref, x_ref, g_ref, sc_ref, dr_ref, dx_ref, psh, psc, pg = refs
        xv, dhv = x_ref[...], dh_ref[...]
        rstd = lax.rsqrt(jnp.mean(xv * xv, axis=-1, keepdims=True) + EPS)
        xhat = xv * rstd
        n = xhat * g_ref[...]
        dn = dhv * (1.0 + sc_ref[...])
        dxhat = dn * g_ref[...]
        dx = dr_ref[...] + rstd * (dxhat - xhat * jnp.mean(dxhat * xhat, axis=-1, keepdims=True))
        dx_ref[...] = dx
        psh[...] = _fold8(dhv)
        psc[...] = _fold8(dhv * n)
        pg[...] = _fold8(dn * xhat)
        if with_gate:
            dmo_ref[...] = (gt_ref[...] * dx).astype(BF16)
            pgt[...] = _fold8(dx * mo_ref[...].astype(F32))

    row = pl.BlockSpec((tr, D), lambda i: (i, 0))
    vec = pl.BlockSpec((1, D), lambda i: (0, 0))
    part = pl.BlockSpec((8, D), lambda i: (i, 0))
    p_shape = jax.ShapeDtypeStruct((nR * 8, D), F32)
    ins = [dh, x, gain, sc, dres] + ([gate, mo] if with_gate else [])
    in_specs = [row, row, vec, vec, row] + ([vec, row] if with_gate else [])
    out_shape = [jax.ShapeDtypeStruct((T, D), F32), p_shape, p_shape, p_shape]
    out_specs = [row, part, part, part]
    if with_gate:
        out_shape += [jax.ShapeDtypeStruct((T, D), BF16), p_shape]
        out_specs += [row, part]
    return pl.pallas_call(
        body, grid=(nR,), in_specs=in_specs, out_specs=out_specs, out_shape=out_shape, name=name,
        compiler_params=_params(("parallel",)),
    )(*ins)


def _split3(v):
    h = v.astype(BF16)
    r1 = v - h.astype(F32)
    m = r1.astype(BF16)
    lo = (r1 - m.astype(F32)).astype(BF16)
    return h, m, lo


def _tri_mm(tri, v, dims=NN):
    h, m, lo = _split3(v)
    t = tri.astype(BF16)
    mm = lambda p: lax.dot_general(t, p, dims, preferred_element_type=F32)
    return (mm(lo) + mm(m)) + mm(h)


def _hgrn_chunk_terms(q, fl, lb):
    sig = _sigmoid(fl)
    f = lb + (1.0 - lb) * sig
    lf = jnp.log(f)
    kk = 1.0 - f
    sq = _sigmoid(q)
    qf = q * sq
    return sig, f, lf, kk, sq, qf


def _causal(n):
    r = lax.broadcasted_iota(jnp.int32, (n, n), 0)
    c = lax.broadcasted_iota(jnp.int32, (n, n), 1)
    return r >= c


def _hgrn_fwd(proj, lb_logits, o_gain, tt):
    T = proj.shape[0]
    nT, ncl = T // tt, tt // CHUNK
    C = CHUNK

    def body(q_ref, f_ref, i_ref, g_ref, lbl_ref, og_ref, y_ref, st_ref, S):
        @pl.when(pl.program_id(1) == 0)
        def _():
            S[...] = jnp.zeros_like(S)

        lbl = lbl_ref[...]
        lb = _sigmoid(lbl[0:1, :] - lbl[1:2, :])
        og = og_ref[...]
        tri = _causal(C)
        for ci in range(ncl):
            sl = pl.ds(ci * C, C)
            q, fl, v, g = q_ref[sl, :], f_ref[sl, :], i_ref[sl, :], g_ref[sl, :]
            _, _, lf, kk, _, qf = _hgrn_chunk_terms(q, fl, lb)
            b = _tri_mm(tri, lf)
            bm, bl = b[C // 2 - 1:C // 2, :], b[C - 1:C, :]
            qd, kd = qf * jnp.exp(b - bm), kk * jnp.exp(bm - b)
            A = jnp.where(tri, _dot(qd, kd, NT), 0.0)
            ST0 = S[...]
            o = _dot(A, v) + _dot(qf * jnp.exp(b), ST0, NT)
            ke = kk * jnp.exp(bl - b)
            st_ref[0, ci] = ST0
            S[...] = ST0 * jnp.exp(bl) + _dot(v, ke, TN)
            r = lax.rsqrt(jnp.mean(o * o, axis=-1, keepdims=True) + EPS)
            y_ref[sl, :] = (o * r * og * (g * _sigmoid(g))).astype(BF16)

    def col(off):
        return pl.BlockSpec((tt, A_HD), lambda h, t: (t, off // A_HD + h))

    head_vec = lambda rows: pl.BlockSpec((rows, A_HD), lambda h, t: (0, h))
    return pl.pallas_call(
        body, grid=(A_HEADS, nT),
        in_specs=[col(OFF_QA), col(OFF_FA), col(OFF_IA), col(OFF_GA), head_vec(2), head_vec(1)],
        out_specs=[pl.BlockSpec((tt, A_HD), lambda h, t: (t, h)),
                   pl.BlockSpec((1, ncl, A_HD, A_HD), lambda h, t: (h, t, 0, 0))],
        out_shape=[jax.ShapeDtypeStruct((T, AW), BF16),
                   jax.ShapeDtypeStruct((A_HEADS, T // C, A_HD, A_HD), F32)],
        scratch_shapes=[pltpu.VMEM((A_HD, A_HD), F32)], name="hgrn_fwd",
        compiler_params=_params(("parallel", "arbitrary")),
    )(proj, proj, proj, proj, lb_logits, o_gain)


def _hgrn_bwd(proj, st, dy, lb_logits, o_gain, tt):
    T = proj.shape[0]
    nT, ncl = T // tt, tt // CHUNK
    C = CHUNK

    def body(q_ref, f_ref, i_ref, g_ref, st_ref, dy_ref, lbl_ref, og_ref,
             dq_ref, df_ref, di_ref, dg_ref, plb_ref, pog_ref, dS):
        @pl.when(pl.program_id(1) == 0)
        def _():
            dS[...] = jnp.zeros_like(dS)

        lbl = lbl_ref[...]
        lb = _sigmoid(lbl[0:1, :] - lbl[1:2, :])
        og = og_ref[...]
        tri = _causal(C)
        rowi = lax.broadcasted_iota(jnp.int32, (C, A_HD), 0)
        acc_lb = jnp.zeros((8, A_HD), F32)
        acc_og = jnp.zeros((8, A_HD), F32)
        for ci in reversed(range(ncl)):
            sl = pl.ds(ci * C, C)
            q, fl, v, g = q_ref[sl, :], f_ref[sl, :], i_ref[sl, :], g_ref[sl, :]
            dout = dy_ref[sl, :]
            ST0 = st_ref[0, ci]
            dST1 = dS[...]
            sig, f, lf, kk, sq, qf = _hgrn_chunk_terms(q, fl, lb)
            b = _tri_mm(tri, lf)
            bm, bl = b[C // 2 - 1:C // 2, :], b[C - 1:C, :]
            e_qd, e_kd, e_ke, e_b = jnp.exp(b - bm), jnp.exp(bm - b), jnp.exp(bl - b), jnp.exp(b)
            qd, kd, ke, qe = qf * e_qd, kk * e_kd, kk * e_ke, qf * e_b
            dec = jnp.exp(bl)
            A = jnp.where(tri, _dot(qd, kd, NT), 0.0)
            o = _dot(A, v) + _dot(qe, ST0, NT)
            r = lax.rsqrt(jnp.mean(o * o, axis=-1, keepdims=True) + EPS)
            sg = _sigmoid(g)
            silu_g = g * sg
            on = o * r * og
            dg_ref[sl, :] = (dout * on * (sg * (1.0 + g * (1.0 - sg)))).astype(BF16)
            don = dout * silu_g
            acc_og = acc_og + _fold8(don * o * r)
            dyh = don * og
            do = r * (dyh - o * (r * r) * jnp.mean(dyh * o, axis=-1, keepdims=True))
            dqe = _dot(do, ST0)
            dST0 = _dot(do, qe, TN) + dST1 * dec
            ddec = jnp.sum(dST1 * ST0, axis=0, keepdims=True)
            dke = _dot(v, dST1)
            dv = _dot(ke, dST1, NT)
            dA = jnp.where(tri, _dot(do, v, NT), 0.0)
            dv = dv + _dot(A, do, TN)
            dqd = _dot(dA, kd)
            dkd = _dot(dA, qd, TN)
            di_ref[sl, :] = dv.astype(BF16)
            dqf = dqe * e_b + dqd * e_qd
            dkk = dkd * e_kd + dke * e_ke
            t_qd, t_kd, t_ke = dqd * qd, dkd * kd, dke * ke
            db = dqe * qe + t_qd - t_kd - t_ke
            dbm = jnp.sum(t_kd - t_qd, axis=0, keepdims=True)
            dbl = jnp.sum(t_ke, axis=0, keepdims=True) + ddec * dec
            db = db + jnp.where(rowi == C // 2 - 1, dbm, 0.0) + jnp.where(rowi == C - 1, dbl, 0.0)
            dlf = _tri_mm(tri, db, TN)
            dfv = dlf / f - dkk
            df_ref[sl, :] = (dfv * (1.0 - lb) * sig * (1.0 - sig)).astype(BF16)
            acc_lb = acc_lb + _fold8(dfv * (1.0 - sig))
            dq_ref[sl, :] = (dqf * (sq * (1.0 + q * (1.0 - sq)))).astype(BF16)
            dS[...] = dST0
        plb_ref[...] = acc_lb
        pog_ref[...] = acc_og

    def col(off):
        return pl.BlockSpec((tt, A_HD), lambda h, t: (nT - 1 - t, off // A_HD + h))

    head_vec = lambda rows: pl.BlockSpec((rows, A_HD), lambda h, t: (0, h))
    o_spec = pl.BlockSpec((tt, A_HD), lambda h, t: (nT - 1 - t, h))
    p_spec = pl.BlockSpec((8, A_HD), lambda h, t: (t, h))
    o_shape = jax.ShapeDtypeStruct((T, AW), BF16)
    p_shape = jax.ShapeDtypeStruct((nT * 8, AW), F32)
    return pl.pallas_call(
        body, grid=(A_HEADS, nT),
        in_specs=[col(OFF_QA), col(OFF_FA), col(OFF_IA), col(OFF_GA),
                  pl.BlockSpec((1, ncl, A_HD, A_HD), lambda h, t: (h, nT - 1 - t, 0, 0)),
                  pl.BlockSpec((tt, A_HD), lambda h, t: (nT - 1 - t, h)), head_vec(2), head_vec(1)],
        out_specs=[o_spec, o_spec, o_spec, o_spec, p_spec, p_spec],
        out_shape=[o_shape, o_shape, o_shape, o_shape, p_shape, p_shape],
        scratch_shapes=[pltpu.VMEM((A_HD, A_HD), F32)], name="hgrn_bwd",
        compiler_params=_params(("parallel", "arbitrary")),
    )(proj, proj, proj, proj, st, dy, lb_logits, o_gain)


def _head_rms(t):
    r = lax.rsqrt(jnp.mean(t * t, axis=-1, keepdims=True) + EPS)
    return t * r, r


def _swa_mask(first_block):
    qi = lax.broadcasted_iota(jnp.int32, (BLK, 2 * BLK), 0) + BLK
    ki = lax.broadcasted_iota(jnp.int32, (BLK, 2 * BLK), 1)
    rel = qi - ki
    return (rel >= 0) & (rel < BLK) & (jnp.logical_not(first_block) | (ki >= BLK))


def _swa_probs(qn, kcat, sink, mask):
    s = jnp.where(mask, _dot(qn, kcat, NT) * (B_HD ** -0.5), NEG)
    m = jnp.maximum(jnp.max(s, axis=-1, keepdims=True), sink)
    p = jnp.exp(s - m)
    ps = jnp.exp(sink - m)
    inv = 1.0 / (jnp.sum(p, axis=-1, keepdims=True) + ps)
    return p * inv, ps * inv


def _swa_fwd(proj, q_gain, k_gain, sinks):
    T = proj.shape[0]
    nb = T // BLK

    def body(q_ref, kc_ref, kp_ref, vc_ref, vp_ref, qg_ref, kg_ref, sk_ref, o_ref):
        mask = _swa_mask(pl.program_id(0) == 0)
        qg, kg = qg_ref[...], kg_ref[...]
        for hk in range(KV_HEADS):
            ks = slice(hk * B_HD, (hk + 1) * B_HD)
            kcat = jnp.concatenate([_head_rms(kp_ref[:, ks])[0] * kg, _head_rms(kc_ref[:, ks])[0] * kg], axis=0)
            vcat = jnp.concatenate([vp_ref[:, ks], vc_ref[:, ks]], axis=0)
            for g in range(GROUP):
                h = hk * GROUP + g
                hs = slice(h * B_HD, (h + 1) * B_HD)
                qn = _head_rms(q_ref[:, hs])[0] * qg
                p, _ = _swa_probs(qn, kcat, sk_ref[0, h], mask)
                o_ref[:, hs] = _dot(p, vcat).astype(BF16)

    cur = lambda w, off: pl.BlockSpec((BLK, w), lambda i: (i, off // w))
    prev = lambda w, off: pl.BlockSpec((BLK, w), lambda i: (jnp.maximum(i - 1, 0), off // w))
    small = lambda n: pl.BlockSpec((1, n), lambda i: (0, 0))
    return pl.pallas_call(
        body, grid=(nb,),
        in_specs=[cur(BW, OFF_QB), cur(KVW, OFF_KB), prev(KVW, OFF_KB), cur(KVW, OFF_VB), prev(KVW, OFF_VB),
                  small(B_HD), small(B_HD), pl.BlockSpec(memory_space=pltpu.SMEM)],
        out_specs=pl.BlockSpec((BLK, BW), lambda i: (i, 0)),
        out_shape=jax.ShapeDtypeStruct((T, BW), BF16), name="swa_fwd", compiler_params=_params(("parallel",)),
    )(proj, proj, proj, proj, proj, q_gain, k_gain, sinks)


def _swa_bwd(proj, dout, q_gain, k_gain, sinks):
    T = proj.shape[0]
    nb = T // BLK
    W = BW + 2 * KVW

    def body(q_ref, kc_ref, kp_ref, vc_ref, vp_ref, do_ref, qg_ref, kg_ref, sk_ref,
             dq_ref, dkv_ref, pqg_ref, pkg_ref, psk_ref, dkn_c, dv_c):
        i = pl.program_id(0)
        live = i < nb
        qg, kg = qg_ref[...], kg_ref[...]
        mask = _swa_mask(i == 0)
        lane = lax.broadcasted_iota(jnp.int32, (1, 128), 1)

        @pl.when(i == 0)
        def _():
            dkn_c[...] = jnp.zeros_like(dkn_c)
            dv_c[...] = jnp.zeros_like(dv_c)

        acc_qg = jnp.zeros((8, B_HD), F32)
        acc_sk = jnp.zeros((1, 128), F32)
        acc_kg = jnp.zeros((8, B_HD), F32)
        for hk in range(KV_HEADS):
            ks = slice(hk * B_HD, (hk + 1) * B_HD)
            kp_hat, kp_r = _head_rms(kp_ref[:, ks])
            kc_hat, _ = _head_rms(kc_ref[:, ks])
            kcat = jnp.concatenate([kp_hat * kg, kc_hat * kg], axis=0)
            vcat = jnp.concatenate([vp_ref[:, ks], vc_ref[:, ks]], axis=0)
            dkn = jnp.zeros((2 * BLK, B_HD), F32)
            dvc = jnp.zeros((2 * BLK, B_HD), F32)
            for g in range(GROUP):
                h = hk * GROUP + g
                hs = slice(h * B_HD, (h + 1) * B_HD)
                q_hat, q_r = _head_rms(q_ref[:, hs])
                qn = q_hat * qg
                p, ps = _swa_probs(qn, kcat, sk_ref[0, h], mask)
                do = do_ref[:, hs]
                dp = _dot(do, vcat, NT)
                delta = jnp.sum(p * dp, axis=-1, keepdims=True)
                ds = p * (dp - delta) * (B_HD ** -0.5)
                dvc = dvc + _dot(p, do, TN)
                dkn = dkn + _dot(ds, qn, TN)
                dqn = _dot(ds, kcat)
                acc_qg = acc_qg + _fold8(dqn * q_hat)
                dqh = dqn * qg
                dq = (q_r * (dqh - q_hat * jnp.mean(dqh * q_hat, axis=-1, keepdims=True))).astype(BF16)

                @pl.when(live)
                def _(dq=dq, hs=hs):
                    dq_ref[:, hs] = dq

                acc_sk = acc_sk + jnp.where(lane == h, -jnp.sum(ps * delta, axis=0, keepdims=True), 0.0)
            dkn = jnp.where(live, dkn, 0.0)
            dvc = jnp.where(live, dvc, 0.0)
            dkn_prev = dkn_c[:, ks] + dkn[:BLK]
            dv_prev = dv_c[:, ks] + dvc[:BLK]
            acc_kg = acc_kg + _fold8(dkn_prev * kp_hat)
            dkh = dkn_prev * kg
            dkv_ref[:, ks] = (kp_r * (dkh - kp_hat * jnp.mean(dkh * kp_hat, axis=-1, keepdims=True))).astype(BF16)
            dkv_ref[:, KVW + hk * B_HD:KVW + (hk + 1) * B_HD] = dv_prev.astype(BF16)
            dkn_c[:, ks] = dkn[BLK:]
            dv_c[:, ks] = dvc[BLK:]
        keep = jnp.where(i > 0, 1.0, 0.0)
        pqg_ref[...] = jnp.zeros_like(pqg_ref)
        pkg_ref[...] = jnp.zeros_like(pkg_ref)
        pqg_ref[:, :B_HD] = jnp.where(live, acc_qg, 0.0)
        pkg_ref[:, :B_HD] = acc_kg * keep
        psk_ref[...] = jnp.broadcast_to(jnp.where(live, acc_sk, 0.0), (8, 128)) * (
            lax.broadcasted_iota(jnp.int32, (8, 128), 0) == 0).astype(F32)

    last = nb - 1
    cur = lambda w, off: pl.BlockSpec((BLK, w), lambda i: (jnp.minimum(i, last), off // w))
    prev = lambda w, off: pl.BlockSpec((BLK, w), lambda i: (jnp.maximum(i - 1, 0), off // w))
    small = lambda n: pl.BlockSpec((1, n), lambda i: (0, 0))
    part = pl.BlockSpec((8, 128), lambda i: (i, 0))
    p_shape = jax.ShapeDtypeStruct(((nb + 1) * 8, 128), F32)
    return pl.pallas_call(
        body, grid=(nb + 1,),
        in_specs=[cur(BW, OFF_QB), cur(KVW, OFF_KB), prev(KVW, OFF_KB), cur(KVW, OFF_VB), prev(KVW, OFF_VB),
                  pl.BlockSpec((BLK, BW), lambda i: (jnp.minimum(i, last), 0)), small(B_HD), small(B_HD), pl.BlockSpec(memory_space=pltpu.SMEM)],
        out_specs=[pl.BlockSpec((BLK, BW), lambda i: (jnp.minimum(i, last), 0)),
                   pl.BlockSpec((BLK, 2 * KVW), lambda i: (jnp.maximum(i - 1, 0), 0)), part, part, part],
        out_shape=[jax.ShapeDtypeStruct((T, BW), BF16), jax.ShapeDtypeStruct((T, 2 * KVW), BF16), p_shape, p_shape, p_shape],
        scratch_shapes=[pltpu.VMEM((BLK, KVW), F32), pltpu.VMEM((BLK, KVW), F32)], name="swa_bwd",
        compiler_params=_params(("arbitrary",)),
    )(proj, proj, proj, proj, proj, dout, q_gain, k_gain, sinks)


def _branch_merge(ya_pre, attn, wa_t, wb_t, proj, tm, tn):
    T = ya_pre.shape[0]

    def body(a_ref, b_ref, wa_ref, wb_ref, ga_ref, gb_ref, ya_ref, yb_ref, mg_ref):
        ya = lax.dot_general(a_ref[...], wa_ref[...], NT, preferred_element_type=F32)
        yb = lax.dot_general(b_ref[...], wb_ref[...], NT, preferred_element_type=F32)
        ya_ref[...] = ya.astype(BF16)
        yb_ref[...] = yb.astype(BF16)
        mg_ref[...] = (_sigmoid(ga_ref[...]) * ya + _sigmoid(gb_ref[...]) * yb).astype(BF16)

    o_spec = pl.BlockSpec((tm, tn), lambda i, j: (i, j))
    o_shape = jax.ShapeDtypeStruct((T, D), BF16)
    return pl.pallas_call(
        body, grid=(T // tm, D // tn),
        in_specs=[pl.BlockSpec((tm, AW), lambda i, j: (i, 0)), pl.BlockSpec((tm, BW), lambda i, j: (i, 0)),
                  pl.BlockSpec((tn, AW), lambda i, j: (j, 0)), pl.BlockSpec((tn, BW), lambda i, j: (j, 0)),
                  pl.BlockSpec((tm, tn), lambda i, j: (i, OFF_GTA // tn + j)),
                  pl.BlockSpec((tm, tn), lambda i, j: (i, OFF_GTB // tn + j))],
        out_specs=[o_spec, o_spec, o_spec], out_shape=[o_shape, o_shape, o_shape], name="branch_merge",
        compiler_params=_params(("parallel", "parallel")),
    )(ya_pre, attn, wa_t, wb_t, proj, proj)


def _ij(i, j, k):
    return (i, j)


def _local_step(x, tgt, mod, g1, g2, lbl, og, qg, kg, sk, win_t, wa_t, wb_t, w_out, wmi_t, w_mo):
    T = x.shape[0]
    tm, tr, tt = min(1024, T), min(256, T), min(512, T)
    tk_t = min(1024, T)
    tn = 512
    sh1, sc1, gt1, sh2, sc2, gt2 = (mod[:, i * D:(i + 1) * D] for i in range(N_MOD))
    nI = T // tm
    blk = (tm, tn)
    part = lambda: ((nI * 8, D), F32, (8, tn), _ij)
    vec_j = ((1, tn), lambda i, j, k: (0, j))

    h = _rms_mod_fwd("rms1_fwd", x, g1, sc1, sh1, tr)

    def epi_store(acc, ex, ou):
        ou[0][...] = acc.astype(ou[0].dtype)

    (proj,) = _mm("in_proj", "nt", [(h, D)], win_t, T, IN_W, D, tm, tn, D, [], [((T, IN_W), F32, blk, _ij)], epi_store)
    ya_pre, st = _hgrn_fwd(proj, lbl, og, tt)
    attn = _swa_fwd(proj, qg, kg, sk)
    ya, yb, merged = _branch_merge(ya_pre, attn, wa_t, wb_t, proj, tm, tn)

    def epi_res1(acc, ex, ou):
        x_ref, gt_ref = ex
        ou[0][...] = acc.astype(BF16)
        ou[1][...] = x_ref[...] + gt_ref[...] * acc

    mo, x1 = _mm("out_proj", "nn", [(merged, D)], w_out, T, D, D, tm, tn, D,
                 [(x, blk, _ij), (gt1, *vec_j)], [((T, D), BF16, blk, _ij), ((T, D), F32, blk, _ij)], epi_res1)
    h2 = _rms_mod_fwd("rms2_fwd", x1, g2, sc2, sh2, tr)

    def epi_relu2(acc, ex, ou):
        r = jnp.maximum(acc, 0.0)
        ou[0][...] = r.astype(BF16)
        ou[1][...] = (r * r).astype(BF16)

    r, a = _mm("mlp_in", "nt", [(h2, D)], wmi_t, T, HID, D, tm, tn, D, [],
               [((T, HID), BF16, blk, _ij), ((T, HID), BF16, blk, _ij)], epi_relu2)

    def epi_loss(acc, ex, ou):
        x1_ref, t_ref, gt_ref = ex
        e = x1_ref[...] + gt_ref[...] * acc - t_ref[...]
        dy = e * (1.0 / D)
        ou[0][...] = dy
        ou[1][...] = (gt_ref[...] * dy).astype(BF16)
        ou[2][...] = _fold8(e * e) * (0.5 / D)
        ou[3][...] = _fold8(dy * acc)

    dy, dz, p_loss, p_gt2 = _mm(
        "mlp_out", "nn", [(a, HID)], w_mo, T, D, HID, tm, tn, 1024, [(x1, blk, _ij), (tgt, blk, _ij), (gt2, *vec_j)],
        [((T, D), F32, blk, _ij), ((T, D), BF16, blk, _ij), part(), part()], epi_loss)

    def epi_du(acc, ex, ou):
        ou[0][...] = (acc * (2.0 * ex[0][...].astype(F32))).astype(BF16)

    (du,) = _mm("mlp_out_dx", "nt", [(dz, D)], w_mo, T, HID, D, tm, tn, D, [(r, blk, _ij)],
                [((T, HID), BF16, blk, _ij)], epi_du)
    gblk = (1024, 1024)
    (g_mo,) = _mm("mlp_out_dw", "tn", [(a, HID)], dz, HID, D, T, 1024, 1024, tk_t, [], [((HID, D), BF16, gblk, _ij)], epi_store)
    (dh2,) = _mm("mlp_in_dx", "nn", [(du, HID)], wmi_t, T, D, HID, tm, 1024, 1024, [], [((T, D), F32, (tm, 1024), _ij)], epi_store)
    (g_mi,) = _mm("mlp_in_dw", "tn", [(du, HID)], h2, HID, D, T, 1024, 1024, tk_t, [], [((HID, D), BF16, gblk, _ij)], epi_store)
    dx1, p_sh2, p_sc2, p_g2, dmo, p_gt1 = _rms_mod_bwd("rms2_bwd", dh2, x1, g2, sc2, dy, tr, gate=gt1, mo=mo)

    def epi_gates(acc, ex, ou):
        ya_ref, yb_ref, ga_ref, gb_ref = ex
        sa, sb = _sigmoid(ga_ref[...]), _sigmoid(gb_ref[...])
        ou[0][...] = (acc * sa).astype(BF16)
        ou[1][...] = (acc * sb).astype(BF16)
        ou[2][...] = (acc * ya_ref[...].astype(F32) * (sa * (1.0 - sa))).astype(BF16)
        ou[3][...] = (acc * yb_ref[...].astype(F32) * (sb * (1.0 - sb))).astype(BF16)

    o_bf = ((T, D), BF16, blk, _ij)
    dya, dyb, dga, dgb = _mm(
        "out_proj_dx", "nt", [(dmo, D)], w_out, T, D, D, tm, tn, D,
        [(ya, blk, _ij), (yb, blk, _ij), (proj, blk, lambda i, j, k: (i, OFF_GTA // tn + j)),
         (proj, blk, lambda i, j, k: (i, OFF_GTB // tn + j))], [o_bf, o_bf, o_bf, o_bf], epi_gates)
    (g_out,) = _mm("out_proj_dw", "tn", [(merged, D)], dmo, D, D, T, 1024, 1024, tk_t, [], [((D, D), BF16, gblk, _ij)], epi_store)
    (dya_pre,) = _mm("branch_a_dx", "nn", [(dya, D)], wa_t, T, AW, D, tm, tn, D, [], [((T, AW), F32, blk, _ij)], epi_store)
    (dattn,) = _mm("branch_b_dx", "nn", [(dyb, D)], wb_t, T, BW, D, tm, tn, D, [], [((T, BW), F32, blk, _ij)], epi_store)
    (g_a,) = _mm("branch_a_dw", "tn", [(dya, D)], ya_pre, D, AW, T, 1024, 1024, tk_t, [], [((D, AW), BF16, gblk, _ij)], epi_store)
    (g_b,) = _mm("branch_b_dw", "tn", [(dyb, D)], attn, D, BW, T, 1024, 1024, tk_t, [], [((D, BW), BF16, gblk, _ij)], epi_store)
    dqa, dfa, dia, dgg, p_lb, p_og = _hgrn_bwd(proj, st, dya_pre, lbl, og, tt)
    dqb, dkv, p_qg, p_kg, p_sk = _swa_bwd(proj, dattn, qg, kg, sk)
    pieces = [(dqa, AW), (dfa, AW), (dia, AW), (dgg, AW), (dqb, BW), (dkv, 2 * KVW), (dga, D), (dgb, D)]
    (dh,) = _mm("in_proj_dx", "nn", pieces, win_t, T, D, IN_W, tm, 1024, 512, [], [((T, D), F32, (tm, 1024), _ij)], epi_store)
    (g_in,) = _mm("in_proj_dw", "tn", pieces, h, IN_W, D, T, 512, 1024, tk_t, [], [((IN_W, D), BF16, (512, 1024), _ij)], epi_store)
    dx, p_sh1, p_sc1, p_g1 = _rms_mod_bwd("rms1_bwd", dh, x, g1, sc1, dx1, tr)

    partials = dict(sh1=p_sh1, sc1=p_sc1, gt1=p_gt1, sh2=p_sh2, sc2=p_sc2, gt2=p_gt2, g1=p_g1, g2=p_g2,
                    lb=p_lb, og=p_og, qg=p_qg, kg=p_kg, sk=p_sk, loss=p_loss)
    grads = dict(w_in=g_in, w_branch_a=g_a, w_branch_b=g_b, w_out=g_out, w_mlp_in=g_mi, w_mlp_out=g_mo)
    return dx, grads, partials


def _mesh_pos():
    return lax.axis_index("x"), lax.axis_index("y"), lax.axis_index("c")


def _flip(pos, k):
    return tuple(1 - p if (k >> s) & 1 else p for p, s in zip(pos, (2, 1, 0)))


def _index(pos):
    return 4 * pos[0] + 2 * pos[1] + pos[2]


def _exchange_slots(buf, send_sems, recv_sems):
    me = _mesh_pos()
    mine = buf.at[_index(me)]
    sends = []
    for k in range(1, N_DEV):
        cp = pltpu.make_async_remote_copy(src_ref=mine, dst_ref=mine, send_sem=send_sems.at[k - 1],
                                          recv_sem=recv_sems.at[k - 1], device_id=_flip(me, k), device_id_type=MESH)
        cp.start()
        sends.append(cp)
    for k in range(1, N_DEV):
        theirs = buf.at[_index(_flip(me, k))]
        pltpu.make_async_remote_copy(src_ref=theirs, dst_ref=theirs, send_sem=send_sems.at[k - 1],
                                     recv_sem=recv_sems.at[k - 1], device_id=_flip(me, k), device_id_type=MESH).wait_recv()
    for cp in sends:
        cp.wait_send()


_VMEM = pl.BlockSpec(memory_space=pltpu.VMEM)
_ANY = pl.BlockSpec(memory_space=pl.ANY)
_SEMS = lambda n: pltpu.SemaphoreType.DMA((n,))
ADA_W = N_MOD * D // N_DEV


def _ada_mod(c, w_ada, b_shard):
    def body(c_ref, w_ref, b_ref, mod_ref, sc_ref, cbuf, mbuf, s1, r1, s2, r2):
        me = _index(_mesh_pos())
        cbuf[me] = c_ref[...]
        _exchange_slots(cbuf, s1, r1)
        row = lax.broadcasted_iota(jnp.int32, (N_DEV, D), 0)
        call = jnp.zeros((N_DEV, D), F32)
        for d in range(N_DEV):
            call = jnp.where(row == d, cbuf[d], call)
        sc = call * _sigmoid(call)
        sc_ref[...] = sc
        mbuf[me] = _dot(sc, w_ref[...]) + b_ref[...]
        _exchange_slots(mbuf, s2, r2)
        for s in range(N_DEV):
            mod_ref[:, s * ADA_W:(s + 1) * ADA_W] = mbuf[s, pl.ds(me, 1), :]

    return pl.pallas_call(
        body, in_specs=[_VMEM, _VMEM, _VMEM], out_specs=[_VMEM, _VMEM],
        out_shape=[jax.ShapeDtypeStruct((1, N_MOD * D), F32), jax.ShapeDtypeStruct((N_DEV, D), F32)],
        scratch_shapes=[pltpu.VMEM((N_DEV, 1, D), F32), pltpu.VMEM((N_DEV, N_DEV, ADA_W), F32),
                        _SEMS(N_DEV - 1), _SEMS(N_DEV - 1), _SEMS(N_DEV - 1), _SEMS(N_DEV - 1)],
        name="ada_mod", compiler_params=pltpu.CompilerParams(vmem_limit_bytes=VMEM_LIMIT),
    )(c, w_ada, b_shard)


def _allgather_weights(shards):
    n = len(shards)

    def body(*refs):
        ins, outs = refs[:n], refs[n:2 * n]
        send_sems, recv_sems, local_sems = refs[2 * n:]
        x, y, c = _mesh_pos()
        me, sib = (x, y, c), (x, y, 1 - c)
        chips = [(1 - x, y), (x, 1 - y), (1 - x, 1 - y)]

        def rows(a, p):
            rs = shards[a].shape[0]
            return outs[a].at[pl.ds(_index(p) * rs, rs), :]

        def copy(a, k, block, to, src=None):
            return pltpu.make_async_remote_copy(
                src_ref=rows(a, block) if src is None else src, dst_ref=rows(a, block),
                send_sem=send_sems.at[7 * a + k], recv_sem=recv_sems.at[7 * a + k], device_id=to, device_id_type=MESH)

        mine = [pltpu.make_async_copy(ins[a], rows(a, me), local_sems.at[a]) for a in range(n)]
        for cp in mine:
            cp.start()
        first = []
        for a in range(n):
            first.append(copy(a, 0, me, sib, src=ins[a]))
            first += [copy(a, 1 + j, me, (*chip, c), src=ins[a]) for j, chip in enumerate(chips)]
        for cp in first:
            cp.start()
        passed = []
        for j, chip in enumerate(chips):
            for a in range(n):
                copy(a, 1 + j, (*chip, c), me).wait_recv()
                cp = copy(a, 4 + j, (*chip, c), sib)
                cp.start()
                passed.append(cp)
        for a in range(n):
            copy(a, 0, sib, me).wait_recv()
            for j, chip in enumerate(chips):
                copy(a, 4 + j, (*chip, 1 - c), me).wait_recv()
        for cp in first + passed:
            cp.wait_send()
        for cp in mine:
            cp.wait()

    return pl.pallas_call(
        body, in_specs=[_ANY] * n, out_specs=[_ANY] * n,
        out_shape=[jax.ShapeDtypeStruct((N_DEV * s.shape[0], s.shape[1]), s.dtype) for s in shards],
        scratch_shapes=[_SEMS(7 * n), _SEMS(7 * n), _SEMS(n)], name="weights_allgather",
    )(*shards)


def _grad_exchange(grads):
    n = len(grads)

    def body(*refs):
        ins, outs = refs[:n], refs[n:2 * n]
        send_sems, recv_sems, local_sems = refs[2 * n:]
        me = _mesh_pos()

        def block(a, p):
            rs = grads[a].shape[0] // N_DEV
            return ins[a].at[pl.ds(_index(p) * rs, rs), :]

        def copy(a, k):
            peer = _flip(me, k)
            return pltpu.make_async_remote_copy(
                src_ref=block(a, peer), dst_ref=outs[a].at[_index(me)], send_sem=send_sems.at[7 * a + k - 1],
                recv_sem=recv_sems.at[7 * a + k - 1], device_id=peer, device_id_type=MESH)

        def arrival(a, k):
            peer = _flip(me, k)
            return pltpu.make_async_remote_copy(
                src_ref=block(a, peer), dst_ref=outs[a].at[_index(peer)], send_sem=send_sems.at[7 * a + k - 1],
                recv_sem=recv_sems.at[7 * a + k - 1], device_id=peer, device_id_type=MESH)

        mine = [pltpu.make_async_copy(block(a, me), outs[a].at[_index(me)], local_sems.at[a]) for a in range(n)]
        for cp in mine:
            cp.start()
        sends = [copy(a, k) for k in range(1, N_DEV) for a in range(n)]
        for cp in sends:
            cp.start()
        for k in range(1, N_DEV):
            for a in range(n):
                arrival(a, k).wait_recv()
        for cp in sends:
            cp.wait_send()
        for cp in mine:
            cp.wait()

    return pl.pallas_call(
        body, in_specs=[_ANY] * n, out_specs=[_ANY] * n,
        out_shape=[jax.ShapeDtypeStruct((N_DEV, g.shape[0] // N_DEV, g.shape[1]), g.dtype) for g in grads],
        scratch_shapes=[_SEMS(7 * n), _SEMS(7 * n), _SEMS(n)], name="grad_exchange",
    )(*grads)


SMALL_SEGS = (("b_ada", N_MOD * D), ("norm1_gain", D), ("norm2_gain", D), ("lb0", AW), ("lb1", AW),
              ("hgrn_o_gain", AW), ("q_norm_gain", 128), ("k_norm_gain", 128), ("sinks", 128))
SMALL_W = sum(w for _, w in SMALL_SEGS)
X_SEGS = (("sh1", D), ("sc1", D), ("gt1", D), ("sh2", D), ("sc2", D), ("gt2", D), ("g1", D), ("g2", D),
          ("lb", AW), ("og", AW), ("qg", 128), ("kg", 128), ("sk", 128), ("loss", 128))
X_W = sum(w for _, w in X_SEGS)


def _offsets(segs):
    out, o = {}, 0
    for name, w in segs:
        out[name] = (o, w)
        o += w
    return out


def _small_reduce(parts, lb_logits):
    xo, so = _offsets(X_SEGS), _offsets(SMALL_SEGS)
    names = [nm for nm, _ in X_SEGS]

    def body(*refs):
        p_refs = dict(zip(names, refs[:len(names)]))
        lbl_ref, allx, gs_ref, loss_ref, send_sems, recv_sems = refs[len(names):]
        me = _index(_mesh_pos())
        for nm, (o, w) in xo.items():
            if nm == "loss":
                allx[me, :, o:o + w] = jnp.broadcast_to(jnp.sum(p_refs[nm][...]), (1, w))
            else:
                allx[me, :, o:o + w] = jnp.sum(p_refs[nm][...], axis=0, keepdims=True)
        _exchange_slots(allx, send_sems, recv_sems)
        tot = allx[0]
        for d in range(1, N_DEV):
            tot = tot + allx[d]
        seg = lambda nm: tot[:, xo[nm][0]:xo[nm][0] + xo[nm][1]]

        def put(nm, v):
            gs_ref[:, so[nm][0]:so[nm][0] + so[nm][1]] = v

        put("b_ada", tot[:, 0:N_MOD * D])
        put("norm1_gain", seg("g1"))
        put("norm2_gain", seg("g2"))
        lbl = lbl_ref[...]
        lb = _sigmoid(lbl[0:1, :] - lbl[1:2, :])
        dl0 = seg("lb") * lb * (1.0 - lb)
        put("lb0", dl0)
        put("lb1", -dl0)
        put("hgrn_o_gain", seg("og"))
        put("q_norm_gain", seg("qg"))
        put("k_norm_gain", seg("kg"))
        put("sinks", seg("sk"))
        loss_ref[...] = seg("loss")

    return pl.pallas_call(
        body, in_specs=[_VMEM] * (len(names) + 1), out_specs=[_VMEM, _VMEM, _VMEM],
        out_shape=[jax.ShapeDtypeStruct((N_DEV, 1, X_W), F32), jax.ShapeDtypeStruct((1, SMALL_W), F32),
                   jax.ShapeDtypeStruct((1, 128), F32)],
        scratch_shapes=[_SEMS(N_DEV - 1), _SEMS(N_DEV - 1)], name="small_reduce",
        compiler_params=pltpu.CompilerParams(vmem_limit_bytes=VMEM_LIMIT),
    )(*[parts[nm] for nm in names], lb_logits)


def _adamw_math(w, g, m, v):
    m = B1 * m + (1.0 - B1) * g
    v = B2 * v + (1.0 - B2) * (g * g)
    m_hat = m / (1.0 - B1 ** STEP)
    v_hat = v / (1.0 - B2 ** STEP)
    return -LR * (m_hat / (jnp.sqrt(v_hat) + ADAM_EPS) + WD * w), m, v


def _sum_slots(name, recv, tr):
    _, rs, cols = recv.shape

    def body(r_ref, g_ref):
        acc = r_ref[0].astype(F32)
        for d in range(1, N_DEV):
            acc = acc + r_ref[d].astype(F32)
        g_ref[...] = acc

    return pl.pallas_call(
        body, grid=(rs // tr,), in_specs=[pl.BlockSpec((N_DEV, tr, cols), lambda i: (0, i, 0))],
        out_specs=pl.BlockSpec((tr, cols), lambda i: (i, 0)), out_shape=jax.ShapeDtypeStruct((rs, cols), F32),
        name=name, compiler_params=_params(("parallel",)),
    )(recv)


def _adamw(name, w, g, m, v, tr):
    rows, cols = w.shape

    def body(w_ref, g_ref, m_ref, v_ref, d_ref, nm_ref, nv_ref):
        d_ref[...], nm_ref[...], nv_ref[...] = _adamw_math(w_ref[...], g_ref[...], m_ref[...], v_ref[...])

    spec = pl.BlockSpec((tr, cols), lambda i: (i, 0))
    shape = jax.ShapeDtypeStruct((rows, cols), F32)
    return pl.pallas_call(
        body, grid=(rows // tr,), in_specs=[spec] * 4, out_specs=[spec] * 3, out_shape=[shape] * 3, name=name,
        compiler_params=_params(("parallel",)),
    )(w, g, m, v)


def _ada_update(sc_t, dmod_cols, w, m, v, tr):
    rows, cols = w.shape

    def body(s_ref, d_ref, w_ref, m_ref, v_ref, g_ref, dl_ref, nm_ref, nv_ref):
        g = jnp.dot(s_ref[...], d_ref[...], precision=lax.Precision.HIGHEST, preferred_element_type=F32)
        g_ref[...] = g
        dl_ref[...], nm_ref[...], nv_ref[...] = _adamw_math(w_ref[...], g, m_ref[...], v_ref[...])

    spec = pl.BlockSpec((tr, cols), lambda i: (i, 0))
    shape = jax.ShapeDtypeStruct((rows, cols), F32)
    return pl.pallas_call(
        body, grid=(rows // tr,),
        in_specs=[pl.BlockSpec((tr, N_DEV), lambda i: (i, 0)), pl.BlockSpec((N_DEV, cols), lambda i: (0, 0)), spec, spec, spec],
        out_specs=[spec] * 4, out_shape=[shape] * 4, name="ada_update", compiler_params=_params(("parallel",)),
    )(sc_t, dmod_cols, w, m, v)


BIG = ("w_in", "w_branch_a", "w_branch_b", "w_out", "w_mlp_in", "w_mlp_out")
COLUMN_SHARDED = ("w_in", "w_branch_a", "w_branch_b", "w_mlp_in")
WEIGHTS = ("w_ada", "b_ada", "norm1_gain", "w_in", "lb_logits", "hgrn_o_gain", "q_norm_gain", "k_norm_gain", "sinks",
           "w_branch_a", "w_branch_b", "w_out", "norm2_gain", "w_mlp_in", "w_mlp_out")


def _pack_small(p):
    lb = p["lb_logits"]
    src = dict(p, lb0=lb[0:1], lb1=lb[1:2])
    return jnp.concatenate([jnp.pad(src[nm], ((0, 0), (0, w - src[nm].shape[1]))) for nm, w in SMALL_SEGS], axis=1)


def _unpack_small(vec, shapes):
    so = _offsets(SMALL_SEGS)
    out = {}
    for nm, shp in shapes.items():
        if nm == "lb_logits":
            o = so["lb0"][0]
            out[nm] = vec[0, o:o + 2 * AW].reshape(2, AW)
        else:
            o = so[nm][0]
            out[nm] = vec[:, o:o + shp[1]]
    return out


def kernel(x, c, w_ada, b_ada, norm1_gain, w_in, lb_logits, hgrn_o_gain, q_norm_gain, k_norm_gain, sinks, w_branch_a, w_branch_b, w_out, norm2_gain, w_mlp_in, w_mlp_out, loss_target, m_w_ada, m_b_ada, m_norm1_gain, m_w_in, m_lb_logits, m_hgrn_o_gain, m_q_norm_gain, m_k_norm_gain, m_sinks, m_w_branch_a, m_w_branch_b, m_w_out, m_norm2_gain, m_w_mlp_in, m_w_mlp_out, v_w_ada, v_b_ada, v_norm1_gain, v_w_in, v_lb_logits, v_hgrn_o_gain, v_q_norm_gain, v_k_norm_gain, v_sinks, v_w_branch_a, v_w_branch_b, v_w_out, v_norm2_gain, v_w_mlp_in, v_w_mlp_out):
    w = dict(w_ada=w_ada, b_ada=b_ada, norm1_gain=norm1_gain, w_in=w_in, lb_logits=lb_logits, hgrn_o_gain=hgrn_o_gain,
             q_norm_gain=q_norm_gain, k_norm_gain=k_norm_gain, sinks=sinks, w_branch_a=w_branch_a, w_branch_b=w_branch_b,
             w_out=w_out, norm2_gain=norm2_gain, w_mlp_in=w_mlp_in, w_mlp_out=w_mlp_out)
    m = dict(w_ada=m_w_ada, b_ada=m_b_ada, norm1_gain=m_norm1_gain, w_in=m_w_in, lb_logits=m_lb_logits,
             hgrn_o_gain=m_hgrn_o_gain, q_norm_gain=m_q_norm_gain, k_norm_gain=m_k_norm_gain, sinks=m_sinks,
             w_branch_a=m_w_branch_a, w_branch_b=m_w_branch_b, w_out=m_w_out, norm2_gain=m_norm2_gain,
             w_mlp_in=m_w_mlp_in, w_mlp_out=m_w_mlp_out)
    v = dict(w_ada=v_w_ada, b_ada=v_b_ada, norm1_gain=v_norm1_gain, w_in=v_w_in, lb_logits=v_lb_logits,
             hgrn_o_gain=v_hgrn_o_gain, q_norm_gain=v_q_norm_gain, k_norm_gain=v_k_norm_gain, sinks=v_sinks,
             w_branch_a=v_w_branch_a, w_branch_b=v_w_branch_b, w_out=v_w_out, norm2_gain=v_norm2_gain,
             w_mlp_in=v_w_mlp_in, w_mlp_out=v_w_mlp_out)
    for d in (w, m, v):
        for nm in ("w_ada",) + BIG:
            d[nm] = d[nm][0]
    me = _index(_mesh_pos())

    shards = [(w[nm].T if nm in COLUMN_SHARDED else w[nm]).astype(BF16) for nm in BIG]
    gathered = _allgather_weights(shards)
    b_shard = lax.dynamic_slice(b_ada, (0, me * ADA_W), (1, ADA_W))
    mod, sc_all = _ada_mod(c, w["w_ada"], b_shard)

    dx, grads, parts = _local_step(x[0], loss_target[0], mod, norm1_gain, norm2_gain, lb_logits, hgrn_o_gain,
                                   q_norm_gain, k_norm_gain, sinks, *gathered)

    allx, g_small, loss = _small_reduce(parts, lb_logits)
    recv = _grad_exchange([grads[nm] for nm in BIG])

    grad, delta, new_m, new_v = {}, {}, {}, {}
    for nm, r in zip(BIG, recv):
        rs = r.shape[1]
        g = _sum_slots("sum_" + nm, r, 64 if rs % 256 else 256)
        g = g.T if nm in COLUMN_SHARDED else g
        rows = g.shape[0]
        grad[nm] = g
        delta[nm], new_m[nm], new_v[nm] = _adamw("adamw_" + nm, w[nm], g, m[nm], v[nm], 128 if rows % 128 == 0 else rows)

    dmod_cols = lax.dynamic_slice(allx[:, 0, :], (0, me * ADA_W), (N_DEV, ADA_W))
    grad["w_ada"], delta["w_ada"], new_m["w_ada"], new_v["w_ada"] = _ada_update(
        sc_all.T, dmod_cols, w["w_ada"], m["w_ada"], v["w_ada"], 256)

    small_names = [nm for nm in WEIGHTS if nm not in BIG and nm != "w_ada"]
    shapes = {nm: w[nm].shape for nm in small_names}
    ds, ms, vs = _adamw("adamw_small", _pack_small(w), g_small, _pack_small(m), _pack_small(v), 1)
    for dst, vec in ((grad, g_small), (delta, ds), (new_m, ms), (new_v, vs)):
        dst.update(_unpack_small(vec, shapes))

    def full(d, nm):
        return d[nm][None] if nm in BIG or nm == "w_ada" else d[nm]

    return (loss[0, 0], dx[None], *[full(grad, nm) for nm in WEIGHTS], *[full(delta, nm) for nm in WEIGHTS],
            *[full(new_m, nm) for nm in WEIGHTS], *[full(new_v, nm) for nm in WEIGHTS])
```

```python
import functools

import jax
import jax.numpy as jnp
from jax import lax
from jax.experimental import pallas as pl
from jax.experimental.pallas import tpu as pltpu

F32 = jnp.float32
BF16 = jnp.bfloat16
MESH = pl.DeviceIdType.MESH

N_DEV = 8
D = 2048
A_HEADS, A_HD, CHUNK = 8, 128, 64
AW = A_HEADS * A_HD
Q_HEADS, KV_HEADS, GROUP, B_HD, BLK = 16, 4, 4, 64, 128
BW = Q_HEADS * B_HD
KVW = KV_HEADS * B_HD
HID = 4 * D
IN_W = 4 * AW + BW + 2 * KVW + 2 * D
OFF_QA, OFF_FA, OFF_IA, OFF_GA = 0, AW, 2 * AW, 3 * AW
OFF_QB = 4 * AW
OFF_KB = OFF_QB + BW
OFF_VB = OFF_KB + KVW
OFF_GTA = OFF_VB + KVW
OFF_GTB = OFF_GTA + D
N_MOD = 6
EPS = 1e-6
LR, B1, B2, ADAM_EPS, WD, STEP = 1e-3, 0.9, 0.999, 1e-8, 0.01, 10
NEG = -1e30

VMEM_LIMIT = 56 * 1024 * 1024

NN = (((1,), (0,)), ((), ()))
NT = (((1,), (1,)), ((), ()))
TN = (((0,), (0,)), ((), ()))


def _dot(a, b, dims=NN):
    return lax.dot_general(a.astype(BF16), b.astype(BF16), dims, preferred_element_type=F32)


def _params(sem):
    return pltpu.CompilerParams(dimension_semantics=sem, vmem_limit_bytes=VMEM_LIMIT)


def _sigmoid(x):
    return 1.0 / (1.0 + jnp.exp(-x))


def _fold8(v):
    r, n = v.shape
    return jnp.sum(v.reshape(r // 8, 8, n), axis=0)


_VMEM = pl.BlockSpec(memory_space=pltpu.VMEM)
_ANY = pl.BlockSpec(memory_space=pl.ANY)
_SEMS = lambda n: pltpu.SemaphoreType.DMA((n,))


def _mesh_pos():
    return lax.axis_index("x"), lax.axis_index("y"), lax.axis_index("c")


def _flip(pos, k):
    return tuple(1 - p if (k >> s) & 1 else p for p, s in zip(pos, (2, 1, 0)))


def _index(pos):
    return 4 * pos[0] + 2 * pos[1] + pos[2]


class _Job:
    def __init__(self, ins, out_shape, sems, start, finish):
        self.ins, self.out_shape, self.sems, self.start, self.finish = list(ins), list(out_shape), list(sems), start, finish


def _pcall(body, *, grid, in_specs, out_specs, out_shape, scratch_shapes, name, semantics, args, job=None):
    if job is None:
        outs = pl.pallas_call(body, grid=grid, in_specs=in_specs, out_specs=out_specs, out_shape=out_shape,
                              scratch_shapes=scratch_shapes, name=name, compiler_params=_params(semantics))(*args)
        return list(outs), []
    n_in, n_out, n_scr = len(in_specs), len(out_specs), len(scratch_shapes)
    j_in, j_out = len(job.ins), len(job.out_shape)
    steps = tuple(grid)

    def carrier(*refs):
        o = 0
        main_in, o = refs[o:o + n_in], o + n_in
        job_in, o = refs[o:o + j_in], o + j_in
        main_out, o = refs[o:o + n_out], o + n_out
        job_out, o = refs[o:o + j_out], o + j_out
        main_scr, job_sems = refs[o:o + n_scr], refs[o + n_scr:]
        ids = [pl.program_id(a) for a in range(len(steps))]
        first = functools.reduce(lambda p, q: p & q, [i == 0 for i in ids])
        last = functools.reduce(lambda p, q: p & q, [i == s - 1 for i, s in zip(ids, steps)])

        @pl.when(first)
        def _():
            job.start(job_in, job_out, job_sems)

        body(*main_in, *main_out, *main_scr)

        @pl.when(last)
        def _():
            job.finish(job_in, job_out, job_sems)

    outs = pl.pallas_call(
        carrier, grid=grid, in_specs=list(in_specs) + [_ANY] * j_in, out_specs=list(out_specs) + [_ANY] * j_out,
        out_shape=list(out_shape) + job.out_shape, scratch_shapes=list(scratch_shapes) + job.sems, name=name,
        compiler_params=_params(("arbitrary",) * len(steps)),
    )(*args, *job.ins)
    return list(outs[:n_out]), list(outs[n_out:])


def _run_job(name, job):
    j_in, j_out = len(job.ins), len(job.out_shape)

    def body(*refs):
        ins, outs, sems = refs[:j_in], refs[j_in:j_in + j_out], refs[j_in + j_out:]
        job.start(ins, outs, sems)
        job.finish(ins, outs, sems)

    return list(pl.pallas_call(body, in_specs=[_ANY] * j_in, out_specs=[_ANY] * j_out, out_shape=job.out_shape,
                               scratch_shapes=job.sems, name=name)(*job.ins))


def _gather_job(shards):
    n = len(shards)

    def copies(ins, outs, sems):
        send_sems, recv_sems, local_sems = sems
        x, y, c = _mesh_pos()
        me, sib = (x, y, c), (x, y, 1 - c)
        chips = [(1 - x, y), (x, 1 - y), (1 - x, 1 - y)]

        def rows(a, p):
            rs = shards[a].shape[0]
            return outs[a].at[pl.ds(_index(p) * rs, rs), :]

        def copy(a, k, block, to, src=None):
            return pltpu.make_async_remote_copy(
                src_ref=rows(a, block) if src is None else src, dst_ref=rows(a, block),
                send_sem=send_sems.at[7 * a + k], recv_sem=recv_sems.at[7 * a + k], device_id=to, device_id_type=MESH)

        mine = [pltpu.make_async_copy(ins[a], rows(a, me), local_sems.at[a]) for a in range(n)]
        first = []
        for a in range(n):
            first.append(copy(a, 0, me, sib, src=ins[a]))
            first += [copy(a, 1 + j, me, (*chip, c), src=ins[a]) for j, chip in enumerate(chips)]
        return me, sib, c, chips, copy, mine, first

    def start(ins, outs, sems):
        *_, mine, first = copies(ins, outs, sems)
        for cp in mine + first:
            cp.start()

    def finish(ins, outs, sems):
        me, sib, c, chips, copy, mine, first = copies(ins, outs, sems)
        passed = []
        for j, chip in enumerate(chips):
            for a in range(n):
                copy(a, 1 + j, (*chip, c), me).wait_recv()
                cp = copy(a, 4 + j, (*chip, c), sib)
                cp.start()
                passed.append(cp)
        for a in range(n):
            copy(a, 0, sib, me).wait_recv()
            for j, chip in enumerate(chips):
                copy(a, 4 + j, (*chip, 1 - c), me).wait_recv()
        for cp in first + passed:
            cp.wait_send()
        for cp in mine:
            cp.wait()

    return _Job(shards, [jax.ShapeDtypeStruct((N_DEV * s.shape[0], s.shape[1]), s.dtype) for s in shards],
                [_SEMS(7 * n), _SEMS(7 * n), _SEMS(n)], start, finish)


def _pair_job(grads):
    n = len(grads)

    def copies(ins, outs, sems):
        send_sems, recv_sems = sems
        x, y, c = _mesh_pos()
        out = []
        for a in range(n):
            rs = grads[a].shape[0] // N_DEV
            for q in range(4):
                blk = ins[a].at[pl.ds((2 * q + 1 - c) * rs, rs), :]
                out.append(pltpu.make_async_remote_copy(
                    src_ref=blk, dst_ref=outs[a].at[q], send_sem=send_sems.at[4 * a + q], recv_sem=recv_sems.at[4 * a + q],
                    device_id=(x, y, 1 - c), device_id_type=MESH))
        return out

    def start(ins, outs, sems):
        for cp in copies(ins, outs, sems):
            cp.start()

    def finish(ins, outs, sems):
        for cp in copies(ins, outs, sems):
            cp.wait()

    return _Job(grads, [jax.ShapeDtypeStruct((4, g.shape[0] // N_DEV, g.shape[1]), g.dtype) for g in grads],
                [_SEMS(4 * n), _SEMS(4 * n)], start, finish)


def _chip_job(sums):
    n = len(sums)

    def copies(ins, outs, sems):
        send_sems, recv_sems = sems
        x, y, c = _mesh_pos()
        out = []
        for a in range(n):
            for r in (1, 2, 3):
                px, py = (1 - x if r & 2 else x), (1 - y if r & 1 else y)
                out.append(pltpu.make_async_remote_copy(
                    src_ref=ins[a].at[2 * px + py], dst_ref=outs[a].at[r - 1], send_sem=send_sems.at[3 * a + r - 1],
                    recv_sem=recv_sems.at[3 * a + r - 1], device_id=(px, py, c), device_id_type=MESH))
        return out

    def start(ins, outs, sems):
        for cp in copies(ins, outs, sems):
            cp.start()

    def finish(ins, outs, sems):
        for cp in copies(ins, outs, sems):
            cp.wait()

    return _Job(sums, [jax.ShapeDtypeStruct((3,) + s.shape[1:], s.dtype) for s in sums],
                [_SEMS(3 * n), _SEMS(3 * n)], start, finish)


def _mm(name, form, a_list, b, M, N, K, tm, tn, tk, extras, outs, epi, job=None):
    nI, nJ, nK = M // tm, N // tn, K // tk
    assert nI * tm == M and nJ * tn == N and nK * tk == K
    dims = {"nn": NN, "nt": NT, "tn": TN}[form]
    split = tm if form == "tn" else tk
    starts, s = [], 0
    for _, w in a_list:
        assert w % split == 0
        starts.append(s // split)
        s += w
    counts = [w // split for _, w in a_list]
    assert s == (M if form == "tn" else K)
    nP = len(a_list)

    def a_spec(p):
        st, cn = starts[p], counts[p]
        if form == "tn":
            return pl.BlockSpec((tk, tm), lambda i, j, k: (jnp.where((i >= st) & (i < st + cn), k, 0),
                                                           jnp.clip(i - st, 0, cn - 1)))
        return pl.BlockSpec((tm, tk), lambda i, j, k: (i, jnp.clip(k - st, 0, cn - 1)))

    if form == "nn":
        b_spec = pl.BlockSpec((tk, tn), lambda i, j, k: (k, j))
    elif form == "nt":
        b_spec = pl.BlockSpec((tn, tk), lambda i, j, k: (j, k))
    else:
        b_spec = pl.BlockSpec((tk, tn), lambda i, j, k: (k, j))
    in_specs = [a_spec(p) for p in range(nP)] + [b_spec] + [pl.BlockSpec(bs, im) for _, bs, im in extras]
    out_shape = [jax.ShapeDtypeStruct(s_, d_) for s_, d_, _, _ in outs]
    out_specs = [pl.BlockSpec(bs, im) for _, _, bs, im in outs]
    nE, nO = len(extras), len(outs)

    def body(*refs):
        a_refs, b_ref = refs[:nP], refs[nP]
        ex, ou = refs[nP + 1:nP + 1 + nE], refs[nP + 1 + nE:nP + 1 + nE + nO]
        sel = pl.program_id(0) if form == "tn" else pl.program_id(2)

        def partial_of(p):
            return lax.dot_general(a_refs[p][...], b_ref[...], dims, preferred_element_type=F32)

        if nK == 1 and nP == 1:
            epi(partial_of(0), ex, ou)
            return
        acc = refs[-1]
        k = pl.program_id(2)
        for p in range(nP):
            inside = (sel >= starts[p]) & (sel < starts[p] + counts[p]) if nP > 1 else None

            def first(p=p):
                acc[...] = partial_of(p)

            def later(p=p):
                acc[...] += partial_of(p)

            if nP == 1:
                pl.when(k == 0)(first)
                pl.when(k > 0)(later)
            else:
                pl.when(inside & (k == 0))(first)
                pl.when(inside & (k > 0))(later)

        @pl.when(k == nK - 1)
        def _():
            epi(acc[...], ex, ou)

    scratch = [] if (nK == 1 and nP == 1) else [pltpu.VMEM((tm, tn), F32)]
    res, job_res = _pcall(
        body, grid=(nI, nJ, nK), in_specs=in_specs, out_specs=out_specs, out_shape=out_shape, scratch_shapes=scratch,
        name=name, semantics=("parallel", "parallel", "arbitrary"),
        args=[a for a, _ in a_list] + [b] + [e for e, _, _ in extras], job=job)
    return res if job is None else (res, job_res)


def _rms_mod_fwd(name, x, gain, sc, sh, tr):
    T = x.shape[0]

    def body(x_ref, g_ref, sc_ref, sh_ref, h_ref):
        xv = x_ref[...]
        rstd = lax.rsqrt(jnp.mean(xv * xv, axis=-1, keepdims=True) + EPS)
        h_ref[...] = ((xv * rstd * g_ref[...]) * (1.0 + sc_ref[...]) + sh_ref[...]).astype(BF16)

    row = pl.BlockSpec((tr, D), lambda i: (i, 0))
    vec = pl.BlockSpec((1, D), lambda i: (0, 0))
    return pl.pallas_call(
        body, grid=(T // tr,), in_specs=[row, vec, vec, vec], out_specs=row,
        out_shape=jax.ShapeDtypeStruct((T, D), BF16), name=name, compiler_params=_params(("parallel",)),
    )(x, gain, sc, sh)


def _rms_mod_bwd(name, dh, x, gain, sc, dres, tr, gate=None, mo=None):
    T = x.shape[0]
    nR = T // tr
    with_gate = gate is not None

    def body(*refs):
        if with_gate:
            dh_ref, x_ref, g_ref, sc_ref, dr_ref, gt_ref, mo_ref, dx_ref, psh, psc, pg, dmo_ref, pgt = refs
        else:
            dh_ref, x_ref, g_ref, sc_ref, dr_ref, dx_ref, psh, psc, pg = refs
        xv, dhv = x_ref[...], dh_ref[...]
        rstd = lax.rsqrt(jnp.mean(xv * xv, axis=-1, keepdims=True) + EPS)
        xhat = xv * rstd
        n = xhat * g_ref[...]
        dn = dhv * (1.0 + sc_ref[...])
        dxhat = dn * g_ref[...]
        dx = dr_ref[...] + rstd * (dxhat - xhat * jnp.mean(dxhat * xhat, axis=-1, keepdims=True))
        dx_ref[...] = dx
        psh[...] = _fold8(dhv)
        psc[...] = _fold8(dhv * n)
        pg[...] = _fold8(dn * xhat)
        if with_gate:
            dmo_ref[...] = (gt_ref[...] * dx).astype(BF16)
            pgt[...] = _fold8(dx * mo_ref[...].astype(F32))

    row = pl.BlockSpec((tr, D), lambda i: (i, 0))
    vec = pl.BlockSpec((1, D), lambda i: (0, 0))
    part = pl.BlockSpec((8, D), lambda i: (i, 0))
    p_shape = jax.ShapeDtypeStruct((nR * 8, D), F32)
    ins = [dh, x, gain, sc, dres] + ([gate, mo] if with_gate else [])
    in_specs = [row, row, vec, vec, row] + ([vec, row] if with_gate else [])
    out_shape = [jax.ShapeDtypeStruct((T, D), F32), p_shape, p_shape, p_shape]
    out_specs = [row, part, part, part]
    if with_gate:
        out_shape += [jax.ShapeDtypeStruct((T, D), BF16), p_shape]
        out_specs += [row, part]
    return pl.pallas_call(
        body, grid=(nR,), in_specs=in_specs, out_specs=out_specs, out_shape=out_shape, name=name,
        compiler_params=_params(("parallel",)),
    )(*ins)


def _split3(v):
    h = v.astype(BF16)
    r1 = v - h.astype(F32)
    m = r1.astype(BF16)
    lo = (r1 - m.astype(F32)).astype(BF16)
    return h, m, lo


def _tri_mm(tri, v, dims=NN):
    h, m, lo = _split3(v)
    t = tri.astype(BF16)
    mm = lambda p: lax.dot_general(t, p, dims, preferred_element_type=F32)
    return (mm(lo) + mm(m)) + mm(h)


def _hgrn_chunk_terms(q, fl, lb):
    sig = _sigmoid(fl)
    f = lb + (1.0 - lb) * sig
    lf = jnp.log(f)
    kk = 1.0 - f
    sq = _sigmoid(q)
    qf = q * sq
    return sig, f, lf, kk, sq, qf


def _causal(n):
    r = lax.broadcasted_iota(jnp.int32, (n, n), 0)
    c = lax.broadcasted_iota(jnp.int32, (n, n), 1)
    return r >= c


def _hgrn_fwd(proj, lb_logits, o_gain, tt, job=None):
    T = proj.shape[0]
    nT, ncl = T // tt, tt // CHUNK
    C = CHUNK

    def body(q_ref, f_ref, i_ref, g_ref, lbl_ref, og_ref, y_ref, st_ref, S):
        @pl.when(pl.program_id(1) == 0)
        def _():
            S[...] = jnp.zeros_like(S)

        lbl = lbl_ref[...]
        lb = _sigmoid(lbl[0:1, :] - lbl[1:2, :])
        og = og_ref[...]
        tri = _causal(C)
        for ci in range(ncl):
            sl = pl.ds(ci * C, C)
            q, fl, v, g = q_ref[sl, :], f_ref[sl, :], i_ref[sl, :], g_ref[sl, :]
            _, _, lf, kk, _, qf = _hgrn_chunk_terms(q, fl, lb)
            b = _tri_mm(tri, lf)
            bm, bl = b[C // 2 - 1:C // 2, :], b[C - 1:C, :]
            qd, kd = qf * jnp.exp(b - bm), kk * jnp.exp(bm - b)
            A = jnp.where(tri, _dot(qd, kd, NT), 0.0)
            ST0 = S[...]
            o = _dot(A, v) + _dot(qf * jnp.exp(b), ST0, NT)
            ke = kk * jnp.exp(bl - b)
            st_ref[0, ci] = ST0
            S[...] = ST0 * jnp.exp(bl) + _dot(v, ke, TN)
            r = lax.rsqrt(jnp.mean(o * o, axis=-1, keepdims=True) + EPS)
            y_ref[sl, :] = (o * r * og * (g * _sigmoid(g))).astype(BF16)

    def col(off):
        return pl.BlockSpec((tt, A_HD), lambda h, t: (t, off // A_HD + h))

    head_vec = lambda rows: pl.BlockSpec((rows, A_HD), lambda h, t: (0, h))
    return _pcall(
        body, grid=(A_HEADS, nT),
        in_specs=[col(OFF_QA), col(OFF_FA), col(OFF_IA), col(OFF_GA), head_vec(2), head_vec(1)],
        out_specs=[pl.BlockSpec((tt, A_HD), lambda h, t: (t, h)),
                   pl.BlockSpec((1, ncl, A_HD, A_HD), lambda h, t: (h, t, 0, 0))],
        out_shape=[jax.ShapeDtypeStruct((T, AW), BF16),
                   jax.ShapeDtypeStruct((A_HEADS, T // C, A_HD, A_HD), F32)],
        scratch_shapes=[pltpu.VMEM((A_HD, A_HD), F32)], name="hgrn_fwd", semantics=("parallel", "arbitrary"),
        args=[proj, proj, proj, proj, lb_logits, o_gain], job=job)


def _hgrn_bwd(proj, st, dy, lb_logits, o_gain, tt, job=None):
    T = proj.shape[0]
    nT, ncl = T // tt, tt // CHUNK
    C = CHUNK

    def body(q_ref, f_ref, i_ref, g_ref, st_ref, dy_ref, lbl_ref, og_ref,
             dq_ref, df_ref, di_ref, dg_ref, plb_ref, pog_ref, dS):
        @pl.when(pl.program_id(1) == 0)
        def _():
            dS[...] = jnp.zeros_like(dS)

        lbl = lbl_ref[...]
        lb = _sigmoid(lbl[0:1, :] - lbl[1:2, :])
        og = og_ref[...]
        tri = _causal(C)
        rowi = lax.broadcasted_iota(jnp.int32, (C, A_HD), 0)
        acc_lb = jnp.zeros((8, A_HD), F32)
        acc_og = jnp.zeros((8, A_HD), F32)
        for ci in reversed(range(ncl)):
            sl = pl.ds(ci * C, C)
            q, fl, v, g = q_ref[sl, :], f_ref[sl, :], i_ref[sl, :], g_ref[sl, :]
            dout = dy_ref[sl, :]
            ST0 = st_ref[0, ci]
            dST1 = dS[...]
            sig, f, lf, kk, sq, qf = _hgrn_chunk_terms(q, fl, lb)
            b = _tri_mm(tri, lf)
            bm, bl = b[C // 2 - 1:C // 2, :], b[C - 1:C, :]
            e_qd, e_kd, e_ke, e_b = jnp.exp(b - bm), jnp.exp(bm - b), jnp.exp(bl - b), jnp.exp(b)
            qd, kd, ke, qe = qf * e_qd, kk * e_kd, kk * e_ke, qf * e_b
            dec = jnp.exp(bl)
            A = jnp.where(tri, _dot(qd, kd, NT), 0.0)
            o = _dot(A, v) + _dot(qe, ST0, NT)
            r = lax.rsqrt(jnp.mean(o * o, axis=-1, keepdims=True) + EPS)
            sg = _sigmoid(g)
            silu_g = g * sg
            on = o * r * og
            dg_ref[sl, :] = (dout * on * (sg * (1.0 + g * (1.0 - sg)))).astype(BF16)
            don = dout * silu_g
            acc_og = acc_og + _fold8(don * o * r)
            dyh = don * og
            do = r * (dyh - o * (r * r) * jnp.mean(dyh * o, axis=-1, keepdims=True))
            dqe = _dot(do, ST0)
            dST0 = _dot(do, qe, TN) + dST1 * dec
            ddec = jnp.sum(dST1 * ST0, axis=0, keepdims=True)
            dke = _dot(v, dST1)
            dv = _dot(ke, dST1, NT)
            dA = jnp.where(tri, _dot(do, v, NT), 0.0)
            dv = dv + _dot(A, do, TN)
            dqd = _dot(dA, kd)
            dkd = _dot(dA, qd, TN)
            di_ref[sl, :] = dv.astype(BF16)
            dqf = dqe * e_b + dqd * e_qd
            dkk = dkd * e_kd + dke * e_ke
            t_qd, t_kd, t_ke = dqd * qd, dkd * kd, dke * ke
            db = dqe * qe + t_qd - t_kd - t_ke
            dbm = jnp.sum(t_kd - t_qd, axis=0, keepdims=True)
            dbl = jnp.sum(t_ke, axis=0, keepdims=True) + ddec * dec
            db = db + jnp.where(rowi == C // 2 - 1, dbm, 0.0) + jnp.where(rowi == C - 1, dbl, 0.0)
            dlf = _tri_mm(tri, db, TN)
            dfv = dlf / f - dkk
            df_ref[sl, :] = (dfv * (1.0 - lb) * sig * (1.0 - sig)).astype(BF16)
            acc_lb = acc_lb + _fold8(dfv * (1.0 - sig))
            dq_ref[sl, :] = (dqf * (sq * (1.0 + q * (1.0 - sq)))).astype(BF16)
            dS[...] = dST0
        plb_ref[...] = acc_lb
        pog_ref[...] = acc_og

    def col(off):
        return pl.BlockSpec((tt, A_HD), lambda h, t: (nT - 1 - t, off // A_HD + h))

    head_vec = lambda rows: pl.BlockSpec((rows, A_HD), lambda h, t: (0, h))
    o_spec = pl.BlockSpec((tt, A_HD), lambda h, t: (nT - 1 - t, h))
    p_spec = pl.BlockSpec((8, A_HD), lambda h, t: (t, h))
    o_shape = jax.ShapeDtypeStruct((T, AW), BF16)
    p_shape = jax.ShapeDtypeStruct((nT * 8, AW), F32)
    return _pcall(
        body, grid=(A_HEADS, nT),
        in_specs=[col(OFF_QA), col(OFF_FA), col(OFF_IA), col(OFF_GA),
                  pl.BlockSpec((1, ncl, A_HD, A_HD), lambda h, t: (h, nT - 1 - t, 0, 0)),
                  pl.BlockSpec((tt, A_HD), lambda h, t: (nT - 1 - t, h)), head_vec(2), head_vec(1)],
        out_specs=[o_spec, o_spec, o_spec, o_spec, p_spec, p_spec],
        out_shape=[o_shape, o_shape, o_shape, o_shape, p_shape, p_shape],
        scratch_shapes=[pltpu.VMEM((A_HD, A_HD), F32)], name="hgrn_bwd", semantics=("parallel", "arbitrary"),
        args=[proj, proj, proj, proj, st, dy, lb_logits, o_gain], job=job)


def _head_rms(t):
    r = lax.rsqrt(jnp.mean(t * t, axis=-1, keepdims=True) + EPS)
    return t * r, r


def _swa_mask(first_block):
    qi = lax.broadcasted_iota(jnp.int32, (BLK, 2 * BLK), 0) + BLK
    ki = lax.broadcasted_iota(jnp.int32, (BLK, 2 * BLK), 1)
    rel = qi - ki
    return (rel >= 0) & (rel < BLK) & (jnp.logical_not(first_block) | (ki >= BLK))


def _swa_probs(qn, kcat, sink, mask):
    s = jnp.where(mask, _dot(qn, kcat, NT) * (B_HD ** -0.5), NEG)
    m = jnp.maximum(jnp.max(s, axis=-1, keepdims=True), sink)
    p = jnp.exp(s - m)
    ps = jnp.exp(sink - m)
    inv = 1.0 / (jnp.sum(p, axis=-1, keepdims=True) + ps)
    return p * inv, ps * inv


def _swa_fwd(proj, q_gain, k_gain, sinks, job=None):
    T = proj.shape[0]
    nb = T // BLK

    def body(q_ref, kc_ref, kp_ref, vc_ref, vp_ref, qg_ref, kg_ref, sk_ref, o_ref):
        mask = _swa_mask(pl.program_id(0) == 0)
        qg, kg = qg_ref[...], kg_ref[...]
        for hk in range(KV_HEADS):
            ks = slice(hk * B_HD, (hk + 1) * B_HD)
            kcat = jnp.concatenate([_head_rms(kp_ref[:, ks])[0] * kg, _head_rms(kc_ref[:, ks])[0] * kg], axis=0)
            vcat = jnp.concatenate([vp_ref[:, ks], vc_ref[:, ks]], axis=0)
            for g in range(GROUP):
                h = hk * GROUP + g
                hs = slice(h * B_HD, (h + 1) * B_HD)
                qn = _head_rms(q_ref[:, hs])[0] * qg
                p, _ = _swa_probs(qn, kcat, sk_ref[0, h], mask)
                o_ref[:, hs] = _dot(p, vcat).astype(BF16)

    cur = lambda w, off: pl.BlockSpec((BLK, w), lambda i: (i, off // w))
    prev = lambda w, off: pl.BlockSpec((BLK, w), lambda i: (jnp.maximum(i - 1, 0), off // w))
    small = lambda n: pl.BlockSpec((1, n), lambda i: (0, 0))
    return _pcall(
        body, grid=(nb,),
        in_specs=[cur(BW, OFF_QB), cur(KVW, OFF_KB), prev(KVW, OFF_KB), cur(KVW, OFF_VB), prev(KVW, OFF_VB),
                  small(B_HD), small(B_HD), pl.BlockSpec(memory_space=pltpu.SMEM)],
        out_specs=[pl.BlockSpec((BLK, BW), lambda i: (i, 0))],
        out_shape=[jax.ShapeDtypeStruct((T, BW), BF16)], scratch_shapes=[], name="swa_fwd", semantics=("parallel",),
        args=[proj, proj, proj, proj, proj, q_gain, k_gain, sinks], job=job)


def _swa_bwd(proj, dout, q_gain, k_gain, sinks, job=None):
    T = proj.shape[0]
    nb = T // BLK
    W = BW + 2 * KVW

    def body(q_ref, kc_ref, kp_ref, vc_ref, vp_ref, do_ref, qg_ref, kg_ref, sk_ref,
             dq_ref, dkv_ref, pqg_ref, pkg_ref, psk_ref, dkn_c, dv_c):
        i = pl.program_id(0)
        live = i < nb
        qg, kg = qg_ref[...], kg_ref[...]
        mask = _swa_mask(i == 0)
        lane = lax.broadcasted_iota(jnp.int32, (1, 128), 1)

        @pl.when(i == 0)
        def _():
            dkn_c[...] = jnp.zeros_like(dkn_c)
            dv_c[...] = jnp.zeros_like(dv_c)

        acc_qg = jnp.zeros((8, B_HD), F32)
        acc_sk = jnp.zeros((1, 128), F32)
        acc_kg = jnp.zeros((8, B_HD), F32)
        for hk in range(KV_HEADS):
            ks = slice(hk * B_HD, (hk + 1) * B_HD)
            kp_hat, kp_r = _head_rms(kp_ref[:, ks])
            kc_hat, _ = _head_rms(kc_ref[:, ks])
            kcat = jnp.concatenate([kp_hat * kg, kc_hat * kg], axis=0)
            vcat = jnp.concatenate([vp_ref[:, ks], vc_ref[:, ks]], axis=0)
            dkn = jnp.zeros((2 * BLK, B_HD), F32)
            dvc = jnp.zeros((2 * BLK, B_HD), F32)
            for g in range(GROUP):
                h = hk * GROUP + g
                hs = slice(h * B_HD, (h + 1) * B_HD)
                q_hat, q_r = _head_rms(q_ref[:, hs])
                qn = q_hat * qg
                p, ps = _swa_probs(qn, kcat, sk_ref[0, h], mask)
                do = do_ref[:, hs]
                dp = _dot(do, vcat, NT)
                delta = jnp.sum(p * dp, axis=-1, keepdims=True)
                ds = p * (dp - delta) * (B_HD ** -0.5)
                dvc = dvc + _dot(p, do, TN)
                dkn = dkn + _dot(ds, qn, TN)
                dqn = _dot(ds, kcat)
                acc_qg = acc_qg + _fold8(dqn * q_hat)
                dqh = dqn * qg
                dq = (q_r * (dqh - q_hat * jnp.mean(dqh * q_hat, axis=-1, keepdims=True))).astype(BF16)

                @pl.when(live)
                def _(dq=dq, hs=hs):
                    dq_ref[:, hs] = dq

                acc_sk = acc_sk + jnp.where(lane == h, -jnp.sum(ps * delta, axis=0, keepdims=True), 0.0)
            dkn = jnp.where(live, dkn, 0.0)
            dvc = jnp.where(live, dvc, 0.0)
            dkn_prev = dkn_c[:, ks] + dkn[:BLK]
            dv_prev = dv_c[:, ks] + dvc[:BLK]
            acc_kg = acc_kg + _fold8(dkn_prev * kp_hat)
            dkh = dkn_prev * kg
            dkv_ref[:, ks] = (kp_r * (dkh - kp_hat * jnp.mean(dkh * kp_hat, axis=-1, keepdims=True))).astype(BF16)
            dkv_ref[:, KVW + hk * B_HD:KVW + (hk + 1) * B_HD] = dv_prev.astype(BF16)
            dkn_c[:, ks] = dkn[BLK:]
            dv_c[:, ks] = dvc[BLK:]
        keep = jnp.where(i > 0, 1.0, 0.0)
        pqg_ref[...] = jnp.zeros_like(pqg_ref)
        pkg_ref[...] = jnp.zeros_like(pkg_ref)
        pqg_ref[:, :B_HD] = jnp.where(live, acc_qg, 0.0)
        pkg_ref[:, :B_HD] = acc_kg * keep
        psk_ref[...] = jnp.broadcast_to(jnp.where(live, acc_sk, 0.0), (8, 128)) * (
            lax.broadcasted_iota(jnp.int32, (8, 128), 0) == 0).astype(F32)

    last = nb - 1
    cur = lambda w, off: pl.BlockSpec((BLK, w), lambda i: (jnp.minimum(i, last), off // w))
    prev = lambda w, off: pl.BlockSpec((BLK, w), lambda i: (jnp.maximum(i - 1, 0), off // w))
    small = lambda n: pl.BlockSpec((1, n), lambda i: (0, 0))
    part = pl.BlockSpec((8, 128), lambda i: (i, 0))
    p_shape = jax.ShapeDtypeStruct(((nb + 1) * 8, 128), F32)
    return _pcall(
        body, grid=(nb + 1,),
        in_specs=[cur(BW, OFF_QB), cur(KVW, OFF_KB), prev(KVW, OFF_KB), cur(KVW, OFF_VB), prev(KVW, OFF_VB),
                  pl.BlockSpec((BLK, BW), lambda i: (jnp.minimum(i, last), 0)), small(B_HD), small(B_HD),
                  pl.BlockSpec(memory_space=pltpu.SMEM)],
        out_specs=[pl.BlockSpec((BLK, BW), lambda i: (jnp.minimum(i, last), 0)),
                   pl.BlockSpec((BLK, 2 * KVW), lambda i: (jnp.maximum(i - 1, 0), 0)), part, part, part],
        out_shape=[jax.ShapeDtypeStruct((T, BW), BF16), jax.ShapeDtypeStruct((T, 2 * KVW), BF16), p_shape, p_shape, p_shape],
        scratch_shapes=[pltpu.VMEM((BLK, KVW), F32), pltpu.VMEM((BLK, KVW), F32)], name="swa_bwd",
        semantics=("arbitrary",), args=[proj, proj, proj, proj, proj, dout, q_gain, k_gain, sinks], job=job)


def _branch_merge(ya_pre, attn, wa_t, wb_t, proj, tm, tn):
    T = ya_pre.shape[0]

    def body(a_ref, b_ref, wa_ref, wb_ref, ga_ref, gb_ref, ya_ref, yb_ref, mg_ref):
        ya = lax.dot_general(a_ref[...], wa_ref[...], NT, preferred_element_type=F32)
        yb = lax.dot_general(b_ref[...], wb_ref[...], NT, preferred_element_type=F32)
        ya_ref[...] = ya.astype(BF16)
        yb_ref[...] = yb.astype(BF16)
        mg_ref[...] = (_sigmoid(ga_ref[...]) * ya + _sigmoid(gb_ref[...]) * yb).astype(BF16)

    o_spec = pl.BlockSpec((tm, tn), lambda i, j: (i, j))
    o_shape = jax.ShapeDtypeStruct((T, D), BF16)
    return pl.pallas_call(
        body, grid=(T // tm, D // tn),
        in_specs=[pl.BlockSpec((tm, AW), lambda i, j: (i, 0)), pl.BlockSpec((tm, BW), lambda i, j: (i, 0)),
                  pl.BlockSpec((tn, AW), lambda i, j: (j, 0)), pl.BlockSpec((tn, BW), lambda i, j: (j, 0)),
                  pl.BlockSpec((tm, tn), lambda i, j: (i, OFF_GTA // tn + j)),
                  pl.BlockSpec((tm, tn), lambda i, j: (i, OFF_GTB // tn + j))],
        out_specs=[o_spec, o_spec, o_spec], out_shape=[o_shape, o_shape, o_shape], name="branch_merge",
        compiler_params=_params(("parallel", "parallel")),
    )(ya_pre, attn, wa_t, wb_t, proj, proj)


def _ij(i, j, k):
    return (i, j)


def _local_step(x, tgt, mod, g1, g2, lbl, og, qg, kg, sk, shards, c_arr):
    win_s, wa_s, wb_s, wout_s, wmi_s, wmo_s = shards
    T = x.shape[0]
    tm, tr, tt = min(1024, T), min(256, T), min(512, T)
    tk_t = min(1024, T)
    tn = 512
    sh1, sc1, gt1, sh2, sc2, gt2 = (mod[:, i * D:(i + 1) * D] for i in range(N_MOD))
    nI = T // tm
    blk = (tm, tn)
    part = lambda: ((nI * 8, D), F32, (8, tn), _ij)
    vec_j = ((1, tn), lambda i, j, k: (0, j))

    h = _rms_mod_fwd("rms1_fwd", x, g1, sc1, sh1, tr)

    def epi_store(acc, ex, ou):
        ou[0][...] = acc.astype(ou[0].dtype)

    (win_t,) = _run_job("gather_w_in", _gather_job([win_s]))
    (proj,), (wa_t, wb_t, w_out, wmi_t) = _mm(
        "in_proj", "nt", [(h, D)], win_t, T, IN_W, D, tm, tn, D, [], [((T, IN_W), F32, blk, _ij)], epi_store,
        job=_gather_job([wa_s, wb_s, wout_s, wmi_s]))
    (ya_pre, st), _ = _hgrn_fwd(proj, lbl, og, tt)
    (attn,), (w_mo,) = _swa_fwd(proj, qg, kg, sk, job=_gather_job([wmo_s]))
    ya, yb, merged = _branch_merge(ya_pre, attn, wa_t, wb_t, proj, tm, tn)

    def epi_res1(acc, ex, ou):
        x_ref, gt_ref = ex
        ou[0][...] = acc.astype(BF16)
        ou[1][...] = x_ref[...] + gt_ref[...] * acc

    mo, x1 = _mm("out_proj", "nn", [(merged, D)], w_out, T, D, D, tm, tn, D,
                 [(x, blk, _ij), (gt1, *vec_j)], [((T, D), BF16, blk, _ij), ((T, D), F32, blk, _ij)], epi_res1)
    h2 = _rms_mod_fwd("rms2_fwd", x1, g2, sc2, sh2, tr)

    def epi_relu2(acc, ex, ou):
        r = jnp.maximum(acc, 0.0)
        ou[0][...] = r.astype(BF16)
        ou[1][...] = (r * r).astype(BF16)

    r, a = _mm("mlp_in", "nt", [(h2, D)], wmi_t, T, HID, D, tm, tn, D, [],
               [((T, HID), BF16, blk, _ij), ((T, HID), BF16, blk, _ij)], epi_relu2)

    def epi_loss(acc, ex, ou):
        x1_ref, t_ref, gt_ref = ex
        e = x1_ref[...] + gt_ref[...] * acc - t_ref[...]
        dy = e * (1.0 / D)
        ou[0][...] = dy
        ou[1][...] = (gt_ref[...] * dy).astype(BF16)
        ou[2][...] = _fold8(e * e) * (0.5 / D)
        ou[3][...] = _fold8(dy * acc)

    dy, dz, p_loss, p_gt2 = _mm(
        "mlp_out", "nn", [(a, HID)], w_mo, T, D, HID, tm, tn, 1024, [(x1, blk, _ij), (tgt, blk, _ij), (gt2, *vec_j)],
        [((T, D), F32, blk, _ij), ((T, D), BF16, blk, _ij), part(), part()], epi_loss)

    def epi_du(acc, ex, ou):
        ou[0][...] = (acc * (2.0 * ex[0][...].astype(F32))).astype(BF16)

    (du,) = _mm("mlp_out_dx", "nt", [(dz, D)], w_mo, T, HID, D, tm, tn, D, [(r, blk, _ij)],
                [((T, HID), BF16, blk, _ij)], epi_du)
    gblk = (1024, 1024)
    sum_tr = lambda rs: 256 if rs % 256 == 0 else 64
    pair_sum = lambda nm, g, r1: _pair_sum("pair_sum_" + nm, g, r1, c_arr, sum_tr(r1.shape[1]))
    (g_mo,) = _mm("mlp_out_dw", "tn", [(a, HID)], dz, HID, D, T, 1024, 1024, tk_t, [], [((HID, D), BF16, gblk, _ij)], epi_store)
    (dh2,), (r1_mo,) = _mm("mlp_in_dx", "nn", [(du, HID)], wmi_t, T, D, HID, tm, 1024, 1024, [],
                           [((T, D), F32, (tm, 1024), _ij)], epi_store, job=_pair_job([g_mo]))
    s_mo = pair_sum("mlp_out", g_mo, r1_mo)
    (g_mi,), (r2_mo,) = _mm("mlp_in_dw", "tn", [(du, HID)], h2, HID, D, T, 1024, 1024, tk_t, [],
                            [((HID, D), BF16, gblk, _ij)], epi_store, job=_chip_job([s_mo]))
    dx1, p_sh2, p_sc2, p_g2, dmo, p_gt1 = _rms_mod_bwd("rms2_bwd", dh2, x1, g2, sc2, dy, tr, gate=gt1, mo=mo)

    def epi_gates(acc, ex, ou):
        ya_ref, yb_ref, ga_ref, gb_ref = ex
        sa, sb = _sigmoid(ga_ref[...]), _sigmoid(gb_ref[...])
        ou[0][...] = (acc * sa).astype(BF16)
        ou[1][...] = (acc * sb).astype(BF16)
        ou[2][...] = (acc * ya_ref[...].astype(F32) * (sa * (1.0 - sa))).astype(BF16)
        ou[3][...] = (acc * yb_ref[...].astype(F32) * (sb * (1.0 - sb))).astype(BF16)

    o_bf = ((T, D), BF16, blk, _ij)
    (dya, dyb, dga, dgb), (r1_mi,) = _mm(
        "out_proj_dx", "nt", [(dmo, D)], w_out, T, D, D, tm, tn, D,
        [(ya, blk, _ij), (yb, blk, _ij), (proj, blk, lambda i, j, k: (i, OFF_GTA // tn + j)),
         (proj, blk, lambda i, j, k: (i, OFF_GTB // tn + j))], [o_bf, o_bf, o_bf, o_bf], epi_gates,
        job=_pair_job([g_mi]))
    s_mi = pair_sum("mlp_in", g_mi, r1_mi)
    (g_out,) = _mm("out_proj_dw", "tn", [(merged, D)], dmo, D, D, T, 1024, 1024, tk_t, [], [((D, D), BF16, gblk, _ij)], epi_store)
    (dya_pre,) = _mm("branch_a_dx", "nn", [(dya, D)], wa_t, T, AW, D, tm, tn, D, [], [((T, AW), F32, blk, _ij)], epi_store)
    (dattn,) = _mm("branch_b_dx", "nn", [(dyb, D)], wb_t, T, BW, D, tm, tn, D, [], [((T, BW), F32, blk, _ij)], epi_store)
    (g_a,) = _mm("branch_a_dw", "tn", [(dya, D)], ya_pre, D, AW, T, 1024, 1024, tk_t, [], [((D, AW), BF16, gblk, _ij)], epi_store)
    (g_b,) = _mm("branch_b_dw", "tn", [(dyb, D)], attn, D, BW, T, 1024, 1024, tk_t, [], [((D, BW), BF16, gblk, _ij)], epi_store)
    (dqa, dfa, dia, dgg, p_lb, p_og), (r2_mi,) = _hgrn_bwd(proj, st, dya_pre, lbl, og, tt, job=_chip_job([s_mi]))
    (dqb, dkv, p_qg, p_kg, p_sk), (r1_out, r1_a, r1_b) = _swa_bwd(proj, dattn, qg, kg, sk, job=_pair_job([g_out, g_a, g_b]))
    s_out, s_a, s_b = pair_sum("out", g_out, r1_out), pair_sum("branch_a", g_a, r1_a), pair_sum("branch_b", g_b, r1_b)
    pieces = [(dqa, AW), (dfa, AW), (dia, AW), (dgg, AW), (dqb, BW), (dkv, 2 * KVW), (dga, D), (dgb, D)]
    (g_in,), (r2_out, r2_a, r2_b) = _mm(
        "in_proj_dw", "tn", pieces, h, IN_W, D, T, 512, 1024, tk_t, [], [((IN_W, D), BF16, (512, 1024), _ij)], epi_store,
        job=_chip_job([s_out, s_a, s_b]))
    (r1_in,) = _run_job("pair_w_in", _pair_job([g_in]))
    s_in = pair_sum("in", g_in, r1_in)
    (dh,), (r2_in,) = _mm("in_proj_dx", "nn", pieces, win_t, T, D, IN_W, tm, 1024, 512, [],
                          [((T, D), F32, (tm, 1024), _ij)], epi_store, job=_chip_job([s_in]))
    dx, p_sh1, p_sc1, p_g1 = _rms_mod_bwd("rms1_bwd", dh, x, g1, sc1, dx1, tr)

    partials = dict(sh1=p_sh1, sc1=p_sc1, gt1=p_gt1, sh2=p_sh2, sc2=p_sc2, gt2=p_gt2, g1=p_g1, g2=p_g2,
                    lb=p_lb, og=p_og, qg=p_qg, kg=p_kg, sk=p_sk, loss=p_loss)
    sums = dict(w_in=(s_in, r2_in), w_branch_a=(s_a, r2_a), w_branch_b=(s_b, r2_b), w_out=(s_out, r2_out),
                w_mlp_in=(s_mi, r2_mi), w_mlp_out=(s_mo, r2_mo))
    return dx, sums, partials


def _exchange_slots(buf, send_sems, recv_sems):
    me = _mesh_pos()
    mine = buf.at[_index(me)]
    sends = []
    for k in range(1, N_DEV):
        cp = pltpu.make_async_remote_copy(src_ref=mine, dst_ref=mine, send_sem=send_sems.at[k - 1],
                                          recv_sem=recv_sems.at[k - 1], device_id=_flip(me, k), device_id_type=MESH)
        cp.start()
        sends.append(cp)
    for k in range(1, N_DEV):
        theirs = buf.at[_index(_flip(me, k))]
        pltpu.make_async_remote_copy(src_ref=theirs, dst_ref=theirs, send_sem=send_sems.at[k - 1],
                                     recv_sem=recv_sems.at[k - 1], device_id=_flip(me, k), device_id_type=MESH).wait_recv()
    for cp in sends:
        cp.wait_send()


ADA_W = N_MOD * D // N_DEV


def _ada_mod(c, w_ada, b_shard):
    def body(c_ref, w_ref, b_ref, mod_ref, sc_ref, cbuf, mbuf, s1, r1, s2, r2):
        me = _index(_mesh_pos())
        cbuf[me] = c_ref[...]
        _exchange_slots(cbuf, s1, r1)
        row = lax.broadcasted_iota(jnp.int32, (N_DEV, D), 0)
        call = jnp.zeros((N_DEV, D), F32)
        for d in range(N_DEV):
            call = jnp.where(row == d, cbuf[d], call)
        sc = call * _sigmoid(call)
        sc_ref[...] = sc
        mbuf[me] = _dot(sc, w_ref[...]) + b_ref[...]
        _exchange_slots(mbuf, s2, r2)
        for s in range(N_DEV):
            mod_ref[:, s * ADA_W:(s + 1) * ADA_W] = mbuf[s, pl.ds(me, 1), :]

    return pl.pallas_call(
        body, in_specs=[_VMEM, _VMEM, _VMEM], out_specs=[_VMEM, _VMEM],
        out_shape=[jax.ShapeDtypeStruct((1, N_MOD * D), F32), jax.ShapeDtypeStruct((N_DEV, D), F32)],
        scratch_shapes=[pltpu.VMEM((N_DEV, 1, D), F32), pltpu.VMEM((N_DEV, N_DEV, ADA_W), F32),
                        _SEMS(N_DEV - 1), _SEMS(N_DEV - 1), _SEMS(N_DEV - 1), _SEMS(N_DEV - 1)],
        name="ada_mod", compiler_params=pltpu.CompilerParams(vmem_limit_bytes=VMEM_LIMIT),
    )(c, w_ada, b_shard)


SMALL_SEGS = (("b_ada", N_MOD * D), ("norm1_gain", D), ("norm2_gain", D), ("lb0", AW), ("lb1", AW),
              ("hgrn_o_gain", AW), ("q_norm_gain", 128), ("k_norm_gain", 128), ("sinks", 128))
SMALL_W = sum(w for _, w in SMALL_SEGS)
X_SEGS = (("sh1", D), ("sc1", D), ("gt1", D), ("sh2", D), ("sc2", D), ("gt2", D), ("g1", D), ("g2", D),
          ("lb", AW), ("og", AW), ("qg", 128), ("kg", 128), ("sk", 128), ("loss", 128))
X_W = sum(w for _, w in X_SEGS)


def _offsets(segs):
    out, o = {}, 0
    for name, w in segs:
        out[name] = (o, w)
        o += w
    return out


def _small_reduce(parts, lb_logits):
    xo, so = _offsets(X_SEGS), _offsets(SMALL_SEGS)
    names = [nm for nm, _ in X_SEGS]

    def body(*refs):
        p_refs = dict(zip(names, refs[:len(names)]))
        lbl_ref, allx, gs_ref, loss_ref, send_sems, recv_sems = refs[len(names):]
        me = _index(_mesh_pos())
        for nm, (o, w) in xo.items():
            if nm == "loss":
                allx[me, :, o:o + w] = jnp.broadcast_to(jnp.sum(p_refs[nm][...]), (1, w))
            else:
                allx[me, :, o:o + w] = jnp.sum(p_refs[nm][...], axis=0, keepdims=True)
        _exchange_slots(allx, send_sems, recv_sems)
        tot = allx[0]
        for d in range(1, N_DEV):
            tot = tot + allx[d]
        seg = lambda nm: tot[:, xo[nm][0]:xo[nm][0] + xo[nm][1]]

        def put(nm, v):
            gs_ref[:, so[nm][0]:so[nm][0] + so[nm][1]] = v

        put("b_ada", tot[:, 0:N_MOD * D])
        put("norm1_gain", seg("g1"))
        put("norm2_gain", seg("g2"))
        lbl = lbl_ref[...]
        lb = _sigmoid(lbl[0:1, :] - lbl[1:2, :])
        dl0 = seg("lb") * lb * (1.0 - lb)
        put("lb0", dl0)
        put("lb1", -dl0)
        put("hgrn_o_gain", seg("og"))
        put("q_norm_gain", seg("qg"))
        put("k_norm_gain", seg("kg"))
        put("sinks", seg("sk"))
        loss_ref[...] = seg("loss")

    return pl.pallas_call(
        body, in_specs=[_VMEM] * (len(names) + 1), out_specs=[_VMEM, _VMEM, _VMEM],
        out_shape=[jax.ShapeDtypeStruct((N_DEV, 1, X_W), F32), jax.ShapeDtypeStruct((1, SMALL_W), F32),
                   jax.ShapeDtypeStruct((1, 128), F32)],
        scratch_shapes=[_SEMS(N_DEV - 1), _SEMS(N_DEV - 1)], name="small_reduce",
        compiler_params=pltpu.CompilerParams(vmem_limit_bytes=VMEM_LIMIT),
    )(*[parts[nm] for nm in names], lb_logits)


def _adamw_math(w, g, m, v):
    m = B1 * m + (1.0 - B1) * g
    v = B2 * v + (1.0 - B2) * (g * g)
    m_hat = m / (1.0 - B1 ** STEP)
    v_hat = v / (1.0 - B2 ** STEP)
    return -LR * (m_hat / (jnp.sqrt(v_hat) + ADAM_EPS) + WD * w), m, v


def _pair_sum(name, g, recv, c_arr, tr):
    _, rs, cols = recv.shape
    blk = (1, tr, cols)

    def body(c_ref, g_ref, r_ref, o_ref):
        o_ref[...] = (g_ref[...].astype(F32) + r_ref[...].astype(F32)).astype(BF16)

    grid_spec = pltpu.PrefetchScalarGridSpec(
        num_scalar_prefetch=1, grid=(4, rs // tr),
        in_specs=[pl.BlockSpec(blk, lambda q, i, c: (2 * q + c[0], i, 0)), pl.BlockSpec(blk, lambda q, i, c: (q, i, 0))],
        out_specs=pl.BlockSpec(blk, lambda q, i, c: (q, i, 0)))
    return pl.pallas_call(body, grid_spec=grid_spec, out_shape=jax.ShapeDtypeStruct((4, rs, cols), BF16), name=name,
                          compiler_params=_params(("parallel", "parallel")))(c_arr, g.reshape(N_DEV, rs, cols), recv)


def _final_sum(name, sums, recv, q_arr, tr):
    _, rs, cols = sums.shape

    def body(q_ref, s_ref, r_ref, o_ref):
        o_ref[...] = ((s_ref[0].astype(F32) + r_ref[0].astype(F32)) + r_ref[1].astype(F32)) + r_ref[2].astype(F32)

    grid_spec = pltpu.PrefetchScalarGridSpec(
        num_scalar_prefetch=1, grid=(rs // tr,),
        in_specs=[pl.BlockSpec((1, tr, cols), lambda i, q: (q[0], i, 0)), pl.BlockSpec((3, tr, cols), lambda i, q: (0, i, 0))],
        out_specs=pl.BlockSpec((tr, cols), lambda i, q: (i, 0)))
    return pl.pallas_call(body, grid_spec=grid_spec, out_shape=jax.ShapeDtypeStruct((rs, cols), F32), name=name,
                          compiler_params=_params(("parallel",)))(q_arr, sums, recv)


def _adamw(name, w, g, m, v, tr):
    rows, cols = w.shape

    def body(w_ref, g_ref, m_ref, v_ref, d_ref, nm_ref, nv_ref):
        d_ref[...], nm_ref[...], nv_ref[...] = _adamw_math(w_ref[...], g_ref[...], m_ref[...], v_ref[...])

    spec = pl.BlockSpec((tr, cols), lambda i: (i, 0))
    shape = jax.ShapeDtypeStruct((rows, cols), F32)
    return pl.pallas_call(
        body, grid=(rows // tr,), in_specs=[spec] * 4, out_specs=[spec] * 3, out_shape=[shape] * 3, name=name,
        compiler_params=_params(("parallel",)),
    )(w, g, m, v)


def _ada_update(sc_t, dmod_cols, w, m, v, tr):
    rows, cols = w.shape

    def body(s_ref, d_ref, w_ref, m_ref, v_ref, g_ref, dl_ref, nm_ref, nv_ref):
        g = jnp.dot(s_ref[...], d_ref[...], precision=lax.Precision.HIGHEST, preferred_element_type=F32)
        g_ref[...] = g
        dl_ref[...], nm_ref[...], nv_ref[...] = _adamw_math(w_ref[...], g, m_ref[...], v_ref[...])

    spec = pl.BlockSpec((tr, cols), lambda i: (i, 0))
    shape = jax.ShapeDtypeStruct((rows, cols), F32)
    return pl.pallas_call(
        body, grid=(rows // tr,),
        in_specs=[pl.BlockSpec((tr, N_DEV), lambda i: (i, 0)), pl.BlockSpec((N_DEV, cols), lambda i: (0, 0)), spec, spec, spec],
        out_specs=[spec] * 4, out_shape=[shape] * 4, name="ada_update", compiler_params=_params(("parallel",)),
    )(sc_t, dmod_cols, w, m, v)


BIG = ("w_in", "w_branch_a", "w_branch_b", "w_out", "w_mlp_in", "w_mlp_out")
COLUMN_SHARDED = ("w_in", "w_branch_a", "w_branch_b", "w_mlp_in")
WEIGHTS = ("w_ada", "b_ada", "norm1_gain", "w_in", "lb_logits", "hgrn_o_gain", "q_norm_gain", "k_norm_gain", "sinks",
           "w_branch_a", "w_branch_b", "w_out", "norm2_gain", "w_mlp_in", "w_mlp_out")


def _pack_small(p):
    lb = p["lb_logits"]
    src = dict(p, lb0=lb[0:1], lb1=lb[1:2])
    return jnp.concatenate([jnp.pad(src[nm], ((0, 0), (0, w - src[nm].shape[1]))) for nm, w in SMALL_SEGS], axis=1)


def _unpack_small(vec, shapes):
    so = _offsets(SMALL_SEGS)
    out = {}
    for nm, shp in shapes.items():
        if nm == "lb_logits":
            o = so["lb0"][0]
            out[nm] = vec[0, o:o + 2 * AW].reshape(2, AW)
        else:
            o = so[nm][0]
            out[nm] = vec[:, o:o + shp[1]]
    return out


def kernel(x, c, w_ada, b_ada, norm1_gain, w_in, lb_logits, hgrn_o_gain, q_norm_gain, k_norm_gain, sinks, w_branch_a, w_branch_b, w_out, norm2_gain, w_mlp_in, w_mlp_out, loss_target, m_w_ada, m_b_ada, m_norm1_gain, m_w_in, m_lb_logits, m_hgrn_o_gain, m_q_norm_gain, m_k_norm_gain, m_sinks, m_w_branch_a, m_w_branch_b, m_w_out, m_norm2_gain, m_w_mlp_in, m_w_mlp_out, v_w_ada, v_b_ada, v_norm1_gain, v_w_in, v_lb_logits, v_hgrn_o_gain, v_q_norm_gain, v_k_norm_gain, v_sinks, v_w_branch_a, v_w_branch_b, v_w_out, v_norm2_gain, v_w_mlp_in, v_w_mlp_out):
    w = dict(w_ada=w_ada, b_ada=b_ada, norm1_gain=norm1_gain, w_in=w_in, lb_logits=lb_logits, hgrn_o_gain=hgrn_o_gain,
             q_norm_gain=q_norm_gain, k_norm_gain=k_norm_gain, sinks=sinks, w_branch_a=w_branch_a, w_branch_b=w_branch_b,
             w_out=w_out, norm2_gain=norm2_gain, w_mlp_in=w_mlp_in, w_mlp_out=w_mlp_out)
    m = dict(w_ada=m_w_ada, b_ada=m_b_ada, norm1_gain=m_norm1_gain, w_in=m_w_in, lb_logits=m_lb_logits,
             hgrn_o_gain=m_hgrn_o_gain, q_norm_gain=m_q_norm_gain, k_norm_gain=m_k_norm_gain, sinks=m_sinks,
             w_branch_a=m_w_branch_a, w_branch_b=m_w_branch_b, w_out=m_w_out, norm2_gain=m_norm2_gain,
             w_mlp_in=m_w_mlp_in, w_mlp_out=m_w_mlp_out)
    v = dict(w_ada=v_w_ada, b_ada=v_b_ada, norm1_gain=v_norm1_gain, w_in=v_w_in, lb_logits=v_lb_logits,
             hgrn_o_gain=v_hgrn_o_gain, q_norm_gain=v_q_norm_gain, k_norm_gain=v_k_norm_gain, sinks=v_sinks,
             w_branch_a=v_w_branch_a, w_branch_b=v_w_branch_b, w_out=v_w_out, norm2_gain=v_norm2_gain,
             w_mlp_in=v_w_mlp_in, w_mlp_out=v_w_mlp_out)
    for d in (w, m, v):
        for nm in ("w_ada",) + BIG:
            d[nm] = d[nm][0]
    px, py, pc = _mesh_pos()
    me = _index((px, py, pc))
    c_arr = jnp.reshape(pc, (1,)).astype(jnp.int32)
    q_arr = jnp.reshape(2 * px + py, (1,)).astype(jnp.int32)

    shards = [(w[nm].T if nm in COLUMN_SHARDED else w[nm]).astype(BF16) for nm in BIG]
    b_shard = lax.dynamic_slice(b_ada, (0, me * ADA_W), (1, ADA_W))
    mod, sc_all = _ada_mod(c, w["w_ada"], b_shard)

    dx, sums, parts = _local_step(x[0], loss_target[0], mod, norm1_gain, norm2_gain, lb_logits, hgrn_o_gain,
                                  q_norm_gain, k_norm_gain, sinks, shards, c_arr)

    allx, g_small, loss = _small_reduce(parts, lb_logits)

    grad, delta, new_m, new_v = {}, {}, {}, {}
    for nm in BIG:
        s, r2 = sums[nm]
        rs = s.shape[1]
        g = _final_sum("sum_" + nm, s, r2, q_arr, 64 if rs % 256 else 256)
        g = g.T if nm in COLUMN_SHARDED else g
        rows = g.shape[0]
        grad[nm] = g
        delta[nm], new_m[nm], new_v[nm] = _adamw("adamw_" + nm, w[nm], g, m[nm], v[nm], 128 if rows % 128 == 0 else rows)

    dmod_cols = lax.dynamic_slice(allx[:, 0, :], (0, me * ADA_W), (N_DEV, ADA_W))
    grad["w_ada"], delta["w_ada"], new_m["w_ada"], new_v["w_ada"] = _ada_update(
        sc_all.T, dmod_cols, w["w_ada"], m["w_ada"], v["w_ada"], 256)

    small_names = [nm for nm in WEIGHTS if nm not in BIG and nm != "w_ada"]
    shapes = {nm: w[nm].shape for nm in small_names}
    ds, ms, vs = _adamw("adamw_small", _pack_small(w), g_small, _pack_small(m), _pack_small(v), 1)
    for dst, vec in ((grad, g_small), (delta, ds), (new_m, ms), (new_v, vs)):
        dst.update(_unpack_small(vec, shapes))

    def full(d, nm):
        return d[nm][None] if nm in BIG or nm == "w_ada" else d[nm]

    return (loss[0, 0], dx[None], *[full(grad, nm) for nm in WEIGHTS], *[full(delta, nm) for nm in WEIGHTS],
            *[full(new_m, nm) for nm in WEIGHTS], *[full(new_v, nm) for nm in WEIGHTS])
```

```python
import functools

import jax
import jax.numpy as jnp
from jax import lax
from jax.experimental import pallas as pl
from jax.experimental.pallas import tpu as pltpu

F32 = jnp.float32
BF16 = jnp.bfloat16
MESH = pl.DeviceIdType.MESH

N_DEV = 8
D = 2048
A_HEADS, A_HD, CHUNK = 8, 128, 64
AW = A_HEADS * A_HD
Q_HEADS, KV_HEADS, GROUP, B_HD, BLK = 16, 4, 4, 64, 128
BW = Q_HEADS * B_HD
KVW = KV_HEADS * B_HD
HID = 4 * D
IN_W = 4 * AW + BW + 2 * KVW + 2 * D
OFF_QA, OFF_FA, OFF_IA, OFF_GA = 0, AW, 2 * AW, 3 * AW
OFF_QB = 4 * AW
OFF_KB = OFF_QB + BW
OFF_VB = OFF_KB + KVW
OFF_GTA = OFF_VB + KVW
OFF_GTB = OFF_GTA + D
N_MOD = 6
EPS = 1e-6
LR, B1, B2, ADAM_EPS, WD, STEP = 1e-3, 0.9, 0.999, 1e-8, 0.01, 10
NEG = -1e30

VMEM_LIMIT = 56 * 1024 * 1024

NN = (((1,), (0,)), ((), ()))
NT = (((1,), (1,)), ((), ()))
TN = (((0,), (0,)), ((), ()))
BNN = (((2,), (1,)), ((0,), (0,)))
BNT = (((2,), (2,)), ((0,), (0,)))
BTN = (((1,), (1,)), ((0,), (0,)))


def _dot(a, b, dims=NN):
    return lax.dot_general(a.astype(BF16), b.astype(BF16), dims, preferred_element_type=F32)


def _params(sem):
    return pltpu.CompilerParams(dimension_semantics=sem, vmem_limit_bytes=VMEM_LIMIT)


def _sigmoid(x):
    return 1.0 / (1.0 + jnp.exp(-x))


def _fold8(v):
    r, n = v.shape
    return jnp.sum(v.reshape(r // 8, 8, n), axis=0)


_VMEM = pl.BlockSpec(memory_space=pltpu.VMEM)
_ANY = pl.BlockSpec(memory_space=pl.ANY)
_SEMS = lambda n: pltpu.SemaphoreType.DMA((n,))


def _mesh_pos():
    return lax.axis_index("x"), lax.axis_index("y"), lax.axis_index("c")


def _flip(pos, k):
    return tuple(1 - p if (k >> s) & 1 else p for p, s in zip(pos, (2, 1, 0)))


def _index(pos):
    return 4 * pos[0] + 2 * pos[1] + pos[2]


class _Job:
    def __init__(self, ins, out_shape, sems, start, finish):
        self.ins, self.out_shape, self.sems, self.start, self.finish = list(ins), list(out_shape), list(sems), start, finish


def _pcall(body, *, grid, in_specs, out_specs, out_shape, scratch_shapes, name, semantics, args, job=None):
    if job is None:
        outs = pl.pallas_call(body, grid=grid, in_specs=in_specs, out_specs=out_specs, out_shape=out_shape,
                              scratch_shapes=scratch_shapes, name=name, compiler_params=_params(semantics))(*args)
        return list(outs), []
    n_in, n_out, n_scr = len(in_specs), len(out_specs), len(scratch_shapes)
    j_in, j_out = len(job.ins), len(job.out_shape)
    steps = tuple(grid)

    def carrier(*refs):
        o = 0
        main_in, o = refs[o:o + n_in], o + n_in
        job_in, o = refs[o:o + j_in], o + j_in
        main_out, o = refs[o:o + n_out], o + n_out
        job_out, o = refs[o:o + j_out], o + j_out
        main_scr, job_sems = refs[o:o + n_scr], refs[o + n_scr:]
        ids = [pl.program_id(a) for a in range(len(steps))]
        first = functools.reduce(lambda p, q: p & q, [i == 0 for i in ids])
        last = functools.reduce(lambda p, q: p & q, [i == s - 1 for i, s in zip(ids, steps)])

        @pl.when(first)
        def _():
            job.start(job_in, job_out, job_sems)

        body(*main_in, *main_out, *main_scr)

        @pl.when(last)
        def _():
            job.finish(job_in, job_out, job_sems)

    outs = pl.pallas_call(
        carrier, grid=grid, in_specs=list(in_specs) + [_ANY] * j_in, out_specs=list(out_specs) + [_ANY] * j_out,
        out_shape=list(out_shape) + job.out_shape, scratch_shapes=list(scratch_shapes) + job.sems, name=name,
        compiler_params=_params(("arbitrary",) * len(steps)),
    )(*args, *job.ins)
    return list(outs[:n_out]), list(outs[n_out:])


def _run_job(name, job):
    j_in, j_out = len(job.ins), len(job.out_shape)

    def body(*refs):
        ins, outs, sems = refs[:j_in], refs[j_in:j_in + j_out], refs[j_in + j_out:]
        job.start(ins, outs, sems)
        job.finish(ins, outs, sems)

    return list(pl.pallas_call(body, in_specs=[_ANY] * j_in, out_specs=[_ANY] * j_out, out_shape=job.out_shape,
                               scratch_shapes=job.sems, name=name)(*job.ins))


def _gather_job(shards):
    n = len(shards)

    def copies(ins, outs, sems):
        send_sems, recv_sems, local_sems = sems
        x, y, c = _mesh_pos()
        me, sib = (x, y, c), (x, y, 1 - c)
        chips = [(1 - x, y), (x, 1 - y), (1 - x, 1 - y)]

        def rows(a, p):
            rs = shards[a].shape[0]
            return outs[a].at[pl.ds(_index(p) * rs, rs), :]

        def copy(a, k, block, to, src=None):
            return pltpu.make_async_remote_copy(
                src_ref=rows(a, block) if src is None else src, dst_ref=rows(a, block),
                send_sem=send_sems.at[7 * a + k], recv_sem=recv_sems.at[7 * a + k], device_id=to, device_id_type=MESH)

        mine = [pltpu.make_async_copy(ins[a], rows(a, me), local_sems.at[a]) for a in range(n)]
        first = []
        for a in range(n):
            first.append(copy(a, 0, me, sib, src=ins[a]))
            first += [copy(a, 1 + j, me, (*chip, c), src=ins[a]) for j, chip in enumerate(chips)]
        return me, sib, c, chips, copy, mine, first

    def start(ins, outs, sems):
        *_, mine, first = copies(ins, outs, sems)
        for cp in mine + first:
            cp.start()

    def finish(ins, outs, sems):
        me, sib, c, chips, copy, mine, first = copies(ins, outs, sems)
        passed = []
        for j, chip in enumerate(chips):
            for a in range(n):
                copy(a, 1 + j, (*chip, c), me).wait_recv()
                cp = copy(a, 4 + j, (*chip, c), sib)
                cp.start()
                passed.append(cp)
        for a in range(n):
            copy(a, 0, sib, me).wait_recv()
            for j, chip in enumerate(chips):
                copy(a, 4 + j, (*chip, 1 - c), me).wait_recv()
        for cp in first + passed:
            cp.wait_send()
        for cp in mine:
            cp.wait()

    return _Job(shards, [jax.ShapeDtypeStruct((N_DEV * s.shape[0], s.shape[1]), s.dtype) for s in shards],
                [_SEMS(7 * n), _SEMS(7 * n), _SEMS(n)], start, finish)


def _pair_job(grads):
    n = len(grads)

    def copies(ins, outs, sems):
        send_sems, recv_sems = sems
        x, y, c = _mesh_pos()
        out = []
        for a in range(n):
            rs = grads[a].shape[0] // N_DEV
            for q in range(4):
                blk = ins[a].at[pl.ds((2 * q + 1 - c) * rs, rs), :]
                out.append(pltpu.make_async_remote_copy(
                    src_ref=blk, dst_ref=outs[a].at[q], send_sem=send_sems.at[4 * a + q], recv_sem=recv_sems.at[4 * a + q],
                    device_id=(x, y, 1 - c), device_id_type=MESH))
        return out

    def start(ins, outs, sems):
        for cp in copies(ins, outs, sems):
            cp.start()

    def finish(ins, outs, sems):
        for cp in copies(ins, outs, sems):
            cp.wait()

    return _Job(grads, [jax.ShapeDtypeStruct((4, g.shape[0] // N_DEV, g.shape[1]), g.dtype) for g in grads],
                [_SEMS(4 * n), _SEMS(4 * n)], start, finish)


def _chip_job(sums):
    n = len(sums)

    def copies(ins, outs, sems):
        send_sems, recv_sems = sems
        x, y, c = _mesh_pos()
        out = []
        for a in range(n):
            for r in (1, 2, 3):
                px, py = (1 - x if r & 2 else x), (1 - y if r & 1 else y)
                out.append(pltpu.make_async_remote_copy(
                    src_ref=ins[a].at[2 * px + py], dst_ref=outs[a].at[r - 1], send_sem=send_sems.at[3 * a + r - 1],
                    recv_sem=recv_sems.at[3 * a + r - 1], device_id=(px, py, c), device_id_type=MESH))
        return out

    def start(ins, outs, sems):
        for cp in copies(ins, outs, sems):
            cp.start()

    def finish(ins, outs, sems):
        for cp in copies(ins, outs, sems):
            cp.wait()

    return _Job(sums, [jax.ShapeDtypeStruct((3,) + s.shape[1:], s.dtype) for s in sums],
                [_SEMS(3 * n), _SEMS(3 * n)], start, finish)


def _mm(name, form, a_list, b, M, N, K, tm, tn, tk, extras, outs, epi, job=None):
    nI, nJ, nK = M // tm, N // tn, K // tk
    assert nI * tm == M and nJ * tn == N and nK * tk == K
    dims = {"nn": NN, "nt": NT, "tn": TN}[form]
    split = tm if form == "tn" else tk
    starts, s = [], 0
    for _, w in a_list:
        assert w % split == 0
        starts.append(s // split)
        s += w
    counts = [w // split for _, w in a_list]
    assert s == (M if form == "tn" else K)
    nP = len(a_list)

    def a_spec(p):
        st, cn = starts[p], counts[p]
        if form == "tn":
            return pl.BlockSpec((tk, tm), lambda i, j, k: (jnp.where((i >= st) & (i < st + cn), k, 0),
                                                           jnp.clip(i - st, 0, cn - 1)))
        return pl.BlockSpec((tm, tk), lambda i, j, k: (i, jnp.clip(k - st, 0, cn - 1)))

    if form == "nn":
        b_spec = pl.BlockSpec((tk, tn), lambda i, j, k: (k, j))
    elif form == "nt":
        b_spec = pl.BlockSpec((tn, tk), lambda i, j, k: (j, k))
    else:
        b_spec = pl.BlockSpec((tk, tn), lambda i, j, k: (k, j))
    in_specs = [a_spec(p) for p in range(nP)] + [b_spec] + [pl.BlockSpec(bs, im) for _, bs, im in extras]
    out_shape = [jax.ShapeDtypeStruct(s_, d_) for s_, d_, _, _ in outs]
    out_specs = [pl.BlockSpec(bs, im) for _, _, bs, im in outs]
    nE, nO = len(extras), len(outs)

    def body(*refs):
        a_refs, b_ref = refs[:nP], refs[nP]
        ex, ou = refs[nP + 1:nP + 1 + nE], refs[nP + 1 + nE:nP + 1 + nE + nO]
        sel = pl.program_id(0) if form == "tn" else pl.program_id(2)

        def partial_of(p):
            return lax.dot_general(a_refs[p][...], b_ref[...], dims, preferred_element_type=F32)

        if nK == 1 and nP == 1:
            epi(partial_of(0), ex, ou)
            return
        acc = refs[-1]
        k = pl.program_id(2)
        for p in range(nP):
            inside = (sel >= starts[p]) & (sel < starts[p] + counts[p]) if nP > 1 else None

            def first(p=p):
                acc[...] = partial_of(p)

            def later(p=p):
                acc[...] += partial_of(p)

            if nP == 1:
                pl.when(k == 0)(first)
                pl.when(k > 0)(later)
            else:
                pl.when(inside & (k == 0))(first)
                pl.when(inside & (k > 0))(later)

        @pl.when(k == nK - 1)
        def _():
            epi(acc[...], ex, ou)

    scratch = [] if (nK == 1 and nP == 1) else [pltpu.VMEM((tm, tn), F32)]
    res, job_res = _pcall(
        body, grid=(nI, nJ, nK), in_specs=in_specs, out_specs=out_specs, out_shape=out_shape, scratch_shapes=scratch,
        name=name, semantics=("parallel", "parallel", "arbitrary"),
        args=[a for a, _ in a_list] + [b] + [e for e, _, _ in extras], job=job)
    return res if job is None else (res, job_res)


def _rms_mod_fwd(name, x, gain, sc, sh, tr):
    T = x.shape[0]

    def body(x_ref, g_ref, sc_ref, sh_ref, h_ref):
        xv = x_ref[...]
        rstd = lax.rsqrt(jnp.mean(xv * xv, axis=-1, keepdims=True) + EPS)
        h_ref[...] = ((xv * rstd * g_ref[...]) * (1.0 + sc_ref[...]) + sh_ref[...]).astype(BF16)

    row = pl.BlockSpec((tr, D), lambda i: (i, 0))
    vec = pl.BlockSpec((1, D), lambda i: (0, 0))
    return pl.pallas_call(
        body, grid=(T // tr,), in_specs=[row, vec, vec, vec], out_specs=row,
        out_shape=jax.ShapeDtypeStruct((T, D), BF16), name=name, compiler_params=_params(("parallel",)),
    )(x, gain, sc, sh)


def _rms_mod_bwd(name, dh, x, gain, sc, dres, tr, gate=None, mo=None):
    T = x.shape[0]
    nR = T // tr
    with_gate = gate is not None

    def body(*refs):
        if with_gate:
            dh_ref, x_ref, g_ref, sc_ref, dr_ref, gt_ref, mo_ref, dx_ref, psh, psc, pg, dmo_ref, pgt = refs
        else:
            dh_ref, x_ref, g_ref, sc_ref, dr_ref, dx_ref, psh, psc, pg = refs
        xv, dhv = x_ref[...], dh_ref[...]
        rstd = lax.rsqrt(jnp.mean(xv * xv, axis=-1, keepdims=True) + EPS)
        xhat = xv * rstd
        n = xhat * g_ref[...]
        dn = dhv * (1.0 + sc_ref[...])
        dxhat = dn * g_ref[...]
        dx = dr_ref[...] + rstd * (dxhat - xhat * jnp.mean(dxhat * xhat, axis=-1, keepdims=True))
        dx_ref[...] = dx
        psh[...] = _fold8(dhv)
        psc[...] = _fold8(dhv * n)
        pg[...] = _fold8(dn * xhat)
        if with_gate:
            dmo_ref[...] = (gt_ref[...] * dx).astype(BF16)
            pgt[...] = _fold8(dx * mo_ref[...].astype(F32))

    row = pl.BlockSpec((tr, D), lambda i: (i, 0))
    vec = pl.BlockSpec((1, D), lambda i: (0, 0))
    part = pl.BlockSpec((8, D), lambda i: (i, 0))
    p_shape = jax.ShapeDtypeStruct((nR * 8, D), F32)
    ins = [dh, x, gain, sc, dres] + ([gate, mo] if with_gate else [])
    in_specs = [row, row, vec, vec, row] + ([vec, row] if with_gate else [])
    out_shape = [jax.ShapeDtypeStruct((T, D), F32), p_shape, p_shape, p_shape]
    out_specs = [row, part, part, part]
    if with_gate:
        out_shape += [jax.ShapeDtypeStruct((T, D), BF16), p_shape]
        out_specs += [row, part]
    return pl.pallas_call(
        body, grid=(nR,), in_specs=in_specs, out_specs=out_specs, out_shape=out_shape, name=name,
        compiler_params=_params(("parallel",)),
    )(*ins)


def _split3(v):
    h = v.astype(BF16)
    r1 = v - h.astype(F32)
    m = r1.astype(BF16)
    lo = (r1 - m.astype(F32)).astype(BF16)
    return h, m, lo


def _tri_mm(tri, v, dims=NN):
    h, m, lo = _split3(v)
    t = tri.astype(BF16)
    mm = lambda p: lax.dot_general(t, p, dims, preferred_element_type=F32)
    return (mm(lo) + mm(m)) + mm(h)


def _hgrn_chunk_terms(q, fl, lb):
    sig = _sigmoid(fl)
    f = lb + (1.0 - lb) * sig
    lf = jnp.log(f)
    kk = 1.0 - f
    sq = _sigmoid(q)
    qf = q * sq
    return sig, f, lf, kk, sq, qf


def _causal(n):
    r = lax.broadcasted_iota(jnp.int32, (n, n), 0)
    c = lax.broadcasted_iota(jnp.int32, (n, n), 1)
    return r >= c


def _hgrn_fwd(proj, lb_logits, o_gain, tt, job=None):
    T = proj.shape[0]
    nT, ncl = T // tt, tt // CHUNK
    C = CHUNK

    def body(q_ref, f_ref, i_ref, g_ref, lbl_ref, og_ref, y_ref, st_ref, S):
        @pl.when(pl.program_id(1) == 0)
        def _():
            S[...] = jnp.zeros_like(S)

        lbl = lbl_ref[...]
        lb = _sigmoid(lbl[0:1, :] - lbl[1:2, :])
        og = og_ref[...]
        shp = (ncl, C, A_HD)
        q, fl, v, g = (r[...].reshape(shp) for r in (q_ref, f_ref, i_ref, g_ref))
        tri = jnp.broadcast_to(_causal(C), (ncl, C, C))
        _, _, lf, kk, _, qf = _hgrn_chunk_terms(q, fl, lb)
        b = _tri_mm(tri, lf, BNN)
        bm, bl = b[:, C // 2 - 1:C // 2, :], b[:, C - 1:C, :]
        qd, kd = qf * jnp.exp(b - bm), kk * jnp.exp(bm - b)
        A = jnp.where(tri, _dot(qd, kd, BNT), 0.0)
        d_st = _dot(v, kk * jnp.exp(bl - b), BTN)
        dec = jnp.exp(bl)
        st = S[...]
        for ci in range(ncl):
            st_ref[0, ci] = st
            st = st * dec[ci] + d_st[ci]
        S[...] = st
        o = _dot(A, v, BNN) + _dot(qf * jnp.exp(b), st_ref[0], BNT)
        r = lax.rsqrt(jnp.mean(o * o, axis=-1, keepdims=True) + EPS)
        y_ref[...] = (o * r * og * (g * _sigmoid(g))).astype(BF16).reshape(tt, A_HD)

    def col(off):
        return pl.BlockSpec((tt, A_HD), lambda h, t: (t, off // A_HD + h))

    head_vec = lambda rows: pl.BlockSpec((rows, A_HD), lambda h, t: (0, h))
    return _pcall(
        body, grid=(A_HEADS, nT),
        in_specs=[col(OFF_QA), col(OFF_FA), col(OFF_IA), col(OFF_GA), head_vec(2), head_vec(1)],
        out_specs=[pl.BlockSpec((tt, A_HD), lambda h, t: (t, h)),
                   pl.BlockSpec((1, ncl, A_HD, A_HD), lambda h, t: (h, t, 0, 0))],
        out_shape=[jax.ShapeDtypeStruct((T, AW), BF16),
                   jax.ShapeDtypeStruct((A_HEADS, T // C, A_HD, A_HD), F32)],
        scratch_shapes=[pltpu.VMEM((A_HD, A_HD), F32)], name="hgrn_fwd", semantics=("parallel", "arbitrary"),
        args=[proj, proj, proj, proj, lb_logits, o_gain], job=job)


def _hgrn_bwd(proj, st, dy, lb_logits, o_gain, tt, job=None):
    T = proj.shape[0]
    nT, ncl = T // tt, tt // CHUNK
    C = CHUNK

    def body(q_ref, f_ref, i_ref, g_ref, st_ref, dy_ref, lbl_ref, og_ref,
             dq_ref, df_ref, di_ref, dg_ref, plb_ref, pog_ref, dS):
        @pl.when(pl.program_id(1) == 0)
        def _():
            dS[...] = jnp.zeros_like(dS)

        lbl = lbl_ref[...]
        lb = _sigmoid(lbl[0:1, :] - lbl[1:2, :])
        og = og_ref[...]
        shp = (ncl, C, A_HD)
        flat = lambda t: t.reshape(tt, A_HD)
        q, fl, v, g, dout = (r[...].reshape(shp) for r in (q_ref, f_ref, i_ref, g_ref, dy_ref))
        tri = jnp.broadcast_to(_causal(C), (ncl, C, C))
        rowi = lax.broadcasted_iota(jnp.int32, shp, 1)
        st0 = st_ref[0]
        sig, f, lf, kk, sq, qf = _hgrn_chunk_terms(q, fl, lb)
        b = _tri_mm(tri, lf, BNN)
        bm, bl = b[:, C // 2 - 1:C // 2, :], b[:, C - 1:C, :]
        e_qd, e_kd, e_ke, e_b = jnp.exp(b - bm), jnp.exp(bm - b), jnp.exp(bl - b), jnp.exp(b)
        qd, kd, ke, qe = qf * e_qd, kk * e_kd, kk * e_ke, qf * e_b
        dec = jnp.exp(bl)
        A = jnp.where(tri, _dot(qd, kd, BNT), 0.0)
        o = _dot(A, v, BNN) + _dot(qe, st0, BNT)
        r = lax.rsqrt(jnp.mean(o * o, axis=-1, keepdims=True) + EPS)
        sg = _sigmoid(g)
        on = o * r * og
        dg_ref[...] = flat((dout * on * (sg * (1.0 + g * (1.0 - sg)))).astype(BF16))
        don = dout * (g * sg)
        pog_ref[...] = _fold8(flat(don * o * r))
        dyh = don * og
        do = r * (dyh - o * (r * r) * jnp.mean(dyh * o, axis=-1, keepdims=True))
        g_st = _dot(do, qe, BTN)
        run = dS[...]
        after = [None] * ncl
        for ci in reversed(range(ncl)):
            after[ci] = run
            run = g_st[ci] + run * dec[ci]
        dS[...] = run
        d_after = jnp.stack(after, axis=0)
        ddec = jnp.sum(d_after * st0, axis=1, keepdims=True)
        dqe = _dot(do, st0, BNN)
        dke = _dot(v, d_after, BNN)
        dA = jnp.where(tri, _dot(do, v, BNT), 0.0)
        dv = _dot(ke, d_after, BNT) + _dot(A, do, BTN)
        dqd = _dot(dA, kd, BNN)
        dkd = _dot(dA, qd, BTN)
        di_ref[...] = flat(dv.astype(BF16))
        dqf = dqe * e_b + dqd * e_qd
        dkk = dkd * e_kd + dke * e_ke
        t_qd, t_kd, t_ke = dqd * qd, dkd * kd, dke * ke
        db = dqe * qe + t_qd - t_kd - t_ke
        dbm = jnp.sum(t_kd - t_qd, axis=1, keepdims=True)
        dbl = jnp.sum(t_ke, axis=1, keepdims=True) + ddec * dec
        db = db + jnp.where(rowi == C // 2 - 1, dbm, 0.0) + jnp.where(rowi == C - 1, dbl, 0.0)
        dlf = _tri_mm(tri, db, BTN)
        dfv = dlf / f - dkk
        df_ref[...] = flat((dfv * (1.0 - lb) * sig * (1.0 - sig)).astype(BF16))
        plb_ref[...] = _fold8(flat(dfv * (1.0 - sig)))
        dq_ref[...] = flat((dqf * (sq * (1.0 + q * (1.0 - sq)))).astype(BF16))

    def col(off):
        return pl.BlockSpec((tt, A_HD), lambda h, t: (nT - 1 - t, off // A_HD + h))

    head_vec = lambda rows: pl.BlockSpec((rows, A_HD), lambda h, t: (0, h))
    o_spec = pl.BlockSpec((tt, A_HD), lambda h, t: (nT - 1 - t, h))
    p_spec = pl.BlockSpec((8, A_HD), lambda h, t: (t, h))
    o_shape = jax.ShapeDtypeStruct((T, AW), BF16)
    p_shape = jax.ShapeDtypeStruct((nT * 8, AW), F32)
    return _pcall(
        body, grid=(A_HEADS, nT),
        in_specs=[col(OFF_QA), col(OFF_FA), col(OFF_IA), col(OFF_GA),
                  pl.BlockSpec((1, ncl, A_HD, A_HD), lambda h, t: (h, nT - 1 - t, 0, 0)),
                  pl.BlockSpec((tt, A_HD), lambda h, t: (nT - 1 - t, h)), head_vec(2), head_vec(1)],
        out_specs=[o_spec, o_spec, o_spec, o_spec, p_spec, p_spec],
        out_shape=[o_shape, o_shape, o_shape, o_shape, p_shape, p_shape],
        scratch_shapes=[pltpu.VMEM((A_HD, A_HD), F32)], name="hgrn_bwd", semantics=("parallel", "arbitrary"),
        args=[proj, proj, proj, proj, st, dy, lb_logits, o_gain], job=job)


LANES = 128
Q_COLS = BW // LANES


def _low_half():
    return lax.broadcasted_iota(jnp.int32, (1, LANES), 1) < B_HD


def _half_sum(t, low):
    lo = jnp.sum(jnp.where(low, t, 0.0), axis=-1, keepdims=True)
    hi = jnp.sum(jnp.where(low, 0.0, t), axis=-1, keepdims=True)
    return jnp.where(low, lo, hi)


def _half_rms(t, low):
    r = lax.rsqrt(_half_sum(t * t, low) * (1.0 / B_HD) + EPS)
    return t * r, r


def _fold_halves(p, low):
    return jnp.where(low, p + pltpu.roll(p, B_HD, 1), 0.0)


def _stack_cols(x):
    return jnp.stack([x[:, c * LANES:(c + 1) * LANES] for c in range(Q_COLS)], axis=0).reshape(KV_HEADS, 2 * BLK, LANES)


def _col_of(t, c):
    return t[c // 2, (c % 2) * BLK:(c % 2 + 1) * BLK]


def _split_halves(col, s, low):
    own = jnp.where(low if s == 0 else jnp.logical_not(low), col, 0.0)
    other = pltpu.roll(own, B_HD, 1)
    return (own, other) if s == 0 else (other, own)


def _swa_keys(kp_ref, kc_ref, vp_ref, vc_ref, kg, low):
    k_lo, k_hi, v_lo, v_hi, hats = [], [], [], [], []
    for j in range(KVW // LANES):
        cs = slice(j * LANES, (j + 1) * LANES)
        k_hat, k_r = _half_rms(jnp.concatenate([kp_ref[:, cs], kc_ref[:, cs]], axis=0), low)
        vcol = jnp.concatenate([vp_ref[:, cs], vc_ref[:, cs]], axis=0)
        hats.append((k_hat, k_r))
        for s in range(2):
            for dst_lo, dst_hi, col in ((k_lo, k_hi, k_hat * kg), (v_lo, v_hi, vcol)):
                lo, hi = _split_halves(col, s, low)
                dst_lo.append(lo)
                dst_hi.append(hi)
    st = lambda parts: jnp.stack(parts, axis=0)
    return st(k_lo), st(k_hi), st(v_lo), st(v_hi), hats


def _swa_mask(first_block):
    qi = lax.broadcasted_iota(jnp.int32, (BLK, 2 * BLK), 0) + BLK
    ki = lax.broadcasted_iota(jnp.int32, (BLK, 2 * BLK), 1)
    rel = qi - ki
    m = (rel >= 0) & (rel < BLK) & (jnp.logical_not(first_block) | (ki >= BLK))
    return jnp.concatenate([m, m], axis=0)


def _sink_cols(sk_ref, hi):
    top = lax.broadcasted_iota(jnp.int32, (2 * BLK, 1), 0) < BLK
    return jnp.stack([jnp.where(top, sk_ref[0, GROUP * hk + hi], sk_ref[0, GROUP * hk + 2 + hi])
                      for hk in range(KV_HEADS)], axis=0)


def _swa_probs(qn, k_half, sink, mask):
    s = jnp.where(mask, _dot(qn, k_half, BNT) * (B_HD ** -0.5), NEG)
    m = jnp.maximum(jnp.max(s, axis=-1, keepdims=True), sink)
    p = jnp.exp(s - m)
    ps = jnp.exp(sink - m)
    inv = 1.0 / (jnp.sum(p, axis=-1, keepdims=True) + ps)
    return p * inv, ps * inv


def _swa_fwd(proj, q_gain, k_gain, sinks, job=None):
    T = proj.shape[0]
    nb = T // BLK

    def body(q_ref, kc_ref, kp_ref, vc_ref, vp_ref, qg_ref, kg_ref, sk_ref, o_ref):
        low = _low_half()
        mask = _swa_mask(pl.program_id(0) == 0)
        qn = _half_rms(_stack_cols(q_ref[...]), low)[0] * qg_ref[...]
        k_lo, k_hi, v_lo, v_hi, _ = _swa_keys(kp_ref, kc_ref, vp_ref, vc_ref, kg_ref[...], low)
        p_lo, _ = _swa_probs(qn, k_lo, _sink_cols(sk_ref, 0), mask)
        p_hi, _ = _swa_probs(qn, k_hi, _sink_cols(sk_ref, 1), mask)
        o = (_dot(p_lo, v_lo, BNN) + _dot(p_hi, v_hi, BNN)).astype(BF16)
        for c in range(Q_COLS):
            o_ref[:, c * LANES:(c + 1) * LANES] = _col_of(o, c)

    q_gain, k_gain = jnp.tile(q_gain, (1, 2)), jnp.tile(k_gain, (1, 2))
    cur = lambda w, off: pl.BlockSpec((BLK, w), lambda i: (i, off // w))
    prev = lambda w, off: pl.BlockSpec((BLK, w), lambda i: (jnp.maximum(i - 1, 0), off // w))
    small = lambda n: pl.BlockSpec((1, 2 * n), lambda i: (0, 0))
    return _pcall(
        body, grid=(nb,),
        in_specs=[cur(BW, OFF_QB), cur(KVW, OFF_KB), prev(KVW, OFF_KB), cur(KVW, OFF_VB), prev(KVW, OFF_VB),
                  small(B_HD), small(B_HD), pl.BlockSpec(memory_space=pltpu.SMEM)],
        out_specs=[pl.BlockSpec((BLK, BW), lambda i: (i, 0))],
        out_shape=[jax.ShapeDtypeStruct((T, BW), BF16)], scratch_shapes=[], name="swa_fwd", semantics=("parallel",),
        args=[proj, proj, proj, proj, proj, q_gain, k_gain, sinks], job=job)


def _swa_bwd(proj, dout, q_gain, k_gain, sinks, job=None):
    T = proj.shape[0]
    nb = T // BLK
    W = BW + 2 * KVW

    def body(q_ref, kc_ref, kp_ref, vc_ref, vp_ref, do_ref, qg_ref, kg_ref, sk_ref,
             dq_ref, dkv_ref, pqg_ref, pkg_ref, psk_ref, dkn_c, dv_c):
        i = pl.program_id(0)
        live = i < nb
        low = _low_half()
        high = jnp.logical_not(low)
        qg, kg = qg_ref[...], kg_ref[...]
        mask = _swa_mask(i == 0)
        lane = lax.broadcasted_iota(jnp.int32, (1, LANES), 1)
        scale = B_HD ** -0.5

        @pl.when(i == 0)
        def _():
            dkn_c[...] = jnp.zeros_like(dkn_c)
            dv_c[...] = jnp.zeros_like(dv_c)

        q_hat, q_r = _half_rms(_stack_cols(q_ref[...]), low)
        qn = q_hat * qg
        k_lo, k_hi, v_lo, v_hi, hats = _swa_keys(kp_ref, kc_ref, vp_ref, vc_ref, kg, low)
        do = _stack_cols(do_ref[...])
        dqn = jnp.zeros((KV_HEADS, 2 * BLK, LANES), F32)
        acc_sk = jnp.zeros((1, LANES), F32)
        dk_parts, dv_parts = [], []
        for hi, (k_h, v_h) in enumerate(((k_lo, v_lo), (k_hi, v_hi))):
            p, ps = _swa_probs(qn, k_h, _sink_cols(sk_ref, hi), mask)
            dp = _dot(do, v_h, BNT)
            delta = jnp.sum(p * dp, axis=-1, keepdims=True)
            ds = p * (dp - delta) * scale
            dqn = dqn + _dot(ds, k_h, BNN)
            dk_parts.append(_dot(ds, qn, BTN))
            dv_parts.append(_dot(p, do, BTN))
            t = ps * delta
            for hk in range(KV_HEADS):
                for rows in range(2):
                    h = GROUP * hk + 2 * rows + hi
                    acc_sk = acc_sk + jnp.where(
                        lane == h, -jnp.sum(t[hk, rows * BLK:(rows + 1) * BLK], axis=0, keepdims=True), 0.0)
        dqh = dqn * qg
        dq = (q_r * (dqh - q_hat * (_half_sum(dqh * q_hat, low) * (1.0 / B_HD)))).astype(BF16)
        for c in range(Q_COLS):
            dq_ref[:, c * LANES:(c + 1) * LANES] = _col_of(dq, c)
        acc_qg = _fold_halves(_fold8((dqn * q_hat).reshape(KV_HEADS * 2 * BLK, LANES)), low)

        def native(parts, j):
            lo_arr, hi_arr = parts
            a, b = 2 * j, 2 * j + 1
            return (jnp.where(low, lo_arr[a], 0.0) + pltpu.roll(jnp.where(high, hi_arr[a], 0.0), B_HD, 1)
                    + jnp.where(high, hi_arr[b], 0.0) + pltpu.roll(jnp.where(low, lo_arr[b], 0.0), B_HD, 1))

        acc_kg = jnp.zeros((8, LANES), F32)
        for j in range(KVW // LANES):
            cs = slice(j * LANES, (j + 1) * LANES)
            dkn = jnp.where(live, native(dk_parts, j), 0.0)
            dvc = jnp.where(live, native(dv_parts, j), 0.0)
            kp_hat, kp_r = hats[j][0][:BLK], hats[j][1][:BLK]
            dkn_prev = dkn_c[:, cs] + dkn[:BLK]
            dv_prev = dv_c[:, cs] + dvc[:BLK]
            acc_kg = acc_kg + _fold8(dkn_prev * kp_hat)
            dkh = dkn_prev * kg
            dkv_ref[:, cs] = (kp_r * (dkh - kp_hat * (_half_sum(dkh * kp_hat, low) * (1.0 / B_HD)))).astype(BF16)
            dkv_ref[:, KVW + j * LANES:KVW + (j + 1) * LANES] = dv_prev.astype(BF16)
            dkn_c[:, cs] = dkn[BLK:]
            dv_c[:, cs] = dvc[BLK:]
        keep = jnp.where(i > 0, 1.0, 0.0)
        pqg_ref[...] = jnp.where(live, acc_qg, 0.0)
        pkg_ref[...] = _fold_halves(acc_kg, low) * keep
        psk_ref[...] = jnp.broadcast_to(jnp.where(live, acc_sk, 0.0), (8, LANES)) * (
            lax.broadcasted_iota(jnp.int32, (8, LANES), 0) == 0).astype(F32)

    q_gain, k_gain = jnp.tile(q_gain, (1, 2)), jnp.tile(k_gain, (1, 2))
    last = nb - 1
    cur = lambda w, off: pl.BlockSpec((BLK, w), lambda i: (jnp.minimum(i, last), off // w))
    prev = lambda w, off: pl.BlockSpec((BLK, w), lambda i: (jnp.maximum(i - 1, 0), off // w))
    small = lambda n: pl.BlockSpec((1, 2 * n), lambda i: (0, 0))
    part = pl.BlockSpec((8, 128), lambda i: (i, 0))
    p_shape = jax.ShapeDtypeStruct(((nb + 1) * 8, 128), F32)
    return _pcall(
        body, grid=(nb + 1,),
        in_specs=[cur(BW, OFF_QB), cur(KVW, OFF_KB), prev(KVW, OFF_KB), cur(KVW, OFF_VB), prev(KVW, OFF_VB),
                  pl.BlockSpec((BLK, BW), lambda i: (jnp.minimum(i, last), 0)), small(B_HD), small(B_HD),
                  pl.BlockSpec(memory_space=pltpu.SMEM)],
        out_specs=[pl.BlockSpec((BLK, BW), lambda i: (i, 0)),
                   pl.BlockSpec((BLK, 2 * KVW), lambda i: (jnp.maximum(i - 1, 0), 0)), part, part, part],
        out_shape=[jax.ShapeDtypeStruct((T + BLK, BW), BF16), jax.ShapeDtypeStruct((T, 2 * KVW), BF16),
                   p_shape, p_shape, p_shape],
        scratch_shapes=[pltpu.VMEM((BLK, KVW), F32), pltpu.VMEM((BLK, KVW), F32)], name="swa_bwd",
        semantics=("arbitrary",), args=[proj, proj, proj, proj, proj, dout, q_gain, k_gain, sinks], job=job)


def _branch_merge(ya_pre, attn, wa_t, wb_t, proj, tm, tn):
    T = ya_pre.shape[0]

    def body(a_ref, b_ref, wa_ref, wb_ref, ga_ref, gb_ref, ya_ref, yb_ref, mg_ref):
        ya = lax.dot_general(a_ref[...], wa_ref[...], NT, preferred_element_type=F32)
        yb = lax.dot_general(b_ref[...], wb_ref[...], NT, preferred_element_type=F32)
        ya_ref[...] = ya.astype(BF16)
        yb_ref[...] = yb.astype(BF16)
        mg_ref[...] = (_sigmoid(ga_ref[...]) * ya + _sigmoid(gb_ref[...]) * yb).astype(BF16)

    o_spec = pl.BlockSpec((tm, tn), lambda i, j: (i, j))
    o_shape = jax.ShapeDtypeStruct((T, D), BF16)
    return pl.pallas_call(
        body, grid=(T // tm, D // tn),
        in_specs=[pl.BlockSpec((tm, AW), lambda i, j: (i, 0)), pl.BlockSpec((tm, BW), lambda i, j: (i, 0)),
                  pl.BlockSpec((tn, AW), lambda i, j: (j, 0)), pl.BlockSpec((tn, BW), lambda i, j: (j, 0)),
                  pl.BlockSpec((tm, tn), lambda i, j: (i, OFF_GTA // tn + j)),
                  pl.BlockSpec((tm, tn), lambda i, j: (i, OFF_GTB // tn + j))],
        out_specs=[o_spec, o_spec, o_spec], out_shape=[o_shape, o_shape, o_shape], name="branch_merge",
        compiler_params=_params(("parallel", "parallel")),
    )(ya_pre, attn, wa_t, wb_t, proj, proj)


def _ij(i, j, k):
    return (i, j)


def _local_step(x, tgt, mod, g1, g2, lbl, og, qg, kg, sk, shards, c_arr):
    win_s, wa_s, wb_s, wout_s, wmi_s, wmo_s = shards
    T = x.shape[0]
    tm, tr, tt = min(1024, T), min(256, T), min(512, T)
    tk_t = min(1024, T)
    tn = 512
    sh1, sc1, gt1, sh2, sc2, gt2 = (mod[:, i * D:(i + 1) * D] for i in range(N_MOD))
    nI = T // tm
    blk = (tm, tn)
    part = lambda: ((nI * 8, D), F32, (8, tn), _ij)
    vec_j = ((1, tn), lambda i, j, k: (0, j))

    h = _rms_mod_fwd("rms1_fwd", x, g1, sc1, sh1, tr)

    def epi_store(acc, ex, ou):
        ou[0][...] = acc.astype(ou[0].dtype)

    (win_t,) = _run_job("gather_w_in", _gather_job([win_s]))
    tm2 = min(2048, T)
    blk2 = (tm2, tn)
    (proj,), (wa_t, wb_t, w_out, wmi_t) = _mm(
        "in_proj", "nt", [(h, D)], win_t, T, IN_W, D, tm2, tn, D, [], [((T, IN_W), F32, blk2, _ij)], epi_store,
        job=_gather_job([wa_s, wb_s, wout_s, wmi_s]))
    (ya_pre, st), _ = _hgrn_fwd(proj, lbl, og, tt)
    (attn,), (w_mo,) = _swa_fwd(proj, qg, kg, sk, job=_gather_job([wmo_s]))
    ya, yb, merged = _branch_merge(ya_pre, attn, wa_t, wb_t, proj, tm, tn)

    def epi_res1(acc, ex, ou):
        x_ref, gt_ref = ex
        ou[0][...] = acc.astype(BF16)
        ou[1][...] = x_ref[...] + gt_ref[...] * acc

    mo, x1 = _mm("out_proj", "nn", [(merged, D)], w_out, T, D, D, tm, tn, D,
                 [(x, blk, _ij), (gt1, *vec_j)], [((T, D), BF16, blk, _ij), ((T, D), F32, blk, _ij)], epi_res1)
    h2 = _rms_mod_fwd("rms2_fwd", x1, g2, sc2, sh2, tr)

    def epi_relu2(acc, ex, ou):
        r = jnp.maximum(acc, 0.0)
        ou[0][...] = r.astype(BF16)
        ou[1][...] = (r * r).astype(BF16)

    r, a = _mm("mlp_in", "nt", [(h2, D)], wmi_t, T, HID, D, tm2, tn, D, [],
               [((T, HID), BF16, blk2, _ij), ((T, HID), BF16, blk2, _ij)], epi_relu2)

    def epi_loss(acc, ex, ou):
        x1_ref, t_ref, gt_ref = ex
        e = x1_ref[...] + gt_ref[...] * acc - t_ref[...]
        dy = e * (1.0 / D)
        ou[0][...] = dy
        ou[1][...] = (gt_ref[...] * dy).astype(BF16)
        ou[2][...] = _fold8(e * e) * (0.5 / D)
        ou[3][...] = _fold8(dy * acc)

    wide = (tm, 1024)
    part_w = ((nI * 8, D), F32, (8, 1024), _ij)
    dy, dz, p_loss, p_gt2 = _mm(
        "mlp_out", "nn", [(a, HID)], w_mo, T, D, HID, tm, 1024, 1024,
        [(x1, wide, _ij), (tgt, wide, _ij), (gt2, (1, 1024), lambda i, j, k: (0, j))],
        [((T, D), F32, wide, _ij), ((T, D), BF16, wide, _ij), part_w, part_w], epi_loss)

    def epi_du(acc, ex, ou):
        ou[0][...] = (acc * (2.0 * ex[0][...].astype(F32))).astype(BF16)

    (du,) = _mm("mlp_out_dx", "nt", [(dz, D)], w_mo, T, HID, D, tm2, tn, D, [(r, blk2, _ij)],
                [((T, HID), BF16, blk2, _ij)], epi_du)
    gblk = (1024, 1024)
    gwide = (1024, D)
    pair_sum = lambda nm, g, r1: _pair_sum("pair_sum_" + nm, g, r1, c_arr, _sum_rows(r1.shape[1]))
    (g_mo,) = _mm("mlp_out_dw", "tn", [(a, HID)], dz, HID, D, T, 1024, D, tk_t, [], [((HID, D), BF16, gwide, _ij)], epi_store)
    (dh2,), (r1_mo,) = _mm("mlp_in_dx", "nn", [(du, HID)], wmi_t, T, D, HID, tm, 1024, 1024, [],
                           [((T, D), F32, (tm, 1024), _ij)], epi_store, job=_pair_job([g_mo]))
    s_mo = pair_sum("mlp_out", g_mo, r1_mo)
    (g_mi,), (r2_mo,) = _mm("mlp_in_dw", "tn", [(du, HID)], h2, HID, D, T, 1024, D, tk_t, [],
                            [((HID, D), BF16, gwide, _ij)], epi_store, job=_chip_job([s_mo]))
    dx1, p_sh2, p_sc2, p_g2, dmo, p_gt1 = _rms_mod_bwd("rms2_bwd", dh2, x1, g2, sc2, dy, tr, gate=gt1, mo=mo)

    def epi_gates(acc, ex, ou):
        ya_ref, yb_ref, ga_ref, gb_ref = ex
        sa, sb = _sigmoid(ga_ref[...]), _sigmoid(gb_ref[...])
        ou[0][...] = (acc * sa).astype(BF16)
        ou[1][...] = (acc * sb).astype(BF16)
        ou[2][...] = (acc * ya_ref[...].astype(F32) * (sa * (1.0 - sa))).astype(BF16)
        ou[3][...] = (acc * yb_ref[...].astype(F32) * (sb * (1.0 - sb))).astype(BF16)

    o_bf = ((T, D), BF16, blk, _ij)
    (dya, dyb, dga, dgb), (r1_mi,) = _mm(
        "out_proj_dx", "nt", [(dmo, D)], w_out, T, D, D, tm, tn, D,
        [(ya, blk, _ij), (yb, blk, _ij), (proj, blk, lambda i, j, k: (i, OFF_GTA // tn + j)),
         (proj, blk, lambda i, j, k: (i, OFF_GTB // tn + j))], [o_bf, o_bf, o_bf, o_bf], epi_gates,
        job=_pair_job([g_mi]))
    s_mi = pair_sum("mlp_in", g_mi, r1_mi)
    (g_out,) = _mm("out_proj_dw", "tn", [(merged, D)], dmo, D, D, T, 1024, 1024, tk_t, [], [((D, D), BF16, gblk, _ij)], epi_store)
    (dya_pre,) = _mm("branch_a_dx", "nn", [(dya, D)], wa_t, T, AW, D, tm, tn, D, [], [((T, AW), F32, blk, _ij)], epi_store)
    (dattn,) = _mm("branch_b_dx", "nn", [(dyb, D)], wb_t, T, BW, D, tm, tn, D, [], [((T, BW), F32, blk, _ij)], epi_store)
    (g_a,) = _mm("branch_a_dw", "tn", [(dya, D)], ya_pre, D, AW, T, 1024, 1024, tk_t, [], [((D, AW), BF16, gblk, _ij)], epi_store)
    (g_b,) = _mm("branch_b_dw", "tn", [(dyb, D)], attn, D, BW, T, 1024, 1024, tk_t, [], [((D, BW), BF16, gblk, _ij)], epi_store)
    (dqa, dfa, dia, dgg, p_lb, p_og), (r2_mi,) = _hgrn_bwd(proj, st, dya_pre, lbl, og, tt, job=_chip_job([s_mi]))
    (dqb, dkv, p_qg, p_kg, p_sk), (r1_out, r1_a, r1_b) = _swa_bwd(proj, dattn, qg, kg, sk, job=_pair_job([g_out, g_a, g_b]))
    s_out, s_a, s_b = pair_sum("out", g_out, r1_out), pair_sum("branch_a", g_a, r1_a), pair_sum("branch_b", g_b, r1_b)
    pieces = [(dqa, AW), (dfa, AW), (dia, AW), (dgg, AW), (dqb, BW), (dkv, 2 * KVW), (dga, D), (dgb, D)]
    (g_in,), (r2_out, r2_a, r2_b) = _mm(
        "in_proj_dw", "tn", pieces, h, IN_W, D, T, 512, 1024, tk_t, [], [((IN_W, D), BF16, (512, 1024), _ij)], epi_store,
        job=_chip_job([s_out, s_a, s_b]))
    (r1_in,) = _run_job("pair_w_in", _pair_job([g_in]))
    s_in = pair_sum("in", g_in, r1_in)
    (dh,), (r2_in,) = _mm("in_proj_dx", "nn", pieces, win_t, T, D, IN_W, tm, 1024, 512, [],
                          [((T, D), F32, (tm, 1024), _ij)], epi_store, job=_chip_job([s_in]))
    dx, p_sh1, p_sc1, p_g1 = _rms_mod_bwd("rms1_bwd", dh, x, g1, sc1, dx1, tr)

    partials = dict(sh1=p_sh1, sc1=p_sc1, gt1=p_gt1, sh2=p_sh2, sc2=p_sc2, gt2=p_gt2, g1=p_g1, g2=p_g2,
                    lb=p_lb, og=p_og, qg=p_qg, kg=p_kg, sk=p_sk, loss=p_loss)
    sums = dict(w_in=(s_in, r2_in), w_branch_a=(s_a, r2_a), w_branch_b=(s_b, r2_b), w_out=(s_out, r2_out),
                w_mlp_in=(s_mi, r2_mi), w_mlp_out=(s_mo, r2_mo))
    return dx, sums, partials


def _exchange_slots(buf, send_sems, recv_sems):
    me = _mesh_pos()
    mine = buf.at[_index(me)]
    sends = []
    for k in range(1, N_DEV):
        cp = pltpu.make_async_remote_copy(src_ref=mine, dst_ref=mine, send_sem=send_sems.at[k - 1],
                                          recv_sem=recv_sems.at[k - 1], device_id=_flip(me, k), device_id_type=MESH)
        cp.start()
        sends.append(cp)
    for k in range(1, N_DEV):
        theirs = buf.at[_index(_flip(me, k))]
        pltpu.make_async_remote_copy(src_ref=theirs, dst_ref=theirs, send_sem=send_sems.at[k - 1],
                                     recv_sem=recv_sems.at[k - 1], device_id=_flip(me, k), device_id_type=MESH).wait_recv()
    for cp in sends:
        cp.wait_send()


ADA_W = N_MOD * D // N_DEV


def _ada_mod(c, w_ada, b_shard):
    def body(c_ref, w_ref, b_ref, mod_ref, sc_ref, cbuf, mbuf, s1, r1, s2, r2):
        me = _index(_mesh_pos())
        cbuf[me] = c_ref[...]
        _exchange_slots(cbuf, s1, r1)
        row = lax.broadcasted_iota(jnp.int32, (N_DEV, D), 0)
        call = jnp.zeros((N_DEV, D), F32)
        for d in range(N_DEV):
            call = jnp.where(row == d, cbuf[d], call)
        sc = call * _sigmoid(call)
        sc_ref[...] = sc
        mbuf[me] = _dot(sc, w_ref[...]) + b_ref[...]
        _exchange_slots(mbuf, s2, r2)
        for s in range(N_DEV):
            mod_ref[:, s * ADA_W:(s + 1) * ADA_W] = mbuf[s, pl.ds(me, 1), :]

    return pl.pallas_call(
        body, in_specs=[_VMEM, _VMEM, _VMEM], out_specs=[_VMEM, _VMEM],
        out_shape=[jax.ShapeDtypeStruct((1, N_MOD * D), F32), jax.ShapeDtypeStruct((N_DEV, D), F32)],
        scratch_shapes=[pltpu.VMEM((N_DEV, 1, D), F32), pltpu.VMEM((N_DEV, N_DEV, ADA_W), F32),
                        _SEMS(N_DEV - 1), _SEMS(N_DEV - 1), _SEMS(N_DEV - 1), _SEMS(N_DEV - 1)],
        name="ada_mod", compiler_params=pltpu.CompilerParams(vmem_limit_bytes=VMEM_LIMIT),
    )(c, w_ada, b_shard)


SMALL_SEGS = (("b_ada", N_MOD * D), ("norm1_gain", D), ("norm2_gain", D), ("lb0", AW), ("lb1", AW),
              ("hgrn_o_gain", AW), ("q_norm_gain", 128), ("k_norm_gain", 128), ("sinks", 128))
SMALL_W = sum(w for _, w in SMALL_SEGS)
X_SEGS = (("sh1", D), ("sc1", D), ("gt1", D), ("sh2", D), ("sc2", D), ("gt2", D), ("g1", D), ("g2", D),
          ("lb", AW), ("og", AW), ("qg", 128), ("kg", 128), ("sk", 128), ("loss", 128))
X_W = sum(w for _, w in X_SEGS)


def _offsets(segs):
    out, o = {}, 0
    for name, w in segs:
        out[name] = (o, w)
        o += w
    return out


def _small_reduce(parts, lb_logits):
    xo, so = _offsets(X_SEGS), _offsets(SMALL_SEGS)
    names = [nm for nm, _ in X_SEGS]

    def body(*refs):
        p_refs = dict(zip(names, refs[:len(names)]))
        lbl_ref, allx, gs_ref, loss_ref, send_sems, recv_sems = refs[len(names):]
        me = _index(_mesh_pos())
        for nm, (o, w) in xo.items():
            if nm == "loss":
                allx[me, :, o:o + w] = jnp.broadcast_to(jnp.sum(p_refs[nm][...]), (1, w))
            else:
                allx[me, :, o:o + w] = jnp.sum(p_refs[nm][...], axis=0, keepdims=True)
        _exchange_slots(allx, send_sems, recv_sems)
        tot = allx[0]
        for d in range(1, N_DEV):
            tot = tot + allx[d]
        seg = lambda nm: tot[:, xo[nm][0]:xo[nm][0] + xo[nm][1]]

        def put(nm, v):
            gs_ref[:, so[nm][0]:so[nm][0] + so[nm][1]] = v

        put("b_ada", tot[:, 0:N_MOD * D])
        put("norm1_gain", seg("g1"))
        put("norm2_gain", seg("g2"))
        lbl = lbl_ref[...]
        lb = _sigmoid(lbl[0:1, :] - lbl[1:2, :])
        dl0 = seg("lb") * lb * (1.0 - lb)
        put("lb0", dl0)
        put("lb1", -dl0)
        put("hgrn_o_gain", seg("og"))
        put("q_norm_gain", seg("qg"))
        put("k_norm_gain", seg("kg"))
        put("sinks", seg("sk"))
        loss_ref[...] = seg("loss")

    return pl.pallas_call(
        body, in_specs=[_VMEM] * (len(names) + 1), out_specs=[_VMEM, _VMEM, _VMEM],
        out_shape=[jax.ShapeDtypeStruct((N_DEV, 1, X_W), F32), jax.ShapeDtypeStruct((1, SMALL_W), F32),
                   jax.ShapeDtypeStruct((1, 128), F32)],
        scratch_shapes=[_SEMS(N_DEV - 1), _SEMS(N_DEV - 1)], name="small_reduce",
        compiler_params=pltpu.CompilerParams(vmem_limit_bytes=VMEM_LIMIT),
    )(*[parts[nm] for nm in names], lb_logits)


def _adamw_math(w, g, m, v):
    m = B1 * m + (1.0 - B1) * g
    v = B2 * v + (1.0 - B2) * (g * g)
    m_hat = m / (1.0 - B1 ** STEP)
    v_hat = v / (1.0 - B2 ** STEP)
    return -LR * (m_hat / (jnp.sqrt(v_hat) + ADAM_EPS) + WD * w), m, v


def _sum_rows(rs):
    return 256 if rs % 256 == 0 else rs // 2


def _pair_sum(name, g, recv, c_arr, tr):
    _, rs, cols = recv.shape
    blk = (1, tr, cols)

    def body(c_ref, g_ref, r_ref, o_ref):
        o_ref[...] = (g_ref[...].astype(F32) + r_ref[...].astype(F32)).astype(BF16)

    grid_spec = pltpu.PrefetchScalarGridSpec(
        num_scalar_prefetch=1, grid=(4, rs // tr),
        in_specs=[pl.BlockSpec(blk, lambda q, i, c: (2 * q + c[0], i, 0)), pl.BlockSpec(blk, lambda q, i, c: (q, i, 0))],
        out_specs=pl.BlockSpec(blk, lambda q, i, c: (q, i, 0)))
    return pl.pallas_call(body, grid_spec=grid_spec, out_shape=jax.ShapeDtypeStruct((4, rs, cols), BF16), name=name,
                          compiler_params=_params(("parallel", "parallel")))(c_arr, g.reshape(N_DEV, rs, cols), recv)


def _final_sum(name, sums, recv, q_arr, tr):
    _, rs, cols = sums.shape

    def body(q_ref, s_ref, r_ref, o_ref):
        o_ref[...] = ((s_ref[0].astype(F32) + r_ref[0].astype(F32)) + r_ref[1].astype(F32)) + r_ref[2].astype(F32)

    grid_spec = pltpu.PrefetchScalarGridSpec(
        num_scalar_prefetch=1, grid=(rs // tr,),
        in_specs=[pl.BlockSpec((1, tr, cols), lambda i, q: (q[0], i, 0)), pl.BlockSpec((3, tr, cols), lambda i, q: (0, i, 0))],
        out_specs=pl.BlockSpec((tr, cols), lambda i, q: (i, 0)))
    return pl.pallas_call(body, grid_spec=grid_spec, out_shape=jax.ShapeDtypeStruct((rs, cols), F32), name=name,
                          compiler_params=_params(("parallel",)))(q_arr, sums, recv)


def _adamw(name, w, g, m, v, tr):
    rows, cols = w.shape

    def body(w_ref, g_ref, m_ref, v_ref, d_ref, nm_ref, nv_ref):
        d_ref[...], nm_ref[...], nv_ref[...] = _adamw_math(w_ref[...], g_ref[...], m_ref[...], v_ref[...])

    spec = pl.BlockSpec((tr, cols), lambda i: (i, 0))
    shape = jax.ShapeDtypeStruct((rows, cols), F32)
    return pl.pallas_call(
        body, grid=(rows // tr,), in_specs=[spec] * 4, out_specs=[spec] * 3, out_shape=[shape] * 3, name=name,
        compiler_params=_params(("parallel",)),
    )(w, g, m, v)


def _ada_update(sc_t, dmod_cols, w, m, v, tr):
    rows, cols = w.shape

    def body(s_ref, d_ref, w_ref, m_ref, v_ref, g_ref, dl_ref, nm_ref, nv_ref):
        g = jnp.dot(s_ref[...], d_ref[...], precision=lax.Precision.HIGHEST, preferred_element_type=F32)
        g_ref[...] = g
        dl_ref[...], nm_ref[...], nv_ref[...] = _adamw_math(w_ref[...], g, m_ref[...], v_ref[...])

    spec = pl.BlockSpec((tr, cols), lambda i: (i, 0))
    shape = jax.ShapeDtypeStruct((rows, cols), F32)
    return pl.pallas_call(
        body, grid=(rows // tr,),
        in_specs=[pl.BlockSpec((tr, N_DEV), lambda i: (i, 0)), pl.BlockSpec((N_DEV, cols), lambda i: (0, 0)), spec, spec, spec],
        out_specs=[spec] * 4, out_shape=[shape] * 4, name="ada_update", compiler_params=_params(("parallel",)),
    )(sc_t, dmod_cols, w, m, v)


BIG = ("w_in", "w_branch_a", "w_branch_b", "w_out", "w_mlp_in", "w_mlp_out")
COLUMN_SHARDED = ("w_in", "w_branch_a", "w_branch_b", "w_mlp_in")
WEIGHTS = ("w_ada", "b_ada", "norm1_gain", "w_in", "lb_logits", "hgrn_o_gain", "q_norm_gain", "k_norm_gain", "sinks",
           "w_branch_a", "w_branch_b", "w_out", "norm2_gain", "w_mlp_in", "w_mlp_out")


def _pack_small(p):
    lb = p["lb_logits"]
    src = dict(p, lb0=lb[0:1], lb1=lb[1:2])
    return jnp.concatenate([jnp.pad(src[nm], ((0, 0), (0, w - src[nm].shape[1]))) for nm, w in SMALL_SEGS], axis=1)


def _unpack_small(vec, shapes):
    so = _offsets(SMALL_SEGS)
    out = {}
    for nm, shp in shapes.items():
        if nm == "lb_logits":
            o = so["lb0"][0]
            out[nm] = vec[0, o:o + 2 * AW].reshape(2, AW)
        else:
            o = so[nm][0]
            out[nm] = vec[:, o:o + shp[1]]
    return out


def kernel(x, c, w_ada, b_ada, norm1_gain, w_in, lb_logits, hgrn_o_gain, q_norm_gain, k_norm_gain, sinks, w_branch_a, w_branch_b, w_out, norm2_gain, w_mlp_in, w_mlp_out, loss_target, m_w_ada, m_b_ada, m_norm1_gain, m_w_in, m_lb_logits, m_hgrn_o_gain, m_q_norm_gain, m_k_norm_gain, m_sinks, m_w_branch_a, m_w_branch_b, m_w_out, m_norm2_gain, m_w_mlp_in, m_w_mlp_out, v_w_ada, v_b_ada, v_norm1_gain, v_w_in, v_lb_logits, v_hgrn_o_gain, v_q_norm_gain, v_k_norm_gain, v_sinks, v_w_branch_a, v_w_branch_b, v_w_out, v_norm2_gain, v_w_mlp_in, v_w_mlp_out):
    w = dict(w_ada=w_ada, b_ada=b_ada, norm1_gain=norm1_gain, w_in=w_in, lb_logits=lb_logits, hgrn_o_gain=hgrn_o_gain,
             q_norm_gain=q_norm_gain, k_norm_gain=k_norm_gain, sinks=sinks, w_branch_a=w_branch_a, w_branch_b=w_branch_b,
             w_out=w_out, norm2_gain=norm2_gain, w_mlp_in=w_mlp_in, w_mlp_out=w_mlp_out)
    m = dict(w_ada=m_w_ada, b_ada=m_b_ada, norm1_gain=m_norm1_gain, w_in=m_w_in, lb_logits=m_lb_logits,
             hgrn_o_gain=m_hgrn_o_gain, q_norm_gain=m_q_norm_gain, k_norm_gain=m_k_norm_gain, sinks=m_sinks,
             w_branch_a=m_w_branch_a, w_branch_b=m_w_branch_b, w_out=m_w_out, norm2_gain=m_norm2_gain,
             w_mlp_in=m_w_mlp_in, w_mlp_out=m_w_mlp_out)
    v = dict(w_ada=v_w_ada, b_ada=v_b_ada, norm1_gain=v_norm1_gain, w_in=v_w_in, lb_logits=v_lb_logits,
             hgrn_o_gain=v_hgrn_o_gain, q_norm_gain=v_q_norm_gain, k_norm_gain=v_k_norm_gain, sinks=v_sinks,
             w_branch_a=v_w_branch_a, w_branch_b=v_w_branch_b, w_out=v_w_out, norm2_gain=v_norm2_gain,
             w_mlp_in=v_w_mlp_in, w_mlp_out=v_w_mlp_out)
    for d in (w, m, v):
        for nm in ("w_ada",) + BIG:
            d[nm] = d[nm][0]
    px, py, pc = _mesh_pos()
    me = _index((px, py, pc))
    c_arr = jnp.reshape(pc, (1,)).astype(jnp.int32)
    q_arr = jnp.reshape(2 * px + py, (1,)).astype(jnp.int32)

    shards = [(w[nm].T if nm in COLUMN_SHARDED else w[nm]).astype(BF16) for nm in BIG]
    b_shard = lax.dynamic_slice(b_ada, (0, me * ADA_W), (1, ADA_W))
    mod, sc_all = _ada_mod(c, w["w_ada"], b_shard)

    dx, sums, parts = _local_step(x[0], loss_target[0], mod, norm1_gain, norm2_gain, lb_logits, hgrn_o_gain,
                                  q_norm_gain, k_norm_gain, sinks, shards, c_arr)

    allx, g_small, loss = _small_reduce(parts, lb_logits)

    grad, delta, new_m, new_v = {}, {}, {}, {}
    for nm in BIG:
        s, r2 = sums[nm]
        rs = s.shape[1]
        g = _final_sum("sum_" + nm, s, r2, q_arr, _sum_rows(rs))
        g = g.T if nm in COLUMN_SHARDED else g
        rows = g.shape[0]
        grad[nm] = g
        delta[nm], new_m[nm], new_v[nm] = _adamw("adamw_" + nm, w[nm], g, m[nm], v[nm], 128 if rows % 128 == 0 else rows)

    dmod_cols = lax.dynamic_slice(allx[:, 0, :], (0, me * ADA_W), (N_DEV, ADA_W))
    grad["w_ada"], delta["w_ada"], new_m["w_ada"], new_v["w_ada"] = _ada_update(
        sc_all.T, dmod_cols, w["w_ada"], m["w_ada"], v["w_ada"], 256)

    small_names = [nm for nm in WEIGHTS if nm not in BIG and nm != "w_ada"]
    shapes = {nm: w[nm].shape for nm in small_names}
    ds, ms, vs = _adamw("adamw_small", _pack_small(w), g_small, _pack_small(m), _pack_small(v), 1)
    for dst, vec in ((grad, g_small), (delta, ds), (new_m, ms), (new_v, vs)):
        dst.update(_unpack_small(vec, shapes))

    def full(d, nm):
        return d[nm][None] if nm in BIG or nm == "w_ada" else d[nm]

    return (loss[0, 0], dx[None], *[full(grad, nm) for nm in WEIGHTS], *[full(delta, nm) for nm in WEIGHTS],
            *[full(new_m, nm) for nm in WEIGHTS], *[full(new_v, nm) for nm in WEIGHTS])
```

```python
import functools

import jax
import jax.numpy as jnp
from jax import lax
from jax.experimental import pallas as pl
from jax.experimental.pallas import tpu as pltpu

F32 = jnp.float32
BF16 = jnp.bfloat16
MESH = pl.DeviceIdType.MESH

N_DEV = 8
D = 2048
A_HEADS, A_HD, CHUNK = 8, 128, 64
AW = A_HEADS * A_HD
Q_HEADS, KV_HEADS, GROUP, B_HD, BLK = 16, 4, 4, 64, 128
BW = Q_HEADS * B_HD
KVW = KV_HEADS * B_HD
HID = 4 * D
IN_W = 4 * AW + BW + 2 * KVW + 2 * D
OFF_QA, OFF_FA, OFF_IA, OFF_GA = 0, AW, 2 * AW, 3 * AW
OFF_QB = 4 * AW
OFF_KB = OFF_QB + BW
OFF_VB = OFF_KB + KVW
OFF_GTA = OFF_VB + KVW
OFF_GTB = OFF_GTA + D
N_MOD = 6
EPS = 1e-6
LR, B1, B2, ADAM_EPS, WD, STEP = 1e-3, 0.9, 0.999, 1e-8, 0.01, 10
NEG = -1e30

VMEM_LIMIT = 56 * 1024 * 1024

NN = (((1,), (0,)), ((), ()))
NT = (((1,), (1,)), ((), ()))
TN = (((0,), (0,)), ((), ()))
BNN = (((2,), (1,)), ((0,), (0,)))
BNT = (((2,), (2,)), ((0,), (0,)))
BTN = (((1,), (1,)), ((0,), (0,)))


def _dot(a, b, dims=NN):
    return lax.dot_general(a.astype(BF16), b.astype(BF16), dims, preferred_element_type=F32)


def _params(sem):
    return pltpu.CompilerParams(dimension_semantics=sem, vmem_limit_bytes=VMEM_LIMIT)


def _sigmoid(x):
    return 1.0 / (1.0 + jnp.exp(-x))


def _fold8(v):
    r, n = v.shape
    return jnp.sum(v.reshape(r // 8, 8, n), axis=0)


_VMEM = pl.BlockSpec(memory_space=pltpu.VMEM)
_ANY = pl.BlockSpec(memory_space=pl.ANY)
_SEMS = lambda n: pltpu.SemaphoreType.DMA((n,))


def _mesh_pos():
    return lax.axis_index("x"), lax.axis_index("y"), lax.axis_index("c")


def _flip(pos, k):
    return tuple(1 - p if (k >> s) & 1 else p for p, s in zip(pos, (2, 1, 0)))


def _index(pos):
    return 4 * pos[0] + 2 * pos[1] + pos[2]


class _Job:
    def __init__(self, ins, out_shape, sems, start, finish):
        self.ins, self.out_shape, self.sems, self.start, self.finish = list(ins), list(out_shape), list(sems), start, finish


def _pcall(body, *, grid, in_specs, out_specs, out_shape, scratch_shapes, name, semantics, args, job=None):
    if job is None:
        outs = pl.pallas_call(body, grid=grid, in_specs=in_specs, out_specs=out_specs, out_shape=out_shape,
                              scratch_shapes=scratch_shapes, name=name, compiler_params=_params(semantics))(*args)
        return list(outs), []
    n_in, n_out, n_scr = len(in_specs), len(out_specs), len(scratch_shapes)
    j_in, j_out = len(job.ins), len(job.out_shape)
    steps = tuple(grid)

    def carrier(*refs):
        o = 0
        main_in, o = refs[o:o + n_in], o + n_in
        job_in, o = refs[o:o + j_in], o + j_in
        main_out, o = refs[o:o + n_out], o + n_out
        job_out, o = refs[o:o + j_out], o + j_out
        main_scr, job_sems = refs[o:o + n_scr], refs[o + n_scr:]
        ids = [pl.program_id(a) for a in range(len(steps))]
        first = functools.reduce(lambda p, q: p & q, [i == 0 for i in ids])
        last = functools.reduce(lambda p, q: p & q, [i == s - 1 for i, s in zip(ids, steps)])

        @pl.when(first)
        def _():
            job.start(job_in, job_out, job_sems)

        body(*main_in, *main_out, *main_scr)

        @pl.when(last)
        def _():
            job.finish(job_in, job_out, job_sems)

    outs = pl.pallas_call(
        carrier, grid=grid, in_specs=list(in_specs) + [_ANY] * j_in, out_specs=list(out_specs) + [_ANY] * j_out,
        out_shape=list(out_shape) + job.out_shape, scratch_shapes=list(scratch_shapes) + job.sems, name=name,
        compiler_params=_params(("arbitrary",) * len(steps)),
    )(*args, *job.ins)
    return list(outs[:n_out]), list(outs[n_out:])


def _run_job(name, job):
    j_in, j_out = len(job.ins), len(job.out_shape)

    def body(*refs):
        ins, outs, sems = refs[:j_in], refs[j_in:j_in + j_out], refs[j_in + j_out:]
        job.start(ins, outs, sems)
        job.finish(ins, outs, sems)

    return list(pl.pallas_call(body, in_specs=[_ANY] * j_in, out_specs=[_ANY] * j_out, out_shape=job.out_shape,
                               scratch_shapes=job.sems, name=name)(*job.ins))


def _gather_job(shards):
    n = len(shards)

    def copies(ins, outs, sems):
        send_sems, recv_sems, local_sems = sems
        x, y, c = _mesh_pos()
        me, sib = (x, y, c), (x, y, 1 - c)
        chips = [(1 - x, y), (x, 1 - y), (1 - x, 1 - y)]

        def rows(a, p):
            rs = shards[a].shape[0]
            return outs[a].at[pl.ds(_index(p) * rs, rs), :]

        def copy(a, k, block, to, src=None):
            return pltpu.make_async_remote_copy(
                src_ref=rows(a, block) if src is None else src, dst_ref=rows(a, block),
                send_sem=send_sems.at[7 * a + k], recv_sem=recv_sems.at[7 * a + k], device_id=to, device_id_type=MESH)

        mine = [pltpu.make_async_copy(ins[a], rows(a, me), local_sems.at[a]) for a in range(n)]
        first = []
        for a in range(n):
            first.append(copy(a, 0, me, sib, src=ins[a]))
            first += [copy(a, 1 + j, me, (*chip, c), src=ins[a]) for j, chip in enumerate(chips)]
        return me, sib, c, chips, copy, mine, first

    def start(ins, outs, sems):
        *_, mine, first = copies(ins, outs, sems)
        for cp in mine + first:
            cp.start()

    def finish(ins, outs, sems):
        me, sib, c, chips, copy, mine, first = copies(ins, outs, sems)
        passed = []
        for j, chip in enumerate(chips):
            for a in range(n):
                copy(a, 1 + j, (*chip, c), me).wait_recv()
                cp = copy(a, 4 + j, (*chip, c), sib)
                cp.start()
                passed.append(cp)
        for a in range(n):
            copy(a, 0, sib, me).wait_recv()
            for j, chip in enumerate(chips):
                copy(a, 4 + j, (*chip, 1 - c), me).wait_recv()
        for cp in first + passed:
            cp.wait_send()
        for cp in mine:
            cp.wait()

    return _Job(shards, [jax.ShapeDtypeStruct((N_DEV * s.shape[0], s.shape[1]), s.dtype) for s in shards],
                [_SEMS(7 * n), _SEMS(7 * n), _SEMS(n)], start, finish)


def _pair_job(grads):
    n = len(grads)

    def copies(ins, outs, sems):
        send_sems, recv_sems = sems
        x, y, c = _mesh_pos()
        out = []
        for a in range(n):
            rs = grads[a].shape[0] // N_DEV
            for q in range(4):
                blk = ins[a].at[pl.ds((2 * q + 1 - c) * rs, rs), :]
                out.append(pltpu.make_async_remote_copy(
                    src_ref=blk, dst_ref=outs[a].at[q], send_sem=send_sems.at[4 * a + q], recv_sem=recv_sems.at[4 * a + q],
                    device_id=(x, y, 1 - c), device_id_type=MESH))
        return out

    def start(ins, outs, sems):
        for cp in copies(ins, outs, sems):
            cp.start()

    def finish(ins, outs, sems):
        for cp in copies(ins, outs, sems):
            cp.wait()

    return _Job(grads, [jax.ShapeDtypeStruct((4, g.shape[0] // N_DEV, g.shape[1]), g.dtype) for g in grads],
                [_SEMS(4 * n), _SEMS(4 * n)], start, finish)


def _chip_job(sums):
    n = len(sums)

    def copies(ins, outs, sems):
        send_sems, recv_sems = sems
        x, y, c = _mesh_pos()
        out = []
        for a in range(n):
            for r in (1, 2, 3):
                px, py = (1 - x if r & 2 else x), (1 - y if r & 1 else y)
                out.append(pltpu.make_async_remote_copy(
                    src_ref=ins[a].at[2 * px + py], dst_ref=outs[a].at[r - 1], send_sem=send_sems.at[3 * a + r - 1],
                    recv_sem=recv_sems.at[3 * a + r - 1], device_id=(px, py, c), device_id_type=MESH))
        return out

    def start(ins, outs, sems):
        for cp in copies(ins, outs, sems):
            cp.start()

    def finish(ins, outs, sems):
        for cp in copies(ins, outs, sems):
            cp.wait()

    return _Job(sums, [jax.ShapeDtypeStruct((3,) + s.shape[1:], s.dtype) for s in sums],
                [_SEMS(3 * n), _SEMS(3 * n)], start, finish)


def _mm(name, form, a_list, b, M, N, K, tm, tn, tk, extras, outs, epi, job=None, acc_as_ref=False):
    nI, nJ, nK = M // tm, N // tn, K // tk
    assert nI * tm == M and nJ * tn == N and nK * tk == K
    dims = {"nn": NN, "nt": NT, "tn": TN}[form]
    split = tm if form == "tn" else tk
    starts, s = [], 0
    for _, w in a_list:
        assert w % split == 0
        starts.append(s // split)
        s += w
    counts = [w // split for _, w in a_list]
    assert s == (M if form == "tn" else K)
    nP = len(a_list)

    def a_spec(p):
        st, cn = starts[p], counts[p]
        if form == "tn":
            return pl.BlockSpec((tk, tm), lambda i, j, k: (jnp.where((i >= st) & (i < st + cn), k, 0),
                                                           jnp.clip(i - st, 0, cn - 1)))
        return pl.BlockSpec((tm, tk), lambda i, j, k: (i, jnp.clip(k - st, 0, cn - 1)))

    if form == "nn":
        b_spec = pl.BlockSpec((tk, tn), lambda i, j, k: (k, j))
    elif form == "nt":
        b_spec = pl.BlockSpec((tn, tk), lambda i, j, k: (j, k))
    else:
        b_spec = pl.BlockSpec((tk, tn), lambda i, j, k: (k, j))
    in_specs = [a_spec(p) for p in range(nP)] + [b_spec] + [pl.BlockSpec(bs, im) for _, bs, im in extras]
    out_shape = [jax.ShapeDtypeStruct(s_, d_) for s_, d_, _, _ in outs]
    out_specs = [pl.BlockSpec(bs, im) for _, _, bs, im in outs]
    nE, nO = len(extras), len(outs)

    def body(*refs):
        a_refs, b_ref = refs[:nP], refs[nP]
        ex, ou = refs[nP + 1:nP + 1 + nE], refs[nP + 1 + nE:nP + 1 + nE + nO]
        sel = pl.program_id(0) if form == "tn" else pl.program_id(2)

        def partial_of(p):
            return lax.dot_general(a_refs[p][...], b_ref[...], dims, preferred_element_type=F32)

        if nK == 1 and nP == 1:
            epi(partial_of(0), ex, ou)
            return
        acc = refs[-1]
        k = pl.program_id(2)
        for p in range(nP):
            inside = (sel >= starts[p]) & (sel < starts[p] + counts[p]) if nP > 1 else None

            def first(p=p):
                acc[...] = partial_of(p)

            def later(p=p):
                acc[...] += partial_of(p)

            if nP == 1:
                pl.when(k == 0)(first)
                pl.when(k > 0)(later)
            else:
                pl.when(inside & (k == 0))(first)
                pl.when(inside & (k > 0))(later)

        @pl.when(k == nK - 1)
        def _():
            epi(acc if acc_as_ref else acc[...], ex, ou)

    scratch = [] if (nK == 1 and nP == 1) else [pltpu.VMEM((tm, tn), F32)]
    res, job_res = _pcall(
        body, grid=(nI, nJ, nK), in_specs=in_specs, out_specs=out_specs, out_shape=out_shape, scratch_shapes=scratch,
        name=name, semantics=("parallel", "parallel", "arbitrary"),
        args=[a for a, _ in a_list] + [b] + [e for e, _, _ in extras], job=job)
    return res if job is None else (res, job_res)


def _rms_mod_fwd(name, x, gain, sc, sh, tr):
    T = x.shape[0]

    def body(x_ref, g_ref, sc_ref, sh_ref, h_ref):
        xv = x_ref[...]
        rstd = lax.rsqrt(jnp.mean(xv * xv, axis=-1, keepdims=True) + EPS)
        h_ref[...] = ((xv * rstd * g_ref[...]) * (1.0 + sc_ref[...]) + sh_ref[...]).astype(BF16)

    row = pl.BlockSpec((tr, D), lambda i: (i, 0))
    vec = pl.BlockSpec((1, D), lambda i: (0, 0))
    return pl.pallas_call(
        body, grid=(T // tr,), in_specs=[row, vec, vec, vec], out_specs=row,
        out_shape=jax.ShapeDtypeStruct((T, D), BF16), name=name, compiler_params=_params(("parallel",)),
    )(x, gain, sc, sh)


def _rms_mod_bwd_epilogue(x, gain, sc, dres, tm, gate=None, mo=None):
    T = x.shape[0]
    with_gate = gate is not None
    row = ((tm, D), lambda i, j, k: (i, 0))
    vec = ((1, D), lambda i, j, k: (0, 0))
    part = ((T // tm * 8, D), F32, (8, D), lambda i, j, k: (i, 0))
    extras = [(x, *row), (gain, *vec), (sc, *vec), (dres, *row)]
    outs = [((T, D), F32, *row), part, part, part]
    if with_gate:
        extras += [(gate, *vec), (mo, *row)]
        outs += [((T, D), BF16, *row), part]

    rows = min(128, tm)

    def epi(acc, ex, ou):
        g = ex[1][...]
        sums = [jnp.zeros((8, D), F32) for _ in range(4)]
        for r0 in range(0, tm, rows):
            rs = slice(r0, r0 + rows)
            dhv, xv = acc[rs, :], ex[0][rs, :]
            rstd = lax.rsqrt(jnp.mean(xv * xv, axis=-1, keepdims=True) + EPS)
            xhat = xv * rstd
            dn = dhv * (1.0 + ex[2][...])
            dxhat = dn * g
            dx = ex[3][rs, :] + rstd * (dxhat - xhat * jnp.mean(dxhat * xhat, axis=-1, keepdims=True))
            ou[0][rs, :] = dx
            terms = [dhv, dhv * (xhat * g), dn * xhat]
            if with_gate:
                ou[4][rs, :] = (ex[4][...] * dx).astype(BF16)
                terms.append(dx * ex[5][rs, :].astype(F32))
            sums = [s + _fold8(t) for s, t in zip(sums, terms)] + sums[len(terms):]
        ou[1][...], ou[2][...], ou[3][...] = sums[:3]
        if with_gate:
            ou[5][...] = sums[3]

    return extras, outs, epi


def _split3(v):
    h = v.astype(BF16)
    r1 = v - h.astype(F32)
    m = r1.astype(BF16)
    lo = (r1 - m.astype(F32)).astype(BF16)
    return h, m, lo


def _tri_mm(tri, v, dims=NN):
    h, m, lo = _split3(v)
    t = tri.astype(BF16)
    mm = lambda p: lax.dot_general(t, p, dims, preferred_element_type=F32)
    return (mm(lo) + mm(m)) + mm(h)


def _hgrn_chunk_terms(q, fl, lb):
    sig = _sigmoid(fl)
    f = lb + (1.0 - lb) * sig
    lf = jnp.log(f)
    kk = 1.0 - f
    sq = _sigmoid(q)
    qf = q * sq
    return sig, f, lf, kk, sq, qf


def _causal(n):
    r = lax.broadcasted_iota(jnp.int32, (n, n), 0)
    c = lax.broadcasted_iota(jnp.int32, (n, n), 1)
    return r >= c


def _hgrn_fwd(proj, lb_logits, o_gain, tt, job=None):
    T = proj.shape[0]
    nT, ncl = T // tt, tt // CHUNK
    C = CHUNK

    def body(q_ref, f_ref, i_ref, g_ref, lbl_ref, og_ref, y_ref, st_ref, S):
        @pl.when(pl.program_id(1) == 0)
        def _():
            S[...] = jnp.zeros_like(S)

        lbl = lbl_ref[...]
        lb = _sigmoid(lbl[0:1, :] - lbl[1:2, :])
        og = og_ref[...]
        shp = (ncl, C, A_HD)
        q, fl, v, g = (r[...].reshape(shp) for r in (q_ref, f_ref, i_ref, g_ref))
        tri = jnp.broadcast_to(_causal(C), (ncl, C, C))
        _, _, lf, kk, _, qf = _hgrn_chunk_terms(q, fl, lb)
        b = _tri_mm(tri, lf, BNN)
        bm, bl = b[:, C // 2 - 1:C // 2, :], b[:, C - 1:C, :]
        qd, kd = qf * jnp.exp(b - bm), kk * jnp.exp(bm - b)
        A = jnp.where(tri, _dot(qd, kd, BNT), 0.0)
        d_st = _dot(v, kk * jnp.exp(bl - b), BTN)
        dec = jnp.exp(bl)
        st = S[...]
        for ci in range(ncl):
            st_ref[0, ci] = st
            st = st * dec[ci] + d_st[ci]
        S[...] = st
        o = _dot(A, v, BNN) + _dot(qf * jnp.exp(b), st_ref[0], BNT)
        r = lax.rsqrt(jnp.mean(o * o, axis=-1, keepdims=True) + EPS)
        y_ref[...] = (o * r * og * (g * _sigmoid(g))).astype(BF16).reshape(tt, A_HD)

    def col(off):
        return pl.BlockSpec((tt, A_HD), lambda h, t: (t, off // A_HD + h))

    head_vec = lambda rows: pl.BlockSpec((rows, A_HD), lambda h, t: (0, h))
    return _pcall(
        body, grid=(A_HEADS, nT),
        in_specs=[col(OFF_QA), col(OFF_FA), col(OFF_IA), col(OFF_GA), head_vec(2), head_vec(1)],
        out_specs=[pl.BlockSpec((tt, A_HD), lambda h, t: (t, h)),
                   pl.BlockSpec((1, ncl, A_HD, A_HD), lambda h, t: (h, t, 0, 0))],
        out_shape=[jax.ShapeDtypeStruct((T, AW), BF16),
                   jax.ShapeDtypeStruct((A_HEADS, T // C, A_HD, A_HD), F32)],
        scratch_shapes=[pltpu.VMEM((A_HD, A_HD), F32)], name="hgrn_fwd", semantics=("parallel", "arbitrary"),
        args=[proj, proj, proj, proj, lb_logits, o_gain], job=job)


def _hgrn_bwd(proj, st, dy, lb_logits, o_gain, tt, job=None):
    T = proj.shape[0]
    nT, ncl = T // tt, tt // CHUNK
    C = CHUNK

    def body(q_ref, f_ref, i_ref, g_ref, st_ref, dy_ref, lbl_ref, og_ref,
             dq_ref, df_ref, di_ref, dg_ref, plb_ref, pog_ref, dS):
        @pl.when(pl.program_id(1) == 0)
        def _():
            dS[...] = jnp.zeros_like(dS)

        lbl = lbl_ref[...]
        lb = _sigmoid(lbl[0:1, :] - lbl[1:2, :])
        og = og_ref[...]
        shp = (ncl, C, A_HD)
        flat = lambda t: t.reshape(tt, A_HD)
        q, fl, v, g, dout = (r[...].reshape(shp) for r in (q_ref, f_ref, i_ref, g_ref, dy_ref))
        tri = jnp.broadcast_to(_causal(C), (ncl, C, C))
        rowi = lax.broadcasted_iota(jnp.int32, shp, 1)
        st0 = st_ref[0]
        sig, f, lf, kk, sq, qf = _hgrn_chunk_terms(q, fl, lb)
        b = _tri_mm(tri, lf, BNN)
        bm, bl = b[:, C // 2 - 1:C // 2, :], b[:, C - 1:C, :]
        e_qd, e_kd, e_ke, e_b = jnp.exp(b - bm), jnp.exp(bm - b), jnp.exp(bl - b), jnp.exp(b)
        qd, kd, ke, qe = qf * e_qd, kk * e_kd, kk * e_ke, qf * e_b
        dec = jnp.exp(bl)
        A = jnp.where(tri, _dot(qd, kd, BNT), 0.0)
        o = _dot(A, v, BNN) + _dot(qe, st0, BNT)
        r = lax.rsqrt(jnp.mean(o * o, axis=-1, keepdims=True) + EPS)
        sg = _sigmoid(g)
        on = o * r * og
        dg_ref[...] = flat((dout * on * (sg * (1.0 + g * (1.0 - sg)))).astype(BF16))
        don = dout * (g * sg)
        pog_ref[...] = _fold8(flat(don * o * r))
        dyh = don * og
        do = r * (dyh - o * (r * r) * jnp.mean(dyh * o, axis=-1, keepdims=True))
        g_st = _dot(do, qe, BTN)
        run = dS[...]
        after = [None] * ncl
        for ci in reversed(range(ncl)):
            after[ci] = run
            run = g_st[ci] + run * dec[ci]
        dS[...] = run
        d_after = jnp.stack(after, axis=0)
        ddec = jnp.sum(d_after * st0, axis=1, keepdims=True)
        dqe = _dot(do, st0, BNN)
        dke = _dot(v, d_after, BNN)
        dA = jnp.where(tri, _dot(do, v, BNT), 0.0)
        dv = _dot(ke, d_after, BNT) + _dot(A, do, BTN)
        dqd = _dot(dA, kd, BNN)
        dkd = _dot(dA, qd, BTN)
        di_ref[...] = flat(dv.astype(BF16))
        dqf = dqe * e_b + dqd * e_qd
        dkk = dkd * e_kd + dke * e_ke
        t_qd, t_kd, t_ke = dqd * qd, dkd * kd, dke * ke
        db = dqe * qe + t_qd - t_kd - t_ke
        dbm = jnp.sum(t_kd - t_qd, axis=1, keepdims=True)
        dbl = jnp.sum(t_ke, axis=1, keepdims=True) + ddec * dec
        db = db + jnp.where(rowi == C // 2 - 1, dbm, 0.0) + jnp.where(rowi == C - 1, dbl, 0.0)
        dlf = _tri_mm(tri, db, BTN)
        dfv = dlf / f - dkk
        df_ref[...] = flat((dfv * (1.0 - lb) * sig * (1.0 - sig)).astype(BF16))
        plb_ref[...] = _fold8(flat(dfv * (1.0 - sig)))
        dq_ref[...] = flat((dqf * (sq * (1.0 + q * (1.0 - sq)))).astype(BF16))

    def col(off):
        return pl.BlockSpec((tt, A_HD), lambda h, t: (nT - 1 - t, off // A_HD + h))

    head_vec = lambda rows: pl.BlockSpec((rows, A_HD), lambda h, t: (0, h))
    o_spec = pl.BlockSpec((tt, A_HD), lambda h, t: (nT - 1 - t, h))
    p_spec = pl.BlockSpec((8, A_HD), lambda h, t: (t, h))
    o_shape = jax.ShapeDtypeStruct((T, AW), BF16)
    p_shape = jax.ShapeDtypeStruct((nT * 8, AW), F32)
    return _pcall(
        body, grid=(A_HEADS, nT),
        in_specs=[col(OFF_QA), col(OFF_FA), col(OFF_IA), col(OFF_GA),
                  pl.BlockSpec((1, ncl, A_HD, A_HD), lambda h, t: (h, nT - 1 - t, 0, 0)),
                  pl.BlockSpec((tt, A_HD), lambda h, t: (nT - 1 - t, h)), head_vec(2), head_vec(1)],
        out_specs=[o_spec, o_spec, o_spec, o_spec, p_spec, p_spec],
        out_shape=[o_shape, o_shape, o_shape, o_shape, p_shape, p_shape],
        scratch_shapes=[pltpu.VMEM((A_HD, A_HD), F32)], name="hgrn_bwd", semantics=("parallel", "arbitrary"),
        args=[proj, proj, proj, proj, st, dy, lb_logits, o_gain], job=job)


LANES = 128
Q_COLS = BW // LANES


def _low_half():
    return lax.broadcasted_iota(jnp.int32, (1, LANES), 1) < B_HD


def _half_sum(t, low):
    lo = jnp.sum(jnp.where(low, t, 0.0), axis=-1, keepdims=True)
    hi = jnp.sum(jnp.where(low, 0.0, t), axis=-1, keepdims=True)
    return jnp.where(low, lo, hi)


def _half_rms(t, low):
    r = lax.rsqrt(_half_sum(t * t, low) * (1.0 / B_HD) + EPS)
    return t * r, r


def _fold_halves(p, low):
    return jnp.where(low, p + pltpu.roll(p, B_HD, 1), 0.0)


def _stack_cols(x):
    return jnp.stack([x[:, c * LANES:(c + 1) * LANES] for c in range(Q_COLS)], axis=0).reshape(KV_HEADS, 2 * BLK, LANES)


def _col_of(t, c):
    return t[c // 2, (c % 2) * BLK:(c % 2 + 1) * BLK]


def _split_halves(col, s, low):
    own = jnp.where(low if s == 0 else jnp.logical_not(low), col, 0.0)
    other = pltpu.roll(own, B_HD, 1)
    return (own, other) if s == 0 else (other, own)


def _swa_keys(kp_ref, kc_ref, vp_ref, vc_ref, kg, low):
    k_lo, k_hi, v_lo, v_hi, hats = [], [], [], [], []
    for j in range(KVW // LANES):
        cs = slice(j * LANES, (j + 1) * LANES)
        k_hat, k_r = _half_rms(jnp.concatenate([kp_ref[:, cs], kc_ref[:, cs]], axis=0), low)
        vcol = jnp.concatenate([vp_ref[:, cs], vc_ref[:, cs]], axis=0)
        hats.append((k_hat, k_r))
        for s in range(2):
            for dst_lo, dst_hi, col in ((k_lo, k_hi, k_hat * kg), (v_lo, v_hi, vcol)):
                lo, hi = _split_halves(col, s, low)
                dst_lo.append(lo)
                dst_hi.append(hi)
    st = lambda parts: jnp.stack(parts, axis=0)
    return st(k_lo), st(k_hi), st(v_lo), st(v_hi), hats


def _swa_mask(first_block):
    qi = lax.broadcasted_iota(jnp.int32, (BLK, 2 * BLK), 0) + BLK
    ki = lax.broadcasted_iota(jnp.int32, (BLK, 2 * BLK), 1)
    rel = qi - ki
    m = (rel >= 0) & (rel < BLK) & (jnp.logical_not(first_block) | (ki >= BLK))
    return jnp.concatenate([m, m], axis=0)


def _sink_cols(sk_ref, hi):
    top = lax.broadcasted_iota(jnp.int32, (2 * BLK, 1), 0) < BLK
    return jnp.stack([jnp.where(top, sk_ref[0, GROUP * hk + hi], sk_ref[0, GROUP * hk + 2 + hi])
                      for hk in range(KV_HEADS)], axis=0)


def _swa_probs(qn, k_half, sink, mask):
    s = jnp.where(mask, _dot(qn, k_half, BNT) * (B_HD ** -0.5), NEG)
    m = jnp.maximum(jnp.max(s, axis=-1, keepdims=True), sink)
    p = jnp.exp(s - m)
    ps = jnp.exp(sink - m)
    inv = 1.0 / (jnp.sum(p, axis=-1, keepdims=True) + ps)
    return p * inv, ps * inv


def _swa_fwd(proj, q_gain, k_gain, sinks, job=None):
    T = proj.shape[0]
    nb = T // BLK

    def body(q_ref, kc_ref, kp_ref, vc_ref, vp_ref, qg_ref, kg_ref, sk_ref, o_ref):
        low = _low_half()
        mask = _swa_mask(pl.program_id(0) == 0)
        qn = _half_rms(_stack_cols(q_ref[...]), low)[0] * qg_ref[...]
        k_lo, k_hi, v_lo, v_hi, _ = _swa_keys(kp_ref, kc_ref, vp_ref, vc_ref, kg_ref[...], low)
        p_lo, _ = _swa_probs(qn, k_lo, _sink_cols(sk_ref, 0), mask)
        p_hi, _ = _swa_probs(qn, k_hi, _sink_cols(sk_ref, 1), mask)
        o = (_dot(p_lo, v_lo, BNN) + _dot(p_hi, v_hi, BNN)).astype(BF16)
        for c in range(Q_COLS):
            o_ref[:, c * LANES:(c + 1) * LANES] = _col_of(o, c)

    q_gain, k_gain = jnp.tile(q_gain, (1, 2)), jnp.tile(k_gain, (1, 2))
    cur = lambda w, off: pl.BlockSpec((BLK, w), lambda i: (i, off // w))
    prev = lambda w, off: pl.BlockSpec((BLK, w), lambda i: (jnp.maximum(i - 1, 0), off // w))
    small = lambda n: pl.BlockSpec((1, 2 * n), lambda i: (0, 0))
    return _pcall(
        body, grid=(nb,),
        in_specs=[cur(BW, OFF_QB), cur(KVW, OFF_KB), prev(KVW, OFF_KB), cur(KVW, OFF_VB), prev(KVW, OFF_VB),
                  small(B_HD), small(B_HD), pl.BlockSpec(memory_space=pltpu.SMEM)],
        out_specs=[pl.BlockSpec((BLK, BW), lambda i: (i, 0))],
        out_shape=[jax.ShapeDtypeStruct((T, BW), BF16)], scratch_shapes=[], name="swa_fwd", semantics=("parallel",),
        args=[proj, proj, proj, proj, proj, q_gain, k_gain, sinks], job=job)


def _swa_bwd(proj, dout, q_gain, k_gain, sinks, job=None):
    T = proj.shape[0]
    nb = T // BLK
    W = BW + 2 * KVW

    def body(q_ref, kc_ref, kp_ref, vc_ref, vp_ref, do_ref, qg_ref, kg_ref, sk_ref,
             dq_ref, dkv_ref, pqg_ref, pkg_ref, psk_ref, dkn_c, dv_c):
        i = pl.program_id(0)
        live = i < nb
        low = _low_half()
        high = jnp.logical_not(low)
        qg, kg = qg_ref[...], kg_ref[...]
        mask = _swa_mask(i == 0)
        lane = lax.broadcasted_iota(jnp.int32, (1, LANES), 1)
        scale = B_HD ** -0.5

        @pl.when(i == 0)
        def _():
            dkn_c[...] = jnp.zeros_like(dkn_c)
            dv_c[...] = jnp.zeros_like(dv_c)

        q_hat, q_r = _half_rms(_stack_cols(q_ref[...]), low)
        qn = q_hat * qg
        k_lo, k_hi, v_lo, v_hi, hats = _swa_keys(kp_ref, kc_ref, vp_ref, vc_ref, kg, low)
        do = _stack_cols(do_ref[...])
        dqn = jnp.zeros((KV_HEADS, 2 * BLK, LANES), F32)
        acc_sk = jnp.zeros((1, LANES), F32)
        dk_parts, dv_parts = [], []
        for hi, (k_h, v_h) in enumerate(((k_lo, v_lo), (k_hi, v_hi))):
            p, ps = _swa_probs(qn, k_h, _sink_cols(sk_ref, hi), mask)
            dp = _dot(do, v_h, BNT)
            delta = jnp.sum(p * dp, axis=-1, keepdims=True)
            ds = p * (dp - delta) * scale
            dqn = dqn + _dot(ds, k_h, BNN)
            dk_parts.append(_dot(ds, qn, BTN))
            dv_parts.append(_dot(p, do, BTN))
            t = ps * delta
            for hk in range(KV_HEADS):
                for rows in range(2):
                    h = GROUP * hk + 2 * rows + hi
                    acc_sk = acc_sk + jnp.where(
                        lane == h, -jnp.sum(t[hk, rows * BLK:(rows + 1) * BLK], axis=0, keepdims=True), 0.0)
        dqh = dqn * qg
        dq = (q_r * (dqh - q_hat * (_half_sum(dqh * q_hat, low) * (1.0 / B_HD)))).astype(BF16)
        for c in range(Q_COLS):
            dq_ref[:, c * LANES:(c + 1) * LANES] = _col_of(dq, c)
        acc_qg = _fold_halves(_fold8((dqn * q_hat).reshape(KV_HEADS * 2 * BLK, LANES)), low)

        def native(parts, j):
            lo_arr, hi_arr = parts
            a, b = 2 * j, 2 * j + 1
            return (jnp.where(low, lo_arr[a], 0.0) + pltpu.roll(jnp.where(high, hi_arr[a], 0.0), B_HD, 1)
                    + jnp.where(high, hi_arr[b], 0.0) + pltpu.roll(jnp.where(low, lo_arr[b], 0.0), B_HD, 1))

        acc_kg = jnp.zeros((8, LANES), F32)
        for j in range(KVW // LANES):
            cs = slice(j * LANES, (j + 1) * LANES)
            dkn = jnp.where(live, native(dk_parts, j), 0.0)
            dvc = jnp.where(live, native(dv_parts, j), 0.0)
            kp_hat, kp_r = hats[j][0][:BLK], hats[j][1][:BLK]
            dkn_prev = dkn_c[:, cs] + dkn[:BLK]
            dv_prev = dv_c[:, cs] + dvc[:BLK]
            acc_kg = acc_kg + _fold8(dkn_prev * kp_hat)
            dkh = dkn_prev * kg
            dkv_ref[:, cs] = (kp_r * (dkh - kp_hat * (_half_sum(dkh * kp_hat, low) * (1.0 / B_HD)))).astype(BF16)
            dkv_ref[:, KVW + j * LANES:KVW + (j + 1) * LANES] = dv_prev.astype(BF16)
            dkn_c[:, cs] = dkn[BLK:]
            dv_c[:, cs] = dvc[BLK:]
        keep = jnp.where(i > 0, 1.0, 0.0)
        pqg_ref[...] = jnp.where(live, acc_qg, 0.0)
        pkg_ref[...] = _fold_halves(acc_kg, low) * keep
        psk_ref[...] = jnp.broadcast_to(jnp.where(live, acc_sk, 0.0), (8, LANES)) * (
            lax.broadcasted_iota(jnp.int32, (8, LANES), 0) == 0).astype(F32)

    q_gain, k_gain = jnp.tile(q_gain, (1, 2)), jnp.tile(k_gain, (1, 2))
    last = nb - 1
    cur = lambda w, off: pl.BlockSpec((BLK, w), lambda i: (jnp.minimum(i, last), off // w))
    prev = lambda w, off: pl.BlockSpec((BLK, w), lambda i: (jnp.maximum(i - 1, 0), off // w))
    small = lambda n: pl.BlockSpec((1, 2 * n), lambda i: (0, 0))
    part = pl.BlockSpec((8, 128), lambda i: (i, 0))
    p_shape = jax.ShapeDtypeStruct(((nb + 1) * 8, 128), F32)
    return _pcall(
        body, grid=(nb + 1,),
        in_specs=[cur(BW, OFF_QB), cur(KVW, OFF_KB), prev(KVW, OFF_KB), cur(KVW, OFF_VB), prev(KVW, OFF_VB),
                  pl.BlockSpec((BLK, BW), lambda i: (jnp.minimum(i, last), 0)), small(B_HD), small(B_HD),
                  pl.BlockSpec(memory_space=pltpu.SMEM)],
        out_specs=[pl.BlockSpec((BLK, BW), lambda i: (i, 0)),
                   pl.BlockSpec((BLK, 2 * KVW), lambda i: (jnp.maximum(i - 1, 0), 0)), part, part, part],
        out_shape=[jax.ShapeDtypeStruct((T + BLK, BW), BF16), jax.ShapeDtypeStruct((T, 2 * KVW), BF16),
                   p_shape, p_shape, p_shape],
        scratch_shapes=[pltpu.VMEM((BLK, KVW), F32), pltpu.VMEM((BLK, KVW), F32)], name="swa_bwd",
        semantics=("arbitrary",), args=[proj, proj, proj, proj, proj, dout, q_gain, k_gain, sinks], job=job)


def _branch_merge(ya_pre, attn, wa_t, wb_t, proj, tm, tn):
    T = ya_pre.shape[0]

    def body(a_ref, b_ref, wa_ref, wb_ref, ga_ref, gb_ref, ya_ref, yb_ref, mg_ref):
        ya = lax.dot_general(a_ref[...], wa_ref[...], NT, preferred_element_type=F32)
        yb = lax.dot_general(b_ref[...], wb_ref[...], NT, preferred_element_type=F32)
        ya_ref[...] = ya.astype(BF16)
        yb_ref[...] = yb.astype(BF16)
        mg_ref[...] = (_sigmoid(ga_ref[...]) * ya + _sigmoid(gb_ref[...]) * yb).astype(BF16)

    o_spec = pl.BlockSpec((tm, tn), lambda i, j: (i, j))
    o_shape = jax.ShapeDtypeStruct((T, D), BF16)
    return pl.pallas_call(
        body, grid=(T // tm, D // tn),
        in_specs=[pl.BlockSpec((tm, AW), lambda i, j: (i, 0)), pl.BlockSpec((tm, BW), lambda i, j: (i, 0)),
                  pl.BlockSpec((tn, AW), lambda i, j: (j, 0)), pl.BlockSpec((tn, BW), lambda i, j: (j, 0)),
                  pl.BlockSpec((tm, tn), lambda i, j: (i, OFF_GTA // tn + j)),
                  pl.BlockSpec((tm, tn), lambda i, j: (i, OFF_GTB // tn + j))],
        out_specs=[o_spec, o_spec, o_spec], out_shape=[o_shape, o_shape, o_shape], name="branch_merge",
        compiler_params=_params(("parallel", "parallel")),
    )(ya_pre, attn, wa_t, wb_t, proj, proj)


def _ij(i, j, k):
    return (i, j)


def _local_step(x, tgt, mod, g1, g2, lbl, og, qg, kg, sk, shards, c_arr):
    win_s, wa_s, wb_s, wout_s, wmi_s, wmo_s = shards
    T = x.shape[0]
    tm, tr, tt = min(1024, T), min(256, T), min(512, T)
    tk_t = min(1024, T)
    tn = 512
    sh1, sc1, gt1, sh2, sc2, gt2 = (mod[:, i * D:(i + 1) * D] for i in range(N_MOD))
    nI = T // tm
    blk = (tm, tn)
    part = lambda: ((nI * 8, D), F32, (8, tn), _ij)
    vec_j = ((1, tn), lambda i, j, k: (0, j))

    h = _rms_mod_fwd("rms1_fwd", x, g1, sc1, sh1, tr)

    def epi_store(acc, ex, ou):
        ou[0][...] = acc.astype(ou[0].dtype)

    (win_t,) = _run_job("gather_w_in", _gather_job([win_s]))
    tm2 = min(2048, T)
    blk2 = (tm2, tn)
    (proj,), (wa_t, wb_t, w_out, wmi_t) = _mm(
        "in_proj", "nt", [(h, D)], win_t, T, IN_W, D, tm2, tn, D, [], [((T, IN_W), F32, blk2, _ij)], epi_store,
        job=_gather_job([wa_s, wb_s, wout_s, wmi_s]))
    (ya_pre, st), _ = _hgrn_fwd(proj, lbl, og, tt)
    (attn,), _ = _swa_fwd(proj, qg, kg, sk)
    ya, yb, merged = _branch_merge(ya_pre, attn, wa_t, wb_t, proj, tm, tn)

    def epi_res1(acc, ex, ou):
        x_ref, gt_ref = ex
        ou[0][...] = acc.astype(BF16)
        ou[1][...] = x_ref[...] + gt_ref[...] * acc

    mo, x1 = _mm("out_proj", "nn", [(merged, D)], w_out, T, D, D, tm, tn, D,
                 [(x, blk, _ij), (gt1, *vec_j)], [((T, D), BF16, blk, _ij), ((T, D), F32, blk, _ij)], epi_res1)
    h2 = _rms_mod_fwd("rms2_fwd", x1, g2, sc2, sh2, tr)

    def epi_relu2(acc, ex, ou):
        r = jnp.maximum(acc, 0.0)
        ou[0][...] = r.astype(BF16)
        ou[1][...] = (r * r).astype(BF16)

    (r, a), (w_mo,) = _mm("mlp_in", "nt", [(h2, D)], wmi_t, T, HID, D, tm2, tn, D, [],
                          [((T, HID), BF16, blk2, _ij), ((T, HID), BF16, blk2, _ij)], epi_relu2,
                          job=_gather_job([wmo_s]))

    def epi_loss(acc, ex, ou):
        x1_ref, t_ref, gt_ref = ex
        e = x1_ref[...] + gt_ref[...] * acc - t_ref[...]
        dy = e * (1.0 / D)
        ou[0][...] = dy
        ou[1][...] = (gt_ref[...] * dy).astype(BF16)
        ou[2][...] = _fold8(e * e) * (0.5 / D)
        ou[3][...] = _fold8(dy * acc)

    wide = (tm, 1024)
    part_w = ((nI * 8, D), F32, (8, 1024), _ij)
    dy, dz, p_loss, p_gt2 = _mm(
        "mlp_out", "nn", [(a, HID)], w_mo, T, D, HID, tm, 1024, 1024,
        [(x1, wide, _ij), (tgt, wide, _ij), (gt2, (1, 1024), lambda i, j, k: (0, j))],
        [((T, D), F32, wide, _ij), ((T, D), BF16, wide, _ij), part_w, part_w], epi_loss)

    def epi_du(acc, ex, ou):
        ou[0][...] = (acc * (2.0 * ex[0][...].astype(F32))).astype(BF16)

    (du,) = _mm("mlp_out_dx", "nt", [(dz, D)], w_mo, T, HID, D, tm2, tn, D, [(r, blk2, _ij)],
                [((T, HID), BF16, blk2, _ij)], epi_du)
    gblk = (1024, 1024)
    gwide = (1024, D)
    pair_sum = lambda nm, g, r1: _pair_sum("pair_sum_" + nm, g, r1, c_arr, _sum_rows(r1.shape[1]))
    (g_mo,) = _mm("mlp_out_dw", "tn", [(a, HID)], dz, HID, D, T, 1024, D, tk_t, [], [((HID, D), BF16, gwide, _ij)], epi_store)
    tm_row = min(512, T)
    extras, outs, epi = _rms_mod_bwd_epilogue(x1, g2, sc2, dy, tm_row, gate=gt1, mo=mo)
    (dx1, p_sh2, p_sc2, p_g2, dmo, p_gt1), (r1_mo,) = _mm(
        "mlp_in_dx", "nn", [(du, HID)], wmi_t, T, D, HID, tm_row, D, 512, extras, outs, epi, job=_pair_job([g_mo]),
        acc_as_ref=True)
    s_mo = pair_sum("mlp_out", g_mo, r1_mo)
    (g_mi,), (r2_mo,) = _mm("mlp_in_dw", "tn", [(du, HID)], h2, HID, D, T, 1024, D, tk_t, [],
                            [((HID, D), BF16, gwide, _ij)], epi_store, job=_chip_job([s_mo]))

    def epi_gates(acc, ex, ou):
        ya_ref, yb_ref, ga_ref, gb_ref = ex
        sa, sb = _sigmoid(ga_ref[...]), _sigmoid(gb_ref[...])
        ou[0][...] = (acc * sa).astype(BF16)
        ou[1][...] = (acc * sb).astype(BF16)
        ou[2][...] = (acc * ya_ref[...].astype(F32) * (sa * (1.0 - sa))).astype(BF16)
        ou[3][...] = (acc * yb_ref[...].astype(F32) * (sb * (1.0 - sb))).astype(BF16)

    o_bf = ((T, D), BF16, blk, _ij)
    (dya, dyb, dga, dgb), (r1_mi,) = _mm(
        "out_proj_dx", "nt", [(dmo, D)], w_out, T, D, D, tm, tn, D,
        [(ya, blk, _ij), (yb, blk, _ij), (proj, blk, lambda i, j, k: (i, OFF_GTA // tn + j)),
         (proj, blk, lambda i, j, k: (i, OFF_GTB // tn + j))], [o_bf, o_bf, o_bf, o_bf], epi_gates,
        job=_pair_job([g_mi]))
    s_mi = pair_sum("mlp_in", g_mi, r1_mi)
    (g_out,) = _mm("out_proj_dw", "tn", [(merged, D)], dmo, D, D, T, 1024, 1024, tk_t, [], [((D, D), BF16, gblk, _ij)], epi_store)
    (dya_pre,) = _mm("branch_a_dx", "nn", [(dya, D)], wa_t, T, AW, D, tm, tn, D, [], [((T, AW), F32, blk, _ij)], epi_store)
    (dattn,) = _mm("branch_b_dx", "nn", [(dyb, D)], wb_t, T, BW, D, tm, tn, D, [], [((T, BW), F32, blk, _ij)], epi_store)
    (g_a,) = _mm("branch_a_dw", "tn", [(dya, D)], ya_pre, D, AW, T, 1024, 1024, tk_t, [], [((D, AW), BF16, gblk, _ij)], epi_store)
    (g_b,) = _mm("branch_b_dw", "tn", [(dyb, D)], attn, D, BW, T, 1024, 1024, tk_t, [], [((D, BW), BF16, gblk, _ij)], epi_store)
    (dqa, dfa, dia, dgg, p_lb, p_og), (r2_mi,) = _hgrn_bwd(proj, st, dya_pre, lbl, og, tt, job=_chip_job([s_mi]))
    (dqb, dkv, p_qg, p_kg, p_sk), (r1_out, r1_a, r1_b) = _swa_bwd(proj, dattn, qg, kg, sk, job=_pair_job([g_out, g_a, g_b]))
    s_out, s_a, s_b = pair_sum("out", g_out, r1_out), pair_sum("branch_a", g_a, r1_a), pair_sum("branch_b", g_b, r1_b)
    pieces = [(dqa, AW), (dfa, AW), (dia, AW), (dgg, AW), (dqb, BW), (dkv, 2 * KVW), (dga, D), (dgb, D)]
    (g_in,), (r2_out, r2_a, r2_b) = _mm(
        "in_proj_dw", "tn", pieces, h, IN_W, D, T, 512, D, tk_t, [], [((IN_W, D), BF16, (512, D), _ij)], epi_store,
        job=_chip_job([s_out, s_a, s_b]))
    (r1_in,) = _run_job("pair_w_in", _pair_job([g_in]))
    s_in = pair_sum("in", g_in, r1_in)
    extras, outs, epi = _rms_mod_bwd_epilogue(x, g1, sc1, dx1, tm_row)
    (dx, p_sh1, p_sc1, p_g1), (r2_in,) = _mm(
        "in_proj_dx", "nn", pieces, win_t, T, D, IN_W, tm_row, D, 512, extras, outs, epi, job=_chip_job([s_in]),
        acc_as_ref=True)

    partials = dict(sh1=p_sh1, sc1=p_sc1, gt1=p_gt1, sh2=p_sh2, sc2=p_sc2, gt2=p_gt2, g1=p_g1, g2=p_g2,
                    lb=p_lb, og=p_og, qg=p_qg, kg=p_kg, sk=p_sk, loss=p_loss)
    sums = dict(w_in=(s_in, r2_in), w_branch_a=(s_a, r2_a), w_branch_b=(s_b, r2_b), w_out=(s_out, r2_out),
                w_mlp_in=(s_mi, r2_mi), w_mlp_out=(s_mo, r2_mo))
    return dx, sums, partials


def _exchange_slots(buf, send_sems, recv_sems):
    me = _mesh_pos()
    mine = buf.at[_index(me)]
    sends = []
    for k in range(1, N_DEV):
        cp = pltpu.make_async_remote_copy(src_ref=mine, dst_ref=mine, send_sem=send_sems.at[k - 1],
                                          recv_sem=recv_sems.at[k - 1], device_id=_flip(me, k), device_id_type=MESH)
        cp.start()
        sends.append(cp)
    for k in range(1, N_DEV):
        theirs = buf.at[_index(_flip(me, k))]
        pltpu.make_async_remote_copy(src_ref=theirs, dst_ref=theirs, send_sem=send_sems.at[k - 1],
                                     recv_sem=recv_sems.at[k - 1], device_id=_flip(me, k), device_id_type=MESH).wait_recv()
    for cp in sends:
        cp.wait_send()


ADA_W = N_MOD * D // N_DEV


def _ada_mod(c, w_ada, b_shard):
    def body(c_ref, w_ref, b_ref, mod_ref, sc_ref, cbuf, mbuf, s1, r1, s2, r2):
        me = _index(_mesh_pos())
        cbuf[me] = c_ref[...]
        _exchange_slots(cbuf, s1, r1)
        row = lax.broadcasted_iota(jnp.int32, (N_DEV, D), 0)
        call = jnp.zeros((N_DEV, D), F32)
        for d in range(N_DEV):
            call = jnp.where(row == d, cbuf[d], call)
        sc = call * _sigmoid(call)
        sc_ref[...] = sc
        mbuf[me] = _dot(sc, w_ref[...]) + b_ref[...]
        _exchange_slots(mbuf, s2, r2)
        for s in range(N_DEV):
            mod_ref[:, s * ADA_W:(s + 1) * ADA_W] = mbuf[s, pl.ds(me, 1), :]

    return pl.pallas_call(
        body, in_specs=[_VMEM, _VMEM, _VMEM], out_specs=[_VMEM, _VMEM],
        out_shape=[jax.ShapeDtypeStruct((1, N_MOD * D), F32), jax.ShapeDtypeStruct((N_DEV, D), F32)],
        scratch_shapes=[pltpu.VMEM((N_DEV, 1, D), F32), pltpu.VMEM((N_DEV, N_DEV, ADA_W), F32),
                        _SEMS(N_DEV - 1), _SEMS(N_DEV - 1), _SEMS(N_DEV - 1), _SEMS(N_DEV - 1)],
        name="ada_mod", compiler_params=pltpu.CompilerParams(vmem_limit_bytes=VMEM_LIMIT),
    )(c, w_ada, b_shard)


SMALL_SEGS = (("b_ada", N_MOD * D), ("norm1_gain", D), ("norm2_gain", D), ("lb0", AW), ("lb1", AW),
              ("hgrn_o_gain", AW), ("q_norm_gain", 128), ("k_norm_gain", 128), ("sinks", 128))
SMALL_W = sum(w for _, w in SMALL_SEGS)
X_SEGS = (("sh1", D), ("sc1", D), ("gt1", D), ("sh2", D), ("sc2", D), ("gt2", D), ("g1", D), ("g2", D),
          ("lb", AW), ("og", AW), ("qg", 128), ("kg", 128), ("sk", 128), ("loss", 128))
X_W = sum(w for _, w in X_SEGS)


def _offsets(segs):
    out, o = {}, 0
    for name, w in segs:
        out[name] = (o, w)
        o += w
    return out


def _small_reduce(parts, lb_logits):
    xo, so = _offsets(X_SEGS), _offsets(SMALL_SEGS)
    names = [nm for nm, _ in X_SEGS]

    def body(*refs):
        p_refs = dict(zip(names, refs[:len(names)]))
        lbl_ref, allx, gs_ref, loss_ref, send_sems, recv_sems = refs[len(names):]
        me = _index(_mesh_pos())
        for nm, (o, w) in xo.items():
            if nm == "loss":
                allx[me, :, o:o + w] = jnp.broadcast_to(jnp.sum(p_refs[nm][...]), (1, w))
            else:
                allx[me, :, o:o + w] = jnp.sum(p_refs[nm][...], axis=0, keepdims=True)
        _exchange_slots(allx, send_sems, recv_sems)
        tot = allx[0]
        for d in range(1, N_DEV):
            tot = tot + allx[d]
        seg = lambda nm: tot[:, xo[nm][0]:xo[nm][0] + xo[nm][1]]

        def put(nm, v):
            gs_ref[:, so[nm][0]:so[nm][0] + so[nm][1]] = v

        put("b_ada", tot[:, 0:N_MOD * D])
        put("norm1_gain", seg("g1"))
        put("norm2_gain", seg("g2"))
        lbl = lbl_ref[...]
        lb = _sigmoid(lbl[0:1, :] - lbl[1:2, :])
        dl0 = seg("lb") * lb * (1.0 - lb)
        put("lb0", dl0)
        put("lb1", -dl0)
        put("hgrn_o_gain", seg("og"))
        put("q_norm_gain", seg("qg"))
        put("k_norm_gain", seg("kg"))
        put("sinks", seg("sk"))
        loss_ref[...] = seg("loss")

    return pl.pallas_call(
        body, in_specs=[_VMEM] * (len(names) + 1), out_specs=[_VMEM, _VMEM, _VMEM],
        out_shape=[jax.ShapeDtypeStruct((N_DEV, 1, X_W), F32), jax.ShapeDtypeStruct((1, SMALL_W), F32),
                   jax.ShapeDtypeStruct((1, 128), F32)],
        scratch_shapes=[_SEMS(N_DEV - 1), _SEMS(N_DEV - 1)], name="small_reduce",
        compiler_params=pltpu.CompilerParams(vmem_limit_bytes=VMEM_LIMIT),
    )(*[parts[nm] for nm in names], lb_logits)


def _adamw_math(w, g, m, v):
    m = B1 * m + (1.0 - B1) * g
    v = B2 * v + (1.0 - B2) * (g * g)
    m_hat = m / (1.0 - B1 ** STEP)
    v_hat = v / (1.0 - B2 ** STEP)
    return -LR * (m_hat / (jnp.sqrt(v_hat) + ADAM_EPS) + WD * w), m, v


def _sum_rows(rs):
    return 256 if rs % 256 == 0 else rs // 2


def _pair_sum(name, g, recv, c_arr, tr):
    _, rs, cols = recv.shape
    blk = (1, tr, cols)

    def body(c_ref, g_ref, r_ref, o_ref):
        o_ref[...] = (g_ref[...].astype(F32) + r_ref[...].astype(F32)).astype(BF16)

    grid_spec = pltpu.PrefetchScalarGridSpec(
        num_scalar_prefetch=1, grid=(4, rs // tr),
        in_specs=[pl.BlockSpec(blk, lambda q, i, c: (2 * q + c[0], i, 0)), pl.BlockSpec(blk, lambda q, i, c: (q, i, 0))],
        out_specs=pl.BlockSpec(blk, lambda q, i, c: (q, i, 0)))
    return pl.pallas_call(body, grid_spec=grid_spec, out_shape=jax.ShapeDtypeStruct((4, rs, cols), BF16), name=name,
                          compiler_params=_params(("parallel", "parallel")))(c_arr, g.reshape(N_DEV, rs, cols), recv)


def _final_sum(name, sums, recv, q_arr, tr):
    _, rs, cols = sums.shape

    def body(q_ref, s_ref, r_ref, o_ref):
        o_ref[...] = ((s_ref[0].astype(F32) + r_ref[0].astype(F32)) + r_ref[1].astype(F32)) + r_ref[2].astype(F32)

    grid_spec = pltpu.PrefetchScalarGridSpec(
        num_scalar_prefetch=1, grid=(rs // tr,),
        in_specs=[pl.BlockSpec((1, tr, cols), lambda i, q: (q[0], i, 0)), pl.BlockSpec((3, tr, cols), lambda i, q: (0, i, 0))],
        out_specs=pl.BlockSpec((tr, cols), lambda i, q: (i, 0)))
    return pl.pallas_call(body, grid_spec=grid_spec, out_shape=jax.ShapeDtypeStruct((rs, cols), F32), name=name,
                          compiler_params=_params(("parallel",)))(q_arr, sums, recv)


def _adamw(name, w, g, m, v, tr):
    rows, cols = w.shape

    def body(w_ref, g_ref, m_ref, v_ref, d_ref, nm_ref, nv_ref):
        d_ref[...], nm_ref[...], nv_ref[...] = _adamw_math(w_ref[...], g_ref[...], m_ref[...], v_ref[...])

    spec = pl.BlockSpec((tr, cols), lambda i: (i, 0))
    shape = jax.ShapeDtypeStruct((rows, cols), F32)
    return pl.pallas_call(
        body, grid=(rows // tr,), in_specs=[spec] * 4, out_specs=[spec] * 3, out_shape=[shape] * 3, name=name,
        compiler_params=_params(("parallel",)),
    )(w, g, m, v)


def _ada_update(sc_t, dmod_cols, w, m, v, tr):
    rows, cols = w.shape

    def body(s_ref, d_ref, w_ref, m_ref, v_ref, g_ref, dl_ref, nm_ref, nv_ref):
        g = jnp.dot(s_ref[...], d_ref[...], precision=lax.Precision.HIGHEST, preferred_element_type=F32)
        g_ref[...] = g
        dl_ref[...], nm_ref[...], nv_ref[...] = _adamw_math(w_ref[...], g, m_ref[...], v_ref[...])

    spec = pl.BlockSpec((tr, cols), lambda i: (i, 0))
    shape = jax.ShapeDtypeStruct((rows, cols), F32)
    return pl.pallas_call(
        body, grid=(rows // tr,),
        in_specs=[pl.BlockSpec((tr, N_DEV), lambda i: (i, 0)), pl.BlockSpec((N_DEV, cols), lambda i: (0, 0)), spec, spec, spec],
        out_specs=[spec] * 4, out_shape=[shape] * 4, name="ada_update", compiler_params=_params(("parallel",)),
    )(sc_t, dmod_cols, w, m, v)


BIG = ("w_in", "w_branch_a", "w_branch_b", "w_out", "w_mlp_in", "w_mlp_out")
COLUMN_SHARDED = ("w_in", "w_branch_a", "w_branch_b", "w_mlp_in")
WEIGHTS = ("w_ada", "b_ada", "norm1_gain", "w_in", "lb_logits", "hgrn_o_gain", "q_norm_gain", "k_norm_gain", "sinks",
           "w_branch_a", "w_branch_b", "w_out", "norm2_gain", "w_mlp_in", "w_mlp_out")


def _pack_small(p):
    lb = p["lb_logits"]
    src = dict(p, lb0=lb[0:1], lb1=lb[1:2])
    return jnp.concatenate([jnp.pad(src[nm], ((0, 0), (0, w - src[nm].shape[1]))) for nm, w in SMALL_SEGS], axis=1)


def _unpack_small(vec, shapes):
    so = _offsets(SMALL_SEGS)
    out = {}
    for nm, shp in shapes.items():
        if nm == "lb_logits":
            o = so["lb0"][0]
            out[nm] = vec[0, o:o + 2 * AW].reshape(2, AW)
        else:
            o = so[nm][0]
            out[nm] = vec[:, o:o + shp[1]]
    return out


def kernel(x, c, w_ada, b_ada, norm1_gain, w_in, lb_logits, hgrn_o_gain, q_norm_gain, k_norm_gain, sinks, w_branch_a, w_branch_b, w_out, norm2_gain, w_mlp_in, w_mlp_out, loss_target, m_w_ada, m_b_ada, m_norm1_gain, m_w_in, m_lb_logits, m_hgrn_o_gain, m_q_norm_gain, m_k_norm_gain, m_sinks, m_w_branch_a, m_w_branch_b, m_w_out, m_norm2_gain, m_w_mlp_in, m_w_mlp_out, v_w_ada, v_b_ada, v_norm1_gain, v_w_in, v_lb_logits, v_hgrn_o_gain, v_q_norm_gain, v_k_norm_gain, v_sinks, v_w_branch_a, v_w_branch_b, v_w_out, v_norm2_gain, v_w_mlp_in, v_w_mlp_out):
    w = dict(w_ada=w_ada, b_ada=b_ada, norm1_gain=norm1_gain, w_in=w_in, lb_logits=lb_logits, hgrn_o_gain=hgrn_o_gain,
             q_norm_gain=q_norm_gain, k_norm_gain=k_norm_gain, sinks=sinks, w_branch_a=w_branch_a, w_branch_b=w_branch_b,
             w_out=w_out, norm2_gain=norm2_gain, w_mlp_in=w_mlp_in, w_mlp_out=w_mlp_out)
    m = dict(w_ada=m_w_ada, b_ada=m_b_ada, norm1_gain=m_norm1_gain, w_in=m_w_in, lb_logits=m_lb_logits,
             hgrn_o_gain=m_hgrn_o_gain, q_norm_gain=m_q_norm_gain, k_norm_gain=m_k_norm_gain, sinks=m_sinks,
             w_branch_a=m_w_branch_a, w_branch_b=m_w_branch_b, w_out=m_w_out, norm2_gain=m_norm2_gain,
             w_mlp_in=m_w_mlp_in, w_mlp_out=m_w_mlp_out)
    v = dict(w_ada=v_w_ada, b_ada=v_b_ada, norm1_gain=v_norm1_gain, w_in=v_w_in, lb_logits=v_lb_logits,
             hgrn_o_gain=v_hgrn_o_gain, q_norm_gain=v_q_norm_gain, k_norm_gain=v_k_norm_gain, sinks=v_sinks,
             w_branch_a=v_w_branch_a, w_branch_b=v_w_branch_b, w_out=v_w_out, norm2_gain=v_norm2_gain,
             w_mlp_in=v_w_mlp_in, w_mlp_out=v_w_mlp_out)
    for d in (w, m, v):
        for nm in ("w_ada",) + BIG:
            d[nm] = d[nm][0]
    px, py, pc = _mesh_pos()
    me = _index((px, py, pc))
    c_arr = jnp.reshape(pc, (1,)).astype(jnp.int32)
    q_arr = jnp.reshape(2 * px + py, (1,)).astype(jnp.int32)

    shards = [(w[nm].T if nm in COLUMN_SHARDED else w[nm]).astype(BF16) for nm in BIG]
    b_shard = lax.dynamic_slice(b_ada, (0, me * ADA_W), (1, ADA_W))
    mod, sc_all = _ada_mod(c, w["w_ada"], b_shard)

    dx, sums, parts = _local_step(x[0], loss_target[0], mod, norm1_gain, norm2_gain, lb_logits, hgrn_o_gain,
                                  q_norm_gain, k_norm_gain, sinks, shards, c_arr)

    allx, g_small, loss = _small_reduce(parts, lb_logits)

    grad, delta, new_m, new_v = {}, {}, {}, {}
    for nm in BIG:
        s, r2 = sums[nm]
        rs = s.shape[1]
        g = _final_sum("sum_" + nm, s, r2, q_arr, _sum_rows(rs))
        g = g.T if nm in COLUMN_SHARDED else g
        rows = g.shape[0]
        grad[nm] = g
        delta[nm], new_m[nm], new_v[nm] = _adamw("adamw_" + nm, w[nm], g, m[nm], v[nm], 128 if rows % 128 == 0 else rows)

    dmod_cols = lax.dynamic_slice(allx[:, 0, :], (0, me * ADA_W), (N_DEV, ADA_W))
    grad["w_ada"], delta["w_ada"], new_m["w_ada"], new_v["w_ada"] = _ada_update(
        sc_all.T, dmod_cols, w["w_ada"], m["w_ada"], v["w_ada"], 256)

    small_names = [nm for nm in WEIGHTS if nm not in BIG and nm != "w_ada"]
    shapes = {nm: w[nm].shape for nm in small_names}
    ds, ms, vs = _adamw("adamw_small", _pack_small(w), g_small, _pack_small(m), _pack_small(v), 1)
    for dst, vec in ((grad, g_small), (delta, ds), (new_m, ms), (new_v, vs)):
        dst.update(_unpack_small(vec, shapes))

    def full(d, nm):
        return d[nm][None] if nm in BIG or nm == "w_ada" else d[nm]

    return (loss[0, 0], dx[None], *[full(grad, nm) for nm in WEIGHTS], *[full(delta, nm) for nm in WEIGHTS],
            *[full(new_m, nm) for nm in WEIGHTS], *[full(new_v, nm) for nm in WEIGHTS])
```

```python
import functools

import jax
import jax.numpy as jnp
from jax import lax
from jax.experimental import pallas as pl
from jax.experimental.pallas import tpu as pltpu

F32 = jnp.float32
BF16 = jnp.bfloat16
MESH = pl.DeviceIdType.MESH

N_DEV = 8
D = 2048
A_HEADS, A_HD, CHUNK = 8, 128, 64
AW = A_HEADS * A_HD
Q_HEADS, KV_HEADS, GROUP, B_HD, BLK = 16, 4, 4, 64, 128
BW = Q_HEADS * B_HD
KVW = KV_HEADS * B_HD
HID = 4 * D
IN_W = 4 * AW + BW + 2 * KVW + 2 * D
OFF_QA, OFF_FA, OFF_IA, OFF_GA = 0, AW, 2 * AW, 3 * AW
OFF_QB = 4 * AW
OFF_KB = OFF_QB + BW
OFF_VB = OFF_KB + KVW
OFF_GTA = OFF_VB + KVW
OFF_GTB = OFF_GTA + D
N_MOD = 6
EPS = 1e-6
LR, B1, B2, ADAM_EPS, WD, STEP = 1e-3, 0.9, 0.999, 1e-8, 0.01, 10
NEG = -1e30

VMEM_LIMIT = 56 * 1024 * 1024

NN = (((1,), (0,)), ((), ()))
NT = (((1,), (1,)), ((), ()))
TN = (((0,), (0,)), ((), ()))
BNN = (((2,), (1,)), ((0,), (0,)))
BNT = (((2,), (2,)), ((0,), (0,)))
BTN = (((1,), (1,)), ((0,), (0,)))


def _dot(a, b, dims=NN):
    return lax.dot_general(a.astype(BF16), b.astype(BF16), dims, preferred_element_type=F32)


def _params(sem):
    return pltpu.CompilerParams(dimension_semantics=sem, vmem_limit_bytes=VMEM_LIMIT)


def _sigmoid(x):
    return 1.0 / (1.0 + jnp.exp(-x))


def _fold8(v):
    r, n = v.shape
    return jnp.sum(v.reshape(r // 8, 8, n), axis=0)


_VMEM = pl.BlockSpec(memory_space=pltpu.VMEM)
_ANY = pl.BlockSpec(memory_space=pl.ANY)
_SEMS = lambda n: pltpu.SemaphoreType.DMA((n,))


def _mesh_pos():
    return lax.axis_index("x"), lax.axis_index("y"), lax.axis_index("c")


def _flip(pos, k):
    return tuple(1 - p if (k >> s) & 1 else p for p, s in zip(pos, (2, 1, 0)))


def _index(pos):
    return 4 * pos[0] + 2 * pos[1] + pos[2]


class _Job:
    def __init__(self, ins, out_shape, sems, start, finish):
        self.ins, self.out_shape, self.sems, self.start, self.finish = list(ins), list(out_shape), list(sems), start, finish


def _pcall(body, *, grid, in_specs, out_specs, out_shape, scratch_shapes, name, semantics, args, job=None):
    if job is None:
        outs = pl.pallas_call(body, grid=grid, in_specs=in_specs, out_specs=out_specs, out_shape=out_shape,
                              scratch_shapes=scratch_shapes, name=name, compiler_params=_params(semantics))(*args)
        return list(outs), []
    n_in, n_out, n_scr = len(in_specs), len(out_specs), len(scratch_shapes)
    j_in, j_out = len(job.ins), len(job.out_shape)
    steps = tuple(grid)

    def carrier(*refs):
        o = 0
        main_in, o = refs[o:o + n_in], o + n_in
        job_in, o = refs[o:o + j_in], o + j_in
        main_out, o = refs[o:o + n_out], o + n_out
        job_out, o = refs[o:o + j_out], o + j_out
        main_scr, job_sems = refs[o:o + n_scr], refs[o + n_scr:]
        ids = [pl.program_id(a) for a in range(len(steps))]
        first = functools.reduce(lambda p, q: p & q, [i == 0 for i in ids])
        last = functools.reduce(lambda p, q: p & q, [i == s - 1 for i, s in zip(ids, steps)])

        @pl.when(first)
        def _():
            job.start(job_in, job_out, job_sems)

        body(*main_in, *main_out, *main_scr)

        @pl.when(last)
        def _():
            job.finish(job_in, job_out, job_sems)

    outs = pl.pallas_call(
        carrier, grid=grid, in_specs=list(in_specs) + [_ANY] * j_in, out_specs=list(out_specs) + [_ANY] * j_out,
        out_shape=list(out_shape) + job.out_shape, scratch_shapes=list(scratch_shapes) + job.sems, name=name,
        compiler_params=_params(("arbitrary",) * len(steps)),
    )(*args, *job.ins)
    return list(outs[:n_out]), list(outs[n_out:])


def _run_job(name, job):
    j_in, j_out = len(job.ins), len(job.out_shape)

    def body(*refs):
        ins, outs, sems = refs[:j_in], refs[j_in:j_in + j_out], refs[j_in + j_out:]
        job.start(ins, outs, sems)
        job.finish(ins, outs, sems)

    return list(pl.pallas_call(body, in_specs=[_ANY] * j_in, out_specs=[_ANY] * j_out, out_shape=job.out_shape,
                               scratch_shapes=job.sems, name=name)(*job.ins))


def _gather_job(shards):
    n = len(shards)

    def copies(ins, outs, sems):
        send_sems, recv_sems, local_sems = sems
        x, y, c = _mesh_pos()
        me, sib = (x, y, c), (x, y, 1 - c)
        chips = [(1 - x, y), (x, 1 - y), (1 - x, 1 - y)]

        def rows(a, p):
            rs = shards[a].shape[0]
            return outs[a].at[pl.ds(_index(p) * rs, rs), :]

        def copy(a, k, block, to, src=None):
            return pltpu.make_async_remote_copy(
                src_ref=rows(a, block) if src is None else src, dst_ref=rows(a, block),
                send_sem=send_sems.at[7 * a + k], recv_sem=recv_sems.at[7 * a + k], device_id=to, device_id_type=MESH)

        mine = [pltpu.make_async_copy(ins[a], rows(a, me), local_sems.at[a]) for a in range(n)]
        first = []
        for a in range(n):
            first.append(copy(a, 0, me, sib, src=ins[a]))
            first += [copy(a, 1 + j, me, (*chip, c), src=ins[a]) for j, chip in enumerate(chips)]
        return me, sib, c, chips, copy, mine, first

    def start(ins, outs, sems):
        *_, mine, first = copies(ins, outs, sems)
        for cp in mine + first:
            cp.start()

    def finish(ins, outs, sems):
        me, sib, c, chips, copy, mine, first = copies(ins, outs, sems)
        passed = []
        for j, chip in enumerate(chips):
            for a in range(n):
                copy(a, 1 + j, (*chip, c), me).wait_recv()
                cp = copy(a, 4 + j, (*chip, c), sib)
                cp.start()
                passed.append(cp)
        for a in range(n):
            copy(a, 0, sib, me).wait_recv()
            for j, chip in enumerate(chips):
                copy(a, 4 + j, (*chip, 1 - c), me).wait_recv()
        for cp in first + passed:
            cp.wait_send()
        for cp in mine:
            cp.wait()

    return _Job(shards, [jax.ShapeDtypeStruct((N_DEV * s.shape[0], s.shape[1]), s.dtype) for s in shards],
                [_SEMS(7 * n), _SEMS(7 * n), _SEMS(n)], start, finish)


def _pair_job(grads):
    n = len(grads)

    def copies(ins, outs, sems):
        send_sems, recv_sems = sems
        x, y, c = _mesh_pos()
        out = []
        for a in range(n):
            rs = grads[a].shape[0] // N_DEV
            for q in range(4):
                blk = ins[a].at[pl.ds((2 * q + 1 - c) * rs, rs), :]
                out.append(pltpu.make_async_remote_copy(
                    src_ref=blk, dst_ref=outs[a].at[q], send_sem=send_sems.at[4 * a + q], recv_sem=recv_sems.at[4 * a + q],
                    device_id=(x, y, 1 - c), device_id_type=MESH))
        return out

    def start(ins, outs, sems):
        for cp in copies(ins, outs, sems):
            cp.start()

    def finish(ins, outs, sems):
        for cp in copies(ins, outs, sems):
            cp.wait()

    return _Job(grads, [jax.ShapeDtypeStruct((4, g.shape[0] // N_DEV, g.shape[1]), g.dtype) for g in grads],
                [_SEMS(4 * n), _SEMS(4 * n)], start, finish)


def _chip_job(sums):
    n = len(sums)

    def copies(ins, outs, sems):
        send_sems, recv_sems = sems
        x, y, c = _mesh_pos()
        out = []
        for a in range(n):
            for r in (1, 2, 3):
                px, py = (1 - x if r & 2 else x), (1 - y if r & 1 else y)
                out.append(pltpu.make_async_remote_copy(
                    src_ref=ins[a].at[2 * px + py], dst_ref=outs[a].at[r - 1], send_sem=send_sems.at[3 * a + r - 1],
                    recv_sem=recv_sems.at[3 * a + r - 1], device_id=(px, py, c), device_id_type=MESH))
        return out

    def start(ins, outs, sems):
        for cp in copies(ins, outs, sems):
            cp.start()

    def finish(ins, outs, sems):
        for cp in copies(ins, outs, sems):
            cp.wait()

    return _Job(sums, [jax.ShapeDtypeStruct((3,) + s.shape[1:], s.dtype) for s in sums],
                [_SEMS(3 * n), _SEMS(3 * n)], start, finish)


def _mm(name, form, a_list, b, M, N, K, tm, tn, tk, extras, outs, epi, job=None, acc_as_ref=False):
    nI, nJ, nK = M // tm, N // tn, K // tk
    assert nI * tm == M and nJ * tn == N and nK * tk == K
    dims = {"nn": NN, "nt": NT, "tn": TN}[form]
    b_list = b if isinstance(b, list) else [(b, {"nn": N, "nt": K, "tn": N}[form])]
    nA, nB = len(a_list), len(b_list)
    assert nA == 1 or nB == 1
    assert nB == 1 or form in ("nn", "nt")
    AXIS = {"i": 0, "j": 1, "k": 2}
    a_axis, a_tile = ("i", tm) if form == "tn" else ("k", tk)
    b_axis, b_tile = ("k", tk) if form == "nt" else ("j", tn)

    def cut(pieces, tile, total):
        starts, s = [], 0
        for _, w in pieces:
            assert w % tile == 0
            starts.append(s // tile)
            s += w
        assert s == total
        return starts, [w // tile for _, w in pieces]

    a_st, a_cn = cut(a_list, a_tile, M if form == "tn" else K)
    b_st, b_cn = cut(b_list, b_tile, K if form == "nt" else N)

    def inside(idx, st, cn):
        return (idx >= st) & (idx < st + cn)

    def a_spec(p):
        st, cn = a_st[p], a_cn[p]
        if form == "tn":
            return pl.BlockSpec((tk, tm), lambda i, j, k: (jnp.where(inside(i, st, cn), k, 0), jnp.clip(i - st, 0, cn - 1)))
        return pl.BlockSpec((tm, tk), lambda i, j, k: (i, jnp.clip(k - st, 0, cn - 1)))

    def b_spec(p):
        st, cn = b_st[p], b_cn[p]
        if form == "nt":
            return pl.BlockSpec((tn, tk), lambda i, j, k: (j, jnp.clip(k - st, 0, cn - 1)))
        if nB == 1:
            return pl.BlockSpec((tk, tn), lambda i, j, k: (k, j))
        return pl.BlockSpec((tk, tn), lambda i, j, k: (jnp.where(inside(j, st, cn), k, 0), jnp.clip(j - st, 0, cn - 1)))

    in_specs = ([a_spec(p) for p in range(nA)] + [b_spec(p) for p in range(nB)]
                + [pl.BlockSpec(bs, im) for _, bs, im in extras])
    out_shape = [jax.ShapeDtypeStruct(s_, d_) for s_, d_, _, _ in outs]
    out_specs = [pl.BlockSpec(bs, im) for _, _, bs, im in outs]
    nE, nO = len(extras), len(outs)
    single = nA == 1 and nB == 1

    def body(*refs):
        a_refs, b_refs = refs[:nA], refs[nA:nA + nB]
        ex, ou = refs[nA + nB:nA + nB + nE], refs[nA + nB + nE:nA + nB + nE + nO]
        ids = [pl.program_id(a) for a in range(3)]

        def partial_of(p, q):
            return lax.dot_general(a_refs[p][...], b_refs[q][...], dims, preferred_element_type=F32)

        if nK == 1 and single:
            epi(partial_of(0, 0), ex, ou)
            return
        acc = refs[-1]
        k = ids[2]
        for p in range(nA):
            for q in range(nB):
                def first(p=p, q=q):
                    acc[...] = partial_of(p, q)

                def later(p=p, q=q):
                    acc[...] += partial_of(p, q)

                here = None
                if nA > 1:
                    here = inside(ids[AXIS[a_axis]], a_st[p], a_cn[p])
                if nB > 1:
                    here = inside(ids[AXIS[b_axis]], b_st[q], b_cn[q])
                pl.when(k == 0 if here is None else here & (k == 0))(first)
                pl.when(k > 0 if here is None else here & (k > 0))(later)

        @pl.when(k == nK - 1)
        def _():
            epi(acc if acc_as_ref else acc[...], ex, ou)

    scratch = [] if (nK == 1 and single) else [pltpu.VMEM((tm, tn), F32)]
    res, job_res = _pcall(
        body, grid=(nI, nJ, nK), in_specs=in_specs, out_specs=out_specs, out_shape=out_shape, scratch_shapes=scratch,
        name=name, semantics=("parallel", "parallel", "arbitrary"),
        args=[a for a, _ in a_list] + [p for p, _ in b_list] + [e for e, _, _ in extras], job=job)
    return res if job is None else (res, job_res)


def _rms_mod_fwd(name, x, gain, sc, sh, tr):
    T = x.shape[0]

    def body(x_ref, g_ref, sc_ref, sh_ref, h_ref):
        xv = x_ref[...]
        rstd = lax.rsqrt(jnp.mean(xv * xv, axis=-1, keepdims=True) + EPS)
        h_ref[...] = ((xv * rstd * g_ref[...]) * (1.0 + sc_ref[...]) + sh_ref[...]).astype(BF16)

    row = pl.BlockSpec((tr, D), lambda i: (i, 0))
    vec = pl.BlockSpec((1, D), lambda i: (0, 0))
    return pl.pallas_call(
        body, grid=(T // tr,), in_specs=[row, vec, vec, vec], out_specs=row,
        out_shape=jax.ShapeDtypeStruct((T, D), BF16), name=name, compiler_params=_params(("parallel",)),
    )(x, gain, sc, sh)


def _rms_mod_bwd_epilogue(x, gain, sc, dres, tm, gate=None, mo=None):
    T = x.shape[0]
    with_gate = gate is not None
    row = ((tm, D), lambda i, j, k: (i, 0))
    vec = ((1, D), lambda i, j, k: (0, 0))
    part = ((T // tm * 8, D), F32, (8, D), lambda i, j, k: (i, 0))
    extras = [(x, *row), (gain, *vec), (sc, *vec), (dres, *row)]
    outs = [((T, D), F32, *row), part, part, part]
    if with_gate:
        extras += [(gate, *vec), (mo, *row)]
        outs += [((T, D), BF16, *row), part]

    rows = min(128, tm)

    def epi(acc, ex, ou):
        g = ex[1][...]
        sums = [jnp.zeros((8, D), F32) for _ in range(4)]
        for r0 in range(0, tm, rows):
            rs = slice(r0, r0 + rows)
            dhv, xv = acc[rs, :], ex[0][rs, :]
            rstd = lax.rsqrt(jnp.mean(xv * xv, axis=-1, keepdims=True) + EPS)
            xhat = xv * rstd
            dn = dhv * (1.0 + ex[2][...])
            dxhat = dn * g
            dx = ex[3][rs, :] + rstd * (dxhat - xhat * jnp.mean(dxhat * xhat, axis=-1, keepdims=True))
            ou[0][rs, :] = dx
            terms = [dhv, dhv * (xhat * g), dn * xhat]
            if with_gate:
                ou[4][rs, :] = (ex[4][...] * dx).astype(BF16)
                terms.append(dx * ex[5][rs, :].astype(F32))
            sums = [s + _fold8(t) for s, t in zip(sums, terms)] + sums[len(terms):]
        ou[1][...], ou[2][...], ou[3][...] = sums[:3]
        if with_gate:
            ou[5][...] = sums[3]

    return extras, outs, epi


def _rms_mod_bwd(name, dh, x, gain, sc, dres, tr, gate=None, mo=None):
    T = x.shape[0]
    extras, outs, epi = _rms_mod_bwd_epilogue(x, gain, sc, dres, tr, gate, mo)
    rows_only = lambda im: (lambda i: im(i, 0, 0))
    nE = len(extras)

    def body(dh_ref, *refs):
        epi(dh_ref, refs[:nE], refs[nE:])

    return pl.pallas_call(
        body, grid=(T // tr,),
        in_specs=[pl.BlockSpec((tr, D), lambda i: (i, 0))] + [pl.BlockSpec(bs, rows_only(im)) for _, bs, im in extras],
        out_specs=[pl.BlockSpec(bs, rows_only(im)) for _, _, bs, im in outs],
        out_shape=[jax.ShapeDtypeStruct(s, d) for s, d, _, _ in outs], name=name, compiler_params=_params(("parallel",)),
    )(dh, *[e for e, _, _ in extras])


def _split3(v):
    h = v.astype(BF16)
    r1 = v - h.astype(F32)
    m = r1.astype(BF16)
    lo = (r1 - m.astype(F32)).astype(BF16)
    return h, m, lo


def _tri_mm(tri, v, dims=NN):
    h, m, lo = _split3(v)
    t = tri.astype(BF16)
    mm = lambda p: lax.dot_general(t, p, dims, preferred_element_type=F32)
    return (mm(lo) + mm(m)) + mm(h)


def _hgrn_chunk_terms(q, fl, lb):
    sig = _sigmoid(fl)
    f = lb + (1.0 - lb) * sig
    lf = jnp.log(f)
    kk = 1.0 - f
    sq = _sigmoid(q)
    qf = q * sq
    return sig, f, lf, kk, sq, qf


def _causal(n):
    r = lax.broadcasted_iota(jnp.int32, (n, n), 0)
    c = lax.broadcasted_iota(jnp.int32, (n, n), 1)
    return r >= c


def _hgrn_fwd(proj, lb_logits, o_gain, tt, job=None):
    T = proj.shape[0]
    nT, ncl = T // tt, tt // CHUNK
    C = CHUNK

    def body(q_ref, f_ref, i_ref, g_ref, lbl_ref, og_ref, y_ref, st_ref, S):
        @pl.when(pl.program_id(1) == 0)
        def _():
            S[...] = jnp.zeros_like(S)

        lbl = lbl_ref[...]
        lb = _sigmoid(lbl[0:1, :] - lbl[1:2, :])
        og = og_ref[...]
        shp = (ncl, C, A_HD)
        q, fl, v, g = (r[...].reshape(shp) for r in (q_ref, f_ref, i_ref, g_ref))
        tri = jnp.broadcast_to(_causal(C), (ncl, C, C))
        _, _, lf, kk, _, qf = _hgrn_chunk_terms(q, fl, lb)
        b = _tri_mm(tri, lf, BNN)
        bm, bl = b[:, C // 2 - 1:C // 2, :], b[:, C - 1:C, :]
        qd, kd = qf * jnp.exp(b - bm), kk * jnp.exp(bm - b)
        A = jnp.where(tri, _dot(qd, kd, BNT), 0.0)
        d_st = _dot(v, kk * jnp.exp(bl - b), BTN)
        dec = jnp.exp(bl)
        st = S[...]
        for ci in range(ncl):
            st_ref[0, ci] = st
            st = st * dec[ci] + d_st[ci]
        S[...] = st
        o = _dot(A, v, BNN) + _dot(qf * jnp.exp(b), st_ref[0], BNT)
        r = lax.rsqrt(jnp.mean(o * o, axis=-1, keepdims=True) + EPS)
        y_ref[...] = (o * r * og * (g * _sigmoid(g))).astype(BF16).reshape(tt, A_HD)

    def col(off):
        return pl.BlockSpec((tt, A_HD), lambda h, t: (t, off // A_HD + h))

    head_vec = lambda rows: pl.BlockSpec((rows, A_HD), lambda h, t: (0, h))
    return _pcall(
        body, grid=(A_HEADS, nT),
        in_specs=[col(OFF_QA), col(OFF_FA), col(OFF_IA), col(OFF_GA), head_vec(2), head_vec(1)],
        out_specs=[pl.BlockSpec((tt, A_HD), lambda h, t: (t, h)),
                   pl.BlockSpec((1, ncl, A_HD, A_HD), lambda h, t: (h, t, 0, 0))],
        out_shape=[jax.ShapeDtypeStruct((T, AW), BF16),
                   jax.ShapeDtypeStruct((A_HEADS, T // C, A_HD, A_HD), F32)],
        scratch_shapes=[pltpu.VMEM((A_HD, A_HD), F32)], name="hgrn_fwd", semantics=("parallel", "arbitrary"),
        args=[proj, proj, proj, proj, lb_logits, o_gain], job=job)


def _hgrn_bwd(proj, st, dy, lb_logits, o_gain, tt, job=None):
    T = proj.shape[0]
    nT, ncl = T // tt, tt // CHUNK
    C = CHUNK

    def body(q_ref, f_ref, i_ref, g_ref, st_ref, dy_ref, lbl_ref, og_ref,
             dq_ref, df_ref, di_ref, dg_ref, plb_ref, pog_ref, dS):
        @pl.when(pl.program_id(1) == 0)
        def _():
            dS[...] = jnp.zeros_like(dS)

        lbl = lbl_ref[...]
        lb = _sigmoid(lbl[0:1, :] - lbl[1:2, :])
        og = og_ref[...]
        shp = (ncl, C, A_HD)
        flat = lambda t: t.reshape(tt, A_HD)
        q, fl, v, g, dout = (r[...].reshape(shp) for r in (q_ref, f_ref, i_ref, g_ref, dy_ref))
        tri = jnp.broadcast_to(_causal(C), (ncl, C, C))
        rowi = lax.broadcasted_iota(jnp.int32, shp, 1)
        st0 = st_ref[0]
        sig, f, lf, kk, sq, qf = _hgrn_chunk_terms(q, fl, lb)
        b = _tri_mm(tri, lf, BNN)
        bm, bl = b[:, C // 2 - 1:C // 2, :], b[:, C - 1:C, :]
        e_qd, e_kd, e_ke, e_b = jnp.exp(b - bm), jnp.exp(bm - b), jnp.exp(bl - b), jnp.exp(b)
        qd, kd, ke, qe = qf * e_qd, kk * e_kd, kk * e_ke, qf * e_b
        dec = jnp.exp(bl)
        A = jnp.where(tri, _dot(qd, kd, BNT), 0.0)
        o = _dot(A, v, BNN) + _dot(qe, st0, BNT)
        r = lax.rsqrt(jnp.mean(o * o, axis=-1, keepdims=True) + EPS)
        sg = _sigmoid(g)
        on = o * r * og
        dg_ref[...] = flat((dout * on * (sg * (1.0 + g * (1.0 - sg)))).astype(BF16))
        don = dout * (g * sg)
        pog_ref[...] = _fold8(flat(don * o * r))
        dyh = don * og
        do = r * (dyh - o * (r * r) * jnp.mean(dyh * o, axis=-1, keepdims=True))
        g_st = _dot(do, qe, BTN)
        run = dS[...]
        after = [None] * ncl
        for ci in reversed(range(ncl)):
            after[ci] = run
            run = g_st[ci] + run * dec[ci]
        dS[...] = run
        d_after = jnp.stack(after, axis=0)
        ddec = jnp.sum(d_after * st0, axis=1, keepdims=True)
        dqe = _dot(do, st0, BNN)
        dke = _dot(v, d_after, BNN)
        dA = jnp.where(tri, _dot(do, v, BNT), 0.0)
        dv = _dot(ke, d_after, BNT) + _dot(A, do, BTN)
        dqd = _dot(dA, kd, BNN)
        dkd = _dot(dA, qd, BTN)
        di_ref[...] = flat(dv.astype(BF16))
        dqf = dqe * e_b + dqd * e_qd
        dkk = dkd * e_kd + dke * e_ke
        t_qd, t_kd, t_ke = dqd * qd, dkd * kd, dke * ke
        db = dqe * qe + t_qd - t_kd - t_ke
        dbm = jnp.sum(t_kd - t_qd, axis=1, keepdims=True)
        dbl = jnp.sum(t_ke, axis=1, keepdims=True) + ddec * dec
        db = db + jnp.where(rowi == C // 2 - 1, dbm, 0.0) + jnp.where(rowi == C - 1, dbl, 0.0)
        dlf = _tri_mm(tri, db, BTN)
        dfv = dlf / f - dkk
        df_ref[...] = flat((dfv * (1.0 - lb) * sig * (1.0 - sig)).astype(BF16))
        plb_ref[...] = _fold8(flat(dfv * (1.0 - sig)))
        dq_ref[...] = flat((dqf * (sq * (1.0 + q * (1.0 - sq)))).astype(BF16))

    def col(off):
        return pl.BlockSpec((tt, A_HD), lambda h, t: (nT - 1 - t, off // A_HD + h))

    head_vec = lambda rows: pl.BlockSpec((rows, A_HD), lambda h, t: (0, h))
    o_spec = pl.BlockSpec((tt, A_HD), lambda h, t: (nT - 1 - t, h))
    p_spec = pl.BlockSpec((8, A_HD), lambda h, t: (t, h))
    o_shape = jax.ShapeDtypeStruct((T, AW), BF16)
    p_shape = jax.ShapeDtypeStruct((nT * 8, AW), F32)
    return _pcall(
        body, grid=(A_HEADS, nT),
        in_specs=[col(OFF_QA), col(OFF_FA), col(OFF_IA), col(OFF_GA),
                  pl.BlockSpec((1, ncl, A_HD, A_HD), lambda h, t: (h, nT - 1 - t, 0, 0)),
                  pl.BlockSpec((tt, A_HD), lambda h, t: (nT - 1 - t, h)), head_vec(2), head_vec(1)],
        out_specs=[o_spec, o_spec, o_spec, o_spec, p_spec, p_spec],
        out_shape=[o_shape, o_shape, o_shape, o_shape, p_shape, p_shape],
        scratch_shapes=[pltpu.VMEM((A_HD, A_HD), F32)], name="hgrn_bwd", semantics=("parallel", "arbitrary"),
        args=[proj, proj, proj, proj, st, dy, lb_logits, o_gain], job=job)


LANES = 128
Q_COLS = BW // LANES


def _low_half():
    return lax.broadcasted_iota(jnp.int32, (1, LANES), 1) < B_HD


def _half_sum(t, low):
    lo = jnp.sum(jnp.where(low, t, 0.0), axis=-1, keepdims=True)
    hi = jnp.sum(jnp.where(low, 0.0, t), axis=-1, keepdims=True)
    return jnp.where(low, lo, hi)


def _half_rms(t, low):
    r = lax.rsqrt(_half_sum(t * t, low) * (1.0 / B_HD) + EPS)
    return t * r, r


def _fold_halves(p, low):
    return jnp.where(low, p + pltpu.roll(p, B_HD, 1), 0.0)


def _stack_cols(x):
    return jnp.stack([x[:, c * LANES:(c + 1) * LANES] for c in range(Q_COLS)], axis=0).reshape(KV_HEADS, 2 * BLK, LANES)


def _col_of(t, c):
    return t[c // 2, (c % 2) * BLK:(c % 2 + 1) * BLK]


def _split_halves(col, s, low):
    own = jnp.where(low if s == 0 else jnp.logical_not(low), col, 0.0)
    other = pltpu.roll(own, B_HD, 1)
    return (own, other) if s == 0 else (other, own)


def _swa_keys(kp_ref, kc_ref, vp_ref, vc_ref, kg, low):
    k_lo, k_hi, v_lo, v_hi, hats = [], [], [], [], []
    for j in range(KVW // LANES):
        cs = slice(j * LANES, (j + 1) * LANES)
        k_hat, k_r = _half_rms(jnp.concatenate([kp_ref[:, cs], kc_ref[:, cs]], axis=0), low)
        vcol = jnp.concatenate([vp_ref[:, cs], vc_ref[:, cs]], axis=0)
        hats.append((k_hat, k_r))
        for s in range(2):
            for dst_lo, dst_hi, col in ((k_lo, k_hi, k_hat * kg), (v_lo, v_hi, vcol)):
                lo, hi = _split_halves(col, s, low)
                dst_lo.append(lo)
                dst_hi.append(hi)
    st = lambda parts: jnp.stack(parts, axis=0)
    return st(k_lo), st(k_hi), st(v_lo), st(v_hi), hats


def _swa_mask(first_block):
    qi = lax.broadcasted_iota(jnp.int32, (BLK, 2 * BLK), 0) + BLK
    ki = lax.broadcasted_iota(jnp.int32, (BLK, 2 * BLK), 1)
    rel = qi - ki
    m = (rel >= 0) & (rel < BLK) & (jnp.logical_not(first_block) | (ki >= BLK))
    return jnp.concatenate([m, m], axis=0)


def _sink_cols(sk_ref, hi):
    top = lax.broadcasted_iota(jnp.int32, (2 * BLK, 1), 0) < BLK
    return jnp.stack([jnp.where(top, sk_ref[0, GROUP * hk + hi], sk_ref[0, GROUP * hk + 2 + hi])
                      for hk in range(KV_HEADS)], axis=0)


def _swa_probs(qn, k_half, sink, mask):
    s = jnp.where(mask, _dot(qn, k_half, BNT) * (B_HD ** -0.5), NEG)
    m = jnp.maximum(jnp.max(s, axis=-1, keepdims=True), sink)
    p = jnp.exp(s - m)
    ps = jnp.exp(sink - m)
    inv = 1.0 / (jnp.sum(p, axis=-1, keepdims=True) + ps)
    return p * inv, ps * inv


def _swa_fwd(proj, q_gain, k_gain, sinks, job=None):
    T = proj.shape[0]
    nb = T // BLK

    def body(q_ref, kc_ref, kp_ref, vc_ref, vp_ref, qg_ref, kg_ref, sk_ref, o_ref):
        low = _low_half()
        mask = _swa_mask(pl.program_id(0) == 0)
        qn = _half_rms(_stack_cols(q_ref[...]), low)[0] * qg_ref[...]
        k_lo, k_hi, v_lo, v_hi, _ = _swa_keys(kp_ref, kc_ref, vp_ref, vc_ref, kg_ref[...], low)
        p_lo, _ = _swa_probs(qn, k_lo, _sink_cols(sk_ref, 0), mask)
        p_hi, _ = _swa_probs(qn, k_hi, _sink_cols(sk_ref, 1), mask)
        o = (_dot(p_lo, v_lo, BNN) + _dot(p_hi, v_hi, BNN)).astype(BF16)
        for c in range(Q_COLS):
            o_ref[:, c * LANES:(c + 1) * LANES] = _col_of(o, c)

    q_gain, k_gain = jnp.tile(q_gain, (1, 2)), jnp.tile(k_gain, (1, 2))
    cur = lambda w, off: pl.BlockSpec((BLK, w), lambda i: (i, off // w))
    prev = lambda w, off: pl.BlockSpec((BLK, w), lambda i: (jnp.maximum(i - 1, 0), off // w))
    small = lambda n: pl.BlockSpec((1, 2 * n), lambda i: (0, 0))
    return _pcall(
        body, grid=(nb,),
        in_specs=[cur(BW, OFF_QB), cur(KVW, OFF_KB), prev(KVW, OFF_KB), cur(KVW, OFF_VB), prev(KVW, OFF_VB),
                  small(B_HD), small(B_HD), pl.BlockSpec(memory_space=pltpu.SMEM)],
        out_specs=[pl.BlockSpec((BLK, BW), lambda i: (i, 0))],
        out_shape=[jax.ShapeDtypeStruct((T, BW), BF16)], scratch_shapes=[], name="swa_fwd", semantics=("parallel",),
        args=[proj, proj, proj, proj, proj, q_gain, k_gain, sinks], job=job)


def _swa_bwd(proj, dout, q_gain, k_gain, sinks, job=None):
    T = proj.shape[0]
    nb = T // BLK
    W = BW + 2 * KVW

    def body(q_ref, kc_ref, kp_ref, vc_ref, vp_ref, do_ref, qg_ref, kg_ref, sk_ref,
             dq_ref, dkv_ref, pqg_ref, pkg_ref, psk_ref, dkn_c, dv_c):
        i = pl.program_id(0)
        live = i < nb
        low = _low_half()
        high = jnp.logical_not(low)
        qg, kg = qg_ref[...], kg_ref[...]
        mask = _swa_mask(i == 0)
        lane = lax.broadcasted_iota(jnp.int32, (1, LANES), 1)
        scale = B_HD ** -0.5

        @pl.when(i == 0)
        def _():
            dkn_c[...] = jnp.zeros_like(dkn_c)
            dv_c[...] = jnp.zeros_like(dv_c)

        q_hat, q_r = _half_rms(_stack_cols(q_ref[...]), low)
        qn = q_hat * qg
        k_lo, k_hi, v_lo, v_hi, hats = _swa_keys(kp_ref, kc_ref, vp_ref, vc_ref, kg, low)
        do = _stack_cols(do_ref[...])
        dqn = jnp.zeros((KV_HEADS, 2 * BLK, LANES), F32)
        acc_sk = jnp.zeros((1, LANES), F32)
        dk_parts, dv_parts = [], []
        for hi, (k_h, v_h) in enumerate(((k_lo, v_lo), (k_hi, v_hi))):
            p, ps = _swa_probs(qn, k_h, _sink_cols(sk_ref, hi), mask)
            dp = _dot(do, v_h, BNT)
            delta = jnp.sum(p * dp, axis=-1, keepdims=True)
            ds = p * (dp - delta) * scale
            dqn = dqn + _dot(ds, k_h, BNN)
            dk_parts.append(_dot(ds, qn, BTN))
            dv_parts.append(_dot(p, do, BTN))
            t = ps * delta
            for hk in range(KV_HEADS):
                for rows in range(2):
                    h = GROUP * hk + 2 * rows + hi
                    acc_sk = acc_sk + jnp.where(
                        lane == h, -jnp.sum(t[hk, rows * BLK:(rows + 1) * BLK], axis=0, keepdims=True), 0.0)
        dqh = dqn * qg
        dq = (q_r * (dqh - q_hat * (_half_sum(dqh * q_hat, low) * (1.0 / B_HD)))).astype(BF16)
        for c in range(Q_COLS):
            dq_ref[:, c * LANES:(c + 1) * LANES] = _col_of(dq, c)
        acc_qg = _fold_halves(_fold8((dqn * q_hat).reshape(KV_HEADS * 2 * BLK, LANES)), low)

        def native(parts, j):
            lo_arr, hi_arr = parts
            a, b = 2 * j, 2 * j + 1
            return (jnp.where(low, lo_arr[a], 0.0) + pltpu.roll(jnp.where(high, hi_arr[a], 0.0), B_HD, 1)
                    + jnp.where(high, hi_arr[b], 0.0) + pltpu.roll(jnp.where(low, lo_arr[b], 0.0), B_HD, 1))

        acc_kg = jnp.zeros((8, LANES), F32)
        for j in range(KVW // LANES):
            cs = slice(j * LANES, (j + 1) * LANES)
            dkn = jnp.where(live, native(dk_parts, j), 0.0)
            dvc = jnp.where(live, native(dv_parts, j), 0.0)
            kp_hat, kp_r = hats[j][0][:BLK], hats[j][1][:BLK]
            dkn_prev = dkn_c[:, cs] + dkn[:BLK]
            dv_prev = dv_c[:, cs] + dvc[:BLK]
            acc_kg = acc_kg + _fold8(dkn_prev * kp_hat)
            dkh = dkn_prev * kg
            dkv_ref[:, cs] = (kp_r * (dkh - kp_hat * (_half_sum(dkh * kp_hat, low) * (1.0 / B_HD)))).astype(BF16)
            dkv_ref[:, KVW + j * LANES:KVW + (j + 1) * LANES] = dv_prev.astype(BF16)
            dkn_c[:, cs] = dkn[BLK:]
            dv_c[:, cs] = dvc[BLK:]
        keep = jnp.where(i > 0, 1.0, 0.0)
        pqg_ref[...] = jnp.where(live, acc_qg, 0.0)
        pkg_ref[...] = _fold_halves(acc_kg, low) * keep
        psk_ref[...] = jnp.broadcast_to(jnp.where(live, acc_sk, 0.0), (8, LANES)) * (
            lax.broadcasted_iota(jnp.int32, (8, LANES), 0) == 0).astype(F32)

    q_gain, k_gain = jnp.tile(q_gain, (1, 2)), jnp.tile(k_gain, (1, 2))
    last = nb - 1
    cur = lambda w, off: pl.BlockSpec((BLK, w), lambda i: (jnp.minimum(i, last), off // w))
    prev = lambda w, off: pl.BlockSpec((BLK, w), lambda i: (jnp.maximum(i - 1, 0), off // w))
    small = lambda n: pl.BlockSpec((1, 2 * n), lambda i: (0, 0))
    part = pl.BlockSpec((8, 128), lambda i: (i, 0))
    p_shape = jax.ShapeDtypeStruct(((nb + 1) * 8, 128), F32)
    return _pcall(
        body, grid=(nb + 1,),
        in_specs=[cur(BW, OFF_QB), cur(KVW, OFF_KB), prev(KVW, OFF_KB), cur(KVW, OFF_VB), prev(KVW, OFF_VB),
                  pl.BlockSpec((BLK, BW), lambda i: (jnp.minimum(i, last), 0)), small(B_HD), small(B_HD),
                  pl.BlockSpec(memory_space=pltpu.SMEM)],
        out_specs=[pl.BlockSpec((BLK, BW), lambda i: (i, 0)),
                   pl.BlockSpec((BLK, 2 * KVW), lambda i: (jnp.maximum(i - 1, 0), 0)), part, part, part],
        out_shape=[jax.ShapeDtypeStruct((T + BLK, BW), BF16), jax.ShapeDtypeStruct((T, 2 * KVW), BF16),
                   p_shape, p_shape, p_shape],
        scratch_shapes=[pltpu.VMEM((BLK, KVW), F32), pltpu.VMEM((BLK, KVW), F32)], name="swa_bwd",
        semantics=("arbitrary",), args=[proj, proj, proj, proj, proj, dout, q_gain, k_gain, sinks], job=job)


def _branch_merge(ya_pre, attn, wa_t, wb_t, proj, tm, tn):
    T = ya_pre.shape[0]

    def body(a_ref, b_ref, wa_ref, wb_ref, ga_ref, gb_ref, ya_ref, yb_ref, mg_ref):
        ya = lax.dot_general(a_ref[...], wa_ref[...], NT, preferred_element_type=F32)
        yb = lax.dot_general(b_ref[...], wb_ref[...], NT, preferred_element_type=F32)
        ya_ref[...] = ya.astype(BF16)
        yb_ref[...] = yb.astype(BF16)
        mg_ref[...] = (_sigmoid(ga_ref[...]) * ya + _sigmoid(gb_ref[...]) * yb).astype(BF16)

    o_spec = pl.BlockSpec((tm, tn), lambda i, j: (i, j))
    o_shape = jax.ShapeDtypeStruct((T, D), BF16)
    return pl.pallas_call(
        body, grid=(T // tm, D // tn),
        in_specs=[pl.BlockSpec((tm, AW), lambda i, j: (i, 0)), pl.BlockSpec((tm, BW), lambda i, j: (i, 0)),
                  pl.BlockSpec((tn, AW), lambda i, j: (j, 0)), pl.BlockSpec((tn, BW), lambda i, j: (j, 0)),
                  pl.BlockSpec((tm, tn), lambda i, j: (i, OFF_GTA // tn + j)),
                  pl.BlockSpec((tm, tn), lambda i, j: (i, OFF_GTB // tn + j))],
        out_specs=[o_spec, o_spec, o_spec], out_shape=[o_shape, o_shape, o_shape], name="branch_merge",
        compiler_params=_params(("parallel", "parallel")),
    )(ya_pre, attn, wa_t, wb_t, proj, proj)


def _ij(i, j, k):
    return (i, j)


def _local_step(x, tgt, mod, g1, g2, lbl, og, qg, kg, sk, shards, c_arr):
    win_s, wa_s, wb_s, wout_s, wmi_s, wmo_s = shards
    T = x.shape[0]
    tm, tr, tt = min(1024, T), min(256, T), min(512, T)
    tk_t = min(1024, T)
    tn = 512
    sh1, sc1, gt1, sh2, sc2, gt2 = (mod[:, i * D:(i + 1) * D] for i in range(N_MOD))
    nI = T // tm
    blk = (tm, tn)
    part = lambda: ((nI * 8, D), F32, (8, tn), _ij)
    vec_j = ((1, tn), lambda i, j, k: (0, j))

    h = _rms_mod_fwd("rms1_fwd", x, g1, sc1, sh1, tr)

    def epi_store(acc, ex, ou):
        ou[0][...] = acc.astype(ou[0].dtype)

    (win_t,) = _run_job("gather_w_in", _gather_job([win_s]))
    tm2 = min(2048, T)
    blk2 = (tm2, tn)
    half = D // 2
    halves = lambda s: (s[:, :half], s[:, half:])
    (wmi_sa, wmi_sb), (wmo_sa, wmo_sb) = halves(wmi_s), halves(wmo_s)
    (proj,), (wa_t, wb_t, w_out, wmi_ta) = _mm(
        "in_proj", "nt", [(h, D)], win_t, T, IN_W, D, tm2, tn, D, [], [((T, IN_W), F32, blk2, _ij)], epi_store,
        job=_gather_job([wa_s, wb_s, wout_s, wmi_sa]))
    (ya_pre, st), (wmi_tb,) = _hgrn_fwd(proj, lbl, og, tt, job=_gather_job([wmi_sb]))
    (attn,), (wmo_a,) = _swa_fwd(proj, qg, kg, sk, job=_gather_job([wmo_sa]))
    wmi_halves = [(wmi_ta, half), (wmi_tb, half)]
    ya, yb, merged = _branch_merge(ya_pre, attn, wa_t, wb_t, proj, tm, tn)

    def epi_res1(acc, ex, ou):
        x_ref, gt_ref = ex
        ou[0][...] = acc.astype(BF16)
        ou[1][...] = x_ref[...] + gt_ref[...] * acc

    mo, x1 = _mm("out_proj", "nn", [(merged, D)], w_out, T, D, D, tm, tn, D,
                 [(x, blk, _ij), (gt1, *vec_j)], [((T, D), BF16, blk, _ij), ((T, D), F32, blk, _ij)], epi_res1)
    h2 = _rms_mod_fwd("rms2_fwd", x1, g2, sc2, sh2, tr)

    def epi_relu2(acc, ex, ou):
        r = jnp.maximum(acc, 0.0)
        ou[0][...] = r.astype(BF16)
        ou[1][...] = (r * r).astype(BF16)

    (r, a), (wmo_b,) = _mm("mlp_in", "nt", [(h2, D)], wmi_halves, T, HID, D, tm2, tn, half, [],
                           [((T, HID), BF16, blk2, _ij), ((T, HID), BF16, blk2, _ij)], epi_relu2,
                           job=_gather_job([wmo_sb]))
    wmo_halves = [(wmo_a, half), (wmo_b, half)]

    def epi_loss(acc, ex, ou):
        x1_ref, t_ref, gt_ref = ex
        e = x1_ref[...] + gt_ref[...] * acc - t_ref[...]
        dy = e * (1.0 / D)
        ou[0][...] = dy
        ou[1][...] = (gt_ref[...] * dy).astype(BF16)
        ou[2][...] = _fold8(e * e) * (0.5 / D)
        ou[3][...] = _fold8(dy * acc)

    wide = (tm, 1024)
    part_w = ((nI * 8, D), F32, (8, 1024), _ij)
    dy, dz, p_loss, p_gt2 = _mm(
        "mlp_out", "nn", [(a, HID)], wmo_halves, T, D, HID, tm, half, 1024,
        [(x1, wide, _ij), (tgt, wide, _ij), (gt2, (1, 1024), lambda i, j, k: (0, j))],
        [((T, D), F32, wide, _ij), ((T, D), BF16, wide, _ij), part_w, part_w], epi_loss)

    def epi_du(acc, ex, ou):
        ou[0][...] = (acc * (2.0 * ex[0][...].astype(F32))).astype(BF16)

    (du,) = _mm("mlp_out_dx", "nt", [(dz, D)], wmo_halves, T, HID, D, tm2, tn, half, [(r, blk2, _ij)],
                [((T, HID), BF16, blk2, _ij)], epi_du)
    gblk = (1024, 1024)
    gwide = (1024, D)
    pair_sum = lambda nm, g, r1: _pair_sum("pair_sum_" + nm, g, r1, c_arr, _sum_rows(r1.shape[1]))
    (g_mo,) = _mm("mlp_out_dw", "tn", [(a, HID)], dz, HID, D, T, 1024, D, tk_t, [], [((HID, D), BF16, gwide, _ij)], epi_store)
    (dh2,), (r1_mo,) = _mm("mlp_in_dx", "nn", [(du, HID)], wmi_halves, T, D, HID, tm, half, 1024, [],
                           [((T, D), F32, (tm, half), _ij)], epi_store, job=_pair_job([g_mo]))
    dx1, p_sh2, p_sc2, p_g2, dmo, p_gt1 = _rms_mod_bwd("rms2_bwd", dh2, x1, g2, sc2, dy, tr, gate=gt1, mo=mo)
    s_mo = pair_sum("mlp_out", g_mo, r1_mo)
    tm_row = min(512, T)
    (g_mi,), (r2_mo,) = _mm("mlp_in_dw", "tn", [(du, HID)], h2, HID, D, T, 1024, D, tk_t, [],
                            [((HID, D), BF16, gwide, _ij)], epi_store, job=_chip_job([s_mo]))

    def epi_gates(acc, ex, ou):
        ya_ref, yb_ref, ga_ref, gb_ref = ex
        sa, sb = _sigmoid(ga_ref[...]), _sigmoid(gb_ref[...])
        ou[0][...] = (acc * sa).astype(BF16)
        ou[1][...] = (acc * sb).astype(BF16)
        ou[2][...] = (acc * ya_ref[...].astype(F32) * (sa * (1.0 - sa))).astype(BF16)
        ou[3][...] = (acc * yb_ref[...].astype(F32) * (sb * (1.0 - sb))).astype(BF16)

    o_bf = ((T, D), BF16, blk, _ij)
    (dya, dyb, dga, dgb), (r1_mi,) = _mm(
        "out_proj_dx", "nt", [(dmo, D)], w_out, T, D, D, tm, tn, D,
        [(ya, blk, _ij), (yb, blk, _ij), (proj, blk, lambda i, j, k: (i, OFF_GTA // tn + j)),
         (proj, blk, lambda i, j, k: (i, OFF_GTB // tn + j))], [o_bf, o_bf, o_bf, o_bf], epi_gates,
        job=_pair_job([g_mi]))
    s_mi = pair_sum("mlp_in", g_mi, r1_mi)
    (g_out,) = _mm("out_proj_dw", "tn", [(merged, D)], dmo, D, D, T, 1024, 1024, tk_t, [], [((D, D), BF16, gblk, _ij)], epi_store)
    (dya_pre,) = _mm("branch_a_dx", "nn", [(dya, D)], wa_t, T, AW, D, tm, tn, D, [], [((T, AW), F32, blk, _ij)], epi_store)
    (dattn,) = _mm("branch_b_dx", "nn", [(dyb, D)], wb_t, T, BW, D, tm, tn, D, [], [((T, BW), F32, blk, _ij)], epi_store)
    (g_a,) = _mm("branch_a_dw", "tn", [(dya, D)], ya_pre, D, AW, T, 1024, 1024, tk_t, [], [((D, AW), BF16, gblk, _ij)], epi_store)
    (g_b,) = _mm("branch_b_dw", "tn", [(dyb, D)], attn, D, BW, T, 1024, 1024, tk_t, [], [((D, BW), BF16, gblk, _ij)], epi_store)
    (dqa, dfa, dia, dgg, p_lb, p_og), (r2_mi,) = _hgrn_bwd(proj, st, dya_pre, lbl, og, tt, job=_chip_job([s_mi]))
    (dqb, dkv, p_qg, p_kg, p_sk), (r1_out, r1_a, r1_b) = _swa_bwd(proj, dattn, qg, kg, sk, job=_pair_job([g_out, g_a, g_b]))
    s_out, s_a, s_b = pair_sum("out", g_out, r1_out), pair_sum("branch_a", g_a, r1_a), pair_sum("branch_b", g_b, r1_b)
    pieces = [(dqa, AW), (dfa, AW), (dia, AW), (dgg, AW), (dqb, BW), (dkv, 2 * KVW), (dga, D), (dgb, D)]
    (g_in,), (r2_out, r2_a, r2_b) = _mm(
        "in_proj_dw", "tn", pieces, h, IN_W, D, T, 512, D, tk_t, [], [((IN_W, D), BF16, (512, D), _ij)], epi_store,
        job=_chip_job([s_out, s_a, s_b]))
    (r1_in,) = _run_job("pair_w_in", _pair_job([g_in]))
    s_in = pair_sum("in", g_in, r1_in)
    extras, outs, epi = _rms_mod_bwd_epilogue(x, g1, sc1, dx1, tm_row)
    (dx, p_sh1, p_sc1, p_g1), (r2_in,) = _mm(
        "in_proj_dx", "nn", pieces, win_t, T, D, IN_W, tm_row, D, 512, extras, outs, epi, job=_chip_job([s_in]),
        acc_as_ref=True)

    partials = dict(sh1=p_sh1, sc1=p_sc1, gt1=p_gt1, sh2=p_sh2, sc2=p_sc2, gt2=p_gt2, g1=p_g1, g2=p_g2,
                    lb=p_lb, og=p_og, qg=p_qg, kg=p_kg, sk=p_sk, loss=p_loss)
    sums = dict(w_in=(s_in, r2_in), w_branch_a=(s_a, r2_a), w_branch_b=(s_b, r2_b), w_out=(s_out, r2_out),
                w_mlp_in=(s_mi, r2_mi), w_mlp_out=(s_mo, r2_mo))
    return dx, sums, partials


def _exchange_slots(buf, send_sems, recv_sems):
    me = _mesh_pos()
    mine = buf.at[_index(me)]
    sends = []
    for k in range(1, N_DEV):
        cp = pltpu.make_async_remote_copy(src_ref=mine, dst_ref=mine, send_sem=send_sems.at[k - 1],
                                          recv_sem=recv_sems.at[k - 1], device_id=_flip(me, k), device_id_type=MESH)
        cp.start()
        sends.append(cp)
    for k in range(1, N_DEV):
        theirs = buf.at[_index(_flip(me, k))]
        pltpu.make_async_remote_copy(src_ref=theirs, dst_ref=theirs, send_sem=send_sems.at[k - 1],
                                     recv_sem=recv_sems.at[k - 1], device_id=_flip(me, k), device_id_type=MESH).wait_recv()
    for cp in sends:
        cp.wait_send()


ADA_W = N_MOD * D // N_DEV


def _ada_mod(c, w_ada, b_shard):
    def body(c_ref, w_ref, b_ref, mod_ref, sc_ref, cbuf, mbuf, s1, r1, s2, r2):
        me = _index(_mesh_pos())
        cbuf[me] = c_ref[...]
        _exchange_slots(cbuf, s1, r1)
        row = lax.broadcasted_iota(jnp.int32, (N_DEV, D), 0)
        call = jnp.zeros((N_DEV, D), F32)
        for d in range(N_DEV):
            call = jnp.where(row == d, cbuf[d], call)
        sc = call * _sigmoid(call)
        sc_ref[...] = sc
        mbuf[me] = _dot(sc, w_ref[...]) + b_ref[...]
        _exchange_slots(mbuf, s2, r2)
        for s in range(N_DEV):
            mod_ref[:, s * ADA_W:(s + 1) * ADA_W] = mbuf[s, pl.ds(me, 1), :]

    return pl.pallas_call(
        body, in_specs=[_VMEM, _VMEM, _VMEM], out_specs=[_VMEM, _VMEM],
        out_shape=[jax.ShapeDtypeStruct((1, N_MOD * D), F32), jax.ShapeDtypeStruct((N_DEV, D), F32)],
        scratch_shapes=[pltpu.VMEM((N_DEV, 1, D), F32), pltpu.VMEM((N_DEV, N_DEV, ADA_W), F32),
                        _SEMS(N_DEV - 1), _SEMS(N_DEV - 1), _SEMS(N_DEV - 1), _SEMS(N_DEV - 1)],
        name="ada_mod", compiler_params=pltpu.CompilerParams(vmem_limit_bytes=VMEM_LIMIT),
    )(c, w_ada, b_shard)


SMALL_SEGS = (("b_ada", N_MOD * D), ("norm1_gain", D), ("norm2_gain", D), ("lb0", AW), ("lb1", AW),
              ("hgrn_o_gain", AW), ("q_norm_gain", 128), ("k_norm_gain", 128), ("sinks", 128))
SMALL_W = sum(w for _, w in SMALL_SEGS)
X_SEGS = (("sh1", D), ("sc1", D), ("gt1", D), ("sh2", D), ("sc2", D), ("gt2", D), ("g1", D), ("g2", D),
          ("lb", AW), ("og", AW), ("qg", 128), ("kg", 128), ("sk", 128), ("loss", 128))
X_W = sum(w for _, w in X_SEGS)


def _offsets(segs):
    out, o = {}, 0
    for name, w in segs:
        out[name] = (o, w)
        o += w
    return out


def _small_reduce(parts, lb_logits):
    xo, so = _offsets(X_SEGS), _offsets(SMALL_SEGS)
    names = [nm for nm, _ in X_SEGS]

    def body(*refs):
        p_refs = dict(zip(names, refs[:len(names)]))
        lbl_ref, allx, gs_ref, loss_ref, send_sems, recv_sems = refs[len(names):]
        me = _index(_mesh_pos())
        for nm, (o, w) in xo.items():
            if nm == "loss":
                allx[me, :, o:o + w] = jnp.broadcast_to(jnp.sum(p_refs[nm][...]), (1, w))
            else:
                allx[me, :, o:o + w] = jnp.sum(p_refs[nm][...], axis=0, keepdims=True)
        _exchange_slots(allx, send_sems, recv_sems)
        tot = allx[0]
        for d in range(1, N_DEV):
            tot = tot + allx[d]
        seg = lambda nm: tot[:, xo[nm][0]:xo[nm][0] + xo[nm][1]]

        def put(nm, v):
            gs_ref[:, so[nm][0]:so[nm][0] + so[nm][1]] = v

        put("b_ada", tot[:, 0:N_MOD * D])
        put("norm1_gain", seg("g1"))
        put("norm2_gain", seg("g2"))
        lbl = lbl_ref[...]
        lb = _sigmoid(lbl[0:1, :] - lbl[1:2, :])
        dl0 = seg("lb") * lb * (1.0 - lb)
        put("lb0", dl0)
        put("lb1", -dl0)
        put("hgrn_o_gain", seg("og"))
        put("q_norm_gain", seg("qg"))
        put("k_norm_gain", seg("kg"))
        put("sinks", seg("sk"))
        loss_ref[...] = seg("loss")

    return pl.pallas_call(
        body, in_specs=[_VMEM] * (len(names) + 1), out_specs=[_VMEM, _VMEM, _VMEM],
        out_shape=[jax.ShapeDtypeStruct((N_DEV, 1, X_W), F32), jax.ShapeDtypeStruct((1, SMALL_W), F32),
                   jax.ShapeDtypeStruct((1, 128), F32)],
        scratch_shapes=[_SEMS(N_DEV - 1), _SEMS(N_DEV - 1)], name="small_reduce",
        compiler_params=pltpu.CompilerParams(vmem_limit_bytes=VMEM_LIMIT),
    )(*[parts[nm] for nm in names], lb_logits)


def _adamw_math(w, g, m, v):
    m = B1 * m + (1.0 - B1) * g
    v = B2 * v + (1.0 - B2) * (g * g)
    m_hat = m / (1.0 - B1 ** STEP)
    v_hat = v / (1.0 - B2 ** STEP)
    return -LR * (m_hat / (jnp.sqrt(v_hat) + ADAM_EPS) + WD * w), m, v


def _sum_rows(rs):
    return 256 if rs % 256 == 0 else rs // 2


def _pair_sum(name, g, recv, c_arr, tr):
    _, rs, cols = recv.shape
    blk = (1, tr, cols)

    def body(c_ref, g_ref, r_ref, o_ref):
        o_ref[...] = (g_ref[...].astype(F32) + r_ref[...].astype(F32)).astype(BF16)

    grid_spec = pltpu.PrefetchScalarGridSpec(
        num_scalar_prefetch=1, grid=(4, rs // tr),
        in_specs=[pl.BlockSpec(blk, lambda q, i, c: (2 * q + c[0], i, 0)), pl.BlockSpec(blk, lambda q, i, c: (q, i, 0))],
        out_specs=pl.BlockSpec(blk, lambda q, i, c: (q, i, 0)))
    return pl.pallas_call(body, grid_spec=grid_spec, out_shape=jax.ShapeDtypeStruct((4, rs, cols), BF16), name=name,
                          compiler_params=_params(("parallel", "parallel")))(c_arr, g.reshape(N_DEV, rs, cols), recv)


def _final_sum(name, sums, recv, q_arr, tr):
    _, rs, cols = sums.shape

    def body(q_ref, s_ref, r_ref, o_ref):
        o_ref[...] = ((s_ref[0].astype(F32) + r_ref[0].astype(F32)) + r_ref[1].astype(F32)) + r_ref[2].astype(F32)

    grid_spec = pltpu.PrefetchScalarGridSpec(
        num_scalar_prefetch=1, grid=(rs // tr,),
        in_specs=[pl.BlockSpec((1, tr, cols), lambda i, q: (q[0], i, 0)), pl.BlockSpec((3, tr, cols), lambda i, q: (0, i, 0))],
        out_specs=pl.BlockSpec((tr, cols), lambda i, q: (i, 0)))
    return pl.pallas_call(body, grid_spec=grid_spec, out_shape=jax.ShapeDtypeStruct((rs, cols), F32), name=name,
                          compiler_params=_params(("parallel",)))(q_arr, sums, recv)


def _adamw(name, w, g, m, v, tr):
    rows, cols = w.shape

    def body(w_ref, g_ref, m_ref, v_ref, d_ref, nm_ref, nv_ref):
        d_ref[...], nm_ref[...], nv_ref[...] = _adamw_math(w_ref[...], g_ref[...], m_ref[...], v_ref[...])

    spec = pl.BlockSpec((tr, cols), lambda i: (i, 0))
    shape = jax.ShapeDtypeStruct((rows, cols), F32)
    return pl.pallas_call(
        body, grid=(rows // tr,), in_specs=[spec] * 4, out_specs=[spec] * 3, out_shape=[shape] * 3, name=name,
        compiler_params=_params(("parallel",)),
    )(w, g, m, v)


def _ada_update(sc_t, dmod_cols, w, m, v, tr):
    rows, cols = w.shape

    def body(s_ref, d_ref, w_ref, m_ref, v_ref, g_ref, dl_ref, nm_ref, nv_ref):
        g = jnp.dot(s_ref[...], d_ref[...], precision=lax.Precision.HIGHEST, preferred_element_type=F32)
        g_ref[...] = g
        dl_ref[...], nm_ref[...], nv_ref[...] = _adamw_math(w_ref[...], g, m_ref[...], v_ref[...])

    spec = pl.BlockSpec((tr, cols), lambda i: (i, 0))
    shape = jax.ShapeDtypeStruct((rows, cols), F32)
    return pl.pallas_call(
        body, grid=(rows // tr,),
        in_specs=[pl.BlockSpec((tr, N_DEV), lambda i: (i, 0)), pl.BlockSpec((N_DEV, cols), lambda i: (0, 0)), spec, spec, spec],
        out_specs=[spec] * 4, out_shape=[shape] * 4, name="ada_update", compiler_params=_params(("parallel",)),
    )(sc_t, dmod_cols, w, m, v)


BIG = ("w_in", "w_branch_a", "w_branch_b", "w_out", "w_mlp_in", "w_mlp_out")
COLUMN_SHARDED = ("w_in", "w_branch_a", "w_branch_b", "w_mlp_in")
WEIGHTS = ("w_ada", "b_ada", "norm1_gain", "w_in", "lb_logits", "hgrn_o_gain", "q_norm_gain", "k_norm_gain", "sinks",
           "w_branch_a", "w_branch_b", "w_out", "norm2_gain", "w_mlp_in", "w_mlp_out")


def _pack_small(p):
    lb = p["lb_logits"]
    src = dict(p, lb0=lb[0:1], lb1=lb[1:2])
    return jnp.concatenate([jnp.pad(src[nm], ((0, 0), (0, w - src[nm].shape[1]))) for nm, w in SMALL_SEGS], axis=1)


def _unpack_small(vec, shapes):
    so = _offsets(SMALL_SEGS)
    out = {}
    for nm, shp in shapes.items():
        if nm == "lb_logits":
            o = so["lb0"][0]
            out[nm] = vec[0, o:o + 2 * AW].reshape(2, AW)
        else:
            o = so[nm][0]
            out[nm] = vec[:, o:o + shp[1]]
    return out


def kernel(x, c, w_ada, b_ada, norm1_gain, w_in, lb_logits, hgrn_o_gain, q_norm_gain, k_norm_gain, sinks, w_branch_a, w_branch_b, w_out, norm2_gain, w_mlp_in, w_mlp_out, loss_target, m_w_ada, m_b_ada, m_norm1_gain, m_w_in, m_lb_logits, m_hgrn_o_gain, m_q_norm_gain, m_k_norm_gain, m_sinks, m_w_branch_a, m_w_branch_b, m_w_out, m_norm2_gain, m_w_mlp_in, m_w_mlp_out, v_w_ada, v_b_ada, v_norm1_gain, v_w_in, v_lb_logits, v_hgrn_o_gain, v_q_norm_gain, v_k_norm_gain, v_sinks, v_w_branch_a, v_w_branch_b, v_w_out, v_norm2_gain, v_w_mlp_in, v_w_mlp_out):
    w = dict(w_ada=w_ada, b_ada=b_ada, norm1_gain=norm1_gain, w_in=w_in, lb_logits=lb_logits, hgrn_o_gain=hgrn_o_gain,
             q_norm_gain=q_norm_gain, k_norm_gain=k_norm_gain, sinks=sinks, w_branch_a=w_branch_a, w_branch_b=w_branch_b,
             w_out=w_out, norm2_gain=norm2_gain, w_mlp_in=w_mlp_in, w_mlp_out=w_mlp_out)
    m = dict(w_ada=m_w_ada, b_ada=m_b_ada, norm1_gain=m_norm1_gain, w_in=m_w_in, lb_logits=m_lb_logits,
             hgrn_o_gain=m_hgrn_o_gain, q_norm_gain=m_q_norm_gain, k_norm_gain=m_k_norm_gain, sinks=m_sinks,
             w_branch_a=m_w_branch_a, w_branch_b=m_w_branch_b, w_out=m_w_out, norm2_gain=m_norm2_gain,
             w_mlp_in=m_w_mlp_in, w_mlp_out=m_w_mlp_out)
    v = dict(w_ada=v_w_ada, b_ada=v_b_ada, norm1_gain=v_norm1_gain, w_in=v_w_in, lb_logits=v_lb_logits,
             hgrn_o_gain=v_hgrn_o_gain, q_norm_gain=v_q_norm_gain, k_norm_gain=v_k_norm_gain, sinks=v_sinks,
             w_branch_a=v_w_branch_a, w_branch_b=v_w_branch_b, w_out=v_w_out, norm2_gain=v_norm2_gain,
             w_mlp_in=v_w_mlp_in, w_mlp_out=v_w_mlp_out)
    for d in (w, m, v):
        for nm in ("w_ada",) + BIG:
            d[nm] = d[nm][0]
    px, py, pc = _mesh_pos()
    me = _index((px, py, pc))
    c_arr = jnp.reshape(pc, (1,)).astype(jnp.int32)
    q_arr = jnp.reshape(2 * px + py, (1,)).astype(jnp.int32)

    shards = [(w[nm].T if nm in COLUMN_SHARDED else w[nm]).astype(BF16) for nm in BIG]
    b_shard = lax.dynamic_slice(b_ada, (0, me * ADA_W), (1, ADA_W))
    mod, sc_all = _ada_mod(c, w["w_ada"], b_shard)

    dx, sums, parts = _local_step(x[0], loss_target[0], mod, norm1_gain, norm2_gain, lb_logits, hgrn_o_gain,
                                  q_norm_gain, k_norm_gain, sinks, shards, c_arr)

    allx, g_small, loss = _small_reduce(parts, lb_logits)

    grad, delta, new_m, new_v = {}, {}, {}, {}
    for nm in BIG:
        s, r2 = sums[nm]
        rs = s.shape[1]
        g = _final_sum("sum_" + nm, s, r2, q_arr, _sum_rows(rs))
        g = g.T if nm in COLUMN_SHARDED else g
        rows = g.shape[0]
        grad[nm] = g
        delta[nm], new_m[nm], new_v[nm] = _adamw("adamw_" + nm, w[nm], g, m[nm], v[nm], 128 if rows % 128 == 0 else rows)

    dmod_cols = lax.dynamic_slice(allx[:, 0, :], (0, me * ADA_W), (N_DEV, ADA_W))
    grad["w_ada"], delta["w_ada"], new_m["w_ada"], new_v["w_ada"] = _ada_update(
        sc_all.T, dmod_cols, w["w_ada"], m["w_ada"], v["w_ada"], 256)

    small_names = [nm for nm in WEIGHTS if nm not in BIG and nm != "w_ada"]
    shapes = {nm: w[nm].shape for nm in small_names}
    ds, ms, vs = _adamw("adamw_small", _pack_small(w), g_small, _pack_small(m), _pack_small(v), 1)
    for dst, vec in ((grad, g_small), (delta, ds), (new_m, ms), (new_v, vs)):
        dst.update(_unpack_small(vec, shapes))

    def full(d, nm):
        return d[nm][None] if nm in BIG or nm == "w_ada" else d[nm]

    return (loss[0, 0], dx[None], *[full(grad, nm) for nm in WEIGHTS], *[full(delta, nm) for nm in WEIGHTS],
            *[full(new_m, nm) for nm in WEIGHTS], *[full(new_v, nm) for nm in WEIGHTS])
```

```python
import functools

import jax
import jax.numpy as jnp
import numpy as np
from jax import lax
from jax.experimental import pallas as pl
from jax.experimental.pallas import tpu as pltpu

F32 = jnp.float32
BF16 = jnp.bfloat16
MESH = pl.DeviceIdType.MESH

N_DEV = 8
D = 2048
A_HEADS, A_HD, CHUNK = 8, 128, 64
AW = A_HEADS * A_HD
Q_HEADS, KV_HEADS, GROUP, B_HD, BLK = 16, 4, 4, 64, 128
BW = Q_HEADS * B_HD
KVW = KV_HEADS * B_HD
HID = 4 * D
IN_W = 4 * AW + BW + 2 * KVW + 2 * D
OFF_QA, OFF_FA, OFF_IA, OFF_GA = 0, AW, 2 * AW, 3 * AW
OFF_QB = 4 * AW
OFF_KB = OFF_QB + BW
OFF_VB = OFF_KB + KVW
OFF_GTA = OFF_VB + KVW
OFF_GTB = OFF_GTA + D
N_MOD = 6
EPS = 1e-6
LR, B1, B2, ADAM_EPS, WD, STEP = 1e-3, 0.9, 0.999, 1e-8, 0.01, 10
NEG = -1e30

VMEM_LIMIT = 56 * 1024 * 1024

NN = (((1,), (0,)), ((), ()))
NT = (((1,), (1,)), ((), ()))
TN = (((0,), (0,)), ((), ()))
BNN = (((2,), (1,)), ((0,), (0,)))
BNT = (((2,), (2,)), ((0,), (0,)))
BTN = (((1,), (1,)), ((0,), (0,)))


def _dot(a, b, dims=NN):
    return lax.dot_general(a.astype(BF16), b.astype(BF16), dims, preferred_element_type=F32)


def _params(sem):
    return pltpu.CompilerParams(dimension_semantics=sem, vmem_limit_bytes=VMEM_LIMIT)


def _sigmoid(x):
    return 1.0 / (1.0 + jnp.exp(-x))


def _fold8(v):
    r, n = v.shape
    return jnp.sum(v.reshape(r // 8, 8, n), axis=0)


_VMEM = pl.BlockSpec(memory_space=pltpu.VMEM)
_ANY = pl.BlockSpec(memory_space=pl.ANY)
_SEMS = lambda n: pltpu.SemaphoreType.DMA((n,))


def _mesh_pos():
    return lax.axis_index("x"), lax.axis_index("y"), lax.axis_index("c")


def _flip(pos, k):
    return tuple(1 - p if (k >> s) & 1 else p for p, s in zip(pos, (2, 1, 0)))


def _index(pos):
    return 4 * pos[0] + 2 * pos[1] + pos[2]


class _Job:
    def __init__(self, ins, out_shape, sems, start, finish):
        self.ins, self.out_shape, self.sems, self.start, self.finish = list(ins), list(out_shape), list(sems), start, finish


def _pcall(body, *, grid, in_specs, out_specs, out_shape, scratch_shapes, name, semantics, args, job=None):
    if job is None:
        outs = pl.pallas_call(body, grid=grid, in_specs=in_specs, out_specs=out_specs, out_shape=out_shape,
                              scratch_shapes=scratch_shapes, name=name, compiler_params=_params(semantics))(*args)
        return list(outs), []
    n_in, n_out, n_scr = len(in_specs), len(out_specs), len(scratch_shapes)
    j_in, j_out = len(job.ins), len(job.out_shape)
    steps = tuple(grid)

    def carrier(*refs):
        o = 0
        main_in, o = refs[o:o + n_in], o + n_in
        job_in, o = refs[o:o + j_in], o + j_in
        main_out, o = refs[o:o + n_out], o + n_out
        job_out, o = refs[o:o + j_out], o + j_out
        main_scr, job_sems = refs[o:o + n_scr], refs[o + n_scr:]
        ids = [pl.program_id(a) for a in range(len(steps))]
        first = functools.reduce(lambda p, q: p & q, [i == 0 for i in ids])
        last = functools.reduce(lambda p, q: p & q, [i == s - 1 for i, s in zip(ids, steps)])

        @pl.when(first)
        def _():
            job.start(job_in, job_out, job_sems)

        body(*main_in, *main_out, *main_scr)

        @pl.when(last)
        def _():
            job.finish(job_in, job_out, job_sems)

    outs = pl.pallas_call(
        carrier, grid=grid, in_specs=list(in_specs) + [_ANY] * j_in, out_specs=list(out_specs) + [_ANY] * j_out,
        out_shape=list(out_shape) + job.out_shape, scratch_shapes=list(scratch_shapes) + job.sems, name=name,
        compiler_params=_params(("arbitrary",) * len(steps)),
    )(*args, *job.ins)
    return list(outs[:n_out]), list(outs[n_out:])


def _run_job(name, job):
    j_in, j_out = len(job.ins), len(job.out_shape)

    def body(*refs):
        ins, outs, sems = refs[:j_in], refs[j_in:j_in + j_out], refs[j_in + j_out:]
        job.start(ins, outs, sems)
        job.finish(ins, outs, sems)

    return list(pl.pallas_call(body, in_specs=[_ANY] * j_in, out_specs=[_ANY] * j_out, out_shape=job.out_shape,
                               scratch_shapes=job.sems, name=name)(*job.ins))


def _gather_job(shards):
    n = len(shards)

    def copies(ins, outs, sems):
        send_sems, recv_sems, local_sems = sems
        x, y, c = _mesh_pos()
        me, sib = (x, y, c), (x, y, 1 - c)
        chips = [(1 - x, y), (x, 1 - y), (1 - x, 1 - y)]

        def rows(a, p):
            rs = shards[a].shape[0]
            return outs[a].at[pl.ds(_index(p) * rs, rs), :]

        def copy(a, k, block, to, src=None):
            return pltpu.make_async_remote_copy(
                src_ref=rows(a, block) if src is None else src, dst_ref=rows(a, block),
                send_sem=send_sems.at[7 * a + k], recv_sem=recv_sems.at[7 * a + k], device_id=to, device_id_type=MESH)

        mine = [pltpu.make_async_copy(ins[a], rows(a, me), local_sems.at[a]) for a in range(n)]
        first = []
        for a in range(n):
            first.append(copy(a, 0, me, sib, src=ins[a]))
            first += [copy(a, 1 + j, me, (*chip, c), src=ins[a]) for j, chip in enumerate(chips)]
        return me, sib, c, chips, copy, mine, first

    def start(ins, outs, sems):
        *_, mine, first = copies(ins, outs, sems)
        for cp in mine + first:
            cp.start()

    def finish(ins, outs, sems):
        me, sib, c, chips, copy, mine, first = copies(ins, outs, sems)
        passed = []
        for j, chip in enumerate(chips):
            for a in range(n):
                copy(a, 1 + j, (*chip, c), me).wait_recv()
                cp = copy(a, 4 + j, (*chip, c), sib)
                cp.start()
                passed.append(cp)
        for a in range(n):
            copy(a, 0, sib, me).wait_recv()
            for j, chip in enumerate(chips):
                copy(a, 4 + j, (*chip, 1 - c), me).wait_recv()
        for cp in first + passed:
            cp.wait_send()
        for cp in mine:
            cp.wait()

    return _Job(shards, [jax.ShapeDtypeStruct((N_DEV * s.shape[0], s.shape[1]), s.dtype) for s in shards],
                [_SEMS(7 * n), _SEMS(7 * n), _SEMS(n)], start, finish)


def _pair_job(grads):
    n = len(grads)

    def copies(ins, outs, sems):
        send_sems, recv_sems = sems
        x, y, c = _mesh_pos()
        out = []
        for a in range(n):
            rs = grads[a].shape[0] // N_DEV
            for q in range(4):
                blk = ins[a].at[pl.ds((2 * q + 1 - c) * rs, rs), :]
                out.append(pltpu.make_async_remote_copy(
                    src_ref=blk, dst_ref=outs[a].at[q], send_sem=send_sems.at[4 * a + q], recv_sem=recv_sems.at[4 * a + q],
                    device_id=(x, y, 1 - c), device_id_type=MESH))
        return out

    def start(ins, outs, sems):
        for cp in copies(ins, outs, sems):
            cp.start()

    def finish(ins, outs, sems):
        for cp in copies(ins, outs, sems):
            cp.wait()

    return _Job(grads, [jax.ShapeDtypeStruct((4, g.shape[0] // N_DEV, g.shape[1]), g.dtype) for g in grads],
                [_SEMS(4 * n), _SEMS(4 * n)], start, finish)


def _chip_job(sums):
    n = len(sums)

    def copies(ins, outs, sems):
        send_sems, recv_sems = sems
        x, y, c = _mesh_pos()
        out = []
        for a in range(n):
            for r in (1, 2, 3):
                px, py = (1 - x if r & 2 else x), (1 - y if r & 1 else y)
                out.append(pltpu.make_async_remote_copy(
                    src_ref=ins[a].at[2 * px + py], dst_ref=outs[a].at[r - 1], send_sem=send_sems.at[3 * a + r - 1],
                    recv_sem=recv_sems.at[3 * a + r - 1], device_id=(px, py, c), device_id_type=MESH))
        return out

    def start(ins, outs, sems):
        for cp in copies(ins, outs, sems):
            cp.start()

    def finish(ins, outs, sems):
        for cp in copies(ins, outs, sems):
            cp.wait()

    return _Job(sums, [jax.ShapeDtypeStruct((3,) + s.shape[1:], s.dtype) for s in sums],
                [_SEMS(3 * n), _SEMS(3 * n)], start, finish)


def _mm(name, form, a_list, b, M, N, K, tm, tn, tk, extras, outs, epi, job=None, acc_as_ref=False):
    nI, nJ, nK = M // tm, N // tn, K // tk
    assert nI * tm == M and nJ * tn == N and nK * tk == K
    dims = {"nn": NN, "nt": NT, "tn": TN}[form]
    b_list = b if isinstance(b, list) else [(b, {"nn": N, "nt": K, "tn": N}[form])]
    nA, nB = len(a_list), len(b_list)
    assert nA == 1 or nB == 1
    assert nB == 1 or form in ("nn", "nt")
    AXIS = {"i": 0, "j": 1, "k": 2}
    a_axis, a_tile = ("i", tm) if form == "tn" else ("k", tk)
    b_axis, b_tile = ("k", tk) if form == "nt" else ("j", tn)

    def cut(pieces, tile, total):
        starts, s = [], 0
        for _, w in pieces:
            assert w % tile == 0
            starts.append(s // tile)
            s += w
        assert s == total
        return starts, [w // tile for _, w in pieces]

    a_st, a_cn = cut(a_list, a_tile, M if form == "tn" else K)
    b_st, b_cn = cut(b_list, b_tile, K if form == "nt" else N)

    def inside(idx, st, cn):
        return (idx >= st) & (idx < st + cn)

    def a_spec(p):
        st, cn = a_st[p], a_cn[p]
        if form == "tn":
            return pl.BlockSpec((tk, tm), lambda i, j, k: (jnp.where(inside(i, st, cn), k, 0), jnp.clip(i - st, 0, cn - 1)))
        return pl.BlockSpec((tm, tk), lambda i, j, k: (i, jnp.clip(k - st, 0, cn - 1)))

    def b_spec(p):
        st, cn = b_st[p], b_cn[p]
        if form == "nt":
            return pl.BlockSpec((tn, tk), lambda i, j, k: (j, jnp.clip(k - st, 0, cn - 1)))
        if nB == 1:
            return pl.BlockSpec((tk, tn), lambda i, j, k: (k, j))
        return pl.BlockSpec((tk, tn), lambda i, j, k: (jnp.where(inside(j, st, cn), k, 0), jnp.clip(j - st, 0, cn - 1)))

    in_specs = ([a_spec(p) for p in range(nA)] + [b_spec(p) for p in range(nB)]
                + [pl.BlockSpec(bs, im) for _, bs, im in extras])
    out_shape = [jax.ShapeDtypeStruct(s_, d_) for s_, d_, _, _ in outs]
    out_specs = [pl.BlockSpec(bs, im) for _, _, bs, im in outs]
    nE, nO = len(extras), len(outs)
    single = nA == 1 and nB == 1

    def body(*refs):
        a_refs, b_refs = refs[:nA], refs[nA:nA + nB]
        ex, ou = refs[nA + nB:nA + nB + nE], refs[nA + nB + nE:nA + nB + nE + nO]
        ids = [pl.program_id(a) for a in range(3)]

        def partial_of(p, q):
            return lax.dot_general(a_refs[p][...], b_refs[q][...], dims, preferred_element_type=F32)

        if nK == 1 and single:
            epi(partial_of(0, 0), ex, ou)
            return
        acc = refs[-1]
        k = ids[2]
        for p in range(nA):
            for q in range(nB):
                def first(p=p, q=q):
                    acc[...] = partial_of(p, q)

                def later(p=p, q=q):
                    acc[...] += partial_of(p, q)

                here = None
                if nA > 1:
                    here = inside(ids[AXIS[a_axis]], a_st[p], a_cn[p])
                if nB > 1:
                    here = inside(ids[AXIS[b_axis]], b_st[q], b_cn[q])
                pl.when(k == 0 if here is None else here & (k == 0))(first)
                pl.when(k > 0 if here is None else here & (k > 0))(later)

        @pl.when(k == nK - 1)
        def _():
            epi(acc if acc_as_ref else acc[...], ex, ou)

    scratch = [] if (nK == 1 and single) else [pltpu.VMEM((tm, tn), F32)]
    res, job_res = _pcall(
        body, grid=(nI, nJ, nK), in_specs=in_specs, out_specs=out_specs, out_shape=out_shape, scratch_shapes=scratch,
        name=name, semantics=("parallel", "parallel", "arbitrary"),
        args=[a for a, _ in a_list] + [p for p, _ in b_list] + [e for e, _, _ in extras], job=job)
    return res if job is None else (res, job_res)


def _stream_plan(rs, tn):
    n_tiles = N_DEV * rs // tn
    order = np.zeros((N_DEV, n_tiles), np.int32)
    need = np.full((N_DEV, n_tiles, 2), -1, np.int32)
    for me in range(N_DEV):
        q, c = me // 2, me % 2
        rank = {}
        for chip in (q, q ^ 1, q ^ 2, q ^ 3):
            for blk in (2 * chip + c, 2 * chip + 1 - c):
                rank[blk] = len(rank)
        blocks_of = lambda t: range(tn * t // rs, (tn * t + tn - 1) // rs + 1)
        tiles = sorted(range(n_tiles), key=lambda t: (max(rank[b] for b in blocks_of(t)), t))
        seen = {me}
        for s, t in enumerate(tiles):
            order[me, s] = t
            new = sorted((b for b in blocks_of(t) if b not in seen), key=rank.get)
            assert len(new) <= 2
            need[me, s, :len(new)] = new
            seen.update(new)
        assert seen == set(range(N_DEV))
    return order, need.reshape(N_DEV, 2 * n_tiles)


def _mm_stream(name, a, shard, me, M, K, tm, tn, outs, epi):
    rs = shard.shape[0]
    nI, nJ = M // tm, N_DEV * rs // tn
    order, need = _stream_plan(rs, tn)
    order, need = jnp.asarray(order)[me], jnp.asarray(need)[me]
    nO = len(outs)
    steps = nI * nJ

    def body(order_ref, need_ref, a_ref, shard_ref, *refs):
        out_refs, w_ref = refs[:nO], refs[nO]
        bbuf, dsem, send_sems, recv_sems, lsem = refs[nO + 1:]
        i, s = pl.program_id(0), pl.program_id(1)
        g = i * nJ + s
        x, y, c = _mesh_pos()
        q, mine = 2 * x + y, _index((x, y, c))
        sib = (x, y, 1 - c)
        rows = lambda b: w_ref.at[pl.ds(b * rs, rs), :]

        def own_copy(k):
            to = sib if k == 0 else (1 - x if k & 2 else x, 1 - y if k & 1 else y, c)
            return pltpu.make_async_remote_copy(src_ref=shard_ref, dst_ref=rows(mine), send_sem=send_sems.at[k],
                                                recv_sem=recv_sems.at[mine], device_id=to, device_id_type=MESH)

        def pass_on(b):
            return pltpu.make_async_remote_copy(src_ref=rows(b), dst_ref=rows(b), send_sem=send_sems.at[3 + ((b // 2) ^ q)],
                                                recv_sem=recv_sems.at[b], device_id=sib, device_id_type=MESH)

        def tile_dma(t, slot):
            return pltpu.make_async_copy(w_ref.at[pl.ds(t * tn, tn), :], bbuf.at[slot], dsem.at[slot])

        def await_blocks(step):
            for u in range(2):
                b = need_ref[2 * step + u]

                @pl.when(b >= 0)
                def _():
                    pass_on(b).wait_recv()

                    @pl.when((b % 2 == c) & (b // 2 != q))
                    def _():
                        pass_on(b).start()

        @pl.when(g == 0)
        def _():
            local = pltpu.make_async_copy(shard_ref, rows(mine), lsem)
            local.start()
            for k in range(4):
                own_copy(k).start()
            local.wait()
            await_blocks(0)
            tile_dma(order_ref[0], 0).start()

        @pl.when(g + 1 < steps)
        def _():
            nxt = jnp.where(s + 1 < nJ, s + 1, 0)

            @pl.when((i == 0) & (s + 1 < nJ))
            def _():
                await_blocks(s + 1)

            tile_dma(order_ref[nxt], (g + 1) % 2).start()

        tile_dma(order_ref[s], g % 2).wait()
        epi(lax.dot_general(a_ref[...], bbuf[g % 2], NT, preferred_element_type=F32), out_refs)

        @pl.when(g == steps - 1)
        def _():
            for k in range(4):
                own_copy(k).wait_send()
            for r in (1, 2, 3):
                pass_on(2 * (q ^ r) + c).wait_send()

    tile = lambda i, s, order_ref, need_ref: (i, order_ref[s])
    grid_spec = pltpu.PrefetchScalarGridSpec(
        num_scalar_prefetch=2, grid=(nI, nJ),
        in_specs=[pl.BlockSpec((tm, K), lambda i, s, o, n: (i, 0)), _ANY],
        out_specs=[pl.BlockSpec((tm, tn), tile)] * nO + [_ANY],
        scratch_shapes=[pltpu.VMEM((2, tn, K), shard.dtype), _SEMS(2), _SEMS(7), _SEMS(N_DEV), pltpu.SemaphoreType.DMA])
    res = pl.pallas_call(
        body, grid_spec=grid_spec,
        out_shape=[jax.ShapeDtypeStruct(sh, dt) for sh, dt in outs] + [jax.ShapeDtypeStruct((N_DEV * rs, K), shard.dtype)],
        name=name, compiler_params=_params(("arbitrary", "arbitrary")),
    )(order, need, a, shard)
    return list(res[:nO]), res[nO]


def _rms_mod_fwd(name, x, gain, sc, sh, tr):
    T = x.shape[0]

    def body(x_ref, g_ref, sc_ref, sh_ref, h_ref):
        xv = x_ref[...]
        rstd = lax.rsqrt(jnp.mean(xv * xv, axis=-1, keepdims=True) + EPS)
        h_ref[...] = ((xv * rstd * g_ref[...]) * (1.0 + sc_ref[...]) + sh_ref[...]).astype(BF16)

    row = pl.BlockSpec((tr, D), lambda i: (i, 0))
    vec = pl.BlockSpec((1, D), lambda i: (0, 0))
    return pl.pallas_call(
        body, grid=(T // tr,), in_specs=[row, vec, vec, vec], out_specs=row,
        out_shape=jax.ShapeDtypeStruct((T, D), BF16), name=name, compiler_params=_params(("parallel",)),
    )(x, gain, sc, sh)


def _rms_mod_bwd_epilogue(x, gain, sc, dres, tm, gate=None, mo=None):
    T = x.shape[0]
    with_gate = gate is not None
    row = ((tm, D), lambda i, j, k: (i, 0))
    vec = ((1, D), lambda i, j, k: (0, 0))
    part = ((T // tm * 8, D), F32, (8, D), lambda i, j, k: (i, 0))
    extras = [(x, *row), (gain, *vec), (sc, *vec), (dres, *row)]
    outs = [((T, D), F32, *row), part, part, part]
    if with_gate:
        extras += [(gate, *vec), (mo, *row)]
        outs += [((T, D), BF16, *row), part]

    rows = min(128, tm)

    def epi(acc, ex, ou):
        g = ex[1][...]
        sums = [jnp.zeros((8, D), F32) for _ in range(4)]
        for r0 in range(0, tm, rows):
            rs = slice(r0, r0 + rows)
            dhv, xv = acc[rs, :], ex[0][rs, :]
            rstd = lax.rsqrt(jnp.mean(xv * xv, axis=-1, keepdims=True) + EPS)
            xhat = xv * rstd
            dn = dhv * (1.0 + ex[2][...])
            dxhat = dn * g
            dx = ex[3][rs, :] + rstd * (dxhat - xhat * jnp.mean(dxhat * xhat, axis=-1, keepdims=True))
            ou[0][rs, :] = dx
            terms = [dhv, dhv * (xhat * g), dn * xhat]
            if with_gate:
                ou[4][rs, :] = (ex[4][...] * dx).astype(BF16)
                terms.append(dx * ex[5][rs, :].astype(F32))
            sums = [s + _fold8(t) for s, t in zip(sums, terms)] + sums[len(terms):]
        ou[1][...], ou[2][...], ou[3][...] = sums[:3]
        if with_gate:
            ou[5][...] = sums[3]

    return extras, outs, epi


def _rms_mod_bwd(name, dh, x, gain, sc, dres, tr, gate=None, mo=None):
    T = x.shape[0]
    extras, outs, epi = _rms_mod_bwd_epilogue(x, gain, sc, dres, tr, gate, mo)
    rows_only = lambda im: (lambda i: im(i, 0, 0))
    nE = len(extras)

    def body(dh_ref, *refs):
        epi(dh_ref, refs[:nE], refs[nE:])

    return pl.pallas_call(
        body, grid=(T // tr,),
        in_specs=[pl.BlockSpec((tr, D), lambda i: (i, 0))] + [pl.BlockSpec(bs, rows_only(im)) for _, bs, im in extras],
        out_specs=[pl.BlockSpec(bs, rows_only(im)) for _, _, bs, im in outs],
        out_shape=[jax.ShapeDtypeStruct(s, d) for s, d, _, _ in outs], name=name, compiler_params=_params(("parallel",)),
    )(dh, *[e for e, _, _ in extras])


def _split3(v):
    h = v.astype(BF16)
    r1 = v - h.astype(F32)
    m = r1.astype(BF16)
    lo = (r1 - m.astype(F32)).astype(BF16)
    return h, m, lo


def _tri_mm(tri, v, dims=NN):
    h, m, lo = _split3(v)
    t = tri.astype(BF16)
    mm = lambda p: lax.dot_general(t, p, dims, preferred_element_type=F32)
    return (mm(lo) + mm(m)) + mm(h)


def _hgrn_chunk_terms(q, fl, lb):
    sig = _sigmoid(fl)
    f = lb + (1.0 - lb) * sig
    lf = jnp.log(f)
    kk = 1.0 - f
    sq = _sigmoid(q)
    qf = q * sq
    return sig, f, lf, kk, sq, qf


def _causal(n):
    r = lax.broadcasted_iota(jnp.int32, (n, n), 0)
    c = lax.broadcasted_iota(jnp.int32, (n, n), 1)
    return r >= c


def _hgrn_fwd(proj, lb_logits, o_gain, tt, job=None):
    T = proj.shape[0]
    nT, ncl = T // tt, tt // CHUNK
    C = CHUNK

    def body(q_ref, f_ref, i_ref, g_ref, lbl_ref, og_ref, y_ref, st_ref, S):
        @pl.when(pl.program_id(1) == 0)
        def _():
            S[...] = jnp.zeros_like(S)

        lbl = lbl_ref[...]
        lb = _sigmoid(lbl[0:1, :] - lbl[1:2, :])
        og = og_ref[...]
        shp = (ncl, C, A_HD)
        q, fl, v, g = (r[...].reshape(shp) for r in (q_ref, f_ref, i_ref, g_ref))
        tri = jnp.broadcast_to(_causal(C), (ncl, C, C))
        _, _, lf, kk, _, qf = _hgrn_chunk_terms(q, fl, lb)
        b = _tri_mm(tri, lf, BNN)
        bm, bl = b[:, C // 2 - 1:C // 2, :], b[:, C - 1:C, :]
        qd, kd = qf * jnp.exp(b - bm), kk * jnp.exp(bm - b)
        A = jnp.where(tri, _dot(qd, kd, BNT), 0.0)
        d_st = _dot(v, kk * jnp.exp(bl - b), BTN)
        dec = jnp.exp(bl)
        st = S[...]
        for ci in range(ncl):
            st_ref[0, ci] = st
            st = st * dec[ci] + d_st[ci]
        S[...] = st
        o = _dot(A, v, BNN) + _dot(qf * jnp.exp(b), st_ref[0], BNT)
        r = lax.rsqrt(jnp.mean(o * o, axis=-1, keepdims=True) + EPS)
        y_ref[...] = (o * r * og * (g * _sigmoid(g))).astype(BF16).reshape(tt, A_HD)

    def col(off):
        return pl.BlockSpec((tt, A_HD), lambda h, t: (t, off // A_HD + h))

    head_vec = lambda rows: pl.BlockSpec((rows, A_HD), lambda h, t: (0, h))
    return _pcall(
        body, grid=(A_HEADS, nT),
        in_specs=[col(OFF_QA), col(OFF_FA), col(OFF_IA), col(OFF_GA), head_vec(2), head_vec(1)],
        out_specs=[pl.BlockSpec((tt, A_HD), lambda h, t: (t, h)),
                   pl.BlockSpec((1, ncl, A_HD, A_HD), lambda h, t: (h, t, 0, 0))],
        out_shape=[jax.ShapeDtypeStruct((T, AW), BF16),
                   jax.ShapeDtypeStruct((A_HEADS, T // C, A_HD, A_HD), F32)],
        scratch_shapes=[pltpu.VMEM((A_HD, A_HD), F32)], name="hgrn_fwd", semantics=("parallel", "arbitrary"),
        args=[proj, proj, proj, proj, lb_logits, o_gain], job=job)


def _hgrn_bwd(proj, st, dy, lb_logits, o_gain, tt, job=None):
    T = proj.shape[0]
    nT, ncl = T // tt, tt // CHUNK
    C = CHUNK

    def body(q_ref, f_ref, i_ref, g_ref, st_ref, dy_ref, lbl_ref, og_ref,
             dq_ref, df_ref, di_ref, dg_ref, plb_ref, pog_ref, dS):
        @pl.when(pl.program_id(1) == 0)
        def _():
            dS[...] = jnp.zeros_like(dS)

        lbl = lbl_ref[...]
        lb = _sigmoid(lbl[0:1, :] - lbl[1:2, :])
        og = og_ref[...]
        shp = (ncl, C, A_HD)
        flat = lambda t: t.reshape(tt, A_HD)
        q, fl, v, g, dout = (r[...].reshape(shp) for r in (q_ref, f_ref, i_ref, g_ref, dy_ref))
        tri = jnp.broadcast_to(_causal(C), (ncl, C, C))
        rowi = lax.broadcasted_iota(jnp.int32, shp, 1)
        st0 = st_ref[0]
        sig, f, lf, kk, sq, qf = _hgrn_chunk_terms(q, fl, lb)
        b = _tri_mm(tri, lf, BNN)
        bm, bl = b[:, C // 2 - 1:C // 2, :], b[:, C - 1:C, :]
        e_qd, e_kd, e_ke, e_b = jnp.exp(b - bm), jnp.exp(bm - b), jnp.exp(bl - b), jnp.exp(b)
        qd, kd, ke, qe = qf * e_qd, kk * e_kd, kk * e_ke, qf * e_b
        dec = jnp.exp(bl)
        A = jnp.where(tri, _dot(qd, kd, BNT), 0.0)
        o = _dot(A, v, BNN) + _dot(qe, st0, BNT)
        r = lax.rsqrt(jnp.mean(o * o, axis=-1, keepdims=True) + EPS)
        sg = _sigmoid(g)
        on = o * r * og
        dg_ref[...] = flat((dout * on * (sg * (1.0 + g * (1.0 - sg)))).astype(BF16))
        don = dout * (g * sg)
        pog_ref[...] = _fold8(flat(don * o * r))
        dyh = don * og
        do = r * (dyh - o * (r * r) * jnp.mean(dyh * o, axis=-1, keepdims=True))
        g_st = _dot(do, qe, BTN)
        run = dS[...]
        after = [None] * ncl
        for ci in reversed(range(ncl)):
            after[ci] = run
            run = g_st[ci] + run * dec[ci]
        dS[...] = run
        d_after = jnp.stack(after, axis=0)
        ddec = jnp.sum(d_after * st0, axis=1, keepdims=True)
        dqe = _dot(do, st0, BNN)
        dke = _dot(v, d_after, BNN)
        dA = jnp.where(tri, _dot(do, v, BNT), 0.0)
        dv = _dot(ke, d_after, BNT) + _dot(A, do, BTN)
        dqd = _dot(dA, kd, BNN)
        dkd = _dot(dA, qd, BTN)
        di_ref[...] = flat(dv.astype(BF16))
        dqf = dqe * e_b + dqd * e_qd
        dkk = dkd * e_kd + dke * e_ke
        t_qd, t_kd, t_ke = dqd * qd, dkd * kd, dke * ke
        db = dqe * qe + t_qd - t_kd - t_ke
        dbm = jnp.sum(t_kd - t_qd, axis=1, keepdims=True)
        dbl = jnp.sum(t_ke, axis=1, keepdims=True) + ddec * dec
        db = db + jnp.where(rowi == C // 2 - 1, dbm, 0.0) + jnp.where(rowi == C - 1, dbl, 0.0)
        dlf = _tri_mm(tri, db, BTN)
        dfv = dlf / f - dkk
        df_ref[...] = flat((dfv * (1.0 - lb) * sig * (1.0 - sig)).astype(BF16))
        plb_ref[...] = _fold8(flat(dfv * (1.0 - sig)))
        dq_ref[...] = flat((dqf * (sq * (1.0 + q * (1.0 - sq)))).astype(BF16))

    def col(off):
        return pl.BlockSpec((tt, A_HD), lambda h, t: (nT - 1 - t, off // A_HD + h))

    head_vec = lambda rows: pl.BlockSpec((rows, A_HD), lambda h, t: (0, h))
    o_spec = pl.BlockSpec((tt, A_HD), lambda h, t: (nT - 1 - t, h))
    p_spec = pl.BlockSpec((8, A_HD), lambda h, t: (t, h))
    o_shape = jax.ShapeDtypeStruct((T, AW), BF16)
    p_shape = jax.ShapeDtypeStruct((nT * 8, AW), F32)
    return _pcall(
        body, grid=(A_HEADS, nT),
        in_specs=[col(OFF_QA), col(OFF_FA), col(OFF_IA), col(OFF_GA),
                  pl.BlockSpec((1, ncl, A_HD, A_HD), lambda h, t: (h, nT - 1 - t, 0, 0)),
                  pl.BlockSpec((tt, A_HD), lambda h, t: (nT - 1 - t, h)), head_vec(2), head_vec(1)],
        out_specs=[o_spec, o_spec, o_spec, o_spec, p_spec, p_spec],
        out_shape=[o_shape, o_shape, o_shape, o_shape, p_shape, p_shape],
        scratch_shapes=[pltpu.VMEM((A_HD, A_HD), F32)], name="hgrn_bwd", semantics=("parallel", "arbitrary"),
        args=[proj, proj, proj, proj, st, dy, lb_logits, o_gain], job=job)


LANES = 128
Q_COLS = BW // LANES


def _low_half():
    return lax.broadcasted_iota(jnp.int32, (1, LANES), 1) < B_HD


def _half_sum(t, low):
    lo = jnp.sum(jnp.where(low, t, 0.0), axis=-1, keepdims=True)
    hi = jnp.sum(jnp.where(low, 0.0, t), axis=-1, keepdims=True)
    return jnp.where(low, lo, hi)


def _half_rms(t, low):
    r = lax.rsqrt(_half_sum(t * t, low) * (1.0 / B_HD) + EPS)
    return t * r, r


def _fold_halves(p, low):
    return jnp.where(low, p + pltpu.roll(p, B_HD, 1), 0.0)


def _stack_cols(x):
    return jnp.stack([x[:, c * LANES:(c + 1) * LANES] for c in range(Q_COLS)], axis=0).reshape(KV_HEADS, 2 * BLK, LANES)


def _col_of(t, c):
    return t[c // 2, (c % 2) * BLK:(c % 2 + 1) * BLK]


def _split_halves(col, s, low):
    own = jnp.where(low if s == 0 else jnp.logical_not(low), col, 0.0)
    other = pltpu.roll(own, B_HD, 1)
    return (own, other) if s == 0 else (other, own)


def _swa_keys(kp_ref, kc_ref, vp_ref, vc_ref, kg, low):
    k_lo, k_hi, v_lo, v_hi, hats = [], [], [], [], []
    for j in range(KVW // LANES):
        cs = slice(j * LANES, (j + 1) * LANES)
        k_hat, k_r = _half_rms(jnp.concatenate([kp_ref[:, cs], kc_ref[:, cs]], axis=0), low)
        vcol = jnp.concatenate([vp_ref[:, cs], vc_ref[:, cs]], axis=0)
        hats.append((k_hat, k_r))
        for s in range(2):
            for dst_lo, dst_hi, col in ((k_lo, k_hi, k_hat * kg), (v_lo, v_hi, vcol)):
                lo, hi = _split_halves(col, s, low)
                dst_lo.append(lo)
                dst_hi.append(hi)
    st = lambda parts: jnp.stack(parts, axis=0)
    return st(k_lo), st(k_hi), st(v_lo), st(v_hi), hats


def _swa_mask(first_block):
    qi = lax.broadcasted_iota(jnp.int32, (BLK, 2 * BLK), 0) + BLK
    ki = lax.broadcasted_iota(jnp.int32, (BLK, 2 * BLK), 1)
    rel = qi - ki
    m = (rel >= 0) & (rel < BLK) & (jnp.logical_not(first_block) | (ki >= BLK))
    return jnp.concatenate([m, m], axis=0)


def _sink_cols(sk_ref, hi):
    top = lax.broadcasted_iota(jnp.int32, (2 * BLK, 1), 0) < BLK
    return jnp.stack([jnp.where(top, sk_ref[0, GROUP * hk + hi], sk_ref[0, GROUP * hk + 2 + hi])
                      for hk in range(KV_HEADS)], axis=0)


def _swa_probs(qn, k_half, sink, mask):
    s = jnp.where(mask, _dot(qn, k_half, BNT) * (B_HD ** -0.5), NEG)
    m = jnp.maximum(jnp.max(s, axis=-1, keepdims=True), sink)
    p = jnp.exp(s - m)
    ps = jnp.exp(sink - m)
    inv = 1.0 / (jnp.sum(p, axis=-1, keepdims=True) + ps)
    return p * inv, ps * inv


def _swa_fwd(proj, q_gain, k_gain, sinks, job=None):
    T = proj.shape[0]
    nb = T // BLK

    def body(q_ref, kc_ref, kp_ref, vc_ref, vp_ref, qg_ref, kg_ref, sk_ref, o_ref):
        low = _low_half()
        mask = _swa_mask(pl.program_id(0) == 0)
        qn = _half_rms(_stack_cols(q_ref[...]), low)[0] * qg_ref[...]
        k_lo, k_hi, v_lo, v_hi, _ = _swa_keys(kp_ref, kc_ref, vp_ref, vc_ref, kg_ref[...], low)
        p_lo, _ = _swa_probs(qn, k_lo, _sink_cols(sk_ref, 0), mask)
        p_hi, _ = _swa_probs(qn, k_hi, _sink_cols(sk_ref, 1), mask)
        o = (_dot(p_lo, v_lo, BNN) + _dot(p_hi, v_hi, BNN)).astype(BF16)
        for c in range(Q_COLS):
            o_ref[:, c * LANES:(c + 1) * LANES] = _col_of(o, c)

    q_gain, k_gain = jnp.tile(q_gain, (1, 2)), jnp.tile(k_gain, (1, 2))
    cur = lambda w, off: pl.BlockSpec((BLK, w), lambda i: (i, off // w))
    prev = lambda w, off: pl.BlockSpec((BLK, w), lambda i: (jnp.maximum(i - 1, 0), off // w))
    small = lambda n: pl.BlockSpec((1, 2 * n), lambda i: (0, 0))
    return _pcall(
        body, grid=(nb,),
        in_specs=[cur(BW, OFF_QB), cur(KVW, OFF_KB), prev(KVW, OFF_KB), cur(KVW, OFF_VB), prev(KVW, OFF_VB),
                  small(B_HD), small(B_HD), pl.BlockSpec(memory_space=pltpu.SMEM)],
        out_specs=[pl.BlockSpec((BLK, BW), lambda i: (i, 0))],
        out_shape=[jax.ShapeDtypeStruct((T, BW), BF16)], scratch_shapes=[], name="swa_fwd", semantics=("parallel",),
        args=[proj, proj, proj, proj, proj, q_gain, k_gain, sinks], job=job)


def _swa_bwd(proj, dout, q_gain, k_gain, sinks, job=None):
    T = proj.shape[0]
    nb = T // BLK
    W = BW + 2 * KVW

    def body(q_ref, kc_ref, kp_ref, vc_ref, vp_ref, do_ref, qg_ref, kg_ref, sk_ref,
             dq_ref, dkv_ref, pqg_ref, pkg_ref, psk_ref, dkn_c, dv_c):
        i = pl.program_id(0)
        live = i < nb
        low = _low_half()
        high = jnp.logical_not(low)
        qg, kg = qg_ref[...], kg_ref[...]
        mask = _swa_mask(i == 0)
        lane = lax.broadcasted_iota(jnp.int32, (1, LANES), 1)
        scale = B_HD ** -0.5

        @pl.when(i == 0)
        def _():
            dkn_c[...] = jnp.zeros_like(dkn_c)
            dv_c[...] = jnp.zeros_like(dv_c)

        q_hat, q_r = _half_rms(_stack_cols(q_ref[...]), low)
        qn = q_hat * qg
        k_lo, k_hi, v_lo, v_hi, hats = _swa_keys(kp_ref, kc_ref, vp_ref, vc_ref, kg, low)
        do = _stack_cols(do_ref[...])
        dqn = jnp.zeros((KV_HEADS, 2 * BLK, LANES), F32)
        acc_sk = jnp.zeros((1, LANES), F32)
        dk_parts, dv_parts = [], []
        for hi, (k_h, v_h) in enumerate(((k_lo, v_lo), (k_hi, v_hi))):
            p, ps = _swa_probs(qn, k_h, _sink_cols(sk_ref, hi), mask)
            dp = _dot(do, v_h, BNT)
            delta = jnp.sum(p * dp, axis=-1, keepdims=True)
            ds = p * (dp - delta) * scale
            dqn = dqn + _dot(ds, k_h, BNN)
            dk_parts.append(_dot(ds, qn, BTN))
            dv_parts.append(_dot(p, do, BTN))
            t = ps * delta
            for hk in range(KV_HEADS):
                for rows in range(2):
                    h = GROUP * hk + 2 * rows + hi
                    acc_sk = acc_sk + jnp.where(
                        lane == h, -jnp.sum(t[hk, rows * BLK:(rows + 1) * BLK], axis=0, keepdims=True), 0.0)
        dqh = dqn * qg
        dq = (q_r * (dqh - q_hat * (_half_sum(dqh * q_hat, low) * (1.0 / B_HD)))).astype(BF16)
        for c in range(Q_COLS):
            dq_ref[:, c * LANES:(c + 1) * LANES] = _col_of(dq, c)
        acc_qg = _fold_halves(_fold8((dqn * q_hat).reshape(KV_HEADS * 2 * BLK, LANES)), low)

        def native(parts, j):
            lo_arr, hi_arr = parts
            a, b = 2 * j, 2 * j + 1
            return (jnp.where(low, lo_arr[a], 0.0) + pltpu.roll(jnp.where(high, hi_arr[a], 0.0), B_HD, 1)
                    + jnp.where(high, hi_arr[b], 0.0) + pltpu.roll(jnp.where(low, lo_arr[b], 0.0), B_HD, 1))

        acc_kg = jnp.zeros((8, LANES), F32)
        for j in range(KVW // LANES):
            cs = slice(j * LANES, (j + 1) * LANES)
            dkn = jnp.where(live, native(dk_parts, j), 0.0)
            dvc = jnp.where(live, native(dv_parts, j), 0.0)
            kp_hat, kp_r = hats[j][0][:BLK], hats[j][1][:BLK]
            dkn_prev = dkn_c[:, cs] + dkn[:BLK]
            dv_prev = dv_c[:, cs] + dvc[:BLK]
            acc_kg = acc_kg + _fold8(dkn_prev * kp_hat)
            dkh = dkn_prev * kg
            dkv_ref[:, cs] = (kp_r * (dkh - kp_hat * (_half_sum(dkh * kp_hat, low) * (1.0 / B_HD)))).astype(BF16)
            dkv_ref[:, KVW + j * LANES:KVW + (j + 1) * LANES] = dv_prev.astype(BF16)
            dkn_c[:, cs] = dkn[BLK:]
            dv_c[:, cs] = dvc[BLK:]
        keep = jnp.where(i > 0, 1.0, 0.0)
        pqg_ref[...] = jnp.where(live, acc_qg, 0.0)
        pkg_ref[...] = _fold_halves(acc_kg, low) * keep
        psk_ref[...] = jnp.broadcast_to(jnp.where(live, acc_sk, 0.0), (8, LANES)) * (
            lax.broadcasted_iota(jnp.int32, (8, LANES), 0) == 0).astype(F32)

    q_gain, k_gain = jnp.tile(q_gain, (1, 2)), jnp.tile(k_gain, (1, 2))
    last = nb - 1
    cur = lambda w, off: pl.BlockSpec((BLK, w), lambda i: (jnp.minimum(i, last), off // w))
    prev = lambda w, off: pl.BlockSpec((BLK, w), lambda i: (jnp.maximum(i - 1, 0), off // w))
    small = lambda n: pl.BlockSpec((1, 2 * n), lambda i: (0, 0))
    part = pl.BlockSpec((8, 128), lambda i: (i, 0))
    p_shape = jax.ShapeDtypeStruct(((nb + 1) * 8, 128), F32)
    return _pcall(
        body, grid=(nb + 1,),
        in_specs=[cur(BW, OFF_QB), cur(KVW, OFF_KB), prev(KVW, OFF_KB), cur(KVW, OFF_VB), prev(KVW, OFF_VB),
                  pl.BlockSpec((BLK, BW), lambda i: (jnp.minimum(i, last), 0)), small(B_HD), small(B_HD),
                  pl.BlockSpec(memory_space=pltpu.SMEM)],
        out_specs=[pl.BlockSpec((BLK, BW), lambda i: (i, 0)),
                   pl.BlockSpec((BLK, 2 * KVW), lambda i: (jnp.maximum(i - 1, 0), 0)), part, part, part],
        out_shape=[jax.ShapeDtypeStruct((T + BLK, BW), BF16), jax.ShapeDtypeStruct((T, 2 * KVW), BF16),
                   p_shape, p_shape, p_shape],
        scratch_shapes=[pltpu.VMEM((BLK, KVW), F32), pltpu.VMEM((BLK, KVW), F32)], name="swa_bwd",
        semantics=("arbitrary",), args=[proj, proj, proj, proj, proj, dout, q_gain, k_gain, sinks], job=job)


def _branch_merge(ya_pre, attn, wa_t, wb_t, proj, tm, tn):
    T = ya_pre.shape[0]

    def body(a_ref, b_ref, wa_ref, wb_ref, ga_ref, gb_ref, ya_ref, yb_ref, mg_ref):
        ya = lax.dot_general(a_ref[...], wa_ref[...], NT, preferred_element_type=F32)
        yb = lax.dot_general(b_ref[...], wb_ref[...], NT, preferred_element_type=F32)
        ya_ref[...] = ya.astype(BF16)
        yb_ref[...] = yb.astype(BF16)
        mg_ref[...] = (_sigmoid(ga_ref[...]) * ya + _sigmoid(gb_ref[...]) * yb).astype(BF16)

    o_spec = pl.BlockSpec((tm, tn), lambda i, j: (i, j))
    o_shape = jax.ShapeDtypeStruct((T, D), BF16)
    return pl.pallas_call(
        body, grid=(T // tm, D // tn),
        in_specs=[pl.BlockSpec((tm, AW), lambda i, j: (i, 0)), pl.BlockSpec((tm, BW), lambda i, j: (i, 0)),
                  pl.BlockSpec((tn, AW), lambda i, j: (j, 0)), pl.BlockSpec((tn, BW), lambda i, j: (j, 0)),
                  pl.BlockSpec((tm, tn), lambda i, j: (i, OFF_GTA // tn + j)),
                  pl.BlockSpec((tm, tn), lambda i, j: (i, OFF_GTB // tn + j))],
        out_specs=[o_spec, o_spec, o_spec], out_shape=[o_shape, o_shape, o_shape], name="branch_merge",
        compiler_params=_params(("parallel", "parallel")),
    )(ya_pre, attn, wa_t, wb_t, proj, proj)


def _ij(i, j, k):
    return (i, j)


def _local_step(x, tgt, mod, g1, g2, lbl, og, qg, kg, sk, shards, me, c_arr):
    win_s, wa_s, wb_s, wout_s, wmi_s, wmo_s = shards
    T = x.shape[0]
    tm, tr, tt = min(1024, T), min(256, T), min(512, T)
    tk_t = min(1024, T)
    tn = 512
    sh1, sc1, gt1, sh2, sc2, gt2 = (mod[:, i * D:(i + 1) * D] for i in range(N_MOD))
    nI = T // tm
    blk = (tm, tn)
    part = lambda: ((nI * 8, D), F32, (8, tn), _ij)
    vec_j = ((1, tn), lambda i, j, k: (0, j))

    h = _rms_mod_fwd("rms1_fwd", x, g1, sc1, sh1, tr)

    def epi_store(acc, ex, ou):
        ou[0][...] = acc.astype(ou[0].dtype)

    tm2 = min(2048, T)
    blk2 = (tm2, tn)

    def store_tile(acc, ou):
        ou[0][...] = acc

    (proj,), win_t = _mm_stream("in_proj", h, win_s, me, T, D, tm2, tn, [((T, IN_W), F32)], store_tile)
    (ya_pre, st), (wa_t, wb_t, w_out) = _hgrn_fwd(proj, lbl, og, tt, job=_gather_job([wa_s, wb_s, wout_s]))
    (attn,), (w_mo,) = _swa_fwd(proj, qg, kg, sk, job=_gather_job([wmo_s]))
    ya, yb, merged = _branch_merge(ya_pre, attn, wa_t, wb_t, proj, tm, tn)

    def epi_res1(acc, ex, ou):
        x_ref, gt_ref = ex
        ou[0][...] = acc.astype(BF16)
        ou[1][...] = x_ref[...] + gt_ref[...] * acc

    mo, x1 = _mm("out_proj", "nn", [(merged, D)], w_out, T, D, D, tm, tn, D,
                 [(x, blk, _ij), (gt1, *vec_j)], [((T, D), BF16, blk, _ij), ((T, D), F32, blk, _ij)], epi_res1)
    h2 = _rms_mod_fwd("rms2_fwd", x1, g2, sc2, sh2, tr)

    def relu2_tile(acc, ou):
        r = jnp.maximum(acc, 0.0)
        ou[0][...] = r.astype(BF16)
        ou[1][...] = (r * r).astype(BF16)

    (r, a), wmi_t = _mm_stream("mlp_in", h2, wmi_s, me, T, D, tm2, tn, [((T, HID), BF16), ((T, HID), BF16)], relu2_tile)

    def epi_loss(acc, ex, ou):
        x1_ref, t_ref, gt_ref = ex
        e = x1_ref[...] + gt_ref[...] * acc - t_ref[...]
        dy = e * (1.0 / D)
        ou[0][...] = dy
        ou[1][...] = (gt_ref[...] * dy).astype(BF16)
        ou[2][...] = _fold8(e * e) * (0.5 / D)
        ou[3][...] = _fold8(dy * acc)

    wide = (tm, 1024)
    part_w = ((nI * 8, D), F32, (8, 1024), _ij)
    dy, dz, p_loss, p_gt2 = _mm(
        "mlp_out", "nn", [(a, HID)], w_mo, T, D, HID, tm, 1024, 1024,
        [(x1, wide, _ij), (tgt, wide, _ij), (gt2, (1, 1024), lambda i, j, k: (0, j))],
        [((T, D), F32, wide, _ij), ((T, D), BF16, wide, _ij), part_w, part_w], epi_loss)

    def epi_du(acc, ex, ou):
        ou[0][...] = (acc * (2.0 * ex[0][...].astype(F32))).astype(BF16)

    (du,) = _mm("mlp_out_dx", "nt", [(dz, D)], w_mo, T, HID, D, tm2, tn, D, [(r, blk2, _ij)],
                [((T, HID), BF16, blk2, _ij)], epi_du)
    gblk = (1024, 1024)
    gwide = (1024, D)
    pair_sum = lambda nm, g, r1: _pair_sum("pair_sum_" + nm, g, r1, c_arr, _sum_rows(r1.shape[1]))
    (g_mo,) = _mm("mlp_out_dw", "tn", [(a, HID)], dz, HID, D, T, 1024, D, tk_t, [], [((HID, D), BF16, gwide, _ij)], epi_store)
    (dh2,), (r1_mo,) = _mm("mlp_in_dx", "nn", [(du, HID)], wmi_t, T, D, HID, tm, 1024, 1024, [],
                           [((T, D), F32, (tm, 1024), _ij)], epi_store, job=_pair_job([g_mo]))
    dx1, p_sh2, p_sc2, p_g2, dmo, p_gt1 = _rms_mod_bwd("rms2_bwd", dh2, x1, g2, sc2, dy, tr, gate=gt1, mo=mo)
    s_mo = pair_sum("mlp_out", g_mo, r1_mo)
    tm_row = min(512, T)
    (g_mi,), (r2_mo,) = _mm("mlp_in_dw", "tn", [(du, HID)], h2, HID, D, T, 1024, D, tk_t, [],
                            [((HID, D), BF16, gwide, _ij)], epi_store, job=_chip_job([s_mo]))

    def epi_gates(acc, ex, ou):
        ya_ref, yb_ref, ga_ref, gb_ref = ex
        sa, sb = _sigmoid(ga_ref[...]), _sigmoid(gb_ref[...])
        ou[0][...] = (acc * sa).astype(BF16)
        ou[1][...] = (acc * sb).astype(BF16)
        ou[2][...] = (acc * ya_ref[...].astype(F32) * (sa * (1.0 - sa))).astype(BF16)
        ou[3][...] = (acc * yb_ref[...].astype(F32) * (sb * (1.0 - sb))).astype(BF16)

    o_bf = ((T, D), BF16, blk, _ij)
    (dya, dyb, dga, dgb), (r1_mi,) = _mm(
        "out_proj_dx", "nt", [(dmo, D)], w_out, T, D, D, tm, tn, D,
        [(ya, blk, _ij), (yb, blk, _ij), (proj, blk, lambda i, j, k: (i, OFF_GTA // tn + j)),
         (proj, blk, lambda i, j, k: (i, OFF_GTB // tn + j))], [o_bf, o_bf, o_bf, o_bf], epi_gates,
        job=_pair_job([g_mi]))
    s_mi = pair_sum("mlp_in", g_mi, r1_mi)
    (g_out,) = _mm("out_proj_dw", "tn", [(merged, D)], dmo, D, D, T, 1024, 1024, tk_t, [], [((D, D), BF16, gblk, _ij)], epi_store)
    (dya_pre,) = _mm("branch_a_dx", "nn", [(dya, D)], wa_t, T, AW, D, tm, tn, D, [], [((T, AW), F32, blk, _ij)], epi_store)
    (dattn,) = _mm("branch_b_dx", "nn", [(dyb, D)], wb_t, T, BW, D, tm, tn, D, [], [((T, BW), F32, blk, _ij)], epi_store)
    (g_a,) = _mm("branch_a_dw", "tn", [(dya, D)], ya_pre, D, AW, T, 1024, 1024, tk_t, [], [((D, AW), BF16, gblk, _ij)], epi_store)
    (g_b,) = _mm("branch_b_dw", "tn", [(dyb, D)], attn, D, BW, T, 1024, 1024, tk_t, [], [((D, BW), BF16, gblk, _ij)], epi_store)
    (dqa, dfa, dia, dgg, p_lb, p_og), (r2_mi,) = _hgrn_bwd(proj, st, dya_pre, lbl, og, tt, job=_chip_job([s_mi]))
    (dqb, dkv, p_qg, p_kg, p_sk), (r1_out, r1_a, r1_b) = _swa_bwd(proj, dattn, qg, kg, sk, job=_pair_job([g_out, g_a, g_b]))
    s_out, s_a, s_b = pair_sum("out", g_out, r1_out), pair_sum("branch_a", g_a, r1_a), pair_sum("branch_b", g_b, r1_b)
    pieces = [(dqa, AW), (dfa, AW), (dia, AW), (dgg, AW), (dqb, BW), (dkv, 2 * KVW), (dga, D), (dgb, D)]
    (g_in,), (r2_out, r2_a, r2_b) = _mm(
        "in_proj_dw", "tn", pieces, h, IN_W, D, T, 512, D, tk_t, [], [((IN_W, D), BF16, (512, D), _ij)], epi_store,
        job=_chip_job([s_out, s_a, s_b]))
    (r1_in,) = _run_job("pair_w_in", _pair_job([g_in]))
    s_in = pair_sum("in", g_in, r1_in)
    extras, outs, epi = _rms_mod_bwd_epilogue(x, g1, sc1, dx1, tm_row)
    (dx, p_sh1, p_sc1, p_g1), (r2_in,) = _mm(
        "in_proj_dx", "nn", pieces, win_t, T, D, IN_W, tm_row, D, 512, extras, outs, epi, job=_chip_job([s_in]),
        acc_as_ref=True)

    partials = dict(sh1=p_sh1, sc1=p_sc1, gt1=p_gt1, sh2=p_sh2, sc2=p_sc2, gt2=p_gt2, g1=p_g1, g2=p_g2,
                    lb=p_lb, og=p_og, qg=p_qg, kg=p_kg, sk=p_sk, loss=p_loss)
    sums = dict(w_in=(s_in, r2_in), w_branch_a=(s_a, r2_a), w_branch_b=(s_b, r2_b), w_out=(s_out, r2_out),
                w_mlp_in=(s_mi, r2_mi), w_mlp_out=(s_mo, r2_mo))
    return dx, sums, partials


def _exchange_slots(buf, send_sems, recv_sems):
    me = _mesh_pos()
    mine = buf.at[_index(me)]
    sends = []
    for k in range(1, N_DEV):
        cp = pltpu.make_async_remote_copy(src_ref=mine, dst_ref=mine, send_sem=send_sems.at[k - 1],
                                          recv_sem=recv_sems.at[k - 1], device_id=_flip(me, k), device_id_type=MESH)
        cp.start()
        sends.append(cp)
    for k in range(1, N_DEV):
        theirs = buf.at[_index(_flip(me, k))]
        pltpu.make_async_remote_copy(src_ref=theirs, dst_ref=theirs, send_sem=send_sems.at[k - 1],
                                     recv_sem=recv_sems.at[k - 1], device_id=_flip(me, k), device_id_type=MESH).wait_recv()
    for cp in sends:
        cp.wait_send()


ADA_W = N_MOD * D // N_DEV


def _ada_mod(c, w_ada, b_shard):
    def body(c_ref, w_ref, b_ref, mod_ref, sc_ref, cbuf, mbuf, s1, r1, s2, r2):
        me = _index(_mesh_pos())
        cbuf[me] = c_ref[...]
        _exchange_slots(cbuf, s1, r1)
        row = lax.broadcasted_iota(jnp.int32, (N_DEV, D), 0)
        call = jnp.zeros((N_DEV, D), F32)
        for d in range(N_DEV):
            call = jnp.where(row == d, cbuf[d], call)
        sc = call * _sigmoid(call)
        sc_ref[...] = sc
        mbuf[me] = _dot(sc, w_ref[...]) + b_ref[...]
        _exchange_slots(mbuf, s2, r2)
        for s in range(N_DEV):
            mod_ref[:, s * ADA_W:(s + 1) * ADA_W] = mbuf[s, pl.ds(me, 1), :]

    return pl.pallas_call(
        body, in_specs=[_VMEM, _VMEM, _VMEM], out_specs=[_VMEM, _VMEM],
        out_shape=[jax.ShapeDtypeStruct((1, N_MOD * D), F32), jax.ShapeDtypeStruct((N_DEV, D), F32)],
        scratch_shapes=[pltpu.VMEM((N_DEV, 1, D), F32), pltpu.VMEM((N_DEV, N_DEV, ADA_W), F32),
                        _SEMS(N_DEV - 1), _SEMS(N_DEV - 1), _SEMS(N_DEV - 1), _SEMS(N_DEV - 1)],
        name="ada_mod", compiler_params=pltpu.CompilerParams(vmem_limit_bytes=VMEM_LIMIT),
    )(c, w_ada, b_shard)


SMALL_SEGS = (("b_ada", N_MOD * D), ("norm1_gain", D), ("norm2_gain", D), ("lb0", AW), ("lb1", AW),
              ("hgrn_o_gain", AW), ("q_norm_gain", 128), ("k_norm_gain", 128), ("sinks", 128))
SMALL_W = sum(w for _, w in SMALL_SEGS)
X_SEGS = (("sh1", D), ("sc1", D), ("gt1", D), ("sh2", D), ("sc2", D), ("gt2", D), ("g1", D), ("g2", D),
          ("lb", AW), ("og", AW), ("qg", 128), ("kg", 128), ("sk", 128), ("loss", 128))
X_W = sum(w for _, w in X_SEGS)


def _offsets(segs):
    out, o = {}, 0
    for name, w in segs:
        out[name] = (o, w)
        o += w
    return out


def _small_reduce(parts, lb_logits):
    xo, so = _offsets(X_SEGS), _offsets(SMALL_SEGS)
    names = [nm for nm, _ in X_SEGS]

    def body(*refs):
        p_refs = dict(zip(names, refs[:len(names)]))
        lbl_ref, allx, gs_ref, loss_ref, send_sems, recv_sems = refs[len(names):]
        me = _index(_mesh_pos())
        for nm, (o, w) in xo.items():
            if nm == "loss":
                allx[me, :, o:o + w] = jnp.broadcast_to(jnp.sum(p_refs[nm][...]), (1, w))
            else:
                allx[me, :, o:o + w] = jnp.sum(p_refs[nm][...], axis=0, keepdims=True)
        _exchange_slots(allx, send_sems, recv_sems)
        tot = allx[0]
        for d in range(1, N_DEV):
            tot = tot + allx[d]
        seg = lambda nm: tot[:, xo[nm][0]:xo[nm][0] + xo[nm][1]]

        def put(nm, v):
            gs_ref[:, so[nm][0]:so[nm][0] + so[nm][1]] = v

        put("b_ada", tot[:, 0:N_MOD * D])
        put("norm1_gain", seg("g1"))
        put("norm2_gain", seg("g2"))
        lbl = lbl_ref[...]
        lb = _sigmoid(lbl[0:1, :] - lbl[1:2, :])
        dl0 = seg("lb") * lb * (1.0 - lb)
        put("lb0", dl0)
        put("lb1", -dl0)
        put("hgrn_o_gain", seg("og"))
        put("q_norm_gain", seg("qg"))
        put("k_norm_gain", seg("kg"))
        put("sinks", seg("sk"))
        loss_ref[...] = seg("loss")

    return pl.pallas_call(
        body, in_specs=[_VMEM] * (len(names) + 1), out_specs=[_VMEM, _VMEM, _VMEM],
        out_shape=[jax.ShapeDtypeStruct((N_DEV, 1, X_W), F32), jax.ShapeDtypeStruct((1, SMALL_W), F32),
                   jax.ShapeDtypeStruct((1, 128), F32)],
        scratch_shapes=[_SEMS(N_DEV - 1), _SEMS(N_DEV - 1)], name="small_reduce",
        compiler_params=pltpu.CompilerParams(vmem_limit_bytes=VMEM_LIMIT),
    )(*[parts[nm] for nm in names], lb_logits)


def _adamw_math(w, g, m, v):
    m = B1 * m + (1.0 - B1) * g
    v = B2 * v + (1.0 - B2) * (g * g)
    m_hat = m / (1.0 - B1 ** STEP)
    v_hat = v / (1.0 - B2 ** STEP)
    return -LR * (m_hat / (jnp.sqrt(v_hat) + ADAM_EPS) + WD * w), m, v


def _sum_rows(rs):
    return 256 if rs % 256 == 0 else rs // 2


def _pair_sum(name, g, recv, c_arr, tr):
    _, rs, cols = recv.shape
    blk = (1, tr, cols)

    def body(c_ref, g_ref, r_ref, o_ref):
        o_ref[...] = (g_ref[...].astype(F32) + r_ref[...].astype(F32)).astype(BF16)

    grid_spec = pltpu.PrefetchScalarGridSpec(
        num_scalar_prefetch=1, grid=(4, rs // tr),
        in_specs=[pl.BlockSpec(blk, lambda q, i, c: (2 * q + c[0], i, 0)), pl.BlockSpec(blk, lambda q, i, c: (q, i, 0))],
        out_specs=pl.BlockSpec(blk, lambda q, i, c: (q, i, 0)))
    return pl.pallas_call(body, grid_spec=grid_spec, out_shape=jax.ShapeDtypeStruct((4, rs, cols), BF16), name=name,
                          compiler_params=_params(("parallel", "parallel")))(c_arr, g.reshape(N_DEV, rs, cols), recv)


def _final_sum(name, sums, recv, q_arr, tr):
    _, rs, cols = sums.shape

    def body(q_ref, s_ref, r_ref, o_ref):
        o_ref[...] = ((s_ref[0].astype(F32) + r_ref[0].astype(F32)) + r_ref[1].astype(F32)) + r_ref[2].astype(F32)

    grid_spec = pltpu.PrefetchScalarGridSpec(
        num_scalar_prefetch=1, grid=(rs // tr,),
        in_specs=[pl.BlockSpec((1, tr, cols), lambda i, q: (q[0], i, 0)), pl.BlockSpec((3, tr, cols), lambda i, q: (0, i, 0))],
        out_specs=pl.BlockSpec((tr, cols), lambda i, q: (i, 0)))
    return pl.pallas_call(body, grid_spec=grid_spec, out_shape=jax.ShapeDtypeStruct((rs, cols), F32), name=name,
                          compiler_params=_params(("parallel",)))(q_arr, sums, recv)


def _adamw(name, w, g, m, v, tr):
    rows, cols = w.shape

    def body(w_ref, g_ref, m_ref, v_ref, d_ref, nm_ref, nv_ref):
        d_ref[...], nm_ref[...], nv_ref[...] = _adamw_math(w_ref[...], g_ref[...], m_ref[...], v_ref[...])

    spec = pl.BlockSpec((tr, cols), lambda i: (i, 0))
    shape = jax.ShapeDtypeStruct((rows, cols), F32)
    return pl.pallas_call(
        body, grid=(rows // tr,), in_specs=[spec] * 4, out_specs=[spec] * 3, out_shape=[shape] * 3, name=name,
        compiler_params=_params(("parallel",)),
    )(w, g, m, v)


def _ada_update(sc_t, dmod_cols, w, m, v, tr):
    rows, cols = w.shape

    def body(s_ref, d_ref, w_ref, m_ref, v_ref, g_ref, dl_ref, nm_ref, nv_ref):
        g = jnp.dot(s_ref[...], d_ref[...], precision=lax.Precision.HIGHEST, preferred_element_type=F32)
        g_ref[...] = g
        dl_ref[...], nm_ref[...], nv_ref[...] = _adamw_math(w_ref[...], g, m_ref[...], v_ref[...])

    spec = pl.BlockSpec((tr, cols), lambda i: (i, 0))
    shape = jax.ShapeDtypeStruct((rows, cols), F32)
    return pl.pallas_call(
        body, grid=(rows // tr,),
        in_specs=[pl.BlockSpec((tr, N_DEV), lambda i: (i, 0)), pl.BlockSpec((N_DEV, cols), lambda i: (0, 0)), spec, spec, spec],
        out_specs=[spec] * 4, out_shape=[shape] * 4, name="ada_update", compiler_params=_params(("parallel",)),
    )(sc_t, dmod_cols, w, m, v)


BIG = ("w_in", "w_branch_a", "w_branch_b", "w_out", "w_mlp_in", "w_mlp_out")
COLUMN_SHARDED = ("w_in", "w_branch_a", "w_branch_b", "w_mlp_in")
WEIGHTS = ("w_ada", "b_ada", "norm1_gain", "w_in", "lb_logits", "hgrn_o_gain", "q_norm_gain", "k_norm_gain", "sinks",
           "w_branch_a", "w_branch_b", "w_out", "norm2_gain", "w_mlp_in", "w_mlp_out")


def _pack_small(p):
    lb = p["lb_logits"]
    src = dict(p, lb0=lb[0:1], lb1=lb[1:2])
    return jnp.concatenate([jnp.pad(src[nm], ((0, 0), (0, w - src[nm].shape[1]))) for nm, w in SMALL_SEGS], axis=1)


def _unpack_small(vec, shapes):
    so = _offsets(SMALL_SEGS)
    out = {}
    for nm, shp in shapes.items():
        if nm == "lb_logits":
            o = so["lb0"][0]
            out[nm] = vec[0, o:o + 2 * AW].reshape(2, AW)
        else:
            o = so[nm][0]
            out[nm] = vec[:, o:o + shp[1]]
    return out


def kernel(x, c, w_ada, b_ada, norm1_gain, w_in, lb_logits, hgrn_o_gain, q_norm_gain, k_norm_gain, sinks, w_branch_a, w_branch_b, w_out, norm2_gain, w_mlp_in, w_mlp_out, loss_target, m_w_ada, m_b_ada, m_norm1_gain, m_w_in, m_lb_logits, m_hgrn_o_gain, m_q_norm_gain, m_k_norm_gain, m_sinks, m_w_branch_a, m_w_branch_b, m_w_out, m_norm2_gain, m_w_mlp_in, m_w_mlp_out, v_w_ada, v_b_ada, v_norm1_gain, v_w_in, v_lb_logits, v_hgrn_o_gain, v_q_norm_gain, v_k_norm_gain, v_sinks, v_w_branch_a, v_w_branch_b, v_w_out, v_norm2_gain, v_w_mlp_in, v_w_mlp_out):
    w = dict(w_ada=w_ada, b_ada=b_ada, norm1_gain=norm1_gain, w_in=w_in, lb_logits=lb_logits, hgrn_o_gain=hgrn_o_gain,
             q_norm_gain=q_norm_gain, k_norm_gain=k_norm_gain, sinks=sinks, w_branch_a=w_branch_a, w_branch_b=w_branch_b,
             w_out=w_out, norm2_gain=norm2_gain, w_mlp_in=w_mlp_in, w_mlp_out=w_mlp_out)
    m = dict(w_ada=m_w_ada, b_ada=m_b_ada, norm1_gain=m_norm1_gain, w_in=m_w_in, lb_logits=m_lb_logits,
             hgrn_o_gain=m_hgrn_o_gain, q_norm_gain=m_q_norm_gain, k_norm_gain=m_k_norm_gain, sinks=m_sinks,
             w_branch_a=m_w_branch_a, w_branch_b=m_w_branch_b, w_out=m_w_out, norm2_gain=m_norm2_gain,
             w_mlp_in=m_w_mlp_in, w_mlp_out=m_w_mlp_out)
    v = dict(w_ada=v_w_ada, b_ada=v_b_ada, norm1_gain=v_norm1_gain, w_in=v_w_in, lb_logits=v_lb_logits,
             hgrn_o_gain=v_hgrn_o_gain, q_norm_gain=v_q_norm_gain, k_norm_gain=v_k_norm_gain, sinks=v_sinks,
             w_branch_a=v_w_branch_a, w_branch_b=v_w_branch_b, w_out=v_w_out, norm2_gain=v_norm2_gain,
             w_mlp_in=v_w_mlp_in, w_mlp_out=v_w_mlp_out)
    for d in (w, m, v):
        for nm in ("w_ada",) + BIG:
            d[nm] = d[nm][0]
    px, py, pc = _mesh_pos()
    me = _index((px, py, pc))
    c_arr = jnp.reshape(pc, (1,)).astype(jnp.int32)
    q_arr = jnp.reshape(2 * px + py, (1,)).astype(jnp.int32)

    shards = [(w[nm].T if nm in COLUMN_SHARDED else w[nm]).astype(BF16) for nm in BIG]
    b_shard = lax.dynamic_slice(b_ada, (0, me * ADA_W), (1, ADA_W))
    mod, sc_all = _ada_mod(c, w["w_ada"], b_shard)

    dx, sums, parts = _local_step(x[0], loss_target[0], mod, norm1_gain, norm2_gain, lb_logits, hgrn_o_gain,
                                  q_norm_gain, k_norm_gain, sinks, shards, me, c_arr)

    allx, g_small, loss = _small_reduce(parts, lb_logits)

    grad, delta, new_m, new_v = {}, {}, {}, {}
    for nm in BIG:
        s, r2 = sums[nm]
        rs = s.shape[1]
        g = _final_sum("sum_" + nm, s, r2, q_arr, _sum_rows(rs))
        g = g.T if nm in COLUMN_SHARDED else g
        rows = g.shape[0]
        grad[nm] = g
        delta[nm], new_m[nm], new_v[nm] = _adamw("adamw_" + nm, w[nm], g, m[nm], v[nm], 128 if rows % 128 == 0 else rows)

    dmod_cols = lax.dynamic_slice(allx[:, 0, :], (0, me * ADA_W), (N_DEV, ADA_W))
    grad["w_ada"], delta["w_ada"], new_m["w_ada"], new_v["w_ada"] = _ada_update(
        sc_all.T, dmod_cols, w["w_ada"], m["w_ada"], v["w_ada"], 256)

    small_names = [nm for nm in WEIGHTS if nm not in BIG and nm != "w_ada"]
    shapes = {nm: w[nm].shape for nm in small_names}
    ds, ms, vs = _adamw("adamw_small", _pack_small(w), g_small, _pack_small(m), _pack_small(v), 1)
    for dst, vec in ((grad, g_small), (delta, ds), (new_m, ms), (new_v, vs)):
        dst.update(_unpack_small(vec, shapes))

    def full(d, nm):
        return d[nm][None] if nm in BIG or nm == "w_ada" else d[nm]

    return (loss[0, 0], dx[None], *[full(grad, nm) for nm in WEIGHTS], *[full(delta, nm) for nm in WEIGHTS],
            *[full(new_m, nm) for nm in WEIGHTS], *[full(new_v, nm) for nm in WEIGHTS])
```

```python
import functools

import jax
import jax.numpy as jnp
import numpy as np
from jax import lax
from jax.experimental import pallas as pl
from jax.experimental.pallas import tpu as pltpu

F32 = jnp.float32
BF16 = jnp.bfloat16
MESH = pl.DeviceIdType.MESH

N_DEV = 8
D = 2048
A_HEADS, A_HD, CHUNK = 8, 128, 64
AW = A_HEADS * A_HD
Q_HEADS, KV_HEADS, GROUP, B_HD, BLK = 16, 4, 4, 64, 128
BW = Q_HEADS * B_HD
KVW = KV_HEADS * B_HD
HID = 4 * D
IN_W = 4 * AW + BW + 2 * KVW + 2 * D
OFF_QA, OFF_FA, OFF_IA, OFF_GA = 0, AW, 2 * AW, 3 * AW
OFF_QB = 4 * AW
OFF_KB = OFF_QB + BW
OFF_VB = OFF_KB + KVW
OFF_GTA = OFF_VB + KVW
OFF_GTB = OFF_GTA + D
N_MOD = 6
EPS = 1e-6
LR, B1, B2, ADAM_EPS, WD, STEP = 1e-3, 0.9, 0.999, 1e-8, 0.01, 10
NEG = -1e30

VMEM_LIMIT = 56 * 1024 * 1024
MI_CUT = 576
MO_CUTS = (480, 800)

NN = (((1,), (0,)), ((), ()))
NT = (((1,), (1,)), ((), ()))
TN = (((0,), (0,)), ((), ()))
BNN = (((2,), (1,)), ((0,), (0,)))
BNT = (((2,), (2,)), ((0,), (0,)))
BTN = (((1,), (1,)), ((0,), (0,)))


def _dot(a, b, dims=NN):
    return lax.dot_general(a.astype(BF16), b.astype(BF16), dims, preferred_element_type=F32)


def _params(sem):
    return pltpu.CompilerParams(dimension_semantics=sem, vmem_limit_bytes=VMEM_LIMIT)


def _sigmoid(x):
    return 1.0 / (1.0 + jnp.exp(-x))


def _fold8(v):
    r, n = v.shape
    return jnp.sum(v.reshape(r // 8, 8, n), axis=0)


_VMEM = pl.BlockSpec(memory_space=pltpu.VMEM)
_ANY = pl.BlockSpec(memory_space=pl.ANY)
_SEMS = lambda n: pltpu.SemaphoreType.DMA((n,))


def _mesh_pos():
    return lax.axis_index("x"), lax.axis_index("y"), lax.axis_index("c")


def _flip(pos, k):
    return tuple(1 - p if (k >> s) & 1 else p for p, s in zip(pos, (2, 1, 0)))


def _index(pos):
    return 4 * pos[0] + 2 * pos[1] + pos[2]


class _Job:
    def __init__(self, ins, out_shape, sems, start, finish, aliases=None):
        self.ins, self.out_shape, self.sems, self.start, self.finish = list(ins), list(out_shape), list(sems), start, finish
        self.aliases = dict(aliases or {})


def _pcall(body, *, grid, in_specs, out_specs, out_shape, scratch_shapes, name, semantics, args, job=None):
    if job is None:
        outs = pl.pallas_call(body, grid=grid, in_specs=in_specs, out_specs=out_specs, out_shape=out_shape,
                              scratch_shapes=scratch_shapes, name=name, compiler_params=_params(semantics))(*args)
        return list(outs), []
    n_in, n_out, n_scr = len(in_specs), len(out_specs), len(scratch_shapes)
    j_in, j_out = len(job.ins), len(job.out_shape)
    steps = tuple(grid)

    def carrier(*refs):
        o = 0
        main_in, o = refs[o:o + n_in], o + n_in
        job_in, o = refs[o:o + j_in], o + j_in
        main_out, o = refs[o:o + n_out], o + n_out
        job_out, o = refs[o:o + j_out], o + j_out
        main_scr, job_sems = refs[o:o + n_scr], refs[o + n_scr:]
        ids = [pl.program_id(a) for a in range(len(steps))]
        first = functools.reduce(lambda p, q: p & q, [i == 0 for i in ids])
        last = functools.reduce(lambda p, q: p & q, [i == s - 1 for i, s in zip(ids, steps)])

        @pl.when(first)
        def _():
            job.start(job_in, job_out, job_sems)

        body(*main_in, *main_out, *main_scr)

        @pl.when(last)
        def _():
            job.finish(job_in, job_out, job_sems)

    outs = pl.pallas_call(
        carrier, grid=grid, in_specs=list(in_specs) + [_ANY] * j_in, out_specs=list(out_specs) + [_ANY] * j_out,
        out_shape=list(out_shape) + job.out_shape, scratch_shapes=list(scratch_shapes) + job.sems, name=name,
        input_output_aliases={n_in + i: n_out + o for i, o in job.aliases.items()},
        compiler_params=_params(("arbitrary",) * len(steps)),
    )(*args, *job.ins)
    return list(outs[:n_out]), list(outs[n_out:])


def _run_job(name, job):
    j_in, j_out = len(job.ins), len(job.out_shape)

    def body(*refs):
        ins, outs, sems = refs[:j_in], refs[j_in:j_in + j_out], refs[j_in + j_out:]
        job.start(ins, outs, sems)
        job.finish(ins, outs, sems)

    return list(pl.pallas_call(body, in_specs=[_ANY] * j_in, out_specs=[_ANY] * j_out, out_shape=job.out_shape,
                               scratch_shapes=job.sems, name=name,
                               input_output_aliases=job.aliases)(*job.ins))


def _gather_job(shards, rows=None, into=None):
    n = len(shards)
    rows = rows or [(0, s.shape[0]) for s in shards]
    into = into or [None] * n
    olds, aliases = [], {}
    for a, buf in enumerate(into):
        if buf is not None:
            aliases[n + len(olds)] = a
            olds.append(buf)

    def copies(ins, outs, sems):
        send_sems, recv_sems, local_sems = sems
        x, y, c = _mesh_pos()
        me, sib = (x, y, c), (x, y, 1 - c)
        chips = [(1 - x, y), (x, 1 - y), (1 - x, 1 - y)]

        def part(a, p):
            rs, (r0, r1) = shards[a].shape[0], rows[a]
            return outs[a].at[pl.ds(_index(p) * rs + r0, r1 - r0), :]

        own = lambda a: ins[a].at[pl.ds(rows[a][0], rows[a][1] - rows[a][0]), :]

        def copy(a, k, block, to, src=None):
            return pltpu.make_async_remote_copy(
                src_ref=part(a, block) if src is None else src, dst_ref=part(a, block),
                send_sem=send_sems.at[7 * a + k], recv_sem=recv_sems.at[7 * a + k], device_id=to, device_id_type=MESH)

        mine = [pltpu.make_async_copy(own(a), part(a, me), local_sems.at[a]) for a in range(n)]
        first = []
        for a in range(n):
            first.append(copy(a, 0, me, sib, src=own(a)))
            first += [copy(a, 1 + j, me, (*chip, c), src=own(a)) for j, chip in enumerate(chips)]
        return me, sib, c, chips, copy, mine, first

    def start(ins, outs, sems):
        *_, mine, first = copies(ins, outs, sems)
        for cp in mine + first:
            cp.start()

    def finish(ins, outs, sems):
        me, sib, c, chips, copy, mine, first = copies(ins, outs, sems)
        passed = []
        for j, chip in enumerate(chips):
            for a in range(n):
                copy(a, 1 + j, (*chip, c), me).wait_recv()
                cp = copy(a, 4 + j, (*chip, c), sib)
                cp.start()
                passed.append(cp)
        for a in range(n):
            copy(a, 0, sib, me).wait_recv()
            for j, chip in enumerate(chips):
                copy(a, 4 + j, (*chip, 1 - c), me).wait_recv()
        for cp in first + passed:
            cp.wait_send()
        for cp in mine:
            cp.wait()

    return _Job(list(shards) + olds, [jax.ShapeDtypeStruct((N_DEV * s.shape[0], s.shape[1]), s.dtype) for s in shards],
                [_SEMS(7 * n), _SEMS(7 * n), _SEMS(n)], start, finish, aliases)


def _pair_job(grads):
    n = len(grads)

    def copies(ins, outs, sems):
        send_sems, recv_sems = sems
        x, y, c = _mesh_pos()
        out = []
        for a in range(n):
            rs = grads[a].shape[0] // N_DEV
            for q in range(4):
                blk = ins[a].at[pl.ds((2 * q + 1 - c) * rs, rs), :]
                out.append(pltpu.make_async_remote_copy(
                    src_ref=blk, dst_ref=outs[a].at[q], send_sem=send_sems.at[4 * a + q], recv_sem=recv_sems.at[4 * a + q],
                    device_id=(x, y, 1 - c), device_id_type=MESH))
        return out

    def start(ins, outs, sems):
        for cp in copies(ins, outs, sems):
            cp.start()

    def finish(ins, outs, sems):
        for cp in copies(ins, outs, sems):
            cp.wait()

    return _Job(grads, [jax.ShapeDtypeStruct((4, g.shape[0] // N_DEV, g.shape[1]), g.dtype) for g in grads],
                [_SEMS(4 * n), _SEMS(4 * n)], start, finish)


def _chip_job(sums):
    n = len(sums)

    def copies(ins, outs, sems):
        send_sems, recv_sems = sems
        x, y, c = _mesh_pos()
        out = []
        for a in range(n):
            for r in (1, 2, 3):
                px, py = (1 - x if r & 2 else x), (1 - y if r & 1 else y)
                out.append(pltpu.make_async_remote_copy(
                    src_ref=ins[a].at[2 * px + py], dst_ref=outs[a].at[r - 1], send_sem=send_sems.at[3 * a + r - 1],
                    recv_sem=recv_sems.at[3 * a + r - 1], device_id=(px, py, c), device_id_type=MESH))
        return out

    def start(ins, outs, sems):
        for cp in copies(ins, outs, sems):
            cp.start()

    def finish(ins, outs, sems):
        for cp in copies(ins, outs, sems):
            cp.wait()

    return _Job(sums, [jax.ShapeDtypeStruct((3,) + s.shape[1:], s.dtype) for s in sums],
                [_SEMS(3 * n), _SEMS(3 * n)], start, finish)


def _mm(name, form, a_list, b, M, N, K, tm, tn, tk, extras, outs, epi, job=None, acc_as_ref=False):
    nI, nJ, nK = M // tm, N // tn, K // tk
    assert nI * tm == M and nJ * tn == N and nK * tk == K
    dims = {"nn": NN, "nt": NT, "tn": TN}[form]
    b_list = b if isinstance(b, list) else [(b, {"nn": N, "nt": K, "tn": N}[form])]
    nA, nB = len(a_list), len(b_list)
    assert nA == 1 or nB == 1
    assert nB == 1 or form in ("nn", "nt")
    AXIS = {"i": 0, "j": 1, "k": 2}
    a_axis, a_tile = ("i", tm) if form == "tn" else ("k", tk)
    b_axis, b_tile = ("k", tk) if form == "nt" else ("j", tn)

    def cut(pieces, tile, total):
        starts, s = [], 0
        for _, w in pieces:
            assert w % tile == 0
            starts.append(s // tile)
            s += w
        assert s == total
        return starts, [w // tile for _, w in pieces]

    a_st, a_cn = cut(a_list, a_tile, M if form == "tn" else K)
    b_st, b_cn = cut(b_list, b_tile, K if form == "nt" else N)

    def inside(idx, st, cn):
        return (idx >= st) & (idx < st + cn)

    def a_spec(p):
        st, cn = a_st[p], a_cn[p]
        if form == "tn":
            return pl.BlockSpec((tk, tm), lambda i, j, k: (jnp.where(inside(i, st, cn), k, 0), jnp.clip(i - st, 0, cn - 1)))
        return pl.BlockSpec((tm, tk), lambda i, j, k: (i, jnp.clip(k - st, 0, cn - 1)))

    def b_spec(p):
        st, cn = b_st[p], b_cn[p]
        if form == "nt":
            return pl.BlockSpec((tn, tk), lambda i, j, k: (j, jnp.clip(k - st, 0, cn - 1)))
        if nB == 1:
            return pl.BlockSpec((tk, tn), lambda i, j, k: (k, j))
        return pl.BlockSpec((tk, tn), lambda i, j, k: (jnp.where(inside(j, st, cn), k, 0), jnp.clip(j - st, 0, cn - 1)))

    in_specs = ([a_spec(p) for p in range(nA)] + [b_spec(p) for p in range(nB)]
                + [pl.BlockSpec(bs, im) for _, bs, im in extras])
    out_shape = [jax.ShapeDtypeStruct(s_, d_) for s_, d_, _, _ in outs]
    out_specs = [pl.BlockSpec(bs, im) for _, _, bs, im in outs]
    nE, nO = len(extras), len(outs)
    single = nA == 1 and nB == 1

    def body(*refs):
        a_refs, b_refs = refs[:nA], refs[nA:nA + nB]
        ex, ou = refs[nA + nB:nA + nB + nE], refs[nA + nB + nE:nA + nB + nE + nO]
        ids = [pl.program_id(a) for a in range(3)]

        def partial_of(p, q):
            return lax.dot_general(a_refs[p][...], b_refs[q][...], dims, preferred_element_type=F32)

        if nK == 1 and single:
            epi(partial_of(0, 0), ex, ou)
            return
        acc = refs[-1]
        k = ids[2]
        for p in range(nA):
            for q in range(nB):
                def first(p=p, q=q):
                    acc[...] = partial_of(p, q)

                def later(p=p, q=q):
                    acc[...] += partial_of(p, q)

                here = None
                if nA > 1:
                    here = inside(ids[AXIS[a_axis]], a_st[p], a_cn[p])
                if nB > 1:
                    here = inside(ids[AXIS[b_axis]], b_st[q], b_cn[q])
                pl.when(k == 0 if here is None else here & (k == 0))(first)
                pl.when(k > 0 if here is None else here & (k > 0))(later)

        @pl.when(k == nK - 1)
        def _():
            epi(acc if acc_as_ref else acc[...], ex, ou)

    scratch = [] if (nK == 1 and single) else [pltpu.VMEM((tm, tn), F32)]
    res, job_res = _pcall(
        body, grid=(nI, nJ, nK), in_specs=in_specs, out_specs=out_specs, out_shape=out_shape, scratch_shapes=scratch,
        name=name, semantics=("parallel", "parallel", "arbitrary"),
        args=[a for a, _ in a_list] + [p for p, _ in b_list] + [e for e, _, _ in extras], job=job)
    return res if job is None else (res, job_res)


def _stream_plan(rs, tn):
    n_tiles = N_DEV * rs // tn
    order = np.zeros((N_DEV, n_tiles), np.int32)
    need = np.full((N_DEV, n_tiles, 2), -1, np.int32)
    for me in range(N_DEV):
        q, c = me // 2, me % 2
        rank = {}
        for chip in (q, q ^ 1, q ^ 2, q ^ 3):
            for blk in (2 * chip + c, 2 * chip + 1 - c):
                rank[blk] = len(rank)
        blocks_of = lambda t: range(tn * t // rs, (tn * t + tn - 1) // rs + 1)
        tiles = sorted(range(n_tiles), key=lambda t: (max(rank[b] for b in blocks_of(t)), t))
        seen = {me}
        for s, t in enumerate(tiles):
            order[me, s] = t
            new = sorted((b for b in blocks_of(t) if b not in seen), key=rank.get)
            assert len(new) <= 2
            need[me, s, :len(new)] = new
            seen.update(new)
        assert seen == set(range(N_DEV))
    return order, need.reshape(N_DEV, 2 * n_tiles)


def _mm_stream(name, a, shard, me, M, K, tm, tn, outs, epi):
    rs = shard.shape[0]
    nI, nJ = M // tm, N_DEV * rs // tn
    order, need = _stream_plan(rs, tn)
    order, need = jnp.asarray(order)[me], jnp.asarray(need)[me]
    nO = len(outs)
    steps = nI * nJ

    def body(order_ref, need_ref, a_ref, shard_ref, *refs):
        out_refs, w_ref = refs[:nO], refs[nO]
        bbuf, dsem, send_sems, recv_sems, lsem = refs[nO + 1:]
        i, s = pl.program_id(0), pl.program_id(1)
        g = i * nJ + s
        x, y, c = _mesh_pos()
        q, mine = 2 * x + y, _index((x, y, c))
        sib = (x, y, 1 - c)
        rows = lambda b: w_ref.at[pl.ds(b * rs, rs), :]

        def own_copy(k):
            to = sib if k == 0 else (1 - x if k & 2 else x, 1 - y if k & 1 else y, c)
            return pltpu.make_async_remote_copy(src_ref=shard_ref, dst_ref=rows(mine), send_sem=send_sems.at[k],
                                                recv_sem=recv_sems.at[mine], device_id=to, device_id_type=MESH)

        def pass_on(b):
            return pltpu.make_async_remote_copy(src_ref=rows(b), dst_ref=rows(b), send_sem=send_sems.at[3 + ((b // 2) ^ q)],
                                                recv_sem=recv_sems.at[b], device_id=sib, device_id_type=MESH)

        def tile_dma(t, slot):
            return pltpu.make_async_copy(w_ref.at[pl.ds(t * tn, tn), :], bbuf.at[slot], dsem.at[slot])

        def await_blocks(step):
            for u in range(2):
                b = need_ref[2 * step + u]

                @pl.when(b >= 0)
                def _():
                    pass_on(b).wait_recv()

                    @pl.when((b % 2 == c) & (b // 2 != q))
                    def _():
                        pass_on(b).start()

        @pl.when(g == 0)
        def _():
            local = pltpu.make_async_copy(shard_ref, rows(mine), lsem)
            local.start()
            for k in range(4):
                own_copy(k).start()
            local.wait()
            await_blocks(0)
            tile_dma(order_ref[0], 0).start()

        @pl.when(g + 1 < steps)
        def _():
            nxt = jnp.where(s + 1 < nJ, s + 1, 0)

            @pl.when((i == 0) & (s + 1 < nJ))
            def _():
                await_blocks(s + 1)

            tile_dma(order_ref[nxt], (g + 1) % 2).start()

        tile_dma(order_ref[s], g % 2).wait()
        epi(lax.dot_general(a_ref[...], bbuf[g % 2], NT, preferred_element_type=F32), out_refs)

        @pl.when(g == steps - 1)
        def _():
            for k in range(4):
                own_copy(k).wait_send()
            for r in (1, 2, 3):
                pass_on(2 * (q ^ r) + c).wait_send()

    tile = lambda i, s, order_ref, need_ref: (i, order_ref[s])
    grid_spec = pltpu.PrefetchScalarGridSpec(
        num_scalar_prefetch=2, grid=(nI, nJ),
        in_specs=[pl.BlockSpec((tm, K), lambda i, s, o, n: (i, 0)), _ANY],
        out_specs=[pl.BlockSpec((tm, tn), tile)] * nO + [_ANY],
        scratch_shapes=[pltpu.VMEM((2, tn, K), shard.dtype), _SEMS(2), _SEMS(7), _SEMS(N_DEV), pltpu.SemaphoreType.DMA])
    res = pl.pallas_call(
        body, grid_spec=grid_spec,
        out_shape=[jax.ShapeDtypeStruct(sh, dt) for sh, dt in outs] + [jax.ShapeDtypeStruct((N_DEV * rs, K), shard.dtype)],
        name=name, compiler_params=_params(("arbitrary", "arbitrary")),
    )(order, need, a, shard)
    return list(res[:nO]), res[nO]


def _rms_mod_fwd(name, x, gain, sc, sh, tr):
    T = x.shape[0]

    def body(x_ref, g_ref, sc_ref, sh_ref, h_ref):
        xv = x_ref[...]
        rstd = lax.rsqrt(jnp.mean(xv * xv, axis=-1, keepdims=True) + EPS)
        h_ref[...] = ((xv * rstd * g_ref[...]) * (1.0 + sc_ref[...]) + sh_ref[...]).astype(BF16)

    row = pl.BlockSpec((tr, D), lambda i: (i, 0))
    vec = pl.BlockSpec((1, D), lambda i: (0, 0))
    return pl.pallas_call(
        body, grid=(T // tr,), in_specs=[row, vec, vec, vec], out_specs=row,
        out_shape=jax.ShapeDtypeStruct((T, D), BF16), name=name, compiler_params=_params(("parallel",)),
    )(x, gain, sc, sh)


def _rms_mod_bwd_epilogue(x, gain, sc, dres, tm, gate=None, mo=None):
    T = x.shape[0]
    with_gate = gate is not None
    row = ((tm, D), lambda i, j, k: (i, 0))
    vec = ((1, D), lambda i, j, k: (0, 0))
    part = ((T // tm * 8, D), F32, (8, D), lambda i, j, k: (i, 0))
    extras = [(x, *row), (gain, *vec), (sc, *vec), (dres, *row)]
    outs = [((T, D), F32, *row), part, part, part]
    if with_gate:
        extras += [(gate, *vec), (mo, *row)]
        outs += [((T, D), BF16, *row), part]

    rows = min(128, tm)

    def epi(acc, ex, ou):
        g = ex[1][...]
        sums = [jnp.zeros((8, D), F32) for _ in range(4)]
        for r0 in range(0, tm, rows):
            rs = slice(r0, r0 + rows)
            dhv, xv = acc[rs, :], ex[0][rs, :]
            rstd = lax.rsqrt(jnp.mean(xv * xv, axis=-1, keepdims=True) + EPS)
            xhat = xv * rstd
            dn = dhv * (1.0 + ex[2][...])
            dxhat = dn * g
            dx = ex[3][rs, :] + rstd * (dxhat - xhat * jnp.mean(dxhat * xhat, axis=-1, keepdims=True))
            ou[0][rs, :] = dx
            terms = [dhv, dhv * (xhat * g), dn * xhat]
            if with_gate:
                ou[4][rs, :] = (ex[4][...] * dx).astype(BF16)
                terms.append(dx * ex[5][rs, :].astype(F32))
            sums = [s + _fold8(t) for s, t in zip(sums, terms)] + sums[len(terms):]
        ou[1][...], ou[2][...], ou[3][...] = sums[:3]
        if with_gate:
            ou[5][...] = sums[3]

    return extras, outs, epi


def _rms_mod_bwd(name, dh, x, gain, sc, dres, tr, gate=None, mo=None):
    T = x.shape[0]
    extras, outs, epi = _rms_mod_bwd_epilogue(x, gain, sc, dres, tr, gate, mo)
    rows_only = lambda im: (lambda i: im(i, 0, 0))
    nE = len(extras)

    def body(dh_ref, *refs):
        epi(dh_ref, refs[:nE], refs[nE:])

    return pl.pallas_call(
        body, grid=(T // tr,),
        in_specs=[pl.BlockSpec((tr, D), lambda i: (i, 0))] + [pl.BlockSpec(bs, rows_only(im)) for _, bs, im in extras],
        out_specs=[pl.BlockSpec(bs, rows_only(im)) for _, _, bs, im in outs],
        out_shape=[jax.ShapeDtypeStruct(s, d) for s, d, _, _ in outs], name=name, compiler_params=_params(("parallel",)),
    )(dh, *[e for e, _, _ in extras])


def _split3(v):
    h = v.astype(BF16)
    r1 = v - h.astype(F32)
    m = r1.astype(BF16)
    lo = (r1 - m.astype(F32)).astype(BF16)
    return h, m, lo


def _tri_mm(tri, v, dims=NN):
    h, m, lo = _split3(v)
    t = tri.astype(BF16)
    mm = lambda p: lax.dot_general(t, p, dims, preferred_element_type=F32)
    return (mm(lo) + mm(m)) + mm(h)


def _hgrn_chunk_terms(q, fl, lb):
    sig = _sigmoid(fl)
    f = lb + (1.0 - lb) * sig
    lf = jnp.log(f)
    kk = 1.0 - f
    sq = _sigmoid(q)
    qf = q * sq
    return sig, f, lf, kk, sq, qf


def _causal(n):
    r = lax.broadcasted_iota(jnp.int32, (n, n), 0)
    c = lax.broadcasted_iota(jnp.int32, (n, n), 1)
    return r >= c


def _hgrn_fwd(proj, lb_logits, o_gain, tt, job=None):
    T = proj.shape[0]
    nT, ncl = T // tt, tt // CHUNK
    C = CHUNK

    def body(q_ref, f_ref, i_ref, g_ref, lbl_ref, og_ref, y_ref, st_ref, S):
        @pl.when(pl.program_id(1) == 0)
        def _():
            S[...] = jnp.zeros_like(S)

        lbl = lbl_ref[...]
        lb = _sigmoid(lbl[0:1, :] - lbl[1:2, :])
        og = og_ref[...]
        shp = (ncl, C, A_HD)
        q, fl, v, g = (r[...].reshape(shp) for r in (q_ref, f_ref, i_ref, g_ref))
        tri = jnp.broadcast_to(_causal(C), (ncl, C, C))
        _, _, lf, kk, _, qf = _hgrn_chunk_terms(q, fl, lb)
        b = _tri_mm(tri, lf, BNN)
        bm, bl = b[:, C // 2 - 1:C // 2, :], b[:, C - 1:C, :]
        qd, kd = qf * jnp.exp(b - bm), kk * jnp.exp(bm - b)
        A = jnp.where(tri, _dot(qd, kd, BNT), 0.0)
        d_st = _dot(v, kk * jnp.exp(bl - b), BTN)
        dec = jnp.exp(bl)
        st = S[...]
        for ci in range(ncl):
            st_ref[0, ci] = st
            st = st * dec[ci] + d_st[ci]
        S[...] = st
        o = _dot(A, v, BNN) + _dot(qf * jnp.exp(b), st_ref[0], BNT)
        r = lax.rsqrt(jnp.mean(o * o, axis=-1, keepdims=True) + EPS)
        y_ref[...] = (o * r * og * (g * _sigmoid(g))).astype(BF16).reshape(tt, A_HD)

    def col(off):
        return pl.BlockSpec((tt, A_HD), lambda h, t: (t, off // A_HD + h))

    head_vec = lambda rows: pl.BlockSpec((rows, A_HD), lambda h, t: (0, h))
    return _pcall(
        body, grid=(A_HEADS, nT),
        in_specs=[col(OFF_QA), col(OFF_FA), col(OFF_IA), col(OFF_GA), head_vec(2), head_vec(1)],
        out_specs=[pl.BlockSpec((tt, A_HD), lambda h, t: (t, h)),
                   pl.BlockSpec((1, ncl, A_HD, A_HD), lambda h, t: (h, t, 0, 0))],
        out_shape=[jax.ShapeDtypeStruct((T, AW), BF16),
                   jax.ShapeDtypeStruct((A_HEADS, T // C, A_HD, A_HD), F32)],
        scratch_shapes=[pltpu.VMEM((A_HD, A_HD), F32)], name="hgrn_fwd", semantics=("parallel", "arbitrary"),
        args=[proj, proj, proj, proj, lb_logits, o_gain], job=job)


def _hgrn_bwd(proj, st, dy, lb_logits, o_gain, tt, job=None):
    T = proj.shape[0]
    nT, ncl = T // tt, tt // CHUNK
    C = CHUNK

    def body(q_ref, f_ref, i_ref, g_ref, st_ref, dy_ref, lbl_ref, og_ref,
             dq_ref, df_ref, di_ref, dg_ref, plb_ref, pog_ref, dS):
        @pl.when(pl.program_id(1) == 0)
        def _():
            dS[...] = jnp.zeros_like(dS)

        lbl = lbl_ref[...]
        lb = _sigmoid(lbl[0:1, :] - lbl[1:2, :])
        og = og_ref[...]
        shp = (ncl, C, A_HD)
        flat = lambda t: t.reshape(tt, A_HD)
        q, fl, v, g, dout = (r[...].reshape(shp) for r in (q_ref, f_ref, i_ref, g_ref, dy_ref))
        tri = jnp.broadcast_to(_causal(C), (ncl, C, C))
        rowi = lax.broadcasted_iota(jnp.int32, shp, 1)
        st0 = st_ref[0]
        sig, f, lf, kk, sq, qf = _hgrn_chunk_terms(q, fl, lb)
        b = _tri_mm(tri, lf, BNN)
        bm, bl = b[:, C // 2 - 1:C // 2, :], b[:, C - 1:C, :]
        e_qd, e_kd, e_ke, e_b = jnp.exp(b - bm), jnp.exp(bm - b), jnp.exp(bl - b), jnp.exp(b)
        qd, kd, ke, qe = qf * e_qd, kk * e_kd, kk * e_ke, qf * e_b
        dec = jnp.exp(bl)
        A = jnp.where(tri, _dot(qd, kd, BNT), 0.0)
        o = _dot(A, v, BNN) + _dot(qe, st0, BNT)
        r = lax.rsqrt(jnp.mean(o * o, axis=-1, keepdims=True) + EPS)
        sg = _sigmoid(g)
        on = o * r * og
        dg_ref[...] = flat((dout * on * (sg * (1.0 + g * (1.0 - sg)))).astype(BF16))
        don = dout * (g * sg)
        pog_ref[...] = _fold8(flat(don * o * r))
        dyh = don * og
        do = r * (dyh - o * (r * r) * jnp.mean(dyh * o, axis=-1, keepdims=True))
        g_st = _dot(do, qe, BTN)
        run = dS[...]
        after = [None] * ncl
        for ci in reversed(range(ncl)):
            after[ci] = run
            run = g_st[ci] + run * dec[ci]
        dS[...] = run
        d_after = jnp.stack(after, axis=0)
        ddec = jnp.sum(d_after * st0, axis=1, keepdims=True)
        dqe = _dot(do, st0, BNN)
        dke = _dot(v, d_after, BNN)
        dA = jnp.where(tri, _dot(do, v, BNT), 0.0)
        dv = _dot(ke, d_after, BNT) + _dot(A, do, BTN)
        dqd = _dot(dA, kd, BNN)
        dkd = _dot(dA, qd, BTN)
        di_ref[...] = flat(dv.astype(BF16))
        dqf = dqe * e_b + dqd * e_qd
        dkk = dkd * e_kd + dke * e_ke
        t_qd, t_kd, t_ke = dqd * qd, dkd * kd, dke * ke
        db = dqe * qe + t_qd - t_kd - t_ke
        dbm = jnp.sum(t_kd - t_qd, axis=1, keepdims=True)
        dbl = jnp.sum(t_ke, axis=1, keepdims=True) + ddec * dec
        db = db + jnp.where(rowi == C // 2 - 1, dbm, 0.0) + jnp.where(rowi == C - 1, dbl, 0.0)
        dlf = _tri_mm(tri, db, BTN)
        dfv = dlf / f - dkk
        df_ref[...] = flat((dfv * (1.0 - lb) * sig * (1.0 - sig)).astype(BF16))
        plb_ref[...] = _fold8(flat(dfv * (1.0 - sig)))
        dq_ref[...] = flat((dqf * (sq * (1.0 + q * (1.0 - sq)))).astype(BF16))

    def col(off):
        return pl.BlockSpec((tt, A_HD), lambda h, t: (nT - 1 - t, off // A_HD + h))

    head_vec = lambda rows: pl.BlockSpec((rows, A_HD), lambda h, t: (0, h))
    o_spec = pl.BlockSpec((tt, A_HD), lambda h, t: (nT - 1 - t, h))
    p_spec = pl.BlockSpec((8, A_HD), lambda h, t: (t, h))
    o_shape = jax.ShapeDtypeStruct((T, AW), BF16)
    p_shape = jax.ShapeDtypeStruct((nT * 8, AW), F32)
    return _pcall(
        body, grid=(A_HEADS, nT),
        in_specs=[col(OFF_QA), col(OFF_FA), col(OFF_IA), col(OFF_GA),
                  pl.BlockSpec((1, ncl, A_HD, A_HD), lambda h, t: (h, nT - 1 - t, 0, 0)),
                  pl.BlockSpec((tt, A_HD), lambda h, t: (nT - 1 - t, h)), head_vec(2), head_vec(1)],
        out_specs=[o_spec, o_spec, o_spec, o_spec, p_spec, p_spec],
        out_shape=[o_shape, o_shape, o_shape, o_shape, p_shape, p_shape],
        scratch_shapes=[pltpu.VMEM((A_HD, A_HD), F32)], name="hgrn_bwd", semantics=("parallel", "arbitrary"),
        args=[proj, proj, proj, proj, st, dy, lb_logits, o_gain], job=job)


LANES = 128
Q_COLS = BW // LANES


def _low_half():
    return lax.broadcasted_iota(jnp.int32, (1, LANES), 1) < B_HD


def _half_sum(t, low):
    lo = jnp.sum(jnp.where(low, t, 0.0), axis=-1, keepdims=True)
    hi = jnp.sum(jnp.where(low, 0.0, t), axis=-1, keepdims=True)
    return jnp.where(low, lo, hi)


def _half_rms(t, low):
    r = lax.rsqrt(_half_sum(t * t, low) * (1.0 / B_HD) + EPS)
    return t * r, r


def _fold_halves(p, low):
    return jnp.where(low, p + pltpu.roll(p, B_HD, 1), 0.0)


def _stack_cols(x):
    return jnp.stack([x[:, c * LANES:(c + 1) * LANES] for c in range(Q_COLS)], axis=0).reshape(KV_HEADS, 2 * BLK, LANES)


def _col_of(t, c):
    return t[c // 2, (c % 2) * BLK:(c % 2 + 1) * BLK]


def _split_halves(col, s, low):
    own = jnp.where(low if s == 0 else jnp.logical_not(low), col, 0.0)
    other = pltpu.roll(own, B_HD, 1)
    return (own, other) if s == 0 else (other, own)


def _swa_keys(kp_ref, kc_ref, vp_ref, vc_ref, kg, low):
    k_lo, k_hi, v_lo, v_hi, hats = [], [], [], [], []
    for j in range(KVW // LANES):
        cs = slice(j * LANES, (j + 1) * LANES)
        k_hat, k_r = _half_rms(jnp.concatenate([kp_ref[:, cs], kc_ref[:, cs]], axis=0), low)
        vcol = jnp.concatenate([vp_ref[:, cs], vc_ref[:, cs]], axis=0)
        hats.append((k_hat, k_r))
        for s in range(2):
            for dst_lo, dst_hi, col in ((k_lo, k_hi, k_hat * kg), (v_lo, v_hi, vcol)):
                lo, hi = _split_halves(col, s, low)
                dst_lo.append(lo)
                dst_hi.append(hi)
    st = lambda parts: jnp.stack(parts, axis=0)
    return st(k_lo), st(k_hi), st(v_lo), st(v_hi), hats


def _swa_mask(first_block):
    qi = lax.broadcasted_iota(jnp.int32, (BLK, 2 * BLK), 0) + BLK
    ki = lax.broadcasted_iota(jnp.int32, (BLK, 2 * BLK), 1)
    rel = qi - ki
    m = (rel >= 0) & (rel < BLK) & (jnp.logical_not(first_block) | (ki >= BLK))
    return jnp.concatenate([m, m], axis=0)


def _sink_cols(sk_ref, hi):
    top = lax.broadcasted_iota(jnp.int32, (2 * BLK, 1), 0) < BLK
    return jnp.stack([jnp.where(top, sk_ref[0, GROUP * hk + hi], sk_ref[0, GROUP * hk + 2 + hi])
                      for hk in range(KV_HEADS)], axis=0)


def _swa_probs(qn, k_half, sink, mask):
    s = jnp.where(mask, _dot(qn, k_half, BNT) * (B_HD ** -0.5), NEG)
    m = jnp.maximum(jnp.max(s, axis=-1, keepdims=True), sink)
    p = jnp.exp(s - m)
    ps = jnp.exp(sink - m)
    inv = 1.0 / (jnp.sum(p, axis=-1, keepdims=True) + ps)
    return p * inv, ps * inv


def _swa_fwd(proj, q_gain, k_gain, sinks, job=None):
    T = proj.shape[0]
    nb = T // BLK

    def body(q_ref, kc_ref, kp_ref, vc_ref, vp_ref, qg_ref, kg_ref, sk_ref, o_ref):
        low = _low_half()
        mask = _swa_mask(pl.program_id(0) == 0)
        qn = _half_rms(_stack_cols(q_ref[...]), low)[0] * qg_ref[...]
        k_lo, k_hi, v_lo, v_hi, _ = _swa_keys(kp_ref, kc_ref, vp_ref, vc_ref, kg_ref[...], low)
        p_lo, _ = _swa_probs(qn, k_lo, _sink_cols(sk_ref, 0), mask)
        p_hi, _ = _swa_probs(qn, k_hi, _sink_cols(sk_ref, 1), mask)
        o = (_dot(p_lo, v_lo, BNN) + _dot(p_hi, v_hi, BNN)).astype(BF16)
        for c in range(Q_COLS):
            o_ref[:, c * LANES:(c + 1) * LANES] = _col_of(o, c)

    q_gain, k_gain = jnp.tile(q_gain, (1, 2)), jnp.tile(k_gain, (1, 2))
    cur = lambda w, off: pl.BlockSpec((BLK, w), lambda i: (i, off // w))
    prev = lambda w, off: pl.BlockSpec((BLK, w), lambda i: (jnp.maximum(i - 1, 0), off // w))
    small = lambda n: pl.BlockSpec((1, 2 * n), lambda i: (0, 0))
    return _pcall(
        body, grid=(nb,),
        in_specs=[cur(BW, OFF_QB), cur(KVW, OFF_KB), prev(KVW, OFF_KB), cur(KVW, OFF_VB), prev(KVW, OFF_VB),
                  small(B_HD), small(B_HD), pl.BlockSpec(memory_space=pltpu.SMEM)],
        out_specs=[pl.BlockSpec((BLK, BW), lambda i: (i, 0))],
        out_shape=[jax.ShapeDtypeStruct((T, BW), BF16)], scratch_shapes=[], name="swa_fwd", semantics=("parallel",),
        args=[proj, proj, proj, proj, proj, q_gain, k_gain, sinks], job=job)


def _swa_bwd(proj, dout, q_gain, k_gain, sinks, job=None):
    T = proj.shape[0]
    nb = T // BLK
    W = BW + 2 * KVW

    def body(q_ref, kc_ref, kp_ref, vc_ref, vp_ref, do_ref, qg_ref, kg_ref, sk_ref,
             dq_ref, dkv_ref, pqg_ref, pkg_ref, psk_ref, dkn_c, dv_c):
        i = pl.program_id(0)
        live = i < nb
        low = _low_half()
        high = jnp.logical_not(low)
        qg, kg = qg_ref[...], kg_ref[...]
        mask = _swa_mask(i == 0)
        lane = lax.broadcasted_iota(jnp.int32, (1, LANES), 1)
        scale = B_HD ** -0.5

        @pl.when(i == 0)
        def _():
            dkn_c[...] = jnp.zeros_like(dkn_c)
            dv_c[...] = jnp.zeros_like(dv_c)

        q_hat, q_r = _half_rms(_stack_cols(q_ref[...]), low)
        qn = q_hat * qg
        k_lo, k_hi, v_lo, v_hi, hats = _swa_keys(kp_ref, kc_ref, vp_ref, vc_ref, kg, low)
        do = _stack_cols(do_ref[...])
        dqn = jnp.zeros((KV_HEADS, 2 * BLK, LANES), F32)
        acc_sk = jnp.zeros((1, LANES), F32)
        dk_parts, dv_parts = [], []
        for hi, (k_h, v_h) in enumerate(((k_lo, v_lo), (k_hi, v_hi))):
            p, ps = _swa_probs(qn, k_h, _sink_cols(sk_ref, hi), mask)
            dp = _dot(do, v_h, BNT)
            delta = jnp.sum(p * dp, axis=-1, keepdims=True)
            ds = p * (dp - delta) * scale
            dqn = dqn + _dot(ds, k_h, BNN)
            dk_parts.append(_dot(ds, qn, BTN))
            dv_parts.append(_dot(p, do, BTN))
            t = ps * delta
            for hk in range(KV_HEADS):
                for rows in range(2):
                    h = GROUP * hk + 2 * rows + hi
                    acc_sk = acc_sk + jnp.where(
                        lane == h, -jnp.sum(t[hk, rows * BLK:(rows + 1) * BLK], axis=0, keepdims=True), 0.0)
        dqh = dqn * qg
        dq = (q_r * (dqh - q_hat * (_half_sum(dqh * q_hat, low) * (1.0 / B_HD)))).astype(BF16)
        for c in range(Q_COLS):
            dq_ref[:, c * LANES:(c + 1) * LANES] = _col_of(dq, c)
        acc_qg = _fold_halves(_fold8((dqn * q_hat).reshape(KV_HEADS * 2 * BLK, LANES)), low)

        def native(parts, j):
            lo_arr, hi_arr = parts
            a, b = 2 * j, 2 * j + 1
            return (jnp.where(low, lo_arr[a], 0.0) + pltpu.roll(jnp.where(high, hi_arr[a], 0.0), B_HD, 1)
                    + jnp.where(high, hi_arr[b], 0.0) + pltpu.roll(jnp.where(low, lo_arr[b], 0.0), B_HD, 1))

        acc_kg = jnp.zeros((8, LANES), F32)
        for j in range(KVW // LANES):
            cs = slice(j * LANES, (j + 1) * LANES)
            dkn = jnp.where(live, native(dk_parts, j), 0.0)
            dvc = jnp.where(live, native(dv_parts, j), 0.0)
            kp_hat, kp_r = hats[j][0][:BLK], hats[j][1][:BLK]
            dkn_prev = dkn_c[:, cs] + dkn[:BLK]
            dv_prev = dv_c[:, cs] + dvc[:BLK]
            acc_kg = acc_kg + _fold8(dkn_prev * kp_hat)
            dkh = dkn_prev * kg
            dkv_ref[:, cs] = (kp_r * (dkh - kp_hat * (_half_sum(dkh * kp_hat, low) * (1.0 / B_HD)))).astype(BF16)
            dkv_ref[:, KVW + j * LANES:KVW + (j + 1) * LANES] = dv_prev.astype(BF16)
            dkn_c[:, cs] = dkn[BLK:]
            dv_c[:, cs] = dvc[BLK:]
        keep = jnp.where(i > 0, 1.0, 0.0)
        pqg_ref[...] = jnp.where(live, acc_qg, 0.0)
        pkg_ref[...] = _fold_halves(acc_kg, low) * keep
        psk_ref[...] = jnp.broadcast_to(jnp.where(live, acc_sk, 0.0), (8, LANES)) * (
            lax.broadcasted_iota(jnp.int32, (8, LANES), 0) == 0).astype(F32)

    q_gain, k_gain = jnp.tile(q_gain, (1, 2)), jnp.tile(k_gain, (1, 2))
    last = nb - 1
    cur = lambda w, off: pl.BlockSpec((BLK, w), lambda i: (jnp.minimum(i, last), off // w))
    prev = lambda w, off: pl.BlockSpec((BLK, w), lambda i: (jnp.maximum(i - 1, 0), off // w))
    small = lambda n: pl.BlockSpec((1, 2 * n), lambda i: (0, 0))
    part = pl.BlockSpec((8, 128), lambda i: (i, 0))
    p_shape = jax.ShapeDtypeStruct(((nb + 1) * 8, 128), F32)
    return _pcall(
        body, grid=(nb + 1,),
        in_specs=[cur(BW, OFF_QB), cur(KVW, OFF_KB), prev(KVW, OFF_KB), cur(KVW, OFF_VB), prev(KVW, OFF_VB),
                  pl.BlockSpec((BLK, BW), lambda i: (jnp.minimum(i, last), 0)), small(B_HD), small(B_HD),
                  pl.BlockSpec(memory_space=pltpu.SMEM)],
        out_specs=[pl.BlockSpec((BLK, BW), lambda i: (i, 0)),
                   pl.BlockSpec((BLK, 2 * KVW), lambda i: (jnp.maximum(i - 1, 0), 0)), part, part, part],
        out_shape=[jax.ShapeDtypeStruct((T + BLK, BW), BF16), jax.ShapeDtypeStruct((T, 2 * KVW), BF16),
                   p_shape, p_shape, p_shape],
        scratch_shapes=[pltpu.VMEM((BLK, KVW), F32), pltpu.VMEM((BLK, KVW), F32)], name="swa_bwd",
        semantics=("arbitrary",), args=[proj, proj, proj, proj, proj, dout, q_gain, k_gain, sinks], job=job)


def _branch_merge(ya_pre, attn, wa_t, wb_t, proj, tm, tn, job=None):
    T = ya_pre.shape[0]

    def body(a_ref, b_ref, wa_ref, wb_ref, ga_ref, gb_ref, ya_ref, yb_ref, mg_ref):
        ya = lax.dot_general(a_ref[...], wa_ref[...], NT, preferred_element_type=F32)
        yb = lax.dot_general(b_ref[...], wb_ref[...], NT, preferred_element_type=F32)
        ya_ref[...] = ya.astype(BF16)
        yb_ref[...] = yb.astype(BF16)
        mg_ref[...] = (_sigmoid(ga_ref[...]) * ya + _sigmoid(gb_ref[...]) * yb).astype(BF16)

    o_spec = pl.BlockSpec((tm, tn), lambda i, j: (i, j))
    o_shape = jax.ShapeDtypeStruct((T, D), BF16)
    return _pcall(
        body, grid=(T // tm, D // tn),
        in_specs=[pl.BlockSpec((tm, AW), lambda i, j: (i, 0)), pl.BlockSpec((tm, BW), lambda i, j: (i, 0)),
                  pl.BlockSpec((tn, AW), lambda i, j: (j, 0)), pl.BlockSpec((tn, BW), lambda i, j: (j, 0)),
                  pl.BlockSpec((tm, tn), lambda i, j: (i, OFF_GTA // tn + j)),
                  pl.BlockSpec((tm, tn), lambda i, j: (i, OFF_GTB // tn + j))],
        out_specs=[o_spec, o_spec, o_spec], out_shape=[o_shape, o_shape, o_shape], scratch_shapes=[], name="branch_merge",
        semantics=("parallel", "parallel"), args=[ya_pre, attn, wa_t, wb_t, proj, proj], job=job)


def _ij(i, j, k):
    return (i, j)


def _local_step(x, tgt, mod, g1, g2, lbl, og, qg, kg, sk, shards, me, c_arr):
    win_s, wa_s, wb_s, wout_s, wmi_s, wmo_s = shards
    T = x.shape[0]
    tm, tr, tt = min(1024, T), min(256, T), min(512, T)
    tk_t = min(1024, T)
    tn = 512
    sh1, sc1, gt1, sh2, sc2, gt2 = (mod[:, i * D:(i + 1) * D] for i in range(N_MOD))
    nI = T // tm
    blk = (tm, tn)
    part = lambda: ((nI * 8, D), F32, (8, tn), _ij)
    vec_j = ((1, tn), lambda i, j, k: (0, j))

    h = _rms_mod_fwd("rms1_fwd", x, g1, sc1, sh1, tr)

    def epi_store(acc, ex, ou):
        ou[0][...] = acc.astype(ou[0].dtype)

    tm2 = min(2048, T)
    blk2 = (tm2, tn)

    full = lambda s: (0, s.shape[0])
    (win_t,) = _run_job("gather_w_in", _gather_job([win_s]))
    (proj,), (wa_t, wb_t, w_out, wmi_part) = _mm(
        "in_proj", "nt", [(h, D)], win_t, T, IN_W, D, tm2, tn, D, [], [((T, IN_W), F32, blk2, _ij)], epi_store,
        job=_gather_job([wa_s, wb_s, wout_s, wmi_s], rows=[full(wa_s), full(wb_s), full(wout_s), (0, MI_CUT)]))
    (ya_pre, st), (wmi_t,) = _hgrn_fwd(
        proj, lbl, og, tt, job=_gather_job([wmi_s], rows=[(MI_CUT, wmi_s.shape[0])], into=[wmi_part]))
    (attn,), (wmo_part,) = _swa_fwd(proj, qg, kg, sk, job=_gather_job([wmo_s], rows=[(0, MO_CUTS[0])]))
    (ya, yb, merged), (wmo_part,) = _branch_merge(
        ya_pre, attn, wa_t, wb_t, proj, tm, tn, job=_gather_job([wmo_s], rows=[MO_CUTS], into=[wmo_part]))

    def epi_res1(acc, ex, ou):
        x_ref, gt_ref = ex
        ou[0][...] = acc.astype(BF16)
        ou[1][...] = x_ref[...] + gt_ref[...] * acc

    (mo, x1), (w_mo,) = _mm(
        "out_proj", "nn", [(merged, D)], w_out, T, D, D, tm, tn, D, [(x, blk, _ij), (gt1, *vec_j)],
        [((T, D), BF16, blk, _ij), ((T, D), F32, blk, _ij)], epi_res1,
        job=_gather_job([wmo_s], rows=[(MO_CUTS[1], wmo_s.shape[0])], into=[wmo_part]))
    h2 = _rms_mod_fwd("rms2_fwd", x1, g2, sc2, sh2, tr)

    def epi_relu2(acc, ex, ou):
        r = jnp.maximum(acc, 0.0)
        ou[0][...] = r.astype(BF16)
        ou[1][...] = (r * r).astype(BF16)

    r, a = _mm("mlp_in", "nt", [(h2, D)], wmi_t, T, HID, D, tm2, tn, D, [],
               [((T, HID), BF16, blk2, _ij), ((T, HID), BF16, blk2, _ij)], epi_relu2)

    def epi_loss(acc, ex, ou):
        x1_ref, t_ref, gt_ref = ex
        e = x1_ref[...] + gt_ref[...] * acc - t_ref[...]
        dy = e * (1.0 / D)
        ou[0][...] = dy
        ou[1][...] = (gt_ref[...] * dy).astype(BF16)
        ou[2][...] = _fold8(e * e) * (0.5 / D)
        ou[3][...] = _fold8(dy * acc)

    wide = (tm, 1024)
    part_w = ((nI * 8, D), F32, (8, 1024), _ij)
    dy, dz, p_loss, p_gt2 = _mm(
        "mlp_out", "nn", [(a, HID)], w_mo, T, D, HID, tm, 1024, 1024,
        [(x1, wide, _ij), (tgt, wide, _ij), (gt2, (1, 1024), lambda i, j, k: (0, j))],
        [((T, D), F32, wide, _ij), ((T, D), BF16, wide, _ij), part_w, part_w], epi_loss)

    def epi_du(acc, ex, ou):
        ou[0][...] = (acc * (2.0 * ex[0][...].astype(F32))).astype(BF16)

    (du,) = _mm("mlp_out_dx", "nt", [(dz, D)], w_mo, T, HID, D, tm2, tn, D, [(r, blk2, _ij)],
                [((T, HID), BF16, blk2, _ij)], epi_du)
    gblk = (1024, 1024)
    gwide = (1024, D)
    pair_sum = lambda nm, g, r1: _pair_sum("pair_sum_" + nm, g, r1, c_arr, _sum_rows(r1.shape[1]))
    (g_mo,) = _mm("mlp_out_dw", "tn", [(a, HID)], dz, HID, D, T, 1024, D, tk_t, [], [((HID, D), BF16, gwide, _ij)], epi_store)
    (dh2,), (r1_mo,) = _mm("mlp_in_dx", "nn", [(du, HID)], wmi_t, T, D, HID, tm, 1024, 1024, [],
                           [((T, D), F32, (tm, 1024), _ij)], epi_store, job=_pair_job([g_mo]))
    dx1, p_sh2, p_sc2, p_g2, dmo, p_gt1 = _rms_mod_bwd("rms2_bwd", dh2, x1, g2, sc2, dy, tr, gate=gt1, mo=mo)
    s_mo = pair_sum("mlp_out", g_mo, r1_mo)
    tm_row = min(512, T)
    (g_mi,), (r2_mo,) = _mm("mlp_in_dw", "tn", [(du, HID)], h2, HID, D, T, 1024, D, tk_t, [],
                            [((HID, D), BF16, gwide, _ij)], epi_store, job=_chip_job([s_mo]))

    def epi_gates(acc, ex, ou):
        ya_ref, yb_ref, ga_ref, gb_ref = ex
        sa, sb = _sigmoid(ga_ref[...]), _sigmoid(gb_ref[...])
        ou[0][...] = (acc * sa).astype(BF16)
        ou[1][...] = (acc * sb).astype(BF16)
        ou[2][...] = (acc * ya_ref[...].astype(F32) * (sa * (1.0 - sa))).astype(BF16)
        ou[3][...] = (acc * yb_ref[...].astype(F32) * (sb * (1.0 - sb))).astype(BF16)

    o_bf = ((T, D), BF16, blk, _ij)
    (dya, dyb, dga, dgb), (r1_mi,) = _mm(
        "out_proj_dx", "nt", [(dmo, D)], w_out, T, D, D, tm, tn, D,
        [(ya, blk, _ij), (yb, blk, _ij), (proj, blk, lambda i, j, k: (i, OFF_GTA // tn + j)),
         (proj, blk, lambda i, j, k: (i, OFF_GTB // tn + j))], [o_bf, o_bf, o_bf, o_bf], epi_gates,
        job=_pair_job([g_mi]))
    s_mi = pair_sum("mlp_in", g_mi, r1_mi)
    (g_out,) = _mm("out_proj_dw", "tn", [(merged, D)], dmo, D, D, T, 1024, 1024, tk_t, [], [((D, D), BF16, gblk, _ij)], epi_store)
    (dya_pre,) = _mm("branch_a_dx", "nn", [(dya, D)], wa_t, T, AW, D, tm, tn, D, [], [((T, AW), F32, blk, _ij)], epi_store)
    (dattn,) = _mm("branch_b_dx", "nn", [(dyb, D)], wb_t, T, BW, D, tm, tn, D, [], [((T, BW), F32, blk, _ij)], epi_store)
    (g_a,) = _mm("branch_a_dw", "tn", [(dya, D)], ya_pre, D, AW, T, 1024, 1024, tk_t, [], [((D, AW), BF16, gblk, _ij)], epi_store)
    (g_b,) = _mm("branch_b_dw", "tn", [(dyb, D)], attn, D, BW, T, 1024, 1024, tk_t, [], [((D, BW), BF16, gblk, _ij)], epi_store)
    (dqa, dfa, dia, dgg, p_lb, p_og), (r2_mi,) = _hgrn_bwd(proj, st, dya_pre, lbl, og, tt, job=_chip_job([s_mi]))
    (dqb, dkv, p_qg, p_kg, p_sk), (r1_out, r1_a, r1_b) = _swa_bwd(proj, dattn, qg, kg, sk, job=_pair_job([g_out, g_a, g_b]))
    s_out, s_a, s_b = pair_sum("out", g_out, r1_out), pair_sum("branch_a", g_a, r1_a), pair_sum("branch_b", g_b, r1_b)
    pieces = [(dqa, AW), (dfa, AW), (dia, AW), (dgg, AW), (dqb, BW), (dkv, 2 * KVW), (dga, D), (dgb, D)]
    (g_in,), (r2_out, r2_a, r2_b) = _mm(
        "in_proj_dw", "tn", pieces, h, IN_W, D, T, 512, D, tk_t, [], [((IN_W, D), BF16, (512, D), _ij)], epi_store,
        job=_chip_job([s_out, s_a, s_b]))
    (r1_in,) = _run_job("pair_w_in", _pair_job([g_in]))
    s_in = pair_sum("in", g_in, r1_in)
    extras, outs, epi = _rms_mod_bwd_epilogue(x, g1, sc1, dx1, tm_row)
    (dx, p_sh1, p_sc1, p_g1), (r2_in,) = _mm(
        "in_proj_dx", "nn", pieces, win_t, T, D, IN_W, tm_row, D, 512, extras, outs, epi, job=_chip_job([s_in]),
        acc_as_ref=True)

    partials = dict(sh1=p_sh1, sc1=p_sc1, gt1=p_gt1, sh2=p_sh2, sc2=p_sc2, gt2=p_gt2, g1=p_g1, g2=p_g2,
                    lb=p_lb, og=p_og, qg=p_qg, kg=p_kg, sk=p_sk, loss=p_loss)
    sums = dict(w_in=(s_in, r2_in), w_branch_a=(s_a, r2_a), w_branch_b=(s_b, r2_b), w_out=(s_out, r2_out),
                w_mlp_in=(s_mi, r2_mi), w_mlp_out=(s_mo, r2_mo))
    return dx, sums, partials


def _exchange_slots(buf, send_sems, recv_sems):
    me = _mesh_pos()
    mine = buf.at[_index(me)]
    sends = []
    for k in range(1, N_DEV):
        cp = pltpu.make_async_remote_copy(src_ref=mine, dst_ref=mine, send_sem=send_sems.at[k - 1],
                                          recv_sem=recv_sems.at[k - 1], device_id=_flip(me, k), device_id_type=MESH)
        cp.start()
        sends.append(cp)
    for k in range(1, N_DEV):
        theirs = buf.at[_index(_flip(me, k))]
        pltpu.make_async_remote_copy(src_ref=theirs, dst_ref=theirs, send_sem=send_sems.at[k - 1],
                                     recv_sem=recv_sems.at[k - 1], device_id=_flip(me, k), device_id_type=MESH).wait_recv()
    for cp in sends:
        cp.wait_send()


ADA_W = N_MOD * D // N_DEV


def _ada_mod(c, w_ada, b_shard):
    def body(c_ref, w_ref, b_ref, mod_ref, sc_ref, cbuf, mbuf, s1, r1, s2, r2):
        me = _index(_mesh_pos())
        cbuf[me] = c_ref[...]
        _exchange_slots(cbuf, s1, r1)
        row = lax.broadcasted_iota(jnp.int32, (N_DEV, D), 0)
        call = jnp.zeros((N_DEV, D), F32)
        for d in range(N_DEV):
            call = jnp.where(row == d, cbuf[d], call)
        sc = call * _sigmoid(call)
        sc_ref[...] = sc
        mbuf[me] = _dot(sc, w_ref[...]) + b_ref[...]
        _exchange_slots(mbuf, s2, r2)
        for s in range(N_DEV):
            mod_ref[:, s * ADA_W:(s + 1) * ADA_W] = mbuf[s, pl.ds(me, 1), :]

    return pl.pallas_call(
        body, in_specs=[_VMEM, _VMEM, _VMEM], out_specs=[_VMEM, _VMEM],
        out_shape=[jax.ShapeDtypeStruct((1, N_MOD * D), F32), jax.ShapeDtypeStruct((N_DEV, D), F32)],
        scratch_shapes=[pltpu.VMEM((N_DEV, 1, D), F32), pltpu.VMEM((N_DEV, N_DEV, ADA_W), F32),
                        _SEMS(N_DEV - 1), _SEMS(N_DEV - 1), _SEMS(N_DEV - 1), _SEMS(N_DEV - 1)],
        name="ada_mod", compiler_params=pltpu.CompilerParams(vmem_limit_bytes=VMEM_LIMIT),
    )(c, w_ada, b_shard)


SMALL_SEGS = (("b_ada", N_MOD * D), ("norm1_gain", D), ("norm2_gain", D), ("lb0", AW), ("lb1", AW),
              ("hgrn_o_gain", AW), ("q_norm_gain", 128), ("k_norm_gain", 128), ("sinks", 128))
SMALL_W = sum(w for _, w in SMALL_SEGS)
X_SEGS = (("sh1", D), ("sc1", D), ("gt1", D), ("sh2", D), ("sc2", D), ("gt2", D), ("g1", D), ("g2", D),
          ("lb", AW), ("og", AW), ("qg", 128), ("kg", 128), ("sk", 128), ("loss", 128))
X_W = sum(w for _, w in X_SEGS)


def _offsets(segs):
    out, o = {}, 0
    for name, w in segs:
        out[name] = (o, w)
        o += w
    return out


def _small_reduce(parts, lb_logits):
    xo, so = _offsets(X_SEGS), _offsets(SMALL_SEGS)
    names = [nm for nm, _ in X_SEGS]

    def body(*refs):
        p_refs = dict(zip(names, refs[:len(names)]))
        lbl_ref, allx, gs_ref, loss_ref, send_sems, recv_sems = refs[len(names):]
        me = _index(_mesh_pos())
        for nm, (o, w) in xo.items():
            if nm == "loss":
                allx[me, :, o:o + w] = jnp.broadcast_to(jnp.sum(p_refs[nm][...]), (1, w))
            else:
                allx[me, :, o:o + w] = jnp.sum(p_refs[nm][...], axis=0, keepdims=True)
        _exchange_slots(allx, send_sems, recv_sems)
        tot = allx[0]
        for d in range(1, N_DEV):
            tot = tot + allx[d]
        seg = lambda nm: tot[:, xo[nm][0]:xo[nm][0] + xo[nm][1]]

        def put(nm, v):
            gs_ref[:, so[nm][0]:so[nm][0] + so[nm][1]] = v

        put("b_ada", tot[:, 0:N_MOD * D])
        put("norm1_gain", seg("g1"))
        put("norm2_gain", seg("g2"))
        lbl = lbl_ref[...]
        lb = _sigmoid(lbl[0:1, :] - lbl[1:2, :])
        dl0 = seg("lb") * lb * (1.0 - lb)
        put("lb0", dl0)
        put("lb1", -dl0)
        put("hgrn_o_gain", seg("og"))
        put("q_norm_gain", seg("qg"))
        put("k_norm_gain", seg("kg"))
        put("sinks", seg("sk"))
        loss_ref[...] = seg("loss")

    return pl.pallas_call(
        body, in_specs=[_VMEM] * (len(names) + 1), out_specs=[_VMEM, _VMEM, _VMEM],
        out_shape=[jax.ShapeDtypeStruct((N_DEV, 1, X_W), F32), jax.ShapeDtypeStruct((1, SMALL_W), F32),
                   jax.ShapeDtypeStruct((1, 128), F32)],
        scratch_shapes=[_SEMS(N_DEV - 1), _SEMS(N_DEV - 1)], name="small_reduce",
        compiler_params=pltpu.CompilerParams(vmem_limit_bytes=VMEM_LIMIT),
    )(*[parts[nm] for nm in names], lb_logits)


def _adamw_math(w, g, m, v):
    m = B1 * m + (1.0 - B1) * g
    v = B2 * v + (1.0 - B2) * (g * g)
    m_hat = m / (1.0 - B1 ** STEP)
    v_hat = v / (1.0 - B2 ** STEP)
    return -LR * (m_hat / (jnp.sqrt(v_hat) + ADAM_EPS) + WD * w), m, v


def _sum_rows(rs):
    return 256 if rs % 256 == 0 else rs // 2


def _pair_sum(name, g, recv, c_arr, tr):
    _, rs, cols = recv.shape
    blk = (1, tr, cols)

    def body(c_ref, g_ref, r_ref, o_ref):
        o_ref[...] = (g_ref[...].astype(F32) + r_ref[...].astype(F32)).astype(BF16)

    grid_spec = pltpu.PrefetchScalarGridSpec(
        num_scalar_prefetch=1, grid=(4, rs // tr),
        in_specs=[pl.BlockSpec(blk, lambda q, i, c: (2 * q + c[0], i, 0)), pl.BlockSpec(blk, lambda q, i, c: (q, i, 0))],
        out_specs=pl.BlockSpec(blk, lambda q, i, c: (q, i, 0)))
    return pl.pallas_call(body, grid_spec=grid_spec, out_shape=jax.ShapeDtypeStruct((4, rs, cols), BF16), name=name,
                          compiler_params=_params(("parallel", "parallel")))(c_arr, g.reshape(N_DEV, rs, cols), recv)


def _final_sum(name, sums, recv, q_arr, tr):
    _, rs, cols = sums.shape

    def body(q_ref, s_ref, r_ref, o_ref):
        o_ref[...] = ((s_ref[0].astype(F32) + r_ref[0].astype(F32)) + r_ref[1].astype(F32)) + r_ref[2].astype(F32)

    grid_spec = pltpu.PrefetchScalarGridSpec(
        num_scalar_prefetch=1, grid=(rs // tr,),
        in_specs=[pl.BlockSpec((1, tr, cols), lambda i, q: (q[0], i, 0)), pl.BlockSpec((3, tr, cols), lambda i, q: (0, i, 0))],
        out_specs=pl.BlockSpec((tr, cols), lambda i, q: (i, 0)))
    return pl.pallas_call(body, grid_spec=grid_spec, out_shape=jax.ShapeDtypeStruct((rs, cols), F32), name=name,
                          compiler_params=_params(("parallel",)))(q_arr, sums, recv)


def _adamw(name, w, g, m, v, tr):
    rows, cols = w.shape

    def body(w_ref, g_ref, m_ref, v_ref, d_ref, nm_ref, nv_ref):
        d_ref[...], nm_ref[...], nv_ref[...] = _adamw_math(w_ref[...], g_ref[...], m_ref[...], v_ref[...])

    spec = pl.BlockSpec((tr, cols), lambda i: (i, 0))
    shape = jax.ShapeDtypeStruct((rows, cols), F32)
    return pl.pallas_call(
        body, grid=(rows // tr,), in_specs=[spec] * 4, out_specs=[spec] * 3, out_shape=[shape] * 3, name=name,
        compiler_params=_params(("parallel",)),
    )(w, g, m, v)


def _ada_update(sc_t, dmod_cols, w, m, v, tr):
    rows, cols = w.shape

    def body(s_ref, d_ref, w_ref, m_ref, v_ref, g_ref, dl_ref, nm_ref, nv_ref):
        g = jnp.dot(s_ref[...], d_ref[...], precision=lax.Precision.HIGHEST, preferred_element_type=F32)
        g_ref[...] = g
        dl_ref[...], nm_ref[...], nv_ref[...] = _adamw_math(w_ref[...], g, m_ref[...], v_ref[...])

    spec = pl.BlockSpec((tr, cols), lambda i: (i, 0))
    shape = jax.ShapeDtypeStruct((rows, cols), F32)
    return pl.pallas_call(
        body, grid=(rows // tr,),
        in_specs=[pl.BlockSpec((tr, N_DEV), lambda i: (i, 0)), pl.BlockSpec((N_DEV, cols), lambda i: (0, 0)), spec, spec, spec],
        out_specs=[spec] * 4, out_shape=[shape] * 4, name="ada_update", compiler_params=_params(("parallel",)),
    )(sc_t, dmod_cols, w, m, v)


BIG = ("w_in", "w_branch_a", "w_branch_b", "w_out", "w_mlp_in", "w_mlp_out")
COLUMN_SHARDED = ("w_in", "w_branch_a", "w_branch_b", "w_mlp_in")
WEIGHTS = ("w_ada", "b_ada", "norm1_gain", "w_in", "lb_logits", "hgrn_o_gain", "q_norm_gain", "k_norm_gain", "sinks",
           "w_branch_a", "w_branch_b", "w_out", "norm2_gain", "w_mlp_in", "w_mlp_out")


def _pack_small(p):
    lb = p["lb_logits"]
    src = dict(p, lb0=lb[0:1], lb1=lb[1:2])
    return jnp.concatenate([jnp.pad(src[nm], ((0, 0), (0, w - src[nm].shape[1]))) for nm, w in SMALL_SEGS], axis=1)


def _unpack_small(vec, shapes):
    so = _offsets(SMALL_SEGS)
    out = {}
    for nm, shp in shapes.items():
        if nm == "lb_logits":
            o = so["lb0"][0]
            out[nm] = vec[0, o:o + 2 * AW].reshape(2, AW)
        else:
            o = so[nm][0]
            out[nm] = vec[:, o:o + shp[1]]
    return out


def kernel(x, c, w_ada, b_ada, norm1_gain, w_in, lb_logits, hgrn_o_gain, q_norm_gain, k_norm_gain, sinks, w_branch_a, w_branch_b, w_out, norm2_gain, w_mlp_in, w_mlp_out, loss_target, m_w_ada, m_b_ada, m_norm1_gain, m_w_in, m_lb_logits, m_hgrn_o_gain, m_q_norm_gain, m_k_norm_gain, m_sinks, m_w_branch_a, m_w_branch_b, m_w_out, m_norm2_gain, m_w_mlp_in, m_w_mlp_out, v_w_ada, v_b_ada, v_norm1_gain, v_w_in, v_lb_logits, v_hgrn_o_gain, v_q_norm_gain, v_k_norm_gain, v_sinks, v_w_branch_a, v_w_branch_b, v_w_out, v_norm2_gain, v_w_mlp_in, v_w_mlp_out):
    w = dict(w_ada=w_ada, b_ada=b_ada, norm1_gain=norm1_gain, w_in=w_in, lb_logits=lb_logits, hgrn_o_gain=hgrn_o_gain,
             q_norm_gain=q_norm_gain, k_norm_gain=k_norm_gain, sinks=sinks, w_branch_a=w_branch_a, w_branch_b=w_branch_b,
             w_out=w_out, norm2_gain=norm2_gain, w_mlp_in=w_mlp_in, w_mlp_out=w_mlp_out)
    m = dict(w_ada=m_w_ada, b_ada=m_b_ada, norm1_gain=m_norm1_gain, w_in=m_w_in, lb_logits=m_lb_logits,
             hgrn_o_gain=m_hgrn_o_gain, q_norm_gain=m_q_norm_gain, k_norm_gain=m_k_norm_gain, sinks=m_sinks,
             w_branch_a=m_w_branch_a, w_branch_b=m_w_branch_b, w_out=m_w_out, norm2_gain=m_norm2_gain,
             w_mlp_in=m_w_mlp_in, w_mlp_out=m_w_mlp_out)
    v = dict(w_ada=v_w_ada, b_ada=v_b_ada, norm1_gain=v_norm1_gain, w_in=v_w_in, lb_logits=v_lb_logits,
             hgrn_o_gain=v_hgrn_o_gain, q_norm_gain=v_q_norm_gain, k_norm_gain=v_k_norm_gain, sinks=v_sinks,
             w_branch_a=v_w_branch_a, w_branch_b=v_w_branch_b, w_out=v_w_out, norm2_gain=v_norm2_gain,
             w_mlp_in=v_w_mlp_in, w_mlp_out=v_w_mlp_out)
    for d in (w, m, v):
        for nm in ("w_ada",) + BIG:
            d[nm] = d[nm][0]
    px, py, pc = _mesh_pos()
    me = _index((px, py, pc))
    c_arr = jnp.reshape(pc, (1,)).astype(jnp.int32)
    q_arr = jnp.reshape(2 * px + py, (1,)).astype(jnp.int32)

    shards = [(w[nm].T if nm in COLUMN_SHARDED else w[nm]).astype(BF16) for nm in BIG]
    b_shard = lax.dynamic_slice(b_ada, (0, me * ADA_W), (1, ADA_W))
    mod, sc_all = _ada_mod(c, w["w_ada"], b_shard)

    dx, sums, parts = _local_step(x[0], loss_target[0], mod, norm1_gain, norm2_gain, lb_logits, hgrn_o_gain,
                                  q_norm_gain, k_norm_gain, sinks, shards, me, c_arr)

    allx, g_small, loss = _small_reduce(parts, lb_logits)

    grad, delta, new_m, new_v = {}, {}, {}, {}
    for nm in BIG:
        s, r2 = sums[nm]
        rs = s.shape[1]
        g = _final_sum("sum_" + nm, s, r2, q_arr, _sum_rows(rs))
        g = g.T if nm in COLUMN_SHARDED else g
        rows = g.shape[0]
        grad[nm] = g
        delta[nm], new_m[nm], new_v[nm] = _adamw("adamw_" + nm, w[nm], g, m[nm], v[nm], 128 if rows % 128 == 0 else rows)

    dmod_cols = lax.dynamic_slice(allx[:, 0, :], (0, me * ADA_W), (N_DEV, ADA_W))
    grad["w_ada"], delta["w_ada"], new_m["w_ada"], new_v["w_ada"] = _ada_update(
        sc_all.T, dmod_cols, w["w_ada"], m["w_ada"], v["w_ada"], 256)

    small_names = [nm for nm in WEIGHTS if nm not in BIG and nm != "w_ada"]
    shapes = {nm: w[nm].shape for nm in small_names}
    ds, ms, vs = _adamw("adamw_small", _pack_small(w), g_small, _pack_small(m), _pack_small(v), 1)
    for dst, vec in ((grad, g_small), (delta, ds), (new_m, ms), (new_v, vs)):
        dst.update(_unpack_small(vec, shapes))

    def full(d, nm):
        return d[nm][None] if nm in BIG or nm == "w_ada" else d[nm]

    return (loss[0, 0], dx[None], *[full(grad, nm) for nm in WEIGHTS], *[full(delta, nm) for nm in WEIGHTS],
            *[full(new_m, nm) for nm in WEIGHTS], *[full(new_v, nm) for nm in WEIGHTS])
```

```python
import functools

import jax
import jax.numpy as jnp
import numpy as np
from jax import lax
from jax.experimental import pallas as pl
from jax.experimental.pallas import tpu as pltpu

F32 = jnp.float32
BF16 = jnp.bfloat16
MESH = pl.DeviceIdType.MESH

N_DEV = 8
D = 2048
A_HEADS, A_HD, CHUNK = 8, 128, 64
AW = A_HEADS * A_HD
Q_HEADS, KV_HEADS, GROUP, B_HD, BLK = 16, 4, 4, 64, 128
BW = Q_HEADS * B_HD
KVW = KV_HEADS * B_HD
HID = 4 * D
IN_W = 4 * AW + BW + 2 * KVW + 2 * D
OFF_QA, OFF_FA, OFF_IA, OFF_GA = 0, AW, 2 * AW, 3 * AW
OFF_QB = 4 * AW
OFF_KB = OFF_QB + BW
OFF_VB = OFF_KB + KVW
OFF_GTA = OFF_VB + KVW
OFF_GTB = OFF_GTA + D
N_MOD = 6
EPS = 1e-6
LR, B1, B2, ADAM_EPS, WD, STEP = 1e-3, 0.9, 0.999, 1e-8, 0.01, 10
NEG = -1e30

VMEM_LIMIT = 56 * 1024 * 1024
MI_CUTS = (224, 464, 784)

NN = (((1,), (0,)), ((), ()))
NT = (((1,), (1,)), ((), ()))
TN = (((0,), (0,)), ((), ()))
BNN = (((2,), (1,)), ((0,), (0,)))
BNT = (((2,), (2,)), ((0,), (0,)))
BTN = (((1,), (1,)), ((0,), (0,)))


def _dot(a, b, dims=NN):
    return lax.dot_general(a.astype(BF16), b.astype(BF16), dims, preferred_element_type=F32)


def _params(sem):
    return pltpu.CompilerParams(dimension_semantics=sem, vmem_limit_bytes=VMEM_LIMIT)


def _sigmoid(x):
    return 1.0 / (1.0 + jnp.exp(-x))


def _fold8(v):
    r, n = v.shape
    return jnp.sum(v.reshape(r // 8, 8, n), axis=0)


_VMEM = pl.BlockSpec(memory_space=pltpu.VMEM)
_ANY = pl.BlockSpec(memory_space=pl.ANY)
_SEMS = lambda n: pltpu.SemaphoreType.DMA((n,))


def _mesh_pos():
    return lax.axis_index("x"), lax.axis_index("y"), lax.axis_index("c")


def _flip(pos, k):
    return tuple(1 - p if (k >> s) & 1 else p for p, s in zip(pos, (2, 1, 0)))


def _index(pos):
    return 4 * pos[0] + 2 * pos[1] + pos[2]


class _Job:
    def __init__(self, ins, out_shape, sems, start, finish, aliases=None):
        self.ins, self.out_shape, self.sems, self.start, self.finish = list(ins), list(out_shape), list(sems), start, finish
        self.aliases = dict(aliases or {})


def _pcall(body, *, grid, in_specs, out_specs, out_shape, scratch_shapes, name, semantics, args, job=None):
    if job is None:
        outs = pl.pallas_call(body, grid=grid, in_specs=in_specs, out_specs=out_specs, out_shape=out_shape,
                              scratch_shapes=scratch_shapes, name=name, compiler_params=_params(semantics))(*args)
        return list(outs), []
    n_in, n_out, n_scr = len(in_specs), len(out_specs), len(scratch_shapes)
    j_in, j_out = len(job.ins), len(job.out_shape)
    steps = tuple(grid)

    def carrier(*refs):
        o = 0
        main_in, o = refs[o:o + n_in], o + n_in
        job_in, o = refs[o:o + j_in], o + j_in
        main_out, o = refs[o:o + n_out], o + n_out
        job_out, o = refs[o:o + j_out], o + j_out
        main_scr, job_sems = refs[o:o + n_scr], refs[o + n_scr:]
        ids = [pl.program_id(a) for a in range(len(steps))]
        first = functools.reduce(lambda p, q: p & q, [i == 0 for i in ids])
        last = functools.reduce(lambda p, q: p & q, [i == s - 1 for i, s in zip(ids, steps)])

        @pl.when(first)
        def _():
            job.start(job_in, job_out, job_sems)

        body(*main_in, *main_out, *main_scr)

        @pl.when(last)
        def _():
            job.finish(job_in, job_out, job_sems)

    outs = pl.pallas_call(
        carrier, grid=grid, in_specs=list(in_specs) + [_ANY] * j_in, out_specs=list(out_specs) + [_ANY] * j_out,
        out_shape=list(out_shape) + job.out_shape, scratch_shapes=list(scratch_shapes) + job.sems, name=name,
        input_output_aliases={n_in + i: n_out + o for i, o in job.aliases.items()},
        compiler_params=_params(("arbitrary",) * len(steps)),
    )(*args, *job.ins)
    return list(outs[:n_out]), list(outs[n_out:])


def _run_job(name, job):
    j_in, j_out = len(job.ins), len(job.out_shape)

    def body(*refs):
        ins, outs, sems = refs[:j_in], refs[j_in:j_in + j_out], refs[j_in + j_out:]
        job.start(ins, outs, sems)
        job.finish(ins, outs, sems)

    return list(pl.pallas_call(body, in_specs=[_ANY] * j_in, out_specs=[_ANY] * j_out, out_shape=job.out_shape,
                               scratch_shapes=job.sems, name=name,
                               input_output_aliases=job.aliases)(*job.ins))


def _gather_job(shards, rows=None, into=None):
    n = len(shards)
    rows = rows or [(0, s.shape[0]) for s in shards]
    into = into or [None] * n
    olds, aliases = [], {}
    for a, buf in enumerate(into):
        if buf is not None:
            aliases[n + len(olds)] = a
            olds.append(buf)

    def copies(ins, outs, sems):
        send_sems, recv_sems, local_sems = sems
        x, y, c = _mesh_pos()
        me, sib = (x, y, c), (x, y, 1 - c)
        chips = [(1 - x, y), (x, 1 - y), (1 - x, 1 - y)]

        def part(a, p):
            rs, (r0, r1) = shards[a].shape[0], rows[a]
            return outs[a].at[pl.ds(_index(p) * rs + r0, r1 - r0), :]

        own = lambda a: ins[a].at[pl.ds(rows[a][0], rows[a][1] - rows[a][0]), :]

        def copy(a, k, block, to, src=None):
            return pltpu.make_async_remote_copy(
                src_ref=part(a, block) if src is None else src, dst_ref=part(a, block),
                send_sem=send_sems.at[7 * a + k], recv_sem=recv_sems.at[7 * a + k], device_id=to, device_id_type=MESH)

        mine = [pltpu.make_async_copy(own(a), part(a, me), local_sems.at[a]) for a in range(n)]
        first = []
        for a in range(n):
            first.append(copy(a, 0, me, sib, src=own(a)))
            first += [copy(a, 1 + j, me, (*chip, c), src=own(a)) for j, chip in enumerate(chips)]
        return me, sib, c, chips, copy, mine, first

    def start(ins, outs, sems):
        *_, mine, first = copies(ins, outs, sems)
        for cp in mine + first:
            cp.start()

    def finish(ins, outs, sems):
        me, sib, c, chips, copy, mine, first = copies(ins, outs, sems)
        passed = []
        for j, chip in enumerate(chips):
            for a in range(n):
                copy(a, 1 + j, (*chip, c), me).wait_recv()
                cp = copy(a, 4 + j, (*chip, c), sib)
                cp.start()
                passed.append(cp)
        for a in range(n):
            copy(a, 0, sib, me).wait_recv()
            for j, chip in enumerate(chips):
                copy(a, 4 + j, (*chip, 1 - c), me).wait_recv()
        for cp in first + passed:
            cp.wait_send()
        for cp in mine:
            cp.wait()

    return _Job(list(shards) + olds, [jax.ShapeDtypeStruct((N_DEV * s.shape[0], s.shape[1]), s.dtype) for s in shards],
                [_SEMS(7 * n), _SEMS(7 * n), _SEMS(n)], start, finish, aliases)


def _pair_job(grads):
    n = len(grads)

    def copies(ins, outs, sems):
        send_sems, recv_sems = sems
        x, y, c = _mesh_pos()
        out = []
        for a in range(n):
            rs = grads[a].shape[0] // N_DEV
            for q in range(4):
                blk = ins[a].at[pl.ds((2 * q + 1 - c) * rs, rs), :]
                out.append(pltpu.make_async_remote_copy(
                    src_ref=blk, dst_ref=outs[a].at[q], send_sem=send_sems.at[4 * a + q], recv_sem=recv_sems.at[4 * a + q],
                    device_id=(x, y, 1 - c), device_id_type=MESH))
        return out

    def start(ins, outs, sems):
        for cp in copies(ins, outs, sems):
            cp.start()

    def finish(ins, outs, sems):
        for cp in copies(ins, outs, sems):
            cp.wait()

    return _Job(grads, [jax.ShapeDtypeStruct((4, g.shape[0] // N_DEV, g.shape[1]), g.dtype) for g in grads],
                [_SEMS(4 * n), _SEMS(4 * n)], start, finish)


def _chip_job(sums):
    n = len(sums)

    def copies(ins, outs, sems):
        send_sems, recv_sems = sems
        x, y, c = _mesh_pos()
        out = []
        for a in range(n):
            for r in (1, 2, 3):
                px, py = (1 - x if r & 2 else x), (1 - y if r & 1 else y)
                out.append(pltpu.make_async_remote_copy(
                    src_ref=ins[a].at[2 * px + py], dst_ref=outs[a].at[r - 1], send_sem=send_sems.at[3 * a + r - 1],
                    recv_sem=recv_sems.at[3 * a + r - 1], device_id=(px, py, c), device_id_type=MESH))
        return out

    def start(ins, outs, sems):
        for cp in copies(ins, outs, sems):
            cp.start()

    def finish(ins, outs, sems):
        for cp in copies(ins, outs, sems):
            cp.wait()

    return _Job(sums, [jax.ShapeDtypeStruct((3,) + s.shape[1:], s.dtype) for s in sums],
                [_SEMS(3 * n), _SEMS(3 * n)], start, finish)


def _mm(name, form, a_list, b, M, N, K, tm, tn, tk, extras, outs, epi, job=None, acc_as_ref=False):
    nI, nJ, nK = M // tm, N // tn, K // tk
    assert nI * tm == M and nJ * tn == N and nK * tk == K
    dims = {"nn": NN, "nt": NT, "tn": TN}[form]
    b_list = b if isinstance(b, list) else [(b, {"nn": N, "nt": K, "tn": N}[form])]
    nA, nB = len(a_list), len(b_list)
    assert nA == 1 or nB == 1
    assert nB == 1 or form in ("nn", "nt")
    AXIS = {"i": 0, "j": 1, "k": 2}
    a_axis, a_tile = ("i", tm) if form == "tn" else ("k", tk)
    b_axis, b_tile = ("k", tk) if form == "nt" else ("j", tn)

    def cut(pieces, tile, total):
        starts, s = [], 0
        for _, w in pieces:
            assert w % tile == 0
            starts.append(s // tile)
            s += w
        assert s == total
        return starts, [w // tile for _, w in pieces]

    a_st, a_cn = cut(a_list, a_tile, M if form == "tn" else K)
    b_st, b_cn = cut(b_list, b_tile, K if form == "nt" else N)

    def inside(idx, st, cn):
        return (idx >= st) & (idx < st + cn)

    def a_spec(p):
        st, cn = a_st[p], a_cn[p]
        if form == "tn":
            return pl.BlockSpec((tk, tm), lambda i, j, k: (jnp.where(inside(i, st, cn), k, 0), jnp.clip(i - st, 0, cn - 1)))
        return pl.BlockSpec((tm, tk), lambda i, j, k: (i, jnp.clip(k - st, 0, cn - 1)))

    def b_spec(p):
        st, cn = b_st[p], b_cn[p]
        if form == "nt":
            return pl.BlockSpec((tn, tk), lambda i, j, k: (j, jnp.clip(k - st, 0, cn - 1)))
        if nB == 1:
            return pl.BlockSpec((tk, tn), lambda i, j, k: (k, j))
        return pl.BlockSpec((tk, tn), lambda i, j, k: (jnp.where(inside(j, st, cn), k, 0), jnp.clip(j - st, 0, cn - 1)))

    in_specs = ([a_spec(p) for p in range(nA)] + [b_spec(p) for p in range(nB)]
                + [pl.BlockSpec(bs, im) for _, bs, im in extras])
    out_shape = [jax.ShapeDtypeStruct(s_, d_) for s_, d_, _, _ in outs]
    out_specs = [pl.BlockSpec(bs, im) for _, _, bs, im in outs]
    nE, nO = len(extras), len(outs)
    single = nA == 1 and nB == 1

    def body(*refs):
        a_refs, b_refs = refs[:nA], refs[nA:nA + nB]
        ex, ou = refs[nA + nB:nA + nB + nE], refs[nA + nB + nE:nA + nB + nE + nO]
        ids = [pl.program_id(a) for a in range(3)]

        def partial_of(p, q):
            return lax.dot_general(a_refs[p][...], b_refs[q][...], dims, preferred_element_type=F32)

        if nK == 1 and single:
            epi(partial_of(0, 0), ex, ou)
            return
        acc = refs[-1]
        k = ids[2]
        for p in range(nA):
            for q in range(nB):
                def first(p=p, q=q):
                    acc[...] = partial_of(p, q)

                def later(p=p, q=q):
                    acc[...] += partial_of(p, q)

                here = None
                if nA > 1:
                    here = inside(ids[AXIS[a_axis]], a_st[p], a_cn[p])
                if nB > 1:
                    here = inside(ids[AXIS[b_axis]], b_st[q], b_cn[q])
                pl.when(k == 0 if here is None else here & (k == 0))(first)
                pl.when(k > 0 if here is None else here & (k > 0))(later)

        @pl.when(k == nK - 1)
        def _():
            epi(acc if acc_as_ref else acc[...], ex, ou)

    scratch = [] if (nK == 1 and single) else [pltpu.VMEM((tm, tn), F32)]
    res, job_res = _pcall(
        body, grid=(nI, nJ, nK), in_specs=in_specs, out_specs=out_specs, out_shape=out_shape, scratch_shapes=scratch,
        name=name, semantics=("parallel", "parallel", "arbitrary"),
        args=[a for a, _ in a_list] + [p for p, _ in b_list] + [e for e, _, _ in extras], job=job)
    return res if job is None else (res, job_res)


def _stream_plan(rs, tn):
    n_tiles = N_DEV * rs // tn
    order = np.zeros((N_DEV, n_tiles), np.int32)
    need = np.zeros((N_DEV, n_tiles, 3), np.int32)
    need[:, :, :2] = -1
    for me in range(N_DEV):
        q, c = me // 2, me % 2
        first, second = (q ^ 1, q ^ 2) if c == 1 else (q ^ 2, q ^ 1)
        ranked = [me, 2 * q + 1 - c, 2 * first + c, 2 * second + 1 - c, 2 * second + c, 2 * first + 1 - c,
                  2 * (q ^ 3) + c, 2 * (q ^ 3) + 1 - c]
        rank = {b: r for r, b in enumerate(ranked)}
        launch = {2 * first + c: 1, 2 * second + c: 2}
        blocks_of = lambda t: range(tn * t // rs, (tn * t + tn - 1) // rs + 1)
        tiles = sorted(range(n_tiles), key=lambda t: (max(rank[b] for b in blocks_of(t)), t))
        seen = {me}
        for s, t in enumerate(tiles):
            order[me, s] = t
            new = sorted((b for b in blocks_of(t) if b not in seen), key=rank.get)
            assert len(new) <= 2
            need[me, s, :len(new)] = new
            need[me, s, 2] = max([launch.get(b, 0) for b in new], default=0)
            seen.update(new)
        assert seen == set(range(N_DEV))
        phases = need[me, :, 2]
        assert list(phases[phases > 0]) == [1, 2]
    return order, need.reshape(N_DEV, 3 * n_tiles)


def _mm_stream(name, a, shard, me, M, K, tm, tn, outs, epi):
    rs = shard.shape[0]
    nI, nJ = M // tm, N_DEV * rs // tn
    order, need = _stream_plan(rs, tn)
    order, need = jnp.asarray(order)[me], jnp.asarray(need)[me]
    nO = len(outs)
    steps = nI * nJ

    def body(order_ref, need_ref, a_ref, shard_ref, *refs):
        out_refs, w_ref = refs[:nO], refs[nO]
        bbuf, dsem, send_sems, recv_sems, lsem = refs[nO + 1:]
        i, s = pl.program_id(0), pl.program_id(1)
        g = i * nJ + s
        x, y, c = _mesh_pos()
        q, mine = 2 * x + y, _index((x, y, c))
        sib = (x, y, 1 - c)
        rows = lambda b: w_ref.at[pl.ds(b * rs, rs), :]

        first_is_y = c == 1
        targets = [sib,
                   (jnp.where(first_is_y, x, 1 - x), jnp.where(first_is_y, 1 - y, y), c),
                   (jnp.where(first_is_y, 1 - x, x), jnp.where(first_is_y, y, 1 - y), c),
                   (1 - x, 1 - y, c)]

        def own_copy(k):
            return pltpu.make_async_remote_copy(src_ref=shard_ref, dst_ref=rows(mine), send_sem=send_sems.at[k],
                                                recv_sem=recv_sems.at[mine], device_id=targets[k], device_id_type=MESH)

        def pass_on(b):
            return pltpu.make_async_remote_copy(src_ref=rows(b), dst_ref=rows(b), send_sem=send_sems.at[3 + ((b // 2) ^ q)],
                                                recv_sem=recv_sems.at[b], device_id=sib, device_id_type=MESH)

        def tile_dma(t, slot):
            return pltpu.make_async_copy(w_ref.at[pl.ds(t * tn, tn), :], bbuf.at[slot], dsem.at[slot])

        def await_blocks(step):
            for u in range(2):
                b = need_ref[3 * step + u]

                @pl.when(b >= 0)
                def _():
                    pass_on(b).wait_recv()

                    @pl.when((b % 2 == c) & (b // 2 != q))
                    def _():
                        pass_on(b).start()

            for phase in (1, 2):
                @pl.when(need_ref[3 * step + 2] == phase)
                def _():
                    own_copy(phase + 1).start()

        @pl.when(g == 0)
        def _():
            local = pltpu.make_async_copy(shard_ref, rows(mine), lsem)
            local.start()
            own_copy(0).start()
            own_copy(1).start()
            local.wait()
            await_blocks(0)
            tile_dma(order_ref[0], 0).start()

        @pl.when(g + 1 < steps)
        def _():
            nxt = jnp.where(s + 1 < nJ, s + 1, 0)

            @pl.when((i == 0) & (s + 1 < nJ))
            def _():
                await_blocks(s + 1)

            tile_dma(order_ref[nxt], (g + 1) % 2).start()

        tile_dma(order_ref[s], g % 2).wait()
        epi(lax.dot_general(a_ref[...], bbuf[g % 2], NT, preferred_element_type=F32), out_refs)

        @pl.when(g == steps - 1)
        def _():
            for k in range(4):
                own_copy(k).wait_send()
            for r in (1, 2, 3):
                pass_on(2 * (q ^ r) + c).wait_send()

    tile = lambda i, s, order_ref, need_ref: (i, order_ref[s])
    grid_spec = pltpu.PrefetchScalarGridSpec(
        num_scalar_prefetch=2, grid=(nI, nJ),
        in_specs=[pl.BlockSpec((tm, K), lambda i, s, o, n: (i, 0)), _ANY],
        out_specs=[pl.BlockSpec((tm, tn), tile)] * nO + [_ANY],
        scratch_shapes=[pltpu.VMEM((2, tn, K), shard.dtype), _SEMS(2), _SEMS(7), _SEMS(N_DEV), pltpu.SemaphoreType.DMA])
    res = pl.pallas_call(
        body, grid_spec=grid_spec,
        out_shape=[jax.ShapeDtypeStruct(sh, dt) for sh, dt in outs] + [jax.ShapeDtypeStruct((N_DEV * rs, K), shard.dtype)],
        name=name, compiler_params=_params(("arbitrary", "arbitrary")),
    )(order, need, a, shard)
    return list(res[:nO]), res[nO]


def _rms_mod_fwd(name, x, gain, sc, sh, tr):
    T = x.shape[0]

    def body(x_ref, g_ref, sc_ref, sh_ref, h_ref):
        xv = x_ref[...]
        rstd = lax.rsqrt(jnp.mean(xv * xv, axis=-1, keepdims=True) + EPS)
        h_ref[...] = ((xv * rstd * g_ref[...]) * (1.0 + sc_ref[...]) + sh_ref[...]).astype(BF16)

    row = pl.BlockSpec((tr, D), lambda i: (i, 0))
    vec = pl.BlockSpec((1, D), lambda i: (0, 0))
    return pl.pallas_call(
        body, grid=(T // tr,), in_specs=[row, vec, vec, vec], out_specs=row,
        out_shape=jax.ShapeDtypeStruct((T, D), BF16), name=name, compiler_params=_params(("parallel",)),
    )(x, gain, sc, sh)


def _rms_mod_bwd_epilogue(x, gain, sc, dres, tm, gate=None, mo=None):
    T = x.shape[0]
    with_gate = gate is not None
    row = ((tm, D), lambda i, j, k: (i, 0))
    vec = ((1, D), lambda i, j, k: (0, 0))
    part = ((T // tm * 8, D), F32, (8, D), lambda i, j, k: (i, 0))
    extras = [(x, *row), (gain, *vec), (sc, *vec), (dres, *row)]
    outs = [((T, D), F32, *row), part, part, part]
    if with_gate:
        extras += [(gate, *vec), (mo, *row)]
        outs += [((T, D), BF16, *row), part]

    rows = min(128, tm)

    def epi(acc, ex, ou):
        g = ex[1][...]
        sums = [jnp.zeros((8, D), F32) for _ in range(4)]
        for r0 in range(0, tm, rows):
            rs = slice(r0, r0 + rows)
            dhv, xv = acc[rs, :], ex[0][rs, :]
            rstd = lax.rsqrt(jnp.mean(xv * xv, axis=-1, keepdims=True) + EPS)
            xhat = xv * rstd
            dn = dhv * (1.0 + ex[2][...])
            dxhat = dn * g
            dx = ex[3][rs, :] + rstd * (dxhat - xhat * jnp.mean(dxhat * xhat, axis=-1, keepdims=True))
            ou[0][rs, :] = dx
            terms = [dhv, dhv * (xhat * g), dn * xhat]
            if with_gate:
                ou[4][rs, :] = (ex[4][...] * dx).astype(BF16)
                terms.append(dx * ex[5][rs, :].astype(F32))
            sums = [s + _fold8(t) for s, t in zip(sums, terms)] + sums[len(terms):]
        ou[1][...], ou[2][...], ou[3][...] = sums[:3]
        if with_gate:
            ou[5][...] = sums[3]

    return extras, outs, epi


def _rms_mod_bwd(name, dh, x, gain, sc, dres, tr, gate=None, mo=None):
    T = x.shape[0]
    extras, outs, epi = _rms_mod_bwd_epilogue(x, gain, sc, dres, tr, gate, mo)
    rows_only = lambda im: (lambda i: im(i, 0, 0))
    nE = len(extras)

    def body(dh_ref, *refs):
        epi(dh_ref, refs[:nE], refs[nE:])

    return pl.pallas_call(
        body, grid=(T // tr,),
        in_specs=[pl.BlockSpec((tr, D), lambda i: (i, 0))] + [pl.BlockSpec(bs, rows_only(im)) for _, bs, im in extras],
        out_specs=[pl.BlockSpec(bs, rows_only(im)) for _, _, bs, im in outs],
        out_shape=[jax.ShapeDtypeStruct(s, d) for s, d, _, _ in outs], name=name, compiler_params=_params(("parallel",)),
    )(dh, *[e for e, _, _ in extras])


def _split3(v):
    h = v.astype(BF16)
    r1 = v - h.astype(F32)
    m = r1.astype(BF16)
    lo = (r1 - m.astype(F32)).astype(BF16)
    return h, m, lo


def _tri_mm(tri, v, dims=NN):
    h, m, lo = _split3(v)
    t = tri.astype(BF16)
    mm = lambda p: lax.dot_general(t, p, dims, preferred_element_type=F32)
    return (mm(lo) + mm(m)) + mm(h)


def _hgrn_chunk_terms(q, fl, lb):
    sig = _sigmoid(fl)
    f = lb + (1.0 - lb) * sig
    lf = jnp.log(f)
    kk = 1.0 - f
    sq = _sigmoid(q)
    qf = q * sq
    return sig, f, lf, kk, sq, qf


def _causal(n):
    r = lax.broadcasted_iota(jnp.int32, (n, n), 0)
    c = lax.broadcasted_iota(jnp.int32, (n, n), 1)
    return r >= c


def _hgrn_fwd(proj, lb_logits, o_gain, tt, job=None):
    T = proj.shape[0]
    nT, ncl = T // tt, tt // CHUNK
    C = CHUNK

    def body(q_ref, f_ref, i_ref, g_ref, lbl_ref, og_ref, y_ref, st_ref, S):
        @pl.when(pl.program_id(1) == 0)
        def _():
            S[...] = jnp.zeros_like(S)

        lbl = lbl_ref[...]
        lb = _sigmoid(lbl[0:1, :] - lbl[1:2, :])
        og = og_ref[...]
        shp = (ncl, C, A_HD)
        q, fl, v, g = (r[...].reshape(shp) for r in (q_ref, f_ref, i_ref, g_ref))
        tri = jnp.broadcast_to(_causal(C), (ncl, C, C))
        _, _, lf, kk, _, qf = _hgrn_chunk_terms(q, fl, lb)
        b = _tri_mm(tri, lf, BNN)
        bm, bl = b[:, C // 2 - 1:C // 2, :], b[:, C - 1:C, :]
        qd, kd = qf * jnp.exp(b - bm), kk * jnp.exp(bm - b)
        A = jnp.where(tri, _dot(qd, kd, BNT), 0.0)
        d_st = _dot(v, kk * jnp.exp(bl - b), BTN)
        dec = jnp.exp(bl)
        st = S[...]
        for ci in range(ncl):
            st_ref[0, ci] = st
            st = st * dec[ci] + d_st[ci]
        S[...] = st
        o = _dot(A, v, BNN) + _dot(qf * jnp.exp(b), st_ref[0], BNT)
        r = lax.rsqrt(jnp.mean(o * o, axis=-1, keepdims=True) + EPS)
        y_ref[...] = (o * r * og * (g * _sigmoid(g))).astype(BF16).reshape(tt, A_HD)

    def col(off):
        return pl.BlockSpec((tt, A_HD), lambda h, t: (t, off // A_HD + h))

    head_vec = lambda rows: pl.BlockSpec((rows, A_HD), lambda h, t: (0, h))
    return _pcall(
        body, grid=(A_HEADS, nT),
        in_specs=[col(OFF_QA), col(OFF_FA), col(OFF_IA), col(OFF_GA), head_vec(2), head_vec(1)],
        out_specs=[pl.BlockSpec((tt, A_HD), lambda h, t: (t, h)),
                   pl.BlockSpec((1, ncl, A_HD, A_HD), lambda h, t: (h, t, 0, 0))],
        out_shape=[jax.ShapeDtypeStruct((T, AW), BF16),
                   jax.ShapeDtypeStruct((A_HEADS, T // C, A_HD, A_HD), F32)],
        scratch_shapes=[pltpu.VMEM((A_HD, A_HD), F32)], name="hgrn_fwd", semantics=("parallel", "arbitrary"),
        args=[proj, proj, proj, proj, lb_logits, o_gain], job=job)


def _hgrn_bwd(proj, st, dy, lb_logits, o_gain, tt, job=None):
    T = proj.shape[0]
    nT, ncl = T // tt, tt // CHUNK
    C = CHUNK

    def body(q_ref, f_ref, i_ref, g_ref, st_ref, dy_ref, lbl_ref, og_ref,
             dq_ref, df_ref, di_ref, dg_ref, plb_ref, pog_ref, dS):
        @pl.when(pl.program_id(1) == 0)
        def _():
            dS[...] = jnp.zeros_like(dS)

        lbl = lbl_ref[...]
        lb = _sigmoid(lbl[0:1, :] - lbl[1:2, :])
        og = og_ref[...]
        shp = (ncl, C, A_HD)
        flat = lambda t: t.reshape(tt, A_HD)
        q, fl, v, g, dout = (r[...].reshape(shp) for r in (q_ref, f_ref, i_ref, g_ref, dy_ref))
        tri = jnp.broadcast_to(_causal(C), (ncl, C, C))
        rowi = lax.broadcasted_iota(jnp.int32, shp, 1)
        st0 = st_ref[0]
        sig, f, lf, kk, sq, qf = _hgrn_chunk_terms(q, fl, lb)
        b = _tri_mm(tri, lf, BNN)
        bm, bl = b[:, C // 2 - 1:C // 2, :], b[:, C - 1:C, :]
        e_qd, e_kd, e_ke, e_b = jnp.exp(b - bm), jnp.exp(bm - b), jnp.exp(bl - b), jnp.exp(b)
        qd, kd, ke, qe = qf * e_qd, kk * e_kd, kk * e_ke, qf * e_b
        dec = jnp.exp(bl)
        A = jnp.where(tri, _dot(qd, kd, BNT), 0.0)
        o = _dot(A, v, BNN) + _dot(qe, st0, BNT)
        r = lax.rsqrt(jnp.mean(o * o, axis=-1, keepdims=True) + EPS)
        sg = _sigmoid(g)
        on = o * r * og
        dg_ref[...] = flat((dout * on * (sg * (1.0 + g * (1.0 - sg)))).astype(BF16))
        don = dout * (g * sg)
        pog_ref[...] = _fold8(flat(don * o * r))
        dyh = don * og
        do = r * (dyh - o * (r * r) * jnp.mean(dyh * o, axis=-1, keepdims=True))
        g_st = _dot(do, qe, BTN)
        run = dS[...]
        after = [None] * ncl
        for ci in reversed(range(ncl)):
            after[ci] = run
            run = g_st[ci] + run * dec[ci]
        dS[...] = run
        d_after = jnp.stack(after, axis=0)
        ddec = jnp.sum(d_after * st0, axis=1, keepdims=True)
        dqe = _dot(do, st0, BNN)
        dke = _dot(v, d_after, BNN)
        dA = jnp.where(tri, _dot(do, v, BNT), 0.0)
        dv = _dot(ke, d_after, BNT) + _dot(A, do, BTN)
        dqd = _dot(dA, kd, BNN)
        dkd = _dot(dA, qd, BTN)
        di_ref[...] = flat(dv.astype(BF16))
        dqf = dqe * e_b + dqd * e_qd
        dkk = dkd * e_kd + dke * e_ke
        t_qd, t_kd, t_ke = dqd * qd, dkd * kd, dke * ke
        db = dqe * qe + t_qd - t_kd - t_ke
        dbm = jnp.sum(t_kd - t_qd, axis=1, keepdims=True)
        dbl = jnp.sum(t_ke, axis=1, keepdims=True) + ddec * dec
        db = db + jnp.where(rowi == C // 2 - 1, dbm, 0.0) + jnp.where(rowi == C - 1, dbl, 0.0)
        dlf = _tri_mm(tri, db, BTN)
        dfv = dlf / f - dkk
        df_ref[...] = flat((dfv * (1.0 - lb) * sig * (1.0 - sig)).astype(BF16))
        plb_ref[...] = _fold8(flat(dfv * (1.0 - sig)))
        dq_ref[...] = flat((dqf * (sq * (1.0 + q * (1.0 - sq)))).astype(BF16))

    def col(off):
        return pl.BlockSpec((tt, A_HD), lambda h, t: (nT - 1 - t, off // A_HD + h))

    head_vec = lambda rows: pl.BlockSpec((rows, A_HD), lambda h, t: (0, h))
    o_spec = pl.BlockSpec((tt, A_HD), lambda h, t: (nT - 1 - t, h))
    p_spec = pl.BlockSpec((8, A_HD), lambda h, t: (t, h))
    o_shape = jax.ShapeDtypeStruct((T, AW), BF16)
    p_shape = jax.ShapeDtypeStruct((nT * 8, AW), F32)
    return _pcall(
        body, grid=(A_HEADS, nT),
        in_specs=[col(OFF_QA), col(OFF_FA), col(OFF_IA), col(OFF_GA),
                  pl.BlockSpec((1, ncl, A_HD, A_HD), lambda h, t: (h, nT - 1 - t, 0, 0)),
                  pl.BlockSpec((tt, A_HD), lambda h, t: (nT - 1 - t, h)), head_vec(2), head_vec(1)],
        out_specs=[o_spec, o_spec, o_spec, o_spec, p_spec, p_spec],
        out_shape=[o_shape, o_shape, o_shape, o_shape, p_shape, p_shape],
        scratch_shapes=[pltpu.VMEM((A_HD, A_HD), F32)], name="hgrn_bwd", semantics=("parallel", "arbitrary"),
        args=[proj, proj, proj, proj, st, dy, lb_logits, o_gain], job=job)


LANES = 128
Q_COLS = BW // LANES


def _low_half():
    return lax.broadcasted_iota(jnp.int32, (1, LANES), 1) < B_HD


def _half_sum(t, low):
    lo = jnp.sum(jnp.where(low, t, 0.0), axis=-1, keepdims=True)
    hi = jnp.sum(jnp.where(low, 0.0, t), axis=-1, keepdims=True)
    return jnp.where(low, lo, hi)


def _half_rms(t, low):
    r = lax.rsqrt(_half_sum(t * t, low) * (1.0 / B_HD) + EPS)
    return t * r, r


def _fold_halves(p, low):
    return jnp.where(low, p + pltpu.roll(p, B_HD, 1), 0.0)


def _stack_cols(x):
    return jnp.stack([x[:, c * LANES:(c + 1) * LANES] for c in range(Q_COLS)], axis=0).reshape(KV_HEADS, 2 * BLK, LANES)


def _col_of(t, c):
    return t[c // 2, (c % 2) * BLK:(c % 2 + 1) * BLK]


def _split_halves(col, s, low):
    own = jnp.where(low if s == 0 else jnp.logical_not(low), col, 0.0)
    other = pltpu.roll(own, B_HD, 1)
    return (own, other) if s == 0 else (other, own)


def _swa_keys(kp_ref, kc_ref, vp_ref, vc_ref, kg, low):
    k_lo, k_hi, v_lo, v_hi, hats = [], [], [], [], []
    for j in range(KVW // LANES):
        cs = slice(j * LANES, (j + 1) * LANES)
        k_hat, k_r = _half_rms(jnp.concatenate([kp_ref[:, cs], kc_ref[:, cs]], axis=0), low)
        vcol = jnp.concatenate([vp_ref[:, cs], vc_ref[:, cs]], axis=0)
        hats.append((k_hat, k_r))
        for s in range(2):
            for dst_lo, dst_hi, col in ((k_lo, k_hi, k_hat * kg), (v_lo, v_hi, vcol)):
                lo, hi = _split_halves(col, s, low)
                dst_lo.append(lo)
                dst_hi.append(hi)
    st = lambda parts: jnp.stack(parts, axis=0)
    return st(k_lo), st(k_hi), st(v_lo), st(v_hi), hats


def _swa_mask(first_block):
    qi = lax.broadcasted_iota(jnp.int32, (BLK, 2 * BLK), 0) + BLK
    ki = lax.broadcasted_iota(jnp.int32, (BLK, 2 * BLK), 1)
    rel = qi - ki
    m = (rel >= 0) & (rel < BLK) & (jnp.logical_not(first_block) | (ki >= BLK))
    return jnp.concatenate([m, m], axis=0)


def _sink_cols(sk_ref, hi):
    top = lax.broadcasted_iota(jnp.int32, (2 * BLK, 1), 0) < BLK
    return jnp.stack([jnp.where(top, sk_ref[0, GROUP * hk + hi], sk_ref[0, GROUP * hk + 2 + hi])
                      for hk in range(KV_HEADS)], axis=0)


def _swa_probs(qn, k_half, sink, mask):
    s = jnp.where(mask, _dot(qn, k_half, BNT) * (B_HD ** -0.5), NEG)
    m = jnp.maximum(jnp.max(s, axis=-1, keepdims=True), sink)
    p = jnp.exp(s - m)
    ps = jnp.exp(sink - m)
    inv = 1.0 / (jnp.sum(p, axis=-1, keepdims=True) + ps)
    return p * inv, ps * inv


def _swa_fwd(proj, q_gain, k_gain, sinks, job=None):
    T = proj.shape[0]
    nb = T // BLK

    def body(q_ref, kc_ref, kp_ref, vc_ref, vp_ref, qg_ref, kg_ref, sk_ref, o_ref):
        low = _low_half()
        mask = _swa_mask(pl.program_id(0) == 0)
        qn = _half_rms(_stack_cols(q_ref[...]), low)[0] * qg_ref[...]
        k_lo, k_hi, v_lo, v_hi, _ = _swa_keys(kp_ref, kc_ref, vp_ref, vc_ref, kg_ref[...], low)
        p_lo, _ = _swa_probs(qn, k_lo, _sink_cols(sk_ref, 0), mask)
        p_hi, _ = _swa_probs(qn, k_hi, _sink_cols(sk_ref, 1), mask)
        o = (_dot(p_lo, v_lo, BNN) + _dot(p_hi, v_hi, BNN)).astype(BF16)
        for c in range(Q_COLS):
            o_ref[:, c * LANES:(c + 1) * LANES] = _col_of(o, c)

    q_gain, k_gain = jnp.tile(q_gain, (1, 2)), jnp.tile(k_gain, (1, 2))
    cur = lambda w, off: pl.BlockSpec((BLK, w), lambda i: (i, off // w))
    prev = lambda w, off: pl.BlockSpec((BLK, w), lambda i: (jnp.maximum(i - 1, 0), off // w))
    small = lambda n: pl.BlockSpec((1, 2 * n), lambda i: (0, 0))
    return _pcall(
        body, grid=(nb,),
        in_specs=[cur(BW, OFF_QB), cur(KVW, OFF_KB), prev(KVW, OFF_KB), cur(KVW, OFF_VB), prev(KVW, OFF_VB),
                  small(B_HD), small(B_HD), pl.BlockSpec(memory_space=pltpu.SMEM)],
        out_specs=[pl.BlockSpec((BLK, BW), lambda i: (i, 0))],
        out_shape=[jax.ShapeDtypeStruct((T, BW), BF16)], scratch_shapes=[], name="swa_fwd", semantics=("parallel",),
        args=[proj, proj, proj, proj, proj, q_gain, k_gain, sinks], job=job)


def _swa_bwd(proj, dout, q_gain, k_gain, sinks, job=None):
    T = proj.shape[0]
    nb = T // BLK
    W = BW + 2 * KVW

    def body(q_ref, kc_ref, kp_ref, vc_ref, vp_ref, do_ref, qg_ref, kg_ref, sk_ref,
             dq_ref, dkv_ref, pqg_ref, pkg_ref, psk_ref, dkn_c, dv_c):
        i = pl.program_id(0)
        live = i < nb
        low = _low_half()
        high = jnp.logical_not(low)
        qg, kg = qg_ref[...], kg_ref[...]
        mask = _swa_mask(i == 0)
        lane = lax.broadcasted_iota(jnp.int32, (1, LANES), 1)
        scale = B_HD ** -0.5

        @pl.when(i == 0)
        def _():
            dkn_c[...] = jnp.zeros_like(dkn_c)
            dv_c[...] = jnp.zeros_like(dv_c)

        q_hat, q_r = _half_rms(_stack_cols(q_ref[...]), low)
        qn = q_hat * qg
        k_lo, k_hi, v_lo, v_hi, hats = _swa_keys(kp_ref, kc_ref, vp_ref, vc_ref, kg, low)
        do = _stack_cols(do_ref[...])
        dqn = jnp.zeros((KV_HEADS, 2 * BLK, LANES), F32)
        acc_sk = jnp.zeros((1, LANES), F32)
        dk_parts, dv_parts = [], []
        for hi, (k_h, v_h) in enumerate(((k_lo, v_lo), (k_hi, v_hi))):
            p, ps = _swa_probs(qn, k_h, _sink_cols(sk_ref, hi), mask)
            dp = _dot(do, v_h, BNT)
            delta = jnp.sum(p * dp, axis=-1, keepdims=True)
            ds = p * (dp - delta) * scale
            dqn = dqn + _dot(ds, k_h, BNN)
            dk_parts.append(_dot(ds, qn, BTN))
            dv_parts.append(_dot(p, do, BTN))
            t = ps * delta
            for hk in range(KV_HEADS):
                for rows in range(2):
                    h = GROUP * hk + 2 * rows + hi
                    acc_sk = acc_sk + jnp.where(
                        lane == h, -jnp.sum(t[hk, rows * BLK:(rows + 1) * BLK], axis=0, keepdims=True), 0.0)
        dqh = dqn * qg
        dq = (q_r * (dqh - q_hat * (_half_sum(dqh * q_hat, low) * (1.0 / B_HD)))).astype(BF16)
        for c in range(Q_COLS):
            dq_ref[:, c * LANES:(c + 1) * LANES] = _col_of(dq, c)
        acc_qg = _fold_halves(_fold8((dqn * q_hat).reshape(KV_HEADS * 2 * BLK, LANES)), low)

        def native(parts, j):
            lo_arr, hi_arr = parts
            a, b = 2 * j, 2 * j + 1
            return (jnp.where(low, lo_arr[a], 0.0) + pltpu.roll(jnp.where(high, hi_arr[a], 0.0), B_HD, 1)
                    + jnp.where(high, hi_arr[b], 0.0) + pltpu.roll(jnp.where(low, lo_arr[b], 0.0), B_HD, 1))

        acc_kg = jnp.zeros((8, LANES), F32)
        for j in range(KVW // LANES):
            cs = slice(j * LANES, (j + 1) * LANES)
            dkn = jnp.where(live, native(dk_parts, j), 0.0)
            dvc = jnp.where(live, native(dv_parts, j), 0.0)
            kp_hat, kp_r = hats[j][0][:BLK], hats[j][1][:BLK]
            dkn_prev = dkn_c[:, cs] + dkn[:BLK]
            dv_prev = dv_c[:, cs] + dvc[:BLK]
            acc_kg = acc_kg + _fold8(dkn_prev * kp_hat)
            dkh = dkn_prev * kg
            dkv_ref[:, cs] = (kp_r * (dkh - kp_hat * (_half_sum(dkh * kp_hat, low) * (1.0 / B_HD)))).astype(BF16)
            dkv_ref[:, KVW + j * LANES:KVW + (j + 1) * LANES] = dv_prev.astype(BF16)
            dkn_c[:, cs] = dkn[BLK:]
            dv_c[:, cs] = dvc[BLK:]
        keep = jnp.where(i > 0, 1.0, 0.0)
        pqg_ref[...] = jnp.where(live, acc_qg, 0.0)
        pkg_ref[...] = _fold_halves(acc_kg, low) * keep
        psk_ref[...] = jnp.broadcast_to(jnp.where(live, acc_sk, 0.0), (8, LANES)) * (
            lax.broadcasted_iota(jnp.int32, (8, LANES), 0) == 0).astype(F32)

    q_gain, k_gain = jnp.tile(q_gain, (1, 2)), jnp.tile(k_gain, (1, 2))
    last = nb - 1
    cur = lambda w, off: pl.BlockSpec((BLK, w), lambda i: (jnp.minimum(i, last), off // w))
    prev = lambda w, off: pl.BlockSpec((BLK, w), lambda i: (jnp.maximum(i - 1, 0), off // w))
    small = lambda n: pl.BlockSpec((1, 2 * n), lambda i: (0, 0))
    part = pl.BlockSpec((8, 128), lambda i: (i, 0))
    p_shape = jax.ShapeDtypeStruct(((nb + 1) * 8, 128), F32)
    return _pcall(
        body, grid=(nb + 1,),
        in_specs=[cur(BW, OFF_QB), cur(KVW, OFF_KB), prev(KVW, OFF_KB), cur(KVW, OFF_VB), prev(KVW, OFF_VB),
                  pl.BlockSpec((BLK, BW), lambda i: (jnp.minimum(i, last), 0)), small(B_HD), small(B_HD),
                  pl.BlockSpec(memory_space=pltpu.SMEM)],
        out_specs=[pl.BlockSpec((BLK, BW), lambda i: (i, 0)),
                   pl.BlockSpec((BLK, 2 * KVW), lambda i: (jnp.maximum(i - 1, 0), 0)), part, part, part],
        out_shape=[jax.ShapeDtypeStruct((T + BLK, BW), BF16), jax.ShapeDtypeStruct((T, 2 * KVW), BF16),
                   p_shape, p_shape, p_shape],
        scratch_shapes=[pltpu.VMEM((BLK, KVW), F32), pltpu.VMEM((BLK, KVW), F32)], name="swa_bwd",
        semantics=("arbitrary",), args=[proj, proj, proj, proj, proj, dout, q_gain, k_gain, sinks], job=job)


def _branch_merge(ya_pre, attn, wa_t, wb_t, proj, tm, tn, job=None):
    T = ya_pre.shape[0]

    def body(a_ref, b_ref, wa_ref, wb_ref, ga_ref, gb_ref, ya_ref, yb_ref, mg_ref):
        ya = lax.dot_general(a_ref[...], wa_ref[...], NT, preferred_element_type=F32)
        yb = lax.dot_general(b_ref[...], wb_ref[...], NT, preferred_element_type=F32)
        ya_ref[...] = ya.astype(BF16)
        yb_ref[...] = yb.astype(BF16)
        mg_ref[...] = (_sigmoid(ga_ref[...]) * ya + _sigmoid(gb_ref[...]) * yb).astype(BF16)

    o_spec = pl.BlockSpec((tm, tn), lambda i, j: (i, j))
    o_shape = jax.ShapeDtypeStruct((T, D), BF16)
    return _pcall(
        body, grid=(T // tm, D // tn),
        in_specs=[pl.BlockSpec((tm, AW), lambda i, j: (i, 0)), pl.BlockSpec((tm, BW), lambda i, j: (i, 0)),
                  pl.BlockSpec((tn, AW), lambda i, j: (j, 0)), pl.BlockSpec((tn, BW), lambda i, j: (j, 0)),
                  pl.BlockSpec((tm, tn), lambda i, j: (i, OFF_GTA // tn + j)),
                  pl.BlockSpec((tm, tn), lambda i, j: (i, OFF_GTB // tn + j))],
        out_specs=[o_spec, o_spec, o_spec], out_shape=[o_shape, o_shape, o_shape], scratch_shapes=[], name="branch_merge",
        semantics=("parallel", "parallel"), args=[ya_pre, attn, wa_t, wb_t, proj, proj], job=job)


def _ij(i, j, k):
    return (i, j)


def _local_step(x, tgt, mod, g1, g2, lbl, og, qg, kg, sk, shards, me, c_arr):
    win_s, wa_s, wb_s, wout_s, wmi_s, wmo_s = shards
    T = x.shape[0]
    tm, tr, tt = min(1024, T), min(256, T), min(512, T)
    tk_t = min(1024, T)
    tn = 512
    sh1, sc1, gt1, sh2, sc2, gt2 = (mod[:, i * D:(i + 1) * D] for i in range(N_MOD))
    nI = T // tm
    blk = (tm, tn)
    part = lambda: ((nI * 8, D), F32, (8, tn), _ij)
    vec_j = ((1, tn), lambda i, j, k: (0, j))

    h = _rms_mod_fwd("rms1_fwd", x, g1, sc1, sh1, tr)

    def epi_store(acc, ex, ou):
        ou[0][...] = acc.astype(ou[0].dtype)

    tm2 = min(2048, T)
    blk2 = (tm2, tn)

    full = lambda s: (0, s.shape[0])

    def store_tile(acc, ou):
        ou[0][...] = acc

    (proj,), win_t = _mm_stream("in_proj", h, win_s, me, T, D, tm2, tn, [((T, IN_W), F32)], store_tile)
    c0, c1, c2 = MI_CUTS
    (ya_pre, st), (wa_t, wb_t, wmi_part) = _hgrn_fwd(
        proj, lbl, og, tt, job=_gather_job([wa_s, wb_s, wmi_s], rows=[full(wa_s), full(wb_s), (0, c0)]))
    (attn,), (w_out, wmi_part) = _swa_fwd(
        proj, qg, kg, sk, job=_gather_job([wout_s, wmi_s], rows=[full(wout_s), (c0, c1)], into=[None, wmi_part]))
    (ya, yb, merged), (wmi_part,) = _branch_merge(
        ya_pre, attn, wa_t, wb_t, proj, tm, tn, job=_gather_job([wmi_s], rows=[(c1, c2)], into=[wmi_part]))

    def epi_res1(acc, ex, ou):
        x_ref, gt_ref = ex
        ou[0][...] = acc.astype(BF16)
        ou[1][...] = x_ref[...] + gt_ref[...] * acc

    (mo, x1), (wmi_t,) = _mm(
        "out_proj", "nn", [(merged, D)], w_out, T, D, D, tm, tn, D, [(x, blk, _ij), (gt1, *vec_j)],
        [((T, D), BF16, blk, _ij), ((T, D), F32, blk, _ij)], epi_res1,
        job=_gather_job([wmi_s], rows=[(c2, wmi_s.shape[0])], into=[wmi_part]))
    h2 = _rms_mod_fwd("rms2_fwd", x1, g2, sc2, sh2, tr)

    def epi_relu2(acc, ex, ou):
        r = jnp.maximum(acc, 0.0)
        ou[0][...] = r.astype(BF16)
        ou[1][...] = (r * r).astype(BF16)

    (r, a), (w_mo,) = _mm("mlp_in", "nt", [(h2, D)], wmi_t, T, HID, D, tm2, tn, D, [],
                          [((T, HID), BF16, blk2, _ij), ((T, HID), BF16, blk2, _ij)], epi_relu2,
                          job=_gather_job([wmo_s]))

    def epi_loss(acc, ex, ou):
        x1_ref, t_ref, gt_ref = ex
        e = x1_ref[...] + gt_ref[...] * acc - t_ref[...]
        dy = e * (1.0 / D)
        ou[0][...] = dy
        ou[1][...] = (gt_ref[...] * dy).astype(BF16)
        ou[2][...] = _fold8(e * e) * (0.5 / D)
        ou[3][...] = _fold8(dy * acc)

    wide = (tm, 1024)
    part_w = ((nI * 8, D), F32, (8, 1024), _ij)
    dy, dz, p_loss, p_gt2 = _mm(
        "mlp_out", "nn", [(a, HID)], w_mo, T, D, HID, tm, 1024, 1024,
        [(x1, wide, _ij), (tgt, wide, _ij), (gt2, (1, 1024), lambda i, j, k: (0, j))],
        [((T, D), F32, wide, _ij), ((T, D), BF16, wide, _ij), part_w, part_w], epi_loss)

    def epi_du(acc, ex, ou):
        ou[0][...] = (acc * (2.0 * ex[0][...].astype(F32))).astype(BF16)

    (du,) = _mm("mlp_out_dx", "nt", [(dz, D)], w_mo, T, HID, D, tm2, tn, D, [(r, blk2, _ij)],
                [((T, HID), BF16, blk2, _ij)], epi_du)
    gblk = (1024, 1024)
    gwide = (1024, D)
    pair_sum = lambda nm, g, r1: _pair_sum("pair_sum_" + nm, g, r1, c_arr, _sum_rows(r1.shape[1]))
    (g_mo,) = _mm("mlp_out_dw", "tn", [(a, HID)], dz, HID, D, T, 1024, D, tk_t, [], [((HID, D), BF16, gwide, _ij)], epi_store)
    (dh2,), (r1_mo,) = _mm("mlp_in_dx", "nn", [(du, HID)], wmi_t, T, D, HID, tm, 1024, 1024, [],
                           [((T, D), F32, (tm, 1024), _ij)], epi_store, job=_pair_job([g_mo]))
    dx1, p_sh2, p_sc2, p_g2, dmo, p_gt1 = _rms_mod_bwd("rms2_bwd", dh2, x1, g2, sc2, dy, tr, gate=gt1, mo=mo)
    s_mo = pair_sum("mlp_out", g_mo, r1_mo)
    tm_row = min(512, T)
    (g_mi,), (r2_mo,) = _mm("mlp_in_dw", "tn", [(du, HID)], h2, HID, D, T, 1024, D, tk_t, [],
                            [((HID, D), BF16, gwide, _ij)], epi_store, job=_chip_job([s_mo]))

    def epi_gates(acc, ex, ou):
        ya_ref, yb_ref, ga_ref, gb_ref = ex
        sa, sb = _sigmoid(ga_ref[...]), _sigmoid(gb_ref[...])
        ou[0][...] = (acc * sa).astype(BF16)
        ou[1][...] = (acc * sb).astype(BF16)
        ou[2][...] = (acc * ya_ref[...].astype(F32) * (sa * (1.0 - sa))).astype(BF16)
        ou[3][...] = (acc * yb_ref[...].astype(F32) * (sb * (1.0 - sb))).astype(BF16)

    o_bf = ((T, D), BF16, blk, _ij)
    (dya, dyb, dga, dgb), (r1_mi,) = _mm(
        "out_proj_dx", "nt", [(dmo, D)], w_out, T, D, D, tm, tn, D,
        [(ya, blk, _ij), (yb, blk, _ij), (proj, blk, lambda i, j, k: (i, OFF_GTA // tn + j)),
         (proj, blk, lambda i, j, k: (i, OFF_GTB // tn + j))], [o_bf, o_bf, o_bf, o_bf], epi_gates,
        job=_pair_job([g_mi]))
    s_mi = pair_sum("mlp_in", g_mi, r1_mi)
    (g_out,) = _mm("out_proj_dw", "tn", [(merged, D)], dmo, D, D, T, 1024, 1024, tk_t, [], [((D, D), BF16, gblk, _ij)], epi_store)
    (dya_pre,) = _mm("branch_a_dx", "nn", [(dya, D)], wa_t, T, AW, D, tm, tn, D, [], [((T, AW), F32, blk, _ij)], epi_store)
    (dattn,) = _mm("branch_b_dx", "nn", [(dyb, D)], wb_t, T, BW, D, tm, tn, D, [], [((T, BW), F32, blk, _ij)], epi_store)
    (g_a,) = _mm("branch_a_dw", "tn", [(dya, D)], ya_pre, D, AW, T, 1024, 1024, tk_t, [], [((D, AW), BF16, gblk, _ij)], epi_store)
    (g_b,) = _mm("branch_b_dw", "tn", [(dyb, D)], attn, D, BW, T, 1024, 1024, tk_t, [], [((D, BW), BF16, gblk, _ij)], epi_store)
    (dqa, dfa, dia, dgg, p_lb, p_og), (r2_mi,) = _hgrn_bwd(proj, st, dya_pre, lbl, og, tt, job=_chip_job([s_mi]))
    (dqb, dkv, p_qg, p_kg, p_sk), (r1_out, r1_a, r1_b) = _swa_bwd(proj, dattn, qg, kg, sk, job=_pair_job([g_out, g_a, g_b]))
    s_out, s_a, s_b = pair_sum("out", g_out, r1_out), pair_sum("branch_a", g_a, r1_a), pair_sum("branch_b", g_b, r1_b)
    pieces = [(dqa, AW), (dfa, AW), (dia, AW), (dgg, AW), (dqb, BW), (dkv, 2 * KVW), (dga, D), (dgb, D)]
    (g_in,), (r2_out, r2_a, r2_b) = _mm(
        "in_proj_dw", "tn", pieces, h, IN_W, D, T, 512, D, tk_t, [], [((IN_W, D), BF16, (512, D), _ij)], epi_store,
        job=_chip_job([s_out, s_a, s_b]))
    (r1_in,) = _run_job("pair_w_in", _pair_job([g_in]))
    s_in = pair_sum("in", g_in, r1_in)
    extras, outs, epi = _rms_mod_bwd_epilogue(x, g1, sc1, dx1, tm_row)
    (dx, p_sh1, p_sc1, p_g1), (r2_in,) = _mm(
        "in_proj_dx", "nn", pieces, win_t, T, D, IN_W, tm_row, D, 512, extras, outs, epi, job=_chip_job([s_in]),
        acc_as_ref=True)

    partials = dict(sh1=p_sh1, sc1=p_sc1, gt1=p_gt1, sh2=p_sh2, sc2=p_sc2, gt2=p_gt2, g1=p_g1, g2=p_g2,
                    lb=p_lb, og=p_og, qg=p_qg, kg=p_kg, sk=p_sk, loss=p_loss)
    sums = dict(w_in=(s_in, r2_in), w_branch_a=(s_a, r2_a), w_branch_b=(s_b, r2_b), w_out=(s_out, r2_out),
                w_mlp_in=(s_mi, r2_mi), w_mlp_out=(s_mo, r2_mo))
    return dx, sums, partials


def _exchange_slots(buf, send_sems, recv_sems):
    me = _mesh_pos()
    mine = buf.at[_index(me)]
    sends = []
    for k in range(1, N_DEV):
        cp = pltpu.make_async_remote_copy(src_ref=mine, dst_ref=mine, send_sem=send_sems.at[k - 1],
                                          recv_sem=recv_sems.at[k - 1], device_id=_flip(me, k), device_id_type=MESH)
        cp.start()
        sends.append(cp)
    for k in range(1, N_DEV):
        theirs = buf.at[_index(_flip(me, k))]
        pltpu.make_async_remote_copy(src_ref=theirs, dst_ref=theirs, send_sem=send_sems.at[k - 1],
                                     recv_sem=recv_sems.at[k - 1], device_id=_flip(me, k), device_id_type=MESH).wait_recv()
    for cp in sends:
        cp.wait_send()


ADA_W = N_MOD * D // N_DEV


def _ada_mod(c, w_ada, b_shard):
    def body(c_ref, w_ref, b_ref, mod_ref, sc_ref, cbuf, mbuf, s1, r1, s2, r2):
        me = _index(_mesh_pos())
        cbuf[me] = c_ref[...]
        _exchange_slots(cbuf, s1, r1)
        row = lax.broadcasted_iota(jnp.int32, (N_DEV, D), 0)
        call = jnp.zeros((N_DEV, D), F32)
        for d in range(N_DEV):
            call = jnp.where(row == d, cbuf[d], call)
        sc = call * _sigmoid(call)
        sc_ref[...] = sc
        mbuf[me] = _dot(sc, w_ref[...]) + b_ref[...]
        _exchange_slots(mbuf, s2, r2)
        for s in range(N_DEV):
            mod_ref[:, s * ADA_W:(s + 1) * ADA_W] = mbuf[s, pl.ds(me, 1), :]

    return pl.pallas_call(
        body, in_specs=[_VMEM, _VMEM, _VMEM], out_specs=[_VMEM, _VMEM],
        out_shape=[jax.ShapeDtypeStruct((1, N_MOD * D), F32), jax.ShapeDtypeStruct((N_DEV, D), F32)],
        scratch_shapes=[pltpu.VMEM((N_DEV, 1, D), F32), pltpu.VMEM((N_DEV, N_DEV, ADA_W), F32),
                        _SEMS(N_DEV - 1), _SEMS(N_DEV - 1), _SEMS(N_DEV - 1), _SEMS(N_DEV - 1)],
        name="ada_mod", compiler_params=pltpu.CompilerParams(vmem_limit_bytes=VMEM_LIMIT),
    )(c, w_ada, b_shard)


SMALL_SEGS = (("b_ada", N_MOD * D), ("norm1_gain", D), ("norm2_gain", D), ("lb0", AW), ("lb1", AW),
              ("hgrn_o_gain", AW), ("q_norm_gain", 128), ("k_norm_gain", 128), ("sinks", 128))
SMALL_W = sum(w for _, w in SMALL_SEGS)
X_SEGS = (("sh1", D), ("sc1", D), ("gt1", D), ("sh2", D), ("sc2", D), ("gt2", D), ("g1", D), ("g2", D),
          ("lb", AW), ("og", AW), ("qg", 128), ("kg", 128), ("sk", 128), ("loss", 128))
X_W = sum(w for _, w in X_SEGS)


def _offsets(segs):
    out, o = {}, 0
    for name, w in segs:
        out[name] = (o, w)
        o += w
    return out


def _small_reduce(parts, lb_logits):
    xo, so = _offsets(X_SEGS), _offsets(SMALL_SEGS)
    names = [nm for nm, _ in X_SEGS]

    def body(*refs):
        p_refs = dict(zip(names, refs[:len(names)]))
        lbl_ref, allx, gs_ref, loss_ref, send_sems, recv_sems = refs[len(names):]
        me = _index(_mesh_pos())
        for nm, (o, w) in xo.items():
            if nm == "loss":
                allx[me, :, o:o + w] = jnp.broadcast_to(jnp.sum(p_refs[nm][...]), (1, w))
            else:
                allx[me, :, o:o + w] = jnp.sum(p_refs[nm][...], axis=0, keepdims=True)
        _exchange_slots(allx, send_sems, recv_sems)
        tot = allx[0]
        for d in range(1, N_DEV):
            tot = tot + allx[d]
        seg = lambda nm: tot[:, xo[nm][0]:xo[nm][0] + xo[nm][1]]

        def put(nm, v):
            gs_ref[:, so[nm][0]:so[nm][0] + so[nm][1]] = v

        put("b_ada", tot[:, 0:N_MOD * D])
        put("norm1_gain", seg("g1"))
        put("norm2_gain", seg("g2"))
        lbl = lbl_ref[...]
        lb = _sigmoid(lbl[0:1, :] - lbl[1:2, :])
        dl0 = seg("lb") * lb * (1.0 - lb)
        put("lb0", dl0)
        put("lb1", -dl0)
        put("hgrn_o_gain", seg("og"))
        put("q_norm_gain", seg("qg"))
        put("k_norm_gain", seg("kg"))
        put("sinks", seg("sk"))
        loss_ref[...] = seg("loss")

    return pl.pallas_call(
        body, in_specs=[_VMEM] * (len(names) + 1), out_specs=[_VMEM, _VMEM, _VMEM],
        out_shape=[jax.ShapeDtypeStruct((N_DEV, 1, X_W), F32), jax.ShapeDtypeStruct((1, SMALL_W), F32),
                   jax.ShapeDtypeStruct((1, 128), F32)],
        scratch_shapes=[_SEMS(N_DEV - 1), _SEMS(N_DEV - 1)], name="small_reduce",
        compiler_params=pltpu.CompilerParams(vmem_limit_bytes=VMEM_LIMIT),
    )(*[parts[nm] for nm in names], lb_logits)


def _adamw_math(w, g, m, v):
    m = B1 * m + (1.0 - B1) * g
    v = B2 * v + (1.0 - B2) * (g * g)
    m_hat = m / (1.0 - B1 ** STEP)
    v_hat = v / (1.0 - B2 ** STEP)
    return -LR * (m_hat / (jnp.sqrt(v_hat) + ADAM_EPS) + WD * w), m, v


def _sum_rows(rs):
    return 256 if rs % 256 == 0 else rs // 2


def _pair_sum(name, g, recv, c_arr, tr):
    _, rs, cols = recv.shape
    blk = (1, tr, cols)

    def body(c_ref, g_ref, r_ref, o_ref):
        o_ref[...] = (g_ref[...].astype(F32) + r_ref[...].astype(F32)).astype(BF16)

    grid_spec = pltpu.PrefetchScalarGridSpec(
        num_scalar_prefetch=1, grid=(4, rs // tr),
        in_specs=[pl.BlockSpec(blk, lambda q, i, c: (2 * q + c[0], i, 0)), pl.BlockSpec(blk, lambda q, i, c: (q, i, 0))],
        out_specs=pl.BlockSpec(blk, lambda q, i, c: (q, i, 0)))
    return pl.pallas_call(body, grid_spec=grid_spec, out_shape=jax.ShapeDtypeStruct((4, rs, cols), BF16), name=name,
                          compiler_params=_params(("parallel", "parallel")))(c_arr, g.reshape(N_DEV, rs, cols), recv)


def _final_sum(name, sums, recv, q_arr, tr):
    _, rs, cols = sums.shape

    def body(q_ref, s_ref, r_ref, o_ref):
        o_ref[...] = ((s_ref[0].astype(F32) + r_ref[0].astype(F32)) + r_ref[1].astype(F32)) + r_ref[2].astype(F32)

    grid_spec = pltpu.PrefetchScalarGridSpec(
        num_scalar_prefetch=1, grid=(rs // tr,),
        in_specs=[pl.BlockSpec((1, tr, cols), lambda i, q: (q[0], i, 0)), pl.BlockSpec((3, tr, cols), lambda i, q: (0, i, 0))],
        out_specs=pl.BlockSpec((tr, cols), lambda i, q: (i, 0)))
    return pl.pallas_call(body, grid_spec=grid_spec, out_shape=jax.ShapeDtypeStruct((rs, cols), F32), name=name,
                          compiler_params=_params(("parallel",)))(q_arr, sums, recv)


def _adamw(name, w, g, m, v, tr):
    rows, cols = w.shape

    def body(w_ref, g_ref, m_ref, v_ref, d_ref, nm_ref, nv_ref):
        d_ref[...], nm_ref[...], nv_ref[...] = _adamw_math(w_ref[...], g_ref[...], m_ref[...], v_ref[...])

    spec = pl.BlockSpec((tr, cols), lambda i: (i, 0))
    shape = jax.ShapeDtypeStruct((rows, cols), F32)
    return pl.pallas_call(
        body, grid=(rows // tr,), in_specs=[spec] * 4, out_specs=[spec] * 3, out_shape=[shape] * 3, name=name,
        compiler_params=_params(("parallel",)),
    )(w, g, m, v)


def _ada_update(sc_t, dmod_cols, w, m, v, tr):
    rows, cols = w.shape

    def body(s_ref, d_ref, w_ref, m_ref, v_ref, g_ref, dl_ref, nm_ref, nv_ref):
        g = jnp.dot(s_ref[...], d_ref[...], precision=lax.Precision.HIGHEST, preferred_element_type=F32)
        g_ref[...] = g
        dl_ref[...], nm_ref[...], nv_ref[...] = _adamw_math(w_ref[...], g, m_ref[...], v_ref[...])

    spec = pl.BlockSpec((tr, cols), lambda i: (i, 0))
    shape = jax.ShapeDtypeStruct((rows, cols), F32)
    return pl.pallas_call(
        body, grid=(rows // tr,),
        in_specs=[pl.BlockSpec((tr, N_DEV), lambda i: (i, 0)), pl.BlockSpec((N_DEV, cols), lambda i: (0, 0)), spec, spec, spec],
        out_specs=[spec] * 4, out_shape=[shape] * 4, name="ada_update", compiler_params=_params(("parallel",)),
    )(sc_t, dmod_cols, w, m, v)


BIG = ("w_in", "w_branch_a", "w_branch_b", "w_out", "w_mlp_in", "w_mlp_out")
COLUMN_SHARDED = ("w_in", "w_branch_a", "w_branch_b", "w_mlp_in")
WEIGHTS = ("w_ada", "b_ada", "norm1_gain", "w_in", "lb_logits", "hgrn_o_gain", "q_norm_gain", "k_norm_gain", "sinks",
           "w_branch_a", "w_branch_b", "w_out", "norm2_gain", "w_mlp_in", "w_mlp_out")


def _pack_small(p):
    lb = p["lb_logits"]
    src = dict(p, lb0=lb[0:1], lb1=lb[1:2])
    return jnp.concatenate([jnp.pad(src[nm], ((0, 0), (0, w - src[nm].shape[1]))) for nm, w in SMALL_SEGS], axis=1)


def _unpack_small(vec, shapes):
    so = _offsets(SMALL_SEGS)
    out = {}
    for nm, shp in shapes.items():
        if nm == "lb_logits":
            o = so["lb0"][0]
            out[nm] = vec[0, o:o + 2 * AW].reshape(2, AW)
        else:
            o = so[nm][0]
            out[nm] = vec[:, o:o + shp[1]]
    return out


def kernel(x, c, w_ada, b_ada, norm1_gain, w_in, lb_logits, hgrn_o_gain, q_norm_gain, k_norm_gain, sinks, w_branch_a, w_branch_b, w_out, norm2_gain, w_mlp_in, w_mlp_out, loss_target, m_w_ada, m_b_ada, m_norm1_gain, m_w_in, m_lb_logits, m_hgrn_o_gain, m_q_norm_gain, m_k_norm_gain, m_sinks, m_w_branch_a, m_w_branch_b, m_w_out, m_norm2_gain, m_w_mlp_in, m_w_mlp_out, v_w_ada, v_b_ada, v_norm1_gain, v_w_in, v_lb_logits, v_hgrn_o_gain, v_q_norm_gain, v_k_norm_gain, v_sinks, v_w_branch_a, v_w_branch_b, v_w_out, v_norm2_gain, v_w_mlp_in, v_w_mlp_out):
    w = dict(w_ada=w_ada, b_ada=b_ada, norm1_gain=norm1_gain, w_in=w_in, lb_logits=lb_logits, hgrn_o_gain=hgrn_o_gain,
             q_norm_gain=q_norm_gain, k_norm_gain=k_norm_gain, sinks=sinks, w_branch_a=w_branch_a, w_branch_b=w_branch_b,
             w_out=w_out, norm2_gain=norm2_gain, w_mlp_in=w_mlp_in, w_mlp_out=w_mlp_out)
    m = dict(w_ada=m_w_ada, b_ada=m_b_ada, norm1_gain=m_norm1_gain, w_in=m_w_in, lb_logits=m_lb_logits,
             hgrn_o_gain=m_hgrn_o_gain, q_norm_gain=m_q_norm_gain, k_norm_gain=m_k_norm_gain, sinks=m_sinks,
             w_branch_a=m_w_branch_a, w_branch_b=m_w_branch_b, w_out=m_w_out, norm2_gain=m_norm2_gain,
             w_mlp_in=m_w_mlp_in, w_mlp_out=m_w_mlp_out)
    v = dict(w_ada=v_w_ada, b_ada=v_b_ada, norm1_gain=v_norm1_gain, w_in=v_w_in, lb_logits=v_lb_logits,
             hgrn_o_gain=v_hgrn_o_gain, q_norm_gain=v_q_norm_gain, k_norm_gain=v_k_norm_gain, sinks=v_sinks,
             w_branch_a=v_w_branch_a, w_branch_b=v_w_branch_b, w_out=v_w_out, norm2_gain=v_norm2_gain,
             w_mlp_in=v_w_mlp_in, w_mlp_out=v_w_mlp_out)
    for d in (w, m, v):
        for nm in ("w_ada",) + BIG:
            d[nm] = d[nm][0]
    px, py, pc = _mesh_pos()
    me = _index((px, py, pc))
    c_arr = jnp.reshape(pc, (1,)).astype(jnp.int32)
    q_arr = jnp.reshape(2 * px + py, (1,)).astype(jnp.int32)

    shards = [(w[nm].T if nm in COLUMN_SHARDED else w[nm]).astype(BF16) for nm in BIG]
    b_shard = lax.dynamic_slice(b_ada, (0, me * ADA_W), (1, ADA_W))
    mod, sc_all = _ada_mod(c, w["w_ada"], b_shard)

    dx, sums, parts = _local_step(x[0], loss_target[0], mod, norm1_gain, norm2_gain, lb_logits, hgrn_o_gain,
                                  q_norm_gain, k_norm_gain, sinks, shards, me, c_arr)

    allx, g_small, loss = _small_reduce(parts, lb_logits)

    grad, delta, new_m, new_v = {}, {}, {}, {}
    for nm in BIG:
        s, r2 = sums[nm]
        rs = s.shape[1]
        g = _final_sum("sum_" + nm, s, r2, q_arr, _sum_rows(rs))
        g = g.T if nm in COLUMN_SHARDED else g
        rows = g.shape[0]
        grad[nm] = g
        delta[nm], new_m[nm], new_v[nm] = _adamw("adamw_" + nm, w[nm], g, m[nm], v[nm], 128 if rows % 128 == 0 else rows)

    dmod_cols = lax.dynamic_slice(allx[:, 0, :], (0, me * ADA_W), (N_DEV, ADA_W))
    grad["w_ada"], delta["w_ada"], new_m["w_ada"], new_v["w_ada"] = _ada_update(
        sc_all.T, dmod_cols, w["w_ada"], m["w_ada"], v["w_ada"], 256)

    small_names = [nm for nm in WEIGHTS if nm not in BIG and nm != "w_ada"]
    shapes = {nm: w[nm].shape for nm in small_names}
    ds, ms, vs = _adamw("adamw_small", _pack_small(w), g_small, _pack_small(m), _pack_small(v), 1)
    for dst, vec in ((grad, g_small), (delta, ds), (new_m, ms), (new_v, vs)):
        dst.update(_unpack_small(vec, shapes))

    def full(d, nm):
        return d[nm][None] if nm in BIG or nm == "w_ada" else d[nm]

    return (loss[0, 0], dx[None], *[full(grad, nm) for nm in WEIGHTS], *[full(delta, nm) for nm in WEIGHTS],
            *[full(new_m, nm) for nm in WEIGHTS], *[full(new_v, nm) for nm in WEIGHTS])
```

```python
import functools

import jax
import jax.numpy as jnp
from jax import lax
from jax.experimental import pallas as pl
from jax.experimental.pallas import tpu as pltpu

F32 = jnp.float32
BF16 = jnp.bfloat16
MESH = pl.DeviceIdType.MESH

N_DEV = 8
D = 2048
A_HEADS, A_HD, CHUNK = 8, 128, 64
AW = A_HEADS * A_HD
Q_HEADS, KV_HEADS, GROUP, B_HD, BLK = 16, 4, 4, 64, 128
BW = Q_HEADS * B_HD
KVW = KV_HEADS * B_HD
HID = 4 * D
IN_W = 4 * AW + BW + 2 * KVW + 2 * D
OFF_QA, OFF_FA, OFF_IA, OFF_GA = 0, AW, 2 * AW, 3 * AW
OFF_QB = 4 * AW
OFF_KB = OFF_QB + BW
OFF_VB = OFF_KB + KVW
OFF_GTA = OFF_VB + KVW
OFF_GTB = OFF_GTA + D
N_MOD = 6
EPS = 1e-6
LR, B1, B2, ADAM_EPS, WD, STEP = 1e-3, 0.9, 0.999, 1e-8, 0.01, 10
NEG = -1e30

VMEM_LIMIT = 56 * 1024 * 1024
MI_CUTS = (480, 864)
MO_CUTS = (224, 512, 752)

NN = (((1,), (0,)), ((), ()))
NT = (((1,), (1,)), ((), ()))
TN = (((0,), (0,)), ((), ()))
BNN = (((2,), (1,)), ((0,), (0,)))
BNT = (((2,), (2,)), ((0,), (0,)))
BTN = (((1,), (1,)), ((0,), (0,)))


def _dot(a, b, dims=NN):
    return lax.dot_general(a.astype(BF16), b.astype(BF16), dims, preferred_element_type=F32)


def _params(sem):
    return pltpu.CompilerParams(dimension_semantics=sem, vmem_limit_bytes=VMEM_LIMIT)


def _sigmoid(x):
    return 1.0 / (1.0 + jnp.exp(-x))


def _fold8(v):
    r, n = v.shape
    return jnp.sum(v.reshape(r // 8, 8, n), axis=0)


_VMEM = pl.BlockSpec(memory_space=pltpu.VMEM)
_ANY = pl.BlockSpec(memory_space=pl.ANY)
_SEMS = lambda n: pltpu.SemaphoreType.DMA((n,))


def _mesh_pos():
    return lax.axis_index("x"), lax.axis_index("y"), lax.axis_index("c")


def _flip(pos, k):
    return tuple(1 - p if (k >> s) & 1 else p for p, s in zip(pos, (2, 1, 0)))


def _index(pos):
    return 4 * pos[0] + 2 * pos[1] + pos[2]


class _Job:
    def __init__(self, ins, out_shape, sems, start, finish, aliases=None):
        self.ins, self.out_shape, self.sems, self.start, self.finish = list(ins), list(out_shape), list(sems), start, finish
        self.aliases = dict(aliases or {})


def _pcall(body, *, grid, in_specs, out_specs, out_shape, scratch_shapes, name, semantics, args, job=None):
    if job is None:
        outs = pl.pallas_call(body, grid=grid, in_specs=in_specs, out_specs=out_specs, out_shape=out_shape,
                              scratch_shapes=scratch_shapes, name=name, compiler_params=_params(semantics))(*args)
        return list(outs), []
    n_in, n_out, n_scr = len(in_specs), len(out_specs), len(scratch_shapes)
    j_in, j_out = len(job.ins), len(job.out_shape)
    steps = tuple(grid)

    def carrier(*refs):
        o = 0
        main_in, o = refs[o:o + n_in], o + n_in
        job_in, o = refs[o:o + j_in], o + j_in
        main_out, o = refs[o:o + n_out], o + n_out
        job_out, o = refs[o:o + j_out], o + j_out
        main_scr, job_sems = refs[o:o + n_scr], refs[o + n_scr:]
        ids = [pl.program_id(a) for a in range(len(steps))]
        first = functools.reduce(lambda p, q: p & q, [i == 0 for i in ids])
        last = functools.reduce(lambda p, q: p & q, [i == s - 1 for i, s in zip(ids, steps)])

        @pl.when(first)
        def _():
            job.start(job_in, job_out, job_sems)

        body(*main_in, *main_out, *main_scr)

        @pl.when(last)
        def _():
            job.finish(job_in, job_out, job_sems)

    outs = pl.pallas_call(
        carrier, grid=grid, in_specs=list(in_specs) + [_ANY] * j_in, out_specs=list(out_specs) + [_ANY] * j_out,
        out_shape=list(out_shape) + job.out_shape, scratch_shapes=list(scratch_shapes) + job.sems, name=name,
        input_output_aliases={n_in + i: n_out + o for i, o in job.aliases.items()},
        compiler_params=_params(("arbitrary",) * len(steps)),
    )(*args, *job.ins)
    return list(outs[:n_out]), list(outs[n_out:])


def _run_job(name, job):
    j_in, j_out = len(job.ins), len(job.out_shape)

    def body(*refs):
        ins, outs, sems = refs[:j_in], refs[j_in:j_in + j_out], refs[j_in + j_out:]
        job.start(ins, outs, sems)
        job.finish(ins, outs, sems)

    return list(pl.pallas_call(body, in_specs=[_ANY] * j_in, out_specs=[_ANY] * j_out, out_shape=job.out_shape,
                               scratch_shapes=job.sems, name=name,
                               input_output_aliases=job.aliases)(*job.ins))


def _gather_job(shards, rows=None, into=None):
    n = len(shards)
    rows = rows or [(0, s.shape[0]) for s in shards]
    into = into or [None] * n
    olds, aliases = [], {}
    for a, buf in enumerate(into):
        if buf is not None:
            aliases[n + len(olds)] = a
            olds.append(buf)

    def copies(ins, outs, sems):
        send_sems, recv_sems, local_sems = sems
        x, y, c = _mesh_pos()
        me, sib = (x, y, c), (x, y, 1 - c)
        chips = [(1 - x, y), (x, 1 - y), (1 - x, 1 - y)]

        def part(a, p):
            rs, (r0, r1) = shards[a].shape[0], rows[a]
            return outs[a].at[pl.ds(_index(p) * rs + r0, r1 - r0), :]

        own = lambda a: ins[a].at[pl.ds(rows[a][0], rows[a][1] - rows[a][0]), :]

        def copy(a, k, block, to, src=None):
            return pltpu.make_async_remote_copy(
                src_ref=part(a, block) if src is None else src, dst_ref=part(a, block),
                send_sem=send_sems.at[7 * a + k], recv_sem=recv_sems.at[7 * a + k], device_id=to, device_id_type=MESH)

        mine = [pltpu.make_async_copy(own(a), part(a, me), local_sems.at[a]) for a in range(n)]
        first = []
        for a in range(n):
            first.append(copy(a, 0, me, sib, src=own(a)))
            first += [copy(a, 1 + j, me, (*chip, c), src=own(a)) for j, chip in enumerate(chips)]
        return me, sib, c, chips, copy, mine, first

    def start(ins, outs, sems):
        *_, mine, first = copies(ins, outs, sems)
        for cp in mine + first:
            cp.start()

    def finish(ins, outs, sems):
        me, sib, c, chips, copy, mine, first = copies(ins, outs, sems)
        passed = []
        for j, chip in enumerate(chips):
            for a in range(n):
                copy(a, 1 + j, (*chip, c), me).wait_recv()
                cp = copy(a, 4 + j, (*chip, c), sib)
                cp.start()
                passed.append(cp)
        for a in range(n):
            copy(a, 0, sib, me).wait_recv()
            for j, chip in enumerate(chips):
                copy(a, 4 + j, (*chip, 1 - c), me).wait_recv()
        for cp in first + passed:
            cp.wait_send()
        for cp in mine:
            cp.wait()

    return _Job(list(shards) + olds, [jax.ShapeDtypeStruct((N_DEV * s.shape[0], s.shape[1]), s.dtype) for s in shards],
                [_SEMS(7 * n), _SEMS(7 * n), _SEMS(n)], start, finish, aliases)


def _gather_relay_job(shards):
    n = len(shards)

    def tools(ins, outs, sems):
        send_sems, recv_sems, local_sems = sems
        x, y, c = _mesh_pos()
        q = 2 * x + y
        chip_at = lambda rel: (1 - x if rel & 2 else x, 1 - y if rel & 1 else y)

        def rows(a, chip, core):
            rs = shards[a].shape[0]
            return outs[a].at[pl.ds((2 * chip + core) * rs, rs), :]

        def copy(a, slot, chip, core, to, src=None):
            blk = rows(a, chip, core)
            return pltpu.make_async_remote_copy(src_ref=blk if src is None else src, dst_ref=blk,
                                                send_sem=send_sems.at[7 * a + slot], recv_sem=recv_sems.at[7 * a + slot],
                                                device_id=to, device_id_type=MESH)

        mine = [pltpu.make_async_copy(ins[a], rows(a, q, c), local_sems.at[a]) for a in range(n)]
        first = [copy(a, slot, q, c, (x, y, 1 - c) if slot == 0 else (*chip_at(slot), c), src=ins[a])
                 for a in range(n) for slot in (0, 1, 2)]
        return x, y, c, q, chip_at, copy, mine, first

    def start(ins, outs, sems):
        *_, mine, first = tools(ins, outs, sems)
        for cp in mine + first:
            cp.start()

    def finish(ins, outs, sems):
        x, y, c, q, chip_at, copy, mine, first = tools(ins, outs, sems)
        me, sib = (x, y, c), (x, y, 1 - c)

        def relay(src, dst):
            for a in range(n):
                copy(a, src, q ^ src, c, me).wait_recv()
                copy(a, 3, q ^ src, c, (*chip_at(dst), c)).start()
                copy(a, 3 + src, q ^ src, c, sib).start()
            for a in range(n):
                copy(a, dst, q ^ dst, c, me).wait_recv()
                copy(a, 3 + dst, q ^ dst, c, sib).start()

        pl.when(c == 1)(lambda: relay(1, 2))
        pl.when(c == 0)(lambda: relay(2, 1))
        for a in range(n):
            copy(a, 3, q ^ 3, c, me).wait_recv()
            copy(a, 6, q ^ 3, c, sib).start()
        for a in range(n):
            copy(a, 0, q, 1 - c, me).wait_recv()
            for rel in (1, 2, 3):
                copy(a, 3 + rel, q ^ rel, 1 - c, me).wait_recv()
        for a in range(n):
            for slot in range(3, 7):
                copy(a, slot, q, c, sib).wait_send()
        for cp in first:
            cp.wait_send()
        for cp in mine:
            cp.wait()

    return _Job(shards, [jax.ShapeDtypeStruct((N_DEV * s.shape[0], s.shape[1]), s.dtype) for s in shards],
                [_SEMS(7 * n), _SEMS(7 * n), _SEMS(n)], start, finish)


def _pair_job(grads):
    n = len(grads)

    def copies(ins, outs, sems):
        send_sems, recv_sems = sems
        x, y, c = _mesh_pos()
        out = []
        for a in range(n):
            rs = grads[a].shape[0] // N_DEV
            for q in range(4):
                blk = ins[a].at[pl.ds((2 * q + 1 - c) * rs, rs), :]
                out.append(pltpu.make_async_remote_copy(
                    src_ref=blk, dst_ref=outs[a].at[q], send_sem=send_sems.at[4 * a + q], recv_sem=recv_sems.at[4 * a + q],
                    device_id=(x, y, 1 - c), device_id_type=MESH))
        return out

    def start(ins, outs, sems):
        for cp in copies(ins, outs, sems):
            cp.start()

    def finish(ins, outs, sems):
        for cp in copies(ins, outs, sems):
            cp.wait()

    return _Job(grads, [jax.ShapeDtypeStruct((4, g.shape[0] // N_DEV, g.shape[1]), g.dtype) for g in grads],
                [_SEMS(4 * n), _SEMS(4 * n)], start, finish)


def _chip_job(sums):
    n = len(sums)

    def copies(ins, outs, sems):
        send_sems, recv_sems = sems
        x, y, c = _mesh_pos()
        out = []
        for a in range(n):
            for r in (1, 2, 3):
                px, py = (1 - x if r & 2 else x), (1 - y if r & 1 else y)
                out.append(pltpu.make_async_remote_copy(
                    src_ref=ins[a].at[2 * px + py], dst_ref=outs[a].at[r - 1], send_sem=send_sems.at[3 * a + r - 1],
                    recv_sem=recv_sems.at[3 * a + r - 1], device_id=(px, py, c), device_id_type=MESH))
        return out

    def start(ins, outs, sems):
        for cp in copies(ins, outs, sems):
            cp.start()

    def finish(ins, outs, sems):
        for cp in copies(ins, outs, sems):
            cp.wait()

    return _Job(sums, [jax.ShapeDtypeStruct((3,) + s.shape[1:], s.dtype) for s in sums],
                [_SEMS(3 * n), _SEMS(3 * n)], start, finish)


def _mm(name, form, a_list, b, M, N, K, tm, tn, tk, extras, outs, epi, job=None, acc_as_ref=False):
    nI, nJ, nK = M // tm, N // tn, K // tk
    assert nI * tm == M and nJ * tn == N and nK * tk == K
    dims = {"nn": NN, "nt": NT, "tn": TN}[form]
    b_list = b if isinstance(b, list) else [(b, {"nn": N, "nt": K, "tn": N}[form])]
    nA, nB = len(a_list), len(b_list)
    assert nA == 1 or nB == 1
    assert nB == 1 or form in ("nn", "nt")
    AXIS = {"i": 0, "j": 1, "k": 2}
    a_axis, a_tile = ("i", tm) if form == "tn" else ("k", tk)
    b_axis, b_tile = ("k", tk) if form == "nt" else ("j", tn)

    def cut(pieces, tile, total):
        starts, s = [], 0
        for _, w in pieces:
            assert w % tile == 0
            starts.append(s // tile)
            s += w
        assert s == total
        return starts, [w // tile for _, w in pieces]

    a_st, a_cn = cut(a_list, a_tile, M if form == "tn" else K)
    b_st, b_cn = cut(b_list, b_tile, K if form == "nt" else N)

    def inside(idx, st, cn):
        return (idx >= st) & (idx < st + cn)

    def a_spec(p):
        st, cn = a_st[p], a_cn[p]
        if form == "tn":
            return pl.BlockSpec((tk, tm), lambda i, j, k: (jnp.where(inside(i, st, cn), k, 0), jnp.clip(i - st, 0, cn - 1)))
        return pl.BlockSpec((tm, tk), lambda i, j, k: (i, jnp.clip(k - st, 0, cn - 1)))

    def b_spec(p):
        st, cn = b_st[p], b_cn[p]
        if form == "nt":
            return pl.BlockSpec((tn, tk), lambda i, j, k: (j, jnp.clip(k - st, 0, cn - 1)))
        if nB == 1:
            return pl.BlockSpec((tk, tn), lambda i, j, k: (k, j))
        return pl.BlockSpec((tk, tn), lambda i, j, k: (jnp.where(inside(j, st, cn), k, 0), jnp.clip(j - st, 0, cn - 1)))

    in_specs = ([a_spec(p) for p in range(nA)] + [b_spec(p) for p in range(nB)]
                + [pl.BlockSpec(bs, im) for _, bs, im in extras])
    out_shape = [jax.ShapeDtypeStruct(s_, d_) for s_, d_, _, _ in outs]
    out_specs = [pl.BlockSpec(bs, im) for _, _, bs, im in outs]
    nE, nO = len(extras), len(outs)
    single = nA == 1 and nB == 1

    def body(*refs):
        a_refs, b_refs = refs[:nA], refs[nA:nA + nB]
        ex, ou = refs[nA + nB:nA + nB + nE], refs[nA + nB + nE:nA + nB + nE + nO]
        ids = [pl.program_id(a) for a in range(3)]

        def partial_of(p, q):
            return lax.dot_general(a_refs[p][...], b_refs[q][...], dims, preferred_element_type=F32)

        if nK == 1 and single:
            epi(partial_of(0, 0), ex, ou)
            return
        acc = refs[-1]
        k = ids[2]
        for p in range(nA):
            for q in range(nB):
                def first(p=p, q=q):
                    acc[...] = partial_of(p, q)

                def later(p=p, q=q):
                    acc[...] += partial_of(p, q)

                here = None
                if nA > 1:
                    here = inside(ids[AXIS[a_axis]], a_st[p], a_cn[p])
                if nB > 1:
                    here = inside(ids[AXIS[b_axis]], b_st[q], b_cn[q])
                pl.when(k == 0 if here is None else here & (k == 0))(first)
                pl.when(k > 0 if here is None else here & (k > 0))(later)

        @pl.when(k == nK - 1)
        def _():
            epi(acc if acc_as_ref else acc[...], ex, ou)

    scratch = [] if (nK == 1 and single) else [pltpu.VMEM((tm, tn), F32)]
    res, job_res = _pcall(
        body, grid=(nI, nJ, nK), in_specs=in_specs, out_specs=out_specs, out_shape=out_shape, scratch_shapes=scratch,
        name=name, semantics=("parallel", "parallel", "arbitrary"),
        args=[a for a, _ in a_list] + [p for p, _ in b_list] + [e for e, _, _ in extras], job=job)
    return res if job is None else (res, job_res)


def _rms_mod_fwd(name, x, gain, sc, sh, tr):
    T = x.shape[0]

    def body(x_ref, g_ref, sc_ref, sh_ref, h_ref):
        xv = x_ref[...]
        rstd = lax.rsqrt(jnp.mean(xv * xv, axis=-1, keepdims=True) + EPS)
        h_ref[...] = ((xv * rstd * g_ref[...]) * (1.0 + sc_ref[...]) + sh_ref[...]).astype(BF16)

    row = pl.BlockSpec((tr, D), lambda i: (i, 0))
    vec = pl.BlockSpec((1, D), lambda i: (0, 0))
    return pl.pallas_call(
        body, grid=(T // tr,), in_specs=[row, vec, vec, vec], out_specs=row,
        out_shape=jax.ShapeDtypeStruct((T, D), BF16), name=name, compiler_params=_params(("parallel",)),
    )(x, gain, sc, sh)


def _rms_mod_bwd_epilogue(x, gain, sc, dres, tm, gate=None, mo=None):
    T = x.shape[0]
    with_gate = gate is not None
    row = ((tm, D), lambda i, j, k: (i, 0))
    vec = ((1, D), lambda i, j, k: (0, 0))
    part = ((T // tm * 8, D), F32, (8, D), lambda i, j, k: (i, 0))
    extras = [(x, *row), (gain, *vec), (sc, *vec), (dres, *row)]
    outs = [((T, D), F32, *row), part, part, part]
    if with_gate:
        extras += [(gate, *vec), (mo, *row)]
        outs += [((T, D), BF16, *row), part]

    rows = min(128, tm)

    def epi(acc, ex, ou):
        g = ex[1][...]
        sums = [jnp.zeros((8, D), F32) for _ in range(4)]
        for r0 in range(0, tm, rows):
            rs = slice(r0, r0 + rows)
            dhv, xv = acc[rs, :], ex[0][rs, :]
            rstd = lax.rsqrt(jnp.mean(xv * xv, axis=-1, keepdims=True) + EPS)
            xhat = xv * rstd
            dn = dhv * (1.0 + ex[2][...])
            dxhat = dn * g
            dx = ex[3][rs, :] + rstd * (dxhat - xhat * jnp.mean(dxhat * xhat, axis=-1, keepdims=True))
            ou[0][rs, :] = dx
            terms = [dhv, dhv * (xhat * g), dn * xhat]
            if with_gate:
                ou[4][rs, :] = (ex[4][...] * dx).astype(BF16)
                terms.append(dx * ex[5][rs, :].astype(F32))
            sums = [s + _fold8(t) for s, t in zip(sums, terms)] + sums[len(terms):]
        ou[1][...], ou[2][...], ou[3][...] = sums[:3]
        if with_gate:
            ou[5][...] = sums[3]

    return extras, outs, epi


def _rms_mod_bwd(name, dh, x, gain, sc, dres, tr, gate=None, mo=None):
    T = x.shape[0]
    extras, outs, epi = _rms_mod_bwd_epilogue(x, gain, sc, dres, tr, gate, mo)
    rows_only = lambda im: (lambda i: im(i, 0, 0))
    nE = len(extras)

    def body(dh_ref, *refs):
        epi(dh_ref, refs[:nE], refs[nE:])

    return pl.pallas_call(
        body, grid=(T // tr,),
        in_specs=[pl.BlockSpec((tr, D), lambda i: (i, 0))] + [pl.BlockSpec(bs, rows_only(im)) for _, bs, im in extras],
        out_specs=[pl.BlockSpec(bs, rows_only(im)) for _, _, bs, im in outs],
        out_shape=[jax.ShapeDtypeStruct(s, d) for s, d, _, _ in outs], name=name, compiler_params=_params(("parallel",)),
    )(dh, *[e for e, _, _ in extras])


def _split3(v):
    h = v.astype(BF16)
    r1 = v - h.astype(F32)
    m = r1.astype(BF16)
    lo = (r1 - m.astype(F32)).astype(BF16)
    return h, m, lo


def _tri_mm(tri, v, dims=NN):
    h, m, lo = _split3(v)
    t = tri.astype(BF16)
    mm = lambda p: lax.dot_general(t, p, dims, preferred_element_type=F32)
    return (mm(lo) + mm(m)) + mm(h)


def _hgrn_chunk_terms(q, fl, lb):
    sig = _sigmoid(fl)
    f = lb + (1.0 - lb) * sig
    lf = jnp.log(f)
    kk = 1.0 - f
    sq = _sigmoid(q)
    qf = q * sq
    return sig, f, lf, kk, sq, qf


def _causal(n):
    r = lax.broadcasted_iota(jnp.int32, (n, n), 0)
    c = lax.broadcasted_iota(jnp.int32, (n, n), 1)
    return r >= c


def _hgrn_fwd(proj, lb_logits, o_gain, tt, job=None):
    T = proj.shape[0]
    nT, ncl = T // tt, tt // CHUNK
    C = CHUNK

    def body(q_ref, f_ref, i_ref, g_ref, lbl_ref, og_ref, y_ref, st_ref, S):
        @pl.when(pl.program_id(1) == 0)
        def _():
            S[...] = jnp.zeros_like(S)

        lbl = lbl_ref[...]
        lb = _sigmoid(lbl[0:1, :] - lbl[1:2, :])
        og = og_ref[...]
        shp = (ncl, C, A_HD)
        q, fl, v, g = (r[...].reshape(shp) for r in (q_ref, f_ref, i_ref, g_ref))
        tri = jnp.broadcast_to(_causal(C), (ncl, C, C))
        _, _, lf, kk, _, qf = _hgrn_chunk_terms(q, fl, lb)
        b = _tri_mm(tri, lf, BNN)
        bm, bl = b[:, C // 2 - 1:C // 2, :], b[:, C - 1:C, :]
        qd, kd = qf * jnp.exp(b - bm), kk * jnp.exp(bm - b)
        A = jnp.where(tri, _dot(qd, kd, BNT), 0.0)
        d_st = _dot(v, kk * jnp.exp(bl - b), BTN)
        dec = jnp.exp(bl)
        st = S[...]
        for ci in range(ncl):
            st_ref[0, ci] = st
            st = st * dec[ci] + d_st[ci]
        S[...] = st
        o = _dot(A, v, BNN) + _dot(qf * jnp.exp(b), st_ref[0], BNT)
        r = lax.rsqrt(jnp.mean(o * o, axis=-1, keepdims=True) + EPS)
        y_ref[...] = (o * r * og * (g * _sigmoid(g))).astype(BF16).reshape(tt, A_HD)

    def col(off):
        return pl.BlockSpec((tt, A_HD), lambda h, t: (t, off // A_HD + h))

    head_vec = lambda rows: pl.BlockSpec((rows, A_HD), lambda h, t: (0, h))
    return _pcall(
        body, grid=(A_HEADS, nT),
        in_specs=[col(OFF_QA), col(OFF_FA), col(OFF_IA), col(OFF_GA), head_vec(2), head_vec(1)],
        out_specs=[pl.BlockSpec((tt, A_HD), lambda h, t: (t, h)),
                   pl.BlockSpec((1, ncl, A_HD, A_HD), lambda h, t: (h, t, 0, 0))],
        out_shape=[jax.ShapeDtypeStruct((T, AW), BF16),
                   jax.ShapeDtypeStruct((A_HEADS, T // C, A_HD, A_HD), F32)],
        scratch_shapes=[pltpu.VMEM((A_HD, A_HD), F32)], name="hgrn_fwd", semantics=("parallel", "arbitrary"),
        args=[proj, proj, proj, proj, lb_logits, o_gain], job=job)


def _hgrn_bwd(proj, st, dy, lb_logits, o_gain, tt, job=None):
    T = proj.shape[0]
    nT, ncl = T // tt, tt // CHUNK
    C = CHUNK

    def body(q_ref, f_ref, i_ref, g_ref, st_ref, dy_ref, lbl_ref, og_ref,
             dq_ref, df_ref, di_ref, dg_ref, plb_ref, pog_ref, dS):
        @pl.when(pl.program_id(1) == 0)
        def _():
            dS[...] = jnp.zeros_like(dS)

        lbl = lbl_ref[...]
        lb = _sigmoid(lbl[0:1, :] - lbl[1:2, :])
        og = og_ref[...]
        shp = (ncl, C, A_HD)
        flat = lambda t: t.reshape(tt, A_HD)
        q, fl, v, g, dout = (r[...].reshape(shp) for r in (q_ref, f_ref, i_ref, g_ref, dy_ref))
        tri = jnp.broadcast_to(_causal(C), (ncl, C, C))
        rowi = lax.broadcasted_iota(jnp.int32, shp, 1)
        st0 = st_ref[0]
        sig, f, lf, kk, sq, qf = _hgrn_chunk_terms(q, fl, lb)
        b = _tri_mm(tri, lf, BNN)
        bm, bl = b[:, C // 2 - 1:C // 2, :], b[:, C - 1:C, :]
        e_qd, e_kd, e_ke, e_b = jnp.exp(b - bm), jnp.exp(bm - b), jnp.exp(bl - b), jnp.exp(b)
        qd, kd, ke, qe = qf * e_qd, kk * e_kd, kk * e_ke, qf * e_b
        dec = jnp.exp(bl)
        A = jnp.where(tri, _dot(qd, kd, BNT), 0.0)
        o = _dot(A, v, BNN) + _dot(qe, st0, BNT)
        r = lax.rsqrt(jnp.mean(o * o, axis=-1, keepdims=True) + EPS)
        sg = _sigmoid(g)
        on = o * r * og
        dg_ref[...] = flat((dout * on * (sg * (1.0 + g * (1.0 - sg)))).astype(BF16))
        don = dout * (g * sg)
        pog_ref[...] = _fold8(flat(don * o * r))
        dyh = don * og
        do = r * (dyh - o * (r * r) * jnp.mean(dyh * o, axis=-1, keepdims=True))
        g_st = _dot(do, qe, BTN)
        run = dS[...]
        after = [None] * ncl
        for ci in reversed(range(ncl)):
            after[ci] = run
            run = g_st[ci] + run * dec[ci]
        dS[...] = run
        d_after = jnp.stack(after, axis=0)
        ddec = jnp.sum(d_after * st0, axis=1, keepdims=True)
        dqe = _dot(do, st0, BNN)
        dke = _dot(v, d_after, BNN)
        dA = jnp.where(tri, _dot(do, v, BNT), 0.0)
        dv = _dot(ke, d_after, BNT) + _dot(A, do, BTN)
        dqd = _dot(dA, kd, BNN)
        dkd = _dot(dA, qd, BTN)
        di_ref[...] = flat(dv.astype(BF16))
        dqf = dqe * e_b + dqd * e_qd
        dkk = dkd * e_kd + dke * e_ke
        t_qd, t_kd, t_ke = dqd * qd, dkd * kd, dke * ke
        db = dqe * qe + t_qd - t_kd - t_ke
        dbm = jnp.sum(t_kd - t_qd, axis=1, keepdims=True)
        dbl = jnp.sum(t_ke, axis=1, keepdims=True) + ddec * dec
        db = db + jnp.where(rowi == C // 2 - 1, dbm, 0.0) + jnp.where(rowi == C - 1, dbl, 0.0)
        dlf = _tri_mm(tri, db, BTN)
        dfv = dlf / f - dkk
        df_ref[...] = flat((dfv * (1.0 - lb) * sig * (1.0 - sig)).astype(BF16))
        plb_ref[...] = _fold8(flat(dfv * (1.0 - sig)))
        dq_ref[...] = flat((dqf * (sq * (1.0 + q * (1.0 - sq)))).astype(BF16))

    def col(off):
        return pl.BlockSpec((tt, A_HD), lambda h, t: (nT - 1 - t, off // A_HD + h))

    head_vec = lambda rows: pl.BlockSpec((rows, A_HD), lambda h, t: (0, h))
    o_spec = pl.BlockSpec((tt, A_HD), lambda h, t: (nT - 1 - t, h))
    p_spec = pl.BlockSpec((8, A_HD), lambda h, t: (t, h))
    o_shape = jax.ShapeDtypeStruct((T, AW), BF16)
    p_shape = jax.ShapeDtypeStruct((nT * 8, AW), F32)
    return _pcall(
        body, grid=(A_HEADS, nT),
        in_specs=[col(OFF_QA), col(OFF_FA), col(OFF_IA), col(OFF_GA),
                  pl.BlockSpec((1, ncl, A_HD, A_HD), lambda h, t: (h, nT - 1 - t, 0, 0)),
                  pl.BlockSpec((tt, A_HD), lambda h, t: (nT - 1 - t, h)), head_vec(2), head_vec(1)],
        out_specs=[o_spec, o_spec, o_spec, o_spec, p_spec, p_spec],
        out_shape=[o_shape, o_shape, o_shape, o_shape, p_shape, p_shape],
        scratch_shapes=[pltpu.VMEM((A_HD, A_HD), F32)], name="hgrn_bwd", semantics=("parallel", "arbitrary"),
        args=[proj, proj, proj, proj, st, dy, lb_logits, o_gain], job=job)


LANES = 128
Q_COLS = BW // LANES


def _low_half():
    return lax.broadcasted_iota(jnp.int32, (1, LANES), 1) < B_HD


def _half_sum(t, low):
    lo = jnp.sum(jnp.where(low, t, 0.0), axis=-1, keepdims=True)
    hi = jnp.sum(jnp.where(low, 0.0, t), axis=-1, keepdims=True)
    return jnp.where(low, lo, hi)


def _half_rms(t, low):
    r = lax.rsqrt(_half_sum(t * t, low) * (1.0 / B_HD) + EPS)
    return t * r, r


def _fold_halves(p, low):
    return jnp.where(low, p + pltpu.roll(p, B_HD, 1), 0.0)


def _stack_cols(x):
    return jnp.stack([x[:, c * LANES:(c + 1) * LANES] for c in range(Q_COLS)], axis=0).reshape(KV_HEADS, 2 * BLK, LANES)


def _col_of(t, c):
    return t[c // 2, (c % 2) * BLK:(c % 2 + 1) * BLK]


def _split_halves(col, s, low):
    own = jnp.where(low if s == 0 else jnp.logical_not(low), col, 0.0)
    other = pltpu.roll(own, B_HD, 1)
    return (own, other) if s == 0 else (other, own)


def _swa_keys(kp_ref, kc_ref, vp_ref, vc_ref, kg, low):
    k_lo, k_hi, v_lo, v_hi, hats = [], [], [], [], []
    for j in range(KVW // LANES):
        cs = slice(j * LANES, (j + 1) * LANES)
        k_hat, k_r = _half_rms(jnp.concatenate([kp_ref[:, cs], kc_ref[:, cs]], axis=0), low)
        vcol = jnp.concatenate([vp_ref[:, cs], vc_ref[:, cs]], axis=0)
        hats.append((k_hat, k_r))
        for s in range(2):
            for dst_lo, dst_hi, col in ((k_lo, k_hi, k_hat * kg), (v_lo, v_hi, vcol)):
                lo, hi = _split_halves(col, s, low)
                dst_lo.append(lo)
                dst_hi.append(hi)
    st = lambda parts: jnp.stack(parts, axis=0)
    return st(k_lo), st(k_hi), st(v_lo), st(v_hi), hats


def _swa_mask(first_block):
    qi = lax.broadcasted_iota(jnp.int32, (BLK, 2 * BLK), 0) + BLK
    ki = lax.broadcasted_iota(jnp.int32, (BLK, 2 * BLK), 1)
    rel = qi - ki
    m = (rel >= 0) & (rel < BLK) & (jnp.logical_not(first_block) | (ki >= BLK))
    return jnp.concatenate([m, m], axis=0)


def _sink_cols(sk_ref, hi):
    top = lax.broadcasted_iota(jnp.int32, (2 * BLK, 1), 0) < BLK
    return jnp.stack([jnp.where(top, sk_ref[0, GROUP * hk + hi], sk_ref[0, GROUP * hk + 2 + hi])
                      for hk in range(KV_HEADS)], axis=0)


def _swa_probs(qn, k_half, sink, mask):
    s = jnp.where(mask, _dot(qn, k_half, BNT) * (B_HD ** -0.5), NEG)
    m = jnp.maximum(jnp.max(s, axis=-1, keepdims=True), sink)
    p = jnp.exp(s - m)
    ps = jnp.exp(sink - m)
    inv = 1.0 / (jnp.sum(p, axis=-1, keepdims=True) + ps)
    return p * inv, ps * inv


def _swa_fwd(proj, q_gain, k_gain, sinks, job=None):
    T = proj.shape[0]
    nb = T // BLK

    def body(q_ref, kc_ref, kp_ref, vc_ref, vp_ref, qg_ref, kg_ref, sk_ref, o_ref):
        low = _low_half()
        mask = _swa_mask(pl.program_id(0) == 0)
        qn = _half_rms(_stack_cols(q_ref[...]), low)[0] * qg_ref[...]
        k_lo, k_hi, v_lo, v_hi, _ = _swa_keys(kp_ref, kc_ref, vp_ref, vc_ref, kg_ref[...], low)
        p_lo, _ = _swa_probs(qn, k_lo, _sink_cols(sk_ref, 0), mask)
        p_hi, _ = _swa_probs(qn, k_hi, _sink_cols(sk_ref, 1), mask)
        o = (_dot(p_lo, v_lo, BNN) + _dot(p_hi, v_hi, BNN)).astype(BF16)
        for c in range(Q_COLS):
            o_ref[:, c * LANES:(c + 1) * LANES] = _col_of(o, c)

    q_gain, k_gain = jnp.tile(q_gain, (1, 2)), jnp.tile(k_gain, (1, 2))
    cur = lambda w, off: pl.BlockSpec((BLK, w), lambda i: (i, off // w))
    prev = lambda w, off: pl.BlockSpec((BLK, w), lambda i: (jnp.maximum(i - 1, 0), off // w))
    small = lambda n: pl.BlockSpec((1, 2 * n), lambda i: (0, 0))
    return _pcall(
        body, grid=(nb,),
        in_specs=[cur(BW, OFF_QB), cur(KVW, OFF_KB), prev(KVW, OFF_KB), cur(KVW, OFF_VB), prev(KVW, OFF_VB),
                  small(B_HD), small(B_HD), pl.BlockSpec(memory_space=pltpu.SMEM)],
        out_specs=[pl.BlockSpec((BLK, BW), lambda i: (i, 0))],
        out_shape=[jax.ShapeDtypeStruct((T, BW), BF16)], scratch_shapes=[], name="swa_fwd", semantics=("parallel",),
        args=[proj, proj, proj, proj, proj, q_gain, k_gain, sinks], job=job)


def _swa_bwd(proj, dout, q_gain, k_gain, sinks, job=None):
    T = proj.shape[0]
    nb = T // BLK
    W = BW + 2 * KVW

    def body(q_ref, kc_ref, kp_ref, vc_ref, vp_ref, do_ref, qg_ref, kg_ref, sk_ref,
             dq_ref, dkv_ref, pqg_ref, pkg_ref, psk_ref, dkn_c, dv_c):
        i = pl.program_id(0)
        live = i < nb
        low = _low_half()
        high = jnp.logical_not(low)
        qg, kg = qg_ref[...], kg_ref[...]
        mask = _swa_mask(i == 0)
        lane = lax.broadcasted_iota(jnp.int32, (1, LANES), 1)
        scale = B_HD ** -0.5

        @pl.when(i == 0)
        def _():
            dkn_c[...] = jnp.zeros_like(dkn_c)
            dv_c[...] = jnp.zeros_like(dv_c)

        q_hat, q_r = _half_rms(_stack_cols(q_ref[...]), low)
        qn = q_hat * qg
        k_lo, k_hi, v_lo, v_hi, hats = _swa_keys(kp_ref, kc_ref, vp_ref, vc_ref, kg, low)
        do = _stack_cols(do_ref[...])
        dqn = jnp.zeros((KV_HEADS, 2 * BLK, LANES), F32)
        acc_sk = jnp.zeros((1, LANES), F32)
        dk_parts, dv_parts = [], []
        for hi, (k_h, v_h) in enumerate(((k_lo, v_lo), (k_hi, v_hi))):
            p, ps = _swa_probs(qn, k_h, _sink_cols(sk_ref, hi), mask)
            dp = _dot(do, v_h, BNT)
            delta = jnp.sum(p * dp, axis=-1, keepdims=True)
            ds = p * (dp - delta) * scale
            dqn = dqn + _dot(ds, k_h, BNN)
            dk_parts.append(_dot(ds, qn, BTN))
            dv_parts.append(_dot(p, do, BTN))
            t = ps * delta
            for hk in range(KV_HEADS):
                for rows in range(2):
                    h = GROUP * hk + 2 * rows + hi
                    acc_sk = acc_sk + jnp.where(
                        lane == h, -jnp.sum(t[hk, rows * BLK:(rows + 1) * BLK], axis=0, keepdims=True), 0.0)
        dqh = dqn * qg
        dq = (q_r * (dqh - q_hat * (_half_sum(dqh * q_hat, low) * (1.0 / B_HD)))).astype(BF16)
        for c in range(Q_COLS):
            dq_ref[:, c * LANES:(c + 1) * LANES] = _col_of(dq, c)
        acc_qg = _fold_halves(_fold8((dqn * q_hat).reshape(KV_HEADS * 2 * BLK, LANES)), low)

        def native(parts, j):
            lo_arr, hi_arr = parts
            a, b = 2 * j, 2 * j + 1
            return (jnp.where(low, lo_arr[a], 0.0) + pltpu.roll(jnp.where(high, hi_arr[a], 0.0), B_HD, 1)
                    + jnp.where(high, hi_arr[b], 0.0) + pltpu.roll(jnp.where(low, lo_arr[b], 0.0), B_HD, 1))

        acc_kg = jnp.zeros((8, LANES), F32)
        for j in range(KVW // LANES):
            cs = slice(j * LANES, (j + 1) * LANES)
            dkn = jnp.where(live, native(dk_parts, j), 0.0)
            dvc = jnp.where(live, native(dv_parts, j), 0.0)
            kp_hat, kp_r = hats[j][0][:BLK], hats[j][1][:BLK]
            dkn_prev = dkn_c[:, cs] + dkn[:BLK]
            dv_prev = dv_c[:, cs] + dvc[:BLK]
            acc_kg = acc_kg + _fold8(dkn_prev * kp_hat)
            dkh = dkn_prev * kg
            dkv_ref[:, cs] = (kp_r * (dkh - kp_hat * (_half_sum(dkh * kp_hat, low) * (1.0 / B_HD)))).astype(BF16)
            dkv_ref[:, KVW + j * LANES:KVW + (j + 1) * LANES] = dv_prev.astype(BF16)
            dkn_c[:, cs] = dkn[BLK:]
            dv_c[:, cs] = dvc[BLK:]
        keep = jnp.where(i > 0, 1.0, 0.0)
        pqg_ref[...] = jnp.where(live, acc_qg, 0.0)
        pkg_ref[...] = _fold_halves(acc_kg, low) * keep
        psk_ref[...] = jnp.broadcast_to(jnp.where(live, acc_sk, 0.0), (8, LANES)) * (
            lax.broadcasted_iota(jnp.int32, (8, LANES), 0) == 0).astype(F32)

    q_gain, k_gain = jnp.tile(q_gain, (1, 2)), jnp.tile(k_gain, (1, 2))
    last = nb - 1
    cur = lambda w, off: pl.BlockSpec((BLK, w), lambda i: (jnp.minimum(i, last), off // w))
    prev = lambda w, off: pl.BlockSpec((BLK, w), lambda i: (jnp.maximum(i - 1, 0), off // w))
    small = lambda n: pl.BlockSpec((1, 2 * n), lambda i: (0, 0))
    part = pl.BlockSpec((8, 128), lambda i: (i, 0))
    p_shape = jax.ShapeDtypeStruct(((nb + 1) * 8, 128), F32)
    return _pcall(
        body, grid=(nb + 1,),
        in_specs=[cur(BW, OFF_QB), cur(KVW, OFF_KB), prev(KVW, OFF_KB), cur(KVW, OFF_VB), prev(KVW, OFF_VB),
                  pl.BlockSpec((BLK, BW), lambda i: (jnp.minimum(i, last), 0)), small(B_HD), small(B_HD),
                  pl.BlockSpec(memory_space=pltpu.SMEM)],
        out_specs=[pl.BlockSpec((BLK, BW), lambda i: (i, 0)),
                   pl.BlockSpec((BLK, 2 * KVW), lambda i: (jnp.maximum(i - 1, 0), 0)), part, part, part],
        out_shape=[jax.ShapeDtypeStruct((T + BLK, BW), BF16), jax.ShapeDtypeStruct((T, 2 * KVW), BF16),
                   p_shape, p_shape, p_shape],
        scratch_shapes=[pltpu.VMEM((BLK, KVW), F32), pltpu.VMEM((BLK, KVW), F32)], name="swa_bwd",
        semantics=("arbitrary",), args=[proj, proj, proj, proj, proj, dout, q_gain, k_gain, sinks], job=job)


def _branch_merge(ya_pre, attn, wa_t, wb_t, proj, tm, tn, job=None):
    T = ya_pre.shape[0]

    def body(a_ref, b_ref, wa_ref, wb_ref, ga_ref, gb_ref, ya_ref, yb_ref, mg_ref):
        ya = lax.dot_general(a_ref[...], wa_ref[...], NT, preferred_element_type=F32)
        yb = lax.dot_general(b_ref[...], wb_ref[...], NT, preferred_element_type=F32)
        ya_ref[...] = ya.astype(BF16)
        yb_ref[...] = yb.astype(BF16)
        mg_ref[...] = (_sigmoid(ga_ref[...]) * ya + _sigmoid(gb_ref[...]) * yb).astype(BF16)

    o_spec = pl.BlockSpec((tm, tn), lambda i, j: (i, j))
    o_shape = jax.ShapeDtypeStruct((T, D), BF16)
    return _pcall(
        body, grid=(T // tm, D // tn),
        in_specs=[pl.BlockSpec((tm, AW), lambda i, j: (i, 0)), pl.BlockSpec((tm, BW), lambda i, j: (i, 0)),
                  pl.BlockSpec((tn, AW), lambda i, j: (j, 0)), pl.BlockSpec((tn, BW), lambda i, j: (j, 0)),
                  pl.BlockSpec((tm, tn), lambda i, j: (i, OFF_GTA // tn + j)),
                  pl.BlockSpec((tm, tn), lambda i, j: (i, OFF_GTB // tn + j))],
        out_specs=[o_spec, o_spec, o_spec], out_shape=[o_shape, o_shape, o_shape], scratch_shapes=[], name="branch_merge",
        semantics=("parallel", "parallel"), args=[ya_pre, attn, wa_t, wb_t, proj, proj], job=job)


def _ij(i, j, k):
    return (i, j)


def _local_step(x, tgt, mod, g1, g2, lbl, og, qg, kg, sk, shards, me, c_arr):
    win_s, wa_s, wb_s, wout_s, wmi_s, wmo_s = shards
    T = x.shape[0]
    tm, tr, tt = min(1024, T), min(256, T), min(512, T)
    tk_t = min(1024, T)
    tn = 512
    sh1, sc1, gt1, sh2, sc2, gt2 = (mod[:, i * D:(i + 1) * D] for i in range(N_MOD))
    nI = T // tm
    blk = (tm, tn)
    part = lambda: ((nI * 8, D), F32, (8, tn), _ij)
    vec_j = ((1, tn), lambda i, j, k: (0, j))

    h = _rms_mod_fwd("rms1_fwd", x, g1, sc1, sh1, tr)

    def epi_store(acc, ex, ou):
        ou[0][...] = acc.astype(ou[0].dtype)

    tm2 = min(2048, T)
    blk2 = (tm2, tn)

    full = lambda s: (0, s.shape[0])
    last = wmi_s.shape[0]
    (win_t,) = _run_job("gather_w_in", _gather_relay_job([win_s]))
    (proj,), (wa_t, wb_t, w_out, wmi_part) = _mm(
        "in_proj", "nt", [(h, D)], win_t, T, IN_W, D, tm2, tn, D, [], [((T, IN_W), F32, blk2, _ij)], epi_store,
        job=_gather_job([wa_s, wb_s, wout_s, wmi_s], rows=[full(wa_s), full(wb_s), full(wout_s), (0, MI_CUTS[0])]))
    (ya_pre, st), (wmi_part,) = _hgrn_fwd(
        proj, lbl, og, tt, job=_gather_job([wmi_s], rows=[MI_CUTS], into=[wmi_part]))
    (attn,), (wmi_t, wmo_part) = _swa_fwd(
        proj, qg, kg, sk, job=_gather_job([wmi_s, wmo_s], rows=[(MI_CUTS[1], last), (0, MO_CUTS[0])], into=[wmi_part, None]))
    (ya, yb, merged), (wmo_part,) = _branch_merge(
        ya_pre, attn, wa_t, wb_t, proj, tm, tn, job=_gather_job([wmo_s], rows=[MO_CUTS[:2]], into=[wmo_part]))

    def epi_res1(acc, ex, ou):
        x_ref, gt_ref = ex
        ou[0][...] = acc.astype(BF16)
        ou[1][...] = x_ref[...] + gt_ref[...] * acc

    (mo, x1), (wmo_part,) = _mm(
        "out_proj", "nn", [(merged, D)], w_out, T, D, D, tm, tn, D, [(x, blk, _ij), (gt1, *vec_j)],
        [((T, D), BF16, blk, _ij), ((T, D), F32, blk, _ij)], epi_res1,
        job=_gather_job([wmo_s], rows=[MO_CUTS[1:]], into=[wmo_part]))
    h2 = _rms_mod_fwd("rms2_fwd", x1, g2, sc2, sh2, tr)

    def epi_relu2(acc, ex, ou):
        r = jnp.maximum(acc, 0.0)
        ou[0][...] = r.astype(BF16)
        ou[1][...] = (r * r).astype(BF16)

    (r, a), (w_mo,) = _mm("mlp_in", "nt", [(h2, D)], wmi_t, T, HID, D, tm2, tn, D, [],
                          [((T, HID), BF16, blk2, _ij), ((T, HID), BF16, blk2, _ij)], epi_relu2,
                          job=_gather_job([wmo_s], rows=[(MO_CUTS[2], last)], into=[wmo_part]))

    def epi_loss(acc, ex, ou):
        x1_ref, t_ref, gt_ref = ex
        e = x1_ref[...] + gt_ref[...] * acc - t_ref[...]
        dy = e * (1.0 / D)
        ou[0][...] = dy
        ou[1][...] = (gt_ref[...] * dy).astype(BF16)
        ou[2][...] = _fold8(e * e) * (0.5 / D)
        ou[3][...] = _fold8(dy * acc)

    wide = (tm, 1024)
    part_w = ((nI * 8, D), F32, (8, 1024), _ij)
    dy, dz, p_loss, p_gt2 = _mm(
        "mlp_out", "nn", [(a, HID)], w_mo, T, D, HID, tm, 1024, 1024,
        [(x1, wide, _ij), (tgt, wide, _ij), (gt2, (1, 1024), lambda i, j, k: (0, j))],
        [((T, D), F32, wide, _ij), ((T, D), BF16, wide, _ij), part_w, part_w], epi_loss)

    def epi_du(acc, ex, ou):
        ou[0][...] = (acc * (2.0 * ex[0][...].astype(F32))).astype(BF16)

    (du,) = _mm("mlp_out_dx", "nt", [(dz, D)], w_mo, T, HID, D, tm2, tn, D, [(r, blk2, _ij)],
                [((T, HID), BF16, blk2, _ij)], epi_du)
    gblk = (1024, 1024)
    gwide = (1024, D)
    pair_sum = lambda nm, g, r1: _pair_sum("pair_sum_" + nm, g, r1, c_arr, _sum_rows(r1.shape[1]))
    (g_mo,) = _mm("mlp_out_dw", "tn", [(a, HID)], dz, HID, D, T, 1024, D, tk_t, [], [((HID, D), BF16, gwide, _ij)], epi_store)
    (dh2,), (r1_mo,) = _mm("mlp_in_dx", "nn", [(du, HID)], wmi_t, T, D, HID, tm, 1024, 1024, [],
                           [((T, D), F32, (tm, 1024), _ij)], epi_store, job=_pair_job([g_mo]))
    dx1, p_sh2, p_sc2, p_g2, dmo, p_gt1 = _rms_mod_bwd("rms2_bwd", dh2, x1, g2, sc2, dy, tr, gate=gt1, mo=mo)
    s_mo = pair_sum("mlp_out", g_mo, r1_mo)
    tm_row = min(512, T)
    (g_mi,), (r2_mo,) = _mm("mlp_in_dw", "tn", [(du, HID)], h2, HID, D, T, 1024, D, tk_t, [],
                            [((HID, D), BF16, gwide, _ij)], epi_store, job=_chip_job([s_mo]))

    def epi_gates(acc, ex, ou):
        ya_ref, yb_ref, ga_ref, gb_ref = ex
        sa, sb = _sigmoid(ga_ref[...]), _sigmoid(gb_ref[...])
        ou[0][...] = (acc * sa).astype(BF16)
        ou[1][...] = (acc * sb).astype(BF16)
        ou[2][...] = (acc * ya_ref[...].astype(F32) * (sa * (1.0 - sa))).astype(BF16)
        ou[3][...] = (acc * yb_ref[...].astype(F32) * (sb * (1.0 - sb))).astype(BF16)

    o_bf = ((T, D), BF16, blk, _ij)
    (dya, dyb, dga, dgb), (r1_mi,) = _mm(
        "out_proj_dx", "nt", [(dmo, D)], w_out, T, D, D, tm, tn, D,
        [(ya, blk, _ij), (yb, blk, _ij), (proj, blk, lambda i, j, k: (i, OFF_GTA // tn + j)),
         (proj, blk, lambda i, j, k: (i, OFF_GTB // tn + j))], [o_bf, o_bf, o_bf, o_bf], epi_gates,
        job=_pair_job([g_mi]))
    s_mi = pair_sum("mlp_in", g_mi, r1_mi)
    (g_out,) = _mm("out_proj_dw", "tn", [(merged, D)], dmo, D, D, T, 1024, 1024, tk_t, [], [((D, D), BF16, gblk, _ij)], epi_store)
    (dya_pre,) = _mm("branch_a_dx", "nn", [(dya, D)], wa_t, T, AW, D, tm, tn, D, [], [((T, AW), F32, blk, _ij)], epi_store)
    (dattn,) = _mm("branch_b_dx", "nn", [(dyb, D)], wb_t, T, BW, D, tm, tn, D, [], [((T, BW), F32, blk, _ij)], epi_store)
    (g_a,) = _mm("branch_a_dw", "tn", [(dya, D)], ya_pre, D, AW, T, 1024, 1024, tk_t, [], [((D, AW), BF16, gblk, _ij)], epi_store)
    (g_b,) = _mm("branch_b_dw", "tn", [(dyb, D)], attn, D, BW, T, 1024, 1024, tk_t, [], [((D, BW), BF16, gblk, _ij)], epi_store)
    (dqa, dfa, dia, dgg, p_lb, p_og), (r2_mi,) = _hgrn_bwd(proj, st, dya_pre, lbl, og, tt, job=_chip_job([s_mi]))
    (dqb, dkv, p_qg, p_kg, p_sk), (r1_out, r1_a, r1_b) = _swa_bwd(proj, dattn, qg, kg, sk, job=_pair_job([g_out, g_a, g_b]))
    s_out, s_a, s_b = pair_sum("out", g_out, r1_out), pair_sum("branch_a", g_a, r1_a), pair_sum("branch_b", g_b, r1_b)
    pieces = [(dqa, AW), (dfa, AW), (dia, AW), (dgg, AW), (dqb, BW), (dkv, 2 * KVW), (dga, D), (dgb, D)]
    (g_in,), (r2_out, r2_a, r2_b) = _mm(
        "in_proj_dw", "tn", pieces, h, IN_W, D, T, 512, D, tk_t, [], [((IN_W, D), BF16, (512, D), _ij)], epi_store,
        job=_chip_job([s_out, s_a, s_b]))
    (r1_in,) = _run_job("pair_w_in", _pair_job([g_in]))
    s_in = pair_sum("in", g_in, r1_in)
    extras, outs, epi = _rms_mod_bwd_epilogue(x, g1, sc1, dx1, tm_row)
    (dx, p_sh1, p_sc1, p_g1), (r2_in,) = _mm(
        "in_proj_dx", "nn", pieces, win_t, T, D, IN_W, tm_row, D, 512, extras, outs, epi, job=_chip_job([s_in]),
        acc_as_ref=True)

    partials = dict(sh1=p_sh1, sc1=p_sc1, gt1=p_gt1, sh2=p_sh2, sc2=p_sc2, gt2=p_gt2, g1=p_g1, g2=p_g2,
                    lb=p_lb, og=p_og, qg=p_qg, kg=p_kg, sk=p_sk, loss=p_loss)
    sums = dict(w_in=(s_in, r2_in), w_branch_a=(s_a, r2_a), w_branch_b=(s_b, r2_b), w_out=(s_out, r2_out),
                w_mlp_in=(s_mi, r2_mi), w_mlp_out=(s_mo, r2_mo))
    return dx, sums, partials


def _exchange_slots(buf, send_sems, recv_sems):
    me = _mesh_pos()
    mine = buf.at[_index(me)]
    sends = []
    for k in range(1, N_DEV):
        cp = pltpu.make_async_remote_copy(src_ref=mine, dst_ref=mine, send_sem=send_sems.at[k - 1],
                                          recv_sem=recv_sems.at[k - 1], device_id=_flip(me, k), device_id_type=MESH)
        cp.start()
        sends.append(cp)
    for k in range(1, N_DEV):
        theirs = buf.at[_index(_flip(me, k))]
        pltpu.make_async_remote_copy(src_ref=theirs, dst_ref=theirs, send_sem=send_sems.at[k - 1],
                                     recv_sem=recv_sems.at[k - 1], device_id=_flip(me, k), device_id_type=MESH).wait_recv()
    for cp in sends:
        cp.wait_send()


ADA_W = N_MOD * D // N_DEV


def _ada_mod(c, w_ada, b_shard):
    def body(c_ref, w_ref, b_ref, mod_ref, sc_ref, cbuf, mbuf, s1, r1, s2, r2):
        me = _index(_mesh_pos())
        cbuf[me] = c_ref[...]
        _exchange_slots(cbuf, s1, r1)
        row = lax.broadcasted_iota(jnp.int32, (N_DEV, D), 0)
        call = jnp.zeros((N_DEV, D), F32)
        for d in range(N_DEV):
            call = jnp.where(row == d, cbuf[d], call)
        sc = call * _sigmoid(call)
        sc_ref[...] = sc
        mbuf[me] = _dot(sc, w_ref[...]) + b_ref[...]
        _exchange_slots(mbuf, s2, r2)
        for s in range(N_DEV):
            mod_ref[:, s * ADA_W:(s + 1) * ADA_W] = mbuf[s, pl.ds(me, 1), :]

    return pl.pallas_call(
        body, in_specs=[_VMEM, _VMEM, _VMEM], out_specs=[_VMEM, _VMEM],
        out_shape=[jax.ShapeDtypeStruct((1, N_MOD * D), F32), jax.ShapeDtypeStruct((N_DEV, D), F32)],
        scratch_shapes=[pltpu.VMEM((N_DEV, 1, D), F32), pltpu.VMEM((N_DEV, N_DEV, ADA_W), F32),
                        _SEMS(N_DEV - 1), _SEMS(N_DEV - 1), _SEMS(N_DEV - 1), _SEMS(N_DEV - 1)],
        name="ada_mod", compiler_params=pltpu.CompilerParams(vmem_limit_bytes=VMEM_LIMIT),
    )(c, w_ada, b_shard)


SMALL_SEGS = (("b_ada", N_MOD * D), ("norm1_gain", D), ("norm2_gain", D), ("lb0", AW), ("lb1", AW),
              ("hgrn_o_gain", AW), ("q_norm_gain", 128), ("k_norm_gain", 128), ("sinks", 128))
SMALL_W = sum(w for _, w in SMALL_SEGS)
X_SEGS = (("sh1", D), ("sc1", D), ("gt1", D), ("sh2", D), ("sc2", D), ("gt2", D), ("g1", D), ("g2", D),
          ("lb", AW), ("og", AW), ("qg", 128), ("kg", 128), ("sk", 128), ("loss", 128))
X_W = sum(w for _, w in X_SEGS)


def _offsets(segs):
    out, o = {}, 0
    for name, w in segs:
        out[name] = (o, w)
        o += w
    return out


def _small_reduce(parts, lb_logits):
    xo, so = _offsets(X_SEGS), _offsets(SMALL_SEGS)
    names = [nm for nm, _ in X_SEGS]

    def body(*refs):
        p_refs = dict(zip(names, refs[:len(names)]))
        lbl_ref, allx, gs_ref, loss_ref, send_sems, recv_sems = refs[len(names):]
        me = _index(_mesh_pos())
        for nm, (o, w) in xo.items():
            if nm == "loss":
                allx[me, :, o:o + w] = jnp.broadcast_to(jnp.sum(p_refs[nm][...]), (1, w))
            else:
                allx[me, :, o:o + w] = jnp.sum(p_refs[nm][...], axis=0, keepdims=True)
        _exchange_slots(allx, send_sems, recv_sems)
        tot = allx[0]
        for d in range(1, N_DEV):
            tot = tot + allx[d]
        seg = lambda nm: tot[:, xo[nm][0]:xo[nm][0] + xo[nm][1]]

        def put(nm, v):
            gs_ref[:, so[nm][0]:so[nm][0] + so[nm][1]] = v

        put("b_ada", tot[:, 0:N_MOD * D])
        put("norm1_gain", seg("g1"))
        put("norm2_gain", seg("g2"))
        lbl = lbl_ref[...]
        lb = _sigmoid(lbl[0:1, :] - lbl[1:2, :])
        dl0 = seg("lb") * lb * (1.0 - lb)
        put("lb0", dl0)
        put("lb1", -dl0)
        put("hgrn_o_gain", seg("og"))
        put("q_norm_gain", seg("qg"))
        put("k_norm_gain", seg("kg"))
        put("sinks", seg("sk"))
        loss_ref[...] = seg("loss")

    return pl.pallas_call(
        body, in_specs=[_VMEM] * (len(names) + 1), out_specs=[_VMEM, _VMEM, _VMEM],
        out_shape=[jax.ShapeDtypeStruct((N_DEV, 1, X_W), F32), jax.ShapeDtypeStruct((1, SMALL_W), F32),
                   jax.ShapeDtypeStruct((1, 128), F32)],
        scratch_shapes=[_SEMS(N_DEV - 1), _SEMS(N_DEV - 1)], name="small_reduce",
        compiler_params=pltpu.CompilerParams(vmem_limit_bytes=VMEM_LIMIT),
    )(*[parts[nm] for nm in names], lb_logits)


def _adamw_math(w, g, m, v):
    m = B1 * m + (1.0 - B1) * g
    v = B2 * v + (1.0 - B2) * (g * g)
    m_hat = m / (1.0 - B1 ** STEP)
    v_hat = v / (1.0 - B2 ** STEP)
    return -LR * (m_hat / (jnp.sqrt(v_hat) + ADAM_EPS) + WD * w), m, v


def _sum_rows(rs):
    return 256 if rs % 256 == 0 else rs // 2


def _pair_sum(name, g, recv, c_arr, tr):
    _, rs, cols = recv.shape
    blk = (1, tr, cols)

    def body(c_ref, g_ref, r_ref, o_ref):
        o_ref[...] = (g_ref[...].astype(F32) + r_ref[...].astype(F32)).astype(BF16)

    grid_spec = pltpu.PrefetchScalarGridSpec(
        num_scalar_prefetch=1, grid=(4, rs // tr),
        in_specs=[pl.BlockSpec(blk, lambda q, i, c: (2 * q + c[0], i, 0)), pl.BlockSpec(blk, lambda q, i, c: (q, i, 0))],
        out_specs=pl.BlockSpec(blk, lambda q, i, c: (q, i, 0)))
    return pl.pallas_call(body, grid_spec=grid_spec, out_shape=jax.ShapeDtypeStruct((4, rs, cols), BF16), name=name,
                          compiler_params=_params(("parallel", "parallel")))(c_arr, g.reshape(N_DEV, rs, cols), recv)


def _sum_adamw(name, sums, recv, q_arr, w, m, v, transposed, tile):
    rows, cols = w.shape

    def body(q_ref, s_ref, r_ref, w_ref, m_ref, v_ref, g_ref, d_ref, nm_ref, nv_ref):
        g = ((s_ref[0].astype(F32) + r_ref[0].astype(F32)) + r_ref[1].astype(F32)) + r_ref[2].astype(F32)
        g = g.T if transposed else g
        g_ref[...] = g
        d_ref[...], nm_ref[...], nv_ref[...] = _adamw_math(w_ref[...], g, m_ref[...], v_ref[...])

    if transposed:
        s_spec = pl.BlockSpec((1, cols, tile), lambda i, q: (q[0], 0, i))
        r_spec = pl.BlockSpec((3, cols, tile), lambda i, q: (0, 0, i))
    else:
        s_spec = pl.BlockSpec((1, tile, cols), lambda i, q: (q[0], i, 0))
        r_spec = pl.BlockSpec((3, tile, cols), lambda i, q: (0, i, 0))
    spec = pl.BlockSpec((tile, cols), lambda i, q: (i, 0))
    shape = jax.ShapeDtypeStruct((rows, cols), F32)
    grid_spec = pltpu.PrefetchScalarGridSpec(num_scalar_prefetch=1, grid=(rows // tile,),
                                             in_specs=[s_spec, r_spec, spec, spec, spec], out_specs=[spec] * 4)
    return pl.pallas_call(body, grid_spec=grid_spec, out_shape=[shape] * 4, name=name,
                          compiler_params=_params(("parallel",)))(q_arr, sums, recv, w, m, v)


def _adamw(name, w, g, m, v, tr):
    rows, cols = w.shape

    def body(w_ref, g_ref, m_ref, v_ref, d_ref, nm_ref, nv_ref):
        d_ref[...], nm_ref[...], nv_ref[...] = _adamw_math(w_ref[...], g_ref[...], m_ref[...], v_ref[...])

    spec = pl.BlockSpec((tr, cols), lambda i: (i, 0))
    shape = jax.ShapeDtypeStruct((rows, cols), F32)
    return pl.pallas_call(
        body, grid=(rows // tr,), in_specs=[spec] * 4, out_specs=[spec] * 3, out_shape=[shape] * 3, name=name,
        compiler_params=_params(("parallel",)),
    )(w, g, m, v)


def _ada_update(sc_t, dmod_cols, w, m, v, tr):
    rows, cols = w.shape

    def body(s_ref, d_ref, w_ref, m_ref, v_ref, g_ref, dl_ref, nm_ref, nv_ref):
        g = jnp.dot(s_ref[...], d_ref[...], precision=lax.Precision.HIGHEST, preferred_element_type=F32)
        g_ref[...] = g
        dl_ref[...], nm_ref[...], nv_ref[...] = _adamw_math(w_ref[...], g, m_ref[...], v_ref[...])

    spec = pl.BlockSpec((tr, cols), lambda i: (i, 0))
    shape = jax.ShapeDtypeStruct((rows, cols), F32)
    return pl.pallas_call(
        body, grid=(rows // tr,),
        in_specs=[pl.BlockSpec((tr, N_DEV), lambda i: (i, 0)), pl.BlockSpec((N_DEV, cols), lambda i: (0, 0)), spec, spec, spec],
        out_specs=[spec] * 4, out_shape=[shape] * 4, name="ada_update", compiler_params=_params(("parallel",)),
    )(sc_t, dmod_cols, w, m, v)


BIG = ("w_in", "w_branch_a", "w_branch_b", "w_out", "w_mlp_in", "w_mlp_out")
COLUMN_SHARDED = ("w_in", "w_branch_a", "w_branch_b", "w_mlp_in")
WEIGHTS = ("w_ada", "b_ada", "norm1_gain", "w_in", "lb_logits", "hgrn_o_gain", "q_norm_gain", "k_norm_gain", "sinks",
           "w_branch_a", "w_branch_b", "w_out", "norm2_gain", "w_mlp_in", "w_mlp_out")


def _pack_small(p):
    lb = p["lb_logits"]
    src = dict(p, lb0=lb[0:1], lb1=lb[1:2])
    return jnp.concatenate([jnp.pad(src[nm], ((0, 0), (0, w - src[nm].shape[1]))) for nm, w in SMALL_SEGS], axis=1)


def _unpack_small(vec, shapes):
    so = _offsets(SMALL_SEGS)
    out = {}
    for nm, shp in shapes.items():
        if nm == "lb_logits":
            o = so["lb0"][0]
            out[nm] = vec[0, o:o + 2 * AW].reshape(2, AW)
        else:
            o = so[nm][0]
            out[nm] = vec[:, o:o + shp[1]]
    return out


def kernel(x, c, w_ada, b_ada, norm1_gain, w_in, lb_logits, hgrn_o_gain, q_norm_gain, k_norm_gain, sinks, w_branch_a, w_branch_b, w_out, norm2_gain, w_mlp_in, w_mlp_out, loss_target, m_w_ada, m_b_ada, m_norm1_gain, m_w_in, m_lb_logits, m_hgrn_o_gain, m_q_norm_gain, m_k_norm_gain, m_sinks, m_w_branch_a, m_w_branch_b, m_w_out, m_norm2_gain, m_w_mlp_in, m_w_mlp_out, v_w_ada, v_b_ada, v_norm1_gain, v_w_in, v_lb_logits, v_hgrn_o_gain, v_q_norm_gain, v_k_norm_gain, v_sinks, v_w_branch_a, v_w_branch_b, v_w_out, v_norm2_gain, v_w_mlp_in, v_w_mlp_out):
    w = dict(w_ada=w_ada, b_ada=b_ada, norm1_gain=norm1_gain, w_in=w_in, lb_logits=lb_logits, hgrn_o_gain=hgrn_o_gain,
             q_norm_gain=q_norm_gain, k_norm_gain=k_norm_gain, sinks=sinks, w_branch_a=w_branch_a, w_branch_b=w_branch_b,
             w_out=w_out, norm2_gain=norm2_gain, w_mlp_in=w_mlp_in, w_mlp_out=w_mlp_out)
    m = dict(w_ada=m_w_ada, b_ada=m_b_ada, norm1_gain=m_norm1_gain, w_in=m_w_in, lb_logits=m_lb_logits,
             hgrn_o_gain=m_hgrn_o_gain, q_norm_gain=m_q_norm_gain, k_norm_gain=m_k_norm_gain, sinks=m_sinks,
             w_branch_a=m_w_branch_a, w_branch_b=m_w_branch_b, w_out=m_w_out, norm2_gain=m_norm2_gain,
             w_mlp_in=m_w_mlp_in, w_mlp_out=m_w_mlp_out)
    v = dict(w_ada=v_w_ada, b_ada=v_b_ada, norm1_gain=v_norm1_gain, w_in=v_w_in, lb_logits=v_lb_logits,
             hgrn_o_gain=v_hgrn_o_gain, q_norm_gain=v_q_norm_gain, k_norm_gain=v_k_norm_gain, sinks=v_sinks,
             w_branch_a=v_w_branch_a, w_branch_b=v_w_branch_b, w_out=v_w_out, norm2_gain=v_norm2_gain,
             w_mlp_in=v_w_mlp_in, w_mlp_out=v_w_mlp_out)
    for d in (w, m, v):
        for nm in ("w_ada",) + BIG:
            d[nm] = d[nm][0]
    px, py, pc = _mesh_pos()
    me = _index((px, py, pc))
    c_arr = jnp.reshape(pc, (1,)).astype(jnp.int32)
    q_arr = jnp.reshape(2 * px + py, (1,)).astype(jnp.int32)

    shards = [(w[nm].T if nm in COLUMN_SHARDED else w[nm]).astype(BF16) for nm in BIG]
    b_shard = lax.dynamic_slice(b_ada, (0, me * ADA_W), (1, ADA_W))
    mod, sc_all = _ada_mod(c, w["w_ada"], b_shard)

    dx, sums, parts = _local_step(x[0], loss_target[0], mod, norm1_gain, norm2_gain, lb_logits, hgrn_o_gain,
                                  q_norm_gain, k_norm_gain, sinks, shards, me, c_arr)

    allx, g_small, loss = _small_reduce(parts, lb_logits)

    grad, delta, new_m, new_v = {}, {}, {}, {}
    for nm in BIG:
        s, r2 = sums[nm]
        grad[nm], delta[nm], new_m[nm], new_v[nm] = _sum_adamw(
            "adamw_" + nm, s, r2, q_arr, w[nm], m[nm], v[nm], nm in COLUMN_SHARDED, 128)

    dmod_cols = lax.dynamic_slice(allx[:, 0, :], (0, me * ADA_W), (N_DEV, ADA_W))
    grad["w_ada"], delta["w_ada"], new_m["w_ada"], new_v["w_ada"] = _ada_update(
        sc_all.T, dmod_cols, w["w_ada"], m["w_ada"], v["w_ada"], 256)

    small_names = [nm for nm in WEIGHTS if nm not in BIG and nm != "w_ada"]
    shapes = {nm: w[nm].shape for nm in small_names}
    ds, ms, vs = _adamw("adamw_small", _pack_small(w), g_small, _pack_small(m), _pack_small(v), 1)
    for dst, vec in ((grad, g_small), (delta, ds), (new_m, ms), (new_v, vs)):
        dst.update(_unpack_small(vec, shapes))

    def full(d, nm):
        return d[nm][None] if nm in BIG or nm == "w_ada" else d[nm]

    return (loss[0, 0], dx[None], *[full(grad, nm) for nm in WEIGHTS], *[full(delta, nm) for nm in WEIGHTS],
            *[full(new_m, nm) for nm in WEIGHTS], *[full(new_v, nm) for nm in WEIGHTS])
```

```python
import functools

import jax
import jax.numpy as jnp
from jax import lax
from jax.experimental import pallas as pl
from jax.experimental.pallas import tpu as pltpu

F32 = jnp.float32
BF16 = jnp.bfloat16
MESH = pl.DeviceIdType.MESH

N_DEV = 8
D = 2048
A_HEADS, A_HD, CHUNK = 8, 128, 64
AW = A_HEADS * A_HD
Q_HEADS, KV_HEADS, GROUP, B_HD, BLK = 16, 4, 4, 64, 128
BW = Q_HEADS * B_HD
KVW = KV_HEADS * B_HD
HID = 4 * D
IN_W = 4 * AW + BW + 2 * KVW + 2 * D
OFF_QA, OFF_FA, OFF_IA, OFF_GA = 0, AW, 2 * AW, 3 * AW
OFF_QB = 4 * AW
OFF_KB = OFF_QB + BW
OFF_VB = OFF_KB + KVW
OFF_GTA = OFF_VB + KVW
OFF_GTB = OFF_GTA + D
N_MOD = 6
EPS = 1e-6
LR, B1, B2, ADAM_EPS, WD, STEP = 1e-3, 0.9, 0.999, 1e-8, 0.01, 10
NEG = -1e30

VMEM_LIMIT = 56 * 1024 * 1024
MI_CUTS = (512, 928)
MO_CUT = 336

NN = (((1,), (0,)), ((), ()))
NT = (((1,), (1,)), ((), ()))
TN = (((0,), (0,)), ((), ()))
BNN = (((2,), (1,)), ((0,), (0,)))
BNT = (((2,), (2,)), ((0,), (0,)))
BTN = (((1,), (1,)), ((0,), (0,)))


def _dot(a, b, dims=NN):
    return lax.dot_general(a.astype(BF16), b.astype(BF16), dims, preferred_element_type=F32)


def _params(sem):
    return pltpu.CompilerParams(dimension_semantics=sem, vmem_limit_bytes=VMEM_LIMIT)


def _sigmoid(x):
    return 1.0 / (1.0 + jnp.exp(-x))


def _fold8(v):
    r, n = v.shape
    return jnp.sum(v.reshape(r // 8, 8, n), axis=0)


_VMEM = pl.BlockSpec(memory_space=pltpu.VMEM)
_ANY = pl.BlockSpec(memory_space=pl.ANY)
_SEMS = lambda n: pltpu.SemaphoreType.DMA((n,))


def _mesh_pos():
    return lax.axis_index("x"), lax.axis_index("y"), lax.axis_index("c")


def _flip(pos, k):
    return tuple(1 - p if (k >> s) & 1 else p for p, s in zip(pos, (2, 1, 0)))


def _index(pos):
    return 4 * pos[0] + 2 * pos[1] + pos[2]


class _Job:
    def __init__(self, ins, out_shape, sems, start, finish, aliases=None):
        self.ins, self.out_shape, self.sems, self.start, self.finish = list(ins), list(out_shape), list(sems), start, finish
        self.aliases = dict(aliases or {})


def _both(j1, j2):
    assert not j1.aliases and not j2.aliases
    n_in, n_out, n_sem = len(j1.ins), len(j1.out_shape), len(j1.sems)
    first = lambda ins, outs, sems: (ins[:n_in], outs[:n_out], sems[:n_sem])
    second = lambda ins, outs, sems: (ins[n_in:], outs[n_out:], sems[n_sem:])

    def start(*refs):
        j1.start(*first(*refs))
        j2.start(*second(*refs))

    def finish(*refs):
        j1.finish(*first(*refs))
        j2.finish(*second(*refs))

    return _Job(j1.ins + j2.ins, j1.out_shape + j2.out_shape, j1.sems + j2.sems, start, finish)


def _pcall(body, *, grid, in_specs, out_specs, out_shape, scratch_shapes, name, semantics, args, job=None):
    if job is None:
        outs = pl.pallas_call(body, grid=grid, in_specs=in_specs, out_specs=out_specs, out_shape=out_shape,
                              scratch_shapes=scratch_shapes, name=name, compiler_params=_params(semantics))(*args)
        return list(outs), []
    n_in, n_out, n_scr = len(in_specs), len(out_specs), len(scratch_shapes)
    j_in, j_out = len(job.ins), len(job.out_shape)
    steps = tuple(grid)

    def carrier(*refs):
        o = 0
        main_in, o = refs[o:o + n_in], o + n_in
        job_in, o = refs[o:o + j_in], o + j_in
        main_out, o = refs[o:o + n_out], o + n_out
        job_out, o = refs[o:o + j_out], o + j_out
        main_scr, job_sems = refs[o:o + n_scr], refs[o + n_scr:]
        ids = [pl.program_id(a) for a in range(len(steps))]
        first = functools.reduce(lambda p, q: p & q, [i == 0 for i in ids])
        last = functools.reduce(lambda p, q: p & q, [i == s - 1 for i, s in zip(ids, steps)])

        @pl.when(first)
        def _():
            job.start(job_in, job_out, job_sems)

        body(*main_in, *main_out, *main_scr)

        @pl.when(last)
        def _():
            job.finish(job_in, job_out, job_sems)

    outs = pl.pallas_call(
        carrier, grid=grid, in_specs=list(in_specs) + [_ANY] * j_in, out_specs=list(out_specs) + [_ANY] * j_out,
        out_shape=list(out_shape) + job.out_shape, scratch_shapes=list(scratch_shapes) + job.sems, name=name,
        input_output_aliases={n_in + i: n_out + o for i, o in job.aliases.items()},
        compiler_params=_params(("arbitrary",) * len(steps)),
    )(*args, *job.ins)
    return list(outs[:n_out]), list(outs[n_out:])


def _run_job(name, job):
    j_in, j_out = len(job.ins), len(job.out_shape)

    def body(*refs):
        ins, outs, sems = refs[:j_in], refs[j_in:j_in + j_out], refs[j_in + j_out:]
        job.start(ins, outs, sems)
        job.finish(ins, outs, sems)

    return list(pl.pallas_call(body, in_specs=[_ANY] * j_in, out_specs=[_ANY] * j_out, out_shape=job.out_shape,
                               scratch_shapes=job.sems, name=name,
                               input_output_aliases=job.aliases)(*job.ins))


def _gather_job(shards, rows=None, into=None):
    n = len(shards)
    rows = rows or [(0, s.shape[0]) for s in shards]
    into = into or [None] * n
    olds, aliases = [], {}
    for a, buf in enumerate(into):
        if buf is not None:
            aliases[n + len(olds)] = a
            olds.append(buf)

    def copies(ins, outs, sems):
        send_sems, recv_sems, local_sems = sems
        x, y, c = _mesh_pos()
        me, sib = (x, y, c), (x, y, 1 - c)
        chips = [(1 - x, y), (x, 1 - y), (1 - x, 1 - y)]

        def part(a, p):
            rs, (r0, r1) = shards[a].shape[0], rows[a]
            return outs[a].at[pl.ds(_index(p) * rs + r0, r1 - r0), :]

        own = lambda a: ins[a].at[pl.ds(rows[a][0], rows[a][1] - rows[a][0]), :]

        def copy(a, k, block, to, src=None):
            return pltpu.make_async_remote_copy(
                src_ref=part(a, block) if src is None else src, dst_ref=part(a, block),
                send_sem=send_sems.at[7 * a + k], recv_sem=recv_sems.at[7 * a + k], device_id=to, device_id_type=MESH)

        mine = [pltpu.make_async_copy(own(a), part(a, me), local_sems.at[a]) for a in range(n)]
        first = []
        for a in range(n):
            first.append(copy(a, 0, me, sib, src=own(a)))
            first += [copy(a, 1 + j, me, (*chip, c), src=own(a)) for j, chip in enumerate(chips)]
        return me, sib, c, chips, copy, mine, first

    def start(ins, outs, sems):
        *_, mine, first = copies(ins, outs, sems)
        for cp in mine + first:
            cp.start()

    def finish(ins, outs, sems):
        me, sib, c, chips, copy, mine, first = copies(ins, outs, sems)
        passed = []
        for j, chip in enumerate(chips):
            for a in range(n):
                copy(a, 1 + j, (*chip, c), me).wait_recv()
                cp = copy(a, 4 + j, (*chip, c), sib)
                cp.start()
                passed.append(cp)
        for a in range(n):
            copy(a, 0, sib, me).wait_recv()
            for j, chip in enumerate(chips):
                copy(a, 4 + j, (*chip, 1 - c), me).wait_recv()
        for cp in first + passed:
            cp.wait_send()
        for cp in mine:
            cp.wait()

    return _Job(list(shards) + olds, [jax.ShapeDtypeStruct((N_DEV * s.shape[0], s.shape[1]), s.dtype) for s in shards],
                [_SEMS(7 * n), _SEMS(7 * n), _SEMS(n)], start, finish, aliases)


def _gather_relay_job(shards):
    n = len(shards)

    def tools(ins, outs, sems):
        send_sems, recv_sems, local_sems = sems
        x, y, c = _mesh_pos()
        q = 2 * x + y
        chip_at = lambda rel: (1 - x if rel & 2 else x, 1 - y if rel & 1 else y)

        def rows(a, chip, core):
            rs = shards[a].shape[0]
            return outs[a].at[pl.ds((2 * chip + core) * rs, rs), :]

        def copy(a, slot, chip, core, to, src=None):
            blk = rows(a, chip, core)
            return pltpu.make_async_remote_copy(src_ref=blk if src is None else src, dst_ref=blk,
                                                send_sem=send_sems.at[7 * a + slot], recv_sem=recv_sems.at[7 * a + slot],
                                                device_id=to, device_id_type=MESH)

        mine = [pltpu.make_async_copy(ins[a], rows(a, q, c), local_sems.at[a]) for a in range(n)]
        first = [copy(a, slot, q, c, (x, y, 1 - c) if slot == 0 else (*chip_at(slot), c), src=ins[a])
                 for a in range(n) for slot in (0, 1, 2)]
        return x, y, c, q, chip_at, copy, mine, first

    def start(ins, outs, sems):
        *_, mine, first = tools(ins, outs, sems)
        for cp in mine + first:
            cp.start()

    def finish(ins, outs, sems):
        x, y, c, q, chip_at, copy, mine, first = tools(ins, outs, sems)
        me, sib = (x, y, c), (x, y, 1 - c)

        def relay(src, dst):
            for a in range(n):
                copy(a, src, q ^ src, c, me).wait_recv()
                copy(a, 3, q ^ src, c, (*chip_at(dst), c)).start()
                copy(a, 3 + src, q ^ src, c, sib).start()
            for a in range(n):
                copy(a, dst, q ^ dst, c, me).wait_recv()
                copy(a, 3 + dst, q ^ dst, c, sib).start()

        pl.when(c == 1)(lambda: relay(1, 2))
        pl.when(c == 0)(lambda: relay(2, 1))
        for a in range(n):
            copy(a, 3, q ^ 3, c, me).wait_recv()
            copy(a, 6, q ^ 3, c, sib).start()
        for a in range(n):
            copy(a, 0, q, 1 - c, me).wait_recv()
            for rel in (1, 2, 3):
                copy(a, 3 + rel, q ^ rel, 1 - c, me).wait_recv()
        for a in range(n):
            for slot in range(3, 7):
                copy(a, slot, q, c, sib).wait_send()
        for cp in first:
            cp.wait_send()
        for cp in mine:
            cp.wait()

    return _Job(shards, [jax.ShapeDtypeStruct((N_DEV * s.shape[0], s.shape[1]), s.dtype) for s in shards],
                [_SEMS(7 * n), _SEMS(7 * n), _SEMS(n)], start, finish)


def _pair_job(grads):
    n = len(grads)

    def copies(ins, outs, sems):
        send_sems, recv_sems = sems
        x, y, c = _mesh_pos()
        out = []
        for a in range(n):
            rs = grads[a].shape[0] // N_DEV
            for q in range(4):
                blk = ins[a].at[pl.ds((2 * q + 1 - c) * rs, rs), :]
                out.append(pltpu.make_async_remote_copy(
                    src_ref=blk, dst_ref=outs[a].at[q], send_sem=send_sems.at[4 * a + q], recv_sem=recv_sems.at[4 * a + q],
                    device_id=(x, y, 1 - c), device_id_type=MESH))
        return out

    def start(ins, outs, sems):
        for cp in copies(ins, outs, sems):
            cp.start()

    def finish(ins, outs, sems):
        for cp in copies(ins, outs, sems):
            cp.wait()

    return _Job(grads, [jax.ShapeDtypeStruct((4, g.shape[0] // N_DEV, g.shape[1]), g.dtype) for g in grads],
                [_SEMS(4 * n), _SEMS(4 * n)], start, finish)


def _chip_job(sums):
    n = len(sums)

    def copies(ins, outs, sems):
        send_sems, recv_sems = sems
        x, y, c = _mesh_pos()
        out = []
        for a in range(n):
            for r in (1, 2, 3):
                px, py = (1 - x if r & 2 else x), (1 - y if r & 1 else y)
                out.append(pltpu.make_async_remote_copy(
                    src_ref=ins[a].at[2 * px + py], dst_ref=outs[a].at[r - 1], send_sem=send_sems.at[3 * a + r - 1],
                    recv_sem=recv_sems.at[3 * a + r - 1], device_id=(px, py, c), device_id_type=MESH))
        return out

    def start(ins, outs, sems):
        for cp in copies(ins, outs, sems):
            cp.start()

    def finish(ins, outs, sems):
        for cp in copies(ins, outs, sems):
            cp.wait()

    return _Job(sums, [jax.ShapeDtypeStruct((3,) + s.shape[1:], s.dtype) for s in sums],
                [_SEMS(3 * n), _SEMS(3 * n)], start, finish)


def _mm(name, form, a_list, b, M, N, K, tm, tn, tk, extras, outs, epi, job=None, acc_as_ref=False):
    nI, nJ, nK = M // tm, N // tn, K // tk
    assert nI * tm == M and nJ * tn == N and nK * tk == K
    dims = {"nn": NN, "nt": NT, "tn": TN}[form]
    b_list = b if isinstance(b, list) else [(b, {"nn": N, "nt": K, "tn": N}[form])]
    nA, nB = len(a_list), len(b_list)
    assert nA == 1 or nB == 1
    assert nB == 1 or form in ("nn", "nt")
    AXIS = {"i": 0, "j": 1, "k": 2}
    a_axis, a_tile = ("i", tm) if form == "tn" else ("k", tk)
    b_axis, b_tile = ("k", tk) if form == "nt" else ("j", tn)

    def cut(pieces, tile, total):
        starts, s = [], 0
        for _, w in pieces:
            assert w % tile == 0
            starts.append(s // tile)
            s += w
        assert s == total
        return starts, [w // tile for _, w in pieces]

    a_st, a_cn = cut(a_list, a_tile, M if form == "tn" else K)
    b_st, b_cn = cut(b_list, b_tile, K if form == "nt" else N)

    def inside(idx, st, cn):
        return (idx >= st) & (idx < st + cn)

    def a_spec(p):
        st, cn = a_st[p], a_cn[p]
        if form == "tn":
            return pl.BlockSpec((tk, tm), lambda i, j, k: (jnp.where(inside(i, st, cn), k, 0), jnp.clip(i - st, 0, cn - 1)))
        return pl.BlockSpec((tm, tk), lambda i, j, k: (i, jnp.clip(k - st, 0, cn - 1)))

    def b_spec(p):
        st, cn = b_st[p], b_cn[p]
        if form == "nt":
            return pl.BlockSpec((tn, tk), lambda i, j, k: (j, jnp.clip(k - st, 0, cn - 1)))
        if nB == 1:
            return pl.BlockSpec((tk, tn), lambda i, j, k: (k, j))
        return pl.BlockSpec((tk, tn), lambda i, j, k: (jnp.where(inside(j, st, cn), k, 0), jnp.clip(j - st, 0, cn - 1)))

    in_specs = ([a_spec(p) for p in range(nA)] + [b_spec(p) for p in range(nB)]
                + [pl.BlockSpec(bs, im) for _, bs, im in extras])
    out_shape = [jax.ShapeDtypeStruct(s_, d_) for s_, d_, _, _ in outs]
    out_specs = [pl.BlockSpec(bs, im) for _, _, bs, im in outs]
    nE, nO = len(extras), len(outs)
    single = nA == 1 and nB == 1

    def body(*refs):
        a_refs, b_refs = refs[:nA], refs[nA:nA + nB]
        ex, ou = refs[nA + nB:nA + nB + nE], refs[nA + nB + nE:nA + nB + nE + nO]
        ids = [pl.program_id(a) for a in range(3)]

        def partial_of(p, q):
            return lax.dot_general(a_refs[p][...], b_refs[q][...], dims, preferred_element_type=F32)

        if nK == 1 and single:
            epi(partial_of(0, 0), ex, ou)
            return
        acc = refs[-1]
        k = ids[2]
        for p in range(nA):
            for q in range(nB):
                def first(p=p, q=q):
                    acc[...] = partial_of(p, q)

                def later(p=p, q=q):
                    acc[...] += partial_of(p, q)

                here = None
                if nA > 1:
                    here = inside(ids[AXIS[a_axis]], a_st[p], a_cn[p])
                if nB > 1:
                    here = inside(ids[AXIS[b_axis]], b_st[q], b_cn[q])
                pl.when(k == 0 if here is None else here & (k == 0))(first)
                pl.when(k > 0 if here is None else here & (k > 0))(later)

        @pl.when(k == nK - 1)
        def _():
            epi(acc if acc_as_ref else acc[...], ex, ou)

    scratch = [] if (nK == 1 and single) else [pltpu.VMEM((tm, tn), F32)]
    res, job_res = _pcall(
        body, grid=(nI, nJ, nK), in_specs=in_specs, out_specs=out_specs, out_shape=out_shape, scratch_shapes=scratch,
        name=name, semantics=("parallel", "parallel", "arbitrary"),
        args=[a for a, _ in a_list] + [p for p, _ in b_list] + [e for e, _, _ in extras], job=job)
    return res if job is None else (res, job_res)


def _piece_tiles(pieces, tile):
    starts, s = [], 0
    for _, w in pieces:
        assert w % tile == 0
        starts.append(s // tile)
        s += w
    return starts, [w // tile for _, w in pieces], s


def _pieces_tn(name, pieces, b, tile, job=None):
    T, N = b.shape
    st, cn, M = _piece_tiles(pieces, tile)
    nP, nI = len(pieces), M // tile

    def body(*refs):
        p_refs, b_hbm, o_ref = refs[:nP], refs[nP], refs[nP + 1]
        bbuf, abuf, bsem, asem = refs[nP + 2:]
        i = pl.program_id(0)

        def fetch(step, slot):
            for p in range(nP):
                @pl.when((step >= st[p]) & (step < st[p] + cn[p]))
                def _():
                    col = pl.multiple_of((step - st[p]) * tile, tile)
                    pltpu.make_async_copy(p_refs[p].at[pl.ds(0, T), pl.ds(col, tile)], abuf.at[slot], asem.at[slot]).start()

        @pl.when(i == 0)
        def _():
            whole = pltpu.make_async_copy(b_hbm, bbuf, bsem)
            whole.start()
            fetch(0, 0)
            whole.wait()

        @pl.when(i + 1 < nI)
        def _():
            fetch(i + 1, (i + 1) % 2)

        pltpu.make_async_copy(p_refs[0].at[pl.ds(0, T), pl.ds(0, tile)], abuf.at[i % 2], asem.at[i % 2]).wait()
        o_ref[...] = lax.dot_general(abuf[i % 2], bbuf[...], TN, preferred_element_type=F32).astype(BF16)

    res, job_res = _pcall(
        body, grid=(nI,), in_specs=[_ANY] * (nP + 1), out_specs=[pl.BlockSpec((tile, N), lambda i: (i, 0))],
        out_shape=[jax.ShapeDtypeStruct((M, N), BF16)],
        scratch_shapes=[pltpu.VMEM((T, N), b.dtype), pltpu.VMEM((2, T, tile), b.dtype), pltpu.SemaphoreType.DMA, _SEMS(2)],
        name=name, semantics=("arbitrary",), args=[p for p, _ in pieces] + [b], job=job)
    return res if job is None else (res, job_res)


def _pieces_nn_rms(name, pieces, w, x, gain, sc, dres, tm, tk, job=None):
    T = x.shape[0]
    st, cn, K = _piece_tiles(pieces, tk)
    nP, nI, nK = len(pieces), T // tm, K // tk
    extras, outs, epi = _rms_mod_bwd_epilogue(x, gain, sc, dres, tm)
    part_specs = [pl.BlockSpec(bs, lambda i, k, im=im: im(i, 0, k)) for _, _, bs, im in outs[1:]]

    def body(*refs):
        p_refs = refs[:nP]
        w_ref, x_hbm, g_ref, sc_ref, r_hbm, dx_hbm = refs[nP:nP + 6]
        p_outs = refs[nP + 6:nP + 9]
        acc, abuf, xbuf, rbuf, asem, xsem, rsem, osem = refs[nP + 9:]
        i, k = pl.program_id(0), pl.program_id(1)
        g = i * nK + k
        rows_of = lambda ref, ii: ref.at[pl.ds(pl.multiple_of(ii * tm, tm), tm), :]

        def fetch(ii, kk, slot):
            for p in range(nP):
                @pl.when((kk >= st[p]) & (kk < st[p] + cn[p]))
                def _():
                    col = pl.multiple_of((kk - st[p]) * tk, tk)
                    src = p_refs[p].at[pl.ds(pl.multiple_of(ii * tm, tm), tm), pl.ds(col, tk)]
                    pltpu.make_async_copy(src, abuf.at[slot], asem.at[slot]).start()

        write_out = lambda ii: pltpu.make_async_copy(rbuf, rows_of(dx_hbm, ii), osem)

        @pl.when(g == 0)
        def _():
            fetch(0, 0, 0)

        @pl.when(g + 1 < nI * nK)
        def _():
            last_k = k == nK - 1
            fetch(jnp.where(last_k, i + 1, i), jnp.where(last_k, 0, k + 1), (g + 1) % 2)

        @pl.when(k == 0)
        def _():
            @pl.when(i > 0)
            def _():
                write_out(i - 1).wait()
            pltpu.make_async_copy(rows_of(x_hbm, i), xbuf, xsem).start()
            pltpu.make_async_copy(rows_of(r_hbm, i), rbuf, rsem).start()

        pltpu.make_async_copy(p_refs[0].at[pl.ds(0, tm), pl.ds(0, tk)], abuf.at[g % 2], asem.at[g % 2]).wait()
        def product(cols):
            return jnp.dot(abuf[g % 2], w_ref[:, cols], preferred_element_type=F32)

        col_blocks = [slice(c0, c0 + 512) for c0 in range(0, D, 512)]

        @pl.when(k == 0)
        def _():
            for cols in col_blocks:
                acc[:, cols] = product(cols)

        @pl.when(k > 0)
        def _():
            for cols in col_blocks:
                acc[:, cols] += product(cols)

        @pl.when(k == nK - 1)
        def _():
            pltpu.make_async_copy(rows_of(x_hbm, i), xbuf, xsem).wait()
            pltpu.make_async_copy(rows_of(r_hbm, i), rbuf, rsem).wait()
            epi(acc, [xbuf, g_ref, sc_ref, rbuf], [rbuf, *p_outs])
            write_out(i).start()

            @pl.when(i == nI - 1)
            def _():
                write_out(i).wait()

    vec = pl.BlockSpec((1, D), lambda i, k: (0, 0))
    res, job_res = _pcall(
        body, grid=(nI, nK),
        in_specs=[_ANY] * nP + [pl.BlockSpec((tk, D), lambda i, k: (k, 0)), _ANY, vec, vec, _ANY],
        out_specs=[_ANY] + part_specs,
        out_shape=[jax.ShapeDtypeStruct((T, D), F32)] + [jax.ShapeDtypeStruct(s, d) for s, d, _, _ in outs[1:]],
        scratch_shapes=[pltpu.VMEM((tm, D), F32), pltpu.VMEM((2, tm, tk), BF16), pltpu.VMEM((tm, D), F32),
                        pltpu.VMEM((tm, D), F32), _SEMS(2), pltpu.SemaphoreType.DMA, pltpu.SemaphoreType.DMA,
                        pltpu.SemaphoreType.DMA],
        name=name, semantics=("arbitrary", "arbitrary"), args=[p for p, _ in pieces] + [w, x, gain, sc, dres], job=job)
    return res if job is None else (res, job_res)


def _rms_mod_fwd(name, x, gain, sc, sh, tr):
    T = x.shape[0]

    def body(x_ref, g_ref, sc_ref, sh_ref, h_ref):
        xv = x_ref[...]
        rstd = lax.rsqrt(jnp.mean(xv * xv, axis=-1, keepdims=True) + EPS)
        h_ref[...] = ((xv * rstd * g_ref[...]) * (1.0 + sc_ref[...]) + sh_ref[...]).astype(BF16)

    row = pl.BlockSpec((tr, D), lambda i: (i, 0))
    vec = pl.BlockSpec((1, D), lambda i: (0, 0))
    return pl.pallas_call(
        body, grid=(T // tr,), in_specs=[row, vec, vec, vec], out_specs=row,
        out_shape=jax.ShapeDtypeStruct((T, D), BF16), name=name, compiler_params=_params(("parallel",)),
    )(x, gain, sc, sh)


def _rms_mod_bwd_epilogue(x, gain, sc, dres, tm, gate=None, mo=None):
    T = x.shape[0]
    with_gate = gate is not None
    row = ((tm, D), lambda i, j, k: (i, 0))
    vec = ((1, D), lambda i, j, k: (0, 0))
    part = ((T // tm * 8, D), F32, (8, D), lambda i, j, k: (i, 0))
    extras = [(x, *row), (gain, *vec), (sc, *vec), (dres, *row)]
    outs = [((T, D), F32, *row), part, part, part]
    if with_gate:
        extras += [(gate, *vec), (mo, *row)]
        outs += [((T, D), BF16, *row), part]

    rows = min(64, tm)

    def epi(acc, ex, ou):
        g = ex[1][...]
        sums = [jnp.zeros((8, D), F32) for _ in range(4)]
        for r0 in range(0, tm, rows):
            rs = slice(r0, r0 + rows)
            dhv, xv = acc[rs, :], ex[0][rs, :]
            rstd = lax.rsqrt(jnp.mean(xv * xv, axis=-1, keepdims=True) + EPS)
            xhat = xv * rstd
            dn = dhv * (1.0 + ex[2][...])
            dxhat = dn * g
            dx = ex[3][rs, :] + rstd * (dxhat - xhat * jnp.mean(dxhat * xhat, axis=-1, keepdims=True))
            ou[0][rs, :] = dx
            terms = [dhv, dhv * (xhat * g), dn * xhat]
            if with_gate:
                ou[4][rs, :] = (ex[4][...] * dx).astype(BF16)
                terms.append(dx * ex[5][rs, :].astype(F32))
            sums = [s + _fold8(t) for s, t in zip(sums, terms)] + sums[len(terms):]
        ou[1][...], ou[2][...], ou[3][...] = sums[:3]
        if with_gate:
            ou[5][...] = sums[3]

    return extras, outs, epi


def _rms_mod_bwd(name, dh, x, gain, sc, dres, tr, gate=None, mo=None):
    T = x.shape[0]
    extras, outs, epi = _rms_mod_bwd_epilogue(x, gain, sc, dres, tr, gate, mo)
    rows_only = lambda im: (lambda i: im(i, 0, 0))
    nE = len(extras)

    def body(dh_ref, *refs):
        epi(dh_ref, refs[:nE], refs[nE:])

    return pl.pallas_call(
        body, grid=(T // tr,),
        in_specs=[pl.BlockSpec((tr, D), lambda i: (i, 0))] + [pl.BlockSpec(bs, rows_only(im)) for _, bs, im in extras],
        out_specs=[pl.BlockSpec(bs, rows_only(im)) for _, _, bs, im in outs],
        out_shape=[jax.ShapeDtypeStruct(s, d) for s, d, _, _ in outs], name=name, compiler_params=_params(("parallel",)),
    )(dh, *[e for e, _, _ in extras])


def _split3(v):
    h = v.astype(BF16)
    r1 = v - h.astype(F32)
    m = r1.astype(BF16)
    lo = (r1 - m.astype(F32)).astype(BF16)
    return h, m, lo


def _tri_mm(tri, v, dims=NN):
    h, m, lo = _split3(v)
    t = tri.astype(BF16)
    mm = lambda p: lax.dot_general(t, p, dims, preferred_element_type=F32)
    return (mm(lo) + mm(m)) + mm(h)


def _hgrn_chunk_terms(q, fl, lb):
    sig = _sigmoid(fl)
    f = lb + (1.0 - lb) * sig
    lf = jnp.log(f)
    kk = 1.0 - f
    sq = _sigmoid(q)
    qf = q * sq
    return sig, f, lf, kk, sq, qf


def _causal(n):
    r = lax.broadcasted_iota(jnp.int32, (n, n), 0)
    c = lax.broadcasted_iota(jnp.int32, (n, n), 1)
    return r >= c


def _hgrn_fwd(proj, lb_logits, o_gain, tt, job=None):
    T = proj.shape[0]
    nT, ncl = T // tt, tt // CHUNK
    C = CHUNK

    def body(q_ref, f_ref, i_ref, g_ref, lbl_ref, og_ref, y_ref, st_ref, S):
        @pl.when(pl.program_id(1) == 0)
        def _():
            S[...] = jnp.zeros_like(S)

        lbl = lbl_ref[...]
        lb = _sigmoid(lbl[0:1, :] - lbl[1:2, :])
        og = og_ref[...]
        shp = (ncl, C, A_HD)
        q, fl, v, g = (r[...].reshape(shp) for r in (q_ref, f_ref, i_ref, g_ref))
        tri = jnp.broadcast_to(_causal(C), (ncl, C, C))
        _, _, lf, kk, _, qf = _hgrn_chunk_terms(q, fl, lb)
        b = _tri_mm(tri, lf, BNN)
        bm, bl = b[:, C // 2 - 1:C // 2, :], b[:, C - 1:C, :]
        qd, kd = qf * jnp.exp(b - bm), kk * jnp.exp(bm - b)
        A = jnp.where(tri, _dot(qd, kd, BNT), 0.0)
        d_st = _dot(v, kk * jnp.exp(bl - b), BTN)
        dec = jnp.exp(bl)
        st = S[...]
        for ci in range(ncl):
            st_ref[0, ci] = st
            st = st * dec[ci] + d_st[ci]
        S[...] = st
        o = _dot(A, v, BNN) + _dot(qf * jnp.exp(b), st_ref[0], BNT)
        r = lax.rsqrt(jnp.mean(o * o, axis=-1, keepdims=True) + EPS)
        y_ref[...] = (o * r * og * (g * _sigmoid(g))).astype(BF16).reshape(tt, A_HD)

    def col(off):
        return pl.BlockSpec((tt, A_HD), lambda h, t: (t, off // A_HD + h))

    head_vec = lambda rows: pl.BlockSpec((rows, A_HD), lambda h, t: (0, h))
    return _pcall(
        body, grid=(A_HEADS, nT),
        in_specs=[col(OFF_QA), col(OFF_FA), col(OFF_IA), col(OFF_GA), head_vec(2), head_vec(1)],
        out_specs=[pl.BlockSpec((tt, A_HD), lambda h, t: (t, h)),
                   pl.BlockSpec((1, ncl, A_HD, A_HD), lambda h, t: (h, t, 0, 0))],
        out_shape=[jax.ShapeDtypeStruct((T, AW), BF16),
                   jax.ShapeDtypeStruct((A_HEADS, T // C, A_HD, A_HD), F32)],
        scratch_shapes=[pltpu.VMEM((A_HD, A_HD), F32)], name="hgrn_fwd", semantics=("parallel", "arbitrary"),
        args=[proj, proj, proj, proj, lb_logits, o_gain], job=job)


def _hgrn_bwd(proj, st, dy, lb_logits, o_gain, tt, job=None):
    T = proj.shape[0]
    nT, ncl = T // tt, tt // CHUNK
    C = CHUNK

    def body(q_ref, f_ref, i_ref, g_ref, st_ref, dy_ref, lbl_ref, og_ref,
             dq_ref, df_ref, di_ref, dg_ref, plb_ref, pog_ref, dS):
        @pl.when(pl.program_id(1) == 0)
        def _():
            dS[...] = jnp.zeros_like(dS)

        lbl = lbl_ref[...]
        lb = _sigmoid(lbl[0:1, :] - lbl[1:2, :])
        og = og_ref[...]
        shp = (ncl, C, A_HD)
        flat = lambda t: t.reshape(tt, A_HD)
        q, fl, v, g, dout = (r[...].reshape(shp) for r in (q_ref, f_ref, i_ref, g_ref, dy_ref))
        tri = jnp.broadcast_to(_causal(C), (ncl, C, C))
        rowi = lax.broadcasted_iota(jnp.int32, shp, 1)
        st0 = st_ref[0]
        sig, f, lf, kk, sq, qf = _hgrn_chunk_terms(q, fl, lb)
        b = _tri_mm(tri, lf, BNN)
        bm, bl = b[:, C // 2 - 1:C // 2, :], b[:, C - 1:C, :]
        e_qd, e_kd, e_ke, e_b = jnp.exp(b - bm), jnp.exp(bm - b), jnp.exp(bl - b), jnp.exp(b)
        qd, kd, ke, qe = qf * e_qd, kk * e_kd, kk * e_ke, qf * e_b
        dec = jnp.exp(bl)
        A = jnp.where(tri, _dot(qd, kd, BNT), 0.0)
        o = _dot(A, v, BNN) + _dot(qe, st0, BNT)
        r = lax.rsqrt(jnp.mean(o * o, axis=-1, keepdims=True) + EPS)
        sg = _sigmoid(g)
        on = o * r * og
        dg_ref[...] = flat((dout * on * (sg * (1.0 + g * (1.0 - sg)))).astype(BF16))
        don = dout * (g * sg)
        pog_ref[...] = _fold8(flat(don * o * r))
        dyh = don * og
        do = r * (dyh - o * (r * r) * jnp.mean(dyh * o, axis=-1, keepdims=True))
        g_st = _dot(do, qe, BTN)
        run = dS[...]
        after = [None] * ncl
        for ci in reversed(range(ncl)):
            after[ci] = run
            run = g_st[ci] + run * dec[ci]
        dS[...] = run
        d_after = jnp.stack(after, axis=0)
        ddec = jnp.sum(d_after * st0, axis=1, keepdims=True)
        dqe = _dot(do, st0, BNN)
        dke = _dot(v, d_after, BNN)
        dA = jnp.where(tri, _dot(do, v, BNT), 0.0)
        dv = _dot(ke, d_after, BNT) + _dot(A, do, BTN)
        dqd = _dot(dA, kd, BNN)
        dkd = _dot(dA, qd, BTN)
        di_ref[...] = flat(dv.astype(BF16))
        dqf = dqe * e_b + dqd * e_qd
        dkk = dkd * e_kd + dke * e_ke
        t_qd, t_kd, t_ke = dqd * qd, dkd * kd, dke * ke
        db = dqe * qe + t_qd - t_kd - t_ke
        dbm = jnp.sum(t_kd - t_qd, axis=1, keepdims=True)
        dbl = jnp.sum(t_ke, axis=1, keepdims=True) + ddec * dec
        db = db + jnp.where(rowi == C // 2 - 1, dbm, 0.0) + jnp.where(rowi == C - 1, dbl, 0.0)
        dlf = _tri_mm(tri, db, BTN)
        dfv = dlf / f - dkk
        df_ref[...] = flat((dfv * (1.0 - lb) * sig * (1.0 - sig)).astype(BF16))
        plb_ref[...] = _fold8(flat(dfv * (1.0 - sig)))
        dq_ref[...] = flat((dqf * (sq * (1.0 + q * (1.0 - sq)))).astype(BF16))

    def col(off):
        return pl.BlockSpec((tt, A_HD), lambda h, t: (nT - 1 - t, off // A_HD + h))

    head_vec = lambda rows: pl.BlockSpec((rows, A_HD), lambda h, t: (0, h))
    o_spec = pl.BlockSpec((tt, A_HD), lambda h, t: (nT - 1 - t, h))
    p_spec = pl.BlockSpec((8, A_HD), lambda h, t: (t, h))
    o_shape = jax.ShapeDtypeStruct((T, AW), BF16)
    p_shape = jax.ShapeDtypeStruct((nT * 8, AW), F32)
    return _pcall(
        body, grid=(A_HEADS, nT),
        in_specs=[col(OFF_QA), col(OFF_FA), col(OFF_IA), col(OFF_GA),
                  pl.BlockSpec((1, ncl, A_HD, A_HD), lambda h, t: (h, nT - 1 - t, 0, 0)),
                  pl.BlockSpec((tt, A_HD), lambda h, t: (nT - 1 - t, h)), head_vec(2), head_vec(1)],
        out_specs=[o_spec, o_spec, o_spec, o_spec, p_spec, p_spec],
        out_shape=[o_shape, o_shape, o_shape, o_shape, p_shape, p_shape],
        scratch_shapes=[pltpu.VMEM((A_HD, A_HD), F32)], name="hgrn_bwd", semantics=("parallel", "arbitrary"),
        args=[proj, proj, proj, proj, st, dy, lb_logits, o_gain], job=job)


LANES = 128
Q_COLS = BW // LANES


def _low_half():
    return lax.broadcasted_iota(jnp.int32, (1, LANES), 1) < B_HD


def _half_sum(t, low):
    lo = jnp.sum(jnp.where(low, t, 0.0), axis=-1, keepdims=True)
    hi = jnp.sum(jnp.where(low, 0.0, t), axis=-1, keepdims=True)
    return jnp.where(low, lo, hi)


def _half_rms(t, low):
    r = lax.rsqrt(_half_sum(t * t, low) * (1.0 / B_HD) + EPS)
    return t * r, r


def _fold_halves(p, low):
    return jnp.where(low, p + pltpu.roll(p, B_HD, 1), 0.0)


def _stack_cols(x):
    return jnp.stack([x[:, c * LANES:(c + 1) * LANES] for c in range(Q_COLS)], axis=0).reshape(KV_HEADS, 2 * BLK, LANES)


def _col_of(t, c):
    return t[c // 2, (c % 2) * BLK:(c % 2 + 1) * BLK]


def _split_halves(col, s, low):
    own = jnp.where(low if s == 0 else jnp.logical_not(low), col, 0.0)
    other = pltpu.roll(own, B_HD, 1)
    return (own, other) if s == 0 else (other, own)


def _swa_keys(kp_ref, kc_ref, vp_ref, vc_ref, kg, low):
    k_lo, k_hi, v_lo, v_hi, hats = [], [], [], [], []
    for j in range(KVW // LANES):
        cs = slice(j * LANES, (j + 1) * LANES)
        k_hat, k_r = _half_rms(jnp.concatenate([kp_ref[:, cs], kc_ref[:, cs]], axis=0), low)
        vcol = jnp.concatenate([vp_ref[:, cs], vc_ref[:, cs]], axis=0)
        hats.append((k_hat, k_r))
        for s in range(2):
            for dst_lo, dst_hi, col in ((k_lo, k_hi, k_hat * kg), (v_lo, v_hi, vcol)):
                lo, hi = _split_halves(col, s, low)
                dst_lo.append(lo)
                dst_hi.append(hi)
    st = lambda parts: jnp.stack(parts, axis=0)
    return st(k_lo), st(k_hi), st(v_lo), st(v_hi), hats


def _swa_mask(first_block):
    qi = lax.broadcasted_iota(jnp.int32, (BLK, 2 * BLK), 0) + BLK
    ki = lax.broadcasted_iota(jnp.int32, (BLK, 2 * BLK), 1)
    rel = qi - ki
    m = (rel >= 0) & (rel < BLK) & (jnp.logical_not(first_block) | (ki >= BLK))
    return jnp.concatenate([m, m], axis=0)


def _sink_cols(sk_ref, hi):
    top = lax.broadcasted_iota(jnp.int32, (2 * BLK, 1), 0) < BLK
    return jnp.stack([jnp.where(top, sk_ref[0, GROUP * hk + hi], sk_ref[0, GROUP * hk + 2 + hi])
                      for hk in range(KV_HEADS)], axis=0)


def _swa_probs(qn, k_half, sink, mask):
    s = jnp.where(mask, _dot(qn, k_half, BNT) * (B_HD ** -0.5), NEG)
    m = jnp.maximum(jnp.max(s, axis=-1, keepdims=True), sink)
    p = jnp.exp(s - m)
    ps = jnp.exp(sink - m)
    inv = 1.0 / (jnp.sum(p, axis=-1, keepdims=True) + ps)
    return p * inv, ps * inv


def _swa_fwd(proj, q_gain, k_gain, sinks, job=None):
    T = proj.shape[0]
    nb = T // BLK

    def body(q_ref, kc_ref, kp_ref, vc_ref, vp_ref, qg_ref, kg_ref, sk_ref, o_ref):
        low = _low_half()
        mask = _swa_mask(pl.program_id(0) == 0)
        qn = _half_rms(_stack_cols(q_ref[...]), low)[0] * qg_ref[...]
        k_lo, k_hi, v_lo, v_hi, _ = _swa_keys(kp_ref, kc_ref, vp_ref, vc_ref, kg_ref[...], low)
        p_lo, _ = _swa_probs(qn, k_lo, _sink_cols(sk_ref, 0), mask)
        p_hi, _ = _swa_probs(qn, k_hi, _sink_cols(sk_ref, 1), mask)
        o = (_dot(p_lo, v_lo, BNN) + _dot(p_hi, v_hi, BNN)).astype(BF16)
        for c in range(Q_COLS):
            o_ref[:, c * LANES:(c + 1) * LANES] = _col_of(o, c)

    q_gain, k_gain = jnp.tile(q_gain, (1, 2)), jnp.tile(k_gain, (1, 2))
    cur = lambda w, off: pl.BlockSpec((BLK, w), lambda i: (i, off // w))
    prev = lambda w, off: pl.BlockSpec((BLK, w), lambda i: (jnp.maximum(i - 1, 0), off // w))
    small = lambda n: pl.BlockSpec((1, 2 * n), lambda i: (0, 0))
    return _pcall(
        body, grid=(nb,),
        in_specs=[cur(BW, OFF_QB), cur(KVW, OFF_KB), prev(KVW, OFF_KB), cur(KVW, OFF_VB), prev(KVW, OFF_VB),
                  small(B_HD), small(B_HD), pl.BlockSpec(memory_space=pltpu.SMEM)],
        out_specs=[pl.BlockSpec((BLK, BW), lambda i: (i, 0))],
        out_shape=[jax.ShapeDtypeStruct((T, BW), BF16)], scratch_shapes=[], name="swa_fwd", semantics=("parallel",),
        args=[proj, proj, proj, proj, proj, q_gain, k_gain, sinks], job=job)


def _swa_bwd(proj, dout, q_gain, k_gain, sinks, job=None):
    T = proj.shape[0]
    nb = T // BLK
    W = BW + 2 * KVW

    def body(q_ref, kc_ref, kp_ref, vc_ref, vp_ref, do_ref, qg_ref, kg_ref, sk_ref,
             dq_ref, dkv_ref, pqg_ref, pkg_ref, psk_ref, dkn_c, dv_c):
        i = pl.program_id(0)
        live = i < nb
        low = _low_half()
        high = jnp.logical_not(low)
        qg, kg = qg_ref[...], kg_ref[...]
        mask = _swa_mask(i == 0)
        lane = lax.broadcasted_iota(jnp.int32, (1, LANES), 1)
        scale = B_HD ** -0.5

        @pl.when(i == 0)
        def _():
            dkn_c[...] = jnp.zeros_like(dkn_c)
            dv_c[...] = jnp.zeros_like(dv_c)

        q_hat, q_r = _half_rms(_stack_cols(q_ref[...]), low)
        qn = q_hat * qg
        k_lo, k_hi, v_lo, v_hi, hats = _swa_keys(kp_ref, kc_ref, vp_ref, vc_ref, kg, low)
        do = _stack_cols(do_ref[...])
        dqn = jnp.zeros((KV_HEADS, 2 * BLK, LANES), F32)
        acc_sk = jnp.zeros((1, LANES), F32)
        dk_parts, dv_parts = [], []
        for hi, (k_h, v_h) in enumerate(((k_lo, v_lo), (k_hi, v_hi))):
            p, ps = _swa_probs(qn, k_h, _sink_cols(sk_ref, hi), mask)
            dp = _dot(do, v_h, BNT)
            delta = jnp.sum(p * dp, axis=-1, keepdims=True)
            ds = p * (dp - delta) * scale
            dqn = dqn + _dot(ds, k_h, BNN)
            dk_parts.append(_dot(ds, qn, BTN))
            dv_parts.append(_dot(p, do, BTN))
            t = ps * delta
            for hk in range(KV_HEADS):
                for rows in range(2):
                    h = GROUP * hk + 2 * rows + hi
                    acc_sk = acc_sk + jnp.where(
                        lane == h, -jnp.sum(t[hk, rows * BLK:(rows + 1) * BLK], axis=0, keepdims=True), 0.0)
        dqh = dqn * qg
        dq = (q_r * (dqh - q_hat * (_half_sum(dqh * q_hat, low) * (1.0 / B_HD)))).astype(BF16)
        for c in range(Q_COLS):
            dq_ref[:, c * LANES:(c + 1) * LANES] = _col_of(dq, c)
        acc_qg = _fold_halves(_fold8((dqn * q_hat).reshape(KV_HEADS * 2 * BLK, LANES)), low)

        def native(parts, j):
            lo_arr, hi_arr = parts
            a, b = 2 * j, 2 * j + 1
            return (jnp.where(low, lo_arr[a], 0.0) + pltpu.roll(jnp.where(high, hi_arr[a], 0.0), B_HD, 1)
                    + jnp.where(high, hi_arr[b], 0.0) + pltpu.roll(jnp.where(low, lo_arr[b], 0.0), B_HD, 1))

        acc_kg = jnp.zeros((8, LANES), F32)
        for j in range(KVW // LANES):
            cs = slice(j * LANES, (j + 1) * LANES)
            dkn = jnp.where(live, native(dk_parts, j), 0.0)
            dvc = jnp.where(live, native(dv_parts, j), 0.0)
            kp_hat, kp_r = hats[j][0][:BLK], hats[j][1][:BLK]
            dkn_prev = dkn_c[:, cs] + dkn[:BLK]
            dv_prev = dv_c[:, cs] + dvc[:BLK]
            acc_kg = acc_kg + _fold8(dkn_prev * kp_hat)
            dkh = dkn_prev * kg
            dkv_ref[:, cs] = (kp_r * (dkh - kp_hat * (_half_sum(dkh * kp_hat, low) * (1.0 / B_HD)))).astype(BF16)
            dkv_ref[:, KVW + j * LANES:KVW + (j + 1) * LANES] = dv_prev.astype(BF16)
            dkn_c[:, cs] = dkn[BLK:]
            dv_c[:, cs] = dvc[BLK:]
        keep = jnp.where(i > 0, 1.0, 0.0)
        pqg_ref[...] = jnp.where(live, acc_qg, 0.0)
        pkg_ref[...] = _fold_halves(acc_kg, low) * keep
        psk_ref[...] = jnp.broadcast_to(jnp.where(live, acc_sk, 0.0), (8, LANES)) * (
            lax.broadcasted_iota(jnp.int32, (8, LANES), 0) == 0).astype(F32)

    q_gain, k_gain = jnp.tile(q_gain, (1, 2)), jnp.tile(k_gain, (1, 2))
    last = nb - 1
    cur = lambda w, off: pl.BlockSpec((BLK, w), lambda i: (jnp.minimum(i, last), off // w))
    prev = lambda w, off: pl.BlockSpec((BLK, w), lambda i: (jnp.maximum(i - 1, 0), off // w))
    small = lambda n: pl.BlockSpec((1, 2 * n), lambda i: (0, 0))
    part = pl.BlockSpec((8, 128), lambda i: (i, 0))
    p_shape = jax.ShapeDtypeStruct(((nb + 1) * 8, 128), F32)
    return _pcall(
        body, grid=(nb + 1,),
        in_specs=[cur(BW, OFF_QB), cur(KVW, OFF_KB), prev(KVW, OFF_KB), cur(KVW, OFF_VB), prev(KVW, OFF_VB),
                  pl.BlockSpec((BLK, BW), lambda i: (jnp.minimum(i, last), 0)), small(B_HD), small(B_HD),
                  pl.BlockSpec(memory_space=pltpu.SMEM)],
        out_specs=[pl.BlockSpec((BLK, BW), lambda i: (i, 0)),
                   pl.BlockSpec((BLK, 2 * KVW), lambda i: (jnp.maximum(i - 1, 0), 0)), part, part, part],
        out_shape=[jax.ShapeDtypeStruct((T + BLK, BW), BF16), jax.ShapeDtypeStruct((T, 2 * KVW), BF16),
                   p_shape, p_shape, p_shape],
        scratch_shapes=[pltpu.VMEM((BLK, KVW), F32), pltpu.VMEM((BLK, KVW), F32)], name="swa_bwd",
        semantics=("arbitrary",), args=[proj, proj, proj, proj, proj, dout, q_gain, k_gain, sinks], job=job)


def _branch_merge(ya_pre, attn, wa_t, wb_t, proj, tm, tn, job=None):
    T = ya_pre.shape[0]

    def body(a_ref, b_ref, wa_ref, wb_ref, ga_ref, gb_ref, ya_ref, yb_ref, mg_ref):
        ya = lax.dot_general(a_ref[...], wa_ref[...], NT, preferred_element_type=F32)
        yb = lax.dot_general(b_ref[...], wb_ref[...], NT, preferred_element_type=F32)
        ya_ref[...] = ya.astype(BF16)
        yb_ref[...] = yb.astype(BF16)
        mg_ref[...] = (_sigmoid(ga_ref[...]) * ya + _sigmoid(gb_ref[...]) * yb).astype(BF16)

    o_spec = pl.BlockSpec((tm, tn), lambda i, j: (i, j))
    o_shape = jax.ShapeDtypeStruct((T, D), BF16)
    return _pcall(
        body, grid=(T // tm, D // tn),
        in_specs=[pl.BlockSpec((tm, AW), lambda i, j: (i, 0)), pl.BlockSpec((tm, BW), lambda i, j: (i, 0)),
                  pl.BlockSpec((tn, AW), lambda i, j: (j, 0)), pl.BlockSpec((tn, BW), lambda i, j: (j, 0)),
                  pl.BlockSpec((tm, tn), lambda i, j: (i, OFF_GTA // tn + j)),
                  pl.BlockSpec((tm, tn), lambda i, j: (i, OFF_GTB // tn + j))],
        out_specs=[o_spec, o_spec, o_spec], out_shape=[o_shape, o_shape, o_shape], scratch_shapes=[], name="branch_merge",
        semantics=("parallel", "parallel"), args=[ya_pre, attn, wa_t, wb_t, proj, proj], job=job)


def _ij(i, j, k):
    return (i, j)


def _local_step(x, tgt, mod, g1, g2, lbl, og, qg, kg, sk, shards, me, c_arr):
    win_s, wa_s, wb_s, wout_s, wmi_s, wmo_s = shards
    T = x.shape[0]
    tm, tr, tt = min(1024, T), min(256, T), min(512, T)
    tk_t = min(1024, T)
    tn = 512
    sh1, sc1, gt1, sh2, sc2, gt2 = (mod[:, i * D:(i + 1) * D] for i in range(N_MOD))
    nI = T // tm
    blk = (tm, tn)
    part = lambda: ((nI * 8, D), F32, (8, tn), _ij)
    vec_j = ((1, tn), lambda i, j, k: (0, j))

    h = _rms_mod_fwd("rms1_fwd", x, g1, sc1, sh1, tr)

    def epi_store(acc, ex, ou):
        ou[0][...] = acc.astype(ou[0].dtype)

    tm2 = min(2048, T)
    blk2 = (tm2, tn)

    full = lambda s: (0, s.shape[0])
    last = wmi_s.shape[0]
    (win_t,) = _run_job("gather_w_in", _gather_relay_job([win_s]))
    (proj,), (wa_t, wb_t, w_out, wmi_part) = _mm(
        "in_proj", "nt", [(h, D)], win_t, T, IN_W, D, tm2, tn, D, [], [((T, IN_W), F32, blk2, _ij)], epi_store,
        job=_gather_job([wa_s, wb_s, wout_s, wmi_s], rows=[full(wa_s), full(wb_s), full(wout_s), (0, MI_CUTS[0])]))
    (ya_pre, st), (wmi_part,) = _hgrn_fwd(
        proj, lbl, og, tt, job=_gather_job([wmi_s], rows=[MI_CUTS], into=[wmi_part]))
    (attn,), (wmi_t, wmo_part) = _swa_fwd(
        proj, qg, kg, sk, job=_gather_job([wmi_s, wmo_s], rows=[(MI_CUTS[1], last), (0, MO_CUT)], into=[wmi_part, None]))
    (ya, yb, merged), _ = _branch_merge(ya_pre, attn, wa_t, wb_t, proj, tm, tn)

    def epi_res1(acc, ex, ou):
        x_ref, gt_ref = ex
        ou[0][...] = acc.astype(BF16)
        ou[1][...] = x_ref[...] + gt_ref[...] * acc

    mo, x1 = _mm("out_proj", "nn", [(merged, D)], w_out, T, D, D, tm, tn, D, [(x, blk, _ij), (gt1, *vec_j)],
                 [((T, D), BF16, blk, _ij), ((T, D), F32, blk, _ij)], epi_res1)
    h2 = _rms_mod_fwd("rms2_fwd", x1, g2, sc2, sh2, tr)

    def epi_relu2(acc, ex, ou):
        r = jnp.maximum(acc, 0.0)
        ou[0][...] = r.astype(BF16)
        ou[1][...] = (r * r).astype(BF16)

    (r, a), (w_mo,) = _mm("mlp_in", "nt", [(h2, D)], wmi_t, T, HID, D, tm2, tn, D, [],
                          [((T, HID), BF16, blk2, _ij), ((T, HID), BF16, blk2, _ij)], epi_relu2,
                          job=_gather_job([wmo_s], rows=[(MO_CUT, last)], into=[wmo_part]))

    def epi_loss(acc, ex, ou):
        x1_ref, t_ref, gt_ref = ex
        e = x1_ref[...] + gt_ref[...] * acc - t_ref[...]
        dy = e * (1.0 / D)
        ou[0][...] = dy
        ou[1][...] = (gt_ref[...] * dy).astype(BF16)
        ou[2][...] = _fold8(e * e) * (0.5 / D)
        ou[3][...] = _fold8(dy * acc)

    wide = (tm, 1024)
    part_w = ((nI * 8, D), F32, (8, 1024), _ij)
    dy, dz, p_loss, p_gt2 = _mm(
        "mlp_out", "nn", [(a, HID)], w_mo, T, D, HID, tm, 1024, 1024,
        [(x1, wide, _ij), (tgt, wide, _ij), (gt2, (1, 1024), lambda i, j, k: (0, j))],
        [((T, D), F32, wide, _ij), ((T, D), BF16, wide, _ij), part_w, part_w], epi_loss)

    def epi_du(acc, ex, ou):
        ou[0][...] = (acc * (2.0 * ex[0][...].astype(F32))).astype(BF16)

    (du,) = _mm("mlp_out_dx", "nt", [(dz, D)], w_mo, T, HID, D, tm2, tn, D, [(r, blk2, _ij)],
                [((T, HID), BF16, blk2, _ij)], epi_du)
    gblk = (1024, 1024)
    gwide = (1024, D)
    pair_sum = lambda nm, g, r1: _pair_sum("pair_sum_" + nm, g, r1, c_arr, _sum_rows(r1.shape[1]))
    (g_mo,) = _mm("mlp_out_dw", "tn", [(a, HID)], dz, HID, D, T, 1024, D, tk_t, [], [((HID, D), BF16, gwide, _ij)], epi_store)
    (dh2,), (r1_mo,) = _mm("mlp_in_dx", "nn", [(du, HID)], wmi_t, T, D, HID, tm, 1024, 1024, [],
                           [((T, D), F32, (tm, 1024), _ij)], epi_store, job=_pair_job([g_mo]))
    dx1, p_sh2, p_sc2, p_g2, dmo, p_gt1 = _rms_mod_bwd("rms2_bwd", dh2, x1, g2, sc2, dy, tr, gate=gt1, mo=mo)
    s_mo = pair_sum("mlp_out", g_mo, r1_mo)
    tm_row = min(512, T)
    (g_mi,), (r2_mo,) = _mm("mlp_in_dw", "tn", [(du, HID)], h2, HID, D, T, 1024, D, tk_t, [],
                            [((HID, D), BF16, gwide, _ij)], epi_store, job=_chip_job([s_mo]))

    def epi_gates(acc, ex, ou):
        ya_ref, yb_ref, ga_ref, gb_ref = ex
        sa, sb = _sigmoid(ga_ref[...]), _sigmoid(gb_ref[...])
        ou[0][...] = (acc * sa).astype(BF16)
        ou[1][...] = (acc * sb).astype(BF16)
        ou[2][...] = (acc * ya_ref[...].astype(F32) * (sa * (1.0 - sa))).astype(BF16)
        ou[3][...] = (acc * yb_ref[...].astype(F32) * (sb * (1.0 - sb))).astype(BF16)

    o_bf = ((T, D), BF16, blk, _ij)
    (dya, dyb, dga, dgb), (r1_mi,) = _mm(
        "out_proj_dx", "nt", [(dmo, D)], w_out, T, D, D, tm, tn, D,
        [(ya, blk, _ij), (yb, blk, _ij), (proj, blk, lambda i, j, k: (i, OFF_GTA // tn + j)),
         (proj, blk, lambda i, j, k: (i, OFF_GTB // tn + j))], [o_bf, o_bf, o_bf, o_bf], epi_gates,
        job=_pair_job([g_mi]))
    s_mi = pair_sum("mlp_in", g_mi, r1_mi)
    (g_out,) = _mm("out_proj_dw", "tn", [(merged, D)], dmo, D, D, T, 1024, 1024, tk_t, [], [((D, D), BF16, gblk, _ij)], epi_store)
    (dya_pre,) = _mm("branch_a_dx", "nn", [(dya, D)], wa_t, T, AW, D, tm, tn, D, [], [((T, AW), F32, blk, _ij)], epi_store)
    (dattn,) = _mm("branch_b_dx", "nn", [(dyb, D)], wb_t, T, BW, D, tm, tn, D, [], [((T, BW), F32, blk, _ij)], epi_store)
    (g_a,) = _mm("branch_a_dw", "tn", [(dya, D)], ya_pre, D, AW, T, 1024, 1024, tk_t, [], [((D, AW), BF16, gblk, _ij)], epi_store)
    (g_b,) = _mm("branch_b_dw", "tn", [(dyb, D)], attn, D, BW, T, 1024, 1024, tk_t, [], [((D, BW), BF16, gblk, _ij)], epi_store)
    (dqa, dfa, dia, dgg, p_lb, p_og), (r2_mi, r1_out, r1_a, r1_b) = _hgrn_bwd(
        proj, st, dya_pre, lbl, og, tt, job=_both(_chip_job([s_mi]), _pair_job([g_out, g_a, g_b])))
    (dqb, dkv, p_qg, p_kg, p_sk), _ = _swa_bwd(proj, dattn, qg, kg, sk)
    s_out, s_a, s_b = pair_sum("out", g_out, r1_out), pair_sum("branch_a", g_a, r1_a), pair_sum("branch_b", g_b, r1_b)
    pieces = [(dqa, AW), (dfa, AW), (dia, AW), (dgg, AW), (dqb, BW), (dkv, 2 * KVW), (dga, D), (dgb, D)]
    (g_in,), (r2_out, r2_a, r2_b) = _pieces_tn("in_proj_dw", pieces, h, 512, job=_chip_job([s_out, s_a, s_b]))
    (r1_in,) = _run_job("pair_w_in", _pair_job([g_in]))
    s_in = pair_sum("in", g_in, r1_in)
    (dx, p_sh1, p_sc1, p_g1), (r2_in,) = _pieces_nn_rms(
        "in_proj_dx", pieces, win_t, x, g1, sc1, dx1, tm, 512, job=_chip_job([s_in]))

    partials = dict(sh1=p_sh1, sc1=p_sc1, gt1=p_gt1, sh2=p_sh2, sc2=p_sc2, gt2=p_gt2, g1=p_g1, g2=p_g2,
                    lb=p_lb, og=p_og, qg=p_qg, kg=p_kg, sk=p_sk, loss=p_loss)
    sums = dict(w_in=(s_in, r2_in), w_branch_a=(s_a, r2_a), w_branch_b=(s_b, r2_b), w_out=(s_out, r2_out),
                w_mlp_in=(s_mi, r2_mi), w_mlp_out=(s_mo, r2_mo))
    return dx, sums, partials


def _exchange_slots(buf, send_sems, recv_sems):
    me = _mesh_pos()
    mine = buf.at[_index(me)]
    sends = []
    for k in range(1, N_DEV):
        cp = pltpu.make_async_remote_copy(src_ref=mine, dst_ref=mine, send_sem=send_sems.at[k - 1],
                                          recv_sem=recv_sems.at[k - 1], device_id=_flip(me, k), device_id_type=MESH)
        cp.start()
        sends.append(cp)
    for k in range(1, N_DEV):
        theirs = buf.at[_index(_flip(me, k))]
        pltpu.make_async_remote_copy(src_ref=theirs, dst_ref=theirs, send_sem=send_sems.at[k - 1],
                                     recv_sem=recv_sems.at[k - 1], device_id=_flip(me, k), device_id_type=MESH).wait_recv()
    for cp in sends:
        cp.wait_send()


ADA_W = N_MOD * D // N_DEV


def _ada_mod(c, w_ada, b_shard):
    def body(c_ref, w_ref, b_ref, mod_ref, sc_ref, cbuf, mbuf, s1, r1, s2, r2):
        me = _index(_mesh_pos())
        cbuf[me] = c_ref[...]
        _exchange_slots(cbuf, s1, r1)
        row = lax.broadcasted_iota(jnp.int32, (N_DEV, D), 0)
        call = jnp.zeros((N_DEV, D), F32)
        for d in range(N_DEV):
            call = jnp.where(row == d, cbuf[d], call)
        sc = call * _sigmoid(call)
        sc_ref[...] = sc
        mbuf[me] = _dot(sc, w_ref[...]) + b_ref[...]
        _exchange_slots(mbuf, s2, r2)
        for s in range(N_DEV):
            mod_ref[:, s * ADA_W:(s + 1) * ADA_W] = mbuf[s, pl.ds(me, 1), :]

    return pl.pallas_call(
        body, in_specs=[_VMEM, _VMEM, _VMEM], out_specs=[_VMEM, _VMEM],
        out_shape=[jax.ShapeDtypeStruct((1, N_MOD * D), F32), jax.ShapeDtypeStruct((N_DEV, D), F32)],
        scratch_shapes=[pltpu.VMEM((N_DEV, 1, D), F32), pltpu.VMEM((N_DEV, N_DEV, ADA_W), F32),
                        _SEMS(N_DEV - 1), _SEMS(N_DEV - 1), _SEMS(N_DEV - 1), _SEMS(N_DEV - 1)],
        name="ada_mod", compiler_params=pltpu.CompilerParams(vmem_limit_bytes=VMEM_LIMIT),
    )(c, w_ada, b_shard)


SMALL_SEGS = (("b_ada", N_MOD * D), ("norm1_gain", D), ("norm2_gain", D), ("lb0", AW), ("lb1", AW),
              ("hgrn_o_gain", AW), ("q_norm_gain", 128), ("k_norm_gain", 128), ("sinks", 128))
SMALL_W = sum(w for _, w in SMALL_SEGS)
X_SEGS = (("sh1", D), ("sc1", D), ("gt1", D), ("sh2", D), ("sc2", D), ("gt2", D), ("g1", D), ("g2", D),
          ("lb", AW), ("og", AW), ("qg", 128), ("kg", 128), ("sk", 128), ("loss", 128))
X_W = sum(w for _, w in X_SEGS)


def _offsets(segs):
    out, o = {}, 0
    for name, w in segs:
        out[name] = (o, w)
        o += w
    return out


def _small_reduce(parts, lb_logits):
    xo, so = _offsets(X_SEGS), _offsets(SMALL_SEGS)
    names = [nm for nm, _ in X_SEGS]

    def body(*refs):
        p_refs = dict(zip(names, refs[:len(names)]))
        lbl_ref, allx, gs_ref, loss_ref, send_sems, recv_sems = refs[len(names):]
        me = _index(_mesh_pos())
        for nm, (o, w) in xo.items():
            if nm == "loss":
                allx[me, :, o:o + w] = jnp.broadcast_to(jnp.sum(p_refs[nm][...]), (1, w))
            else:
                allx[me, :, o:o + w] = jnp.sum(p_refs[nm][...], axis=0, keepdims=True)
        _exchange_slots(allx, send_sems, recv_sems)
        tot = allx[0]
        for d in range(1, N_DEV):
            tot = tot + allx[d]
        seg = lambda nm: tot[:, xo[nm][0]:xo[nm][0] + xo[nm][1]]

        def put(nm, v):
            gs_ref[:, so[nm][0]:so[nm][0] + so[nm][1]] = v

        put("b_ada", tot[:, 0:N_MOD * D])
        put("norm1_gain", seg("g1"))
        put("norm2_gain", seg("g2"))
        lbl = lbl_ref[...]
        lb = _sigmoid(lbl[0:1, :] - lbl[1:2, :])
        dl0 = seg("lb") * lb * (1.0 - lb)
        put("lb0", dl0)
        put("lb1", -dl0)
        put("hgrn_o_gain", seg("og"))
        put("q_norm_gain", seg("qg"))
        put("k_norm_gain", seg("kg"))
        put("sinks", seg("sk"))
        loss_ref[...] = seg("loss")

    return pl.pallas_call(
        body, in_specs=[_VMEM] * (len(names) + 1), out_specs=[_VMEM, _VMEM, _VMEM],
        out_shape=[jax.ShapeDtypeStruct((N_DEV, 1, X_W), F32), jax.ShapeDtypeStruct((1, SMALL_W), F32),
                   jax.ShapeDtypeStruct((1, 128), F32)],
        scratch_shapes=[_SEMS(N_DEV - 1), _SEMS(N_DEV - 1)], name="small_reduce",
        compiler_params=pltpu.CompilerParams(vmem_limit_bytes=VMEM_LIMIT),
    )(*[parts[nm] for nm in names], lb_logits)


def _adamw_math(w, g, m, v):
    m = B1 * m + (1.0 - B1) * g
    v = B2 * v + (1.0 - B2) * (g * g)
    m_hat = m / (1.0 - B1 ** STEP)
    v_hat = v / (1.0 - B2 ** STEP)
    return -LR * (m_hat / (jnp.sqrt(v_hat) + ADAM_EPS) + WD * w), m, v


def _sum_rows(rs):
    return 256 if rs % 256 == 0 else rs // 2


def _pair_sum(name, g, recv, c_arr, tr):
    _, rs, cols = recv.shape
    blk = (1, tr, cols)

    def body(c_ref, g_ref, r_ref, o_ref):
        o_ref[...] = (g_ref[...].astype(F32) + r_ref[...].astype(F32)).astype(BF16)

    grid_spec = pltpu.PrefetchScalarGridSpec(
        num_scalar_prefetch=1, grid=(4, rs // tr),
        in_specs=[pl.BlockSpec(blk, lambda q, i, c: (2 * q + c[0], i, 0)), pl.BlockSpec(blk, lambda q, i, c: (q, i, 0))],
        out_specs=pl.BlockSpec(blk, lambda q, i, c: (q, i, 0)))
    return pl.pallas_call(body, grid_spec=grid_spec, out_shape=jax.ShapeDtypeStruct((4, rs, cols), BF16), name=name,
                          compiler_params=_params(("parallel", "parallel")))(c_arr, g.reshape(N_DEV, rs, cols), recv)


def _sum_adamw(name, sums, recv, q_arr, w, m, v, transposed, tile):
    rows, cols = w.shape

    def body(q_ref, s_ref, r_ref, w_ref, m_ref, v_ref, g_ref, d_ref, nm_ref, nv_ref):
        g = ((s_ref[0].astype(F32) + r_ref[0].astype(F32)) + r_ref[1].astype(F32)) + r_ref[2].astype(F32)
        g = g.T if transposed else g
        g_ref[...] = g
        d_ref[...], nm_ref[...], nv_ref[...] = _adamw_math(w_ref[...], g, m_ref[...], v_ref[...])

    if transposed:
        s_spec = pl.BlockSpec((1, cols, tile), lambda i, q: (q[0], 0, i))
        r_spec = pl.BlockSpec((3, cols, tile), lambda i, q: (0, 0, i))
    else:
        s_spec = pl.BlockSpec((1, tile, cols), lambda i, q: (q[0], i, 0))
        r_spec = pl.BlockSpec((3, tile, cols), lambda i, q: (0, i, 0))
    spec = pl.BlockSpec((tile, cols), lambda i, q: (i, 0))
    shape = jax.ShapeDtypeStruct((rows, cols), F32)
    grid_spec = pltpu.PrefetchScalarGridSpec(num_scalar_prefetch=1, grid=(rows // tile,),
                                             in_specs=[s_spec, r_spec, spec, spec, spec], out_specs=[spec] * 4)
    return pl.pallas_call(body, grid_spec=grid_spec, out_shape=[shape] * 4, name=name,
                          compiler_params=_params(("parallel",)))(q_arr, sums, recv, w, m, v)


def _adamw(name, w, g, m, v, tr):
    rows, cols = w.shape

    def body(w_ref, g_ref, m_ref, v_ref, d_ref, nm_ref, nv_ref):
        d_ref[...], nm_ref[...], nv_ref[...] = _adamw_math(w_ref[...], g_ref[...], m_ref[...], v_ref[...])

    spec = pl.BlockSpec((tr, cols), lambda i: (i, 0))
    shape = jax.ShapeDtypeStruct((rows, cols), F32)
    return pl.pallas_call(
        body, grid=(rows // tr,), in_specs=[spec] * 4, out_specs=[spec] * 3, out_shape=[shape] * 3, name=name,
        compiler_params=_params(("parallel",)),
    )(w, g, m, v)


def _ada_update(sc_t, dmod_cols, w, m, v, tr):
    rows, cols = w.shape

    def body(s_ref, d_ref, w_ref, m_ref, v_ref, g_ref, dl_ref, nm_ref, nv_ref):
        g = jnp.dot(s_ref[...], d_ref[...], precision=lax.Precision.HIGHEST, preferred_element_type=F32)
        g_ref[...] = g
        dl_ref[...], nm_ref[...], nv_ref[...] = _adamw_math(w_ref[...], g, m_ref[...], v_ref[...])

    spec = pl.BlockSpec((tr, cols), lambda i: (i, 0))
    shape = jax.ShapeDtypeStruct((rows, cols), F32)
    return pl.pallas_call(
        body, grid=(rows // tr,),
        in_specs=[pl.BlockSpec((tr, N_DEV), lambda i: (i, 0)), pl.BlockSpec((N_DEV, cols), lambda i: (0, 0)), spec, spec, spec],
        out_specs=[spec] * 4, out_shape=[shape] * 4, name="ada_update", compiler_params=_params(("parallel",)),
    )(sc_t, dmod_cols, w, m, v)


BIG = ("w_in", "w_branch_a", "w_branch_b", "w_out", "w_mlp_in", "w_mlp_out")
COLUMN_SHARDED = ("w_in", "w_branch_a", "w_branch_b", "w_mlp_in")
WEIGHTS = ("w_ada", "b_ada", "norm1_gain", "w_in", "lb_logits", "hgrn_o_gain", "q_norm_gain", "k_norm_gain", "sinks",
           "w_branch_a", "w_branch_b", "w_out", "norm2_gain", "w_mlp_in", "w_mlp_out")


def _pack_small(p):
    lb = p["lb_logits"]
    src = dict(p, lb0=lb[0:1], lb1=lb[1:2])
    return jnp.concatenate([jnp.pad(src[nm], ((0, 0), (0, w - src[nm].shape[1]))) for nm, w in SMALL_SEGS], axis=1)


def _unpack_small(vec, shapes):
    so = _offsets(SMALL_SEGS)
    out = {}
    for nm, shp in shapes.items():
        if nm == "lb_logits":
            o = so["lb0"][0]
            out[nm] = vec[0, o:o + 2 * AW].reshape(2, AW)
        else:
            o = so[nm][0]
            out[nm] = vec[:, o:o + shp[1]]
    return out


def kernel(x, c, w_ada, b_ada, norm1_gain, w_in, lb_logits, hgrn_o_gain, q_norm_gain, k_norm_gain, sinks, w_branch_a, w_branch_b, w_out, norm2_gain, w_mlp_in, w_mlp_out, loss_target, m_w_ada, m_b_ada, m_norm1_gain, m_w_in, m_lb_logits, m_hgrn_o_gain, m_q_norm_gain, m_k_norm_gain, m_sinks, m_w_branch_a, m_w_branch_b, m_w_out, m_norm2_gain, m_w_mlp_in, m_w_mlp_out, v_w_ada, v_b_ada, v_norm1_gain, v_w_in, v_lb_logits, v_hgrn_o_gain, v_q_norm_gain, v_k_norm_gain, v_sinks, v_w_branch_a, v_w_branch_b, v_w_out, v_norm2_gain, v_w_mlp_in, v_w_mlp_out):
    w = dict(w_ada=w_ada, b_ada=b_ada, norm1_gain=norm1_gain, w_in=w_in, lb_logits=lb_logits, hgrn_o_gain=hgrn_o_gain,
             q_norm_gain=q_norm_gain, k_norm_gain=k_norm_gain, sinks=sinks, w_branch_a=w_branch_a, w_branch_b=w_branch_b,
             w_out=w_out, norm2_gain=norm2_gain, w_mlp_in=w_mlp_in, w_mlp_out=w_mlp_out)
    m = dict(w_ada=m_w_ada, b_ada=m_b_ada, norm1_gain=m_norm1_gain, w_in=m_w_in, lb_logits=m_lb_logits,
             hgrn_o_gain=m_hgrn_o_gain, q_norm_gain=m_q_norm_gain, k_norm_gain=m_k_norm_gain, sinks=m_sinks,
             w_branch_a=m_w_branch_a, w_branch_b=m_w_branch_b, w_out=m_w_out, norm2_gain=m_norm2_gain,
             w_mlp_in=m_w_mlp_in, w_mlp_out=m_w_mlp_out)
    v = dict(w_ada=v_w_ada, b_ada=v_b_ada, norm1_gain=v_norm1_gain, w_in=v_w_in, lb_logits=v_lb_logits,
             hgrn_o_gain=v_hgrn_o_gain, q_norm_gain=v_q_norm_gain, k_norm_gain=v_k_norm_gain, sinks=v_sinks,
             w_branch_a=v_w_branch_a, w_branch_b=v_w_branch_b, w_out=v_w_out, norm2_gain=v_norm2_gain,
             w_mlp_in=v_w_mlp_in, w_mlp_out=v_w_mlp_out)
    for d in (w, m, v):
        for nm in ("w_ada",) + BIG:
            d[nm] = d[nm][0]
    px, py, pc = _mesh_pos()
    me = _index((px, py, pc))
    c_arr = jnp.reshape(pc, (1,)).astype(jnp.int32)
    q_arr = jnp.reshape(2 * px + py, (1,)).astype(jnp.int32)

    shards = [(w[nm].T if nm in COLUMN_SHARDED else w[nm]).astype(BF16) for nm in BIG]
    b_shard = lax.dynamic_slice(b_ada, (0, me * ADA_W), (1, ADA_W))
    mod, sc_all = _ada_mod(c, w["w_ada"], b_shard)

    dx, sums, parts = _local_step(x[0], loss_target[0], mod, norm1_gain, norm2_gain, lb_logits, hgrn_o_gain,
                                  q_norm_gain, k_norm_gain, sinks, shards, me, c_arr)

    allx, g_small, loss = _small_reduce(parts, lb_logits)

    grad, delta, new_m, new_v = {}, {}, {}, {}
    for nm in BIG:
        s, r2 = sums[nm]
        grad[nm], delta[nm], new_m[nm], new_v[nm] = _sum_adamw(
            "adamw_" + nm, s, r2, q_arr, w[nm], m[nm], v[nm], nm in COLUMN_SHARDED, 128)

    dmod_cols = lax.dynamic_slice(allx[:, 0, :], (0, me * ADA_W), (N_DEV, ADA_W))
    grad["w_ada"], delta["w_ada"], new_m["w_ada"], new_v["w_ada"] = _ada_update(
        sc_all.T, dmod_cols, w["w_ada"], m["w_ada"], v["w_ada"], 256)

    small_names = [nm for nm in WEIGHTS if nm not in BIG and nm != "w_ada"]
    shapes = {nm: w[nm].shape for nm in small_names}
    ds, ms, vs = _adamw("adamw_small", _pack_small(w), g_small, _pack_small(m), _pack_small(v), 1)
    for dst, vec in ((grad, g_small), (delta, ds), (new_m, ms), (new_v, vs)):
        dst.update(_unpack_small(vec, shapes))

    def full(d, nm):
        return d[nm][None] if nm in BIG or nm == "w_ada" else d[nm]

    return (loss[0, 0], dx[None], *[full(grad, nm) for nm in WEIGHTS], *[full(delta, nm) for nm in WEIGHTS],
            *[full(new_m, nm) for nm in WEIGHTS], *[full(new_v, nm) for nm in WEIGHTS])
```

```python
import functools

import jax
import jax.numpy as jnp
from jax import lax
from jax.experimental import pallas as pl
from jax.experimental.pallas import tpu as pltpu

F32 = jnp.float32
BF16 = jnp.bfloat16
MESH = pl.DeviceIdType.MESH

N_DEV = 8
D = 2048
A_HEADS, A_HD, CHUNK = 8, 128, 64
AW = A_HEADS * A_HD
Q_HEADS, KV_HEADS, GROUP, B_HD, BLK = 16, 4, 4, 64, 128
BW = Q_HEADS * B_HD
KVW = KV_HEADS * B_HD
HID = 4 * D
IN_W = 4 * AW + BW + 2 * KVW + 2 * D
OFF_QA, OFF_FA, OFF_IA, OFF_GA = 0, AW, 2 * AW, 3 * AW
OFF_QB = 4 * AW
OFF_KB = OFF_QB + BW
OFF_VB = OFF_KB + KVW
OFF_GTA = OFF_VB + KVW
OFF_GTB = OFF_GTA + D
N_MOD = 6
EPS = 1e-6
LR, B1, B2, ADAM_EPS, WD, STEP = 1e-3, 0.9, 0.999, 1e-8, 0.01, 10
NEG = -1e30

VMEM_LIMIT = 56 * 1024 * 1024
MI_CUTS = (512, 928)
MO_CUT = 336

NN = (((1,), (0,)), ((), ()))
NT = (((1,), (1,)), ((), ()))
TN = (((0,), (0,)), ((), ()))
BNN = (((2,), (1,)), ((0,), (0,)))
BNT = (((2,), (2,)), ((0,), (0,)))
BTN = (((1,), (1,)), ((0,), (0,)))


def _dot(a, b, dims=NN):
    return lax.dot_general(a.astype(BF16), b.astype(BF16), dims, preferred_element_type=F32)


def _params(sem):
    return pltpu.CompilerParams(dimension_semantics=sem, vmem_limit_bytes=VMEM_LIMIT)


def _sigmoid(x):
    return 1.0 / (1.0 + jnp.exp(-x))


def _fold8(v):
    r, n = v.shape
    return jnp.sum(v.reshape(r // 8, 8, n), axis=0)


_VMEM = pl.BlockSpec(memory_space=pltpu.VMEM)
_ANY = pl.BlockSpec(memory_space=pl.ANY)
_SEMS = lambda n: pltpu.SemaphoreType.DMA((n,))


def _mesh_pos():
    return lax.axis_index("x"), lax.axis_index("y"), lax.axis_index("c")


def _flip(pos, k):
    return tuple(1 - p if (k >> s) & 1 else p for p, s in zip(pos, (2, 1, 0)))


def _index(pos):
    return 4 * pos[0] + 2 * pos[1] + pos[2]


class _Job:
    def __init__(self, ins, out_shape, sems, start, finish, aliases=None):
        self.ins, self.out_shape, self.sems, self.start, self.finish = list(ins), list(out_shape), list(sems), start, finish
        self.aliases = dict(aliases or {})


def _both(j1, j2):
    assert not j1.aliases and not j2.aliases
    n_in, n_out, n_sem = len(j1.ins), len(j1.out_shape), len(j1.sems)
    first = lambda ins, outs, sems: (ins[:n_in], outs[:n_out], sems[:n_sem])
    second = lambda ins, outs, sems: (ins[n_in:], outs[n_out:], sems[n_sem:])

    def start(*refs):
        j1.start(*first(*refs))
        j2.start(*second(*refs))

    def finish(*refs):
        j1.finish(*first(*refs))
        j2.finish(*second(*refs))

    return _Job(j1.ins + j2.ins, j1.out_shape + j2.out_shape, j1.sems + j2.sems, start, finish)


def _pcall(body, *, grid, in_specs, out_specs, out_shape, scratch_shapes, name, semantics, args, job=None):
    if job is None:
        outs = pl.pallas_call(body, grid=grid, in_specs=in_specs, out_specs=out_specs, out_shape=out_shape,
                              scratch_shapes=scratch_shapes, name=name, compiler_params=_params(semantics))(*args)
        return list(outs), []
    n_in, n_out, n_scr = len(in_specs), len(out_specs), len(scratch_shapes)
    j_in, j_out = len(job.ins), len(job.out_shape)
    steps = tuple(grid)

    def carrier(*refs):
        o = 0
        main_in, o = refs[o:o + n_in], o + n_in
        job_in, o = refs[o:o + j_in], o + j_in
        main_out, o = refs[o:o + n_out], o + n_out
        job_out, o = refs[o:o + j_out], o + j_out
        main_scr, job_sems = refs[o:o + n_scr], refs[o + n_scr:]
        ids = [pl.program_id(a) for a in range(len(steps))]
        first = functools.reduce(lambda p, q: p & q, [i == 0 for i in ids])
        last = functools.reduce(lambda p, q: p & q, [i == s - 1 for i, s in zip(ids, steps)])

        @pl.when(first)
        def _():
            job.start(job_in, job_out, job_sems)

        body(*main_in, *main_out, *main_scr)

        @pl.when(last)
        def _():
            job.finish(job_in, job_out, job_sems)

    outs = pl.pallas_call(
        carrier, grid=grid, in_specs=list(in_specs) + [_ANY] * j_in, out_specs=list(out_specs) + [_ANY] * j_out,
        out_shape=list(out_shape) + job.out_shape, scratch_shapes=list(scratch_shapes) + job.sems, name=name,
        input_output_aliases={n_in + i: n_out + o for i, o in job.aliases.items()},
        compiler_params=_params(("arbitrary",) * len(steps)),
    )(*args, *job.ins)
    return list(outs[:n_out]), list(outs[n_out:])


def _run_job(name, job):
    j_in, j_out = len(job.ins), len(job.out_shape)

    def body(*refs):
        ins, outs, sems = refs[:j_in], refs[j_in:j_in + j_out], refs[j_in + j_out:]
        job.start(ins, outs, sems)
        job.finish(ins, outs, sems)

    return list(pl.pallas_call(body, in_specs=[_ANY] * j_in, out_specs=[_ANY] * j_out, out_shape=job.out_shape,
                               scratch_shapes=job.sems, name=name,
                               input_output_aliases=job.aliases)(*job.ins))


def _gather_job(shards, rows=None, into=None):
    n = len(shards)
    rows = rows or [(0, s.shape[0]) for s in shards]
    into = into or [None] * n
    olds, aliases = [], {}
    for a, buf in enumerate(into):
        if buf is not None:
            aliases[n + len(olds)] = a
            olds.append(buf)

    def copies(ins, outs, sems):
        send_sems, recv_sems, local_sems = sems
        x, y, c = _mesh_pos()
        me, sib = (x, y, c), (x, y, 1 - c)
        chips = [(1 - x, y), (x, 1 - y), (1 - x, 1 - y)]

        def part(a, p):
            rs, (r0, r1) = shards[a].shape[0], rows[a]
            return outs[a].at[pl.ds(_index(p) * rs + r0, r1 - r0), :]

        own = lambda a: ins[a].at[pl.ds(rows[a][0], rows[a][1] - rows[a][0]), :]

        def copy(a, k, block, to, src=None):
            return pltpu.make_async_remote_copy(
                src_ref=part(a, block) if src is None else src, dst_ref=part(a, block),
                send_sem=send_sems.at[7 * a + k], recv_sem=recv_sems.at[7 * a + k], device_id=to, device_id_type=MESH)

        mine = [pltpu.make_async_copy(own(a), part(a, me), local_sems.at[a]) for a in range(n)]
        first = []
        for a in range(n):
            first.append(copy(a, 0, me, sib, src=own(a)))
            first += [copy(a, 1 + j, me, (*chip, c), src=own(a)) for j, chip in enumerate(chips)]
        return me, sib, c, chips, copy, mine, first

    def start(ins, outs, sems):
        *_, mine, first = copies(ins, outs, sems)
        for cp in mine + first:
            cp.start()

    def finish(ins, outs, sems):
        me, sib, c, chips, copy, mine, first = copies(ins, outs, sems)
        passed = []
        for j, chip in enumerate(chips):
            for a in range(n):
                copy(a, 1 + j, (*chip, c), me).wait_recv()
                cp = copy(a, 4 + j, (*chip, c), sib)
                cp.start()
                passed.append(cp)
        for a in range(n):
            copy(a, 0, sib, me).wait_recv()
            for j, chip in enumerate(chips):
                copy(a, 4 + j, (*chip, 1 - c), me).wait_recv()
        for cp in first + passed:
            cp.wait_send()
        for cp in mine:
            cp.wait()

    return _Job(list(shards) + olds, [jax.ShapeDtypeStruct((N_DEV * s.shape[0], s.shape[1]), s.dtype) for s in shards],
                [_SEMS(7 * n), _SEMS(7 * n), _SEMS(n)], start, finish, aliases)


def _gather_relay_job(shards):
    n = len(shards)

    def tools(ins, outs, sems):
        send_sems, recv_sems, local_sems = sems
        x, y, c = _mesh_pos()
        q = 2 * x + y
        chip_at = lambda rel: (1 - x if rel & 2 else x, 1 - y if rel & 1 else y)

        def rows(a, chip, core):
            rs = shards[a].shape[0]
            return outs[a].at[pl.ds((2 * chip + core) * rs, rs), :]

        def copy(a, slot, chip, core, to, src=None):
            blk = rows(a, chip, core)
            return pltpu.make_async_remote_copy(src_ref=blk if src is None else src, dst_ref=blk,
                                                send_sem=send_sems.at[7 * a + slot], recv_sem=recv_sems.at[7 * a + slot],
                                                device_id=to, device_id_type=MESH)

        mine = [pltpu.make_async_copy(ins[a], rows(a, q, c), local_sems.at[a]) for a in range(n)]
        first = [copy(a, slot, q, c, (x, y, 1 - c) if slot == 0 else (*chip_at(slot), c), src=ins[a])
                 for a in range(n) for slot in (0, 1, 2)]
        return x, y, c, q, chip_at, copy, mine, first

    def start(ins, outs, sems):
        *_, mine, first = tools(ins, outs, sems)
        for cp in mine + first:
            cp.start()

    def finish(ins, outs, sems):
        x, y, c, q, chip_at, copy, mine, first = tools(ins, outs, sems)
        me, sib = (x, y, c), (x, y, 1 - c)

        def relay(src, dst):
            for a in range(n):
                copy(a, src, q ^ src, c, me).wait_recv()
                copy(a, 3, q ^ src, c, (*chip_at(dst), c)).start()
                copy(a, 3 + src, q ^ src, c, sib).start()
            for a in range(n):
                copy(a, dst, q ^ dst, c, me).wait_recv()
                copy(a, 3 + dst, q ^ dst, c, sib).start()

        pl.when(c == 1)(lambda: relay(1, 2))
        pl.when(c == 0)(lambda: relay(2, 1))
        for a in range(n):
            copy(a, 3, q ^ 3, c, me).wait_recv()
            copy(a, 6, q ^ 3, c, sib).start()
        for a in range(n):
            copy(a, 0, q, 1 - c, me).wait_recv()
            for rel in (1, 2, 3):
                copy(a, 3 + rel, q ^ rel, 1 - c, me).wait_recv()
        for a in range(n):
            for slot in range(3, 7):
                copy(a, slot, q, c, sib).wait_send()
        for cp in first:
            cp.wait_send()
        for cp in mine:
            cp.wait()

    return _Job(shards, [jax.ShapeDtypeStruct((N_DEV * s.shape[0], s.shape[1]), s.dtype) for s in shards],
                [_SEMS(7 * n), _SEMS(7 * n), _SEMS(n)], start, finish)


def _pair_job(grads):
    n = len(grads)

    def copies(ins, outs, sems):
        send_sems, recv_sems = sems
        x, y, c = _mesh_pos()
        out = []
        for a in range(n):
            rs = grads[a].shape[0] // N_DEV
            for q in range(4):
                blk = ins[a].at[pl.ds((2 * q + 1 - c) * rs, rs), :]
                out.append(pltpu.make_async_remote_copy(
                    src_ref=blk, dst_ref=outs[a].at[q], send_sem=send_sems.at[4 * a + q], recv_sem=recv_sems.at[4 * a + q],
                    device_id=(x, y, 1 - c), device_id_type=MESH))
        return out

    def start(ins, outs, sems):
        for cp in copies(ins, outs, sems):
            cp.start()

    def finish(ins, outs, sems):
        for cp in copies(ins, outs, sems):
            cp.wait()

    return _Job(grads, [jax.ShapeDtypeStruct((4, g.shape[0] // N_DEV, g.shape[1]), g.dtype) for g in grads],
                [_SEMS(4 * n), _SEMS(4 * n)], start, finish)


def _chip_job(sums, rels=(1, 2, 3)):
    n, nr = len(sums), len(rels)

    def copies(ins, outs, sems):
        send_sems, recv_sems = sems
        x, y, c = _mesh_pos()
        out = []
        for a in range(n):
            for slot, r in enumerate(rels):
                px, py = (1 - x if r & 2 else x), (1 - y if r & 1 else y)
                out.append(pltpu.make_async_remote_copy(
                    src_ref=ins[a].at[2 * px + py], dst_ref=outs[a].at[slot], send_sem=send_sems.at[nr * a + slot],
                    recv_sem=recv_sems.at[nr * a + slot], device_id=(px, py, c), device_id_type=MESH))
        return out

    def start(ins, outs, sems):
        for cp in copies(ins, outs, sems):
            cp.start()

    def finish(ins, outs, sems):
        for cp in copies(ins, outs, sems):
            cp.wait()

    return _Job(sums, [jax.ShapeDtypeStruct((nr,) + s.shape[1:], s.dtype) for s in sums],
                [_SEMS(nr * n), _SEMS(nr * n)], start, finish)


def _mm(name, form, a_list, b, M, N, K, tm, tn, tk, extras, outs, epi, job=None, acc_as_ref=False):
    nI, nJ, nK = M // tm, N // tn, K // tk
    assert nI * tm == M and nJ * tn == N and nK * tk == K
    dims = {"nn": NN, "nt": NT, "tn": TN}[form]
    b_list = b if isinstance(b, list) else [(b, {"nn": N, "nt": K, "tn": N}[form])]
    nA, nB = len(a_list), len(b_list)
    assert nA == 1 or nB == 1
    assert nB == 1 or form in ("nn", "nt")
    AXIS = {"i": 0, "j": 1, "k": 2}
    a_axis, a_tile = ("i", tm) if form == "tn" else ("k", tk)
    b_axis, b_tile = ("k", tk) if form == "nt" else ("j", tn)

    def cut(pieces, tile, total):
        starts, s = [], 0
        for _, w in pieces:
            assert w % tile == 0
            starts.append(s // tile)
            s += w
        assert s == total
        return starts, [w // tile for _, w in pieces]

    a_st, a_cn = cut(a_list, a_tile, M if form == "tn" else K)
    b_st, b_cn = cut(b_list, b_tile, K if form == "nt" else N)

    def inside(idx, st, cn):
        return (idx >= st) & (idx < st + cn)

    def a_spec(p):
        st, cn = a_st[p], a_cn[p]
        if form == "tn":
            return pl.BlockSpec((tk, tm), lambda i, j, k: (jnp.where(inside(i, st, cn), k, 0), jnp.clip(i - st, 0, cn - 1)))
        return pl.BlockSpec((tm, tk), lambda i, j, k: (i, jnp.clip(k - st, 0, cn - 1)))

    def b_spec(p):
        st, cn = b_st[p], b_cn[p]
        if form == "nt":
            return pl.BlockSpec((tn, tk), lambda i, j, k: (j, jnp.clip(k - st, 0, cn - 1)))
        if nB == 1:
            return pl.BlockSpec((tk, tn), lambda i, j, k: (k, j))
        return pl.BlockSpec((tk, tn), lambda i, j, k: (jnp.where(inside(j, st, cn), k, 0), jnp.clip(j - st, 0, cn - 1)))

    in_specs = ([a_spec(p) for p in range(nA)] + [b_spec(p) for p in range(nB)]
                + [pl.BlockSpec(bs, im) for _, bs, im in extras])
    out_shape = [jax.ShapeDtypeStruct(s_, d_) for s_, d_, _, _ in outs]
    out_specs = [pl.BlockSpec(bs, im) for _, _, bs, im in outs]
    nE, nO = len(extras), len(outs)
    single = nA == 1 and nB == 1

    def body(*refs):
        a_refs, b_refs = refs[:nA], refs[nA:nA + nB]
        ex, ou = refs[nA + nB:nA + nB + nE], refs[nA + nB + nE:nA + nB + nE + nO]
        ids = [pl.program_id(a) for a in range(3)]

        def partial_of(p, q):
            return lax.dot_general(a_refs[p][...], b_refs[q][...], dims, preferred_element_type=F32)

        if nK == 1 and single:
            epi(partial_of(0, 0), ex, ou)
            return
        acc = refs[-1]
        k = ids[2]
        for p in range(nA):
            for q in range(nB):
                def first(p=p, q=q):
                    acc[...] = partial_of(p, q)

                def later(p=p, q=q):
                    acc[...] += partial_of(p, q)

                here = None
                if nA > 1:
                    here = inside(ids[AXIS[a_axis]], a_st[p], a_cn[p])
                if nB > 1:
                    here = inside(ids[AXIS[b_axis]], b_st[q], b_cn[q])
                pl.when(k == 0 if here is None else here & (k == 0))(first)
                pl.when(k > 0 if here is None else here & (k > 0))(later)

        @pl.when(k == nK - 1)
        def _():
            epi(acc if acc_as_ref else acc[...], ex, ou)

    scratch = [] if (nK == 1 and single) else [pltpu.VMEM((tm, tn), F32)]
    res, job_res = _pcall(
        body, grid=(nI, nJ, nK), in_specs=in_specs, out_specs=out_specs, out_shape=out_shape, scratch_shapes=scratch,
        name=name, semantics=("parallel", "parallel", "arbitrary"),
        args=[a for a, _ in a_list] + [p for p, _ in b_list] + [e for e, _, _ in extras], job=job)
    return res if job is None else (res, job_res)


def _piece_tiles(pieces, tile):
    starts, s = [], 0
    for _, w in pieces:
        assert w % tile == 0
        starts.append(s // tile)
        s += w
    return starts, [w // tile for _, w in pieces], s


def _pieces_tn(name, pieces, b, tile, job=None):
    T, N = b.shape
    st, cn, M = _piece_tiles(pieces, tile)
    nP, nI = len(pieces), M // tile

    def body(*refs):
        p_refs, b_hbm, o_ref = refs[:nP], refs[nP], refs[nP + 1]
        bbuf, abuf, bsem, asem = refs[nP + 2:]
        i = pl.program_id(0)

        def fetch(step, slot):
            for p in range(nP):
                @pl.when((step >= st[p]) & (step < st[p] + cn[p]))
                def _():
                    col = pl.multiple_of((step - st[p]) * tile, tile)
                    pltpu.make_async_copy(p_refs[p].at[pl.ds(0, T), pl.ds(col, tile)], abuf.at[slot], asem.at[slot]).start()

        @pl.when(i == 0)
        def _():
            whole = pltpu.make_async_copy(b_hbm, bbuf, bsem)
            whole.start()
            fetch(0, 0)
            whole.wait()

        @pl.when(i + 1 < nI)
        def _():
            fetch(i + 1, (i + 1) % 2)

        pltpu.make_async_copy(p_refs[0].at[pl.ds(0, T), pl.ds(0, tile)], abuf.at[i % 2], asem.at[i % 2]).wait()
        o_ref[...] = lax.dot_general(abuf[i % 2], bbuf[...], TN, preferred_element_type=F32).astype(BF16)

    res, job_res = _pcall(
        body, grid=(nI,), in_specs=[_ANY] * (nP + 1), out_specs=[pl.BlockSpec((tile, N), lambda i: (i, 0))],
        out_shape=[jax.ShapeDtypeStruct((M, N), BF16)],
        scratch_shapes=[pltpu.VMEM((T, N), b.dtype), pltpu.VMEM((2, T, tile), b.dtype), pltpu.SemaphoreType.DMA, _SEMS(2)],
        name=name, semantics=("arbitrary",), args=[p for p, _ in pieces] + [b], job=job)
    return res if job is None else (res, job_res)


def _pieces_nn_rms(name, pieces, w, x, gain, sc, dres, tm, tk, job=None):
    T = x.shape[0]
    st, cn, K = _piece_tiles(pieces, tk)
    nP, nI, nK = len(pieces), T // tm, K // tk
    extras, outs, epi = _rms_mod_bwd_epilogue(x, gain, sc, dres, tm)
    part_specs = [pl.BlockSpec(bs, lambda i, k, im=im: im(i, 0, k)) for _, _, bs, im in outs[1:]]

    def body(*refs):
        p_refs = refs[:nP]
        w_ref, x_hbm, g_ref, sc_ref, r_hbm, dx_hbm = refs[nP:nP + 6]
        p_outs = refs[nP + 6:nP + 9]
        acc, abuf, xbuf, rbuf, asem, xsem, rsem, osem = refs[nP + 9:]
        i, k = pl.program_id(0), pl.program_id(1)
        g = i * nK + k
        rows_of = lambda ref, ii: ref.at[pl.ds(pl.multiple_of(ii * tm, tm), tm), :]

        def fetch(ii, kk, slot):
            for p in range(nP):
                @pl.when((kk >= st[p]) & (kk < st[p] + cn[p]))
                def _():
                    col = pl.multiple_of((kk - st[p]) * tk, tk)
                    src = p_refs[p].at[pl.ds(pl.multiple_of(ii * tm, tm), tm), pl.ds(col, tk)]
                    pltpu.make_async_copy(src, abuf.at[slot], asem.at[slot]).start()

        write_out = lambda ii: pltpu.make_async_copy(rbuf, rows_of(dx_hbm, ii), osem)

        @pl.when(g == 0)
        def _():
            fetch(0, 0, 0)

        @pl.when(g + 1 < nI * nK)
        def _():
            last_k = k == nK - 1
            fetch(jnp.where(last_k, i + 1, i), jnp.where(last_k, 0, k + 1), (g + 1) % 2)

        @pl.when(k == 0)
        def _():
            @pl.when(i > 0)
            def _():
                write_out(i - 1).wait()
            pltpu.make_async_copy(rows_of(x_hbm, i), xbuf, xsem).start()
            pltpu.make_async_copy(rows_of(r_hbm, i), rbuf, rsem).start()

        pltpu.make_async_copy(p_refs[0].at[pl.ds(0, tm), pl.ds(0, tk)], abuf.at[g % 2], asem.at[g % 2]).wait()
        def product(cols):
            return jnp.dot(abuf[g % 2], w_ref[:, cols], preferred_element_type=F32)

        col_blocks = [slice(c0, c0 + 512) for c0 in range(0, D, 512)]

        @pl.when(k == 0)
        def _():
            for cols in col_blocks:
                acc[:, cols] = product(cols)

        @pl.when(k > 0)
        def _():
            for cols in col_blocks:
                acc[:, cols] += product(cols)

        @pl.when(k == nK - 1)
        def _():
            pltpu.make_async_copy(rows_of(x_hbm, i), xbuf, xsem).wait()
            pltpu.make_async_copy(rows_of(r_hbm, i), rbuf, rsem).wait()
            epi(acc, [xbuf, g_ref, sc_ref, rbuf], [rbuf, *p_outs])
            write_out(i).start()

            @pl.when(i == nI - 1)
            def _():
                write_out(i).wait()

    vec = pl.BlockSpec((1, D), lambda i, k: (0, 0))
    res, job_res = _pcall(
        body, grid=(nI, nK),
        in_specs=[_ANY] * nP + [pl.BlockSpec((tk, D), lambda i, k: (k, 0)), _ANY, vec, vec, _ANY],
        out_specs=[_ANY] + part_specs,
        out_shape=[jax.ShapeDtypeStruct((T, D), F32)] + [jax.ShapeDtypeStruct(s, d) for s, d, _, _ in outs[1:]],
        scratch_shapes=[pltpu.VMEM((tm, D), F32), pltpu.VMEM((2, tm, tk), BF16), pltpu.VMEM((tm, D), F32),
                        pltpu.VMEM((tm, D), F32), _SEMS(2), pltpu.SemaphoreType.DMA, pltpu.SemaphoreType.DMA,
                        pltpu.SemaphoreType.DMA],
        name=name, semantics=("arbitrary", "arbitrary"), args=[p for p, _ in pieces] + [w, x, gain, sc, dres], job=job)
    return res if job is None else (res, job_res)


def _rms_mod_fwd(name, x, gain, sc, sh, tr):
    T = x.shape[0]

    def body(x_ref, g_ref, sc_ref, sh_ref, h_ref):
        xv = x_ref[...]
        rstd = lax.rsqrt(jnp.mean(xv * xv, axis=-1, keepdims=True) + EPS)
        h_ref[...] = ((xv * rstd * g_ref[...]) * (1.0 + sc_ref[...]) + sh_ref[...]).astype(BF16)

    row = pl.BlockSpec((tr, D), lambda i: (i, 0))
    vec = pl.BlockSpec((1, D), lambda i: (0, 0))
    return pl.pallas_call(
        body, grid=(T // tr,), in_specs=[row, vec, vec, vec], out_specs=row,
        out_shape=jax.ShapeDtypeStruct((T, D), BF16), name=name, compiler_params=_params(("parallel",)),
    )(x, gain, sc, sh)


def _rms_mod_bwd_epilogue(x, gain, sc, dres, tm, gate=None, mo=None):
    T = x.shape[0]
    with_gate = gate is not None
    row = ((tm, D), lambda i, j, k: (i, 0))
    vec = ((1, D), lambda i, j, k: (0, 0))
    part = ((T // tm * 8, D), F32, (8, D), lambda i, j, k: (i, 0))
    extras = [(x, *row), (gain, *vec), (sc, *vec), (dres, *row)]
    outs = [((T, D), F32, *row), part, part, part]
    if with_gate:
        extras += [(gate, *vec), (mo, *row)]
        outs += [((T, D), BF16, *row), part]

    rows = min(64, tm)

    def epi(acc, ex, ou):
        g = ex[1][...]
        sums = [jnp.zeros((8, D), F32) for _ in range(4)]
        for r0 in range(0, tm, rows):
            rs = slice(r0, r0 + rows)
            dhv, xv = acc[rs, :], ex[0][rs, :]
            rstd = lax.rsqrt(jnp.mean(xv * xv, axis=-1, keepdims=True) + EPS)
            xhat = xv * rstd
            dn = dhv * (1.0 + ex[2][...])
            dxhat = dn * g
            dx = ex[3][rs, :] + rstd * (dxhat - xhat * jnp.mean(dxhat * xhat, axis=-1, keepdims=True))
            ou[0][rs, :] = dx
            terms = [dhv, dhv * (xhat * g), dn * xhat]
            if with_gate:
                ou[4][rs, :] = (ex[4][...] * dx).astype(BF16)
                terms.append(dx * ex[5][rs, :].astype(F32))
            sums = [s + _fold8(t) for s, t in zip(sums, terms)] + sums[len(terms):]
        ou[1][...], ou[2][...], ou[3][...] = sums[:3]
        if with_gate:
            ou[5][...] = sums[3]

    return extras, outs, epi


def _rms_mod_bwd(name, dh, x, gain, sc, dres, tr, gate=None, mo=None):
    T = x.shape[0]
    extras, outs, epi = _rms_mod_bwd_epilogue(x, gain, sc, dres, tr, gate, mo)
    rows_only = lambda im: (lambda i: im(i, 0, 0))
    nE = len(extras)

    def body(dh_ref, *refs):
        epi(dh_ref, refs[:nE], refs[nE:])

    return pl.pallas_call(
        body, grid=(T // tr,),
        in_specs=[pl.BlockSpec((tr, D), lambda i: (i, 0))] + [pl.BlockSpec(bs, rows_only(im)) for _, bs, im in extras],
        out_specs=[pl.BlockSpec(bs, rows_only(im)) for _, _, bs, im in outs],
        out_shape=[jax.ShapeDtypeStruct(s, d) for s, d, _, _ in outs], name=name, compiler_params=_params(("parallel",)),
    )(dh, *[e for e, _, _ in extras])


def _split3(v):
    h = v.astype(BF16)
    r1 = v - h.astype(F32)
    m = r1.astype(BF16)
    lo = (r1 - m.astype(F32)).astype(BF16)
    return h, m, lo


def _tri_mm(tri, v, dims=NN):
    h, m, lo = _split3(v)
    t = tri.astype(BF16)
    mm = lambda p: lax.dot_general(t, p, dims, preferred_element_type=F32)
    return (mm(lo) + mm(m)) + mm(h)


def _hgrn_chunk_terms(q, fl, lb):
    sig = _sigmoid(fl)
    f = lb + (1.0 - lb) * sig
    lf = jnp.log(f)
    kk = 1.0 - f
    sq = _sigmoid(q)
    qf = q * sq
    return sig, f, lf, kk, sq, qf


def _causal(n):
    r = lax.broadcasted_iota(jnp.int32, (n, n), 0)
    c = lax.broadcasted_iota(jnp.int32, (n, n), 1)
    return r >= c


def _hgrn_fwd(proj, lb_logits, o_gain, tt, job=None):
    T = proj.shape[0]
    nT, ncl = T // tt, tt // CHUNK
    C = CHUNK

    def body(q_ref, f_ref, i_ref, g_ref, lbl_ref, og_ref, y_ref, st_ref, S):
        @pl.when(pl.program_id(1) == 0)
        def _():
            S[...] = jnp.zeros_like(S)

        lbl = lbl_ref[...]
        lb = _sigmoid(lbl[0:1, :] - lbl[1:2, :])
        og = og_ref[...]
        shp = (ncl, C, A_HD)
        q, fl, v, g = (r[...].reshape(shp) for r in (q_ref, f_ref, i_ref, g_ref))
        tri = jnp.broadcast_to(_causal(C), (ncl, C, C))
        _, _, lf, kk, _, qf = _hgrn_chunk_terms(q, fl, lb)
        b = _tri_mm(tri, lf, BNN)
        bm, bl = b[:, C // 2 - 1:C // 2, :], b[:, C - 1:C, :]
        qd, kd = qf * jnp.exp(b - bm), kk * jnp.exp(bm - b)
        A = jnp.where(tri, _dot(qd, kd, BNT), 0.0)
        d_st = _dot(v, kk * jnp.exp(bl - b), BTN)
        dec = jnp.exp(bl)
        st = S[...]
        for ci in range(ncl):
            st_ref[0, ci] = st
            st = st * dec[ci] + d_st[ci]
        S[...] = st
        o = _dot(A, v, BNN) + _dot(qf * jnp.exp(b), st_ref[0], BNT)
        r = lax.rsqrt(jnp.mean(o * o, axis=-1, keepdims=True) + EPS)
        y_ref[...] = (o * r * og * (g * _sigmoid(g))).astype(BF16).reshape(tt, A_HD)

    def col(off):
        return pl.BlockSpec((tt, A_HD), lambda h, t: (t, off // A_HD + h))

    head_vec = lambda rows: pl.BlockSpec((rows, A_HD), lambda h, t: (0, h))
    return _pcall(
        body, grid=(A_HEADS, nT),
        in_specs=[col(OFF_QA), col(OFF_FA), col(OFF_IA), col(OFF_GA), head_vec(2), head_vec(1)],
        out_specs=[pl.BlockSpec((tt, A_HD), lambda h, t: (t, h)),
                   pl.BlockSpec((1, ncl, A_HD, A_HD), lambda h, t: (h, t, 0, 0))],
        out_shape=[jax.ShapeDtypeStruct((T, AW), BF16),
                   jax.ShapeDtypeStruct((A_HEADS, T // C, A_HD, A_HD), F32)],
        scratch_shapes=[pltpu.VMEM((A_HD, A_HD), F32)], name="hgrn_fwd", semantics=("parallel", "arbitrary"),
        args=[proj, proj, proj, proj, lb_logits, o_gain], job=job)


def _hgrn_bwd(proj, st, dy, lb_logits, o_gain, tt, job=None):
    T = proj.shape[0]
    nT, ncl = T // tt, tt // CHUNK
    C = CHUNK

    def body(q_ref, f_ref, i_ref, g_ref, st_ref, dy_ref, lbl_ref, og_ref,
             dq_ref, df_ref, di_ref, dg_ref, plb_ref, pog_ref, dS):
        @pl.when(pl.program_id(1) == 0)
        def _():
            dS[...] = jnp.zeros_like(dS)

        lbl = lbl_ref[...]
        lb = _sigmoid(lbl[0:1, :] - lbl[1:2, :])
        og = og_ref[...]
        shp = (ncl, C, A_HD)
        flat = lambda t: t.reshape(tt, A_HD)
        q, fl, v, g, dout = (r[...].reshape(shp) for r in (q_ref, f_ref, i_ref, g_ref, dy_ref))
        tri = jnp.broadcast_to(_causal(C), (ncl, C, C))
        rowi = lax.broadcasted_iota(jnp.int32, shp, 1)
        st0 = st_ref[0]
        sig, f, lf, kk, sq, qf = _hgrn_chunk_terms(q, fl, lb)
        b = _tri_mm(tri, lf, BNN)
        bm, bl = b[:, C // 2 - 1:C // 2, :], b[:, C - 1:C, :]
        e_qd, e_kd, e_ke, e_b = jnp.exp(b - bm), jnp.exp(bm - b), jnp.exp(bl - b), jnp.exp(b)
        qd, kd, ke, qe = qf * e_qd, kk * e_kd, kk * e_ke, qf * e_b
        dec = jnp.exp(bl)
        A = jnp.where(tri, _dot(qd, kd, BNT), 0.0)
        o = _dot(A, v, BNN) + _dot(qe, st0, BNT)
        r = lax.rsqrt(jnp.mean(o * o, axis=-1, keepdims=True) + EPS)
        sg = _sigmoid(g)
        on = o * r * og
        dg_ref[...] = flat((dout * on * (sg * (1.0 + g * (1.0 - sg)))).astype(BF16))
        don = dout * (g * sg)
        pog_ref[...] = _fold8(flat(don * o * r))
        dyh = don * og
        do = r * (dyh - o * (r * r) * jnp.mean(dyh * o, axis=-1, keepdims=True))
        g_st = _dot(do, qe, BTN)
        run = dS[...]
        after = [None] * ncl
        for ci in reversed(range(ncl)):
            after[ci] = run
            run = g_st[ci] + run * dec[ci]
        dS[...] = run
        d_after = jnp.stack(after, axis=0)
        ddec = jnp.sum(d_after * st0, axis=1, keepdims=True)
        dqe = _dot(do, st0, BNN)
        dke = _dot(v, d_after, BNN)
        dA = jnp.where(tri, _dot(do, v, BNT), 0.0)
        dv = _dot(ke, d_after, BNT) + _dot(A, do, BTN)
        dqd = _dot(dA, kd, BNN)
        dkd = _dot(dA, qd, BTN)
        di_ref[...] = flat(dv.astype(BF16))
        dqf = dqe * e_b + dqd * e_qd
        dkk = dkd * e_kd + dke * e_ke
        t_qd, t_kd, t_ke = dqd * qd, dkd * kd, dke * ke
        db = dqe * qe + t_qd - t_kd - t_ke
        dbm = jnp.sum(t_kd - t_qd, axis=1, keepdims=True)
        dbl = jnp.sum(t_ke, axis=1, keepdims=True) + ddec * dec
        db = db + jnp.where(rowi == C // 2 - 1, dbm, 0.0) + jnp.where(rowi == C - 1, dbl, 0.0)
        dlf = _tri_mm(tri, db, BTN)
        dfv = dlf / f - dkk
        df_ref[...] = flat((dfv * (1.0 - lb) * sig * (1.0 - sig)).astype(BF16))
        plb_ref[...] = _fold8(flat(dfv * (1.0 - sig)))
        dq_ref[...] = flat((dqf * (sq * (1.0 + q * (1.0 - sq)))).astype(BF16))

    def col(off):
        return pl.BlockSpec((tt, A_HD), lambda h, t: (nT - 1 - t, off // A_HD + h))

    head_vec = lambda rows: pl.BlockSpec((rows, A_HD), lambda h, t: (0, h))
    o_spec = pl.BlockSpec((tt, A_HD), lambda h, t: (nT - 1 - t, h))
    p_spec = pl.BlockSpec((8, A_HD), lambda h, t: (t, h))
    o_shape = jax.ShapeDtypeStruct((T, AW), BF16)
    p_shape = jax.ShapeDtypeStruct((nT * 8, AW), F32)
    return _pcall(
        body, grid=(A_HEADS, nT),
        in_specs=[col(OFF_QA), col(OFF_FA), col(OFF_IA), col(OFF_GA),
                  pl.BlockSpec((1, ncl, A_HD, A_HD), lambda h, t: (h, nT - 1 - t, 0, 0)),
                  pl.BlockSpec((tt, A_HD), lambda h, t: (nT - 1 - t, h)), head_vec(2), head_vec(1)],
        out_specs=[o_spec, o_spec, o_spec, o_spec, p_spec, p_spec],
        out_shape=[o_shape, o_shape, o_shape, o_shape, p_shape, p_shape],
        scratch_shapes=[pltpu.VMEM((A_HD, A_HD), F32)], name="hgrn_bwd", semantics=("parallel", "arbitrary"),
        args=[proj, proj, proj, proj, st, dy, lb_logits, o_gain], job=job)


LANES = 128
Q_COLS = BW // LANES


def _low_half():
    return lax.broadcasted_iota(jnp.int32, (1, LANES), 1) < B_HD


def _half_sum(t, low):
    lo = jnp.sum(jnp.where(low, t, 0.0), axis=-1, keepdims=True)
    hi = jnp.sum(jnp.where(low, 0.0, t), axis=-1, keepdims=True)
    return jnp.where(low, lo, hi)


def _half_rms(t, low):
    r = lax.rsqrt(_half_sum(t * t, low) * (1.0 / B_HD) + EPS)
    return t * r, r


def _fold_halves(p, low):
    return jnp.where(low, p + pltpu.roll(p, B_HD, 1), 0.0)


def _stack_cols(x):
    return jnp.stack([x[:, c * LANES:(c + 1) * LANES] for c in range(Q_COLS)], axis=0).reshape(KV_HEADS, 2 * BLK, LANES)


def _col_of(t, c):
    return t[c // 2, (c % 2) * BLK:(c % 2 + 1) * BLK]


def _split_halves(col, s, low):
    own = jnp.where(low if s == 0 else jnp.logical_not(low), col, 0.0)
    other = pltpu.roll(own, B_HD, 1)
    return (own, other) if s == 0 else (other, own)


def _swa_keys(kp_ref, kc_ref, vp_ref, vc_ref, kg, low):
    k_lo, k_hi, v_lo, v_hi, hats = [], [], [], [], []
    for j in range(KVW // LANES):
        cs = slice(j * LANES, (j + 1) * LANES)
        k_hat, k_r = _half_rms(jnp.concatenate([kp_ref[:, cs], kc_ref[:, cs]], axis=0), low)
        vcol = jnp.concatenate([vp_ref[:, cs], vc_ref[:, cs]], axis=0)
        hats.append((k_hat, k_r))
        for s in range(2):
            for dst_lo, dst_hi, col in ((k_lo, k_hi, k_hat * kg), (v_lo, v_hi, vcol)):
                lo, hi = _split_halves(col, s, low)
                dst_lo.append(lo)
                dst_hi.append(hi)
    st = lambda parts: jnp.stack(parts, axis=0)
    return st(k_lo), st(k_hi), st(v_lo), st(v_hi), hats


def _swa_mask(first_block):
    qi = lax.broadcasted_iota(jnp.int32, (BLK, 2 * BLK), 0) + BLK
    ki = lax.broadcasted_iota(jnp.int32, (BLK, 2 * BLK), 1)
    rel = qi - ki
    m = (rel >= 0) & (rel < BLK) & (jnp.logical_not(first_block) | (ki >= BLK))
    return jnp.concatenate([m, m], axis=0)


def _sink_cols(sk_ref, hi):
    top = lax.broadcasted_iota(jnp.int32, (2 * BLK, 1), 0) < BLK
    return jnp.stack([jnp.where(top, sk_ref[0, GROUP * hk + hi], sk_ref[0, GROUP * hk + 2 + hi])
                      for hk in range(KV_HEADS)], axis=0)


def _swa_probs(qn, k_half, sink, mask):
    s = jnp.where(mask, _dot(qn, k_half, BNT) * (B_HD ** -0.5), NEG)
    m = jnp.maximum(jnp.max(s, axis=-1, keepdims=True), sink)
    p = jnp.exp(s - m)
    ps = jnp.exp(sink - m)
    inv = 1.0 / (jnp.sum(p, axis=-1, keepdims=True) + ps)
    return p * inv, ps * inv


def _swa_fwd(proj, q_gain, k_gain, sinks, job=None):
    T = proj.shape[0]
    nb = T // BLK

    def body(q_ref, kc_ref, kp_ref, vc_ref, vp_ref, qg_ref, kg_ref, sk_ref, o_ref):
        low = _low_half()
        mask = _swa_mask(pl.program_id(0) == 0)
        qn = _half_rms(_stack_cols(q_ref[...]), low)[0] * qg_ref[...]
        k_lo, k_hi, v_lo, v_hi, _ = _swa_keys(kp_ref, kc_ref, vp_ref, vc_ref, kg_ref[...], low)
        p_lo, _ = _swa_probs(qn, k_lo, _sink_cols(sk_ref, 0), mask)
        p_hi, _ = _swa_probs(qn, k_hi, _sink_cols(sk_ref, 1), mask)
        o = (_dot(p_lo, v_lo, BNN) + _dot(p_hi, v_hi, BNN)).astype(BF16)
        for c in range(Q_COLS):
            o_ref[:, c * LANES:(c + 1) * LANES] = _col_of(o, c)

    q_gain, k_gain = jnp.tile(q_gain, (1, 2)), jnp.tile(k_gain, (1, 2))
    cur = lambda w, off: pl.BlockSpec((BLK, w), lambda i: (i, off // w))
    prev = lambda w, off: pl.BlockSpec((BLK, w), lambda i: (jnp.maximum(i - 1, 0), off // w))
    small = lambda n: pl.BlockSpec((1, 2 * n), lambda i: (0, 0))
    return _pcall(
        body, grid=(nb,),
        in_specs=[cur(BW, OFF_QB), cur(KVW, OFF_KB), prev(KVW, OFF_KB), cur(KVW, OFF_VB), prev(KVW, OFF_VB),
                  small(B_HD), small(B_HD), pl.BlockSpec(memory_space=pltpu.SMEM)],
        out_specs=[pl.BlockSpec((BLK, BW), lambda i: (i, 0))],
        out_shape=[jax.ShapeDtypeStruct((T, BW), BF16)], scratch_shapes=[], name="swa_fwd", semantics=("parallel",),
        args=[proj, proj, proj, proj, proj, q_gain, k_gain, sinks], job=job)


def _swa_bwd(proj, dout, q_gain, k_gain, sinks, job=None):
    T = proj.shape[0]
    nb = T // BLK
    W = BW + 2 * KVW

    def body(q_ref, kc_ref, kp_ref, vc_ref, vp_ref, do_ref, qg_ref, kg_ref, sk_ref,
             dq_ref, dkv_ref, pqg_ref, pkg_ref, psk_ref, dkn_c, dv_c):
        i = pl.program_id(0)
        live = i < nb
        low = _low_half()
        high = jnp.logical_not(low)
        qg, kg = qg_ref[...], kg_ref[...]
        mask = _swa_mask(i == 0)
        lane = lax.broadcasted_iota(jnp.int32, (1, LANES), 1)
        scale = B_HD ** -0.5

        @pl.when(i == 0)
        def _():
            dkn_c[...] = jnp.zeros_like(dkn_c)
            dv_c[...] = jnp.zeros_like(dv_c)

        q_hat, q_r = _half_rms(_stack_cols(q_ref[...]), low)
        qn = q_hat * qg
        k_lo, k_hi, v_lo, v_hi, hats = _swa_keys(kp_ref, kc_ref, vp_ref, vc_ref, kg, low)
        do = _stack_cols(do_ref[...])
        dqn = jnp.zeros((KV_HEADS, 2 * BLK, LANES), F32)
        acc_sk = jnp.zeros((1, LANES), F32)
        dk_parts, dv_parts = [], []
        for hi, (k_h, v_h) in enumerate(((k_lo, v_lo), (k_hi, v_hi))):
            p, ps = _swa_probs(qn, k_h, _sink_cols(sk_ref, hi), mask)
            dp = _dot(do, v_h, BNT)
            delta = jnp.sum(p * dp, axis=-1, keepdims=True)
            ds = p * (dp - delta) * scale
            dqn = dqn + _dot(ds, k_h, BNN)
            dk_parts.append(_dot(ds, qn, BTN))
            dv_parts.append(_dot(p, do, BTN))
            t = ps * delta
            for hk in range(KV_HEADS):
                for rows in range(2):
                    h = GROUP * hk + 2 * rows + hi
                    acc_sk = acc_sk + jnp.where(
                        lane == h, -jnp.sum(t[hk, rows * BLK:(rows + 1) * BLK], axis=0, keepdims=True), 0.0)
        dqh = dqn * qg
        dq = (q_r * (dqh - q_hat * (_half_sum(dqh * q_hat, low) * (1.0 / B_HD)))).astype(BF16)
        for c in range(Q_COLS):
            dq_ref[:, c * LANES:(c + 1) * LANES] = _col_of(dq, c)
        acc_qg = _fold_halves(_fold8((dqn * q_hat).reshape(KV_HEADS * 2 * BLK, LANES)), low)

        def native(parts, j):
            lo_arr, hi_arr = parts
            a, b = 2 * j, 2 * j + 1
            return (jnp.where(low, lo_arr[a], 0.0) + pltpu.roll(jnp.where(high, hi_arr[a], 0.0), B_HD, 1)
                    + jnp.where(high, hi_arr[b], 0.0) + pltpu.roll(jnp.where(low, lo_arr[b], 0.0), B_HD, 1))

        acc_kg = jnp.zeros((8, LANES), F32)
        for j in range(KVW // LANES):
            cs = slice(j * LANES, (j + 1) * LANES)
            dkn = jnp.where(live, native(dk_parts, j), 0.0)
            dvc = jnp.where(live, native(dv_parts, j), 0.0)
            kp_hat, kp_r = hats[j][0][:BLK], hats[j][1][:BLK]
            dkn_prev = dkn_c[:, cs] + dkn[:BLK]
            dv_prev = dv_c[:, cs] + dvc[:BLK]
            acc_kg = acc_kg + _fold8(dkn_prev * kp_hat)
            dkh = dkn_prev * kg
            dkv_ref[:, cs] = (kp_r * (dkh - kp_hat * (_half_sum(dkh * kp_hat, low) * (1.0 / B_HD)))).astype(BF16)
            dkv_ref[:, KVW + j * LANES:KVW + (j + 1) * LANES] = dv_prev.astype(BF16)
            dkn_c[:, cs] = dkn[BLK:]
            dv_c[:, cs] = dvc[BLK:]
        keep = jnp.where(i > 0, 1.0, 0.0)
        pqg_ref[...] = jnp.where(live, acc_qg, 0.0)
        pkg_ref[...] = _fold_halves(acc_kg, low) * keep
        psk_ref[...] = jnp.broadcast_to(jnp.where(live, acc_sk, 0.0), (8, LANES)) * (
            lax.broadcasted_iota(jnp.int32, (8, LANES), 0) == 0).astype(F32)

    q_gain, k_gain = jnp.tile(q_gain, (1, 2)), jnp.tile(k_gain, (1, 2))
    last = nb - 1
    cur = lambda w, off: pl.BlockSpec((BLK, w), lambda i: (jnp.minimum(i, last), off // w))
    prev = lambda w, off: pl.BlockSpec((BLK, w), lambda i: (jnp.maximum(i - 1, 0), off // w))
    small = lambda n: pl.BlockSpec((1, 2 * n), lambda i: (0, 0))
    part = pl.BlockSpec((8, 128), lambda i: (i, 0))
    p_shape = jax.ShapeDtypeStruct(((nb + 1) * 8, 128), F32)
    return _pcall(
        body, grid=(nb + 1,),
        in_specs=[cur(BW, OFF_QB), cur(KVW, OFF_KB), prev(KVW, OFF_KB), cur(KVW, OFF_VB), prev(KVW, OFF_VB),
                  pl.BlockSpec((BLK, BW), lambda i: (jnp.minimum(i, last), 0)), small(B_HD), small(B_HD),
                  pl.BlockSpec(memory_space=pltpu.SMEM)],
        out_specs=[pl.BlockSpec((BLK, BW), lambda i: (i, 0)),
                   pl.BlockSpec((BLK, 2 * KVW), lambda i: (jnp.maximum(i - 1, 0), 0)), part, part, part],
        out_shape=[jax.ShapeDtypeStruct((T + BLK, BW), BF16), jax.ShapeDtypeStruct((T, 2 * KVW), BF16),
                   p_shape, p_shape, p_shape],
        scratch_shapes=[pltpu.VMEM((BLK, KVW), F32), pltpu.VMEM((BLK, KVW), F32)], name="swa_bwd",
        semantics=("arbitrary",), args=[proj, proj, proj, proj, proj, dout, q_gain, k_gain, sinks], job=job)


def _branch_merge(ya_pre, attn, wa_t, wb_t, proj, tm, tn, job=None):
    T = ya_pre.shape[0]

    def body(a_ref, b_ref, wa_ref, wb_ref, ga_ref, gb_ref, ya_ref, yb_ref, mg_ref):
        ya = lax.dot_general(a_ref[...], wa_ref[...], NT, preferred_element_type=F32)
        yb = lax.dot_general(b_ref[...], wb_ref[...], NT, preferred_element_type=F32)
        ya_ref[...] = ya.astype(BF16)
        yb_ref[...] = yb.astype(BF16)
        mg_ref[...] = (_sigmoid(ga_ref[...]) * ya + _sigmoid(gb_ref[...]) * yb).astype(BF16)

    o_spec = pl.BlockSpec((tm, tn), lambda i, j: (i, j))
    o_shape = jax.ShapeDtypeStruct((T, D), BF16)
    return _pcall(
        body, grid=(T // tm, D // tn),
        in_specs=[pl.BlockSpec((tm, AW), lambda i, j: (i, 0)), pl.BlockSpec((tm, BW), lambda i, j: (i, 0)),
                  pl.BlockSpec((tn, AW), lambda i, j: (j, 0)), pl.BlockSpec((tn, BW), lambda i, j: (j, 0)),
                  pl.BlockSpec((tm, tn), lambda i, j: (i, OFF_GTA // tn + j)),
                  pl.BlockSpec((tm, tn), lambda i, j: (i, OFF_GTB // tn + j))],
        out_specs=[o_spec, o_spec, o_spec], out_shape=[o_shape, o_shape, o_shape], scratch_shapes=[], name="branch_merge",
        semantics=("parallel", "parallel"), args=[ya_pre, attn, wa_t, wb_t, proj, proj], job=job)


def _ij(i, j, k):
    return (i, j)


def _local_step(x, tgt, mod, g1, g2, lbl, og, qg, kg, sk, shards, me, c_arr):
    win_s, wa_s, wb_s, wout_s, wmi_s, wmo_s = shards
    T = x.shape[0]
    tm, tr, tt = min(1024, T), min(256, T), min(512, T)
    tk_t = min(1024, T)
    tn = 512
    sh1, sc1, gt1, sh2, sc2, gt2 = (mod[:, i * D:(i + 1) * D] for i in range(N_MOD))
    nI = T // tm
    blk = (tm, tn)
    part = lambda: ((nI * 8, D), F32, (8, tn), _ij)
    vec_j = ((1, tn), lambda i, j, k: (0, j))

    h = _rms_mod_fwd("rms1_fwd", x, g1, sc1, sh1, tr)

    def epi_store(acc, ex, ou):
        ou[0][...] = acc.astype(ou[0].dtype)

    tm2 = min(2048, T)
    blk2 = (tm2, tn)

    full = lambda s: (0, s.shape[0])
    last = wmi_s.shape[0]
    (win_t,) = _run_job("gather_w_in", _gather_relay_job([win_s]))
    (proj,), (wa_t, wb_t, w_out, wmi_part) = _mm(
        "in_proj", "nt", [(h, D)], win_t, T, IN_W, D, tm2, tn, D, [], [((T, IN_W), F32, blk2, _ij)], epi_store,
        job=_gather_job([wa_s, wb_s, wout_s, wmi_s], rows=[full(wa_s), full(wb_s), full(wout_s), (0, MI_CUTS[0])]))
    (ya_pre, st), (wmi_part,) = _hgrn_fwd(
        proj, lbl, og, tt, job=_gather_job([wmi_s], rows=[MI_CUTS], into=[wmi_part]))
    (attn,), (wmi_t, wmo_part) = _swa_fwd(
        proj, qg, kg, sk, job=_gather_job([wmi_s, wmo_s], rows=[(MI_CUTS[1], last), (0, MO_CUT)], into=[wmi_part, None]))
    (ya, yb, merged), _ = _branch_merge(ya_pre, attn, wa_t, wb_t, proj, tm, tn)

    def epi_res1(acc, ex, ou):
        x_ref, gt_ref = ex
        ou[0][...] = acc.astype(BF16)
        ou[1][...] = x_ref[...] + gt_ref[...] * acc

    mo, x1 = _mm("out_proj", "nn", [(merged, D)], w_out, T, D, D, tm, tn, D, [(x, blk, _ij), (gt1, *vec_j)],
                 [((T, D), BF16, blk, _ij), ((T, D), F32, blk, _ij)], epi_res1)
    h2 = _rms_mod_fwd("rms2_fwd", x1, g2, sc2, sh2, tr)

    def epi_relu2(acc, ex, ou):
        r = jnp.maximum(acc, 0.0)
        ou[0][...] = r.astype(BF16)
        ou[1][...] = (r * r).astype(BF16)

    (r, a), (w_mo,) = _mm("mlp_in", "nt", [(h2, D)], wmi_t, T, HID, D, tm2, tn, D, [],
                          [((T, HID), BF16, blk2, _ij), ((T, HID), BF16, blk2, _ij)], epi_relu2,
                          job=_gather_job([wmo_s], rows=[(MO_CUT, last)], into=[wmo_part]))

    def epi_loss(acc, ex, ou):
        x1_ref, t_ref, gt_ref = ex
        e = x1_ref[...] + gt_ref[...] * acc - t_ref[...]
        dy = e * (1.0 / D)
        ou[0][...] = dy
        ou[1][...] = (gt_ref[...] * dy).astype(BF16)
        ou[2][...] = _fold8(e * e) * (0.5 / D)
        ou[3][...] = _fold8(dy * acc)

    wide = (tm, 1024)
    part_w = ((nI * 8, D), F32, (8, 1024), _ij)
    dy, dz, p_loss, p_gt2 = _mm(
        "mlp_out", "nn", [(a, HID)], w_mo, T, D, HID, tm, 1024, 1024,
        [(x1, wide, _ij), (tgt, wide, _ij), (gt2, (1, 1024), lambda i, j, k: (0, j))],
        [((T, D), F32, wide, _ij), ((T, D), BF16, wide, _ij), part_w, part_w], epi_loss)

    def epi_du(acc, ex, ou):
        ou[0][...] = (acc * (2.0 * ex[0][...].astype(F32))).astype(BF16)

    (du,) = _mm("mlp_out_dx", "nt", [(dz, D)], w_mo, T, HID, D, tm2, tn, D, [(r, blk2, _ij)],
                [((T, HID), BF16, blk2, _ij)], epi_du)
    gblk = (1024, 1024)
    gwide = (1024, D)
    pair_sum = lambda nm, g, r1: _pair_sum("pair_sum_" + nm, g, r1, c_arr, _sum_rows(r1.shape[1]))
    (g_mo,) = _mm("mlp_out_dw", "tn", [(a, HID)], dz, HID, D, T, 1024, D, tk_t, [], [((HID, D), BF16, gwide, _ij)], epi_store)
    (dh2,), (r1_mo,) = _mm("mlp_in_dx", "nn", [(du, HID)], wmi_t, T, D, HID, tm, 1024, 1024, [],
                           [((T, D), F32, (tm, 1024), _ij)], epi_store, job=_pair_job([g_mo]))
    dx1, p_sh2, p_sc2, p_g2, dmo, p_gt1 = _rms_mod_bwd("rms2_bwd", dh2, x1, g2, sc2, dy, tr, gate=gt1, mo=mo)
    s_mo = pair_sum("mlp_out", g_mo, r1_mo)
    tm_row = min(512, T)
    near, far = (1, 2), (3,)
    (g_mi,), (rn_mo,) = _mm("mlp_in_dw", "tn", [(du, HID)], h2, HID, D, T, 1024, D, tk_t, [],
                            [((HID, D), BF16, gwide, _ij)], epi_store, job=_chip_job([s_mo], near))

    def epi_gates(acc, ex, ou):
        ya_ref, yb_ref, ga_ref, gb_ref = ex
        sa, sb = _sigmoid(ga_ref[...]), _sigmoid(gb_ref[...])
        ou[0][...] = (acc * sa).astype(BF16)
        ou[1][...] = (acc * sb).astype(BF16)
        ou[2][...] = (acc * ya_ref[...].astype(F32) * (sa * (1.0 - sa))).astype(BF16)
        ou[3][...] = (acc * yb_ref[...].astype(F32) * (sb * (1.0 - sb))).astype(BF16)

    o_bf = ((T, D), BF16, blk, _ij)
    (dya, dyb, dga, dgb), (rf_mo, r1_mi) = _mm(
        "out_proj_dx", "nt", [(dmo, D)], w_out, T, D, D, tm, tn, D,
        [(ya, blk, _ij), (yb, blk, _ij), (proj, blk, lambda i, j, k: (i, OFF_GTA // tn + j)),
         (proj, blk, lambda i, j, k: (i, OFF_GTB // tn + j))], [o_bf, o_bf, o_bf, o_bf], epi_gates,
        job=_both(_chip_job([s_mo], far), _pair_job([g_mi])))
    s_mi = pair_sum("mlp_in", g_mi, r1_mi)
    (g_out,) = _mm("out_proj_dw", "tn", [(merged, D)], dmo, D, D, T, 1024, 1024, tk_t, [], [((D, D), BF16, gblk, _ij)], epi_store)
    (dya_pre,) = _mm("branch_a_dx", "nn", [(dya, D)], wa_t, T, AW, D, tm, tn, D, [], [((T, AW), F32, blk, _ij)], epi_store)
    (dattn,) = _mm("branch_b_dx", "nn", [(dyb, D)], wb_t, T, BW, D, tm, tn, D, [], [((T, BW), F32, blk, _ij)], epi_store)
    (g_a,) = _mm("branch_a_dw", "tn", [(dya, D)], ya_pre, D, AW, T, 1024, 1024, tk_t, [], [((D, AW), BF16, gblk, _ij)], epi_store)
    (g_b,) = _mm("branch_b_dw", "tn", [(dyb, D)], attn, D, BW, T, 1024, 1024, tk_t, [], [((D, BW), BF16, gblk, _ij)], epi_store)
    (dqa, dfa, dia, dgg, p_lb, p_og), (rn_mi, r1_out, r1_a, r1_b) = _hgrn_bwd(
        proj, st, dya_pre, lbl, og, tt, job=_both(_chip_job([s_mi], near), _pair_job([g_out, g_a, g_b])))
    (dqb, dkv, p_qg, p_kg, p_sk), (rf_mi,) = _swa_bwd(proj, dattn, qg, kg, sk, job=_chip_job([s_mi], far))
    s_out, s_a, s_b = pair_sum("out", g_out, r1_out), pair_sum("branch_a", g_a, r1_a), pair_sum("branch_b", g_b, r1_b)
    pieces = [(dqa, AW), (dfa, AW), (dia, AW), (dgg, AW), (dqb, BW), (dkv, 2 * KVW), (dga, D), (dgb, D)]
    (g_in,), (r2_out, r2_a, r2_b) = _pieces_tn("in_proj_dw", pieces, h, 512, job=_chip_job([s_out, s_a, s_b]))
    (r1_in,) = _run_job("pair_w_in", _pair_job([g_in]))
    s_in = pair_sum("in", g_in, r1_in)
    (dx, p_sh1, p_sc1, p_g1), (r2_in,) = _pieces_nn_rms(
        "in_proj_dx", pieces, win_t, x, g1, sc1, dx1, tm, 512, job=_chip_job([s_in]))

    partials = dict(sh1=p_sh1, sc1=p_sc1, gt1=p_gt1, sh2=p_sh2, sc2=p_sc2, gt2=p_gt2, g1=p_g1, g2=p_g2,
                    lb=p_lb, og=p_og, qg=p_qg, kg=p_kg, sk=p_sk, loss=p_loss)
    sums = dict(w_in=(s_in, [r2_in]), w_branch_a=(s_a, [r2_a]), w_branch_b=(s_b, [r2_b]), w_out=(s_out, [r2_out]),
                w_mlp_in=(s_mi, [rn_mi, rf_mi]), w_mlp_out=(s_mo, [rn_mo, rf_mo]))
    return dx, sums, partials


def _exchange_slots(buf, send_sems, recv_sems):
    me = _mesh_pos()
    mine = buf.at[_index(me)]
    sends = []
    for k in range(1, N_DEV):
        cp = pltpu.make_async_remote_copy(src_ref=mine, dst_ref=mine, send_sem=send_sems.at[k - 1],
                                          recv_sem=recv_sems.at[k - 1], device_id=_flip(me, k), device_id_type=MESH)
        cp.start()
        sends.append(cp)
    for k in range(1, N_DEV):
        theirs = buf.at[_index(_flip(me, k))]
        pltpu.make_async_remote_copy(src_ref=theirs, dst_ref=theirs, send_sem=send_sems.at[k - 1],
                                     recv_sem=recv_sems.at[k - 1], device_id=_flip(me, k), device_id_type=MESH).wait_recv()
    for cp in sends:
        cp.wait_send()


ADA_W = N_MOD * D // N_DEV


def _ada_mod(c, w_ada, b_shard):
    def body(c_ref, w_ref, b_ref, mod_ref, sc_ref, cbuf, mbuf, s1, r1, s2, r2):
        me = _index(_mesh_pos())
        cbuf[me] = c_ref[...]
        _exchange_slots(cbuf, s1, r1)
        row = lax.broadcasted_iota(jnp.int32, (N_DEV, D), 0)
        call = jnp.zeros((N_DEV, D), F32)
        for d in range(N_DEV):
            call = jnp.where(row == d, cbuf[d], call)
        sc = call * _sigmoid(call)
        sc_ref[...] = sc
        mbuf[me] = _dot(sc, w_ref[...]) + b_ref[...]
        _exchange_slots(mbuf, s2, r2)
        for s in range(N_DEV):
            mod_ref[:, s * ADA_W:(s + 1) * ADA_W] = mbuf[s, pl.ds(me, 1), :]

    return pl.pallas_call(
        body, in_specs=[_VMEM, _VMEM, _VMEM], out_specs=[_VMEM, _VMEM],
        out_shape=[jax.ShapeDtypeStruct((1, N_MOD * D), F32), jax.ShapeDtypeStruct((N_DEV, D), F32)],
        scratch_shapes=[pltpu.VMEM((N_DEV, 1, D), F32), pltpu.VMEM((N_DEV, N_DEV, ADA_W), F32),
                        _SEMS(N_DEV - 1), _SEMS(N_DEV - 1), _SEMS(N_DEV - 1), _SEMS(N_DEV - 1)],
        name="ada_mod", compiler_params=pltpu.CompilerParams(vmem_limit_bytes=VMEM_LIMIT),
    )(c, w_ada, b_shard)


SMALL_SEGS = (("b_ada", N_MOD * D), ("norm1_gain", D), ("norm2_gain", D), ("lb0", AW), ("lb1", AW),
              ("hgrn_o_gain", AW), ("q_norm_gain", 128), ("k_norm_gain", 128), ("sinks", 128))
SMALL_W = sum(w for _, w in SMALL_SEGS)
X_SEGS = (("sh1", D), ("sc1", D), ("gt1", D), ("sh2", D), ("sc2", D), ("gt2", D), ("g1", D), ("g2", D),
          ("lb", AW), ("og", AW), ("qg", 128), ("kg", 128), ("sk", 128), ("loss", 128))
X_W = sum(w for _, w in X_SEGS)


def _offsets(segs):
    out, o = {}, 0
    for name, w in segs:
        out[name] = (o, w)
        o += w
    return out


def _small_reduce(parts, lb_logits):
    xo, so = _offsets(X_SEGS), _offsets(SMALL_SEGS)
    names = [nm for nm, _ in X_SEGS]

    def body(*refs):
        p_refs = dict(zip(names, refs[:len(names)]))
        lbl_ref, allx, gs_ref, loss_ref, send_sems, recv_sems = refs[len(names):]
        me = _index(_mesh_pos())
        for nm, (o, w) in xo.items():
            if nm == "loss":
                allx[me, :, o:o + w] = jnp.broadcast_to(jnp.sum(p_refs[nm][...]), (1, w))
            else:
                allx[me, :, o:o + w] = jnp.sum(p_refs[nm][...], axis=0, keepdims=True)
        _exchange_slots(allx, send_sems, recv_sems)
        tot = allx[0]
        for d in range(1, N_DEV):
            tot = tot + allx[d]
        seg = lambda nm: tot[:, xo[nm][0]:xo[nm][0] + xo[nm][1]]

        def put(nm, v):
            gs_ref[:, so[nm][0]:so[nm][0] + so[nm][1]] = v

        put("b_ada", tot[:, 0:N_MOD * D])
        put("norm1_gain", seg("g1"))
        put("norm2_gain", seg("g2"))
        lbl = lbl_ref[...]
        lb = _sigmoid(lbl[0:1, :] - lbl[1:2, :])
        dl0 = seg("lb") * lb * (1.0 - lb)
        put("lb0", dl0)
        put("lb1", -dl0)
        put("hgrn_o_gain", seg("og"))
        put("q_norm_gain", seg("qg"))
        put("k_norm_gain", seg("kg"))
        put("sinks", seg("sk"))
        loss_ref[...] = seg("loss")

    return pl.pallas_call(
        body, in_specs=[_VMEM] * (len(names) + 1), out_specs=[_VMEM, _VMEM, _VMEM],
        out_shape=[jax.ShapeDtypeStruct((N_DEV, 1, X_W), F32), jax.ShapeDtypeStruct((1, SMALL_W), F32),
                   jax.ShapeDtypeStruct((1, 128), F32)],
        scratch_shapes=[_SEMS(N_DEV - 1), _SEMS(N_DEV - 1)], name="small_reduce",
        compiler_params=pltpu.CompilerParams(vmem_limit_bytes=VMEM_LIMIT),
    )(*[parts[nm] for nm in names], lb_logits)


def _adamw_math(w, g, m, v):
    m = B1 * m + (1.0 - B1) * g
    v = B2 * v + (1.0 - B2) * (g * g)
    m_hat = m / (1.0 - B1 ** STEP)
    v_hat = v / (1.0 - B2 ** STEP)
    return -LR * (m_hat / (jnp.sqrt(v_hat) + ADAM_EPS) + WD * w), m, v


def _sum_rows(rs):
    return 256 if rs % 256 == 0 else rs // 2


def _pair_sum(name, g, recv, c_arr, tr):
    _, rs, cols = recv.shape
    blk = (1, tr, cols)

    def body(c_ref, g_ref, r_ref, o_ref):
        o_ref[...] = (g_ref[...].astype(F32) + r_ref[...].astype(F32)).astype(BF16)

    grid_spec = pltpu.PrefetchScalarGridSpec(
        num_scalar_prefetch=1, grid=(4, rs // tr),
        in_specs=[pl.BlockSpec(blk, lambda q, i, c: (2 * q + c[0], i, 0)), pl.BlockSpec(blk, lambda q, i, c: (q, i, 0))],
        out_specs=pl.BlockSpec(blk, lambda q, i, c: (q, i, 0)))
    return pl.pallas_call(body, grid_spec=grid_spec, out_shape=jax.ShapeDtypeStruct((4, rs, cols), BF16), name=name,
                          compiler_params=_params(("parallel", "parallel")))(c_arr, g.reshape(N_DEV, rs, cols), recv)


def _sum_adamw(name, sums, recvs, q_arr, w, m, v, transposed, tile):
    rows, cols = w.shape
    nR = len(recvs)

    def body(q_ref, s_ref, *refs):
        r_refs = refs[:nR]
        w_ref, m_ref, v_ref, g_ref, d_ref, nm_ref, nv_ref = refs[nR:]
        g = s_ref[0].astype(F32)
        for r_ref in r_refs:
            for slot in range(r_ref.shape[0]):
                g = g + r_ref[slot].astype(F32)
        g = g.T if transposed else g
        g_ref[...] = g
        d_ref[...], nm_ref[...], nv_ref[...] = _adamw_math(w_ref[...], g, m_ref[...], v_ref[...])

    if transposed:
        slab = lambda n, first: pl.BlockSpec((n, cols, tile), lambda i, q: (first(q), 0, i))
    else:
        slab = lambda n, first: pl.BlockSpec((n, tile, cols), lambda i, q: (first(q), i, 0))
    spec = pl.BlockSpec((tile, cols), lambda i, q: (i, 0))
    shape = jax.ShapeDtypeStruct((rows, cols), F32)
    grid_spec = pltpu.PrefetchScalarGridSpec(
        num_scalar_prefetch=1, grid=(rows // tile,),
        in_specs=[slab(1, lambda q: q[0])] + [slab(r.shape[0], lambda q: 0) for r in recvs] + [spec] * 3,
        out_specs=[spec] * 4)
    return pl.pallas_call(body, grid_spec=grid_spec, out_shape=[shape] * 4, name=name,
                          compiler_params=_params(("parallel",)))(q_arr, sums, *recvs, w, m, v)


def _adamw(name, w, g, m, v, tr):
    rows, cols = w.shape

    def body(w_ref, g_ref, m_ref, v_ref, d_ref, nm_ref, nv_ref):
        d_ref[...], nm_ref[...], nv_ref[...] = _adamw_math(w_ref[...], g_ref[...], m_ref[...], v_ref[...])

    spec = pl.BlockSpec((tr, cols), lambda i: (i, 0))
    shape = jax.ShapeDtypeStruct((rows, cols), F32)
    return pl.pallas_call(
        body, grid=(rows // tr,), in_specs=[spec] * 4, out_specs=[spec] * 3, out_shape=[shape] * 3, name=name,
        compiler_params=_params(("parallel",)),
    )(w, g, m, v)


def _ada_update(sc_t, dmod_cols, w, m, v, tr):
    rows, cols = w.shape

    def body(s_ref, d_ref, w_ref, m_ref, v_ref, g_ref, dl_ref, nm_ref, nv_ref):
        g = jnp.dot(s_ref[...], d_ref[...], precision=lax.Precision.HIGHEST, preferred_element_type=F32)
        g_ref[...] = g
        dl_ref[...], nm_ref[...], nv_ref[...] = _adamw_math(w_ref[...], g, m_ref[...], v_ref[...])

    spec = pl.BlockSpec((tr, cols), lambda i: (i, 0))
    shape = jax.ShapeDtypeStruct((rows, cols), F32)
    return pl.pallas_call(
        body, grid=(rows // tr,),
        in_specs=[pl.BlockSpec((tr, N_DEV), lambda i: (i, 0)), pl.BlockSpec((N_DEV, cols), lambda i: (0, 0)), spec, spec, spec],
        out_specs=[spec] * 4, out_shape=[shape] * 4, name="ada_update", compiler_params=_params(("parallel",)),
    )(sc_t, dmod_cols, w, m, v)


BIG = ("w_in", "w_branch_a", "w_branch_b", "w_out", "w_mlp_in", "w_mlp_out")
COLUMN_SHARDED = ("w_in", "w_branch_a", "w_branch_b", "w_mlp_in")
WEIGHTS = ("w_ada", "b_ada", "norm1_gain", "w_in", "lb_logits", "hgrn_o_gain", "q_norm_gain", "k_norm_gain", "sinks",
           "w_branch_a", "w_branch_b", "w_out", "norm2_gain", "w_mlp_in", "w_mlp_out")


def _pack_small(p):
    lb = p["lb_logits"]
    src = dict(p, lb0=lb[0:1], lb1=lb[1:2])
    return jnp.concatenate([jnp.pad(src[nm], ((0, 0), (0, w - src[nm].shape[1]))) for nm, w in SMALL_SEGS], axis=1)


def _unpack_small(vec, shapes):
    so = _offsets(SMALL_SEGS)
    out = {}
    for nm, shp in shapes.items():
        if nm == "lb_logits":
            o = so["lb0"][0]
            out[nm] = vec[0, o:o + 2 * AW].reshape(2, AW)
        else:
            o = so[nm][0]
            out[nm] = vec[:, o:o + shp[1]]
    return out


def kernel(x, c, w_ada, b_ada, norm1_gain, w_in, lb_logits, hgrn_o_gain, q_norm_gain, k_norm_gain, sinks, w_branch_a, w_branch_b, w_out, norm2_gain, w_mlp_in, w_mlp_out, loss_target, m_w_ada, m_b_ada, m_norm1_gain, m_w_in, m_lb_logits, m_hgrn_o_gain, m_q_norm_gain, m_k_norm_gain, m_sinks, m_w_branch_a, m_w_branch_b, m_w_out, m_norm2_gain, m_w_mlp_in, m_w_mlp_out, v_w_ada, v_b_ada, v_norm1_gain, v_w_in, v_lb_logits, v_hgrn_o_gain, v_q_norm_gain, v_k_norm_gain, v_sinks, v_w_branch_a, v_w_branch_b, v_w_out, v_norm2_gain, v_w_mlp_in, v_w_mlp_out):
    w = dict(w_ada=w_ada, b_ada=b_ada, norm1_gain=norm1_gain, w_in=w_in, lb_logits=lb_logits, hgrn_o_gain=hgrn_o_gain,
             q_norm_gain=q_norm_gain, k_norm_gain=k_norm_gain, sinks=sinks, w_branch_a=w_branch_a, w_branch_b=w_branch_b,
             w_out=w_out, norm2_gain=norm2_gain, w_mlp_in=w_mlp_in, w_mlp_out=w_mlp_out)
    m = dict(w_ada=m_w_ada, b_ada=m_b_ada, norm1_gain=m_norm1_gain, w_in=m_w_in, lb_logits=m_lb_logits,
             hgrn_o_gain=m_hgrn_o_gain, q_norm_gain=m_q_norm_gain, k_norm_gain=m_k_norm_gain, sinks=m_sinks,
             w_branch_a=m_w_branch_a, w_branch_b=m_w_branch_b, w_out=m_w_out, norm2_gain=m_norm2_gain,
             w_mlp_in=m_w_mlp_in, w_mlp_out=m_w_mlp_out)
    v = dict(w_ada=v_w_ada, b_ada=v_b_ada, norm1_gain=v_norm1_gain, w_in=v_w_in, lb_logits=v_lb_logits,
             hgrn_o_gain=v_hgrn_o_gain, q_norm_gain=v_q_norm_gain, k_norm_gain=v_k_norm_gain, sinks=v_sinks,
             w_branch_a=v_w_branch_a, w_branch_b=v_w_branch_b, w_out=v_w_out, norm2_gain=v_norm2_gain,
             w_mlp_in=v_w_mlp_in, w_mlp_out=v_w_mlp_out)
    for d in (w, m, v):
        for nm in ("w_ada",) + BIG:
            d[nm] = d[nm][0]
    px, py, pc = _mesh_pos()
    me = _index((px, py, pc))
    c_arr = jnp.reshape(pc, (1,)).astype(jnp.int32)
    q_arr = jnp.reshape(2 * px + py, (1,)).astype(jnp.int32)

    shards = [(w[nm].T if nm in COLUMN_SHARDED else w[nm]).astype(BF16) for nm in BIG]
    b_shard = lax.dynamic_slice(b_ada, (0, me * ADA_W), (1, ADA_W))
    mod, sc_all = _ada_mod(c, w["w_ada"], b_shard)

    dx, sums, parts = _local_step(x[0], loss_target[0], mod, norm1_gain, norm2_gain, lb_logits, hgrn_o_gain,
                                  q_norm_gain, k_norm_gain, sinks, shards, me, c_arr)

    allx, g_small, loss = _small_reduce(parts, lb_logits)

    grad, delta, new_m, new_v = {}, {}, {}, {}
    for nm in BIG:
        s, r2 = sums[nm]
        grad[nm], delta[nm], new_m[nm], new_v[nm] = _sum_adamw(
            "adamw_" + nm, s, r2, q_arr, w[nm], m[nm], v[nm], nm in COLUMN_SHARDED, 128)

    dmod_cols = lax.dynamic_slice(allx[:, 0, :], (0, me * ADA_W), (N_DEV, ADA_W))
    grad["w_ada"], delta["w_ada"], new_m["w_ada"], new_v["w_ada"] = _ada_update(
        sc_all.T, dmod_cols, w["w_ada"], m["w_ada"], v["w_ada"], 256)

    small_names = [nm for nm in WEIGHTS if nm not in BIG and nm != "w_ada"]
    shapes = {nm: w[nm].shape for nm in small_names}
    ds, ms, vs = _adamw("adamw_small", _pack_small(w), g_small, _pack_small(m), _pack_small(v), 1)
    for dst, vec in ((grad, g_small), (delta, ds), (new_m, ms), (new_v, vs)):
        dst.update(_unpack_small(vec, shapes))

    def full(d, nm):
        return d[nm][None] if nm in BIG or nm == "w_ada" else d[nm]

    return (loss[0, 0], dx[None], *[full(grad, nm) for nm in WEIGHTS], *[full(delta, nm) for nm in WEIGHTS],
            *[full(new_m, nm) for nm in WEIGHTS], *[full(new_v, nm) for nm in WEIGHTS])
```

```python
import functools

import jax
import jax.numpy as jnp
from jax import lax
from jax.experimental import pallas as pl
from jax.experimental.pallas import tpu as pltpu

F32 = jnp.float32
BF16 = jnp.bfloat16
MESH = pl.DeviceIdType.MESH

N_DEV = 8
D = 2048
A_HEADS, A_HD, CHUNK = 8, 128, 64
AW = A_HEADS * A_HD
Q_HEADS, KV_HEADS, GROUP, B_HD, BLK = 16, 4, 4, 64, 128
BW = Q_HEADS * B_HD
KVW = KV_HEADS * B_HD
HID = 4 * D
IN_W = 4 * AW + BW + 2 * KVW + 2 * D
OFF_QA, OFF_FA, OFF_IA, OFF_GA = 0, AW, 2 * AW, 3 * AW
OFF_QB = 4 * AW
OFF_KB = OFF_QB + BW
OFF_VB = OFF_KB + KVW
OFF_GTA = OFF_VB + KVW
OFF_GTB = OFF_GTA + D
N_MOD = 6
EPS = 1e-6
LR, B1, B2, ADAM_EPS, WD, STEP = 1e-3, 0.9, 0.999, 1e-8, 0.01, 10
NEG = -1e30

VMEM_LIMIT = 56 * 1024 * 1024
MI_CUTS = (512, 928)
MO_CUT = 336

NN = (((1,), (0,)), ((), ()))
NT = (((1,), (1,)), ((), ()))
TN = (((0,), (0,)), ((), ()))
BNN = (((2,), (1,)), ((0,), (0,)))
BNT = (((2,), (2,)), ((0,), (0,)))
BTN = (((1,), (1,)), ((0,), (0,)))


def _dot(a, b, dims=NN):
    return lax.dot_general(a.astype(BF16), b.astype(BF16), dims, preferred_element_type=F32)


def _params(sem):
    return pltpu.CompilerParams(dimension_semantics=sem, vmem_limit_bytes=VMEM_LIMIT)


def _sigmoid(x):
    return 1.0 / (1.0 + jnp.exp(-x))


def _fold8(v):
    r, n = v.shape
    return jnp.sum(v.reshape(r // 8, 8, n), axis=0)


_VMEM = pl.BlockSpec(memory_space=pltpu.VMEM)
_ANY = pl.BlockSpec(memory_space=pl.ANY)
_SEMS = lambda n: pltpu.SemaphoreType.DMA((n,))


def _mesh_pos():
    return lax.axis_index("x"), lax.axis_index("y"), lax.axis_index("c")


def _flip(pos, k):
    return tuple(1 - p if (k >> s) & 1 else p for p, s in zip(pos, (2, 1, 0)))


def _index(pos):
    return 4 * pos[0] + 2 * pos[1] + pos[2]


class _Job:
    def __init__(self, ins, out_shape, sems, start, finish, aliases=None):
        self.ins, self.out_shape, self.sems, self.start, self.finish = list(ins), list(out_shape), list(sems), start, finish
        self.aliases = dict(aliases or {})


def _both(j1, j2):
    assert not j1.aliases and not j2.aliases
    n_in, n_out, n_sem = len(j1.ins), len(j1.out_shape), len(j1.sems)
    first = lambda ins, outs, sems: (ins[:n_in], outs[:n_out], sems[:n_sem])
    second = lambda ins, outs, sems: (ins[n_in:], outs[n_out:], sems[n_sem:])

    def start(*refs):
        j1.start(*first(*refs))
        j2.start(*second(*refs))

    def finish(*refs):
        j1.finish(*first(*refs))
        j2.finish(*second(*refs))

    return _Job(j1.ins + j2.ins, j1.out_shape + j2.out_shape, j1.sems + j2.sems, start, finish)


def _pcall(body, *, grid, in_specs, out_specs, out_shape, scratch_shapes, name, semantics, args, job=None):
    if job is None:
        outs = pl.pallas_call(body, grid=grid, in_specs=in_specs, out_specs=out_specs, out_shape=out_shape,
                              scratch_shapes=scratch_shapes, name=name, compiler_params=_params(semantics))(*args)
        return list(outs), []
    n_in, n_out, n_scr = len(in_specs), len(out_specs), len(scratch_shapes)
    j_in, j_out = len(job.ins), len(job.out_shape)
    steps = tuple(grid)

    def carrier(*refs):
        o = 0
        main_in, o = refs[o:o + n_in], o + n_in
        job_in, o = refs[o:o + j_in], o + j_in
        main_out, o = refs[o:o + n_out], o + n_out
        job_out, o = refs[o:o + j_out], o + j_out
        main_scr, job_sems = refs[o:o + n_scr], refs[o + n_scr:]
        ids = [pl.program_id(a) for a in range(len(steps))]
        first = functools.reduce(lambda p, q: p & q, [i == 0 for i in ids])
        last = functools.reduce(lambda p, q: p & q, [i == s - 1 for i, s in zip(ids, steps)])

        @pl.when(first)
        def _():
            job.start(job_in, job_out, job_sems)

        body(*main_in, *main_out, *main_scr)

        @pl.when(last)
        def _():
            job.finish(job_in, job_out, job_sems)

    outs = pl.pallas_call(
        carrier, grid=grid, in_specs=list(in_specs) + [_ANY] * j_in, out_specs=list(out_specs) + [_ANY] * j_out,
        out_shape=list(out_shape) + job.out_shape, scratch_shapes=list(scratch_shapes) + job.sems, name=name,
        input_output_aliases={n_in + i: n_out + o for i, o in job.aliases.items()},
        compiler_params=_params(("arbitrary",) * len(steps)),
    )(*args, *job.ins)
    return list(outs[:n_out]), list(outs[n_out:])


def _run_job(name, job):
    j_in, j_out = len(job.ins), len(job.out_shape)

    def body(*refs):
        ins, outs, sems = refs[:j_in], refs[j_in:j_in + j_out], refs[j_in + j_out:]
        job.start(ins, outs, sems)
        job.finish(ins, outs, sems)

    return list(pl.pallas_call(body, in_specs=[_ANY] * j_in, out_specs=[_ANY] * j_out, out_shape=job.out_shape,
                               scratch_shapes=job.sems, name=name,
                               input_output_aliases=job.aliases)(*job.ins))


def _gather_job(shards, rows=None, into=None):
    n = len(shards)
    rows = rows or [(0, s.shape[0]) for s in shards]
    into = into or [None] * n
    olds, aliases = [], {}
    for a, buf in enumerate(into):
        if buf is not None:
            aliases[n + len(olds)] = a
            olds.append(buf)

    def copies(ins, outs, sems):
        send_sems, recv_sems, local_sems = sems
        x, y, c = _mesh_pos()
        me, sib = (x, y, c), (x, y, 1 - c)
        chips = [(1 - x, y), (x, 1 - y), (1 - x, 1 - y)]

        def part(a, p):
            rs, (r0, r1) = shards[a].shape[0], rows[a]
            return outs[a].at[pl.ds(_index(p) * rs + r0, r1 - r0), :]

        own = lambda a: ins[a].at[pl.ds(rows[a][0], rows[a][1] - rows[a][0]), :]

        def copy(a, k, block, to, src=None):
            return pltpu.make_async_remote_copy(
                src_ref=part(a, block) if src is None else src, dst_ref=part(a, block),
                send_sem=send_sems.at[7 * a + k], recv_sem=recv_sems.at[7 * a + k], device_id=to, device_id_type=MESH)

        mine = [pltpu.make_async_copy(own(a), part(a, me), local_sems.at[a]) for a in range(n)]
        first = []
        for a in range(n):
            first.append(copy(a, 0, me, sib, src=own(a)))
            first += [copy(a, 1 + j, me, (*chip, c), src=own(a)) for j, chip in enumerate(chips)]
        return me, sib, c, chips, copy, mine, first

    def start(ins, outs, sems):
        *_, mine, first = copies(ins, outs, sems)
        for cp in mine + first:
            cp.start()

    def finish(ins, outs, sems):
        me, sib, c, chips, copy, mine, first = copies(ins, outs, sems)
        passed = []
        for j, chip in enumerate(chips):
            for a in range(n):
                copy(a, 1 + j, (*chip, c), me).wait_recv()
                cp = copy(a, 4 + j, (*chip, c), sib)
                cp.start()
                passed.append(cp)
        for a in range(n):
            copy(a, 0, sib, me).wait_recv()
            for j, chip in enumerate(chips):
                copy(a, 4 + j, (*chip, 1 - c), me).wait_recv()
        for cp in first + passed:
            cp.wait_send()
        for cp in mine:
            cp.wait()

    return _Job(list(shards) + olds, [jax.ShapeDtypeStruct((N_DEV * s.shape[0], s.shape[1]), s.dtype) for s in shards],
                [_SEMS(7 * n), _SEMS(7 * n), _SEMS(n)], start, finish, aliases)


def _gather_relay_job(shards):
    n = len(shards)

    def tools(ins, outs, sems):
        send_sems, recv_sems, local_sems = sems
        x, y, c = _mesh_pos()
        q = 2 * x + y
        chip_at = lambda rel: (1 - x if rel & 2 else x, 1 - y if rel & 1 else y)

        def rows(a, chip, core):
            rs = shards[a].shape[0]
            return outs[a].at[pl.ds((2 * chip + core) * rs, rs), :]

        def copy(a, slot, chip, core, to, src=None):
            blk = rows(a, chip, core)
            return pltpu.make_async_remote_copy(src_ref=blk if src is None else src, dst_ref=blk,
                                                send_sem=send_sems.at[7 * a + slot], recv_sem=recv_sems.at[7 * a + slot],
                                                device_id=to, device_id_type=MESH)

        mine = [pltpu.make_async_copy(ins[a], rows(a, q, c), local_sems.at[a]) for a in range(n)]
        first = [copy(a, slot, q, c, (x, y, 1 - c) if slot == 0 else (*chip_at(slot), c), src=ins[a])
                 for a in range(n) for slot in (0, 1, 2)]
        return x, y, c, q, chip_at, copy, mine, first

    def start(ins, outs, sems):
        *_, mine, first = tools(ins, outs, sems)
        for cp in mine + first:
            cp.start()

    def finish(ins, outs, sems):
        x, y, c, q, chip_at, copy, mine, first = tools(ins, outs, sems)
        me, sib = (x, y, c), (x, y, 1 - c)

        def relay(src, dst):
            for a in range(n):
                copy(a, src, q ^ src, c, me).wait_recv()
                copy(a, 3, q ^ src, c, (*chip_at(dst), c)).start()
                copy(a, 3 + src, q ^ src, c, sib).start()
            for a in range(n):
                copy(a, dst, q ^ dst, c, me).wait_recv()
                copy(a, 3 + dst, q ^ dst, c, sib).start()

        pl.when(c == 1)(lambda: relay(1, 2))
        pl.when(c == 0)(lambda: relay(2, 1))
        for a in range(n):
            copy(a, 3, q ^ 3, c, me).wait_recv()
            copy(a, 6, q ^ 3, c, sib).start()
        for a in range(n):
            copy(a, 0, q, 1 - c, me).wait_recv()
            for rel in (1, 2, 3):
                copy(a, 3 + rel, q ^ rel, 1 - c, me).wait_recv()
        for a in range(n):
            for slot in range(3, 7):
                copy(a, slot, q, c, sib).wait_send()
        for cp in first:
            cp.wait_send()
        for cp in mine:
            cp.wait()

    return _Job(shards, [jax.ShapeDtypeStruct((N_DEV * s.shape[0], s.shape[1]), s.dtype) for s in shards],
                [_SEMS(7 * n), _SEMS(7 * n), _SEMS(n)], start, finish)


def _pair_job(grads):
    n = len(grads)

    def copies(ins, outs, sems):
        send_sems, recv_sems = sems
        x, y, c = _mesh_pos()
        out = []
        for a in range(n):
            rs = grads[a].shape[0] // N_DEV
            for q in range(4):
                blk = ins[a].at[pl.ds((2 * q + 1 - c) * rs, rs), :]
                out.append(pltpu.make_async_remote_copy(
                    src_ref=blk, dst_ref=outs[a].at[q], send_sem=send_sems.at[4 * a + q], recv_sem=recv_sems.at[4 * a + q],
                    device_id=(x, y, 1 - c), device_id_type=MESH))
        return out

    def start(ins, outs, sems):
        for cp in copies(ins, outs, sems):
            cp.start()

    def finish(ins, outs, sems):
        for cp in copies(ins, outs, sems):
            cp.wait()

    return _Job(grads, [jax.ShapeDtypeStruct((4, g.shape[0] // N_DEV, g.shape[1]), g.dtype) for g in grads],
                [_SEMS(4 * n), _SEMS(4 * n)], start, finish)


def _chip_job(sums, rels=(1, 2, 3)):
    n, nr = len(sums), len(rels)

    def copies(ins, outs, sems):
        send_sems, recv_sems = sems
        x, y, c = _mesh_pos()
        out = []
        for a in range(n):
            for slot, r in enumerate(rels):
                px, py = (1 - x if r & 2 else x), (1 - y if r & 1 else y)
                out.append(pltpu.make_async_remote_copy(
                    src_ref=ins[a].at[2 * px + py], dst_ref=outs[a].at[slot], send_sem=send_sems.at[nr * a + slot],
                    recv_sem=recv_sems.at[nr * a + slot], device_id=(px, py, c), device_id_type=MESH))
        return out

    def start(ins, outs, sems):
        for cp in copies(ins, outs, sems):
            cp.start()

    def finish(ins, outs, sems):
        for cp in copies(ins, outs, sems):
            cp.wait()

    return _Job(sums, [jax.ShapeDtypeStruct((nr,) + s.shape[1:], s.dtype) for s in sums],
                [_SEMS(nr * n), _SEMS(nr * n)], start, finish)


def _mm(name, form, a_list, b, M, N, K, tm, tn, tk, extras, outs, epi, job=None, acc_as_ref=False):
    nI, nJ, nK = M // tm, N // tn, K // tk
    assert nI * tm == M and nJ * tn == N and nK * tk == K
    dims = {"nn": NN, "nt": NT, "tn": TN}[form]
    b_list = b if isinstance(b, list) else [(b, {"nn": N, "nt": K, "tn": N}[form])]
    nA, nB = len(a_list), len(b_list)
    assert nA == 1 or nB == 1
    assert nB == 1 or form in ("nn", "nt")
    AXIS = {"i": 0, "j": 1, "k": 2}
    a_axis, a_tile = ("i", tm) if form == "tn" else ("k", tk)
    b_axis, b_tile = ("k", tk) if form == "nt" else ("j", tn)

    def cut(pieces, tile, total):
        starts, s = [], 0
        for _, w in pieces:
            assert w % tile == 0
            starts.append(s // tile)
            s += w
        assert s == total
        return starts, [w // tile for _, w in pieces]

    a_st, a_cn = cut(a_list, a_tile, M if form == "tn" else K)
    b_st, b_cn = cut(b_list, b_tile, K if form == "nt" else N)

    def inside(idx, st, cn):
        return (idx >= st) & (idx < st + cn)

    def a_spec(p):
        st, cn = a_st[p], a_cn[p]
        if form == "tn":
            return pl.BlockSpec((tk, tm), lambda i, j, k: (jnp.where(inside(i, st, cn), k, 0), jnp.clip(i - st, 0, cn - 1)))
        return pl.BlockSpec((tm, tk), lambda i, j, k: (i, jnp.clip(k - st, 0, cn - 1)))

    def b_spec(p):
        st, cn = b_st[p], b_cn[p]
        if form == "nt":
            return pl.BlockSpec((tn, tk), lambda i, j, k: (j, jnp.clip(k - st, 0, cn - 1)))
        if nB == 1:
            return pl.BlockSpec((tk, tn), lambda i, j, k: (k, j))
        return pl.BlockSpec((tk, tn), lambda i, j, k: (jnp.where(inside(j, st, cn), k, 0), jnp.clip(j - st, 0, cn - 1)))

    in_specs = ([a_spec(p) for p in range(nA)] + [b_spec(p) for p in range(nB)]
                + [pl.BlockSpec(bs, im) for _, bs, im in extras])
    out_shape = [jax.ShapeDtypeStruct(s_, d_) for s_, d_, _, _ in outs]
    out_specs = [pl.BlockSpec(bs, im) for _, _, bs, im in outs]
    nE, nO = len(extras), len(outs)
    single = nA == 1 and nB == 1

    def body(*refs):
        a_refs, b_refs = refs[:nA], refs[nA:nA + nB]
        ex, ou = refs[nA + nB:nA + nB + nE], refs[nA + nB + nE:nA + nB + nE + nO]
        ids = [pl.program_id(a) for a in range(3)]

        def partial_of(p, q):
            return lax.dot_general(a_refs[p][...], b_refs[q][...], dims, preferred_element_type=F32)

        if nK == 1 and single:
            epi(partial_of(0, 0), ex, ou)
            return
        acc = refs[-1]
        k = ids[2]
        for p in range(nA):
            for q in range(nB):
                def first(p=p, q=q):
                    acc[...] = partial_of(p, q)

                def later(p=p, q=q):
                    acc[...] += partial_of(p, q)

                here = None
                if nA > 1:
                    here = inside(ids[AXIS[a_axis]], a_st[p], a_cn[p])
                if nB > 1:
                    here = inside(ids[AXIS[b_axis]], b_st[q], b_cn[q])
                pl.when(k == 0 if here is None else here & (k == 0))(first)
                pl.when(k > 0 if here is None else here & (k > 0))(later)

        @pl.when(k == nK - 1)
        def _():
            epi(acc if acc_as_ref else acc[...], ex, ou)

    scratch = [] if (nK == 1 and single) else [pltpu.VMEM((tm, tn), F32)]
    res, job_res = _pcall(
        body, grid=(nI, nJ, nK), in_specs=in_specs, out_specs=out_specs, out_shape=out_shape, scratch_shapes=scratch,
        name=name, semantics=("parallel", "parallel", "arbitrary"),
        args=[a for a, _ in a_list] + [p for p, _ in b_list] + [e for e, _, _ in extras], job=job)
    return res if job is None else (res, job_res)


def _piece_tiles(pieces, tile):
    starts, s = [], 0
    for _, w in pieces:
        assert w % tile == 0
        starts.append(s // tile)
        s += w
    return starts, [w // tile for _, w in pieces], s


def _pieces_tn(name, pieces, b, tile, job=None):
    T, N = b.shape
    st, cn, M = _piece_tiles(pieces, tile)
    nP, nI = len(pieces), M // tile

    def body(*refs):
        p_refs, b_hbm, o_ref = refs[:nP], refs[nP], refs[nP + 1]
        bbuf, abuf, bsem, asem = refs[nP + 2:]
        i = pl.program_id(0)

        def fetch(step, slot):
            for p in range(nP):
                @pl.when((step >= st[p]) & (step < st[p] + cn[p]))
                def _():
                    col = pl.multiple_of((step - st[p]) * tile, tile)
                    pltpu.make_async_copy(p_refs[p].at[pl.ds(0, T), pl.ds(col, tile)], abuf.at[slot], asem.at[slot]).start()

        @pl.when(i == 0)
        def _():
            whole = pltpu.make_async_copy(b_hbm, bbuf, bsem)
            whole.start()
            fetch(0, 0)
            whole.wait()

        @pl.when(i + 1 < nI)
        def _():
            fetch(i + 1, (i + 1) % 2)

        pltpu.make_async_copy(p_refs[0].at[pl.ds(0, T), pl.ds(0, tile)], abuf.at[i % 2], asem.at[i % 2]).wait()
        o_ref[...] = lax.dot_general(abuf[i % 2], bbuf[...], TN, preferred_element_type=F32).astype(BF16)

    res, job_res = _pcall(
        body, grid=(nI,), in_specs=[_ANY] * (nP + 1), out_specs=[pl.BlockSpec((tile, N), lambda i: (i, 0))],
        out_shape=[jax.ShapeDtypeStruct((M, N), BF16)],
        scratch_shapes=[pltpu.VMEM((T, N), b.dtype), pltpu.VMEM((2, T, tile), b.dtype), pltpu.SemaphoreType.DMA, _SEMS(2)],
        name=name, semantics=("arbitrary",), args=[p for p, _ in pieces] + [b], job=job)
    return res if job is None else (res, job_res)


def _pieces_nn_rms(name, pieces, w, x, gain, sc, dres, tm, tk, gate=None, mo=None, job=None):
    T = x.shape[0]
    with_gate = gate is not None
    st, cn, K = _piece_tiles(pieces, tk)
    nP, nI, nK = len(pieces), T // tm, K // tk
    _, outs, epi = _rms_mod_bwd_epilogue(x, gain, sc, dres, tm, gate, mo)
    parts = [outs[1], outs[2], outs[3]] + ([outs[5]] if with_gate else [])
    part_specs = [pl.BlockSpec(bs, lambda i, k, im=im: im(i, 0, k)) for _, _, bs, im in parts]
    n_vec, n_any_in, n_any_out = (3 if with_gate else 2), (3 if with_gate else 2), (2 if with_gate else 1)

    def body(*refs):
        o = nP
        p_refs, w_ref = refs[:nP], refs[o]
        vecs = refs[o + 1:o + 1 + n_vec]
        ins = refs[o + 1 + n_vec:o + 1 + n_vec + n_any_in]
        o = o + 1 + n_vec + n_any_in
        hbm_outs, p_outs = refs[o:o + n_any_out], refs[o + n_any_out:o + n_any_out + len(parts)]
        o = o + n_any_out + len(parts)
        acc, abuf, xbuf, rbuf = refs[o:o + 4]
        mbuf = refs[o + 4] if with_gate else None
        asem, in_sems, out_sems = refs[-3:]
        i, k = pl.program_id(0), pl.program_id(1)
        g = i * nK + k
        rows_of = lambda ref, ii: ref.at[pl.ds(pl.multiple_of(ii * tm, tm), tm), :]
        bufs_in = [xbuf, rbuf] + ([mbuf] if with_gate else [])
        bufs_out = [rbuf] + ([mbuf] if with_gate else [])

        def fetch(ii, kk, slot):
            for p in range(nP):
                @pl.when((kk >= st[p]) & (kk < st[p] + cn[p]))
                def _():
                    col = pl.multiple_of((kk - st[p]) * tk, tk)
                    src = p_refs[p].at[pl.ds(pl.multiple_of(ii * tm, tm), tm), pl.ds(col, tk)]
                    pltpu.make_async_copy(src, abuf.at[slot], asem.at[slot]).start()

        loads = lambda ii: [pltpu.make_async_copy(rows_of(src, ii), buf, in_sems.at[n])
                            for n, (src, buf) in enumerate(zip(ins, bufs_in))]
        stores = lambda ii: [pltpu.make_async_copy(buf, rows_of(dst, ii), out_sems.at[n])
                             for n, (buf, dst) in enumerate(zip(bufs_out, hbm_outs))]

        @pl.when(g == 0)
        def _():
            fetch(0, 0, 0)

        @pl.when(g + 1 < nI * nK)
        def _():
            last_k = k == nK - 1
            fetch(jnp.where(last_k, i + 1, i), jnp.where(last_k, 0, k + 1), (g + 1) % 2)

        @pl.when(k == 0)
        def _():
            @pl.when(i > 0)
            def _():
                for cp in stores(i - 1):
                    cp.wait()
            for cp in loads(i):
                cp.start()

        pltpu.make_async_copy(p_refs[0].at[pl.ds(0, tm), pl.ds(0, tk)], abuf.at[g % 2], asem.at[g % 2]).wait()

        def product(cols):
            return jnp.dot(abuf[g % 2], w_ref[:, cols], preferred_element_type=F32)

        col_blocks = [slice(c0, c0 + 512) for c0 in range(0, D, 512)]

        @pl.when(k == 0)
        def _():
            for cols in col_blocks:
                acc[:, cols] = product(cols)

        @pl.when(k > 0)
        def _():
            for cols in col_blocks:
                acc[:, cols] += product(cols)

        @pl.when(k == nK - 1)
        def _():
            for cp in loads(i):
                cp.wait()
            if with_gate:
                epi(acc, [xbuf, vecs[0], vecs[1], rbuf, vecs[2], mbuf], [rbuf, *p_outs[:3], mbuf, p_outs[3]])
            else:
                epi(acc, [xbuf, vecs[0], vecs[1], rbuf], [rbuf, *p_outs])
            for cp in stores(i):
                cp.start()

            @pl.when(i == nI - 1)
            def _():
                for cp in stores(i):
                    cp.wait()

    vec = pl.BlockSpec((1, D), lambda i, k: (0, 0))
    scratch = [pltpu.VMEM((tm, D), F32), pltpu.VMEM((2, tm, tk), BF16), pltpu.VMEM((tm, D), F32), pltpu.VMEM((tm, D), F32)]
    scratch += ([pltpu.VMEM((tm, D), BF16)] if with_gate else []) + [_SEMS(2), _SEMS(n_any_in), _SEMS(n_any_out)]
    res, job_res = _pcall(
        body, grid=(nI, nK),
        in_specs=[_ANY] * nP + [pl.BlockSpec((tk, D), lambda i, k: (k, 0))] + [vec] * n_vec + [_ANY] * n_any_in,
        out_specs=[_ANY] * n_any_out + part_specs,
        out_shape=([jax.ShapeDtypeStruct((T, D), F32)] + ([jax.ShapeDtypeStruct((T, D), BF16)] if with_gate else [])
                   + [jax.ShapeDtypeStruct(s, d) for s, d, _, _ in parts]),
        scratch_shapes=scratch, name=name, semantics=("arbitrary", "arbitrary"),
        args=([p for p, _ in pieces] + [w, gain, sc] + ([gate] if with_gate else []) + [x, dres]
              + ([mo] if with_gate else [])), job=job)
    return res if job is None else (res, job_res)


def _rms_mod_fwd(name, x, gain, sc, sh, tr):
    T = x.shape[0]

    def body(x_ref, g_ref, sc_ref, sh_ref, h_ref):
        xv = x_ref[...]
        rstd = lax.rsqrt(jnp.mean(xv * xv, axis=-1, keepdims=True) + EPS)
        h_ref[...] = ((xv * rstd * g_ref[...]) * (1.0 + sc_ref[...]) + sh_ref[...]).astype(BF16)

    row = pl.BlockSpec((tr, D), lambda i: (i, 0))
    vec = pl.BlockSpec((1, D), lambda i: (0, 0))
    return pl.pallas_call(
        body, grid=(T // tr,), in_specs=[row, vec, vec, vec], out_specs=row,
        out_shape=jax.ShapeDtypeStruct((T, D), BF16), name=name, compiler_params=_params(("parallel",)),
    )(x, gain, sc, sh)


def _rms_mod_bwd_epilogue(x, gain, sc, dres, tm, gate=None, mo=None):
    T = x.shape[0]
    with_gate = gate is not None
    row = ((tm, D), lambda i, j, k: (i, 0))
    vec = ((1, D), lambda i, j, k: (0, 0))
    part = ((T // tm * 8, D), F32, (8, D), lambda i, j, k: (i, 0))
    extras = [(x, *row), (gain, *vec), (sc, *vec), (dres, *row)]
    outs = [((T, D), F32, *row), part, part, part]
    if with_gate:
        extras += [(gate, *vec), (mo, *row)]
        outs += [((T, D), BF16, *row), part]

    rows = min(64, tm)

    def epi(acc, ex, ou):
        g = ex[1][...]
        sums = [jnp.zeros((8, D), F32) for _ in range(4)]
        for r0 in range(0, tm, rows):
            rs = slice(r0, r0 + rows)
            dhv, xv = acc[rs, :], ex[0][rs, :]
            rstd = lax.rsqrt(jnp.mean(xv * xv, axis=-1, keepdims=True) + EPS)
            xhat = xv * rstd
            dn = dhv * (1.0 + ex[2][...])
            dxhat = dn * g
            dx = ex[3][rs, :] + rstd * (dxhat - xhat * jnp.mean(dxhat * xhat, axis=-1, keepdims=True))
            ou[0][rs, :] = dx
            terms = [dhv, dhv * (xhat * g), dn * xhat]
            if with_gate:
                terms.append(dx * ex[5][rs, :].astype(F32))
                ou[4][rs, :] = (ex[4][...] * dx).astype(BF16)
            sums = [s + _fold8(t) for s, t in zip(sums, terms)] + sums[len(terms):]
        ou[1][...], ou[2][...], ou[3][...] = sums[:3]
        if with_gate:
            ou[5][...] = sums[3]

    return extras, outs, epi


def _rms_mod_bwd(name, dh, x, gain, sc, dres, tr, gate=None, mo=None):
    T = x.shape[0]
    extras, outs, epi = _rms_mod_bwd_epilogue(x, gain, sc, dres, tr, gate, mo)
    rows_only = lambda im: (lambda i: im(i, 0, 0))
    nE = len(extras)

    def body(dh_ref, *refs):
        epi(dh_ref, refs[:nE], refs[nE:])

    return pl.pallas_call(
        body, grid=(T // tr,),
        in_specs=[pl.BlockSpec((tr, D), lambda i: (i, 0))] + [pl.BlockSpec(bs, rows_only(im)) for _, bs, im in extras],
        out_specs=[pl.BlockSpec(bs, rows_only(im)) for _, _, bs, im in outs],
        out_shape=[jax.ShapeDtypeStruct(s, d) for s, d, _, _ in outs], name=name, compiler_params=_params(("parallel",)),
    )(dh, *[e for e, _, _ in extras])


def _split3(v):
    h = v.astype(BF16)
    r1 = v - h.astype(F32)
    m = r1.astype(BF16)
    lo = (r1 - m.astype(F32)).astype(BF16)
    return h, m, lo


def _tri_mm(tri, v, dims=NN):
    h, m, lo = _split3(v)
    t = tri.astype(BF16)
    mm = lambda p: lax.dot_general(t, p, dims, preferred_element_type=F32)
    return (mm(lo) + mm(m)) + mm(h)


def _hgrn_chunk_terms(q, fl, lb):
    sig = _sigmoid(fl)
    f = lb + (1.0 - lb) * sig
    lf = jnp.log(f)
    kk = 1.0 - f
    sq = _sigmoid(q)
    qf = q * sq
    return sig, f, lf, kk, sq, qf


def _causal(n):
    r = lax.broadcasted_iota(jnp.int32, (n, n), 0)
    c = lax.broadcasted_iota(jnp.int32, (n, n), 1)
    return r >= c


def _hgrn_fwd(proj, lb_logits, o_gain, tt, job=None):
    T = proj.shape[0]
    nT, ncl = T // tt, tt // CHUNK
    C = CHUNK

    def body(q_ref, f_ref, i_ref, g_ref, lbl_ref, og_ref, y_ref, st_ref, S):
        @pl.when(pl.program_id(1) == 0)
        def _():
            S[...] = jnp.zeros_like(S)

        lbl = lbl_ref[...]
        lb = _sigmoid(lbl[0:1, :] - lbl[1:2, :])
        og = og_ref[...]
        shp = (ncl, C, A_HD)
        q, fl, v, g = (r[...].reshape(shp) for r in (q_ref, f_ref, i_ref, g_ref))
        tri = jnp.broadcast_to(_causal(C), (ncl, C, C))
        _, _, lf, kk, _, qf = _hgrn_chunk_terms(q, fl, lb)
        b = _tri_mm(tri, lf, BNN)
        bm, bl = b[:, C // 2 - 1:C // 2, :], b[:, C - 1:C, :]
        qd, kd = qf * jnp.exp(b - bm), kk * jnp.exp(bm - b)
        A = jnp.where(tri, _dot(qd, kd, BNT), 0.0)
        d_st = _dot(v, kk * jnp.exp(bl - b), BTN)
        dec = jnp.exp(bl)
        st = S[...]
        for ci in range(ncl):
            st_ref[0, ci] = st
            st = st * dec[ci] + d_st[ci]
        S[...] = st
        o = _dot(A, v, BNN) + _dot(qf * jnp.exp(b), st_ref[0], BNT)
        r = lax.rsqrt(jnp.mean(o * o, axis=-1, keepdims=True) + EPS)
        y_ref[...] = (o * r * og * (g * _sigmoid(g))).astype(BF16).reshape(tt, A_HD)

    def col(off):
        return pl.BlockSpec((tt, A_HD), lambda h, t: (t, off // A_HD + h))

    head_vec = lambda rows: pl.BlockSpec((rows, A_HD), lambda h, t: (0, h))
    return _pcall(
        body, grid=(A_HEADS, nT),
        in_specs=[col(OFF_QA), col(OFF_FA), col(OFF_IA), col(OFF_GA), head_vec(2), head_vec(1)],
        out_specs=[pl.BlockSpec((tt, A_HD), lambda h, t: (t, h)),
                   pl.BlockSpec((1, ncl, A_HD, A_HD), lambda h, t: (h, t, 0, 0))],
        out_shape=[jax.ShapeDtypeStruct((T, AW), BF16),
                   jax.ShapeDtypeStruct((A_HEADS, T // C, A_HD, A_HD), F32)],
        scratch_shapes=[pltpu.VMEM((A_HD, A_HD), F32)], name="hgrn_fwd", semantics=("parallel", "arbitrary"),
        args=[proj, proj, proj, proj, lb_logits, o_gain], job=job)


def _hgrn_bwd(proj, st, dy, lb_logits, o_gain, tt, job=None):
    T = proj.shape[0]
    nT, ncl = T // tt, tt // CHUNK
    C = CHUNK

    def body(q_ref, f_ref, i_ref, g_ref, st_ref, dy_ref, lbl_ref, og_ref,
             dq_ref, df_ref, di_ref, dg_ref, plb_ref, pog_ref, dS):
        @pl.when(pl.program_id(1) == 0)
        def _():
            dS[...] = jnp.zeros_like(dS)

        lbl = lbl_ref[...]
        lb = _sigmoid(lbl[0:1, :] - lbl[1:2, :])
        og = og_ref[...]
        shp = (ncl, C, A_HD)
        flat = lambda t: t.reshape(tt, A_HD)
        q, fl, v, g, dout = (r[...].reshape(shp) for r in (q_ref, f_ref, i_ref, g_ref, dy_ref))
        tri = jnp.broadcast_to(_causal(C), (ncl, C, C))
        rowi = lax.broadcasted_iota(jnp.int32, shp, 1)
        st0 = st_ref[0]
        sig, f, lf, kk, sq, qf = _hgrn_chunk_terms(q, fl, lb)
        b = _tri_mm(tri, lf, BNN)
        bm, bl = b[:, C // 2 - 1:C // 2, :], b[:, C - 1:C, :]
        e_qd, e_kd, e_ke, e_b = jnp.exp(b - bm), jnp.exp(bm - b), jnp.exp(bl - b), jnp.exp(b)
        qd, kd, ke, qe = qf * e_qd, kk * e_kd, kk * e_ke, qf * e_b
        dec = jnp.exp(bl)
        A = jnp.where(tri, _dot(qd, kd, BNT), 0.0)
        o = _dot(A, v, BNN) + _dot(qe, st0, BNT)
        r = lax.rsqrt(jnp.mean(o * o, axis=-1, keepdims=True) + EPS)
        sg = _sigmoid(g)
        on = o * r * og
        dg_ref[...] = flat((dout * on * (sg * (1.0 + g * (1.0 - sg)))).astype(BF16))
        don = dout * (g * sg)
        pog_ref[...] = _fold8(flat(don * o * r))
        dyh = don * og
        do = r * (dyh - o * (r * r) * jnp.mean(dyh * o, axis=-1, keepdims=True))
        g_st = _dot(do, qe, BTN)
        run = dS[...]
        after = [None] * ncl
        for ci in reversed(range(ncl)):
            after[ci] = run
            run = g_st[ci] + run * dec[ci]
        dS[...] = run
        d_after = jnp.stack(after, axis=0)
        ddec = jnp.sum(d_after * st0, axis=1, keepdims=True)
        dqe = _dot(do, st0, BNN)
        dke = _dot(v, d_after, BNN)
        dA = jnp.where(tri, _dot(do, v, BNT), 0.0)
        dv = _dot(ke, d_after, BNT) + _dot(A, do, BTN)
        dqd = _dot(dA, kd, BNN)
        dkd = _dot(dA, qd, BTN)
        di_ref[...] = flat(dv.astype(BF16))
        dqf = dqe * e_b + dqd * e_qd
        dkk = dkd * e_kd + dke * e_ke
        t_qd, t_kd, t_ke = dqd * qd, dkd * kd, dke * ke
        db = dqe * qe + t_qd - t_kd - t_ke
        dbm = jnp.sum(t_kd - t_qd, axis=1, keepdims=True)
        dbl = jnp.sum(t_ke, axis=1, keepdims=True) + ddec * dec
        db = db + jnp.where(rowi == C // 2 - 1, dbm, 0.0) + jnp.where(rowi == C - 1, dbl, 0.0)
        dlf = _tri_mm(tri, db, BTN)
        dfv = dlf / f - dkk
        df_ref[...] = flat((dfv * (1.0 - lb) * sig * (1.0 - sig)).astype(BF16))
        plb_ref[...] = _fold8(flat(dfv * (1.0 - sig)))
        dq_ref[...] = flat((dqf * (sq * (1.0 + q * (1.0 - sq)))).astype(BF16))

    def col(off):
        return pl.BlockSpec((tt, A_HD), lambda h, t: (nT - 1 - t, off // A_HD + h))

    head_vec = lambda rows: pl.BlockSpec((rows, A_HD), lambda h, t: (0, h))
    o_spec = pl.BlockSpec((tt, A_HD), lambda h, t: (nT - 1 - t, h))
    p_spec = pl.BlockSpec((8, A_HD), lambda h, t: (t, h))
    o_shape = jax.ShapeDtypeStruct((T, AW), BF16)
    p_shape = jax.ShapeDtypeStruct((nT * 8, AW), F32)
    return _pcall(
        body, grid=(A_HEADS, nT),
        in_specs=[col(OFF_QA), col(OFF_FA), col(OFF_IA), col(OFF_GA),
                  pl.BlockSpec((1, ncl, A_HD, A_HD), lambda h, t: (h, nT - 1 - t, 0, 0)),
                  pl.BlockSpec((tt, A_HD), lambda h, t: (nT - 1 - t, h)), head_vec(2), head_vec(1)],
        out_specs=[o_spec, o_spec, o_spec, o_spec, p_spec, p_spec],
        out_shape=[o_shape, o_shape, o_shape, o_shape, p_shape, p_shape],
        scratch_shapes=[pltpu.VMEM((A_HD, A_HD), F32)], name="hgrn_bwd", semantics=("parallel", "arbitrary"),
        args=[proj, proj, proj, proj, st, dy, lb_logits, o_gain], job=job)


LANES = 128
Q_COLS = BW // LANES


def _low_half():
    return lax.broadcasted_iota(jnp.int32, (1, LANES), 1) < B_HD


def _half_sum(t, low):
    lo = jnp.sum(jnp.where(low, t, 0.0), axis=-1, keepdims=True)
    hi = jnp.sum(jnp.where(low, 0.0, t), axis=-1, keepdims=True)
    return jnp.where(low, lo, hi)


def _half_rms(t, low):
    r = lax.rsqrt(_half_sum(t * t, low) * (1.0 / B_HD) + EPS)
    return t * r, r


def _fold_halves(p, low):
    return jnp.where(low, p + pltpu.roll(p, B_HD, 1), 0.0)


def _stack_cols(x):
    return jnp.stack([x[:, c * LANES:(c + 1) * LANES] for c in range(Q_COLS)], axis=0).reshape(KV_HEADS, 2 * BLK, LANES)


def _col_of(t, c):
    return t[c // 2, (c % 2) * BLK:(c % 2 + 1) * BLK]


def _split_halves(col, s, low):
    own = jnp.where(low if s == 0 else jnp.logical_not(low), col, 0.0)
    other = pltpu.roll(own, B_HD, 1)
    return (own, other) if s == 0 else (other, own)


def _swa_keys(kp_ref, kc_ref, vp_ref, vc_ref, kg, low):
    k_lo, k_hi, v_lo, v_hi, hats = [], [], [], [], []
    for j in range(KVW // LANES):
        cs = slice(j * LANES, (j + 1) * LANES)
        k_hat, k_r = _half_rms(jnp.concatenate([kp_ref[:, cs], kc_ref[:, cs]], axis=0), low)
        vcol = jnp.concatenate([vp_ref[:, cs], vc_ref[:, cs]], axis=0)
        hats.append((k_hat, k_r))
        for s in range(2):
            for dst_lo, dst_hi, col in ((k_lo, k_hi, k_hat * kg), (v_lo, v_hi, vcol)):
                lo, hi = _split_halves(col, s, low)
                dst_lo.append(lo)
                dst_hi.append(hi)
    st = lambda parts: jnp.stack(parts, axis=0)
    return st(k_lo), st(k_hi), st(v_lo), st(v_hi), hats


def _swa_mask(first_block):
    qi = lax.broadcasted_iota(jnp.int32, (BLK, 2 * BLK), 0) + BLK
    ki = lax.broadcasted_iota(jnp.int32, (BLK, 2 * BLK), 1)
    rel = qi - ki
    m = (rel >= 0) & (rel < BLK) & (jnp.logical_not(first_block) | (ki >= BLK))
    return jnp.concatenate([m, m], axis=0)


def _sink_cols(sk_ref, hi):
    top = lax.broadcasted_iota(jnp.int32, (2 * BLK, 1), 0) < BLK
    return jnp.stack([jnp.where(top, sk_ref[0, GROUP * hk + hi], sk_ref[0, GROUP * hk + 2 + hi])
                      for hk in range(KV_HEADS)], axis=0)


def _swa_probs(qn, k_half, sink, mask):
    s = jnp.where(mask, _dot(qn, k_half, BNT) * (B_HD ** -0.5), NEG)
    m = jnp.maximum(jnp.max(s, axis=-1, keepdims=True), sink)
    p = jnp.exp(s - m)
    ps = jnp.exp(sink - m)
    inv = 1.0 / (jnp.sum(p, axis=-1, keepdims=True) + ps)
    return p * inv, ps * inv


def _swa_fwd(proj, q_gain, k_gain, sinks, job=None):
    T = proj.shape[0]
    nb = T // BLK

    def body(q_ref, kc_ref, kp_ref, vc_ref, vp_ref, qg_ref, kg_ref, sk_ref, o_ref):
        low = _low_half()
        mask = _swa_mask(pl.program_id(0) == 0)
        qn = _half_rms(_stack_cols(q_ref[...]), low)[0] * qg_ref[...]
        k_lo, k_hi, v_lo, v_hi, _ = _swa_keys(kp_ref, kc_ref, vp_ref, vc_ref, kg_ref[...], low)
        p_lo, _ = _swa_probs(qn, k_lo, _sink_cols(sk_ref, 0), mask)
        p_hi, _ = _swa_probs(qn, k_hi, _sink_cols(sk_ref, 1), mask)
        o = (_dot(p_lo, v_lo, BNN) + _dot(p_hi, v_hi, BNN)).astype(BF16)
        for c in range(Q_COLS):
            o_ref[:, c * LANES:(c + 1) * LANES] = _col_of(o, c)

    q_gain, k_gain = jnp.tile(q_gain, (1, 2)), jnp.tile(k_gain, (1, 2))
    cur = lambda w, off: pl.BlockSpec((BLK, w), lambda i: (i, off // w))
    prev = lambda w, off: pl.BlockSpec((BLK, w), lambda i: (jnp.maximum(i - 1, 0), off // w))
    small = lambda n: pl.BlockSpec((1, 2 * n), lambda i: (0, 0))
    return _pcall(
        body, grid=(nb,),
        in_specs=[cur(BW, OFF_QB), cur(KVW, OFF_KB), prev(KVW, OFF_KB), cur(KVW, OFF_VB), prev(KVW, OFF_VB),
                  small(B_HD), small(B_HD), pl.BlockSpec(memory_space=pltpu.SMEM)],
        out_specs=[pl.BlockSpec((BLK, BW), lambda i: (i, 0))],
        out_shape=[jax.ShapeDtypeStruct((T, BW), BF16)], scratch_shapes=[], name="swa_fwd", semantics=("parallel",),
        args=[proj, proj, proj, proj, proj, q_gain, k_gain, sinks], job=job)


def _swa_bwd(proj, dout, q_gain, k_gain, sinks, job=None):
    T = proj.shape[0]
    nb = T // BLK
    W = BW + 2 * KVW

    def body(q_ref, kc_ref, kp_ref, vc_ref, vp_ref, do_ref, qg_ref, kg_ref, sk_ref,
             dq_ref, dkv_ref, pqg_ref, pkg_ref, psk_ref, dkn_c, dv_c):
        i = pl.program_id(0)
        live = i < nb
        low = _low_half()
        high = jnp.logical_not(low)
        qg, kg = qg_ref[...], kg_ref[...]
        mask = _swa_mask(i == 0)
        lane = lax.broadcasted_iota(jnp.int32, (1, LANES), 1)
        scale = B_HD ** -0.5

        @pl.when(i == 0)
        def _():
            dkn_c[...] = jnp.zeros_like(dkn_c)
            dv_c[...] = jnp.zeros_like(dv_c)

        q_hat, q_r = _half_rms(_stack_cols(q_ref[...]), low)
        qn = q_hat * qg
        k_lo, k_hi, v_lo, v_hi, hats = _swa_keys(kp_ref, kc_ref, vp_ref, vc_ref, kg, low)
        do = _stack_cols(do_ref[...])
        dqn = jnp.zeros((KV_HEADS, 2 * BLK, LANES), F32)
        acc_sk = jnp.zeros((1, LANES), F32)
        dk_parts, dv_parts = [], []
        for hi, (k_h, v_h) in enumerate(((k_lo, v_lo), (k_hi, v_hi))):
            p, ps = _swa_probs(qn, k_h, _sink_cols(sk_ref, hi), mask)
            dp = _dot(do, v_h, BNT)
            delta = jnp.sum(p * dp, axis=-1, keepdims=True)
            ds = p * (dp - delta) * scale
            dqn = dqn + _dot(ds, k_h, BNN)
            dk_parts.append(_dot(ds, qn, BTN))
            dv_parts.append(_dot(p, do, BTN))
            t = ps * delta
            for hk in range(KV_HEADS):
                for rows in range(2):
                    h = GROUP * hk + 2 * rows + hi
                    acc_sk = acc_sk + jnp.where(
                        lane == h, -jnp.sum(t[hk, rows * BLK:(rows + 1) * BLK], axis=0, keepdims=True), 0.0)
        dqh = dqn * qg
        dq = (q_r * (dqh - q_hat * (_half_sum(dqh * q_hat, low) * (1.0 / B_HD)))).astype(BF16)
        for c in range(Q_COLS):
            dq_ref[:, c * LANES:(c + 1) * LANES] = _col_of(dq, c)
        acc_qg = _fold_halves(_fold8((dqn * q_hat).reshape(KV_HEADS * 2 * BLK, LANES)), low)

        def native(parts, j):
            lo_arr, hi_arr = parts
            a, b = 2 * j, 2 * j + 1
            return (jnp.where(low, lo_arr[a], 0.0) + pltpu.roll(jnp.where(high, hi_arr[a], 0.0), B_HD, 1)
                    + jnp.where(high, hi_arr[b], 0.0) + pltpu.roll(jnp.where(low, lo_arr[b], 0.0), B_HD, 1))

        acc_kg = jnp.zeros((8, LANES), F32)
        for j in range(KVW // LANES):
            cs = slice(j * LANES, (j + 1) * LANES)
            dkn = jnp.where(live, native(dk_parts, j), 0.0)
            dvc = jnp.where(live, native(dv_parts, j), 0.0)
            kp_hat, kp_r = hats[j][0][:BLK], hats[j][1][:BLK]
            dkn_prev = dkn_c[:, cs] + dkn[:BLK]
            dv_prev = dv_c[:, cs] + dvc[:BLK]
            acc_kg = acc_kg + _fold8(dkn_prev * kp_hat)
            dkh = dkn_prev * kg
            dkv_ref[:, cs] = (kp_r * (dkh - kp_hat * (_half_sum(dkh * kp_hat, low) * (1.0 / B_HD)))).astype(BF16)
            dkv_ref[:, KVW + j * LANES:KVW + (j + 1) * LANES] = dv_prev.astype(BF16)
            dkn_c[:, cs] = dkn[BLK:]
            dv_c[:, cs] = dvc[BLK:]
        keep = jnp.where(i > 0, 1.0, 0.0)
        pqg_ref[...] = jnp.where(live, acc_qg, 0.0)
        pkg_ref[...] = _fold_halves(acc_kg, low) * keep
        psk_ref[...] = jnp.broadcast_to(jnp.where(live, acc_sk, 0.0), (8, LANES)) * (
            lax.broadcasted_iota(jnp.int32, (8, LANES), 0) == 0).astype(F32)

    q_gain, k_gain = jnp.tile(q_gain, (1, 2)), jnp.tile(k_gain, (1, 2))
    last = nb - 1
    cur = lambda w, off: pl.BlockSpec((BLK, w), lambda i: (jnp.minimum(i, last), off // w))
    prev = lambda w, off: pl.BlockSpec((BLK, w), lambda i: (jnp.maximum(i - 1, 0), off // w))
    small = lambda n: pl.BlockSpec((1, 2 * n), lambda i: (0, 0))
    part = pl.BlockSpec((8, 128), lambda i: (i, 0))
    p_shape = jax.ShapeDtypeStruct(((nb + 1) * 8, 128), F32)
    return _pcall(
        body, grid=(nb + 1,),
        in_specs=[cur(BW, OFF_QB), cur(KVW, OFF_KB), prev(KVW, OFF_KB), cur(KVW, OFF_VB), prev(KVW, OFF_VB),
                  pl.BlockSpec((BLK, BW), lambda i: (jnp.minimum(i, last), 0)), small(B_HD), small(B_HD),
                  pl.BlockSpec(memory_space=pltpu.SMEM)],
        out_specs=[pl.BlockSpec((BLK, BW), lambda i: (i, 0)),
                   pl.BlockSpec((BLK, 2 * KVW), lambda i: (jnp.maximum(i - 1, 0), 0)), part, part, part],
        out_shape=[jax.ShapeDtypeStruct((T + BLK, BW), BF16), jax.ShapeDtypeStruct((T, 2 * KVW), BF16),
                   p_shape, p_shape, p_shape],
        scratch_shapes=[pltpu.VMEM((BLK, KVW), F32), pltpu.VMEM((BLK, KVW), F32)], name="swa_bwd",
        semantics=("arbitrary",), args=[proj, proj, proj, proj, proj, dout, q_gain, k_gain, sinks], job=job)


def _branch_merge(ya_pre, attn, wa_t, wb_t, proj, tm, tn, job=None):
    T = ya_pre.shape[0]

    def body(a_ref, b_ref, wa_ref, wb_ref, ga_ref, gb_ref, ya_ref, yb_ref, mg_ref):
        ya = lax.dot_general(a_ref[...], wa_ref[...], NT, preferred_element_type=F32)
        yb = lax.dot_general(b_ref[...], wb_ref[...], NT, preferred_element_type=F32)
        ya_ref[...] = ya.astype(BF16)
        yb_ref[...] = yb.astype(BF16)
        mg_ref[...] = (_sigmoid(ga_ref[...]) * ya + _sigmoid(gb_ref[...]) * yb).astype(BF16)

    o_spec = pl.BlockSpec((tm, tn), lambda i, j: (i, j))
    o_shape = jax.ShapeDtypeStruct((T, D), BF16)
    return _pcall(
        body, grid=(T // tm, D // tn),
        in_specs=[pl.BlockSpec((tm, AW), lambda i, j: (i, 0)), pl.BlockSpec((tm, BW), lambda i, j: (i, 0)),
                  pl.BlockSpec((tn, AW), lambda i, j: (j, 0)), pl.BlockSpec((tn, BW), lambda i, j: (j, 0)),
                  pl.BlockSpec((tm, tn), lambda i, j: (i, OFF_GTA // tn + j)),
                  pl.BlockSpec((tm, tn), lambda i, j: (i, OFF_GTB // tn + j))],
        out_specs=[o_spec, o_spec, o_spec], out_shape=[o_shape, o_shape, o_shape], scratch_shapes=[], name="branch_merge",
        semantics=("parallel", "parallel"), args=[ya_pre, attn, wa_t, wb_t, proj, proj], job=job)


def _ij(i, j, k):
    return (i, j)


def _local_step(x, tgt, mod, g1, g2, lbl, og, qg, kg, sk, shards, me, c_arr):
    win_s, wa_s, wb_s, wout_s, wmi_s, wmo_s = shards
    T = x.shape[0]
    tm, tr, tt = min(1024, T), min(256, T), min(512, T)
    tk_t = min(1024, T)
    tn = 512
    sh1, sc1, gt1, sh2, sc2, gt2 = (mod[:, i * D:(i + 1) * D] for i in range(N_MOD))
    nI = T // tm
    blk = (tm, tn)
    part = lambda: ((nI * 8, D), F32, (8, tn), _ij)
    vec_j = ((1, tn), lambda i, j, k: (0, j))

    h = _rms_mod_fwd("rms1_fwd", x, g1, sc1, sh1, tr)

    def epi_store(acc, ex, ou):
        ou[0][...] = acc.astype(ou[0].dtype)

    tm2 = min(2048, T)
    blk2 = (tm2, tn)

    full = lambda s: (0, s.shape[0])
    last = wmi_s.shape[0]
    (win_t,) = _run_job("gather_w_in", _gather_relay_job([win_s]))
    (proj,), (wa_t, wb_t, w_out, wmi_part) = _mm(
        "in_proj", "nt", [(h, D)], win_t, T, IN_W, D, tm2, tn, D, [], [((T, IN_W), F32, blk2, _ij)], epi_store,
        job=_gather_job([wa_s, wb_s, wout_s, wmi_s], rows=[full(wa_s), full(wb_s), full(wout_s), (0, MI_CUTS[0])]))
    (ya_pre, st), (wmi_part,) = _hgrn_fwd(
        proj, lbl, og, tt, job=_gather_job([wmi_s], rows=[MI_CUTS], into=[wmi_part]))
    (attn,), (wmi_t, wmo_part) = _swa_fwd(
        proj, qg, kg, sk, job=_gather_job([wmi_s, wmo_s], rows=[(MI_CUTS[1], last), (0, MO_CUT)], into=[wmi_part, None]))
    (ya, yb, merged), _ = _branch_merge(ya_pre, attn, wa_t, wb_t, proj, tm, tn)

    def epi_res1(acc, ex, ou):
        x_ref, gt_ref = ex
        ou[0][...] = acc.astype(BF16)
        ou[1][...] = x_ref[...] + gt_ref[...] * acc

    mo, x1 = _mm("out_proj", "nn", [(merged, D)], w_out, T, D, D, tm, tn, D, [(x, blk, _ij), (gt1, *vec_j)],
                 [((T, D), BF16, blk, _ij), ((T, D), F32, blk, _ij)], epi_res1)
    h2 = _rms_mod_fwd("rms2_fwd", x1, g2, sc2, sh2, tr)

    def epi_relu2(acc, ex, ou):
        r = jnp.maximum(acc, 0.0)
        ou[0][...] = r.astype(BF16)
        ou[1][...] = (r * r).astype(BF16)

    (r, a), (w_mo,) = _mm("mlp_in", "nt", [(h2, D)], wmi_t, T, HID, D, tm2, tn, D, [],
                          [((T, HID), BF16, blk2, _ij), ((T, HID), BF16, blk2, _ij)], epi_relu2,
                          job=_gather_job([wmo_s], rows=[(MO_CUT, last)], into=[wmo_part]))

    def epi_loss(acc, ex, ou):
        x1_ref, t_ref, gt_ref = ex
        e = x1_ref[...] + gt_ref[...] * acc - t_ref[...]
        dy = e * (1.0 / D)
        ou[0][...] = dy
        ou[1][...] = (gt_ref[...] * dy).astype(BF16)
        ou[2][...] = _fold8(e * e) * (0.5 / D)
        ou[3][...] = _fold8(dy * acc)

    wide = (tm, 1024)
    part_w = ((nI * 8, D), F32, (8, 1024), _ij)
    dy, dz, p_loss, p_gt2 = _mm(
        "mlp_out", "nn", [(a, HID)], w_mo, T, D, HID, tm, 1024, 1024,
        [(x1, wide, _ij), (tgt, wide, _ij), (gt2, (1, 1024), lambda i, j, k: (0, j))],
        [((T, D), F32, wide, _ij), ((T, D), BF16, wide, _ij), part_w, part_w], epi_loss)

    def epi_du(acc, ex, ou):
        ou[0][...] = (acc * (2.0 * ex[0][...].astype(F32))).astype(BF16)

    (du,) = _mm("mlp_out_dx", "nt", [(dz, D)], w_mo, T, HID, D, tm2, tn, D, [(r, blk2, _ij)],
                [((T, HID), BF16, blk2, _ij)], epi_du)
    gblk = (1024, 1024)
    gwide = (1024, D)
    pair_sum = lambda nm, g, r1: _pair_sum("pair_sum_" + nm, g, r1, c_arr, _sum_rows(r1.shape[1]))
    (g_mo,) = _mm("mlp_out_dw", "tn", [(a, HID)], dz, HID, D, T, 1024, D, tk_t, [], [((HID, D), BF16, gwide, _ij)], epi_store)
    (dx1, dmo, p_sh2, p_sc2, p_g2, p_gt1), (r1_mo,) = _pieces_nn_rms(
        "mlp_in_dx", [(du, HID)], wmi_t, x1, g2, sc2, dy, min(512, T), 1024, gate=gt1, mo=mo, job=_pair_job([g_mo]))
    s_mo = pair_sum("mlp_out", g_mo, r1_mo)
    near, far = (1, 2), (3,)
    (g_mi,), (rn_mo,) = _mm("mlp_in_dw", "tn", [(du, HID)], h2, HID, D, T, 1024, D, tk_t, [],
                            [((HID, D), BF16, gwide, _ij)], epi_store, job=_chip_job([s_mo], near))

    def epi_gates(acc, ex, ou):
        ya_ref, yb_ref, ga_ref, gb_ref = ex
        sa, sb = _sigmoid(ga_ref[...]), _sigmoid(gb_ref[...])
        ou[0][...] = (acc * sa).astype(BF16)
        ou[1][...] = (acc * sb).astype(BF16)
        ou[2][...] = (acc * ya_ref[...].astype(F32) * (sa * (1.0 - sa))).astype(BF16)
        ou[3][...] = (acc * yb_ref[...].astype(F32) * (sb * (1.0 - sb))).astype(BF16)

    o_bf = ((T, D), BF16, blk, _ij)
    (dya, dyb, dga, dgb), (rf_mo, r1_mi) = _mm(
        "out_proj_dx", "nt", [(dmo, D)], w_out, T, D, D, tm, tn, D,
        [(ya, blk, _ij), (yb, blk, _ij), (proj, blk, lambda i, j, k: (i, OFF_GTA // tn + j)),
         (proj, blk, lambda i, j, k: (i, OFF_GTB // tn + j))], [o_bf, o_bf, o_bf, o_bf], epi_gates,
        job=_both(_chip_job([s_mo], far), _pair_job([g_mi])))
    s_mi = pair_sum("mlp_in", g_mi, r1_mi)
    (g_out,) = _mm("out_proj_dw", "tn", [(merged, D)], dmo, D, D, T, 1024, 1024, tk_t, [], [((D, D), BF16, gblk, _ij)], epi_store)
    (dya_pre,) = _mm("branch_a_dx", "nn", [(dya, D)], wa_t, T, AW, D, tm, tn, D, [], [((T, AW), F32, blk, _ij)], epi_store)
    (dattn,) = _mm("branch_b_dx", "nn", [(dyb, D)], wb_t, T, BW, D, tm, tn, D, [], [((T, BW), F32, blk, _ij)], epi_store)
    (g_a,) = _mm("branch_a_dw", "tn", [(dya, D)], ya_pre, D, AW, T, 1024, 1024, tk_t, [], [((D, AW), BF16, gblk, _ij)], epi_store)
    (g_b,) = _mm("branch_b_dw", "tn", [(dyb, D)], attn, D, BW, T, 1024, 1024, tk_t, [], [((D, BW), BF16, gblk, _ij)], epi_store)
    (dqa, dfa, dia, dgg, p_lb, p_og), (rn_mi, r1_out, r1_a, r1_b) = _hgrn_bwd(
        proj, st, dya_pre, lbl, og, tt, job=_both(_chip_job([s_mi], near), _pair_job([g_out, g_a, g_b])))
    (dqb, dkv, p_qg, p_kg, p_sk), (rf_mi,) = _swa_bwd(proj, dattn, qg, kg, sk, job=_chip_job([s_mi], far))
    s_out, s_a, s_b = pair_sum("out", g_out, r1_out), pair_sum("branch_a", g_a, r1_a), pair_sum("branch_b", g_b, r1_b)
    pieces = [(dqa, AW), (dfa, AW), (dia, AW), (dgg, AW), (dqb, BW), (dkv, 2 * KVW), (dga, D), (dgb, D)]
    (g_in,), (r2_out, r2_a, r2_b) = _pieces_tn("in_proj_dw", pieces, h, 512, job=_chip_job([s_out, s_a, s_b]))
    (r1_in,) = _run_job("pair_w_in", _pair_job([g_in]))
    s_in = pair_sum("in", g_in, r1_in)
    (dx, p_sh1, p_sc1, p_g1), (r2_in,) = _pieces_nn_rms(
        "in_proj_dx", pieces, win_t, x, g1, sc1, dx1, tm, 512, job=_chip_job([s_in]))

    partials = dict(sh1=p_sh1, sc1=p_sc1, gt1=p_gt1, sh2=p_sh2, sc2=p_sc2, gt2=p_gt2, g1=p_g1, g2=p_g2,
                    lb=p_lb, og=p_og, qg=p_qg, kg=p_kg, sk=p_sk, loss=p_loss)
    sums = dict(w_in=(s_in, [r2_in]), w_branch_a=(s_a, [r2_a]), w_branch_b=(s_b, [r2_b]), w_out=(s_out, [r2_out]),
                w_mlp_in=(s_mi, [rn_mi, rf_mi]), w_mlp_out=(s_mo, [rn_mo, rf_mo]))
    return dx, sums, partials


def _exchange_slots(buf, send_sems, recv_sems):
    me = _mesh_pos()
    mine = buf.at[_index(me)]
    sends = []
    for k in range(1, N_DEV):
        cp = pltpu.make_async_remote_copy(src_ref=mine, dst_ref=mine, send_sem=send_sems.at[k - 1],
                                          recv_sem=recv_sems.at[k - 1], device_id=_flip(me, k), device_id_type=MESH)
        cp.start()
        sends.append(cp)
    for k in range(1, N_DEV):
        theirs = buf.at[_index(_flip(me, k))]
        pltpu.make_async_remote_copy(src_ref=theirs, dst_ref=theirs, send_sem=send_sems.at[k - 1],
                                     recv_sem=recv_sems.at[k - 1], device_id=_flip(me, k), device_id_type=MESH).wait_recv()
    for cp in sends:
        cp.wait_send()


ADA_W = N_MOD * D // N_DEV


def _ada_mod(c, w_ada, b_shard):
    def body(c_ref, w_ref, b_ref, mod_ref, sc_ref, cbuf, mbuf, s1, r1, s2, r2):
        me = _index(_mesh_pos())
        cbuf[me] = c_ref[...]
        _exchange_slots(cbuf, s1, r1)
        row = lax.broadcasted_iota(jnp.int32, (N_DEV, D), 0)
        call = jnp.zeros((N_DEV, D), F32)
        for d in range(N_DEV):
            call = jnp.where(row == d, cbuf[d], call)
        sc = call * _sigmoid(call)
        sc_ref[...] = sc
        mbuf[me] = _dot(sc, w_ref[...]) + b_ref[...]
        _exchange_slots(mbuf, s2, r2)
        for s in range(N_DEV):
            mod_ref[:, s * ADA_W:(s + 1) * ADA_W] = mbuf[s, pl.ds(me, 1), :]

    return pl.pallas_call(
        body, in_specs=[_VMEM, _VMEM, _VMEM], out_specs=[_VMEM, _VMEM],
        out_shape=[jax.ShapeDtypeStruct((1, N_MOD * D), F32), jax.ShapeDtypeStruct((N_DEV, D), F32)],
        scratch_shapes=[pltpu.VMEM((N_DEV, 1, D), F32), pltpu.VMEM((N_DEV, N_DEV, ADA_W), F32),
                        _SEMS(N_DEV - 1), _SEMS(N_DEV - 1), _SEMS(N_DEV - 1), _SEMS(N_DEV - 1)],
        name="ada_mod", compiler_params=pltpu.CompilerParams(vmem_limit_bytes=VMEM_LIMIT),
    )(c, w_ada, b_shard)


SMALL_SEGS = (("b_ada", N_MOD * D), ("norm1_gain", D), ("norm2_gain", D), ("lb0", AW), ("lb1", AW),
              ("hgrn_o_gain", AW), ("q_norm_gain", 128), ("k_norm_gain", 128), ("sinks", 128))
SMALL_W = sum(w for _, w in SMALL_SEGS)
X_SEGS = (("sh1", D), ("sc1", D), ("gt1", D), ("sh2", D), ("sc2", D), ("gt2", D), ("g1", D), ("g2", D),
          ("lb", AW), ("og", AW), ("qg", 128), ("kg", 128), ("sk", 128), ("loss", 128))
X_W = sum(w for _, w in X_SEGS)


def _offsets(segs):
    out, o = {}, 0
    for name, w in segs:
        out[name] = (o, w)
        o += w
    return out


def _small_reduce(parts, lb_logits):
    xo, so = _offsets(X_SEGS), _offsets(SMALL_SEGS)
    names = [nm for nm, _ in X_SEGS]

    def body(*refs):
        p_refs = dict(zip(names, refs[:len(names)]))
        lbl_ref, allx, gs_ref, loss_ref, send_sems, recv_sems = refs[len(names):]
        me = _index(_mesh_pos())
        for nm, (o, w) in xo.items():
            if nm == "loss":
                allx[me, :, o:o + w] = jnp.broadcast_to(jnp.sum(p_refs[nm][...]), (1, w))
            else:
                allx[me, :, o:o + w] = jnp.sum(p_refs[nm][...], axis=0, keepdims=True)
        _exchange_slots(allx, send_sems, recv_sems)
        tot = allx[0]
        for d in range(1, N_DEV):
            tot = tot + allx[d]
        seg = lambda nm: tot[:, xo[nm][0]:xo[nm][0] + xo[nm][1]]

        def put(nm, v):
            gs_ref[:, so[nm][0]:so[nm][0] + so[nm][1]] = v

        put("b_ada", tot[:, 0:N_MOD * D])
        put("norm1_gain", seg("g1"))
        put("norm2_gain", seg("g2"))
        lbl = lbl_ref[...]
        lb = _sigmoid(lbl[0:1, :] - lbl[1:2, :])
        dl0 = seg("lb") * lb * (1.0 - lb)
        put("lb0", dl0)
        put("lb1", -dl0)
        put("hgrn_o_gain", seg("og"))
        put("q_norm_gain", seg("qg"))
        put("k_norm_gain", seg("kg"))
        put("sinks", seg("sk"))
        loss_ref[...] = seg("loss")

    return pl.pallas_call(
        body, in_specs=[_VMEM] * (len(names) + 1), out_specs=[_VMEM, _VMEM, _VMEM],
        out_shape=[jax.ShapeDtypeStruct((N_DEV, 1, X_W), F32), jax.ShapeDtypeStruct((1, SMALL_W), F32),
                   jax.ShapeDtypeStruct((1, 128), F32)],
        scratch_shapes=[_SEMS(N_DEV - 1), _SEMS(N_DEV - 1)], name="small_reduce",
        compiler_params=pltpu.CompilerParams(vmem_limit_bytes=VMEM_LIMIT),
    )(*[parts[nm] for nm in names], lb_logits)


def _adamw_math(w, g, m, v):
    m = B1 * m + (1.0 - B1) * g
    v = B2 * v + (1.0 - B2) * (g * g)
    m_hat = m / (1.0 - B1 ** STEP)
    v_hat = v / (1.0 - B2 ** STEP)
    return -LR * (m_hat / (jnp.sqrt(v_hat) + ADAM_EPS) + WD * w), m, v


def _sum_rows(rs):
    return 256 if rs % 256 == 0 else rs // 2


def _pair_sum(name, g, recv, c_arr, tr):
    _, rs, cols = recv.shape
    blk = (1, tr, cols)

    def body(c_ref, g_ref, r_ref, o_ref):
        o_ref[...] = (g_ref[...].astype(F32) + r_ref[...].astype(F32)).astype(BF16)

    grid_spec = pltpu.PrefetchScalarGridSpec(
        num_scalar_prefetch=1, grid=(4, rs // tr),
        in_specs=[pl.BlockSpec(blk, lambda q, i, c: (2 * q + c[0], i, 0)), pl.BlockSpec(blk, lambda q, i, c: (q, i, 0))],
        out_specs=pl.BlockSpec(blk, lambda q, i, c: (q, i, 0)))
    return pl.pallas_call(body, grid_spec=grid_spec, out_shape=jax.ShapeDtypeStruct((4, rs, cols), BF16), name=name,
                          compiler_params=_params(("parallel", "parallel")))(c_arr, g.reshape(N_DEV, rs, cols), recv)


def _sum_adamw(name, sums, recvs, q_arr, w, m, v, transposed, tile):
    rows, cols = w.shape
    nR = len(recvs)

    def body(q_ref, s_ref, *refs):
        r_refs = refs[:nR]
        w_ref, m_ref, v_ref, g_ref, d_ref, nm_ref, nv_ref = refs[nR:]
        g = s_ref[0].astype(F32)
        for r_ref in r_refs:
            for slot in range(r_ref.shape[0]):
                g = g + r_ref[slot].astype(F32)
        g = g.T if transposed else g
        g_ref[...] = g
        d_ref[...], nm_ref[...], nv_ref[...] = _adamw_math(w_ref[...], g, m_ref[...], v_ref[...])

    if transposed:
        slab = lambda n, first: pl.BlockSpec((n, cols, tile), lambda i, q: (first(q), 0, i))
    else:
        slab = lambda n, first: pl.BlockSpec((n, tile, cols), lambda i, q: (first(q), i, 0))
    spec = pl.BlockSpec((tile, cols), lambda i, q: (i, 0))
    shape = jax.ShapeDtypeStruct((rows, cols), F32)
    grid_spec = pltpu.PrefetchScalarGridSpec(
        num_scalar_prefetch=1, grid=(rows // tile,),
        in_specs=[slab(1, lambda q: q[0])] + [slab(r.shape[0], lambda q: 0) for r in recvs] + [spec] * 3,
        out_specs=[spec] * 4)
    return pl.pallas_call(body, grid_spec=grid_spec, out_shape=[shape] * 4, name=name,
                          compiler_params=_params(("parallel",)))(q_arr, sums, *recvs, w, m, v)


def _adamw(name, w, g, m, v, tr):
    rows, cols = w.shape

    def body(w_ref, g_ref, m_ref, v_ref, d_ref, nm_ref, nv_ref):
        d_ref[...], nm_ref[...], nv_ref[...] = _adamw_math(w_ref[...], g_ref[...], m_ref[...], v_ref[...])

    spec = pl.BlockSpec((tr, cols), lambda i: (i, 0))
    shape = jax.ShapeDtypeStruct((rows, cols), F32)
    return pl.pallas_call(
        body, grid=(rows // tr,), in_specs=[spec] * 4, out_specs=[spec] * 3, out_shape=[shape] * 3, name=name,
        compiler_params=_params(("parallel",)),
    )(w, g, m, v)


def _ada_update(sc_t, dmod_cols, w, m, v, tr):
    rows, cols = w.shape

    def body(s_ref, d_ref, w_ref, m_ref, v_ref, g_ref, dl_ref, nm_ref, nv_ref):
        g = jnp.dot(s_ref[...], d_ref[...], precision=lax.Precision.HIGHEST, preferred_element_type=F32)
        g_ref[...] = g
        dl_ref[...], nm_ref[...], nv_ref[...] = _adamw_math(w_ref[...], g, m_ref[...], v_ref[...])

    spec = pl.BlockSpec((tr, cols), lambda i: (i, 0))
    shape = jax.ShapeDtypeStruct((rows, cols), F32)
    return pl.pallas_call(
        body, grid=(rows // tr,),
        in_specs=[pl.BlockSpec((tr, N_DEV), lambda i: (i, 0)), pl.BlockSpec((N_DEV, cols), lambda i: (0, 0)), spec, spec, spec],
        out_specs=[spec] * 4, out_shape=[shape] * 4, name="ada_update", compiler_params=_params(("parallel",)),
    )(sc_t, dmod_cols, w, m, v)


BIG = ("w_in", "w_branch_a", "w_branch_b", "w_out", "w_mlp_in", "w_mlp_out")
COLUMN_SHARDED = ("w_in", "w_branch_a", "w_branch_b", "w_mlp_in")
WEIGHTS = ("w_ada", "b_ada", "norm1_gain", "w_in", "lb_logits", "hgrn_o_gain", "q_norm_gain", "k_norm_gain", "sinks",
           "w_branch_a", "w_branch_b", "w_out", "norm2_gain", "w_mlp_in", "w_mlp_out")


def _pack_small(p):
    lb = p["lb_logits"]
    src = dict(p, lb0=lb[0:1], lb1=lb[1:2])
    return jnp.concatenate([jnp.pad(src[nm], ((0, 0), (0, w - src[nm].shape[1]))) for nm, w in SMALL_SEGS], axis=1)


def _unpack_small(vec, shapes):
    so = _offsets(SMALL_SEGS)
    out = {}
    for nm, shp in shapes.items():
        if nm == "lb_logits":
            o = so["lb0"][0]
            out[nm] = vec[0, o:o + 2 * AW].reshape(2, AW)
        else:
            o = so[nm][0]
            out[nm] = vec[:, o:o + shp[1]]
    return out


def kernel(x, c, w_ada, b_ada, norm1_gain, w_in, lb_logits, hgrn_o_gain, q_norm_gain, k_norm_gain, sinks, w_branch_a, w_branch_b, w_out, norm2_gain, w_mlp_in, w_mlp_out, loss_target, m_w_ada, m_b_ada, m_norm1_gain, m_w_in, m_lb_logits, m_hgrn_o_gain, m_q_norm_gain, m_k_norm_gain, m_sinks, m_w_branch_a, m_w_branch_b, m_w_out, m_norm2_gain, m_w_mlp_in, m_w_mlp_out, v_w_ada, v_b_ada, v_norm1_gain, v_w_in, v_lb_logits, v_hgrn_o_gain, v_q_norm_gain, v_k_norm_gain, v_sinks, v_w_branch_a, v_w_branch_b, v_w_out, v_norm2_gain, v_w_mlp_in, v_w_mlp_out):
    w = dict(w_ada=w_ada, b_ada=b_ada, norm1_gain=norm1_gain, w_in=w_in, lb_logits=lb_logits, hgrn_o_gain=hgrn_o_gain,
             q_norm_gain=q_norm_gain, k_norm_gain=k_norm_gain, sinks=sinks, w_branch_a=w_branch_a, w_branch_b=w_branch_b,
             w_out=w_out, norm2_gain=norm2_gain, w_mlp_in=w_mlp_in, w_mlp_out=w_mlp_out)
    m = dict(w_ada=m_w_ada, b_ada=m_b_ada, norm1_gain=m_norm1_gain, w_in=m_w_in, lb_logits=m_lb_logits,
             hgrn_o_gain=m_hgrn_o_gain, q_norm_gain=m_q_norm_gain, k_norm_gain=m_k_norm_gain, sinks=m_sinks,
             w_branch_a=m_w_branch_a, w_branch_b=m_w_branch_b, w_out=m_w_out, norm2_gain=m_norm2_gain,
             w_mlp_in=m_w_mlp_in, w_mlp_out=m_w_mlp_out)
    v = dict(w_ada=v_w_ada, b_ada=v_b_ada, norm1_gain=v_norm1_gain, w_in=v_w_in, lb_logits=v_lb_logits,
             hgrn_o_gain=v_hgrn_o_gain, q_norm_gain=v_q_norm_gain, k_norm_gain=v_k_norm_gain, sinks=v_sinks,
             w_branch_a=v_w_branch_a, w_branch_b=v_w_branch_b, w_out=v_w_out, norm2_gain=v_norm2_gain,
             w_mlp_in=v_w_mlp_in, w_mlp_out=v_w_mlp_out)
    for d in (w, m, v):
        for nm in ("w_ada",) + BIG:
            d[nm] = d[nm][0]
    px, py, pc = _mesh_pos()
    me = _index((px, py, pc))
    c_arr = jnp.reshape(pc, (1,)).astype(jnp.int32)
    q_arr = jnp.reshape(2 * px + py, (1,)).astype(jnp.int32)

    shards = [(w[nm].T if nm in COLUMN_SHARDED else w[nm]).astype(BF16) for nm in BIG]
    b_shard = lax.dynamic_slice(b_ada, (0, me * ADA_W), (1, ADA_W))
    mod, sc_all = _ada_mod(c, w["w_ada"], b_shard)

    dx, sums, parts = _local_step(x[0], loss_target[0], mod, norm1_gain, norm2_gain, lb_logits, hgrn_o_gain,
                                  q_norm_gain, k_norm_gain, sinks, shards, me, c_arr)

    allx, g_small, loss = _small_reduce(parts, lb_logits)

    grad, delta, new_m, new_v = {}, {}, {}, {}
    for nm in BIG:
        s, r2 = sums[nm]
        grad[nm], delta[nm], new_m[nm], new_v[nm] = _sum_adamw(
            "adamw_" + nm, s, r2, q_arr, w[nm], m[nm], v[nm], nm in COLUMN_SHARDED, 128)

    dmod_cols = lax.dynamic_slice(allx[:, 0, :], (0, me * ADA_W), (N_DEV, ADA_W))
    grad["w_ada"], delta["w_ada"], new_m["w_ada"], new_v["w_ada"] = _ada_update(
        sc_all.T, dmod_cols, w["w_ada"], m["w_ada"], v["w_ada"], 256)

    small_names = [nm for nm in WEIGHTS if nm not in BIG and nm != "w_ada"]
    shapes = {nm: w[nm].shape for nm in small_names}
    ds, ms, vs = _adamw("adamw_small", _pack_small(w), g_small, _pack_small(m), _pack_small(v), 1)
    for dst, vec in ((grad, g_small), (delta, ds), (new_m, ms), (new_v, vs)):
        dst.update(_unpack_small(vec, shapes))

    def full(d, nm):
        return d[nm][None] if nm in BIG or nm == "w_ada" else d[nm]

    return (loss[0, 0], dx[None], *[full(grad, nm) for nm in WEIGHTS], *[full(delta, nm) for nm in WEIGHTS],
            *[full(new_m, nm) for nm in WEIGHTS], *[full(new_v, nm) for nm in WEIGHTS])
```

```python
import functools

import jax
import jax.numpy as jnp
from jax import lax
from jax.experimental import pallas as pl
from jax.experimental.pallas import tpu as pltpu

F32 = jnp.float32
BF16 = jnp.bfloat16
MESH = pl.DeviceIdType.MESH

N_DEV = 8
D = 2048
A_HEADS, A_HD, CHUNK = 8, 128, 64
AW = A_HEADS * A_HD
Q_HEADS, KV_HEADS, GROUP, B_HD, BLK = 16, 4, 4, 64, 128
BW = Q_HEADS * B_HD
KVW = KV_HEADS * B_HD
HID = 4 * D
IN_W = 4 * AW + BW + 2 * KVW + 2 * D
OFF_QA, OFF_FA, OFF_IA, OFF_GA = 0, AW, 2 * AW, 3 * AW
OFF_QB = 4 * AW
OFF_KB = OFF_QB + BW
OFF_VB = OFF_KB + KVW
OFF_GTA = OFF_VB + KVW
OFF_GTB = OFF_GTA + D
N_MOD = 6
EPS = 1e-6
LR, B1, B2, ADAM_EPS, WD, STEP = 1e-3, 0.9, 0.999, 1e-8, 0.01, 10
NEG = -1e30

VMEM_LIMIT = 56 * 1024 * 1024
MI_CUTS = (512, 928)
MO_CUT = 336

NN = (((1,), (0,)), ((), ()))
NT = (((1,), (1,)), ((), ()))
TN = (((0,), (0,)), ((), ()))
BNN = (((2,), (1,)), ((0,), (0,)))
BNT = (((2,), (2,)), ((0,), (0,)))
BTN = (((1,), (1,)), ((0,), (0,)))


def _dot(a, b, dims=NN):
    return lax.dot_general(a.astype(BF16), b.astype(BF16), dims, preferred_element_type=F32)


def _params(sem):
    return pltpu.CompilerParams(dimension_semantics=sem, vmem_limit_bytes=VMEM_LIMIT)


def _sigmoid(x):
    return 1.0 / (1.0 + jnp.exp(-x))


def _fold8(v):
    r, n = v.shape
    return jnp.sum(v.reshape(r // 8, 8, n), axis=0)


_VMEM = pl.BlockSpec(memory_space=pltpu.VMEM)
_ANY = pl.BlockSpec(memory_space=pl.ANY)
_SEMS = lambda n: pltpu.SemaphoreType.DMA((n,))


def _mesh_pos():
    return lax.axis_index("x"), lax.axis_index("y"), lax.axis_index("c")


def _flip(pos, k):
    return tuple(1 - p if (k >> s) & 1 else p for p, s in zip(pos, (2, 1, 0)))


def _index(pos):
    return 4 * pos[0] + 2 * pos[1] + pos[2]


class _Job:
    def __init__(self, ins, out_shape, sems, start, finish, aliases=None):
        self.ins, self.out_shape, self.sems, self.start, self.finish = list(ins), list(out_shape), list(sems), start, finish
        self.aliases = dict(aliases or {})


def _both(j1, j2):
    assert not j1.aliases and not j2.aliases
    n_in, n_out, n_sem = len(j1.ins), len(j1.out_shape), len(j1.sems)
    first = lambda ins, outs, sems: (ins[:n_in], outs[:n_out], sems[:n_sem])
    second = lambda ins, outs, sems: (ins[n_in:], outs[n_out:], sems[n_sem:])

    def start(*refs):
        j1.start(*first(*refs))
        j2.start(*second(*refs))

    def finish(*refs):
        j1.finish(*first(*refs))
        j2.finish(*second(*refs))

    return _Job(j1.ins + j2.ins, j1.out_shape + j2.out_shape, j1.sems + j2.sems, start, finish)


def _pcall(body, *, grid, in_specs, out_specs, out_shape, scratch_shapes, name, semantics, args, job=None):
    if job is None:
        outs = pl.pallas_call(body, grid=grid, in_specs=in_specs, out_specs=out_specs, out_shape=out_shape,
                              scratch_shapes=scratch_shapes, name=name, compiler_params=_params(semantics))(*args)
        return list(outs), []
    n_in, n_out, n_scr = len(in_specs), len(out_specs), len(scratch_shapes)
    j_in, j_out = len(job.ins), len(job.out_shape)
    steps = tuple(grid)

    def carrier(*refs):
        o = 0
        main_in, o = refs[o:o + n_in], o + n_in
        job_in, o = refs[o:o + j_in], o + j_in
        main_out, o = refs[o:o + n_out], o + n_out
        job_out, o = refs[o:o + j_out], o + j_out
        main_scr, job_sems = refs[o:o + n_scr], refs[o + n_scr:]
        ids = [pl.program_id(a) for a in range(len(steps))]
        first = functools.reduce(lambda p, q: p & q, [i == 0 for i in ids])
        last = functools.reduce(lambda p, q: p & q, [i == s - 1 for i, s in zip(ids, steps)])

        @pl.when(first)
        def _():
            job.start(job_in, job_out, job_sems)

        body(*main_in, *main_out, *main_scr)

        @pl.when(last)
        def _():
            job.finish(job_in, job_out, job_sems)

    outs = pl.pallas_call(
        carrier, grid=grid, in_specs=list(in_specs) + [_ANY] * j_in, out_specs=list(out_specs) + [_ANY] * j_out,
        out_shape=list(out_shape) + job.out_shape, scratch_shapes=list(scratch_shapes) + job.sems, name=name,
        input_output_aliases={n_in + i: n_out + o for i, o in job.aliases.items()},
        compiler_params=_params(("arbitrary",) * len(steps)),
    )(*args, *job.ins)
    return list(outs[:n_out]), list(outs[n_out:])


def _run_job(name, job):
    j_in, j_out = len(job.ins), len(job.out_shape)

    def body(*refs):
        ins, outs, sems = refs[:j_in], refs[j_in:j_in + j_out], refs[j_in + j_out:]
        job.start(ins, outs, sems)
        job.finish(ins, outs, sems)

    return list(pl.pallas_call(body, in_specs=[_ANY] * j_in, out_specs=[_ANY] * j_out, out_shape=job.out_shape,
                               scratch_shapes=job.sems, name=name,
                               input_output_aliases=job.aliases)(*job.ins))


def _gather_job(shards, rows=None, into=None):
    n = len(shards)
    rows = rows or [(0, s.shape[0]) for s in shards]
    into = into or [None] * n
    olds, aliases = [], {}
    for a, buf in enumerate(into):
        if buf is not None:
            aliases[n + len(olds)] = a
            olds.append(buf)

    def copies(ins, outs, sems):
        send_sems, recv_sems, local_sems = sems
        x, y, c = _mesh_pos()
        me, sib = (x, y, c), (x, y, 1 - c)
        chips = [(1 - x, y), (x, 1 - y), (1 - x, 1 - y)]

        def part(a, p):
            rs, (r0, r1) = shards[a].shape[0], rows[a]
            return outs[a].at[pl.ds(_index(p) * rs + r0, r1 - r0), :]

        own = lambda a: ins[a].at[pl.ds(rows[a][0], rows[a][1] - rows[a][0]), :]

        def copy(a, k, block, to, src=None):
            return pltpu.make_async_remote_copy(
                src_ref=part(a, block) if src is None else src, dst_ref=part(a, block),
                send_sem=send_sems.at[7 * a + k], recv_sem=recv_sems.at[7 * a + k], device_id=to, device_id_type=MESH)

        mine = [pltpu.make_async_copy(own(a), part(a, me), local_sems.at[a]) for a in range(n)]
        first = []
        for a in range(n):
            first.append(copy(a, 0, me, sib, src=own(a)))
            first += [copy(a, 1 + j, me, (*chip, c), src=own(a)) for j, chip in enumerate(chips)]
        return me, sib, c, chips, copy, mine, first

    def start(ins, outs, sems):
        *_, mine, first = copies(ins, outs, sems)
        for cp in mine + first:
            cp.start()

    def finish(ins, outs, sems):
        me, sib, c, chips, copy, mine, first = copies(ins, outs, sems)
        passed = []
        for j, chip in enumerate(chips):
            for a in range(n):
                copy(a, 1 + j, (*chip, c), me).wait_recv()
                cp = copy(a, 4 + j, (*chip, c), sib)
                cp.start()
                passed.append(cp)
        for a in range(n):
            copy(a, 0, sib, me).wait_recv()
            for j, chip in enumerate(chips):
                copy(a, 4 + j, (*chip, 1 - c), me).wait_recv()
        for cp in first + passed:
            cp.wait_send()
        for cp in mine:
            cp.wait()

    return _Job(list(shards) + olds, [jax.ShapeDtypeStruct((N_DEV * s.shape[0], s.shape[1]), s.dtype) for s in shards],
                [_SEMS(7 * n), _SEMS(7 * n), _SEMS(n)], start, finish, aliases)


def _gather_relay_job(shards):
    n = len(shards)

    def tools(ins, outs, sems):
        send_sems, recv_sems, local_sems = sems
        x, y, c = _mesh_pos()
        q = 2 * x + y
        chip_at = lambda rel: (1 - x if rel & 2 else x, 1 - y if rel & 1 else y)

        def rows(a, chip, core):
            rs = shards[a].shape[0]
            return outs[a].at[pl.ds((2 * chip + core) * rs, rs), :]

        def copy(a, slot, chip, core, to, src=None):
            blk = rows(a, chip, core)
            return pltpu.make_async_remote_copy(src_ref=blk if src is None else src, dst_ref=blk,
                                                send_sem=send_sems.at[7 * a + slot], recv_sem=recv_sems.at[7 * a + slot],
                                                device_id=to, device_id_type=MESH)

        mine = [pltpu.make_async_copy(ins[a], rows(a, q, c), local_sems.at[a]) for a in range(n)]
        first = [copy(a, slot, q, c, (x, y, 1 - c) if slot == 0 else (*chip_at(slot), c), src=ins[a])
                 for a in range(n) for slot in (0, 1, 2)]
        return x, y, c, q, chip_at, copy, mine, first

    def start(ins, outs, sems):
        *_, mine, first = tools(ins, outs, sems)
        for cp in mine + first:
            cp.start()

    def finish(ins, outs, sems):
        x, y, c, q, chip_at, copy, mine, first = tools(ins, outs, sems)
        me, sib = (x, y, c), (x, y, 1 - c)

        def relay(src, dst):
            for a in range(n):
                copy(a, src, q ^ src, c, me).wait_recv()
                copy(a, 3, q ^ src, c, (*chip_at(dst), c)).start()
                copy(a, 3 + src, q ^ src, c, sib).start()
            for a in range(n):
                copy(a, dst, q ^ dst, c, me).wait_recv()
                copy(a, 3 + dst, q ^ dst, c, sib).start()

        pl.when(c == 1)(lambda: relay(1, 2))
        pl.when(c == 0)(lambda: relay(2, 1))
        for a in range(n):
            copy(a, 3, q ^ 3, c, me).wait_recv()
            copy(a, 6, q ^ 3, c, sib).start()
        for a in range(n):
            copy(a, 0, q, 1 - c, me).wait_recv()
            for rel in (1, 2, 3):
                copy(a, 3 + rel, q ^ rel, 1 - c, me).wait_recv()
        for a in range(n):
            for slot in range(3, 7):
                copy(a, slot, q, c, sib).wait_send()
        for cp in first:
            cp.wait_send()
        for cp in mine:
            cp.wait()

    return _Job(shards, [jax.ShapeDtypeStruct((N_DEV * s.shape[0], s.shape[1]), s.dtype) for s in shards],
                [_SEMS(7 * n), _SEMS(7 * n), _SEMS(n)], start, finish)


def _pair_job(grads):
    n = len(grads)

    def copies(ins, outs, sems):
        send_sems, recv_sems = sems
        x, y, c = _mesh_pos()
        out = []
        for a in range(n):
            rs = grads[a].shape[0] // N_DEV
            for q in range(4):
                blk = ins[a].at[pl.ds((2 * q + 1 - c) * rs, rs), :]
                out.append(pltpu.make_async_remote_copy(
                    src_ref=blk, dst_ref=outs[a].at[q], send_sem=send_sems.at[4 * a + q], recv_sem=recv_sems.at[4 * a + q],
                    device_id=(x, y, 1 - c), device_id_type=MESH))
        return out

    def start(ins, outs, sems):
        for cp in copies(ins, outs, sems):
            cp.start()

    def finish(ins, outs, sems):
        for cp in copies(ins, outs, sems):
            cp.wait()

    return _Job(grads, [jax.ShapeDtypeStruct((4, g.shape[0] // N_DEV, g.shape[1]), g.dtype) for g in grads],
                [_SEMS(4 * n), _SEMS(4 * n)], start, finish)


def _chip_job(sums, rels=(1, 2, 3)):
    n, nr = len(sums), len(rels)

    def copies(ins, outs, sems):
        send_sems, recv_sems = sems
        x, y, c = _mesh_pos()
        out = []
        for a in range(n):
            for slot, r in enumerate(rels):
                px, py = (1 - x if r & 2 else x), (1 - y if r & 1 else y)
                out.append(pltpu.make_async_remote_copy(
                    src_ref=ins[a].at[2 * px + py], dst_ref=outs[a].at[slot], send_sem=send_sems.at[nr * a + slot],
                    recv_sem=recv_sems.at[nr * a + slot], device_id=(px, py, c), device_id_type=MESH))
        return out

    def start(ins, outs, sems):
        for cp in copies(ins, outs, sems):
            cp.start()

    def finish(ins, outs, sems):
        for cp in copies(ins, outs, sems):
            cp.wait()

    return _Job(sums, [jax.ShapeDtypeStruct((nr,) + s.shape[1:], s.dtype) for s in sums],
                [_SEMS(nr * n), _SEMS(nr * n)], start, finish)


def _mm(name, form, a_list, b, M, N, K, tm, tn, tk, extras, outs, epi, job=None, acc_as_ref=False):
    nI, nJ, nK = M // tm, N // tn, K // tk
    assert nI * tm == M and nJ * tn == N and nK * tk == K
    dims = {"nn": NN, "nt": NT, "tn": TN}[form]
    b_list = b if isinstance(b, list) else [(b, {"nn": N, "nt": K, "tn": N}[form])]
    nA, nB = len(a_list), len(b_list)
    assert nA == 1 or nB == 1
    assert nB == 1 or form in ("nn", "nt")
    AXIS = {"i": 0, "j": 1, "k": 2}
    a_axis, a_tile = ("i", tm) if form == "tn" else ("k", tk)
    b_axis, b_tile = ("k", tk) if form == "nt" else ("j", tn)

    def cut(pieces, tile, total):
        starts, s = [], 0
        for _, w in pieces:
            assert w % tile == 0
            starts.append(s // tile)
            s += w
        assert s == total
        return starts, [w // tile for _, w in pieces]

    a_st, a_cn = cut(a_list, a_tile, M if form == "tn" else K)
    b_st, b_cn = cut(b_list, b_tile, K if form == "nt" else N)

    def inside(idx, st, cn):
        return (idx >= st) & (idx < st + cn)

    def a_spec(p):
        st, cn = a_st[p], a_cn[p]
        if form == "tn":
            return pl.BlockSpec((tk, tm), lambda i, j, k: (jnp.where(inside(i, st, cn), k, 0), jnp.clip(i - st, 0, cn - 1)))
        return pl.BlockSpec((tm, tk), lambda i, j, k: (i, jnp.clip(k - st, 0, cn - 1)))

    def b_spec(p):
        st, cn = b_st[p], b_cn[p]
        if form == "nt":
            return pl.BlockSpec((tn, tk), lambda i, j, k: (j, jnp.clip(k - st, 0, cn - 1)))
        if nB == 1:
            return pl.BlockSpec((tk, tn), lambda i, j, k: (k, j))
        return pl.BlockSpec((tk, tn), lambda i, j, k: (jnp.where(inside(j, st, cn), k, 0), jnp.clip(j - st, 0, cn - 1)))

    in_specs = ([a_spec(p) for p in range(nA)] + [b_spec(p) for p in range(nB)]
                + [pl.BlockSpec(bs, im) for _, bs, im in extras])
    out_shape = [jax.ShapeDtypeStruct(s_, d_) for s_, d_, _, _ in outs]
    out_specs = [pl.BlockSpec(bs, im) for _, _, bs, im in outs]
    nE, nO = len(extras), len(outs)
    single = nA == 1 and nB == 1

    def body(*refs):
        a_refs, b_refs = refs[:nA], refs[nA:nA + nB]
        ex, ou = refs[nA + nB:nA + nB + nE], refs[nA + nB + nE:nA + nB + nE + nO]
        ids = [pl.program_id(a) for a in range(3)]

        def partial_of(p, q):
            return lax.dot_general(a_refs[p][...], b_refs[q][...], dims, preferred_element_type=F32)

        if nK == 1 and single:
            epi(partial_of(0, 0), ex, ou)
            return
        acc = refs[-1]
        k = ids[2]
        for p in range(nA):
            for q in range(nB):
                def first(p=p, q=q):
                    acc[...] = partial_of(p, q)

                def later(p=p, q=q):
                    acc[...] += partial_of(p, q)

                here = None
                if nA > 1:
                    here = inside(ids[AXIS[a_axis]], a_st[p], a_cn[p])
                if nB > 1:
                    here = inside(ids[AXIS[b_axis]], b_st[q], b_cn[q])
                pl.when(k == 0 if here is None else here & (k == 0))(first)
                pl.when(k > 0 if here is None else here & (k > 0))(later)

        @pl.when(k == nK - 1)
        def _():
            epi(acc if acc_as_ref else acc[...], ex, ou)

    scratch = [] if (nK == 1 and single) else [pltpu.VMEM((tm, tn), F32)]
    res, job_res = _pcall(
        body, grid=(nI, nJ, nK), in_specs=in_specs, out_specs=out_specs, out_shape=out_shape, scratch_shapes=scratch,
        name=name, semantics=("parallel", "parallel", "arbitrary"),
        args=[a for a, _ in a_list] + [p for p, _ in b_list] + [e for e, _, _ in extras], job=job)
    return res if job is None else (res, job_res)


def _piece_tiles(pieces, tile):
    starts, s = [], 0
    for _, w in pieces:
        assert w % tile == 0
        starts.append(s // tile)
        s += w
    return starts, [w // tile for _, w in pieces], s


def _pieces_tn(name, pieces, b, tile, job=None):
    T, N = b.shape
    st, cn, M = _piece_tiles(pieces, tile)
    nP, nI = len(pieces), M // tile

    def body(*refs):
        p_refs, b_hbm, o_ref = refs[:nP], refs[nP], refs[nP + 1]
        bbuf, abuf, bsem, asem = refs[nP + 2:]
        i = pl.program_id(0)

        def fetch(step, slot):
            for p in range(nP):
                @pl.when((step >= st[p]) & (step < st[p] + cn[p]))
                def _():
                    col = pl.multiple_of((step - st[p]) * tile, tile)
                    pltpu.make_async_copy(p_refs[p].at[pl.ds(0, T), pl.ds(col, tile)], abuf.at[slot], asem.at[slot]).start()

        @pl.when(i == 0)
        def _():
            whole = pltpu.make_async_copy(b_hbm, bbuf, bsem)
            whole.start()
            fetch(0, 0)
            whole.wait()

        @pl.when(i + 1 < nI)
        def _():
            fetch(i + 1, (i + 1) % 2)

        pltpu.make_async_copy(p_refs[0].at[pl.ds(0, T), pl.ds(0, tile)], abuf.at[i % 2], asem.at[i % 2]).wait()
        o_ref[...] = lax.dot_general(abuf[i % 2], bbuf[...], TN, preferred_element_type=F32).astype(BF16)

    res, job_res = _pcall(
        body, grid=(nI,), in_specs=[_ANY] * (nP + 1), out_specs=[pl.BlockSpec((tile, N), lambda i: (i, 0))],
        out_shape=[jax.ShapeDtypeStruct((M, N), BF16)],
        scratch_shapes=[pltpu.VMEM((T, N), b.dtype), pltpu.VMEM((2, T, tile), b.dtype), pltpu.SemaphoreType.DMA, _SEMS(2)],
        name=name, semantics=("arbitrary",), args=[p for p, _ in pieces] + [b], job=job)
    return res if job is None else (res, job_res)


def _pieces_nn_rms(name, pieces, w, x, gain, sc, dres, tm, tk, gate=None, mo=None, job=None):
    T = x.shape[0]
    with_gate = gate is not None
    st, cn, K = _piece_tiles(pieces, tk)
    nP, nI, nK = len(pieces), T // tm, K // tk
    _, outs, epi = _rms_mod_bwd_epilogue(x, gain, sc, dres, tm, gate, mo)
    parts = [outs[1], outs[2], outs[3]] + ([outs[5]] if with_gate else [])
    part_specs = [pl.BlockSpec(bs, lambda i, k, im=im: im(i, 0, k)) for _, _, bs, im in parts]
    n_vec, n_any_in, n_any_out = (3 if with_gate else 2), (3 if with_gate else 2), (2 if with_gate else 1)

    def body(*refs):
        o = nP
        p_refs, w_ref = refs[:nP], refs[o]
        vecs = refs[o + 1:o + 1 + n_vec]
        ins = refs[o + 1 + n_vec:o + 1 + n_vec + n_any_in]
        o = o + 1 + n_vec + n_any_in
        hbm_outs, p_outs = refs[o:o + n_any_out], refs[o + n_any_out:o + n_any_out + len(parts)]
        o = o + n_any_out + len(parts)
        acc, abuf, xbuf, rbuf = refs[o:o + 4]
        mbuf = refs[o + 4] if with_gate else None
        asem, in_sems, out_sems = refs[-3:]
        i, k = pl.program_id(0), pl.program_id(1)
        g = i * nK + k
        rows_of = lambda ref, ii: ref.at[pl.ds(pl.multiple_of(ii * tm, tm), tm), :]
        bufs_in = [xbuf, rbuf] + ([mbuf] if with_gate else [])
        bufs_out = [rbuf] + ([mbuf] if with_gate else [])

        def fetch(ii, kk, slot):
            for p in range(nP):
                @pl.when((kk >= st[p]) & (kk < st[p] + cn[p]))
                def _():
                    col = pl.multiple_of((kk - st[p]) * tk, tk)
                    src = p_refs[p].at[pl.ds(pl.multiple_of(ii * tm, tm), tm), pl.ds(col, tk)]
                    pltpu.make_async_copy(src, abuf.at[slot], asem.at[slot]).start()

        loads = lambda ii: [pltpu.make_async_copy(rows_of(src, ii), buf, in_sems.at[n])
                            for n, (src, buf) in enumerate(zip(ins, bufs_in))]
        stores = lambda ii: [pltpu.make_async_copy(buf, rows_of(dst, ii), out_sems.at[n])
                             for n, (buf, dst) in enumerate(zip(bufs_out, hbm_outs))]

        @pl.when(g == 0)
        def _():
            fetch(0, 0, 0)

        @pl.when(g + 1 < nI * nK)
        def _():
            last_k = k == nK - 1
            fetch(jnp.where(last_k, i + 1, i), jnp.where(last_k, 0, k + 1), (g + 1) % 2)

        @pl.when(k == 0)
        def _():
            @pl.when(i > 0)
            def _():
                for cp in stores(i - 1):
                    cp.wait()
            for cp in loads(i):
                cp.start()

        pltpu.make_async_copy(p_refs[0].at[pl.ds(0, tm), pl.ds(0, tk)], abuf.at[g % 2], asem.at[g % 2]).wait()

        def product(cols):
            return jnp.dot(abuf[g % 2], w_ref[:, cols], preferred_element_type=F32)

        col_blocks = [slice(c0, c0 + 512) for c0 in range(0, D, 512)]

        @pl.when(k == 0)
        def _():
            for cols in col_blocks:
                acc[:, cols] = product(cols)

        @pl.when(k > 0)
        def _():
            for cols in col_blocks:
                acc[:, cols] += product(cols)

        @pl.when(k == nK - 1)
        def _():
            for cp in loads(i):
                cp.wait()
            if with_gate:
                epi(acc, [xbuf, vecs[0], vecs[1], rbuf, vecs[2], mbuf], [rbuf, *p_outs[:3], mbuf, p_outs[3]])
            else:
                epi(acc, [xbuf, vecs[0], vecs[1], rbuf], [rbuf, *p_outs])
            for cp in stores(i):
                cp.start()

            @pl.when(i == nI - 1)
            def _():
                for cp in stores(i):
                    cp.wait()

    vec = pl.BlockSpec((1, D), lambda i, k: (0, 0))
    scratch = [pltpu.VMEM((tm, D), F32), pltpu.VMEM((2, tm, tk), BF16), pltpu.VMEM((tm, D), F32), pltpu.VMEM((tm, D), F32)]
    scratch += ([pltpu.VMEM((tm, D), BF16)] if with_gate else []) + [_SEMS(2), _SEMS(n_any_in), _SEMS(n_any_out)]
    res, job_res = _pcall(
        body, grid=(nI, nK),
        in_specs=[_ANY] * nP + [pl.BlockSpec((tk, D), lambda i, k: (k, 0))] + [vec] * n_vec + [_ANY] * n_any_in,
        out_specs=[_ANY] * n_any_out + part_specs,
        out_shape=([jax.ShapeDtypeStruct((T, D), F32)] + ([jax.ShapeDtypeStruct((T, D), BF16)] if with_gate else [])
                   + [jax.ShapeDtypeStruct(s, d) for s, d, _, _ in parts]),
        scratch_shapes=scratch, name=name, semantics=("arbitrary", "arbitrary"),
        args=([p for p, _ in pieces] + [w, gain, sc] + ([gate] if with_gate else []) + [x, dres]
              + ([mo] if with_gate else [])), job=job)
    return res if job is None else (res, job_res)


def _rms_mod_fwd(name, x, gain, sc, sh, tr):
    T = x.shape[0]

    def body(x_ref, g_ref, sc_ref, sh_ref, h_ref):
        xv = x_ref[...]
        rstd = lax.rsqrt(jnp.mean(xv * xv, axis=-1, keepdims=True) + EPS)
        h_ref[...] = ((xv * rstd * g_ref[...]) * (1.0 + sc_ref[...]) + sh_ref[...]).astype(BF16)

    row = pl.BlockSpec((tr, D), lambda i: (i, 0))
    vec = pl.BlockSpec((1, D), lambda i: (0, 0))
    return pl.pallas_call(
        body, grid=(T // tr,), in_specs=[row, vec, vec, vec], out_specs=row,
        out_shape=jax.ShapeDtypeStruct((T, D), BF16), name=name, compiler_params=_params(("parallel",)),
    )(x, gain, sc, sh)


def _rms_mod_bwd_epilogue(x, gain, sc, dres, tm, gate=None, mo=None):
    T = x.shape[0]
    with_gate = gate is not None
    row = ((tm, D), lambda i, j, k: (i, 0))
    vec = ((1, D), lambda i, j, k: (0, 0))
    part = ((T // tm * 8, D), F32, (8, D), lambda i, j, k: (i, 0))
    extras = [(x, *row), (gain, *vec), (sc, *vec), (dres, *row)]
    outs = [((T, D), F32, *row), part, part, part]
    if with_gate:
        extras += [(gate, *vec), (mo, *row)]
        outs += [((T, D), BF16, *row), part]

    rows = min(64, tm)

    def epi(acc, ex, ou):
        g = ex[1][...]
        sums = [jnp.zeros((8, D), F32) for _ in range(4)]
        for r0 in range(0, tm, rows):
            rs = slice(r0, r0 + rows)
            dhv, xv = acc[rs, :], ex[0][rs, :]
            rstd = lax.rsqrt(jnp.mean(xv * xv, axis=-1, keepdims=True) + EPS)
            xhat = xv * rstd
            dn = dhv * (1.0 + ex[2][...])
            dxhat = dn * g
            dx = ex[3][rs, :] + rstd * (dxhat - xhat * jnp.mean(dxhat * xhat, axis=-1, keepdims=True))
            ou[0][rs, :] = dx
            terms = [dhv, dhv * (xhat * g), dn * xhat]
            if with_gate:
                terms.append(dx * ex[5][rs, :].astype(F32))
                ou[4][rs, :] = (ex[4][...] * dx).astype(BF16)
            sums = [s + _fold8(t) for s, t in zip(sums, terms)] + sums[len(terms):]
        ou[1][...], ou[2][...], ou[3][...] = sums[:3]
        if with_gate:
            ou[5][...] = sums[3]

    return extras, outs, epi


def _rms_mod_bwd(name, dh, x, gain, sc, dres, tr, gate=None, mo=None):
    T = x.shape[0]
    extras, outs, epi = _rms_mod_bwd_epilogue(x, gain, sc, dres, tr, gate, mo)
    rows_only = lambda im: (lambda i: im(i, 0, 0))
    nE = len(extras)

    def body(dh_ref, *refs):
        epi(dh_ref, refs[:nE], refs[nE:])

    return pl.pallas_call(
        body, grid=(T // tr,),
        in_specs=[pl.BlockSpec((tr, D), lambda i: (i, 0))] + [pl.BlockSpec(bs, rows_only(im)) for _, bs, im in extras],
        out_specs=[pl.BlockSpec(bs, rows_only(im)) for _, _, bs, im in outs],
        out_shape=[jax.ShapeDtypeStruct(s, d) for s, d, _, _ in outs], name=name, compiler_params=_params(("parallel",)),
    )(dh, *[e for e, _, _ in extras])


def _split3(v):
    h = v.astype(BF16)
    r1 = v - h.astype(F32)
    m = r1.astype(BF16)
    lo = (r1 - m.astype(F32)).astype(BF16)
    return h, m, lo


def _tri_mm(tri, v, dims=NN):
    h, m, lo = _split3(v)
    t = tri.astype(BF16)
    mm = lambda p: lax.dot_general(t, p, dims, preferred_element_type=F32)
    return (mm(lo) + mm(m)) + mm(h)


def _hgrn_chunk_terms(q, fl, lb):
    sig = _sigmoid(fl)
    f = lb + (1.0 - lb) * sig
    lf = jnp.log(f)
    kk = 1.0 - f
    sq = _sigmoid(q)
    qf = q * sq
    return sig, f, lf, kk, sq, qf


def _causal(n):
    r = lax.broadcasted_iota(jnp.int32, (n, n), 0)
    c = lax.broadcasted_iota(jnp.int32, (n, n), 1)
    return r >= c


def _hgrn_fwd(proj, lb_logits, o_gain, tt, job=None):
    T = proj.shape[0]
    nT, ncl = T // tt, tt // CHUNK
    C = CHUNK

    def body(q_ref, f_ref, i_ref, g_ref, lbl_ref, og_ref, y_ref, st_ref, S):
        @pl.when(pl.program_id(1) == 0)
        def _():
            S[...] = jnp.zeros_like(S)

        lbl = lbl_ref[...]
        lb = _sigmoid(lbl[0:1, :] - lbl[1:2, :])
        og = og_ref[...]
        shp = (ncl, C, A_HD)
        q, fl, v, g = (r[...].reshape(shp) for r in (q_ref, f_ref, i_ref, g_ref))
        tri = jnp.broadcast_to(_causal(C), (ncl, C, C))
        _, _, lf, kk, _, qf = _hgrn_chunk_terms(q, fl, lb)
        b = _tri_mm(tri, lf, BNN)
        bm, bl = b[:, C // 2 - 1:C // 2, :], b[:, C - 1:C, :]
        qd, kd = qf * jnp.exp(b - bm), kk * jnp.exp(bm - b)
        A = jnp.where(tri, _dot(qd, kd, BNT), 0.0)
        d_st = _dot(v, kk * jnp.exp(bl - b), BTN)
        dec = jnp.exp(bl)
        st = S[...]
        for ci in range(ncl):
            st_ref[0, ci] = st
            st = st * dec[ci] + d_st[ci]
        S[...] = st
        o = _dot(A, v, BNN) + _dot(qf * jnp.exp(b), st_ref[0], BNT)
        r = lax.rsqrt(jnp.mean(o * o, axis=-1, keepdims=True) + EPS)
        y_ref[...] = (o * r * og * (g * _sigmoid(g))).astype(BF16).reshape(tt, A_HD)

    def col(off):
        return pl.BlockSpec((tt, A_HD), lambda h, t: (t, off // A_HD + h))

    head_vec = lambda rows: pl.BlockSpec((rows, A_HD), lambda h, t: (0, h))
    return _pcall(
        body, grid=(A_HEADS, nT),
        in_specs=[col(OFF_QA), col(OFF_FA), col(OFF_IA), col(OFF_GA), head_vec(2), head_vec(1)],
        out_specs=[pl.BlockSpec((tt, A_HD), lambda h, t: (t, h)),
                   pl.BlockSpec((1, ncl, A_HD, A_HD), lambda h, t: (h, t, 0, 0))],
        out_shape=[jax.ShapeDtypeStruct((T, AW), BF16),
                   jax.ShapeDtypeStruct((A_HEADS, T // C, A_HD, A_HD), F32)],
        scratch_shapes=[pltpu.VMEM((A_HD, A_HD), F32)], name="hgrn_fwd", semantics=("parallel", "arbitrary"),
        args=[proj, proj, proj, proj, lb_logits, o_gain], job=job)


def _hgrn_bwd(proj, st, dy, lb_logits, o_gain, tt, job=None):
    T = proj.shape[0]
    nT, ncl = T // tt, tt // CHUNK
    C = CHUNK

    def body(q_ref, f_ref, i_ref, g_ref, st_ref, dy_ref, lbl_ref, og_ref,
             dq_ref, df_ref, di_ref, dg_ref, plb_ref, pog_ref, dS):
        @pl.when(pl.program_id(1) == 0)
        def _():
            dS[...] = jnp.zeros_like(dS)

        lbl = lbl_ref[...]
        lb = _sigmoid(lbl[0:1, :] - lbl[1:2, :])
        og = og_ref[...]
        shp = (ncl, C, A_HD)
        flat = lambda t: t.reshape(tt, A_HD)
        q, fl, v, g, dout = (r[...].reshape(shp) for r in (q_ref, f_ref, i_ref, g_ref, dy_ref))
        tri = jnp.broadcast_to(_causal(C), (ncl, C, C))
        rowi = lax.broadcasted_iota(jnp.int32, shp, 1)
        st0 = st_ref[0]
        sig, f, lf, kk, sq, qf = _hgrn_chunk_terms(q, fl, lb)
        b = _tri_mm(tri, lf, BNN)
        bm, bl = b[:, C // 2 - 1:C // 2, :], b[:, C - 1:C, :]
        e_qd, e_kd, e_ke, e_b = jnp.exp(b - bm), jnp.exp(bm - b), jnp.exp(bl - b), jnp.exp(b)
        qd, kd, ke, qe = qf * e_qd, kk * e_kd, kk * e_ke, qf * e_b
        dec = jnp.exp(bl)
        A = jnp.where(tri, _dot(qd, kd, BNT), 0.0)
        o = _dot(A, v, BNN) + _dot(qe, st0, BNT)
        r = lax.rsqrt(jnp.mean(o * o, axis=-1, keepdims=True) + EPS)
        sg = _sigmoid(g)
        on = o * r * og
        dg_ref[...] = flat((dout * on * (sg * (1.0 + g * (1.0 - sg)))).astype(BF16))
        don = dout * (g * sg)
        pog_ref[...] = _fold8(flat(don * o * r))
        dyh = don * og
        do = r * (dyh - o * (r * r) * jnp.mean(dyh * o, axis=-1, keepdims=True))
        g_st = _dot(do, qe, BTN)
        run = dS[...]
        after = [None] * ncl
        for ci in reversed(range(ncl)):
            after[ci] = run
            run = g_st[ci] + run * dec[ci]
        dS[...] = run
        d_after = jnp.stack(after, axis=0)
        ddec = jnp.sum(d_after * st0, axis=1, keepdims=True)
        dqe = _dot(do, st0, BNN)
        dke = _dot(v, d_after, BNN)
        dA = jnp.where(tri, _dot(do, v, BNT), 0.0)
        dv = _dot(ke, d_after, BNT) + _dot(A, do, BTN)
        dqd = _dot(dA, kd, BNN)
        dkd = _dot(dA, qd, BTN)
        di_ref[...] = flat(dv.astype(BF16))
        dqf = dqe * e_b + dqd * e_qd
        dkk = dkd * e_kd + dke * e_ke
        t_qd, t_kd, t_ke = dqd * qd, dkd * kd, dke * ke
        db = dqe * qe + t_qd - t_kd - t_ke
        dbm = jnp.sum(t_kd - t_qd, axis=1, keepdims=True)
        dbl = jnp.sum(t_ke, axis=1, keepdims=True) + ddec * dec
        db = db + jnp.where(rowi == C // 2 - 1, dbm, 0.0) + jnp.where(rowi == C - 1, dbl, 0.0)
        dlf = _tri_mm(tri, db, BTN)
        dfv = dlf / f - dkk
        df_ref[...] = flat((dfv * (1.0 - lb) * sig * (1.0 - sig)).astype(BF16))
        plb_ref[...] = _fold8(flat(dfv * (1.0 - sig)))
        dq_ref[...] = flat((dqf * (sq * (1.0 + q * (1.0 - sq)))).astype(BF16))

    def col(off):
        return pl.BlockSpec((tt, A_HD), lambda h, t: (nT - 1 - t, off // A_HD + h))

    head_vec = lambda rows: pl.BlockSpec((rows, A_HD), lambda h, t: (0, h))
    o_spec = pl.BlockSpec((tt, A_HD), lambda h, t: (nT - 1 - t, h))
    p_spec = pl.BlockSpec((8, A_HD), lambda h, t: (t, h))
    o_shape = jax.ShapeDtypeStruct((T, AW), BF16)
    p_shape = jax.ShapeDtypeStruct((nT * 8, AW), F32)
    return _pcall(
        body, grid=(A_HEADS, nT),
        in_specs=[col(OFF_QA), col(OFF_FA), col(OFF_IA), col(OFF_GA),
                  pl.BlockSpec((1, ncl, A_HD, A_HD), lambda h, t: (h, nT - 1 - t, 0, 0)),
                  pl.BlockSpec((tt, A_HD), lambda h, t: (nT - 1 - t, h)), head_vec(2), head_vec(1)],
        out_specs=[o_spec, o_spec, o_spec, o_spec, p_spec, p_spec],
        out_shape=[o_shape, o_shape, o_shape, o_shape, p_shape, p_shape],
        scratch_shapes=[pltpu.VMEM((A_HD, A_HD), F32)], name="hgrn_bwd", semantics=("parallel", "arbitrary"),
        args=[proj, proj, proj, proj, st, dy, lb_logits, o_gain], job=job)


LANES = 128
Q_COLS = BW // LANES


def _low_half():
    return lax.broadcasted_iota(jnp.int32, (1, LANES), 1) < B_HD


def _half_sum(t, low):
    lo = jnp.sum(jnp.where(low, t, 0.0), axis=-1, keepdims=True)
    hi = jnp.sum(jnp.where(low, 0.0, t), axis=-1, keepdims=True)
    return jnp.where(low, lo, hi)


def _half_rms(t, low):
    r = lax.rsqrt(_half_sum(t * t, low) * (1.0 / B_HD) + EPS)
    return t * r, r


def _fold_halves(p, low):
    return jnp.where(low, p + pltpu.roll(p, B_HD, 1), 0.0)


def _stack_cols(x):
    return jnp.stack([x[:, c * LANES:(c + 1) * LANES] for c in range(Q_COLS)], axis=0).reshape(KV_HEADS, 2 * BLK, LANES)


def _col_of(t, c):
    return t[c // 2, (c % 2) * BLK:(c % 2 + 1) * BLK]


def _split_halves(col, s, low):
    own = jnp.where(low if s == 0 else jnp.logical_not(low), col, 0.0)
    other = pltpu.roll(own, B_HD, 1)
    return (own, other) if s == 0 else (other, own)


def _swa_keys(kp_ref, kc_ref, vp_ref, vc_ref, kg, low):
    k_lo, k_hi, v_lo, v_hi, hats = [], [], [], [], []
    for j in range(KVW // LANES):
        cs = slice(j * LANES, (j + 1) * LANES)
        k_hat, k_r = _half_rms(jnp.concatenate([kp_ref[:, cs], kc_ref[:, cs]], axis=0), low)
        vcol = jnp.concatenate([vp_ref[:, cs], vc_ref[:, cs]], axis=0)
        hats.append((k_hat, k_r))
        for s in range(2):
            for dst_lo, dst_hi, col in ((k_lo, k_hi, k_hat * kg), (v_lo, v_hi, vcol)):
                lo, hi = _split_halves(col, s, low)
                dst_lo.append(lo)
                dst_hi.append(hi)
    st = lambda parts: jnp.stack(parts, axis=0)
    return st(k_lo), st(k_hi), st(v_lo), st(v_hi), hats


def _swa_mask(first_block):
    qi = lax.broadcasted_iota(jnp.int32, (BLK, 2 * BLK), 0) + BLK
    ki = lax.broadcasted_iota(jnp.int32, (BLK, 2 * BLK), 1)
    rel = qi - ki
    m = (rel >= 0) & (rel < BLK) & (jnp.logical_not(first_block) | (ki >= BLK))
    return jnp.concatenate([m, m], axis=0)


def _sink_cols(sk_ref, hi):
    top = lax.broadcasted_iota(jnp.int32, (2 * BLK, 1), 0) < BLK
    return jnp.stack([jnp.where(top, sk_ref[0, GROUP * hk + hi], sk_ref[0, GROUP * hk + 2 + hi])
                      for hk in range(KV_HEADS)], axis=0)


def _swa_probs(qn, k_half, sink, mask):
    s = jnp.where(mask, _dot(qn, k_half, BNT) * (B_HD ** -0.5), NEG)
    m = jnp.maximum(jnp.max(s, axis=-1, keepdims=True), sink)
    p = jnp.exp(s - m)
    ps = jnp.exp(sink - m)
    inv = 1.0 / (jnp.sum(p, axis=-1, keepdims=True) + ps)
    return p * inv, ps * inv


def _swa_fwd(proj, q_gain, k_gain, sinks, job=None):
    T = proj.shape[0]
    nb = T // BLK

    def body(q_ref, kc_ref, kp_ref, vc_ref, vp_ref, qg_ref, kg_ref, sk_ref, o_ref):
        low = _low_half()
        mask = _swa_mask(pl.program_id(0) == 0)
        qn = _half_rms(_stack_cols(q_ref[...]), low)[0] * qg_ref[...]
        k_lo, k_hi, v_lo, v_hi, _ = _swa_keys(kp_ref, kc_ref, vp_ref, vc_ref, kg_ref[...], low)
        p_lo, _ = _swa_probs(qn, k_lo, _sink_cols(sk_ref, 0), mask)
        p_hi, _ = _swa_probs(qn, k_hi, _sink_cols(sk_ref, 1), mask)
        o = (_dot(p_lo, v_lo, BNN) + _dot(p_hi, v_hi, BNN)).astype(BF16)
        for c in range(Q_COLS):
            o_ref[:, c * LANES:(c + 1) * LANES] = _col_of(o, c)

    q_gain, k_gain = jnp.tile(q_gain, (1, 2)), jnp.tile(k_gain, (1, 2))
    cur = lambda w, off: pl.BlockSpec((BLK, w), lambda i: (i, off // w))
    prev = lambda w, off: pl.BlockSpec((BLK, w), lambda i: (jnp.maximum(i - 1, 0), off // w))
    small = lambda n: pl.BlockSpec((1, 2 * n), lambda i: (0, 0))
    return _pcall(
        body, grid=(nb,),
        in_specs=[cur(BW, OFF_QB), cur(KVW, OFF_KB), prev(KVW, OFF_KB), cur(KVW, OFF_VB), prev(KVW, OFF_VB),
                  small(B_HD), small(B_HD), pl.BlockSpec(memory_space=pltpu.SMEM)],
        out_specs=[pl.BlockSpec((BLK, BW), lambda i: (i, 0))],
        out_shape=[jax.ShapeDtypeStruct((T, BW), BF16)], scratch_shapes=[], name="swa_fwd", semantics=("parallel",),
        args=[proj, proj, proj, proj, proj, q_gain, k_gain, sinks], job=job)


def _swa_bwd(proj, dout, q_gain, k_gain, sinks, job=None):
    T = proj.shape[0]
    nb = T // BLK
    W = BW + 2 * KVW

    def body(q_ref, kc_ref, kp_ref, vc_ref, vp_ref, do_ref, qg_ref, kg_ref, sk_ref,
             dq_ref, dkv_ref, pqg_ref, pkg_ref, psk_ref, dkn_c, dv_c):
        i = pl.program_id(0)
        live = i < nb
        low = _low_half()
        high = jnp.logical_not(low)
        qg, kg = qg_ref[...], kg_ref[...]
        mask = _swa_mask(i == 0)
        lane = lax.broadcasted_iota(jnp.int32, (1, LANES), 1)
        scale = B_HD ** -0.5

        @pl.when(i == 0)
        def _():
            dkn_c[...] = jnp.zeros_like(dkn_c)
            dv_c[...] = jnp.zeros_like(dv_c)

        q_hat, q_r = _half_rms(_stack_cols(q_ref[...]), low)
        qn = q_hat * qg
        k_lo, k_hi, v_lo, v_hi, hats = _swa_keys(kp_ref, kc_ref, vp_ref, vc_ref, kg, low)
        do = _stack_cols(do_ref[...])
        dqn = jnp.zeros((KV_HEADS, 2 * BLK, LANES), F32)
        acc_sk = jnp.zeros((1, LANES), F32)
        dk_parts, dv_parts = [], []
        for hi, (k_h, v_h) in enumerate(((k_lo, v_lo), (k_hi, v_hi))):
            p, ps = _swa_probs(qn, k_h, _sink_cols(sk_ref, hi), mask)
            dp = _dot(do, v_h, BNT)
            delta = jnp.sum(p * dp, axis=-1, keepdims=True)
            ds = p * (dp - delta) * scale
            dqn = dqn + _dot(ds, k_h, BNN)
            dk_parts.append(_dot(ds, qn, BTN))
            dv_parts.append(_dot(p, do, BTN))
            t = ps * delta
            for hk in range(KV_HEADS):
                for rows in range(2):
                    h = GROUP * hk + 2 * rows + hi
                    acc_sk = acc_sk + jnp.where(
                        lane == h, -jnp.sum(t[hk, rows * BLK:(rows + 1) * BLK], axis=0, keepdims=True), 0.0)
        dqh = dqn * qg
        dq = (q_r * (dqh - q_hat * (_half_sum(dqh * q_hat, low) * (1.0 / B_HD)))).astype(BF16)
        for c in range(Q_COLS):
            dq_ref[:, c * LANES:(c + 1) * LANES] = _col_of(dq, c)
        acc_qg = _fold_halves(_fold8((dqn * q_hat).reshape(KV_HEADS * 2 * BLK, LANES)), low)

        def native(parts, j):
            lo_arr, hi_arr = parts
            a, b = 2 * j, 2 * j + 1
            return (jnp.where(low, lo_arr[a], 0.0) + pltpu.roll(jnp.where(high, hi_arr[a], 0.0), B_HD, 1)
                    + jnp.where(high, hi_arr[b], 0.0) + pltpu.roll(jnp.where(low, lo_arr[b], 0.0), B_HD, 1))

        acc_kg = jnp.zeros((8, LANES), F32)
        for j in range(KVW // LANES):
            cs = slice(j * LANES, (j + 1) * LANES)
            dkn = jnp.where(live, native(dk_parts, j), 0.0)
            dvc = jnp.where(live, native(dv_parts, j), 0.0)
            kp_hat, kp_r = hats[j][0][:BLK], hats[j][1][:BLK]
            dkn_prev = dkn_c[:, cs] + dkn[:BLK]
            dv_prev = dv_c[:, cs] + dvc[:BLK]
            acc_kg = acc_kg + _fold8(dkn_prev * kp_hat)
            dkh = dkn_prev * kg
            dkv_ref[:, cs] = (kp_r * (dkh - kp_hat * (_half_sum(dkh * kp_hat, low) * (1.0 / B_HD)))).astype(BF16)
            dkv_ref[:, KVW + j * LANES:KVW + (j + 1) * LANES] = dv_prev.astype(BF16)
            dkn_c[:, cs] = dkn[BLK:]
            dv_c[:, cs] = dvc[BLK:]
        keep = jnp.where(i > 0, 1.0, 0.0)
        pqg_ref[...] = jnp.where(live, acc_qg, 0.0)
        pkg_ref[...] = _fold_halves(acc_kg, low) * keep
        psk_ref[...] = jnp.broadcast_to(jnp.where(live, acc_sk, 0.0), (8, LANES)) * (
            lax.broadcasted_iota(jnp.int32, (8, LANES), 0) == 0).astype(F32)

    q_gain, k_gain = jnp.tile(q_gain, (1, 2)), jnp.tile(k_gain, (1, 2))
    last = nb - 1
    cur = lambda w, off: pl.BlockSpec((BLK, w), lambda i: (jnp.minimum(i, last), off // w))
    prev = lambda w, off: pl.BlockSpec((BLK, w), lambda i: (jnp.maximum(i - 1, 0), off // w))
    small = lambda n: pl.BlockSpec((1, 2 * n), lambda i: (0, 0))
    part = pl.BlockSpec((8, 128), lambda i: (i, 0))
    p_shape = jax.ShapeDtypeStruct(((nb + 1) * 8, 128), F32)
    return _pcall(
        body, grid=(nb + 1,),
        in_specs=[cur(BW, OFF_QB), cur(KVW, OFF_KB), prev(KVW, OFF_KB), cur(KVW, OFF_VB), prev(KVW, OFF_VB),
                  pl.BlockSpec((BLK, BW), lambda i: (jnp.minimum(i, last), 0)), small(B_HD), small(B_HD),
                  pl.BlockSpec(memory_space=pltpu.SMEM)],
        out_specs=[pl.BlockSpec((BLK, BW), lambda i: (i, 0)),
                   pl.BlockSpec((BLK, 2 * KVW), lambda i: (jnp.maximum(i - 1, 0), 0)), part, part, part],
        out_shape=[jax.ShapeDtypeStruct((T + BLK, BW), BF16), jax.ShapeDtypeStruct((T, 2 * KVW), BF16),
                   p_shape, p_shape, p_shape],
        scratch_shapes=[pltpu.VMEM((BLK, KVW), F32), pltpu.VMEM((BLK, KVW), F32)], name="swa_bwd",
        semantics=("arbitrary",), args=[proj, proj, proj, proj, proj, dout, q_gain, k_gain, sinks], job=job)


def _branch_merge(ya_pre, attn, wa_t, wb_t, proj, tm, tn, job=None):
    T = ya_pre.shape[0]

    def body(a_ref, b_ref, wa_ref, wb_ref, ga_ref, gb_ref, ya_ref, yb_ref, mg_ref):
        ya = lax.dot_general(a_ref[...], wa_ref[...], NT, preferred_element_type=F32)
        yb = lax.dot_general(b_ref[...], wb_ref[...], NT, preferred_element_type=F32)
        ya_ref[...] = ya.astype(BF16)
        yb_ref[...] = yb.astype(BF16)
        mg_ref[...] = (_sigmoid(ga_ref[...]) * ya + _sigmoid(gb_ref[...]) * yb).astype(BF16)

    o_spec = pl.BlockSpec((tm, tn), lambda i, j: (i, j))
    o_shape = jax.ShapeDtypeStruct((T, D), BF16)
    return _pcall(
        body, grid=(T // tm, D // tn),
        in_specs=[pl.BlockSpec((tm, AW), lambda i, j: (i, 0)), pl.BlockSpec((tm, BW), lambda i, j: (i, 0)),
                  pl.BlockSpec((tn, AW), lambda i, j: (j, 0)), pl.BlockSpec((tn, BW), lambda i, j: (j, 0)),
                  pl.BlockSpec((tm, tn), lambda i, j: (i, OFF_GTA // tn + j)),
                  pl.BlockSpec((tm, tn), lambda i, j: (i, OFF_GTB // tn + j))],
        out_specs=[o_spec, o_spec, o_spec], out_shape=[o_shape, o_shape, o_shape], scratch_shapes=[], name="branch_merge",
        semantics=("parallel", "parallel"), args=[ya_pre, attn, wa_t, wb_t, proj, proj], job=job)


def _ij(i, j, k):
    return (i, j)


def _local_step(x, tgt, mod, g1, g2, lbl, og, qg, kg, sk, shards, me, c_arr):
    win_s, wa_s, wb_s, wout_s, wmi_s, wmo_s = shards
    T = x.shape[0]
    tm, tr, tt = min(1024, T), min(256, T), min(1024, T)
    tk_t = min(1024, T)
    tn = 512
    sh1, sc1, gt1, sh2, sc2, gt2 = (mod[:, i * D:(i + 1) * D] for i in range(N_MOD))
    nI = T // tm
    blk = (tm, tn)
    part = lambda: ((nI * 8, D), F32, (8, tn), _ij)
    vec_j = ((1, tn), lambda i, j, k: (0, j))

    h = _rms_mod_fwd("rms1_fwd", x, g1, sc1, sh1, tr)

    def epi_store(acc, ex, ou):
        ou[0][...] = acc.astype(ou[0].dtype)

    tm2 = min(2048, T)
    blk2 = (tm2, tn)

    full = lambda s: (0, s.shape[0])
    last = wmi_s.shape[0]
    (win_t,) = _run_job("gather_w_in", _gather_relay_job([win_s]))
    (proj,), (wa_t, wb_t, w_out, wmi_part) = _mm(
        "in_proj", "nt", [(h, D)], win_t, T, IN_W, D, tm2, tn, D, [], [((T, IN_W), F32, blk2, _ij)], epi_store,
        job=_gather_job([wa_s, wb_s, wout_s, wmi_s], rows=[full(wa_s), full(wb_s), full(wout_s), (0, MI_CUTS[0])]))
    (ya_pre, st), (wmi_part,) = _hgrn_fwd(
        proj, lbl, og, tt, job=_gather_job([wmi_s], rows=[MI_CUTS], into=[wmi_part]))
    (attn,), (wmi_t, wmo_part) = _swa_fwd(
        proj, qg, kg, sk, job=_gather_job([wmi_s, wmo_s], rows=[(MI_CUTS[1], last), (0, MO_CUT)], into=[wmi_part, None]))
    (ya, yb, merged), _ = _branch_merge(ya_pre, attn, wa_t, wb_t, proj, tm, tn)

    def epi_res1(acc, ex, ou):
        x_ref, gt_ref = ex
        ou[0][...] = acc.astype(BF16)
        ou[1][...] = x_ref[...] + gt_ref[...] * acc

    mo, x1 = _mm("out_proj", "nn", [(merged, D)], w_out, T, D, D, tm, tn, D, [(x, blk, _ij), (gt1, *vec_j)],
                 [((T, D), BF16, blk, _ij), ((T, D), F32, blk, _ij)], epi_res1)
    h2 = _rms_mod_fwd("rms2_fwd", x1, g2, sc2, sh2, tr)

    def epi_relu2(acc, ex, ou):
        r = jnp.maximum(acc, 0.0)
        ou[0][...] = r.astype(BF16)
        ou[1][...] = (r * r).astype(BF16)

    (r, a), (w_mo,) = _mm("mlp_in", "nt", [(h2, D)], wmi_t, T, HID, D, tm2, tn, D, [],
                          [((T, HID), BF16, blk2, _ij), ((T, HID), BF16, blk2, _ij)], epi_relu2,
                          job=_gather_job([wmo_s], rows=[(MO_CUT, last)], into=[wmo_part]))

    def epi_loss(acc, ex, ou):
        x1_ref, t_ref, gt_ref = ex
        e = x1_ref[...] + gt_ref[...] * acc - t_ref[...]
        dy = e * (1.0 / D)
        ou[0][...] = dy
        ou[1][...] = (gt_ref[...] * dy).astype(BF16)
        ou[2][...] = _fold8(e * e) * (0.5 / D)
        ou[3][...] = _fold8(dy * acc)

    wide = (tm, 1024)
    part_w = ((nI * 8, D), F32, (8, 1024), _ij)
    dy, dz, p_loss, p_gt2 = _mm(
        "mlp_out", "nn", [(a, HID)], w_mo, T, D, HID, tm, 1024, 1024,
        [(x1, wide, _ij), (tgt, wide, _ij), (gt2, (1, 1024), lambda i, j, k: (0, j))],
        [((T, D), F32, wide, _ij), ((T, D), BF16, wide, _ij), part_w, part_w], epi_loss)

    def epi_du(acc, ex, ou):
        ou[0][...] = (acc * (2.0 * ex[0][...].astype(F32))).astype(BF16)

    (du,) = _mm("mlp_out_dx", "nt", [(dz, D)], w_mo, T, HID, D, tm2, tn, D, [(r, blk2, _ij)],
                [((T, HID), BF16, blk2, _ij)], epi_du)
    gblk = (1024, 1024)
    gwide = (1024, D)
    pair_sum = lambda nm, g, r1: _pair_sum("pair_sum_" + nm, g, r1, c_arr, _sum_rows(r1.shape[1]))
    (g_mo,) = _mm("mlp_out_dw", "tn", [(a, HID)], dz, HID, D, T, 1024, D, tk_t, [], [((HID, D), BF16, gwide, _ij)], epi_store)
    (dh2,), (r1_mo,) = _mm("mlp_in_dx", "nn", [(du, HID)], wmi_t, T, D, HID, tm, D, 1024, [],
                           [((T, D), F32, (tm, D), _ij)], epi_store, job=_pair_job([g_mo]))
    dx1, p_sh2, p_sc2, p_g2, dmo, p_gt1 = _rms_mod_bwd("rms2_bwd", dh2, x1, g2, sc2, dy, tr, gate=gt1, mo=mo)
    s_mo = pair_sum("mlp_out", g_mo, r1_mo)
    near, far = (1, 2), (3,)
    (g_mi,), (rn_mo,) = _mm("mlp_in_dw", "tn", [(du, HID)], h2, HID, D, T, 1024, D, tk_t, [],
                            [((HID, D), BF16, gwide, _ij)], epi_store, job=_chip_job([s_mo], near))

    def epi_gates(acc, ex, ou):
        ya_ref, yb_ref, ga_ref, gb_ref = ex
        sa, sb = _sigmoid(ga_ref[...]), _sigmoid(gb_ref[...])
        ou[0][...] = (acc * sa).astype(BF16)
        ou[1][...] = (acc * sb).astype(BF16)
        ou[2][...] = (acc * ya_ref[...].astype(F32) * (sa * (1.0 - sa))).astype(BF16)
        ou[3][...] = (acc * yb_ref[...].astype(F32) * (sb * (1.0 - sb))).astype(BF16)

    o_bf = ((T, D), BF16, blk, _ij)
    (dya, dyb, dga, dgb), (rf_mo, r1_mi) = _mm(
        "out_proj_dx", "nt", [(dmo, D)], w_out, T, D, D, tm, tn, D,
        [(ya, blk, _ij), (yb, blk, _ij), (proj, blk, lambda i, j, k: (i, OFF_GTA // tn + j)),
         (proj, blk, lambda i, j, k: (i, OFF_GTB // tn + j))], [o_bf, o_bf, o_bf, o_bf], epi_gates,
        job=_both(_chip_job([s_mo], far), _pair_job([g_mi])))
    s_mi = pair_sum("mlp_in", g_mi, r1_mi)
    (g_out,) = _mm("out_proj_dw", "tn", [(merged, D)], dmo, D, D, T, 1024, 1024, tk_t, [], [((D, D), BF16, gblk, _ij)], epi_store)
    (dya_pre,) = _mm("branch_a_dx", "nn", [(dya, D)], wa_t, T, AW, D, tm, tn, D, [], [((T, AW), F32, blk, _ij)], epi_store)
    (dattn,) = _mm("branch_b_dx", "nn", [(dyb, D)], wb_t, T, BW, D, tm, tn, D, [], [((T, BW), F32, blk, _ij)], epi_store)
    (g_a,) = _mm("branch_a_dw", "tn", [(dya, D)], ya_pre, D, AW, T, 1024, 1024, tk_t, [], [((D, AW), BF16, gblk, _ij)], epi_store)
    (g_b,) = _mm("branch_b_dw", "tn", [(dyb, D)], attn, D, BW, T, 1024, 1024, tk_t, [], [((D, BW), BF16, gblk, _ij)], epi_store)
    (dqa, dfa, dia, dgg, p_lb, p_og), (rn_mi, r1_out, r1_a, r1_b) = _hgrn_bwd(
        proj, st, dya_pre, lbl, og, tt, job=_both(_chip_job([s_mi], near), _pair_job([g_out, g_a, g_b])))
    (dqb, dkv, p_qg, p_kg, p_sk), (rf_mi,) = _swa_bwd(proj, dattn, qg, kg, sk, job=_chip_job([s_mi], far))
    s_out, s_a, s_b = pair_sum("out", g_out, r1_out), pair_sum("branch_a", g_a, r1_a), pair_sum("branch_b", g_b, r1_b)
    pieces = [(dqa, AW), (dfa, AW), (dia, AW), (dgg, AW), (dqb, BW), (dkv, 2 * KVW), (dga, D), (dgb, D)]
    (g_in,), (r2_out, r2_a, r2_b) = _pieces_tn("in_proj_dw", pieces, h, 512, job=_chip_job([s_out, s_a, s_b]))
    (r1_in,) = _run_job("pair_w_in", _pair_job([g_in]))
    s_in = pair_sum("in", g_in, r1_in)
    (dx, p_sh1, p_sc1, p_g1), (r2_in,) = _pieces_nn_rms(
        "in_proj_dx", pieces, win_t, x, g1, sc1, dx1, tm, 512, job=_chip_job([s_in]))

    partials = dict(sh1=p_sh1, sc1=p_sc1, gt1=p_gt1, sh2=p_sh2, sc2=p_sc2, gt2=p_gt2, g1=p_g1, g2=p_g2,
                    lb=p_lb, og=p_og, qg=p_qg, kg=p_kg, sk=p_sk, loss=p_loss)
    sums = dict(w_in=(s_in, [r2_in]), w_branch_a=(s_a, [r2_a]), w_branch_b=(s_b, [r2_b]), w_out=(s_out, [r2_out]),
                w_mlp_in=(s_mi, [rn_mi, rf_mi]), w_mlp_out=(s_mo, [rn_mo, rf_mo]))
    return dx, sums, partials


def _exchange_slots(buf, send_sems, recv_sems):
    me = _mesh_pos()
    mine = buf.at[_index(me)]
    sends = []
    for k in range(1, N_DEV):
        cp = pltpu.make_async_remote_copy(src_ref=mine, dst_ref=mine, send_sem=send_sems.at[k - 1],
                                          recv_sem=recv_sems.at[k - 1], device_id=_flip(me, k), device_id_type=MESH)
        cp.start()
        sends.append(cp)
    for k in range(1, N_DEV):
        theirs = buf.at[_index(_flip(me, k))]
        pltpu.make_async_remote_copy(src_ref=theirs, dst_ref=theirs, send_sem=send_sems.at[k - 1],
                                     recv_sem=recv_sems.at[k - 1], device_id=_flip(me, k), device_id_type=MESH).wait_recv()
    for cp in sends:
        cp.wait_send()


ADA_W = N_MOD * D // N_DEV


def _ada_mod(c, w_ada, b_shard):
    def body(c_ref, w_ref, b_ref, mod_ref, sc_ref, cbuf, mbuf, s1, r1, s2, r2):
        me = _index(_mesh_pos())
        cbuf[me] = c_ref[...]
        _exchange_slots(cbuf, s1, r1)
        row = lax.broadcasted_iota(jnp.int32, (N_DEV, D), 0)
        call = jnp.zeros((N_DEV, D), F32)
        for d in range(N_DEV):
            call = jnp.where(row == d, cbuf[d], call)
        sc = call * _sigmoid(call)
        sc_ref[...] = sc
        mbuf[me] = _dot(sc, w_ref[...]) + b_ref[...]
        _exchange_slots(mbuf, s2, r2)
        for s in range(N_DEV):
            mod_ref[:, s * ADA_W:(s + 1) * ADA_W] = mbuf[s, pl.ds(me, 1), :]

    return pl.pallas_call(
        body, in_specs=[_VMEM, _VMEM, _VMEM], out_specs=[_VMEM, _VMEM],
        out_shape=[jax.ShapeDtypeStruct((1, N_MOD * D), F32), jax.ShapeDtypeStruct((N_DEV, D), F32)],
        scratch_shapes=[pltpu.VMEM((N_DEV, 1, D), F32), pltpu.VMEM((N_DEV, N_DEV, ADA_W), F32),
                        _SEMS(N_DEV - 1), _SEMS(N_DEV - 1), _SEMS(N_DEV - 1), _SEMS(N_DEV - 1)],
        name="ada_mod", compiler_params=pltpu.CompilerParams(vmem_limit_bytes=VMEM_LIMIT),
    )(c, w_ada, b_shard)


SMALL_SEGS = (("b_ada", N_MOD * D), ("norm1_gain", D), ("norm2_gain", D), ("lb0", AW), ("lb1", AW),
              ("hgrn_o_gain", AW), ("q_norm_gain", 128), ("k_norm_gain", 128), ("sinks", 128))
SMALL_W = sum(w for _, w in SMALL_SEGS)
X_SEGS = (("sh1", D), ("sc1", D), ("gt1", D), ("sh2", D), ("sc2", D), ("gt2", D), ("g1", D), ("g2", D),
          ("lb", AW), ("og", AW), ("qg", 128), ("kg", 128), ("sk", 128), ("loss", 128))
X_W = sum(w for _, w in X_SEGS)


def _offsets(segs):
    out, o = {}, 0
    for name, w in segs:
        out[name] = (o, w)
        o += w
    return out


def _small_reduce(parts, lb_logits):
    xo, so = _offsets(X_SEGS), _offsets(SMALL_SEGS)
    names = [nm for nm, _ in X_SEGS]

    def body(*refs):
        p_refs = dict(zip(names, refs[:len(names)]))
        lbl_ref, allx, gs_ref, loss_ref, send_sems, recv_sems = refs[len(names):]
        me = _index(_mesh_pos())
        for nm, (o, w) in xo.items():
            if nm == "loss":
                allx[me, :, o:o + w] = jnp.broadcast_to(jnp.sum(p_refs[nm][...]), (1, w))
            else:
                allx[me, :, o:o + w] = jnp.sum(p_refs[nm][...], axis=0, keepdims=True)
        _exchange_slots(allx, send_sems, recv_sems)
        tot = allx[0]
        for d in range(1, N_DEV):
            tot = tot + allx[d]
        seg = lambda nm: tot[:, xo[nm][0]:xo[nm][0] + xo[nm][1]]

        def put(nm, v):
            gs_ref[:, so[nm][0]:so[nm][0] + so[nm][1]] = v

        put("b_ada", tot[:, 0:N_MOD * D])
        put("norm1_gain", seg("g1"))
        put("norm2_gain", seg("g2"))
        lbl = lbl_ref[...]
        lb = _sigmoid(lbl[0:1, :] - lbl[1:2, :])
        dl0 = seg("lb") * lb * (1.0 - lb)
        put("lb0", dl0)
        put("lb1", -dl0)
        put("hgrn_o_gain", seg("og"))
        put("q_norm_gain", seg("qg"))
        put("k_norm_gain", seg("kg"))
        put("sinks", seg("sk"))
        loss_ref[...] = seg("loss")

    return pl.pallas_call(
        body, in_specs=[_VMEM] * (len(names) + 1), out_specs=[_VMEM, _VMEM, _VMEM],
        out_shape=[jax.ShapeDtypeStruct((N_DEV, 1, X_W), F32), jax.ShapeDtypeStruct((1, SMALL_W), F32),
                   jax.ShapeDtypeStruct((1, 128), F32)],
        scratch_shapes=[_SEMS(N_DEV - 1), _SEMS(N_DEV - 1)], name="small_reduce",
        compiler_params=pltpu.CompilerParams(vmem_limit_bytes=VMEM_LIMIT),
    )(*[parts[nm] for nm in names], lb_logits)


def _adamw_math(w, g, m, v):
    m = B1 * m + (1.0 - B1) * g
    v = B2 * v + (1.0 - B2) * (g * g)
    m_hat = m / (1.0 - B1 ** STEP)
    v_hat = v / (1.0 - B2 ** STEP)
    return -LR * (m_hat / (jnp.sqrt(v_hat) + ADAM_EPS) + WD * w), m, v


def _sum_rows(rs):
    return 256 if rs % 256 == 0 else rs // 2


def _pair_sum(name, g, recv, c_arr, tr):
    _, rs, cols = recv.shape
    blk = (1, tr, cols)

    def body(c_ref, g_ref, r_ref, o_ref):
        o_ref[...] = (g_ref[...].astype(F32) + r_ref[...].astype(F32)).astype(BF16)

    grid_spec = pltpu.PrefetchScalarGridSpec(
        num_scalar_prefetch=1, grid=(4, rs // tr),
        in_specs=[pl.BlockSpec(blk, lambda q, i, c: (2 * q + c[0], i, 0)), pl.BlockSpec(blk, lambda q, i, c: (q, i, 0))],
        out_specs=pl.BlockSpec(blk, lambda q, i, c: (q, i, 0)))
    return pl.pallas_call(body, grid_spec=grid_spec, out_shape=jax.ShapeDtypeStruct((4, rs, cols), BF16), name=name,
                          compiler_params=_params(("parallel", "parallel")))(c_arr, g.reshape(N_DEV, rs, cols), recv)


def _sum_adamw(name, sums, recvs, q_arr, w, m, v, transposed, tile):
    rows, cols = w.shape
    nR = len(recvs)

    def body(q_ref, s_ref, *refs):
        r_refs = refs[:nR]
        w_ref, m_ref, v_ref, g_ref, d_ref, nm_ref, nv_ref = refs[nR:]
        g = s_ref[0].astype(F32)
        for r_ref in r_refs:
            for slot in range(r_ref.shape[0]):
                g = g + r_ref[slot].astype(F32)
        g = g.T if transposed else g
        g_ref[...] = g
        d_ref[...], nm_ref[...], nv_ref[...] = _adamw_math(w_ref[...], g, m_ref[...], v_ref[...])

    if transposed:
        slab = lambda n, first: pl.BlockSpec((n, cols, tile), lambda i, q: (first(q), 0, i))
    else:
        slab = lambda n, first: pl.BlockSpec((n, tile, cols), lambda i, q: (first(q), i, 0))
    spec = pl.BlockSpec((tile, cols), lambda i, q: (i, 0))
    shape = jax.ShapeDtypeStruct((rows, cols), F32)
    grid_spec = pltpu.PrefetchScalarGridSpec(
        num_scalar_prefetch=1, grid=(rows // tile,),
        in_specs=[slab(1, lambda q: q[0])] + [slab(r.shape[0], lambda q: 0) for r in recvs] + [spec] * 3,
        out_specs=[spec] * 4)
    return pl.pallas_call(body, grid_spec=grid_spec, out_shape=[shape] * 4, name=name,
                          compiler_params=_params(("parallel",)))(q_arr, sums, *recvs, w, m, v)


def _adamw(name, w, g, m, v, tr):
    rows, cols = w.shape

    def body(w_ref, g_ref, m_ref, v_ref, d_ref, nm_ref, nv_ref):
        d_ref[...], nm_ref[...], nv_ref[...] = _adamw_math(w_ref[...], g_ref[...], m_ref[...], v_ref[...])

    spec = pl.BlockSpec((tr, cols), lambda i: (i, 0))
    shape = jax.ShapeDtypeStruct((rows, cols), F32)
    return pl.pallas_call(
        body, grid=(rows // tr,), in_specs=[spec] * 4, out_specs=[spec] * 3, out_shape=[shape] * 3, name=name,
        compiler_params=_params(("parallel",)),
    )(w, g, m, v)


def _ada_update(sc_t, dmod_cols, w, m, v, tr):
    rows, cols = w.shape

    def body(s_ref, d_ref, w_ref, m_ref, v_ref, g_ref, dl_ref, nm_ref, nv_ref):
        g = jnp.dot(s_ref[...], d_ref[...], precision=lax.Precision.HIGHEST, preferred_element_type=F32)
        g_ref[...] = g
        dl_ref[...], nm_ref[...], nv_ref[...] = _adamw_math(w_ref[...], g, m_ref[...], v_ref[...])

    spec = pl.BlockSpec((tr, cols), lambda i: (i, 0))
    shape = jax.ShapeDtypeStruct((rows, cols), F32)
    return pl.pallas_call(
        body, grid=(rows // tr,),
        in_specs=[pl.BlockSpec((tr, N_DEV), lambda i: (i, 0)), pl.BlockSpec((N_DEV, cols), lambda i: (0, 0)), spec, spec, spec],
        out_specs=[spec] * 4, out_shape=[shape] * 4, name="ada_update", compiler_params=_params(("parallel",)),
    )(sc_t, dmod_cols, w, m, v)


BIG = ("w_in", "w_branch_a", "w_branch_b", "w_out", "w_mlp_in", "w_mlp_out")
COLUMN_SHARDED = ("w_in", "w_branch_a", "w_branch_b", "w_mlp_in")
WEIGHTS = ("w_ada", "b_ada", "norm1_gain", "w_in", "lb_logits", "hgrn_o_gain", "q_norm_gain", "k_norm_gain", "sinks",
           "w_branch_a", "w_branch_b", "w_out", "norm2_gain", "w_mlp_in", "w_mlp_out")


def _pack_small(p):
    lb = p["lb_logits"]
    src = dict(p, lb0=lb[0:1], lb1=lb[1:2])
    return jnp.concatenate([jnp.pad(src[nm], ((0, 0), (0, w - src[nm].shape[1]))) for nm, w in SMALL_SEGS], axis=1)


def _unpack_small(vec, shapes):
    so = _offsets(SMALL_SEGS)
    out = {}
    for nm, shp in shapes.items():
        if nm == "lb_logits":
            o = so["lb0"][0]
            out[nm] = vec[0, o:o + 2 * AW].reshape(2, AW)
        else:
            o = so[nm][0]
            out[nm] = vec[:, o:o + shp[1]]
    return out


def kernel(x, c, w_ada, b_ada, norm1_gain, w_in, lb_logits, hgrn_o_gain, q_norm_gain, k_norm_gain, sinks, w_branch_a, w_branch_b, w_out, norm2_gain, w_mlp_in, w_mlp_out, loss_target, m_w_ada, m_b_ada, m_norm1_gain, m_w_in, m_lb_logits, m_hgrn_o_gain, m_q_norm_gain, m_k_norm_gain, m_sinks, m_w_branch_a, m_w_branch_b, m_w_out, m_norm2_gain, m_w_mlp_in, m_w_mlp_out, v_w_ada, v_b_ada, v_norm1_gain, v_w_in, v_lb_logits, v_hgrn_o_gain, v_q_norm_gain, v_k_norm_gain, v_sinks, v_w_branch_a, v_w_branch_b, v_w_out, v_norm2_gain, v_w_mlp_in, v_w_mlp_out):
    w = dict(w_ada=w_ada, b_ada=b_ada, norm1_gain=norm1_gain, w_in=w_in, lb_logits=lb_logits, hgrn_o_gain=hgrn_o_gain,
             q_norm_gain=q_norm_gain, k_norm_gain=k_norm_gain, sinks=sinks, w_branch_a=w_branch_a, w_branch_b=w_branch_b,
             w_out=w_out, norm2_gain=norm2_gain, w_mlp_in=w_mlp_in, w_mlp_out=w_mlp_out)
    m = dict(w_ada=m_w_ada, b_ada=m_b_ada, norm1_gain=m_norm1_gain, w_in=m_w_in, lb_logits=m_lb_logits,
             hgrn_o_gain=m_hgrn_o_gain, q_norm_gain=m_q_norm_gain, k_norm_gain=m_k_norm_gain, sinks=m_sinks,
             w_branch_a=m_w_branch_a, w_branch_b=m_w_branch_b, w_out=m_w_out, norm2_gain=m_norm2_gain,
             w_mlp_in=m_w_mlp_in, w_mlp_out=m_w_mlp_out)
    v = dict(w_ada=v_w_ada, b_ada=v_b_ada, norm1_gain=v_norm1_gain, w_in=v_w_in, lb_logits=v_lb_logits,
             hgrn_o_gain=v_hgrn_o_gain, q_norm_gain=v_q_norm_gain, k_norm_gain=v_k_norm_gain, sinks=v_sinks,
             w_branch_a=v_w_branch_a, w_branch_b=v_w_branch_b, w_out=v_w_out, norm2_gain=v_norm2_gain,
             w_mlp_in=v_w_mlp_in, w_mlp_out=v_w_mlp_out)
    for d in (w, m, v):
        for nm in ("w_ada",) + BIG:
            d[nm] = d[nm][0]
    px, py, pc = _mesh_pos()
    me = _index((px, py, pc))
    c_arr = jnp.reshape(pc, (1,)).astype(jnp.int32)
    q_arr = jnp.reshape(2 * px + py, (1,)).astype(jnp.int32)

    shards = [(w[nm].T if nm in COLUMN_SHARDED else w[nm]).astype(BF16) for nm in BIG]
    b_shard = lax.dynamic_slice(b_ada, (0, me * ADA_W), (1, ADA_W))
    mod, sc_all = _ada_mod(c, w["w_ada"], b_shard)

    dx, sums, parts = _local_step(x[0], loss_target[0], mod, norm1_gain, norm2_gain, lb_logits, hgrn_o_gain,
                                  q_norm_gain, k_norm_gain, sinks, shards, me, c_arr)

    allx, g_small, loss = _small_reduce(parts, lb_logits)

    grad, delta, new_m, new_v = {}, {}, {}, {}
    for nm in BIG:
        s, r2 = sums[nm]
        grad[nm], delta[nm], new_m[nm], new_v[nm] = _sum_adamw(
            "adamw_" + nm, s, r2, q_arr, w[nm], m[nm], v[nm], nm in COLUMN_SHARDED, 128)

    dmod_cols = lax.dynamic_slice(allx[:, 0, :], (0, me * ADA_W), (N_DEV, ADA_W))
    grad["w_ada"], delta["w_ada"], new_m["w_ada"], new_v["w_ada"] = _ada_update(
        sc_all.T, dmod_cols, w["w_ada"], m["w_ada"], v["w_ada"], 256)

    small_names = [nm for nm in WEIGHTS if nm not in BIG and nm != "w_ada"]
    shapes = {nm: w[nm].shape for nm in small_names}
    ds, ms, vs = _adamw("adamw_small", _pack_small(w), g_small, _pack_small(m), _pack_small(v), 1)
    for dst, vec in ((grad, g_small), (delta, ds), (new_m, ms), (new_v, vs)):
        dst.update(_unpack_small(vec, shapes))

    def full(d, nm):
        return d[nm][None] if nm in BIG or nm == "w_ada" else d[nm]

    return (loss[0, 0], dx[None], *[full(grad, nm) for nm in WEIGHTS], *[full(delta, nm) for nm in WEIGHTS],
            *[full(new_m, nm) for nm in WEIGHTS], *[full(new_v, nm) for nm in WEIGHTS])
```

```python
import functools

import jax
import jax.numpy as jnp
from jax import lax
from jax.experimental import pallas as pl
from jax.experimental.pallas import tpu as pltpu

F32 = jnp.float32
BF16 = jnp.bfloat16
MESH = pl.DeviceIdType.MESH

N_DEV = 8
D = 2048
A_HEADS, A_HD, CHUNK = 8, 128, 64
AW = A_HEADS * A_HD
Q_HEADS, KV_HEADS, GROUP, B_HD, BLK = 16, 4, 4, 64, 128
BW = Q_HEADS * B_HD
KVW = KV_HEADS * B_HD
HID = 4 * D
IN_W = 4 * AW + BW + 2 * KVW + 2 * D
OFF_QA, OFF_FA, OFF_IA, OFF_GA = 0, AW, 2 * AW, 3 * AW
OFF_QB = 4 * AW
OFF_KB = OFF_QB + BW
OFF_VB = OFF_KB + KVW
OFF_GTA = OFF_VB + KVW
OFF_GTB = OFF_GTA + D
N_MOD = 6
EPS = 1e-6
LR, B1, B2, ADAM_EPS, WD, STEP = 1e-3, 0.9, 0.999, 1e-8, 0.01, 10
NEG = -1e30

VMEM_LIMIT = 56 * 1024 * 1024
MI_CUTS = (512, 928)
MO_CUT = 336

NN = (((1,), (0,)), ((), ()))
NT = (((1,), (1,)), ((), ()))
TN = (((0,), (0,)), ((), ()))
BNN = (((2,), (1,)), ((0,), (0,)))
BNT = (((2,), (2,)), ((0,), (0,)))
BTN = (((1,), (1,)), ((0,), (0,)))


def _dot(a, b, dims=NN):
    return lax.dot_general(a.astype(BF16), b.astype(BF16), dims, preferred_element_type=F32)


def _params(sem):
    return pltpu.CompilerParams(dimension_semantics=sem, vmem_limit_bytes=VMEM_LIMIT)


def _sigmoid(x):
    return 1.0 / (1.0 + jnp.exp(-x))


def _fold8(v):
    r, n = v.shape
    return jnp.sum(v.reshape(r // 8, 8, n), axis=0)


_VMEM = pl.BlockSpec(memory_space=pltpu.VMEM)
_ANY = pl.BlockSpec(memory_space=pl.ANY)
_SEMS = lambda n: pltpu.SemaphoreType.DMA((n,))


def _mesh_pos():
    return lax.axis_index("x"), lax.axis_index("y"), lax.axis_index("c")


def _flip(pos, k):
    return tuple(1 - p if (k >> s) & 1 else p for p, s in zip(pos, (2, 1, 0)))


def _index(pos):
    return 4 * pos[0] + 2 * pos[1] + pos[2]


class _Job:
    def __init__(self, ins, out_shape, sems, start, finish, aliases=None):
        self.ins, self.out_shape, self.sems, self.start, self.finish = list(ins), list(out_shape), list(sems), start, finish
        self.aliases = dict(aliases or {})


def _both(j1, j2):
    assert not j1.aliases and not j2.aliases
    n_in, n_out, n_sem = len(j1.ins), len(j1.out_shape), len(j1.sems)
    first = lambda ins, outs, sems: (ins[:n_in], outs[:n_out], sems[:n_sem])
    second = lambda ins, outs, sems: (ins[n_in:], outs[n_out:], sems[n_sem:])

    def start(*refs):
        j1.start(*first(*refs))
        j2.start(*second(*refs))

    def finish(*refs):
        j1.finish(*first(*refs))
        j2.finish(*second(*refs))

    return _Job(j1.ins + j2.ins, j1.out_shape + j2.out_shape, j1.sems + j2.sems, start, finish)


def _pcall(body, *, grid, in_specs, out_specs, out_shape, scratch_shapes, name, semantics, args, job=None):
    if job is None:
        outs = pl.pallas_call(body, grid=grid, in_specs=in_specs, out_specs=out_specs, out_shape=out_shape,
                              scratch_shapes=scratch_shapes, name=name, compiler_params=_params(semantics))(*args)
        return list(outs), []
    n_in, n_out, n_scr = len(in_specs), len(out_specs), len(scratch_shapes)
    j_in, j_out = len(job.ins), len(job.out_shape)
    steps = tuple(grid)

    def carrier(*refs):
        o = 0
        main_in, o = refs[o:o + n_in], o + n_in
        job_in, o = refs[o:o + j_in], o + j_in
        main_out, o = refs[o:o + n_out], o + n_out
        job_out, o = refs[o:o + j_out], o + j_out
        main_scr, job_sems = refs[o:o + n_scr], refs[o + n_scr:]
        ids = [pl.program_id(a) for a in range(len(steps))]
        first = functools.reduce(lambda p, q: p & q, [i == 0 for i in ids])
        last = functools.reduce(lambda p, q: p & q, [i == s - 1 for i, s in zip(ids, steps)])

        @pl.when(first)
        def _():
            job.start(job_in, job_out, job_sems)

        body(*main_in, *main_out, *main_scr)

        @pl.when(last)
        def _():
            job.finish(job_in, job_out, job_sems)

    outs = pl.pallas_call(
        carrier, grid=grid, in_specs=list(in_specs) + [_ANY] * j_in, out_specs=list(out_specs) + [_ANY] * j_out,
        out_shape=list(out_shape) + job.out_shape, scratch_shapes=list(scratch_shapes) + job.sems, name=name,
        input_output_aliases={n_in + i: n_out + o for i, o in job.aliases.items()},
        compiler_params=_params(("arbitrary",) * len(steps)),
    )(*args, *job.ins)
    return list(outs[:n_out]), list(outs[n_out:])


def _run_job(name, job):
    j_in, j_out = len(job.ins), len(job.out_shape)

    def body(*refs):
        ins, outs, sems = refs[:j_in], refs[j_in:j_in + j_out], refs[j_in + j_out:]
        job.start(ins, outs, sems)
        job.finish(ins, outs, sems)

    return list(pl.pallas_call(body, in_specs=[_ANY] * j_in, out_specs=[_ANY] * j_out, out_shape=job.out_shape,
                               scratch_shapes=job.sems, name=name,
                               input_output_aliases=job.aliases)(*job.ins))


def _gather_job(shards, rows=None, into=None):
    n = len(shards)
    rows = rows or [(0, s.shape[0]) for s in shards]
    into = into or [None] * n
    olds, aliases = [], {}
    for a, buf in enumerate(into):
        if buf is not None:
            aliases[n + len(olds)] = a
            olds.append(buf)

    def copies(ins, outs, sems):
        send_sems, recv_sems, local_sems = sems
        x, y, c = _mesh_pos()
        me, sib = (x, y, c), (x, y, 1 - c)
        chips = [(1 - x, y), (x, 1 - y), (1 - x, 1 - y)]

        def part(a, p):
            rs, (r0, r1) = shards[a].shape[0], rows[a]
            return outs[a].at[pl.ds(_index(p) * rs + r0, r1 - r0), :]

        own = lambda a: ins[a].at[pl.ds(rows[a][0], rows[a][1] - rows[a][0]), :]

        def copy(a, k, block, to, src=None):
            return pltpu.make_async_remote_copy(
                src_ref=part(a, block) if src is None else src, dst_ref=part(a, block),
                send_sem=send_sems.at[7 * a + k], recv_sem=recv_sems.at[7 * a + k], device_id=to, device_id_type=MESH)

        mine = [pltpu.make_async_copy(own(a), part(a, me), local_sems.at[a]) for a in range(n)]
        first = []
        for a in range(n):
            first.append(copy(a, 0, me, sib, src=own(a)))
            first += [copy(a, 1 + j, me, (*chip, c), src=own(a)) for j, chip in enumerate(chips)]
        return me, sib, c, chips, copy, mine, first

    def start(ins, outs, sems):
        *_, mine, first = copies(ins, outs, sems)
        for cp in mine + first:
            cp.start()

    def finish(ins, outs, sems):
        me, sib, c, chips, copy, mine, first = copies(ins, outs, sems)
        passed = []
        for j, chip in enumerate(chips):
            for a in range(n):
                copy(a, 1 + j, (*chip, c), me).wait_recv()
                cp = copy(a, 4 + j, (*chip, c), sib)
                cp.start()
                passed.append(cp)
        for a in range(n):
            copy(a, 0, sib, me).wait_recv()
            for j, chip in enumerate(chips):
                copy(a, 4 + j, (*chip, 1 - c), me).wait_recv()
        for cp in first + passed:
            cp.wait_send()
        for cp in mine:
            cp.wait()

    return _Job(list(shards) + olds, [jax.ShapeDtypeStruct((N_DEV * s.shape[0], s.shape[1]), s.dtype) for s in shards],
                [_SEMS(7 * n), _SEMS(7 * n), _SEMS(n)], start, finish, aliases)


def _gather_relay_job(shards):
    n = len(shards)

    def tools(ins, outs, sems):
        send_sems, recv_sems, local_sems = sems
        x, y, c = _mesh_pos()
        q = 2 * x + y
        chip_at = lambda rel: (1 - x if rel & 2 else x, 1 - y if rel & 1 else y)

        def rows(a, chip, core):
            rs = shards[a].shape[0]
            return outs[a].at[pl.ds((2 * chip + core) * rs, rs), :]

        def copy(a, slot, chip, core, to, src=None):
            blk = rows(a, chip, core)
            return pltpu.make_async_remote_copy(src_ref=blk if src is None else src, dst_ref=blk,
                                                send_sem=send_sems.at[7 * a + slot], recv_sem=recv_sems.at[7 * a + slot],
                                                device_id=to, device_id_type=MESH)

        mine = [pltpu.make_async_copy(ins[a], rows(a, q, c), local_sems.at[a]) for a in range(n)]
        first = [copy(a, slot, q, c, (x, y, 1 - c) if slot == 0 else (*chip_at(slot), c), src=ins[a])
                 for a in range(n) for slot in (0, 1, 2)]
        return x, y, c, q, chip_at, copy, mine, first

    def start(ins, outs, sems):
        *_, mine, first = tools(ins, outs, sems)
        for cp in mine + first:
            cp.start()

    def finish(ins, outs, sems):
        x, y, c, q, chip_at, copy, mine, first = tools(ins, outs, sems)
        me, sib = (x, y, c), (x, y, 1 - c)

        def relay(src, dst):
            for a in range(n):
                copy(a, src, q ^ src, c, me).wait_recv()
                copy(a, 3, q ^ src, c, (*chip_at(dst), c)).start()
                copy(a, 3 + src, q ^ src, c, sib).start()
            for a in range(n):
                copy(a, dst, q ^ dst, c, me).wait_recv()
                copy(a, 3 + dst, q ^ dst, c, sib).start()

        pl.when(c == 1)(lambda: relay(1, 2))
        pl.when(c == 0)(lambda: relay(2, 1))
        for a in range(n):
            copy(a, 3, q ^ 3, c, me).wait_recv()
            copy(a, 6, q ^ 3, c, sib).start()
        for a in range(n):
            copy(a, 0, q, 1 - c, me).wait_recv()
            for rel in (1, 2, 3):
                copy(a, 3 + rel, q ^ rel, 1 - c, me).wait_recv()
        for a in range(n):
            for slot in range(3, 7):
                copy(a, slot, q, c, sib).wait_send()
        for cp in first:
            cp.wait_send()
        for cp in mine:
            cp.wait()

    return _Job(shards, [jax.ShapeDtypeStruct((N_DEV * s.shape[0], s.shape[1]), s.dtype) for s in shards],
                [_SEMS(7 * n), _SEMS(7 * n), _SEMS(n)], start, finish)


def _pair_job(grads):
    n = len(grads)

    def copies(ins, outs, sems):
        send_sems, recv_sems = sems
        x, y, c = _mesh_pos()
        out = []
        for a in range(n):
            rs = grads[a].shape[0] // N_DEV
            for q in range(4):
                blk = ins[a].at[pl.ds((2 * q + 1 - c) * rs, rs), :]
                out.append(pltpu.make_async_remote_copy(
                    src_ref=blk, dst_ref=outs[a].at[q], send_sem=send_sems.at[4 * a + q], recv_sem=recv_sems.at[4 * a + q],
                    device_id=(x, y, 1 - c), device_id_type=MESH))
        return out

    def start(ins, outs, sems):
        for cp in copies(ins, outs, sems):
            cp.start()

    def finish(ins, outs, sems):
        for cp in copies(ins, outs, sems):
            cp.wait()

    return _Job(grads, [jax.ShapeDtypeStruct((4, g.shape[0] // N_DEV, g.shape[1]), g.dtype) for g in grads],
                [_SEMS(4 * n), _SEMS(4 * n)], start, finish)


def _chip_job(sums, rels=(1, 2, 3)):
    n, nr = len(sums), len(rels)

    def copies(ins, outs, sems):
        send_sems, recv_sems = sems
        x, y, c = _mesh_pos()
        out = []
        for a in range(n):
            for slot, r in enumerate(rels):
                px, py = (1 - x if r & 2 else x), (1 - y if r & 1 else y)
                out.append(pltpu.make_async_remote_copy(
                    src_ref=ins[a].at[2 * px + py], dst_ref=outs[a].at[slot], send_sem=send_sems.at[nr * a + slot],
                    recv_sem=recv_sems.at[nr * a + slot], device_id=(px, py, c), device_id_type=MESH))
        return out

    def start(ins, outs, sems):
        for cp in copies(ins, outs, sems):
            cp.start()

    def finish(ins, outs, sems):
        for cp in copies(ins, outs, sems):
            cp.wait()

    return _Job(sums, [jax.ShapeDtypeStruct((nr,) + s.shape[1:], s.dtype) for s in sums],
                [_SEMS(nr * n), _SEMS(nr * n)], start, finish)


def _mm(name, form, a_list, b, M, N, K, tm, tn, tk, extras, outs, epi, job=None, acc_as_ref=False):
    nI, nJ, nK = M // tm, N // tn, K // tk
    assert nI * tm == M and nJ * tn == N and nK * tk == K
    dims = {"nn": NN, "nt": NT, "tn": TN}[form]
    b_list = b if isinstance(b, list) else [(b, {"nn": N, "nt": K, "tn": N}[form])]
    nA, nB = len(a_list), len(b_list)
    assert nA == 1 or nB == 1
    assert nB == 1 or form in ("nn", "nt")
    AXIS = {"i": 0, "j": 1, "k": 2}
    a_axis, a_tile = ("i", tm) if form == "tn" else ("k", tk)
    b_axis, b_tile = ("k", tk) if form == "nt" else ("j", tn)

    def cut(pieces, tile, total):
        starts, s = [], 0
        for _, w in pieces:
            assert w % tile == 0
            starts.append(s // tile)
            s += w
        assert s == total
        return starts, [w // tile for _, w in pieces]

    a_st, a_cn = cut(a_list, a_tile, M if form == "tn" else K)
    b_st, b_cn = cut(b_list, b_tile, K if form == "nt" else N)

    def inside(idx, st, cn):
        return (idx >= st) & (idx < st + cn)

    def a_spec(p):
        st, cn = a_st[p], a_cn[p]
        if form == "tn":
            return pl.BlockSpec((tk, tm), lambda i, j, k: (jnp.where(inside(i, st, cn), k, 0), jnp.clip(i - st, 0, cn - 1)))
        return pl.BlockSpec((tm, tk), lambda i, j, k: (i, jnp.clip(k - st, 0, cn - 1)))

    def b_spec(p):
        st, cn = b_st[p], b_cn[p]
        if form == "nt":
            return pl.BlockSpec((tn, tk), lambda i, j, k: (j, jnp.clip(k - st, 0, cn - 1)))
        if nB == 1:
            return pl.BlockSpec((tk, tn), lambda i, j, k: (k, j))
        return pl.BlockSpec((tk, tn), lambda i, j, k: (jnp.where(inside(j, st, cn), k, 0), jnp.clip(j - st, 0, cn - 1)))

    in_specs = ([a_spec(p) for p in range(nA)] + [b_spec(p) for p in range(nB)]
                + [pl.BlockSpec(bs, im) for _, bs, im in extras])
    out_shape = [jax.ShapeDtypeStruct(s_, d_) for s_, d_, _, _ in outs]
    out_specs = [pl.BlockSpec(bs, im) for _, _, bs, im in outs]
    nE, nO = len(extras), len(outs)
    single = nA == 1 and nB == 1

    def body(*refs):
        a_refs, b_refs = refs[:nA], refs[nA:nA + nB]
        ex, ou = refs[nA + nB:nA + nB + nE], refs[nA + nB + nE:nA + nB + nE + nO]
        ids = [pl.program_id(a) for a in range(3)]

        def partial_of(p, q):
            return lax.dot_general(a_refs[p][...], b_refs[q][...], dims, preferred_element_type=F32)

        if nK == 1 and single:
            epi(partial_of(0, 0), ex, ou)
            return
        acc = refs[-1]
        k = ids[2]
        for p in range(nA):
            for q in range(nB):
                def first(p=p, q=q):
                    acc[...] = partial_of(p, q)

                def later(p=p, q=q):
                    acc[...] += partial_of(p, q)

                here = None
                if nA > 1:
                    here = inside(ids[AXIS[a_axis]], a_st[p], a_cn[p])
                if nB > 1:
                    here = inside(ids[AXIS[b_axis]], b_st[q], b_cn[q])
                pl.when(k == 0 if here is None else here & (k == 0))(first)
                pl.when(k > 0 if here is None else here & (k > 0))(later)

        @pl.when(k == nK - 1)
        def _():
            epi(acc if acc_as_ref else acc[...], ex, ou)

    scratch = [] if (nK == 1 and single) else [pltpu.VMEM((tm, tn), F32)]
    res, job_res = _pcall(
        body, grid=(nI, nJ, nK), in_specs=in_specs, out_specs=out_specs, out_shape=out_shape, scratch_shapes=scratch,
        name=name, semantics=("parallel", "parallel", "arbitrary"),
        args=[a for a, _ in a_list] + [p for p, _ in b_list] + [e for e, _, _ in extras], job=job)
    return res if job is None else (res, job_res)


def _piece_tiles(pieces, tile):
    starts, s = [], 0
    for _, w in pieces:
        assert w % tile == 0
        starts.append(s // tile)
        s += w
    return starts, [w // tile for _, w in pieces], s


def _pieces_tn(name, pieces, b, tile, job=None):
    T, N = b.shape
    st, cn, M = _piece_tiles(pieces, tile)
    nP, nI = len(pieces), M // tile

    def body(*refs):
        p_refs, b_hbm, o_ref = refs[:nP], refs[nP], refs[nP + 1]
        bbuf, abuf, bsem, asem = refs[nP + 2:]
        i = pl.program_id(0)

        def fetch(step, slot):
            for p in range(nP):
                @pl.when((step >= st[p]) & (step < st[p] + cn[p]))
                def _():
                    col = pl.multiple_of((step - st[p]) * tile, tile)
                    pltpu.make_async_copy(p_refs[p].at[pl.ds(0, T), pl.ds(col, tile)], abuf.at[slot], asem.at[slot]).start()

        @pl.when(i == 0)
        def _():
            whole = pltpu.make_async_copy(b_hbm, bbuf, bsem)
            whole.start()
            fetch(0, 0)
            whole.wait()

        @pl.when(i + 1 < nI)
        def _():
            fetch(i + 1, (i + 1) % 2)

        pltpu.make_async_copy(p_refs[0].at[pl.ds(0, T), pl.ds(0, tile)], abuf.at[i % 2], asem.at[i % 2]).wait()
        o_ref[...] = lax.dot_general(abuf[i % 2], bbuf[...], TN, preferred_element_type=F32).astype(BF16)

    res, job_res = _pcall(
        body, grid=(nI,), in_specs=[_ANY] * (nP + 1), out_specs=[pl.BlockSpec((tile, N), lambda i: (i, 0))],
        out_shape=[jax.ShapeDtypeStruct((M, N), BF16)],
        scratch_shapes=[pltpu.VMEM((T, N), b.dtype), pltpu.VMEM((2, T, tile), b.dtype), pltpu.SemaphoreType.DMA, _SEMS(2)],
        name=name, semantics=("arbitrary",), args=[p for p, _ in pieces] + [b], job=job)
    return res if job is None else (res, job_res)


def _rows_mm(name, pieces, w, T, tm, tk, vecs, bufs, parts, epi, job=None):
    st, cn, K = _piece_tiles(pieces, tk)
    nP, nI, nK = len(pieces), T // tm, K // tk
    part_specs = [pl.BlockSpec(bs, lambda i, k, im=im: im(i, 0, k)) for _, _, bs, im in parts]
    n_vec, nB = len(vecs), len(bufs)
    load_ix = [n for n, (_, src, _) in enumerate(bufs) if src is not None]
    store_ix = [n for n, (_, _, store) in enumerate(bufs) if store]
    n_any_in, n_any_out = len(load_ix), len(store_ix)

    def body(*refs):
        o = nP
        p_refs, w_ref = refs[:nP], refs[o]
        vec_refs = refs[o + 1:o + 1 + n_vec]
        ins = refs[o + 1 + n_vec:o + 1 + n_vec + n_any_in]
        o = o + 1 + n_vec + n_any_in
        hbm_outs, p_outs = refs[o:o + n_any_out], refs[o + n_any_out:o + n_any_out + len(parts)]
        o = o + n_any_out + len(parts)
        acc, abuf = refs[o:o + 2]
        buf_refs = refs[o + 2:o + 2 + nB]
        asem, in_sems, out_sems = refs[-3:]
        i, k = pl.program_id(0), pl.program_id(1)
        g = i * nK + k
        rows_of = lambda ref, ii: ref.at[pl.ds(pl.multiple_of(ii * tm, tm), tm), :]
        bufs_in = [buf_refs[n] for n in load_ix]
        bufs_out = [buf_refs[n] for n in store_ix]

        def fetch(ii, kk, slot):
            for p in range(nP):
                @pl.when((kk >= st[p]) & (kk < st[p] + cn[p]))
                def _():
                    col = pl.multiple_of((kk - st[p]) * tk, tk)
                    src = p_refs[p].at[pl.ds(pl.multiple_of(ii * tm, tm), tm), pl.ds(col, tk)]
                    pltpu.make_async_copy(src, abuf.at[slot], asem.at[slot]).start()

        loads = lambda ii: [pltpu.make_async_copy(rows_of(src, ii), buf, in_sems.at[n])
                            for n, (src, buf) in enumerate(zip(ins, bufs_in))]
        stores = lambda ii: [pltpu.make_async_copy(buf, rows_of(dst, ii), out_sems.at[n])
                             for n, (buf, dst) in enumerate(zip(bufs_out, hbm_outs))]

        @pl.when(g == 0)
        def _():
            fetch(0, 0, 0)

        @pl.when(g + 1 < nI * nK)
        def _():
            last_k = k == nK - 1
            fetch(jnp.where(last_k, i + 1, i), jnp.where(last_k, 0, k + 1), (g + 1) % 2)

        @pl.when(k == 0)
        def _():
            @pl.when(i > 0)
            def _():
                for cp in stores(i - 1):
                    cp.wait()
            for cp in loads(i):
                cp.start()

        pltpu.make_async_copy(p_refs[0].at[pl.ds(0, tm), pl.ds(0, tk)], abuf.at[g % 2], asem.at[g % 2]).wait()

        def product(cols):
            return jnp.dot(abuf[g % 2], w_ref[:, cols], preferred_element_type=F32)

        col_blocks = [slice(c0, c0 + 512) for c0 in range(0, D, 512)]

        @pl.when(k == 0)
        def _():
            for cols in col_blocks:
                acc[:, cols] = product(cols)

        @pl.when(k > 0)
        def _():
            for cols in col_blocks:
                acc[:, cols] += product(cols)

        @pl.when(k == nK - 1)
        def _():
            for cp in loads(i):
                cp.wait()
            epi(acc, vec_refs, buf_refs, p_outs)
            for cp in stores(i):
                cp.start()

            @pl.when(i == nI - 1)
            def _():
                for cp in stores(i):
                    cp.wait()

    vec = pl.BlockSpec((1, D), lambda i, k: (0, 0))
    scratch = ([pltpu.VMEM((tm, D), F32), pltpu.VMEM((2, tm, tk), BF16)] + [pltpu.VMEM((tm, D), dt) for dt, _, _ in bufs]
               + [_SEMS(2), _SEMS(n_any_in), _SEMS(n_any_out)])
    res, job_res = _pcall(
        body, grid=(nI, nK),
        in_specs=[_ANY] * nP + [pl.BlockSpec((tk, D), lambda i, k: (k, 0))] + [vec] * n_vec + [_ANY] * n_any_in,
        out_specs=[_ANY] * n_any_out + part_specs,
        out_shape=([jax.ShapeDtypeStruct((T, D), bufs[n][0]) for n in store_ix]
                   + [jax.ShapeDtypeStruct(s, d) for s, d, _, _ in parts]),
        scratch_shapes=scratch, name=name, semantics=("arbitrary", "arbitrary"),
        args=[p for p, _ in pieces] + [w] + list(vecs) + [bufs[n][1] for n in load_ix], job=job)
    return res if job is None else (res, job_res)


def _pieces_nn_rms(name, pieces, w, x, gain, sc, dres, tm, tk, job=None):
    _, outs, epi = _rms_mod_bwd_epilogue(x, gain, sc, dres, tm)

    def on_rows(acc, vecs, bufs, parts):
        epi(acc, [bufs[0], vecs[0], vecs[1], bufs[1]], [bufs[1], *parts])

    return _rows_mm(name, pieces, w, x.shape[0], tm, tk, [gain, sc], [(F32, x, False), (F32, dres, True)],
                    outs[1:], on_rows, job=job)


def _rms_mod_fwd(name, x, gain, sc, sh, tr):
    T = x.shape[0]

    def body(x_ref, g_ref, sc_ref, sh_ref, h_ref):
        xv = x_ref[...]
        rstd = lax.rsqrt(jnp.mean(xv * xv, axis=-1, keepdims=True) + EPS)
        h_ref[...] = ((xv * rstd * g_ref[...]) * (1.0 + sc_ref[...]) + sh_ref[...]).astype(BF16)

    row = pl.BlockSpec((tr, D), lambda i: (i, 0))
    vec = pl.BlockSpec((1, D), lambda i: (0, 0))
    return pl.pallas_call(
        body, grid=(T // tr,), in_specs=[row, vec, vec, vec], out_specs=row,
        out_shape=jax.ShapeDtypeStruct((T, D), BF16), name=name, compiler_params=_params(("parallel",)),
    )(x, gain, sc, sh)


def _rms_mod_bwd_epilogue(x, gain, sc, dres, tm, gate=None, mo=None):
    T = x.shape[0]
    with_gate = gate is not None
    row = ((tm, D), lambda i, j, k: (i, 0))
    vec = ((1, D), lambda i, j, k: (0, 0))
    part = ((T // tm * 8, D), F32, (8, D), lambda i, j, k: (i, 0))
    extras = [(x, *row), (gain, *vec), (sc, *vec), (dres, *row)]
    outs = [((T, D), F32, *row), part, part, part]
    if with_gate:
        extras += [(gate, *vec), (mo, *row)]
        outs += [((T, D), BF16, *row), part]

    rows = min(64, tm)

    def epi(acc, ex, ou):
        g = ex[1][...]
        sums = [jnp.zeros((8, D), F32) for _ in range(4)]
        for r0 in range(0, tm, rows):
            rs = slice(r0, r0 + rows)
            dhv, xv = acc[rs, :], ex[0][rs, :]
            rstd = lax.rsqrt(jnp.mean(xv * xv, axis=-1, keepdims=True) + EPS)
            xhat = xv * rstd
            dn = dhv * (1.0 + ex[2][...])
            dxhat = dn * g
            dx = ex[3][rs, :] + rstd * (dxhat - xhat * jnp.mean(dxhat * xhat, axis=-1, keepdims=True))
            ou[0][rs, :] = dx
            terms = [dhv, dhv * (xhat * g), dn * xhat]
            if with_gate:
                terms.append(dx * ex[5][rs, :].astype(F32))
                ou[4][rs, :] = (ex[4][...] * dx).astype(BF16)
            sums = [s + _fold8(t) for s, t in zip(sums, terms)] + sums[len(terms):]
        ou[1][...], ou[2][...], ou[3][...] = sums[:3]
        if with_gate:
            ou[5][...] = sums[3]

    return extras, outs, epi


def _rms_mod_bwd(name, dh, x, gain, sc, dres, tr, gate=None, mo=None):
    T = x.shape[0]
    extras, outs, epi = _rms_mod_bwd_epilogue(x, gain, sc, dres, tr, gate, mo)
    rows_only = lambda im: (lambda i: im(i, 0, 0))
    nE = len(extras)

    def body(dh_ref, *refs):
        epi(dh_ref, refs[:nE], refs[nE:])

    return pl.pallas_call(
        body, grid=(T // tr,),
        in_specs=[pl.BlockSpec((tr, D), lambda i: (i, 0))] + [pl.BlockSpec(bs, rows_only(im)) for _, bs, im in extras],
        out_specs=[pl.BlockSpec(bs, rows_only(im)) for _, _, bs, im in outs],
        out_shape=[jax.ShapeDtypeStruct(s, d) for s, d, _, _ in outs], name=name, compiler_params=_params(("parallel",)),
    )(dh, *[e for e, _, _ in extras])


def _split3(v):
    h = v.astype(BF16)
    r1 = v - h.astype(F32)
    m = r1.astype(BF16)
    lo = (r1 - m.astype(F32)).astype(BF16)
    return h, m, lo


def _tri_mm(tri, v, dims=NN):
    h, m, lo = _split3(v)
    t = tri.astype(BF16)
    mm = lambda p: lax.dot_general(t, p, dims, preferred_element_type=F32)
    return (mm(lo) + mm(m)) + mm(h)


def _hgrn_chunk_terms(q, fl, lb):
    sig = _sigmoid(fl)
    f = lb + (1.0 - lb) * sig
    lf = jnp.log(f)
    kk = 1.0 - f
    sq = _sigmoid(q)
    qf = q * sq
    return sig, f, lf, kk, sq, qf


def _causal(n):
    r = lax.broadcasted_iota(jnp.int32, (n, n), 0)
    c = lax.broadcasted_iota(jnp.int32, (n, n), 1)
    return r >= c


def _hgrn_fwd(proj, lb_logits, o_gain, tt, job=None):
    T = proj.shape[0]
    nT, ncl = T // tt, tt // CHUNK
    C = CHUNK

    def body(q_ref, f_ref, i_ref, g_ref, lbl_ref, og_ref, y_ref, st_ref, S):
        @pl.when(pl.program_id(1) == 0)
        def _():
            S[...] = jnp.zeros_like(S)

        lbl = lbl_ref[...]
        lb = _sigmoid(lbl[0:1, :] - lbl[1:2, :])
        og = og_ref[...]
        shp = (ncl, C, A_HD)
        q, fl, v, g = (r[...].reshape(shp) for r in (q_ref, f_ref, i_ref, g_ref))
        tri = jnp.broadcast_to(_causal(C), (ncl, C, C))
        _, _, lf, kk, _, qf = _hgrn_chunk_terms(q, fl, lb)
        b = _tri_mm(tri, lf, BNN)
        bm, bl = b[:, C // 2 - 1:C // 2, :], b[:, C - 1:C, :]
        qd, kd = qf * jnp.exp(b - bm), kk * jnp.exp(bm - b)
        A = jnp.where(tri, _dot(qd, kd, BNT), 0.0)
        d_st = _dot(v, kk * jnp.exp(bl - b), BTN)
        dec = jnp.exp(bl)
        st = S[...]
        for ci in range(ncl):
            st_ref[0, ci] = st
            st = st * dec[ci] + d_st[ci]
        S[...] = st
        o = _dot(A, v, BNN) + _dot(qf * jnp.exp(b), st_ref[0], BNT)
        r = lax.rsqrt(jnp.mean(o * o, axis=-1, keepdims=True) + EPS)
        y_ref[...] = (o * r * og * (g * _sigmoid(g))).astype(BF16).reshape(tt, A_HD)

    def col(off):
        return pl.BlockSpec((tt, A_HD), lambda h, t: (t, off // A_HD + h))

    head_vec = lambda rows: pl.BlockSpec((rows, A_HD), lambda h, t: (0, h))
    return _pcall(
        body, grid=(A_HEADS, nT),
        in_specs=[col(OFF_QA), col(OFF_FA), col(OFF_IA), col(OFF_GA), head_vec(2), head_vec(1)],
        out_specs=[pl.BlockSpec((tt, A_HD), lambda h, t: (t, h)),
                   pl.BlockSpec((1, ncl, A_HD, A_HD), lambda h, t: (h, t, 0, 0))],
        out_shape=[jax.ShapeDtypeStruct((T, AW), BF16),
                   jax.ShapeDtypeStruct((A_HEADS, T // C, A_HD, A_HD), F32)],
        scratch_shapes=[pltpu.VMEM((A_HD, A_HD), F32)], name="hgrn_fwd", semantics=("parallel", "arbitrary"),
        args=[proj, proj, proj, proj, lb_logits, o_gain], job=job)


def _hgrn_bwd(proj, st, dy, lb_logits, o_gain, tt, job=None):
    T = proj.shape[0]
    nT, ncl = T // tt, tt // CHUNK
    C = CHUNK

    def body(q_ref, f_ref, i_ref, g_ref, st_ref, dy_ref, lbl_ref, og_ref,
             dq_ref, df_ref, di_ref, dg_ref, plb_ref, pog_ref, dS):
        @pl.when(pl.program_id(1) == 0)
        def _():
            dS[...] = jnp.zeros_like(dS)

        lbl = lbl_ref[...]
        lb = _sigmoid(lbl[0:1, :] - lbl[1:2, :])
        og = og_ref[...]
        shp = (ncl, C, A_HD)
        flat = lambda t: t.reshape(tt, A_HD)
        q, fl, v, g, dout = (r[...].reshape(shp) for r in (q_ref, f_ref, i_ref, g_ref, dy_ref))
        tri = jnp.broadcast_to(_causal(C), (ncl, C, C))
        rowi = lax.broadcasted_iota(jnp.int32, shp, 1)
        st0 = st_ref[0]
        sig, f, lf, kk, sq, qf = _hgrn_chunk_terms(q, fl, lb)
        b = _tri_mm(tri, lf, BNN)
        bm, bl = b[:, C // 2 - 1:C // 2, :], b[:, C - 1:C, :]
        e_qd, e_kd, e_ke, e_b = jnp.exp(b - bm), jnp.exp(bm - b), jnp.exp(bl - b), jnp.exp(b)
        qd, kd, ke, qe = qf * e_qd, kk * e_kd, kk * e_ke, qf * e_b
        dec = jnp.exp(bl)
        A = jnp.where(tri, _dot(qd, kd, BNT), 0.0)
        o = _dot(A, v, BNN) + _dot(qe, st0, BNT)
        r = lax.rsqrt(jnp.mean(o * o, axis=-1, keepdims=True) + EPS)
        sg = _sigmoid(g)
        on = o * r * og
        dg_ref[...] = flat((dout * on * (sg * (1.0 + g * (1.0 - sg)))).astype(BF16))
        don = dout * (g * sg)
        pog_ref[...] = _fold8(flat(don * o * r))
        dyh = don * og
        do = r * (dyh - o * (r * r) * jnp.mean(dyh * o, axis=-1, keepdims=True))
        g_st = _dot(do, qe, BTN)
        run = dS[...]
        after = [None] * ncl
        for ci in reversed(range(ncl)):
            after[ci] = run
            run = g_st[ci] + run * dec[ci]
        dS[...] = run
        d_after = jnp.stack(after, axis=0)
        ddec = jnp.sum(d_after * st0, axis=1, keepdims=True)
        dqe = _dot(do, st0, BNN)
        dke = _dot(v, d_after, BNN)
        dA = jnp.where(tri, _dot(do, v, BNT), 0.0)
        dv = _dot(ke, d_after, BNT) + _dot(A, do, BTN)
        dqd = _dot(dA, kd, BNN)
        dkd = _dot(dA, qd, BTN)
        di_ref[...] = flat(dv.astype(BF16))
        dqf = dqe * e_b + dqd * e_qd
        dkk = dkd * e_kd + dke * e_ke
        t_qd, t_kd, t_ke = dqd * qd, dkd * kd, dke * ke
        db = dqe * qe + t_qd - t_kd - t_ke
        dbm = jnp.sum(t_kd - t_qd, axis=1, keepdims=True)
        dbl = jnp.sum(t_ke, axis=1, keepdims=True) + ddec * dec
        db = db + jnp.where(rowi == C // 2 - 1, dbm, 0.0) + jnp.where(rowi == C - 1, dbl, 0.0)
        dlf = _tri_mm(tri, db, BTN)
        dfv = dlf / f - dkk
        df_ref[...] = flat((dfv * (1.0 - lb) * sig * (1.0 - sig)).astype(BF16))
        plb_ref[...] = _fold8(flat(dfv * (1.0 - sig)))
        dq_ref[...] = flat((dqf * (sq * (1.0 + q * (1.0 - sq)))).astype(BF16))

    def col(off):
        return pl.BlockSpec((tt, A_HD), lambda h, t: (nT - 1 - t, off // A_HD + h))

    head_vec = lambda rows: pl.BlockSpec((rows, A_HD), lambda h, t: (0, h))
    o_spec = pl.BlockSpec((tt, A_HD), lambda h, t: (nT - 1 - t, h))
    p_spec = pl.BlockSpec((8, A_HD), lambda h, t: (t, h))
    o_shape = jax.ShapeDtypeStruct((T, AW), BF16)
    p_shape = jax.ShapeDtypeStruct((nT * 8, AW), F32)
    return _pcall(
        body, grid=(A_HEADS, nT),
        in_specs=[col(OFF_QA), col(OFF_FA), col(OFF_IA), col(OFF_GA),
                  pl.BlockSpec((1, ncl, A_HD, A_HD), lambda h, t: (h, nT - 1 - t, 0, 0)),
                  pl.BlockSpec((tt, A_HD), lambda h, t: (nT - 1 - t, h)), head_vec(2), head_vec(1)],
        out_specs=[o_spec, o_spec, o_spec, o_spec, p_spec, p_spec],
        out_shape=[o_shape, o_shape, o_shape, o_shape, p_shape, p_shape],
        scratch_shapes=[pltpu.VMEM((A_HD, A_HD), F32)], name="hgrn_bwd", semantics=("parallel", "arbitrary"),
        args=[proj, proj, proj, proj, st, dy, lb_logits, o_gain], job=job)


LANES = 128
Q_COLS = BW // LANES


def _low_half():
    return lax.broadcasted_iota(jnp.int32, (1, LANES), 1) < B_HD


def _half_sum(t, low):
    lo = jnp.sum(jnp.where(low, t, 0.0), axis=-1, keepdims=True)
    hi = jnp.sum(jnp.where(low, 0.0, t), axis=-1, keepdims=True)
    return jnp.where(low, lo, hi)


def _half_rms(t, low):
    r = lax.rsqrt(_half_sum(t * t, low) * (1.0 / B_HD) + EPS)
    return t * r, r


def _fold_halves(p, low):
    return jnp.where(low, p + pltpu.roll(p, B_HD, 1), 0.0)


def _stack_cols(x):
    return jnp.stack([x[:, c * LANES:(c + 1) * LANES] for c in range(Q_COLS)], axis=0).reshape(KV_HEADS, 2 * BLK, LANES)


def _col_of(t, c):
    return t[c // 2, (c % 2) * BLK:(c % 2 + 1) * BLK]


def _split_halves(col, s, low):
    own = jnp.where(low if s == 0 else jnp.logical_not(low), col, 0.0)
    other = pltpu.roll(own, B_HD, 1)
    return (own, other) if s == 0 else (other, own)


def _swa_keys(kp_ref, kc_ref, vp_ref, vc_ref, kg, low):
    k_lo, k_hi, v_lo, v_hi, hats = [], [], [], [], []
    for j in range(KVW // LANES):
        cs = slice(j * LANES, (j + 1) * LANES)
        k_hat, k_r = _half_rms(jnp.concatenate([kp_ref[:, cs], kc_ref[:, cs]], axis=0), low)
        vcol = jnp.concatenate([vp_ref[:, cs], vc_ref[:, cs]], axis=0)
        hats.append((k_hat, k_r))
        for s in range(2):
            for dst_lo, dst_hi, col in ((k_lo, k_hi, k_hat * kg), (v_lo, v_hi, vcol)):
                lo, hi = _split_halves(col, s, low)
                dst_lo.append(lo)
                dst_hi.append(hi)
    st = lambda parts: jnp.stack(parts, axis=0)
    return st(k_lo), st(k_hi), st(v_lo), st(v_hi), hats


def _swa_mask(first_block):
    qi = lax.broadcasted_iota(jnp.int32, (BLK, 2 * BLK), 0) + BLK
    ki = lax.broadcasted_iota(jnp.int32, (BLK, 2 * BLK), 1)
    rel = qi - ki
    m = (rel >= 0) & (rel < BLK) & (jnp.logical_not(first_block) | (ki >= BLK))
    return jnp.concatenate([m, m], axis=0)


def _sink_cols(sk_ref, hi):
    top = lax.broadcasted_iota(jnp.int32, (2 * BLK, 1), 0) < BLK
    return jnp.stack([jnp.where(top, sk_ref[0, GROUP * hk + hi], sk_ref[0, GROUP * hk + 2 + hi])
                      for hk in range(KV_HEADS)], axis=0)


def _swa_probs(qn, k_half, sink, mask):
    s = jnp.where(mask, _dot(qn, k_half, BNT) * (B_HD ** -0.5), NEG)
    m = jnp.maximum(jnp.max(s, axis=-1, keepdims=True), sink)
    p = jnp.exp(s - m)
    ps = jnp.exp(sink - m)
    inv = 1.0 / (jnp.sum(p, axis=-1, keepdims=True) + ps)
    return p * inv, ps * inv


def _swa_fwd(proj, q_gain, k_gain, sinks, job=None):
    T = proj.shape[0]
    nb = T // BLK

    def body(q_ref, kc_ref, kp_ref, vc_ref, vp_ref, qg_ref, kg_ref, sk_ref, o_ref):
        low = _low_half()
        mask = _swa_mask(pl.program_id(0) == 0)
        qn = _half_rms(_stack_cols(q_ref[...]), low)[0] * qg_ref[...]
        k_lo, k_hi, v_lo, v_hi, _ = _swa_keys(kp_ref, kc_ref, vp_ref, vc_ref, kg_ref[...], low)
        p_lo, _ = _swa_probs(qn, k_lo, _sink_cols(sk_ref, 0), mask)
        p_hi, _ = _swa_probs(qn, k_hi, _sink_cols(sk_ref, 1), mask)
        o = (_dot(p_lo, v_lo, BNN) + _dot(p_hi, v_hi, BNN)).astype(BF16)
        for c in range(Q_COLS):
            o_ref[:, c * LANES:(c + 1) * LANES] = _col_of(o, c)

    q_gain, k_gain = jnp.tile(q_gain, (1, 2)), jnp.tile(k_gain, (1, 2))
    cur = lambda w, off: pl.BlockSpec((BLK, w), lambda i: (i, off // w))
    prev = lambda w, off: pl.BlockSpec((BLK, w), lambda i: (jnp.maximum(i - 1, 0), off // w))
    small = lambda n: pl.BlockSpec((1, 2 * n), lambda i: (0, 0))
    return _pcall(
        body, grid=(nb,),
        in_specs=[cur(BW, OFF_QB), cur(KVW, OFF_KB), prev(KVW, OFF_KB), cur(KVW, OFF_VB), prev(KVW, OFF_VB),
                  small(B_HD), small(B_HD), pl.BlockSpec(memory_space=pltpu.SMEM)],
        out_specs=[pl.BlockSpec((BLK, BW), lambda i: (i, 0))],
        out_shape=[jax.ShapeDtypeStruct((T, BW), BF16)], scratch_shapes=[], name="swa_fwd", semantics=("parallel",),
        args=[proj, proj, proj, proj, proj, q_gain, k_gain, sinks], job=job)


def _swa_bwd(proj, dout, q_gain, k_gain, sinks, job=None):
    T = proj.shape[0]
    nb = T // BLK
    W = BW + 2 * KVW

    def body(q_ref, kc_ref, kp_ref, vc_ref, vp_ref, do_ref, qg_ref, kg_ref, sk_ref,
             dq_ref, dkv_ref, pqg_ref, pkg_ref, psk_ref, dkn_c, dv_c):
        i = pl.program_id(0)
        live = i < nb
        low = _low_half()
        high = jnp.logical_not(low)
        qg, kg = qg_ref[...], kg_ref[...]
        mask = _swa_mask(i == 0)
        lane = lax.broadcasted_iota(jnp.int32, (1, LANES), 1)
        scale = B_HD ** -0.5

        @pl.when(i == 0)
        def _():
            dkn_c[...] = jnp.zeros_like(dkn_c)
            dv_c[...] = jnp.zeros_like(dv_c)

        q_hat, q_r = _half_rms(_stack_cols(q_ref[...]), low)
        qn = q_hat * qg
        k_lo, k_hi, v_lo, v_hi, hats = _swa_keys(kp_ref, kc_ref, vp_ref, vc_ref, kg, low)
        do = _stack_cols(do_ref[...])
        dqn = jnp.zeros((KV_HEADS, 2 * BLK, LANES), F32)
        acc_sk = jnp.zeros((1, LANES), F32)
        dk_parts, dv_parts = [], []
        for hi, (k_h, v_h) in enumerate(((k_lo, v_lo), (k_hi, v_hi))):
            p, ps = _swa_probs(qn, k_h, _sink_cols(sk_ref, hi), mask)
            dp = _dot(do, v_h, BNT)
            delta = jnp.sum(p * dp, axis=-1, keepdims=True)
            ds = p * (dp - delta) * scale
            dqn = dqn + _dot(ds, k_h, BNN)
            dk_parts.append(_dot(ds, qn, BTN))
            dv_parts.append(_dot(p, do, BTN))
            t = ps * delta
            for hk in range(KV_HEADS):
                for rows in range(2):
                    h = GROUP * hk + 2 * rows + hi
                    acc_sk = acc_sk + jnp.where(
                        lane == h, -jnp.sum(t[hk, rows * BLK:(rows + 1) * BLK], axis=0, keepdims=True), 0.0)
        dqh = dqn * qg
        dq = (q_r * (dqh - q_hat * (_half_sum(dqh * q_hat, low) * (1.0 / B_HD)))).astype(BF16)
        for c in range(Q_COLS):
            dq_ref[:, c * LANES:(c + 1) * LANES] = _col_of(dq, c)
        acc_qg = _fold_halves(_fold8((dqn * q_hat).reshape(KV_HEADS * 2 * BLK, LANES)), low)

        def native(parts, j):
            lo_arr, hi_arr = parts
            a, b = 2 * j, 2 * j + 1
            return (jnp.where(low, lo_arr[a], 0.0) + pltpu.roll(jnp.where(high, hi_arr[a], 0.0), B_HD, 1)
                    + jnp.where(high, hi_arr[b], 0.0) + pltpu.roll(jnp.where(low, lo_arr[b], 0.0), B_HD, 1))

        acc_kg = jnp.zeros((8, LANES), F32)
        for j in range(KVW // LANES):
            cs = slice(j * LANES, (j + 1) * LANES)
            dkn = jnp.where(live, native(dk_parts, j), 0.0)
            dvc = jnp.where(live, native(dv_parts, j), 0.0)
            kp_hat, kp_r = hats[j][0][:BLK], hats[j][1][:BLK]
            dkn_prev = dkn_c[:, cs] + dkn[:BLK]
            dv_prev = dv_c[:, cs] + dvc[:BLK]
            acc_kg = acc_kg + _fold8(dkn_prev * kp_hat)
            dkh = dkn_prev * kg
            dkv_ref[:, cs] = (kp_r * (dkh - kp_hat * (_half_sum(dkh * kp_hat, low) * (1.0 / B_HD)))).astype(BF16)
            dkv_ref[:, KVW + j * LANES:KVW + (j + 1) * LANES] = dv_prev.astype(BF16)
            dkn_c[:, cs] = dkn[BLK:]
            dv_c[:, cs] = dvc[BLK:]
        keep = jnp.where(i > 0, 1.0, 0.0)
        pqg_ref[...] = jnp.where(live, acc_qg, 0.0)
        pkg_ref[...] = _fold_halves(acc_kg, low) * keep
        psk_ref[...] = jnp.broadcast_to(jnp.where(live, acc_sk, 0.0), (8, LANES)) * (
            lax.broadcasted_iota(jnp.int32, (8, LANES), 0) == 0).astype(F32)

    q_gain, k_gain = jnp.tile(q_gain, (1, 2)), jnp.tile(k_gain, (1, 2))
    last = nb - 1
    cur = lambda w, off: pl.BlockSpec((BLK, w), lambda i: (jnp.minimum(i, last), off // w))
    prev = lambda w, off: pl.BlockSpec((BLK, w), lambda i: (jnp.maximum(i - 1, 0), off // w))
    small = lambda n: pl.BlockSpec((1, 2 * n), lambda i: (0, 0))
    part = pl.BlockSpec((8, 128), lambda i: (i, 0))
    p_shape = jax.ShapeDtypeStruct(((nb + 1) * 8, 128), F32)
    return _pcall(
        body, grid=(nb + 1,),
        in_specs=[cur(BW, OFF_QB), cur(KVW, OFF_KB), prev(KVW, OFF_KB), cur(KVW, OFF_VB), prev(KVW, OFF_VB),
                  pl.BlockSpec((BLK, BW), lambda i: (jnp.minimum(i, last), 0)), small(B_HD), small(B_HD),
                  pl.BlockSpec(memory_space=pltpu.SMEM)],
        out_specs=[pl.BlockSpec((BLK, BW), lambda i: (i, 0)),
                   pl.BlockSpec((BLK, 2 * KVW), lambda i: (jnp.maximum(i - 1, 0), 0)), part, part, part],
        out_shape=[jax.ShapeDtypeStruct((T + BLK, BW), BF16), jax.ShapeDtypeStruct((T, 2 * KVW), BF16),
                   p_shape, p_shape, p_shape],
        scratch_shapes=[pltpu.VMEM((BLK, KVW), F32), pltpu.VMEM((BLK, KVW), F32)], name="swa_bwd",
        semantics=("arbitrary",), args=[proj, proj, proj, proj, proj, dout, q_gain, k_gain, sinks], job=job)


def _branch_merge(ya_pre, attn, wa_t, wb_t, proj, tm, tn, job=None):
    T = ya_pre.shape[0]

    def body(a_ref, b_ref, wa_ref, wb_ref, ga_ref, gb_ref, ya_ref, yb_ref, mg_ref):
        ya = lax.dot_general(a_ref[...], wa_ref[...], NT, preferred_element_type=F32)
        yb = lax.dot_general(b_ref[...], wb_ref[...], NT, preferred_element_type=F32)
        ya_ref[...] = ya.astype(BF16)
        yb_ref[...] = yb.astype(BF16)
        mg_ref[...] = (_sigmoid(ga_ref[...]) * ya + _sigmoid(gb_ref[...]) * yb).astype(BF16)

    o_spec = pl.BlockSpec((tm, tn), lambda i, j: (i, j))
    o_shape = jax.ShapeDtypeStruct((T, D), BF16)
    return _pcall(
        body, grid=(T // tm, D // tn),
        in_specs=[pl.BlockSpec((tm, AW), lambda i, j: (i, 0)), pl.BlockSpec((tm, BW), lambda i, j: (i, 0)),
                  pl.BlockSpec((tn, AW), lambda i, j: (j, 0)), pl.BlockSpec((tn, BW), lambda i, j: (j, 0)),
                  pl.BlockSpec((tm, tn), lambda i, j: (i, OFF_GTA // tn + j)),
                  pl.BlockSpec((tm, tn), lambda i, j: (i, OFF_GTB // tn + j))],
        out_specs=[o_spec, o_spec, o_spec], out_shape=[o_shape, o_shape, o_shape], scratch_shapes=[], name="branch_merge",
        semantics=("parallel", "parallel"), args=[ya_pre, attn, wa_t, wb_t, proj, proj], job=job)


def _ij(i, j, k):
    return (i, j)


def _local_step(x, tgt, mod, g1, g2, lbl, og, qg, kg, sk, shards, me, c_arr):
    win_s, wa_s, wb_s, wout_s, wmi_s, wmo_s = shards
    T = x.shape[0]
    tm, tr, tt = min(1024, T), min(256, T), min(2048, T)
    tk_t = min(1024, T)
    tn = 512
    sh1, sc1, gt1, sh2, sc2, gt2 = (mod[:, i * D:(i + 1) * D] for i in range(N_MOD))
    nI = T // tm
    blk = (tm, tn)
    part = lambda: ((nI * 8, D), F32, (8, tn), _ij)
    vec_j = ((1, tn), lambda i, j, k: (0, j))

    h = _rms_mod_fwd("rms1_fwd", x, g1, sc1, sh1, tr)

    def epi_store(acc, ex, ou):
        ou[0][...] = acc.astype(ou[0].dtype)

    tm2 = min(2048, T)
    blk2 = (tm2, tn)

    full = lambda s: (0, s.shape[0])
    last = wmi_s.shape[0]
    (win_t,) = _run_job("gather_w_in", _gather_relay_job([win_s]))
    (proj,), (wa_t, wb_t, w_out, wmi_part) = _mm(
        "in_proj", "nt", [(h, D)], win_t, T, IN_W, D, tm2, tn, D, [], [((T, IN_W), F32, blk2, _ij)], epi_store,
        job=_gather_job([wa_s, wb_s, wout_s, wmi_s], rows=[full(wa_s), full(wb_s), full(wout_s), (0, MI_CUTS[0])]))
    (ya_pre, st), (wmi_part,) = _hgrn_fwd(
        proj, lbl, og, tt, job=_gather_job([wmi_s], rows=[MI_CUTS], into=[wmi_part]))
    (attn,), (wmi_t, wmo_part) = _swa_fwd(
        proj, qg, kg, sk, job=_gather_job([wmi_s, wmo_s], rows=[(MI_CUTS[1], last), (0, MO_CUT)], into=[wmi_part, None]))
    (ya, yb, merged), _ = _branch_merge(ya_pre, attn, wa_t, wb_t, proj, tm, tn)

    def epi_res1(acc, ex, ou):
        x_ref, gt_ref = ex
        ou[0][...] = acc.astype(BF16)
        ou[1][...] = x_ref[...] + gt_ref[...] * acc

    mo, x1 = _mm("out_proj", "nn", [(merged, D)], w_out, T, D, D, tm, tn, D, [(x, blk, _ij), (gt1, *vec_j)],
                 [((T, D), BF16, blk, _ij), ((T, D), F32, blk, _ij)], epi_res1)
    h2 = _rms_mod_fwd("rms2_fwd", x1, g2, sc2, sh2, tr)

    def epi_relu2(acc, ex, ou):
        r = jnp.maximum(acc, 0.0)
        ou[0][...] = r.astype(BF16)
        ou[1][...] = (r * r).astype(BF16)

    (r, a), (w_mo,) = _mm("mlp_in", "nt", [(h2, D)], wmi_t, T, HID, D, tm2, tn, D, [],
                          [((T, HID), BF16, blk2, _ij), ((T, HID), BF16, blk2, _ij)], epi_relu2,
                          job=_gather_job([wmo_s], rows=[(MO_CUT, last)], into=[wmo_part]))

    def loss_rows(acc, vecs, bufs, parts):
        gt = vecs[0][...]
        x1_buf, t_buf, dz_buf = bufs
        rows = min(64, tm)
        loss_sum, gate_sum = jnp.zeros((8, D), F32), jnp.zeros((8, D), F32)
        for r0 in range(0, tm, rows):
            rs = slice(r0, r0 + rows)
            z = acc[rs, :]
            e = x1_buf[rs, :] + gt * z - t_buf[rs, :]
            dy = e * (1.0 / D)
            t_buf[rs, :] = dy
            dz_buf[rs, :] = (gt * dy).astype(BF16)
            loss_sum = loss_sum + _fold8(e * e)
            gate_sum = gate_sum + _fold8(dy * z)
        parts[0][...] = loss_sum * (0.5 / D)
        parts[1][...] = gate_sum

    part_rows = ((nI * 8, D), F32, (8, D), lambda i, j, k: (i, 0))
    dy, dz, p_loss, p_gt2 = _rows_mm(
        "mlp_out", [(a, HID)], w_mo, T, tm, 1024, [gt2], [(F32, x1, False), (F32, tgt, True), (BF16, None, True)],
        [part_rows, part_rows], loss_rows)

    def epi_du(acc, ex, ou):
        ou[0][...] = (acc * (2.0 * ex[0][...].astype(F32))).astype(BF16)

    (du,) = _mm("mlp_out_dx", "nt", [(dz, D)], w_mo, T, HID, D, tm2, tn, D, [(r, blk2, _ij)],
                [((T, HID), BF16, blk2, _ij)], epi_du)
    gblk = (1024, 1024)
    gwide = (1024, D)
    pair_sum = lambda nm, g, r1: _pair_sum("pair_sum_" + nm, g, r1, c_arr, _sum_rows(r1.shape[1]))
    (g_mo,) = _mm("mlp_out_dw", "tn", [(a, HID)], dz, HID, D, T, 1024, D, tk_t, [], [((HID, D), BF16, gwide, _ij)], epi_store)
    (dh2,), (r1_mo,) = _mm("mlp_in_dx", "nn", [(du, HID)], wmi_t, T, D, HID, tm, D, 1024, [],
                           [((T, D), F32, (tm, D), _ij)], epi_store, job=_pair_job([g_mo]))
    dx1, p_sh2, p_sc2, p_g2, dmo, p_gt1 = _rms_mod_bwd("rms2_bwd", dh2, x1, g2, sc2, dy, tr, gate=gt1, mo=mo)
    s_mo = pair_sum("mlp_out", g_mo, r1_mo)
    near, far = (1, 2), (3,)
    (g_mi,), (rn_mo,) = _mm("mlp_in_dw", "tn", [(du, HID)], h2, HID, D, T, 1024, D, tk_t, [],
                            [((HID, D), BF16, gwide, _ij)], epi_store, job=_chip_job([s_mo], near))

    def epi_gates(acc, ex, ou):
        ya_ref, yb_ref, ga_ref, gb_ref = ex
        sa, sb = _sigmoid(ga_ref[...]), _sigmoid(gb_ref[...])
        ou[0][...] = (acc * sa).astype(BF16)
        ou[1][...] = (acc * sb).astype(BF16)
        ou[2][...] = (acc * ya_ref[...].astype(F32) * (sa * (1.0 - sa))).astype(BF16)
        ou[3][...] = (acc * yb_ref[...].astype(F32) * (sb * (1.0 - sb))).astype(BF16)

    o_bf = ((T, D), BF16, blk, _ij)
    (dya, dyb, dga, dgb), (rf_mo, r1_mi) = _mm(
        "out_proj_dx", "nt", [(dmo, D)], w_out, T, D, D, tm, tn, D,
        [(ya, blk, _ij), (yb, blk, _ij), (proj, blk, lambda i, j, k: (i, OFF_GTA // tn + j)),
         (proj, blk, lambda i, j, k: (i, OFF_GTB // tn + j))], [o_bf, o_bf, o_bf, o_bf], epi_gates,
        job=_both(_chip_job([s_mo], far), _pair_job([g_mi])))
    s_mi = pair_sum("mlp_in", g_mi, r1_mi)
    (g_out,) = _mm("out_proj_dw", "tn", [(merged, D)], dmo, D, D, T, 1024, 1024, tk_t, [], [((D, D), BF16, gblk, _ij)], epi_store)
    (dya_pre,) = _mm("branch_a_dx", "nn", [(dya, D)], wa_t, T, AW, D, tm, tn, D, [], [((T, AW), F32, blk, _ij)], epi_store)
    (dattn,) = _mm("branch_b_dx", "nn", [(dyb, D)], wb_t, T, BW, D, tm, tn, D, [], [((T, BW), F32, blk, _ij)], epi_store)
    (g_a,) = _mm("branch_a_dw", "tn", [(dya, D)], ya_pre, D, AW, T, 1024, 1024, tk_t, [], [((D, AW), BF16, gblk, _ij)], epi_store)
    (g_b,) = _mm("branch_b_dw", "tn", [(dyb, D)], attn, D, BW, T, 1024, 1024, tk_t, [], [((D, BW), BF16, gblk, _ij)], epi_store)
    (dqa, dfa, dia, dgg, p_lb, p_og), (rn_mi, r1_out, r1_a, r1_b) = _hgrn_bwd(
        proj, st, dya_pre, lbl, og, tt, job=_both(_chip_job([s_mi], near), _pair_job([g_out, g_a, g_b])))
    (dqb, dkv, p_qg, p_kg, p_sk), (rf_mi,) = _swa_bwd(proj, dattn, qg, kg, sk, job=_chip_job([s_mi], far))
    s_out, s_a, s_b = pair_sum("out", g_out, r1_out), pair_sum("branch_a", g_a, r1_a), pair_sum("branch_b", g_b, r1_b)
    pieces = [(dqa, AW), (dfa, AW), (dia, AW), (dgg, AW), (dqb, BW), (dkv, 2 * KVW), (dga, D), (dgb, D)]
    (g_in,), (r2_out, r2_a, r2_b) = _pieces_tn("in_proj_dw", pieces, h, 512, job=_chip_job([s_out, s_a, s_b]))
    (r1_in,) = _run_job("pair_w_in", _pair_job([g_in]))
    s_in = pair_sum("in", g_in, r1_in)
    (dx, p_sh1, p_sc1, p_g1), (r2_in,) = _pieces_nn_rms(
        "in_proj_dx", pieces, win_t, x, g1, sc1, dx1, tm, 512, job=_chip_job([s_in]))

    partials = dict(sh1=p_sh1, sc1=p_sc1, gt1=p_gt1, sh2=p_sh2, sc2=p_sc2, gt2=p_gt2, g1=p_g1, g2=p_g2,
                    lb=p_lb, og=p_og, qg=p_qg, kg=p_kg, sk=p_sk, loss=p_loss)
    sums = dict(w_in=(s_in, [r2_in]), w_branch_a=(s_a, [r2_a]), w_branch_b=(s_b, [r2_b]), w_out=(s_out, [r2_out]),
                w_mlp_in=(s_mi, [rn_mi, rf_mi]), w_mlp_out=(s_mo, [rn_mo, rf_mo]))
    return dx, sums, partials


def _exchange_slots(buf, send_sems, recv_sems):
    me = _mesh_pos()
    mine = buf.at[_index(me)]
    sends = []
    for k in range(1, N_DEV):
        cp = pltpu.make_async_remote_copy(src_ref=mine, dst_ref=mine, send_sem=send_sems.at[k - 1],
                                          recv_sem=recv_sems.at[k - 1], device_id=_flip(me, k), device_id_type=MESH)
        cp.start()
        sends.append(cp)
    for k in range(1, N_DEV):
        theirs = buf.at[_index(_flip(me, k))]
        pltpu.make_async_remote_copy(src_ref=theirs, dst_ref=theirs, send_sem=send_sems.at[k - 1],
                                     recv_sem=recv_sems.at[k - 1], device_id=_flip(me, k), device_id_type=MESH).wait_recv()
    for cp in sends:
        cp.wait_send()


ADA_W = N_MOD * D // N_DEV


def _ada_mod(c, w_ada, b_shard):
    def body(c_ref, w_ref, b_ref, mod_ref, sc_ref, cbuf, mbuf, s1, r1, s2, r2):
        me = _index(_mesh_pos())
        cbuf[me] = c_ref[...]
        _exchange_slots(cbuf, s1, r1)
        row = lax.broadcasted_iota(jnp.int32, (N_DEV, D), 0)
        call = jnp.zeros((N_DEV, D), F32)
        for d in range(N_DEV):
            call = jnp.where(row == d, cbuf[d], call)
        sc = call * _sigmoid(call)
        sc_ref[...] = sc
        mbuf[me] = _dot(sc, w_ref[...]) + b_ref[...]
        _exchange_slots(mbuf, s2, r2)
        for s in range(N_DEV):
            mod_ref[:, s * ADA_W:(s + 1) * ADA_W] = mbuf[s, pl.ds(me, 1), :]

    return pl.pallas_call(
        body, in_specs=[_VMEM, _VMEM, _VMEM], out_specs=[_VMEM, _VMEM],
        out_shape=[jax.ShapeDtypeStruct((1, N_MOD * D), F32), jax.ShapeDtypeStruct((N_DEV, D), F32)],
        scratch_shapes=[pltpu.VMEM((N_DEV, 1, D), F32), pltpu.VMEM((N_DEV, N_DEV, ADA_W), F32),
                        _SEMS(N_DEV - 1), _SEMS(N_DEV - 1), _SEMS(N_DEV - 1), _SEMS(N_DEV - 1)],
        name="ada_mod", compiler_params=pltpu.CompilerParams(vmem_limit_bytes=VMEM_LIMIT),
    )(c, w_ada, b_shard)


SMALL_SEGS = (("b_ada", N_MOD * D), ("norm1_gain", D), ("norm2_gain", D), ("lb0", AW), ("lb1", AW),
              ("hgrn_o_gain", AW), ("q_norm_gain", 128), ("k_norm_gain", 128), ("sinks", 128))
SMALL_W = sum(w for _, w in SMALL_SEGS)
X_SEGS = (("sh1", D), ("sc1", D), ("gt1", D), ("sh2", D), ("sc2", D), ("gt2", D), ("g1", D), ("g2", D),
          ("lb", AW), ("og", AW), ("qg", 128), ("kg", 128), ("sk", 128), ("loss", 128))
X_W = sum(w for _, w in X_SEGS)


def _offsets(segs):
    out, o = {}, 0
    for name, w in segs:
        out[name] = (o, w)
        o += w
    return out


def _small_reduce(parts, lb_logits):
    xo, so = _offsets(X_SEGS), _offsets(SMALL_SEGS)
    names = [nm for nm, _ in X_SEGS]

    def body(*refs):
        p_refs = dict(zip(names, refs[:len(names)]))
        lbl_ref, allx, gs_ref, loss_ref, send_sems, recv_sems = refs[len(names):]
        me = _index(_mesh_pos())
        for nm, (o, w) in xo.items():
            if nm == "loss":
                allx[me, :, o:o + w] = jnp.broadcast_to(jnp.sum(p_refs[nm][...]), (1, w))
            else:
                allx[me, :, o:o + w] = jnp.sum(p_refs[nm][...], axis=0, keepdims=True)
        _exchange_slots(allx, send_sems, recv_sems)
        tot = allx[0]
        for d in range(1, N_DEV):
            tot = tot + allx[d]
        seg = lambda nm: tot[:, xo[nm][0]:xo[nm][0] + xo[nm][1]]

        def put(nm, v):
            gs_ref[:, so[nm][0]:so[nm][0] + so[nm][1]] = v

        put("b_ada", tot[:, 0:N_MOD * D])
        put("norm1_gain", seg("g1"))
        put("norm2_gain", seg("g2"))
        lbl = lbl_ref[...]
        lb = _sigmoid(lbl[0:1, :] - lbl[1:2, :])
        dl0 = seg("lb") * lb * (1.0 - lb)
        put("lb0", dl0)
        put("lb1", -dl0)
        put("hgrn_o_gain", seg("og"))
        put("q_norm_gain", seg("qg"))
        put("k_norm_gain", seg("kg"))
        put("sinks", seg("sk"))
        loss_ref[...] = seg("loss")

    return pl.pallas_call(
        body, in_specs=[_VMEM] * (len(names) + 1), out_specs=[_VMEM, _VMEM, _VMEM],
        out_shape=[jax.ShapeDtypeStruct((N_DEV, 1, X_W), F32), jax.ShapeDtypeStruct((1, SMALL_W), F32),
                   jax.ShapeDtypeStruct((1, 128), F32)],
        scratch_shapes=[_SEMS(N_DEV - 1), _SEMS(N_DEV - 1)], name="small_reduce",
        compiler_params=pltpu.CompilerParams(vmem_limit_bytes=VMEM_LIMIT),
    )(*[parts[nm] for nm in names], lb_logits)


def _adamw_math(w, g, m, v):
    m = B1 * m + (1.0 - B1) * g
    v = B2 * v + (1.0 - B2) * (g * g)
    m_hat = m / (1.0 - B1 ** STEP)
    v_hat = v / (1.0 - B2 ** STEP)
    return -LR * (m_hat / (jnp.sqrt(v_hat) + ADAM_EPS) + WD * w), m, v


def _sum_rows(rs):
    return 256 if rs % 256 == 0 else rs // 2


def _pair_sum(name, g, recv, c_arr, tr):
    _, rs, cols = recv.shape
    blk = (1, tr, cols)

    def body(c_ref, g_ref, r_ref, o_ref):
        o_ref[...] = (g_ref[...].astype(F32) + r_ref[...].astype(F32)).astype(BF16)

    grid_spec = pltpu.PrefetchScalarGridSpec(
        num_scalar_prefetch=1, grid=(4, rs // tr),
        in_specs=[pl.BlockSpec(blk, lambda q, i, c: (2 * q + c[0], i, 0)), pl.BlockSpec(blk, lambda q, i, c: (q, i, 0))],
        out_specs=pl.BlockSpec(blk, lambda q, i, c: (q, i, 0)))
    return pl.pallas_call(body, grid_spec=grid_spec, out_shape=jax.ShapeDtypeStruct((4, rs, cols), BF16), name=name,
                          compiler_params=_params(("parallel", "parallel")))(c_arr, g.reshape(N_DEV, rs, cols), recv)


def _sum_adamw(name, sums, recvs, q_arr, w, m, v, transposed, tile):
    rows, cols = w.shape
    nR = len(recvs)

    def body(q_ref, s_ref, *refs):
        r_refs = refs[:nR]
        w_ref, m_ref, v_ref, g_ref, d_ref, nm_ref, nv_ref = refs[nR:]
        g = s_ref[0].astype(F32)
        for r_ref in r_refs:
            for slot in range(r_ref.shape[0]):
                g = g + r_ref[slot].astype(F32)
        g = g.T if transposed else g
        g_ref[...] = g
        d_ref[...], nm_ref[...], nv_ref[...] = _adamw_math(w_ref[...], g, m_ref[...], v_ref[...])

    if transposed:
        slab = lambda n, first: pl.BlockSpec((n, cols, tile), lambda i, q: (first(q), 0, i))
    else:
        slab = lambda n, first: pl.BlockSpec((n, tile, cols), lambda i, q: (first(q), i, 0))
    spec = pl.BlockSpec((tile, cols), lambda i, q: (i, 0))
    shape = jax.ShapeDtypeStruct((rows, cols), F32)
    grid_spec = pltpu.PrefetchScalarGridSpec(
        num_scalar_prefetch=1, grid=(rows // tile,),
        in_specs=[slab(1, lambda q: q[0])] + [slab(r.shape[0], lambda q: 0) for r in recvs] + [spec] * 3,
        out_specs=[spec] * 4)
    return pl.pallas_call(body, grid_spec=grid_spec, out_shape=[shape] * 4, name=name,
                          compiler_params=_params(("parallel",)))(q_arr, sums, *recvs, w, m, v)


def _adamw(name, w, g, m, v, tr):
    rows, cols = w.shape

    def body(w_ref, g_ref, m_ref, v_ref, d_ref, nm_ref, nv_ref):
        d_ref[...], nm_ref[...], nv_ref[...] = _adamw_math(w_ref[...], g_ref[...], m_ref[...], v_ref[...])

    spec = pl.BlockSpec((tr, cols), lambda i: (i, 0))
    shape = jax.ShapeDtypeStruct((rows, cols), F32)
    return pl.pallas_call(
        body, grid=(rows // tr,), in_specs=[spec] * 4, out_specs=[spec] * 3, out_shape=[shape] * 3, name=name,
        compiler_params=_params(("parallel",)),
    )(w, g, m, v)


def _ada_update(sc_t, dmod_cols, w, m, v, tr):
    rows, cols = w.shape

    def body(s_ref, d_ref, w_ref, m_ref, v_ref, g_ref, dl_ref, nm_ref, nv_ref):
        g = jnp.dot(s_ref[...], d_ref[...], precision=lax.Precision.HIGHEST, preferred_element_type=F32)
        g_ref[...] = g
        dl_ref[...], nm_ref[...], nv_ref[...] = _adamw_math(w_ref[...], g, m_ref[...], v_ref[...])

    spec = pl.BlockSpec((tr, cols), lambda i: (i, 0))
    shape = jax.ShapeDtypeStruct((rows, cols), F32)
    return pl.pallas_call(
        body, grid=(rows // tr,),
        in_specs=[pl.BlockSpec((tr, N_DEV), lambda i: (i, 0)), pl.BlockSpec((N_DEV, cols), lambda i: (0, 0)), spec, spec, spec],
        out_specs=[spec] * 4, out_shape=[shape] * 4, name="ada_update", compiler_params=_params(("parallel",)),
    )(sc_t, dmod_cols, w, m, v)


BIG = ("w_in", "w_branch_a", "w_branch_b", "w_out", "w_mlp_in", "w_mlp_out")
COLUMN_SHARDED = ("w_in", "w_branch_a", "w_branch_b", "w_mlp_in")
WEIGHTS = ("w_ada", "b_ada", "norm1_gain", "w_in", "lb_logits", "hgrn_o_gain", "q_norm_gain", "k_norm_gain", "sinks",
           "w_branch_a", "w_branch_b", "w_out", "norm2_gain", "w_mlp_in", "w_mlp_out")


def _pack_small(p):
    lb = p["lb_logits"]
    src = dict(p, lb0=lb[0:1], lb1=lb[1:2])
    return jnp.concatenate([jnp.pad(src[nm], ((0, 0), (0, w - src[nm].shape[1]))) for nm, w in SMALL_SEGS], axis=1)


def _unpack_small(vec, shapes):
    so = _offsets(SMALL_SEGS)
    out = {}
    for nm, shp in shapes.items():
        if nm == "lb_logits":
            o = so["lb0"][0]
            out[nm] = vec[0, o:o + 2 * AW].reshape(2, AW)
        else:
            o = so[nm][0]
            out[nm] = vec[:, o:o + shp[1]]
    return out


def kernel(x, c, w_ada, b_ada, norm1_gain, w_in, lb_logits, hgrn_o_gain, q_norm_gain, k_norm_gain, sinks, w_branch_a, w_branch_b, w_out, norm2_gain, w_mlp_in, w_mlp_out, loss_target, m_w_ada, m_b_ada, m_norm1_gain, m_w_in, m_lb_logits, m_hgrn_o_gain, m_q_norm_gain, m_k_norm_gain, m_sinks, m_w_branch_a, m_w_branch_b, m_w_out, m_norm2_gain, m_w_mlp_in, m_w_mlp_out, v_w_ada, v_b_ada, v_norm1_gain, v_w_in, v_lb_logits, v_hgrn_o_gain, v_q_norm_gain, v_k_norm_gain, v_sinks, v_w_branch_a, v_w_branch_b, v_w_out, v_norm2_gain, v_w_mlp_in, v_w_mlp_out):
    w = dict(w_ada=w_ada, b_ada=b_ada, norm1_gain=norm1_gain, w_in=w_in, lb_logits=lb_logits, hgrn_o_gain=hgrn_o_gain,
             q_norm_gain=q_norm_gain, k_norm_gain=k_norm_gain, sinks=sinks, w_branch_a=w_branch_a, w_branch_b=w_branch_b,
             w_out=w_out, norm2_gain=norm2_gain, w_mlp_in=w_mlp_in, w_mlp_out=w_mlp_out)
    m = dict(w_ada=m_w_ada, b_ada=m_b_ada, norm1_gain=m_norm1_gain, w_in=m_w_in, lb_logits=m_lb_logits,
             hgrn_o_gain=m_hgrn_o_gain, q_norm_gain=m_q_norm_gain, k_norm_gain=m_k_norm_gain, sinks=m_sinks,
             w_branch_a=m_w_branch_a, w_branch_b=m_w_branch_b, w_out=m_w_out, norm2_gain=m_norm2_gain,
             w_mlp_in=m_w_mlp_in, w_mlp_out=m_w_mlp_out)
    v = dict(w_ada=v_w_ada, b_ada=v_b_ada, norm1_gain=v_norm1_gain, w_in=v_w_in, lb_logits=v_lb_logits,
             hgrn_o_gain=v_hgrn_o_gain, q_norm_gain=v_q_norm_gain, k_norm_gain=v_k_norm_gain, sinks=v_sinks,
             w_branch_a=v_w_branch_a, w_branch_b=v_w_branch_b, w_out=v_w_out, norm2_gain=v_norm2_gain,
             w_mlp_in=v_w_mlp_in, w_mlp_out=v_w_mlp_out)
    for d in (w, m, v):
        for nm in ("w_ada",) + BIG:
            d[nm] = d[nm][0]
    px, py, pc = _mesh_pos()
    me = _index((px, py, pc))
    c_arr = jnp.reshape(pc, (1,)).astype(jnp.int32)
    q_arr = jnp.reshape(2 * px + py, (1,)).astype(jnp.int32)

    shards = [(w[nm].T if nm in COLUMN_SHARDED else w[nm]).astype(BF16) for nm in BIG]
    b_shard = lax.dynamic_slice(b_ada, (0, me * ADA_W), (1, ADA_W))
    mod, sc_all = _ada_mod(c, w["w_ada"], b_shard)

    dx, sums, parts = _local_step(x[0], loss_target[0], mod, norm1_gain, norm2_gain, lb_logits, hgrn_o_gain,
                                  q_norm_gain, k_norm_gain, sinks, shards, me, c_arr)

    allx, g_small, loss = _small_reduce(parts, lb_logits)

    grad, delta, new_m, new_v = {}, {}, {}, {}
    for nm in BIG:
        s, r2 = sums[nm]
        grad[nm], delta[nm], new_m[nm], new_v[nm] = _sum_adamw(
            "adamw_" + nm, s, r2, q_arr, w[nm], m[nm], v[nm], nm in COLUMN_SHARDED, 128)

    dmod_cols = lax.dynamic_slice(allx[:, 0, :], (0, me * ADA_W), (N_DEV, ADA_W))
    grad["w_ada"], delta["w_ada"], new_m["w_ada"], new_v["w_ada"] = _ada_update(
        sc_all.T, dmod_cols, w["w_ada"], m["w_ada"], v["w_ada"], 256)

    small_names = [nm for nm in WEIGHTS if nm not in BIG and nm != "w_ada"]
    shapes = {nm: w[nm].shape for nm in small_names}
    ds, ms, vs = _adamw("adamw_small", _pack_small(w), g_small, _pack_small(m), _pack_small(v), 1)
    for dst, vec in ((grad, g_small), (delta, ds), (new_m, ms), (new_v, vs)):
        dst.update(_unpack_small(vec, shapes))

    def full(d, nm):
        return d[nm][None] if nm in BIG or nm == "w_ada" else d[nm]

    return (loss[0, 0], dx[None], *[full(grad, nm) for nm in WEIGHTS], *[full(delta, nm) for nm in WEIGHTS],
            *[full(new_m, nm) for nm in WEIGHTS], *[full(new_v, nm) for nm in WEIGHTS])
```

```python
import functools

import jax
import jax.numpy as jnp
from jax import lax
from jax.experimental import pallas as pl
from jax.experimental.pallas import tpu as pltpu

F32 = jnp.float32
BF16 = jnp.bfloat16
MESH = pl.DeviceIdType.MESH

N_DEV = 8
D = 2048
A_HEADS, A_HD, CHUNK = 8, 128, 64
AW = A_HEADS * A_HD
Q_HEADS, KV_HEADS, GROUP, B_HD, BLK = 16, 4, 4, 64, 128
BW = Q_HEADS * B_HD
KVW = KV_HEADS * B_HD
HID = 4 * D
IN_W = 4 * AW + BW + 2 * KVW + 2 * D
OFF_QA, OFF_FA, OFF_IA, OFF_GA = 0, AW, 2 * AW, 3 * AW
OFF_QB = 4 * AW
OFF_KB = OFF_QB + BW
OFF_VB = OFF_KB + KVW
OFF_GTA = OFF_VB + KVW
OFF_GTB = OFF_GTA + D
N_MOD = 6
EPS = 1e-6
LR, B1, B2, ADAM_EPS, WD, STEP = 1e-3, 0.9, 0.999, 1e-8, 0.01, 10
NEG = -1e30

VMEM_LIMIT = 56 * 1024 * 1024
MI_CUTS = (544, 864)
MO_CUT = 272

NN = (((1,), (0,)), ((), ()))
NT = (((1,), (1,)), ((), ()))
TN = (((0,), (0,)), ((), ()))
BNN = (((2,), (1,)), ((0,), (0,)))
BNT = (((2,), (2,)), ((0,), (0,)))
BTN = (((1,), (1,)), ((0,), (0,)))


def _dot(a, b, dims=NN):
    return lax.dot_general(a.astype(BF16), b.astype(BF16), dims, preferred_element_type=F32)


def _params(sem):
    return pltpu.CompilerParams(dimension_semantics=sem, vmem_limit_bytes=VMEM_LIMIT)


def _sigmoid(x):
    return 1.0 / (1.0 + jnp.exp(-x))


def _fold8(v):
    r, n = v.shape
    return jnp.sum(v.reshape(r // 8, 8, n), axis=0)


_VMEM = pl.BlockSpec(memory_space=pltpu.VMEM)
_ANY = pl.BlockSpec(memory_space=pl.ANY)
_SEMS = lambda n: pltpu.SemaphoreType.DMA((n,))


def _mesh_pos():
    return lax.axis_index("x"), lax.axis_index("y"), lax.axis_index("c")


def _flip(pos, k):
    return tuple(1 - p if (k >> s) & 1 else p for p, s in zip(pos, (2, 1, 0)))


def _index(pos):
    return 4 * pos[0] + 2 * pos[1] + pos[2]


class _Job:
    def __init__(self, ins, out_shape, sems, start, finish, aliases=None):
        self.ins, self.out_shape, self.sems, self.start, self.finish = list(ins), list(out_shape), list(sems), start, finish
        self.aliases = dict(aliases or {})


def _both(j1, j2):
    assert not j1.aliases and not j2.aliases
    n_in, n_out, n_sem = len(j1.ins), len(j1.out_shape), len(j1.sems)
    first = lambda ins, outs, sems: (ins[:n_in], outs[:n_out], sems[:n_sem])
    second = lambda ins, outs, sems: (ins[n_in:], outs[n_out:], sems[n_sem:])

    def start(*refs):
        j1.start(*first(*refs))
        j2.start(*second(*refs))

    def finish(*refs):
        j1.finish(*first(*refs))
        j2.finish(*second(*refs))

    return _Job(j1.ins + j2.ins, j1.out_shape + j2.out_shape, j1.sems + j2.sems, start, finish)


def _pcall(body, *, grid, in_specs, out_specs, out_shape, scratch_shapes, name, semantics, args, job=None):
    if job is None:
        outs = pl.pallas_call(body, grid=grid, in_specs=in_specs, out_specs=out_specs, out_shape=out_shape,
                              scratch_shapes=scratch_shapes, name=name, compiler_params=_params(semantics))(*args)
        return list(outs), []
    n_in, n_out, n_scr = len(in_specs), len(out_specs), len(scratch_shapes)
    j_in, j_out = len(job.ins), len(job.out_shape)
    steps = tuple(grid)

    def carrier(*refs):
        o = 0
        main_in, o = refs[o:o + n_in], o + n_in
        job_in, o = refs[o:o + j_in], o + j_in
        main_out, o = refs[o:o + n_out], o + n_out
        job_out, o = refs[o:o + j_out], o + j_out
        main_scr, job_sems = refs[o:o + n_scr], refs[o + n_scr:]
        ids = [pl.program_id(a) for a in range(len(steps))]
        first = functools.reduce(lambda p, q: p & q, [i == 0 for i in ids])
        last = functools.reduce(lambda p, q: p & q, [i == s - 1 for i, s in zip(ids, steps)])

        @pl.when(first)
        def _():
            job.start(job_in, job_out, job_sems)

        body(*main_in, *main_out, *main_scr)

        @pl.when(last)
        def _():
            job.finish(job_in, job_out, job_sems)

    outs = pl.pallas_call(
        carrier, grid=grid, in_specs=list(in_specs) + [_ANY] * j_in, out_specs=list(out_specs) + [_ANY] * j_out,
        out_shape=list(out_shape) + job.out_shape, scratch_shapes=list(scratch_shapes) + job.sems, name=name,
        input_output_aliases={n_in + i: n_out + o for i, o in job.aliases.items()},
        compiler_params=_params(("arbitrary",) * len(steps)),
    )(*args, *job.ins)
    return list(outs[:n_out]), list(outs[n_out:])


def _run_job(name, job):
    j_in, j_out = len(job.ins), len(job.out_shape)

    def body(*refs):
        ins, outs, sems = refs[:j_in], refs[j_in:j_in + j_out], refs[j_in + j_out:]
        job.start(ins, outs, sems)
        job.finish(ins, outs, sems)

    return list(pl.pallas_call(body, in_specs=[_ANY] * j_in, out_specs=[_ANY] * j_out, out_shape=job.out_shape,
                               scratch_shapes=job.sems, name=name,
                               input_output_aliases=job.aliases)(*job.ins))


def _gather_job(shards, rows=None, into=None):
    n = len(shards)
    rows = rows or [(0, s.shape[0]) for s in shards]
    into = into or [None] * n
    olds, aliases = [], {}
    for a, buf in enumerate(into):
        if buf is not None:
            aliases[n + len(olds)] = a
            olds.append(buf)

    def copies(ins, outs, sems):
        send_sems, recv_sems, local_sems = sems
        x, y, c = _mesh_pos()
        me, sib = (x, y, c), (x, y, 1 - c)
        chips = [(1 - x, y), (x, 1 - y), (1 - x, 1 - y)]

        def part(a, p):
            rs, (r0, r1) = shards[a].shape[0], rows[a]
            return outs[a].at[pl.ds(_index(p) * rs + r0, r1 - r0), :]

        own = lambda a: ins[a].at[pl.ds(rows[a][0], rows[a][1] - rows[a][0]), :]

        def copy(a, k, block, to, src=None):
            return pltpu.make_async_remote_copy(
                src_ref=part(a, block) if src is None else src, dst_ref=part(a, block),
                send_sem=send_sems.at[7 * a + k], recv_sem=recv_sems.at[7 * a + k], device_id=to, device_id_type=MESH)

        mine = [pltpu.make_async_copy(own(a), part(a, me), local_sems.at[a]) for a in range(n)]
        first = []
        for a in range(n):
            first.append(copy(a, 0, me, sib, src=own(a)))
            first += [copy(a, 1 + j, me, (*chip, c), src=own(a)) for j, chip in enumerate(chips)]
        return me, sib, c, chips, copy, mine, first

    def start(ins, outs, sems):
        *_, mine, first = copies(ins, outs, sems)
        for cp in mine + first:
            cp.start()

    def finish(ins, outs, sems):
        me, sib, c, chips, copy, mine, first = copies(ins, outs, sems)
        passed = []
        for j, chip in enumerate(chips):
            for a in range(n):
                copy(a, 1 + j, (*chip, c), me).wait_recv()
                cp = copy(a, 4 + j, (*chip, c), sib)
                cp.start()
                passed.append(cp)
        for a in range(n):
            copy(a, 0, sib, me).wait_recv()
            for j, chip in enumerate(chips):
                copy(a, 4 + j, (*chip, 1 - c), me).wait_recv()
        for cp in first + passed:
            cp.wait_send()
        for cp in mine:
            cp.wait()

    return _Job(list(shards) + olds, [jax.ShapeDtypeStruct((N_DEV * s.shape[0], s.shape[1]), s.dtype) for s in shards],
                [_SEMS(7 * n), _SEMS(7 * n), _SEMS(n)], start, finish, aliases)


def _gather_relay_job(shards):
    n = len(shards)

    def tools(ins, outs, sems):
        send_sems, recv_sems, local_sems = sems
        x, y, c = _mesh_pos()
        q = 2 * x + y
        chip_at = lambda rel: (1 - x if rel & 2 else x, 1 - y if rel & 1 else y)

        def rows(a, chip, core):
            rs = shards[a].shape[0]
            return outs[a].at[pl.ds((2 * chip + core) * rs, rs), :]

        def copy(a, slot, chip, core, to, src=None):
            blk = rows(a, chip, core)
            return pltpu.make_async_remote_copy(src_ref=blk if src is None else src, dst_ref=blk,
                                                send_sem=send_sems.at[7 * a + slot], recv_sem=recv_sems.at[7 * a + slot],
                                                device_id=to, device_id_type=MESH)

        mine = [pltpu.make_async_copy(ins[a], rows(a, q, c), local_sems.at[a]) for a in range(n)]
        first = [copy(a, slot, q, c, (x, y, 1 - c) if slot == 0 else (*chip_at(slot), c), src=ins[a])
                 for a in range(n) for slot in (0, 1, 2)]
        return x, y, c, q, chip_at, copy, mine, first

    def start(ins, outs, sems):
        *_, mine, first = tools(ins, outs, sems)
        for cp in mine + first:
            cp.start()

    def finish(ins, outs, sems):
        x, y, c, q, chip_at, copy, mine, first = tools(ins, outs, sems)
        me, sib = (x, y, c), (x, y, 1 - c)

        def relay(src, dst):
            for a in range(n):
                copy(a, src, q ^ src, c, me).wait_recv()
                copy(a, 3, q ^ src, c, (*chip_at(dst), c)).start()
                copy(a, 3 + src, q ^ src, c, sib).start()
            for a in range(n):
                copy(a, dst, q ^ dst, c, me).wait_recv()
                copy(a, 3 + dst, q ^ dst, c, sib).start()

        pl.when(c == 1)(lambda: relay(1, 2))
        pl.when(c == 0)(lambda: relay(2, 1))
        for a in range(n):
            copy(a, 3, q ^ 3, c, me).wait_recv()
            copy(a, 6, q ^ 3, c, sib).start()
        for a in range(n):
            copy(a, 0, q, 1 - c, me).wait_recv()
            for rel in (1, 2, 3):
                copy(a, 3 + rel, q ^ rel, 1 - c, me).wait_recv()
        for a in range(n):
            for slot in range(3, 7):
                copy(a, slot, q, c, sib).wait_send()
        for cp in first:
            cp.wait_send()
        for cp in mine:
            cp.wait()

    return _Job(shards, [jax.ShapeDtypeStruct((N_DEV * s.shape[0], s.shape[1]), s.dtype) for s in shards],
                [_SEMS(7 * n), _SEMS(7 * n), _SEMS(n)], start, finish)


def _pair_job(grads):
    n = len(grads)

    def copies(ins, outs, sems):
        send_sems, recv_sems = sems
        x, y, c = _mesh_pos()
        out = []
        for a in range(n):
            rs = grads[a].shape[0] // N_DEV
            for q in range(4):
                blk = ins[a].at[pl.ds((2 * q + 1 - c) * rs, rs), :]
                out.append(pltpu.make_async_remote_copy(
                    src_ref=blk, dst_ref=outs[a].at[q], send_sem=send_sems.at[4 * a + q], recv_sem=recv_sems.at[4 * a + q],
                    device_id=(x, y, 1 - c), device_id_type=MESH))
        return out

    def start(ins, outs, sems):
        for cp in copies(ins, outs, sems):
            cp.start()

    def finish(ins, outs, sems):
        for cp in copies(ins, outs, sems):
            cp.wait()

    return _Job(grads, [jax.ShapeDtypeStruct((4, g.shape[0] // N_DEV, g.shape[1]), g.dtype) for g in grads],
                [_SEMS(4 * n), _SEMS(4 * n)], start, finish)


def _chip_job(sums, rels=(1, 2, 3)):
    n, nr = len(sums), len(rels)

    def copies(ins, outs, sems):
        send_sems, recv_sems = sems
        x, y, c = _mesh_pos()
        out = []
        for a in range(n):
            for slot, r in enumerate(rels):
                px, py = (1 - x if r & 2 else x), (1 - y if r & 1 else y)
                out.append(pltpu.make_async_remote_copy(
                    src_ref=ins[a].at[2 * px + py], dst_ref=outs[a].at[slot], send_sem=send_sems.at[nr * a + slot],
                    recv_sem=recv_sems.at[nr * a + slot], device_id=(px, py, c), device_id_type=MESH))
        return out

    def start(ins, outs, sems):
        for cp in copies(ins, outs, sems):
            cp.start()

    def finish(ins, outs, sems):
        for cp in copies(ins, outs, sems):
            cp.wait()

    return _Job(sums, [jax.ShapeDtypeStruct((nr,) + s.shape[1:], s.dtype) for s in sums],
                [_SEMS(nr * n), _SEMS(nr * n)], start, finish)


def _mm(name, form, a_list, b, M, N, K, tm, tn, tk, extras, outs, epi, job=None):
    nI, nJ, nK = M // tm, N // tn, K // tk
    assert nI * tm == M and nJ * tn == N and nK * tk == K
    dims = {"nn": NN, "nt": NT, "tn": TN}[form]
    b_list = b if isinstance(b, list) else [(b, {"nn": N, "nt": K, "tn": N}[form])]
    nA, nB = len(a_list), len(b_list)
    assert nA == 1 or nB == 1
    assert nB == 1 or form in ("nn", "nt")
    AXIS = {"i": 0, "j": 1, "k": 2}
    a_axis, a_tile = ("i", tm) if form == "tn" else ("k", tk)
    b_axis, b_tile = ("k", tk) if form == "nt" else ("j", tn)

    def cut(pieces, tile, total):
        starts, s = [], 0
        for _, w in pieces:
            assert w % tile == 0
            starts.append(s // tile)
            s += w
        assert s == total
        return starts, [w // tile for _, w in pieces]

    a_st, a_cn = cut(a_list, a_tile, M if form == "tn" else K)
    b_st, b_cn = cut(b_list, b_tile, K if form == "nt" else N)

    def inside(idx, st, cn):
        return (idx >= st) & (idx < st + cn)

    def a_spec(p):
        st, cn = a_st[p], a_cn[p]
        if form == "tn":
            return pl.BlockSpec((tk, tm), lambda i, j, k: (jnp.where(inside(i, st, cn), k, 0), jnp.clip(i - st, 0, cn - 1)))
        return pl.BlockSpec((tm, tk), lambda i, j, k: (i, jnp.clip(k - st, 0, cn - 1)))

    def b_spec(p):
        st, cn = b_st[p], b_cn[p]
        if form == "nt":
            return pl.BlockSpec((tn, tk), lambda i, j, k: (j, jnp.clip(k - st, 0, cn - 1)))
        if nB == 1:
            return pl.BlockSpec((tk, tn), lambda i, j, k: (k, j))
        return pl.BlockSpec((tk, tn), lambda i, j, k: (jnp.where(inside(j, st, cn), k, 0), jnp.clip(j - st, 0, cn - 1)))

    in_specs = ([a_spec(p) for p in range(nA)] + [b_spec(p) for p in range(nB)]
                + [pl.BlockSpec(bs, im) for _, bs, im in extras])
    out_shape = [jax.ShapeDtypeStruct(s_, d_) for s_, d_, _, _ in outs]
    out_specs = [pl.BlockSpec(bs, im) for _, _, bs, im in outs]
    nE, nO = len(extras), len(outs)
    single = nA == 1 and nB == 1

    def body(*refs):
        a_refs, b_refs = refs[:nA], refs[nA:nA + nB]
        ex, ou = refs[nA + nB:nA + nB + nE], refs[nA + nB + nE:nA + nB + nE + nO]
        ids = [pl.program_id(a) for a in range(3)]

        def partial_of(p, q):
            return lax.dot_general(a_refs[p][...], b_refs[q][...], dims, preferred_element_type=F32)

        if nK == 1 and single:
            epi(partial_of(0, 0), ex, ou)
            return
        acc = refs[-1]
        k = ids[2]
        for p in range(nA):
            for q in range(nB):
                def first(p=p, q=q):
                    acc[...] = partial_of(p, q)

                def later(p=p, q=q):
                    acc[...] += partial_of(p, q)

                here = None
                if nA > 1:
                    here = inside(ids[AXIS[a_axis]], a_st[p], a_cn[p])
                if nB > 1:
                    here = inside(ids[AXIS[b_axis]], b_st[q], b_cn[q])
                pl.when(k == 0 if here is None else here & (k == 0))(first)
                pl.when(k > 0 if here is None else here & (k > 0))(later)

        @pl.when(k == nK - 1)
        def _():
            epi(acc[...], ex, ou)

    scratch = [] if (nK == 1 and single) else [pltpu.VMEM((tm, tn), F32)]
    res, job_res = _pcall(
        body, grid=(nI, nJ, nK), in_specs=in_specs, out_specs=out_specs, out_shape=out_shape, scratch_shapes=scratch,
        name=name, semantics=("parallel", "parallel", "arbitrary"),
        args=[a for a, _ in a_list] + [p for p, _ in b_list] + [e for e, _, _ in extras], job=job)
    return res if job is None else (res, job_res)


def _twin_mm(name, form, pairs, M, N, K, tm, tn, tk, out_dtype):
    nI, nJ, nK = M // tm, N // tn, K // tk
    dims = {"nn": NN, "tn": TN}[form]
    a_spec = (pl.BlockSpec((tm, tk), lambda i, j, k: (i, k)) if form == "nn" else pl.BlockSpec((tk, tm), lambda i, j, k: (k, i)))
    b_spec = pl.BlockSpec((tk, tn), lambda i, j, k: (k, j))
    o_spec = pl.BlockSpec((tm, tn), lambda i, j, k: (i, j))

    def body(a1, b1, a2, b2, o1, o2, *accs):
        k = pl.program_id(2)
        for a_ref, b_ref, o_ref, acc in ((a1, b1, o1, accs[0] if accs else None), (a2, b2, o2, accs[1] if accs else None)):
            part = lax.dot_general(a_ref[...], b_ref[...], dims, preferred_element_type=F32)
            if nK == 1:
                o_ref[...] = part.astype(out_dtype)
                continue

            @pl.when(k == 0)
            def _(acc=acc, part=part):
                acc[...] = part

            @pl.when(k > 0)
            def _(acc=acc, part=part):
                acc[...] += part

            @pl.when(k == nK - 1)
            def _(acc=acc, o_ref=o_ref):
                o_ref[...] = acc[...].astype(out_dtype)

    (a1, b1), (a2, b2) = pairs
    shape = jax.ShapeDtypeStruct((M, N), out_dtype)
    return pl.pallas_call(
        body, grid=(nI, nJ, nK), in_specs=[a_spec, b_spec, a_spec, b_spec], out_specs=[o_spec, o_spec],
        out_shape=[shape, shape], scratch_shapes=[] if nK == 1 else [pltpu.VMEM((tm, tn), F32)] * 2, name=name,
        compiler_params=_params(("parallel", "parallel", "arbitrary")))(a1, b1, a2, b2)


def _piece_tiles(pieces, tile):
    starts, s = [], 0
    for _, w in pieces:
        assert w % tile == 0
        starts.append(s // tile)
        s += w
    return starts, [w // tile for _, w in pieces], s


def _pieces_tn(name, pieces, b, tile, job=None):
    T, N = b.shape
    st, cn, M = _piece_tiles(pieces, tile)
    nP, nI = len(pieces), M // tile

    def body(*refs):
        p_refs, b_hbm, o_ref = refs[:nP], refs[nP], refs[nP + 1]
        bbuf, abuf, bsem, asem = refs[nP + 2:]
        i = pl.program_id(0)

        def fetch(step, slot):
            for p in range(nP):
                @pl.when((step >= st[p]) & (step < st[p] + cn[p]))
                def _():
                    col = pl.multiple_of((step - st[p]) * tile, tile)
                    pltpu.make_async_copy(p_refs[p].at[pl.ds(0, T), pl.ds(col, tile)], abuf.at[slot], asem.at[slot]).start()

        @pl.when(i == 0)
        def _():
            whole = pltpu.make_async_copy(b_hbm, bbuf, bsem)
            whole.start()
            fetch(0, 0)
            whole.wait()

        @pl.when(i + 1 < nI)
        def _():
            fetch(i + 1, (i + 1) % 2)

        pltpu.make_async_copy(p_refs[0].at[pl.ds(0, T), pl.ds(0, tile)], abuf.at[i % 2], asem.at[i % 2]).wait()
        o_ref[...] = lax.dot_general(abuf[i % 2], bbuf[...], TN, preferred_element_type=F32).astype(BF16)

    res, job_res = _pcall(
        body, grid=(nI,), in_specs=[_ANY] * (nP + 1), out_specs=[pl.BlockSpec((tile, N), lambda i: (i, 0))],
        out_shape=[jax.ShapeDtypeStruct((M, N), BF16)],
        scratch_shapes=[pltpu.VMEM((T, N), b.dtype), pltpu.VMEM((2, T, tile), b.dtype), pltpu.SemaphoreType.DMA, _SEMS(2)],
        name=name, semantics=("arbitrary",), args=[p for p, _ in pieces] + [b], job=job)
    return res if job is None else (res, job_res)


def _rows_mm(name, pieces, w, T, tm, tk, vecs, bufs, parts, epi, job=None):
    st, cn, K = _piece_tiles(pieces, tk)
    nP, nI, nK = len(pieces), T // tm, K // tk
    part_specs = [pl.BlockSpec(bs, lambda i, k, im=im: im(i, 0, k)) for _, _, bs, im in parts]
    n_vec, nB = len(vecs), len(bufs)
    load_ix = [n for n, (_, src, _) in enumerate(bufs) if src is not None]
    store_ix = [n for n, (_, _, store) in enumerate(bufs) if store]
    n_any_in, n_any_out = len(load_ix), len(store_ix)

    def body(*refs):
        o = nP
        p_refs, w_ref = refs[:nP], refs[o]
        vec_refs = refs[o + 1:o + 1 + n_vec]
        ins = refs[o + 1 + n_vec:o + 1 + n_vec + n_any_in]
        o = o + 1 + n_vec + n_any_in
        hbm_outs, p_outs = refs[o:o + n_any_out], refs[o + n_any_out:o + n_any_out + len(parts)]
        o = o + n_any_out + len(parts)
        acc, abuf = refs[o:o + 2]
        buf_refs = refs[o + 2:o + 2 + nB]
        asem, in_sems, out_sems = refs[-3:]
        i, k = pl.program_id(0), pl.program_id(1)
        g = i * nK + k
        rows_of = lambda ref, ii: ref.at[pl.ds(pl.multiple_of(ii * tm, tm), tm), :]
        bufs_in = [buf_refs[n] for n in load_ix]
        bufs_out = [buf_refs[n] for n in store_ix]

        def fetch(ii, kk, slot):
            for p in range(nP):
                @pl.when((kk >= st[p]) & (kk < st[p] + cn[p]))
                def _():
                    col = pl.multiple_of((kk - st[p]) * tk, tk)
                    src = p_refs[p].at[pl.ds(pl.multiple_of(ii * tm, tm), tm), pl.ds(col, tk)]
                    pltpu.make_async_copy(src, abuf.at[slot], asem.at[slot]).start()

        loads = lambda ii: [pltpu.make_async_copy(rows_of(src, ii), buf, in_sems.at[n])
                            for n, (src, buf) in enumerate(zip(ins, bufs_in))]
        stores = lambda ii: [pltpu.make_async_copy(buf, rows_of(dst, ii), out_sems.at[n])
                             for n, (buf, dst) in enumerate(zip(bufs_out, hbm_outs))]

        @pl.when(g == 0)
        def _():
            fetch(0, 0, 0)

        @pl.when(g + 1 < nI * nK)
        def _():
            last_k = k == nK - 1
            fetch(jnp.where(last_k, i + 1, i), jnp.where(last_k, 0, k + 1), (g + 1) % 2)

        @pl.when(k == 0)
        def _():
            @pl.when(i > 0)
            def _():
                for cp in stores(i - 1):
                    cp.wait()
            for cp in loads(i):
                cp.start()

        pltpu.make_async_copy(p_refs[0].at[pl.ds(0, tm), pl.ds(0, tk)], abuf.at[g % 2], asem.at[g % 2]).wait()

        def product(cols):
            return jnp.dot(abuf[g % 2], w_ref[:, cols], preferred_element_type=F32)

        col_blocks = [slice(c0, c0 + 512) for c0 in range(0, D, 512)]

        @pl.when(k == 0)
        def _():
            for cols in col_blocks:
                acc[:, cols] = product(cols)

        @pl.when(k > 0)
        def _():
            for cols in col_blocks:
                acc[:, cols] += product(cols)

        @pl.when(k == nK - 1)
        def _():
            for cp in loads(i):
                cp.wait()
            epi(acc, vec_refs, buf_refs, p_outs)
            for cp in stores(i):
                cp.start()

            @pl.when(i == nI - 1)
            def _():
                for cp in stores(i):
                    cp.wait()

    vec = pl.BlockSpec((1, D), lambda i, k: (0, 0))
    scratch = ([pltpu.VMEM((tm, D), F32), pltpu.VMEM((2, tm, tk), BF16)] + [pltpu.VMEM((tm, D), dt) for dt, _, _ in bufs]
               + [_SEMS(2), _SEMS(n_any_in), _SEMS(n_any_out)])
    res, job_res = _pcall(
        body, grid=(nI, nK),
        in_specs=[_ANY] * nP + [pl.BlockSpec((tk, D), lambda i, k: (k, 0))] + [vec] * n_vec + [_ANY] * n_any_in,
        out_specs=[_ANY] * n_any_out + part_specs,
        out_shape=([jax.ShapeDtypeStruct((T, D), bufs[n][0]) for n in store_ix]
                   + [jax.ShapeDtypeStruct(s, d) for s, d, _, _ in parts]),
        scratch_shapes=scratch, name=name, semantics=("arbitrary", "arbitrary"),
        args=[p for p, _ in pieces] + [w] + list(vecs) + [bufs[n][1] for n in load_ix], job=job)
    return res if job is None else (res, job_res)


def _pieces_nn_rms(name, pieces, w, x, gain, sc, dres, tm, tk, job=None):
    _, outs, epi = _rms_mod_bwd_epilogue(x, gain, sc, dres, tm)

    def on_rows(acc, vecs, bufs, parts):
        epi(acc, [bufs[0], vecs[0], vecs[1], bufs[1]], [bufs[1], *parts])

    return _rows_mm(name, pieces, w, x.shape[0], tm, tk, [gain, sc], [(F32, x, False), (F32, dres, True)],
                    outs[1:], on_rows, job=job)


def _rms_mod_fwd(name, x, gain, sc, sh, tr):
    T = x.shape[0]

    def body(x_ref, g_ref, sc_ref, sh_ref, h_ref):
        xv = x_ref[...]
        rstd = lax.rsqrt(jnp.mean(xv * xv, axis=-1, keepdims=True) + EPS)
        h_ref[...] = ((xv * rstd * g_ref[...]) * (1.0 + sc_ref[...]) + sh_ref[...]).astype(BF16)

    row = pl.BlockSpec((tr, D), lambda i: (i, 0))
    vec = pl.BlockSpec((1, D), lambda i: (0, 0))
    return pl.pallas_call(
        body, grid=(T // tr,), in_specs=[row, vec, vec, vec], out_specs=row,
        out_shape=jax.ShapeDtypeStruct((T, D), BF16), name=name, compiler_params=_params(("parallel",)),
    )(x, gain, sc, sh)


def _rms_mod_bwd_epilogue(x, gain, sc, dres, tm, gate=None, mo=None):
    T = x.shape[0]
    with_gate = gate is not None
    row = ((tm, D), lambda i, j, k: (i, 0))
    vec = ((1, D), lambda i, j, k: (0, 0))
    part = ((T // tm * 8, D), F32, (8, D), lambda i, j, k: (i, 0))
    extras = [(x, *row), (gain, *vec), (sc, *vec), (dres, *row)]
    outs = [((T, D), F32, *row), part, part, part]
    if with_gate:
        extras += [(gate, *vec), (mo, *row)]
        outs += [((T, D), BF16, *row), part]

    rows = min(64, tm)

    def epi(acc, ex, ou):
        g = ex[1][...]
        sums = [jnp.zeros((8, D), F32) for _ in range(4)]
        for r0 in range(0, tm, rows):
            rs = slice(r0, r0 + rows)
            dhv, xv = acc[rs, :], ex[0][rs, :]
            rstd = lax.rsqrt(jnp.mean(xv * xv, axis=-1, keepdims=True) + EPS)
            xhat = xv * rstd
            dn = dhv * (1.0 + ex[2][...])
            dxhat = dn * g
            dx = ex[3][rs, :] + rstd * (dxhat - xhat * jnp.mean(dxhat * xhat, axis=-1, keepdims=True))
            ou[0][rs, :] = dx
            terms = [dhv, dhv * (xhat * g), dn * xhat]
            if with_gate:
                terms.append(dx * ex[5][rs, :].astype(F32))
                ou[4][rs, :] = (ex[4][...] * dx).astype(BF16)
            sums = [s + _fold8(t) for s, t in zip(sums, terms)] + sums[len(terms):]
        ou[1][...], ou[2][...], ou[3][...] = sums[:3]
        if with_gate:
            ou[5][...] = sums[3]

    return extras, outs, epi


def _rms_mod_bwd(name, dh, x, gain, sc, dres, tr, gate=None, mo=None):
    T = x.shape[0]
    extras, outs, epi = _rms_mod_bwd_epilogue(x, gain, sc, dres, tr, gate, mo)
    rows_only = lambda im: (lambda i: im(i, 0, 0))
    nE = len(extras)

    def body(dh_ref, *refs):
        epi(dh_ref, refs[:nE], refs[nE:])

    return pl.pallas_call(
        body, grid=(T // tr,),
        in_specs=[pl.BlockSpec((tr, D), lambda i: (i, 0))] + [pl.BlockSpec(bs, rows_only(im)) for _, bs, im in extras],
        out_specs=[pl.BlockSpec(bs, rows_only(im)) for _, _, bs, im in outs],
        out_shape=[jax.ShapeDtypeStruct(s, d) for s, d, _, _ in outs], name=name, compiler_params=_params(("parallel",)),
    )(dh, *[e for e, _, _ in extras])


def _split3(v):
    h = v.astype(BF16)
    r1 = v - h.astype(F32)
    m = r1.astype(BF16)
    lo = (r1 - m.astype(F32)).astype(BF16)
    return h, m, lo


def _tri_mm(tri, v, dims=NN):
    h, m, lo = _split3(v)
    t = tri.astype(BF16)
    mm = lambda p: lax.dot_general(t, p, dims, preferred_element_type=F32)
    return (mm(lo) + mm(m)) + mm(h)


def _hgrn_chunk_terms(q, fl, lb):
    sig = _sigmoid(fl)
    f = lb + (1.0 - lb) * sig
    lf = jnp.log(f)
    kk = 1.0 - f
    sq = _sigmoid(q)
    qf = q * sq
    return sig, f, lf, kk, sq, qf


def _causal(n):
    r = lax.broadcasted_iota(jnp.int32, (n, n), 0)
    c = lax.broadcasted_iota(jnp.int32, (n, n), 1)
    return r >= c


def _hgrn_fwd(proj, lb_logits, o_gain, tt, job=None):
    T = proj.shape[0]
    nT, ncl = T // tt, tt // CHUNK
    C = CHUNK

    def body(q_ref, f_ref, i_ref, g_ref, lbl_ref, og_ref, y_ref, st_ref, S):
        @pl.when(pl.program_id(1) == 0)
        def _():
            S[...] = jnp.zeros_like(S)

        lbl = lbl_ref[...]
        lb = _sigmoid(lbl[0:1, :] - lbl[1:2, :])
        og = og_ref[...]
        shp = (ncl, C, A_HD)
        q, fl, v, g = (r[...].reshape(shp) for r in (q_ref, f_ref, i_ref, g_ref))
        tri = jnp.broadcast_to(_causal(C), (ncl, C, C))
        _, _, lf, kk, _, qf = _hgrn_chunk_terms(q, fl, lb)
        b = _tri_mm(tri, lf, BNN)
        bm, bl = b[:, C // 2 - 1:C // 2, :], b[:, C - 1:C, :]
        qd, kd = qf * jnp.exp(b - bm), kk * jnp.exp(bm - b)
        A = jnp.where(tri, _dot(qd, kd, BNT), 0.0)
        d_st = _dot(v, kk * jnp.exp(bl - b), BTN)
        dec = jnp.exp(bl)
        st = S[...]
        for ci in range(ncl):
            st_ref[0, ci] = st
            st = st * dec[ci] + d_st[ci]
        S[...] = st
        o = _dot(A, v, BNN) + _dot(qf * jnp.exp(b), st_ref[0], BNT)
        r = lax.rsqrt(jnp.mean(o * o, axis=-1, keepdims=True) + EPS)
        y_ref[...] = (o * r * og * (g * _sigmoid(g))).astype(BF16).reshape(tt, A_HD)

    def col(off):
        return pl.BlockSpec((tt, A_HD), lambda h, t: (t, off // A_HD + h))

    head_vec = lambda rows: pl.BlockSpec((rows, A_HD), lambda h, t: (0, h))
    return _pcall(
        body, grid=(A_HEADS, nT),
        in_specs=[col(OFF_QA), col(OFF_FA), col(OFF_IA), col(OFF_GA), head_vec(2), head_vec(1)],
        out_specs=[pl.BlockSpec((tt, A_HD), lambda h, t: (t, h)),
                   pl.BlockSpec((1, ncl, A_HD, A_HD), lambda h, t: (h, t, 0, 0))],
        out_shape=[jax.ShapeDtypeStruct((T, AW), BF16),
                   jax.ShapeDtypeStruct((A_HEADS, T // C, A_HD, A_HD), F32)],
        scratch_shapes=[pltpu.VMEM((A_HD, A_HD), F32)], name="hgrn_fwd", semantics=("parallel", "arbitrary"),
        args=[proj, proj, proj, proj, lb_logits, o_gain], job=job)


def _hgrn_bwd(proj, st, dy, lb_logits, o_gain, tt, job=None):
    T = proj.shape[0]
    nT, ncl = T // tt, tt // CHUNK
    C = CHUNK

    def body(q_ref, f_ref, i_ref, g_ref, st_ref, dy_ref, lbl_ref, og_ref,
             dq_ref, df_ref, di_ref, dg_ref, plb_ref, pog_ref, dS):
        @pl.when(pl.program_id(1) == 0)
        def _():
            dS[...] = jnp.zeros_like(dS)

        lbl = lbl_ref[...]
        lb = _sigmoid(lbl[0:1, :] - lbl[1:2, :])
        og = og_ref[...]
        shp = (ncl, C, A_HD)
        flat = lambda t: t.reshape(tt, A_HD)
        q, fl, v, g, dout = (r[...].reshape(shp) for r in (q_ref, f_ref, i_ref, g_ref, dy_ref))
        tri = jnp.broadcast_to(_causal(C), (ncl, C, C))
        rowi = lax.broadcasted_iota(jnp.int32, shp, 1)
        st0 = st_ref[0]
        sig, f, lf, kk, sq, qf = _hgrn_chunk_terms(q, fl, lb)
        b = _tri_mm(tri, lf, BNN)
        bm, bl = b[:, C // 2 - 1:C // 2, :], b[:, C - 1:C, :]
        e_qd, e_kd, e_ke, e_b = jnp.exp(b - bm), jnp.exp(bm - b), jnp.exp(bl - b), jnp.exp(b)
        qd, kd, ke, qe = qf * e_qd, kk * e_kd, kk * e_ke, qf * e_b
        dec = jnp.exp(bl)
        A = jnp.where(tri, _dot(qd, kd, BNT), 0.0)
        o = _dot(A, v, BNN) + _dot(qe, st0, BNT)
        r = lax.rsqrt(jnp.mean(o * o, axis=-1, keepdims=True) + EPS)
        sg = _sigmoid(g)
        on = o * r * og
        dg_ref[...] = flat((dout * on * (sg * (1.0 + g * (1.0 - sg)))).astype(BF16))
        don = dout * (g * sg)
        pog_ref[...] = _fold8(flat(don * o * r))
        dyh = don * og
        do = r * (dyh - o * (r * r) * jnp.mean(dyh * o, axis=-1, keepdims=True))
        g_st = _dot(do, qe, BTN)
        run = dS[...]
        after = [None] * ncl
        for ci in reversed(range(ncl)):
            after[ci] = run
            run = g_st[ci] + run * dec[ci]
        dS[...] = run
        d_after = jnp.stack(after, axis=0)
        ddec = jnp.sum(d_after * st0, axis=1, keepdims=True)
        dqe = _dot(do, st0, BNN)
        dke = _dot(v, d_after, BNN)
        dA = jnp.where(tri, _dot(do, v, BNT), 0.0)
        dv = _dot(ke, d_after, BNT) + _dot(A, do, BTN)
        dqd = _dot(dA, kd, BNN)
        dkd = _dot(dA, qd, BTN)
        di_ref[...] = flat(dv.astype(BF16))
        dqf = dqe * e_b + dqd * e_qd
        dkk = dkd * e_kd + dke * e_ke
        t_qd, t_kd, t_ke = dqd * qd, dkd * kd, dke * ke
        db = dqe * qe + t_qd - t_kd - t_ke
        dbm = jnp.sum(t_kd - t_qd, axis=1, keepdims=True)
        dbl = jnp.sum(t_ke, axis=1, keepdims=True) + ddec * dec
        db = db + jnp.where(rowi == C // 2 - 1, dbm, 0.0) + jnp.where(rowi == C - 1, dbl, 0.0)
        dlf = _tri_mm(tri, db, BTN)
        dfv = dlf / f - dkk
        df_ref[...] = flat((dfv * (1.0 - lb) * sig * (1.0 - sig)).astype(BF16))
        plb_ref[...] = _fold8(flat(dfv * (1.0 - sig)))
        dq_ref[...] = flat((dqf * (sq * (1.0 + q * (1.0 - sq)))).astype(BF16))

    def col(off):
        return pl.BlockSpec((tt, A_HD), lambda h, t: (nT - 1 - t, off // A_HD + h))

    head_vec = lambda rows: pl.BlockSpec((rows, A_HD), lambda h, t: (0, h))
    o_spec = pl.BlockSpec((tt, A_HD), lambda h, t: (nT - 1 - t, h))
    p_spec = pl.BlockSpec((8, A_HD), lambda h, t: (t, h))
    o_shape = jax.ShapeDtypeStruct((T, AW), BF16)
    p_shape = jax.ShapeDtypeStruct((nT * 8, AW), F32)
    return _pcall(
        body, grid=(A_HEADS, nT),
        in_specs=[col(OFF_QA), col(OFF_FA), col(OFF_IA), col(OFF_GA),
                  pl.BlockSpec((1, ncl, A_HD, A_HD), lambda h, t: (h, nT - 1 - t, 0, 0)),
                  pl.BlockSpec((tt, A_HD), lambda h, t: (nT - 1 - t, h)), head_vec(2), head_vec(1)],
        out_specs=[o_spec, o_spec, o_spec, o_spec, p_spec, p_spec],
        out_shape=[o_shape, o_shape, o_shape, o_shape, p_shape, p_shape],
        scratch_shapes=[pltpu.VMEM((A_HD, A_HD), F32)], name="hgrn_bwd", semantics=("parallel", "arbitrary"),
        args=[proj, proj, proj, proj, st, dy, lb_logits, o_gain], job=job)


LANES = 128
Q_COLS = BW // LANES


def _low_half():
    return lax.broadcasted_iota(jnp.int32, (1, LANES), 1) < B_HD


def _half_sum(t, low):
    lo = jnp.sum(jnp.where(low, t, 0.0), axis=-1, keepdims=True)
    hi = jnp.sum(jnp.where(low, 0.0, t), axis=-1, keepdims=True)
    return jnp.where(low, lo, hi)


def _half_rms(t, low):
    r = lax.rsqrt(_half_sum(t * t, low) * (1.0 / B_HD) + EPS)
    return t * r, r


def _fold_halves(p, low):
    return jnp.where(low, p + pltpu.roll(p, B_HD, 1), 0.0)


def _stack_cols(x):
    return jnp.stack([x[:, c * LANES:(c + 1) * LANES] for c in range(Q_COLS)], axis=0).reshape(KV_HEADS, 2 * BLK, LANES)


def _col_of(t, c):
    return t[c // 2, (c % 2) * BLK:(c % 2 + 1) * BLK]


def _split_halves(col, s, low):
    own = jnp.where(low if s == 0 else jnp.logical_not(low), col, 0.0)
    other = pltpu.roll(own, B_HD, 1)
    return (own, other) if s == 0 else (other, own)


def _swa_keys(kp_ref, kc_ref, vp_ref, vc_ref, kg, low):
    k_lo, k_hi, v_lo, v_hi, hats = [], [], [], [], []
    for j in range(KVW // LANES):
        cs = slice(j * LANES, (j + 1) * LANES)
        k_hat, k_r = _half_rms(jnp.concatenate([kp_ref[:, cs], kc_ref[:, cs]], axis=0), low)
        vcol = jnp.concatenate([vp_ref[:, cs], vc_ref[:, cs]], axis=0)
        hats.append((k_hat, k_r))
        for s in range(2):
            for dst_lo, dst_hi, col in ((k_lo, k_hi, k_hat * kg), (v_lo, v_hi, vcol)):
                lo, hi = _split_halves(col, s, low)
                dst_lo.append(lo)
                dst_hi.append(hi)
    st = lambda parts: jnp.stack(parts, axis=0)
    return st(k_lo), st(k_hi), st(v_lo), st(v_hi), hats


def _swa_mask(first_block):
    qi = lax.broadcasted_iota(jnp.int32, (BLK, 2 * BLK), 0) + BLK
    ki = lax.broadcasted_iota(jnp.int32, (BLK, 2 * BLK), 1)
    rel = qi - ki
    m = (rel >= 0) & (rel < BLK) & (jnp.logical_not(first_block) | (ki >= BLK))
    return jnp.concatenate([m, m], axis=0)


def _sink_cols(sk_ref, hi):
    top = lax.broadcasted_iota(jnp.int32, (2 * BLK, 1), 0) < BLK
    return jnp.stack([jnp.where(top, sk_ref[0, GROUP * hk + hi], sk_ref[0, GROUP * hk + 2 + hi])
                      for hk in range(KV_HEADS)], axis=0)


def _swa_probs(qn, k_half, sink, mask):
    s = jnp.where(mask, _dot(qn, k_half, BNT) * (B_HD ** -0.5), NEG)
    m = jnp.maximum(jnp.max(s, axis=-1, keepdims=True), sink)
    p = jnp.exp(s - m)
    ps = jnp.exp(sink - m)
    inv = 1.0 / (jnp.sum(p, axis=-1, keepdims=True) + ps)
    return p * inv, ps * inv


def _swa_fwd(proj, q_gain, k_gain, sinks, job=None):
    T = proj.shape[0]
    nb = T // BLK

    def body(q_ref, kc_ref, kp_ref, vc_ref, vp_ref, qg_ref, kg_ref, sk_ref, o_ref):
        low = _low_half()
        mask = _swa_mask(pl.program_id(0) == 0)
        qn = _half_rms(_stack_cols(q_ref[...]), low)[0] * qg_ref[...]
        k_lo, k_hi, v_lo, v_hi, _ = _swa_keys(kp_ref, kc_ref, vp_ref, vc_ref, kg_ref[...], low)
        p_lo, _ = _swa_probs(qn, k_lo, _sink_cols(sk_ref, 0), mask)
        p_hi, _ = _swa_probs(qn, k_hi, _sink_cols(sk_ref, 1), mask)
        o = (_dot(p_lo, v_lo, BNN) + _dot(p_hi, v_hi, BNN)).astype(BF16)
        for c in range(Q_COLS):
            o_ref[:, c * LANES:(c + 1) * LANES] = _col_of(o, c)

    q_gain, k_gain = jnp.tile(q_gain, (1, 2)), jnp.tile(k_gain, (1, 2))
    cur = lambda w, off: pl.BlockSpec((BLK, w), lambda i: (i, off // w))
    prev = lambda w, off: pl.BlockSpec((BLK, w), lambda i: (jnp.maximum(i - 1, 0), off // w))
    small = lambda n: pl.BlockSpec((1, 2 * n), lambda i: (0, 0))
    return _pcall(
        body, grid=(nb,),
        in_specs=[cur(BW, OFF_QB), cur(KVW, OFF_KB), prev(KVW, OFF_KB), cur(KVW, OFF_VB), prev(KVW, OFF_VB),
                  small(B_HD), small(B_HD), pl.BlockSpec(memory_space=pltpu.SMEM)],
        out_specs=[pl.BlockSpec((BLK, BW), lambda i: (i, 0))],
        out_shape=[jax.ShapeDtypeStruct((T, BW), BF16)], scratch_shapes=[], name="swa_fwd", semantics=("parallel",),
        args=[proj, proj, proj, proj, proj, q_gain, k_gain, sinks], job=job)


def _swa_bwd(proj, dout, q_gain, k_gain, sinks, job=None):
    T = proj.shape[0]
    nb = T // BLK
    W = BW + 2 * KVW

    def body(q_ref, kc_ref, kp_ref, vc_ref, vp_ref, do_ref, qg_ref, kg_ref, sk_ref,
             dq_ref, dkv_ref, pqg_ref, pkg_ref, psk_ref, dkn_c, dv_c):
        i = pl.program_id(0)
        live = i < nb
        low = _low_half()
        high = jnp.logical_not(low)
        qg, kg = qg_ref[...], kg_ref[...]
        mask = _swa_mask(i == 0)
        lane = lax.broadcasted_iota(jnp.int32, (1, LANES), 1)
        scale = B_HD ** -0.5

        @pl.when(i == 0)
        def _():
            dkn_c[...] = jnp.zeros_like(dkn_c)
            dv_c[...] = jnp.zeros_like(dv_c)

        q_hat, q_r = _half_rms(_stack_cols(q_ref[...]), low)
        qn = q_hat * qg
        k_lo, k_hi, v_lo, v_hi, hats = _swa_keys(kp_ref, kc_ref, vp_ref, vc_ref, kg, low)
        do = _stack_cols(do_ref[...])
        dqn = jnp.zeros((KV_HEADS, 2 * BLK, LANES), F32)
        acc_sk = jnp.zeros((1, LANES), F32)
        dk_parts, dv_parts = [], []
        for hi, (k_h, v_h) in enumerate(((k_lo, v_lo), (k_hi, v_hi))):
            p, ps = _swa_probs(qn, k_h, _sink_cols(sk_ref, hi), mask)
            dp = _dot(do, v_h, BNT)
            delta = jnp.sum(p * dp, axis=-1, keepdims=True)
            ds = p * (dp - delta) * scale
            dqn = dqn + _dot(ds, k_h, BNN)
            dk_parts.append(_dot(ds, qn, BTN))
            dv_parts.append(_dot(p, do, BTN))
            t = ps * delta
            for hk in range(KV_HEADS):
                for rows in range(2):
                    h = GROUP * hk + 2 * rows + hi
                    acc_sk = acc_sk + jnp.where(
                        lane == h, -jnp.sum(t[hk, rows * BLK:(rows + 1) * BLK], axis=0, keepdims=True), 0.0)
        dqh = dqn * qg
        dq = (q_r * (dqh - q_hat * (_half_sum(dqh * q_hat, low) * (1.0 / B_HD)))).astype(BF16)
        for c in range(Q_COLS):
            dq_ref[:, c * LANES:(c + 1) * LANES] = _col_of(dq, c)
        acc_qg = _fold_halves(_fold8((dqn * q_hat).reshape(KV_HEADS * 2 * BLK, LANES)), low)

        def native(parts, j):
            lo_arr, hi_arr = parts
            a, b = 2 * j, 2 * j + 1
            return (jnp.where(low, lo_arr[a], 0.0) + pltpu.roll(jnp.where(high, hi_arr[a], 0.0), B_HD, 1)
                    + jnp.where(high, hi_arr[b], 0.0) + pltpu.roll(jnp.where(low, lo_arr[b], 0.0), B_HD, 1))

        acc_kg = jnp.zeros((8, LANES), F32)
        for j in range(KVW // LANES):
            cs = slice(j * LANES, (j + 1) * LANES)
            dkn = jnp.where(live, native(dk_parts, j), 0.0)
            dvc = jnp.where(live, native(dv_parts, j), 0.0)
            kp_hat, kp_r = hats[j][0][:BLK], hats[j][1][:BLK]
            dkn_prev = dkn_c[:, cs] + dkn[:BLK]
            dv_prev = dv_c[:, cs] + dvc[:BLK]
            acc_kg = acc_kg + _fold8(dkn_prev * kp_hat)
            dkh = dkn_prev * kg
            dkv_ref[:, cs] = (kp_r * (dkh - kp_hat * (_half_sum(dkh * kp_hat, low) * (1.0 / B_HD)))).astype(BF16)
            dkv_ref[:, KVW + j * LANES:KVW + (j + 1) * LANES] = dv_prev.astype(BF16)
            dkn_c[:, cs] = dkn[BLK:]
            dv_c[:, cs] = dvc[BLK:]
        keep = jnp.where(i > 0, 1.0, 0.0)
        pqg_ref[...] = jnp.where(live, acc_qg, 0.0)
        pkg_ref[...] = _fold_halves(acc_kg, low) * keep
        psk_ref[...] = jnp.broadcast_to(jnp.where(live, acc_sk, 0.0), (8, LANES)) * (
            lax.broadcasted_iota(jnp.int32, (8, LANES), 0) == 0).astype(F32)

    q_gain, k_gain = jnp.tile(q_gain, (1, 2)), jnp.tile(k_gain, (1, 2))
    last = nb - 1
    cur = lambda w, off: pl.BlockSpec((BLK, w), lambda i: (jnp.minimum(i, last), off // w))
    prev = lambda w, off: pl.BlockSpec((BLK, w), lambda i: (jnp.maximum(i - 1, 0), off // w))
    small = lambda n: pl.BlockSpec((1, 2 * n), lambda i: (0, 0))
    part = pl.BlockSpec((8, 128), lambda i: (i, 0))
    p_shape = jax.ShapeDtypeStruct(((nb + 1) * 8, 128), F32)
    return _pcall(
        body, grid=(nb + 1,),
        in_specs=[cur(BW, OFF_QB), cur(KVW, OFF_KB), prev(KVW, OFF_KB), cur(KVW, OFF_VB), prev(KVW, OFF_VB),
                  pl.BlockSpec((BLK, BW), lambda i: (jnp.minimum(i, last), 0)), small(B_HD), small(B_HD),
                  pl.BlockSpec(memory_space=pltpu.SMEM)],
        out_specs=[pl.BlockSpec((BLK, BW), lambda i: (i, 0)),
                   pl.BlockSpec((BLK, 2 * KVW), lambda i: (jnp.maximum(i - 1, 0), 0)), part, part, part],
        out_shape=[jax.ShapeDtypeStruct((T + BLK, BW), BF16), jax.ShapeDtypeStruct((T, 2 * KVW), BF16),
                   p_shape, p_shape, p_shape],
        scratch_shapes=[pltpu.VMEM((BLK, KVW), F32), pltpu.VMEM((BLK, KVW), F32)], name="swa_bwd",
        semantics=("arbitrary",), args=[proj, proj, proj, proj, proj, dout, q_gain, k_gain, sinks], job=job)


def _branch_merge(ya_pre, attn, wa_t, wb_t, proj, tm, tn, job=None):
    T = ya_pre.shape[0]

    def body(a_ref, b_ref, wa_ref, wb_ref, ga_ref, gb_ref, ya_ref, yb_ref, mg_ref):
        ya = lax.dot_general(a_ref[...], wa_ref[...], NT, preferred_element_type=F32)
        yb = lax.dot_general(b_ref[...], wb_ref[...], NT, preferred_element_type=F32)
        ya_ref[...] = ya.astype(BF16)
        yb_ref[...] = yb.astype(BF16)
        mg_ref[...] = (_sigmoid(ga_ref[...]) * ya + _sigmoid(gb_ref[...]) * yb).astype(BF16)

    o_spec = pl.BlockSpec((tm, tn), lambda i, j: (i, j))
    o_shape = jax.ShapeDtypeStruct((T, D), BF16)
    return _pcall(
        body, grid=(T // tm, D // tn),
        in_specs=[pl.BlockSpec((tm, AW), lambda i, j: (i, 0)), pl.BlockSpec((tm, BW), lambda i, j: (i, 0)),
                  pl.BlockSpec((tn, AW), lambda i, j: (j, 0)), pl.BlockSpec((tn, BW), lambda i, j: (j, 0)),
                  pl.BlockSpec((tm, tn), lambda i, j: (i, OFF_GTA // tn + j)),
                  pl.BlockSpec((tm, tn), lambda i, j: (i, OFF_GTB // tn + j))],
        out_specs=[o_spec, o_spec, o_spec], out_shape=[o_shape, o_shape, o_shape], scratch_shapes=[], name="branch_merge",
        semantics=("parallel", "parallel"), args=[ya_pre, attn, wa_t, wb_t, proj, proj], job=job)


def _ij(i, j, k):
    return (i, j)


def _local_step(x, tgt, mod, g1, g2, lbl, og, qg, kg, sk, shards, c_arr):
    win_s, wa_s, wb_s, wout_s, wmi_s, wmo_s = shards
    T = x.shape[0]
    tm, tr, tt = min(1024, T), min(256, T), min(2048, T)
    tk_t = min(1024, T)
    tn = 512
    sh1, sc1, gt1, sh2, sc2, gt2 = (mod[:, i * D:(i + 1) * D] for i in range(N_MOD))
    nI = T // tm
    blk = (tm, tn)
    vec_j = ((1, tn), lambda i, j, k: (0, j))

    h = _rms_mod_fwd("rms1_fwd", x, g1, sc1, sh1, tr)

    def epi_store(acc, ex, ou):
        ou[0][...] = acc.astype(ou[0].dtype)

    tm2 = min(2048, T)
    blk2 = (tm2, tn)

    full = lambda s: (0, s.shape[0])
    last = wmi_s.shape[0]
    (win_t,) = _run_job("gather_w_in", _gather_relay_job([win_s]))
    (proj,), (wa_t, wb_t, w_out, wmi_part) = _mm(
        "in_proj", "nt", [(h, D)], win_t, T, IN_W, D, tm2, tn, D, [], [((T, IN_W), F32, blk2, _ij)], epi_store,
        job=_gather_job([wa_s, wb_s, wout_s, wmi_s], rows=[full(wa_s), full(wb_s), full(wout_s), (0, MI_CUTS[0])]))
    (ya_pre, st), (wmi_part,) = _hgrn_fwd(
        proj, lbl, og, tt, job=_gather_job([wmi_s], rows=[MI_CUTS], into=[wmi_part]))
    (attn,), (wmi_t, wmo_part) = _swa_fwd(
        proj, qg, kg, sk, job=_gather_job([wmi_s, wmo_s], rows=[(MI_CUTS[1], last), (0, MO_CUT)], into=[wmi_part, None]))
    (ya, yb, merged), _ = _branch_merge(ya_pre, attn, wa_t, wb_t, proj, tm, tn)

    def epi_res1(acc, ex, ou):
        x_ref, gt_ref = ex
        ou[0][...] = acc.astype(BF16)
        ou[1][...] = x_ref[...] + gt_ref[...] * acc

    mo, x1 = _mm("out_proj", "nn", [(merged, D)], w_out, T, D, D, tm, tn, D, [(x, blk, _ij), (gt1, *vec_j)],
                 [((T, D), BF16, blk, _ij), ((T, D), F32, blk, _ij)], epi_res1)
    h2 = _rms_mod_fwd("rms2_fwd", x1, g2, sc2, sh2, tr)

    def epi_relu2(acc, ex, ou):
        r = jnp.maximum(acc, 0.0)
        ou[0][...] = r.astype(BF16)
        ou[1][...] = (r * r).astype(BF16)

    (r, a), (w_mo,) = _mm("mlp_in", "nt", [(h2, D)], wmi_t, T, HID, D, tm2, tn, D, [],
                          [((T, HID), BF16, blk2, _ij), ((T, HID), BF16, blk2, _ij)], epi_relu2,
                          job=_gather_job([wmo_s], rows=[(MO_CUT, last)], into=[wmo_part]))

    def loss_rows(acc, vecs, bufs, parts):
        gt = vecs[0][...]
        x1_buf, t_buf, dz_buf = bufs
        rows = min(64, tm)
        loss_sum, gate_sum = jnp.zeros((8, D), F32), jnp.zeros((8, D), F32)
        for r0 in range(0, tm, rows):
            rs = slice(r0, r0 + rows)
            z = acc[rs, :]
            e = x1_buf[rs, :] + gt * z - t_buf[rs, :]
            dy = e * (1.0 / D)
            t_buf[rs, :] = dy
            dz_buf[rs, :] = (gt * dy).astype(BF16)
            loss_sum = loss_sum + _fold8(e * e)
            gate_sum = gate_sum + _fold8(dy * z)
        parts[0][...] = loss_sum * (0.5 / D)
        parts[1][...] = gate_sum

    part_rows = ((nI * 8, D), F32, (8, D), lambda i, j, k: (i, 0))
    dy, dz, p_loss, p_gt2 = _rows_mm(
        "mlp_out", [(a, HID)], w_mo, T, tm, 1024, [gt2], [(F32, x1, False), (F32, tgt, True), (BF16, None, True)],
        [part_rows, part_rows], loss_rows)

    def epi_du(acc, ex, ou):
        ou[0][...] = (acc * (2.0 * ex[0][...].astype(F32))).astype(BF16)

    (du,) = _mm("mlp_out_dx", "nt", [(dz, D)], w_mo, T, HID, D, tm2, tn, D, [(r, blk2, _ij)],
                [((T, HID), BF16, blk2, _ij)], epi_du)
    gblk = (1024, 1024)
    gwide = (1024, D)
    pair_sum = lambda nm, g, r1: _pair_sum("pair_sum_" + nm, g, r1, c_arr, _sum_rows(r1.shape[1]))
    (g_mo,) = _mm("mlp_out_dw", "tn", [(a, HID)], dz, HID, D, T, 1024, D, tk_t, [], [((HID, D), BF16, gwide, _ij)], epi_store)
    (dh2,), (r1_mo,) = _mm("mlp_in_dx", "nn", [(du, HID)], wmi_t, T, D, HID, tm, D, 1024, [],
                           [((T, D), F32, (tm, D), _ij)], epi_store, job=_pair_job([g_mo]))
    dx1, p_sh2, p_sc2, p_g2, dmo, p_gt1 = _rms_mod_bwd("rms2_bwd", dh2, x1, g2, sc2, dy, tr, gate=gt1, mo=mo)
    s_mo = pair_sum("mlp_out", g_mo, r1_mo)
    near, far = (1, 2), (3,)
    (g_mi,), (rn_mo,) = _mm("mlp_in_dw", "tn", [(du, HID)], h2, HID, D, T, 1024, D, tk_t, [],
                            [((HID, D), BF16, gwide, _ij)], epi_store, job=_chip_job([s_mo], near))

    def epi_gates(acc, ex, ou):
        ya_ref, yb_ref, ga_ref, gb_ref = ex
        sa, sb = _sigmoid(ga_ref[...]), _sigmoid(gb_ref[...])
        ou[0][...] = (acc * sa).astype(BF16)
        ou[1][...] = (acc * sb).astype(BF16)
        ou[2][...] = (acc * ya_ref[...].astype(F32) * (sa * (1.0 - sa))).astype(BF16)
        ou[3][...] = (acc * yb_ref[...].astype(F32) * (sb * (1.0 - sb))).astype(BF16)

    o_bf = ((T, D), BF16, blk, _ij)
    (dya, dyb, dga, dgb), (rf_mo, r1_mi) = _mm(
        "out_proj_dx", "nt", [(dmo, D)], w_out, T, D, D, tm, tn, D,
        [(ya, blk, _ij), (yb, blk, _ij), (proj, blk, lambda i, j, k: (i, OFF_GTA // tn + j)),
         (proj, blk, lambda i, j, k: (i, OFF_GTB // tn + j))], [o_bf, o_bf, o_bf, o_bf], epi_gates,
        job=_both(_chip_job([s_mo], far), _pair_job([g_mi])))
    s_mi = pair_sum("mlp_in", g_mi, r1_mi)
    (g_out,) = _mm("out_proj_dw", "tn", [(merged, D)], dmo, D, D, T, 1024, 1024, tk_t, [], [((D, D), BF16, gblk, _ij)], epi_store)
    dya_pre, dattn = _twin_mm("branch_dx", "nn", [(dya, wa_t), (dyb, wb_t)], T, AW, D, tm, tn, D, F32)
    g_a, g_b = _twin_mm("branch_dw", "tn", [(dya, ya_pre), (dyb, attn)], D, AW, T, 1024, 1024, tk_t, BF16)
    (dqa, dfa, dia, dgg, p_lb, p_og), (rn_mi, r1_out, r1_a, r1_b) = _hgrn_bwd(
        proj, st, dya_pre, lbl, og, tt, job=_both(_chip_job([s_mi], near), _pair_job([g_out, g_a, g_b])))
    (dqb, dkv, p_qg, p_kg, p_sk), (rf_mi,) = _swa_bwd(proj, dattn, qg, kg, sk, job=_chip_job([s_mi], far))
    s_out, s_a, s_b = pair_sum("out", g_out, r1_out), pair_sum("branch_a", g_a, r1_a), pair_sum("branch_b", g_b, r1_b)
    pieces = [(dqa, AW), (dfa, AW), (dia, AW), (dgg, AW), (dqb, BW), (dkv, 2 * KVW), (dga, D), (dgb, D)]
    (g_in,), (r2_out, r2_a, r2_b) = _pieces_tn("in_proj_dw", pieces, h, 512, job=_chip_job([s_out, s_a, s_b]))
    (r1_in,) = _run_job("pair_w_in", _pair_job([g_in]))
    s_in = pair_sum("in", g_in, r1_in)
    (dx, p_sh1, p_sc1, p_g1), (r2_in,) = _pieces_nn_rms(
        "in_proj_dx", pieces, win_t, x, g1, sc1, dx1, tm, 512, job=_chip_job([s_in]))

    partials = dict(sh1=p_sh1, sc1=p_sc1, gt1=p_gt1, sh2=p_sh2, sc2=p_sc2, gt2=p_gt2, g1=p_g1, g2=p_g2,
                    lb=p_lb, og=p_og, qg=p_qg, kg=p_kg, sk=p_sk, loss=p_loss)
    sums = dict(w_in=(s_in, [r2_in]), w_branch_a=(s_a, [r2_a]), w_branch_b=(s_b, [r2_b]), w_out=(s_out, [r2_out]),
                w_mlp_in=(s_mi, [rn_mi, rf_mi]), w_mlp_out=(s_mo, [rn_mo, rf_mo]))
    return dx, sums, partials


def _exchange_slots(buf, send_sems, recv_sems):
    me = _mesh_pos()
    mine = buf.at[_index(me)]
    sends = []
    for k in range(1, N_DEV):
        cp = pltpu.make_async_remote_copy(src_ref=mine, dst_ref=mine, send_sem=send_sems.at[k - 1],
                                          recv_sem=recv_sems.at[k - 1], device_id=_flip(me, k), device_id_type=MESH)
        cp.start()
        sends.append(cp)
    for k in range(1, N_DEV):
        theirs = buf.at[_index(_flip(me, k))]
        pltpu.make_async_remote_copy(src_ref=theirs, dst_ref=theirs, send_sem=send_sems.at[k - 1],
                                     recv_sem=recv_sems.at[k - 1], device_id=_flip(me, k), device_id_type=MESH).wait_recv()
    for cp in sends:
        cp.wait_send()


ADA_W = N_MOD * D // N_DEV


def _ada_mod(c, w_ada, b_shard):
    def body(c_ref, w_ref, b_ref, mod_ref, sc_ref, cbuf, mbuf, s1, r1, s2, r2):
        me = _index(_mesh_pos())
        cbuf[me] = c_ref[...]
        _exchange_slots(cbuf, s1, r1)
        row = lax.broadcasted_iota(jnp.int32, (N_DEV, D), 0)
        call = jnp.zeros((N_DEV, D), F32)
        for d in range(N_DEV):
            call = jnp.where(row == d, cbuf[d], call)
        sc = call * _sigmoid(call)
        sc_ref[...] = sc
        mbuf[me] = _dot(sc, w_ref[...]) + b_ref[...]
        _exchange_slots(mbuf, s2, r2)
        for s in range(N_DEV):
            mod_ref[:, s * ADA_W:(s + 1) * ADA_W] = mbuf[s, pl.ds(me, 1), :]

    return pl.pallas_call(
        body, in_specs=[_VMEM, _VMEM, _VMEM], out_specs=[_VMEM, _VMEM],
        out_shape=[jax.ShapeDtypeStruct((1, N_MOD * D), F32), jax.ShapeDtypeStruct((N_DEV, D), F32)],
        scratch_shapes=[pltpu.VMEM((N_DEV, 1, D), F32), pltpu.VMEM((N_DEV, N_DEV, ADA_W), F32),
                        _SEMS(N_DEV - 1), _SEMS(N_DEV - 1), _SEMS(N_DEV - 1), _SEMS(N_DEV - 1)],
        name="ada_mod", compiler_params=pltpu.CompilerParams(vmem_limit_bytes=VMEM_LIMIT),
    )(c, w_ada, b_shard)


SMALL_SEGS = (("b_ada", N_MOD * D), ("norm1_gain", D), ("norm2_gain", D), ("lb0", AW), ("lb1", AW),
              ("hgrn_o_gain", AW), ("q_norm_gain", 128), ("k_norm_gain", 128), ("sinks", 128))
SMALL_W = sum(w for _, w in SMALL_SEGS)
X_SEGS = (("sh1", D), ("sc1", D), ("gt1", D), ("sh2", D), ("sc2", D), ("gt2", D), ("g1", D), ("g2", D),
          ("lb", AW), ("og", AW), ("qg", 128), ("kg", 128), ("sk", 128), ("loss", 128))
X_W = sum(w for _, w in X_SEGS)


def _offsets(segs):
    out, o = {}, 0
    for name, w in segs:
        out[name] = (o, w)
        o += w
    return out


def _small_reduce(parts, lb_logits):
    xo, so = _offsets(X_SEGS), _offsets(SMALL_SEGS)
    names = [nm for nm, _ in X_SEGS]

    def body(*refs):
        p_refs = dict(zip(names, refs[:len(names)]))
        lbl_ref, allx, gs_ref, loss_ref, send_sems, recv_sems = refs[len(names):]
        me = _index(_mesh_pos())
        for nm, (o, w) in xo.items():
            if nm == "loss":
                allx[me, :, o:o + w] = jnp.broadcast_to(jnp.sum(p_refs[nm][...]), (1, w))
            else:
                allx[me, :, o:o + w] = jnp.sum(p_refs[nm][...], axis=0, keepdims=True)
        _exchange_slots(allx, send_sems, recv_sems)
        tot = allx[0]
        for d in range(1, N_DEV):
            tot = tot + allx[d]
        seg = lambda nm: tot[:, xo[nm][0]:xo[nm][0] + xo[nm][1]]

        def put(nm, v):
            gs_ref[:, so[nm][0]:so[nm][0] + so[nm][1]] = v

        put("b_ada", tot[:, 0:N_MOD * D])
        put("norm1_gain", seg("g1"))
        put("norm2_gain", seg("g2"))
        lbl = lbl_ref[...]
        lb = _sigmoid(lbl[0:1, :] - lbl[1:2, :])
        dl0 = seg("lb") * lb * (1.0 - lb)
        put("lb0", dl0)
        put("lb1", -dl0)
        put("hgrn_o_gain", seg("og"))
        put("q_norm_gain", seg("qg"))
        put("k_norm_gain", seg("kg"))
        put("sinks", seg("sk"))
        loss_ref[...] = seg("loss")

    return pl.pallas_call(
        body, in_specs=[_VMEM] * (len(names) + 1), out_specs=[_VMEM, _VMEM, _VMEM],
        out_shape=[jax.ShapeDtypeStruct((N_DEV, 1, X_W), F32), jax.ShapeDtypeStruct((1, SMALL_W), F32),
                   jax.ShapeDtypeStruct((1, 128), F32)],
        scratch_shapes=[_SEMS(N_DEV - 1), _SEMS(N_DEV - 1)], name="small_reduce",
        compiler_params=pltpu.CompilerParams(vmem_limit_bytes=VMEM_LIMIT),
    )(*[parts[nm] for nm in names], lb_logits)


def _adamw_math(w, g, m, v):
    m = B1 * m + (1.0 - B1) * g
    v = B2 * v + (1.0 - B2) * (g * g)
    m_hat = m / (1.0 - B1 ** STEP)
    v_hat = v / (1.0 - B2 ** STEP)
    return -LR * (m_hat / (jnp.sqrt(v_hat) + ADAM_EPS) + WD * w), m, v


def _sum_rows(rs):
    return 256 if rs % 256 == 0 else rs // 2


def _pair_sum(name, g, recv, c_arr, tr):
    _, rs, cols = recv.shape
    blk = (1, tr, cols)

    def body(c_ref, g_ref, r_ref, o_ref):
        o_ref[...] = (g_ref[...].astype(F32) + r_ref[...].astype(F32)).astype(BF16)

    grid_spec = pltpu.PrefetchScalarGridSpec(
        num_scalar_prefetch=1, grid=(4, rs // tr),
        in_specs=[pl.BlockSpec(blk, lambda q, i, c: (2 * q + c[0], i, 0)), pl.BlockSpec(blk, lambda q, i, c: (q, i, 0))],
        out_specs=pl.BlockSpec(blk, lambda q, i, c: (q, i, 0)))
    return pl.pallas_call(body, grid_spec=grid_spec, out_shape=jax.ShapeDtypeStruct((4, rs, cols), BF16), name=name,
                          compiler_params=_params(("parallel", "parallel")))(c_arr, g.reshape(N_DEV, rs, cols), recv)


def _sum_adamw(name, sums, recvs, q_arr, w, m, v, transposed, tile):
    rows, cols = w.shape
    nR = len(recvs)

    def body(q_ref, s_ref, *refs):
        r_refs = refs[:nR]
        w_ref, m_ref, v_ref, g_ref, d_ref, nm_ref, nv_ref = refs[nR:]
        g = s_ref[0].astype(F32)
        for r_ref in r_refs:
            for slot in range(r_ref.shape[0]):
                g = g + r_ref[slot].astype(F32)
        g = g.T if transposed else g
        g_ref[...] = g
        d_ref[...], nm_ref[...], nv_ref[...] = _adamw_math(w_ref[...], g, m_ref[...], v_ref[...])

    if transposed:
        slab = lambda n, first: pl.BlockSpec((n, cols, tile), lambda i, q: (first(q), 0, i))
    else:
        slab = lambda n, first: pl.BlockSpec((n, tile, cols), lambda i, q: (first(q), i, 0))
    spec = pl.BlockSpec((tile, cols), lambda i, q: (i, 0))
    shape = jax.ShapeDtypeStruct((rows, cols), F32)
    grid_spec = pltpu.PrefetchScalarGridSpec(
        num_scalar_prefetch=1, grid=(rows // tile,),
        in_specs=[slab(1, lambda q: q[0])] + [slab(r.shape[0], lambda q: 0) for r in recvs] + [spec] * 3,
        out_specs=[spec] * 4)
    return pl.pallas_call(body, grid_spec=grid_spec, out_shape=[shape] * 4, name=name,
                          compiler_params=_params(("parallel",)))(q_arr, sums, *recvs, w, m, v)


def _adamw(name, w, g, m, v, tr):
    rows, cols = w.shape

    def body(w_ref, g_ref, m_ref, v_ref, d_ref, nm_ref, nv_ref):
        d_ref[...], nm_ref[...], nv_ref[...] = _adamw_math(w_ref[...], g_ref[...], m_ref[...], v_ref[...])

    spec = pl.BlockSpec((tr, cols), lambda i: (i, 0))
    shape = jax.ShapeDtypeStruct((rows, cols), F32)
    return pl.pallas_call(
        body, grid=(rows // tr,), in_specs=[spec] * 4, out_specs=[spec] * 3, out_shape=[shape] * 3, name=name,
        compiler_params=_params(("parallel",)),
    )(w, g, m, v)


def _ada_update(sc_t, dmod_cols, w, m, v, tr):
    rows, cols = w.shape

    def body(s_ref, d_ref, w_ref, m_ref, v_ref, g_ref, dl_ref, nm_ref, nv_ref):
        g = jnp.dot(s_ref[...], d_ref[...], precision=lax.Precision.HIGHEST, preferred_element_type=F32)
        g_ref[...] = g
        dl_ref[...], nm_ref[...], nv_ref[...] = _adamw_math(w_ref[...], g, m_ref[...], v_ref[...])

    spec = pl.BlockSpec((tr, cols), lambda i: (i, 0))
    shape = jax.ShapeDtypeStruct((rows, cols), F32)
    return pl.pallas_call(
        body, grid=(rows // tr,),
        in_specs=[pl.BlockSpec((tr, N_DEV), lambda i: (i, 0)), pl.BlockSpec((N_DEV, cols), lambda i: (0, 0)), spec, spec, spec],
        out_specs=[spec] * 4, out_shape=[shape] * 4, name="ada_update", compiler_params=_params(("parallel",)),
    )(sc_t, dmod_cols, w, m, v)


BIG = ("w_in", "w_branch_a", "w_branch_b", "w_out", "w_mlp_in", "w_mlp_out")
COLUMN_SHARDED = ("w_in", "w_branch_a", "w_branch_b", "w_mlp_in")
WEIGHTS = ("w_ada", "b_ada", "norm1_gain", "w_in", "lb_logits", "hgrn_o_gain", "q_norm_gain", "k_norm_gain", "sinks",
           "w_branch_a", "w_branch_b", "w_out", "norm2_gain", "w_mlp_in", "w_mlp_out")


def _pack_small(p):
    lb = p["lb_logits"]
    src = dict(p, lb0=lb[0:1], lb1=lb[1:2])
    return jnp.concatenate([jnp.pad(src[nm], ((0, 0), (0, w - src[nm].shape[1]))) for nm, w in SMALL_SEGS], axis=1)


def _unpack_small(vec, shapes):
    so = _offsets(SMALL_SEGS)
    out = {}
    for nm, shp in shapes.items():
        if nm == "lb_logits":
            o = so["lb0"][0]
            out[nm] = vec[0, o:o + 2 * AW].reshape(2, AW)
        else:
            o = so[nm][0]
            out[nm] = vec[:, o:o + shp[1]]
    return out


def kernel(x, c, w_ada, b_ada, norm1_gain, w_in, lb_logits, hgrn_o_gain, q_norm_gain, k_norm_gain, sinks, w_branch_a, w_branch_b, w_out, norm2_gain, w_mlp_in, w_mlp_out, loss_target, m_w_ada, m_b_ada, m_norm1_gain, m_w_in, m_lb_logits, m_hgrn_o_gain, m_q_norm_gain, m_k_norm_gain, m_sinks, m_w_branch_a, m_w_branch_b, m_w_out, m_norm2_gain, m_w_mlp_in, m_w_mlp_out, v_w_ada, v_b_ada, v_norm1_gain, v_w_in, v_lb_logits, v_hgrn_o_gain, v_q_norm_gain, v_k_norm_gain, v_sinks, v_w_branch_a, v_w_branch_b, v_w_out, v_norm2_gain, v_w_mlp_in, v_w_mlp_out):
    w = dict(w_ada=w_ada, b_ada=b_ada, norm1_gain=norm1_gain, w_in=w_in, lb_logits=lb_logits, hgrn_o_gain=hgrn_o_gain,
             q_norm_gain=q_norm_gain, k_norm_gain=k_norm_gain, sinks=sinks, w_branch_a=w_branch_a, w_branch_b=w_branch_b,
             w_out=w_out, norm2_gain=norm2_gain, w_mlp_in=w_mlp_in, w_mlp_out=w_mlp_out)
    m = dict(w_ada=m_w_ada, b_ada=m_b_ada, norm1_gain=m_norm1_gain, w_in=m_w_in, lb_logits=m_lb_logits,
             hgrn_o_gain=m_hgrn_o_gain, q_norm_gain=m_q_norm_gain, k_norm_gain=m_k_norm_gain, sinks=m_sinks,
             w_branch_a=m_w_branch_a, w_branch_b=m_w_branch_b, w_out=m_w_out, norm2_gain=m_norm2_gain,
             w_mlp_in=m_w_mlp_in, w_mlp_out=m_w_mlp_out)
    v = dict(w_ada=v_w_ada, b_ada=v_b_ada, norm1_gain=v_norm1_gain, w_in=v_w_in, lb_logits=v_lb_logits,
             hgrn_o_gain=v_hgrn_o_gain, q_norm_gain=v_q_norm_gain, k_norm_gain=v_k_norm_gain, sinks=v_sinks,
             w_branch_a=v_w_branch_a, w_branch_b=v_w_branch_b, w_out=v_w_out, norm2_gain=v_norm2_gain,
             w_mlp_in=v_w_mlp_in, w_mlp_out=v_w_mlp_out)
    for d in (w, m, v):
        for nm in ("w_ada",) + BIG:
            d[nm] = d[nm][0]
    px, py, pc = _mesh_pos()
    me = _index((px, py, pc))
    c_arr = jnp.reshape(pc, (1,)).astype(jnp.int32)
    q_arr = jnp.reshape(2 * px + py, (1,)).astype(jnp.int32)

    shards = [(w[nm].T if nm in COLUMN_SHARDED else w[nm]).astype(BF16) for nm in BIG]
    b_shard = lax.dynamic_slice(b_ada, (0, me * ADA_W), (1, ADA_W))
    mod, sc_all = _ada_mod(c, w["w_ada"], b_shard)

    dx, sums, parts = _local_step(x[0], loss_target[0], mod, norm1_gain, norm2_gain, lb_logits, hgrn_o_gain,
                                  q_norm_gain, k_norm_gain, sinks, shards, c_arr)

    allx, g_small, loss = _small_reduce(parts, lb_logits)

    grad, delta, new_m, new_v = {}, {}, {}, {}
    for nm in BIG:
        s, r2 = sums[nm]
        grad[nm], delta[nm], new_m[nm], new_v[nm] = _sum_adamw(
            "adamw_" + nm, s, r2, q_arr, w[nm], m[nm], v[nm], nm in COLUMN_SHARDED, 128)

    dmod_cols = lax.dynamic_slice(allx[:, 0, :], (0, me * ADA_W), (N_DEV, ADA_W))
    grad["w_ada"], delta["w_ada"], new_m["w_ada"], new_v["w_ada"] = _ada_update(
        sc_all.T, dmod_cols, w["w_ada"], m["w_ada"], v["w_ada"], 256)

    small_names = [nm for nm in WEIGHTS if nm not in BIG and nm != "w_ada"]
    shapes = {nm: w[nm].shape for nm in small_names}
    ds, ms, vs = _adamw("adamw_small", _pack_small(w), g_small, _pack_small(m), _pack_small(v), 1)
    for dst, vec in ((grad, g_small), (delta, ds), (new_m, ms), (new_v, vs)):
        dst.update(_unpack_small(vec, shapes))

    def full(d, nm):
        return d[nm][None] if nm in BIG or nm == "w_ada" else d[nm]

    return (loss[0, 0], dx[None], *[full(grad, nm) for nm in WEIGHTS], *[full(delta, nm) for nm in WEIGHTS],
            *[full(new_m, nm) for nm in WEIGHTS], *[full(new_v, nm) for nm in WEIGHTS])
```

```python
import functools

import jax
import jax.numpy as jnp
from jax import lax
from jax.experimental import pallas as pl
from jax.experimental.pallas import tpu as pltpu

F32 = jnp.float32
BF16 = jnp.bfloat16
MESH = pl.DeviceIdType.MESH

N_DEV = 8
D = 2048
A_HEADS, A_HD, CHUNK = 8, 128, 64
AW = A_HEADS * A_HD
Q_HEADS, KV_HEADS, GROUP, B_HD, BLK = 16, 4, 4, 64, 128
BW = Q_HEADS * B_HD
KVW = KV_HEADS * B_HD
HID = 4 * D
IN_W = 4 * AW + BW + 2 * KVW + 2 * D
OFF_QA, OFF_FA, OFF_IA, OFF_GA = 0, AW, 2 * AW, 3 * AW
OFF_QB = 4 * AW
OFF_KB = OFF_QB + BW
OFF_VB = OFF_KB + KVW
OFF_GTA = OFF_VB + KVW
OFF_GTB = OFF_GTA + D
N_MOD = 6
EPS = 1e-6
LR, B1, B2, ADAM_EPS, WD, STEP = 1e-3, 0.9, 0.999, 1e-8, 0.01, 10
NEG = -1e30

VMEM_LIMIT = 56 * 1024 * 1024
MI_CUTS = (544, 864)
MO_CUT = 272

NN = (((1,), (0,)), ((), ()))
NT = (((1,), (1,)), ((), ()))
TN = (((0,), (0,)), ((), ()))
BNN = (((2,), (1,)), ((0,), (0,)))
BNT = (((2,), (2,)), ((0,), (0,)))
BTN = (((1,), (1,)), ((0,), (0,)))


def _dot(a, b, dims=NN):
    return lax.dot_general(a.astype(BF16), b.astype(BF16), dims, preferred_element_type=F32)


def _params(sem):
    return pltpu.CompilerParams(dimension_semantics=sem, vmem_limit_bytes=VMEM_LIMIT)


def _sigmoid(x):
    return 1.0 / (1.0 + jnp.exp(-x))


def _fold8(v):
    r, n = v.shape
    return jnp.sum(v.reshape(r // 8, 8, n), axis=0)


_VMEM = pl.BlockSpec(memory_space=pltpu.VMEM)
_ANY = pl.BlockSpec(memory_space=pl.ANY)
_SEMS = lambda n: pltpu.SemaphoreType.DMA((n,))


def _mesh_pos():
    return lax.axis_index("x"), lax.axis_index("y"), lax.axis_index("c")


def _flip(pos, k):
    return tuple(1 - p if (k >> s) & 1 else p for p, s in zip(pos, (2, 1, 0)))


def _index(pos):
    return 4 * pos[0] + 2 * pos[1] + pos[2]


class _Job:
    def __init__(self, ins, out_shape, sems, start, finish, aliases=None):
        self.ins, self.out_shape, self.sems, self.start, self.finish = list(ins), list(out_shape), list(sems), start, finish
        self.aliases = dict(aliases or {})


def _both(j1, j2):
    assert not j1.aliases and not j2.aliases
    n_in, n_out, n_sem = len(j1.ins), len(j1.out_shape), len(j1.sems)
    first = lambda ins, outs, sems: (ins[:n_in], outs[:n_out], sems[:n_sem])
    second = lambda ins, outs, sems: (ins[n_in:], outs[n_out:], sems[n_sem:])

    def start(*refs):
        j1.start(*first(*refs))
        j2.start(*second(*refs))

    def finish(*refs):
        j1.finish(*first(*refs))
        j2.finish(*second(*refs))

    return _Job(j1.ins + j2.ins, j1.out_shape + j2.out_shape, j1.sems + j2.sems, start, finish)


def _pcall(body, *, grid, in_specs, out_specs, out_shape, scratch_shapes, name, semantics, args, job=None):
    if job is None:
        outs = pl.pallas_call(body, grid=grid, in_specs=in_specs, out_specs=out_specs, out_shape=out_shape,
                              scratch_shapes=scratch_shapes, name=name, compiler_params=_params(semantics))(*args)
        return list(outs), []
    n_in, n_out, n_scr = len(in_specs), len(out_specs), len(scratch_shapes)
    j_in, j_out = len(job.ins), len(job.out_shape)
    steps = tuple(grid)

    def carrier(*refs):
        o = 0
        main_in, o = refs[o:o + n_in], o + n_in
        job_in, o = refs[o:o + j_in], o + j_in
        main_out, o = refs[o:o + n_out], o + n_out
        job_out, o = refs[o:o + j_out], o + j_out
        main_scr, job_sems = refs[o:o + n_scr], refs[o + n_scr:]
        ids = [pl.program_id(a) for a in range(len(steps))]
        first = functools.reduce(lambda p, q: p & q, [i == 0 for i in ids])
        last = functools.reduce(lambda p, q: p & q, [i == s - 1 for i, s in zip(ids, steps)])

        @pl.when(first)
        def _():
            job.start(job_in, job_out, job_sems)

        body(*main_in, *main_out, *main_scr)

        @pl.when(last)
        def _():
            job.finish(job_in, job_out, job_sems)

    outs = pl.pallas_call(
        carrier, grid=grid, in_specs=list(in_specs) + [_ANY] * j_in, out_specs=list(out_specs) + [_ANY] * j_out,
        out_shape=list(out_shape) + job.out_shape, scratch_shapes=list(scratch_shapes) + job.sems, name=name,
        input_output_aliases={n_in + i: n_out + o for i, o in job.aliases.items()},
        compiler_params=_params(("arbitrary",) * len(steps)),
    )(*args, *job.ins)
    return list(outs[:n_out]), list(outs[n_out:])


def _run_job(name, job):
    j_in, j_out = len(job.ins), len(job.out_shape)

    def body(*refs):
        ins, outs, sems = refs[:j_in], refs[j_in:j_in + j_out], refs[j_in + j_out:]
        job.start(ins, outs, sems)
        job.finish(ins, outs, sems)

    return list(pl.pallas_call(body, in_specs=[_ANY] * j_in, out_specs=[_ANY] * j_out, out_shape=job.out_shape,
                               scratch_shapes=job.sems, name=name,
                               input_output_aliases=job.aliases)(*job.ins))


def _gather_job(shards, rows=None, into=None):
    n = len(shards)
    rows = rows or [(0, s.shape[0]) for s in shards]
    into = into or [None] * n
    olds, aliases = [], {}
    for a, buf in enumerate(into):
        if buf is not None:
            aliases[n + len(olds)] = a
            olds.append(buf)

    def copies(ins, outs, sems):
        send_sems, recv_sems, local_sems = sems
        x, y, c = _mesh_pos()
        me, sib = (x, y, c), (x, y, 1 - c)
        chips = [(1 - x, y), (x, 1 - y), (1 - x, 1 - y)]

        def part(a, p):
            rs, (r0, r1) = shards[a].shape[0], rows[a]
            return outs[a].at[pl.ds(_index(p) * rs + r0, r1 - r0), :]

        own = lambda a: ins[a].at[pl.ds(rows[a][0], rows[a][1] - rows[a][0]), :]

        def copy(a, k, block, to, src=None):
            return pltpu.make_async_remote_copy(
                src_ref=part(a, block) if src is None else src, dst_ref=part(a, block),
                send_sem=send_sems.at[7 * a + k], recv_sem=recv_sems.at[7 * a + k], device_id=to, device_id_type=MESH)

        mine = [pltpu.make_async_copy(own(a), part(a, me), local_sems.at[a]) for a in range(n)]
        first = []
        for a in range(n):
            first.append(copy(a, 0, me, sib, src=own(a)))
            first += [copy(a, 1 + j, me, (*chip, c), src=own(a)) for j, chip in enumerate(chips)]
        return me, sib, c, chips, copy, mine, first

    def start(ins, outs, sems):
        *_, mine, first = copies(ins, outs, sems)
        for cp in mine + first:
            cp.start()

    def finish(ins, outs, sems):
        me, sib, c, chips, copy, mine, first = copies(ins, outs, sems)
        passed = []
        for j, chip in enumerate(chips):
            for a in range(n):
                copy(a, 1 + j, (*chip, c), me).wait_recv()
                cp = copy(a, 4 + j, (*chip, c), sib)
                cp.start()
                passed.append(cp)
        for a in range(n):
            copy(a, 0, sib, me).wait_recv()
            for j, chip in enumerate(chips):
                copy(a, 4 + j, (*chip, 1 - c), me).wait_recv()
        for cp in first + passed:
            cp.wait_send()
        for cp in mine:
            cp.wait()

    return _Job(list(shards) + olds, [jax.ShapeDtypeStruct((N_DEV * s.shape[0], s.shape[1]), s.dtype) for s in shards],
                [_SEMS(7 * n), _SEMS(7 * n), _SEMS(n)], start, finish, aliases)


def _gather_relay_job(shards):
    n = len(shards)

    def tools(ins, outs, sems):
        send_sems, recv_sems, local_sems = sems
        x, y, c = _mesh_pos()
        q = 2 * x + y
        chip_at = lambda rel: (1 - x if rel & 2 else x, 1 - y if rel & 1 else y)

        def rows(a, chip, core):
            rs = shards[a].shape[0]
            return outs[a].at[pl.ds((2 * chip + core) * rs, rs), :]

        def copy(a, slot, chip, core, to, src=None):
            blk = rows(a, chip, core)
            return pltpu.make_async_remote_copy(src_ref=blk if src is None else src, dst_ref=blk,
                                                send_sem=send_sems.at[7 * a + slot], recv_sem=recv_sems.at[7 * a + slot],
                                                device_id=to, device_id_type=MESH)

        mine = [pltpu.make_async_copy(ins[a], rows(a, q, c), local_sems.at[a]) for a in range(n)]
        first = [copy(a, slot, q, c, (x, y, 1 - c) if slot == 0 else (*chip_at(slot), c), src=ins[a])
                 for a in range(n) for slot in (0, 1, 2)]
        return x, y, c, q, chip_at, copy, mine, first

    def start(ins, outs, sems):
        *_, mine, first = tools(ins, outs, sems)
        for cp in mine + first:
            cp.start()

    def finish(ins, outs, sems):
        x, y, c, q, chip_at, copy, mine, first = tools(ins, outs, sems)
        me, sib = (x, y, c), (x, y, 1 - c)

        def relay(src, dst):
            for a in range(n):
                copy(a, src, q ^ src, c, me).wait_recv()
                copy(a, 3, q ^ src, c, (*chip_at(dst), c)).start()
                copy(a, 3 + src, q ^ src, c, sib).start()
            for a in range(n):
                copy(a, dst, q ^ dst, c, me).wait_recv()
                copy(a, 3 + dst, q ^ dst, c, sib).start()

        pl.when(c == 1)(lambda: relay(1, 2))
        pl.when(c == 0)(lambda: relay(2, 1))
        for a in range(n):
            copy(a, 3, q ^ 3, c, me).wait_recv()
            copy(a, 6, q ^ 3, c, sib).start()
        for a in range(n):
            copy(a, 0, q, 1 - c, me).wait_recv()
            for rel in (1, 2, 3):
                copy(a, 3 + rel, q ^ rel, 1 - c, me).wait_recv()
        for a in range(n):
            for slot in range(3, 7):
                copy(a, slot, q, c, sib).wait_send()
        for cp in first:
            cp.wait_send()
        for cp in mine:
            cp.wait()

    return _Job(shards, [jax.ShapeDtypeStruct((N_DEV * s.shape[0], s.shape[1]), s.dtype) for s in shards],
                [_SEMS(7 * n), _SEMS(7 * n), _SEMS(n)], start, finish)


def _pair_job(grads):
    n = len(grads)

    def copies(ins, outs, sems):
        send_sems, recv_sems = sems
        x, y, c = _mesh_pos()
        out = []
        for a in range(n):
            rs = grads[a].shape[0] // N_DEV
            for q in range(4):
                blk = ins[a].at[pl.ds((2 * q + 1 - c) * rs, rs), :]
                out.append(pltpu.make_async_remote_copy(
                    src_ref=blk, dst_ref=outs[a].at[q], send_sem=send_sems.at[4 * a + q], recv_sem=recv_sems.at[4 * a + q],
                    device_id=(x, y, 1 - c), device_id_type=MESH))
        return out

    def start(ins, outs, sems):
        for cp in copies(ins, outs, sems):
            cp.start()

    def finish(ins, outs, sems):
        for cp in copies(ins, outs, sems):
            cp.wait()

    return _Job(grads, [jax.ShapeDtypeStruct((4, g.shape[0] // N_DEV, g.shape[1]), g.dtype) for g in grads],
                [_SEMS(4 * n), _SEMS(4 * n)], start, finish)


def _chip_job(sums, rels=(1, 2, 3)):
    n, nr = len(sums), len(rels)

    def copies(ins, outs, sems):
        send_sems, recv_sems = sems
        x, y, c = _mesh_pos()
        out = []
        for a in range(n):
            for slot, r in enumerate(rels):
                px, py = (1 - x if r & 2 else x), (1 - y if r & 1 else y)
                out.append(pltpu.make_async_remote_copy(
                    src_ref=ins[a].at[2 * px + py], dst_ref=outs[a].at[slot], send_sem=send_sems.at[nr * a + slot],
                    recv_sem=recv_sems.at[nr * a + slot], device_id=(px, py, c), device_id_type=MESH))
        return out

    def start(ins, outs, sems):
        for cp in copies(ins, outs, sems):
            cp.start()

    def finish(ins, outs, sems):
        for cp in copies(ins, outs, sems):
            cp.wait()

    return _Job(sums, [jax.ShapeDtypeStruct((nr,) + s.shape[1:], s.dtype) for s in sums],
                [_SEMS(nr * n), _SEMS(nr * n)], start, finish)


def _mm(name, form, a_list, b, M, N, K, tm, tn, tk, extras, outs, epi, job=None):
    nI, nJ, nK = M // tm, N // tn, K // tk
    assert nI * tm == M and nJ * tn == N and nK * tk == K
    dims = {"nn": NN, "nt": NT, "tn": TN}[form]
    b_list = b if isinstance(b, list) else [(b, {"nn": N, "nt": K, "tn": N}[form])]
    nA, nB = len(a_list), len(b_list)
    assert nA == 1 or nB == 1
    assert nB == 1 or form in ("nn", "nt")
    AXIS = {"i": 0, "j": 1, "k": 2}
    a_axis, a_tile = ("i", tm) if form == "tn" else ("k", tk)
    b_axis, b_tile = ("k", tk) if form == "nt" else ("j", tn)

    def cut(pieces, tile, total):
        starts, s = [], 0
        for _, w in pieces:
            assert w % tile == 0
            starts.append(s // tile)
            s += w
        assert s == total
        return starts, [w // tile for _, w in pieces]

    a_st, a_cn = cut(a_list, a_tile, M if form == "tn" else K)
    b_st, b_cn = cut(b_list, b_tile, K if form == "nt" else N)

    def inside(idx, st, cn):
        return (idx >= st) & (idx < st + cn)

    def a_spec(p):
        st, cn = a_st[p], a_cn[p]
        if form == "tn":
            return pl.BlockSpec((tk, tm), lambda i, j, k: (jnp.where(inside(i, st, cn), k, 0), jnp.clip(i - st, 0, cn - 1)))
        return pl.BlockSpec((tm, tk), lambda i, j, k: (i, jnp.clip(k - st, 0, cn - 1)))

    def b_spec(p):
        st, cn = b_st[p], b_cn[p]
        if form == "nt":
            return pl.BlockSpec((tn, tk), lambda i, j, k: (j, jnp.clip(k - st, 0, cn - 1)))
        if nB == 1:
            return pl.BlockSpec((tk, tn), lambda i, j, k: (k, j))
        return pl.BlockSpec((tk, tn), lambda i, j, k: (jnp.where(inside(j, st, cn), k, 0), jnp.clip(j - st, 0, cn - 1)))

    in_specs = ([a_spec(p) for p in range(nA)] + [b_spec(p) for p in range(nB)]
                + [pl.BlockSpec(bs, im) for _, bs, im in extras])
    out_shape = [jax.ShapeDtypeStruct(s_, d_) for s_, d_, _, _ in outs]
    out_specs = [pl.BlockSpec(bs, im) for _, _, bs, im in outs]
    nE, nO = len(extras), len(outs)
    single = nA == 1 and nB == 1

    def body(*refs):
        a_refs, b_refs = refs[:nA], refs[nA:nA + nB]
        ex, ou = refs[nA + nB:nA + nB + nE], refs[nA + nB + nE:nA + nB + nE + nO]
        ids = [pl.program_id(a) for a in range(3)]

        def partial_of(p, q):
            return lax.dot_general(a_refs[p][...], b_refs[q][...], dims, preferred_element_type=F32)

        if nK == 1 and single:
            epi(partial_of(0, 0), ex, ou)
            return
        acc = refs[-1]
        k = ids[2]
        for p in range(nA):
            for q in range(nB):
                def first(p=p, q=q):
                    acc[...] = partial_of(p, q)

                def later(p=p, q=q):
                    acc[...] += partial_of(p, q)

                here = None
                if nA > 1:
                    here = inside(ids[AXIS[a_axis]], a_st[p], a_cn[p])
                if nB > 1:
                    here = inside(ids[AXIS[b_axis]], b_st[q], b_cn[q])
                pl.when(k == 0 if here is None else here & (k == 0))(first)
                pl.when(k > 0 if here is None else here & (k > 0))(later)

        @pl.when(k == nK - 1)
        def _():
            epi(acc[...], ex, ou)

    scratch = [] if (nK == 1 and single) else [pltpu.VMEM((tm, tn), F32)]
    res, job_res = _pcall(
        body, grid=(nI, nJ, nK), in_specs=in_specs, out_specs=out_specs, out_shape=out_shape, scratch_shapes=scratch,
        name=name, semantics=("parallel", "parallel", "arbitrary"),
        args=[a for a, _ in a_list] + [p for p, _ in b_list] + [e for e, _, _ in extras], job=job)
    return res if job is None else (res, job_res)


def _twin_mm(name, form, pairs, M, N, K, tm, tn, tk, out_dtype):
    nI, nJ, nK = M // tm, N // tn, K // tk
    dims = {"nn": NN, "tn": TN}[form]
    a_spec = (pl.BlockSpec((tm, tk), lambda i, j, k: (i, k)) if form == "nn" else pl.BlockSpec((tk, tm), lambda i, j, k: (k, i)))
    b_spec = pl.BlockSpec((tk, tn), lambda i, j, k: (k, j))
    o_spec = pl.BlockSpec((tm, tn), lambda i, j, k: (i, j))

    def body(a1, b1, a2, b2, o1, o2, *accs):
        k = pl.program_id(2)
        for a_ref, b_ref, o_ref, acc in ((a1, b1, o1, accs[0] if accs else None), (a2, b2, o2, accs[1] if accs else None)):
            part = lax.dot_general(a_ref[...], b_ref[...], dims, preferred_element_type=F32)
            if nK == 1:
                o_ref[...] = part.astype(out_dtype)
                continue

            @pl.when(k == 0)
            def _(acc=acc, part=part):
                acc[...] = part

            @pl.when(k > 0)
            def _(acc=acc, part=part):
                acc[...] += part

            @pl.when(k == nK - 1)
            def _(acc=acc, o_ref=o_ref):
                o_ref[...] = acc[...].astype(out_dtype)

    (a1, b1), (a2, b2) = pairs
    shape = jax.ShapeDtypeStruct((M, N), out_dtype)
    return pl.pallas_call(
        body, grid=(nI, nJ, nK), in_specs=[a_spec, b_spec, a_spec, b_spec], out_specs=[o_spec, o_spec],
        out_shape=[shape, shape], scratch_shapes=[] if nK == 1 else [pltpu.VMEM((tm, tn), F32)] * 2, name=name,
        compiler_params=_params(("parallel", "parallel", "arbitrary")))(a1, b1, a2, b2)


def _piece_tiles(pieces, tile):
    starts, s = [], 0
    for _, w in pieces:
        assert w % tile == 0
        starts.append(s // tile)
        s += w
    return starts, [w // tile for _, w in pieces], s


def _pieces_tn(name, pieces, b, tile, job=None):
    T, N = b.shape
    st, cn, M = _piece_tiles(pieces, tile)
    nP, nI = len(pieces), M // tile

    def body(*refs):
        p_refs, b_hbm, o_ref = refs[:nP], refs[nP], refs[nP + 1]
        bbuf, abuf, bsem, asem = refs[nP + 2:]
        i = pl.program_id(0)

        def fetch(step, slot):
            for p in range(nP):
                @pl.when((step >= st[p]) & (step < st[p] + cn[p]))
                def _():
                    col = pl.multiple_of((step - st[p]) * tile, tile)
                    pltpu.make_async_copy(p_refs[p].at[pl.ds(0, T), pl.ds(col, tile)], abuf.at[slot], asem.at[slot]).start()

        @pl.when(i == 0)
        def _():
            whole = pltpu.make_async_copy(b_hbm, bbuf, bsem)
            whole.start()
            fetch(0, 0)
            whole.wait()

        @pl.when(i + 1 < nI)
        def _():
            fetch(i + 1, (i + 1) % 2)

        pltpu.make_async_copy(p_refs[0].at[pl.ds(0, T), pl.ds(0, tile)], abuf.at[i % 2], asem.at[i % 2]).wait()
        o_ref[...] = lax.dot_general(abuf[i % 2], bbuf[...], TN, preferred_element_type=F32).astype(BF16)

    res, job_res = _pcall(
        body, grid=(nI,), in_specs=[_ANY] * (nP + 1), out_specs=[pl.BlockSpec((tile, N), lambda i: (i, 0))],
        out_shape=[jax.ShapeDtypeStruct((M, N), BF16)],
        scratch_shapes=[pltpu.VMEM((T, N), b.dtype), pltpu.VMEM((2, T, tile), b.dtype), pltpu.SemaphoreType.DMA, _SEMS(2)],
        name=name, semantics=("arbitrary",), args=[p for p, _ in pieces] + [b], job=job)
    return res if job is None else (res, job_res)


def _rows_mm(name, pieces, w, T, tm, tk, vecs, bufs, parts, epi, job=None):
    st, cn, K = _piece_tiles(pieces, tk)
    nP, nI, nK = len(pieces), T // tm, K // tk
    part_specs = [pl.BlockSpec(bs, lambda i, k, im=im: im(i, 0, k)) for _, _, bs, im in parts]
    n_vec, nB = len(vecs), len(bufs)
    load_ix = [n for n, (_, src, _) in enumerate(bufs) if src is not None]
    store_ix = [n for n, (_, _, store) in enumerate(bufs) if store]
    n_any_in, n_any_out = len(load_ix), len(store_ix)

    def body(*refs):
        o = nP
        p_refs, w_ref = refs[:nP], refs[o]
        vec_refs = refs[o + 1:o + 1 + n_vec]
        ins = refs[o + 1 + n_vec:o + 1 + n_vec + n_any_in]
        o = o + 1 + n_vec + n_any_in
        hbm_outs, p_outs = refs[o:o + n_any_out], refs[o + n_any_out:o + n_any_out + len(parts)]
        o = o + n_any_out + len(parts)
        acc, abuf = refs[o:o + 2]
        buf_refs = refs[o + 2:o + 2 + nB]
        asem, in_sems, out_sems = refs[-3:]
        i, k = pl.program_id(0), pl.program_id(1)
        g = i * nK + k
        rows_of = lambda ref, ii: ref.at[pl.ds(pl.multiple_of(ii * tm, tm), tm), :]
        bufs_in = [buf_refs[n] for n in load_ix]
        bufs_out = [buf_refs[n] for n in store_ix]

        def fetch(ii, kk, slot):
            for p in range(nP):
                @pl.when((kk >= st[p]) & (kk < st[p] + cn[p]))
                def _():
                    col = pl.multiple_of((kk - st[p]) * tk, tk)
                    src = p_refs[p].at[pl.ds(pl.multiple_of(ii * tm, tm), tm), pl.ds(col, tk)]
                    pltpu.make_async_copy(src, abuf.at[slot], asem.at[slot]).start()

        loads = lambda ii: [pltpu.make_async_copy(rows_of(src, ii), buf, in_sems.at[n])
                            for n, (src, buf) in enumerate(zip(ins, bufs_in))]
        stores = lambda ii: [pltpu.make_async_copy(buf, rows_of(dst, ii), out_sems.at[n])
                             for n, (buf, dst) in enumerate(zip(bufs_out, hbm_outs))]

        @pl.when(g == 0)
        def _():
            fetch(0, 0, 0)

        @pl.when(g + 1 < nI * nK)
        def _():
            last_k = k == nK - 1
            fetch(jnp.where(last_k, i + 1, i), jnp.where(last_k, 0, k + 1), (g + 1) % 2)

        @pl.when(k == 0)
        def _():
            @pl.when(i > 0)
            def _():
                for cp in stores(i - 1):
                    cp.wait()
            for cp in loads(i):
                cp.start()

        pltpu.make_async_copy(p_refs[0].at[pl.ds(0, tm), pl.ds(0, tk)], abuf.at[g % 2], asem.at[g % 2]).wait()

        def product(cols):
            return jnp.dot(abuf[g % 2], w_ref[:, cols], preferred_element_type=F32)

        col_blocks = [slice(c0, c0 + 512) for c0 in range(0, D, 512)]

        @pl.when(k == 0)
        def _():
            for cols in col_blocks:
                acc[:, cols] = product(cols)

        @pl.when(k > 0)
        def _():
            for cols in col_blocks:
                acc[:, cols] += product(cols)

        @pl.when(k == nK - 1)
        def _():
            for cp in loads(i):
                cp.wait()
            epi(acc, vec_refs, buf_refs, p_outs)
            for cp in stores(i):
                cp.start()

            @pl.when(i == nI - 1)
            def _():
                for cp in stores(i):
                    cp.wait()

    vec = pl.BlockSpec((1, D), lambda i, k: (0, 0))
    scratch = ([pltpu.VMEM((tm, D), F32), pltpu.VMEM((2, tm, tk), BF16)] + [pltpu.VMEM((tm, D), dt) for dt, _, _ in bufs]
               + [_SEMS(2), _SEMS(n_any_in), _SEMS(n_any_out)])
    res, job_res = _pcall(
        body, grid=(nI, nK),
        in_specs=[_ANY] * nP + [pl.BlockSpec((tk, D), lambda i, k: (k, 0))] + [vec] * n_vec + [_ANY] * n_any_in,
        out_specs=[_ANY] * n_any_out + part_specs,
        out_shape=([jax.ShapeDtypeStruct((T, D), bufs[n][0]) for n in store_ix]
                   + [jax.ShapeDtypeStruct(s, d) for s, d, _, _ in parts]),
        scratch_shapes=scratch, name=name, semantics=("arbitrary", "arbitrary"),
        args=[p for p, _ in pieces] + [w] + list(vecs) + [bufs[n][1] for n in load_ix], job=job)
    return res if job is None else (res, job_res)


def _pieces_nn_rms(name, pieces, w, x, gain, sc, dres, tm, tk, job=None):
    _, outs, epi = _rms_mod_bwd_epilogue(x, gain, sc, dres, tm)

    def on_rows(acc, vecs, bufs, parts):
        epi(acc, [bufs[0], vecs[0], vecs[1], bufs[1]], [bufs[1], *parts])

    return _rows_mm(name, pieces, w, x.shape[0], tm, tk, [gain, sc], [(F32, x, False), (F32, dres, True)],
                    outs[1:], on_rows, job=job)


def _rms_mod_fwd(name, x, gain, sc, sh, tr):
    T = x.shape[0]

    def body(x_ref, g_ref, sc_ref, sh_ref, h_ref):
        xv = x_ref[...]
        rstd = lax.rsqrt(jnp.mean(xv * xv, axis=-1, keepdims=True) + EPS)
        h_ref[...] = ((xv * rstd * g_ref[...]) * (1.0 + sc_ref[...]) + sh_ref[...]).astype(BF16)

    row = pl.BlockSpec((tr, D), lambda i: (i, 0))
    vec = pl.BlockSpec((1, D), lambda i: (0, 0))
    return pl.pallas_call(
        body, grid=(T // tr,), in_specs=[row, vec, vec, vec], out_specs=row,
        out_shape=jax.ShapeDtypeStruct((T, D), BF16), name=name, compiler_params=_params(("parallel",)),
    )(x, gain, sc, sh)


def _rms_mod_bwd_epilogue(x, gain, sc, dres, tm, gate=None, mo=None):
    T = x.shape[0]
    with_gate = gate is not None
    row = ((tm, D), lambda i, j, k: (i, 0))
    vec = ((1, D), lambda i, j, k: (0, 0))
    part = ((T // tm * 8, D), F32, (8, D), lambda i, j, k: (i, 0))
    extras = [(x, *row), (gain, *vec), (sc, *vec), (dres, *row)]
    outs = [((T, D), F32, *row), part, part, part]
    if with_gate:
        extras += [(gate, *vec), (mo, *row)]
        outs += [((T, D), BF16, *row), part]

    rows = min(64, tm)

    def epi(acc, ex, ou):
        g = ex[1][...]
        sums = [jnp.zeros((8, D), F32) for _ in range(4)]
        for r0 in range(0, tm, rows):
            rs = slice(r0, r0 + rows)
            dhv, xv = acc[rs, :], ex[0][rs, :]
            rstd = lax.rsqrt(jnp.mean(xv * xv, axis=-1, keepdims=True) + EPS)
            xhat = xv * rstd
            dn = dhv * (1.0 + ex[2][...])
            dxhat = dn * g
            dx = ex[3][rs, :] + rstd * (dxhat - xhat * jnp.mean(dxhat * xhat, axis=-1, keepdims=True))
            ou[0][rs, :] = dx
            terms = [dhv, dhv * (xhat * g), dn * xhat]
            if with_gate:
                terms.append(dx * ex[5][rs, :].astype(F32))
                ou[4][rs, :] = (ex[4][...] * dx).astype(BF16)
            sums = [s + _fold8(t) for s, t in zip(sums, terms)] + sums[len(terms):]
        ou[1][...], ou[2][...], ou[3][...] = sums[:3]
        if with_gate:
            ou[5][...] = sums[3]

    return extras, outs, epi


def _rms_mod_bwd(name, dh, x, gain, sc, dres, tr, gate=None, mo=None):
    T = x.shape[0]
    extras, outs, epi = _rms_mod_bwd_epilogue(x, gain, sc, dres, tr, gate, mo)
    rows_only = lambda im: (lambda i: im(i, 0, 0))
    nE = len(extras)

    def body(dh_ref, *refs):
        epi(dh_ref, refs[:nE], refs[nE:])

    return pl.pallas_call(
        body, grid=(T // tr,),
        in_specs=[pl.BlockSpec((tr, D), lambda i: (i, 0))] + [pl.BlockSpec(bs, rows_only(im)) for _, bs, im in extras],
        out_specs=[pl.BlockSpec(bs, rows_only(im)) for _, _, bs, im in outs],
        out_shape=[jax.ShapeDtypeStruct(s, d) for s, d, _, _ in outs], name=name, compiler_params=_params(("parallel",)),
    )(dh, *[e for e, _, _ in extras])


def _split3(v):
    h = v.astype(BF16)
    r1 = v - h.astype(F32)
    m = r1.astype(BF16)
    lo = (r1 - m.astype(F32)).astype(BF16)
    return h, m, lo


def _tri_mm(tri, v, dims=NN):
    h, m, lo = _split3(v)
    t = tri.astype(BF16)
    mm = lambda p: lax.dot_general(t, p, dims, preferred_element_type=F32)
    return (mm(lo) + mm(m)) + mm(h)


def _hgrn_chunk_terms(q, fl, lb):
    sig = _sigmoid(fl)
    f = lb + (1.0 - lb) * sig
    lf = jnp.log(f)
    kk = 1.0 - f
    sq = _sigmoid(q)
    qf = q * sq
    return sig, f, lf, kk, sq, qf


def _causal(n):
    r = lax.broadcasted_iota(jnp.int32, (n, n), 0)
    c = lax.broadcasted_iota(jnp.int32, (n, n), 1)
    return r >= c


def _hgrn_fwd(proj, lb_logits, o_gain, tt, job=None):
    T = proj.shape[0]
    nT, ncl = T // tt, tt // CHUNK
    C = CHUNK

    def body(q_ref, f_ref, i_ref, g_ref, lbl_ref, og_ref, y_ref, st_ref, S):
        @pl.when(pl.program_id(1) == 0)
        def _():
            S[...] = jnp.zeros_like(S)

        lbl = lbl_ref[...]
        lb = _sigmoid(lbl[0:1, :] - lbl[1:2, :])
        og = og_ref[...]
        shp = (ncl, C, A_HD)
        q, fl, v, g = (r[...].reshape(shp) for r in (q_ref, f_ref, i_ref, g_ref))
        tri = jnp.broadcast_to(_causal(C), (ncl, C, C))
        _, _, lf, kk, _, qf = _hgrn_chunk_terms(q, fl, lb)
        b = _tri_mm(tri, lf, BNN)
        bm, bl = b[:, C // 2 - 1:C // 2, :], b[:, C - 1:C, :]
        qd, kd = qf * jnp.exp(b - bm), kk * jnp.exp(bm - b)
        A = jnp.where(tri, _dot(qd, kd, BNT), 0.0)
        d_st = _dot(v, kk * jnp.exp(bl - b), BTN)
        dec = jnp.exp(bl)
        st = S[...]
        for ci in range(ncl):
            st_ref[0, ci] = st
            st = st * dec[ci] + d_st[ci]
        S[...] = st
        o = _dot(A, v, BNN) + _dot(qf * jnp.exp(b), st_ref[0], BNT)
        r = lax.rsqrt(jnp.mean(o * o, axis=-1, keepdims=True) + EPS)
        y_ref[...] = (o * r * og * (g * _sigmoid(g))).astype(BF16).reshape(tt, A_HD)

    def col(off):
        return pl.BlockSpec((tt, A_HD), lambda h, t: (t, off // A_HD + h))

    head_vec = lambda rows: pl.BlockSpec((rows, A_HD), lambda h, t: (0, h))
    return _pcall(
        body, grid=(A_HEADS, nT),
        in_specs=[col(OFF_QA), col(OFF_FA), col(OFF_IA), col(OFF_GA), head_vec(2), head_vec(1)],
        out_specs=[pl.BlockSpec((tt, A_HD), lambda h, t: (t, h)),
                   pl.BlockSpec((1, ncl, A_HD, A_HD), lambda h, t: (h, t, 0, 0))],
        out_shape=[jax.ShapeDtypeStruct((T, AW), BF16),
                   jax.ShapeDtypeStruct((A_HEADS, T // C, A_HD, A_HD), F32)],
        scratch_shapes=[pltpu.VMEM((A_HD, A_HD), F32)], name="hgrn_fwd", semantics=("parallel", "arbitrary"),
        args=[proj, proj, proj, proj, lb_logits, o_gain], job=job)


def _hgrn_bwd(proj, st, dy, lb_logits, o_gain, tt, job=None):
    T = proj.shape[0]
    nT, ncl = T // tt, tt // CHUNK
    C = CHUNK

    def body(q_ref, f_ref, i_ref, g_ref, st_ref, dy_ref, lbl_ref, og_ref,
             dq_ref, df_ref, di_ref, dg_ref, plb_ref, pog_ref, dS):
        @pl.when(pl.program_id(1) == 0)
        def _():
            dS[...] = jnp.zeros_like(dS)

        lbl = lbl_ref[...]
        lb = _sigmoid(lbl[0:1, :] - lbl[1:2, :])
        og = og_ref[...]
        shp = (ncl, C, A_HD)
        flat = lambda t: t.reshape(tt, A_HD)
        q, fl, v, g, dout = (r[...].reshape(shp) for r in (q_ref, f_ref, i_ref, g_ref, dy_ref))
        tri = jnp.broadcast_to(_causal(C), (ncl, C, C))
        rowi = lax.broadcasted_iota(jnp.int32, shp, 1)
        st0 = st_ref[0]
        sig, f, lf, kk, sq, qf = _hgrn_chunk_terms(q, fl, lb)
        b = _tri_mm(tri, lf, BNN)
        bm, bl = b[:, C // 2 - 1:C // 2, :], b[:, C - 1:C, :]
        e_qd, e_kd, e_ke, e_b = jnp.exp(b - bm), jnp.exp(bm - b), jnp.exp(bl - b), jnp.exp(b)
        qd, kd, ke, qe = qf * e_qd, kk * e_kd, kk * e_ke, qf * e_b
        dec = jnp.exp(bl)
        A = jnp.where(tri, _dot(qd, kd, BNT), 0.0)
        o = _dot(A, v, BNN) + _dot(qe, st0, BNT)
        r = lax.rsqrt(jnp.mean(o * o, axis=-1, keepdims=True) + EPS)
        sg = _sigmoid(g)
        on = o * r * og
        dg_ref[...] = flat((dout * on * (sg * (1.0 + g * (1.0 - sg)))).astype(BF16))
        don = dout * (g * sg)
        pog_ref[...] = _fold8(flat(don * o * r))
        dyh = don * og
        do = r * (dyh - o * (r * r) * jnp.mean(dyh * o, axis=-1, keepdims=True))
        g_st = _dot(do, qe, BTN)
        run = dS[...]
        after = [None] * ncl
        for ci in reversed(range(ncl)):
            after[ci] = run
            run = g_st[ci] + run * dec[ci]
        dS[...] = run
        d_after = jnp.stack(after, axis=0)
        ddec = jnp.sum(d_after * st0, axis=1, keepdims=True)
        dqe = _dot(do, st0, BNN)
        dke = _dot(v, d_after, BNN)
        dA = jnp.where(tri, _dot(do, v, BNT), 0.0)
        dv = _dot(ke, d_after, BNT) + _dot(A, do, BTN)
        dqd = _dot(dA, kd, BNN)
        dkd = _dot(dA, qd, BTN)
        di_ref[...] = flat(dv.astype(BF16))
        dqf = dqe * e_b + dqd * e_qd
        dkk = dkd * e_kd + dke * e_ke
        t_qd, t_kd, t_ke = dqd * qd, dkd * kd, dke * ke
        db = dqe * qe + t_qd - t_kd - t_ke
        dbm = jnp.sum(t_kd - t_qd, axis=1, keepdims=True)
        dbl = jnp.sum(t_ke, axis=1, keepdims=True) + ddec * dec
        db = db + jnp.where(rowi == C // 2 - 1, dbm, 0.0) + jnp.where(rowi == C - 1, dbl, 0.0)
        dlf = _tri_mm(tri, db, BTN)
        dfv = dlf / f - dkk
        df_ref[...] = flat((dfv * (1.0 - lb) * sig * (1.0 - sig)).astype(BF16))
        plb_ref[...] = _fold8(flat(dfv * (1.0 - sig)))
        dq_ref[...] = flat((dqf * (sq * (1.0 + q * (1.0 - sq)))).astype(BF16))

    def col(off):
        return pl.BlockSpec((tt, A_HD), lambda h, t: (nT - 1 - t, off // A_HD + h))

    head_vec = lambda rows: pl.BlockSpec((rows, A_HD), lambda h, t: (0, h))
    o_spec = pl.BlockSpec((tt, A_HD), lambda h, t: (nT - 1 - t, h))
    p_spec = pl.BlockSpec((8, A_HD), lambda h, t: (t, h))
    o_shape = jax.ShapeDtypeStruct((T, AW), BF16)
    p_shape = jax.ShapeDtypeStruct((nT * 8, AW), F32)
    return _pcall(
        body, grid=(A_HEADS, nT),
        in_specs=[col(OFF_QA), col(OFF_FA), col(OFF_IA), col(OFF_GA),
                  pl.BlockSpec((1, ncl, A_HD, A_HD), lambda h, t: (h, nT - 1 - t, 0, 0)),
                  pl.BlockSpec((tt, A_HD), lambda h, t: (nT - 1 - t, h)), head_vec(2), head_vec(1)],
        out_specs=[o_spec, o_spec, o_spec, o_spec, p_spec, p_spec],
        out_shape=[o_shape, o_shape, o_shape, o_shape, p_shape, p_shape],
        scratch_shapes=[pltpu.VMEM((A_HD, A_HD), F32)], name="hgrn_bwd", semantics=("parallel", "arbitrary"),
        args=[proj, proj, proj, proj, st, dy, lb_logits, o_gain], job=job)


LANES = 128
Q_COLS = BW // LANES


def _low_half():
    return lax.broadcasted_iota(jnp.int32, (1, LANES), 1) < B_HD


def _half_sum(t, low):
    lo = jnp.sum(jnp.where(low, t, 0.0), axis=-1, keepdims=True)
    hi = jnp.sum(jnp.where(low, 0.0, t), axis=-1, keepdims=True)
    return jnp.where(low, lo, hi)


def _half_rms(t, low):
    r = lax.rsqrt(_half_sum(t * t, low) * (1.0 / B_HD) + EPS)
    return t * r, r


def _fold_halves(p, low):
    return jnp.where(low, p + pltpu.roll(p, B_HD, 1), 0.0)


def _stack_cols(x):
    return jnp.stack([x[:, c * LANES:(c + 1) * LANES] for c in range(Q_COLS)], axis=0).reshape(KV_HEADS, 2 * BLK, LANES)


def _col_of(t, c):
    return t[c // 2, (c % 2) * BLK:(c % 2 + 1) * BLK]


def _split_halves(col, s, low):
    own = jnp.where(low if s == 0 else jnp.logical_not(low), col, 0.0)
    other = pltpu.roll(own, B_HD, 1)
    return (own, other) if s == 0 else (other, own)


def _swa_keys(kp_ref, kc_ref, vp_ref, vc_ref, kg, low):
    k_lo, k_hi, v_lo, v_hi, hats = [], [], [], [], []
    for j in range(KVW // LANES):
        cs = slice(j * LANES, (j + 1) * LANES)
        k_hat, k_r = _half_rms(jnp.concatenate([kp_ref[:, cs], kc_ref[:, cs]], axis=0), low)
        vcol = jnp.concatenate([vp_ref[:, cs], vc_ref[:, cs]], axis=0)
        hats.append((k_hat, k_r))
        for s in range(2):
            for dst_lo, dst_hi, col in ((k_lo, k_hi, k_hat * kg), (v_lo, v_hi, vcol)):
                lo, hi = _split_halves(col, s, low)
                dst_lo.append(lo)
                dst_hi.append(hi)
    st = lambda parts: jnp.stack(parts, axis=0)
    return st(k_lo), st(k_hi), st(v_lo), st(v_hi), hats


def _swa_mask(first_block):
    qi = lax.broadcasted_iota(jnp.int32, (BLK, 2 * BLK), 0) + BLK
    ki = lax.broadcasted_iota(jnp.int32, (BLK, 2 * BLK), 1)
    rel = qi - ki
    m = (rel >= 0) & (rel < BLK) & (jnp.logical_not(first_block) | (ki >= BLK))
    return jnp.concatenate([m, m], axis=0)


def _sink_cols(sk_ref, hi):
    top = lax.broadcasted_iota(jnp.int32, (2 * BLK, 1), 0) < BLK
    return jnp.stack([jnp.where(top, sk_ref[0, GROUP * hk + hi], sk_ref[0, GROUP * hk + 2 + hi])
                      for hk in range(KV_HEADS)], axis=0)


def _swa_probs(qn, k_half, sink, mask):
    s = jnp.where(mask, _dot(qn, k_half, BNT) * (B_HD ** -0.5), NEG)
    m = jnp.maximum(jnp.max(s, axis=-1, keepdims=True), sink)
    p = jnp.exp(s - m)
    ps = jnp.exp(sink - m)
    inv = 1.0 / (jnp.sum(p, axis=-1, keepdims=True) + ps)
    return p * inv, ps * inv


def _swa_fwd(proj, q_gain, k_gain, sinks, job=None):
    T = proj.shape[0]
    nb = T // BLK

    def body(q_ref, kc_ref, kp_ref, vc_ref, vp_ref, qg_ref, kg_ref, sk_ref, o_ref):
        low = _low_half()
        mask = _swa_mask(pl.program_id(0) == 0)
        qn = _half_rms(_stack_cols(q_ref[...]), low)[0] * qg_ref[...]
        k_lo, k_hi, v_lo, v_hi, _ = _swa_keys(kp_ref, kc_ref, vp_ref, vc_ref, kg_ref[...], low)
        p_lo, _ = _swa_probs(qn, k_lo, _sink_cols(sk_ref, 0), mask)
        p_hi, _ = _swa_probs(qn, k_hi, _sink_cols(sk_ref, 1), mask)
        o = (_dot(p_lo, v_lo, BNN) + _dot(p_hi, v_hi, BNN)).astype(BF16)
        for c in range(Q_COLS):
            o_ref[:, c * LANES:(c + 1) * LANES] = _col_of(o, c)

    q_gain, k_gain = jnp.tile(q_gain, (1, 2)), jnp.tile(k_gain, (1, 2))
    cur = lambda w, off: pl.BlockSpec((BLK, w), lambda i: (i, off // w))
    prev = lambda w, off: pl.BlockSpec((BLK, w), lambda i: (jnp.maximum(i - 1, 0), off // w))
    small = lambda n: pl.BlockSpec((1, 2 * n), lambda i: (0, 0))
    return _pcall(
        body, grid=(nb,),
        in_specs=[cur(BW, OFF_QB), cur(KVW, OFF_KB), prev(KVW, OFF_KB), cur(KVW, OFF_VB), prev(KVW, OFF_VB),
                  small(B_HD), small(B_HD), pl.BlockSpec(memory_space=pltpu.SMEM)],
        out_specs=[pl.BlockSpec((BLK, BW), lambda i: (i, 0))],
        out_shape=[jax.ShapeDtypeStruct((T, BW), BF16)], scratch_shapes=[], name="swa_fwd", semantics=("parallel",),
        args=[proj, proj, proj, proj, proj, q_gain, k_gain, sinks], job=job)


def _swa_bwd(proj, dout, q_gain, k_gain, sinks, job=None):
    T = proj.shape[0]
    nb = T // BLK
    W = BW + 2 * KVW

    def body(q_ref, kc_ref, kp_ref, vc_ref, vp_ref, do_ref, qg_ref, kg_ref, sk_ref,
             dq_ref, dkv_ref, pqg_ref, pkg_ref, psk_ref, dkn_c, dv_c):
        i = pl.program_id(0)
        live = i < nb
        low = _low_half()
        high = jnp.logical_not(low)
        qg, kg = qg_ref[...], kg_ref[...]
        mask = _swa_mask(i == 0)
        lane = lax.broadcasted_iota(jnp.int32, (1, LANES), 1)
        scale = B_HD ** -0.5

        @pl.when(i == 0)
        def _():
            dkn_c[...] = jnp.zeros_like(dkn_c)
            dv_c[...] = jnp.zeros_like(dv_c)

        q_hat, q_r = _half_rms(_stack_cols(q_ref[...]), low)
        qn = q_hat * qg
        k_lo, k_hi, v_lo, v_hi, hats = _swa_keys(kp_ref, kc_ref, vp_ref, vc_ref, kg, low)
        do = _stack_cols(do_ref[...])
        dqn = jnp.zeros((KV_HEADS, 2 * BLK, LANES), F32)
        acc_sk = jnp.zeros((1, LANES), F32)
        dk_parts, dv_parts = [], []
        for hi, (k_h, v_h) in enumerate(((k_lo, v_lo), (k_hi, v_hi))):
            p, ps = _swa_probs(qn, k_h, _sink_cols(sk_ref, hi), mask)
            dp = _dot(do, v_h, BNT)
            delta = jnp.sum(p * dp, axis=-1, keepdims=True)
            ds = p * (dp - delta) * scale
            dqn = dqn + _dot(ds, k_h, BNN)
            dk_parts.append(_dot(ds, qn, BTN))
            dv_parts.append(_dot(p, do, BTN))
            t = ps * delta
            for hk in range(KV_HEADS):
                for rows in range(2):
                    h = GROUP * hk + 2 * rows + hi
                    acc_sk = acc_sk + jnp.where(
                        lane == h, -jnp.sum(t[hk, rows * BLK:(rows + 1) * BLK], axis=0, keepdims=True), 0.0)
        dqh = dqn * qg
        dq = (q_r * (dqh - q_hat * (_half_sum(dqh * q_hat, low) * (1.0 / B_HD)))).astype(BF16)
        for c in range(Q_COLS):
            dq_ref[:, c * LANES:(c + 1) * LANES] = _col_of(dq, c)
        acc_qg = _fold_halves(_fold8((dqn * q_hat).reshape(KV_HEADS * 2 * BLK, LANES)), low)

        def native(parts, j):
            lo_arr, hi_arr = parts
            a, b = 2 * j, 2 * j + 1
            return (jnp.where(low, lo_arr[a], 0.0) + pltpu.roll(jnp.where(high, hi_arr[a], 0.0), B_HD, 1)
                    + jnp.where(high, hi_arr[b], 0.0) + pltpu.roll(jnp.where(low, lo_arr[b], 0.0), B_HD, 1))

        acc_kg = jnp.zeros((8, LANES), F32)
        for j in range(KVW // LANES):
            cs = slice(j * LANES, (j + 1) * LANES)
            dkn = jnp.where(live, native(dk_parts, j), 0.0)
            dvc = jnp.where(live, native(dv_parts, j), 0.0)
            kp_hat, kp_r = hats[j][0][:BLK], hats[j][1][:BLK]
            dkn_prev = dkn_c[:, cs] + dkn[:BLK]
            dv_prev = dv_c[:, cs] + dvc[:BLK]
            acc_kg = acc_kg + _fold8(dkn_prev * kp_hat)
            dkh = dkn_prev * kg
            dkv_ref[:, cs] = (kp_r * (dkh - kp_hat * (_half_sum(dkh * kp_hat, low) * (1.0 / B_HD)))).astype(BF16)
            dkv_ref[:, KVW + j * LANES:KVW + (j + 1) * LANES] = dv_prev.astype(BF16)
            dkn_c[:, cs] = dkn[BLK:]
            dv_c[:, cs] = dvc[BLK:]
        keep = jnp.where(i > 0, 1.0, 0.0)
        pqg_ref[...] = jnp.where(live, acc_qg, 0.0)
        pkg_ref[...] = _fold_halves(acc_kg, low) * keep
        psk_ref[...] = jnp.broadcast_to(jnp.where(live, acc_sk, 0.0), (8, LANES)) * (
            lax.broadcasted_iota(jnp.int32, (8, LANES), 0) == 0).astype(F32)

    q_gain, k_gain = jnp.tile(q_gain, (1, 2)), jnp.tile(k_gain, (1, 2))
    last = nb - 1
    cur = lambda w, off: pl.BlockSpec((BLK, w), lambda i: (jnp.minimum(i, last), off // w))
    prev = lambda w, off: pl.BlockSpec((BLK, w), lambda i: (jnp.maximum(i - 1, 0), off // w))
    small = lambda n: pl.BlockSpec((1, 2 * n), lambda i: (0, 0))
    part = pl.BlockSpec((8, 128), lambda i: (i, 0))
    p_shape = jax.ShapeDtypeStruct(((nb + 1) * 8, 128), F32)
    return _pcall(
        body, grid=(nb + 1,),
        in_specs=[cur(BW, OFF_QB), cur(KVW, OFF_KB), prev(KVW, OFF_KB), cur(KVW, OFF_VB), prev(KVW, OFF_VB),
                  pl.BlockSpec((BLK, BW), lambda i: (jnp.minimum(i, last), 0)), small(B_HD), small(B_HD),
                  pl.BlockSpec(memory_space=pltpu.SMEM)],
        out_specs=[pl.BlockSpec((BLK, BW), lambda i: (i, 0)),
                   pl.BlockSpec((BLK, 2 * KVW), lambda i: (jnp.maximum(i - 1, 0), 0)), part, part, part],
        out_shape=[jax.ShapeDtypeStruct((T + BLK, BW), BF16), jax.ShapeDtypeStruct((T, 2 * KVW), BF16),
                   p_shape, p_shape, p_shape],
        scratch_shapes=[pltpu.VMEM((BLK, KVW), F32), pltpu.VMEM((BLK, KVW), F32)], name="swa_bwd",
        semantics=("arbitrary",), args=[proj, proj, proj, proj, proj, dout, q_gain, k_gain, sinks], job=job)


def _branch_merge(ya_pre, attn, wa_t, wb_t, proj, tm, tn, job=None):
    T = ya_pre.shape[0]

    def body(a_ref, b_ref, wa_ref, wb_ref, ga_ref, gb_ref, ya_ref, yb_ref, mg_ref):
        ya = lax.dot_general(a_ref[...], wa_ref[...], NT, preferred_element_type=F32)
        yb = lax.dot_general(b_ref[...], wb_ref[...], NT, preferred_element_type=F32)
        ya_ref[...] = ya.astype(BF16)
        yb_ref[...] = yb.astype(BF16)
        mg_ref[...] = (_sigmoid(ga_ref[...]) * ya + _sigmoid(gb_ref[...]) * yb).astype(BF16)

    o_spec = pl.BlockSpec((tm, tn), lambda i, j: (i, j))
    o_shape = jax.ShapeDtypeStruct((T, D), BF16)
    return _pcall(
        body, grid=(T // tm, D // tn),
        in_specs=[pl.BlockSpec((tm, AW), lambda i, j: (i, 0)), pl.BlockSpec((tm, BW), lambda i, j: (i, 0)),
                  pl.BlockSpec((tn, AW), lambda i, j: (j, 0)), pl.BlockSpec((tn, BW), lambda i, j: (j, 0)),
                  pl.BlockSpec((tm, tn), lambda i, j: (i, OFF_GTA // tn + j)),
                  pl.BlockSpec((tm, tn), lambda i, j: (i, OFF_GTB // tn + j))],
        out_specs=[o_spec, o_spec, o_spec], out_shape=[o_shape, o_shape, o_shape], scratch_shapes=[], name="branch_merge",
        semantics=("parallel", "parallel"), args=[ya_pre, attn, wa_t, wb_t, proj, proj], job=job)


def _ij(i, j, k):
    return (i, j)


def _local_step(x, tgt, mod, g1, g2, lbl, og, qg, kg, sk, shards, c_arr):
    win_s, wa_s, wb_s, wout_s, wmi_s, wmo_s = shards
    T = x.shape[0]
    tm, tr, tt = min(1024, T), min(256, T), min(2048, T)
    tk_t = min(1024, T)
    tn = 512
    sh1, sc1, gt1, sh2, sc2, gt2 = (mod[:, i * D:(i + 1) * D] for i in range(N_MOD))
    nI = T // tm
    blk = (tm, tn)
    vec_j = ((1, tn), lambda i, j, k: (0, j))

    h = _rms_mod_fwd("rms1_fwd", x, g1, sc1, sh1, tr)

    def epi_store(acc, ex, ou):
        ou[0][...] = acc.astype(ou[0].dtype)

    tm2 = min(2048, T)
    blk2 = (tm2, tn)

    full = lambda s: (0, s.shape[0])
    last = wmi_s.shape[0]
    (win_t,) = _run_job("gather_w_in", _gather_relay_job([win_s]))
    (proj,), (wa_t, wb_t, w_out, wmi_part) = _mm(
        "in_proj", "nt", [(h, D)], win_t, T, IN_W, D, tm2, tn, D, [], [((T, IN_W), F32, blk2, _ij)], epi_store,
        job=_gather_job([wa_s, wb_s, wout_s, wmi_s], rows=[full(wa_s), full(wb_s), full(wout_s), (0, MI_CUTS[0])]))
    (ya_pre, st), (wmi_part,) = _hgrn_fwd(
        proj, lbl, og, tt, job=_gather_job([wmi_s], rows=[MI_CUTS], into=[wmi_part]))
    (attn,), (wmi_t, wmo_part) = _swa_fwd(
        proj, qg, kg, sk, job=_gather_job([wmi_s, wmo_s], rows=[(MI_CUTS[1], last), (0, MO_CUT)], into=[wmi_part, None]))
    (ya, yb, merged), _ = _branch_merge(ya_pre, attn, wa_t, wb_t, proj, tm, tn)

    def epi_res1(acc, ex, ou):
        x_ref, gt_ref = ex
        ou[0][...] = acc.astype(BF16)
        ou[1][...] = x_ref[...] + gt_ref[...] * acc

    mo, x1 = _mm("out_proj", "nn", [(merged, D)], w_out, T, D, D, tm, tn, D, [(x, blk, _ij), (gt1, *vec_j)],
                 [((T, D), BF16, blk, _ij), ((T, D), F32, blk, _ij)], epi_res1)
    h2 = _rms_mod_fwd("rms2_fwd", x1, g2, sc2, sh2, tr)

    def epi_relu2(acc, ex, ou):
        r = jnp.maximum(acc, 0.0)
        ou[0][...] = r.astype(BF16)
        ou[1][...] = (r * r).astype(BF16)

    (r, a), (w_mo,) = _mm("mlp_in", "nt", [(h2, D)], wmi_t, T, HID, D, tm2, tn, D, [],
                          [((T, HID), BF16, blk2, _ij), ((T, HID), BF16, blk2, _ij)], epi_relu2,
                          job=_gather_job([wmo_s], rows=[(MO_CUT, last)], into=[wmo_part]))

    def loss_rows(acc, vecs, bufs, parts):
        gt = vecs[0][...]
        x1_buf, t_buf, dz_buf = bufs
        rows = min(64, tm)
        loss_sum, gate_sum = jnp.zeros((8, D), F32), jnp.zeros((8, D), F32)
        for r0 in range(0, tm, rows):
            rs = slice(r0, r0 + rows)
            z = acc[rs, :]
            e = x1_buf[rs, :] + gt * z - t_buf[rs, :]
            dy = e * (1.0 / D)
            t_buf[rs, :] = dy
            dz_buf[rs, :] = (gt * dy).astype(BF16)
            loss_sum = loss_sum + _fold8(e * e)
            gate_sum = gate_sum + _fold8(dy * z)
        parts[0][...] = loss_sum * (0.5 / D)
        parts[1][...] = gate_sum

    part_rows = ((nI * 8, D), F32, (8, D), lambda i, j, k: (i, 0))
    dy, dz, p_loss, p_gt2 = _rows_mm(
        "mlp_out", [(a, HID)], w_mo, T, tm, 1024, [gt2], [(F32, x1, False), (F32, tgt, True), (BF16, None, True)],
        [part_rows, part_rows], loss_rows)

    def epi_du(acc, ex, ou):
        ou[0][...] = (acc * (2.0 * ex[0][...].astype(F32))).astype(BF16)

    (du,) = _mm("mlp_out_dx", "nt", [(dz, D)], w_mo, T, HID, D, tm2, tn, D, [(r, blk2, _ij)],
                [((T, HID), BF16, blk2, _ij)], epi_du)
    gblk = (1024, 1024)
    gwide = (1024, D)
    pair_sum = lambda nm, g, r1: _pair_sum("pair_sum_" + nm, g, r1, c_arr, _sum_rows(r1.shape[1]))
    (g_mo,) = _mm("mlp_out_dw", "tn", [(a, HID)], dz, HID, D, T, 1024, D, tk_t, [], [((HID, D), BF16, gwide, _ij)], epi_store)
    (dh2,), (r1_mo,) = _mm("mlp_in_dx", "nn", [(du, HID)], wmi_t, T, D, HID, tm, D, 1024, [],
                           [((T, D), F32, (tm, D), _ij)], epi_store, job=_pair_job([g_mo]))
    dx1, p_sh2, p_sc2, p_g2, dmo, p_gt1 = _rms_mod_bwd("rms2_bwd", dh2, x1, g2, sc2, dy, tr, gate=gt1, mo=mo)
    s_mo = pair_sum("mlp_out", g_mo, r1_mo)
    near, far = (1, 2), (3,)
    (g_mi,), (rn_mo,) = _mm("mlp_in_dw", "tn", [(du, HID)], h2, HID, D, T, 1024, D, tk_t, [],
                            [((HID, D), BF16, gwide, _ij)], epi_store, job=_chip_job([s_mo], near))

    def epi_gates(acc, ex, ou):
        ya_ref, yb_ref, ga_ref, gb_ref = ex
        sa, sb = _sigmoid(ga_ref[...]), _sigmoid(gb_ref[...])
        ou[0][...] = (acc * sa).astype(BF16)
        ou[1][...] = (acc * sb).astype(BF16)
        ou[2][...] = (acc * ya_ref[...].astype(F32) * (sa * (1.0 - sa))).astype(BF16)
        ou[3][...] = (acc * yb_ref[...].astype(F32) * (sb * (1.0 - sb))).astype(BF16)

    o_bf = ((T, D), BF16, blk, _ij)
    (dya, dyb, dga, dgb), (rf_mo, r1_mi) = _mm(
        "out_proj_dx", "nt", [(dmo, D)], w_out, T, D, D, tm, tn, D,
        [(ya, blk, _ij), (yb, blk, _ij), (proj, blk, lambda i, j, k: (i, OFF_GTA // tn + j)),
         (proj, blk, lambda i, j, k: (i, OFF_GTB // tn + j))], [o_bf, o_bf, o_bf, o_bf], epi_gates,
        job=_both(_chip_job([s_mo], far), _pair_job([g_mi])))
    s_mi = pair_sum("mlp_in", g_mi, r1_mi)
    (g_out,) = _mm("out_proj_dw", "tn", [(merged, D)], dmo, D, D, T, 1024, 1024, tk_t, [], [((D, D), BF16, gblk, _ij)], epi_store)
    dya_pre, dattn = _twin_mm("branch_dx", "nn", [(dya, wa_t), (dyb, wb_t)], T, AW, D, tm, tn, D, F32)
    g_a, g_b = _twin_mm("branch_dw", "tn", [(dya, ya_pre), (dyb, attn)], D, AW, T, 1024, 1024, tk_t, BF16)
    (dqa, dfa, dia, dgg, p_lb, p_og), (rn_mi, r1_out, r1_a, r1_b) = _hgrn_bwd(
        proj, st, dya_pre, lbl, og, tt, job=_both(_chip_job([s_mi], near), _pair_job([g_out, g_a, g_b])))
    (dqb, dkv, p_qg, p_kg, p_sk), (rf_mi,) = _swa_bwd(proj, dattn, qg, kg, sk, job=_chip_job([s_mi], far))
    s_out, s_a, s_b = pair_sum("out", g_out, r1_out), pair_sum("branch_a", g_a, r1_a), pair_sum("branch_b", g_b, r1_b)
    pieces = [(dqa, AW), (dfa, AW), (dia, AW), (dgg, AW), (dqb, BW), (dkv, 2 * KVW), (dga, D), (dgb, D)]
    (g_in,), (r2_out, r2_a, r2_b) = _pieces_tn("in_proj_dw", pieces, h, 512, job=_chip_job([s_out, s_a, s_b]))
    (r1_in,) = _run_job("pair_w_in", _pair_job([g_in]))
    s_in = pair_sum("in", g_in, r1_in)
    (dx, p_sh1, p_sc1, p_g1), (r2_in,) = _pieces_nn_rms(
        "in_proj_dx", pieces, win_t, x, g1, sc1, dx1, tm, 512, job=_chip_job([s_in]))

    partials = dict(sh1=p_sh1, sc1=p_sc1, gt1=p_gt1, sh2=p_sh2, sc2=p_sc2, gt2=p_gt2, g1=p_g1, g2=p_g2,
                    lb=p_lb, og=p_og, qg=p_qg, kg=p_kg, sk=p_sk, loss=p_loss)
    sums = dict(w_in=(s_in, [r2_in]), w_branch_a=(s_a, [r2_a]), w_branch_b=(s_b, [r2_b]), w_out=(s_out, [r2_out]),
                w_mlp_in=(s_mi, [rn_mi, rf_mi]), w_mlp_out=(s_mo, [rn_mo, rf_mo]))
    return dx, sums, partials


def _exchange_slots(buf, send_sems, recv_sems):
    me = _mesh_pos()
    mine = buf.at[_index(me)]
    sends = []
    for k in range(1, N_DEV):
        cp = pltpu.make_async_remote_copy(src_ref=mine, dst_ref=mine, send_sem=send_sems.at[k - 1],
                                          recv_sem=recv_sems.at[k - 1], device_id=_flip(me, k), device_id_type=MESH)
        cp.start()
        sends.append(cp)
    for k in range(1, N_DEV):
        theirs = buf.at[_index(_flip(me, k))]
        pltpu.make_async_remote_copy(src_ref=theirs, dst_ref=theirs, send_sem=send_sems.at[k - 1],
                                     recv_sem=recv_sems.at[k - 1], device_id=_flip(me, k), device_id_type=MESH).wait_recv()
    for cp in sends:
        cp.wait_send()


ADA_W = N_MOD * D // N_DEV


def _ada_mod(c, w_ada, b_shard):
    def body(c_ref, w_ref, b_ref, mod_ref, sc_ref, cbuf, mbuf, s1, r1, s2, r2):
        me = _index(_mesh_pos())
        cbuf[me] = c_ref[...]
        _exchange_slots(cbuf, s1, r1)
        row = lax.broadcasted_iota(jnp.int32, (N_DEV, D), 0)
        call = jnp.zeros((N_DEV, D), F32)
        for d in range(N_DEV):
            call = jnp.where(row == d, cbuf[d], call)
        sc = call * _sigmoid(call)
        sc_ref[...] = sc
        mbuf[me] = _dot(sc, w_ref[...]) + b_ref[...]
        _exchange_slots(mbuf, s2, r2)
        for s in range(N_DEV):
            mod_ref[:, s * ADA_W:(s + 1) * ADA_W] = mbuf[s, pl.ds(me, 1), :]

    return pl.pallas_call(
        body, in_specs=[_VMEM, _VMEM, _VMEM], out_specs=[_VMEM, _VMEM],
        out_shape=[jax.ShapeDtypeStruct((1, N_MOD * D), F32), jax.ShapeDtypeStruct((N_DEV, D), F32)],
        scratch_shapes=[pltpu.VMEM((N_DEV, 1, D), F32), pltpu.VMEM((N_DEV, N_DEV, ADA_W), F32),
                        _SEMS(N_DEV - 1), _SEMS(N_DEV - 1), _SEMS(N_DEV - 1), _SEMS(N_DEV - 1)],
        name="ada_mod", compiler_params=pltpu.CompilerParams(vmem_limit_bytes=VMEM_LIMIT),
    )(c, w_ada, b_shard)


SMALL_SEGS = (("b_ada", N_MOD * D), ("norm1_gain", D), ("norm2_gain", D), ("lb0", AW), ("lb1", AW),
              ("hgrn_o_gain", AW), ("q_norm_gain", 128), ("k_norm_gain", 128), ("sinks", 128))
SMALL_W = sum(w for _, w in SMALL_SEGS)
X_SEGS = (("sh1", D), ("sc1", D), ("gt1", D), ("sh2", D), ("sc2", D), ("gt2", D), ("g1", D), ("g2", D),
          ("lb", AW), ("og", AW), ("qg", 128), ("kg", 128), ("sk", 128), ("loss", 128))
X_W = sum(w for _, w in X_SEGS)


def _offsets(segs):
    out, o = {}, 0
    for name, w in segs:
        out[name] = (o, w)
        o += w
    return out


def _small_reduce(parts, lb_logits):
    xo, so = _offsets(X_SEGS), _offsets(SMALL_SEGS)
    names = [nm for nm, _ in X_SEGS]

    def body(*refs):
        p_refs = dict(zip(names, refs[:len(names)]))
        lbl_ref, allx, gs_ref, loss_ref, send_sems, recv_sems = refs[len(names):]
        me = _index(_mesh_pos())
        for nm, (o, w) in xo.items():
            if nm == "loss":
                allx[me, :, o:o + w] = jnp.broadcast_to(jnp.sum(p_refs[nm][...]), (1, w))
            else:
                allx[me, :, o:o + w] = jnp.sum(p_refs[nm][...], axis=0, keepdims=True)
        _exchange_slots(allx, send_sems, recv_sems)
        tot = allx[0]
        for d in range(1, N_DEV):
            tot = tot + allx[d]
        seg = lambda nm: tot[:, xo[nm][0]:xo[nm][0] + xo[nm][1]]

        def put(nm, v):
            gs_ref[:, so[nm][0]:so[nm][0] + so[nm][1]] = v

        put("b_ada", tot[:, 0:N_MOD * D])
        put("norm1_gain", seg("g1"))
        put("norm2_gain", seg("g2"))
        lbl = lbl_ref[...]
        lb = _sigmoid(lbl[0:1, :] - lbl[1:2, :])
        dl0 = seg("lb") * lb * (1.0 - lb)
        put("lb0", dl0)
        put("lb1", -dl0)
        put("hgrn_o_gain", seg("og"))
        put("q_norm_gain", seg("qg"))
        put("k_norm_gain", seg("kg"))
        put("sinks", seg("sk"))
        loss_ref[...] = seg("loss")

    return pl.pallas_call(
        body, in_specs=[_VMEM] * (len(names) + 1), out_specs=[_VMEM, _VMEM, _VMEM],
        out_shape=[jax.ShapeDtypeStruct((N_DEV, 1, X_W), F32), jax.ShapeDtypeStruct((1, SMALL_W), F32),
                   jax.ShapeDtypeStruct((1, 128), F32)],
        scratch_shapes=[_SEMS(N_DEV - 1), _SEMS(N_DEV - 1)], name="small_reduce",
        compiler_params=pltpu.CompilerParams(vmem_limit_bytes=VMEM_LIMIT),
    )(*[parts[nm] for nm in names], lb_logits)


def _adamw_math(w, g, m, v):
    m = B1 * m + (1.0 - B1) * g
    v = B2 * v + (1.0 - B2) * (g * g)
    m_hat = m / (1.0 - B1 ** STEP)
    v_hat = v / (1.0 - B2 ** STEP)
    return -LR * (m_hat / (jnp.sqrt(v_hat) + ADAM_EPS) + WD * w), m, v


def _sum_rows(rs):
    return 256 if rs % 256 == 0 else rs // 2


def _pair_sum(name, g, recv, c_arr, tr):
    _, rs, cols = recv.shape
    blk = (1, tr, cols)

    def body(c_ref, g_ref, r_ref, o_ref):
        o_ref[...] = (g_ref[...].astype(F32) + r_ref[...].astype(F32)).astype(BF16)

    grid_spec = pltpu.PrefetchScalarGridSpec(
        num_scalar_prefetch=1, grid=(4, rs // tr),
        in_specs=[pl.BlockSpec(blk, lambda q, i, c: (2 * q + c[0], i, 0)), pl.BlockSpec(blk, lambda q, i, c: (q, i, 0))],
        out_specs=pl.BlockSpec(blk, lambda q, i, c: (q, i, 0)))
    return pl.pallas_call(body, grid_spec=grid_spec, out_shape=jax.ShapeDtypeStruct((4, rs, cols), BF16), name=name,
                          compiler_params=_params(("parallel", "parallel")))(c_arr, g.reshape(N_DEV, rs, cols), recv)


def _sum_adamw(name, sums, recvs, q_arr, w, m, v, transposed, tile):
    rows, cols = w.shape
    nR = len(recvs)

    def body(q_ref, s_ref, *refs):
        r_refs = refs[:nR]
        w_ref, m_ref, v_ref, g_ref, d_ref, nm_ref, nv_ref = refs[nR:]
        g = s_ref[0].astype(F32)
        for r_ref in r_refs:
            for slot in range(r_ref.shape[0]):
                g = g + r_ref[slot].astype(F32)
        g = g.T if transposed else g
        g_ref[...] = g
        d_ref[...], nm_ref[...], nv_ref[...] = _adamw_math(w_ref[...], g, m_ref[...], v_ref[...])

    if transposed:
        slab = lambda n, first: pl.BlockSpec((n, cols, tile), lambda i, q: (first(q), 0, i))
    else:
        slab = lambda n, first: pl.BlockSpec((n, tile, cols), lambda i, q: (first(q), i, 0))
    spec = pl.BlockSpec((tile, cols), lambda i, q: (i, 0))
    shape = jax.ShapeDtypeStruct((rows, cols), F32)
    grid_spec = pltpu.PrefetchScalarGridSpec(
        num_scalar_prefetch=1, grid=(rows // tile,),
        in_specs=[slab(1, lambda q: q[0])] + [slab(r.shape[0], lambda q: 0) for r in recvs] + [spec] * 3,
        out_specs=[spec] * 4)
    return pl.pallas_call(body, grid_spec=grid_spec, out_shape=[shape] * 4, name=name,
                          compiler_params=_params(("parallel",)))(q_arr, sums, *recvs, w, m, v)


def _adamw(name, w, g, m, v, tr):
    rows, cols = w.shape

    def body(w_ref, g_ref, m_ref, v_ref, d_ref, nm_ref, nv_ref):
        d_ref[...], nm_ref[...], nv_ref[...] = _adamw_math(w_ref[...], g_ref[...], m_ref[...], v_ref[...])

    spec = pl.BlockSpec((tr, cols), lambda i: (i, 0))
    shape = jax.ShapeDtypeStruct((rows, cols), F32)
    return pl.pallas_call(
        body, grid=(rows // tr,), in_specs=[spec] * 4, out_specs=[spec] * 3, out_shape=[shape] * 3, name=name,
        compiler_params=_params(("parallel",)),
    )(w, g, m, v)


def _ada_update(sc_t, dmod_cols, w, m, v, tr):
    rows, cols = w.shape

    def body(s_ref, d_ref, w_ref, m_ref, v_ref, g_ref, dl_ref, nm_ref, nv_ref):
        g = jnp.dot(s_ref[...], d_ref[...], precision=lax.Precision.HIGHEST, preferred_element_type=F32)
        g_ref[...] = g
        dl_ref[...], nm_ref[...], nv_ref[...] = _adamw_math(w_ref[...], g, m_ref[...], v_ref[...])

    spec = pl.BlockSpec((tr, cols), lambda i: (i, 0))
    shape = jax.ShapeDtypeStruct((rows, cols), F32)
    return pl.pallas_call(
        body, grid=(rows // tr,),
        in_specs=[pl.BlockSpec((tr, N_DEV), lambda i: (i, 0)), pl.BlockSpec((N_DEV, cols), lambda i: (0, 0)), spec, spec, spec],
        out_specs=[spec] * 4, out_shape=[shape] * 4, name="ada_update", compiler_params=_params(("parallel",)),
    )(sc_t, dmod_cols, w, m, v)


BIG = ("w_in", "w_branch_a", "w_branch_b", "w_out", "w_mlp_in", "w_mlp_out")
COLUMN_SHARDED = ("w_in", "w_branch_a", "w_branch_b", "w_mlp_in")
WEIGHTS = ("w_ada", "b_ada", "norm1_gain", "w_in", "lb_logits", "hgrn_o_gain", "q_norm_gain", "k_norm_gain", "sinks",
           "w_branch_a", "w_branch_b", "w_out", "norm2_gain", "w_mlp_in", "w_mlp_out")


def _to_bf16(name, w, transposed, tile=256):
    rows, cols = w.shape

    def body(w_ref, o_ref):
        v = w_ref[...]
        o_ref[...] = (v.T if transposed else v).astype(BF16)

    out_spec = pl.BlockSpec((cols, tile), lambda i: (0, i)) if transposed else pl.BlockSpec((tile, cols), lambda i: (i, 0))
    return pl.pallas_call(
        body, grid=(rows // tile,), in_specs=[pl.BlockSpec((tile, cols), lambda i: (i, 0))], out_specs=out_spec,
        out_shape=jax.ShapeDtypeStruct((cols, rows) if transposed else (rows, cols), BF16), name=name,
        compiler_params=_params(("parallel",)))(w)


def _pack_small(p):
    lb = p["lb_logits"]
    src = dict(p, lb0=lb[0:1], lb1=lb[1:2])
    return jnp.concatenate([jnp.pad(src[nm], ((0, 0), (0, w - src[nm].shape[1]))) for nm, w in SMALL_SEGS], axis=1)


def _unpack_small(vec, shapes):
    so = _offsets(SMALL_SEGS)
    out = {}
    for nm, shp in shapes.items():
        if nm == "lb_logits":
            o = so["lb0"][0]
            out[nm] = vec[0, o:o + 2 * AW].reshape(2, AW)
        else:
            o = so[nm][0]
            out[nm] = vec[:, o:o + shp[1]]
    return out


def kernel(x, c, w_ada, b_ada, norm1_gain, w_in, lb_logits, hgrn_o_gain, q_norm_gain, k_norm_gain, sinks, w_branch_a, w_branch_b, w_out, norm2_gain, w_mlp_in, w_mlp_out, loss_target, m_w_ada, m_b_ada, m_norm1_gain, m_w_in, m_lb_logits, m_hgrn_o_gain, m_q_norm_gain, m_k_norm_gain, m_sinks, m_w_branch_a, m_w_branch_b, m_w_out, m_norm2_gain, m_w_mlp_in, m_w_mlp_out, v_w_ada, v_b_ada, v_norm1_gain, v_w_in, v_lb_logits, v_hgrn_o_gain, v_q_norm_gain, v_k_norm_gain, v_sinks, v_w_branch_a, v_w_branch_b, v_w_out, v_norm2_gain, v_w_mlp_in, v_w_mlp_out):
    w = dict(w_ada=w_ada, b_ada=b_ada, norm1_gain=norm1_gain, w_in=w_in, lb_logits=lb_logits, hgrn_o_gain=hgrn_o_gain,
             q_norm_gain=q_norm_gain, k_norm_gain=k_norm_gain, sinks=sinks, w_branch_a=w_branch_a, w_branch_b=w_branch_b,
             w_out=w_out, norm2_gain=norm2_gain, w_mlp_in=w_mlp_in, w_mlp_out=w_mlp_out)
    m = dict(w_ada=m_w_ada, b_ada=m_b_ada, norm1_gain=m_norm1_gain, w_in=m_w_in, lb_logits=m_lb_logits,
             hgrn_o_gain=m_hgrn_o_gain, q_norm_gain=m_q_norm_gain, k_norm_gain=m_k_norm_gain, sinks=m_sinks,
             w_branch_a=m_w_branch_a, w_branch_b=m_w_branch_b, w_out=m_w_out, norm2_gain=m_norm2_gain,
             w_mlp_in=m_w_mlp_in, w_mlp_out=m_w_mlp_out)
    v = dict(w_ada=v_w_ada, b_ada=v_b_ada, norm1_gain=v_norm1_gain, w_in=v_w_in, lb_logits=v_lb_logits,
             hgrn_o_gain=v_hgrn_o_gain, q_norm_gain=v_q_norm_gain, k_norm_gain=v_k_norm_gain, sinks=v_sinks,
             w_branch_a=v_w_branch_a, w_branch_b=v_w_branch_b, w_out=v_w_out, norm2_gain=v_norm2_gain,
             w_mlp_in=v_w_mlp_in, w_mlp_out=v_w_mlp_out)
    for d in (w, m, v):
        for nm in ("w_ada",) + BIG:
            d[nm] = d[nm][0]
    px, py, pc = _mesh_pos()
    me = _index((px, py, pc))
    c_arr = jnp.reshape(pc, (1,)).astype(jnp.int32)
    q_arr = jnp.reshape(2 * px + py, (1,)).astype(jnp.int32)

    shards = [_to_bf16("shard_" + nm, w[nm], nm in COLUMN_SHARDED) for nm in BIG]
    b_shard = lax.dynamic_slice(b_ada, (0, me * ADA_W), (1, ADA_W))
    mod, sc_all = _ada_mod(c, w["w_ada"], b_shard)

    dx, sums, parts = _local_step(x[0], loss_target[0], mod, norm1_gain, norm2_gain, lb_logits, hgrn_o_gain,
                                  q_norm_gain, k_norm_gain, sinks, shards, c_arr)

    allx, g_small, loss = _small_reduce(parts, lb_logits)

    grad, delta, new_m, new_v = {}, {}, {}, {}
    for nm in BIG:
        s, r2 = sums[nm]
        grad[nm], delta[nm], new_m[nm], new_v[nm] = _sum_adamw(
            "adamw_" + nm, s, r2, q_arr, w[nm], m[nm], v[nm], nm in COLUMN_SHARDED, 128)

    dmod_cols = lax.dynamic_slice(allx[:, 0, :], (0, me * ADA_W), (N_DEV, ADA_W))
    grad["w_ada"], delta["w_ada"], new_m["w_ada"], new_v["w_ada"] = _ada_update(
        sc_all.T, dmod_cols, w["w_ada"], m["w_ada"], v["w_ada"], 256)

    small_names = [nm for nm in WEIGHTS if nm not in BIG and nm != "w_ada"]
    shapes = {nm: w[nm].shape for nm in small_names}
    ds, ms, vs = _adamw("adamw_small", _pack_small(w), g_small, _pack_small(m), _pack_small(v), 1)
    for dst, vec in ((grad, g_small), (delta, ds), (new_m, ms), (new_v, vs)):
        dst.update(_unpack_small(vec, shapes))

    def full(d, nm):
        return d[nm][None] if nm in BIG or nm == "w_ada" else d[nm]

    return (loss[0, 0], dx[None], *[full(grad, nm) for nm in WEIGHTS], *[full(delta, nm) for nm in WEIGHTS],
            *[full(new_m, nm) for nm in WEIGHTS], *[full(new_v, nm) for nm in WEIGHTS])
```

```python
import functools

import jax
import jax.numpy as jnp
from jax import lax
from jax.experimental import pallas as pl
from jax.experimental.pallas import tpu as pltpu

F32 = jnp.float32
BF16 = jnp.bfloat16
MESH = pl.DeviceIdType.MESH

N_DEV = 8
D = 2048
A_HEADS, A_HD, CHUNK = 8, 128, 64
AW = A_HEADS * A_HD
Q_HEADS, KV_HEADS, GROUP, B_HD, BLK = 16, 4, 4, 64, 128
BW = Q_HEADS * B_HD
KVW = KV_HEADS * B_HD
HID = 4 * D
IN_W = 4 * AW + BW + 2 * KVW + 2 * D
OFF_QA, OFF_FA, OFF_IA, OFF_GA = 0, AW, 2 * AW, 3 * AW
OFF_QB = 4 * AW
OFF_KB = OFF_QB + BW
OFF_VB = OFF_KB + KVW
OFF_GTA = OFF_VB + KVW
OFF_GTB = OFF_GTA + D
N_MOD = 6
EPS = 1e-6
LR, B1, B2, ADAM_EPS, WD, STEP = 1e-3, 0.9, 0.999, 1e-8, 0.01, 10
NEG = -1e30

VMEM_LIMIT = 56 * 1024 * 1024
MI_CUTS = (544, 864)
MO_CUT = 272

NN = (((1,), (0,)), ((), ()))
NT = (((1,), (1,)), ((), ()))
TN = (((0,), (0,)), ((), ()))
BNN = (((2,), (1,)), ((0,), (0,)))
BNT = (((2,), (2,)), ((0,), (0,)))
BTN = (((1,), (1,)), ((0,), (0,)))


def _dot(a, b, dims=NN):
    return lax.dot_general(a.astype(BF16), b.astype(BF16), dims, preferred_element_type=F32)


def _params(sem):
    return pltpu.CompilerParams(dimension_semantics=sem, vmem_limit_bytes=VMEM_LIMIT)


def _sigmoid(x):
    return 1.0 / (1.0 + jnp.exp(-x))


def _fold8(v):
    r, n = v.shape
    return jnp.sum(v.reshape(r // 8, 8, n), axis=0)


_VMEM = pl.BlockSpec(memory_space=pltpu.VMEM)
_ANY = pl.BlockSpec(memory_space=pl.ANY)
_SEMS = lambda n: pltpu.SemaphoreType.DMA((n,))


def _mesh_pos():
    return lax.axis_index("x"), lax.axis_index("y"), lax.axis_index("c")


def _flip(pos, k):
    return tuple(1 - p if (k >> s) & 1 else p for p, s in zip(pos, (2, 1, 0)))


def _index(pos):
    return 4 * pos[0] + 2 * pos[1] + pos[2]


class _Job:
    def __init__(self, ins, out_shape, sems, start, finish, aliases=None):
        self.ins, self.out_shape, self.sems, self.start, self.finish = list(ins), list(out_shape), list(sems), start, finish
        self.aliases = dict(aliases or {})


def _both(j1, j2):
    assert not j1.aliases and not j2.aliases
    n_in, n_out, n_sem = len(j1.ins), len(j1.out_shape), len(j1.sems)
    first = lambda ins, outs, sems: (ins[:n_in], outs[:n_out], sems[:n_sem])
    second = lambda ins, outs, sems: (ins[n_in:], outs[n_out:], sems[n_sem:])

    def start(*refs):
        j1.start(*first(*refs))
        j2.start(*second(*refs))

    def finish(*refs):
        j1.finish(*first(*refs))
        j2.finish(*second(*refs))

    return _Job(j1.ins + j2.ins, j1.out_shape + j2.out_shape, j1.sems + j2.sems, start, finish)


def _pcall(body, *, grid, in_specs, out_specs, out_shape, scratch_shapes, name, semantics, args, job=None):
    if job is None:
        outs = pl.pallas_call(body, grid=grid, in_specs=in_specs, out_specs=out_specs, out_shape=out_shape,
                              scratch_shapes=scratch_shapes, name=name, compiler_params=_params(semantics))(*args)
        return list(outs), []
    n_in, n_out, n_scr = len(in_specs), len(out_specs), len(scratch_shapes)
    j_in, j_out = len(job.ins), len(job.out_shape)
    steps = tuple(grid)

    def carrier(*refs):
        o = 0
        main_in, o = refs[o:o + n_in], o + n_in
        job_in, o = refs[o:o + j_in], o + j_in
        main_out, o = refs[o:o + n_out], o + n_out
        job_out, o = refs[o:o + j_out], o + j_out
        main_scr, job_sems = refs[o:o + n_scr], refs[o + n_scr:]
        ids = [pl.program_id(a) for a in range(len(steps))]
        first = functools.reduce(lambda p, q: p & q, [i == 0 for i in ids])
        last = functools.reduce(lambda p, q: p & q, [i == s - 1 for i, s in zip(ids, steps)])

        @pl.when(first)
        def _():
            job.start(job_in, job_out, job_sems)

        body(*main_in, *main_out, *main_scr)

        @pl.when(last)
        def _():
            job.finish(job_in, job_out, job_sems)

    outs = pl.pallas_call(
        carrier, grid=grid, in_specs=list(in_specs) + [_ANY] * j_in, out_specs=list(out_specs) + [_ANY] * j_out,
        out_shape=list(out_shape) + job.out_shape, scratch_shapes=list(scratch_shapes) + job.sems, name=name,
        input_output_aliases={n_in + i: n_out + o for i, o in job.aliases.items()},
        compiler_params=_params(("arbitrary",) * len(steps)),
    )(*args, *job.ins)
    return list(outs[:n_out]), list(outs[n_out:])


def _run_job(name, job):
    j_in, j_out = len(job.ins), len(job.out_shape)

    def body(*refs):
        ins, outs, sems = refs[:j_in], refs[j_in:j_in + j_out], refs[j_in + j_out:]
        job.start(ins, outs, sems)
        job.finish(ins, outs, sems)

    return list(pl.pallas_call(body, in_specs=[_ANY] * j_in, out_specs=[_ANY] * j_out, out_shape=job.out_shape,
                               scratch_shapes=job.sems, name=name,
                               input_output_aliases=job.aliases)(*job.ins))


def _gather_job(shards, rows=None, into=None):
    n = len(shards)
    rows = rows or [(0, s.shape[0]) for s in shards]
    into = into or [None] * n
    olds, aliases = [], {}
    for a, buf in enumerate(into):
        if buf is not None:
            aliases[n + len(olds)] = a
            olds.append(buf)

    def copies(ins, outs, sems):
        send_sems, recv_sems, local_sems = sems
        x, y, c = _mesh_pos()
        me, sib = (x, y, c), (x, y, 1 - c)
        chips = [(1 - x, y), (x, 1 - y), (1 - x, 1 - y)]

        def part(a, p):
            rs, (r0, r1) = shards[a].shape[0], rows[a]
            return outs[a].at[pl.ds(_index(p) * rs + r0, r1 - r0), :]

        own = lambda a: ins[a].at[pl.ds(rows[a][0], rows[a][1] - rows[a][0]), :]

        def copy(a, k, block, to, src=None):
            return pltpu.make_async_remote_copy(
                src_ref=part(a, block) if src is None else src, dst_ref=part(a, block),
                send_sem=send_sems.at[7 * a + k], recv_sem=recv_sems.at[7 * a + k], device_id=to, device_id_type=MESH)

        mine = [pltpu.make_async_copy(own(a), part(a, me), local_sems.at[a]) for a in range(n)]
        first = []
        for a in range(n):
            first.append(copy(a, 0, me, sib, src=own(a)))
            first += [copy(a, 1 + j, me, (*chip, c), src=own(a)) for j, chip in enumerate(chips)]
        return me, sib, c, chips, copy, mine, first

    def start(ins, outs, sems):
        *_, mine, first = copies(ins, outs, sems)
        for cp in mine + first:
            cp.start()

    def finish(ins, outs, sems):
        me, sib, c, chips, copy, mine, first = copies(ins, outs, sems)
        passed = []
        for j, chip in enumerate(chips):
            for a in range(n):
                copy(a, 1 + j, (*chip, c), me).wait_recv()
                cp = copy(a, 4 + j, (*chip, c), sib)
                cp.start()
                passed.append(cp)
        for a in range(n):
            copy(a, 0, sib, me).wait_recv()
            for j, chip in enumerate(chips):
                copy(a, 4 + j, (*chip, 1 - c), me).wait_recv()
        for cp in first + passed:
            cp.wait_send()
        for cp in mine:
            cp.wait()

    return _Job(list(shards) + olds, [jax.ShapeDtypeStruct((N_DEV * s.shape[0], s.shape[1]), s.dtype) for s in shards],
                [_SEMS(7 * n), _SEMS(7 * n), _SEMS(n)], start, finish, aliases)


def _gather_relay_job(shards):
    n = len(shards)

    def tools(ins, outs, sems):
        send_sems, recv_sems, local_sems = sems
        x, y, c = _mesh_pos()
        q = 2 * x + y
        chip_at = lambda rel: (1 - x if rel & 2 else x, 1 - y if rel & 1 else y)

        def rows(a, chip, core):
            rs = shards[a].shape[0]
            return outs[a].at[pl.ds((2 * chip + core) * rs, rs), :]

        def copy(a, slot, chip, core, to, src=None):
            blk = rows(a, chip, core)
            return pltpu.make_async_remote_copy(src_ref=blk if src is None else src, dst_ref=blk,
                                                send_sem=send_sems.at[7 * a + slot], recv_sem=recv_sems.at[7 * a + slot],
                                                device_id=to, device_id_type=MESH)

        mine = [pltpu.make_async_copy(ins[a], rows(a, q, c), local_sems.at[a]) for a in range(n)]
        first = [copy(a, slot, q, c, (x, y, 1 - c) if slot == 0 else (*chip_at(slot), c), src=ins[a])
                 for a in range(n) for slot in (0, 1, 2)]
        return x, y, c, q, chip_at, copy, mine, first

    def start(ins, outs, sems):
        *_, mine, first = tools(ins, outs, sems)
        for cp in mine + first:
            cp.start()

    def finish(ins, outs, sems):
        x, y, c, q, chip_at, copy, mine, first = tools(ins, outs, sems)
        me, sib = (x, y, c), (x, y, 1 - c)

        def relay(src, dst):
            for a in range(n):
                copy(a, src, q ^ src, c, me).wait_recv()
                copy(a, 3, q ^ src, c, (*chip_at(dst), c)).start()
                copy(a, 3 + src, q ^ src, c, sib).start()
            for a in range(n):
                copy(a, dst, q ^ dst, c, me).wait_recv()
                copy(a, 3 + dst, q ^ dst, c, sib).start()

        pl.when(c == 1)(lambda: relay(1, 2))
        pl.when(c == 0)(lambda: relay(2, 1))
        for a in range(n):
            copy(a, 3, q ^ 3, c, me).wait_recv()
            copy(a, 6, q ^ 3, c, sib).start()
        for a in range(n):
            copy(a, 0, q, 1 - c, me).wait_recv()
            for rel in (1, 2, 3):
                copy(a, 3 + rel, q ^ rel, 1 - c, me).wait_recv()
        for a in range(n):
            for slot in range(3, 7):
                copy(a, slot, q, c, sib).wait_send()
        for cp in first:
            cp.wait_send()
        for cp in mine:
            cp.wait()

    return _Job(shards, [jax.ShapeDtypeStruct((N_DEV * s.shape[0], s.shape[1]), s.dtype) for s in shards],
                [_SEMS(7 * n), _SEMS(7 * n), _SEMS(n)], start, finish)


def _pair_job(grads):
    n = len(grads)

    def copies(ins, outs, sems):
        send_sems, recv_sems = sems
        x, y, c = _mesh_pos()
        out = []
        for a in range(n):
            rs = grads[a].shape[0] // N_DEV
            for q in range(4):
                blk = ins[a].at[pl.ds((2 * q + 1 - c) * rs, rs), :]
                out.append(pltpu.make_async_remote_copy(
                    src_ref=blk, dst_ref=outs[a].at[q], send_sem=send_sems.at[4 * a + q], recv_sem=recv_sems.at[4 * a + q],
                    device_id=(x, y, 1 - c), device_id_type=MESH))
        return out

    def start(ins, outs, sems):
        for cp in copies(ins, outs, sems):
            cp.start()

    def finish(ins, outs, sems):
        for cp in copies(ins, outs, sems):
            cp.wait()

    return _Job(grads, [jax.ShapeDtypeStruct((4, g.shape[0] // N_DEV, g.shape[1]), g.dtype) for g in grads],
                [_SEMS(4 * n), _SEMS(4 * n)], start, finish)


def _chip_job(sums, rels=(1, 2, 3)):
    n, nr = len(sums), len(rels)

    def copies(ins, outs, sems):
        send_sems, recv_sems = sems
        x, y, c = _mesh_pos()
        out = []
        for a in range(n):
            for slot, r in enumerate(rels):
                px, py = (1 - x if r & 2 else x), (1 - y if r & 1 else y)
                out.append(pltpu.make_async_remote_copy(
                    src_ref=ins[a].at[2 * px + py], dst_ref=outs[a].at[slot], send_sem=send_sems.at[nr * a + slot],
                    recv_sem=recv_sems.at[nr * a + slot], device_id=(px, py, c), device_id_type=MESH))
        return out

    def start(ins, outs, sems):
        for cp in copies(ins, outs, sems):
            cp.start()

    def finish(ins, outs, sems):
        for cp in copies(ins, outs, sems):
            cp.wait()

    return _Job(sums, [jax.ShapeDtypeStruct((nr,) + s.shape[1:], s.dtype) for s in sums],
                [_SEMS(nr * n), _SEMS(nr * n)], start, finish)


def _mm(name, form, a_list, b, M, N, K, tm, tn, tk, extras, outs, epi, job=None):
    nI, nJ, nK = M // tm, N // tn, K // tk
    assert nI * tm == M and nJ * tn == N and nK * tk == K
    dims = {"nn": NN, "nt": NT, "tn": TN}[form]
    b_list = b if isinstance(b, list) else [(b, {"nn": N, "nt": K, "tn": N}[form])]
    nA, nB = len(a_list), len(b_list)
    assert nA == 1 or nB == 1
    assert nB == 1 or form in ("nn", "nt")
    AXIS = {"i": 0, "j": 1, "k": 2}
    a_axis, a_tile = ("i", tm) if form == "tn" else ("k", tk)
    b_axis, b_tile = ("k", tk) if form == "nt" else ("j", tn)

    def cut(pieces, tile, total):
        starts, s = [], 0
        for _, w in pieces:
            assert w % tile == 0
            starts.append(s // tile)
            s += w
        assert s == total
        return starts, [w // tile for _, w in pieces]

    a_st, a_cn = cut(a_list, a_tile, M if form == "tn" else K)
    b_st, b_cn = cut(b_list, b_tile, K if form == "nt" else N)

    def inside(idx, st, cn):
        return (idx >= st) & (idx < st + cn)

    def a_spec(p):
        st, cn = a_st[p], a_cn[p]
        if form == "tn":
            return pl.BlockSpec((tk, tm), lambda i, j, k: (jnp.where(inside(i, st, cn), k, 0), jnp.clip(i - st, 0, cn - 1)))
        return pl.BlockSpec((tm, tk), lambda i, j, k: (i, jnp.clip(k - st, 0, cn - 1)))

    def b_spec(p):
        st, cn = b_st[p], b_cn[p]
        if form == "nt":
            return pl.BlockSpec((tn, tk), lambda i, j, k: (j, jnp.clip(k - st, 0, cn - 1)))
        if nB == 1:
            return pl.BlockSpec((tk, tn), lambda i, j, k: (k, j))
        return pl.BlockSpec((tk, tn), lambda i, j, k: (jnp.where(inside(j, st, cn), k, 0), jnp.clip(j - st, 0, cn - 1)))

    in_specs = ([a_spec(p) for p in range(nA)] + [b_spec(p) for p in range(nB)]
                + [pl.BlockSpec(bs, im) for _, bs, im in extras])
    out_shape = [jax.ShapeDtypeStruct(s_, d_) for s_, d_, _, _ in outs]
    out_specs = [pl.BlockSpec(bs, im) for _, _, bs, im in outs]
    nE, nO = len(extras), len(outs)
    single = nA == 1 and nB == 1

    def body(*refs):
        a_refs, b_refs = refs[:nA], refs[nA:nA + nB]
        ex, ou = refs[nA + nB:nA + nB + nE], refs[nA + nB + nE:nA + nB + nE + nO]
        ids = [pl.program_id(a) for a in range(3)]

        def partial_of(p, q):
            return lax.dot_general(a_refs[p][...], b_refs[q][...], dims, preferred_element_type=F32)

        if nK == 1 and single:
            epi(partial_of(0, 0), ex, ou)
            return
        acc = refs[-1]
        k = ids[2]
        for p in range(nA):
            for q in range(nB):
                def first(p=p, q=q):
                    acc[...] = partial_of(p, q)

                def later(p=p, q=q):
                    acc[...] += partial_of(p, q)

                here = None
                if nA > 1:
                    here = inside(ids[AXIS[a_axis]], a_st[p], a_cn[p])
                if nB > 1:
                    here = inside(ids[AXIS[b_axis]], b_st[q], b_cn[q])
                pl.when(k == 0 if here is None else here & (k == 0))(first)
                pl.when(k > 0 if here is None else here & (k > 0))(later)

        @pl.when(k == nK - 1)
        def _():
            epi(acc[...], ex, ou)

    scratch = [] if (nK == 1 and single) else [pltpu.VMEM((tm, tn), F32)]
    res, job_res = _pcall(
        body, grid=(nI, nJ, nK), in_specs=in_specs, out_specs=out_specs, out_shape=out_shape, scratch_shapes=scratch,
        name=name, semantics=("parallel", "parallel", "arbitrary"),
        args=[a for a, _ in a_list] + [p for p, _ in b_list] + [e for e, _, _ in extras], job=job)
    return res if job is None else (res, job_res)


def _twin_mm(name, form, pairs, M, N, K, tm, tn, tk, out_dtype):
    nI, nJ, nK = M // tm, N // tn, K // tk
    dims = {"nn": NN, "tn": TN}[form]
    a_spec = (pl.BlockSpec((tm, tk), lambda i, j, k: (i, k)) if form == "nn" else pl.BlockSpec((tk, tm), lambda i, j, k: (k, i)))
    b_spec = pl.BlockSpec((tk, tn), lambda i, j, k: (k, j))
    o_spec = pl.BlockSpec((tm, tn), lambda i, j, k: (i, j))

    def body(a1, b1, a2, b2, o1, o2, *accs):
        k = pl.program_id(2)
        for a_ref, b_ref, o_ref, acc in ((a1, b1, o1, accs[0] if accs else None), (a2, b2, o2, accs[1] if accs else None)):
            part = lax.dot_general(a_ref[...], b_ref[...], dims, preferred_element_type=F32)
            if nK == 1:
                o_ref[...] = part.astype(out_dtype)
                continue

            @pl.when(k == 0)
            def _(acc=acc, part=part):
                acc[...] = part

            @pl.when(k > 0)
            def _(acc=acc, part=part):
                acc[...] += part

            @pl.when(k == nK - 1)
            def _(acc=acc, o_ref=o_ref):
                o_ref[...] = acc[...].astype(out_dtype)

    (a1, b1), (a2, b2) = pairs
    shape = jax.ShapeDtypeStruct((M, N), out_dtype)
    return pl.pallas_call(
        body, grid=(nI, nJ, nK), in_specs=[a_spec, b_spec, a_spec, b_spec], out_specs=[o_spec, o_spec],
        out_shape=[shape, shape], scratch_shapes=[] if nK == 1 else [pltpu.VMEM((tm, tn), F32)] * 2, name=name,
        compiler_params=_params(("parallel", "parallel", "arbitrary")))(a1, b1, a2, b2)


def _piece_tiles(pieces, tile):
    starts, s = [], 0
    for _, w in pieces:
        assert w % tile == 0
        starts.append(s // tile)
        s += w
    return starts, [w // tile for _, w in pieces], s


def _pieces_tn(name, pieces, b, tile, job=None):
    T, N = b.shape
    st, cn, M = _piece_tiles(pieces, tile)
    nP, nI = len(pieces), M // tile

    def body(*refs):
        p_refs, b_hbm, o_ref = refs[:nP], refs[nP], refs[nP + 1]
        bbuf, abuf, bsem, asem = refs[nP + 2:]
        i = pl.program_id(0)

        def fetch(step, slot):
            for p in range(nP):
                @pl.when((step >= st[p]) & (step < st[p] + cn[p]))
                def _():
                    col = pl.multiple_of((step - st[p]) * tile, tile)
                    pltpu.make_async_copy(p_refs[p].at[pl.ds(0, T), pl.ds(col, tile)], abuf.at[slot], asem.at[slot]).start()

        @pl.when(i == 0)
        def _():
            whole = pltpu.make_async_copy(b_hbm, bbuf, bsem)
            whole.start()
            fetch(0, 0)
            whole.wait()

        @pl.when(i + 1 < nI)
        def _():
            fetch(i + 1, (i + 1) % 2)

        pltpu.make_async_copy(p_refs[0].at[pl.ds(0, T), pl.ds(0, tile)], abuf.at[i % 2], asem.at[i % 2]).wait()
        o_ref[...] = lax.dot_general(abuf[i % 2], bbuf[...], TN, preferred_element_type=F32).astype(BF16)

    res, job_res = _pcall(
        body, grid=(nI,), in_specs=[_ANY] * (nP + 1), out_specs=[pl.BlockSpec((tile, N), lambda i: (i, 0))],
        out_shape=[jax.ShapeDtypeStruct((M, N), BF16)],
        scratch_shapes=[pltpu.VMEM((T, N), b.dtype), pltpu.VMEM((2, T, tile), b.dtype), pltpu.SemaphoreType.DMA, _SEMS(2)],
        name=name, semantics=("arbitrary",), args=[p for p, _ in pieces] + [b], job=job)
    return res if job is None else (res, job_res)


def _rows_mm(name, pieces, w, T, tm, tk, vecs, bufs, parts, epi, job=None):
    st, cn, K = _piece_tiles(pieces, tk)
    nP, nI, nK = len(pieces), T // tm, K // tk
    part_specs = [pl.BlockSpec(bs, lambda i, k, im=im: im(i, 0, k)) for _, _, bs, im in parts]
    n_vec, nB = len(vecs), len(bufs)
    load_ix = [n for n, (_, src, _) in enumerate(bufs) if src is not None]
    store_ix = [n for n, (_, _, store) in enumerate(bufs) if store]
    n_any_in, n_any_out = len(load_ix), len(store_ix)

    def body(*refs):
        o = nP
        p_refs, w_ref = refs[:nP], refs[o]
        vec_refs = refs[o + 1:o + 1 + n_vec]
        ins = refs[o + 1 + n_vec:o + 1 + n_vec + n_any_in]
        o = o + 1 + n_vec + n_any_in
        hbm_outs, p_outs = refs[o:o + n_any_out], refs[o + n_any_out:o + n_any_out + len(parts)]
        o = o + n_any_out + len(parts)
        acc, abuf = refs[o:o + 2]
        buf_refs = refs[o + 2:o + 2 + nB]
        asem, in_sems, out_sems = refs[-3:]
        i, k = pl.program_id(0), pl.program_id(1)
        g = i * nK + k
        rows_of = lambda ref, ii: ref.at[pl.ds(pl.multiple_of(ii * tm, tm), tm), :]
        bufs_in = [buf_refs[n] for n in load_ix]
        bufs_out = [buf_refs[n] for n in store_ix]

        def fetch(ii, kk, slot):
            for p in range(nP):
                @pl.when((kk >= st[p]) & (kk < st[p] + cn[p]))
                def _():
                    col = pl.multiple_of((kk - st[p]) * tk, tk)
                    src = p_refs[p].at[pl.ds(pl.multiple_of(ii * tm, tm), tm), pl.ds(col, tk)]
                    pltpu.make_async_copy(src, abuf.at[slot], asem.at[slot]).start()

        loads = lambda ii: [pltpu.make_async_copy(rows_of(src, ii), buf, in_sems.at[n])
                            for n, (src, buf) in enumerate(zip(ins, bufs_in))]
        stores = lambda ii: [pltpu.make_async_copy(buf, rows_of(dst, ii), out_sems.at[n])
                             for n, (buf, dst) in enumerate(zip(bufs_out, hbm_outs))]

        @pl.when(g == 0)
        def _():
            fetch(0, 0, 0)

        @pl.when(g + 1 < nI * nK)
        def _():
            last_k = k == nK - 1
            fetch(jnp.where(last_k, i + 1, i), jnp.where(last_k, 0, k + 1), (g + 1) % 2)

        @pl.when(k == 0)
        def _():
            @pl.when(i > 0)
            def _():
                for cp in stores(i - 1):
                    cp.wait()
            for cp in loads(i):
                cp.start()

        pltpu.make_async_copy(p_refs[0].at[pl.ds(0, tm), pl.ds(0, tk)], abuf.at[g % 2], asem.at[g % 2]).wait()

        def product(cols):
            return jnp.dot(abuf[g % 2], w_ref[:, cols], preferred_element_type=F32)

        col_blocks = [slice(c0, c0 + 512) for c0 in range(0, D, 512)]

        @pl.when(k == 0)
        def _():
            for cols in col_blocks:
                acc[:, cols] = product(cols)

        @pl.when(k > 0)
        def _():
            for cols in col_blocks:
                acc[:, cols] += product(cols)

        @pl.when(k == nK - 1)
        def _():
            for cp in loads(i):
                cp.wait()
            epi(acc, vec_refs, buf_refs, p_outs)
            for cp in stores(i):
                cp.start()

            @pl.when(i == nI - 1)
            def _():
                for cp in stores(i):
                    cp.wait()

    vec = pl.BlockSpec((1, D), lambda i, k: (0, 0))
    scratch = ([pltpu.VMEM((tm, D), F32), pltpu.VMEM((2, tm, tk), BF16)] + [pltpu.VMEM((tm, D), dt) for dt, _, _ in bufs]
               + [_SEMS(2), _SEMS(n_any_in), _SEMS(n_any_out)])
    res, job_res = _pcall(
        body, grid=(nI, nK),
        in_specs=[_ANY] * nP + [pl.BlockSpec((tk, D), lambda i, k: (k, 0))] + [vec] * n_vec + [_ANY] * n_any_in,
        out_specs=[_ANY] * n_any_out + part_specs,
        out_shape=([jax.ShapeDtypeStruct((T, D), bufs[n][0]) for n in store_ix]
                   + [jax.ShapeDtypeStruct(s, d) for s, d, _, _ in parts]),
        scratch_shapes=scratch, name=name, semantics=("arbitrary", "arbitrary"),
        args=[p for p, _ in pieces] + [w] + list(vecs) + [bufs[n][1] for n in load_ix], job=job)
    return res if job is None else (res, job_res)


def _pieces_nn_rms(name, pieces, w, x, gain, sc, dres, tm, tk, job=None):
    _, outs, epi = _rms_mod_bwd_epilogue(x, gain, sc, dres, tm)

    def on_rows(acc, vecs, bufs, parts):
        epi(acc, [bufs[0], vecs[0], vecs[1], bufs[1]], [bufs[1], *parts])

    return _rows_mm(name, pieces, w, x.shape[0], tm, tk, [gain, sc], [(F32, x, False), (F32, dres, True)],
                    outs[1:], on_rows, job=job)


def _rms_mod_fwd(name, x, gain, sc, sh, tr):
    T = x.shape[0]

    def body(x_ref, g_ref, sc_ref, sh_ref, h_ref):
        xv = x_ref[...]
        rstd = lax.rsqrt(jnp.mean(xv * xv, axis=-1, keepdims=True) + EPS)
        h_ref[...] = ((xv * rstd * g_ref[...]) * (1.0 + sc_ref[...]) + sh_ref[...]).astype(BF16)

    row = pl.BlockSpec((tr, D), lambda i: (i, 0))
    vec = pl.BlockSpec((1, D), lambda i: (0, 0))
    return pl.pallas_call(
        body, grid=(T // tr,), in_specs=[row, vec, vec, vec], out_specs=row,
        out_shape=jax.ShapeDtypeStruct((T, D), BF16), name=name, compiler_params=_params(("parallel",)),
    )(x, gain, sc, sh)


def _rms_mod_bwd_epilogue(x, gain, sc, dres, tm, gate=None, mo=None):
    T = x.shape[0]
    with_gate = gate is not None
    row = ((tm, D), lambda i, j, k: (i, 0))
    vec = ((1, D), lambda i, j, k: (0, 0))
    part = ((T // tm * 8, D), F32, (8, D), lambda i, j, k: (i, 0))
    extras = [(x, *row), (gain, *vec), (sc, *vec), (dres, *row)]
    outs = [((T, D), F32, *row), part, part, part]
    if with_gate:
        extras += [(gate, *vec), (mo, *row)]
        outs += [((T, D), BF16, *row), part]

    rows = min(64, tm)

    def epi(acc, ex, ou):
        g = ex[1][...]
        sums = [jnp.zeros((8, D), F32) for _ in range(4)]
        for r0 in range(0, tm, rows):
            rs = slice(r0, r0 + rows)
            dhv, xv = acc[rs, :], ex[0][rs, :]
            rstd = lax.rsqrt(jnp.mean(xv * xv, axis=-1, keepdims=True) + EPS)
            xhat = xv * rstd
            dn = dhv * (1.0 + ex[2][...])
            dxhat = dn * g
            dx = ex[3][rs, :] + rstd * (dxhat - xhat * jnp.mean(dxhat * xhat, axis=-1, keepdims=True))
            ou[0][rs, :] = dx
            terms = [dhv, dhv * (xhat * g), dn * xhat]
            if with_gate:
                terms.append(dx * ex[5][rs, :].astype(F32))
                ou[4][rs, :] = (ex[4][...] * dx).astype(BF16)
            sums = [s + _fold8(t) for s, t in zip(sums, terms)] + sums[len(terms):]
        ou[1][...], ou[2][...], ou[3][...] = sums[:3]
        if with_gate:
            ou[5][...] = sums[3]

    return extras, outs, epi


def _rms_mod_bwd(name, dh, x, gain, sc, dres, tr, gate=None, mo=None):
    T = x.shape[0]
    extras, outs, epi = _rms_mod_bwd_epilogue(x, gain, sc, dres, tr, gate, mo)
    rows_only = lambda im: (lambda i: im(i, 0, 0))
    nE = len(extras)

    def body(dh_ref, *refs):
        epi(dh_ref, refs[:nE], refs[nE:])

    return pl.pallas_call(
        body, grid=(T // tr,),
        in_specs=[pl.BlockSpec((tr, D), lambda i: (i, 0))] + [pl.BlockSpec(bs, rows_only(im)) for _, bs, im in extras],
        out_specs=[pl.BlockSpec(bs, rows_only(im)) for _, _, bs, im in outs],
        out_shape=[jax.ShapeDtypeStruct(s, d) for s, d, _, _ in outs], name=name, compiler_params=_params(("parallel",)),
    )(dh, *[e for e, _, _ in extras])


def _split3(v):
    h = v.astype(BF16)
    r1 = v - h.astype(F32)
    m = r1.astype(BF16)
    lo = (r1 - m.astype(F32)).astype(BF16)
    return h, m, lo


def _tri_mm(tri, v, dims=NN):
    h, m, lo = _split3(v)
    t = tri.astype(BF16)
    mm = lambda p: lax.dot_general(t, p, dims, preferred_element_type=F32)
    return (mm(lo) + mm(m)) + mm(h)


def _hgrn_chunk_terms(q, fl, lb):
    sig = _sigmoid(fl)
    f = lb + (1.0 - lb) * sig
    lf = jnp.log(f)
    kk = 1.0 - f
    sq = _sigmoid(q)
    qf = q * sq
    return sig, f, lf, kk, sq, qf


def _causal(n):
    r = lax.broadcasted_iota(jnp.int32, (n, n), 0)
    c = lax.broadcasted_iota(jnp.int32, (n, n), 1)
    return r >= c


def _hgrn_fwd(proj, lb_logits, o_gain, tt, job=None):
    T = proj.shape[0]
    nT, ncl = T // tt, tt // CHUNK
    C = CHUNK

    def body(q_ref, f_ref, i_ref, g_ref, lbl_ref, og_ref, y_ref, st_ref, S):
        @pl.when(pl.program_id(1) == 0)
        def _():
            S[...] = jnp.zeros_like(S)

        lbl = lbl_ref[...]
        lb = _sigmoid(lbl[0:1, :] - lbl[1:2, :])
        og = og_ref[...]
        shp = (ncl, C, A_HD)
        q, fl, v, g = (r[...].reshape(shp) for r in (q_ref, f_ref, i_ref, g_ref))
        tri = jnp.broadcast_to(_causal(C), (ncl, C, C))
        _, _, lf, kk, _, qf = _hgrn_chunk_terms(q, fl, lb)
        b = _tri_mm(tri, lf, BNN)
        bm, bl = b[:, C // 2 - 1:C // 2, :], b[:, C - 1:C, :]
        qd, kd = qf * jnp.exp(b - bm), kk * jnp.exp(bm - b)
        A = jnp.where(tri, _dot(qd, kd, BNT), 0.0)
        d_st = _dot(v, kk * jnp.exp(bl - b), BTN)
        dec = jnp.exp(bl)
        st = S[...]
        for ci in range(ncl):
            st_ref[0, ci] = st
            st = st * dec[ci] + d_st[ci]
        S[...] = st
        o = _dot(A, v, BNN) + _dot(qf * jnp.exp(b), st_ref[0], BNT)
        r = lax.rsqrt(jnp.mean(o * o, axis=-1, keepdims=True) + EPS)
        y_ref[...] = (o * r * og * (g * _sigmoid(g))).astype(BF16).reshape(tt, A_HD)

    def col(off):
        return pl.BlockSpec((tt, A_HD), lambda h, t: (t, off // A_HD + h))

    head_vec = lambda rows: pl.BlockSpec((rows, A_HD), lambda h, t: (0, h))
    return _pcall(
        body, grid=(A_HEADS, nT),
        in_specs=[col(OFF_QA), col(OFF_FA), col(OFF_IA), col(OFF_GA), head_vec(2), head_vec(1)],
        out_specs=[pl.BlockSpec((tt, A_HD), lambda h, t: (t, h)),
                   pl.BlockSpec((1, ncl, A_HD, A_HD), lambda h, t: (h, t, 0, 0))],
        out_shape=[jax.ShapeDtypeStruct((T, AW), BF16),
                   jax.ShapeDtypeStruct((A_HEADS, T // C, A_HD, A_HD), F32)],
        scratch_shapes=[pltpu.VMEM((A_HD, A_HD), F32)], name="hgrn_fwd", semantics=("parallel", "arbitrary"),
        args=[proj, proj, proj, proj, lb_logits, o_gain], job=job)


def _hgrn_bwd(proj, st, dy, lb_logits, o_gain, tt, job=None):
    T = proj.shape[0]
    nT, ncl = T // tt, tt // CHUNK
    C = CHUNK

    def body(q_ref, f_ref, i_ref, g_ref, st_ref, dy_ref, lbl_ref, og_ref,
             dq_ref, df_ref, di_ref, dg_ref, plb_ref, pog_ref, dS):
        @pl.when(pl.program_id(1) == 0)
        def _():
            dS[...] = jnp.zeros_like(dS)

        lbl = lbl_ref[...]
        lb = _sigmoid(lbl[0:1, :] - lbl[1:2, :])
        og = og_ref[...]
        shp = (ncl, C, A_HD)
        flat = lambda t: t.reshape(tt, A_HD)
        q, fl, v, g, dout = (r[...].reshape(shp) for r in (q_ref, f_ref, i_ref, g_ref, dy_ref))
        tri = jnp.broadcast_to(_causal(C), (ncl, C, C))
        rowi = lax.broadcasted_iota(jnp.int32, shp, 1)
        st0 = st_ref[0]
        sig, f, lf, kk, sq, qf = _hgrn_chunk_terms(q, fl, lb)
        b = _tri_mm(tri, lf, BNN)
        bm, bl = b[:, C // 2 - 1:C // 2, :], b[:, C - 1:C, :]
        e_qd, e_kd, e_ke, e_b = jnp.exp(b - bm), jnp.exp(bm - b), jnp.exp(bl - b), jnp.exp(b)
        qd, kd, ke, qe = qf * e_qd, kk * e_kd, kk * e_ke, qf * e_b
        dec = jnp.exp(bl)
        A = jnp.where(tri, _dot(qd, kd, BNT), 0.0)
        o = _dot(A, v, BNN) + _dot(qe, st0, BNT)
        r = lax.rsqrt(jnp.mean(o * o, axis=-1, keepdims=True) + EPS)
        sg = _sigmoid(g)
        on = o * r * og
        dg_ref[...] = flat((dout * on * (sg * (1.0 + g * (1.0 - sg)))).astype(BF16))
        don = dout * (g * sg)
        pog_ref[...] = _fold8(flat(don * o * r))
        dyh = don * og
        do = r * (dyh - o * (r * r) * jnp.mean(dyh * o, axis=-1, keepdims=True))
        g_st = _dot(do, qe, BTN)
        run = dS[...]
        after = [None] * ncl
        for ci in reversed(range(ncl)):
            after[ci] = run
            run = g_st[ci] + run * dec[ci]
        dS[...] = run
        d_after = jnp.stack(after, axis=0)
        ddec = jnp.sum(d_after * st0, axis=1, keepdims=True)
        dqe = _dot(do, st0, BNN)
        dke = _dot(v, d_after, BNN)
        dA = jnp.where(tri, _dot(do, v, BNT), 0.0)
        dv = _dot(ke, d_after, BNT) + _dot(A, do, BTN)
        dqd = _dot(dA, kd, BNN)
        dkd = _dot(dA, qd, BTN)
        di_ref[...] = flat(dv.astype(BF16))
        dqf = dqe * e_b + dqd * e_qd
        dkk = dkd * e_kd + dke * e_ke
        t_qd, t_kd, t_ke = dqd * qd, dkd * kd, dke * ke
        db = dqe * qe + t_qd - t_kd - t_ke
        dbm = jnp.sum(t_kd - t_qd, axis=1, keepdims=True)
        dbl = jnp.sum(t_ke, axis=1, keepdims=True) + ddec * dec
        db = db + jnp.where(rowi == C // 2 - 1, dbm, 0.0) + jnp.where(rowi == C - 1, dbl, 0.0)
        dlf = _tri_mm(tri, db, BTN)
        dfv = dlf / f - dkk
        df_ref[...] = flat((dfv * (1.0 - lb) * sig * (1.0 - sig)).astype(BF16))
        plb_ref[...] = _fold8(flat(dfv * (1.0 - sig)))
        dq_ref[...] = flat((dqf * (sq * (1.0 + q * (1.0 - sq)))).astype(BF16))

    def col(off):
        return pl.BlockSpec((tt, A_HD), lambda h, t: (nT - 1 - t, off // A_HD + h))

    head_vec = lambda rows: pl.BlockSpec((rows, A_HD), lambda h, t: (0, h))
    o_spec = pl.BlockSpec((tt, A_HD), lambda h, t: (nT - 1 - t, h))
    p_spec = pl.BlockSpec((8, A_HD), lambda h, t: (t, h))
    o_shape = jax.ShapeDtypeStruct((T, AW), BF16)
    p_shape = jax.ShapeDtypeStruct((nT * 8, AW), F32)
    return _pcall(
        body, grid=(A_HEADS, nT),
        in_specs=[col(OFF_QA), col(OFF_FA), col(OFF_IA), col(OFF_GA),
                  pl.BlockSpec((1, ncl, A_HD, A_HD), lambda h, t: (h, nT - 1 - t, 0, 0)),
                  pl.BlockSpec((tt, A_HD), lambda h, t: (nT - 1 - t, h)), head_vec(2), head_vec(1)],
        out_specs=[o_spec, o_spec, o_spec, o_spec, p_spec, p_spec],
        out_shape=[o_shape, o_shape, o_shape, o_shape, p_shape, p_shape],
        scratch_shapes=[pltpu.VMEM((A_HD, A_HD), F32)], name="hgrn_bwd", semantics=("parallel", "arbitrary"),
        args=[proj, proj, proj, proj, st, dy, lb_logits, o_gain], job=job)


LANES = 128
Q_COLS = BW // LANES


def _low_half():
    return lax.broadcasted_iota(jnp.int32, (1, LANES), 1) < B_HD


def _half_sum(t, low):
    lo = jnp.sum(jnp.where(low, t, 0.0), axis=-1, keepdims=True)
    hi = jnp.sum(jnp.where(low, 0.0, t), axis=-1, keepdims=True)
    return jnp.where(low, lo, hi)


def _half_rms(t, low):
    r = lax.rsqrt(_half_sum(t * t, low) * (1.0 / B_HD) + EPS)
    return t * r, r


def _fold_halves(p, low):
    return jnp.where(low, p + pltpu.roll(p, B_HD, 1), 0.0)


def _stack_cols(x):
    return jnp.stack([x[:, c * LANES:(c + 1) * LANES] for c in range(Q_COLS)], axis=0).reshape(KV_HEADS, 2 * BLK, LANES)


def _col_of(t, c):
    return t[c // 2, (c % 2) * BLK:(c % 2 + 1) * BLK]


def _split_halves(col, s, low):
    own = jnp.where(low if s == 0 else jnp.logical_not(low), col, 0.0)
    other = pltpu.roll(own, B_HD, 1)
    return (own, other) if s == 0 else (other, own)


def _swa_keys(kp_ref, kc_ref, vp_ref, vc_ref, kg, low):
    k_lo, k_hi, v_lo, v_hi, hats = [], [], [], [], []
    for j in range(KVW // LANES):
        cs = slice(j * LANES, (j + 1) * LANES)
        k_hat, k_r = _half_rms(jnp.concatenate([kp_ref[:, cs], kc_ref[:, cs]], axis=0), low)
        vcol = jnp.concatenate([vp_ref[:, cs], vc_ref[:, cs]], axis=0)
        hats.append((k_hat, k_r))
        for s in range(2):
            for dst_lo, dst_hi, col in ((k_lo, k_hi, k_hat * kg), (v_lo, v_hi, vcol)):
                lo, hi = _split_halves(col, s, low)
                dst_lo.append(lo)
                dst_hi.append(hi)
    st = lambda parts: jnp.stack(parts, axis=0)
    return st(k_lo), st(k_hi), st(v_lo), st(v_hi), hats


def _swa_mask(first_block):
    qi = lax.broadcasted_iota(jnp.int32, (BLK, 2 * BLK), 0) + BLK
    ki = lax.broadcasted_iota(jnp.int32, (BLK, 2 * BLK), 1)
    rel = qi - ki
    m = (rel >= 0) & (rel < BLK) & (jnp.logical_not(first_block) | (ki >= BLK))
    return jnp.concatenate([m, m], axis=0)


def _sink_cols(sk_ref, hi):
    top = lax.broadcasted_iota(jnp.int32, (2 * BLK, 1), 0) < BLK
    return jnp.stack([jnp.where(top, sk_ref[0, GROUP * hk + hi], sk_ref[0, GROUP * hk + 2 + hi])
                      for hk in range(KV_HEADS)], axis=0)


def _swa_probs(qn, k_half, sink, mask):
    s = jnp.where(mask, _dot(qn, k_half, BNT) * (B_HD ** -0.5), NEG)
    m = jnp.maximum(jnp.max(s, axis=-1, keepdims=True), sink)
    p = jnp.exp(s - m)
    ps = jnp.exp(sink - m)
    inv = 1.0 / (jnp.sum(p, axis=-1, keepdims=True) + ps)
    return p * inv, ps * inv


def _swa_fwd(proj, q_gain, k_gain, sinks, job=None):
    T = proj.shape[0]
    nb = T // BLK

    def body(q_ref, kc_ref, kp_ref, vc_ref, vp_ref, qg_ref, kg_ref, sk_ref, o_ref):
        low = _low_half()
        mask = _swa_mask(pl.program_id(0) == 0)
        qn = _half_rms(_stack_cols(q_ref[...]), low)[0] * qg_ref[...]
        k_lo, k_hi, v_lo, v_hi, _ = _swa_keys(kp_ref, kc_ref, vp_ref, vc_ref, kg_ref[...], low)
        p_lo, _ = _swa_probs(qn, k_lo, _sink_cols(sk_ref, 0), mask)
        p_hi, _ = _swa_probs(qn, k_hi, _sink_cols(sk_ref, 1), mask)
        o = (_dot(p_lo, v_lo, BNN) + _dot(p_hi, v_hi, BNN)).astype(BF16)
        for c in range(Q_COLS):
            o_ref[:, c * LANES:(c + 1) * LANES] = _col_of(o, c)

    q_gain, k_gain = jnp.tile(q_gain, (1, 2)), jnp.tile(k_gain, (1, 2))
    cur = lambda w, off: pl.BlockSpec((BLK, w), lambda i: (i, off // w))
    prev = lambda w, off: pl.BlockSpec((BLK, w), lambda i: (jnp.maximum(i - 1, 0), off // w))
    small = lambda n: pl.BlockSpec((1, 2 * n), lambda i: (0, 0))
    return _pcall(
        body, grid=(nb,),
        in_specs=[cur(BW, OFF_QB), cur(KVW, OFF_KB), prev(KVW, OFF_KB), cur(KVW, OFF_VB), prev(KVW, OFF_VB),
                  small(B_HD), small(B_HD), pl.BlockSpec(memory_space=pltpu.SMEM)],
        out_specs=[pl.BlockSpec((BLK, BW), lambda i: (i, 0))],
        out_shape=[jax.ShapeDtypeStruct((T, BW), BF16)], scratch_shapes=[], name="swa_fwd", semantics=("parallel",),
        args=[proj, proj, proj, proj, proj, q_gain, k_gain, sinks], job=job)


def _swa_bwd(proj, dout, q_gain, k_gain, sinks, job=None):
    T = proj.shape[0]
    nb = T // BLK
    W = BW + 2 * KVW

    def body(q_ref, kc_ref, kp_ref, vc_ref, vp_ref, do_ref, qg_ref, kg_ref, sk_ref,
             dq_ref, dkv_ref, pqg_ref, pkg_ref, psk_ref, dkn_c, dv_c):
        i = pl.program_id(0)
        live = i < nb
        low = _low_half()
        high = jnp.logical_not(low)
        qg, kg = qg_ref[...], kg_ref[...]
        mask = _swa_mask(i == 0)
        lane = lax.broadcasted_iota(jnp.int32, (1, LANES), 1)
        scale = B_HD ** -0.5

        @pl.when(i == 0)
        def _():
            dkn_c[...] = jnp.zeros_like(dkn_c)
            dv_c[...] = jnp.zeros_like(dv_c)

        q_hat, q_r = _half_rms(_stack_cols(q_ref[...]), low)
        qn = q_hat * qg
        k_lo, k_hi, v_lo, v_hi, hats = _swa_keys(kp_ref, kc_ref, vp_ref, vc_ref, kg, low)
        do = _stack_cols(do_ref[...])
        dqn = jnp.zeros((KV_HEADS, 2 * BLK, LANES), F32)
        acc_sk = jnp.zeros((1, LANES), F32)
        dk_parts, dv_parts = [], []
        for hi, (k_h, v_h) in enumerate(((k_lo, v_lo), (k_hi, v_hi))):
            p, ps = _swa_probs(qn, k_h, _sink_cols(sk_ref, hi), mask)
            dp = _dot(do, v_h, BNT)
            delta = jnp.sum(p * dp, axis=-1, keepdims=True)
            ds = p * (dp - delta) * scale
            dqn = dqn + _dot(ds, k_h, BNN)
            dk_parts.append(_dot(ds, qn, BTN))
            dv_parts.append(_dot(p, do, BTN))
            t = ps * delta
            for hk in range(KV_HEADS):
                for rows in range(2):
                    h = GROUP * hk + 2 * rows + hi
                    acc_sk = acc_sk + jnp.where(
                        lane == h, -jnp.sum(t[hk, rows * BLK:(rows + 1) * BLK], axis=0, keepdims=True), 0.0)
        dqh = dqn * qg
        dq = (q_r * (dqh - q_hat * (_half_sum(dqh * q_hat, low) * (1.0 / B_HD)))).astype(BF16)
        for c in range(Q_COLS):
            dq_ref[:, c * LANES:(c + 1) * LANES] = _col_of(dq, c)
        acc_qg = _fold_halves(_fold8((dqn * q_hat).reshape(KV_HEADS * 2 * BLK, LANES)), low)

        def native(parts, j):
            lo_arr, hi_arr = parts
            a, b = 2 * j, 2 * j + 1
            return (jnp.where(low, lo_arr[a], 0.0) + pltpu.roll(jnp.where(high, hi_arr[a], 0.0), B_HD, 1)
                    + jnp.where(high, hi_arr[b], 0.0) + pltpu.roll(jnp.where(low, lo_arr[b], 0.0), B_HD, 1))

        acc_kg = jnp.zeros((8, LANES), F32)
        for j in range(KVW // LANES):
            cs = slice(j * LANES, (j + 1) * LANES)
            dkn = jnp.where(live, native(dk_parts, j), 0.0)
            dvc = jnp.where(live, native(dv_parts, j), 0.0)
            kp_hat, kp_r = hats[j][0][:BLK], hats[j][1][:BLK]
            dkn_prev = dkn_c[:, cs] + dkn[:BLK]
            dv_prev = dv_c[:, cs] + dvc[:BLK]
            acc_kg = acc_kg + _fold8(dkn_prev * kp_hat)
            dkh = dkn_prev * kg
            dkv_ref[:, cs] = (kp_r * (dkh - kp_hat * (_half_sum(dkh * kp_hat, low) * (1.0 / B_HD)))).astype(BF16)
            dkv_ref[:, KVW + j * LANES:KVW + (j + 1) * LANES] = dv_prev.astype(BF16)
            dkn_c[:, cs] = dkn[BLK:]
            dv_c[:, cs] = dvc[BLK:]
        keep = jnp.where(i > 0, 1.0, 0.0)
        pqg_ref[...] = jnp.where(live, acc_qg, 0.0)
        pkg_ref[...] = _fold_halves(acc_kg, low) * keep
        psk_ref[...] = jnp.broadcast_to(jnp.where(live, acc_sk, 0.0), (8, LANES)) * (
            lax.broadcasted_iota(jnp.int32, (8, LANES), 0) == 0).astype(F32)

    q_gain, k_gain = jnp.tile(q_gain, (1, 2)), jnp.tile(k_gain, (1, 2))
    last = nb - 1
    cur = lambda w, off: pl.BlockSpec((BLK, w), lambda i: (jnp.minimum(i, last), off // w))
    prev = lambda w, off: pl.BlockSpec((BLK, w), lambda i: (jnp.maximum(i - 1, 0), off // w))
    small = lambda n: pl.BlockSpec((1, 2 * n), lambda i: (0, 0))
    part = pl.BlockSpec((8, 128), lambda i: (i, 0))
    p_shape = jax.ShapeDtypeStruct(((nb + 1) * 8, 128), F32)
    return _pcall(
        body, grid=(nb + 1,),
        in_specs=[cur(BW, OFF_QB), cur(KVW, OFF_KB), prev(KVW, OFF_KB), cur(KVW, OFF_VB), prev(KVW, OFF_VB),
                  pl.BlockSpec((BLK, BW), lambda i: (jnp.minimum(i, last), 0)), small(B_HD), small(B_HD),
                  pl.BlockSpec(memory_space=pltpu.SMEM)],
        out_specs=[pl.BlockSpec((BLK, BW), lambda i: (i, 0)),
                   pl.BlockSpec((BLK, 2 * KVW), lambda i: (jnp.maximum(i - 1, 0), 0)), part, part, part],
        out_shape=[jax.ShapeDtypeStruct((T + BLK, BW), BF16), jax.ShapeDtypeStruct((T, 2 * KVW), BF16),
                   p_shape, p_shape, p_shape],
        scratch_shapes=[pltpu.VMEM((BLK, KVW), F32), pltpu.VMEM((BLK, KVW), F32)], name="swa_bwd",
        semantics=("arbitrary",), args=[proj, proj, proj, proj, proj, dout, q_gain, k_gain, sinks], job=job)


def _branch_merge(ya_pre, attn, wa_t, wb_t, proj, tm, tn, job=None):
    T = ya_pre.shape[0]

    def body(a_ref, b_ref, wa_ref, wb_ref, ga_ref, gb_ref, ya_ref, yb_ref, mg_ref):
        ya = lax.dot_general(a_ref[...], wa_ref[...], NT, preferred_element_type=F32)
        yb = lax.dot_general(b_ref[...], wb_ref[...], NT, preferred_element_type=F32)
        ya_ref[...] = ya.astype(BF16)
        yb_ref[...] = yb.astype(BF16)
        mg_ref[...] = (_sigmoid(ga_ref[...]) * ya + _sigmoid(gb_ref[...]) * yb).astype(BF16)

    o_spec = pl.BlockSpec((tm, tn), lambda i, j: (i, j))
    o_shape = jax.ShapeDtypeStruct((T, D), BF16)
    return _pcall(
        body, grid=(T // tm, D // tn),
        in_specs=[pl.BlockSpec((tm, AW), lambda i, j: (i, 0)), pl.BlockSpec((tm, BW), lambda i, j: (i, 0)),
                  pl.BlockSpec((tn, AW), lambda i, j: (j, 0)), pl.BlockSpec((tn, BW), lambda i, j: (j, 0)),
                  pl.BlockSpec((tm, tn), lambda i, j: (i, OFF_GTA // tn + j)),
                  pl.BlockSpec((tm, tn), lambda i, j: (i, OFF_GTB // tn + j))],
        out_specs=[o_spec, o_spec, o_spec], out_shape=[o_shape, o_shape, o_shape], scratch_shapes=[], name="branch_merge",
        semantics=("parallel", "parallel"), args=[ya_pre, attn, wa_t, wb_t, proj, proj], job=job)


def _ij(i, j, k):
    return (i, j)


def _local_step(x, tgt, mod, g1, g2, lbl, og, qg, kg, sk, shards, c_arr):
    win_s, wa_s, wb_s, wout_s, wmi_s, wmo_s = shards
    T = x.shape[0]
    tm, tr, tt = min(1024, T), min(256, T), min(2048, T)
    tk_t = min(1024, T)
    tn = 512
    sh1, sc1, gt1, sh2, sc2, gt2 = (mod[:, i * D:(i + 1) * D] for i in range(N_MOD))
    nI = T // tm
    blk = (tm, tn)
    vec_j = ((1, tn), lambda i, j, k: (0, j))

    h = _rms_mod_fwd("rms1_fwd", x, g1, sc1, sh1, tr)

    def epi_store(acc, ex, ou):
        ou[0][...] = acc.astype(ou[0].dtype)

    tm2 = min(2048, T)
    blk2 = (tm2, tn)

    full = lambda s: (0, s.shape[0])
    last = wmi_s.shape[0]
    (win_t,) = _run_job("gather_w_in", _gather_relay_job([win_s]))
    (proj,), (wa_t, wb_t, w_out, wmi_part) = _mm(
        "in_proj", "nt", [(h, D)], win_t, T, IN_W, D, tm2, tn, D, [], [((T, IN_W), F32, blk2, _ij)], epi_store,
        job=_gather_job([wa_s, wb_s, wout_s, wmi_s], rows=[full(wa_s), full(wb_s), full(wout_s), (0, MI_CUTS[0])]))
    (ya_pre, st), (wmi_part,) = _hgrn_fwd(
        proj, lbl, og, tt, job=_gather_job([wmi_s], rows=[MI_CUTS], into=[wmi_part]))
    (attn,), (wmi_t, wmo_part) = _swa_fwd(
        proj, qg, kg, sk, job=_gather_job([wmi_s, wmo_s], rows=[(MI_CUTS[1], last), (0, MO_CUT)], into=[wmi_part, None]))
    (ya, yb, merged), _ = _branch_merge(ya_pre, attn, wa_t, wb_t, proj, tm, tn)

    def epi_res1(acc, ex, ou):
        x_ref, gt_ref = ex
        ou[0][...] = acc.astype(BF16)
        ou[1][...] = x_ref[...] + gt_ref[...] * acc

    mo, x1 = _mm("out_proj", "nn", [(merged, D)], w_out, T, D, D, tm, tn, D, [(x, blk, _ij), (gt1, *vec_j)],
                 [((T, D), BF16, blk, _ij), ((T, D), F32, blk, _ij)], epi_res1)
    h2 = _rms_mod_fwd("rms2_fwd", x1, g2, sc2, sh2, tr)

    def epi_relu2(acc, ex, ou):
        r = jnp.maximum(acc, 0.0)
        ou[0][...] = r.astype(BF16)
        ou[1][...] = (r * r).astype(BF16)

    (r, a), (w_mo,) = _mm("mlp_in", "nt", [(h2, D)], wmi_t, T, HID, D, tm2, tn, D, [],
                          [((T, HID), BF16, blk2, _ij), ((T, HID), BF16, blk2, _ij)], epi_relu2,
                          job=_gather_job([wmo_s], rows=[(MO_CUT, last)], into=[wmo_part]))

    def loss_rows(acc, vecs, bufs, parts):
        gt = vecs[0][...]
        x1_buf, t_buf, dz_buf = bufs
        rows = min(64, tm)
        loss_sum, gate_sum = jnp.zeros((8, D), F32), jnp.zeros((8, D), F32)
        for r0 in range(0, tm, rows):
            rs = slice(r0, r0 + rows)
            z = acc[rs, :]
            e = x1_buf[rs, :] + gt * z - t_buf[rs, :]
            dy = e * (1.0 / D)
            t_buf[rs, :] = dy
            dz_buf[rs, :] = (gt * dy).astype(BF16)
            loss_sum = loss_sum + _fold8(e * e)
            gate_sum = gate_sum + _fold8(dy * z)
        parts[0][...] = loss_sum * (0.5 / D)
        parts[1][...] = gate_sum

    part_rows = ((nI * 8, D), F32, (8, D), lambda i, j, k: (i, 0))
    dy, dz, p_loss, p_gt2 = _rows_mm(
        "mlp_out", [(a, HID)], w_mo, T, tm, 1024, [gt2], [(F32, x1, False), (F32, tgt, True), (BF16, None, True)],
        [part_rows, part_rows], loss_rows)

    def epi_du(acc, ex, ou):
        ou[0][...] = (acc * (2.0 * ex[0][...].astype(F32))).astype(BF16)

    (du,) = _mm("mlp_out_dx", "nt", [(dz, D)], w_mo, T, HID, D, tm2, tn, D, [(r, blk2, _ij)],
                [((T, HID), BF16, blk2, _ij)], epi_du)
    gblk = (1024, 1024)
    gwide = (1024, D)
    pair_sum = lambda nm, g, r1: _pair_sum("pair_sum_" + nm, g, r1, c_arr, _sum_rows(r1.shape[1]))
    (g_mo,) = _mm("mlp_out_dw", "tn", [(a, HID)], dz, HID, D, T, 1024, D, tk_t, [], [((HID, D), BF16, gwide, _ij)], epi_store)
    (dh2,), (r1_mo,) = _mm("mlp_in_dx", "nn", [(du, HID)], wmi_t, T, D, HID, tm, D, 1024, [],
                           [((T, D), F32, (tm, D), _ij)], epi_store, job=_pair_job([g_mo]))
    dx1, p_sh2, p_sc2, p_g2, dmo, p_gt1 = _rms_mod_bwd("rms2_bwd", dh2, x1, g2, sc2, dy, tr, gate=gt1, mo=mo)
    s_mo = pair_sum("mlp_out", g_mo, r1_mo)
    near, far = (1, 2), (3,)
    (g_mi,), (rn_mo,) = _mm("mlp_in_dw", "tn", [(du, HID)], h2, HID, D, T, 1024, D, tk_t, [],
                            [((HID, D), BF16, gwide, _ij)], epi_store, job=_chip_job([s_mo], near))

    def epi_gates(acc, ex, ou):
        ya_ref, yb_ref, ga_ref, gb_ref = ex
        sa, sb = _sigmoid(ga_ref[...]), _sigmoid(gb_ref[...])
        ou[0][...] = (acc * sa).astype(BF16)
        ou[1][...] = (acc * sb).astype(BF16)
        ou[2][...] = (acc * ya_ref[...].astype(F32) * (sa * (1.0 - sa))).astype(BF16)
        ou[3][...] = (acc * yb_ref[...].astype(F32) * (sb * (1.0 - sb))).astype(BF16)

    o_bf = ((T, D), BF16, blk, _ij)
    (dya, dyb, dga, dgb), (rf_mo, r1_mi) = _mm(
        "out_proj_dx", "nt", [(dmo, D)], w_out, T, D, D, tm, tn, D,
        [(ya, blk, _ij), (yb, blk, _ij), (proj, blk, lambda i, j, k: (i, OFF_GTA // tn + j)),
         (proj, blk, lambda i, j, k: (i, OFF_GTB // tn + j))], [o_bf, o_bf, o_bf, o_bf], epi_gates,
        job=_both(_chip_job([s_mo], far), _pair_job([g_mi])))
    s_mi = pair_sum("mlp_in", g_mi, r1_mi)
    (g_out,) = _mm("out_proj_dw", "tn", [(merged, D)], dmo, D, D, T, 1024, 1024, tk_t, [], [((D, D), BF16, gblk, _ij)], epi_store)
    dya_pre, dattn = _twin_mm("branch_dx", "nn", [(dya, wa_t), (dyb, wb_t)], T, AW, D, tm, tn, D, F32)
    g_a, g_b = _twin_mm("branch_dw", "tn", [(dya, ya_pre), (dyb, attn)], D, AW, T, 1024, 1024, tk_t, BF16)
    (dqa, dfa, dia, dgg, p_lb, p_og), (rn_mi, r1_out, r1_a, r1_b) = _hgrn_bwd(
        proj, st, dya_pre, lbl, og, tt, job=_both(_chip_job([s_mi], near), _pair_job([g_out, g_a, g_b])))
    (dqb, dkv, p_qg, p_kg, p_sk), (rf_mi,) = _swa_bwd(proj, dattn, qg, kg, sk, job=_chip_job([s_mi], far))
    s_out, s_a, s_b = pair_sum("out", g_out, r1_out), pair_sum("branch_a", g_a, r1_a), pair_sum("branch_b", g_b, r1_b)
    pieces = [(dqa, AW), (dfa, AW), (dia, AW), (dgg, AW), (dqb, BW), (dkv, 2 * KVW), (dga, D), (dgb, D)]
    (g_in,), (r2_out, r2_a, r2_b) = _pieces_tn("in_proj_dw", pieces, h, 512, job=_chip_job([s_out, s_a, s_b]))
    (r1_in,) = _run_job("pair_w_in", _pair_job([g_in]))
    s_in = pair_sum("in", g_in, r1_in)
    (dx, p_sh1, p_sc1, p_g1), (r2_in,) = _pieces_nn_rms(
        "in_proj_dx", pieces, win_t, x, g1, sc1, dx1, tm, 512, job=_chip_job([s_in]))

    partials = dict(sh1=p_sh1, sc1=p_sc1, gt1=p_gt1, sh2=p_sh2, sc2=p_sc2, gt2=p_gt2, g1=p_g1, g2=p_g2,
                    lb=p_lb, og=p_og, qg=p_qg, kg=p_kg, sk=p_sk, loss=p_loss)
    sums = dict(w_in=(s_in, [r2_in]), w_branch_a=(s_a, [r2_a]), w_branch_b=(s_b, [r2_b]), w_out=(s_out, [r2_out]),
                w_mlp_in=(s_mi, [rn_mi, rf_mi]), w_mlp_out=(s_mo, [rn_mo, rf_mo]))
    return dx, sums, partials


def _exchange_slots(buf, send_sems, recv_sems):
    me = _mesh_pos()
    mine = buf.at[_index(me)]
    sends = []
    for k in range(1, N_DEV):
        cp = pltpu.make_async_remote_copy(src_ref=mine, dst_ref=mine, send_sem=send_sems.at[k - 1],
                                          recv_sem=recv_sems.at[k - 1], device_id=_flip(me, k), device_id_type=MESH)
        cp.start()
        sends.append(cp)
    for k in range(1, N_DEV):
        theirs = buf.at[_index(_flip(me, k))]
        pltpu.make_async_remote_copy(src_ref=theirs, dst_ref=theirs, send_sem=send_sems.at[k - 1],
                                     recv_sem=recv_sems.at[k - 1], device_id=_flip(me, k), device_id_type=MESH).wait_recv()
    for cp in sends:
        cp.wait_send()


ADA_W = N_MOD * D // N_DEV


def _ada_mod(c, w_ada, b_shard):
    def body(c_ref, w_ref, b_ref, mod_ref, sc_ref, cbuf, mbuf, s1, r1, s2, r2):
        me = _index(_mesh_pos())
        cbuf[me] = c_ref[...]
        _exchange_slots(cbuf, s1, r1)
        row = lax.broadcasted_iota(jnp.int32, (N_DEV, D), 0)
        call = jnp.zeros((N_DEV, D), F32)
        for d in range(N_DEV):
            call = jnp.where(row == d, cbuf[d], call)
        sc = call * _sigmoid(call)
        sc_ref[...] = sc
        mbuf[me] = _dot(sc, w_ref[...]) + b_ref[...]
        _exchange_slots(mbuf, s2, r2)
        for s in range(N_DEV):
            mod_ref[:, s * ADA_W:(s + 1) * ADA_W] = mbuf[s, pl.ds(me, 1), :]

    return pl.pallas_call(
        body, in_specs=[_VMEM, _VMEM, _VMEM], out_specs=[_VMEM, _VMEM],
        out_shape=[jax.ShapeDtypeStruct((1, N_MOD * D), F32), jax.ShapeDtypeStruct((N_DEV, D), F32)],
        scratch_shapes=[pltpu.VMEM((N_DEV, 1, D), F32), pltpu.VMEM((N_DEV, N_DEV, ADA_W), F32),
                        _SEMS(N_DEV - 1), _SEMS(N_DEV - 1), _SEMS(N_DEV - 1), _SEMS(N_DEV - 1)],
        name="ada_mod", compiler_params=pltpu.CompilerParams(vmem_limit_bytes=VMEM_LIMIT),
    )(c, w_ada, b_shard)


SMALL_SEGS = (("b_ada", N_MOD * D), ("norm1_gain", D), ("norm2_gain", D), ("lb0", AW), ("lb1", AW),
              ("hgrn_o_gain", AW), ("q_norm_gain", 128), ("k_norm_gain", 128), ("sinks", 128))
SMALL_W = sum(w for _, w in SMALL_SEGS)
X_SEGS = (("sh1", D), ("sc1", D), ("gt1", D), ("sh2", D), ("sc2", D), ("gt2", D), ("g1", D), ("g2", D),
          ("lb", AW), ("og", AW), ("qg", 128), ("kg", 128), ("sk", 128), ("loss", 128))
X_W = sum(w for _, w in X_SEGS)


def _offsets(segs):
    out, o = {}, 0
    for name, w in segs:
        out[name] = (o, w)
        o += w
    return out


def _small_reduce(parts, lb_logits):
    xo, so = _offsets(X_SEGS), _offsets(SMALL_SEGS)
    names = [nm for nm, _ in X_SEGS]

    def body(*refs):
        p_refs = dict(zip(names, refs[:len(names)]))
        lbl_ref, allx, gs_ref, loss_ref, send_sems, recv_sems = refs[len(names):]
        me = _index(_mesh_pos())
        for nm, (o, w) in xo.items():
            if nm == "loss":
                allx[me, :, o:o + w] = jnp.broadcast_to(jnp.sum(p_refs[nm][...]), (1, w))
            else:
                allx[me, :, o:o + w] = jnp.sum(p_refs[nm][...], axis=0, keepdims=True)
        _exchange_slots(allx, send_sems, recv_sems)
        tot = allx[0]
        for d in range(1, N_DEV):
            tot = tot + allx[d]
        seg = lambda nm: tot[:, xo[nm][0]:xo[nm][0] + xo[nm][1]]

        def put(nm, v):
            gs_ref[:, so[nm][0]:so[nm][0] + so[nm][1]] = v

        put("b_ada", tot[:, 0:N_MOD * D])
        put("norm1_gain", seg("g1"))
        put("norm2_gain", seg("g2"))
        lbl = lbl_ref[...]
        lb = _sigmoid(lbl[0:1, :] - lbl[1:2, :])
        dl0 = seg("lb") * lb * (1.0 - lb)
        put("lb0", dl0)
        put("lb1", -dl0)
        put("hgrn_o_gain", seg("og"))
        put("q_norm_gain", seg("qg"))
        put("k_norm_gain", seg("kg"))
        put("sinks", seg("sk"))
        loss_ref[...] = seg("loss")

    return pl.pallas_call(
        body, in_specs=[_VMEM] * (len(names) + 1), out_specs=[_VMEM, _VMEM, _VMEM],
        out_shape=[jax.ShapeDtypeStruct((N_DEV, 1, X_W), F32), jax.ShapeDtypeStruct((1, SMALL_W), F32),
                   jax.ShapeDtypeStruct((1, 128), F32)],
        scratch_shapes=[_SEMS(N_DEV - 1), _SEMS(N_DEV - 1)], name="small_reduce",
        compiler_params=pltpu.CompilerParams(vmem_limit_bytes=VMEM_LIMIT),
    )(*[parts[nm] for nm in names], lb_logits)


def _adamw_math(w, g, m, v):
    m = B1 * m + (1.0 - B1) * g
    v = B2 * v + (1.0 - B2) * (g * g)
    m_hat = m / (1.0 - B1 ** STEP)
    v_hat = v / (1.0 - B2 ** STEP)
    return -LR * (m_hat / (jnp.sqrt(v_hat) + ADAM_EPS) + WD * w), m, v


def _sum_rows(rs):
    return 256 if rs % 256 == 0 else rs // 2


def _pair_sum(name, g, recv, c_arr, tr):
    _, rs, cols = recv.shape
    blk = (1, tr, cols)

    def body(c_ref, g_ref, r_ref, o_ref):
        o_ref[...] = (g_ref[...].astype(F32) + r_ref[...].astype(F32)).astype(BF16)

    grid_spec = pltpu.PrefetchScalarGridSpec(
        num_scalar_prefetch=1, grid=(4, rs // tr),
        in_specs=[pl.BlockSpec(blk, lambda q, i, c: (2 * q + c[0], i, 0)), pl.BlockSpec(blk, lambda q, i, c: (q, i, 0))],
        out_specs=pl.BlockSpec(blk, lambda q, i, c: (q, i, 0)))
    return pl.pallas_call(body, grid_spec=grid_spec, out_shape=jax.ShapeDtypeStruct((4, rs, cols), BF16), name=name,
                          compiler_params=_params(("parallel", "parallel")))(c_arr, g.reshape(N_DEV, rs, cols), recv)


def _sum_adamw(name, sums, recvs, q_arr, w, m, v, transposed, tile):
    rows, cols = w.shape
    nR = len(recvs)

    def body(q_ref, s_ref, *refs):
        r_refs = refs[:nR]
        w_ref, m_ref, v_ref, g_ref, d_ref, nm_ref, nv_ref = refs[nR:]
        g = s_ref[0].astype(F32)
        for r_ref in r_refs:
            for slot in range(r_ref.shape[0]):
                g = g + r_ref[slot].astype(F32)
        g = g.T if transposed else g
        g_ref[...] = g
        d_ref[...], nm_ref[...], nv_ref[...] = _adamw_math(w_ref[...], g, m_ref[...], v_ref[...])

    if transposed:
        slab = lambda n, first: pl.BlockSpec((n, cols, tile), lambda i, q: (first(q), 0, i))
    else:
        slab = lambda n, first: pl.BlockSpec((n, tile, cols), lambda i, q: (first(q), i, 0))
    spec = pl.BlockSpec((tile, cols), lambda i, q: (i, 0))
    shape = jax.ShapeDtypeStruct((rows, cols), F32)
    grid_spec = pltpu.PrefetchScalarGridSpec(
        num_scalar_prefetch=1, grid=(rows // tile,),
        in_specs=[slab(1, lambda q: q[0])] + [slab(r.shape[0], lambda q: 0) for r in recvs] + [spec] * 3,
        out_specs=[spec] * 4)
    return pl.pallas_call(body, grid_spec=grid_spec, out_shape=[shape] * 4, name=name,
                          compiler_params=_params(("parallel",)))(q_arr, sums, *recvs, w, m, v)


def _adamw(name, w, g, m, v, tr):
    rows, cols = w.shape

    def body(w_ref, g_ref, m_ref, v_ref, d_ref, nm_ref, nv_ref):
        d_ref[...], nm_ref[...], nv_ref[...] = _adamw_math(w_ref[...], g_ref[...], m_ref[...], v_ref[...])

    spec = pl.BlockSpec((tr, cols), lambda i: (i, 0))
    shape = jax.ShapeDtypeStruct((rows, cols), F32)
    return pl.pallas_call(
        body, grid=(rows // tr,), in_specs=[spec] * 4, out_specs=[spec] * 3, out_shape=[shape] * 3, name=name,
        compiler_params=_params(("parallel",)),
    )(w, g, m, v)


def _ada_update(sc_t, dmod_cols, w, m, v, tr):
    rows, cols = w.shape

    def body(s_ref, d_ref, w_ref, m_ref, v_ref, g_ref, dl_ref, nm_ref, nv_ref):
        g = jnp.dot(s_ref[...], d_ref[...], precision=lax.Precision.HIGHEST, preferred_element_type=F32)
        g_ref[...] = g
        dl_ref[...], nm_ref[...], nv_ref[...] = _adamw_math(w_ref[...], g, m_ref[...], v_ref[...])

    spec = pl.BlockSpec((tr, cols), lambda i: (i, 0))
    shape = jax.ShapeDtypeStruct((rows, cols), F32)
    return pl.pallas_call(
        body, grid=(rows // tr,),
        in_specs=[pl.BlockSpec((tr, N_DEV), lambda i: (i, 0)), pl.BlockSpec((N_DEV, cols), lambda i: (0, 0)), spec, spec, spec],
        out_specs=[spec] * 4, out_shape=[shape] * 4, name="ada_update", compiler_params=_params(("parallel",)),
    )(sc_t, dmod_cols, w, m, v)


BIG = ("w_in", "w_branch_a", "w_branch_b", "w_out", "w_mlp_in", "w_mlp_out")
COLUMN_SHARDED = ("w_in", "w_branch_a", "w_branch_b", "w_mlp_in")
AS_TRANSPOSE = ("w_in",)
WEIGHTS = ("w_ada", "b_ada", "norm1_gain", "w_in", "lb_logits", "hgrn_o_gain", "q_norm_gain", "k_norm_gain", "sinks",
           "w_branch_a", "w_branch_b", "w_out", "norm2_gain", "w_mlp_in", "w_mlp_out")


def _to_bf16(name, w, transposed, tile=256):
    rows, cols = w.shape

    def body(w_ref, o_ref):
        v = w_ref[...]
        o_ref[...] = (v.T if transposed else v).astype(BF16)

    out_spec = pl.BlockSpec((cols, tile), lambda i: (0, i)) if transposed else pl.BlockSpec((tile, cols), lambda i: (i, 0))
    return pl.pallas_call(
        body, grid=(rows // tile,), in_specs=[pl.BlockSpec((tile, cols), lambda i: (i, 0))], out_specs=out_spec,
        out_shape=jax.ShapeDtypeStruct((cols, rows) if transposed else (rows, cols), BF16), name=name,
        compiler_params=_params(("parallel",)))(w)


def _pack_small(p):
    lb = p["lb_logits"]
    src = dict(p, lb0=lb[0:1], lb1=lb[1:2])
    return jnp.concatenate([jnp.pad(src[nm], ((0, 0), (0, w - src[nm].shape[1]))) for nm, w in SMALL_SEGS], axis=1)


def _unpack_small(vec, shapes):
    so = _offsets(SMALL_SEGS)
    out = {}
    for nm, shp in shapes.items():
        if nm == "lb_logits":
            o = so["lb0"][0]
            out[nm] = vec[0, o:o + 2 * AW].reshape(2, AW)
        else:
            o = so[nm][0]
            out[nm] = vec[:, o:o + shp[1]]
    return out


def kernel(x, c, w_ada, b_ada, norm1_gain, w_in, lb_logits, hgrn_o_gain, q_norm_gain, k_norm_gain, sinks, w_branch_a, w_branch_b, w_out, norm2_gain, w_mlp_in, w_mlp_out, loss_target, m_w_ada, m_b_ada, m_norm1_gain, m_w_in, m_lb_logits, m_hgrn_o_gain, m_q_norm_gain, m_k_norm_gain, m_sinks, m_w_branch_a, m_w_branch_b, m_w_out, m_norm2_gain, m_w_mlp_in, m_w_mlp_out, v_w_ada, v_b_ada, v_norm1_gain, v_w_in, v_lb_logits, v_hgrn_o_gain, v_q_norm_gain, v_k_norm_gain, v_sinks, v_w_branch_a, v_w_branch_b, v_w_out, v_norm2_gain, v_w_mlp_in, v_w_mlp_out):
    w = dict(w_ada=w_ada, b_ada=b_ada, norm1_gain=norm1_gain, w_in=w_in, lb_logits=lb_logits, hgrn_o_gain=hgrn_o_gain,
             q_norm_gain=q_norm_gain, k_norm_gain=k_norm_gain, sinks=sinks, w_branch_a=w_branch_a, w_branch_b=w_branch_b,
             w_out=w_out, norm2_gain=norm2_gain, w_mlp_in=w_mlp_in, w_mlp_out=w_mlp_out)
    m = dict(w_ada=m_w_ada, b_ada=m_b_ada, norm1_gain=m_norm1_gain, w_in=m_w_in, lb_logits=m_lb_logits,
             hgrn_o_gain=m_hgrn_o_gain, q_norm_gain=m_q_norm_gain, k_norm_gain=m_k_norm_gain, sinks=m_sinks,
             w_branch_a=m_w_branch_a, w_branch_b=m_w_branch_b, w_out=m_w_out, norm2_gain=m_norm2_gain,
             w_mlp_in=m_w_mlp_in, w_mlp_out=m_w_mlp_out)
    v = dict(w_ada=v_w_ada, b_ada=v_b_ada, norm1_gain=v_norm1_gain, w_in=v_w_in, lb_logits=v_lb_logits,
             hgrn_o_gain=v_hgrn_o_gain, q_norm_gain=v_q_norm_gain, k_norm_gain=v_k_norm_gain, sinks=v_sinks,
             w_branch_a=v_w_branch_a, w_branch_b=v_w_branch_b, w_out=v_w_out, norm2_gain=v_norm2_gain,
             w_mlp_in=v_w_mlp_in, w_mlp_out=v_w_mlp_out)
    for d in (w, m, v):
        for nm in ("w_ada",) + BIG:
            d[nm] = d[nm][0]
    px, py, pc = _mesh_pos()
    me = _index((px, py, pc))
    c_arr = jnp.reshape(pc, (1,)).astype(jnp.int32)
    q_arr = jnp.reshape(2 * px + py, (1,)).astype(jnp.int32)

    shards = [_to_bf16("shard_" + nm, w[nm].T, False, w[nm].shape[1] // 4) if nm in AS_TRANSPOSE else
              _to_bf16("shard_" + nm, w[nm], nm in COLUMN_SHARDED) for nm in BIG]
    b_shard = lax.dynamic_slice(b_ada, (0, me * ADA_W), (1, ADA_W))
    mod, sc_all = _ada_mod(c, w["w_ada"], b_shard)

    dx, sums, parts = _local_step(x[0], loss_target[0], mod, norm1_gain, norm2_gain, lb_logits, hgrn_o_gain,
                                  q_norm_gain, k_norm_gain, sinks, shards, c_arr)

    allx, g_small, loss = _small_reduce(parts, lb_logits)

    grad, delta, new_m, new_v = {}, {}, {}, {}
    for nm in BIG:
        s, r2 = sums[nm]
        if nm in AS_TRANSPOSE:
            res = _sum_adamw("adamw_" + nm, s, r2, q_arr, w[nm].T, m[nm].T, v[nm].T, False, w[nm].shape[1] // 4)
            grad[nm], delta[nm], new_m[nm], new_v[nm] = (t.T for t in res)
        else:
            grad[nm], delta[nm], new_m[nm], new_v[nm] = _sum_adamw(
                "adamw_" + nm, s, r2, q_arr, w[nm], m[nm], v[nm], nm in COLUMN_SHARDED, 128)

    dmod_cols = lax.dynamic_slice(allx[:, 0, :], (0, me * ADA_W), (N_DEV, ADA_W))
    grad["w_ada"], delta["w_ada"], new_m["w_ada"], new_v["w_ada"] = _ada_update(
        sc_all.T, dmod_cols, w["w_ada"], m["w_ada"], v["w_ada"], 256)

    small_names = [nm for nm in WEIGHTS if nm not in BIG and nm != "w_ada"]
    shapes = {nm: w[nm].shape for nm in small_names}
    ds, ms, vs = _adamw("adamw_small", _pack_small(w), g_small, _pack_small(m), _pack_small(v), 1)
    for dst, vec in ((grad, g_small), (delta, ds), (new_m, ms), (new_v, vs)):
        dst.update(_unpack_small(vec, shapes))

    def full(d, nm):
        return d[nm][None] if nm in BIG or nm == "w_ada" else d[nm]

    return (loss[0, 0], dx[None], *[full(grad, nm) for nm in WEIGHTS], *[full(delta, nm) for nm in WEIGHTS],
            *[full(new_m, nm) for nm in WEIGHTS], *[full(new_v, nm) for nm in WEIGHTS])
```

```python
import functools

import jax
import jax.numpy as jnp
from jax import lax
from jax.experimental import pallas as pl
from jax.experimental.pallas import tpu as pltpu

F32 = jnp.float32
BF16 = jnp.bfloat16
MESH = pl.DeviceIdType.MESH

N_DEV = 8
D = 2048
A_HEADS, A_HD, CHUNK = 8, 128, 64
AW = A_HEADS * A_HD
Q_HEADS, KV_HEADS, GROUP, B_HD, BLK = 16, 4, 4, 64, 128
BW = Q_HEADS * B_HD
KVW = KV_HEADS * B_HD
HID = 4 * D
IN_W = 4 * AW + BW + 2 * KVW + 2 * D
OFF_QA, OFF_FA, OFF_IA, OFF_GA = 0, AW, 2 * AW, 3 * AW
OFF_QB = 4 * AW
OFF_KB = OFF_QB + BW
OFF_VB = OFF_KB + KVW
OFF_GTA = OFF_VB + KVW
OFF_GTB = OFF_GTA + D
N_MOD = 6
EPS = 1e-6
LR, B1, B2, ADAM_EPS, WD, STEP = 1e-3, 0.9, 0.999, 1e-8, 0.01, 10
NEG = -1e30

VMEM_LIMIT = 56 * 1024 * 1024
MI_CUTS = (544, 864)
MO_CUT = 272

NN = (((1,), (0,)), ((), ()))
NT = (((1,), (1,)), ((), ()))
TN = (((0,), (0,)), ((), ()))
BNN = (((2,), (1,)), ((0,), (0,)))
BNT = (((2,), (2,)), ((0,), (0,)))
BTN = (((1,), (1,)), ((0,), (0,)))


def _dot(a, b, dims=NN):
    return lax.dot_general(a.astype(BF16), b.astype(BF16), dims, preferred_element_type=F32)


def _params(sem):
    return pltpu.CompilerParams(dimension_semantics=sem, vmem_limit_bytes=VMEM_LIMIT)


def _sigmoid(x):
    return 1.0 / (1.0 + jnp.exp(-x))


def _fold8(v):
    r, n = v.shape
    return jnp.sum(v.reshape(r // 8, 8, n), axis=0)


_VMEM = pl.BlockSpec(memory_space=pltpu.VMEM)
_ANY = pl.BlockSpec(memory_space=pl.ANY)
_SEMS = lambda n: pltpu.SemaphoreType.DMA((n,))


def _mesh_pos():
    return lax.axis_index("x"), lax.axis_index("y"), lax.axis_index("c")


def _flip(pos, k):
    return tuple(1 - p if (k >> s) & 1 else p for p, s in zip(pos, (2, 1, 0)))


def _index(pos):
    return 4 * pos[0] + 2 * pos[1] + pos[2]


class _Job:
    def __init__(self, ins, out_shape, sems, start, finish, aliases=None, middle=None):
        self.ins, self.out_shape, self.sems, self.start, self.finish = list(ins), list(out_shape), list(sems), start, finish
        self.aliases = dict(aliases or {})
        self.middle = middle


def _both(j1, j2):
    assert not j1.aliases and not j2.aliases
    n_in, n_out, n_sem = len(j1.ins), len(j1.out_shape), len(j1.sems)
    first = lambda ins, outs, sems: (ins[:n_in], outs[:n_out], sems[:n_sem])
    second = lambda ins, outs, sems: (ins[n_in:], outs[n_out:], sems[n_sem:])

    def start(*refs):
        j1.start(*first(*refs))
        j2.start(*second(*refs))

    def finish(*refs):
        j1.finish(*first(*refs))
        j2.finish(*second(*refs))

    return _Job(j1.ins + j2.ins, j1.out_shape + j2.out_shape, j1.sems + j2.sems, start, finish)


def _pcall(body, *, grid, in_specs, out_specs, out_shape, scratch_shapes, name, semantics, args, job=None):
    if job is None:
        outs = pl.pallas_call(body, grid=grid, in_specs=in_specs, out_specs=out_specs, out_shape=out_shape,
                              scratch_shapes=scratch_shapes, name=name, compiler_params=_params(semantics))(*args)
        return list(outs), []
    n_in, n_out, n_scr = len(in_specs), len(out_specs), len(scratch_shapes)
    j_in, j_out = len(job.ins), len(job.out_shape)
    steps = tuple(grid)

    def carrier(*refs):
        o = 0
        main_in, o = refs[o:o + n_in], o + n_in
        job_in, o = refs[o:o + j_in], o + j_in
        main_out, o = refs[o:o + n_out], o + n_out
        job_out, o = refs[o:o + j_out], o + j_out
        main_scr, job_sems = refs[o:o + n_scr], refs[o + n_scr:]
        ids = [pl.program_id(a) for a in range(len(steps))]
        first = functools.reduce(lambda p, q: p & q, [i == 0 for i in ids])
        last = functools.reduce(lambda p, q: p & q, [i == s - 1 for i, s in zip(ids, steps)])

        @pl.when(first)
        def _():
            job.start(job_in, job_out, job_sems)

        if job.middle is not None:
            flat, total = 0, 1
            for i, s in zip(ids, steps):
                flat, total = flat * s + i, total * s

            @pl.when(flat == total // 2)
            def _():
                job.middle(job_in, job_out, job_sems)

        body(*main_in, *main_out, *main_scr)

        @pl.when(last)
        def _():
            job.finish(job_in, job_out, job_sems)

    outs = pl.pallas_call(
        carrier, grid=grid, in_specs=list(in_specs) + [_ANY] * j_in, out_specs=list(out_specs) + [_ANY] * j_out,
        out_shape=list(out_shape) + job.out_shape, scratch_shapes=list(scratch_shapes) + job.sems, name=name,
        input_output_aliases={n_in + i: n_out + o for i, o in job.aliases.items()},
        compiler_params=_params(("arbitrary",) * len(steps)),
    )(*args, *job.ins)
    return list(outs[:n_out]), list(outs[n_out:])


def _run_job(name, job):
    j_in, j_out = len(job.ins), len(job.out_shape)

    def body(*refs):
        ins, outs, sems = refs[:j_in], refs[j_in:j_in + j_out], refs[j_in + j_out:]
        job.start(ins, outs, sems)
        job.finish(ins, outs, sems)

    return list(pl.pallas_call(body, in_specs=[_ANY] * j_in, out_specs=[_ANY] * j_out, out_shape=job.out_shape,
                               scratch_shapes=job.sems, name=name,
                               input_output_aliases=job.aliases)(*job.ins))


def _gather_job(shards, rows=None, into=None):
    n = len(shards)
    rows = rows or [(0, s.shape[0]) for s in shards]
    into = into or [None] * n
    olds, aliases = [], {}
    for a, buf in enumerate(into):
        if buf is not None:
            aliases[n + len(olds)] = a
            olds.append(buf)

    def copies(ins, outs, sems):
        send_sems, recv_sems, local_sems = sems
        x, y, c = _mesh_pos()
        me, sib = (x, y, c), (x, y, 1 - c)
        chips = [(1 - x, y), (x, 1 - y), (1 - x, 1 - y)]

        def part(a, p):
            rs, (r0, r1) = shards[a].shape[0], rows[a]
            return outs[a].at[pl.ds(_index(p) * rs + r0, r1 - r0), :]

        own = lambda a: ins[a].at[pl.ds(rows[a][0], rows[a][1] - rows[a][0]), :]

        def copy(a, k, block, to, src=None):
            return pltpu.make_async_remote_copy(
                src_ref=part(a, block) if src is None else src, dst_ref=part(a, block),
                send_sem=send_sems.at[7 * a + k], recv_sem=recv_sems.at[7 * a + k], device_id=to, device_id_type=MESH)

        mine = [pltpu.make_async_copy(own(a), part(a, me), local_sems.at[a]) for a in range(n)]
        first = []
        for a in range(n):
            first.append(copy(a, 0, me, sib, src=own(a)))
            first += [copy(a, 1 + j, me, (*chip, c), src=own(a)) for j, chip in enumerate(chips)]
        return me, sib, c, chips, copy, mine, first

    def start(ins, outs, sems):
        *_, mine, first = copies(ins, outs, sems)
        for cp in mine + first:
            cp.start()

    def finish(ins, outs, sems):
        me, sib, c, chips, copy, mine, first = copies(ins, outs, sems)
        passed = []
        for j, chip in enumerate(chips):
            for a in range(n):
                copy(a, 1 + j, (*chip, c), me).wait_recv()
                cp = copy(a, 4 + j, (*chip, c), sib)
                cp.start()
                passed.append(cp)
        for a in range(n):
            copy(a, 0, sib, me).wait_recv()
            for j, chip in enumerate(chips):
                copy(a, 4 + j, (*chip, 1 - c), me).wait_recv()
        for cp in first + passed:
            cp.wait_send()
        for cp in mine:
            cp.wait()

    return _Job(list(shards) + olds, [jax.ShapeDtypeStruct((N_DEV * s.shape[0], s.shape[1]), s.dtype) for s in shards],
                [_SEMS(7 * n), _SEMS(7 * n), _SEMS(n)], start, finish, aliases)


def _gather_relay_job(shards, rows=None, into=None, alone=False):
    n = len(shards)
    rows = rows or [(0, s.shape[0]) for s in shards]
    into = into or [None] * n
    olds, aliases = [], {}
    for a, buf in enumerate(into):
        if buf is not None:
            aliases[n + len(olds)] = a
            olds.append(buf)

    def tools(ins, outs, sems):
        send_sems, recv_sems, local_sems = sems
        x, y, c = _mesh_pos()
        q = 2 * x + y
        chip_at = lambda rel: (1 - x if rel & 2 else x, 1 - y if rel & 1 else y)

        def part(a, chip, core):
            rs, (r0, r1) = shards[a].shape[0], rows[a]
            return outs[a].at[pl.ds((2 * chip + core) * rs + r0, r1 - r0), :]

        own = lambda a: ins[a].at[pl.ds(rows[a][0], rows[a][1] - rows[a][0]), :]

        def copy(a, slot, chip, core, to, src=None):
            blk = part(a, chip, core)
            return pltpu.make_async_remote_copy(src_ref=blk if src is None else src, dst_ref=blk,
                                                send_sem=send_sems.at[7 * a + slot], recv_sem=recv_sems.at[7 * a + slot],
                                                device_id=to, device_id_type=MESH)

        mine = [pltpu.make_async_copy(own(a), part(a, q, c), local_sems.at[a]) for a in range(n)]
        first = [copy(a, slot, q, c, (x, y, 1 - c) if slot == 0 else (*chip_at(slot), c), src=own(a))
                 for a in range(n) for slot in (0, 1, 2)]
        return x, y, c, q, chip_at, copy, mine, first

    def start(ins, outs, sems):
        *_, mine, first = tools(ins, outs, sems)
        for cp in mine + first:
            cp.start()

    def middle(ins, outs, sems):
        x, y, c, q, chip_at, copy, _, _ = tools(ins, outs, sems)
        me, sib = (x, y, c), (x, y, 1 - c)

        def relay(src, dst):
            for a in range(n):
                copy(a, src, q ^ src, c, me).wait_recv()
                copy(a, 3, q ^ src, c, (*chip_at(dst), c)).start()
                copy(a, 3 + src, q ^ src, c, sib).start()
            for a in range(n):
                copy(a, dst, q ^ dst, c, me).wait_recv()
                copy(a, 3 + dst, q ^ dst, c, sib).start()

        pl.when(c == 1)(lambda: relay(1, 2))
        pl.when(c == 0)(lambda: relay(2, 1))

    def finish(ins, outs, sems):
        if alone:
            middle(ins, outs, sems)
        x, y, c, q, chip_at, copy, mine, first = tools(ins, outs, sems)
        me, sib = (x, y, c), (x, y, 1 - c)
        for a in range(n):
            copy(a, 3, q ^ 3, c, me).wait_recv()
            copy(a, 6, q ^ 3, c, sib).start()
        for a in range(n):
            copy(a, 0, q, 1 - c, me).wait_recv()
            for rel in (1, 2, 3):
                copy(a, 3 + rel, q ^ rel, 1 - c, me).wait_recv()
        for a in range(n):
            for slot in range(3, 7):
                copy(a, slot, q, c, sib).wait_send()
        for cp in first:
            cp.wait_send()
        for cp in mine:
            cp.wait()

    return _Job(list(shards) + olds, [jax.ShapeDtypeStruct((N_DEV * s.shape[0], s.shape[1]), s.dtype) for s in shards],
                [_SEMS(7 * n), _SEMS(7 * n), _SEMS(n)], start, finish, aliases, middle=None if alone else middle)


def _pair_job(grads):
    n = len(grads)

    def copies(ins, outs, sems):
        send_sems, recv_sems = sems
        x, y, c = _mesh_pos()
        out = []
        for a in range(n):
            rs = grads[a].shape[0] // N_DEV
            for q in range(4):
                blk = ins[a].at[pl.ds((2 * q + 1 - c) * rs, rs), :]
                out.append(pltpu.make_async_remote_copy(
                    src_ref=blk, dst_ref=outs[a].at[q], send_sem=send_sems.at[4 * a + q], recv_sem=recv_sems.at[4 * a + q],
                    device_id=(x, y, 1 - c), device_id_type=MESH))
        return out

    def start(ins, outs, sems):
        for cp in copies(ins, outs, sems):
            cp.start()

    def finish(ins, outs, sems):
        for cp in copies(ins, outs, sems):
            cp.wait()

    return _Job(grads, [jax.ShapeDtypeStruct((4, g.shape[0] // N_DEV, g.shape[1]), g.dtype) for g in grads],
                [_SEMS(4 * n), _SEMS(4 * n)], start, finish)


def _chip_job(sums, rels=(1, 2, 3)):
    n, nr = len(sums), len(rels)

    def copies(ins, outs, sems):
        send_sems, recv_sems = sems
        x, y, c = _mesh_pos()
        out = []
        for a in range(n):
            for slot, r in enumerate(rels):
                px, py = (1 - x if r & 2 else x), (1 - y if r & 1 else y)
                out.append(pltpu.make_async_remote_copy(
                    src_ref=ins[a].at[2 * px + py], dst_ref=outs[a].at[slot], send_sem=send_sems.at[nr * a + slot],
                    recv_sem=recv_sems.at[nr * a + slot], device_id=(px, py, c), device_id_type=MESH))
        return out

    def start(ins, outs, sems):
        for cp in copies(ins, outs, sems):
            cp.start()

    def finish(ins, outs, sems):
        for cp in copies(ins, outs, sems):
            cp.wait()

    return _Job(sums, [jax.ShapeDtypeStruct((nr,) + s.shape[1:], s.dtype) for s in sums],
                [_SEMS(nr * n), _SEMS(nr * n)], start, finish)


def _mm(name, form, a_list, b, M, N, K, tm, tn, tk, extras, outs, epi, job=None):
    nI, nJ, nK = M // tm, N // tn, K // tk
    assert nI * tm == M and nJ * tn == N and nK * tk == K
    dims = {"nn": NN, "nt": NT, "tn": TN}[form]
    b_list = b if isinstance(b, list) else [(b, {"nn": N, "nt": K, "tn": N}[form])]
    nA, nB = len(a_list), len(b_list)
    assert nA == 1 or nB == 1
    assert nB == 1 or form in ("nn", "nt")
    AXIS = {"i": 0, "j": 1, "k": 2}
    a_axis, a_tile = ("i", tm) if form == "tn" else ("k", tk)
    b_axis, b_tile = ("k", tk) if form == "nt" else ("j", tn)

    def cut(pieces, tile, total):
        starts, s = [], 0
        for _, w in pieces:
            assert w % tile == 0
            starts.append(s // tile)
            s += w
        assert s == total
        return starts, [w // tile for _, w in pieces]

    a_st, a_cn = cut(a_list, a_tile, M if form == "tn" else K)
    b_st, b_cn = cut(b_list, b_tile, K if form == "nt" else N)

    def inside(idx, st, cn):
        return (idx >= st) & (idx < st + cn)

    def a_spec(p):
        st, cn = a_st[p], a_cn[p]
        if form == "tn":
            return pl.BlockSpec((tk, tm), lambda i, j, k: (jnp.where(inside(i, st, cn), k, 0), jnp.clip(i - st, 0, cn - 1)))
        return pl.BlockSpec((tm, tk), lambda i, j, k: (i, jnp.clip(k - st, 0, cn - 1)))

    def b_spec(p):
        st, cn = b_st[p], b_cn[p]
        if form == "nt":
            return pl.BlockSpec((tn, tk), lambda i, j, k: (j, jnp.clip(k - st, 0, cn - 1)))
        if nB == 1:
            return pl.BlockSpec((tk, tn), lambda i, j, k: (k, j))
        return pl.BlockSpec((tk, tn), lambda i, j, k: (jnp.where(inside(j, st, cn), k, 0), jnp.clip(j - st, 0, cn - 1)))

    in_specs = ([a_spec(p) for p in range(nA)] + [b_spec(p) for p in range(nB)]
                + [pl.BlockSpec(bs, im) for _, bs, im in extras])
    out_shape = [jax.ShapeDtypeStruct(s_, d_) for s_, d_, _, _ in outs]
    out_specs = [pl.BlockSpec(bs, im) for _, _, bs, im in outs]
    nE, nO = len(extras), len(outs)
    single = nA == 1 and nB == 1

    def body(*refs):
        a_refs, b_refs = refs[:nA], refs[nA:nA + nB]
        ex, ou = refs[nA + nB:nA + nB + nE], refs[nA + nB + nE:nA + nB + nE + nO]
        ids = [pl.program_id(a) for a in range(3)]

        def partial_of(p, q):
            return lax.dot_general(a_refs[p][...], b_refs[q][...], dims, preferred_element_type=F32)

        if nK == 1 and single:
            epi(partial_of(0, 0), ex, ou)
            return
        acc = refs[-1]
        k = ids[2]
        for p in range(nA):
            for q in range(nB):
                def first(p=p, q=q):
                    acc[...] = partial_of(p, q)

                def later(p=p, q=q):
                    acc[...] += partial_of(p, q)

                here = None
                if nA > 1:
                    here = inside(ids[AXIS[a_axis]], a_st[p], a_cn[p])
                if nB > 1:
                    here = inside(ids[AXIS[b_axis]], b_st[q], b_cn[q])
                pl.when(k == 0 if here is None else here & (k == 0))(first)
                pl.when(k > 0 if here is None else here & (k > 0))(later)

        @pl.when(k == nK - 1)
        def _():
            epi(acc[...], ex, ou)

    scratch = [] if (nK == 1 and single) else [pltpu.VMEM((tm, tn), F32)]
    res, job_res = _pcall(
        body, grid=(nI, nJ, nK), in_specs=in_specs, out_specs=out_specs, out_shape=out_shape, scratch_shapes=scratch,
        name=name, semantics=("parallel", "parallel", "arbitrary"),
        args=[a for a, _ in a_list] + [p for p, _ in b_list] + [e for e, _, _ in extras], job=job)
    return res if job is None else (res, job_res)


def _twin_mm(name, form, pairs, M, N, K, tm, tn, tk, out_dtype):
    nI, nJ, nK = M // tm, N // tn, K // tk
    dims = {"nn": NN, "tn": TN}[form]
    a_spec = (pl.BlockSpec((tm, tk), lambda i, j, k: (i, k)) if form == "nn" else pl.BlockSpec((tk, tm), lambda i, j, k: (k, i)))
    b_spec = pl.BlockSpec((tk, tn), lambda i, j, k: (k, j))
    o_spec = pl.BlockSpec((tm, tn), lambda i, j, k: (i, j))

    def body(a1, b1, a2, b2, o1, o2, *accs):
        k = pl.program_id(2)
        for a_ref, b_ref, o_ref, acc in ((a1, b1, o1, accs[0] if accs else None), (a2, b2, o2, accs[1] if accs else None)):
            part = lax.dot_general(a_ref[...], b_ref[...], dims, preferred_element_type=F32)
            if nK == 1:
                o_ref[...] = part.astype(out_dtype)
                continue

            @pl.when(k == 0)
            def _(acc=acc, part=part):
                acc[...] = part

            @pl.when(k > 0)
            def _(acc=acc, part=part):
                acc[...] += part

            @pl.when(k == nK - 1)
            def _(acc=acc, o_ref=o_ref):
                o_ref[...] = acc[...].astype(out_dtype)

    (a1, b1), (a2, b2) = pairs
    shape = jax.ShapeDtypeStruct((M, N), out_dtype)
    return pl.pallas_call(
        body, grid=(nI, nJ, nK), in_specs=[a_spec, b_spec, a_spec, b_spec], out_specs=[o_spec, o_spec],
        out_shape=[shape, shape], scratch_shapes=[] if nK == 1 else [pltpu.VMEM((tm, tn), F32)] * 2, name=name,
        compiler_params=_params(("parallel", "parallel", "arbitrary")))(a1, b1, a2, b2)


def _piece_tiles(pieces, tile):
    starts, s = [], 0
    for _, w in pieces:
        assert w % tile == 0
        starts.append(s // tile)
        s += w
    return starts, [w // tile for _, w in pieces], s


def _pieces_tn(name, pieces, b, tile, job=None):
    T, N = b.shape
    st, cn, M = _piece_tiles(pieces, tile)
    nP, nI = len(pieces), M // tile

    def body(*refs):
        p_refs, b_hbm, o_ref = refs[:nP], refs[nP], refs[nP + 1]
        bbuf, abuf, bsem, asem = refs[nP + 2:]
        i = pl.program_id(0)

        def fetch(step, slot):
            for p in range(nP):
                @pl.when((step >= st[p]) & (step < st[p] + cn[p]))
                def _():
                    col = pl.multiple_of((step - st[p]) * tile, tile)
                    pltpu.make_async_copy(p_refs[p].at[pl.ds(0, T), pl.ds(col, tile)], abuf.at[slot], asem.at[slot]).start()

        @pl.when(i == 0)
        def _():
            whole = pltpu.make_async_copy(b_hbm, bbuf, bsem)
            whole.start()
            fetch(0, 0)
            whole.wait()

        @pl.when(i + 1 < nI)
        def _():
            fetch(i + 1, (i + 1) % 2)

        pltpu.make_async_copy(p_refs[0].at[pl.ds(0, T), pl.ds(0, tile)], abuf.at[i % 2], asem.at[i % 2]).wait()
        o_ref[...] = lax.dot_general(abuf[i % 2], bbuf[...], TN, preferred_element_type=F32).astype(BF16)

    res, job_res = _pcall(
        body, grid=(nI,), in_specs=[_ANY] * (nP + 1), out_specs=[pl.BlockSpec((tile, N), lambda i: (i, 0))],
        out_shape=[jax.ShapeDtypeStruct((M, N), BF16)],
        scratch_shapes=[pltpu.VMEM((T, N), b.dtype), pltpu.VMEM((2, T, tile), b.dtype), pltpu.SemaphoreType.DMA, _SEMS(2)],
        name=name, semantics=("arbitrary",), args=[p for p, _ in pieces] + [b], job=job)
    return res if job is None else (res, job_res)


def _rows_mm(name, pieces, w, T, tm, tk, vecs, bufs, parts, epi, job=None):
    st, cn, K = _piece_tiles(pieces, tk)
    nP, nI, nK = len(pieces), T // tm, K // tk
    part_specs = [pl.BlockSpec(bs, lambda i, k, im=im: im(i, 0, k)) for _, _, bs, im in parts]
    n_vec, nB = len(vecs), len(bufs)
    load_ix = [n for n, (_, src, _) in enumerate(bufs) if src is not None]
    store_ix = [n for n, (_, _, store) in enumerate(bufs) if store]
    n_any_in, n_any_out = len(load_ix), len(store_ix)

    def body(*refs):
        o = nP
        p_refs, w_ref = refs[:nP], refs[o]
        vec_refs = refs[o + 1:o + 1 + n_vec]
        ins = refs[o + 1 + n_vec:o + 1 + n_vec + n_any_in]
        o = o + 1 + n_vec + n_any_in
        hbm_outs, p_outs = refs[o:o + n_any_out], refs[o + n_any_out:o + n_any_out + len(parts)]
        o = o + n_any_out + len(parts)
        acc, abuf = refs[o:o + 2]
        buf_refs = refs[o + 2:o + 2 + nB]
        asem, in_sems, out_sems = refs[-3:]
        i, k = pl.program_id(0), pl.program_id(1)
        g = i * nK + k
        rows_of = lambda ref, ii: ref.at[pl.ds(pl.multiple_of(ii * tm, tm), tm), :]
        bufs_in = [buf_refs[n] for n in load_ix]
        bufs_out = [buf_refs[n] for n in store_ix]

        def fetch(ii, kk, slot):
            for p in range(nP):
                @pl.when((kk >= st[p]) & (kk < st[p] + cn[p]))
                def _():
                    col = pl.multiple_of((kk - st[p]) * tk, tk)
                    src = p_refs[p].at[pl.ds(pl.multiple_of(ii * tm, tm), tm), pl.ds(col, tk)]
                    pltpu.make_async_copy(src, abuf.at[slot], asem.at[slot]).start()

        loads = lambda ii: [pltpu.make_async_copy(rows_of(src, ii), buf, in_sems.at[n])
                            for n, (src, buf) in enumerate(zip(ins, bufs_in))]
        stores = lambda ii: [pltpu.make_async_copy(buf, rows_of(dst, ii), out_sems.at[n])
                             for n, (buf, dst) in enumerate(zip(bufs_out, hbm_outs))]

        @pl.when(g == 0)
        def _():
            fetch(0, 0, 0)

        @pl.when(g + 1 < nI * nK)
        def _():
            last_k = k == nK - 1
            fetch(jnp.where(last_k, i + 1, i), jnp.where(last_k, 0, k + 1), (g + 1) % 2)

        @pl.when(k == 0)
        def _():
            @pl.when(i > 0)
            def _():
                for cp in stores(i - 1):
                    cp.wait()
            for cp in loads(i):
                cp.start()

        pltpu.make_async_copy(p_refs[0].at[pl.ds(0, tm), pl.ds(0, tk)], abuf.at[g % 2], asem.at[g % 2]).wait()

        def product(cols):
            return jnp.dot(abuf[g % 2], w_ref[:, cols], preferred_element_type=F32)

        col_blocks = [slice(c0, c0 + 512) for c0 in range(0, D, 512)]

        @pl.when(k == 0)
        def _():
            for cols in col_blocks:
                acc[:, cols] = product(cols)

        @pl.when(k > 0)
        def _():
            for cols in col_blocks:
                acc[:, cols] += product(cols)

        @pl.when(k == nK - 1)
        def _():
            for cp in loads(i):
                cp.wait()
            epi(acc, vec_refs, buf_refs, p_outs)
            for cp in stores(i):
                cp.start()

            @pl.when(i == nI - 1)
            def _():
                for cp in stores(i):
                    cp.wait()

    vec = pl.BlockSpec((1, D), lambda i, k: (0, 0))
    scratch = ([pltpu.VMEM((tm, D), F32), pltpu.VMEM((2, tm, tk), BF16)] + [pltpu.VMEM((tm, D), dt) for dt, _, _ in bufs]
               + [_SEMS(2), _SEMS(n_any_in), _SEMS(n_any_out)])
    res, job_res = _pcall(
        body, grid=(nI, nK),
        in_specs=[_ANY] * nP + [pl.BlockSpec((tk, D), lambda i, k: (k, 0))] + [vec] * n_vec + [_ANY] * n_any_in,
        out_specs=[_ANY] * n_any_out + part_specs,
        out_shape=([jax.ShapeDtypeStruct((T, D), bufs[n][0]) for n in store_ix]
                   + [jax.ShapeDtypeStruct(s, d) for s, d, _, _ in parts]),
        scratch_shapes=scratch, name=name, semantics=("arbitrary", "arbitrary"),
        args=[p for p, _ in pieces] + [w] + list(vecs) + [bufs[n][1] for n in load_ix], job=job)
    return res if job is None else (res, job_res)


def _pieces_nn_rms(name, pieces, w, x, gain, sc, dres, tm, tk, job=None):
    _, outs, epi = _rms_mod_bwd_epilogue(x, gain, sc, dres, tm)

    def on_rows(acc, vecs, bufs, parts):
        epi(acc, [bufs[0], vecs[0], vecs[1], bufs[1]], [bufs[1], *parts])

    return _rows_mm(name, pieces, w, x.shape[0], tm, tk, [gain, sc], [(F32, x, False), (F32, dres, True)],
                    outs[1:], on_rows, job=job)


def _rms_mod_fwd(name, x, gain, sc, sh, tr):
    T = x.shape[0]

    def body(x_ref, g_ref, sc_ref, sh_ref, h_ref):
        xv = x_ref[...]
        rstd = lax.rsqrt(jnp.mean(xv * xv, axis=-1, keepdims=True) + EPS)
        h_ref[...] = ((xv * rstd * g_ref[...]) * (1.0 + sc_ref[...]) + sh_ref[...]).astype(BF16)

    row = pl.BlockSpec((tr, D), lambda i: (i, 0))
    vec = pl.BlockSpec((1, D), lambda i: (0, 0))
    return pl.pallas_call(
        body, grid=(T // tr,), in_specs=[row, vec, vec, vec], out_specs=row,
        out_shape=jax.ShapeDtypeStruct((T, D), BF16), name=name, compiler_params=_params(("parallel",)),
    )(x, gain, sc, sh)


def _rms_mod_bwd_epilogue(x, gain, sc, dres, tm, gate=None, mo=None):
    T = x.shape[0]
    with_gate = gate is not None
    row = ((tm, D), lambda i, j, k: (i, 0))
    vec = ((1, D), lambda i, j, k: (0, 0))
    part = ((T // tm * 8, D), F32, (8, D), lambda i, j, k: (i, 0))
    extras = [(x, *row), (gain, *vec), (sc, *vec), (dres, *row)]
    outs = [((T, D), F32, *row), part, part, part]
    if with_gate:
        extras += [(gate, *vec), (mo, *row)]
        outs += [((T, D), BF16, *row), part]

    rows = min(64, tm)

    def epi(acc, ex, ou):
        g = ex[1][...]
        sums = [jnp.zeros((8, D), F32) for _ in range(4)]
        for r0 in range(0, tm, rows):
            rs = slice(r0, r0 + rows)
            dhv, xv = acc[rs, :], ex[0][rs, :]
            rstd = lax.rsqrt(jnp.mean(xv * xv, axis=-1, keepdims=True) + EPS)
            xhat = xv * rstd
            dn = dhv * (1.0 + ex[2][...])
            dxhat = dn * g
            dx = ex[3][rs, :] + rstd * (dxhat - xhat * jnp.mean(dxhat * xhat, axis=-1, keepdims=True))
            ou[0][rs, :] = dx
            terms = [dhv, dhv * (xhat * g), dn * xhat]
            if with_gate:
                terms.append(dx * ex[5][rs, :].astype(F32))
                ou[4][rs, :] = (ex[4][...] * dx).astype(BF16)
            sums = [s + _fold8(t) for s, t in zip(sums, terms)] + sums[len(terms):]
        ou[1][...], ou[2][...], ou[3][...] = sums[:3]
        if with_gate:
            ou[5][...] = sums[3]

    return extras, outs, epi


def _rms_mod_bwd(name, dh, x, gain, sc, dres, tr, gate=None, mo=None):
    T = x.shape[0]
    extras, outs, epi = _rms_mod_bwd_epilogue(x, gain, sc, dres, tr, gate, mo)
    rows_only = lambda im: (lambda i: im(i, 0, 0))
    nE = len(extras)

    def body(dh_ref, *refs):
        epi(dh_ref, refs[:nE], refs[nE:])

    return pl.pallas_call(
        body, grid=(T // tr,),
        in_specs=[pl.BlockSpec((tr, D), lambda i: (i, 0))] + [pl.BlockSpec(bs, rows_only(im)) for _, bs, im in extras],
        out_specs=[pl.BlockSpec(bs, rows_only(im)) for _, _, bs, im in outs],
        out_shape=[jax.ShapeDtypeStruct(s, d) for s, d, _, _ in outs], name=name, compiler_params=_params(("parallel",)),
    )(dh, *[e for e, _, _ in extras])


def _split3(v):
    h = v.astype(BF16)
    r1 = v - h.astype(F32)
    m = r1.astype(BF16)
    lo = (r1 - m.astype(F32)).astype(BF16)
    return h, m, lo


def _tri_mm(tri, v, dims=NN):
    h, m, lo = _split3(v)
    t = tri.astype(BF16)
    mm = lambda p: lax.dot_general(t, p, dims, preferred_element_type=F32)
    return (mm(lo) + mm(m)) + mm(h)


def _hgrn_chunk_terms(q, fl, lb):
    sig = _sigmoid(fl)
    f = lb + (1.0 - lb) * sig
    lf = jnp.log(f)
    kk = 1.0 - f
    sq = _sigmoid(q)
    qf = q * sq
    return sig, f, lf, kk, sq, qf


def _causal(n):
    r = lax.broadcasted_iota(jnp.int32, (n, n), 0)
    c = lax.broadcasted_iota(jnp.int32, (n, n), 1)
    return r >= c


def _hgrn_fwd(proj, lb_logits, o_gain, tt, job=None):
    T = proj.shape[0]
    nT, ncl = T // tt, tt // CHUNK
    C = CHUNK

    def body(q_ref, f_ref, i_ref, g_ref, lbl_ref, og_ref, y_ref, st_ref, S):
        @pl.when(pl.program_id(1) == 0)
        def _():
            S[...] = jnp.zeros_like(S)

        lbl = lbl_ref[...]
        lb = _sigmoid(lbl[0:1, :] - lbl[1:2, :])
        og = og_ref[...]
        shp = (ncl, C, A_HD)
        q, fl, v, g = (r[...].reshape(shp) for r in (q_ref, f_ref, i_ref, g_ref))
        tri = jnp.broadcast_to(_causal(C), (ncl, C, C))
        _, _, lf, kk, _, qf = _hgrn_chunk_terms(q, fl, lb)
        b = _tri_mm(tri, lf, BNN)
        bm, bl = b[:, C // 2 - 1:C // 2, :], b[:, C - 1:C, :]
        qd, kd = qf * jnp.exp(b - bm), kk * jnp.exp(bm - b)
        A = jnp.where(tri, _dot(qd, kd, BNT), 0.0)
        d_st = _dot(v, kk * jnp.exp(bl - b), BTN)
        dec = jnp.exp(bl)
        st = S[...]
        for ci in range(ncl):
            st_ref[0, ci] = st
            st = st * dec[ci] + d_st[ci]
        S[...] = st
        o = _dot(A, v, BNN) + _dot(qf * jnp.exp(b), st_ref[0], BNT)
        r = lax.rsqrt(jnp.mean(o * o, axis=-1, keepdims=True) + EPS)
        y_ref[...] = (o * r * og * (g * _sigmoid(g))).astype(BF16).reshape(tt, A_HD)

    def col(off):
        return pl.BlockSpec((tt, A_HD), lambda h, t: (t, off // A_HD + h))

    head_vec = lambda rows: pl.BlockSpec((rows, A_HD), lambda h, t: (0, h))
    return _pcall(
        body, grid=(A_HEADS, nT),
        in_specs=[col(OFF_QA), col(OFF_FA), col(OFF_IA), col(OFF_GA), head_vec(2), head_vec(1)],
        out_specs=[pl.BlockSpec((tt, A_HD), lambda h, t: (t, h)),
                   pl.BlockSpec((1, ncl, A_HD, A_HD), lambda h, t: (h, t, 0, 0))],
        out_shape=[jax.ShapeDtypeStruct((T, AW), BF16),
                   jax.ShapeDtypeStruct((A_HEADS, T // C, A_HD, A_HD), F32)],
        scratch_shapes=[pltpu.VMEM((A_HD, A_HD), F32)], name="hgrn_fwd", semantics=("parallel", "arbitrary"),
        args=[proj, proj, proj, proj, lb_logits, o_gain], job=job)


def _hgrn_bwd(proj, st, dy, lb_logits, o_gain, tt, job=None):
    T = proj.shape[0]
    nT, ncl = T // tt, tt // CHUNK
    C = CHUNK

    def body(q_ref, f_ref, i_ref, g_ref, st_ref, dy_ref, lbl_ref, og_ref,
             dq_ref, df_ref, di_ref, dg_ref, plb_ref, pog_ref, dS):
        @pl.when(pl.program_id(1) == 0)
        def _():
            dS[...] = jnp.zeros_like(dS)

        lbl = lbl_ref[...]
        lb = _sigmoid(lbl[0:1, :] - lbl[1:2, :])
        og = og_ref[...]
        shp = (ncl, C, A_HD)
        flat = lambda t: t.reshape(tt, A_HD)
        q, fl, v, g, dout = (r[...].reshape(shp) for r in (q_ref, f_ref, i_ref, g_ref, dy_ref))
        tri = jnp.broadcast_to(_causal(C), (ncl, C, C))
        rowi = lax.broadcasted_iota(jnp.int32, shp, 1)
        st0 = st_ref[0]
        sig, f, lf, kk, sq, qf = _hgrn_chunk_terms(q, fl, lb)
        b = _tri_mm(tri, lf, BNN)
        bm, bl = b[:, C // 2 - 1:C // 2, :], b[:, C - 1:C, :]
        e_qd, e_kd, e_ke, e_b = jnp.exp(b - bm), jnp.exp(bm - b), jnp.exp(bl - b), jnp.exp(b)
        qd, kd, ke, qe = qf * e_qd, kk * e_kd, kk * e_ke, qf * e_b
        dec = jnp.exp(bl)
        A = jnp.where(tri, _dot(qd, kd, BNT), 0.0)
        o = _dot(A, v, BNN) + _dot(qe, st0, BNT)
        r = lax.rsqrt(jnp.mean(o * o, axis=-1, keepdims=True) + EPS)
        sg = _sigmoid(g)
        on = o * r * og
        dg_ref[...] = flat((dout * on * (sg * (1.0 + g * (1.0 - sg)))).astype(BF16))
        don = dout * (g * sg)
        pog_ref[...] = _fold8(flat(don * o * r))
        dyh = don * og
        do = r * (dyh - o * (r * r) * jnp.mean(dyh * o, axis=-1, keepdims=True))
        g_st = _dot(do, qe, BTN)
        run = dS[...]
        after = [None] * ncl
        for ci in reversed(range(ncl)):
            after[ci] = run
            run = g_st[ci] + run * dec[ci]
        dS[...] = run
        d_after = jnp.stack(after, axis=0)
        ddec = jnp.sum(d_after * st0, axis=1, keepdims=True)
        dqe = _dot(do, st0, BNN)
        dke = _dot(v, d_after, BNN)
        dA = jnp.where(tri, _dot(do, v, BNT), 0.0)
        dv = _dot(ke, d_after, BNT) + _dot(A, do, BTN)
        dqd = _dot(dA, kd, BNN)
        dkd = _dot(dA, qd, BTN)
        di_ref[...] = flat(dv.astype(BF16))
        dqf = dqe * e_b + dqd * e_qd
        dkk = dkd * e_kd + dke * e_ke
        t_qd, t_kd, t_ke = dqd * qd, dkd * kd, dke * ke
        db = dqe * qe + t_qd - t_kd - t_ke
        dbm = jnp.sum(t_kd - t_qd, axis=1, keepdims=True)
        dbl = jnp.sum(t_ke, axis=1, keepdims=True) + ddec * dec
        db = db + jnp.where(rowi == C // 2 - 1, dbm, 0.0) + jnp.where(rowi == C - 1, dbl, 0.0)
        dlf = _tri_mm(tri, db, BTN)
        dfv = dlf / f - dkk
        df_ref[...] = flat((dfv * (1.0 - lb) * sig * (1.0 - sig)).astype(BF16))
        plb_ref[...] = _fold8(flat(dfv * (1.0 - sig)))
        dq_ref[...] = flat((dqf * (sq * (1.0 + q * (1.0 - sq)))).astype(BF16))

    def col(off):
        return pl.BlockSpec((tt, A_HD), lambda h, t: (nT - 1 - t, off // A_HD + h))

    head_vec = lambda rows: pl.BlockSpec((rows, A_HD), lambda h, t: (0, h))
    o_spec = pl.BlockSpec((tt, A_HD), lambda h, t: (nT - 1 - t, h))
    p_spec = pl.BlockSpec((8, A_HD), lambda h, t: (t, h))
    o_shape = jax.ShapeDtypeStruct((T, AW), BF16)
    p_shape = jax.ShapeDtypeStruct((nT * 8, AW), F32)
    return _pcall(
        body, grid=(A_HEADS, nT),
        in_specs=[col(OFF_QA), col(OFF_FA), col(OFF_IA), col(OFF_GA),
                  pl.BlockSpec((1, ncl, A_HD, A_HD), lambda h, t: (h, nT - 1 - t, 0, 0)),
                  pl.BlockSpec((tt, A_HD), lambda h, t: (nT - 1 - t, h)), head_vec(2), head_vec(1)],
        out_specs=[o_spec, o_spec, o_spec, o_spec, p_spec, p_spec],
        out_shape=[o_shape, o_shape, o_shape, o_shape, p_shape, p_shape],
        scratch_shapes=[pltpu.VMEM((A_HD, A_HD), F32)], name="hgrn_bwd", semantics=("parallel", "arbitrary"),
        args=[proj, proj, proj, proj, st, dy, lb_logits, o_gain], job=job)


LANES = 128
Q_COLS = BW // LANES


def _low_half():
    return lax.broadcasted_iota(jnp.int32, (1, LANES), 1) < B_HD


def _half_sum(t, low):
    lo = jnp.sum(jnp.where(low, t, 0.0), axis=-1, keepdims=True)
    hi = jnp.sum(jnp.where(low, 0.0, t), axis=-1, keepdims=True)
    return jnp.where(low, lo, hi)


def _half_rms(t, low):
    r = lax.rsqrt(_half_sum(t * t, low) * (1.0 / B_HD) + EPS)
    return t * r, r


def _fold_halves(p, low):
    return jnp.where(low, p + pltpu.roll(p, B_HD, 1), 0.0)


def _stack_cols(x):
    return jnp.stack([x[:, c * LANES:(c + 1) * LANES] for c in range(Q_COLS)], axis=0).reshape(KV_HEADS, 2 * BLK, LANES)


def _col_of(t, c):
    return t[c // 2, (c % 2) * BLK:(c % 2 + 1) * BLK]


def _split_halves(col, s, low):
    own = jnp.where(low if s == 0 else jnp.logical_not(low), col, 0.0)
    other = pltpu.roll(own, B_HD, 1)
    return (own, other) if s == 0 else (other, own)


def _swa_keys(kp_ref, kc_ref, vp_ref, vc_ref, kg, low):
    k_lo, k_hi, v_lo, v_hi, hats = [], [], [], [], []
    for j in range(KVW // LANES):
        cs = slice(j * LANES, (j + 1) * LANES)
        k_hat, k_r = _half_rms(jnp.concatenate([kp_ref[:, cs], kc_ref[:, cs]], axis=0), low)
        vcol = jnp.concatenate([vp_ref[:, cs], vc_ref[:, cs]], axis=0)
        hats.append((k_hat, k_r))
        for s in range(2):
            for dst_lo, dst_hi, col in ((k_lo, k_hi, k_hat * kg), (v_lo, v_hi, vcol)):
                lo, hi = _split_halves(col, s, low)
                dst_lo.append(lo)
                dst_hi.append(hi)
    st = lambda parts: jnp.stack(parts, axis=0)
    return st(k_lo), st(k_hi), st(v_lo), st(v_hi), hats


def _swa_mask(first_block):
    qi = lax.broadcasted_iota(jnp.int32, (BLK, 2 * BLK), 0) + BLK
    ki = lax.broadcasted_iota(jnp.int32, (BLK, 2 * BLK), 1)
    rel = qi - ki
    m = (rel >= 0) & (rel < BLK) & (jnp.logical_not(first_block) | (ki >= BLK))
    return jnp.concatenate([m, m], axis=0)


def _sink_cols(sk_ref, hi):
    top = lax.broadcasted_iota(jnp.int32, (2 * BLK, 1), 0) < BLK
    return jnp.stack([jnp.where(top, sk_ref[0, GROUP * hk + hi], sk_ref[0, GROUP * hk + 2 + hi])
                      for hk in range(KV_HEADS)], axis=0)


def _swa_probs(qn, k_half, sink, mask):
    s = jnp.where(mask, _dot(qn, k_half, BNT) * (B_HD ** -0.5), NEG)
    m = jnp.maximum(jnp.max(s, axis=-1, keepdims=True), sink)
    p = jnp.exp(s - m)
    ps = jnp.exp(sink - m)
    inv = 1.0 / (jnp.sum(p, axis=-1, keepdims=True) + ps)
    return p * inv, ps * inv


def _swa_fwd(proj, q_gain, k_gain, sinks, job=None):
    T = proj.shape[0]
    nb = T // BLK

    def body(q_ref, kc_ref, kp_ref, vc_ref, vp_ref, qg_ref, kg_ref, sk_ref, o_ref):
        low = _low_half()
        mask = _swa_mask(pl.program_id(0) == 0)
        qn = _half_rms(_stack_cols(q_ref[...]), low)[0] * qg_ref[...]
        k_lo, k_hi, v_lo, v_hi, _ = _swa_keys(kp_ref, kc_ref, vp_ref, vc_ref, kg_ref[...], low)
        p_lo, _ = _swa_probs(qn, k_lo, _sink_cols(sk_ref, 0), mask)
        p_hi, _ = _swa_probs(qn, k_hi, _sink_cols(sk_ref, 1), mask)
        o = (_dot(p_lo, v_lo, BNN) + _dot(p_hi, v_hi, BNN)).astype(BF16)
        for c in range(Q_COLS):
            o_ref[:, c * LANES:(c + 1) * LANES] = _col_of(o, c)

    q_gain, k_gain = jnp.tile(q_gain, (1, 2)), jnp.tile(k_gain, (1, 2))
    cur = lambda w, off: pl.BlockSpec((BLK, w), lambda i: (i, off // w))
    prev = lambda w, off: pl.BlockSpec((BLK, w), lambda i: (jnp.maximum(i - 1, 0), off // w))
    small = lambda n: pl.BlockSpec((1, 2 * n), lambda i: (0, 0))
    return _pcall(
        body, grid=(nb,),
        in_specs=[cur(BW, OFF_QB), cur(KVW, OFF_KB), prev(KVW, OFF_KB), cur(KVW, OFF_VB), prev(KVW, OFF_VB),
                  small(B_HD), small(B_HD), pl.BlockSpec(memory_space=pltpu.SMEM)],
        out_specs=[pl.BlockSpec((BLK, BW), lambda i: (i, 0))],
        out_shape=[jax.ShapeDtypeStruct((T, BW), BF16)], scratch_shapes=[], name="swa_fwd", semantics=("parallel",),
        args=[proj, proj, proj, proj, proj, q_gain, k_gain, sinks], job=job)


def _swa_bwd(proj, dout, q_gain, k_gain, sinks, job=None):
    T = proj.shape[0]
    nb = T // BLK
    W = BW + 2 * KVW

    def body(q_ref, kc_ref, kp_ref, vc_ref, vp_ref, do_ref, qg_ref, kg_ref, sk_ref,
             dq_ref, dkv_ref, pqg_ref, pkg_ref, psk_ref, dkn_c, dv_c):
        i = pl.program_id(0)
        live = i < nb
        low = _low_half()
        high = jnp.logical_not(low)
        qg, kg = qg_ref[...], kg_ref[...]
        mask = _swa_mask(i == 0)
        lane = lax.broadcasted_iota(jnp.int32, (1, LANES), 1)
        scale = B_HD ** -0.5

        @pl.when(i == 0)
        def _():
            dkn_c[...] = jnp.zeros_like(dkn_c)
            dv_c[...] = jnp.zeros_like(dv_c)

        q_hat, q_r = _half_rms(_stack_cols(q_ref[...]), low)
        qn = q_hat * qg
        k_lo, k_hi, v_lo, v_hi, hats = _swa_keys(kp_ref, kc_ref, vp_ref, vc_ref, kg, low)
        do = _stack_cols(do_ref[...])
        dqn = jnp.zeros((KV_HEADS, 2 * BLK, LANES), F32)
        acc_sk = jnp.zeros((1, LANES), F32)
        dk_parts, dv_parts = [], []
        for hi, (k_h, v_h) in enumerate(((k_lo, v_lo), (k_hi, v_hi))):
            p, ps = _swa_probs(qn, k_h, _sink_cols(sk_ref, hi), mask)
            dp = _dot(do, v_h, BNT)
            delta = jnp.sum(p * dp, axis=-1, keepdims=True)
            ds = p * (dp - delta) * scale
            dqn = dqn + _dot(ds, k_h, BNN)
            dk_parts.append(_dot(ds, qn, BTN))
            dv_parts.append(_dot(p, do, BTN))
            t = ps * delta
            for hk in range(KV_HEADS):
                for rows in range(2):
                    h = GROUP * hk + 2 * rows + hi
                    acc_sk = acc_sk + jnp.where(
                        lane == h, -jnp.sum(t[hk, rows * BLK:(rows + 1) * BLK], axis=0, keepdims=True), 0.0)
        dqh = dqn * qg
        dq = (q_r * (dqh - q_hat * (_half_sum(dqh * q_hat, low) * (1.0 / B_HD)))).astype(BF16)
        for c in range(Q_COLS):
            dq_ref[:, c * LANES:(c + 1) * LANES] = _col_of(dq, c)
        acc_qg = _fold_halves(_fold8((dqn * q_hat).reshape(KV_HEADS * 2 * BLK, LANES)), low)

        def native(parts, j):
            lo_arr, hi_arr = parts
            a, b = 2 * j, 2 * j + 1
            return (jnp.where(low, lo_arr[a], 0.0) + pltpu.roll(jnp.where(high, hi_arr[a], 0.0), B_HD, 1)
                    + jnp.where(high, hi_arr[b], 0.0) + pltpu.roll(jnp.where(low, lo_arr[b], 0.0), B_HD, 1))

        acc_kg = jnp.zeros((8, LANES), F32)
        for j in range(KVW // LANES):
            cs = slice(j * LANES, (j + 1) * LANES)
            dkn = jnp.where(live, native(dk_parts, j), 0.0)
            dvc = jnp.where(live, native(dv_parts, j), 0.0)
            kp_hat, kp_r = hats[j][0][:BLK], hats[j][1][:BLK]
            dkn_prev = dkn_c[:, cs] + dkn[:BLK]
            dv_prev = dv_c[:, cs] + dvc[:BLK]
            acc_kg = acc_kg + _fold8(dkn_prev * kp_hat)
            dkh = dkn_prev * kg
            dkv_ref[:, cs] = (kp_r * (dkh - kp_hat * (_half_sum(dkh * kp_hat, low) * (1.0 / B_HD)))).astype(BF16)
            dkv_ref[:, KVW + j * LANES:KVW + (j + 1) * LANES] = dv_prev.astype(BF16)
            dkn_c[:, cs] = dkn[BLK:]
            dv_c[:, cs] = dvc[BLK:]
        keep = jnp.where(i > 0, 1.0, 0.0)
        pqg_ref[...] = jnp.where(live, acc_qg, 0.0)
        pkg_ref[...] = _fold_halves(acc_kg, low) * keep
        psk_ref[...] = jnp.broadcast_to(jnp.where(live, acc_sk, 0.0), (8, LANES)) * (
            lax.broadcasted_iota(jnp.int32, (8, LANES), 0) == 0).astype(F32)

    q_gain, k_gain = jnp.tile(q_gain, (1, 2)), jnp.tile(k_gain, (1, 2))
    last = nb - 1
    cur = lambda w, off: pl.BlockSpec((BLK, w), lambda i: (jnp.minimum(i, last), off // w))
    prev = lambda w, off: pl.BlockSpec((BLK, w), lambda i: (jnp.maximum(i - 1, 0), off // w))
    small = lambda n: pl.BlockSpec((1, 2 * n), lambda i: (0, 0))
    part = pl.BlockSpec((8, 128), lambda i: (i, 0))
    p_shape = jax.ShapeDtypeStruct(((nb + 1) * 8, 128), F32)
    return _pcall(
        body, grid=(nb + 1,),
        in_specs=[cur(BW, OFF_QB), cur(KVW, OFF_KB), prev(KVW, OFF_KB), cur(KVW, OFF_VB), prev(KVW, OFF_VB),
                  pl.BlockSpec((BLK, BW), lambda i: (jnp.minimum(i, last), 0)), small(B_HD), small(B_HD),
                  pl.BlockSpec(memory_space=pltpu.SMEM)],
        out_specs=[pl.BlockSpec((BLK, BW), lambda i: (i, 0)),
                   pl.BlockSpec((BLK, 2 * KVW), lambda i: (jnp.maximum(i - 1, 0), 0)), part, part, part],
        out_shape=[jax.ShapeDtypeStruct((T + BLK, BW), BF16), jax.ShapeDtypeStruct((T, 2 * KVW), BF16),
                   p_shape, p_shape, p_shape],
        scratch_shapes=[pltpu.VMEM((BLK, KVW), F32), pltpu.VMEM((BLK, KVW), F32)], name="swa_bwd",
        semantics=("arbitrary",), args=[proj, proj, proj, proj, proj, dout, q_gain, k_gain, sinks], job=job)


def _branch_merge(ya_pre, attn, wa_t, wb_t, proj, tm, tn, job=None):
    T = ya_pre.shape[0]

    def body(a_ref, b_ref, wa_ref, wb_ref, ga_ref, gb_ref, ya_ref, yb_ref, mg_ref):
        ya = lax.dot_general(a_ref[...], wa_ref[...], NT, preferred_element_type=F32)
        yb = lax.dot_general(b_ref[...], wb_ref[...], NT, preferred_element_type=F32)
        ya_ref[...] = ya.astype(BF16)
        yb_ref[...] = yb.astype(BF16)
        mg_ref[...] = (_sigmoid(ga_ref[...]) * ya + _sigmoid(gb_ref[...]) * yb).astype(BF16)

    o_spec = pl.BlockSpec((tm, tn), lambda i, j: (i, j))
    o_shape = jax.ShapeDtypeStruct((T, D), BF16)
    return _pcall(
        body, grid=(T // tm, D // tn),
        in_specs=[pl.BlockSpec((tm, AW), lambda i, j: (i, 0)), pl.BlockSpec((tm, BW), lambda i, j: (i, 0)),
                  pl.BlockSpec((tn, AW), lambda i, j: (j, 0)), pl.BlockSpec((tn, BW), lambda i, j: (j, 0)),
                  pl.BlockSpec((tm, tn), lambda i, j: (i, OFF_GTA // tn + j)),
                  pl.BlockSpec((tm, tn), lambda i, j: (i, OFF_GTB // tn + j))],
        out_specs=[o_spec, o_spec, o_spec], out_shape=[o_shape, o_shape, o_shape], scratch_shapes=[], name="branch_merge",
        semantics=("parallel", "parallel"), args=[ya_pre, attn, wa_t, wb_t, proj, proj], job=job)


def _ij(i, j, k):
    return (i, j)


def _local_step(x, tgt, mod, g1, g2, lbl, og, qg, kg, sk, shards, c_arr):
    win_s, wa_s, wb_s, wout_s, wmi_s, wmo_s = shards
    T = x.shape[0]
    tm, tr, tt = min(1024, T), min(256, T), min(2048, T)
    tk_t = min(1024, T)
    tn = 512
    sh1, sc1, gt1, sh2, sc2, gt2 = (mod[:, i * D:(i + 1) * D] for i in range(N_MOD))
    nI = T // tm
    blk = (tm, tn)
    vec_j = ((1, tn), lambda i, j, k: (0, j))

    h = _rms_mod_fwd("rms1_fwd", x, g1, sc1, sh1, tr)

    def epi_store(acc, ex, ou):
        ou[0][...] = acc.astype(ou[0].dtype)

    tm2 = min(2048, T)
    blk2 = (tm2, tn)

    full = lambda s: (0, s.shape[0])
    last = wmi_s.shape[0]
    gather = _gather_relay_job
    (win_t,) = _run_job("gather_w_in", gather([win_s], alone=True))
    (proj,), (wa_t, wb_t, w_out, wmi_part) = _mm(
        "in_proj", "nt", [(h, D)], win_t, T, IN_W, D, tm2, tn, D, [], [((T, IN_W), F32, blk2, _ij)], epi_store,
        job=gather([wa_s, wb_s, wout_s, wmi_s], rows=[full(wa_s), full(wb_s), full(wout_s), (0, MI_CUTS[0])]))
    (ya_pre, st), (wmi_part,) = _hgrn_fwd(
        proj, lbl, og, tt, job=gather([wmi_s], rows=[MI_CUTS], into=[wmi_part]))
    (attn,), (wmi_t, wmo_part) = _swa_fwd(
        proj, qg, kg, sk, job=gather([wmi_s, wmo_s], rows=[(MI_CUTS[1], last), (0, MO_CUT)], into=[wmi_part, None]))
    (ya, yb, merged), _ = _branch_merge(ya_pre, attn, wa_t, wb_t, proj, tm, tn)

    def epi_res1(acc, ex, ou):
        x_ref, gt_ref = ex
        ou[0][...] = acc.astype(BF16)
        ou[1][...] = x_ref[...] + gt_ref[...] * acc

    mo, x1 = _mm("out_proj", "nn", [(merged, D)], w_out, T, D, D, tm, tn, D, [(x, blk, _ij), (gt1, *vec_j)],
                 [((T, D), BF16, blk, _ij), ((T, D), F32, blk, _ij)], epi_res1)
    h2 = _rms_mod_fwd("rms2_fwd", x1, g2, sc2, sh2, tr)

    def epi_relu2(acc, ex, ou):
        r = jnp.maximum(acc, 0.0)
        ou[0][...] = r.astype(BF16)
        ou[1][...] = (r * r).astype(BF16)

    (r, a), (w_mo,) = _mm("mlp_in", "nt", [(h2, D)], wmi_t, T, HID, D, tm2, tn, D, [],
                          [((T, HID), BF16, blk2, _ij), ((T, HID), BF16, blk2, _ij)], epi_relu2,
                          job=gather([wmo_s], rows=[(MO_CUT, last)], into=[wmo_part]))

    def loss_rows(acc, vecs, bufs, parts):
        gt = vecs[0][...]
        x1_buf, t_buf, dz_buf = bufs
        rows = min(64, tm)
        loss_sum, gate_sum = jnp.zeros((8, D), F32), jnp.zeros((8, D), F32)
        for r0 in range(0, tm, rows):
            rs = slice(r0, r0 + rows)
            z = acc[rs, :]
            e = x1_buf[rs, :] + gt * z - t_buf[rs, :]
            dy = e * (1.0 / D)
            t_buf[rs, :] = dy
            dz_buf[rs, :] = (gt * dy).astype(BF16)
            loss_sum = loss_sum + _fold8(e * e)
            gate_sum = gate_sum + _fold8(dy * z)
        parts[0][...] = loss_sum * (0.5 / D)
        parts[1][...] = gate_sum

    part_rows = ((nI * 8, D), F32, (8, D), lambda i, j, k: (i, 0))
    dy, dz, p_loss, p_gt2 = _rows_mm(
        "mlp_out", [(a, HID)], w_mo, T, tm, 1024, [gt2], [(F32, x1, False), (F32, tgt, True), (BF16, None, True)],
        [part_rows, part_rows], loss_rows)

    def epi_du(acc, ex, ou):
        ou[0][...] = (acc * (2.0 * ex[0][...].astype(F32))).astype(BF16)

    (du,) = _mm("mlp_out_dx", "nt", [(dz, D)], w_mo, T, HID, D, tm2, tn, D, [(r, blk2, _ij)],
                [((T, HID), BF16, blk2, _ij)], epi_du)
    gblk = (1024, 1024)
    gwide = (1024, D)
    pair_sum = lambda nm, g, r1: _pair_sum("pair_sum_" + nm, g, r1, c_arr, _sum_rows(r1.shape[1]))
    (g_mo,) = _mm("mlp_out_dw", "tn", [(a, HID)], dz, HID, D, T, 1024, D, tk_t, [], [((HID, D), BF16, gwide, _ij)], epi_store)
    (dh2,), (r1_mo,) = _mm("mlp_in_dx", "nn", [(du, HID)], wmi_t, T, D, HID, tm, D, 1024, [],
                           [((T, D), F32, (tm, D), _ij)], epi_store, job=_pair_job([g_mo]))
    dx1, p_sh2, p_sc2, p_g2, dmo, p_gt1 = _rms_mod_bwd("rms2_bwd", dh2, x1, g2, sc2, dy, tr, gate=gt1, mo=mo)
    s_mo = pair_sum("mlp_out", g_mo, r1_mo)
    near, far = (1, 2), (3,)
    (g_mi,), (rn_mo,) = _mm("mlp_in_dw", "tn", [(du, HID)], h2, HID, D, T, 1024, D, tk_t, [],
                            [((HID, D), BF16, gwide, _ij)], epi_store, job=_chip_job([s_mo], near))

    def epi_gates(acc, ex, ou):
        ya_ref, yb_ref, ga_ref, gb_ref = ex
        sa, sb = _sigmoid(ga_ref[...]), _sigmoid(gb_ref[...])
        ou[0][...] = (acc * sa).astype(BF16)
        ou[1][...] = (acc * sb).astype(BF16)
        ou[2][...] = (acc * ya_ref[...].astype(F32) * (sa * (1.0 - sa))).astype(BF16)
        ou[3][...] = (acc * yb_ref[...].astype(F32) * (sb * (1.0 - sb))).astype(BF16)

    o_bf = ((T, D), BF16, blk, _ij)
    (dya, dyb, dga, dgb), (rf_mo, r1_mi) = _mm(
        "out_proj_dx", "nt", [(dmo, D)], w_out, T, D, D, tm, tn, D,
        [(ya, blk, _ij), (yb, blk, _ij), (proj, blk, lambda i, j, k: (i, OFF_GTA // tn + j)),
         (proj, blk, lambda i, j, k: (i, OFF_GTB // tn + j))], [o_bf, o_bf, o_bf, o_bf], epi_gates,
        job=_both(_chip_job([s_mo], far), _pair_job([g_mi])))
    s_mi = pair_sum("mlp_in", g_mi, r1_mi)
    (g_out,) = _mm("out_proj_dw", "tn", [(merged, D)], dmo, D, D, T, 1024, 1024, tk_t, [], [((D, D), BF16, gblk, _ij)], epi_store)
    dya_pre, dattn = _twin_mm("branch_dx", "nn", [(dya, wa_t), (dyb, wb_t)], T, AW, D, tm, tn, D, F32)
    g_a, g_b = _twin_mm("branch_dw", "tn", [(dya, ya_pre), (dyb, attn)], D, AW, T, 1024, 1024, tk_t, BF16)
    (dqa, dfa, dia, dgg, p_lb, p_og), (rn_mi, r1_out, r1_a, r1_b) = _hgrn_bwd(
        proj, st, dya_pre, lbl, og, tt, job=_both(_chip_job([s_mi], near), _pair_job([g_out, g_a, g_b])))
    (dqb, dkv, p_qg, p_kg, p_sk), (rf_mi,) = _swa_bwd(proj, dattn, qg, kg, sk, job=_chip_job([s_mi], far))
    s_out, s_a, s_b = pair_sum("out", g_out, r1_out), pair_sum("branch_a", g_a, r1_a), pair_sum("branch_b", g_b, r1_b)
    pieces = [(dqa, AW), (dfa, AW), (dia, AW), (dgg, AW), (dqb, BW), (dkv, 2 * KVW), (dga, D), (dgb, D)]
    (g_in,), (r2_out, r2_a, r2_b) = _pieces_tn("in_proj_dw", pieces, h, 512, job=_chip_job([s_out, s_a, s_b]))
    (r1_in,) = _run_job("pair_w_in", _pair_job([g_in]))
    s_in = pair_sum("in", g_in, r1_in)
    (dx, p_sh1, p_sc1, p_g1), (r2_in,) = _pieces_nn_rms(
        "in_proj_dx", pieces, win_t, x, g1, sc1, dx1, tm, 512, job=_chip_job([s_in]))

    partials = dict(sh1=p_sh1, sc1=p_sc1, gt1=p_gt1, sh2=p_sh2, sc2=p_sc2, gt2=p_gt2, g1=p_g1, g2=p_g2,
                    lb=p_lb, og=p_og, qg=p_qg, kg=p_kg, sk=p_sk, loss=p_loss)
    sums = dict(w_in=(s_in, [r2_in]), w_branch_a=(s_a, [r2_a]), w_branch_b=(s_b, [r2_b]), w_out=(s_out, [r2_out]),
                w_mlp_in=(s_mi, [rn_mi, rf_mi]), w_mlp_out=(s_mo, [rn_mo, rf_mo]))
    return dx, sums, partials


def _exchange_slots(buf, send_sems, recv_sems):
    me = _mesh_pos()
    mine = buf.at[_index(me)]
    sends = []
    for k in range(1, N_DEV):
        cp = pltpu.make_async_remote_copy(src_ref=mine, dst_ref=mine, send_sem=send_sems.at[k - 1],
                                          recv_sem=recv_sems.at[k - 1], device_id=_flip(me, k), device_id_type=MESH)
        cp.start()
        sends.append(cp)
    for k in range(1, N_DEV):
        theirs = buf.at[_index(_flip(me, k))]
        pltpu.make_async_remote_copy(src_ref=theirs, dst_ref=theirs, send_sem=send_sems.at[k - 1],
                                     recv_sem=recv_sems.at[k - 1], device_id=_flip(me, k), device_id_type=MESH).wait_recv()
    for cp in sends:
        cp.wait_send()


ADA_W = N_MOD * D // N_DEV


def _ada_mod(c, w_ada, b_shard):
    def body(c_ref, w_ref, b_ref, mod_ref, sc_ref, cbuf, mbuf, s1, r1, s2, r2):
        me = _index(_mesh_pos())
        cbuf[me] = c_ref[...]
        _exchange_slots(cbuf, s1, r1)
        row = lax.broadcasted_iota(jnp.int32, (N_DEV, D), 0)
        call = jnp.zeros((N_DEV, D), F32)
        for d in range(N_DEV):
            call = jnp.where(row == d, cbuf[d], call)
        sc = call * _sigmoid(call)
        sc_ref[...] = sc
        mbuf[me] = _dot(sc, w_ref[...]) + b_ref[...]
        _exchange_slots(mbuf, s2, r2)
        for s in range(N_DEV):
            mod_ref[:, s * ADA_W:(s + 1) * ADA_W] = mbuf[s, pl.ds(me, 1), :]

    return pl.pallas_call(
        body, in_specs=[_VMEM, _VMEM, _VMEM], out_specs=[_VMEM, _VMEM],
        out_shape=[jax.ShapeDtypeStruct((1, N_MOD * D), F32), jax.ShapeDtypeStruct((N_DEV, D), F32)],
        scratch_shapes=[pltpu.VMEM((N_DEV, 1, D), F32), pltpu.VMEM((N_DEV, N_DEV, ADA_W), F32),
                        _SEMS(N_DEV - 1), _SEMS(N_DEV - 1), _SEMS(N_DEV - 1), _SEMS(N_DEV - 1)],
        name="ada_mod", compiler_params=pltpu.CompilerParams(vmem_limit_bytes=VMEM_LIMIT),
    )(c, w_ada, b_shard)


SMALL_SEGS = (("b_ada", N_MOD * D), ("norm1_gain", D), ("norm2_gain", D), ("lb0", AW), ("lb1", AW),
              ("hgrn_o_gain", AW), ("q_norm_gain", 128), ("k_norm_gain", 128), ("sinks", 128))
SMALL_W = sum(w for _, w in SMALL_SEGS)
X_SEGS = (("sh1", D), ("sc1", D), ("gt1", D), ("sh2", D), ("sc2", D), ("gt2", D), ("g1", D), ("g2", D),
          ("lb", AW), ("og", AW), ("qg", 128), ("kg", 128), ("sk", 128), ("loss", 128))
X_W = sum(w for _, w in X_SEGS)


def _offsets(segs):
    out, o = {}, 0
    for name, w in segs:
        out[name] = (o, w)
        o += w
    return out


def _small_reduce(parts, lb_logits):
    xo, so = _offsets(X_SEGS), _offsets(SMALL_SEGS)
    names = [nm for nm, _ in X_SEGS]

    def body(*refs):
        p_refs = dict(zip(names, refs[:len(names)]))
        lbl_ref, allx, gs_ref, loss_ref, send_sems, recv_sems = refs[len(names):]
        me = _index(_mesh_pos())
        for nm, (o, w) in xo.items():
            if nm == "loss":
                allx[me, :, o:o + w] = jnp.broadcast_to(jnp.sum(p_refs[nm][...]), (1, w))
            else:
                allx[me, :, o:o + w] = jnp.sum(p_refs[nm][...], axis=0, keepdims=True)
        _exchange_slots(allx, send_sems, recv_sems)
        tot = allx[0]
        for d in range(1, N_DEV):
            tot = tot + allx[d]
        seg = lambda nm: tot[:, xo[nm][0]:xo[nm][0] + xo[nm][1]]

        def put(nm, v):
            gs_ref[:, so[nm][0]:so[nm][0] + so[nm][1]] = v

        put("b_ada", tot[:, 0:N_MOD * D])
        put("norm1_gain", seg("g1"))
        put("norm2_gain", seg("g2"))
        lbl = lbl_ref[...]
        lb = _sigmoid(lbl[0:1, :] - lbl[1:2, :])
        dl0 = seg("lb") * lb * (1.0 - lb)
        put("lb0", dl0)
        put("lb1", -dl0)
        put("hgrn_o_gain", seg("og"))
        put("q_norm_gain", seg("qg"))
        put("k_norm_gain", seg("kg"))
        put("sinks", seg("sk"))
        loss_ref[...] = seg("loss")

    return pl.pallas_call(
        body, in_specs=[_VMEM] * (len(names) + 1), out_specs=[_VMEM, _VMEM, _VMEM],
        out_shape=[jax.ShapeDtypeStruct((N_DEV, 1, X_W), F32), jax.ShapeDtypeStruct((1, SMALL_W), F32),
                   jax.ShapeDtypeStruct((1, 128), F32)],
        scratch_shapes=[_SEMS(N_DEV - 1), _SEMS(N_DEV - 1)], name="small_reduce",
        compiler_params=pltpu.CompilerParams(vmem_limit_bytes=VMEM_LIMIT),
    )(*[parts[nm] for nm in names], lb_logits)


def _adamw_math(w, g, m, v):
    m = B1 * m + (1.0 - B1) * g
    v = B2 * v + (1.0 - B2) * (g * g)
    m_hat = m / (1.0 - B1 ** STEP)
    v_hat = v / (1.0 - B2 ** STEP)
    return -LR * (m_hat / (jnp.sqrt(v_hat) + ADAM_EPS) + WD * w), m, v


def _sum_rows(rs):
    return 256 if rs % 256 == 0 else rs // 2


def _pair_sum(name, g, recv, c_arr, tr):
    _, rs, cols = recv.shape
    blk = (1, tr, cols)

    def body(c_ref, g_ref, r_ref, o_ref):
        o_ref[...] = (g_ref[...].astype(F32) + r_ref[...].astype(F32)).astype(BF16)

    grid_spec = pltpu.PrefetchScalarGridSpec(
        num_scalar_prefetch=1, grid=(4, rs // tr),
        in_specs=[pl.BlockSpec(blk, lambda q, i, c: (2 * q + c[0], i, 0)), pl.BlockSpec(blk, lambda q, i, c: (q, i, 0))],
        out_specs=pl.BlockSpec(blk, lambda q, i, c: (q, i, 0)))
    return pl.pallas_call(body, grid_spec=grid_spec, out_shape=jax.ShapeDtypeStruct((4, rs, cols), BF16), name=name,
                          compiler_params=_params(("parallel", "parallel")))(c_arr, g.reshape(N_DEV, rs, cols), recv)


def _sum_adamw(name, sums, recvs, q_arr, w, m, v, transposed, tile):
    rows, cols = w.shape
    nR = len(recvs)

    def body(q_ref, s_ref, *refs):
        r_refs = refs[:nR]
        w_ref, m_ref, v_ref, g_ref, d_ref, nm_ref, nv_ref = refs[nR:]
        g = s_ref[0].astype(F32)
        for r_ref in r_refs:
            for slot in range(r_ref.shape[0]):
                g = g + r_ref[slot].astype(F32)
        g = g.T if transposed else g
        g_ref[...] = g
        d_ref[...], nm_ref[...], nv_ref[...] = _adamw_math(w_ref[...], g, m_ref[...], v_ref[...])

    if transposed:
        slab = lambda n, first: pl.BlockSpec((n, cols, tile), lambda i, q: (first(q), 0, i))
    else:
        slab = lambda n, first: pl.BlockSpec((n, tile, cols), lambda i, q: (first(q), i, 0))
    spec = pl.BlockSpec((tile, cols), lambda i, q: (i, 0))
    shape = jax.ShapeDtypeStruct((rows, cols), F32)
    grid_spec = pltpu.PrefetchScalarGridSpec(
        num_scalar_prefetch=1, grid=(rows // tile,),
        in_specs=[slab(1, lambda q: q[0])] + [slab(r.shape[0], lambda q: 0) for r in recvs] + [spec] * 3,
        out_specs=[spec] * 4)
    return pl.pallas_call(body, grid_spec=grid_spec, out_shape=[shape] * 4, name=name,
                          compiler_params=_params(("parallel",)))(q_arr, sums, *recvs, w, m, v)


def _adamw(name, w, g, m, v, tr):
    rows, cols = w.shape

    def body(w_ref, g_ref, m_ref, v_ref, d_ref, nm_ref, nv_ref):
        d_ref[...], nm_ref[...], nv_ref[...] = _adamw_math(w_ref[...], g_ref[...], m_ref[...], v_ref[...])

    spec = pl.BlockSpec((tr, cols), lambda i: (i, 0))
    shape = jax.ShapeDtypeStruct((rows, cols), F32)
    return pl.pallas_call(
        body, grid=(rows // tr,), in_specs=[spec] * 4, out_specs=[spec] * 3, out_shape=[shape] * 3, name=name,
        compiler_params=_params(("parallel",)),
    )(w, g, m, v)


def _ada_update(sc_t, dmod_cols, w, m, v, tr):
    rows, cols = w.shape

    def body(s_ref, d_ref, w_ref, m_ref, v_ref, g_ref, dl_ref, nm_ref, nv_ref):
        g = jnp.dot(s_ref[...], d_ref[...], precision=lax.Precision.HIGHEST, preferred_element_type=F32)
        g_ref[...] = g
        dl_ref[...], nm_ref[...], nv_ref[...] = _adamw_math(w_ref[...], g, m_ref[...], v_ref[...])

    spec = pl.BlockSpec((tr, cols), lambda i: (i, 0))
    shape = jax.ShapeDtypeStruct((rows, cols), F32)
    return pl.pallas_call(
        body, grid=(rows // tr,),
        in_specs=[pl.BlockSpec((tr, N_DEV), lambda i: (i, 0)), pl.BlockSpec((N_DEV, cols), lambda i: (0, 0)), spec, spec, spec],
        out_specs=[spec] * 4, out_shape=[shape] * 4, name="ada_update", compiler_params=_params(("parallel",)),
    )(sc_t, dmod_cols, w, m, v)


BIG = ("w_in", "w_branch_a", "w_branch_b", "w_out", "w_mlp_in", "w_mlp_out")
COLUMN_SHARDED = ("w_in", "w_branch_a", "w_branch_b", "w_mlp_in")
AS_TRANSPOSE = ("w_in",)
WEIGHTS = ("w_ada", "b_ada", "norm1_gain", "w_in", "lb_logits", "hgrn_o_gain", "q_norm_gain", "k_norm_gain", "sinks",
           "w_branch_a", "w_branch_b", "w_out", "norm2_gain", "w_mlp_in", "w_mlp_out")


def _to_bf16(name, w, transposed, tile=256):
    rows, cols = w.shape

    def body(w_ref, o_ref):
        v = w_ref[...]
        o_ref[...] = (v.T if transposed else v).astype(BF16)

    out_spec = pl.BlockSpec((cols, tile), lambda i: (0, i)) if transposed else pl.BlockSpec((tile, cols), lambda i: (i, 0))
    return pl.pallas_call(
        body, grid=(rows // tile,), in_specs=[pl.BlockSpec((tile, cols), lambda i: (i, 0))], out_specs=out_spec,
        out_shape=jax.ShapeDtypeStruct((cols, rows) if transposed else (rows, cols), BF16), name=name,
        compiler_params=_params(("parallel",)))(w)


def _pack_small(p):
    lb = p["lb_logits"]
    src = dict(p, lb0=lb[0:1], lb1=lb[1:2])
    return jnp.concatenate([jnp.pad(src[nm], ((0, 0), (0, w - src[nm].shape[1]))) for nm, w in SMALL_SEGS], axis=1)


def _unpack_small(vec, shapes):
    so = _offsets(SMALL_SEGS)
    out = {}
    for nm, shp in shapes.items():
        if nm == "lb_logits":
            o = so["lb0"][0]
            out[nm] = vec[0, o:o + 2 * AW].reshape(2, AW)
        else:
            o = so[nm][0]
            out[nm] = vec[:, o:o + shp[1]]
    return out


def kernel(x, c, w_ada, b_ada, norm1_gain, w_in, lb_logits, hgrn_o_gain, q_norm_gain, k_norm_gain, sinks, w_branch_a, w_branch_b, w_out, norm2_gain, w_mlp_in, w_mlp_out, loss_target, m_w_ada, m_b_ada, m_norm1_gain, m_w_in, m_lb_logits, m_hgrn_o_gain, m_q_norm_gain, m_k_norm_gain, m_sinks, m_w_branch_a, m_w_branch_b, m_w_out, m_norm2_gain, m_w_mlp_in, m_w_mlp_out, v_w_ada, v_b_ada, v_norm1_gain, v_w_in, v_lb_logits, v_hgrn_o_gain, v_q_norm_gain, v_k_norm_gain, v_sinks, v_w_branch_a, v_w_branch_b, v_w_out, v_norm2_gain, v_w_mlp_in, v_w_mlp_out):
    w = dict(w_ada=w_ada, b_ada=b_ada, norm1_gain=norm1_gain, w_in=w_in, lb_logits=lb_logits, hgrn_o_gain=hgrn_o_gain,
             q_norm_gain=q_norm_gain, k_norm_gain=k_norm_gain, sinks=sinks, w_branch_a=w_branch_a, w_branch_b=w_branch_b,
             w_out=w_out, norm2_gain=norm2_gain, w_mlp_in=w_mlp_in, w_mlp_out=w_mlp_out)
    m = dict(w_ada=m_w_ada, b_ada=m_b_ada, norm1_gain=m_norm1_gain, w_in=m_w_in, lb_logits=m_lb_logits,
             hgrn_o_gain=m_hgrn_o_gain, q_norm_gain=m_q_norm_gain, k_norm_gain=m_k_norm_gain, sinks=m_sinks,
             w_branch_a=m_w_branch_a, w_branch_b=m_w_branch_b, w_out=m_w_out, norm2_gain=m_norm2_gain,
             w_mlp_in=m_w_mlp_in, w_mlp_out=m_w_mlp_out)
    v = dict(w_ada=v_w_ada, b_ada=v_b_ada, norm1_gain=v_norm1_gain, w_in=v_w_in, lb_logits=v_lb_logits,
             hgrn_o_gain=v_hgrn_o_gain, q_norm_gain=v_q_norm_gain, k_norm_gain=v_k_norm_gain, sinks=v_sinks,
             w_branch_a=v_w_branch_a, w_branch_b=v_w_branch_b, w_out=v_w_out, norm2_gain=v_norm2_gain,
             w_mlp_in=v_w_mlp_in, w_mlp_out=v_w_mlp_out)
    for d in (w, m, v):
        for nm in ("w_ada",) + BIG:
            d[nm] = d[nm][0]
    px, py, pc = _mesh_pos()
    me = _index((px, py, pc))
    c_arr = jnp.reshape(pc, (1,)).astype(jnp.int32)
    q_arr = jnp.reshape(2 * px + py, (1,)).astype(jnp.int32)

    shards = [_to_bf16("shard_" + nm, w[nm].T, False, w[nm].shape[1] // 4) if nm in AS_TRANSPOSE else
              _to_bf16("shard_" + nm, w[nm], nm in COLUMN_SHARDED) for nm in BIG]
    b_shard = lax.dynamic_slice(b_ada, (0, me * ADA_W), (1, ADA_W))
    mod, sc_all = _ada_mod(c, w["w_ada"], b_shard)

    dx, sums, parts = _local_step(x[0], loss_target[0], mod, norm1_gain, norm2_gain, lb_logits, hgrn_o_gain,
                                  q_norm_gain, k_norm_gain, sinks, shards, c_arr)

    allx, g_small, loss = _small_reduce(parts, lb_logits)

    grad, delta, new_m, new_v = {}, {}, {}, {}
    for nm in BIG:
        s, r2 = sums[nm]
        if nm in AS_TRANSPOSE:
            res = _sum_adamw("adamw_" + nm, s, r2, q_arr, w[nm].T, m[nm].T, v[nm].T, False, w[nm].shape[1] // 4)
            grad[nm], delta[nm], new_m[nm], new_v[nm] = (t.T for t in res)
        else:
            grad[nm], delta[nm], new_m[nm], new_v[nm] = _sum_adamw(
                "adamw_" + nm, s, r2, q_arr, w[nm], m[nm], v[nm], nm in COLUMN_SHARDED, 128)

    dmod_cols = lax.dynamic_slice(allx[:, 0, :], (0, me * ADA_W), (N_DEV, ADA_W))
    grad["w_ada"], delta["w_ada"], new_m["w_ada"], new_v["w_ada"] = _ada_update(
        sc_all.T, dmod_cols, w["w_ada"], m["w_ada"], v["w_ada"], 256)

    small_names = [nm for nm in WEIGHTS if nm not in BIG and nm != "w_ada"]
    shapes = {nm: w[nm].shape for nm in small_names}
    ds, ms, vs = _adamw("adamw_small", _pack_small(w), g_small, _pack_small(m), _pack_small(v), 1)
    for dst, vec in ((grad, g_small), (delta, ds), (new_m, ms), (new_v, vs)):
        dst.update(_unpack_small(vec, shapes))

    def full(d, nm):
        return d[nm][None] if nm in BIG or nm == "w_ada" else d[nm]

    return (loss[0, 0], dx[None], *[full(grad, nm) for nm in WEIGHTS], *[full(delta, nm) for nm in WEIGHTS],
            *[full(new_m, nm) for nm in WEIGHTS], *[full(new_v, nm) for nm in WEIGHTS])
```

```python
import functools

import jax
import jax.numpy as jnp
from jax import lax
from jax.experimental import pallas as pl
from jax.experimental.pallas import tpu as pltpu

F32 = jnp.float32
BF16 = jnp.bfloat16
MESH = pl.DeviceIdType.MESH

N_DEV = 8
D = 2048
A_HEADS, A_HD, CHUNK = 8, 128, 64
AW = A_HEADS * A_HD
Q_HEADS, KV_HEADS, GROUP, B_HD, BLK = 16, 4, 4, 64, 128
BW = Q_HEADS * B_HD
KVW = KV_HEADS * B_HD
HID = 4 * D
IN_W = 4 * AW + BW + 2 * KVW + 2 * D
OFF_QA, OFF_FA, OFF_IA, OFF_GA = 0, AW, 2 * AW, 3 * AW
OFF_QB = 4 * AW
OFF_KB = OFF_QB + BW
OFF_VB = OFF_KB + KVW
OFF_GTA = OFF_VB + KVW
OFF_GTB = OFF_GTA + D
N_MOD = 6
EPS = 1e-6
LR, B1, B2, ADAM_EPS, WD, STEP = 1e-3, 0.9, 0.999, 1e-8, 0.01, 10
NEG = -1e30

VMEM_LIMIT = 56 * 1024 * 1024
MI_CUTS = (672, 864)
MO_CUT = 272

NN = (((1,), (0,)), ((), ()))
NT = (((1,), (1,)), ((), ()))
TN = (((0,), (0,)), ((), ()))
BNN = (((2,), (1,)), ((0,), (0,)))
BNT = (((2,), (2,)), ((0,), (0,)))
BTN = (((1,), (1,)), ((0,), (0,)))


def _dot(a, b, dims=NN):
    return lax.dot_general(a.astype(BF16), b.astype(BF16), dims, preferred_element_type=F32)


def _params(sem):
    return pltpu.CompilerParams(dimension_semantics=sem, vmem_limit_bytes=VMEM_LIMIT)


def _sigmoid(x):
    return 1.0 / (1.0 + jnp.exp(-x))


def _fold8(v):
    r, n = v.shape
    return jnp.sum(v.reshape(r // 8, 8, n), axis=0)


_VMEM = pl.BlockSpec(memory_space=pltpu.VMEM)
_ANY = pl.BlockSpec(memory_space=pl.ANY)
_SEMS = lambda n: pltpu.SemaphoreType.DMA((n,))


def _mesh_pos():
    return lax.axis_index("x"), lax.axis_index("y"), lax.axis_index("c")


def _flip(pos, k):
    return tuple(1 - p if (k >> s) & 1 else p for p, s in zip(pos, (2, 1, 0)))


def _index(pos):
    return 4 * pos[0] + 2 * pos[1] + pos[2]


class _Job:
    def __init__(self, ins, out_shape, sems, start, finish, aliases=None, middle=None):
        self.ins, self.out_shape, self.sems, self.start, self.finish = list(ins), list(out_shape), list(sems), start, finish
        self.aliases = dict(aliases or {})
        self.middle = middle


def _both(j1, j2):
    assert not j1.aliases and not j2.aliases
    n_in, n_out, n_sem = len(j1.ins), len(j1.out_shape), len(j1.sems)
    first = lambda ins, outs, sems: (ins[:n_in], outs[:n_out], sems[:n_sem])
    second = lambda ins, outs, sems: (ins[n_in:], outs[n_out:], sems[n_sem:])

    def start(*refs):
        j1.start(*first(*refs))
        j2.start(*second(*refs))

    def finish(*refs):
        j1.finish(*first(*refs))
        j2.finish(*second(*refs))

    return _Job(j1.ins + j2.ins, j1.out_shape + j2.out_shape, j1.sems + j2.sems, start, finish)


def _pcall(body, *, grid, in_specs, out_specs, out_shape, scratch_shapes, name, semantics, args, job=None):
    if job is None:
        outs = pl.pallas_call(body, grid=grid, in_specs=in_specs, out_specs=out_specs, out_shape=out_shape,
                              scratch_shapes=scratch_shapes, name=name, compiler_params=_params(semantics))(*args)
        return list(outs), []
    n_in, n_out, n_scr = len(in_specs), len(out_specs), len(scratch_shapes)
    j_in, j_out = len(job.ins), len(job.out_shape)
    steps = tuple(grid)

    def carrier(*refs):
        o = 0
        main_in, o = refs[o:o + n_in], o + n_in
        job_in, o = refs[o:o + j_in], o + j_in
        main_out, o = refs[o:o + n_out], o + n_out
        job_out, o = refs[o:o + j_out], o + j_out
        main_scr, job_sems = refs[o:o + n_scr], refs[o + n_scr:]
        ids = [pl.program_id(a) for a in range(len(steps))]
        first = functools.reduce(lambda p, q: p & q, [i == 0 for i in ids])
        last = functools.reduce(lambda p, q: p & q, [i == s - 1 for i, s in zip(ids, steps)])

        @pl.when(first)
        def _():
            job.start(job_in, job_out, job_sems)

        if job.middle is not None:
            flat, total = 0, 1
            for i, s in zip(ids, steps):
                flat, total = flat * s + i, total * s

            @pl.when(flat == total // 2)
            def _():
                job.middle(job_in, job_out, job_sems)

        body(*main_in, *main_out, *main_scr)

        @pl.when(last)
        def _():
            job.finish(job_in, job_out, job_sems)

    outs = pl.pallas_call(
        carrier, grid=grid, in_specs=list(in_specs) + [_ANY] * j_in, out_specs=list(out_specs) + [_ANY] * j_out,
        out_shape=list(out_shape) + job.out_shape, scratch_shapes=list(scratch_shapes) + job.sems, name=name,
        input_output_aliases={n_in + i: n_out + o for i, o in job.aliases.items()},
        compiler_params=_params(("arbitrary",) * len(steps)),
    )(*args, *job.ins)
    return list(outs[:n_out]), list(outs[n_out:])


def _run_job(name, job):
    j_in, j_out = len(job.ins), len(job.out_shape)

    def body(*refs):
        ins, outs, sems = refs[:j_in], refs[j_in:j_in + j_out], refs[j_in + j_out:]
        job.start(ins, outs, sems)
        job.finish(ins, outs, sems)

    return list(pl.pallas_call(body, in_specs=[_ANY] * j_in, out_specs=[_ANY] * j_out, out_shape=job.out_shape,
                               scratch_shapes=job.sems, name=name,
                               input_output_aliases=job.aliases)(*job.ins))


def _gather_job(shards, rows=None, into=None):
    n = len(shards)
    rows = rows or [(0, s.shape[0]) for s in shards]
    into = into or [None] * n
    olds, aliases = [], {}
    for a, buf in enumerate(into):
        if buf is not None:
            aliases[n + len(olds)] = a
            olds.append(buf)

    def copies(ins, outs, sems):
        send_sems, recv_sems, local_sems = sems
        x, y, c = _mesh_pos()
        me, sib = (x, y, c), (x, y, 1 - c)
        chips = [(1 - x, y), (x, 1 - y), (1 - x, 1 - y)]

        def part(a, p):
            rs, (r0, r1) = shards[a].shape[0], rows[a]
            return outs[a].at[pl.ds(_index(p) * rs + r0, r1 - r0), :]

        own = lambda a: ins[a].at[pl.ds(rows[a][0], rows[a][1] - rows[a][0]), :]

        def copy(a, k, block, to, src=None):
            return pltpu.make_async_remote_copy(
                src_ref=part(a, block) if src is None else src, dst_ref=part(a, block),
                send_sem=send_sems.at[7 * a + k], recv_sem=recv_sems.at[7 * a + k], device_id=to, device_id_type=MESH)

        mine = [pltpu.make_async_copy(own(a), part(a, me), local_sems.at[a]) for a in range(n)]
        first = []
        for a in range(n):
            first.append(copy(a, 0, me, sib, src=own(a)))
            first += [copy(a, 1 + j, me, (*chip, c), src=own(a)) for j, chip in enumerate(chips)]
        return me, sib, c, chips, copy, mine, first

    def start(ins, outs, sems):
        *_, mine, first = copies(ins, outs, sems)
        for cp in mine + first:
            cp.start()

    def finish(ins, outs, sems):
        me, sib, c, chips, copy, mine, first = copies(ins, outs, sems)
        passed = []
        for j, chip in enumerate(chips):
            for a in range(n):
                copy(a, 1 + j, (*chip, c), me).wait_recv()
                cp = copy(a, 4 + j, (*chip, c), sib)
                cp.start()
                passed.append(cp)
        for a in range(n):
            copy(a, 0, sib, me).wait_recv()
            for j, chip in enumerate(chips):
                copy(a, 4 + j, (*chip, 1 - c), me).wait_recv()
        for cp in first + passed:
            cp.wait_send()
        for cp in mine:
            cp.wait()

    return _Job(list(shards) + olds, [jax.ShapeDtypeStruct((N_DEV * s.shape[0], s.shape[1]), s.dtype) for s in shards],
                [_SEMS(7 * n), _SEMS(7 * n), _SEMS(n)], start, finish, aliases)


def _gather_relay_job(shards, rows=None, into=None, alone=False):
    n = len(shards)
    rows = rows or [(0, s.shape[0]) for s in shards]
    into = into or [None] * n
    olds, aliases = [], {}
    for a, buf in enumerate(into):
        if buf is not None:
            aliases[n + len(olds)] = a
            olds.append(buf)

    def tools(ins, outs, sems):
        send_sems, recv_sems, local_sems = sems
        x, y, c = _mesh_pos()
        q = 2 * x + y
        chip_at = lambda rel: (1 - x if rel & 2 else x, 1 - y if rel & 1 else y)

        def part(a, chip, core):
            rs, (r0, r1) = shards[a].shape[0], rows[a]
            return outs[a].at[pl.ds((2 * chip + core) * rs + r0, r1 - r0), :]

        own = lambda a: ins[a].at[pl.ds(rows[a][0], rows[a][1] - rows[a][0]), :]

        def copy(a, slot, chip, core, to, src=None):
            blk = part(a, chip, core)
            return pltpu.make_async_remote_copy(src_ref=blk if src is None else src, dst_ref=blk,
                                                send_sem=send_sems.at[7 * a + slot], recv_sem=recv_sems.at[7 * a + slot],
                                                device_id=to, device_id_type=MESH)

        mine = [pltpu.make_async_copy(own(a), part(a, q, c), local_sems.at[a]) for a in range(n)]
        first = [copy(a, slot, q, c, (x, y, 1 - c) if slot == 0 else (*chip_at(slot), c), src=own(a))
                 for a in range(n) for slot in (0, 1, 2)]
        return x, y, c, q, chip_at, copy, mine, first

    def start(ins, outs, sems):
        *_, mine, first = tools(ins, outs, sems)
        for cp in mine + first:
            cp.start()

    def middle(ins, outs, sems):
        x, y, c, q, chip_at, copy, _, _ = tools(ins, outs, sems)
        me, sib = (x, y, c), (x, y, 1 - c)

        def relay(src, dst):
            for a in range(n):
                copy(a, src, q ^ src, c, me).wait_recv()
                copy(a, 3, q ^ src, c, (*chip_at(dst), c)).start()
                copy(a, 3 + src, q ^ src, c, sib).start()
            for a in range(n):
                copy(a, dst, q ^ dst, c, me).wait_recv()
                copy(a, 3 + dst, q ^ dst, c, sib).start()

        pl.when(c == 1)(lambda: relay(1, 2))
        pl.when(c == 0)(lambda: relay(2, 1))

    def finish(ins, outs, sems):
        if alone:
            middle(ins, outs, sems)
        x, y, c, q, chip_at, copy, mine, first = tools(ins, outs, sems)
        me, sib = (x, y, c), (x, y, 1 - c)
        for a in range(n):
            copy(a, 3, q ^ 3, c, me).wait_recv()
            copy(a, 6, q ^ 3, c, sib).start()
        for a in range(n):
            copy(a, 0, q, 1 - c, me).wait_recv()
            for rel in (1, 2, 3):
                copy(a, 3 + rel, q ^ rel, 1 - c, me).wait_recv()
        for a in range(n):
            for slot in range(3, 7):
                copy(a, slot, q, c, sib).wait_send()
        for cp in first:
            cp.wait_send()
        for cp in mine:
            cp.wait()

    return _Job(list(shards) + olds, [jax.ShapeDtypeStruct((N_DEV * s.shape[0], s.shape[1]), s.dtype) for s in shards],
                [_SEMS(7 * n), _SEMS(7 * n), _SEMS(n)], start, finish, aliases, middle=None if alone else middle)


def _pair_job(grads):
    n = len(grads)

    def copies(ins, outs, sems):
        send_sems, recv_sems = sems
        x, y, c = _mesh_pos()
        out = []
        for a in range(n):
            rs = grads[a].shape[0] // N_DEV
            for q in range(4):
                blk = ins[a].at[pl.ds((2 * q + 1 - c) * rs, rs), :]
                out.append(pltpu.make_async_remote_copy(
                    src_ref=blk, dst_ref=outs[a].at[q], send_sem=send_sems.at[4 * a + q], recv_sem=recv_sems.at[4 * a + q],
                    device_id=(x, y, 1 - c), device_id_type=MESH))
        return out

    def start(ins, outs, sems):
        for cp in copies(ins, outs, sems):
            cp.start()

    def finish(ins, outs, sems):
        for cp in copies(ins, outs, sems):
            cp.wait()

    return _Job(grads, [jax.ShapeDtypeStruct((4, g.shape[0] // N_DEV, g.shape[1]), g.dtype) for g in grads],
                [_SEMS(4 * n), _SEMS(4 * n)], start, finish)


def _chip_job(sums, rels=(1, 2, 3)):
    n, nr = len(sums), len(rels)

    def copies(ins, outs, sems):
        send_sems, recv_sems = sems
        x, y, c = _mesh_pos()
        out = []
        for a in range(n):
            for slot, r in enumerate(rels):
                px, py = (1 - x if r & 2 else x), (1 - y if r & 1 else y)
                out.append(pltpu.make_async_remote_copy(
                    src_ref=ins[a].at[2 * px + py], dst_ref=outs[a].at[slot], send_sem=send_sems.at[nr * a + slot],
                    recv_sem=recv_sems.at[nr * a + slot], device_id=(px, py, c), device_id_type=MESH))
        return out

    def start(ins, outs, sems):
        for cp in copies(ins, outs, sems):
            cp.start()

    def finish(ins, outs, sems):
        for cp in copies(ins, outs, sems):
            cp.wait()

    return _Job(sums, [jax.ShapeDtypeStruct((nr,) + s.shape[1:], s.dtype) for s in sums],
                [_SEMS(nr * n), _SEMS(nr * n)], start, finish)


def _mm(name, form, a_list, b, M, N, K, tm, tn, tk, extras, outs, epi, job=None):
    nI, nJ, nK = M // tm, N // tn, K // tk
    assert nI * tm == M and nJ * tn == N and nK * tk == K
    dims = {"nn": NN, "nt": NT, "tn": TN}[form]
    b_list = b if isinstance(b, list) else [(b, {"nn": N, "nt": K, "tn": N}[form])]
    nA, nB = len(a_list), len(b_list)
    assert nA == 1 or nB == 1
    assert nB == 1 or form in ("nn", "nt")
    AXIS = {"i": 0, "j": 1, "k": 2}
    a_axis, a_tile = ("i", tm) if form == "tn" else ("k", tk)
    b_axis, b_tile = ("k", tk) if form == "nt" else ("j", tn)

    def cut(pieces, tile, total):
        starts, s = [], 0
        for _, w in pieces:
            assert w % tile == 0
            starts.append(s // tile)
            s += w
        assert s == total
        return starts, [w // tile for _, w in pieces]

    a_st, a_cn = cut(a_list, a_tile, M if form == "tn" else K)
    b_st, b_cn = cut(b_list, b_tile, K if form == "nt" else N)

    def inside(idx, st, cn):
        return (idx >= st) & (idx < st + cn)

    def a_spec(p):
        st, cn = a_st[p], a_cn[p]
        if form == "tn":
            return pl.BlockSpec((tk, tm), lambda i, j, k: (jnp.where(inside(i, st, cn), k, 0), jnp.clip(i - st, 0, cn - 1)))
        return pl.BlockSpec((tm, tk), lambda i, j, k: (i, jnp.clip(k - st, 0, cn - 1)))

    def b_spec(p):
        st, cn = b_st[p], b_cn[p]
        if form == "nt":
            return pl.BlockSpec((tn, tk), lambda i, j, k: (j, jnp.clip(k - st, 0, cn - 1)))
        if nB == 1:
            return pl.BlockSpec((tk, tn), lambda i, j, k: (k, j))
        return pl.BlockSpec((tk, tn), lambda i, j, k: (jnp.where(inside(j, st, cn), k, 0), jnp.clip(j - st, 0, cn - 1)))

    in_specs = ([a_spec(p) for p in range(nA)] + [b_spec(p) for p in range(nB)]
                + [pl.BlockSpec(bs, im) for _, bs, im in extras])
    out_shape = [jax.ShapeDtypeStruct(s_, d_) for s_, d_, _, _ in outs]
    out_specs = [pl.BlockSpec(bs, im) for _, _, bs, im in outs]
    nE, nO = len(extras), len(outs)
    single = nA == 1 and nB == 1

    def body(*refs):
        a_refs, b_refs = refs[:nA], refs[nA:nA + nB]
        ex, ou = refs[nA + nB:nA + nB + nE], refs[nA + nB + nE:nA + nB + nE + nO]
        ids = [pl.program_id(a) for a in range(3)]

        def partial_of(p, q):
            return lax.dot_general(a_refs[p][...], b_refs[q][...], dims, preferred_element_type=F32)

        if nK == 1 and single:
            epi(partial_of(0, 0), ex, ou)
            return
        acc = refs[-1]
        k = ids[2]
        for p in range(nA):
            for q in range(nB):
                def first(p=p, q=q):
                    acc[...] = partial_of(p, q)

                def later(p=p, q=q):
                    acc[...] += partial_of(p, q)

                here = None
                if nA > 1:
                    here = inside(ids[AXIS[a_axis]], a_st[p], a_cn[p])
                if nB > 1:
                    here = inside(ids[AXIS[b_axis]], b_st[q], b_cn[q])
                pl.when(k == 0 if here is None else here & (k == 0))(first)
                pl.when(k > 0 if here is None else here & (k > 0))(later)

        @pl.when(k == nK - 1)
        def _():
            epi(acc[...], ex, ou)

    scratch = [] if (nK == 1 and single) else [pltpu.VMEM((tm, tn), F32)]
    res, job_res = _pcall(
        body, grid=(nI, nJ, nK), in_specs=in_specs, out_specs=out_specs, out_shape=out_shape, scratch_shapes=scratch,
        name=name, semantics=("parallel", "parallel", "arbitrary"),
        args=[a for a, _ in a_list] + [p for p, _ in b_list] + [e for e, _, _ in extras], job=job)
    return res if job is None else (res, job_res)


def _twin_mm(name, form, pairs, M, N, K, tm, tn, tk, out_dtype):
    nI, nJ, nK = M // tm, N // tn, K // tk
    dims = {"nn": NN, "tn": TN}[form]
    a_spec = (pl.BlockSpec((tm, tk), lambda i, j, k: (i, k)) if form == "nn" else pl.BlockSpec((tk, tm), lambda i, j, k: (k, i)))
    b_spec = pl.BlockSpec((tk, tn), lambda i, j, k: (k, j))
    o_spec = pl.BlockSpec((tm, tn), lambda i, j, k: (i, j))

    def body(a1, b1, a2, b2, o1, o2, *accs):
        k = pl.program_id(2)
        for a_ref, b_ref, o_ref, acc in ((a1, b1, o1, accs[0] if accs else None), (a2, b2, o2, accs[1] if accs else None)):
            part = lax.dot_general(a_ref[...], b_ref[...], dims, preferred_element_type=F32)
            if nK == 1:
                o_ref[...] = part.astype(out_dtype)
                continue

            @pl.when(k == 0)
            def _(acc=acc, part=part):
                acc[...] = part

            @pl.when(k > 0)
            def _(acc=acc, part=part):
                acc[...] += part

            @pl.when(k == nK - 1)
            def _(acc=acc, o_ref=o_ref):
                o_ref[...] = acc[...].astype(out_dtype)

    (a1, b1), (a2, b2) = pairs
    shape = jax.ShapeDtypeStruct((M, N), out_dtype)
    return pl.pallas_call(
        body, grid=(nI, nJ, nK), in_specs=[a_spec, b_spec, a_spec, b_spec], out_specs=[o_spec, o_spec],
        out_shape=[shape, shape], scratch_shapes=[] if nK == 1 else [pltpu.VMEM((tm, tn), F32)] * 2, name=name,
        compiler_params=_params(("parallel", "parallel", "arbitrary")))(a1, b1, a2, b2)


def _piece_tiles(pieces, tile):
    starts, s = [], 0
    for _, w in pieces:
        assert w % tile == 0
        starts.append(s // tile)
        s += w
    return starts, [w // tile for _, w in pieces], s


def _pieces_tn(name, pieces, b, tile, job=None):
    T, N = b.shape
    st, cn, M = _piece_tiles(pieces, tile)
    nP, nI = len(pieces), M // tile

    def body(*refs):
        p_refs, b_hbm, o_ref = refs[:nP], refs[nP], refs[nP + 1]
        bbuf, abuf, bsem, asem = refs[nP + 2:]
        i = pl.program_id(0)

        def fetch(step, slot):
            for p in range(nP):
                @pl.when((step >= st[p]) & (step < st[p] + cn[p]))
                def _():
                    col = pl.multiple_of((step - st[p]) * tile, tile)
                    pltpu.make_async_copy(p_refs[p].at[pl.ds(0, T), pl.ds(col, tile)], abuf.at[slot], asem.at[slot]).start()

        @pl.when(i == 0)
        def _():
            whole = pltpu.make_async_copy(b_hbm, bbuf, bsem)
            whole.start()
            fetch(0, 0)
            whole.wait()

        @pl.when(i + 1 < nI)
        def _():
            fetch(i + 1, (i + 1) % 2)

        pltpu.make_async_copy(p_refs[0].at[pl.ds(0, T), pl.ds(0, tile)], abuf.at[i % 2], asem.at[i % 2]).wait()
        o_ref[...] = lax.dot_general(abuf[i % 2], bbuf[...], TN, preferred_element_type=F32).astype(BF16)

    res, job_res = _pcall(
        body, grid=(nI,), in_specs=[_ANY] * (nP + 1), out_specs=[pl.BlockSpec((tile, N), lambda i: (i, 0))],
        out_shape=[jax.ShapeDtypeStruct((M, N), BF16)],
        scratch_shapes=[pltpu.VMEM((T, N), b.dtype), pltpu.VMEM((2, T, tile), b.dtype), pltpu.SemaphoreType.DMA, _SEMS(2)],
        name=name, semantics=("arbitrary",), args=[p for p, _ in pieces] + [b], job=job)
    return res if job is None else (res, job_res)


def _rows_mm(name, pieces, w, T, tm, tk, vecs, bufs, parts, epi, job=None):
    st, cn, K = _piece_tiles(pieces, tk)
    nP, nI, nK = len(pieces), T // tm, K // tk
    part_specs = [pl.BlockSpec(bs, lambda i, k, im=im: im(i, 0, k)) for _, _, bs, im in parts]
    n_vec, nB = len(vecs), len(bufs)
    load_ix = [n for n, (_, src, _) in enumerate(bufs) if src is not None]
    store_ix = [n for n, (_, _, store) in enumerate(bufs) if store]
    n_any_in, n_any_out = len(load_ix), len(store_ix)

    def body(*refs):
        o = nP
        p_refs, w_ref = refs[:nP], refs[o]
        vec_refs = refs[o + 1:o + 1 + n_vec]
        ins = refs[o + 1 + n_vec:o + 1 + n_vec + n_any_in]
        o = o + 1 + n_vec + n_any_in
        hbm_outs, p_outs = refs[o:o + n_any_out], refs[o + n_any_out:o + n_any_out + len(parts)]
        o = o + n_any_out + len(parts)
        acc, abuf = refs[o:o + 2]
        buf_refs = refs[o + 2:o + 2 + nB]
        asem, in_sems, out_sems = refs[-3:]
        i, k = pl.program_id(0), pl.program_id(1)
        g = i * nK + k
        rows_of = lambda ref, ii: ref.at[pl.ds(pl.multiple_of(ii * tm, tm), tm), :]
        bufs_in = [buf_refs[n] for n in load_ix]
        bufs_out = [buf_refs[n] for n in store_ix]

        def fetch(ii, kk, slot):
            for p in range(nP):
                @pl.when((kk >= st[p]) & (kk < st[p] + cn[p]))
                def _():
                    col = pl.multiple_of((kk - st[p]) * tk, tk)
                    src = p_refs[p].at[pl.ds(pl.multiple_of(ii * tm, tm), tm), pl.ds(col, tk)]
                    pltpu.make_async_copy(src, abuf.at[slot], asem.at[slot]).start()

        loads = lambda ii: [pltpu.make_async_copy(rows_of(src, ii), buf, in_sems.at[n])
                            for n, (src, buf) in enumerate(zip(ins, bufs_in))]
        stores = lambda ii: [pltpu.make_async_copy(buf, rows_of(dst, ii), out_sems.at[n])
                             for n, (buf, dst) in enumerate(zip(bufs_out, hbm_outs))]

        @pl.when(g == 0)
        def _():
            fetch(0, 0, 0)

        @pl.when(g + 1 < nI * nK)
        def _():
            last_k = k == nK - 1
            fetch(jnp.where(last_k, i + 1, i), jnp.where(last_k, 0, k + 1), (g + 1) % 2)

        @pl.when(k == 0)
        def _():
            @pl.when(i > 0)
            def _():
                for cp in stores(i - 1):
                    cp.wait()
            for cp in loads(i):
                cp.start()

        pltpu.make_async_copy(p_refs[0].at[pl.ds(0, tm), pl.ds(0, tk)], abuf.at[g % 2], asem.at[g % 2]).wait()

        def product(cols):
            return jnp.dot(abuf[g % 2], w_ref[:, cols], preferred_element_type=F32)

        col_blocks = [slice(c0, c0 + 512) for c0 in range(0, D, 512)]

        @pl.when(k == 0)
        def _():
            for cols in col_blocks:
                acc[:, cols] = product(cols)

        @pl.when(k > 0)
        def _():
            for cols in col_blocks:
                acc[:, cols] += product(cols)

        @pl.when(k == nK - 1)
        def _():
            for cp in loads(i):
                cp.wait()
            epi(acc, vec_refs, buf_refs, p_outs)
            for cp in stores(i):
                cp.start()

            @pl.when(i == nI - 1)
            def _():
                for cp in stores(i):
                    cp.wait()

    vec = pl.BlockSpec((1, D), lambda i, k: (0, 0))
    scratch = ([pltpu.VMEM((tm, D), F32), pltpu.VMEM((2, tm, tk), BF16)] + [pltpu.VMEM((tm, D), dt) for dt, _, _ in bufs]
               + [_SEMS(2), _SEMS(n_any_in), _SEMS(n_any_out)])
    res, job_res = _pcall(
        body, grid=(nI, nK),
        in_specs=[_ANY] * nP + [pl.BlockSpec((tk, D), lambda i, k: (k, 0))] + [vec] * n_vec + [_ANY] * n_any_in,
        out_specs=[_ANY] * n_any_out + part_specs,
        out_shape=([jax.ShapeDtypeStruct((T, D), bufs[n][0]) for n in store_ix]
                   + [jax.ShapeDtypeStruct(s, d) for s, d, _, _ in parts]),
        scratch_shapes=scratch, name=name, semantics=("arbitrary", "arbitrary"),
        args=[p for p, _ in pieces] + [w] + list(vecs) + [bufs[n][1] for n in load_ix], job=job)
    return res if job is None else (res, job_res)


def _pieces_nn_rms(name, pieces, w, x, gain, sc, dres, tm, tk, job=None):
    _, outs, epi = _rms_mod_bwd_epilogue(x, gain, sc, dres, tm)

    def on_rows(acc, vecs, bufs, parts):
        epi(acc, [bufs[0], vecs[0], vecs[1], bufs[1]], [bufs[1], *parts])

    return _rows_mm(name, pieces, w, x.shape[0], tm, tk, [gain, sc], [(F32, x, False), (F32, dres, True)],
                    outs[1:], on_rows, job=job)


def _rms_mod_fwd(name, x, gain, sc, sh, tr):
    T = x.shape[0]

    def body(x_ref, g_ref, sc_ref, sh_ref, h_ref):
        xv = x_ref[...]
        rstd = lax.rsqrt(jnp.mean(xv * xv, axis=-1, keepdims=True) + EPS)
        h_ref[...] = ((xv * rstd * g_ref[...]) * (1.0 + sc_ref[...]) + sh_ref[...]).astype(BF16)

    row = pl.BlockSpec((tr, D), lambda i: (i, 0))
    vec = pl.BlockSpec((1, D), lambda i: (0, 0))
    return pl.pallas_call(
        body, grid=(T // tr,), in_specs=[row, vec, vec, vec], out_specs=row,
        out_shape=jax.ShapeDtypeStruct((T, D), BF16), name=name, compiler_params=_params(("parallel",)),
    )(x, gain, sc, sh)


def _rms_mod_bwd_epilogue(x, gain, sc, dres, tm, gate=None, mo=None):
    T = x.shape[0]
    with_gate = gate is not None
    row = ((tm, D), lambda i, j, k: (i, 0))
    vec = ((1, D), lambda i, j, k: (0, 0))
    part = ((T // tm * 8, D), F32, (8, D), lambda i, j, k: (i, 0))
    extras = [(x, *row), (gain, *vec), (sc, *vec), (dres, *row)]
    outs = [((T, D), F32, *row), part, part, part]
    if with_gate:
        extras += [(gate, *vec), (mo, *row)]
        outs += [((T, D), BF16, *row), part]

    rows = min(64, tm)

    def epi(acc, ex, ou):
        g = ex[1][...]
        sums = [jnp.zeros((8, D), F32) for _ in range(4)]
        for r0 in range(0, tm, rows):
            rs = slice(r0, r0 + rows)
            dhv, xv = acc[rs, :], ex[0][rs, :]
            rstd = lax.rsqrt(jnp.mean(xv * xv, axis=-1, keepdims=True) + EPS)
            xhat = xv * rstd
            dn = dhv * (1.0 + ex[2][...])
            dxhat = dn * g
            dx = ex[3][rs, :] + rstd * (dxhat - xhat * jnp.mean(dxhat * xhat, axis=-1, keepdims=True))
            ou[0][rs, :] = dx
            terms = [dhv, dhv * (xhat * g), dn * xhat]
            if with_gate:
                terms.append(dx * ex[5][rs, :].astype(F32))
                ou[4][rs, :] = (ex[4][...] * dx).astype(BF16)
            sums = [s + _fold8(t) for s, t in zip(sums, terms)] + sums[len(terms):]
        ou[1][...], ou[2][...], ou[3][...] = sums[:3]
        if with_gate:
            ou[5][...] = sums[3]

    return extras, outs, epi


def _rms_mod_bwd(name, dh, x, gain, sc, dres, tr, gate=None, mo=None):
    T = x.shape[0]
    extras, outs, epi = _rms_mod_bwd_epilogue(x, gain, sc, dres, tr, gate, mo)
    rows_only = lambda im: (lambda i: im(i, 0, 0))
    nE = len(extras)

    def body(dh_ref, *refs):
        epi(dh_ref, refs[:nE], refs[nE:])

    return pl.pallas_call(
        body, grid=(T // tr,),
        in_specs=[pl.BlockSpec((tr, D), lambda i: (i, 0))] + [pl.BlockSpec(bs, rows_only(im)) for _, bs, im in extras],
        out_specs=[pl.BlockSpec(bs, rows_only(im)) for _, _, bs, im in outs],
        out_shape=[jax.ShapeDtypeStruct(s, d) for s, d, _, _ in outs], name=name, compiler_params=_params(("parallel",)),
    )(dh, *[e for e, _, _ in extras])


def _split3(v):
    h = v.astype(BF16)
    r1 = v - h.astype(F32)
    m = r1.astype(BF16)
    lo = (r1 - m.astype(F32)).astype(BF16)
    return h, m, lo


def _tri_mm(tri, v, dims=NN):
    h, m, lo = _split3(v)
    t = tri.astype(BF16)
    mm = lambda p: lax.dot_general(t, p, dims, preferred_element_type=F32)
    return (mm(lo) + mm(m)) + mm(h)


def _hgrn_chunk_terms(q, fl, lb):
    sig = _sigmoid(fl)
    f = lb + (1.0 - lb) * sig
    lf = jnp.log(f)
    kk = 1.0 - f
    sq = _sigmoid(q)
    qf = q * sq
    return sig, f, lf, kk, sq, qf


def _causal(n):
    r = lax.broadcasted_iota(jnp.int32, (n, n), 0)
    c = lax.broadcasted_iota(jnp.int32, (n, n), 1)
    return r >= c


def _hgrn_fwd(proj, lb_logits, o_gain, tt, job=None):
    T = proj.shape[0]
    nT, ncl = T // tt, tt // CHUNK
    C = CHUNK

    def body(q_ref, f_ref, i_ref, g_ref, lbl_ref, og_ref, y_ref, st_ref, S):
        @pl.when(pl.program_id(1) == 0)
        def _():
            S[...] = jnp.zeros_like(S)

        lbl = lbl_ref[...]
        lb = _sigmoid(lbl[0:1, :] - lbl[1:2, :])
        og = og_ref[...]
        shp = (ncl, C, A_HD)
        q, fl, v, g = (r[...].reshape(shp) for r in (q_ref, f_ref, i_ref, g_ref))
        tri = jnp.broadcast_to(_causal(C), (ncl, C, C))
        _, _, lf, kk, _, qf = _hgrn_chunk_terms(q, fl, lb)
        b = _tri_mm(tri, lf, BNN)
        bm, bl = b[:, C // 2 - 1:C // 2, :], b[:, C - 1:C, :]
        qd, kd = qf * jnp.exp(b - bm), kk * jnp.exp(bm - b)
        A = jnp.where(tri, _dot(qd, kd, BNT), 0.0)
        d_st = _dot(v, kk * jnp.exp(bl - b), BTN)
        dec = jnp.exp(bl)
        st = S[...]
        for ci in range(ncl):
            st_ref[0, ci] = st
            st = st * dec[ci] + d_st[ci]
        S[...] = st
        o = _dot(A, v, BNN) + _dot(qf * jnp.exp(b), st_ref[0], BNT)
        r = lax.rsqrt(jnp.mean(o * o, axis=-1, keepdims=True) + EPS)
        y_ref[...] = (o * r * og * (g * _sigmoid(g))).astype(BF16).reshape(tt, A_HD)

    def col(off):
        return pl.BlockSpec((tt, A_HD), lambda h, t: (t, off // A_HD + h))

    head_vec = lambda rows: pl.BlockSpec((rows, A_HD), lambda h, t: (0, h))
    return _pcall(
        body, grid=(A_HEADS, nT),
        in_specs=[col(OFF_QA), col(OFF_FA), col(OFF_IA), col(OFF_GA), head_vec(2), head_vec(1)],
        out_specs=[pl.BlockSpec((tt, A_HD), lambda h, t: (t, h)),
                   pl.BlockSpec((1, ncl, A_HD, A_HD), lambda h, t: (h, t, 0, 0))],
        out_shape=[jax.ShapeDtypeStruct((T, AW), BF16),
                   jax.ShapeDtypeStruct((A_HEADS, T // C, A_HD, A_HD), F32)],
        scratch_shapes=[pltpu.VMEM((A_HD, A_HD), F32)], name="hgrn_fwd", semantics=("parallel", "arbitrary"),
        args=[proj, proj, proj, proj, lb_logits, o_gain], job=job)


def _hgrn_bwd(proj, st, dy, lb_logits, o_gain, tt, job=None):
    T = proj.shape[0]
    nT, ncl = T // tt, tt // CHUNK
    C = CHUNK

    def body(q_ref, f_ref, i_ref, g_ref, st_ref, dy_ref, lbl_ref, og_ref,
             dq_ref, df_ref, di_ref, dg_ref, plb_ref, pog_ref, dS):
        @pl.when(pl.program_id(1) == 0)
        def _():
            dS[...] = jnp.zeros_like(dS)

        lbl = lbl_ref[...]
        lb = _sigmoid(lbl[0:1, :] - lbl[1:2, :])
        og = og_ref[...]
        shp = (ncl, C, A_HD)
        flat = lambda t: t.reshape(tt, A_HD)
        q, fl, v, g, dout = (r[...].reshape(shp) for r in (q_ref, f_ref, i_ref, g_ref, dy_ref))
        tri = jnp.broadcast_to(_causal(C), (ncl, C, C))
        rowi = lax.broadcasted_iota(jnp.int32, shp, 1)
        st0 = st_ref[0]
        sig, f, lf, kk, sq, qf = _hgrn_chunk_terms(q, fl, lb)
        b = _tri_mm(tri, lf, BNN)
        bm, bl = b[:, C // 2 - 1:C // 2, :], b[:, C - 1:C, :]
        e_qd, e_kd, e_ke, e_b = jnp.exp(b - bm), jnp.exp(bm - b), jnp.exp(bl - b), jnp.exp(b)
        qd, kd, ke, qe = qf * e_qd, kk * e_kd, kk * e_ke, qf * e_b
        dec = jnp.exp(bl)
        A = jnp.where(tri, _dot(qd, kd, BNT), 0.0)
        o = _dot(A, v, BNN) + _dot(qe, st0, BNT)
        r = lax.rsqrt(jnp.mean(o * o, axis=-1, keepdims=True) + EPS)
        sg = _sigmoid(g)
        on = o * r * og
        dg_ref[...] = flat((dout * on * (sg * (1.0 + g * (1.0 - sg)))).astype(BF16))
        don = dout * (g * sg)
        pog_ref[...] = _fold8(flat(don * o * r))
        dyh = don * og
        do = r * (dyh - o * (r * r) * jnp.mean(dyh * o, axis=-1, keepdims=True))
        g_st = _dot(do, qe, BTN)
        run = dS[...]
        after = [None] * ncl
        for ci in reversed(range(ncl)):
            after[ci] = run
            run = g_st[ci] + run * dec[ci]
        dS[...] = run
        d_after = jnp.stack(after, axis=0)
        ddec = jnp.sum(d_after * st0, axis=1, keepdims=True)
        dqe = _dot(do, st0, BNN)
        dke = _dot(v, d_after, BNN)
        dA = jnp.where(tri, _dot(do, v, BNT), 0.0)
        dv = _dot(ke, d_after, BNT) + _dot(A, do, BTN)
        dqd = _dot(dA, kd, BNN)
        dkd = _dot(dA, qd, BTN)
        di_ref[...] = flat(dv.astype(BF16))
        dqf = dqe * e_b + dqd * e_qd
        dkk = dkd * e_kd + dke * e_ke
        t_qd, t_kd, t_ke = dqd * qd, dkd * kd, dke * ke
        db = dqe * qe + t_qd - t_kd - t_ke
        dbm = jnp.sum(t_kd - t_qd, axis=1, keepdims=True)
        dbl = jnp.sum(t_ke, axis=1, keepdims=True) + ddec * dec
        db = db + jnp.where(rowi == C // 2 - 1, dbm, 0.0) + jnp.where(rowi == C - 1, dbl, 0.0)
        dlf = _tri_mm(tri, db, BTN)
        dfv = dlf / f - dkk
        df_ref[...] = flat((dfv * (1.0 - lb) * sig * (1.0 - sig)).astype(BF16))
        plb_ref[...] = _fold8(flat(dfv * (1.0 - sig)))
        dq_ref[...] = flat((dqf * (sq * (1.0 + q * (1.0 - sq)))).astype(BF16))

    def col(off):
        return pl.BlockSpec((tt, A_HD), lambda h, t: (nT - 1 - t, off // A_HD + h))

    head_vec = lambda rows: pl.BlockSpec((rows, A_HD), lambda h, t: (0, h))
    o_spec = pl.BlockSpec((tt, A_HD), lambda h, t: (nT - 1 - t, h))
    p_spec = pl.BlockSpec((8, A_HD), lambda h, t: (t, h))
    o_shape = jax.ShapeDtypeStruct((T, AW), BF16)
    p_shape = jax.ShapeDtypeStruct((nT * 8, AW), F32)
    return _pcall(
        body, grid=(A_HEADS, nT),
        in_specs=[col(OFF_QA), col(OFF_FA), col(OFF_IA), col(OFF_GA),
                  pl.BlockSpec((1, ncl, A_HD, A_HD), lambda h, t: (h, nT - 1 - t, 0, 0)),
                  pl.BlockSpec((tt, A_HD), lambda h, t: (nT - 1 - t, h)), head_vec(2), head_vec(1)],
        out_specs=[o_spec, o_spec, o_spec, o_spec, p_spec, p_spec],
        out_shape=[o_shape, o_shape, o_shape, o_shape, p_shape, p_shape],
        scratch_shapes=[pltpu.VMEM((A_HD, A_HD), F32)], name="hgrn_bwd", semantics=("parallel", "arbitrary"),
        args=[proj, proj, proj, proj, st, dy, lb_logits, o_gain], job=job)


LANES = 128
Q_COLS = BW // LANES


def _low_half():
    return lax.broadcasted_iota(jnp.int32, (1, LANES), 1) < B_HD


def _half_sum(t, low):
    lo = jnp.sum(jnp.where(low, t, 0.0), axis=-1, keepdims=True)
    hi = jnp.sum(jnp.where(low, 0.0, t), axis=-1, keepdims=True)
    return jnp.where(low, lo, hi)


def _half_rms(t, low):
    r = lax.rsqrt(_half_sum(t * t, low) * (1.0 / B_HD) + EPS)
    return t * r, r


def _fold_halves(p, low):
    return jnp.where(low, p + pltpu.roll(p, B_HD, 1), 0.0)


def _stack_cols(x):
    return jnp.stack([x[:, c * LANES:(c + 1) * LANES] for c in range(Q_COLS)], axis=0).reshape(KV_HEADS, 2 * BLK, LANES)


def _col_of(t, c):
    return t[c // 2, (c % 2) * BLK:(c % 2 + 1) * BLK]


def _split_halves(col, s, low):
    own = jnp.where(low if s == 0 else jnp.logical_not(low), col, 0.0)
    other = pltpu.roll(own, B_HD, 1)
    return (own, other) if s == 0 else (other, own)


def _swa_keys(kp_ref, kc_ref, vp_ref, vc_ref, kg, low):
    k_lo, k_hi, v_lo, v_hi, hats = [], [], [], [], []
    for j in range(KVW // LANES):
        cs = slice(j * LANES, (j + 1) * LANES)
        k_hat, k_r = _half_rms(jnp.concatenate([kp_ref[:, cs], kc_ref[:, cs]], axis=0), low)
        vcol = jnp.concatenate([vp_ref[:, cs], vc_ref[:, cs]], axis=0)
        hats.append((k_hat, k_r))
        for s in range(2):
            for dst_lo, dst_hi, col in ((k_lo, k_hi, k_hat * kg), (v_lo, v_hi, vcol)):
                lo, hi = _split_halves(col, s, low)
                dst_lo.append(lo)
                dst_hi.append(hi)
    st = lambda parts: jnp.stack(parts, axis=0)
    return st(k_lo), st(k_hi), st(v_lo), st(v_hi), hats


def _swa_mask(first_block):
    qi = lax.broadcasted_iota(jnp.int32, (BLK, 2 * BLK), 0) + BLK
    ki = lax.broadcasted_iota(jnp.int32, (BLK, 2 * BLK), 1)
    rel = qi - ki
    m = (rel >= 0) & (rel < BLK) & (jnp.logical_not(first_block) | (ki >= BLK))
    return jnp.concatenate([m, m], axis=0)


def _sink_cols(sk_ref, hi):
    top = lax.broadcasted_iota(jnp.int32, (2 * BLK, 1), 0) < BLK
    return jnp.stack([jnp.where(top, sk_ref[0, GROUP * hk + hi], sk_ref[0, GROUP * hk + 2 + hi])
                      for hk in range(KV_HEADS)], axis=0)


def _swa_probs(qn, k_half, sink, mask):
    s = jnp.where(mask, _dot(qn, k_half, BNT) * (B_HD ** -0.5), NEG)
    m = jnp.maximum(jnp.max(s, axis=-1, keepdims=True), sink)
    p = jnp.exp(s - m)
    ps = jnp.exp(sink - m)
    inv = 1.0 / (jnp.sum(p, axis=-1, keepdims=True) + ps)
    return p * inv, ps * inv


def _swa_fwd(proj, q_gain, k_gain, sinks, job=None):
    T = proj.shape[0]
    nb = T // BLK

    def body(q_ref, kc_ref, kp_ref, vc_ref, vp_ref, qg_ref, kg_ref, sk_ref, o_ref):
        low = _low_half()
        mask = _swa_mask(pl.program_id(0) == 0)
        qn = _half_rms(_stack_cols(q_ref[...]), low)[0] * qg_ref[...]
        k_lo, k_hi, v_lo, v_hi, _ = _swa_keys(kp_ref, kc_ref, vp_ref, vc_ref, kg_ref[...], low)
        p_lo, _ = _swa_probs(qn, k_lo, _sink_cols(sk_ref, 0), mask)
        p_hi, _ = _swa_probs(qn, k_hi, _sink_cols(sk_ref, 1), mask)
        o = (_dot(p_lo, v_lo, BNN) + _dot(p_hi, v_hi, BNN)).astype(BF16)
        for c in range(Q_COLS):
            o_ref[:, c * LANES:(c + 1) * LANES] = _col_of(o, c)

    q_gain, k_gain = jnp.tile(q_gain, (1, 2)), jnp.tile(k_gain, (1, 2))
    cur = lambda w, off: pl.BlockSpec((BLK, w), lambda i: (i, off // w))
    prev = lambda w, off: pl.BlockSpec((BLK, w), lambda i: (jnp.maximum(i - 1, 0), off // w))
    small = lambda n: pl.BlockSpec((1, 2 * n), lambda i: (0, 0))
    return _pcall(
        body, grid=(nb,),
        in_specs=[cur(BW, OFF_QB), cur(KVW, OFF_KB), prev(KVW, OFF_KB), cur(KVW, OFF_VB), prev(KVW, OFF_VB),
                  small(B_HD), small(B_HD), pl.BlockSpec(memory_space=pltpu.SMEM)],
        out_specs=[pl.BlockSpec((BLK, BW), lambda i: (i, 0))],
        out_shape=[jax.ShapeDtypeStruct((T, BW), BF16)], scratch_shapes=[], name="swa_fwd", semantics=("parallel",),
        args=[proj, proj, proj, proj, proj, q_gain, k_gain, sinks], job=job)


def _swa_bwd(proj, dout, q_gain, k_gain, sinks, job=None):
    T = proj.shape[0]
    nb = T // BLK
    W = BW + 2 * KVW

    def body(q_ref, kc_ref, kp_ref, vc_ref, vp_ref, do_ref, qg_ref, kg_ref, sk_ref,
             dq_ref, dkv_ref, pqg_ref, pkg_ref, psk_ref, dkn_c, dv_c):
        i = pl.program_id(0)
        live = i < nb
        low = _low_half()
        high = jnp.logical_not(low)
        qg, kg = qg_ref[...], kg_ref[...]
        mask = _swa_mask(i == 0)
        lane = lax.broadcasted_iota(jnp.int32, (1, LANES), 1)
        scale = B_HD ** -0.5

        @pl.when(i == 0)
        def _():
            dkn_c[...] = jnp.zeros_like(dkn_c)
            dv_c[...] = jnp.zeros_like(dv_c)

        q_hat, q_r = _half_rms(_stack_cols(q_ref[...]), low)
        qn = q_hat * qg
        k_lo, k_hi, v_lo, v_hi, hats = _swa_keys(kp_ref, kc_ref, vp_ref, vc_ref, kg, low)
        do = _stack_cols(do_ref[...])
        dqn = jnp.zeros((KV_HEADS, 2 * BLK, LANES), F32)
        acc_sk = jnp.zeros((1, LANES), F32)
        dk_parts, dv_parts = [], []
        for hi, (k_h, v_h) in enumerate(((k_lo, v_lo), (k_hi, v_hi))):
            p, ps = _swa_probs(qn, k_h, _sink_cols(sk_ref, hi), mask)
            dp = _dot(do, v_h, BNT)
            delta = jnp.sum(p * dp, axis=-1, keepdims=True)
            ds = p * (dp - delta) * scale
            dqn = dqn + _dot(ds, k_h, BNN)
            dk_parts.append(_dot(ds, qn, BTN))
            dv_parts.append(_dot(p, do, BTN))
            t = ps * delta
            for hk in range(KV_HEADS):
                for rows in range(2):
                    h = GROUP * hk + 2 * rows + hi
                    acc_sk = acc_sk + jnp.where(
                        lane == h, -jnp.sum(t[hk, rows * BLK:(rows + 1) * BLK], axis=0, keepdims=True), 0.0)
        dqh = dqn * qg
        dq = (q_r * (dqh - q_hat * (_half_sum(dqh * q_hat, low) * (1.0 / B_HD)))).astype(BF16)
        for c in range(Q_COLS):
            dq_ref[:, c * LANES:(c + 1) * LANES] = _col_of(dq, c)
        acc_qg = _fold_halves(_fold8((dqn * q_hat).reshape(KV_HEADS * 2 * BLK, LANES)), low)

        def native(parts, j):
            lo_arr, hi_arr = parts
            a, b = 2 * j, 2 * j + 1
            return (jnp.where(low, lo_arr[a], 0.0) + pltpu.roll(jnp.where(high, hi_arr[a], 0.0), B_HD, 1)
                    + jnp.where(high, hi_arr[b], 0.0) + pltpu.roll(jnp.where(low, lo_arr[b], 0.0), B_HD, 1))

        acc_kg = jnp.zeros((8, LANES), F32)
        for j in range(KVW // LANES):
            cs = slice(j * LANES, (j + 1) * LANES)
            dkn = jnp.where(live, native(dk_parts, j), 0.0)
            dvc = jnp.where(live, native(dv_parts, j), 0.0)
            kp_hat, kp_r = hats[j][0][:BLK], hats[j][1][:BLK]
            dkn_prev = dkn_c[:, cs] + dkn[:BLK]
            dv_prev = dv_c[:, cs] + dvc[:BLK]
            acc_kg = acc_kg + _fold8(dkn_prev * kp_hat)
            dkh = dkn_prev * kg
            dkv_ref[:, cs] = (kp_r * (dkh - kp_hat * (_half_sum(dkh * kp_hat, low) * (1.0 / B_HD)))).astype(BF16)
            dkv_ref[:, KVW + j * LANES:KVW + (j + 1) * LANES] = dv_prev.astype(BF16)
            dkn_c[:, cs] = dkn[BLK:]
            dv_c[:, cs] = dvc[BLK:]
        keep = jnp.where(i > 0, 1.0, 0.0)
        pqg_ref[...] = jnp.where(live, acc_qg, 0.0)
        pkg_ref[...] = _fold_halves(acc_kg, low) * keep
        psk_ref[...] = jnp.broadcast_to(jnp.where(live, acc_sk, 0.0), (8, LANES)) * (
            lax.broadcasted_iota(jnp.int32, (8, LANES), 0) == 0).astype(F32)

    q_gain, k_gain = jnp.tile(q_gain, (1, 2)), jnp.tile(k_gain, (1, 2))
    last = nb - 1
    cur = lambda w, off: pl.BlockSpec((BLK, w), lambda i: (jnp.minimum(i, last), off // w))
    prev = lambda w, off: pl.BlockSpec((BLK, w), lambda i: (jnp.maximum(i - 1, 0), off // w))
    small = lambda n: pl.BlockSpec((1, 2 * n), lambda i: (0, 0))
    part = pl.BlockSpec((8, 128), lambda i: (i, 0))
    p_shape = jax.ShapeDtypeStruct(((nb + 1) * 8, 128), F32)
    return _pcall(
        body, grid=(nb + 1,),
        in_specs=[cur(BW, OFF_QB), cur(KVW, OFF_KB), prev(KVW, OFF_KB), cur(KVW, OFF_VB), prev(KVW, OFF_VB),
                  pl.BlockSpec((BLK, BW), lambda i: (jnp.minimum(i, last), 0)), small(B_HD), small(B_HD),
                  pl.BlockSpec(memory_space=pltpu.SMEM)],
        out_specs=[pl.BlockSpec((BLK, BW), lambda i: (i, 0)),
                   pl.BlockSpec((BLK, 2 * KVW), lambda i: (jnp.maximum(i - 1, 0), 0)), part, part, part],
        out_shape=[jax.ShapeDtypeStruct((T + BLK, BW), BF16), jax.ShapeDtypeStruct((T, 2 * KVW), BF16),
                   p_shape, p_shape, p_shape],
        scratch_shapes=[pltpu.VMEM((BLK, KVW), F32), pltpu.VMEM((BLK, KVW), F32)], name="swa_bwd",
        semantics=("arbitrary",), args=[proj, proj, proj, proj, proj, dout, q_gain, k_gain, sinks], job=job)


def _branch_merge(ya_pre, attn, wa_t, wb_t, proj, tm, tn, job=None):
    T = ya_pre.shape[0]

    def body(a_ref, b_ref, wa_ref, wb_ref, ga_ref, gb_ref, ya_ref, yb_ref, mg_ref):
        ya = lax.dot_general(a_ref[...], wa_ref[...], NT, preferred_element_type=F32)
        yb = lax.dot_general(b_ref[...], wb_ref[...], NT, preferred_element_type=F32)
        ya_ref[...] = ya.astype(BF16)
        yb_ref[...] = yb.astype(BF16)
        mg_ref[...] = (_sigmoid(ga_ref[...]) * ya + _sigmoid(gb_ref[...]) * yb).astype(BF16)

    o_spec = pl.BlockSpec((tm, tn), lambda i, j: (i, j))
    o_shape = jax.ShapeDtypeStruct((T, D), BF16)
    return _pcall(
        body, grid=(T // tm, D // tn),
        in_specs=[pl.BlockSpec((tm, AW), lambda i, j: (i, 0)), pl.BlockSpec((tm, BW), lambda i, j: (i, 0)),
                  pl.BlockSpec((tn, AW), lambda i, j: (j, 0)), pl.BlockSpec((tn, BW), lambda i, j: (j, 0)),
                  pl.BlockSpec((tm, tn), lambda i, j: (i, OFF_GTA // tn + j)),
                  pl.BlockSpec((tm, tn), lambda i, j: (i, OFF_GTB // tn + j))],
        out_specs=[o_spec, o_spec, o_spec], out_shape=[o_shape, o_shape, o_shape], scratch_shapes=[], name="branch_merge",
        semantics=("parallel", "parallel"), args=[ya_pre, attn, wa_t, wb_t, proj, proj], job=job)


def _ij(i, j, k):
    return (i, j)


def _local_step(x, tgt, mod, g1, g2, lbl, og, qg, kg, sk, shards, c_arr):
    win_s, wa_s, wb_s, wout_s, wmi_s, wmo_s = shards
    T = x.shape[0]
    tm, tr, tt = min(1024, T), min(256, T), min(2048, T)
    tk_t = min(1024, T)
    tn = 512
    sh1, sc1, gt1, sh2, sc2, gt2 = (mod[:, i * D:(i + 1) * D] for i in range(N_MOD))
    nI = T // tm
    blk = (tm, tn)

    h = _rms_mod_fwd("rms1_fwd", x, g1, sc1, sh1, tr)

    def epi_store(acc, ex, ou):
        ou[0][...] = acc.astype(ou[0].dtype)

    tm2 = min(2048, T)
    blk2 = (tm2, tn)

    full = lambda s: (0, s.shape[0])
    last = wmi_s.shape[0]
    gather = _gather_relay_job
    (win_t,) = _run_job("gather_w_in", gather([win_s], alone=True))
    (proj,), (wa_t, wb_t, w_out, wmi_part) = _mm(
        "in_proj", "nt", [(h, D)], win_t, T, IN_W, D, tm2, tn, D, [], [((T, IN_W), F32, blk2, _ij)], epi_store,
        job=gather([wa_s, wb_s, wout_s, wmi_s], rows=[full(wa_s), full(wb_s), full(wout_s), (0, MI_CUTS[0])]))
    (ya_pre, st), (wmi_part,) = _hgrn_fwd(
        proj, lbl, og, tt, job=gather([wmi_s], rows=[MI_CUTS], into=[wmi_part]))
    (attn,), (wmi_t, wmo_part) = _swa_fwd(
        proj, qg, kg, sk, job=gather([wmi_s, wmo_s], rows=[(MI_CUTS[1], last), (0, MO_CUT)], into=[wmi_part, None]))
    (ya, yb, merged), _ = _branch_merge(ya_pre, attn, wa_t, wb_t, proj, tm, tn)

    def residual_rows(acc, vecs, bufs, parts):
        gt, gain, sc, sh = (v[...] for v in vecs)
        x_buf, mo_buf, h2_buf = bufs
        rows = min(64, tm)
        for r0 in range(0, tm, rows):
            rs = slice(r0, r0 + rows)
            z = acc[rs, :]
            mo_buf[rs, :] = z.astype(BF16)
            x1v = x_buf[rs, :] + gt * z
            x_buf[rs, :] = x1v
            rstd = lax.rsqrt(jnp.mean(x1v * x1v, axis=-1, keepdims=True) + EPS)
            h2_buf[rs, :] = ((x1v * rstd * gain) * (1.0 + sc) + sh).astype(BF16)

    x1, mo, h2 = _rows_mm("out_proj", [(merged, D)], w_out, T, tm, min(1024, D), [gt1, g2, sc2, sh2],
                          [(F32, x, True), (BF16, None, True), (BF16, None, True)], [], residual_rows)

    def epi_relu2(acc, ex, ou):
        r = jnp.maximum(acc, 0.0)
        ou[0][...] = r.astype(BF16)
        ou[1][...] = (r * r).astype(BF16)

    (r, a), (w_mo,) = _mm("mlp_in", "nt", [(h2, D)], wmi_t, T, HID, D, tm2, tn, D, [],
                          [((T, HID), BF16, blk2, _ij), ((T, HID), BF16, blk2, _ij)], epi_relu2,
                          job=gather([wmo_s], rows=[(MO_CUT, last)], into=[wmo_part]))

    def loss_rows(acc, vecs, bufs, parts):
        gt = vecs[0][...]
        x1_buf, t_buf, dz_buf = bufs
        rows = min(64, tm)
        loss_sum, gate_sum = jnp.zeros((8, D), F32), jnp.zeros((8, D), F32)
        for r0 in range(0, tm, rows):
            rs = slice(r0, r0 + rows)
            z = acc[rs, :]
            e = x1_buf[rs, :] + gt * z - t_buf[rs, :]
            dy = e * (1.0 / D)
            t_buf[rs, :] = dy
            dz_buf[rs, :] = (gt * dy).astype(BF16)
            loss_sum = loss_sum + _fold8(e * e)
            gate_sum = gate_sum + _fold8(dy * z)
        parts[0][...] = loss_sum * (0.5 / D)
        parts[1][...] = gate_sum

    part_rows = ((nI * 8, D), F32, (8, D), lambda i, j, k: (i, 0))
    dy, dz, p_loss, p_gt2 = _rows_mm(
        "mlp_out", [(a, HID)], w_mo, T, tm, 1024, [gt2], [(F32, x1, False), (F32, tgt, True), (BF16, None, True)],
        [part_rows, part_rows], loss_rows)

    def epi_du(acc, ex, ou):
        ou[0][...] = (acc * (2.0 * ex[0][...].astype(F32))).astype(BF16)

    (du,) = _mm("mlp_out_dx", "nt", [(dz, D)], w_mo, T, HID, D, tm2, tn, D, [(r, blk2, _ij)],
                [((T, HID), BF16, blk2, _ij)], epi_du)
    gblk = (1024, 1024)
    gwide = (1024, D)
    pair_sum = lambda nm, g, r1: _pair_sum("pair_sum_" + nm, g, r1, c_arr, _sum_rows(r1.shape[1]))
    (g_mo,) = _mm("mlp_out_dw", "tn", [(a, HID)], dz, HID, D, T, 1024, D, tk_t, [], [((HID, D), BF16, gwide, _ij)], epi_store)
    (dh2,), (r1_mo,) = _mm("mlp_in_dx", "nn", [(du, HID)], wmi_t, T, D, HID, tm, D, 1024, [],
                           [((T, D), F32, (tm, D), _ij)], epi_store, job=_pair_job([g_mo]))
    dx1, p_sh2, p_sc2, p_g2, dmo, p_gt1 = _rms_mod_bwd("rms2_bwd", dh2, x1, g2, sc2, dy, tr, gate=gt1, mo=mo)
    s_mo = pair_sum("mlp_out", g_mo, r1_mo)
    near, far = (1, 2), (3,)
    (g_mi,), (rn_mo,) = _mm("mlp_in_dw", "tn", [(du, HID)], h2, HID, D, T, 1024, D, tk_t, [],
                            [((HID, D), BF16, gwide, _ij)], epi_store, job=_chip_job([s_mo], near))

    def epi_gates(acc, ex, ou):
        ya_ref, yb_ref, ga_ref, gb_ref = ex
        sa, sb = _sigmoid(ga_ref[...]), _sigmoid(gb_ref[...])
        ou[0][...] = (acc * sa).astype(BF16)
        ou[1][...] = (acc * sb).astype(BF16)
        ou[2][...] = (acc * ya_ref[...].astype(F32) * (sa * (1.0 - sa))).astype(BF16)
        ou[3][...] = (acc * yb_ref[...].astype(F32) * (sb * (1.0 - sb))).astype(BF16)

    o_bf = ((T, D), BF16, blk, _ij)
    (dya, dyb, dga, dgb), (rf_mo, r1_mi) = _mm(
        "out_proj_dx", "nt", [(dmo, D)], w_out, T, D, D, tm, tn, D,
        [(ya, blk, _ij), (yb, blk, _ij), (proj, blk, lambda i, j, k: (i, OFF_GTA // tn + j)),
         (proj, blk, lambda i, j, k: (i, OFF_GTB // tn + j))], [o_bf, o_bf, o_bf, o_bf], epi_gates,
        job=_both(_chip_job([s_mo], far), _pair_job([g_mi])))
    s_mi = pair_sum("mlp_in", g_mi, r1_mi)
    (g_out,) = _mm("out_proj_dw", "tn", [(merged, D)], dmo, D, D, T, 1024, 1024, tk_t, [], [((D, D), BF16, gblk, _ij)], epi_store)
    dya_pre, dattn = _twin_mm("branch_dx", "nn", [(dya, wa_t), (dyb, wb_t)], T, AW, D, tm, tn, D, F32)
    g_a, g_b = _twin_mm("branch_dw", "tn", [(dya, ya_pre), (dyb, attn)], D, AW, T, 1024, 1024, tk_t, BF16)
    (dqa, dfa, dia, dgg, p_lb, p_og), (rn_mi, r1_out, r1_a, r1_b) = _hgrn_bwd(
        proj, st, dya_pre, lbl, og, tt, job=_both(_chip_job([s_mi], near), _pair_job([g_out, g_a, g_b])))
    (dqb, dkv, p_qg, p_kg, p_sk), (rf_mi,) = _swa_bwd(proj, dattn, qg, kg, sk, job=_chip_job([s_mi], far))
    s_out, s_a, s_b = pair_sum("out", g_out, r1_out), pair_sum("branch_a", g_a, r1_a), pair_sum("branch_b", g_b, r1_b)
    pieces = [(dqa, AW), (dfa, AW), (dia, AW), (dgg, AW), (dqb, BW), (dkv, 2 * KVW), (dga, D), (dgb, D)]
    (g_in,), (r2_out, r2_a, r2_b) = _pieces_tn("in_proj_dw", pieces, h, 512, job=_chip_job([s_out, s_a, s_b]))
    (r1_in,) = _run_job("pair_w_in", _pair_job([g_in]))
    s_in = pair_sum("in", g_in, r1_in)
    (dx, p_sh1, p_sc1, p_g1), (r2_in,) = _pieces_nn_rms(
        "in_proj_dx", pieces, win_t, x, g1, sc1, dx1, tm, 512, job=_chip_job([s_in]))

    partials = dict(sh1=p_sh1, sc1=p_sc1, gt1=p_gt1, sh2=p_sh2, sc2=p_sc2, gt2=p_gt2, g1=p_g1, g2=p_g2,
                    lb=p_lb, og=p_og, qg=p_qg, kg=p_kg, sk=p_sk, loss=p_loss)
    sums = dict(w_in=(s_in, [r2_in]), w_branch_a=(s_a, [r2_a]), w_branch_b=(s_b, [r2_b]), w_out=(s_out, [r2_out]),
                w_mlp_in=(s_mi, [rn_mi, rf_mi]), w_mlp_out=(s_mo, [rn_mo, rf_mo]))
    return dx, sums, partials


def _exchange_slots(buf, send_sems, recv_sems):
    me = _mesh_pos()
    mine = buf.at[_index(me)]
    sends = []
    for k in range(1, N_DEV):
        cp = pltpu.make_async_remote_copy(src_ref=mine, dst_ref=mine, send_sem=send_sems.at[k - 1],
                                          recv_sem=recv_sems.at[k - 1], device_id=_flip(me, k), device_id_type=MESH)
        cp.start()
        sends.append(cp)
    for k in range(1, N_DEV):
        theirs = buf.at[_index(_flip(me, k))]
        pltpu.make_async_remote_copy(src_ref=theirs, dst_ref=theirs, send_sem=send_sems.at[k - 1],
                                     recv_sem=recv_sems.at[k - 1], device_id=_flip(me, k), device_id_type=MESH).wait_recv()
    for cp in sends:
        cp.wait_send()


ADA_W = N_MOD * D // N_DEV


def _ada_mod(c, w_ada, b_shard):
    def body(c_ref, w_ref, b_ref, mod_ref, sc_ref, cbuf, mbuf, s1, r1, s2, r2):
        me = _index(_mesh_pos())
        cbuf[me] = c_ref[...]
        _exchange_slots(cbuf, s1, r1)
        row = lax.broadcasted_iota(jnp.int32, (N_DEV, D), 0)
        call = jnp.zeros((N_DEV, D), F32)
        for d in range(N_DEV):
            call = jnp.where(row == d, cbuf[d], call)
        sc = call * _sigmoid(call)
        sc_ref[...] = sc
        mbuf[me] = _dot(sc, w_ref[...]) + b_ref[...]
        _exchange_slots(mbuf, s2, r2)
        for s in range(N_DEV):
            mod_ref[:, s * ADA_W:(s + 1) * ADA_W] = mbuf[s, pl.ds(me, 1), :]

    return pl.pallas_call(
        body, in_specs=[_VMEM, _VMEM, _VMEM], out_specs=[_VMEM, _VMEM],
        out_shape=[jax.ShapeDtypeStruct((1, N_MOD * D), F32), jax.ShapeDtypeStruct((N_DEV, D), F32)],
        scratch_shapes=[pltpu.VMEM((N_DEV, 1, D), F32), pltpu.VMEM((N_DEV, N_DEV, ADA_W), F32),
                        _SEMS(N_DEV - 1), _SEMS(N_DEV - 1), _SEMS(N_DEV - 1), _SEMS(N_DEV - 1)],
        name="ada_mod", compiler_params=pltpu.CompilerParams(vmem_limit_bytes=VMEM_LIMIT),
    )(c, w_ada, b_shard)


SMALL_SEGS = (("b_ada", N_MOD * D), ("norm1_gain", D), ("norm2_gain", D), ("lb0", AW), ("lb1", AW),
              ("hgrn_o_gain", AW), ("q_norm_gain", 128), ("k_norm_gain", 128), ("sinks", 128))
SMALL_W = sum(w for _, w in SMALL_SEGS)
X_SEGS = (("sh1", D), ("sc1", D), ("gt1", D), ("sh2", D), ("sc2", D), ("gt2", D), ("g1", D), ("g2", D),
          ("lb", AW), ("og", AW), ("qg", 128), ("kg", 128), ("sk", 128), ("loss", 128))
X_W = sum(w for _, w in X_SEGS)


def _offsets(segs):
    out, o = {}, 0
    for name, w in segs:
        out[name] = (o, w)
        o += w
    return out


def _small_reduce(parts, lb_logits):
    xo, so = _offsets(X_SEGS), _offsets(SMALL_SEGS)
    names = [nm for nm, _ in X_SEGS]

    def body(*refs):
        p_refs = dict(zip(names, refs[:len(names)]))
        lbl_ref, allx, gs_ref, loss_ref, send_sems, recv_sems = refs[len(names):]
        me = _index(_mesh_pos())
        for nm, (o, w) in xo.items():
            if nm == "loss":
                allx[me, :, o:o + w] = jnp.broadcast_to(jnp.sum(p_refs[nm][...]), (1, w))
            else:
                allx[me, :, o:o + w] = jnp.sum(p_refs[nm][...], axis=0, keepdims=True)
        _exchange_slots(allx, send_sems, recv_sems)
        tot = allx[0]
        for d in range(1, N_DEV):
            tot = tot + allx[d]
        seg = lambda nm: tot[:, xo[nm][0]:xo[nm][0] + xo[nm][1]]

        def put(nm, v):
            gs_ref[:, so[nm][0]:so[nm][0] + so[nm][1]] = v

        put("b_ada", tot[:, 0:N_MOD * D])
        put("norm1_gain", seg("g1"))
        put("norm2_gain", seg("g2"))
        lbl = lbl_ref[...]
        lb = _sigmoid(lbl[0:1, :] - lbl[1:2, :])
        dl0 = seg("lb") * lb * (1.0 - lb)
        put("lb0", dl0)
        put("lb1", -dl0)
        put("hgrn_o_gain", seg("og"))
        put("q_norm_gain", seg("qg"))
        put("k_norm_gain", seg("kg"))
        put("sinks", seg("sk"))
        loss_ref[...] = seg("loss")

    return pl.pallas_call(
        body, in_specs=[_VMEM] * (len(names) + 1), out_specs=[_VMEM, _VMEM, _VMEM],
        out_shape=[jax.ShapeDtypeStruct((N_DEV, 1, X_W), F32), jax.ShapeDtypeStruct((1, SMALL_W), F32),
                   jax.ShapeDtypeStruct((1, 128), F32)],
        scratch_shapes=[_SEMS(N_DEV - 1), _SEMS(N_DEV - 1)], name="small_reduce",
        compiler_params=pltpu.CompilerParams(vmem_limit_bytes=VMEM_LIMIT),
    )(*[parts[nm] for nm in names], lb_logits)


def _adamw_math(w, g, m, v):
    m = B1 * m + (1.0 - B1) * g
    v = B2 * v + (1.0 - B2) * (g * g)
    m_hat = m / (1.0 - B1 ** STEP)
    v_hat = v / (1.0 - B2 ** STEP)
    return -LR * (m_hat / (jnp.sqrt(v_hat) + ADAM_EPS) + WD * w), m, v


def _sum_rows(rs):
    return 256 if rs % 256 == 0 else rs // 2


def _pair_sum(name, g, recv, c_arr, tr):
    _, rs, cols = recv.shape
    blk = (1, tr, cols)

    def body(c_ref, g_ref, r_ref, o_ref):
        o_ref[...] = (g_ref[...].astype(F32) + r_ref[...].astype(F32)).astype(BF16)

    grid_spec = pltpu.PrefetchScalarGridSpec(
        num_scalar_prefetch=1, grid=(4, rs // tr),
        in_specs=[pl.BlockSpec(blk, lambda q, i, c: (2 * q + c[0], i, 0)), pl.BlockSpec(blk, lambda q, i, c: (q, i, 0))],
        out_specs=pl.BlockSpec(blk, lambda q, i, c: (q, i, 0)))
    return pl.pallas_call(body, grid_spec=grid_spec, out_shape=jax.ShapeDtypeStruct((4, rs, cols), BF16), name=name,
                          compiler_params=_params(("parallel", "parallel")))(c_arr, g.reshape(N_DEV, rs, cols), recv)


def _sum_adamw(name, sums, recvs, q_arr, w, m, v, transposed, tile):
    rows, cols = w.shape
    nR = len(recvs)

    def body(q_ref, s_ref, *refs):
        r_refs = refs[:nR]
        w_ref, m_ref, v_ref, g_ref, d_ref, nm_ref, nv_ref = refs[nR:]
        g = s_ref[0].astype(F32)
        for r_ref in r_refs:
            for slot in range(r_ref.shape[0]):
                g = g + r_ref[slot].astype(F32)
        g = g.T if transposed else g
        g_ref[...] = g
        d_ref[...], nm_ref[...], nv_ref[...] = _adamw_math(w_ref[...], g, m_ref[...], v_ref[...])

    if transposed:
        slab = lambda n, first: pl.BlockSpec((n, cols, tile), lambda i, q: (first(q), 0, i))
    else:
        slab = lambda n, first: pl.BlockSpec((n, tile, cols), lambda i, q: (first(q), i, 0))
    spec = pl.BlockSpec((tile, cols), lambda i, q: (i, 0))
    shape = jax.ShapeDtypeStruct((rows, cols), F32)
    grid_spec = pltpu.PrefetchScalarGridSpec(
        num_scalar_prefetch=1, grid=(rows // tile,),
        in_specs=[slab(1, lambda q: q[0])] + [slab(r.shape[0], lambda q: 0) for r in recvs] + [spec] * 3,
        out_specs=[spec] * 4)
    return pl.pallas_call(body, grid_spec=grid_spec, out_shape=[shape] * 4, name=name,
                          compiler_params=_params(("parallel",)))(q_arr, sums, *recvs, w, m, v)


def _adamw(name, w, g, m, v, tr):
    rows, cols = w.shape

    def body(w_ref, g_ref, m_ref, v_ref, d_ref, nm_ref, nv_ref):
        d_ref[...], nm_ref[...], nv_ref[...] = _adamw_math(w_ref[...], g_ref[...], m_ref[...], v_ref[...])

    spec = pl.BlockSpec((tr, cols), lambda i: (i, 0))
    shape = jax.ShapeDtypeStruct((rows, cols), F32)
    return pl.pallas_call(
        body, grid=(rows // tr,), in_specs=[spec] * 4, out_specs=[spec] * 3, out_shape=[shape] * 3, name=name,
        compiler_params=_params(("parallel",)),
    )(w, g, m, v)


def _ada_update(sc_t, dmod_cols, w, m, v, tr):
    rows, cols = w.shape

    def body(s_ref, d_ref, w_ref, m_ref, v_ref, g_ref, dl_ref, nm_ref, nv_ref):
        g = jnp.dot(s_ref[...], d_ref[...], precision=lax.Precision.HIGHEST, preferred_element_type=F32)
        g_ref[...] = g
        dl_ref[...], nm_ref[...], nv_ref[...] = _adamw_math(w_ref[...], g, m_ref[...], v_ref[...])

    spec = pl.BlockSpec((tr, cols), lambda i: (i, 0))
    shape = jax.ShapeDtypeStruct((rows, cols), F32)
    return pl.pallas_call(
        body, grid=(rows // tr,),
        in_specs=[pl.BlockSpec((tr, N_DEV), lambda i: (i, 0)), pl.BlockSpec((N_DEV, cols), lambda i: (0, 0)), spec, spec, spec],
        out_specs=[spec] * 4, out_shape=[shape] * 4, name="ada_update", compiler_params=_params(("parallel",)),
    )(sc_t, dmod_cols, w, m, v)


BIG = ("w_in", "w_branch_a", "w_branch_b", "w_out", "w_mlp_in", "w_mlp_out")
COLUMN_SHARDED = ("w_in", "w_branch_a", "w_branch_b", "w_mlp_in")
AS_TRANSPOSE = ("w_in",)
WEIGHTS = ("w_ada", "b_ada", "norm1_gain", "w_in", "lb_logits", "hgrn_o_gain", "q_norm_gain", "k_norm_gain", "sinks",
           "w_branch_a", "w_branch_b", "w_out", "norm2_gain", "w_mlp_in", "w_mlp_out")


def _to_bf16(name, w, transposed, tile=256):
    rows, cols = w.shape

    def body(w_ref, o_ref):
        v = w_ref[...]
        o_ref[...] = (v.T if transposed else v).astype(BF16)

    out_spec = pl.BlockSpec((cols, tile), lambda i: (0, i)) if transposed else pl.BlockSpec((tile, cols), lambda i: (i, 0))
    return pl.pallas_call(
        body, grid=(rows // tile,), in_specs=[pl.BlockSpec((tile, cols), lambda i: (i, 0))], out_specs=out_spec,
        out_shape=jax.ShapeDtypeStruct((cols, rows) if transposed else (rows, cols), BF16), name=name,
        compiler_params=_params(("parallel",)))(w)


def _pack_small(p):
    lb = p["lb_logits"]
    src = dict(p, lb0=lb[0:1], lb1=lb[1:2])
    return jnp.concatenate([jnp.pad(src[nm], ((0, 0), (0, w - src[nm].shape[1]))) for nm, w in SMALL_SEGS], axis=1)


def _unpack_small(vec, shapes):
    so = _offsets(SMALL_SEGS)
    out = {}
    for nm, shp in shapes.items():
        if nm == "lb_logits":
            o = so["lb0"][0]
            out[nm] = vec[0, o:o + 2 * AW].reshape(2, AW)
        else:
            o = so[nm][0]
            out[nm] = vec[:, o:o + shp[1]]
    return out


def kernel(x, c, w_ada, b_ada, norm1_gain, w_in, lb_logits, hgrn_o_gain, q_norm_gain, k_norm_gain, sinks, w_branch_a, w_branch_b, w_out, norm2_gain, w_mlp_in, w_mlp_out, loss_target, m_w_ada, m_b_ada, m_norm1_gain, m_w_in, m_lb_logits, m_hgrn_o_gain, m_q_norm_gain, m_k_norm_gain, m_sinks, m_w_branch_a, m_w_branch_b, m_w_out, m_norm2_gain, m_w_mlp_in, m_w_mlp_out, v_w_ada, v_b_ada, v_norm1_gain, v_w_in, v_lb_logits, v_hgrn_o_gain, v_q_norm_gain, v_k_norm_gain, v_sinks, v_w_branch_a, v_w_branch_b, v_w_out, v_norm2_gain, v_w_mlp_in, v_w_mlp_out):
    w = dict(w_ada=w_ada, b_ada=b_ada, norm1_gain=norm1_gain, w_in=w_in, lb_logits=lb_logits, hgrn_o_gain=hgrn_o_gain,
             q_norm_gain=q_norm_gain, k_norm_gain=k_norm_gain, sinks=sinks, w_branch_a=w_branch_a, w_branch_b=w_branch_b,
             w_out=w_out, norm2_gain=norm2_gain, w_mlp_in=w_mlp_in, w_mlp_out=w_mlp_out)
    m = dict(w_ada=m_w_ada, b_ada=m_b_ada, norm1_gain=m_norm1_gain, w_in=m_w_in, lb_logits=m_lb_logits,
             hgrn_o_gain=m_hgrn_o_gain, q_norm_gain=m_q_norm_gain, k_norm_gain=m_k_norm_gain, sinks=m_sinks,
             w_branch_a=m_w_branch_a, w_branch_b=m_w_branch_b, w_out=m_w_out, norm2_gain=m_norm2_gain,
             w_mlp_in=m_w_mlp_in, w_mlp_out=m_w_mlp_out)
    v = dict(w_ada=v_w_ada, b_ada=v_b_ada, norm1_gain=v_norm1_gain, w_in=v_w_in, lb_logits=v_lb_logits,
             hgrn_o_gain=v_hgrn_o_gain, q_norm_gain=v_q_norm_gain, k_norm_gain=v_k_norm_gain, sinks=v_sinks,
             w_branch_a=v_w_branch_a, w_branch_b=v_w_branch_b, w_out=v_w_out, norm2_gain=v_norm2_gain,
             w_mlp_in=v_w_mlp_in, w_mlp_out=v_w_mlp_out)
    for d in (w, m, v):
        for nm in ("w_ada",) + BIG:
            d[nm] = d[nm][0]
    px, py, pc = _mesh_pos()
    me = _index((px, py, pc))
    c_arr = jnp.reshape(pc, (1,)).astype(jnp.int32)
    q_arr = jnp.reshape(2 * px + py, (1,)).astype(jnp.int32)

    shards = [_to_bf16("shard_" + nm, w[nm].T, False, w[nm].shape[1] // 4) if nm in AS_TRANSPOSE else
              _to_bf16("shard_" + nm, w[nm], nm in COLUMN_SHARDED) for nm in BIG]
    b_shard = lax.dynamic_slice(b_ada, (0, me * ADA_W), (1, ADA_W))
    mod, sc_all = _ada_mod(c, w["w_ada"], b_shard)

    dx, sums, parts = _local_step(x[0], loss_target[0], mod, norm1_gain, norm2_gain, lb_logits, hgrn_o_gain,
                                  q_norm_gain, k_norm_gain, sinks, shards, c_arr)

    allx, g_small, loss = _small_reduce(parts, lb_logits)

    grad, delta, new_m, new_v = {}, {}, {}, {}
    for nm in BIG:
        s, r2 = sums[nm]
        if nm in AS_TRANSPOSE:
            res = _sum_adamw("adamw_" + nm, s, r2, q_arr, w[nm].T, m[nm].T, v[nm].T, False, w[nm].shape[1] // 4)
            grad[nm], delta[nm], new_m[nm], new_v[nm] = (t.T for t in res)
        else:
            grad[nm], delta[nm], new_m[nm], new_v[nm] = _sum_adamw(
                "adamw_" + nm, s, r2, q_arr, w[nm], m[nm], v[nm], nm in COLUMN_SHARDED, 128)

    dmod_cols = lax.dynamic_slice(allx[:, 0, :], (0, me * ADA_W), (N_DEV, ADA_W))
    grad["w_ada"], delta["w_ada"], new_m["w_ada"], new_v["w_ada"] = _ada_update(
        sc_all.T, dmod_cols, w["w_ada"], m["w_ada"], v["w_ada"], 256)

    small_names = [nm for nm in WEIGHTS if nm not in BIG and nm != "w_ada"]
    shapes = {nm: w[nm].shape for nm in small_names}
    ds, ms, vs = _adamw("adamw_small", _pack_small(w), g_small, _pack_small(m), _pack_small(v), 1)
    for dst, vec in ((grad, g_small), (delta, ds), (new_m, ms), (new_v, vs)):
        dst.update(_unpack_small(vec, shapes))

    def full(d, nm):
        return d[nm][None] if nm in BIG or nm == "w_ada" else d[nm]

    return (loss[0, 0], dx[None], *[full(grad, nm) for nm in WEIGHTS], *[full(delta, nm) for nm in WEIGHTS],
            *[full(new_m, nm) for nm in WEIGHTS], *[full(new_v, nm) for nm in WEIGHTS])
```

```python
import functools

import jax
import jax.numpy as jnp
from jax import lax
from jax.experimental import pallas as pl
from jax.experimental.pallas import tpu as pltpu

F32 = jnp.float32
BF16 = jnp.bfloat16
MESH = pl.DeviceIdType.MESH

N_DEV = 8
D = 2048
A_HEADS, A_HD, CHUNK = 8, 128, 64
AW = A_HEADS * A_HD
Q_HEADS, KV_HEADS, GROUP, B_HD, BLK = 16, 4, 4, 64, 128
BW = Q_HEADS * B_HD
KVW = KV_HEADS * B_HD
HID = 4 * D
IN_W = 4 * AW + BW + 2 * KVW + 2 * D
OFF_QA, OFF_FA, OFF_IA, OFF_GA = 0, AW, 2 * AW, 3 * AW
OFF_QB = 4 * AW
OFF_KB = OFF_QB + BW
OFF_VB = OFF_KB + KVW
OFF_GTA = OFF_VB + KVW
OFF_GTB = OFF_GTA + D
N_MOD = 6
EPS = 1e-6
LR, B1, B2, ADAM_EPS, WD, STEP = 1e-3, 0.9, 0.999, 1e-8, 0.01, 10
NEG = -1e30

VMEM_LIMIT = 56 * 1024 * 1024
MI_CUTS = (544, 864)
MO_CUT = 272

NN = (((1,), (0,)), ((), ()))
NT = (((1,), (1,)), ((), ()))
TN = (((0,), (0,)), ((), ()))
BNN = (((2,), (1,)), ((0,), (0,)))
BNT = (((2,), (2,)), ((0,), (0,)))
BTN = (((1,), (1,)), ((0,), (0,)))


def _dot(a, b, dims=NN):
    return lax.dot_general(a.astype(BF16), b.astype(BF16), dims, preferred_element_type=F32)


def _params(sem):
    return pltpu.CompilerParams(dimension_semantics=sem, vmem_limit_bytes=VMEM_LIMIT)


def _sigmoid(x):
    return 1.0 / (1.0 + jnp.exp(-x))


def _fold8(v):
    r, n = v.shape
    return jnp.sum(v.reshape(r // 8, 8, n), axis=0)


_VMEM = pl.BlockSpec(memory_space=pltpu.VMEM)
_ANY = pl.BlockSpec(memory_space=pl.ANY)
_SEMS = lambda n: pltpu.SemaphoreType.DMA((n,))


def _mesh_pos():
    return lax.axis_index("x"), lax.axis_index("y"), lax.axis_index("c")


def _flip(pos, k):
    return tuple(1 - p if (k >> s) & 1 else p for p, s in zip(pos, (2, 1, 0)))


def _index(pos):
    return 4 * pos[0] + 2 * pos[1] + pos[2]


class _Job:
    def __init__(self, ins, out_shape, sems, start, finish, aliases=None, middle=None):
        self.ins, self.out_shape, self.sems, self.start, self.finish = list(ins), list(out_shape), list(sems), start, finish
        self.aliases = dict(aliases or {})
        self.middle = middle


def _both(j1, j2):
    assert not j1.aliases and not j2.aliases
    n_in, n_out, n_sem = len(j1.ins), len(j1.out_shape), len(j1.sems)
    first = lambda ins, outs, sems: (ins[:n_in], outs[:n_out], sems[:n_sem])
    second = lambda ins, outs, sems: (ins[n_in:], outs[n_out:], sems[n_sem:])

    def start(*refs):
        j1.start(*first(*refs))
        j2.start(*second(*refs))

    def finish(*refs):
        j1.finish(*first(*refs))
        j2.finish(*second(*refs))

    return _Job(j1.ins + j2.ins, j1.out_shape + j2.out_shape, j1.sems + j2.sems, start, finish)


def _pcall(body, *, grid, in_specs, out_specs, out_shape, scratch_shapes, name, semantics, args, job=None):
    if job is None:
        outs = pl.pallas_call(body, grid=grid, in_specs=in_specs, out_specs=out_specs, out_shape=out_shape,
                              scratch_shapes=scratch_shapes, name=name, compiler_params=_params(semantics))(*args)
        return list(outs), []
    n_in, n_out, n_scr = len(in_specs), len(out_specs), len(scratch_shapes)
    j_in, j_out = len(job.ins), len(job.out_shape)
    steps = tuple(grid)

    def carrier(*refs):
        o = 0
        main_in, o = refs[o:o + n_in], o + n_in
        job_in, o = refs[o:o + j_in], o + j_in
        main_out, o = refs[o:o + n_out], o + n_out
        job_out, o = refs[o:o + j_out], o + j_out
        main_scr, job_sems = refs[o:o + n_scr], refs[o + n_scr:]
        ids = [pl.program_id(a) for a in range(len(steps))]
        first = functools.reduce(lambda p, q: p & q, [i == 0 for i in ids])
        last = functools.reduce(lambda p, q: p & q, [i == s - 1 for i, s in zip(ids, steps)])

        @pl.when(first)
        def _():
            job.start(job_in, job_out, job_sems)

        if job.middle is not None:
            flat, total = 0, 1
            for i, s in zip(ids, steps):
                flat, total = flat * s + i, total * s

            @pl.when(flat == total // 2)
            def _():
                job.middle(job_in, job_out, job_sems)

        body(*main_in, *main_out, *main_scr)

        @pl.when(last)
        def _():
            job.finish(job_in, job_out, job_sems)

    outs = pl.pallas_call(
        carrier, grid=grid, in_specs=list(in_specs) + [_ANY] * j_in, out_specs=list(out_specs) + [_ANY] * j_out,
        out_shape=list(out_shape) + job.out_shape, scratch_shapes=list(scratch_shapes) + job.sems, name=name,
        input_output_aliases={n_in + i: n_out + o for i, o in job.aliases.items()},
        compiler_params=_params(("arbitrary",) * len(steps)),
    )(*args, *job.ins)
    return list(outs[:n_out]), list(outs[n_out:])


def _run_job(name, job):
    j_in, j_out = len(job.ins), len(job.out_shape)

    def body(*refs):
        ins, outs, sems = refs[:j_in], refs[j_in:j_in + j_out], refs[j_in + j_out:]
        job.start(ins, outs, sems)
        job.finish(ins, outs, sems)

    return list(pl.pallas_call(body, in_specs=[_ANY] * j_in, out_specs=[_ANY] * j_out, out_shape=job.out_shape,
                               scratch_shapes=job.sems, name=name,
                               input_output_aliases=job.aliases)(*job.ins))


def _gather_relay_job(shards, rows=None, into=None, alone=False):
    n = len(shards)
    rows = rows or [(0, s.shape[0]) for s in shards]
    into = into or [None] * n
    olds, aliases = [], {}
    for a, buf in enumerate(into):
        if buf is not None:
            aliases[n + len(olds)] = a
            olds.append(buf)

    def tools(ins, outs, sems):
        send_sems, recv_sems, local_sems = sems
        x, y, c = _mesh_pos()
        q = 2 * x + y
        chip_at = lambda rel: (1 - x if rel & 2 else x, 1 - y if rel & 1 else y)

        def part(a, chip, core):
            rs, (r0, r1) = shards[a].shape[0], rows[a]
            return outs[a].at[pl.ds((2 * chip + core) * rs + r0, r1 - r0), :]

        own = lambda a: ins[a].at[pl.ds(rows[a][0], rows[a][1] - rows[a][0]), :]

        def copy(a, slot, chip, core, to, src=None):
            blk = part(a, chip, core)
            return pltpu.make_async_remote_copy(src_ref=blk if src is None else src, dst_ref=blk,
                                                send_sem=send_sems.at[7 * a + slot], recv_sem=recv_sems.at[7 * a + slot],
                                                device_id=to, device_id_type=MESH)

        mine = [pltpu.make_async_copy(own(a), part(a, q, c), local_sems.at[a]) for a in range(n)]
        first = [copy(a, slot, q, c, (x, y, 1 - c) if slot == 0 else (*chip_at(slot), c), src=own(a))
                 for a in range(n) for slot in (0, 1, 2)]
        return x, y, c, q, chip_at, copy, mine, first

    def start(ins, outs, sems):
        *_, mine, first = tools(ins, outs, sems)
        for cp in mine + first:
            cp.start()

    def middle(ins, outs, sems):
        x, y, c, q, chip_at, copy, _, _ = tools(ins, outs, sems)
        me, sib = (x, y, c), (x, y, 1 - c)

        def relay(src, dst):
            for a in range(n):
                copy(a, src, q ^ src, c, me).wait_recv()
                copy(a, 3, q ^ src, c, (*chip_at(dst), c)).start()
                copy(a, 3 + src, q ^ src, c, sib).start()
            for a in range(n):
                copy(a, dst, q ^ dst, c, me).wait_recv()
                copy(a, 3 + dst, q ^ dst, c, sib).start()

        pl.when(c == 1)(lambda: relay(1, 2))
        pl.when(c == 0)(lambda: relay(2, 1))

    def finish(ins, outs, sems):
        if alone:
            middle(ins, outs, sems)
        x, y, c, q, chip_at, copy, mine, first = tools(ins, outs, sems)
        me, sib = (x, y, c), (x, y, 1 - c)
        for a in range(n):
            copy(a, 3, q ^ 3, c, me).wait_recv()
            copy(a, 6, q ^ 3, c, sib).start()
        for a in range(n):
            copy(a, 0, q, 1 - c, me).wait_recv()
            for rel in (1, 2, 3):
                copy(a, 3 + rel, q ^ rel, 1 - c, me).wait_recv()
        for a in range(n):
            for slot in range(3, 7):
                copy(a, slot, q, c, sib).wait_send()
        for cp in first:
            cp.wait_send()
        for cp in mine:
            cp.wait()

    return _Job(list(shards) + olds, [jax.ShapeDtypeStruct((N_DEV * s.shape[0], s.shape[1]), s.dtype) for s in shards],
                [_SEMS(7 * n), _SEMS(7 * n), _SEMS(n)], start, finish, aliases, middle=None if alone else middle)


def _pair_job(grads):
    n = len(grads)

    def copies(ins, outs, sems):
        send_sems, recv_sems = sems
        x, y, c = _mesh_pos()
        out = []
        for a in range(n):
            rs = grads[a].shape[0] // N_DEV
            for q in range(4):
                blk = ins[a].at[pl.ds((2 * q + 1 - c) * rs, rs), :]
                out.append(pltpu.make_async_remote_copy(
                    src_ref=blk, dst_ref=outs[a].at[q], send_sem=send_sems.at[4 * a + q], recv_sem=recv_sems.at[4 * a + q],
                    device_id=(x, y, 1 - c), device_id_type=MESH))
        return out

    def start(ins, outs, sems):
        for cp in copies(ins, outs, sems):
            cp.start()

    def finish(ins, outs, sems):
        for cp in copies(ins, outs, sems):
            cp.wait()

    return _Job(grads, [jax.ShapeDtypeStruct((4, g.shape[0] // N_DEV, g.shape[1]), g.dtype) for g in grads],
                [_SEMS(4 * n), _SEMS(4 * n)], start, finish)


def _chip_job(sums, rels=(1, 2, 3)):
    n, nr = len(sums), len(rels)

    def copies(ins, outs, sems):
        send_sems, recv_sems = sems
        x, y, c = _mesh_pos()
        out = []
        for a in range(n):
            for slot, r in enumerate(rels):
                px, py = (1 - x if r & 2 else x), (1 - y if r & 1 else y)
                out.append(pltpu.make_async_remote_copy(
                    src_ref=ins[a].at[2 * px + py], dst_ref=outs[a].at[slot], send_sem=send_sems.at[nr * a + slot],
                    recv_sem=recv_sems.at[nr * a + slot], device_id=(px, py, c), device_id_type=MESH))
        return out

    def start(ins, outs, sems):
        for cp in copies(ins, outs, sems):
            cp.start()

    def finish(ins, outs, sems):
        for cp in copies(ins, outs, sems):
            cp.wait()

    return _Job(sums, [jax.ShapeDtypeStruct((nr,) + s.shape[1:], s.dtype) for s in sums],
                [_SEMS(nr * n), _SEMS(nr * n)], start, finish)


def _mm(name, form, a_list, b, M, N, K, tm, tn, tk, extras, outs, epi, job=None):
    nI, nJ, nK = M // tm, N // tn, K // tk
    assert nI * tm == M and nJ * tn == N and nK * tk == K
    dims = {"nn": NN, "nt": NT, "tn": TN}[form]
    b_list = b if isinstance(b, list) else [(b, {"nn": N, "nt": K, "tn": N}[form])]
    nA, nB = len(a_list), len(b_list)
    assert nA == 1 or nB == 1
    assert nB == 1 or form in ("nn", "nt")
    AXIS = {"i": 0, "j": 1, "k": 2}
    a_axis, a_tile = ("i", tm) if form == "tn" else ("k", tk)
    b_axis, b_tile = ("k", tk) if form == "nt" else ("j", tn)

    def cut(pieces, tile, total):
        starts, s = [], 0
        for _, w in pieces:
            assert w % tile == 0
            starts.append(s // tile)
            s += w
        assert s == total
        return starts, [w // tile for _, w in pieces]

    a_st, a_cn = cut(a_list, a_tile, M if form == "tn" else K)
    b_st, b_cn = cut(b_list, b_tile, K if form == "nt" else N)

    def inside(idx, st, cn):
        return (idx >= st) & (idx < st + cn)

    def a_spec(p):
        st, cn = a_st[p], a_cn[p]
        if form == "tn":
            return pl.BlockSpec((tk, tm), lambda i, j, k: (jnp.where(inside(i, st, cn), k, 0), jnp.clip(i - st, 0, cn - 1)))
        return pl.BlockSpec((tm, tk), lambda i, j, k: (i, jnp.clip(k - st, 0, cn - 1)))

    def b_spec(p):
        st, cn = b_st[p], b_cn[p]
        if form == "nt":
            return pl.BlockSpec((tn, tk), lambda i, j, k: (j, jnp.clip(k - st, 0, cn - 1)))
        if nB == 1:
            return pl.BlockSpec((tk, tn), lambda i, j, k: (k, j))
        return pl.BlockSpec((tk, tn), lambda i, j, k: (jnp.where(inside(j, st, cn), k, 0), jnp.clip(j - st, 0, cn - 1)))

    in_specs = ([a_spec(p) for p in range(nA)] + [b_spec(p) for p in range(nB)]
                + [pl.BlockSpec(bs, im) for _, bs, im in extras])
    out_shape = [jax.ShapeDtypeStruct(s_, d_) for s_, d_, _, _ in outs]
    out_specs = [pl.BlockSpec(bs, im) for _, _, bs, im in outs]
    nE, nO = len(extras), len(outs)
    single = nA == 1 and nB == 1

    def body(*refs):
        a_refs, b_refs = refs[:nA], refs[nA:nA + nB]
        ex, ou = refs[nA + nB:nA + nB + nE], refs[nA + nB + nE:nA + nB + nE + nO]
        ids = [pl.program_id(a) for a in range(3)]

        def partial_of(p, q):
            return lax.dot_general(a_refs[p][...], b_refs[q][...], dims, preferred_element_type=F32)

        if nK == 1 and single:
            epi(partial_of(0, 0), ex, ou)
            return
        acc = refs[-1]
        k = ids[2]
        for p in range(nA):
            for q in range(nB):
                def first(p=p, q=q):
                    acc[...] = partial_of(p, q)

                def later(p=p, q=q):
                    acc[...] += partial_of(p, q)

                here = None
                if nA > 1:
                    here = inside(ids[AXIS[a_axis]], a_st[p], a_cn[p])
                if nB > 1:
                    here = inside(ids[AXIS[b_axis]], b_st[q], b_cn[q])
                pl.when(k == 0 if here is None else here & (k == 0))(first)
                pl.when(k > 0 if here is None else here & (k > 0))(later)

        @pl.when(k == nK - 1)
        def _():
            epi(acc[...], ex, ou)

    scratch = [] if (nK == 1 and single) else [pltpu.VMEM((tm, tn), F32)]
    res, job_res = _pcall(
        body, grid=(nI, nJ, nK), in_specs=in_specs, out_specs=out_specs, out_shape=out_shape, scratch_shapes=scratch,
        name=name, semantics=("parallel", "parallel", "arbitrary"),
        args=[a for a, _ in a_list] + [p for p, _ in b_list] + [e for e, _, _ in extras], job=job)
    return res if job is None else (res, job_res)


def _twin_mm(name, form, pairs, M, N, K, tm, tn, tk, out_dtype):
    nI, nJ, nK = M // tm, N // tn, K // tk
    dims = {"nn": NN, "tn": TN}[form]
    a_spec = (pl.BlockSpec((tm, tk), lambda i, j, k: (i, k)) if form == "nn" else pl.BlockSpec((tk, tm), lambda i, j, k: (k, i)))
    b_spec = pl.BlockSpec((tk, tn), lambda i, j, k: (k, j))
    o_spec = pl.BlockSpec((tm, tn), lambda i, j, k: (i, j))

    def body(a1, b1, a2, b2, o1, o2, *accs):
        k = pl.program_id(2)
        for a_ref, b_ref, o_ref, acc in ((a1, b1, o1, accs[0] if accs else None), (a2, b2, o2, accs[1] if accs else None)):
            part = lax.dot_general(a_ref[...], b_ref[...], dims, preferred_element_type=F32)
            if nK == 1:
                o_ref[...] = part.astype(out_dtype)
                continue

            @pl.when(k == 0)
            def _(acc=acc, part=part):
                acc[...] = part

            @pl.when(k > 0)
            def _(acc=acc, part=part):
                acc[...] += part

            @pl.when(k == nK - 1)
            def _(acc=acc, o_ref=o_ref):
                o_ref[...] = acc[...].astype(out_dtype)

    (a1, b1), (a2, b2) = pairs
    shape = jax.ShapeDtypeStruct((M, N), out_dtype)
    return pl.pallas_call(
        body, grid=(nI, nJ, nK), in_specs=[a_spec, b_spec, a_spec, b_spec], out_specs=[o_spec, o_spec],
        out_shape=[shape, shape], scratch_shapes=[] if nK == 1 else [pltpu.VMEM((tm, tn), F32)] * 2, name=name,
        compiler_params=_params(("parallel", "parallel", "arbitrary")))(a1, b1, a2, b2)


def _piece_tiles(pieces, tile):
    starts, s = [], 0
    for _, w in pieces:
        assert w % tile == 0
        starts.append(s // tile)
        s += w
    return starts, [w // tile for _, w in pieces], s


def _pieces_tn(name, pieces, b, tile, job=None):
    T, N = b.shape
    st, cn, M = _piece_tiles(pieces, tile)
    nP, nI = len(pieces), M // tile

    def body(*refs):
        p_refs, b_hbm, o_ref = refs[:nP], refs[nP], refs[nP + 1]
        bbuf, abuf, bsem, asem = refs[nP + 2:]
        i = pl.program_id(0)

        def fetch(step, slot):
            for p in range(nP):
                @pl.when((step >= st[p]) & (step < st[p] + cn[p]))
                def _():
                    col = pl.multiple_of((step - st[p]) * tile, tile)
                    pltpu.make_async_copy(p_refs[p].at[pl.ds(0, T), pl.ds(col, tile)], abuf.at[slot], asem.at[slot]).start()

        @pl.when(i == 0)
        def _():
            whole = pltpu.make_async_copy(b_hbm, bbuf, bsem)
            whole.start()
            fetch(0, 0)
            whole.wait()

        @pl.when(i + 1 < nI)
        def _():
            fetch(i + 1, (i + 1) % 2)

        pltpu.make_async_copy(p_refs[0].at[pl.ds(0, T), pl.ds(0, tile)], abuf.at[i % 2], asem.at[i % 2]).wait()
        o_ref[...] = lax.dot_general(abuf[i % 2], bbuf[...], TN, preferred_element_type=F32).astype(BF16)

    res, job_res = _pcall(
        body, grid=(nI,), in_specs=[_ANY] * (nP + 1), out_specs=[pl.BlockSpec((tile, N), lambda i: (i, 0))],
        out_shape=[jax.ShapeDtypeStruct((M, N), BF16)],
        scratch_shapes=[pltpu.VMEM((T, N), b.dtype), pltpu.VMEM((2, T, tile), b.dtype), pltpu.SemaphoreType.DMA, _SEMS(2)],
        name=name, semantics=("arbitrary",), args=[p for p, _ in pieces] + [b], job=job)
    return res if job is None else (res, job_res)


def _rows_mm(name, pieces, w, T, tm, tk, vecs, bufs, parts, epi, job=None):
    st, cn, K = _piece_tiles(pieces, tk)
    nP, nI, nK = len(pieces), T // tm, K // tk
    part_specs = [pl.BlockSpec(bs, lambda i, k, im=im: im(i, 0, k)) for _, _, bs, im in parts]
    n_vec, nB = len(vecs), len(bufs)
    load_ix = [n for n, (_, src, _) in enumerate(bufs) if src is not None]
    store_ix = [n for n, (_, _, store) in enumerate(bufs) if store]
    n_any_in, n_any_out = len(load_ix), len(store_ix)

    def body(*refs):
        o = nP
        p_refs, w_ref = refs[:nP], refs[o]
        vec_refs = refs[o + 1:o + 1 + n_vec]
        ins = refs[o + 1 + n_vec:o + 1 + n_vec + n_any_in]
        o = o + 1 + n_vec + n_any_in
        hbm_outs, p_outs = refs[o:o + n_any_out], refs[o + n_any_out:o + n_any_out + len(parts)]
        o = o + n_any_out + len(parts)
        acc, abuf = refs[o:o + 2]
        buf_refs = refs[o + 2:o + 2 + nB]
        asem, in_sems, out_sems = refs[-3:]
        i, k = pl.program_id(0), pl.program_id(1)
        g = i * nK + k
        rows_of = lambda ref, ii: ref.at[pl.ds(pl.multiple_of(ii * tm, tm), tm), :]
        bufs_in = [buf_refs[n] for n in load_ix]
        bufs_out = [buf_refs[n] for n in store_ix]

        def fetch(ii, kk, slot):
            for p in range(nP):
                @pl.when((kk >= st[p]) & (kk < st[p] + cn[p]))
                def _():
                    col = pl.multiple_of((kk - st[p]) * tk, tk)
                    src = p_refs[p].at[pl.ds(pl.multiple_of(ii * tm, tm), tm), pl.ds(col, tk)]
                    pltpu.make_async_copy(src, abuf.at[slot], asem.at[slot]).start()

        loads = lambda ii: [pltpu.make_async_copy(rows_of(src, ii), buf, in_sems.at[n])
                            for n, (src, buf) in enumerate(zip(ins, bufs_in))]
        stores = lambda ii: [pltpu.make_async_copy(buf, rows_of(dst, ii), out_sems.at[n])
                             for n, (buf, dst) in enumerate(zip(bufs_out, hbm_outs))]

        @pl.when(g == 0)
        def _():
            fetch(0, 0, 0)

        @pl.when(g + 1 < nI * nK)
        def _():
            last_k = k == nK - 1
            fetch(jnp.where(last_k, i + 1, i), jnp.where(last_k, 0, k + 1), (g + 1) % 2)

        @pl.when(k == 0)
        def _():
            @pl.when(i > 0)
            def _():
                for cp in stores(i - 1):
                    cp.wait()
            for cp in loads(i):
                cp.start()

        pltpu.make_async_copy(p_refs[0].at[pl.ds(0, tm), pl.ds(0, tk)], abuf.at[g % 2], asem.at[g % 2]).wait()

        def product(cols):
            return jnp.dot(abuf[g % 2], w_ref[:, cols], preferred_element_type=F32)

        col_blocks = [slice(c0, c0 + 512) for c0 in range(0, D, 512)]

        @pl.when(k == 0)
        def _():
            for cols in col_blocks:
                acc[:, cols] = product(cols)

        @pl.when(k > 0)
        def _():
            for cols in col_blocks:
                acc[:, cols] += product(cols)

        @pl.when(k == nK - 1)
        def _():
            for cp in loads(i):
                cp.wait()
            epi(acc, vec_refs, buf_refs, p_outs)
            for cp in stores(i):
                cp.start()

            @pl.when(i == nI - 1)
            def _():
                for cp in stores(i):
                    cp.wait()

    vec = pl.BlockSpec((1, D), lambda i, k: (0, 0))
    scratch = ([pltpu.VMEM((tm, D), F32), pltpu.VMEM((2, tm, tk), BF16)] + [pltpu.VMEM((tm, D), dt) for dt, _, _ in bufs]
               + [_SEMS(2), _SEMS(n_any_in), _SEMS(n_any_out)])
    res, job_res = _pcall(
        body, grid=(nI, nK),
        in_specs=[_ANY] * nP + [pl.BlockSpec((tk, D), lambda i, k: (k, 0))] + [vec] * n_vec + [_ANY] * n_any_in,
        out_specs=[_ANY] * n_any_out + part_specs,
        out_shape=([jax.ShapeDtypeStruct((T, D), bufs[n][0]) for n in store_ix]
                   + [jax.ShapeDtypeStruct(s, d) for s, d, _, _ in parts]),
        scratch_shapes=scratch, name=name, semantics=("arbitrary", "arbitrary"),
        args=[p for p, _ in pieces] + [w] + list(vecs) + [bufs[n][1] for n in load_ix], job=job)
    return res if job is None else (res, job_res)


def _pieces_nn_rms(name, pieces, w, x, gain, sc, dres, tm, tk, job=None):
    _, outs, epi = _rms_mod_bwd_epilogue(x, gain, sc, dres, tm)

    def on_rows(acc, vecs, bufs, parts):
        epi(acc, [bufs[0], vecs[0], vecs[1], bufs[1]], [bufs[1], *parts])

    return _rows_mm(name, pieces, w, x.shape[0], tm, tk, [gain, sc], [(F32, x, False), (F32, dres, True)],
                    outs[1:], on_rows, job=job)


def _rms_mod_fwd(name, x, gain, sc, sh, tr):
    T = x.shape[0]

    def body(x_ref, g_ref, sc_ref, sh_ref, h_ref):
        xv = x_ref[...]
        rstd = lax.rsqrt(jnp.mean(xv * xv, axis=-1, keepdims=True) + EPS)
        h_ref[...] = ((xv * rstd * g_ref[...]) * (1.0 + sc_ref[...]) + sh_ref[...]).astype(BF16)

    row = pl.BlockSpec((tr, D), lambda i: (i, 0))
    vec = pl.BlockSpec((1, D), lambda i: (0, 0))
    return pl.pallas_call(
        body, grid=(T // tr,), in_specs=[row, vec, vec, vec], out_specs=row,
        out_shape=jax.ShapeDtypeStruct((T, D), BF16), name=name, compiler_params=_params(("parallel",)),
    )(x, gain, sc, sh)


def _rms_mod_bwd_epilogue(x, gain, sc, dres, tm, gate=None, mo=None):
    T = x.shape[0]
    with_gate = gate is not None
    row = ((tm, D), lambda i, j, k: (i, 0))
    vec = ((1, D), lambda i, j, k: (0, 0))
    part = ((T // tm * 8, D), F32, (8, D), lambda i, j, k: (i, 0))
    extras = [(x, *row), (gain, *vec), (sc, *vec), (dres, *row)]
    outs = [((T, D), F32, *row), part, part, part]
    if with_gate:
        extras += [(gate, *vec), (mo, *row)]
        outs += [((T, D), BF16, *row), part]

    rows = min(64, tm)

    def epi(acc, ex, ou):
        g = ex[1][...]
        sums = [jnp.zeros((8, D), F32) for _ in range(4)]
        for r0 in range(0, tm, rows):
            rs = slice(r0, r0 + rows)
            dhv, xv = acc[rs, :], ex[0][rs, :]
            rstd = lax.rsqrt(jnp.mean(xv * xv, axis=-1, keepdims=True) + EPS)
            xhat = xv * rstd
            dn = dhv * (1.0 + ex[2][...])
            dxhat = dn * g
            dx = ex[3][rs, :] + rstd * (dxhat - xhat * jnp.mean(dxhat * xhat, axis=-1, keepdims=True))
            ou[0][rs, :] = dx
            terms = [dhv, dhv * (xhat * g), dn * xhat]
            if with_gate:
                terms.append(dx * ex[5][rs, :].astype(F32))
                ou[4][rs, :] = (ex[4][...] * dx).astype(BF16)
            sums = [s + _fold8(t) for s, t in zip(sums, terms)] + sums[len(terms):]
        ou[1][...], ou[2][...], ou[3][...] = sums[:3]
        if with_gate:
            ou[5][...] = sums[3]

    return extras, outs, epi


def _rms_mod_bwd(name, dh, x, gain, sc, dres, tr, gate=None, mo=None):
    T = x.shape[0]
    extras, outs, epi = _rms_mod_bwd_epilogue(x, gain, sc, dres, tr, gate, mo)
    rows_only = lambda im: (lambda i: im(i, 0, 0))
    nE = len(extras)

    def body(dh_ref, *refs):
        epi(dh_ref, refs[:nE], refs[nE:])

    return pl.pallas_call(
        body, grid=(T // tr,),
        in_specs=[pl.BlockSpec((tr, D), lambda i: (i, 0))] + [pl.BlockSpec(bs, rows_only(im)) for _, bs, im in extras],
        out_specs=[pl.BlockSpec(bs, rows_only(im)) for _, _, bs, im in outs],
        out_shape=[jax.ShapeDtypeStruct(s, d) for s, d, _, _ in outs], name=name, compiler_params=_params(("parallel",)),
    )(dh, *[e for e, _, _ in extras])


def _split3(v):
    h = v.astype(BF16)
    r1 = v - h.astype(F32)
    m = r1.astype(BF16)
    lo = (r1 - m.astype(F32)).astype(BF16)
    return h, m, lo


def _tri_mm(tri, v, dims=NN):
    h, m, lo = _split3(v)
    t = tri.astype(BF16)
    mm = lambda p: lax.dot_general(t, p, dims, preferred_element_type=F32)
    return (mm(lo) + mm(m)) + mm(h)


def _hgrn_chunk_terms(q, fl, lb):
    sig = _sigmoid(fl)
    f = lb + (1.0 - lb) * sig
    lf = jnp.log(f)
    kk = 1.0 - f
    sq = _sigmoid(q)
    qf = q * sq
    return sig, f, lf, kk, sq, qf


def _causal(n):
    r = lax.broadcasted_iota(jnp.int32, (n, n), 0)
    c = lax.broadcasted_iota(jnp.int32, (n, n), 1)
    return r >= c


def _hgrn_fwd(proj, lb_logits, o_gain, tt, job=None):
    T = proj.shape[0]
    nT, ncl = T // tt, tt // CHUNK
    C = CHUNK

    def body(q_ref, f_ref, i_ref, g_ref, lbl_ref, og_ref, y_ref, st_ref, S):
        @pl.when(pl.program_id(1) == 0)
        def _():
            S[...] = jnp.zeros_like(S)

        lbl = lbl_ref[...]
        lb = _sigmoid(lbl[0:1, :] - lbl[1:2, :])
        og = og_ref[...]
        shp = (ncl, C, A_HD)
        q, fl, v, g = (r[...].reshape(shp) for r in (q_ref, f_ref, i_ref, g_ref))
        tri = jnp.broadcast_to(_causal(C), (ncl, C, C))
        _, _, lf, kk, _, qf = _hgrn_chunk_terms(q, fl, lb)
        b = _tri_mm(tri, lf, BNN)
        bm, bl = b[:, C // 2 - 1:C // 2, :], b[:, C - 1:C, :]
        qd, kd = qf * jnp.exp(b - bm), kk * jnp.exp(bm - b)
        A = jnp.where(tri, _dot(qd, kd, BNT), 0.0)
        d_st = _dot(v, kk * jnp.exp(bl - b), BTN)
        dec = jnp.exp(bl)
        st = S[...]
        for ci in range(ncl):
            st_ref[0, ci] = st
            st = st * dec[ci] + d_st[ci]
        S[...] = st
        o = _dot(A, v, BNN) + _dot(qf * jnp.exp(b), st_ref[0], BNT)
        r = lax.rsqrt(jnp.mean(o * o, axis=-1, keepdims=True) + EPS)
        y_ref[...] = (o * r * og * (g * _sigmoid(g))).astype(BF16).reshape(tt, A_HD)

    def col(off):
        return pl.BlockSpec((tt, A_HD), lambda h, t: (t, off // A_HD + h))

    head_vec = lambda rows: pl.BlockSpec((rows, A_HD), lambda h, t: (0, h))
    return _pcall(
        body, grid=(A_HEADS, nT),
        in_specs=[col(OFF_QA), col(OFF_FA), col(OFF_IA), col(OFF_GA), head_vec(2), head_vec(1)],
        out_specs=[pl.BlockSpec((tt, A_HD), lambda h, t: (t, h)),
                   pl.BlockSpec((1, ncl, A_HD, A_HD), lambda h, t: (h, t, 0, 0))],
        out_shape=[jax.ShapeDtypeStruct((T, AW), BF16),
                   jax.ShapeDtypeStruct((A_HEADS, T // C, A_HD, A_HD), F32)],
        scratch_shapes=[pltpu.VMEM((A_HD, A_HD), F32)], name="hgrn_fwd", semantics=("parallel", "arbitrary"),
        args=[proj, proj, proj, proj, lb_logits, o_gain], job=job)


def _hgrn_bwd(proj, st, dy, lb_logits, o_gain, tt, job=None):
    T = proj.shape[0]
    nT, ncl = T // tt, tt // CHUNK
    C = CHUNK

    def body(q_ref, f_ref, i_ref, g_ref, st_ref, dy_ref, lbl_ref, og_ref,
             dq_ref, df_ref, di_ref, dg_ref, plb_ref, pog_ref, dS):
        @pl.when(pl.program_id(1) == 0)
        def _():
            dS[...] = jnp.zeros_like(dS)

        lbl = lbl_ref[...]
        lb = _sigmoid(lbl[0:1, :] - lbl[1:2, :])
        og = og_ref[...]
        shp = (ncl, C, A_HD)
        flat = lambda t: t.reshape(tt, A_HD)
        q, fl, v, g, dout = (r[...].reshape(shp) for r in (q_ref, f_ref, i_ref, g_ref, dy_ref))
        tri = jnp.broadcast_to(_causal(C), (ncl, C, C))
        rowi = lax.broadcasted_iota(jnp.int32, shp, 1)
        st0 = st_ref[0]
        sig, f, lf, kk, sq, qf = _hgrn_chunk_terms(q, fl, lb)
        b = _tri_mm(tri, lf, BNN)
        bm, bl = b[:, C // 2 - 1:C // 2, :], b[:, C - 1:C, :]
        e_qd, e_kd, e_ke, e_b = jnp.exp(b - bm), jnp.exp(bm - b), jnp.exp(bl - b), jnp.exp(b)
        qd, kd, ke, qe = qf * e_qd, kk * e_kd, kk * e_ke, qf * e_b
        dec = jnp.exp(bl)
        A = jnp.where(tri, _dot(qd, kd, BNT), 0.0)
        o = _dot(A, v, BNN) + _dot(qe, st0, BNT)
        r = lax.rsqrt(jnp.mean(o * o, axis=-1, keepdims=True) + EPS)
        sg = _sigmoid(g)
        on = o * r * og
        dg_ref[...] = flat((dout * on * (sg * (1.0 + g * (1.0 - sg)))).astype(BF16))
        don = dout * (g * sg)
        pog_ref[...] = _fold8(flat(don * o * r))
        dyh = don * og
        do = r * (dyh - o * (r * r) * jnp.mean(dyh * o, axis=-1, keepdims=True))
        g_st = _dot(do, qe, BTN)
        run = dS[...]
        after = [None] * ncl
        for ci in reversed(range(ncl)):
            after[ci] = run
            run = g_st[ci] + run * dec[ci]
        dS[...] = run
        d_after = jnp.stack(after, axis=0)
        ddec = jnp.sum(d_after * st0, axis=1, keepdims=True)
        dqe = _dot(do, st0, BNN)
        dke = _dot(v, d_after, BNN)
        dA = jnp.where(tri, _dot(do, v, BNT), 0.0)
        dv = _dot(ke, d_after, BNT) + _dot(A, do, BTN)
        dqd = _dot(dA, kd, BNN)
        dkd = _dot(dA, qd, BTN)
        di_ref[...] = flat(dv.astype(BF16))
        dqf = dqe * e_b + dqd * e_qd
        dkk = dkd * e_kd + dke * e_ke
        t_qd, t_kd, t_ke = dqd * qd, dkd * kd, dke * ke
        db = dqe * qe + t_qd - t_kd - t_ke
        dbm = jnp.sum(t_kd - t_qd, axis=1, keepdims=True)
        dbl = jnp.sum(t_ke, axis=1, keepdims=True) + ddec * dec
        db = db + jnp.where(rowi == C // 2 - 1, dbm, 0.0) + jnp.where(rowi == C - 1, dbl, 0.0)
        dlf = _tri_mm(tri, db, BTN)
        dfv = dlf / f - dkk
        df_ref[...] = flat((dfv * (1.0 - lb) * sig * (1.0 - sig)).astype(BF16))
        plb_ref[...] = _fold8(flat(dfv * (1.0 - sig)))
        dq_ref[...] = flat((dqf * (sq * (1.0 + q * (1.0 - sq)))).astype(BF16))

    def col(off):
        return pl.BlockSpec((tt, A_HD), lambda h, t: (nT - 1 - t, off // A_HD + h))

    head_vec = lambda rows: pl.BlockSpec((rows, A_HD), lambda h, t: (0, h))
    o_spec = pl.BlockSpec((tt, A_HD), lambda h, t: (nT - 1 - t, h))
    p_spec = pl.BlockSpec((8, A_HD), lambda h, t: (t, h))
    o_shape = jax.ShapeDtypeStruct((T, AW), BF16)
    p_shape = jax.ShapeDtypeStruct((nT * 8, AW), F32)
    return _pcall(
        body, grid=(A_HEADS, nT),
        in_specs=[col(OFF_QA), col(OFF_FA), col(OFF_IA), col(OFF_GA),
                  pl.BlockSpec((1, ncl, A_HD, A_HD), lambda h, t: (h, nT - 1 - t, 0, 0)),
                  pl.BlockSpec((tt, A_HD), lambda h, t: (nT - 1 - t, h)), head_vec(2), head_vec(1)],
        out_specs=[o_spec, o_spec, o_spec, o_spec, p_spec, p_spec],
        out_shape=[o_shape, o_shape, o_shape, o_shape, p_shape, p_shape],
        scratch_shapes=[pltpu.VMEM((A_HD, A_HD), F32)], name="hgrn_bwd", semantics=("parallel", "arbitrary"),
        args=[proj, proj, proj, proj, st, dy, lb_logits, o_gain], job=job)


LANES = 128
Q_COLS = BW // LANES


def _low_half():
    return lax.broadcasted_iota(jnp.int32, (1, LANES), 1) < B_HD


def _half_sum(t, low):
    lo = jnp.sum(jnp.where(low, t, 0.0), axis=-1, keepdims=True)
    hi = jnp.sum(jnp.where(low, 0.0, t), axis=-1, keepdims=True)
    return jnp.where(low, lo, hi)


def _half_rms(t, low):
    r = lax.rsqrt(_half_sum(t * t, low) * (1.0 / B_HD) + EPS)
    return t * r, r


def _fold_halves(p, low):
    return jnp.where(low, p + pltpu.roll(p, B_HD, 1), 0.0)


def _stack_cols(x):
    return jnp.stack([x[:, c * LANES:(c + 1) * LANES] for c in range(Q_COLS)], axis=0).reshape(KV_HEADS, 2 * BLK, LANES)


def _col_of(t, c):
    return t[c // 2, (c % 2) * BLK:(c % 2 + 1) * BLK]


def _split_halves(col, s, low):
    own = jnp.where(low if s == 0 else jnp.logical_not(low), col, 0.0)
    other = pltpu.roll(own, B_HD, 1)
    return (own, other) if s == 0 else (other, own)


def _swa_keys(kp_ref, kc_ref, vp_ref, vc_ref, kg, low):
    k_lo, k_hi, v_lo, v_hi, hats = [], [], [], [], []
    for j in range(KVW // LANES):
        cs = slice(j * LANES, (j + 1) * LANES)
        k_hat, k_r = _half_rms(jnp.concatenate([kp_ref[:, cs], kc_ref[:, cs]], axis=0), low)
        vcol = jnp.concatenate([vp_ref[:, cs], vc_ref[:, cs]], axis=0)
        hats.append((k_hat, k_r))
        for s in range(2):
            for dst_lo, dst_hi, col in ((k_lo, k_hi, k_hat * kg), (v_lo, v_hi, vcol)):
                lo, hi = _split_halves(col, s, low)
                dst_lo.append(lo)
                dst_hi.append(hi)
    st = lambda parts: jnp.stack(parts, axis=0)
    return st(k_lo), st(k_hi), st(v_lo), st(v_hi), hats


def _swa_mask(first_block):
    qi = lax.broadcasted_iota(jnp.int32, (BLK, 2 * BLK), 0) + BLK
    ki = lax.broadcasted_iota(jnp.int32, (BLK, 2 * BLK), 1)
    rel = qi - ki
    m = (rel >= 0) & (rel < BLK) & (jnp.logical_not(first_block) | (ki >= BLK))
    return jnp.concatenate([m, m], axis=0)


def _sink_cols(sk_ref, hi):
    top = lax.broadcasted_iota(jnp.int32, (2 * BLK, 1), 0) < BLK
    return jnp.stack([jnp.where(top, sk_ref[0, GROUP * hk + hi], sk_ref[0, GROUP * hk + 2 + hi])
                      for hk in range(KV_HEADS)], axis=0)


def _swa_probs(qn, k_half, sink, mask):
    s = jnp.where(mask, _dot(qn, k_half, BNT) * (B_HD ** -0.5), NEG)
    m = jnp.maximum(jnp.max(s, axis=-1, keepdims=True), sink)
    p = jnp.exp(s - m)
    ps = jnp.exp(sink - m)
    inv = 1.0 / (jnp.sum(p, axis=-1, keepdims=True) + ps)
    return p * inv, ps * inv


def _swa_fwd(proj, q_gain, k_gain, sinks, job=None):
    T = proj.shape[0]
    nb = T // BLK

    def body(q_ref, kc_ref, kp_ref, vc_ref, vp_ref, qg_ref, kg_ref, sk_ref, o_ref):
        low = _low_half()
        mask = _swa_mask(pl.program_id(0) == 0)
        qn = _half_rms(_stack_cols(q_ref[...]), low)[0] * qg_ref[...]
        k_lo, k_hi, v_lo, v_hi, _ = _swa_keys(kp_ref, kc_ref, vp_ref, vc_ref, kg_ref[...], low)
        p_lo, _ = _swa_probs(qn, k_lo, _sink_cols(sk_ref, 0), mask)
        p_hi, _ = _swa_probs(qn, k_hi, _sink_cols(sk_ref, 1), mask)
        o = (_dot(p_lo, v_lo, BNN) + _dot(p_hi, v_hi, BNN)).astype(BF16)
        for c in range(Q_COLS):
            o_ref[:, c * LANES:(c + 1) * LANES] = _col_of(o, c)

    q_gain, k_gain = jnp.tile(q_gain, (1, 2)), jnp.tile(k_gain, (1, 2))
    cur = lambda w, off: pl.BlockSpec((BLK, w), lambda i: (i, off // w))
    prev = lambda w, off: pl.BlockSpec((BLK, w), lambda i: (jnp.maximum(i - 1, 0), off // w))
    small = lambda n: pl.BlockSpec((1, 2 * n), lambda i: (0, 0))
    return _pcall(
        body, grid=(nb,),
        in_specs=[cur(BW, OFF_QB), cur(KVW, OFF_KB), prev(KVW, OFF_KB), cur(KVW, OFF_VB), prev(KVW, OFF_VB),
                  small(B_HD), small(B_HD), pl.BlockSpec(memory_space=pltpu.SMEM)],
        out_specs=[pl.BlockSpec((BLK, BW), lambda i: (i, 0))],
        out_shape=[jax.ShapeDtypeStruct((T, BW), BF16)], scratch_shapes=[], name="swa_fwd", semantics=("parallel",),
        args=[proj, proj, proj, proj, proj, q_gain, k_gain, sinks], job=job)


def _swa_bwd(proj, dout, q_gain, k_gain, sinks, job=None):
    T = proj.shape[0]
    nb = T // BLK
    W = BW + 2 * KVW

    def body(q_ref, kc_ref, kp_ref, vc_ref, vp_ref, do_ref, qg_ref, kg_ref, sk_ref,
             dq_ref, dkv_ref, pqg_ref, pkg_ref, psk_ref, dkn_c, dv_c):
        i = pl.program_id(0)
        live = i < nb
        low = _low_half()
        high = jnp.logical_not(low)
        qg, kg = qg_ref[...], kg_ref[...]
        mask = _swa_mask(i == 0)
        lane = lax.broadcasted_iota(jnp.int32, (1, LANES), 1)
        scale = B_HD ** -0.5

        @pl.when(i == 0)
        def _():
            dkn_c[...] = jnp.zeros_like(dkn_c)
            dv_c[...] = jnp.zeros_like(dv_c)

        q_hat, q_r = _half_rms(_stack_cols(q_ref[...]), low)
        qn = q_hat * qg
        k_lo, k_hi, v_lo, v_hi, hats = _swa_keys(kp_ref, kc_ref, vp_ref, vc_ref, kg, low)
        do = _stack_cols(do_ref[...])
        dqn = jnp.zeros((KV_HEADS, 2 * BLK, LANES), F32)
        acc_sk = jnp.zeros((1, LANES), F32)
        dk_parts, dv_parts = [], []
        for hi, (k_h, v_h) in enumerate(((k_lo, v_lo), (k_hi, v_hi))):
            p, ps = _swa_probs(qn, k_h, _sink_cols(sk_ref, hi), mask)
            dp = _dot(do, v_h, BNT)
            delta = jnp.sum(p * dp, axis=-1, keepdims=True)
            ds = p * (dp - delta) * scale
            dqn = dqn + _dot(ds, k_h, BNN)
            dk_parts.append(_dot(ds, qn, BTN))
            dv_parts.append(_dot(p, do, BTN))
            t = ps * delta
            for hk in range(KV_HEADS):
                for rows in range(2):
                    h = GROUP * hk + 2 * rows + hi
                    acc_sk = acc_sk + jnp.where(
                        lane == h, -jnp.sum(t[hk, rows * BLK:(rows + 1) * BLK], axis=0, keepdims=True), 0.0)
        dqh = dqn * qg
        dq = (q_r * (dqh - q_hat * (_half_sum(dqh * q_hat, low) * (1.0 / B_HD)))).astype(BF16)
        for c in range(Q_COLS):
            dq_ref[:, c * LANES:(c + 1) * LANES] = _col_of(dq, c)
        acc_qg = _fold_halves(_fold8((dqn * q_hat).reshape(KV_HEADS * 2 * BLK, LANES)), low)

        def native(parts, j):
            lo_arr, hi_arr = parts
            a, b = 2 * j, 2 * j + 1
            return (jnp.where(low, lo_arr[a], 0.0) + pltpu.roll(jnp.where(high, hi_arr[a], 0.0), B_HD, 1)
                    + jnp.where(high, hi_arr[b], 0.0) + pltpu.roll(jnp.where(low, lo_arr[b], 0.0), B_HD, 1))

        acc_kg = jnp.zeros((8, LANES), F32)
        for j in range(KVW // LANES):
            cs = slice(j * LANES, (j + 1) * LANES)
            dkn = jnp.where(live, native(dk_parts, j), 0.0)
            dvc = jnp.where(live, native(dv_parts, j), 0.0)
            kp_hat, kp_r = hats[j][0][:BLK], hats[j][1][:BLK]
            dkn_prev = dkn_c[:, cs] + dkn[:BLK]
            dv_prev = dv_c[:, cs] + dvc[:BLK]
            acc_kg = acc_kg + _fold8(dkn_prev * kp_hat)
            dkh = dkn_prev * kg
            dkv_ref[:, cs] = (kp_r * (dkh - kp_hat * (_half_sum(dkh * kp_hat, low) * (1.0 / B_HD)))).astype(BF16)
            dkv_ref[:, KVW + j * LANES:KVW + (j + 1) * LANES] = dv_prev.astype(BF16)
            dkn_c[:, cs] = dkn[BLK:]
            dv_c[:, cs] = dvc[BLK:]
        keep = jnp.where(i > 0, 1.0, 0.0)
        pqg_ref[...] = jnp.where(live, acc_qg, 0.0)
        pkg_ref[...] = _fold_halves(acc_kg, low) * keep
        psk_ref[...] = jnp.broadcast_to(jnp.where(live, acc_sk, 0.0), (8, LANES)) * (
            lax.broadcasted_iota(jnp.int32, (8, LANES), 0) == 0).astype(F32)

    q_gain, k_gain = jnp.tile(q_gain, (1, 2)), jnp.tile(k_gain, (1, 2))
    last = nb - 1
    cur = lambda w, off: pl.BlockSpec((BLK, w), lambda i: (jnp.minimum(i, last), off // w))
    prev = lambda w, off: pl.BlockSpec((BLK, w), lambda i: (jnp.maximum(i - 1, 0), off // w))
    small = lambda n: pl.BlockSpec((1, 2 * n), lambda i: (0, 0))
    part = pl.BlockSpec((8, 128), lambda i: (i, 0))
    p_shape = jax.ShapeDtypeStruct(((nb + 1) * 8, 128), F32)
    return _pcall(
        body, grid=(nb + 1,),
        in_specs=[cur(BW, OFF_QB), cur(KVW, OFF_KB), prev(KVW, OFF_KB), cur(KVW, OFF_VB), prev(KVW, OFF_VB),
                  pl.BlockSpec((BLK, BW), lambda i: (jnp.minimum(i, last), 0)), small(B_HD), small(B_HD),
                  pl.BlockSpec(memory_space=pltpu.SMEM)],
        out_specs=[pl.BlockSpec((BLK, BW), lambda i: (i, 0)),
                   pl.BlockSpec((BLK, 2 * KVW), lambda i: (jnp.maximum(i - 1, 0), 0)), part, part, part],
        out_shape=[jax.ShapeDtypeStruct((T + BLK, BW), BF16), jax.ShapeDtypeStruct((T, 2 * KVW), BF16),
                   p_shape, p_shape, p_shape],
        scratch_shapes=[pltpu.VMEM((BLK, KVW), F32), pltpu.VMEM((BLK, KVW), F32)], name="swa_bwd",
        semantics=("arbitrary",), args=[proj, proj, proj, proj, proj, dout, q_gain, k_gain, sinks], job=job)


def _branch_merge(ya_pre, attn, wa_t, wb_t, proj, tm, tn, job=None):
    T = ya_pre.shape[0]

    def body(a_ref, b_ref, wa_ref, wb_ref, ga_ref, gb_ref, ya_ref, yb_ref, mg_ref):
        ya = lax.dot_general(a_ref[...], wa_ref[...], NT, preferred_element_type=F32)
        yb = lax.dot_general(b_ref[...], wb_ref[...], NT, preferred_element_type=F32)
        ya_ref[...] = ya.astype(BF16)
        yb_ref[...] = yb.astype(BF16)
        mg_ref[...] = (_sigmoid(ga_ref[...]) * ya + _sigmoid(gb_ref[...]) * yb).astype(BF16)

    o_spec = pl.BlockSpec((tm, tn), lambda i, j: (i, j))
    o_shape = jax.ShapeDtypeStruct((T, D), BF16)
    return _pcall(
        body, grid=(T // tm, D // tn),
        in_specs=[pl.BlockSpec((tm, AW), lambda i, j: (i, 0)), pl.BlockSpec((tm, BW), lambda i, j: (i, 0)),
                  pl.BlockSpec((tn, AW), lambda i, j: (j, 0)), pl.BlockSpec((tn, BW), lambda i, j: (j, 0)),
                  pl.BlockSpec((tm, tn), lambda i, j: (i, OFF_GTA // tn + j)),
                  pl.BlockSpec((tm, tn), lambda i, j: (i, OFF_GTB // tn + j))],
        out_specs=[o_spec, o_spec, o_spec], out_shape=[o_shape, o_shape, o_shape], scratch_shapes=[], name="branch_merge",
        semantics=("parallel", "parallel"), args=[ya_pre, attn, wa_t, wb_t, proj, proj], job=job)


def _ij(i, j, k):
    return (i, j)


def _local_step(x, tgt, mod, g1, g2, lbl, og, qg, kg, sk, shards, c_arr):
    win_s, wa_s, wb_s, wout_s, wmi_s, wmo_s = shards
    T = x.shape[0]
    tm, tr, tt = min(1024, T), min(256, T), min(2048, T)
    tk_t = min(1024, T)
    tn = 512
    sh1, sc1, gt1, sh2, sc2, gt2 = (mod[:, i * D:(i + 1) * D] for i in range(N_MOD))
    nI = T // tm
    blk = (tm, tn)

    h = _rms_mod_fwd("rms1_fwd", x, g1, sc1, sh1, tr)

    def epi_store(acc, ex, ou):
        ou[0][...] = acc.astype(ou[0].dtype)

    tm2 = min(2048, T)
    blk2 = (tm2, tn)

    full = lambda s: (0, s.shape[0])
    last = wmi_s.shape[0]
    gather = _gather_relay_job
    (win_t,) = _run_job("gather_w_in", gather([win_s], alone=True))
    (proj,), (wa_t, wb_t, w_out, wmi_part) = _mm(
        "in_proj", "nt", [(h, D)], win_t, T, IN_W, D, tm2, tn, D, [], [((T, IN_W), F32, blk2, _ij)], epi_store,
        job=gather([wa_s, wb_s, wout_s, wmi_s], rows=[full(wa_s), full(wb_s), full(wout_s), (0, MI_CUTS[0])]))
    (ya_pre, st), (wmi_part,) = _hgrn_fwd(
        proj, lbl, og, tt, job=gather([wmi_s], rows=[MI_CUTS], into=[wmi_part]))
    (attn,), (wmi_t, wmo_part) = _swa_fwd(
        proj, qg, kg, sk, job=gather([wmi_s, wmo_s], rows=[(MI_CUTS[1], last), (0, MO_CUT)], into=[wmi_part, None]))
    (ya, yb, merged), _ = _branch_merge(ya_pre, attn, wa_t, wb_t, proj, tm, tn)

    def residual_rows(acc, vecs, bufs, parts):
        gt, gain, sc, sh = (v[...] for v in vecs)
        x_buf, mo_buf, h2_buf = bufs
        rows = min(64, tm)
        for r0 in range(0, tm, rows):
            rs = slice(r0, r0 + rows)
            z = acc[rs, :]
            mo_buf[rs, :] = z.astype(BF16)
            x1v = x_buf[rs, :] + gt * z
            x_buf[rs, :] = x1v
            rstd = lax.rsqrt(jnp.mean(x1v * x1v, axis=-1, keepdims=True) + EPS)
            h2_buf[rs, :] = ((x1v * rstd * gain) * (1.0 + sc) + sh).astype(BF16)

    x1, mo, h2 = _rows_mm("out_proj", [(merged, D)], w_out, T, tm, min(1024, D), [gt1, g2, sc2, sh2],
                          [(F32, x, True), (BF16, None, True), (BF16, None, True)], [], residual_rows)

    def epi_relu2(acc, ex, ou):
        r = jnp.maximum(acc, 0.0)
        ou[0][...] = r.astype(BF16)
        ou[1][...] = (r * r).astype(BF16)

    (r, a), (w_mo,) = _mm("mlp_in", "nt", [(h2, D)], wmi_t, T, HID, D, tm2, tn, D, [],
                          [((T, HID), BF16, blk2, _ij), ((T, HID), BF16, blk2, _ij)], epi_relu2,
                          job=gather([wmo_s], rows=[(MO_CUT, last)], into=[wmo_part]))

    def loss_rows(acc, vecs, bufs, parts):
        gt = vecs[0][...]
        x1_buf, t_buf, dz_buf = bufs
        rows = min(64, tm)
        loss_sum, gate_sum = jnp.zeros((8, D), F32), jnp.zeros((8, D), F32)
        for r0 in range(0, tm, rows):
            rs = slice(r0, r0 + rows)
            z = acc[rs, :]
            e = x1_buf[rs, :] + gt * z - t_buf[rs, :]
            dy = e * (1.0 / D)
            t_buf[rs, :] = dy
            dz_buf[rs, :] = (gt * dy).astype(BF16)
            loss_sum = loss_sum + _fold8(e * e)
            gate_sum = gate_sum + _fold8(dy * z)
        parts[0][...] = loss_sum * (0.5 / D)
        parts[1][...] = gate_sum

    part_rows = ((nI * 8, D), F32, (8, D), lambda i, j, k: (i, 0))
    dy, dz, p_loss, p_gt2 = _rows_mm(
        "mlp_out", [(a, HID)], w_mo, T, tm, 1024, [gt2], [(F32, x1, False), (F32, tgt, True), (BF16, None, True)],
        [part_rows, part_rows], loss_rows)

    def epi_du(acc, ex, ou):
        ou[0][...] = (acc * (2.0 * ex[0][...].astype(F32))).astype(BF16)

    (du,) = _mm("mlp_out_dx", "nt", [(dz, D)], w_mo, T, HID, D, tm2, tn, D, [(r, blk2, _ij)],
                [((T, HID), BF16, blk2, _ij)], epi_du)
    gblk = (1024, 1024)
    gwide = (1024, D)
    pair_sum = lambda nm, g, r1: _pair_sum("pair_sum_" + nm, g, r1, c_arr, _sum_rows(r1.shape[1]))
    (g_mo,) = _mm("mlp_out_dw", "tn", [(a, HID)], dz, HID, D, T, 1024, D, tk_t, [], [((HID, D), BF16, gwide, _ij)], epi_store)
    (dh2,), (r1_mo,) = _mm("mlp_in_dx", "nn", [(du, HID)], wmi_t, T, D, HID, tm, D, 1024, [],
                           [((T, D), F32, (tm, D), _ij)], epi_store, job=_pair_job([g_mo]))
    dx1, p_sh2, p_sc2, p_g2, dmo, p_gt1 = _rms_mod_bwd("rms2_bwd", dh2, x1, g2, sc2, dy, tr, gate=gt1, mo=mo)
    s_mo = pair_sum("mlp_out", g_mo, r1_mo)
    near, far = (1, 2), (3,)
    (g_mi,), (rn_mo,) = _mm("mlp_in_dw", "tn", [(du, HID)], h2, HID, D, T, 1024, D, tk_t, [],
                            [((HID, D), BF16, gwide, _ij)], epi_store, job=_chip_job([s_mo], near))

    def epi_gates(acc, ex, ou):
        ya_ref, yb_ref, ga_ref, gb_ref = ex
        sa, sb = _sigmoid(ga_ref[...]), _sigmoid(gb_ref[...])
        ou[0][...] = (acc * sa).astype(BF16)
        ou[1][...] = (acc * sb).astype(BF16)
        ou[2][...] = (acc * ya_ref[...].astype(F32) * (sa * (1.0 - sa))).astype(BF16)
        ou[3][...] = (acc * yb_ref[...].astype(F32) * (sb * (1.0 - sb))).astype(BF16)

    o_bf = ((T, D), BF16, blk, _ij)
    (dya, dyb, dga, dgb), (rf_mo, r1_mi) = _mm(
        "out_proj_dx", "nt", [(dmo, D)], w_out, T, D, D, tm, tn, D,
        [(ya, blk, _ij), (yb, blk, _ij), (proj, blk, lambda i, j, k: (i, OFF_GTA // tn + j)),
         (proj, blk, lambda i, j, k: (i, OFF_GTB // tn + j))], [o_bf, o_bf, o_bf, o_bf], epi_gates,
        job=_both(_chip_job([s_mo], far), _pair_job([g_mi])))
    s_mi = pair_sum("mlp_in", g_mi, r1_mi)
    (g_out,) = _mm("out_proj_dw", "tn", [(merged, D)], dmo, D, D, T, 1024, 1024, tk_t, [], [((D, D), BF16, gblk, _ij)], epi_store)
    dya_pre, dattn = _twin_mm("branch_dx", "nn", [(dya, wa_t), (dyb, wb_t)], T, AW, D, tm, tn, D, F32)
    g_a, g_b = _twin_mm("branch_dw", "tn", [(dya, ya_pre), (dyb, attn)], D, AW, T, 1024, 1024, tk_t, BF16)
    (dqa, dfa, dia, dgg, p_lb, p_og), (rn_mi, r1_out, r1_a, r1_b) = _hgrn_bwd(
        proj, st, dya_pre, lbl, og, tt, job=_both(_chip_job([s_mi], near), _pair_job([g_out, g_a, g_b])))
    (dqb, dkv, p_qg, p_kg, p_sk), (rf_mi,) = _swa_bwd(proj, dattn, qg, kg, sk, job=_chip_job([s_mi], far))
    s_out, s_a, s_b = pair_sum("out", g_out, r1_out), pair_sum("branch_a", g_a, r1_a), pair_sum("branch_b", g_b, r1_b)
    pieces = [(dqa, AW), (dfa, AW), (dia, AW), (dgg, AW), (dqb, BW), (dkv, 2 * KVW), (dga, D), (dgb, D)]
    (g_in,), (r2_out, r2_a, r2_b) = _pieces_tn("in_proj_dw", pieces, h, 512, job=_chip_job([s_out, s_a, s_b]))
    (r1_in,) = _run_job("pair_w_in", _pair_job([g_in]))
    s_in = pair_sum("in", g_in, r1_in)
    (dx, p_sh1, p_sc1, p_g1), (r2_in,) = _pieces_nn_rms(
        "in_proj_dx", pieces, win_t, x, g1, sc1, dx1, tm, 512, job=_chip_job([s_in]))

    partials = dict(sh1=p_sh1, sc1=p_sc1, gt1=p_gt1, sh2=p_sh2, sc2=p_sc2, gt2=p_gt2, g1=p_g1, g2=p_g2,
                    lb=p_lb, og=p_og, qg=p_qg, kg=p_kg, sk=p_sk, loss=p_loss)
    sums = dict(w_in=(s_in, [r2_in]), w_branch_a=(s_a, [r2_a]), w_branch_b=(s_b, [r2_b]), w_out=(s_out, [r2_out]),
                w_mlp_in=(s_mi, [rn_mi, rf_mi]), w_mlp_out=(s_mo, [rn_mo, rf_mo]))
    return dx, sums, partials


def _exchange_slots(buf, send_sems, recv_sems):
    me = _mesh_pos()
    mine = buf.at[_index(me)]
    sends = []
    for k in range(1, N_DEV):
        cp = pltpu.make_async_remote_copy(src_ref=mine, dst_ref=mine, send_sem=send_sems.at[k - 1],
                                          recv_sem=recv_sems.at[k - 1], device_id=_flip(me, k), device_id_type=MESH)
        cp.start()
        sends.append(cp)
    for k in range(1, N_DEV):
        theirs = buf.at[_index(_flip(me, k))]
        pltpu.make_async_remote_copy(src_ref=theirs, dst_ref=theirs, send_sem=send_sems.at[k - 1],
                                     recv_sem=recv_sems.at[k - 1], device_id=_flip(me, k), device_id_type=MESH).wait_recv()
    for cp in sends:
        cp.wait_send()


ADA_W = N_MOD * D // N_DEV


def _ada_mod(c, w_ada, b_shard):
    def body(c_ref, w_ref, b_ref, mod_ref, sc_ref, cbuf, mbuf, s1, r1, s2, r2):
        me = _index(_mesh_pos())
        cbuf[me] = c_ref[...]
        _exchange_slots(cbuf, s1, r1)
        row = lax.broadcasted_iota(jnp.int32, (N_DEV, D), 0)
        call = jnp.zeros((N_DEV, D), F32)
        for d in range(N_DEV):
            call = jnp.where(row == d, cbuf[d], call)
        sc = call * _sigmoid(call)
        sc_ref[...] = sc
        mbuf[me] = _dot(sc, w_ref[...]) + b_ref[...]
        _exchange_slots(mbuf, s2, r2)
        for s in range(N_DEV):
            mod_ref[:, s * ADA_W:(s + 1) * ADA_W] = mbuf[s, pl.ds(me, 1), :]

    return pl.pallas_call(
        body, in_specs=[_VMEM, _VMEM, _VMEM], out_specs=[_VMEM, _VMEM],
        out_shape=[jax.ShapeDtypeStruct((1, N_MOD * D), F32), jax.ShapeDtypeStruct((N_DEV, D), F32)],
        scratch_shapes=[pltpu.VMEM((N_DEV, 1, D), F32), pltpu.VMEM((N_DEV, N_DEV, ADA_W), F32),
                        _SEMS(N_DEV - 1), _SEMS(N_DEV - 1), _SEMS(N_DEV - 1), _SEMS(N_DEV - 1)],
        name="ada_mod", compiler_params=pltpu.CompilerParams(vmem_limit_bytes=VMEM_LIMIT),
    )(c, w_ada, b_shard)


SMALL_SEGS = (("b_ada", N_MOD * D), ("norm1_gain", D), ("norm2_gain", D), ("lb0", AW), ("lb1", AW),
              ("hgrn_o_gain", AW), ("q_norm_gain", 128), ("k_norm_gain", 128), ("sinks", 128))
SMALL_W = sum(w for _, w in SMALL_SEGS)
X_SEGS = (("sh1", D), ("sc1", D), ("gt1", D), ("sh2", D), ("sc2", D), ("gt2", D), ("g1", D), ("g2", D),
          ("lb", AW), ("og", AW), ("qg", 128), ("kg", 128), ("sk", 128), ("loss", 128))
X_W = sum(w for _, w in X_SEGS)


def _offsets(segs):
    out, o = {}, 0
    for name, w in segs:
        out[name] = (o, w)
        o += w
    return out


def _small_reduce(parts, lb_logits):
    xo, so = _offsets(X_SEGS), _offsets(SMALL_SEGS)
    names = [nm for nm, _ in X_SEGS]

    def body(*refs):
        p_refs = dict(zip(names, refs[:len(names)]))
        lbl_ref, allx, gs_ref, loss_ref, send_sems, recv_sems = refs[len(names):]
        me = _index(_mesh_pos())
        for nm, (o, w) in xo.items():
            if nm == "loss":
                allx[me, :, o:o + w] = jnp.broadcast_to(jnp.sum(p_refs[nm][...]), (1, w))
            else:
                allx[me, :, o:o + w] = jnp.sum(p_refs[nm][...], axis=0, keepdims=True)
        _exchange_slots(allx, send_sems, recv_sems)
        tot = allx[0]
        for d in range(1, N_DEV):
            tot = tot + allx[d]
        seg = lambda nm: tot[:, xo[nm][0]:xo[nm][0] + xo[nm][1]]

        def put(nm, v):
            gs_ref[:, so[nm][0]:so[nm][0] + so[nm][1]] = v

        put("b_ada", tot[:, 0:N_MOD * D])
        put("norm1_gain", seg("g1"))
        put("norm2_gain", seg("g2"))
        lbl = lbl_ref[...]
        lb = _sigmoid(lbl[0:1, :] - lbl[1:2, :])
        dl0 = seg("lb") * lb * (1.0 - lb)
        put("lb0", dl0)
        put("lb1", -dl0)
        put("hgrn_o_gain", seg("og"))
        put("q_norm_gain", seg("qg"))
        put("k_norm_gain", seg("kg"))
        put("sinks", seg("sk"))
        loss_ref[...] = seg("loss")

    return pl.pallas_call(
        body, in_specs=[_VMEM] * (len(names) + 1), out_specs=[_VMEM, _VMEM, _VMEM],
        out_shape=[jax.ShapeDtypeStruct((N_DEV, 1, X_W), F32), jax.ShapeDtypeStruct((1, SMALL_W), F32),
                   jax.ShapeDtypeStruct((1, 128), F32)],
        scratch_shapes=[_SEMS(N_DEV - 1), _SEMS(N_DEV - 1)], name="small_reduce",
        compiler_params=pltpu.CompilerParams(vmem_limit_bytes=VMEM_LIMIT),
    )(*[parts[nm] for nm in names], lb_logits)


def _adamw_math(w, g, m, v):
    m = B1 * m + (1.0 - B1) * g
    v = B2 * v + (1.0 - B2) * (g * g)
    m_hat = m / (1.0 - B1 ** STEP)
    v_hat = v / (1.0 - B2 ** STEP)
    return -LR * (m_hat / (jnp.sqrt(v_hat) + ADAM_EPS) + WD * w), m, v


def _sum_rows(rs):
    return 256 if rs % 256 == 0 else rs // 2


def _pair_sum(name, g, recv, c_arr, tr):
    _, rs, cols = recv.shape
    blk = (1, tr, cols)

    def body(c_ref, g_ref, r_ref, o_ref):
        o_ref[...] = (g_ref[...].astype(F32) + r_ref[...].astype(F32)).astype(BF16)

    grid_spec = pltpu.PrefetchScalarGridSpec(
        num_scalar_prefetch=1, grid=(4, rs // tr),
        in_specs=[pl.BlockSpec(blk, lambda q, i, c: (2 * q + c[0], i, 0)), pl.BlockSpec(blk, lambda q, i, c: (q, i, 0))],
        out_specs=pl.BlockSpec(blk, lambda q, i, c: (q, i, 0)))
    return pl.pallas_call(body, grid_spec=grid_spec, out_shape=jax.ShapeDtypeStruct((4, rs, cols), BF16), name=name,
                          compiler_params=_params(("parallel", "parallel")))(c_arr, g.reshape(N_DEV, rs, cols), recv)


def _sum_adamw(name, sums, recvs, q_arr, w, m, v, transposed, tile):
    rows, cols = w.shape
    nR = len(recvs)

    def body(q_ref, s_ref, *refs):
        r_refs = refs[:nR]
        w_ref, m_ref, v_ref, g_ref, d_ref, nm_ref, nv_ref = refs[nR:]
        g = s_ref[0].astype(F32)
        for r_ref in r_refs:
            for slot in range(r_ref.shape[0]):
                g = g + r_ref[slot].astype(F32)
        g = g.T if transposed else g
        g_ref[...] = g
        d_ref[...], nm_ref[...], nv_ref[...] = _adamw_math(w_ref[...], g, m_ref[...], v_ref[...])

    if transposed:
        slab = lambda n, first: pl.BlockSpec((n, cols, tile), lambda i, q: (first(q), 0, i))
    else:
        slab = lambda n, first: pl.BlockSpec((n, tile, cols), lambda i, q: (first(q), i, 0))
    spec = pl.BlockSpec((tile, cols), lambda i, q: (i, 0))
    shape = jax.ShapeDtypeStruct((rows, cols), F32)
    grid_spec = pltpu.PrefetchScalarGridSpec(
        num_scalar_prefetch=1, grid=(rows // tile,),
        in_specs=[slab(1, lambda q: q[0])] + [slab(r.shape[0], lambda q: 0) for r in recvs] + [spec] * 3,
        out_specs=[spec] * 4)
    return pl.pallas_call(body, grid_spec=grid_spec, out_shape=[shape] * 4, name=name,
                          compiler_params=_params(("parallel",)))(q_arr, sums, *recvs, w, m, v)


def _adamw(name, w, g, m, v, tr):
    rows, cols = w.shape

    def body(w_ref, g_ref, m_ref, v_ref, d_ref, nm_ref, nv_ref):
        d_ref[...], nm_ref[...], nv_ref[...] = _adamw_math(w_ref[...], g_ref[...], m_ref[...], v_ref[...])

    spec = pl.BlockSpec((tr, cols), lambda i: (i, 0))
    shape = jax.ShapeDtypeStruct((rows, cols), F32)
    return pl.pallas_call(
        body, grid=(rows // tr,), in_specs=[spec] * 4, out_specs=[spec] * 3, out_shape=[shape] * 3, name=name,
        compiler_params=_params(("parallel",)),
    )(w, g, m, v)


def _ada_update(sc_t, dmod_cols, w, m, v, tr):
    rows, cols = w.shape

    def body(s_ref, d_ref, w_ref, m_ref, v_ref, g_ref, dl_ref, nm_ref, nv_ref):
        g = jnp.dot(s_ref[...], d_ref[...], precision=lax.Precision.HIGHEST, preferred_element_type=F32)
        g_ref[...] = g
        dl_ref[...], nm_ref[...], nv_ref[...] = _adamw_math(w_ref[...], g, m_ref[...], v_ref[...])

    spec = pl.BlockSpec((tr, cols), lambda i: (i, 0))
    shape = jax.ShapeDtypeStruct((rows, cols), F32)
    return pl.pallas_call(
        body, grid=(rows // tr,),
        in_specs=[pl.BlockSpec((tr, N_DEV), lambda i: (i, 0)), pl.BlockSpec((N_DEV, cols), lambda i: (0, 0)), spec, spec, spec],
        out_specs=[spec] * 4, out_shape=[shape] * 4, name="ada_update", compiler_params=_params(("parallel",)),
    )(sc_t, dmod_cols, w, m, v)


BIG = ("w_in", "w_branch_a", "w_branch_b", "w_out", "w_mlp_in", "w_mlp_out")
COLUMN_SHARDED = ("w_in", "w_branch_a", "w_branch_b", "w_mlp_in")
AS_TRANSPOSE = ("w_in",)
WEIGHTS = ("w_ada", "b_ada", "norm1_gain", "w_in", "lb_logits", "hgrn_o_gain", "q_norm_gain", "k_norm_gain", "sinks",
           "w_branch_a", "w_branch_b", "w_out", "norm2_gain", "w_mlp_in", "w_mlp_out")


def _to_bf16(name, w, transposed, tile=256):
    rows, cols = w.shape

    def body(w_ref, o_ref):
        v = w_ref[...]
        o_ref[...] = (v.T if transposed else v).astype(BF16)

    out_spec = pl.BlockSpec((cols, tile), lambda i: (0, i)) if transposed else pl.BlockSpec((tile, cols), lambda i: (i, 0))
    return pl.pallas_call(
        body, grid=(rows // tile,), in_specs=[pl.BlockSpec((tile, cols), lambda i: (i, 0))], out_specs=out_spec,
        out_shape=jax.ShapeDtypeStruct((cols, rows) if transposed else (rows, cols), BF16), name=name,
        compiler_params=_params(("parallel",)))(w)


def _pack_small(p):
    lb = p["lb_logits"]
    src = dict(p, lb0=lb[0:1], lb1=lb[1:2])
    return jnp.concatenate([jnp.pad(src[nm], ((0, 0), (0, w - src[nm].shape[1]))) for nm, w in SMALL_SEGS], axis=1)


def _unpack_small(vec, shapes):
    so = _offsets(SMALL_SEGS)
    out = {}
    for nm, shp in shapes.items():
        if nm == "lb_logits":
            o = so["lb0"][0]
            out[nm] = vec[0, o:o + 2 * AW].reshape(2, AW)
        else:
            o = so[nm][0]
            out[nm] = vec[:, o:o + shp[1]]
    return out


def kernel(x, c, w_ada, b_ada, norm1_gain, w_in, lb_logits, hgrn_o_gain, q_norm_gain, k_norm_gain, sinks, w_branch_a, w_branch_b, w_out, norm2_gain, w_mlp_in, w_mlp_out, loss_target, m_w_ada, m_b_ada, m_norm1_gain, m_w_in, m_lb_logits, m_hgrn_o_gain, m_q_norm_gain, m_k_norm_gain, m_sinks, m_w_branch_a, m_w_branch_b, m_w_out, m_norm2_gain, m_w_mlp_in, m_w_mlp_out, v_w_ada, v_b_ada, v_norm1_gain, v_w_in, v_lb_logits, v_hgrn_o_gain, v_q_norm_gain, v_k_norm_gain, v_sinks, v_w_branch_a, v_w_branch_b, v_w_out, v_norm2_gain, v_w_mlp_in, v_w_mlp_out):
    w = dict(w_ada=w_ada, b_ada=b_ada, norm1_gain=norm1_gain, w_in=w_in, lb_logits=lb_logits, hgrn_o_gain=hgrn_o_gain,
             q_norm_gain=q_norm_gain, k_norm_gain=k_norm_gain, sinks=sinks, w_branch_a=w_branch_a, w_branch_b=w_branch_b,
             w_out=w_out, norm2_gain=norm2_gain, w_mlp_in=w_mlp_in, w_mlp_out=w_mlp_out)
    m = dict(w_ada=m_w_ada, b_ada=m_b_ada, norm1_gain=m_norm1_gain, w_in=m_w_in, lb_logits=m_lb_logits,
             hgrn_o_gain=m_hgrn_o_gain, q_norm_gain=m_q_norm_gain, k_norm_gain=m_k_norm_gain, sinks=m_sinks,
             w_branch_a=m_w_branch_a, w_branch_b=m_w_branch_b, w_out=m_w_out, norm2_gain=m_norm2_gain,
             w_mlp_in=m_w_mlp_in, w_mlp_out=m_w_mlp_out)
    v = dict(w_ada=v_w_ada, b_ada=v_b_ada, norm1_gain=v_norm1_gain, w_in=v_w_in, lb_logits=v_lb_logits,
             hgrn_o_gain=v_hgrn_o_gain, q_norm_gain=v_q_norm_gain, k_norm_gain=v_k_norm_gain, sinks=v_sinks,
             w_branch_a=v_w_branch_a, w_branch_b=v_w_branch_b, w_out=v_w_out, norm2_gain=v_norm2_gain,
             w_mlp_in=v_w_mlp_in, w_mlp_out=v_w_mlp_out)
    for d in (w, m, v):
        for nm in ("w_ada",) + BIG:
            d[nm] = d[nm][0]
    px, py, pc = _mesh_pos()
    me = _index((px, py, pc))
    c_arr = jnp.reshape(pc, (1,)).astype(jnp.int32)
    q_arr = jnp.reshape(2 * px + py, (1,)).astype(jnp.int32)

    shards = [_to_bf16("shard_" + nm, w[nm].T, False, w[nm].shape[1] // 4) if nm in AS_TRANSPOSE else
              _to_bf16("shard_" + nm, w[nm], nm in COLUMN_SHARDED) for nm in BIG]
    b_shard = lax.dynamic_slice(b_ada, (0, me * ADA_W), (1, ADA_W))
    mod, sc_all = _ada_mod(c, w["w_ada"], b_shard)

    dx, sums, parts = _local_step(x[0], loss_target[0], mod, norm1_gain, norm2_gain, lb_logits, hgrn_o_gain,
                                  q_norm_gain, k_norm_gain, sinks, shards, c_arr)

    allx, g_small, loss = _small_reduce(parts, lb_logits)

    grad, delta, new_m, new_v = {}, {}, {}, {}
    for nm in BIG:
        s, r2 = sums[nm]
        if nm in AS_TRANSPOSE:
            res = _sum_adamw("adamw_" + nm, s, r2, q_arr, w[nm].T, m[nm].T, v[nm].T, False, w[nm].shape[1] // 4)
            grad[nm], delta[nm], new_m[nm], new_v[nm] = (t.T for t in res)
        else:
            grad[nm], delta[nm], new_m[nm], new_v[nm] = _sum_adamw(
                "adamw_" + nm, s, r2, q_arr, w[nm], m[nm], v[nm], nm in COLUMN_SHARDED, 128)

    dmod_cols = lax.dynamic_slice(allx[:, 0, :], (0, me * ADA_W), (N_DEV, ADA_W))
    grad["w_ada"], delta["w_ada"], new_m["w_ada"], new_v["w_ada"] = _ada_update(
        sc_all.T, dmod_cols, w["w_ada"], m["w_ada"], v["w_ada"], 256)

    small_names = [nm for nm in WEIGHTS if nm not in BIG and nm != "w_ada"]
    shapes = {nm: w[nm].shape for nm in small_names}
    ds, ms, vs = _adamw("adamw_small", _pack_small(w), g_small, _pack_small(m), _pack_small(v), 1)
    for dst, vec in ((grad, g_small), (delta, ds), (new_m, ms), (new_v, vs)):
        dst.update(_unpack_small(vec, shapes))

    def full(d, nm):
        return d[nm][None] if nm in BIG or nm == "w_ada" else d[nm]

    return (loss[0, 0], dx[None], *[full(grad, nm) for nm in WEIGHTS], *[full(delta, nm) for nm in WEIGHTS],
            *[full(new_m, nm) for nm in WEIGHTS], *[full(new_v, nm) for nm in WEIGHTS])
```

```python
import functools

import jax
import jax.numpy as jnp
from jax import lax
from jax.experimental import pallas as pl
from jax.experimental.pallas import tpu as pltpu

F32 = jnp.float32
BF16 = jnp.bfloat16
MESH = pl.DeviceIdType.MESH

N_DEV = 8
D = 2048
A_HEADS, A_HD, CHUNK = 8, 128, 64
AW = A_HEADS * A_HD
Q_HEADS, KV_HEADS, GROUP, B_HD, BLK = 16, 4, 4, 64, 128
BW = Q_HEADS * B_HD
KVW = KV_HEADS * B_HD
HID = 4 * D
IN_W = 4 * AW + BW + 2 * KVW + 2 * D
OFF_QA, OFF_FA, OFF_IA, OFF_GA = 0, AW, 2 * AW, 3 * AW
OFF_QB = 4 * AW
OFF_KB = OFF_QB + BW
OFF_VB = OFF_KB + KVW
OFF_GTA = OFF_VB + KVW
OFF_GTB = OFF_GTA + D
N_MOD = 6
EPS = 1e-6
LR, B1, B2, ADAM_EPS, WD, STEP = 1e-3, 0.9, 0.999, 1e-8, 0.01, 10
NEG = -1e30

VMEM_LIMIT = 56 * 1024 * 1024
MI_CUTS = (544, 864)
MO_CUT = 272

NN = (((1,), (0,)), ((), ()))
NT = (((1,), (1,)), ((), ()))
TN = (((0,), (0,)), ((), ()))
BNN = (((2,), (1,)), ((0,), (0,)))
BNT = (((2,), (2,)), ((0,), (0,)))
BTN = (((1,), (1,)), ((0,), (0,)))


def _dot(a, b, dims=NN):
    return lax.dot_general(a.astype(BF16), b.astype(BF16), dims, preferred_element_type=F32)


def _params(sem):
    return pltpu.CompilerParams(dimension_semantics=sem, vmem_limit_bytes=VMEM_LIMIT)


def _sigmoid(x):
    return 1.0 / (1.0 + jnp.exp(-x))


def _fold8(v):
    r, n = v.shape
    return jnp.sum(v.reshape(r // 8, 8, n), axis=0)


_VMEM = pl.BlockSpec(memory_space=pltpu.VMEM)
_ANY = pl.BlockSpec(memory_space=pl.ANY)
_SEMS = lambda n: pltpu.SemaphoreType.DMA((n,))


def _mesh_pos():
    return lax.axis_index("x"), lax.axis_index("y"), lax.axis_index("c")


def _flip(pos, k):
    return tuple(1 - p if (k >> s) & 1 else p for p, s in zip(pos, (2, 1, 0)))


def _index(pos):
    return 4 * pos[0] + 2 * pos[1] + pos[2]


class _Job:
    def __init__(self, ins, out_shape, sems, start, finish, aliases=None, middle=None):
        self.ins, self.out_shape, self.sems, self.start, self.finish = list(ins), list(out_shape), list(sems), start, finish
        self.aliases = dict(aliases or {})
        self.middle = middle


def _both(j1, j2):
    assert not j1.aliases and not j2.aliases
    n_in, n_out, n_sem = len(j1.ins), len(j1.out_shape), len(j1.sems)
    first = lambda ins, outs, sems: (ins[:n_in], outs[:n_out], sems[:n_sem])
    second = lambda ins, outs, sems: (ins[n_in:], outs[n_out:], sems[n_sem:])

    def start(*refs):
        j1.start(*first(*refs))
        j2.start(*second(*refs))

    def finish(*refs):
        j1.finish(*first(*refs))
        j2.finish(*second(*refs))

    return _Job(j1.ins + j2.ins, j1.out_shape + j2.out_shape, j1.sems + j2.sems, start, finish)


def _pcall(body, *, grid, in_specs, out_specs, out_shape, scratch_shapes, name, semantics, args, job=None, prefetch=()):
    n_pre = len(prefetch)

    def call(fn, in_specs_, out_specs_, out_shape_, scratch_, sem, operands, aliases):
        if n_pre:
            spec = pltpu.PrefetchScalarGridSpec(num_scalar_prefetch=n_pre, grid=grid, in_specs=in_specs_,
                                                out_specs=out_specs_, scratch_shapes=scratch_)
            return pl.pallas_call(fn, grid_spec=spec, out_shape=out_shape_, name=name, input_output_aliases=aliases,
                                  compiler_params=_params(sem))(*prefetch, *operands)
        return pl.pallas_call(fn, grid=grid, in_specs=in_specs_, out_specs=out_specs_, out_shape=out_shape_,
                              scratch_shapes=scratch_, name=name, input_output_aliases=aliases,
                              compiler_params=_params(sem))(*operands)

    if job is None:
        return list(call(body, in_specs, out_specs, out_shape, scratch_shapes, semantics, args, {})), []
    n_in, n_out, n_scr = len(in_specs), len(out_specs), len(scratch_shapes)
    j_in, j_out = len(job.ins), len(job.out_shape)
    steps = tuple(grid)

    def carrier(*refs):
        pre, refs = refs[:n_pre], refs[n_pre:]
        o = 0
        main_in, o = refs[o:o + n_in], o + n_in
        job_in, o = refs[o:o + j_in], o + j_in
        main_out, o = refs[o:o + n_out], o + n_out
        job_out, o = refs[o:o + j_out], o + j_out
        main_scr, job_sems = refs[o:o + n_scr], refs[o + n_scr:]
        ids = [pl.program_id(a) for a in range(len(steps))]
        first = functools.reduce(lambda p, q: p & q, [i == 0 for i in ids])
        last = functools.reduce(lambda p, q: p & q, [i == s - 1 for i, s in zip(ids, steps)])

        @pl.when(first)
        def _():
            job.start(job_in, job_out, job_sems)

        if job.middle is not None:
            flat, total = 0, 1
            for i, s in zip(ids, steps):
                flat, total = flat * s + i, total * s

            @pl.when(flat == total // 2)
            def _():
                job.middle(job_in, job_out, job_sems)

        body(*pre, *main_in, *main_out, *main_scr)

        @pl.when(last)
        def _():
            job.finish(job_in, job_out, job_sems)

    outs = call(carrier, list(in_specs) + [_ANY] * j_in, list(out_specs) + [_ANY] * j_out,
                list(out_shape) + job.out_shape, list(scratch_shapes) + job.sems, ("arbitrary",) * len(steps),
                list(args) + job.ins, {n_pre + n_in + i: n_out + o for i, o in job.aliases.items()})
    return list(outs[:n_out]), list(outs[n_out:])


def _gather_relay_job(shards, rows=None, into=None, alone=False):
    n = len(shards)
    rows = rows or [(0, s.shape[0]) for s in shards]
    into = into or [None] * n
    olds, aliases = [], {}
    for a, buf in enumerate(into):
        if buf is not None:
            aliases[n + len(olds)] = a
            olds.append(buf)

    def tools(ins, outs, sems):
        send_sems, recv_sems, local_sems = sems
        x, y, c = _mesh_pos()
        q = 2 * x + y
        chip_at = lambda rel: (1 - x if rel & 2 else x, 1 - y if rel & 1 else y)

        def part(a, chip, core):
            rs, (r0, r1) = shards[a].shape[0], rows[a]
            return outs[a].at[pl.ds((2 * chip + core) * rs + r0, r1 - r0), :]

        own = lambda a: ins[a].at[pl.ds(rows[a][0], rows[a][1] - rows[a][0]), :]

        def copy(a, slot, chip, core, to, src=None):
            blk = part(a, chip, core)
            return pltpu.make_async_remote_copy(src_ref=blk if src is None else src, dst_ref=blk,
                                                send_sem=send_sems.at[7 * a + slot], recv_sem=recv_sems.at[7 * a + slot],
                                                device_id=to, device_id_type=MESH)

        mine = [pltpu.make_async_copy(own(a), part(a, q, c), local_sems.at[a]) for a in range(n)]
        first = [copy(a, slot, q, c, (x, y, 1 - c) if slot == 0 else (*chip_at(slot), c), src=own(a))
                 for a in range(n) for slot in (0, 1, 2)]
        return x, y, c, q, chip_at, copy, mine, first

    def start(ins, outs, sems):
        *_, mine, first = tools(ins, outs, sems)
        for cp in mine + first:
            cp.start()

    def middle(ins, outs, sems):
        x, y, c, q, chip_at, copy, _, _ = tools(ins, outs, sems)
        me, sib = (x, y, c), (x, y, 1 - c)

        def relay(src, dst):
            for a in range(n):
                copy(a, src, q ^ src, c, me).wait_recv()
                copy(a, 3, q ^ src, c, (*chip_at(dst), c)).start()
                copy(a, 3 + src, q ^ src, c, sib).start()
            for a in range(n):
                copy(a, dst, q ^ dst, c, me).wait_recv()
                copy(a, 3 + dst, q ^ dst, c, sib).start()

        pl.when(c == 1)(lambda: relay(1, 2))
        pl.when(c == 0)(lambda: relay(2, 1))

    def finish(ins, outs, sems):
        if alone:
            middle(ins, outs, sems)
        x, y, c, q, chip_at, copy, mine, first = tools(ins, outs, sems)
        me, sib = (x, y, c), (x, y, 1 - c)
        for a in range(n):
            copy(a, 3, q ^ 3, c, me).wait_recv()
            copy(a, 6, q ^ 3, c, sib).start()
        for a in range(n):
            copy(a, 0, q, 1 - c, me).wait_recv()
            for rel in (1, 2, 3):
                copy(a, 3 + rel, q ^ rel, 1 - c, me).wait_recv()
        for a in range(n):
            for slot in range(3, 7):
                copy(a, slot, q, c, sib).wait_send()
        for cp in first:
            cp.wait_send()
        for cp in mine:
            cp.wait()

    return _Job(list(shards) + olds, [jax.ShapeDtypeStruct((N_DEV * s.shape[0], s.shape[1]), s.dtype) for s in shards],
                [_SEMS(7 * n), _SEMS(7 * n), _SEMS(n)], start, finish, aliases, middle=None if alone else middle)


def _pair_job(grads):
    n = len(grads)

    def copies(ins, outs, sems):
        send_sems, recv_sems = sems
        x, y, c = _mesh_pos()
        out = []
        for a in range(n):
            rs = grads[a].shape[0] // N_DEV
            for q in range(4):
                blk = ins[a].at[pl.ds((2 * q + 1 - c) * rs, rs), :]
                out.append(pltpu.make_async_remote_copy(
                    src_ref=blk, dst_ref=outs[a].at[q], send_sem=send_sems.at[4 * a + q], recv_sem=recv_sems.at[4 * a + q],
                    device_id=(x, y, 1 - c), device_id_type=MESH))
        return out

    def start(ins, outs, sems):
        for cp in copies(ins, outs, sems):
            cp.start()

    def finish(ins, outs, sems):
        for cp in copies(ins, outs, sems):
            cp.wait()

    return _Job(grads, [jax.ShapeDtypeStruct((4, g.shape[0] // N_DEV, g.shape[1]), g.dtype) for g in grads],
                [_SEMS(4 * n), _SEMS(4 * n)], start, finish)


def _chip_job(sums, rels=(1, 2, 3)):
    n, nr = len(sums), len(rels)

    def copies(ins, outs, sems):
        send_sems, recv_sems = sems
        x, y, c = _mesh_pos()
        out = []
        for a in range(n):
            for slot, r in enumerate(rels):
                px, py = (1 - x if r & 2 else x), (1 - y if r & 1 else y)
                out.append(pltpu.make_async_remote_copy(
                    src_ref=ins[a].at[2 * px + py], dst_ref=outs[a].at[slot], send_sem=send_sems.at[nr * a + slot],
                    recv_sem=recv_sems.at[nr * a + slot], device_id=(px, py, c), device_id_type=MESH))
        return out

    def start(ins, outs, sems):
        for cp in copies(ins, outs, sems):
            cp.start()

    def finish(ins, outs, sems):
        for cp in copies(ins, outs, sems):
            cp.wait()

    return _Job(sums, [jax.ShapeDtypeStruct((nr,) + s.shape[1:], s.dtype) for s in sums],
                [_SEMS(nr * n), _SEMS(nr * n)], start, finish)


def _mm(name, form, a_list, b, M, N, K, tm, tn, tk, extras, outs, epi, job=None):
    nI, nJ, nK = M // tm, N // tn, K // tk
    assert nI * tm == M and nJ * tn == N and nK * tk == K
    dims = {"nn": NN, "nt": NT, "tn": TN}[form]
    b_list = b if isinstance(b, list) else [(b, {"nn": N, "nt": K, "tn": N}[form])]
    nA, nB = len(a_list), len(b_list)
    assert nA == 1 or nB == 1
    assert nB == 1 or form in ("nn", "nt")
    AXIS = {"i": 0, "j": 1, "k": 2}
    a_axis, a_tile = ("i", tm) if form == "tn" else ("k", tk)
    b_axis, b_tile = ("k", tk) if form == "nt" else ("j", tn)

    def cut(pieces, tile, total):
        starts, s = [], 0
        for _, w in pieces:
            assert w % tile == 0
            starts.append(s // tile)
            s += w
        assert s == total
        return starts, [w // tile for _, w in pieces]

    a_st, a_cn = cut(a_list, a_tile, M if form == "tn" else K)
    b_st, b_cn = cut(b_list, b_tile, K if form == "nt" else N)

    def inside(idx, st, cn):
        return (idx >= st) & (idx < st + cn)

    def a_spec(p):
        st, cn = a_st[p], a_cn[p]
        if form == "tn":
            return pl.BlockSpec((tk, tm), lambda i, j, k: (jnp.where(inside(i, st, cn), k, 0), jnp.clip(i - st, 0, cn - 1)))
        return pl.BlockSpec((tm, tk), lambda i, j, k: (i, jnp.clip(k - st, 0, cn - 1)))

    def b_spec(p):
        st, cn = b_st[p], b_cn[p]
        if form == "nt":
            return pl.BlockSpec((tn, tk), lambda i, j, k: (j, jnp.clip(k - st, 0, cn - 1)))
        if nB == 1:
            return pl.BlockSpec((tk, tn), lambda i, j, k: (k, j))
        return pl.BlockSpec((tk, tn), lambda i, j, k: (jnp.where(inside(j, st, cn), k, 0), jnp.clip(j - st, 0, cn - 1)))

    in_specs = ([a_spec(p) for p in range(nA)] + [b_spec(p) for p in range(nB)]
                + [pl.BlockSpec(bs, im) for _, bs, im in extras])
    out_shape = [jax.ShapeDtypeStruct(s_, d_) for s_, d_, _, _ in outs]
    out_specs = [pl.BlockSpec(bs, im) for _, _, bs, im in outs]
    nE, nO = len(extras), len(outs)
    single = nA == 1 and nB == 1

    def body(*refs):
        a_refs, b_refs = refs[:nA], refs[nA:nA + nB]
        ex, ou = refs[nA + nB:nA + nB + nE], refs[nA + nB + nE:nA + nB + nE + nO]
        ids = [pl.program_id(a) for a in range(3)]

        def partial_of(p, q):
            return lax.dot_general(a_refs[p][...], b_refs[q][...], dims, preferred_element_type=F32)

        if nK == 1 and single:
            epi(partial_of(0, 0), ex, ou)
            return
        acc = refs[-1]
        k = ids[2]
        for p in range(nA):
            for q in range(nB):
                def first(p=p, q=q):
                    acc[...] = partial_of(p, q)

                def later(p=p, q=q):
                    acc[...] += partial_of(p, q)

                here = None
                if nA > 1:
                    here = inside(ids[AXIS[a_axis]], a_st[p], a_cn[p])
                if nB > 1:
                    here = inside(ids[AXIS[b_axis]], b_st[q], b_cn[q])
                pl.when(k == 0 if here is None else here & (k == 0))(first)
                pl.when(k > 0 if here is None else here & (k > 0))(later)

        @pl.when(k == nK - 1)
        def _():
            epi(acc[...], ex, ou)

    scratch = [] if (nK == 1 and single) else [pltpu.VMEM((tm, tn), F32)]
    res, job_res = _pcall(
        body, grid=(nI, nJ, nK), in_specs=in_specs, out_specs=out_specs, out_shape=out_shape, scratch_shapes=scratch,
        name=name, semantics=("parallel", "parallel", "arbitrary"),
        args=[a for a, _ in a_list] + [p for p, _ in b_list] + [e for e, _, _ in extras], job=job)
    return res if job is None else (res, job_res)


def _twin_mm(name, form, pairs, M, N, K, tm, tn, tk, out_dtype):
    nI, nJ, nK = M // tm, N // tn, K // tk
    dims = {"nn": NN, "tn": TN}[form]
    a_spec = (pl.BlockSpec((tm, tk), lambda i, j, k: (i, k)) if form == "nn" else pl.BlockSpec((tk, tm), lambda i, j, k: (k, i)))
    b_spec = pl.BlockSpec((tk, tn), lambda i, j, k: (k, j))
    o_spec = pl.BlockSpec((tm, tn), lambda i, j, k: (i, j))

    def body(a1, b1, a2, b2, o1, o2, *accs):
        k = pl.program_id(2)
        for a_ref, b_ref, o_ref, acc in ((a1, b1, o1, accs[0] if accs else None), (a2, b2, o2, accs[1] if accs else None)):
            part = lax.dot_general(a_ref[...], b_ref[...], dims, preferred_element_type=F32)
            if nK == 1:
                o_ref[...] = part.astype(out_dtype)
                continue

            @pl.when(k == 0)
            def _(acc=acc, part=part):
                acc[...] = part

            @pl.when(k > 0)
            def _(acc=acc, part=part):
                acc[...] += part

            @pl.when(k == nK - 1)
            def _(acc=acc, o_ref=o_ref):
                o_ref[...] = acc[...].astype(out_dtype)

    (a1, b1), (a2, b2) = pairs
    shape = jax.ShapeDtypeStruct((M, N), out_dtype)
    return pl.pallas_call(
        body, grid=(nI, nJ, nK), in_specs=[a_spec, b_spec, a_spec, b_spec], out_specs=[o_spec, o_spec],
        out_shape=[shape, shape], scratch_shapes=[] if nK == 1 else [pltpu.VMEM((tm, tn), F32)] * 2, name=name,
        compiler_params=_params(("parallel", "parallel", "arbitrary")))(a1, b1, a2, b2)


def _piece_tiles(pieces, tile):
    starts, s = [], 0
    for _, w in pieces:
        assert w % tile == 0
        starts.append(s // tile)
        s += w
    return starts, [w // tile for _, w in pieces], s


def _pieces_tn(name, pieces, b, tile, job=None):
    T, N = b.shape
    st, cn, M = _piece_tiles(pieces, tile)
    nP, nI = len(pieces), M // tile

    def body(*refs):
        p_refs, b_hbm, o_ref = refs[:nP], refs[nP], refs[nP + 1]
        bbuf, abuf, bsem, asem = refs[nP + 2:]
        i = pl.program_id(0)

        def fetch(step, slot):
            for p in range(nP):
                @pl.when((step >= st[p]) & (step < st[p] + cn[p]))
                def _():
                    col = pl.multiple_of((step - st[p]) * tile, tile)
                    pltpu.make_async_copy(p_refs[p].at[pl.ds(0, T), pl.ds(col, tile)], abuf.at[slot], asem.at[slot]).start()

        @pl.when(i == 0)
        def _():
            whole = pltpu.make_async_copy(b_hbm, bbuf, bsem)
            whole.start()
            fetch(0, 0)
            whole.wait()

        @pl.when(i + 1 < nI)
        def _():
            fetch(i + 1, (i + 1) % 2)

        pltpu.make_async_copy(p_refs[0].at[pl.ds(0, T), pl.ds(0, tile)], abuf.at[i % 2], asem.at[i % 2]).wait()
        o_ref[...] = lax.dot_general(abuf[i % 2], bbuf[...], TN, preferred_element_type=F32).astype(BF16)

    res, job_res = _pcall(
        body, grid=(nI,), in_specs=[_ANY] * (nP + 1), out_specs=[pl.BlockSpec((tile, N), lambda i: (i, 0))],
        out_shape=[jax.ShapeDtypeStruct((M, N), BF16)],
        scratch_shapes=[pltpu.VMEM((T, N), b.dtype), pltpu.VMEM((2, T, tile), b.dtype), pltpu.SemaphoreType.DMA, _SEMS(2)],
        name=name, semantics=("arbitrary",), args=[p for p, _ in pieces] + [b], job=job)
    return res if job is None else (res, job_res)


def _rows_mm(name, pieces, w, T, tm, tk, vecs, bufs, parts, epi, job=None):
    st, cn, K = _piece_tiles(pieces, tk)
    nP, nI, nK = len(pieces), T // tm, K // tk
    part_specs = [pl.BlockSpec(bs, lambda i, k, im=im: im(i, 0, k)) for _, _, bs, im in parts]
    n_vec, nB = len(vecs), len(bufs)
    load_ix = [n for n, (_, src, _) in enumerate(bufs) if src is not None]
    store_ix = [n for n, (_, _, store) in enumerate(bufs) if store]
    n_any_in, n_any_out = len(load_ix), len(store_ix)

    def body(*refs):
        o = nP
        p_refs, w_ref = refs[:nP], refs[o]
        vec_refs = refs[o + 1:o + 1 + n_vec]
        ins = refs[o + 1 + n_vec:o + 1 + n_vec + n_any_in]
        o = o + 1 + n_vec + n_any_in
        hbm_outs, p_outs = refs[o:o + n_any_out], refs[o + n_any_out:o + n_any_out + len(parts)]
        o = o + n_any_out + len(parts)
        acc, abuf = refs[o:o + 2]
        buf_refs = refs[o + 2:o + 2 + nB]
        asem, in_sems, out_sems = refs[-3:]
        i, k = pl.program_id(0), pl.program_id(1)
        g = i * nK + k
        rows_of = lambda ref, ii: ref.at[pl.ds(pl.multiple_of(ii * tm, tm), tm), :]
        bufs_in = [buf_refs[n] for n in load_ix]
        bufs_out = [buf_refs[n] for n in store_ix]

        def fetch(ii, kk, slot):
            for p in range(nP):
                @pl.when((kk >= st[p]) & (kk < st[p] + cn[p]))
                def _():
                    col = pl.multiple_of((kk - st[p]) * tk, tk)
                    src = p_refs[p].at[pl.ds(pl.multiple_of(ii * tm, tm), tm), pl.ds(col, tk)]
                    pltpu.make_async_copy(src, abuf.at[slot], asem.at[slot]).start()

        loads = lambda ii: [pltpu.make_async_copy(rows_of(src, ii), buf, in_sems.at[n])
                            for n, (src, buf) in enumerate(zip(ins, bufs_in))]
        stores = lambda ii: [pltpu.make_async_copy(buf, rows_of(dst, ii), out_sems.at[n])
                             for n, (buf, dst) in enumerate(zip(bufs_out, hbm_outs))]

        @pl.when(g == 0)
        def _():
            fetch(0, 0, 0)

        @pl.when(g + 1 < nI * nK)
        def _():
            last_k = k == nK - 1
            fetch(jnp.where(last_k, i + 1, i), jnp.where(last_k, 0, k + 1), (g + 1) % 2)

        @pl.when(k == 0)
        def _():
            @pl.when(i > 0)
            def _():
                for cp in stores(i - 1):
                    cp.wait()
            for cp in loads(i):
                cp.start()

        pltpu.make_async_copy(p_refs[0].at[pl.ds(0, tm), pl.ds(0, tk)], abuf.at[g % 2], asem.at[g % 2]).wait()

        def product(cols):
            return jnp.dot(abuf[g % 2], w_ref[:, cols], preferred_element_type=F32)

        col_blocks = [slice(c0, c0 + 512) for c0 in range(0, D, 512)]

        @pl.when(k == 0)
        def _():
            for cols in col_blocks:
                acc[:, cols] = product(cols)

        @pl.when(k > 0)
        def _():
            for cols in col_blocks:
                acc[:, cols] += product(cols)

        @pl.when(k == nK - 1)
        def _():
            for cp in loads(i):
                cp.wait()
            epi(acc, vec_refs, buf_refs, p_outs)
            for cp in stores(i):
                cp.start()

            @pl.when(i == nI - 1)
            def _():
                for cp in stores(i):
                    cp.wait()

    vec = pl.BlockSpec((1, D), lambda i, k: (0, 0))
    scratch = ([pltpu.VMEM((tm, D), F32), pltpu.VMEM((2, tm, tk), BF16)] + [pltpu.VMEM((tm, D), dt) for dt, _, _ in bufs]
               + [_SEMS(2), _SEMS(n_any_in), _SEMS(n_any_out)])
    res, job_res = _pcall(
        body, grid=(nI, nK),
        in_specs=[_ANY] * nP + [pl.BlockSpec((tk, D), lambda i, k: (k, 0))] + [vec] * n_vec + [_ANY] * n_any_in,
        out_specs=[_ANY] * n_any_out + part_specs,
        out_shape=([jax.ShapeDtypeStruct((T, D), bufs[n][0]) for n in store_ix]
                   + [jax.ShapeDtypeStruct(s, d) for s, d, _, _ in parts]),
        scratch_shapes=scratch, name=name, semantics=("arbitrary", "arbitrary"),
        args=[p for p, _ in pieces] + [w] + list(vecs) + [bufs[n][1] for n in load_ix], job=job)
    return res if job is None else (res, job_res)


def _pieces_nn_rms(name, pieces, w, x, gain, sc, dres, tm, tk, job=None):
    _, outs, epi = _rms_mod_bwd_epilogue(x, gain, sc, dres, tm)

    def on_rows(acc, vecs, bufs, parts):
        epi(acc, [bufs[0], vecs[0], vecs[1], bufs[1]], [bufs[1], *parts])

    return _rows_mm(name, pieces, w, x.shape[0], tm, tk, [gain, sc], [(F32, x, False), (F32, dres, True)],
                    outs[1:], on_rows, job=job)


def _rms_mod_fwd(name, x, gain, sc, sh, tr, job=None):
    T = x.shape[0]

    def body(x_ref, g_ref, sc_ref, sh_ref, h_ref):
        xv = x_ref[...]
        rstd = lax.rsqrt(jnp.mean(xv * xv, axis=-1, keepdims=True) + EPS)
        h_ref[...] = ((xv * rstd * g_ref[...]) * (1.0 + sc_ref[...]) + sh_ref[...]).astype(BF16)

    row = pl.BlockSpec((tr, D), lambda i: (i, 0))
    vec = pl.BlockSpec((1, D), lambda i: (0, 0))
    return _pcall(body, grid=(T // tr,), in_specs=[row, vec, vec, vec], out_specs=[row],
                  out_shape=[jax.ShapeDtypeStruct((T, D), BF16)], scratch_shapes=[], name=name, semantics=("parallel",),
                  args=[x, gain, sc, sh], job=job)


def _rms_mod_bwd_epilogue(x, gain, sc, dres, tm, gate=None, mo=None):
    T = x.shape[0]
    with_gate = gate is not None
    row = ((tm, D), lambda i, j, k: (i, 0))
    vec = ((1, D), lambda i, j, k: (0, 0))
    part = ((T // tm * 8, D), F32, (8, D), lambda i, j, k: (i, 0))
    extras = [(x, *row), (gain, *vec), (sc, *vec), (dres, *row)]
    outs = [((T, D), F32, *row), part, part, part]
    if with_gate:
        extras += [(gate, *vec), (mo, *row)]
        outs += [((T, D), BF16, *row), part]

    rows = min(64, tm)

    def epi(acc, ex, ou):
        g = ex[1][...]
        sums = [jnp.zeros((8, D), F32) for _ in range(4)]
        for r0 in range(0, tm, rows):
            rs = slice(r0, r0 + rows)
            dhv, xv = acc[rs, :], ex[0][rs, :]
            rstd = lax.rsqrt(jnp.mean(xv * xv, axis=-1, keepdims=True) + EPS)
            xhat = xv * rstd
            dn = dhv * (1.0 + ex[2][...])
            dxhat = dn * g
            dx = ex[3][rs, :] + rstd * (dxhat - xhat * jnp.mean(dxhat * xhat, axis=-1, keepdims=True))
            ou[0][rs, :] = dx
            terms = [dhv, dhv * (xhat * g), dn * xhat]
            if with_gate:
                terms.append(dx * ex[5][rs, :].astype(F32))
                ou[4][rs, :] = (ex[4][...] * dx).astype(BF16)
            sums = [s + _fold8(t) for s, t in zip(sums, terms)] + sums[len(terms):]
        ou[1][...], ou[2][...], ou[3][...] = sums[:3]
        if with_gate:
            ou[5][...] = sums[3]

    return extras, outs, epi


def _rms_mod_bwd(name, dh, x, gain, sc, dres, tr, gate=None, mo=None):
    T = x.shape[0]
    extras, outs, epi = _rms_mod_bwd_epilogue(x, gain, sc, dres, tr, gate, mo)
    rows_only = lambda im: (lambda i: im(i, 0, 0))
    nE = len(extras)

    def body(dh_ref, *refs):
        epi(dh_ref, refs[:nE], refs[nE:])

    return pl.pallas_call(
        body, grid=(T // tr,),
        in_specs=[pl.BlockSpec((tr, D), lambda i: (i, 0))] + [pl.BlockSpec(bs, rows_only(im)) for _, bs, im in extras],
        out_specs=[pl.BlockSpec(bs, rows_only(im)) for _, _, bs, im in outs],
        out_shape=[jax.ShapeDtypeStruct(s, d) for s, d, _, _ in outs], name=name, compiler_params=_params(("parallel",)),
    )(dh, *[e for e, _, _ in extras])


def _split3(v):
    h = v.astype(BF16)
    r1 = v - h.astype(F32)
    m = r1.astype(BF16)
    lo = (r1 - m.astype(F32)).astype(BF16)
    return h, m, lo


def _tri_mm(tri, v, dims=NN):
    h, m, lo = _split3(v)
    t = tri.astype(BF16)
    mm = lambda p: lax.dot_general(t, p, dims, preferred_element_type=F32)
    return (mm(lo) + mm(m)) + mm(h)


def _hgrn_chunk_terms(q, fl, lb):
    sig = _sigmoid(fl)
    f = lb + (1.0 - lb) * sig
    lf = jnp.log(f)
    kk = 1.0 - f
    sq = _sigmoid(q)
    qf = q * sq
    return sig, f, lf, kk, sq, qf


def _causal(n):
    r = lax.broadcasted_iota(jnp.int32, (n, n), 0)
    c = lax.broadcasted_iota(jnp.int32, (n, n), 1)
    return r >= c


def _hgrn_fwd(proj, lb_logits, o_gain, tt, job=None):
    T = proj.shape[0]
    nT, ncl = T // tt, tt // CHUNK
    C = CHUNK

    def body(q_ref, f_ref, i_ref, g_ref, lbl_ref, og_ref, y_ref, st_ref, S):
        @pl.when(pl.program_id(1) == 0)
        def _():
            S[...] = jnp.zeros_like(S)

        lbl = lbl_ref[...]
        lb = _sigmoid(lbl[0:1, :] - lbl[1:2, :])
        og = og_ref[...]
        shp = (ncl, C, A_HD)
        q, fl, v, g = (r[...].reshape(shp) for r in (q_ref, f_ref, i_ref, g_ref))
        tri = jnp.broadcast_to(_causal(C), (ncl, C, C))
        _, _, lf, kk, _, qf = _hgrn_chunk_terms(q, fl, lb)
        b = _tri_mm(tri, lf, BNN)
        bm, bl = b[:, C // 2 - 1:C // 2, :], b[:, C - 1:C, :]
        qd, kd = qf * jnp.exp(b - bm), kk * jnp.exp(bm - b)
        A = jnp.where(tri, _dot(qd, kd, BNT), 0.0)
        d_st = _dot(v, kk * jnp.exp(bl - b), BTN)
        dec = jnp.exp(bl)
        st = S[...]
        for ci in range(ncl):
            st_ref[0, ci] = st
            st = st * dec[ci] + d_st[ci]
        S[...] = st
        o = _dot(A, v, BNN) + _dot(qf * jnp.exp(b), st_ref[0], BNT)
        r = lax.rsqrt(jnp.mean(o * o, axis=-1, keepdims=True) + EPS)
        y_ref[...] = (o * r * og * (g * _sigmoid(g))).astype(BF16).reshape(tt, A_HD)

    def col(off):
        return pl.BlockSpec((tt, A_HD), lambda h, t: (t, off // A_HD + h))

    head_vec = lambda rows: pl.BlockSpec((rows, A_HD), lambda h, t: (0, h))
    return _pcall(
        body, grid=(A_HEADS, nT),
        in_specs=[col(OFF_QA), col(OFF_FA), col(OFF_IA), col(OFF_GA), head_vec(2), head_vec(1)],
        out_specs=[pl.BlockSpec((tt, A_HD), lambda h, t: (t, h)),
                   pl.BlockSpec((1, ncl, A_HD, A_HD), lambda h, t: (h, t, 0, 0))],
        out_shape=[jax.ShapeDtypeStruct((T, AW), BF16),
                   jax.ShapeDtypeStruct((A_HEADS, T // C, A_HD, A_HD), F32)],
        scratch_shapes=[pltpu.VMEM((A_HD, A_HD), F32)], name="hgrn_fwd", semantics=("parallel", "arbitrary"),
        args=[proj, proj, proj, proj, lb_logits, o_gain], job=job)


def _hgrn_bwd(proj, st, dy, lb_logits, o_gain, tt, job=None):
    T = proj.shape[0]
    nT, ncl = T // tt, tt // CHUNK
    C = CHUNK

    def body(q_ref, f_ref, i_ref, g_ref, st_ref, dy_ref, lbl_ref, og_ref,
             dq_ref, df_ref, di_ref, dg_ref, plb_ref, pog_ref, dS):
        @pl.when(pl.program_id(1) == 0)
        def _():
            dS[...] = jnp.zeros_like(dS)

        lbl = lbl_ref[...]
        lb = _sigmoid(lbl[0:1, :] - lbl[1:2, :])
        og = og_ref[...]
        shp = (ncl, C, A_HD)
        flat = lambda t: t.reshape(tt, A_HD)
        q, fl, v, g, dout = (r[...].reshape(shp) for r in (q_ref, f_ref, i_ref, g_ref, dy_ref))
        tri = jnp.broadcast_to(_causal(C), (ncl, C, C))
        rowi = lax.broadcasted_iota(jnp.int32, shp, 1)
        st0 = st_ref[0]
        sig, f, lf, kk, sq, qf = _hgrn_chunk_terms(q, fl, lb)
        b = _tri_mm(tri, lf, BNN)
        bm, bl = b[:, C // 2 - 1:C // 2, :], b[:, C - 1:C, :]
        e_qd, e_kd, e_ke, e_b = jnp.exp(b - bm), jnp.exp(bm - b), jnp.exp(bl - b), jnp.exp(b)
        qd, kd, ke, qe = qf * e_qd, kk * e_kd, kk * e_ke, qf * e_b
        dec = jnp.exp(bl)
        A = jnp.where(tri, _dot(qd, kd, BNT), 0.0)
        o = _dot(A, v, BNN) + _dot(qe, st0, BNT)
        r = lax.rsqrt(jnp.mean(o * o, axis=-1, keepdims=True) + EPS)
        sg = _sigmoid(g)
        on = o * r * og
        dg_ref[...] = flat((dout * on * (sg * (1.0 + g * (1.0 - sg)))).astype(BF16))
        don = dout * (g * sg)
        pog_ref[...] = _fold8(flat(don * o * r))
        dyh = don * og
        do = r * (dyh - o * (r * r) * jnp.mean(dyh * o, axis=-1, keepdims=True))
        g_st = _dot(do, qe, BTN)
        run = dS[...]
        after = [None] * ncl
        for ci in reversed(range(ncl)):
            after[ci] = run
            run = g_st[ci] + run * dec[ci]
        dS[...] = run
        d_after = jnp.stack(after, axis=0)
        ddec = jnp.sum(d_after * st0, axis=1, keepdims=True)
        dqe = _dot(do, st0, BNN)
        dke = _dot(v, d_after, BNN)
        dA = jnp.where(tri, _dot(do, v, BNT), 0.0)
        dv = _dot(ke, d_after, BNT) + _dot(A, do, BTN)
        dqd = _dot(dA, kd, BNN)
        dkd = _dot(dA, qd, BTN)
        di_ref[...] = flat(dv.astype(BF16))
        dqf = dqe * e_b + dqd * e_qd
        dkk = dkd * e_kd + dke * e_ke
        t_qd, t_kd, t_ke = dqd * qd, dkd * kd, dke * ke
        db = dqe * qe + t_qd - t_kd - t_ke
        dbm = jnp.sum(t_kd - t_qd, axis=1, keepdims=True)
        dbl = jnp.sum(t_ke, axis=1, keepdims=True) + ddec * dec
        db = db + jnp.where(rowi == C // 2 - 1, dbm, 0.0) + jnp.where(rowi == C - 1, dbl, 0.0)
        dlf = _tri_mm(tri, db, BTN)
        dfv = dlf / f - dkk
        df_ref[...] = flat((dfv * (1.0 - lb) * sig * (1.0 - sig)).astype(BF16))
        plb_ref[...] = _fold8(flat(dfv * (1.0 - sig)))
        dq_ref[...] = flat((dqf * (sq * (1.0 + q * (1.0 - sq)))).astype(BF16))

    def col(off):
        return pl.BlockSpec((tt, A_HD), lambda h, t: (nT - 1 - t, off // A_HD + h))

    head_vec = lambda rows: pl.BlockSpec((rows, A_HD), lambda h, t: (0, h))
    o_spec = pl.BlockSpec((tt, A_HD), lambda h, t: (nT - 1 - t, h))
    p_spec = pl.BlockSpec((8, A_HD), lambda h, t: (t, h))
    o_shape = jax.ShapeDtypeStruct((T, AW), BF16)
    p_shape = jax.ShapeDtypeStruct((nT * 8, AW), F32)
    return _pcall(
        body, grid=(A_HEADS, nT),
        in_specs=[col(OFF_QA), col(OFF_FA), col(OFF_IA), col(OFF_GA),
                  pl.BlockSpec((1, ncl, A_HD, A_HD), lambda h, t: (h, nT - 1 - t, 0, 0)),
                  pl.BlockSpec((tt, A_HD), lambda h, t: (nT - 1 - t, h)), head_vec(2), head_vec(1)],
        out_specs=[o_spec, o_spec, o_spec, o_spec, p_spec, p_spec],
        out_shape=[o_shape, o_shape, o_shape, o_shape, p_shape, p_shape],
        scratch_shapes=[pltpu.VMEM((A_HD, A_HD), F32)], name="hgrn_bwd", semantics=("parallel", "arbitrary"),
        args=[proj, proj, proj, proj, st, dy, lb_logits, o_gain], job=job)


LANES = 128
Q_COLS = BW // LANES


def _low_half():
    return lax.broadcasted_iota(jnp.int32, (1, LANES), 1) < B_HD


def _half_sum(t, low):
    lo = jnp.sum(jnp.where(low, t, 0.0), axis=-1, keepdims=True)
    hi = jnp.sum(jnp.where(low, 0.0, t), axis=-1, keepdims=True)
    return jnp.where(low, lo, hi)


def _half_rms(t, low):
    r = lax.rsqrt(_half_sum(t * t, low) * (1.0 / B_HD) + EPS)
    return t * r, r


def _fold_halves(p, low):
    return jnp.where(low, p + pltpu.roll(p, B_HD, 1), 0.0)


def _stack_cols(x):
    return jnp.stack([x[:, c * LANES:(c + 1) * LANES] for c in range(Q_COLS)], axis=0).reshape(KV_HEADS, 2 * BLK, LANES)


def _col_of(t, c):
    return t[c // 2, (c % 2) * BLK:(c % 2 + 1) * BLK]


def _split_halves(col, s, low):
    own = jnp.where(low if s == 0 else jnp.logical_not(low), col, 0.0)
    other = pltpu.roll(own, B_HD, 1)
    return (own, other) if s == 0 else (other, own)


def _swa_keys(kp_ref, kc_ref, vp_ref, vc_ref, kg, low):
    k_lo, k_hi, v_lo, v_hi, hats = [], [], [], [], []
    for j in range(KVW // LANES):
        cs = slice(j * LANES, (j + 1) * LANES)
        k_hat, k_r = _half_rms(jnp.concatenate([kp_ref[:, cs], kc_ref[:, cs]], axis=0), low)
        vcol = jnp.concatenate([vp_ref[:, cs], vc_ref[:, cs]], axis=0)
        hats.append((k_hat, k_r))
        for s in range(2):
            for dst_lo, dst_hi, col in ((k_lo, k_hi, k_hat * kg), (v_lo, v_hi, vcol)):
                lo, hi = _split_halves(col, s, low)
                dst_lo.append(lo)
                dst_hi.append(hi)
    st = lambda parts: jnp.stack(parts, axis=0)
    return st(k_lo), st(k_hi), st(v_lo), st(v_hi), hats


def _swa_mask(first_block):
    qi = lax.broadcasted_iota(jnp.int32, (BLK, 2 * BLK), 0) + BLK
    ki = lax.broadcasted_iota(jnp.int32, (BLK, 2 * BLK), 1)
    rel = qi - ki
    m = (rel >= 0) & (rel < BLK) & (jnp.logical_not(first_block) | (ki >= BLK))
    return jnp.concatenate([m, m], axis=0)


def _sink_cols(sk_ref, hi):
    top = lax.broadcasted_iota(jnp.int32, (2 * BLK, 1), 0) < BLK
    return jnp.stack([jnp.where(top, sk_ref[0, GROUP * hk + hi], sk_ref[0, GROUP * hk + 2 + hi])
                      for hk in range(KV_HEADS)], axis=0)


def _swa_probs(qn, k_half, sink, mask):
    s = jnp.where(mask, _dot(qn, k_half, BNT) * (B_HD ** -0.5), NEG)
    m = jnp.maximum(jnp.max(s, axis=-1, keepdims=True), sink)
    p = jnp.exp(s - m)
    ps = jnp.exp(sink - m)
    inv = 1.0 / (jnp.sum(p, axis=-1, keepdims=True) + ps)
    return p * inv, ps * inv


def _swa_fwd(proj, q_gain, k_gain, sinks, job=None):
    T = proj.shape[0]
    nb = T // BLK

    def body(q_ref, kc_ref, kp_ref, vc_ref, vp_ref, qg_ref, kg_ref, sk_ref, o_ref):
        low = _low_half()
        mask = _swa_mask(pl.program_id(0) == 0)
        qn = _half_rms(_stack_cols(q_ref[...]), low)[0] * qg_ref[...]
        k_lo, k_hi, v_lo, v_hi, _ = _swa_keys(kp_ref, kc_ref, vp_ref, vc_ref, kg_ref[...], low)
        p_lo, _ = _swa_probs(qn, k_lo, _sink_cols(sk_ref, 0), mask)
        p_hi, _ = _swa_probs(qn, k_hi, _sink_cols(sk_ref, 1), mask)
        o = (_dot(p_lo, v_lo, BNN) + _dot(p_hi, v_hi, BNN)).astype(BF16)
        for c in range(Q_COLS):
            o_ref[:, c * LANES:(c + 1) * LANES] = _col_of(o, c)

    q_gain, k_gain = jnp.tile(q_gain, (1, 2)), jnp.tile(k_gain, (1, 2))
    cur = lambda w, off: pl.BlockSpec((BLK, w), lambda i: (i, off // w))
    prev = lambda w, off: pl.BlockSpec((BLK, w), lambda i: (jnp.maximum(i - 1, 0), off // w))
    small = lambda n: pl.BlockSpec((1, 2 * n), lambda i: (0, 0))
    return _pcall(
        body, grid=(nb,),
        in_specs=[cur(BW, OFF_QB), cur(KVW, OFF_KB), prev(KVW, OFF_KB), cur(KVW, OFF_VB), prev(KVW, OFF_VB),
                  small(B_HD), small(B_HD), pl.BlockSpec(memory_space=pltpu.SMEM)],
        out_specs=[pl.BlockSpec((BLK, BW), lambda i: (i, 0))],
        out_shape=[jax.ShapeDtypeStruct((T, BW), BF16)], scratch_shapes=[], name="swa_fwd", semantics=("parallel",),
        args=[proj, proj, proj, proj, proj, q_gain, k_gain, sinks], job=job)


def _swa_bwd(proj, dout, q_gain, k_gain, sinks, job=None):
    T = proj.shape[0]
    nb = T // BLK
    W = BW + 2 * KVW

    def body(q_ref, kc_ref, kp_ref, vc_ref, vp_ref, do_ref, qg_ref, kg_ref, sk_ref,
             dq_ref, dkv_ref, pqg_ref, pkg_ref, psk_ref, dkn_c, dv_c):
        i = pl.program_id(0)
        live = i < nb
        low = _low_half()
        high = jnp.logical_not(low)
        qg, kg = qg_ref[...], kg_ref[...]
        mask = _swa_mask(i == 0)
        lane = lax.broadcasted_iota(jnp.int32, (1, LANES), 1)
        scale = B_HD ** -0.5

        @pl.when(i == 0)
        def _():
            dkn_c[...] = jnp.zeros_like(dkn_c)
            dv_c[...] = jnp.zeros_like(dv_c)

        q_hat, q_r = _half_rms(_stack_cols(q_ref[...]), low)
        qn = q_hat * qg
        k_lo, k_hi, v_lo, v_hi, hats = _swa_keys(kp_ref, kc_ref, vp_ref, vc_ref, kg, low)
        do = _stack_cols(do_ref[...])
        dqn = jnp.zeros((KV_HEADS, 2 * BLK, LANES), F32)
        acc_sk = jnp.zeros((1, LANES), F32)
        dk_parts, dv_parts = [], []
        for hi, (k_h, v_h) in enumerate(((k_lo, v_lo), (k_hi, v_hi))):
            p, ps = _swa_probs(qn, k_h, _sink_cols(sk_ref, hi), mask)
            dp = _dot(do, v_h, BNT)
            delta = jnp.sum(p * dp, axis=-1, keepdims=True)
            ds = p * (dp - delta) * scale
            dqn = dqn + _dot(ds, k_h, BNN)
            dk_parts.append(_dot(ds, qn, BTN))
            dv_parts.append(_dot(p, do, BTN))
            t = ps * delta
            for hk in range(KV_HEADS):
                for rows in range(2):
                    h = GROUP * hk + 2 * rows + hi
                    acc_sk = acc_sk + jnp.where(
                        lane == h, -jnp.sum(t[hk, rows * BLK:(rows + 1) * BLK], axis=0, keepdims=True), 0.0)
        dqh = dqn * qg
        dq = (q_r * (dqh - q_hat * (_half_sum(dqh * q_hat, low) * (1.0 / B_HD)))).astype(BF16)
        for c in range(Q_COLS):
            dq_ref[:, c * LANES:(c + 1) * LANES] = _col_of(dq, c)
        acc_qg = _fold_halves(_fold8((dqn * q_hat).reshape(KV_HEADS * 2 * BLK, LANES)), low)

        def native(parts, j):
            lo_arr, hi_arr = parts
            a, b = 2 * j, 2 * j + 1
            return (jnp.where(low, lo_arr[a], 0.0) + pltpu.roll(jnp.where(high, hi_arr[a], 0.0), B_HD, 1)
                    + jnp.where(high, hi_arr[b], 0.0) + pltpu.roll(jnp.where(low, lo_arr[b], 0.0), B_HD, 1))

        acc_kg = jnp.zeros((8, LANES), F32)
        for j in range(KVW // LANES):
            cs = slice(j * LANES, (j + 1) * LANES)
            dkn = jnp.where(live, native(dk_parts, j), 0.0)
            dvc = jnp.where(live, native(dv_parts, j), 0.0)
            kp_hat, kp_r = hats[j][0][:BLK], hats[j][1][:BLK]
            dkn_prev = dkn_c[:, cs] + dkn[:BLK]
            dv_prev = dv_c[:, cs] + dvc[:BLK]
            acc_kg = acc_kg + _fold8(dkn_prev * kp_hat)
            dkh = dkn_prev * kg
            dkv_ref[:, cs] = (kp_r * (dkh - kp_hat * (_half_sum(dkh * kp_hat, low) * (1.0 / B_HD)))).astype(BF16)
            dkv_ref[:, KVW + j * LANES:KVW + (j + 1) * LANES] = dv_prev.astype(BF16)
            dkn_c[:, cs] = dkn[BLK:]
            dv_c[:, cs] = dvc[BLK:]
        keep = jnp.where(i > 0, 1.0, 0.0)
        pqg_ref[...] = jnp.where(live, acc_qg, 0.0)
        pkg_ref[...] = _fold_halves(acc_kg, low) * keep
        psk_ref[...] = jnp.broadcast_to(jnp.where(live, acc_sk, 0.0), (8, LANES)) * (
            lax.broadcasted_iota(jnp.int32, (8, LANES), 0) == 0).astype(F32)

    q_gain, k_gain = jnp.tile(q_gain, (1, 2)), jnp.tile(k_gain, (1, 2))
    last = nb - 1
    cur = lambda w, off: pl.BlockSpec((BLK, w), lambda i: (jnp.minimum(i, last), off // w))
    prev = lambda w, off: pl.BlockSpec((BLK, w), lambda i: (jnp.maximum(i - 1, 0), off // w))
    small = lambda n: pl.BlockSpec((1, 2 * n), lambda i: (0, 0))
    part = pl.BlockSpec((8, 128), lambda i: (i, 0))
    p_shape = jax.ShapeDtypeStruct(((nb + 1) * 8, 128), F32)
    return _pcall(
        body, grid=(nb + 1,),
        in_specs=[cur(BW, OFF_QB), cur(KVW, OFF_KB), prev(KVW, OFF_KB), cur(KVW, OFF_VB), prev(KVW, OFF_VB),
                  pl.BlockSpec((BLK, BW), lambda i: (jnp.minimum(i, last), 0)), small(B_HD), small(B_HD),
                  pl.BlockSpec(memory_space=pltpu.SMEM)],
        out_specs=[pl.BlockSpec((BLK, BW), lambda i: (i, 0)),
                   pl.BlockSpec((BLK, 2 * KVW), lambda i: (jnp.maximum(i - 1, 0), 0)), part, part, part],
        out_shape=[jax.ShapeDtypeStruct((T + BLK, BW), BF16), jax.ShapeDtypeStruct((T, 2 * KVW), BF16),
                   p_shape, p_shape, p_shape],
        scratch_shapes=[pltpu.VMEM((BLK, KVW), F32), pltpu.VMEM((BLK, KVW), F32)], name="swa_bwd",
        semantics=("arbitrary",), args=[proj, proj, proj, proj, proj, dout, q_gain, k_gain, sinks], job=job)


def _branch_merge(ya_pre, attn, wa_t, wb_t, proj, tm, tn, job=None):
    T = ya_pre.shape[0]

    def body(a_ref, b_ref, wa_ref, wb_ref, ga_ref, gb_ref, ya_ref, yb_ref, mg_ref):
        ya = lax.dot_general(a_ref[...], wa_ref[...], NT, preferred_element_type=F32)
        yb = lax.dot_general(b_ref[...], wb_ref[...], NT, preferred_element_type=F32)
        ya_ref[...] = ya.astype(BF16)
        yb_ref[...] = yb.astype(BF16)
        mg_ref[...] = (_sigmoid(ga_ref[...]) * ya + _sigmoid(gb_ref[...]) * yb).astype(BF16)

    o_spec = pl.BlockSpec((tm, tn), lambda i, j: (i, j))
    o_shape = jax.ShapeDtypeStruct((T, D), BF16)
    return _pcall(
        body, grid=(T // tm, D // tn),
        in_specs=[pl.BlockSpec((tm, AW), lambda i, j: (i, 0)), pl.BlockSpec((tm, BW), lambda i, j: (i, 0)),
                  pl.BlockSpec((tn, AW), lambda i, j: (j, 0)), pl.BlockSpec((tn, BW), lambda i, j: (j, 0)),
                  pl.BlockSpec((tm, tn), lambda i, j: (i, OFF_GTA // tn + j)),
                  pl.BlockSpec((tm, tn), lambda i, j: (i, OFF_GTB // tn + j))],
        out_specs=[o_spec, o_spec, o_spec], out_shape=[o_shape, o_shape, o_shape], scratch_shapes=[], name="branch_merge",
        semantics=("parallel", "parallel"), args=[ya_pre, attn, wa_t, wb_t, proj, proj], job=job)


def _ij(i, j, k):
    return (i, j)


def _local_step(x, tgt, mod, g1, g2, lbl, og, qg, kg, sk, shards, c_arr, update):
    win_s, wa_s, wb_s, wout_s, wmi_s, wmo_s = shards
    T = x.shape[0]
    tm, tr, tt = min(1024, T), min(256, T), min(2048, T)
    tk_t = min(1024, T)
    tn = 512
    sh1, sc1, gt1, sh2, sc2, gt2 = (mod[:, i * D:(i + 1) * D] for i in range(N_MOD))
    nI = T // tm
    blk = (tm, tn)

    (h,), (win_t,) = _rms_mod_fwd("rms1_fwd", x, g1, sc1, sh1, tr, job=_gather_relay_job([win_s], alone=True))

    def epi_store(acc, ex, ou):
        ou[0][...] = acc.astype(ou[0].dtype)

    tm2 = min(2048, T)
    blk2 = (tm2, tn)

    full = lambda s: (0, s.shape[0])
    last = wmi_s.shape[0]
    gather = _gather_relay_job
    (proj,), (wa_t, wb_t, w_out, wmi_part) = _mm(
        "in_proj", "nt", [(h, D)], win_t, T, IN_W, D, tm2, tn, D, [], [((T, IN_W), F32, blk2, _ij)], epi_store,
        job=gather([wa_s, wb_s, wout_s, wmi_s], rows=[full(wa_s), full(wb_s), full(wout_s), (0, MI_CUTS[0])]))
    (ya_pre, st), (wmi_part,) = _hgrn_fwd(
        proj, lbl, og, tt, job=gather([wmi_s], rows=[MI_CUTS], into=[wmi_part]))
    (attn,), (wmi_t, wmo_part) = _swa_fwd(
        proj, qg, kg, sk, job=gather([wmi_s, wmo_s], rows=[(MI_CUTS[1], last), (0, MO_CUT)], into=[wmi_part, None]))
    (ya, yb, merged), _ = _branch_merge(ya_pre, attn, wa_t, wb_t, proj, tm, tn)

    def residual_rows(acc, vecs, bufs, parts):
        gt, gain, sc, sh = (v[...] for v in vecs)
        x_buf, mo_buf, h2_buf = bufs
        rows = min(64, tm)
        for r0 in range(0, tm, rows):
            rs = slice(r0, r0 + rows)
            z = acc[rs, :]
            mo_buf[rs, :] = z.astype(BF16)
            x1v = x_buf[rs, :] + gt * z
            x_buf[rs, :] = x1v
            rstd = lax.rsqrt(jnp.mean(x1v * x1v, axis=-1, keepdims=True) + EPS)
            h2_buf[rs, :] = ((x1v * rstd * gain) * (1.0 + sc) + sh).astype(BF16)

    x1, mo, h2 = _rows_mm("out_proj", [(merged, D)], w_out, T, tm, min(1024, D), [gt1, g2, sc2, sh2],
                          [(F32, x, True), (BF16, None, True), (BF16, None, True)], [], residual_rows)

    def epi_relu2(acc, ex, ou):
        r = jnp.maximum(acc, 0.0)
        ou[0][...] = r.astype(BF16)
        ou[1][...] = (r * r).astype(BF16)

    (r, a), (w_mo,) = _mm("mlp_in", "nt", [(h2, D)], wmi_t, T, HID, D, tm2, tn, D, [],
                          [((T, HID), BF16, blk2, _ij), ((T, HID), BF16, blk2, _ij)], epi_relu2,
                          job=gather([wmo_s], rows=[(MO_CUT, last)], into=[wmo_part]))

    def loss_rows(acc, vecs, bufs, parts):
        gt = vecs[0][...]
        x1_buf, t_buf, dz_buf = bufs
        rows = min(64, tm)
        loss_sum, gate_sum = jnp.zeros((8, D), F32), jnp.zeros((8, D), F32)
        for r0 in range(0, tm, rows):
            rs = slice(r0, r0 + rows)
            z = acc[rs, :]
            e = x1_buf[rs, :] + gt * z - t_buf[rs, :]
            dy = e * (1.0 / D)
            t_buf[rs, :] = dy
            dz_buf[rs, :] = (gt * dy).astype(BF16)
            loss_sum = loss_sum + _fold8(e * e)
            gate_sum = gate_sum + _fold8(dy * z)
        parts[0][...] = loss_sum * (0.5 / D)
        parts[1][...] = gate_sum

    part_rows = ((nI * 8, D), F32, (8, D), lambda i, j, k: (i, 0))
    dy, dz, p_loss, p_gt2 = _rows_mm(
        "mlp_out", [(a, HID)], w_mo, T, tm, 1024, [gt2], [(F32, x1, False), (F32, tgt, True), (BF16, None, True)],
        [part_rows, part_rows], loss_rows)

    def epi_du(acc, ex, ou):
        ou[0][...] = (acc * (2.0 * ex[0][...].astype(F32))).astype(BF16)

    (du,) = _mm("mlp_out_dx", "nt", [(dz, D)], w_mo, T, HID, D, tm2, tn, D, [(r, blk2, _ij)],
                [((T, HID), BF16, blk2, _ij)], epi_du)
    gblk = (1024, 1024)
    gwide = (1024, D)
    pair_sum = lambda nm, g, r1: _pair_sum("pair_sum_" + nm, g, r1, c_arr, _sum_rows(r1.shape[1]))
    (g_mo,) = _mm("mlp_out_dw", "tn", [(a, HID)], dz, HID, D, T, 1024, D, tk_t, [], [((HID, D), BF16, gwide, _ij)], epi_store)
    (dh2,), (r1_mo,) = _mm("mlp_in_dx", "nn", [(du, HID)], wmi_t, T, D, HID, tm, D, 1024, [],
                           [((T, D), F32, (tm, D), _ij)], epi_store, job=_pair_job([g_mo]))
    dx1, p_sh2, p_sc2, p_g2, dmo, p_gt1 = _rms_mod_bwd("rms2_bwd", dh2, x1, g2, sc2, dy, tr, gate=gt1, mo=mo)
    s_mo = pair_sum("mlp_out", g_mo, r1_mo)
    near, far = (1, 2), (3,)
    (g_mi,), (rn_mo,) = _mm("mlp_in_dw", "tn", [(du, HID)], h2, HID, D, T, 1024, D, tk_t, [],
                            [((HID, D), BF16, gwide, _ij)], epi_store, job=_chip_job([s_mo], near))

    def epi_gates(acc, ex, ou):
        ya_ref, yb_ref, ga_ref, gb_ref = ex
        sa, sb = _sigmoid(ga_ref[...]), _sigmoid(gb_ref[...])
        ou[0][...] = (acc * sa).astype(BF16)
        ou[1][...] = (acc * sb).astype(BF16)
        ou[2][...] = (acc * ya_ref[...].astype(F32) * (sa * (1.0 - sa))).astype(BF16)
        ou[3][...] = (acc * yb_ref[...].astype(F32) * (sb * (1.0 - sb))).astype(BF16)

    o_bf = ((T, D), BF16, blk, _ij)
    (dya, dyb, dga, dgb), (rf_mo, r1_mi) = _mm(
        "out_proj_dx", "nt", [(dmo, D)], w_out, T, D, D, tm, tn, D,
        [(ya, blk, _ij), (yb, blk, _ij), (proj, blk, lambda i, j, k: (i, OFF_GTA // tn + j)),
         (proj, blk, lambda i, j, k: (i, OFF_GTB // tn + j))], [o_bf, o_bf, o_bf, o_bf], epi_gates,
        job=_both(_chip_job([s_mo], far), _pair_job([g_mi])))
    s_mi = pair_sum("mlp_in", g_mi, r1_mi)
    (g_out,) = _mm("out_proj_dw", "tn", [(merged, D)], dmo, D, D, T, 1024, 1024, tk_t, [], [((D, D), BF16, gblk, _ij)], epi_store)
    dya_pre, dattn = _twin_mm("branch_dx", "nn", [(dya, wa_t), (dyb, wb_t)], T, AW, D, tm, tn, D, F32)
    g_a, g_b = _twin_mm("branch_dw", "tn", [(dya, ya_pre), (dyb, attn)], D, AW, T, 1024, 1024, tk_t, BF16)
    (dqa, dfa, dia, dgg, p_lb, p_og), (rn_mi, r1_out, r1_a, r1_b) = _hgrn_bwd(
        proj, st, dya_pre, lbl, og, tt, job=_both(_chip_job([s_mi], near), _pair_job([g_out, g_a, g_b])))
    (dqb, dkv, p_qg, p_kg, p_sk), (rf_mi,) = _swa_bwd(proj, dattn, qg, kg, sk, job=_chip_job([s_mi], far))
    s_out, s_a, s_b = pair_sum("out", g_out, r1_out), pair_sum("branch_a", g_a, r1_a), pair_sum("branch_b", g_b, r1_b)
    pieces = [(dqa, AW), (dfa, AW), (dia, AW), (dgg, AW), (dqb, BW), (dkv, 2 * KVW), (dga, D), (dgb, D)]
    (g_in,), (r2_out, r2_a, r2_b) = _pieces_tn("in_proj_dw", pieces, h, 512, job=_chip_job([s_out, s_a, s_b]))
    (r1_in,) = update("w_mlp_in", s_mi, [rn_mi, rf_mi], job=_pair_job([g_in]))
    s_in = pair_sum("in", g_in, r1_in)
    (dx, p_sh1, p_sc1, p_g1), (r2_in,) = _pieces_nn_rms(
        "in_proj_dx", pieces, win_t, x, g1, sc1, dx1, tm, 512, job=_chip_job([s_in]))

    partials = dict(sh1=p_sh1, sc1=p_sc1, gt1=p_gt1, sh2=p_sh2, sc2=p_sc2, gt2=p_gt2, g1=p_g1, g2=p_g2,
                    lb=p_lb, og=p_og, qg=p_qg, kg=p_kg, sk=p_sk, loss=p_loss)
    sums = dict(w_in=(s_in, [r2_in]), w_branch_a=(s_a, [r2_a]), w_branch_b=(s_b, [r2_b]), w_out=(s_out, [r2_out]),
                w_mlp_in=(s_mi, [rn_mi, rf_mi]), w_mlp_out=(s_mo, [rn_mo, rf_mo]))
    return dx, sums, partials


def _exchange_slots(buf, send_sems, recv_sems):
    me = _mesh_pos()
    mine = buf.at[_index(me)]
    sends = []
    for k in range(1, N_DEV):
        cp = pltpu.make_async_remote_copy(src_ref=mine, dst_ref=mine, send_sem=send_sems.at[k - 1],
                                          recv_sem=recv_sems.at[k - 1], device_id=_flip(me, k), device_id_type=MESH)
        cp.start()
        sends.append(cp)
    for k in range(1, N_DEV):
        theirs = buf.at[_index(_flip(me, k))]
        pltpu.make_async_remote_copy(src_ref=theirs, dst_ref=theirs, send_sem=send_sems.at[k - 1],
                                     recv_sem=recv_sems.at[k - 1], device_id=_flip(me, k), device_id_type=MESH).wait_recv()
    for cp in sends:
        cp.wait_send()


ADA_W = N_MOD * D // N_DEV


def _ada_mod(c, w_ada, b_shard):
    def body(c_ref, w_ref, b_ref, mod_ref, sc_ref, cbuf, mbuf, s1, r1, s2, r2):
        me = _index(_mesh_pos())
        cbuf[me] = c_ref[...]
        _exchange_slots(cbuf, s1, r1)
        row = lax.broadcasted_iota(jnp.int32, (N_DEV, D), 0)
        call = jnp.zeros((N_DEV, D), F32)
        for d in range(N_DEV):
            call = jnp.where(row == d, cbuf[d], call)
        sc = call * _sigmoid(call)
        sc_ref[...] = sc
        mbuf[me] = _dot(sc, w_ref[...]) + b_ref[...]
        _exchange_slots(mbuf, s2, r2)
        for s in range(N_DEV):
            mod_ref[:, s * ADA_W:(s + 1) * ADA_W] = mbuf[s, pl.ds(me, 1), :]

    return pl.pallas_call(
        body, in_specs=[_VMEM, _VMEM, _VMEM], out_specs=[_VMEM, _VMEM],
        out_shape=[jax.ShapeDtypeStruct((1, N_MOD * D), F32), jax.ShapeDtypeStruct((N_DEV, D), F32)],
        scratch_shapes=[pltpu.VMEM((N_DEV, 1, D), F32), pltpu.VMEM((N_DEV, N_DEV, ADA_W), F32),
                        _SEMS(N_DEV - 1), _SEMS(N_DEV - 1), _SEMS(N_DEV - 1), _SEMS(N_DEV - 1)],
        name="ada_mod", compiler_params=pltpu.CompilerParams(vmem_limit_bytes=VMEM_LIMIT),
    )(c, w_ada, b_shard)


SMALL_SEGS = (("b_ada", N_MOD * D), ("norm1_gain", D), ("norm2_gain", D), ("lb0", AW), ("lb1", AW),
              ("hgrn_o_gain", AW), ("q_norm_gain", 128), ("k_norm_gain", 128), ("sinks", 128))
SMALL_W = sum(w for _, w in SMALL_SEGS)
X_SEGS = (("sh1", D), ("sc1", D), ("gt1", D), ("sh2", D), ("sc2", D), ("gt2", D), ("g1", D), ("g2", D),
          ("lb", AW), ("og", AW), ("qg", 128), ("kg", 128), ("sk", 128), ("loss", 128))
X_W = sum(w for _, w in X_SEGS)


def _offsets(segs):
    out, o = {}, 0
    for name, w in segs:
        out[name] = (o, w)
        o += w
    return out


def _small_reduce(parts, lb_logits):
    xo, so = _offsets(X_SEGS), _offsets(SMALL_SEGS)
    names = [nm for nm, _ in X_SEGS]

    def body(*refs):
        p_refs = dict(zip(names, refs[:len(names)]))
        lbl_ref, allx, gs_ref, loss_ref, send_sems, recv_sems = refs[len(names):]
        me = _index(_mesh_pos())
        for nm, (o, w) in xo.items():
            if nm == "loss":
                allx[me, :, o:o + w] = jnp.broadcast_to(jnp.sum(p_refs[nm][...]), (1, w))
            else:
                allx[me, :, o:o + w] = jnp.sum(p_refs[nm][...], axis=0, keepdims=True)
        _exchange_slots(allx, send_sems, recv_sems)
        tot = allx[0]
        for d in range(1, N_DEV):
            tot = tot + allx[d]
        seg = lambda nm: tot[:, xo[nm][0]:xo[nm][0] + xo[nm][1]]

        def put(nm, v):
            gs_ref[:, so[nm][0]:so[nm][0] + so[nm][1]] = v

        put("b_ada", tot[:, 0:N_MOD * D])
        put("norm1_gain", seg("g1"))
        put("norm2_gain", seg("g2"))
        lbl = lbl_ref[...]
        lb = _sigmoid(lbl[0:1, :] - lbl[1:2, :])
        dl0 = seg("lb") * lb * (1.0 - lb)
        put("lb0", dl0)
        put("lb1", -dl0)
        put("hgrn_o_gain", seg("og"))
        put("q_norm_gain", seg("qg"))
        put("k_norm_gain", seg("kg"))
        put("sinks", seg("sk"))
        loss_ref[...] = seg("loss")

    return pl.pallas_call(
        body, in_specs=[_VMEM] * (len(names) + 1), out_specs=[_VMEM, _VMEM, _VMEM],
        out_shape=[jax.ShapeDtypeStruct((N_DEV, 1, X_W), F32), jax.ShapeDtypeStruct((1, SMALL_W), F32),
                   jax.ShapeDtypeStruct((1, 128), F32)],
        scratch_shapes=[_SEMS(N_DEV - 1), _SEMS(N_DEV - 1)], name="small_reduce",
        compiler_params=pltpu.CompilerParams(vmem_limit_bytes=VMEM_LIMIT),
    )(*[parts[nm] for nm in names], lb_logits)


def _adamw_math(w, g, m, v):
    m = B1 * m + (1.0 - B1) * g
    v = B2 * v + (1.0 - B2) * (g * g)
    m_hat = m / (1.0 - B1 ** STEP)
    v_hat = v / (1.0 - B2 ** STEP)
    return -LR * (m_hat / (jnp.sqrt(v_hat) + ADAM_EPS) + WD * w), m, v


def _sum_rows(rs):
    return 256 if rs % 256 == 0 else rs // 2


def _pair_sum(name, g, recv, c_arr, tr):
    _, rs, cols = recv.shape
    blk = (1, tr, cols)

    def body(c_ref, g_ref, r_ref, o_ref):
        o_ref[...] = (g_ref[...].astype(F32) + r_ref[...].astype(F32)).astype(BF16)

    grid_spec = pltpu.PrefetchScalarGridSpec(
        num_scalar_prefetch=1, grid=(4, rs // tr),
        in_specs=[pl.BlockSpec(blk, lambda q, i, c: (2 * q + c[0], i, 0)), pl.BlockSpec(blk, lambda q, i, c: (q, i, 0))],
        out_specs=pl.BlockSpec(blk, lambda q, i, c: (q, i, 0)))
    return pl.pallas_call(body, grid_spec=grid_spec, out_shape=jax.ShapeDtypeStruct((4, rs, cols), BF16), name=name,
                          compiler_params=_params(("parallel", "parallel")))(c_arr, g.reshape(N_DEV, rs, cols), recv)


def _sum_adamw(name, sums, recvs, q_arr, w, m, v, transposed, tile, job=None):
    rows, cols = w.shape
    nR = len(recvs)

    def body(q_ref, s_ref, *refs):
        r_refs = refs[:nR]
        w_ref, m_ref, v_ref, g_ref, d_ref, nm_ref, nv_ref = refs[nR:]
        g = s_ref[0].astype(F32)
        for r_ref in r_refs:
            for slot in range(r_ref.shape[0]):
                g = g + r_ref[slot].astype(F32)
        g = g.T if transposed else g
        g_ref[...] = g
        d_ref[...], nm_ref[...], nv_ref[...] = _adamw_math(w_ref[...], g, m_ref[...], v_ref[...])

    if transposed:
        slab = lambda n, first: pl.BlockSpec((n, cols, tile), lambda i, q: (first(q), 0, i))
    else:
        slab = lambda n, first: pl.BlockSpec((n, tile, cols), lambda i, q: (first(q), i, 0))
    spec = pl.BlockSpec((tile, cols), lambda i, q: (i, 0))
    shape = jax.ShapeDtypeStruct((rows, cols), F32)
    res, job_res = _pcall(
        body, grid=(rows // tile,),
        in_specs=[slab(1, lambda q: q[0])] + [slab(r.shape[0], lambda q: 0) for r in recvs] + [spec] * 3,
        out_specs=[spec] * 4, out_shape=[shape] * 4, scratch_shapes=[], name=name, semantics=("parallel",),
        args=[sums, *recvs, w, m, v], job=job, prefetch=[q_arr])
    return res if job is None else (res, job_res)


def _adamw(name, w, g, m, v, tr):
    rows, cols = w.shape

    def body(w_ref, g_ref, m_ref, v_ref, d_ref, nm_ref, nv_ref):
        d_ref[...], nm_ref[...], nv_ref[...] = _adamw_math(w_ref[...], g_ref[...], m_ref[...], v_ref[...])

    spec = pl.BlockSpec((tr, cols), lambda i: (i, 0))
    shape = jax.ShapeDtypeStruct((rows, cols), F32)
    return pl.pallas_call(
        body, grid=(rows // tr,), in_specs=[spec] * 4, out_specs=[spec] * 3, out_shape=[shape] * 3, name=name,
        compiler_params=_params(("parallel",)),
    )(w, g, m, v)


def _ada_update(sc_t, dmod_cols, w, m, v, tr):
    rows, cols = w.shape

    def body(s_ref, d_ref, w_ref, m_ref, v_ref, g_ref, dl_ref, nm_ref, nv_ref):
        g = jnp.dot(s_ref[...], d_ref[...], precision=lax.Precision.HIGHEST, preferred_element_type=F32)
        g_ref[...] = g
        dl_ref[...], nm_ref[...], nv_ref[...] = _adamw_math(w_ref[...], g, m_ref[...], v_ref[...])

    spec = pl.BlockSpec((tr, cols), lambda i: (i, 0))
    shape = jax.ShapeDtypeStruct((rows, cols), F32)
    return pl.pallas_call(
        body, grid=(rows // tr,),
        in_specs=[pl.BlockSpec((tr, N_DEV), lambda i: (i, 0)), pl.BlockSpec((N_DEV, cols), lambda i: (0, 0)), spec, spec, spec],
        out_specs=[spec] * 4, out_shape=[shape] * 4, name="ada_update", compiler_params=_params(("parallel",)),
    )(sc_t, dmod_cols, w, m, v)


BIG = ("w_in", "w_branch_a", "w_branch_b", "w_out", "w_mlp_in", "w_mlp_out")
COLUMN_SHARDED = ("w_in", "w_branch_a", "w_branch_b", "w_mlp_in")
AS_TRANSPOSE = ("w_in",)
WEIGHTS = ("w_ada", "b_ada", "norm1_gain", "w_in", "lb_logits", "hgrn_o_gain", "q_norm_gain", "k_norm_gain", "sinks",
           "w_branch_a", "w_branch_b", "w_out", "norm2_gain", "w_mlp_in", "w_mlp_out")


def _to_bf16(name, w, transposed, tile=256):
    rows, cols = w.shape

    def body(w_ref, o_ref):
        v = w_ref[...]
        o_ref[...] = (v.T if transposed else v).astype(BF16)

    out_spec = pl.BlockSpec((cols, tile), lambda i: (0, i)) if transposed else pl.BlockSpec((tile, cols), lambda i: (i, 0))
    return pl.pallas_call(
        body, grid=(rows // tile,), in_specs=[pl.BlockSpec((tile, cols), lambda i: (i, 0))], out_specs=out_spec,
        out_shape=jax.ShapeDtypeStruct((cols, rows) if transposed else (rows, cols), BF16), name=name,
        compiler_params=_params(("parallel",)))(w)


def _pack_small(p):
    lb = p["lb_logits"]
    src = dict(p, lb0=lb[0:1], lb1=lb[1:2])
    return jnp.concatenate([jnp.pad(src[nm], ((0, 0), (0, w - src[nm].shape[1]))) for nm, w in SMALL_SEGS], axis=1)


def _unpack_small(vec, shapes):
    so = _offsets(SMALL_SEGS)
    out = {}
    for nm, shp in shapes.items():
        if nm == "lb_logits":
            o = so["lb0"][0]
            out[nm] = vec[0, o:o + 2 * AW].reshape(2, AW)
        else:
            o = so[nm][0]
            out[nm] = vec[:, o:o + shp[1]]
    return out


def kernel(x, c, w_ada, b_ada, norm1_gain, w_in, lb_logits, hgrn_o_gain, q_norm_gain, k_norm_gain, sinks, w_branch_a, w_branch_b, w_out, norm2_gain, w_mlp_in, w_mlp_out, loss_target, m_w_ada, m_b_ada, m_norm1_gain, m_w_in, m_lb_logits, m_hgrn_o_gain, m_q_norm_gain, m_k_norm_gain, m_sinks, m_w_branch_a, m_w_branch_b, m_w_out, m_norm2_gain, m_w_mlp_in, m_w_mlp_out, v_w_ada, v_b_ada, v_norm1_gain, v_w_in, v_lb_logits, v_hgrn_o_gain, v_q_norm_gain, v_k_norm_gain, v_sinks, v_w_branch_a, v_w_branch_b, v_w_out, v_norm2_gain, v_w_mlp_in, v_w_mlp_out):
    w = dict(w_ada=w_ada, b_ada=b_ada, norm1_gain=norm1_gain, w_in=w_in, lb_logits=lb_logits, hgrn_o_gain=hgrn_o_gain,
             q_norm_gain=q_norm_gain, k_norm_gain=k_norm_gain, sinks=sinks, w_branch_a=w_branch_a, w_branch_b=w_branch_b,
             w_out=w_out, norm2_gain=norm2_gain, w_mlp_in=w_mlp_in, w_mlp_out=w_mlp_out)
    m = dict(w_ada=m_w_ada, b_ada=m_b_ada, norm1_gain=m_norm1_gain, w_in=m_w_in, lb_logits=m_lb_logits,
             hgrn_o_gain=m_hgrn_o_gain, q_norm_gain=m_q_norm_gain, k_norm_gain=m_k_norm_gain, sinks=m_sinks,
             w_branch_a=m_w_branch_a, w_branch_b=m_w_branch_b, w_out=m_w_out, norm2_gain=m_norm2_gain,
             w_mlp_in=m_w_mlp_in, w_mlp_out=m_w_mlp_out)
    v = dict(w_ada=v_w_ada, b_ada=v_b_ada, norm1_gain=v_norm1_gain, w_in=v_w_in, lb_logits=v_lb_logits,
             hgrn_o_gain=v_hgrn_o_gain, q_norm_gain=v_q_norm_gain, k_norm_gain=v_k_norm_gain, sinks=v_sinks,
             w_branch_a=v_w_branch_a, w_branch_b=v_w_branch_b, w_out=v_w_out, norm2_gain=v_norm2_gain,
             w_mlp_in=v_w_mlp_in, w_mlp_out=v_w_mlp_out)
    for d in (w, m, v):
        for nm in ("w_ada",) + BIG:
            d[nm] = d[nm][0]
    px, py, pc = _mesh_pos()
    me = _index((px, py, pc))
    c_arr = jnp.reshape(pc, (1,)).astype(jnp.int32)
    q_arr = jnp.reshape(2 * px + py, (1,)).astype(jnp.int32)

    shards = [_to_bf16("shard_" + nm, w[nm].T, False, w[nm].shape[1] // 4) if nm in AS_TRANSPOSE else
              _to_bf16("shard_" + nm, w[nm], nm in COLUMN_SHARDED) for nm in BIG]
    b_shard = lax.dynamic_slice(b_ada, (0, me * ADA_W), (1, ADA_W))
    mod, sc_all = _ada_mod(c, w["w_ada"], b_shard)

    grad, delta, new_m, new_v = {}, {}, {}, {}

    def update(nm, s, recvs, job=None):
        if nm in AS_TRANSPOSE:
            res = _sum_adamw("adamw_" + nm, s, recvs, q_arr, w[nm].T, m[nm].T, v[nm].T, False, w[nm].shape[1] // 4, job=job)
        else:
            res = _sum_adamw("adamw_" + nm, s, recvs, q_arr, w[nm], m[nm], v[nm], nm in COLUMN_SHARDED, 128, job=job)
        res, job_res = res if job is not None else (res, [])
        res = [t.T for t in res] if nm in AS_TRANSPOSE else res
        grad[nm], delta[nm], new_m[nm], new_v[nm] = res
        return job_res

    dx, sums, parts = _local_step(x[0], loss_target[0], mod, norm1_gain, norm2_gain, lb_logits, hgrn_o_gain,
                                  q_norm_gain, k_norm_gain, sinks, shards, c_arr, update)
    for nm in BIG:
        if nm not in grad:
            update(nm, *sums[nm])

    allx, g_small, loss = _small_reduce(parts, lb_logits)

    dmod_cols = lax.dynamic_slice(allx[:, 0, :], (0, me * ADA_W), (N_DEV, ADA_W))
    grad["w_ada"], delta["w_ada"], new_m["w_ada"], new_v["w_ada"] = _ada_update(
        sc_all.T, dmod_cols, w["w_ada"], m["w_ada"], v["w_ada"], 256)

    small_names = [nm for nm in WEIGHTS if nm not in BIG and nm != "w_ada"]
    shapes = {nm: w[nm].shape for nm in small_names}
    ds, ms, vs = _adamw("adamw_small", _pack_small(w), g_small, _pack_small(m), _pack_small(v), 1)
    for dst, vec in ((grad, g_small), (delta, ds), (new_m, ms), (new_v, vs)):
        dst.update(_unpack_small(vec, shapes))

    def full(d, nm):
        return d[nm][None] if nm in BIG or nm == "w_ada" else d[nm]

    return (loss[0, 0], dx[None], *[full(grad, nm) for nm in WEIGHTS], *[full(delta, nm) for nm in WEIGHTS],
            *[full(new_m, nm) for nm in WEIGHTS], *[full(new_v, nm) for nm in WEIGHTS])
```

```python
import functools

import jax
import jax.numpy as jnp
from jax import lax
from jax.experimental import pallas as pl
from jax.experimental.pallas import tpu as pltpu

F32 = jnp.float32
BF16 = jnp.bfloat16
MESH = pl.DeviceIdType.MESH

N_DEV = 8
D = 2048
A_HEADS, A_HD, CHUNK = 8, 128, 64
AW = A_HEADS * A_HD
Q_HEADS, KV_HEADS, GROUP, B_HD, BLK = 16, 4, 4, 64, 128
BW = Q_HEADS * B_HD
KVW = KV_HEADS * B_HD
HID = 4 * D
IN_W = 4 * AW + BW + 2 * KVW + 2 * D
OFF_QA, OFF_FA, OFF_IA, OFF_GA = 0, AW, 2 * AW, 3 * AW
OFF_QB = 4 * AW
OFF_KB = OFF_QB + BW
OFF_VB = OFF_KB + KVW
OFF_GTA = OFF_VB + KVW
OFF_GTB = OFF_GTA + D
N_MOD = 6
EPS = 1e-6
LR, B1, B2, ADAM_EPS, WD, STEP = 1e-3, 0.9, 0.999, 1e-8, 0.01, 10
NEG = -1e30

VMEM_LIMIT = 56 * 1024 * 1024
MI_CUT = 544

NN = (((1,), (0,)), ((), ()))
NT = (((1,), (1,)), ((), ()))
TN = (((0,), (0,)), ((), ()))
BNN = (((2,), (1,)), ((0,), (0,)))
BNT = (((2,), (2,)), ((0,), (0,)))
BTN = (((1,), (1,)), ((0,), (0,)))


def _dot(a, b, dims=NN):
    return lax.dot_general(a.astype(BF16), b.astype(BF16), dims, preferred_element_type=F32)


def _params(sem):
    return pltpu.CompilerParams(dimension_semantics=sem, vmem_limit_bytes=VMEM_LIMIT)


def _sigmoid(x):
    return 1.0 / (1.0 + jnp.exp(-x))


def _fold8(v):
    r, n = v.shape
    return jnp.sum(v.reshape(r // 8, 8, n), axis=0)


_VMEM = pl.BlockSpec(memory_space=pltpu.VMEM)
_ANY = pl.BlockSpec(memory_space=pl.ANY)
_SEMS = lambda n: pltpu.SemaphoreType.DMA((n,))


def _mesh_pos():
    return lax.axis_index("x"), lax.axis_index("y"), lax.axis_index("c")


def _flip(pos, k):
    return tuple(1 - p if (k >> s) & 1 else p for p, s in zip(pos, (2, 1, 0)))


def _index(pos):
    return 4 * pos[0] + 2 * pos[1] + pos[2]


class _Job:
    def __init__(self, ins, out_shape, sems, start, finish, aliases=None, middle=None):
        self.ins, self.out_shape, self.sems, self.start, self.finish = list(ins), list(out_shape), list(sems), start, finish
        self.aliases = dict(aliases or {})
        self.middle = middle


def _both(j1, j2):
    assert not j1.aliases and not j2.aliases
    n_in, n_out, n_sem = len(j1.ins), len(j1.out_shape), len(j1.sems)
    first = lambda ins, outs, sems: (ins[:n_in], outs[:n_out], sems[:n_sem])
    second = lambda ins, outs, sems: (ins[n_in:], outs[n_out:], sems[n_sem:])

    def start(*refs):
        j1.start(*first(*refs))
        j2.start(*second(*refs))

    def finish(*refs):
        j1.finish(*first(*refs))
        j2.finish(*second(*refs))

    return _Job(j1.ins + j2.ins, j1.out_shape + j2.out_shape, j1.sems + j2.sems, start, finish)


def _pcall(body, *, grid, in_specs, out_specs, out_shape, scratch_shapes, name, semantics, args, job=None, prefetch=()):
    n_pre = len(prefetch)

    def call(fn, in_specs_, out_specs_, out_shape_, scratch_, sem, operands, aliases):
        if n_pre:
            spec = pltpu.PrefetchScalarGridSpec(num_scalar_prefetch=n_pre, grid=grid, in_specs=in_specs_,
                                                out_specs=out_specs_, scratch_shapes=scratch_)
            return pl.pallas_call(fn, grid_spec=spec, out_shape=out_shape_, name=name, input_output_aliases=aliases,
                                  compiler_params=_params(sem))(*prefetch, *operands)
        return pl.pallas_call(fn, grid=grid, in_specs=in_specs_, out_specs=out_specs_, out_shape=out_shape_,
                              scratch_shapes=scratch_, name=name, input_output_aliases=aliases,
                              compiler_params=_params(sem))(*operands)

    if job is None:
        return list(call(body, in_specs, out_specs, out_shape, scratch_shapes, semantics, args, {})), []
    n_in, n_out, n_scr = len(in_specs), len(out_specs), len(scratch_shapes)
    j_in, j_out = len(job.ins), len(job.out_shape)
    steps = tuple(grid)

    def carrier(*refs):
        pre, refs = refs[:n_pre], refs[n_pre:]
        o = 0
        main_in, o = refs[o:o + n_in], o + n_in
        job_in, o = refs[o:o + j_in], o + j_in
        main_out, o = refs[o:o + n_out], o + n_out
        job_out, o = refs[o:o + j_out], o + j_out
        main_scr, job_sems = refs[o:o + n_scr], refs[o + n_scr:]
        ids = [pl.program_id(a) for a in range(len(steps))]
        first = functools.reduce(lambda p, q: p & q, [i == 0 for i in ids])
        last = functools.reduce(lambda p, q: p & q, [i == s - 1 for i, s in zip(ids, steps)])

        @pl.when(first)
        def _():
            job.start(job_in, job_out, job_sems)

        if job.middle is not None:
            flat, total = 0, 1
            for i, s in zip(ids, steps):
                flat, total = flat * s + i, total * s

            @pl.when(flat == total // 2)
            def _():
                job.middle(job_in, job_out, job_sems)

        body(*pre, *main_in, *main_out, *main_scr)

        @pl.when(last)
        def _():
            job.finish(job_in, job_out, job_sems)

    outs = call(carrier, list(in_specs) + [_ANY] * j_in, list(out_specs) + [_ANY] * j_out,
                list(out_shape) + job.out_shape, list(scratch_shapes) + job.sems, ("arbitrary",) * len(steps),
                list(args) + job.ins, {n_pre + n_in + i: n_out + o for i, o in job.aliases.items()})
    return list(outs[:n_out]), list(outs[n_out:])


def _gather_relay_job(shards, rows=None, into=None, alone=False):
    n = len(shards)
    rows = rows or [(0, s.shape[0]) for s in shards]
    into = into or [None] * n
    olds, aliases = [], {}
    for a, buf in enumerate(into):
        if buf is not None:
            aliases[n + len(olds)] = a
            olds.append(buf)

    def tools(ins, outs, sems):
        send_sems, recv_sems, local_sems = sems
        x, y, c = _mesh_pos()
        q = 2 * x + y
        chip_at = lambda rel: (1 - x if rel & 2 else x, 1 - y if rel & 1 else y)

        def part(a, chip, core):
            rs, (r0, r1) = shards[a].shape[0], rows[a]
            return outs[a].at[pl.ds((2 * chip + core) * rs + r0, r1 - r0), :]

        own = lambda a: ins[a].at[pl.ds(rows[a][0], rows[a][1] - rows[a][0]), :]

        def copy(a, slot, chip, core, to, src=None):
            blk = part(a, chip, core)
            return pltpu.make_async_remote_copy(src_ref=blk if src is None else src, dst_ref=blk,
                                                send_sem=send_sems.at[7 * a + slot], recv_sem=recv_sems.at[7 * a + slot],
                                                device_id=to, device_id_type=MESH)

        mine = [pltpu.make_async_copy(own(a), part(a, q, c), local_sems.at[a]) for a in range(n)]
        first = [copy(a, slot, q, c, (x, y, 1 - c) if slot == 0 else (*chip_at(slot), c), src=own(a))
                 for a in range(n) for slot in (0, 1, 2)]
        return x, y, c, q, chip_at, copy, mine, first

    def start(ins, outs, sems):
        *_, mine, first = tools(ins, outs, sems)
        for cp in mine + first:
            cp.start()

    def middle(ins, outs, sems):
        x, y, c, q, chip_at, copy, _, _ = tools(ins, outs, sems)
        me, sib = (x, y, c), (x, y, 1 - c)

        def relay(src, dst):
            for a in range(n):
                copy(a, src, q ^ src, c, me).wait_recv()
                copy(a, 3, q ^ src, c, (*chip_at(dst), c)).start()
                copy(a, 3 + src, q ^ src, c, sib).start()
            for a in range(n):
                copy(a, dst, q ^ dst, c, me).wait_recv()
                copy(a, 3 + dst, q ^ dst, c, sib).start()

        pl.when(c == 1)(lambda: relay(1, 2))
        pl.when(c == 0)(lambda: relay(2, 1))

    def finish(ins, outs, sems):
        if alone:
            middle(ins, outs, sems)
        x, y, c, q, chip_at, copy, mine, first = tools(ins, outs, sems)
        me, sib = (x, y, c), (x, y, 1 - c)
        for a in range(n):
            copy(a, 3, q ^ 3, c, me).wait_recv()
            copy(a, 6, q ^ 3, c, sib).start()
        for a in range(n):
            copy(a, 0, q, 1 - c, me).wait_recv()
            for rel in (1, 2, 3):
                copy(a, 3 + rel, q ^ rel, 1 - c, me).wait_recv()
        for a in range(n):
            for slot in range(3, 7):
                copy(a, slot, q, c, sib).wait_send()
        for cp in first:
            cp.wait_send()
        for cp in mine:
            cp.wait()

    return _Job(list(shards) + olds, [jax.ShapeDtypeStruct((N_DEV * s.shape[0], s.shape[1]), s.dtype) for s in shards],
                [_SEMS(7 * n), _SEMS(7 * n), _SEMS(n)], start, finish, aliases, middle=None if alone else middle)


def _pair_job(grads):
    n = len(grads)

    def copies(ins, outs, sems):
        send_sems, recv_sems = sems
        x, y, c = _mesh_pos()
        out = []
        for a in range(n):
            rs = grads[a].shape[0] // N_DEV
            for q in range(4):
                blk = ins[a].at[pl.ds((2 * q + 1 - c) * rs, rs), :]
                out.append(pltpu.make_async_remote_copy(
                    src_ref=blk, dst_ref=outs[a].at[q], send_sem=send_sems.at[4 * a + q], recv_sem=recv_sems.at[4 * a + q],
                    device_id=(x, y, 1 - c), device_id_type=MESH))
        return out

    def start(ins, outs, sems):
        for cp in copies(ins, outs, sems):
            cp.start()

    def finish(ins, outs, sems):
        for cp in copies(ins, outs, sems):
            cp.wait()

    return _Job(grads, [jax.ShapeDtypeStruct((4, g.shape[0] // N_DEV, g.shape[1]), g.dtype) for g in grads],
                [_SEMS(4 * n), _SEMS(4 * n)], start, finish)


def _chip_job(sums, rels=(1, 2, 3)):
    n, nr = len(sums), len(rels)

    def copies(ins, outs, sems):
        send_sems, recv_sems = sems
        x, y, c = _mesh_pos()
        out = []
        for a in range(n):
            for slot, r in enumerate(rels):
                px, py = (1 - x if r & 2 else x), (1 - y if r & 1 else y)
                out.append(pltpu.make_async_remote_copy(
                    src_ref=ins[a].at[2 * px + py], dst_ref=outs[a].at[slot], send_sem=send_sems.at[nr * a + slot],
                    recv_sem=recv_sems.at[nr * a + slot], device_id=(px, py, c), device_id_type=MESH))
        return out

    def start(ins, outs, sems):
        for cp in copies(ins, outs, sems):
            cp.start()

    def finish(ins, outs, sems):
        for cp in copies(ins, outs, sems):
            cp.wait()

    return _Job(sums, [jax.ShapeDtypeStruct((nr,) + s.shape[1:], s.dtype) for s in sums],
                [_SEMS(nr * n), _SEMS(nr * n)], start, finish)


def _mm(name, form, a_list, b, M, N, K, tm, tn, tk, extras, outs, epi, job=None):
    nI, nJ, nK = M // tm, N // tn, K // tk
    assert nI * tm == M and nJ * tn == N and nK * tk == K
    dims = {"nn": NN, "nt": NT, "tn": TN}[form]
    b_list = b if isinstance(b, list) else [(b, {"nn": N, "nt": K, "tn": N}[form])]
    nA, nB = len(a_list), len(b_list)
    assert nA == 1 or nB == 1
    assert nB == 1 or form in ("nn", "nt")
    AXIS = {"i": 0, "j": 1, "k": 2}
    a_axis, a_tile = ("i", tm) if form == "tn" else ("k", tk)
    b_axis, b_tile = ("k", tk) if form == "nt" else ("j", tn)

    def cut(pieces, tile, total):
        starts, s = [], 0
        for _, w in pieces:
            assert w % tile == 0
            starts.append(s // tile)
            s += w
        assert s == total
        return starts, [w // tile for _, w in pieces]

    a_st, a_cn = cut(a_list, a_tile, M if form == "tn" else K)
    b_st, b_cn = cut(b_list, b_tile, K if form == "nt" else N)

    def inside(idx, st, cn):
        return (idx >= st) & (idx < st + cn)

    def a_spec(p):
        st, cn = a_st[p], a_cn[p]
        if form == "tn":
            return pl.BlockSpec((tk, tm), lambda i, j, k: (jnp.where(inside(i, st, cn), k, 0), jnp.clip(i - st, 0, cn - 1)))
        return pl.BlockSpec((tm, tk), lambda i, j, k: (i, jnp.clip(k - st, 0, cn - 1)))

    def b_spec(p):
        st, cn = b_st[p], b_cn[p]
        if form == "nt":
            return pl.BlockSpec((tn, tk), lambda i, j, k: (j, jnp.clip(k - st, 0, cn - 1)))
        if nB == 1:
            return pl.BlockSpec((tk, tn), lambda i, j, k: (k, j))
        return pl.BlockSpec((tk, tn), lambda i, j, k: (jnp.where(inside(j, st, cn), k, 0), jnp.clip(j - st, 0, cn - 1)))

    in_specs = ([a_spec(p) for p in range(nA)] + [b_spec(p) for p in range(nB)]
                + [pl.BlockSpec(bs, im) for _, bs, im in extras])
    out_shape = [jax.ShapeDtypeStruct(s_, d_) for s_, d_, _, _ in outs]
    out_specs = [pl.BlockSpec(bs, im) for _, _, bs, im in outs]
    nE, nO = len(extras), len(outs)
    single = nA == 1 and nB == 1

    def body(*refs):
        a_refs, b_refs = refs[:nA], refs[nA:nA + nB]
        ex, ou = refs[nA + nB:nA + nB + nE], refs[nA + nB + nE:nA + nB + nE + nO]
        ids = [pl.program_id(a) for a in range(3)]

        def partial_of(p, q):
            return lax.dot_general(a_refs[p][...], b_refs[q][...], dims, preferred_element_type=F32)

        if nK == 1 and single:
            epi(partial_of(0, 0), ex, ou)
            return
        acc = refs[-1]
        k = ids[2]
        for p in range(nA):
            for q in range(nB):
                def first(p=p, q=q):
                    acc[...] = partial_of(p, q)

                def later(p=p, q=q):
                    acc[...] += partial_of(p, q)

                here = None
                if nA > 1:
                    here = inside(ids[AXIS[a_axis]], a_st[p], a_cn[p])
                if nB > 1:
                    here = inside(ids[AXIS[b_axis]], b_st[q], b_cn[q])
                pl.when(k == 0 if here is None else here & (k == 0))(first)
                pl.when(k > 0 if here is None else here & (k > 0))(later)

        @pl.when(k == nK - 1)
        def _():
            epi(acc[...], ex, ou)

    scratch = [] if (nK == 1 and single) else [pltpu.VMEM((tm, tn), F32)]
    res, job_res = _pcall(
        body, grid=(nI, nJ, nK), in_specs=in_specs, out_specs=out_specs, out_shape=out_shape, scratch_shapes=scratch,
        name=name, semantics=("parallel", "parallel", "arbitrary"),
        args=[a for a, _ in a_list] + [p for p, _ in b_list] + [e for e, _, _ in extras], job=job)
    return res if job is None else (res, job_res)


def _twin_mm(name, form, pairs, M, N, K, tm, tn, tk, out_dtype):
    nI, nJ, nK = M // tm, N // tn, K // tk
    dims = {"nn": NN, "tn": TN}[form]
    a_spec = (pl.BlockSpec((tm, tk), lambda i, j, k: (i, k)) if form == "nn" else pl.BlockSpec((tk, tm), lambda i, j, k: (k, i)))
    b_spec = pl.BlockSpec((tk, tn), lambda i, j, k: (k, j))
    o_spec = pl.BlockSpec((tm, tn), lambda i, j, k: (i, j))

    def body(a1, b1, a2, b2, o1, o2, *accs):
        k = pl.program_id(2)
        for a_ref, b_ref, o_ref, acc in ((a1, b1, o1, accs[0] if accs else None), (a2, b2, o2, accs[1] if accs else None)):
            part = lax.dot_general(a_ref[...], b_ref[...], dims, preferred_element_type=F32)
            if nK == 1:
                o_ref[...] = part.astype(out_dtype)
                continue

            @pl.when(k == 0)
            def _(acc=acc, part=part):
                acc[...] = part

            @pl.when(k > 0)
            def _(acc=acc, part=part):
                acc[...] += part

            @pl.when(k == nK - 1)
            def _(acc=acc, o_ref=o_ref):
                o_ref[...] = acc[...].astype(out_dtype)

    (a1, b1), (a2, b2) = pairs
    shape = jax.ShapeDtypeStruct((M, N), out_dtype)
    return pl.pallas_call(
        body, grid=(nI, nJ, nK), in_specs=[a_spec, b_spec, a_spec, b_spec], out_specs=[o_spec, o_spec],
        out_shape=[shape, shape], scratch_shapes=[] if nK == 1 else [pltpu.VMEM((tm, tn), F32)] * 2, name=name,
        compiler_params=_params(("parallel", "parallel", "arbitrary")))(a1, b1, a2, b2)


def _piece_tiles(pieces, tile):
    starts, s = [], 0
    for _, w in pieces:
        assert w % tile == 0
        starts.append(s // tile)
        s += w
    return starts, [w // tile for _, w in pieces], s


def _pieces_tn(name, pieces, b, tile, job=None):
    T, N = b.shape
    st, cn, M = _piece_tiles(pieces, tile)
    nP, nI = len(pieces), M // tile

    def body(*refs):
        p_refs, b_hbm, o_ref = refs[:nP], refs[nP], refs[nP + 1]
        bbuf, abuf, bsem, asem = refs[nP + 2:]
        i = pl.program_id(0)

        def fetch(step, slot):
            for p in range(nP):
                @pl.when((step >= st[p]) & (step < st[p] + cn[p]))
                def _():
                    col = pl.multiple_of((step - st[p]) * tile, tile)
                    pltpu.make_async_copy(p_refs[p].at[pl.ds(0, T), pl.ds(col, tile)], abuf.at[slot], asem.at[slot]).start()

        @pl.when(i == 0)
        def _():
            whole = pltpu.make_async_copy(b_hbm, bbuf, bsem)
            whole.start()
            fetch(0, 0)
            whole.wait()

        @pl.when(i + 1 < nI)
        def _():
            fetch(i + 1, (i + 1) % 2)

        pltpu.make_async_copy(p_refs[0].at[pl.ds(0, T), pl.ds(0, tile)], abuf.at[i % 2], asem.at[i % 2]).wait()
        o_ref[...] = lax.dot_general(abuf[i % 2], bbuf[...], TN, preferred_element_type=F32).astype(BF16)

    res, job_res = _pcall(
        body, grid=(nI,), in_specs=[_ANY] * (nP + 1), out_specs=[pl.BlockSpec((tile, N), lambda i: (i, 0))],
        out_shape=[jax.ShapeDtypeStruct((M, N), BF16)],
        scratch_shapes=[pltpu.VMEM((T, N), b.dtype), pltpu.VMEM((2, T, tile), b.dtype), pltpu.SemaphoreType.DMA, _SEMS(2)],
        name=name, semantics=("arbitrary",), args=[p for p, _ in pieces] + [b], job=job)
    return res if job is None else (res, job_res)


def _rows_mm(name, pieces, w, T, tm, tk, vecs, bufs, parts, epi, job=None):
    st, cn, K = _piece_tiles(pieces, tk)
    nP, nI, nK = len(pieces), T // tm, K // tk
    part_specs = [pl.BlockSpec(bs, lambda i, k, im=im: im(i, 0, k)) for _, _, bs, im in parts]
    n_vec, nB = len(vecs), len(bufs)
    load_ix = [n for n, (_, src, _) in enumerate(bufs) if src is not None]
    store_ix = [n for n, (_, _, store) in enumerate(bufs) if store]
    n_any_in, n_any_out = len(load_ix), len(store_ix)

    def body(*refs):
        o = nP
        p_refs, w_ref = refs[:nP], refs[o]
        vec_refs = refs[o + 1:o + 1 + n_vec]
        ins = refs[o + 1 + n_vec:o + 1 + n_vec + n_any_in]
        o = o + 1 + n_vec + n_any_in
        hbm_outs, p_outs = refs[o:o + n_any_out], refs[o + n_any_out:o + n_any_out + len(parts)]
        o = o + n_any_out + len(parts)
        acc, abuf = refs[o:o + 2]
        buf_refs = refs[o + 2:o + 2 + nB]
        asem, in_sems, out_sems = refs[-3:]
        i, k = pl.program_id(0), pl.program_id(1)
        g = i * nK + k
        rows_of = lambda ref, ii: ref.at[pl.ds(pl.multiple_of(ii * tm, tm), tm), :]
        bufs_in = [buf_refs[n] for n in load_ix]
        bufs_out = [buf_refs[n] for n in store_ix]

        def fetch(ii, kk, slot):
            for p in range(nP):
                @pl.when((kk >= st[p]) & (kk < st[p] + cn[p]))
                def _():
                    col = pl.multiple_of((kk - st[p]) * tk, tk)
                    src = p_refs[p].at[pl.ds(pl.multiple_of(ii * tm, tm), tm), pl.ds(col, tk)]
                    pltpu.make_async_copy(src, abuf.at[slot], asem.at[slot]).start()

        loads = lambda ii: [pltpu.make_async_copy(rows_of(src, ii), buf, in_sems.at[n])
                            for n, (src, buf) in enumerate(zip(ins, bufs_in))]
        stores = lambda ii: [pltpu.make_async_copy(buf, rows_of(dst, ii), out_sems.at[n])
                             for n, (buf, dst) in enumerate(zip(bufs_out, hbm_outs))]

        @pl.when(g == 0)
        def _():
            fetch(0, 0, 0)

        @pl.when(g + 1 < nI * nK)
        def _():
            last_k = k == nK - 1
            fetch(jnp.where(last_k, i + 1, i), jnp.where(last_k, 0, k + 1), (g + 1) % 2)

        @pl.when(k == 0)
        def _():
            @pl.when(i > 0)
            def _():
                for cp in stores(i - 1):
                    cp.wait()
            for cp in loads(i):
                cp.start()

        pltpu.make_async_copy(p_refs[0].at[pl.ds(0, tm), pl.ds(0, tk)], abuf.at[g % 2], asem.at[g % 2]).wait()

        def product(cols):
            return jnp.dot(abuf[g % 2], w_ref[:, cols], preferred_element_type=F32)

        col_blocks = [slice(c0, c0 + 512) for c0 in range(0, D, 512)]

        @pl.when(k == 0)
        def _():
            for cols in col_blocks:
                acc[:, cols] = product(cols)

        @pl.when(k > 0)
        def _():
            for cols in col_blocks:
                acc[:, cols] += product(cols)

        @pl.when(k == nK - 1)
        def _():
            for cp in loads(i):
                cp.wait()
            epi(acc, vec_refs, buf_refs, p_outs)
            for cp in stores(i):
                cp.start()

            @pl.when(i == nI - 1)
            def _():
                for cp in stores(i):
                    cp.wait()

    vec = pl.BlockSpec((1, D), lambda i, k: (0, 0))
    scratch = ([pltpu.VMEM((tm, D), F32), pltpu.VMEM((2, tm, tk), BF16)] + [pltpu.VMEM((tm, D), dt) for dt, _, _ in bufs]
               + [_SEMS(2), _SEMS(n_any_in), _SEMS(n_any_out)])
    res, job_res = _pcall(
        body, grid=(nI, nK),
        in_specs=[_ANY] * nP + [pl.BlockSpec((tk, D), lambda i, k: (k, 0))] + [vec] * n_vec + [_ANY] * n_any_in,
        out_specs=[_ANY] * n_any_out + part_specs,
        out_shape=([jax.ShapeDtypeStruct((T, D), bufs[n][0]) for n in store_ix]
                   + [jax.ShapeDtypeStruct(s, d) for s, d, _, _ in parts]),
        scratch_shapes=scratch, name=name, semantics=("arbitrary", "arbitrary"),
        args=[p for p, _ in pieces] + [w] + list(vecs) + [bufs[n][1] for n in load_ix], job=job)
    return res if job is None else (res, job_res)


def _pieces_nn_rms(name, pieces, w, x, gain, sc, dres, tm, tk, job=None):
    _, outs, epi = _rms_mod_bwd_epilogue(x, gain, sc, dres, tm)

    def on_rows(acc, vecs, bufs, parts):
        epi(acc, [bufs[0], vecs[0], vecs[1], bufs[1]], [bufs[1], *parts])

    return _rows_mm(name, pieces, w, x.shape[0], tm, tk, [gain, sc], [(F32, x, False), (F32, dres, True)],
                    outs[1:], on_rows, job=job)


def _rms_mod_fwd(name, x, gain, sc, sh, tr, job=None):
    T = x.shape[0]

    def body(x_ref, g_ref, sc_ref, sh_ref, h_ref):
        xv = x_ref[...]
        rstd = lax.rsqrt(jnp.mean(xv * xv, axis=-1, keepdims=True) + EPS)
        h_ref[...] = ((xv * rstd * g_ref[...]) * (1.0 + sc_ref[...]) + sh_ref[...]).astype(BF16)

    row = pl.BlockSpec((tr, D), lambda i: (i, 0))
    vec = pl.BlockSpec((1, D), lambda i: (0, 0))
    return _pcall(body, grid=(T // tr,), in_specs=[row, vec, vec, vec], out_specs=[row],
                  out_shape=[jax.ShapeDtypeStruct((T, D), BF16)], scratch_shapes=[], name=name, semantics=("parallel",),
                  args=[x, gain, sc, sh], job=job)


def _rms_mod_bwd_epilogue(x, gain, sc, dres, tm, gate=None, mo=None):
    T = x.shape[0]
    with_gate = gate is not None
    row = ((tm, D), lambda i, j, k: (i, 0))
    vec = ((1, D), lambda i, j, k: (0, 0))
    part = ((T // tm * 8, D), F32, (8, D), lambda i, j, k: (i, 0))
    extras = [(x, *row), (gain, *vec), (sc, *vec), (dres, *row)]
    outs = [((T, D), F32, *row), part, part, part]
    if with_gate:
        extras += [(gate, *vec), (mo, *row)]
        outs += [((T, D), BF16, *row), part]

    rows = min(64, tm)

    def epi(acc, ex, ou):
        g = ex[1][...]
        sums = [jnp.zeros((8, D), F32) for _ in range(4)]
        for r0 in range(0, tm, rows):
            rs = slice(r0, r0 + rows)
            dhv, xv = acc[rs, :], ex[0][rs, :]
            rstd = lax.rsqrt(jnp.mean(xv * xv, axis=-1, keepdims=True) + EPS)
            xhat = xv * rstd
            dn = dhv * (1.0 + ex[2][...])
            dxhat = dn * g
            dx = ex[3][rs, :] + rstd * (dxhat - xhat * jnp.mean(dxhat * xhat, axis=-1, keepdims=True))
            ou[0][rs, :] = dx
            terms = [dhv, dhv * (xhat * g), dn * xhat]
            if with_gate:
                terms.append(dx * ex[5][rs, :].astype(F32))
                ou[4][rs, :] = (ex[4][...] * dx).astype(BF16)
            sums = [s + _fold8(t) for s, t in zip(sums, terms)] + sums[len(terms):]
        ou[1][...], ou[2][...], ou[3][...] = sums[:3]
        if with_gate:
            ou[5][...] = sums[3]

    return extras, outs, epi


def _rms_mod_bwd(name, dh, x, gain, sc, dres, tr, gate=None, mo=None):
    T = x.shape[0]
    extras, outs, epi = _rms_mod_bwd_epilogue(x, gain, sc, dres, tr, gate, mo)
    rows_only = lambda im: (lambda i: im(i, 0, 0))
    nE = len(extras)

    def body(dh_ref, *refs):
        epi(dh_ref, refs[:nE], refs[nE:])

    return pl.pallas_call(
        body, grid=(T // tr,),
        in_specs=[pl.BlockSpec((tr, D), lambda i: (i, 0))] + [pl.BlockSpec(bs, rows_only(im)) for _, bs, im in extras],
        out_specs=[pl.BlockSpec(bs, rows_only(im)) for _, _, bs, im in outs],
        out_shape=[jax.ShapeDtypeStruct(s, d) for s, d, _, _ in outs], name=name, compiler_params=_params(("parallel",)),
    )(dh, *[e for e, _, _ in extras])


def _split3(v):
    h = v.astype(BF16)
    r1 = v - h.astype(F32)
    m = r1.astype(BF16)
    lo = (r1 - m.astype(F32)).astype(BF16)
    return h, m, lo


def _tri_mm(tri, v, dims=NN):
    h, m, lo = _split3(v)
    t = tri.astype(BF16)
    mm = lambda p: lax.dot_general(t, p, dims, preferred_element_type=F32)
    return (mm(lo) + mm(m)) + mm(h)


def _hgrn_chunk_terms(q, fl, lb):
    sig = _sigmoid(fl)
    f = lb + (1.0 - lb) * sig
    lf = jnp.log(f)
    kk = 1.0 - f
    sq = _sigmoid(q)
    qf = q * sq
    return sig, f, lf, kk, sq, qf


def _causal(n):
    r = lax.broadcasted_iota(jnp.int32, (n, n), 0)
    c = lax.broadcasted_iota(jnp.int32, (n, n), 1)
    return r >= c


def _hgrn_fwd(proj, lb_logits, o_gain, tt, job=None):
    T = proj.shape[0]
    nT, ncl = T // tt, tt // CHUNK
    C = CHUNK

    def body(q_ref, f_ref, i_ref, g_ref, lbl_ref, og_ref, y_ref, st_ref, S):
        @pl.when(pl.program_id(1) == 0)
        def _():
            S[...] = jnp.zeros_like(S)

        lbl = lbl_ref[...]
        lb = _sigmoid(lbl[0:1, :] - lbl[1:2, :])
        og = og_ref[...]
        shp = (ncl, C, A_HD)
        q, fl, v, g = (r[...].reshape(shp) for r in (q_ref, f_ref, i_ref, g_ref))
        tri = jnp.broadcast_to(_causal(C), (ncl, C, C))
        _, _, lf, kk, _, qf = _hgrn_chunk_terms(q, fl, lb)
        b = _tri_mm(tri, lf, BNN)
        bm, bl = b[:, C // 2 - 1:C // 2, :], b[:, C - 1:C, :]
        qd, kd = qf * jnp.exp(b - bm), kk * jnp.exp(bm - b)
        A = jnp.where(tri, _dot(qd, kd, BNT), 0.0)
        d_st = _dot(v, kk * jnp.exp(bl - b), BTN)
        dec = jnp.exp(bl)
        st = S[...]
        for ci in range(ncl):
            st_ref[0, ci] = st
            st = st * dec[ci] + d_st[ci]
        S[...] = st
        o = _dot(A, v, BNN) + _dot(qf * jnp.exp(b), st_ref[0], BNT)
        r = lax.rsqrt(jnp.mean(o * o, axis=-1, keepdims=True) + EPS)
        y_ref[...] = (o * r * og * (g * _sigmoid(g))).astype(BF16).reshape(tt, A_HD)

    def col(off):
        return pl.BlockSpec((tt, A_HD), lambda h, t: (t, off // A_HD + h))

    head_vec = lambda rows: pl.BlockSpec((rows, A_HD), lambda h, t: (0, h))
    return _pcall(
        body, grid=(A_HEADS, nT),
        in_specs=[col(OFF_QA), col(OFF_FA), col(OFF_IA), col(OFF_GA), head_vec(2), head_vec(1)],
        out_specs=[pl.BlockSpec((tt, A_HD), lambda h, t: (t, h)),
                   pl.BlockSpec((1, ncl, A_HD, A_HD), lambda h, t: (h, t, 0, 0))],
        out_shape=[jax.ShapeDtypeStruct((T, AW), BF16),
                   jax.ShapeDtypeStruct((A_HEADS, T // C, A_HD, A_HD), F32)],
        scratch_shapes=[pltpu.VMEM((A_HD, A_HD), F32)], name="hgrn_fwd", semantics=("parallel", "arbitrary"),
        args=[proj, proj, proj, proj, lb_logits, o_gain], job=job)


def _hgrn_bwd(proj, st, dy, lb_logits, o_gain, tt, job=None):
    T = proj.shape[0]
    nT, ncl = T // tt, tt // CHUNK
    C = CHUNK

    def body(q_ref, f_ref, i_ref, g_ref, st_ref, dy_ref, lbl_ref, og_ref,
             dq_ref, df_ref, di_ref, dg_ref, plb_ref, pog_ref, dS):
        @pl.when(pl.program_id(1) == 0)
        def _():
            dS[...] = jnp.zeros_like(dS)

        lbl = lbl_ref[...]
        lb = _sigmoid(lbl[0:1, :] - lbl[1:2, :])
        og = og_ref[...]
        shp = (ncl, C, A_HD)
        flat = lambda t: t.reshape(tt, A_HD)
        q, fl, v, g, dout = (r[...].reshape(shp) for r in (q_ref, f_ref, i_ref, g_ref, dy_ref))
        tri = jnp.broadcast_to(_causal(C), (ncl, C, C))
        rowi = lax.broadcasted_iota(jnp.int32, shp, 1)
        st0 = st_ref[0]
        sig, f, lf, kk, sq, qf = _hgrn_chunk_terms(q, fl, lb)
        b = _tri_mm(tri, lf, BNN)
        bm, bl = b[:, C // 2 - 1:C // 2, :], b[:, C - 1:C, :]
        e_qd, e_kd, e_ke, e_b = jnp.exp(b - bm), jnp.exp(bm - b), jnp.exp(bl - b), jnp.exp(b)
        qd, kd, ke, qe = qf * e_qd, kk * e_kd, kk * e_ke, qf * e_b
        dec = jnp.exp(bl)
        A = jnp.where(tri, _dot(qd, kd, BNT), 0.0)
        o = _dot(A, v, BNN) + _dot(qe, st0, BNT)
        r = lax.rsqrt(jnp.mean(o * o, axis=-1, keepdims=True) + EPS)
        sg = _sigmoid(g)
        on = o * r * og
        dg_ref[...] = flat((dout * on * (sg * (1.0 + g * (1.0 - sg)))).astype(BF16))
        don = dout * (g * sg)
        pog_ref[...] = _fold8(flat(don * o * r))
        dyh = don * og
        do = r * (dyh - o * (r * r) * jnp.mean(dyh * o, axis=-1, keepdims=True))
        g_st = _dot(do, qe, BTN)
        run = dS[...]
        after = [None] * ncl
        for ci in reversed(range(ncl)):
            after[ci] = run
            run = g_st[ci] + run * dec[ci]
        dS[...] = run
        d_after = jnp.stack(after, axis=0)
        ddec = jnp.sum(d_after * st0, axis=1, keepdims=True)
        dqe = _dot(do, st0, BNN)
        dke = _dot(v, d_after, BNN)
        dA = jnp.where(tri, _dot(do, v, BNT), 0.0)
        dv = _dot(ke, d_after, BNT) + _dot(A, do, BTN)
        dqd = _dot(dA, kd, BNN)
        dkd = _dot(dA, qd, BTN)
        di_ref[...] = flat(dv.astype(BF16))
        dqf = dqe * e_b + dqd * e_qd
        dkk = dkd * e_kd + dke * e_ke
        t_qd, t_kd, t_ke = dqd * qd, dkd * kd, dke * ke
        db = dqe * qe + t_qd - t_kd - t_ke
        dbm = jnp.sum(t_kd - t_qd, axis=1, keepdims=True)
        dbl = jnp.sum(t_ke, axis=1, keepdims=True) + ddec * dec
        db = db + jnp.where(rowi == C // 2 - 1, dbm, 0.0) + jnp.where(rowi == C - 1, dbl, 0.0)
        dlf = _tri_mm(tri, db, BTN)
        dfv = dlf / f - dkk
        df_ref[...] = flat((dfv * (1.0 - lb) * sig * (1.0 - sig)).astype(BF16))
        plb_ref[...] = _fold8(flat(dfv * (1.0 - sig)))
        dq_ref[...] = flat((dqf * (sq * (1.0 + q * (1.0 - sq)))).astype(BF16))

    def col(off):
        return pl.BlockSpec((tt, A_HD), lambda h, t: (nT - 1 - t, off // A_HD + h))

    head_vec = lambda rows: pl.BlockSpec((rows, A_HD), lambda h, t: (0, h))
    o_spec = pl.BlockSpec((tt, A_HD), lambda h, t: (nT - 1 - t, h))
    p_spec = pl.BlockSpec((8, A_HD), lambda h, t: (t, h))
    o_shape = jax.ShapeDtypeStruct((T, AW), BF16)
    p_shape = jax.ShapeDtypeStruct((nT * 8, AW), F32)
    return _pcall(
        body, grid=(A_HEADS, nT),
        in_specs=[col(OFF_QA), col(OFF_FA), col(OFF_IA), col(OFF_GA),
                  pl.BlockSpec((1, ncl, A_HD, A_HD), lambda h, t: (h, nT - 1 - t, 0, 0)),
                  pl.BlockSpec((tt, A_HD), lambda h, t: (nT - 1 - t, h)), head_vec(2), head_vec(1)],
        out_specs=[o_spec, o_spec, o_spec, o_spec, p_spec, p_spec],
        out_shape=[o_shape, o_shape, o_shape, o_shape, p_shape, p_shape],
        scratch_shapes=[pltpu.VMEM((A_HD, A_HD), F32)], name="hgrn_bwd", semantics=("parallel", "arbitrary"),
        args=[proj, proj, proj, proj, st, dy, lb_logits, o_gain], job=job)


LANES = 128
Q_COLS = BW // LANES


def _low_half():
    return lax.broadcasted_iota(jnp.int32, (1, LANES), 1) < B_HD


def _half_sum(t, low):
    lo = jnp.sum(jnp.where(low, t, 0.0), axis=-1, keepdims=True)
    hi = jnp.sum(jnp.where(low, 0.0, t), axis=-1, keepdims=True)
    return jnp.where(low, lo, hi)


def _half_rms(t, low):
    r = lax.rsqrt(_half_sum(t * t, low) * (1.0 / B_HD) + EPS)
    return t * r, r


def _fold_halves(p, low):
    return jnp.where(low, p + pltpu.roll(p, B_HD, 1), 0.0)


def _stack_cols(x):
    return jnp.stack([x[:, c * LANES:(c + 1) * LANES] for c in range(Q_COLS)], axis=0).reshape(KV_HEADS, 2 * BLK, LANES)


def _col_of(t, c):
    return t[c // 2, (c % 2) * BLK:(c % 2 + 1) * BLK]


def _split_halves(col, s, low):
    own = jnp.where(low if s == 0 else jnp.logical_not(low), col, 0.0)
    other = pltpu.roll(own, B_HD, 1)
    return (own, other) if s == 0 else (other, own)


def _swa_keys(kp_ref, kc_ref, vp_ref, vc_ref, kg, low):
    k_lo, k_hi, v_lo, v_hi, hats = [], [], [], [], []
    for j in range(KVW // LANES):
        cs = slice(j * LANES, (j + 1) * LANES)
        k_hat, k_r = _half_rms(jnp.concatenate([kp_ref[:, cs], kc_ref[:, cs]], axis=0), low)
        vcol = jnp.concatenate([vp_ref[:, cs], vc_ref[:, cs]], axis=0)
        hats.append((k_hat, k_r))
        for s in range(2):
            for dst_lo, dst_hi, col in ((k_lo, k_hi, k_hat * kg), (v_lo, v_hi, vcol)):
                lo, hi = _split_halves(col, s, low)
                dst_lo.append(lo)
                dst_hi.append(hi)
    st = lambda parts: jnp.stack(parts, axis=0)
    return st(k_lo), st(k_hi), st(v_lo), st(v_hi), hats


def _swa_mask(first_block):
    qi = lax.broadcasted_iota(jnp.int32, (BLK, 2 * BLK), 0) + BLK
    ki = lax.broadcasted_iota(jnp.int32, (BLK, 2 * BLK), 1)
    rel = qi - ki
    m = (rel >= 0) & (rel < BLK) & (jnp.logical_not(first_block) | (ki >= BLK))
    return jnp.concatenate([m, m], axis=0)


def _sink_cols(sk_ref, hi):
    top = lax.broadcasted_iota(jnp.int32, (2 * BLK, 1), 0) < BLK
    return jnp.stack([jnp.where(top, sk_ref[0, GROUP * hk + hi], sk_ref[0, GROUP * hk + 2 + hi])
                      for hk in range(KV_HEADS)], axis=0)


def _swa_probs(qn, k_half, sink, mask):
    s = jnp.where(mask, _dot(qn, k_half, BNT) * (B_HD ** -0.5), NEG)
    m = jnp.maximum(jnp.max(s, axis=-1, keepdims=True), sink)
    p = jnp.exp(s - m)
    ps = jnp.exp(sink - m)
    inv = 1.0 / (jnp.sum(p, axis=-1, keepdims=True) + ps)
    return p * inv, ps * inv


def _swa_fwd(proj, q_gain, k_gain, sinks, job=None):
    T = proj.shape[0]
    nb = T // BLK

    def body(q_ref, kc_ref, kp_ref, vc_ref, vp_ref, qg_ref, kg_ref, sk_ref, o_ref):
        low = _low_half()
        mask = _swa_mask(pl.program_id(0) == 0)
        qn = _half_rms(_stack_cols(q_ref[...]), low)[0] * qg_ref[...]
        k_lo, k_hi, v_lo, v_hi, _ = _swa_keys(kp_ref, kc_ref, vp_ref, vc_ref, kg_ref[...], low)
        p_lo, _ = _swa_probs(qn, k_lo, _sink_cols(sk_ref, 0), mask)
        p_hi, _ = _swa_probs(qn, k_hi, _sink_cols(sk_ref, 1), mask)
        o = (_dot(p_lo, v_lo, BNN) + _dot(p_hi, v_hi, BNN)).astype(BF16)
        for c in range(Q_COLS):
            o_ref[:, c * LANES:(c + 1) * LANES] = _col_of(o, c)

    q_gain, k_gain = jnp.tile(q_gain, (1, 2)), jnp.tile(k_gain, (1, 2))
    cur = lambda w, off: pl.BlockSpec((BLK, w), lambda i: (i, off // w))
    prev = lambda w, off: pl.BlockSpec((BLK, w), lambda i: (jnp.maximum(i - 1, 0), off // w))
    small = lambda n: pl.BlockSpec((1, 2 * n), lambda i: (0, 0))
    return _pcall(
        body, grid=(nb,),
        in_specs=[cur(BW, OFF_QB), cur(KVW, OFF_KB), prev(KVW, OFF_KB), cur(KVW, OFF_VB), prev(KVW, OFF_VB),
                  small(B_HD), small(B_HD), pl.BlockSpec(memory_space=pltpu.SMEM)],
        out_specs=[pl.BlockSpec((BLK, BW), lambda i: (i, 0))],
        out_shape=[jax.ShapeDtypeStruct((T, BW), BF16)], scratch_shapes=[], name="swa_fwd", semantics=("parallel",),
        args=[proj, proj, proj, proj, proj, q_gain, k_gain, sinks], job=job)


def _swa_bwd(proj, dout, q_gain, k_gain, sinks, job=None):
    T = proj.shape[0]
    nb = T // BLK
    W = BW + 2 * KVW

    def body(q_ref, kc_ref, kp_ref, vc_ref, vp_ref, do_ref, qg_ref, kg_ref, sk_ref,
             dq_ref, dkv_ref, pqg_ref, pkg_ref, psk_ref, dkn_c, dv_c):
        i = pl.program_id(0)
        live = i < nb
        low = _low_half()
        high = jnp.logical_not(low)
        qg, kg = qg_ref[...], kg_ref[...]
        mask = _swa_mask(i == 0)
        lane = lax.broadcasted_iota(jnp.int32, (1, LANES), 1)
        scale = B_HD ** -0.5

        @pl.when(i == 0)
        def _():
            dkn_c[...] = jnp.zeros_like(dkn_c)
            dv_c[...] = jnp.zeros_like(dv_c)

        q_hat, q_r = _half_rms(_stack_cols(q_ref[...]), low)
        qn = q_hat * qg
        k_lo, k_hi, v_lo, v_hi, hats = _swa_keys(kp_ref, kc_ref, vp_ref, vc_ref, kg, low)
        do = _stack_cols(do_ref[...])
        dqn = jnp.zeros((KV_HEADS, 2 * BLK, LANES), F32)
        acc_sk = jnp.zeros((1, LANES), F32)
        dk_parts, dv_parts = [], []
        for hi, (k_h, v_h) in enumerate(((k_lo, v_lo), (k_hi, v_hi))):
            p, ps = _swa_probs(qn, k_h, _sink_cols(sk_ref, hi), mask)
            dp = _dot(do, v_h, BNT)
            delta = jnp.sum(p * dp, axis=-1, keepdims=True)
            ds = p * (dp - delta) * scale
            dqn = dqn + _dot(ds, k_h, BNN)
            dk_parts.append(_dot(ds, qn, BTN))
            dv_parts.append(_dot(p, do, BTN))
            t = ps * delta
            for hk in range(KV_HEADS):
                for rows in range(2):
                    h = GROUP * hk + 2 * rows + hi
                    acc_sk = acc_sk + jnp.where(
                        lane == h, -jnp.sum(t[hk, rows * BLK:(rows + 1) * BLK], axis=0, keepdims=True), 0.0)
        dqh = dqn * qg
        dq = (q_r * (dqh - q_hat * (_half_sum(dqh * q_hat, low) * (1.0 / B_HD)))).astype(BF16)
        for c in range(Q_COLS):
            dq_ref[:, c * LANES:(c + 1) * LANES] = _col_of(dq, c)
        acc_qg = _fold_halves(_fold8((dqn * q_hat).reshape(KV_HEADS * 2 * BLK, LANES)), low)

        def native(parts, j):
            lo_arr, hi_arr = parts
            a, b = 2 * j, 2 * j + 1
            return (jnp.where(low, lo_arr[a], 0.0) + pltpu.roll(jnp.where(high, hi_arr[a], 0.0), B_HD, 1)
                    + jnp.where(high, hi_arr[b], 0.0) + pltpu.roll(jnp.where(low, lo_arr[b], 0.0), B_HD, 1))

        acc_kg = jnp.zeros((8, LANES), F32)
        for j in range(KVW // LANES):
            cs = slice(j * LANES, (j + 1) * LANES)
            dkn = jnp.where(live, native(dk_parts, j), 0.0)
            dvc = jnp.where(live, native(dv_parts, j), 0.0)
            kp_hat, kp_r = hats[j][0][:BLK], hats[j][1][:BLK]
            dkn_prev = dkn_c[:, cs] + dkn[:BLK]
            dv_prev = dv_c[:, cs] + dvc[:BLK]
            acc_kg = acc_kg + _fold8(dkn_prev * kp_hat)
            dkh = dkn_prev * kg
            dkv_ref[:, cs] = (kp_r * (dkh - kp_hat * (_half_sum(dkh * kp_hat, low) * (1.0 / B_HD)))).astype(BF16)
            dkv_ref[:, KVW + j * LANES:KVW + (j + 1) * LANES] = dv_prev.astype(BF16)
            dkn_c[:, cs] = dkn[BLK:]
            dv_c[:, cs] = dvc[BLK:]
        keep = jnp.where(i > 0, 1.0, 0.0)
        pqg_ref[...] = jnp.where(live, acc_qg, 0.0)
        pkg_ref[...] = _fold_halves(acc_kg, low) * keep
        psk_ref[...] = jnp.broadcast_to(jnp.where(live, acc_sk, 0.0), (8, LANES)) * (
            lax.broadcasted_iota(jnp.int32, (8, LANES), 0) == 0).astype(F32)

    q_gain, k_gain = jnp.tile(q_gain, (1, 2)), jnp.tile(k_gain, (1, 2))
    last = nb - 1
    cur = lambda w, off: pl.BlockSpec((BLK, w), lambda i: (jnp.minimum(i, last), off // w))
    prev = lambda w, off: pl.BlockSpec((BLK, w), lambda i: (jnp.maximum(i - 1, 0), off // w))
    small = lambda n: pl.BlockSpec((1, 2 * n), lambda i: (0, 0))
    part = pl.BlockSpec((8, 128), lambda i: (i, 0))
    p_shape = jax.ShapeDtypeStruct(((nb + 1) * 8, 128), F32)
    return _pcall(
        body, grid=(nb + 1,),
        in_specs=[cur(BW, OFF_QB), cur(KVW, OFF_KB), prev(KVW, OFF_KB), cur(KVW, OFF_VB), prev(KVW, OFF_VB),
                  pl.BlockSpec((BLK, BW), lambda i: (jnp.minimum(i, last), 0)), small(B_HD), small(B_HD),
                  pl.BlockSpec(memory_space=pltpu.SMEM)],
        out_specs=[pl.BlockSpec((BLK, BW), lambda i: (i, 0)),
                   pl.BlockSpec((BLK, 2 * KVW), lambda i: (jnp.maximum(i - 1, 0), 0)), part, part, part],
        out_shape=[jax.ShapeDtypeStruct((T + BLK, BW), BF16), jax.ShapeDtypeStruct((T, 2 * KVW), BF16),
                   p_shape, p_shape, p_shape],
        scratch_shapes=[pltpu.VMEM((BLK, KVW), F32), pltpu.VMEM((BLK, KVW), F32)], name="swa_bwd",
        semantics=("arbitrary",), args=[proj, proj, proj, proj, proj, dout, q_gain, k_gain, sinks], job=job)


def _branch_merge(ya_pre, attn, wa_t, wb_t, proj, tm, tn, job=None):
    T = ya_pre.shape[0]

    def body(a_ref, b_ref, wa_ref, wb_ref, ga_ref, gb_ref, ya_ref, yb_ref, mg_ref):
        ya = lax.dot_general(a_ref[...], wa_ref[...], NT, preferred_element_type=F32)
        yb = lax.dot_general(b_ref[...], wb_ref[...], NT, preferred_element_type=F32)
        ya_ref[...] = ya.astype(BF16)
        yb_ref[...] = yb.astype(BF16)
        mg_ref[...] = (_sigmoid(ga_ref[...]) * ya + _sigmoid(gb_ref[...]) * yb).astype(BF16)

    o_spec = pl.BlockSpec((tm, tn), lambda i, j: (i, j))
    o_shape = jax.ShapeDtypeStruct((T, D), BF16)
    return _pcall(
        body, grid=(T // tm, D // tn),
        in_specs=[pl.BlockSpec((tm, AW), lambda i, j: (i, 0)), pl.BlockSpec((tm, BW), lambda i, j: (i, 0)),
                  pl.BlockSpec((tn, AW), lambda i, j: (j, 0)), pl.BlockSpec((tn, BW), lambda i, j: (j, 0)),
                  pl.BlockSpec((tm, tn), lambda i, j: (i, OFF_GTA // tn + j)),
                  pl.BlockSpec((tm, tn), lambda i, j: (i, OFF_GTB // tn + j))],
        out_specs=[o_spec, o_spec, o_spec], out_shape=[o_shape, o_shape, o_shape], scratch_shapes=[], name="branch_merge",
        semantics=("parallel", "parallel"), args=[ya_pre, attn, wa_t, wb_t, proj, proj], job=job)


def _ij(i, j, k):
    return (i, j)


def _local_step(x, tgt, mod, g1, g2, lbl, og, qg, kg, sk, shards, c_arr, update):
    win_s, wa_s, wb_s, wout_s, wmi_s, wmo_s = shards
    T = x.shape[0]
    tm, tr, tt = min(1024, T), min(256, T), min(2048, T)
    tk_t = min(1024, T)
    tn = 512
    sh1, sc1, gt1, sh2, sc2, gt2 = (mod[:, i * D:(i + 1) * D] for i in range(N_MOD))
    nI = T // tm
    blk = (tm, tn)

    (h,), (win_t,) = _rms_mod_fwd("rms1_fwd", x, g1, sc1, sh1, tr, job=_gather_relay_job([win_s], alone=True))

    def epi_store(acc, ex, ou):
        ou[0][...] = acc.astype(ou[0].dtype)

    tm2 = min(2048, T)
    blk2 = (tm2, tn)

    full = lambda s: (0, s.shape[0])
    last = wmi_s.shape[0]
    gather = _gather_relay_job
    (proj,), (wa_t, wb_t, w_out, wmi_part) = _mm(
        "in_proj", "nt", [(h, D)], win_t, T, IN_W, D, tm2, tn, D, [], [((T, IN_W), F32, blk2, _ij)], epi_store,
        job=gather([wa_s, wb_s, wout_s, wmi_s], rows=[full(wa_s), full(wb_s), full(wout_s), (0, MI_CUT)]))
    (ya_pre, st), _ = _hgrn_fwd(proj, lbl, og, tt)
    (attn,), (wmi_t,) = _swa_fwd(proj, qg, kg, sk, job=gather([wmi_s], rows=[(MI_CUT, last)], into=[wmi_part]))
    (ya, yb, merged), _ = _branch_merge(ya_pre, attn, wa_t, wb_t, proj, tm, tn)

    def residual_rows(acc, vecs, bufs, parts):
        gt, gain, sc, sh = (v[...] for v in vecs)
        x_buf, mo_buf, h2_buf = bufs
        rows = min(64, tm)
        for r0 in range(0, tm, rows):
            rs = slice(r0, r0 + rows)
            z = acc[rs, :]
            mo_buf[rs, :] = z.astype(BF16)
            x1v = x_buf[rs, :] + gt * z
            x_buf[rs, :] = x1v
            rstd = lax.rsqrt(jnp.mean(x1v * x1v, axis=-1, keepdims=True) + EPS)
            h2_buf[rs, :] = ((x1v * rstd * gain) * (1.0 + sc) + sh).astype(BF16)

    x1, mo, h2 = _rows_mm("out_proj", [(merged, D)], w_out, T, tm, min(1024, D), [gt1, g2, sc2, sh2],
                          [(F32, x, True), (BF16, None, True), (BF16, None, True)], [], residual_rows)

    def epi_relu2(acc, ex, ou):
        r = jnp.maximum(acc, 0.0)
        ou[0][...] = r.astype(BF16)
        ou[1][...] = (r * r).astype(BF16)

    (r, a), (w_mo,) = _mm("mlp_in", "nt", [(h2, D)], wmi_t, T, HID, D, tm2, tn, D, [],
                          [((T, HID), BF16, blk2, _ij), ((T, HID), BF16, blk2, _ij)], epi_relu2,
                          job=gather([wmo_s]))

    def loss_rows(acc, vecs, bufs, parts):
        gt = vecs[0][...]
        x1_buf, t_buf, dz_buf = bufs
        rows = min(64, tm)
        loss_sum, gate_sum = jnp.zeros((8, D), F32), jnp.zeros((8, D), F32)
        for r0 in range(0, tm, rows):
            rs = slice(r0, r0 + rows)
            z = acc[rs, :]
            e = x1_buf[rs, :] + gt * z - t_buf[rs, :]
            dy = e * (1.0 / D)
            t_buf[rs, :] = dy
            dz_buf[rs, :] = (gt * dy).astype(BF16)
            loss_sum = loss_sum + _fold8(e * e)
            gate_sum = gate_sum + _fold8(dy * z)
        parts[0][...] = loss_sum * (0.5 / D)
        parts[1][...] = gate_sum

    part_rows = ((nI * 8, D), F32, (8, D), lambda i, j, k: (i, 0))
    dy, dz, p_loss, p_gt2 = _rows_mm(
        "mlp_out", [(a, HID)], w_mo, T, tm, 1024, [gt2], [(F32, x1, False), (F32, tgt, True), (BF16, None, True)],
        [part_rows, part_rows], loss_rows)

    def epi_du(acc, ex, ou):
        ou[0][...] = (acc * (2.0 * ex[0][...].astype(F32))).astype(BF16)

    (du,) = _mm("mlp_out_dx", "nt", [(dz, D)], w_mo, T, HID, D, tm2, tn, D, [(r, blk2, _ij)],
                [((T, HID), BF16, blk2, _ij)], epi_du)
    gblk = (1024, 1024)
    gwide = (1024, D)
    pair_sum = lambda nm, g, r1: _pair_sum("pair_sum_" + nm, g, r1, c_arr, _sum_rows(r1.shape[1]))
    (g_mo,) = _mm("mlp_out_dw", "tn", [(a, HID)], dz, HID, D, T, 1024, D, tk_t, [], [((HID, D), BF16, gwide, _ij)], epi_store)
    (dh2,), (r1_mo,) = _mm("mlp_in_dx", "nn", [(du, HID)], wmi_t, T, D, HID, tm, D, 1024, [],
                           [((T, D), F32, (tm, D), _ij)], epi_store, job=_pair_job([g_mo]))
    dx1, p_sh2, p_sc2, p_g2, dmo, p_gt1 = _rms_mod_bwd("rms2_bwd", dh2, x1, g2, sc2, dy, tr, gate=gt1, mo=mo)
    s_mo = pair_sum("mlp_out", g_mo, r1_mo)
    near, far = (1, 2), (3,)
    (g_mi,), (rn_mo,) = _mm("mlp_in_dw", "tn", [(du, HID)], h2, HID, D, T, 1024, D, tk_t, [],
                            [((HID, D), BF16, gwide, _ij)], epi_store, job=_chip_job([s_mo], near))

    def epi_gates(acc, ex, ou):
        ya_ref, yb_ref, ga_ref, gb_ref = ex
        sa, sb = _sigmoid(ga_ref[...]), _sigmoid(gb_ref[...])
        ou[0][...] = (acc * sa).astype(BF16)
        ou[1][...] = (acc * sb).astype(BF16)
        ou[2][...] = (acc * ya_ref[...].astype(F32) * (sa * (1.0 - sa))).astype(BF16)
        ou[3][...] = (acc * yb_ref[...].astype(F32) * (sb * (1.0 - sb))).astype(BF16)

    o_bf = ((T, D), BF16, blk, _ij)
    (dya, dyb, dga, dgb), (rf_mo, r1_mi) = _mm(
        "out_proj_dx", "nt", [(dmo, D)], w_out, T, D, D, tm, tn, D,
        [(ya, blk, _ij), (yb, blk, _ij), (proj, blk, lambda i, j, k: (i, OFF_GTA // tn + j)),
         (proj, blk, lambda i, j, k: (i, OFF_GTB // tn + j))], [o_bf, o_bf, o_bf, o_bf], epi_gates,
        job=_both(_chip_job([s_mo], far), _pair_job([g_mi])))
    s_mi = pair_sum("mlp_in", g_mi, r1_mi)
    (g_out,) = _mm("out_proj_dw", "tn", [(merged, D)], dmo, D, D, T, 1024, 1024, tk_t, [], [((D, D), BF16, gblk, _ij)], epi_store)
    dya_pre, dattn = _twin_mm("branch_dx", "nn", [(dya, wa_t), (dyb, wb_t)], T, AW, D, tm, tn, D, F32)
    g_a, g_b = _twin_mm("branch_dw", "tn", [(dya, ya_pre), (dyb, attn)], D, AW, T, 1024, 1024, tk_t, BF16)
    (dqa, dfa, dia, dgg, p_lb, p_og), (rn_mi, r1_out, r1_a, r1_b) = _hgrn_bwd(
        proj, st, dya_pre, lbl, og, tt, job=_both(_chip_job([s_mi], near), _pair_job([g_out, g_a, g_b])))
    (dqb, dkv, p_qg, p_kg, p_sk), (rf_mi,) = _swa_bwd(proj, dattn, qg, kg, sk, job=_chip_job([s_mi], far))
    s_out, s_a, s_b = pair_sum("out", g_out, r1_out), pair_sum("branch_a", g_a, r1_a), pair_sum("branch_b", g_b, r1_b)
    pieces = [(dqa, AW), (dfa, AW), (dia, AW), (dgg, AW), (dqb, BW), (dkv, 2 * KVW), (dga, D), (dgb, D)]
    (g_in,), (r2_out, r2_a, r2_b) = _pieces_tn("in_proj_dw", pieces, h, 512, job=_chip_job([s_out, s_a, s_b]))
    (r1_in,) = update("w_mlp_in", s_mi, [rn_mi, rf_mi], job=_pair_job([g_in]))
    s_in = pair_sum("in", g_in, r1_in)
    (dx, p_sh1, p_sc1, p_g1), (r2_in,) = _pieces_nn_rms(
        "in_proj_dx", pieces, win_t, x, g1, sc1, dx1, tm, 512, job=_chip_job([s_in]))

    partials = dict(sh1=p_sh1, sc1=p_sc1, gt1=p_gt1, sh2=p_sh2, sc2=p_sc2, gt2=p_gt2, g1=p_g1, g2=p_g2,
                    lb=p_lb, og=p_og, qg=p_qg, kg=p_kg, sk=p_sk, loss=p_loss)
    sums = dict(w_in=(s_in, [r2_in]), w_branch_a=(s_a, [r2_a]), w_branch_b=(s_b, [r2_b]), w_out=(s_out, [r2_out]),
                w_mlp_in=(s_mi, [rn_mi, rf_mi]), w_mlp_out=(s_mo, [rn_mo, rf_mo]))
    return dx, sums, partials


def _exchange_slots(buf, send_sems, recv_sems):
    me = _mesh_pos()
    mine = buf.at[_index(me)]
    sends = []
    for k in range(1, N_DEV):
        cp = pltpu.make_async_remote_copy(src_ref=mine, dst_ref=mine, send_sem=send_sems.at[k - 1],
                                          recv_sem=recv_sems.at[k - 1], device_id=_flip(me, k), device_id_type=MESH)
        cp.start()
        sends.append(cp)
    for k in range(1, N_DEV):
        theirs = buf.at[_index(_flip(me, k))]
        pltpu.make_async_remote_copy(src_ref=theirs, dst_ref=theirs, send_sem=send_sems.at[k - 1],
                                     recv_sem=recv_sems.at[k - 1], device_id=_flip(me, k), device_id_type=MESH).wait_recv()
    for cp in sends:
        cp.wait_send()


ADA_W = N_MOD * D // N_DEV


def _ada_mod(c, w_ada, b_shard):
    def body(c_ref, w_ref, b_ref, mod_ref, sc_ref, cbuf, mbuf, s1, r1, s2, r2):
        me = _index(_mesh_pos())
        cbuf[me] = c_ref[...]
        _exchange_slots(cbuf, s1, r1)
        row = lax.broadcasted_iota(jnp.int32, (N_DEV, D), 0)
        call = jnp.zeros((N_DEV, D), F32)
        for d in range(N_DEV):
            call = jnp.where(row == d, cbuf[d], call)
        sc = call * _sigmoid(call)
        sc_ref[...] = sc
        mbuf[me] = _dot(sc, w_ref[...]) + b_ref[...]
        _exchange_slots(mbuf, s2, r2)
        for s in range(N_DEV):
            mod_ref[:, s * ADA_W:(s + 1) * ADA_W] = mbuf[s, pl.ds(me, 1), :]

    return pl.pallas_call(
        body, in_specs=[_VMEM, _VMEM, _VMEM], out_specs=[_VMEM, _VMEM],
        out_shape=[jax.ShapeDtypeStruct((1, N_MOD * D), F32), jax.ShapeDtypeStruct((N_DEV, D), F32)],
        scratch_shapes=[pltpu.VMEM((N_DEV, 1, D), F32), pltpu.VMEM((N_DEV, N_DEV, ADA_W), F32),
                        _SEMS(N_DEV - 1), _SEMS(N_DEV - 1), _SEMS(N_DEV - 1), _SEMS(N_DEV - 1)],
        name="ada_mod", compiler_params=pltpu.CompilerParams(vmem_limit_bytes=VMEM_LIMIT),
    )(c, w_ada, b_shard)


SMALL_SEGS = (("b_ada", N_MOD * D), ("norm1_gain", D), ("norm2_gain", D), ("lb0", AW), ("lb1", AW),
              ("hgrn_o_gain", AW), ("q_norm_gain", 128), ("k_norm_gain", 128), ("sinks", 128))
SMALL_W = sum(w for _, w in SMALL_SEGS)
X_SEGS = (("sh1", D), ("sc1", D), ("gt1", D), ("sh2", D), ("sc2", D), ("gt2", D), ("g1", D), ("g2", D),
          ("lb", AW), ("og", AW), ("qg", 128), ("kg", 128), ("sk", 128), ("loss", 128))
X_W = sum(w for _, w in X_SEGS)


def _offsets(segs):
    out, o = {}, 0
    for name, w in segs:
        out[name] = (o, w)
        o += w
    return out


def _small_reduce(parts, lb_logits):
    xo, so = _offsets(X_SEGS), _offsets(SMALL_SEGS)
    names = [nm for nm, _ in X_SEGS]

    def body(*refs):
        p_refs = dict(zip(names, refs[:len(names)]))
        lbl_ref, allx, gs_ref, loss_ref, send_sems, recv_sems = refs[len(names):]
        me = _index(_mesh_pos())
        for nm, (o, w) in xo.items():
            if nm == "loss":
                allx[me, :, o:o + w] = jnp.broadcast_to(jnp.sum(p_refs[nm][...]), (1, w))
            else:
                allx[me, :, o:o + w] = jnp.sum(p_refs[nm][...], axis=0, keepdims=True)
        _exchange_slots(allx, send_sems, recv_sems)
        tot = allx[0]
        for d in range(1, N_DEV):
            tot = tot + allx[d]
        seg = lambda nm: tot[:, xo[nm][0]:xo[nm][0] + xo[nm][1]]

        def put(nm, v):
            gs_ref[:, so[nm][0]:so[nm][0] + so[nm][1]] = v

        put("b_ada", tot[:, 0:N_MOD * D])
        put("norm1_gain", seg("g1"))
        put("norm2_gain", seg("g2"))
        lbl = lbl_ref[...]
        lb = _sigmoid(lbl[0:1, :] - lbl[1:2, :])
        dl0 = seg("lb") * lb * (1.0 - lb)
        put("lb0", dl0)
        put("lb1", -dl0)
        put("hgrn_o_gain", seg("og"))
        put("q_norm_gain", seg("qg"))
        put("k_norm_gain", seg("kg"))
        put("sinks", seg("sk"))
        loss_ref[...] = seg("loss")

    return pl.pallas_call(
        body, in_specs=[_VMEM] * (len(names) + 1), out_specs=[_VMEM, _VMEM, _VMEM],
        out_shape=[jax.ShapeDtypeStruct((N_DEV, 1, X_W), F32), jax.ShapeDtypeStruct((1, SMALL_W), F32),
                   jax.ShapeDtypeStruct((1, 128), F32)],
        scratch_shapes=[_SEMS(N_DEV - 1), _SEMS(N_DEV - 1)], name="small_reduce",
        compiler_params=pltpu.CompilerParams(vmem_limit_bytes=VMEM_LIMIT),
    )(*[parts[nm] for nm in names], lb_logits)


def _adamw_math(w, g, m, v):
    m = B1 * m + (1.0 - B1) * g
    v = B2 * v + (1.0 - B2) * (g * g)
    m_hat = m / (1.0 - B1 ** STEP)
    v_hat = v / (1.0 - B2 ** STEP)
    return -LR * (m_hat / (jnp.sqrt(v_hat) + ADAM_EPS) + WD * w), m, v


def _sum_rows(rs):
    return 256 if rs % 256 == 0 else rs // 2


def _pair_sum(name, g, recv, c_arr, tr):
    _, rs, cols = recv.shape
    blk = (1, tr, cols)

    def body(c_ref, g_ref, r_ref, o_ref):
        o_ref[...] = (g_ref[...].astype(F32) + r_ref[...].astype(F32)).astype(BF16)

    grid_spec = pltpu.PrefetchScalarGridSpec(
        num_scalar_prefetch=1, grid=(4, rs // tr),
        in_specs=[pl.BlockSpec(blk, lambda q, i, c: (2 * q + c[0], i, 0)), pl.BlockSpec(blk, lambda q, i, c: (q, i, 0))],
        out_specs=pl.BlockSpec(blk, lambda q, i, c: (q, i, 0)))
    return pl.pallas_call(body, grid_spec=grid_spec, out_shape=jax.ShapeDtypeStruct((4, rs, cols), BF16), name=name,
                          compiler_params=_params(("parallel", "parallel")))(c_arr, g.reshape(N_DEV, rs, cols), recv)


def _sum_adamw(name, sums, recvs, q_arr, w, m, v, transposed, tile, job=None):
    rows, cols = w.shape
    nR = len(recvs)

    def body(q_ref, s_ref, *refs):
        r_refs = refs[:nR]
        w_ref, m_ref, v_ref, g_ref, d_ref, nm_ref, nv_ref = refs[nR:]
        g = s_ref[0].astype(F32)
        for r_ref in r_refs:
            for slot in range(r_ref.shape[0]):
                g = g + r_ref[slot].astype(F32)
        g = g.T if transposed else g
        g_ref[...] = g
        d_ref[...], nm_ref[...], nv_ref[...] = _adamw_math(w_ref[...], g, m_ref[...], v_ref[...])

    if transposed:
        slab = lambda n, first: pl.BlockSpec((n, cols, tile), lambda i, q: (first(q), 0, i))
    else:
        slab = lambda n, first: pl.BlockSpec((n, tile, cols), lambda i, q: (first(q), i, 0))
    spec = pl.BlockSpec((tile, cols), lambda i, q: (i, 0))
    shape = jax.ShapeDtypeStruct((rows, cols), F32)
    res, job_res = _pcall(
        body, grid=(rows // tile,),
        in_specs=[slab(1, lambda q: q[0])] + [slab(r.shape[0], lambda q: 0) for r in recvs] + [spec] * 3,
        out_specs=[spec] * 4, out_shape=[shape] * 4, scratch_shapes=[], name=name, semantics=("parallel",),
        args=[sums, *recvs, w, m, v], job=job, prefetch=[q_arr])
    return res if job is None else (res, job_res)


def _adamw(name, w, g, m, v, tr):
    rows, cols = w.shape

    def body(w_ref, g_ref, m_ref, v_ref, d_ref, nm_ref, nv_ref):
        d_ref[...], nm_ref[...], nv_ref[...] = _adamw_math(w_ref[...], g_ref[...], m_ref[...], v_ref[...])

    spec = pl.BlockSpec((tr, cols), lambda i: (i, 0))
    shape = jax.ShapeDtypeStruct((rows, cols), F32)
    return pl.pallas_call(
        body, grid=(rows // tr,), in_specs=[spec] * 4, out_specs=[spec] * 3, out_shape=[shape] * 3, name=name,
        compiler_params=_params(("parallel",)),
    )(w, g, m, v)


def _ada_update(sc_t, dmod_cols, w, m, v, tr):
    rows, cols = w.shape

    def body(s_ref, d_ref, w_ref, m_ref, v_ref, g_ref, dl_ref, nm_ref, nv_ref):
        g = jnp.dot(s_ref[...], d_ref[...], precision=lax.Precision.HIGHEST, preferred_element_type=F32)
        g_ref[...] = g
        dl_ref[...], nm_ref[...], nv_ref[...] = _adamw_math(w_ref[...], g, m_ref[...], v_ref[...])

    spec = pl.BlockSpec((tr, cols), lambda i: (i, 0))
    shape = jax.ShapeDtypeStruct((rows, cols), F32)
    return pl.pallas_call(
        body, grid=(rows // tr,),
        in_specs=[pl.BlockSpec((tr, N_DEV), lambda i: (i, 0)), pl.BlockSpec((N_DEV, cols), lambda i: (0, 0)), spec, spec, spec],
        out_specs=[spec] * 4, out_shape=[shape] * 4, name="ada_update", compiler_params=_params(("parallel",)),
    )(sc_t, dmod_cols, w, m, v)


BIG = ("w_in", "w_branch_a", "w_branch_b", "w_out", "w_mlp_in", "w_mlp_out")
COLUMN_SHARDED = ("w_in", "w_branch_a", "w_branch_b", "w_mlp_in")
AS_TRANSPOSE = ("w_in",)
WEIGHTS = ("w_ada", "b_ada", "norm1_gain", "w_in", "lb_logits", "hgrn_o_gain", "q_norm_gain", "k_norm_gain", "sinks",
           "w_branch_a", "w_branch_b", "w_out", "norm2_gain", "w_mlp_in", "w_mlp_out")


def _to_bf16(name, w, transposed, tile=256):
    rows, cols = w.shape

    def body(w_ref, o_ref):
        v = w_ref[...]
        o_ref[...] = (v.T if transposed else v).astype(BF16)

    out_spec = pl.BlockSpec((cols, tile), lambda i: (0, i)) if transposed else pl.BlockSpec((tile, cols), lambda i: (i, 0))
    return pl.pallas_call(
        body, grid=(rows // tile,), in_specs=[pl.BlockSpec((tile, cols), lambda i: (i, 0))], out_specs=out_spec,
        out_shape=jax.ShapeDtypeStruct((cols, rows) if transposed else (rows, cols), BF16), name=name,
        compiler_params=_params(("parallel",)))(w)


def _pack_small(p):
    lb = p["lb_logits"]
    src = dict(p, lb0=lb[0:1], lb1=lb[1:2])
    return jnp.concatenate([jnp.pad(src[nm], ((0, 0), (0, w - src[nm].shape[1]))) for nm, w in SMALL_SEGS], axis=1)


def _unpack_small(vec, shapes):
    so = _offsets(SMALL_SEGS)
    out = {}
    for nm, shp in shapes.items():
        if nm == "lb_logits":
            o = so["lb0"][0]
            out[nm] = vec[0, o:o + 2 * AW].reshape(2, AW)
        else:
            o = so[nm][0]
            out[nm] = vec[:, o:o + shp[1]]
    return out


def kernel(x, c, w_ada, b_ada, norm1_gain, w_in, lb_logits, hgrn_o_gain, q_norm_gain, k_norm_gain, sinks, w_branch_a, w_branch_b, w_out, norm2_gain, w_mlp_in, w_mlp_out, loss_target, m_w_ada, m_b_ada, m_norm1_gain, m_w_in, m_lb_logits, m_hgrn_o_gain, m_q_norm_gain, m_k_norm_gain, m_sinks, m_w_branch_a, m_w_branch_b, m_w_out, m_norm2_gain, m_w_mlp_in, m_w_mlp_out, v_w_ada, v_b_ada, v_norm1_gain, v_w_in, v_lb_logits, v_hgrn_o_gain, v_q_norm_gain, v_k_norm_gain, v_sinks, v_w_branch_a, v_w_branch_b, v_w_out, v_norm2_gain, v_w_mlp_in, v_w_mlp_out):
    w = dict(w_ada=w_ada, b_ada=b_ada, norm1_gain=norm1_gain, w_in=w_in, lb_logits=lb_logits, hgrn_o_gain=hgrn_o_gain,
             q_norm_gain=q_norm_gain, k_norm_gain=k_norm_gain, sinks=sinks, w_branch_a=w_branch_a, w_branch_b=w_branch_b,
             w_out=w_out, norm2_gain=norm2_gain, w_mlp_in=w_mlp_in, w_mlp_out=w_mlp_out)
    m = dict(w_ada=m_w_ada, b_ada=m_b_ada, norm1_gain=m_norm1_gain, w_in=m_w_in, lb_logits=m_lb_logits,
             hgrn_o_gain=m_hgrn_o_gain, q_norm_gain=m_q_norm_gain, k_norm_gain=m_k_norm_gain, sinks=m_sinks,
             w_branch_a=m_w_branch_a, w_branch_b=m_w_branch_b, w_out=m_w_out, norm2_gain=m_norm2_gain,
             w_mlp_in=m_w_mlp_in, w_mlp_out=m_w_mlp_out)
    v = dict(w_ada=v_w_ada, b_ada=v_b_ada, norm1_gain=v_norm1_gain, w_in=v_w_in, lb_logits=v_lb_logits,
             hgrn_o_gain=v_hgrn_o_gain, q_norm_gain=v_q_norm_gain, k_norm_gain=v_k_norm_gain, sinks=v_sinks,
             w_branch_a=v_w_branch_a, w_branch_b=v_w_branch_b, w_out=v_w_out, norm2_gain=v_norm2_gain,
             w_mlp_in=v_w_mlp_in, w_mlp_out=v_w_mlp_out)
    for d in (w, m, v):
        for nm in ("w_ada",) + BIG:
            d[nm] = d[nm][0]
    px, py, pc = _mesh_pos()
    me = _index((px, py, pc))
    c_arr = jnp.reshape(pc, (1,)).astype(jnp.int32)
    q_arr = jnp.reshape(2 * px + py, (1,)).astype(jnp.int32)

    shards = [_to_bf16("shard_" + nm, w[nm].T, False, w[nm].shape[1] // 4) if nm in AS_TRANSPOSE else
              _to_bf16("shard_" + nm, w[nm], nm in COLUMN_SHARDED) for nm in BIG]
    b_shard = lax.dynamic_slice(b_ada, (0, me * ADA_W), (1, ADA_W))
    mod, sc_all = _ada_mod(c, w["w_ada"], b_shard)

    grad, delta, new_m, new_v = {}, {}, {}, {}

    def update(nm, s, recvs, job=None):
        if nm in AS_TRANSPOSE:
            res = _sum_adamw("adamw_" + nm, s, recvs, q_arr, w[nm].T, m[nm].T, v[nm].T, False, w[nm].shape[1] // 4, job=job)
        else:
            res = _sum_adamw("adamw_" + nm, s, recvs, q_arr, w[nm], m[nm], v[nm], nm in COLUMN_SHARDED, 128, job=job)
        res, job_res = res if job is not None else (res, [])
        res = [t.T for t in res] if nm in AS_TRANSPOSE else res
        grad[nm], delta[nm], new_m[nm], new_v[nm] = res
        return job_res

    dx, sums, parts = _local_step(x[0], loss_target[0], mod, norm1_gain, norm2_gain, lb_logits, hgrn_o_gain,
                                  q_norm_gain, k_norm_gain, sinks, shards, c_arr, update)
    for nm in BIG:
        if nm not in grad:
            update(nm, *sums[nm])

    allx, g_small, loss = _small_reduce(parts, lb_logits)

    dmod_cols = lax.dynamic_slice(allx[:, 0, :], (0, me * ADA_W), (N_DEV, ADA_W))
    grad["w_ada"], delta["w_ada"], new_m["w_ada"], new_v["w_ada"] = _ada_update(
        sc_all.T, dmod_cols, w["w_ada"], m["w_ada"], v["w_ada"], 256)

    small_names = [nm for nm in WEIGHTS if nm not in BIG and nm != "w_ada"]
    shapes = {nm: w[nm].shape for nm in small_names}
    ds, ms, vs = _adamw("adamw_small", _pack_small(w), g_small, _pack_small(m), _pack_small(v), 1)
    for dst, vec in ((grad, g_small), (delta, ds), (new_m, ms), (new_v, vs)):
        dst.update(_unpack_small(vec, shapes))

    def full(d, nm):
        return d[nm][None] if nm in BIG or nm == "w_ada" else d[nm]

    return (loss[0, 0], dx[None], *[full(grad, nm) for nm in WEIGHTS], *[full(delta, nm) for nm in WEIGHTS],
            *[full(new_m, nm) for nm in WEIGHTS], *[full(new_v, nm) for nm in WEIGHTS])
```

```python
import functools

import jax
import jax.numpy as jnp
from jax import lax
from jax.experimental import pallas as pl
from jax.experimental.pallas import tpu as pltpu

F32 = jnp.float32
BF16 = jnp.bfloat16
MESH = pl.DeviceIdType.MESH

N_DEV = 8
D = 2048
A_HEADS, A_HD, CHUNK = 8, 128, 64
AW = A_HEADS * A_HD
Q_HEADS, KV_HEADS, GROUP, B_HD, BLK = 16, 4, 4, 64, 128
BW = Q_HEADS * B_HD
KVW = KV_HEADS * B_HD
HID = 4 * D
IN_W = 4 * AW + BW + 2 * KVW + 2 * D
OFF_QA, OFF_FA, OFF_IA, OFF_GA = 0, AW, 2 * AW, 3 * AW
OFF_QB = 4 * AW
OFF_KB = OFF_QB + BW
OFF_VB = OFF_KB + KVW
OFF_GTA = OFF_VB + KVW
OFF_GTB = OFF_GTA + D
N_MOD = 6
EPS = 1e-6
LR, B1, B2, ADAM_EPS, WD, STEP = 1e-3, 0.9, 0.999, 1e-8, 0.01, 10
NEG = -1e30

VMEM_LIMIT = 56 * 1024 * 1024
MI_CUT = 544

NN = (((1,), (0,)), ((), ()))
NT = (((1,), (1,)), ((), ()))
TN = (((0,), (0,)), ((), ()))
BNN = (((2,), (1,)), ((0,), (0,)))
BNT = (((2,), (2,)), ((0,), (0,)))
BTN = (((1,), (1,)), ((0,), (0,)))


def _dot(a, b, dims=NN):
    return lax.dot_general(a.astype(BF16), b.astype(BF16), dims, preferred_element_type=F32)


def _params(sem):
    return pltpu.CompilerParams(dimension_semantics=sem, vmem_limit_bytes=VMEM_LIMIT)


def _sigmoid(x):
    return jax.nn.sigmoid(x)


def _fold8(v):
    r, n = v.shape
    return jnp.sum(v.reshape(r // 8, 8, n), axis=0)


_VMEM = pl.BlockSpec(memory_space=pltpu.VMEM)
_ANY = pl.BlockSpec(memory_space=pl.ANY)
_SEMS = lambda n: pltpu.SemaphoreType.DMA((n,))


def _mesh_pos():
    return lax.axis_index("x"), lax.axis_index("y"), lax.axis_index("c")


def _flip(pos, k):
    return tuple(1 - p if (k >> s) & 1 else p for p, s in zip(pos, (2, 1, 0)))


def _index(pos):
    return 4 * pos[0] + 2 * pos[1] + pos[2]


class _Job:
    def __init__(self, ins, out_shape, sems, start, finish, aliases=None, middle=None):
        self.ins, self.out_shape, self.sems, self.start, self.finish = list(ins), list(out_shape), list(sems), start, finish
        self.aliases = dict(aliases or {})
        self.middle = middle


def _both(j1, j2):
    assert not j1.aliases and not j2.aliases
    n_in, n_out, n_sem = len(j1.ins), len(j1.out_shape), len(j1.sems)
    first = lambda ins, outs, sems: (ins[:n_in], outs[:n_out], sems[:n_sem])
    second = lambda ins, outs, sems: (ins[n_in:], outs[n_out:], sems[n_sem:])

    def start(*refs):
        j1.start(*first(*refs))
        j2.start(*second(*refs))

    def finish(*refs):
        j1.finish(*first(*refs))
        j2.finish(*second(*refs))

    return _Job(j1.ins + j2.ins, j1.out_shape + j2.out_shape, j1.sems + j2.sems, start, finish)


def _pcall(body, *, grid, in_specs, out_specs, out_shape, scratch_shapes, name, semantics, args, job=None, prefetch=()):
    n_pre = len(prefetch)

    def call(fn, in_specs_, out_specs_, out_shape_, scratch_, sem, operands, aliases):
        if n_pre:
            spec = pltpu.PrefetchScalarGridSpec(num_scalar_prefetch=n_pre, grid=grid, in_specs=in_specs_,
                                                out_specs=out_specs_, scratch_shapes=scratch_)
            return pl.pallas_call(fn, grid_spec=spec, out_shape=out_shape_, name=name, input_output_aliases=aliases,
                                  compiler_params=_params(sem))(*prefetch, *operands)
        return pl.pallas_call(fn, grid=grid, in_specs=in_specs_, out_specs=out_specs_, out_shape=out_shape_,
                              scratch_shapes=scratch_, name=name, input_output_aliases=aliases,
                              compiler_params=_params(sem))(*operands)

    if job is None:
        return list(call(body, in_specs, out_specs, out_shape, scratch_shapes, semantics, args, {})), []
    n_in, n_out, n_scr = len(in_specs), len(out_specs), len(scratch_shapes)
    j_in, j_out = len(job.ins), len(job.out_shape)
    steps = tuple(grid)

    def carrier(*refs):
        pre, refs = refs[:n_pre], refs[n_pre:]
        o = 0
        main_in, o = refs[o:o + n_in], o + n_in
        job_in, o = refs[o:o + j_in], o + j_in
        main_out, o = refs[o:o + n_out], o + n_out
        job_out, o = refs[o:o + j_out], o + j_out
        main_scr, job_sems = refs[o:o + n_scr], refs[o + n_scr:]
        ids = [pl.program_id(a) for a in range(len(steps))]
        first = functools.reduce(lambda p, q: p & q, [i == 0 for i in ids])
        last = functools.reduce(lambda p, q: p & q, [i == s - 1 for i, s in zip(ids, steps)])

        @pl.when(first)
        def _():
            job.start(job_in, job_out, job_sems)

        if job.middle is not None:
            flat, total = 0, 1
            for i, s in zip(ids, steps):
                flat, total = flat * s + i, total * s

            @pl.when(flat == total * 3 // 5)
            def _():
                job.middle(job_in, job_out, job_sems)

        body(*pre, *main_in, *main_out, *main_scr)

        @pl.when(last)
        def _():
            job.finish(job_in, job_out, job_sems)

    outs = call(carrier, list(in_specs) + [_ANY] * j_in, list(out_specs) + [_ANY] * j_out,
                list(out_shape) + job.out_shape, list(scratch_shapes) + job.sems, ("arbitrary",) * len(steps),
                list(args) + job.ins, {n_pre + n_in + i: n_out + o for i, o in job.aliases.items()})
    return list(outs[:n_out]), list(outs[n_out:])


def _gather_relay_job(shards, rows=None, into=None, alone=False):
    n = len(shards)
    rows = rows or [(0, s.shape[0]) for s in shards]
    into = into or [None] * n
    olds, aliases = [], {}
    for a, buf in enumerate(into):
        if buf is not None:
            aliases[n + len(olds)] = a
            olds.append(buf)

    def tools(ins, outs, sems):
        send_sems, recv_sems, local_sems = sems
        x, y, c = _mesh_pos()
        q = 2 * x + y
        chip_at = lambda rel: (1 - x if rel & 2 else x, 1 - y if rel & 1 else y)

        def part(a, chip, core):
            rs, (r0, r1) = shards[a].shape[0], rows[a]
            return outs[a].at[pl.ds((2 * chip + core) * rs + r0, r1 - r0), :]

        own = lambda a: ins[a].at[pl.ds(rows[a][0], rows[a][1] - rows[a][0]), :]

        def copy(a, slot, chip, core, to, src=None):
            blk = part(a, chip, core)
            return pltpu.make_async_remote_copy(src_ref=blk if src is None else src, dst_ref=blk,
                                                send_sem=send_sems.at[7 * a + slot], recv_sem=recv_sems.at[7 * a + slot],
                                                device_id=to, device_id_type=MESH)

        mine = [pltpu.make_async_copy(own(a), part(a, q, c), local_sems.at[a]) for a in range(n)]
        first = [copy(a, slot, q, c, (x, y, 1 - c) if slot == 0 else (*chip_at(slot), c), src=own(a))
                 for a in range(n) for slot in (0, 1, 2)]
        return x, y, c, q, chip_at, copy, mine, first

    def start(ins, outs, sems):
        *_, mine, first = tools(ins, outs, sems)
        for cp in mine + first:
            cp.start()

    def middle(ins, outs, sems):
        x, y, c, q, chip_at, copy, _, _ = tools(ins, outs, sems)
        me, sib = (x, y, c), (x, y, 1 - c)

        def relay(src, dst):
            for a in range(n):
                copy(a, src, q ^ src, c, me).wait_recv()
                copy(a, 3, q ^ src, c, (*chip_at(dst), c)).start()
                copy(a, 3 + src, q ^ src, c, sib).start()
            for a in range(n):
                copy(a, dst, q ^ dst, c, me).wait_recv()
                copy(a, 3 + dst, q ^ dst, c, sib).start()

        pl.when(c == 1)(lambda: relay(1, 2))
        pl.when(c == 0)(lambda: relay(2, 1))

    def finish(ins, outs, sems):
        if alone:
            middle(ins, outs, sems)
        x, y, c, q, chip_at, copy, mine, first = tools(ins, outs, sems)
        me, sib = (x, y, c), (x, y, 1 - c)
        for a in range(n):
            copy(a, 3, q ^ 3, c, me).wait_recv()
            copy(a, 6, q ^ 3, c, sib).start()
        for a in range(n):
            copy(a, 0, q, 1 - c, me).wait_recv()
            for rel in (1, 2, 3):
                copy(a, 3 + rel, q ^ rel, 1 - c, me).wait_recv()
        for a in range(n):
            for slot in range(3, 7):
                copy(a, slot, q, c, sib).wait_send()
        for cp in first:
            cp.wait_send()
        for cp in mine:
            cp.wait()

    return _Job(list(shards) + olds, [jax.ShapeDtypeStruct((N_DEV * s.shape[0], s.shape[1]), s.dtype) for s in shards],
                [_SEMS(7 * n), _SEMS(7 * n), _SEMS(n)], start, finish, aliases, middle=None if alone else middle)


def _pair_job(grads):
    n = len(grads)

    def copies(ins, outs, sems):
        send_sems, recv_sems = sems
        x, y, c = _mesh_pos()
        out = []
        for a in range(n):
            rs = grads[a].shape[0] // N_DEV
            for q in range(4):
                blk = ins[a].at[pl.ds((2 * q + 1 - c) * rs, rs), :]
                out.append(pltpu.make_async_remote_copy(
                    src_ref=blk, dst_ref=outs[a].at[q], send_sem=send_sems.at[4 * a + q], recv_sem=recv_sems.at[4 * a + q],
                    device_id=(x, y, 1 - c), device_id_type=MESH))
        return out

    def start(ins, outs, sems):
        for cp in copies(ins, outs, sems):
            cp.start()

    def finish(ins, outs, sems):
        for cp in copies(ins, outs, sems):
            cp.wait()

    return _Job(grads, [jax.ShapeDtypeStruct((4, g.shape[0] // N_DEV, g.shape[1]), g.dtype) for g in grads],
                [_SEMS(4 * n), _SEMS(4 * n)], start, finish)


def _chip_job(sums, rels=(1, 2, 3)):
    n, nr = len(sums), len(rels)

    def copies(ins, outs, sems):
        send_sems, recv_sems = sems
        x, y, c = _mesh_pos()
        out = []
        for a in range(n):
            for slot, r in enumerate(rels):
                px, py = (1 - x if r & 2 else x), (1 - y if r & 1 else y)
                out.append(pltpu.make_async_remote_copy(
                    src_ref=ins[a].at[2 * px + py], dst_ref=outs[a].at[slot], send_sem=send_sems.at[nr * a + slot],
                    recv_sem=recv_sems.at[nr * a + slot], device_id=(px, py, c), device_id_type=MESH))
        return out

    def start(ins, outs, sems):
        for cp in copies(ins, outs, sems):
            cp.start()

    def finish(ins, outs, sems):
        for cp in copies(ins, outs, sems):
            cp.wait()

    return _Job(sums, [jax.ShapeDtypeStruct((nr,) + s.shape[1:], s.dtype) for s in sums],
                [_SEMS(nr * n), _SEMS(nr * n)], start, finish)


def _mm(name, form, a_list, b, M, N, K, tm, tn, tk, extras, outs, epi, job=None):
    nI, nJ, nK = M // tm, N // tn, K // tk
    assert nI * tm == M and nJ * tn == N and nK * tk == K
    dims = {"nn": NN, "nt": NT, "tn": TN}[form]
    b_list = b if isinstance(b, list) else [(b, {"nn": N, "nt": K, "tn": N}[form])]
    nA, nB = len(a_list), len(b_list)
    assert nA == 1 or nB == 1
    assert nB == 1 or form in ("nn", "nt")
    AXIS = {"i": 0, "j": 1, "k": 2}
    a_axis, a_tile = ("i", tm) if form == "tn" else ("k", tk)
    b_axis, b_tile = ("k", tk) if form == "nt" else ("j", tn)

    def cut(pieces, tile, total):
        starts, s = [], 0
        for _, w in pieces:
            assert w % tile == 0
            starts.append(s // tile)
            s += w
        assert s == total
        return starts, [w // tile for _, w in pieces]

    a_st, a_cn = cut(a_list, a_tile, M if form == "tn" else K)
    b_st, b_cn = cut(b_list, b_tile, K if form == "nt" else N)

    def inside(idx, st, cn):
        return (idx >= st) & (idx < st + cn)

    def a_spec(p):
        st, cn = a_st[p], a_cn[p]
        if form == "tn":
            return pl.BlockSpec((tk, tm), lambda i, j, k: (jnp.where(inside(i, st, cn), k, 0), jnp.clip(i - st, 0, cn - 1)))
        return pl.BlockSpec((tm, tk), lambda i, j, k: (i, jnp.clip(k - st, 0, cn - 1)))

    def b_spec(p):
        st, cn = b_st[p], b_cn[p]
        if form == "nt":
            return pl.BlockSpec((tn, tk), lambda i, j, k: (j, jnp.clip(k - st, 0, cn - 1)))
        if nB == 1:
            return pl.BlockSpec((tk, tn), lambda i, j, k: (k, j))
        return pl.BlockSpec((tk, tn), lambda i, j, k: (jnp.where(inside(j, st, cn), k, 0), jnp.clip(j - st, 0, cn - 1)))

    in_specs = ([a_spec(p) for p in range(nA)] + [b_spec(p) for p in range(nB)]
                + [pl.BlockSpec(bs, im) for _, bs, im in extras])
    out_shape = [jax.ShapeDtypeStruct(s_, d_) for s_, d_, _, _ in outs]
    out_specs = [pl.BlockSpec(bs, im) for _, _, bs, im in outs]
    nE, nO = len(extras), len(outs)
    single = nA == 1 and nB == 1

    def body(*refs):
        a_refs, b_refs = refs[:nA], refs[nA:nA + nB]
        ex, ou = refs[nA + nB:nA + nB + nE], refs[nA + nB + nE:nA + nB + nE + nO]
        ids = [pl.program_id(a) for a in range(3)]

        def partial_of(p, q):
            return lax.dot_general(a_refs[p][...], b_refs[q][...], dims, preferred_element_type=F32)

        if nK == 1 and single:
            epi(partial_of(0, 0), ex, ou)
            return
        acc = refs[-1]
        k = ids[2]
        for p in range(nA):
            for q in range(nB):
                def first(p=p, q=q):
                    acc[...] = partial_of(p, q)

                def later(p=p, q=q):
                    acc[...] += partial_of(p, q)

                here = None
                if nA > 1:
                    here = inside(ids[AXIS[a_axis]], a_st[p], a_cn[p])
                if nB > 1:
                    here = inside(ids[AXIS[b_axis]], b_st[q], b_cn[q])
                pl.when(k == 0 if here is None else here & (k == 0))(first)
                pl.when(k > 0 if here is None else here & (k > 0))(later)

        @pl.when(k == nK - 1)
        def _():
            epi(acc[...], ex, ou)

    scratch = [] if (nK == 1 and single) else [pltpu.VMEM((tm, tn), F32)]
    res, job_res = _pcall(
        body, grid=(nI, nJ, nK), in_specs=in_specs, out_specs=out_specs, out_shape=out_shape, scratch_shapes=scratch,
        name=name, semantics=("parallel", "parallel", "arbitrary"),
        args=[a for a, _ in a_list] + [p for p, _ in b_list] + [e for e, _, _ in extras], job=job)
    return res if job is None else (res, job_res)


def _twin_mm(name, form, pairs, M, N, K, tm, tn, tk, out_dtype):
    nI, nJ, nK = M // tm, N // tn, K // tk
    dims = {"nn": NN, "tn": TN}[form]
    a_spec = (pl.BlockSpec((tm, tk), lambda i, j, k: (i, k)) if form == "nn" else pl.BlockSpec((tk, tm), lambda i, j, k: (k, i)))
    b_spec = pl.BlockSpec((tk, tn), lambda i, j, k: (k, j))
    o_spec = pl.BlockSpec((tm, tn), lambda i, j, k: (i, j))

    def body(a1, b1, a2, b2, o1, o2, *accs):
        k = pl.program_id(2)
        for a_ref, b_ref, o_ref, acc in ((a1, b1, o1, accs[0] if accs else None), (a2, b2, o2, accs[1] if accs else None)):
            part = lax.dot_general(a_ref[...], b_ref[...], dims, preferred_element_type=F32)
            if nK == 1:
                o_ref[...] = part.astype(out_dtype)
                continue

            @pl.when(k == 0)
            def _(acc=acc, part=part):
                acc[...] = part

            @pl.when(k > 0)
            def _(acc=acc, part=part):
                acc[...] += part

            @pl.when(k == nK - 1)
            def _(acc=acc, o_ref=o_ref):
                o_ref[...] = acc[...].astype(out_dtype)

    (a1, b1), (a2, b2) = pairs
    shape = jax.ShapeDtypeStruct((M, N), out_dtype)
    return pl.pallas_call(
        body, grid=(nI, nJ, nK), in_specs=[a_spec, b_spec, a_spec, b_spec], out_specs=[o_spec, o_spec],
        out_shape=[shape, shape], scratch_shapes=[] if nK == 1 else [pltpu.VMEM((tm, tn), F32)] * 2, name=name,
        compiler_params=_params(("parallel", "parallel", "arbitrary")))(a1, b1, a2, b2)


def _piece_tiles(pieces, tile):
    starts, s = [], 0
    for _, w in pieces:
        assert w % tile == 0
        starts.append(s // tile)
        s += w
    return starts, [w // tile for _, w in pieces], s


def _pieces_tn(name, pieces, b, tile, job=None):
    T, N = b.shape
    st, cn, M = _piece_tiles(pieces, tile)
    nP, nI = len(pieces), M // tile

    def body(*refs):
        p_refs, b_hbm, o_ref = refs[:nP], refs[nP], refs[nP + 1]
        bbuf, abuf, bsem, asem = refs[nP + 2:]
        i = pl.program_id(0)

        def fetch(step, slot):
            for p in range(nP):
                @pl.when((step >= st[p]) & (step < st[p] + cn[p]))
                def _():
                    col = pl.multiple_of((step - st[p]) * tile, tile)
                    pltpu.make_async_copy(p_refs[p].at[pl.ds(0, T), pl.ds(col, tile)], abuf.at[slot], asem.at[slot]).start()

        @pl.when(i == 0)
        def _():
            whole = pltpu.make_async_copy(b_hbm, bbuf, bsem)
            whole.start()
            fetch(0, 0)
            whole.wait()

        @pl.when(i + 1 < nI)
        def _():
            fetch(i + 1, (i + 1) % 2)

        pltpu.make_async_copy(p_refs[0].at[pl.ds(0, T), pl.ds(0, tile)], abuf.at[i % 2], asem.at[i % 2]).wait()
        o_ref[...] = lax.dot_general(abuf[i % 2], bbuf[...], TN, preferred_element_type=F32).astype(BF16)

    res, job_res = _pcall(
        body, grid=(nI,), in_specs=[_ANY] * (nP + 1), out_specs=[pl.BlockSpec((tile, N), lambda i: (i, 0))],
        out_shape=[jax.ShapeDtypeStruct((M, N), BF16)],
        scratch_shapes=[pltpu.VMEM((T, N), b.dtype), pltpu.VMEM((2, T, tile), b.dtype), pltpu.SemaphoreType.DMA, _SEMS(2)],
        name=name, semantics=("arbitrary",), args=[p for p, _ in pieces] + [b], job=job)
    return res if job is None else (res, job_res)


def _rows_mm(name, pieces, w, T, tm, tk, vecs, bufs, parts, epi, job=None):
    st, cn, K = _piece_tiles(pieces, tk)
    nP, nI, nK = len(pieces), T // tm, K // tk
    part_specs = [pl.BlockSpec(bs, lambda i, k, im=im: im(i, 0, k)) for _, _, bs, im in parts]
    n_vec, nB = len(vecs), len(bufs)
    load_ix = [n for n, (_, src, _) in enumerate(bufs) if src is not None]
    store_ix = [n for n, (_, _, store) in enumerate(bufs) if store]
    n_any_in, n_any_out = len(load_ix), len(store_ix)

    def body(*refs):
        o = nP
        p_refs, w_ref = refs[:nP], refs[o]
        vec_refs = refs[o + 1:o + 1 + n_vec]
        ins = refs[o + 1 + n_vec:o + 1 + n_vec + n_any_in]
        o = o + 1 + n_vec + n_any_in
        hbm_outs, p_outs = refs[o:o + n_any_out], refs[o + n_any_out:o + n_any_out + len(parts)]
        o = o + n_any_out + len(parts)
        acc, abuf = refs[o:o + 2]
        buf_refs = refs[o + 2:o + 2 + nB]
        asem, in_sems, out_sems = refs[-3:]
        i, k = pl.program_id(0), pl.program_id(1)
        g = i * nK + k
        rows_of = lambda ref, ii: ref.at[pl.ds(pl.multiple_of(ii * tm, tm), tm), :]
        bufs_in = [buf_refs[n] for n in load_ix]
        bufs_out = [buf_refs[n] for n in store_ix]

        def fetch(ii, kk, slot):
            for p in range(nP):
                @pl.when((kk >= st[p]) & (kk < st[p] + cn[p]))
                def _():
                    col = pl.multiple_of((kk - st[p]) * tk, tk)
                    src = p_refs[p].at[pl.ds(pl.multiple_of(ii * tm, tm), tm), pl.ds(col, tk)]
                    pltpu.make_async_copy(src, abuf.at[slot], asem.at[slot]).start()

        loads = lambda ii: [pltpu.make_async_copy(rows_of(src, ii), buf, in_sems.at[n])
                            for n, (src, buf) in enumerate(zip(ins, bufs_in))]
        stores = lambda ii: [pltpu.make_async_copy(buf, rows_of(dst, ii), out_sems.at[n])
                             for n, (buf, dst) in enumerate(zip(bufs_out, hbm_outs))]

        @pl.when(g == 0)
        def _():
            fetch(0, 0, 0)

        @pl.when(g + 1 < nI * nK)
        def _():
            last_k = k == nK - 1
            fetch(jnp.where(last_k, i + 1, i), jnp.where(last_k, 0, k + 1), (g + 1) % 2)

        @pl.when(k == 0)
        def _():
            @pl.when(i > 0)
            def _():
                for cp in stores(i - 1):
                    cp.wait()
            for cp in loads(i):
                cp.start()

        pltpu.make_async_copy(p_refs[0].at[pl.ds(0, tm), pl.ds(0, tk)], abuf.at[g % 2], asem.at[g % 2]).wait()

        def product(cols):
            return jnp.dot(abuf[g % 2], w_ref[:, cols], preferred_element_type=F32)

        col_blocks = [slice(c0, c0 + 512) for c0 in range(0, D, 512)]

        @pl.when(k == 0)
        def _():
            for cols in col_blocks:
                acc[:, cols] = product(cols)

        @pl.when(k > 0)
        def _():
            for cols in col_blocks:
                acc[:, cols] += product(cols)

        @pl.when(k == nK - 1)
        def _():
            for cp in loads(i):
                cp.wait()
            epi(acc, vec_refs, buf_refs, p_outs)
            for cp in stores(i):
                cp.start()

            @pl.when(i == nI - 1)
            def _():
                for cp in stores(i):
                    cp.wait()

    vec = pl.BlockSpec((1, D), lambda i, k: (0, 0))
    scratch = ([pltpu.VMEM((tm, D), F32), pltpu.VMEM((2, tm, tk), BF16)] + [pltpu.VMEM((tm, D), dt) for dt, _, _ in bufs]
               + [_SEMS(2), _SEMS(n_any_in), _SEMS(n_any_out)])
    res, job_res = _pcall(
        body, grid=(nI, nK),
        in_specs=[_ANY] * nP + [pl.BlockSpec((tk, D), lambda i, k: (k, 0))] + [vec] * n_vec + [_ANY] * n_any_in,
        out_specs=[_ANY] * n_any_out + part_specs,
        out_shape=([jax.ShapeDtypeStruct((T, D), bufs[n][0]) for n in store_ix]
                   + [jax.ShapeDtypeStruct(s, d) for s, d, _, _ in parts]),
        scratch_shapes=scratch, name=name, semantics=("arbitrary", "arbitrary"),
        args=[p for p, _ in pieces] + [w] + list(vecs) + [bufs[n][1] for n in load_ix], job=job)
    return res if job is None else (res, job_res)


def _pieces_nn_rms(name, pieces, w, x, gain, sc, dres, tm, tk, job=None):
    _, outs, epi = _rms_mod_bwd_epilogue(x, gain, sc, dres, tm)

    def on_rows(acc, vecs, bufs, parts):
        epi(acc, [bufs[0], vecs[0], vecs[1], bufs[1]], [bufs[1], *parts])

    return _rows_mm(name, pieces, w, x.shape[0], tm, tk, [gain, sc], [(F32, x, False), (F32, dres, True)],
                    outs[1:], on_rows, job=job)


def _rms_mod_fwd(name, x, gain, sc, sh, tr, job=None):
    T = x.shape[0]

    def body(x_ref, g_ref, sc_ref, sh_ref, h_ref):
        xv = x_ref[...]
        rstd = lax.rsqrt(jnp.mean(xv * xv, axis=-1, keepdims=True) + EPS)
        h_ref[...] = ((xv * rstd * g_ref[...]) * (1.0 + sc_ref[...]) + sh_ref[...]).astype(BF16)

    row = pl.BlockSpec((tr, D), lambda i: (i, 0))
    vec = pl.BlockSpec((1, D), lambda i: (0, 0))
    return _pcall(body, grid=(T // tr,), in_specs=[row, vec, vec, vec], out_specs=[row],
                  out_shape=[jax.ShapeDtypeStruct((T, D), BF16)], scratch_shapes=[], name=name, semantics=("parallel",),
                  args=[x, gain, sc, sh], job=job)


def _rms_mod_bwd_epilogue(x, gain, sc, dres, tm, gate=None, mo=None):
    T = x.shape[0]
    with_gate = gate is not None
    row = ((tm, D), lambda i, j, k: (i, 0))
    vec = ((1, D), lambda i, j, k: (0, 0))
    part = ((T // tm * 8, D), F32, (8, D), lambda i, j, k: (i, 0))
    extras = [(x, *row), (gain, *vec), (sc, *vec), (dres, *row)]
    outs = [((T, D), F32, *row), part, part, part]
    if with_gate:
        extras += [(gate, *vec), (mo, *row)]
        outs += [((T, D), BF16, *row), part]

    rows = min(64, tm)

    def epi(acc, ex, ou):
        g = ex[1][...]
        sums = [jnp.zeros((8, D), F32) for _ in range(4)]
        for r0 in range(0, tm, rows):
            rs = slice(r0, r0 + rows)
            dhv, xv = acc[rs, :], ex[0][rs, :]
            rstd = lax.rsqrt(jnp.mean(xv * xv, axis=-1, keepdims=True) + EPS)
            xhat = xv * rstd
            dn = dhv * (1.0 + ex[2][...])
            dxhat = dn * g
            dx = ex[3][rs, :] + rstd * (dxhat - xhat * jnp.mean(dxhat * xhat, axis=-1, keepdims=True))
            ou[0][rs, :] = dx
            terms = [dhv, dhv * (xhat * g), dn * xhat]
            if with_gate:
                terms.append(dx * ex[5][rs, :].astype(F32))
                ou[4][rs, :] = (ex[4][...] * dx).astype(BF16)
            sums = [s + _fold8(t) for s, t in zip(sums, terms)] + sums[len(terms):]
        ou[1][...], ou[2][...], ou[3][...] = sums[:3]
        if with_gate:
            ou[5][...] = sums[3]

    return extras, outs, epi


def _rms_mod_bwd(name, dh, x, gain, sc, dres, tr, gate=None, mo=None):
    T = x.shape[0]
    extras, outs, epi = _rms_mod_bwd_epilogue(x, gain, sc, dres, tr, gate, mo)
    rows_only = lambda im: (lambda i: im(i, 0, 0))
    nE = len(extras)

    def body(dh_ref, *refs):
        epi(dh_ref, refs[:nE], refs[nE:])

    return pl.pallas_call(
        body, grid=(T // tr,),
        in_specs=[pl.BlockSpec((tr, D), lambda i: (i, 0))] + [pl.BlockSpec(bs, rows_only(im)) for _, bs, im in extras],
        out_specs=[pl.BlockSpec(bs, rows_only(im)) for _, _, bs, im in outs],
        out_shape=[jax.ShapeDtypeStruct(s, d) for s, d, _, _ in outs], name=name, compiler_params=_params(("parallel",)),
    )(dh, *[e for e, _, _ in extras])


def _split3(v):
    h = v.astype(BF16)
    r1 = v - h.astype(F32)
    m = r1.astype(BF16)
    lo = (r1 - m.astype(F32)).astype(BF16)
    return h, m, lo


def _tri_mm(tri, v, dims=NN):
    h, m, lo = _split3(v)
    t = tri.astype(BF16)
    mm = lambda p: lax.dot_general(t, p, dims, preferred_element_type=F32)
    return (mm(lo) + mm(m)) + mm(h)


def _hgrn_chunk_terms(q, fl, lb):
    sig = _sigmoid(fl)
    f = lb + (1.0 - lb) * sig
    lf = jnp.log(f)
    kk = 1.0 - f
    sq = _sigmoid(q)
    qf = q * sq
    return sig, f, lf, kk, sq, qf


def _causal(n):
    r = lax.broadcasted_iota(jnp.int32, (n, n), 0)
    c = lax.broadcasted_iota(jnp.int32, (n, n), 1)
    return r >= c


def _hgrn_fwd(proj, lb_logits, o_gain, tt, job=None):
    T = proj.shape[0]
    nT, ncl = T // tt, tt // CHUNK
    C = CHUNK

    def body(q_ref, f_ref, i_ref, g_ref, lbl_ref, og_ref, y_ref, st_ref, S):
        @pl.when(pl.program_id(1) == 0)
        def _():
            S[...] = jnp.zeros_like(S)

        lbl = lbl_ref[...]
        lb = _sigmoid(lbl[0:1, :] - lbl[1:2, :])
        og = og_ref[...]
        shp = (ncl, C, A_HD)
        q, fl, v, g = (r[...].reshape(shp) for r in (q_ref, f_ref, i_ref, g_ref))
        tri = jnp.broadcast_to(_causal(C), (ncl, C, C))
        _, _, lf, kk, _, qf = _hgrn_chunk_terms(q, fl, lb)
        b = _tri_mm(tri, lf, BNN)
        bm, bl = b[:, C // 2 - 1:C // 2, :], b[:, C - 1:C, :]
        qd, kd = qf * jnp.exp(b - bm), kk * jnp.exp(bm - b)
        A = jnp.where(tri, _dot(qd, kd, BNT), 0.0)
        d_st = _dot(v, kk * jnp.exp(bl - b), BTN)
        dec = jnp.exp(bl)
        st = S[...]
        for ci in range(ncl):
            st_ref[0, ci] = st
            st = st * dec[ci] + d_st[ci]
        S[...] = st
        o = _dot(A, v, BNN) + _dot(qf * jnp.exp(b), st_ref[0], BNT)
        r = lax.rsqrt(jnp.mean(o * o, axis=-1, keepdims=True) + EPS)
        y_ref[...] = (o * r * og * (g * _sigmoid(g))).astype(BF16).reshape(tt, A_HD)

    def col(off):
        return pl.BlockSpec((tt, A_HD), lambda h, t: (t, off // A_HD + h))

    head_vec = lambda rows: pl.BlockSpec((rows, A_HD), lambda h, t: (0, h))
    return _pcall(
        body, grid=(A_HEADS, nT),
        in_specs=[col(OFF_QA), col(OFF_FA), col(OFF_IA), col(OFF_GA), head_vec(2), head_vec(1)],
        out_specs=[pl.BlockSpec((tt, A_HD), lambda h, t: (t, h)),
                   pl.BlockSpec((1, ncl, A_HD, A_HD), lambda h, t: (h, t, 0, 0))],
        out_shape=[jax.ShapeDtypeStruct((T, AW), BF16),
                   jax.ShapeDtypeStruct((A_HEADS, T // C, A_HD, A_HD), F32)],
        scratch_shapes=[pltpu.VMEM((A_HD, A_HD), F32)], name="hgrn_fwd", semantics=("parallel", "arbitrary"),
        args=[proj, proj, proj, proj, lb_logits, o_gain], job=job)


def _hgrn_bwd(proj, st, dy, lb_logits, o_gain, tt, job=None):
    T = proj.shape[0]
    nT, ncl = T // tt, tt // CHUNK
    C = CHUNK

    def body(q_ref, f_ref, i_ref, g_ref, st_ref, dy_ref, lbl_ref, og_ref,
             dq_ref, df_ref, di_ref, dg_ref, plb_ref, pog_ref, dS):
        @pl.when(pl.program_id(1) == 0)
        def _():
            dS[...] = jnp.zeros_like(dS)

        lbl = lbl_ref[...]
        lb = _sigmoid(lbl[0:1, :] - lbl[1:2, :])
        og = og_ref[...]
        shp = (ncl, C, A_HD)
        flat = lambda t: t.reshape(tt, A_HD)
        q, fl, v, g, dout = (r[...].reshape(shp) for r in (q_ref, f_ref, i_ref, g_ref, dy_ref))
        tri = jnp.broadcast_to(_causal(C), (ncl, C, C))
        rowi = lax.broadcasted_iota(jnp.int32, shp, 1)
        st0 = st_ref[0]
        sig, f, lf, kk, sq, qf = _hgrn_chunk_terms(q, fl, lb)
        b = _tri_mm(tri, lf, BNN)
        bm, bl = b[:, C // 2 - 1:C // 2, :], b[:, C - 1:C, :]
        e_qd, e_kd, e_ke, e_b = jnp.exp(b - bm), jnp.exp(bm - b), jnp.exp(bl - b), jnp.exp(b)
        qd, kd, ke, qe = qf * e_qd, kk * e_kd, kk * e_ke, qf * e_b
        dec = jnp.exp(bl)
        A = jnp.where(tri, _dot(qd, kd, BNT), 0.0)
        o = _dot(A, v, BNN) + _dot(qe, st0, BNT)
        r = lax.rsqrt(jnp.mean(o * o, axis=-1, keepdims=True) + EPS)
        sg = _sigmoid(g)
        on = o * r * og
        dg_ref[...] = flat((dout * on * (sg * (1.0 + g * (1.0 - sg)))).astype(BF16))
        don = dout * (g * sg)
        pog_ref[...] = _fold8(flat(don * o * r))
        dyh = don * og
        do = r * (dyh - o * (r * r) * jnp.mean(dyh * o, axis=-1, keepdims=True))
        g_st = _dot(do, qe, BTN)
        run = dS[...]
        after = [None] * ncl
        for ci in reversed(range(ncl)):
            after[ci] = run
            run = g_st[ci] + run * dec[ci]
        dS[...] = run
        d_after = jnp.stack(after, axis=0)
        ddec = jnp.sum(d_after * st0, axis=1, keepdims=True)
        dqe = _dot(do, st0, BNN)
        dke = _dot(v, d_after, BNN)
        dA = jnp.where(tri, _dot(do, v, BNT), 0.0)
        dv = _dot(ke, d_after, BNT) + _dot(A, do, BTN)
        dqd = _dot(dA, kd, BNN)
        dkd = _dot(dA, qd, BTN)
        di_ref[...] = flat(dv.astype(BF16))
        dqf = dqe * e_b + dqd * e_qd
        dkk = dkd * e_kd + dke * e_ke
        t_qd, t_kd, t_ke = dqd * qd, dkd * kd, dke * ke
        db = dqe * qe + t_qd - t_kd - t_ke
        dbm = jnp.sum(t_kd - t_qd, axis=1, keepdims=True)
        dbl = jnp.sum(t_ke, axis=1, keepdims=True) + ddec * dec
        db = db + jnp.where(rowi == C // 2 - 1, dbm, 0.0) + jnp.where(rowi == C - 1, dbl, 0.0)
        dlf = _tri_mm(tri, db, BTN)
        dfv = dlf / f - dkk
        df_ref[...] = flat((dfv * (1.0 - lb) * sig * (1.0 - sig)).astype(BF16))
        plb_ref[...] = _fold8(flat(dfv * (1.0 - sig)))
        dq_ref[...] = flat((dqf * (sq * (1.0 + q * (1.0 - sq)))).astype(BF16))

    def col(off):
        return pl.BlockSpec((tt, A_HD), lambda h, t: (nT - 1 - t, off // A_HD + h))

    head_vec = lambda rows: pl.BlockSpec((rows, A_HD), lambda h, t: (0, h))
    o_spec = pl.BlockSpec((tt, A_HD), lambda h, t: (nT - 1 - t, h))
    p_spec = pl.BlockSpec((8, A_HD), lambda h, t: (t, h))
    o_shape = jax.ShapeDtypeStruct((T, AW), BF16)
    p_shape = jax.ShapeDtypeStruct((nT * 8, AW), F32)
    return _pcall(
        body, grid=(A_HEADS, nT),
        in_specs=[col(OFF_QA), col(OFF_FA), col(OFF_IA), col(OFF_GA),
                  pl.BlockSpec((1, ncl, A_HD, A_HD), lambda h, t: (h, nT - 1 - t, 0, 0)),
                  pl.BlockSpec((tt, A_HD), lambda h, t: (nT - 1 - t, h)), head_vec(2), head_vec(1)],
        out_specs=[o_spec, o_spec, o_spec, o_spec, p_spec, p_spec],
        out_shape=[o_shape, o_shape, o_shape, o_shape, p_shape, p_shape],
        scratch_shapes=[pltpu.VMEM((A_HD, A_HD), F32)], name="hgrn_bwd", semantics=("parallel", "arbitrary"),
        args=[proj, proj, proj, proj, st, dy, lb_logits, o_gain], job=job)


LANES = 128
Q_COLS = BW // LANES


def _low_half():
    return lax.broadcasted_iota(jnp.int32, (1, LANES), 1) < B_HD


def _half_sum(t, low):
    lo = jnp.sum(jnp.where(low, t, 0.0), axis=-1, keepdims=True)
    hi = jnp.sum(jnp.where(low, 0.0, t), axis=-1, keepdims=True)
    return jnp.where(low, lo, hi)


def _half_rms(t, low):
    r = lax.rsqrt(_half_sum(t * t, low) * (1.0 / B_HD) + EPS)
    return t * r, r


def _fold_halves(p, low):
    return jnp.where(low, p + pltpu.roll(p, B_HD, 1), 0.0)


def _stack_cols(x):
    return jnp.stack([x[:, c * LANES:(c + 1) * LANES] for c in range(Q_COLS)], axis=0).reshape(KV_HEADS, 2 * BLK, LANES)


def _col_of(t, c):
    return t[c // 2, (c % 2) * BLK:(c % 2 + 1) * BLK]


def _split_halves(col, s, low):
    own = jnp.where(low if s == 0 else jnp.logical_not(low), col, 0.0)
    other = pltpu.roll(own, B_HD, 1)
    return (own, other) if s == 0 else (other, own)


def _swa_keys(kp_ref, kc_ref, vp_ref, vc_ref, kg, low):
    k_lo, k_hi, v_lo, v_hi, hats = [], [], [], [], []
    for j in range(KVW // LANES):
        cs = slice(j * LANES, (j + 1) * LANES)
        k_hat, k_r = _half_rms(jnp.concatenate([kp_ref[:, cs], kc_ref[:, cs]], axis=0), low)
        vcol = jnp.concatenate([vp_ref[:, cs], vc_ref[:, cs]], axis=0)
        hats.append((k_hat, k_r))
        for s in range(2):
            for dst_lo, dst_hi, col in ((k_lo, k_hi, k_hat * kg), (v_lo, v_hi, vcol)):
                lo, hi = _split_halves(col, s, low)
                dst_lo.append(lo)
                dst_hi.append(hi)
    st = lambda parts: jnp.stack(parts, axis=0)
    return st(k_lo), st(k_hi), st(v_lo), st(v_hi), hats


def _swa_mask(first_block):
    qi = lax.broadcasted_iota(jnp.int32, (BLK, 2 * BLK), 0) + BLK
    ki = lax.broadcasted_iota(jnp.int32, (BLK, 2 * BLK), 1)
    rel = qi - ki
    m = (rel >= 0) & (rel < BLK) & (jnp.logical_not(first_block) | (ki >= BLK))
    return jnp.concatenate([m, m], axis=0)


def _sink_cols(sk_ref, hi):
    top = lax.broadcasted_iota(jnp.int32, (2 * BLK, 1), 0) < BLK
    return jnp.stack([jnp.where(top, sk_ref[0, GROUP * hk + hi], sk_ref[0, GROUP * hk + 2 + hi])
                      for hk in range(KV_HEADS)], axis=0)


def _swa_probs(qn, k_half, sink, mask):
    s = jnp.where(mask, _dot(qn, k_half, BNT) * (B_HD ** -0.5), NEG)
    m = jnp.maximum(jnp.max(s, axis=-1, keepdims=True), sink)
    p = jnp.exp(s - m)
    ps = jnp.exp(sink - m)
    inv = 1.0 / (jnp.sum(p, axis=-1, keepdims=True) + ps)
    return p * inv, ps * inv


def _swa_fwd(proj, q_gain, k_gain, sinks, job=None):
    T = proj.shape[0]
    nb = T // BLK

    def body(q_ref, kc_ref, kp_ref, vc_ref, vp_ref, qg_ref, kg_ref, sk_ref, o_ref):
        low = _low_half()
        mask = _swa_mask(pl.program_id(0) == 0)
        qn = _half_rms(_stack_cols(q_ref[...]), low)[0] * qg_ref[...]
        k_lo, k_hi, v_lo, v_hi, _ = _swa_keys(kp_ref, kc_ref, vp_ref, vc_ref, kg_ref[...], low)
        p_lo, _ = _swa_probs(qn, k_lo, _sink_cols(sk_ref, 0), mask)
        p_hi, _ = _swa_probs(qn, k_hi, _sink_cols(sk_ref, 1), mask)
        o = (_dot(p_lo, v_lo, BNN) + _dot(p_hi, v_hi, BNN)).astype(BF16)
        for c in range(Q_COLS):
            o_ref[:, c * LANES:(c + 1) * LANES] = _col_of(o, c)

    q_gain, k_gain = jnp.tile(q_gain, (1, 2)), jnp.tile(k_gain, (1, 2))
    cur = lambda w, off: pl.BlockSpec((BLK, w), lambda i: (i, off // w))
    prev = lambda w, off: pl.BlockSpec((BLK, w), lambda i: (jnp.maximum(i - 1, 0), off // w))
    small = lambda n: pl.BlockSpec((1, 2 * n), lambda i: (0, 0))
    return _pcall(
        body, grid=(nb,),
        in_specs=[cur(BW, OFF_QB), cur(KVW, OFF_KB), prev(KVW, OFF_KB), cur(KVW, OFF_VB), prev(KVW, OFF_VB),
                  small(B_HD), small(B_HD), pl.BlockSpec(memory_space=pltpu.SMEM)],
        out_specs=[pl.BlockSpec((BLK, BW), lambda i: (i, 0))],
        out_shape=[jax.ShapeDtypeStruct((T, BW), BF16)], scratch_shapes=[], name="swa_fwd", semantics=("parallel",),
        args=[proj, proj, proj, proj, proj, q_gain, k_gain, sinks], job=job)


def _swa_bwd(proj, dout, q_gain, k_gain, sinks, job=None):
    T = proj.shape[0]
    nb = T // BLK
    W = BW + 2 * KVW

    def body(q_ref, kc_ref, kp_ref, vc_ref, vp_ref, do_ref, qg_ref, kg_ref, sk_ref,
             dq_ref, dkv_ref, pqg_ref, pkg_ref, psk_ref, dkn_c, dv_c):
        i = pl.program_id(0)
        live = i < nb
        low = _low_half()
        high = jnp.logical_not(low)
        qg, kg = qg_ref[...], kg_ref[...]
        mask = _swa_mask(i == 0)
        lane = lax.broadcasted_iota(jnp.int32, (1, LANES), 1)
        scale = B_HD ** -0.5

        @pl.when(i == 0)
        def _():
            dkn_c[...] = jnp.zeros_like(dkn_c)
            dv_c[...] = jnp.zeros_like(dv_c)

        q_hat, q_r = _half_rms(_stack_cols(q_ref[...]), low)
        qn = q_hat * qg
        k_lo, k_hi, v_lo, v_hi, hats = _swa_keys(kp_ref, kc_ref, vp_ref, vc_ref, kg, low)
        do = _stack_cols(do_ref[...])
        dqn = jnp.zeros((KV_HEADS, 2 * BLK, LANES), F32)
        acc_sk = jnp.zeros((1, LANES), F32)
        dk_parts, dv_parts = [], []
        for hi, (k_h, v_h) in enumerate(((k_lo, v_lo), (k_hi, v_hi))):
            p, ps = _swa_probs(qn, k_h, _sink_cols(sk_ref, hi), mask)
            dp = _dot(do, v_h, BNT)
            delta = jnp.sum(p * dp, axis=-1, keepdims=True)
            ds = p * (dp - delta) * scale
            dqn = dqn + _dot(ds, k_h, BNN)
            dk_parts.append(_dot(ds, qn, BTN))
            dv_parts.append(_dot(p, do, BTN))
            t = ps * delta
            for hk in range(KV_HEADS):
                for rows in range(2):
                    h = GROUP * hk + 2 * rows + hi
                    acc_sk = acc_sk + jnp.where(
                        lane == h, -jnp.sum(t[hk, rows * BLK:(rows + 1) * BLK], axis=0, keepdims=True), 0.0)
        dqh = dqn * qg
        dq = (q_r * (dqh - q_hat * (_half_sum(dqh * q_hat, low) * (1.0 / B_HD)))).astype(BF16)
        for c in range(Q_COLS):
            dq_ref[:, c * LANES:(c + 1) * LANES] = _col_of(dq, c)
        acc_qg = _fold_halves(_fold8((dqn * q_hat).reshape(KV_HEADS * 2 * BLK, LANES)), low)

        def native(parts, j):
            lo_arr, hi_arr = parts
            a, b = 2 * j, 2 * j + 1
            return (jnp.where(low, lo_arr[a], 0.0) + pltpu.roll(jnp.where(high, hi_arr[a], 0.0), B_HD, 1)
                    + jnp.where(high, hi_arr[b], 0.0) + pltpu.roll(jnp.where(low, lo_arr[b], 0.0), B_HD, 1))

        acc_kg = jnp.zeros((8, LANES), F32)
        for j in range(KVW // LANES):
            cs = slice(j * LANES, (j + 1) * LANES)
            dkn = jnp.where(live, native(dk_parts, j), 0.0)
            dvc = jnp.where(live, native(dv_parts, j), 0.0)
            kp_hat, kp_r = hats[j][0][:BLK], hats[j][1][:BLK]
            dkn_prev = dkn_c[:, cs] + dkn[:BLK]
            dv_prev = dv_c[:, cs] + dvc[:BLK]
            acc_kg = acc_kg + _fold8(dkn_prev * kp_hat)
            dkh = dkn_prev * kg
            dkv_ref[:, cs] = (kp_r * (dkh - kp_hat * (_half_sum(dkh * kp_hat, low) * (1.0 / B_HD)))).astype(BF16)
            dkv_ref[:, KVW + j * LANES:KVW + (j + 1) * LANES] = dv_prev.astype(BF16)
            dkn_c[:, cs] = dkn[BLK:]
            dv_c[:, cs] = dvc[BLK:]
        keep = jnp.where(i > 0, 1.0, 0.0)
        pqg_ref[...] = jnp.where(live, acc_qg, 0.0)
        pkg_ref[...] = _fold_halves(acc_kg, low) * keep
        psk_ref[...] = jnp.broadcast_to(jnp.where(live, acc_sk, 0.0), (8, LANES)) * (
            lax.broadcasted_iota(jnp.int32, (8, LANES), 0) == 0).astype(F32)

    q_gain, k_gain = jnp.tile(q_gain, (1, 2)), jnp.tile(k_gain, (1, 2))
    last = nb - 1
    cur = lambda w, off: pl.BlockSpec((BLK, w), lambda i: (jnp.minimum(i, last), off // w))
    prev = lambda w, off: pl.BlockSpec((BLK, w), lambda i: (jnp.maximum(i - 1, 0), off // w))
    small = lambda n: pl.BlockSpec((1, 2 * n), lambda i: (0, 0))
    part = pl.BlockSpec((8, 128), lambda i: (i, 0))
    p_shape = jax.ShapeDtypeStruct(((nb + 1) * 8, 128), F32)
    return _pcall(
        body, grid=(nb + 1,),
        in_specs=[cur(BW, OFF_QB), cur(KVW, OFF_KB), prev(KVW, OFF_KB), cur(KVW, OFF_VB), prev(KVW, OFF_VB),
                  pl.BlockSpec((BLK, BW), lambda i: (jnp.minimum(i, last), 0)), small(B_HD), small(B_HD),
                  pl.BlockSpec(memory_space=pltpu.SMEM)],
        out_specs=[pl.BlockSpec((BLK, BW), lambda i: (i, 0)),
                   pl.BlockSpec((BLK, 2 * KVW), lambda i: (jnp.maximum(i - 1, 0), 0)), part, part, part],
        out_shape=[jax.ShapeDtypeStruct((T + BLK, BW), BF16), jax.ShapeDtypeStruct((T, 2 * KVW), BF16),
                   p_shape, p_shape, p_shape],
        scratch_shapes=[pltpu.VMEM((BLK, KVW), F32), pltpu.VMEM((BLK, KVW), F32)], name="swa_bwd",
        semantics=("arbitrary",), args=[proj, proj, proj, proj, proj, dout, q_gain, k_gain, sinks], job=job)


def _branch_merge(ya_pre, attn, wa_t, wb_t, proj, tm, tn, job=None):
    T = ya_pre.shape[0]

    def body(a_ref, b_ref, wa_ref, wb_ref, ga_ref, gb_ref, ya_ref, yb_ref, mg_ref):
        ya = lax.dot_general(a_ref[...], wa_ref[...], NT, preferred_element_type=F32)
        yb = lax.dot_general(b_ref[...], wb_ref[...], NT, preferred_element_type=F32)
        ya_ref[...] = ya.astype(BF16)
        yb_ref[...] = yb.astype(BF16)
        mg_ref[...] = (_sigmoid(ga_ref[...]) * ya + _sigmoid(gb_ref[...]) * yb).astype(BF16)

    o_spec = pl.BlockSpec((tm, tn), lambda i, j: (i, j))
    o_shape = jax.ShapeDtypeStruct((T, D), BF16)
    return _pcall(
        body, grid=(T // tm, D // tn),
        in_specs=[pl.BlockSpec((tm, AW), lambda i, j: (i, 0)), pl.BlockSpec((tm, BW), lambda i, j: (i, 0)),
                  pl.BlockSpec((tn, AW), lambda i, j: (j, 0)), pl.BlockSpec((tn, BW), lambda i, j: (j, 0)),
                  pl.BlockSpec((tm, tn), lambda i, j: (i, OFF_GTA // tn + j)),
                  pl.BlockSpec((tm, tn), lambda i, j: (i, OFF_GTB // tn + j))],
        out_specs=[o_spec, o_spec, o_spec], out_shape=[o_shape, o_shape, o_shape], scratch_shapes=[], name="branch_merge",
        semantics=("parallel", "parallel"), args=[ya_pre, attn, wa_t, wb_t, proj, proj], job=job)


def _ij(i, j, k):
    return (i, j)


def _local_step(x, tgt, mod, g1, g2, lbl, og, qg, kg, sk, shards, c_arr, update):
    win_s, wa_s, wb_s, wout_s, wmi_s, wmo_s = shards
    T = x.shape[0]
    tm, tr, tt = min(1024, T), min(256, T), min(2048, T)
    tk_t = min(1024, T)
    tn = 512
    sh1, sc1, gt1, sh2, sc2, gt2 = (mod[:, i * D:(i + 1) * D] for i in range(N_MOD))
    nI = T // tm
    blk = (tm, tn)

    (h,), (win_t,) = _rms_mod_fwd("rms1_fwd", x, g1, sc1, sh1, tr, job=_gather_relay_job([win_s], alone=True))

    def epi_store(acc, ex, ou):
        ou[0][...] = acc.astype(ou[0].dtype)

    tm2 = min(2048, T)
    blk2 = (tm2, tn)

    full = lambda s: (0, s.shape[0])
    last = wmi_s.shape[0]
    gather = _gather_relay_job
    (proj,), (wa_t, wb_t, w_out, wmi_part) = _mm(
        "in_proj", "nt", [(h, D)], win_t, T, IN_W, D, tm2, tn, D, [], [((T, IN_W), F32, blk2, _ij)], epi_store,
        job=gather([wa_s, wb_s, wout_s, wmi_s], rows=[full(wa_s), full(wb_s), full(wout_s), (0, MI_CUT)]))
    (ya_pre, st), _ = _hgrn_fwd(proj, lbl, og, tt)
    (attn,), (wmi_t,) = _swa_fwd(proj, qg, kg, sk, job=gather([wmi_s], rows=[(MI_CUT, last)], into=[wmi_part]))
    (ya, yb, merged), _ = _branch_merge(ya_pre, attn, wa_t, wb_t, proj, tm, tn)

    def residual_rows(acc, vecs, bufs, parts):
        gt, gain, sc, sh = (v[...] for v in vecs)
        x_buf, mo_buf, h2_buf = bufs
        rows = min(64, tm)
        for r0 in range(0, tm, rows):
            rs = slice(r0, r0 + rows)
            z = acc[rs, :]
            mo_buf[rs, :] = z.astype(BF16)
            x1v = x_buf[rs, :] + gt * z
            x_buf[rs, :] = x1v
            rstd = lax.rsqrt(jnp.mean(x1v * x1v, axis=-1, keepdims=True) + EPS)
            h2_buf[rs, :] = ((x1v * rstd * gain) * (1.0 + sc) + sh).astype(BF16)

    x1, mo, h2 = _rows_mm("out_proj", [(merged, D)], w_out, T, tm, min(1024, D), [gt1, g2, sc2, sh2],
                          [(F32, x, True), (BF16, None, True), (BF16, None, True)], [], residual_rows)

    def epi_relu2(acc, ex, ou):
        r = jnp.maximum(acc, 0.0)
        ou[0][...] = r.astype(BF16)
        ou[1][...] = (r * r).astype(BF16)

    (r, a), (w_mo,) = _mm("mlp_in", "nt", [(h2, D)], wmi_t, T, HID, D, tm2, tn, D, [],
                          [((T, HID), BF16, blk2, _ij), ((T, HID), BF16, blk2, _ij)], epi_relu2,
                          job=gather([wmo_s]))

    def loss_rows(acc, vecs, bufs, parts):
        gt = vecs[0][...]
        x1_buf, t_buf, dz_buf = bufs
        rows = min(64, tm)
        loss_sum, gate_sum = jnp.zeros((8, D), F32), jnp.zeros((8, D), F32)
        for r0 in range(0, tm, rows):
            rs = slice(r0, r0 + rows)
            z = acc[rs, :]
            e = x1_buf[rs, :] + gt * z - t_buf[rs, :]
            dy = e * (1.0 / D)
            t_buf[rs, :] = dy
            dz_buf[rs, :] = (gt * dy).astype(BF16)
            loss_sum = loss_sum + _fold8(e * e)
            gate_sum = gate_sum + _fold8(dy * z)
        parts[0][...] = loss_sum * (0.5 / D)
        parts[1][...] = gate_sum

    part_rows = ((nI * 8, D), F32, (8, D), lambda i, j, k: (i, 0))
    dy, dz, p_loss, p_gt2 = _rows_mm(
        "mlp_out", [(a, HID)], w_mo, T, tm, 1024, [gt2], [(F32, x1, False), (F32, tgt, True), (BF16, None, True)],
        [part_rows, part_rows], loss_rows)

    def epi_du(acc, ex, ou):
        ou[0][...] = (acc * (2.0 * ex[0][...].astype(F32))).astype(BF16)

    (du,) = _mm("mlp_out_dx", "nt", [(dz, D)], w_mo, T, HID, D, tm2, tn, D, [(r, blk2, _ij)],
                [((T, HID), BF16, blk2, _ij)], epi_du)
    gblk = (1024, 1024)
    gwide = (1024, D)
    pair_sum = lambda nm, g, r1: _pair_sum("pair_sum_" + nm, g, r1, c_arr, _sum_rows(r1.shape[1]))
    (g_mo,) = _mm("mlp_out_dw", "tn", [(a, HID)], dz, HID, D, T, 1024, D, tk_t, [], [((HID, D), BF16, gwide, _ij)], epi_store)
    (dh2,), (r1_mo,) = _mm("mlp_in_dx", "nn", [(du, HID)], wmi_t, T, D, HID, tm, D, 1024, [],
                           [((T, D), F32, (tm, D), _ij)], epi_store, job=_pair_job([g_mo]))
    dx1, p_sh2, p_sc2, p_g2, dmo, p_gt1 = _rms_mod_bwd("rms2_bwd", dh2, x1, g2, sc2, dy, tr, gate=gt1, mo=mo)
    s_mo = pair_sum("mlp_out", g_mo, r1_mo)
    near, far = (1, 2), (3,)
    (g_mi,), (rn_mo,) = _mm("mlp_in_dw", "tn", [(du, HID)], h2, HID, D, T, 1024, D, tk_t, [],
                            [((HID, D), BF16, gwide, _ij)], epi_store, job=_chip_job([s_mo], near))

    def epi_gates(acc, ex, ou):
        ya_ref, yb_ref, ga_ref, gb_ref = ex
        sa, sb = _sigmoid(ga_ref[...]), _sigmoid(gb_ref[...])
        ou[0][...] = (acc * sa).astype(BF16)
        ou[1][...] = (acc * sb).astype(BF16)
        ou[2][...] = (acc * ya_ref[...].astype(F32) * (sa * (1.0 - sa))).astype(BF16)
        ou[3][...] = (acc * yb_ref[...].astype(F32) * (sb * (1.0 - sb))).astype(BF16)

    o_bf = ((T, D), BF16, blk, _ij)
    (dya, dyb, dga, dgb), (rf_mo, r1_mi) = _mm(
        "out_proj_dx", "nt", [(dmo, D)], w_out, T, D, D, tm, tn, D,
        [(ya, blk, _ij), (yb, blk, _ij), (proj, blk, lambda i, j, k: (i, OFF_GTA // tn + j)),
         (proj, blk, lambda i, j, k: (i, OFF_GTB // tn + j))], [o_bf, o_bf, o_bf, o_bf], epi_gates,
        job=_both(_chip_job([s_mo], far), _pair_job([g_mi])))
    s_mi = pair_sum("mlp_in", g_mi, r1_mi)
    (g_out,) = _mm("out_proj_dw", "tn", [(merged, D)], dmo, D, D, T, 1024, 1024, tk_t, [], [((D, D), BF16, gblk, _ij)], epi_store)
    dya_pre, dattn = _twin_mm("branch_dx", "nn", [(dya, wa_t), (dyb, wb_t)], T, AW, D, tm, tn, D, F32)
    g_a, g_b = _twin_mm("branch_dw", "tn", [(dya, ya_pre), (dyb, attn)], D, AW, T, 1024, 1024, tk_t, BF16)
    (dqa, dfa, dia, dgg, p_lb, p_og), (rn_mi, r1_out, r1_a, r1_b) = _hgrn_bwd(
        proj, st, dya_pre, lbl, og, tt, job=_both(_chip_job([s_mi], near), _pair_job([g_out, g_a, g_b])))
    (dqb, dkv, p_qg, p_kg, p_sk), (rf_mi,) = _swa_bwd(proj, dattn, qg, kg, sk, job=_chip_job([s_mi], far))
    s_out, s_a, s_b = pair_sum("out", g_out, r1_out), pair_sum("branch_a", g_a, r1_a), pair_sum("branch_b", g_b, r1_b)
    pieces = [(dqa, AW), (dfa, AW), (dia, AW), (dgg, AW), (dqb, BW), (dkv, 2 * KVW), (dga, D), (dgb, D)]
    (g_in,), (r2_out, r2_a, r2_b) = _pieces_tn("in_proj_dw", pieces, h, 512, job=_chip_job([s_out, s_a, s_b]))
    (r1_in,) = update("w_mlp_in", s_mi, [rn_mi, rf_mi], job=_pair_job([g_in]))
    s_in = pair_sum("in", g_in, r1_in)
    (dx, p_sh1, p_sc1, p_g1), (r2_in,) = _pieces_nn_rms(
        "in_proj_dx", pieces, win_t, x, g1, sc1, dx1, tm, 512, job=_chip_job([s_in]))

    partials = dict(sh1=p_sh1, sc1=p_sc1, gt1=p_gt1, sh2=p_sh2, sc2=p_sc2, gt2=p_gt2, g1=p_g1, g2=p_g2,
                    lb=p_lb, og=p_og, qg=p_qg, kg=p_kg, sk=p_sk, loss=p_loss)
    sums = dict(w_in=(s_in, [r2_in]), w_branch_a=(s_a, [r2_a]), w_branch_b=(s_b, [r2_b]), w_out=(s_out, [r2_out]),
                w_mlp_in=(s_mi, [rn_mi, rf_mi]), w_mlp_out=(s_mo, [rn_mo, rf_mo]))
    return dx, sums, partials


def _exchange_slots(buf, send_sems, recv_sems):
    me = _mesh_pos()
    mine = buf.at[_index(me)]
    sends = []
    for k in range(1, N_DEV):
        cp = pltpu.make_async_remote_copy(src_ref=mine, dst_ref=mine, send_sem=send_sems.at[k - 1],
                                          recv_sem=recv_sems.at[k - 1], device_id=_flip(me, k), device_id_type=MESH)
        cp.start()
        sends.append(cp)
    for k in range(1, N_DEV):
        theirs = buf.at[_index(_flip(me, k))]
        pltpu.make_async_remote_copy(src_ref=theirs, dst_ref=theirs, send_sem=send_sems.at[k - 1],
                                     recv_sem=recv_sems.at[k - 1], device_id=_flip(me, k), device_id_type=MESH).wait_recv()
    for cp in sends:
        cp.wait_send()


ADA_W = N_MOD * D // N_DEV


def _ada_mod(c, w_ada, b_shard):
    def body(c_ref, w_ref, b_ref, mod_ref, sc_ref, cbuf, mbuf, s1, r1, s2, r2):
        me = _index(_mesh_pos())
        cbuf[me] = c_ref[...]
        _exchange_slots(cbuf, s1, r1)
        row = lax.broadcasted_iota(jnp.int32, (N_DEV, D), 0)
        call = jnp.zeros((N_DEV, D), F32)
        for d in range(N_DEV):
            call = jnp.where(row == d, cbuf[d], call)
        sc = call * _sigmoid(call)
        sc_ref[...] = sc
        mbuf[me] = _dot(sc, w_ref[...]) + b_ref[...]
        _exchange_slots(mbuf, s2, r2)
        for s in range(N_DEV):
            mod_ref[:, s * ADA_W:(s + 1) * ADA_W] = mbuf[s, pl.ds(me, 1), :]

    return pl.pallas_call(
        body, in_specs=[_VMEM, _VMEM, _VMEM], out_specs=[_VMEM, _VMEM],
        out_shape=[jax.ShapeDtypeStruct((1, N_MOD * D), F32), jax.ShapeDtypeStruct((N_DEV, D), F32)],
        scratch_shapes=[pltpu.VMEM((N_DEV, 1, D), F32), pltpu.VMEM((N_DEV, N_DEV, ADA_W), F32),
                        _SEMS(N_DEV - 1), _SEMS(N_DEV - 1), _SEMS(N_DEV - 1), _SEMS(N_DEV - 1)],
        name="ada_mod", compiler_params=pltpu.CompilerParams(vmem_limit_bytes=VMEM_LIMIT),
    )(c, w_ada, b_shard)


SMALL_SEGS = (("b_ada", N_MOD * D), ("norm1_gain", D), ("norm2_gain", D), ("lb0", AW), ("lb1", AW),
              ("hgrn_o_gain", AW), ("q_norm_gain", 128), ("k_norm_gain", 128), ("sinks", 128))
SMALL_W = sum(w for _, w in SMALL_SEGS)
X_SEGS = (("sh1", D), ("sc1", D), ("gt1", D), ("sh2", D), ("sc2", D), ("gt2", D), ("g1", D), ("g2", D),
          ("lb", AW), ("og", AW), ("qg", 128), ("kg", 128), ("sk", 128), ("loss", 128))
X_W = sum(w for _, w in X_SEGS)


def _offsets(segs):
    out, o = {}, 0
    for name, w in segs:
        out[name] = (o, w)
        o += w
    return out


def _small_reduce(parts, lb_logits):
    xo, so = _offsets(X_SEGS), _offsets(SMALL_SEGS)
    names = [nm for nm, _ in X_SEGS]

    def body(*refs):
        p_refs = dict(zip(names, refs[:len(names)]))
        lbl_ref, allx, gs_ref, loss_ref, send_sems, recv_sems = refs[len(names):]
        me = _index(_mesh_pos())
        for nm, (o, w) in xo.items():
            if nm == "loss":
                allx[me, :, o:o + w] = jnp.broadcast_to(jnp.sum(p_refs[nm][...]), (1, w))
            else:
                allx[me, :, o:o + w] = jnp.sum(p_refs[nm][...], axis=0, keepdims=True)
        _exchange_slots(allx, send_sems, recv_sems)
        tot = allx[0]
        for d in range(1, N_DEV):
            tot = tot + allx[d]
        seg = lambda nm: tot[:, xo[nm][0]:xo[nm][0] + xo[nm][1]]

        def put(nm, v):
            gs_ref[:, so[nm][0]:so[nm][0] + so[nm][1]] = v

        put("b_ada", tot[:, 0:N_MOD * D])
        put("norm1_gain", seg("g1"))
        put("norm2_gain", seg("g2"))
        lbl = lbl_ref[...]
        lb = _sigmoid(lbl[0:1, :] - lbl[1:2, :])
        dl0 = seg("lb") * lb * (1.0 - lb)
        put("lb0", dl0)
        put("lb1", -dl0)
        put("hgrn_o_gain", seg("og"))
        put("q_norm_gain", seg("qg"))
        put("k_norm_gain", seg("kg"))
        put("sinks", seg("sk"))
        loss_ref[...] = seg("loss")

    return pl.pallas_call(
        body, in_specs=[_VMEM] * (len(names) + 1), out_specs=[_VMEM, _VMEM, _VMEM],
        out_shape=[jax.ShapeDtypeStruct((N_DEV, 1, X_W), F32), jax.ShapeDtypeStruct((1, SMALL_W), F32),
                   jax.ShapeDtypeStruct((1, 128), F32)],
        scratch_shapes=[_SEMS(N_DEV - 1), _SEMS(N_DEV - 1)], name="small_reduce",
        compiler_params=pltpu.CompilerParams(vmem_limit_bytes=VMEM_LIMIT),
    )(*[parts[nm] for nm in names], lb_logits)


def _adamw_math(w, g, m, v):
    m = B1 * m + (1.0 - B1) * g
    v = B2 * v + (1.0 - B2) * (g * g)
    m_hat = m / (1.0 - B1 ** STEP)
    v_hat = v / (1.0 - B2 ** STEP)
    return -LR * (m_hat / (jnp.sqrt(v_hat) + ADAM_EPS) + WD * w), m, v


def _sum_rows(rs):
    return 256 if rs % 256 == 0 else rs // 2


def _pair_sum(name, g, recv, c_arr, tr):
    _, rs, cols = recv.shape
    blk = (1, tr, cols)

    def body(c_ref, g_ref, r_ref, o_ref):
        o_ref[...] = (g_ref[...].astype(F32) + r_ref[...].astype(F32)).astype(BF16)

    grid_spec = pltpu.PrefetchScalarGridSpec(
        num_scalar_prefetch=1, grid=(4, rs // tr),
        in_specs=[pl.BlockSpec(blk, lambda q, i, c: (2 * q + c[0], i, 0)), pl.BlockSpec(blk, lambda q, i, c: (q, i, 0))],
        out_specs=pl.BlockSpec(blk, lambda q, i, c: (q, i, 0)))
    return pl.pallas_call(body, grid_spec=grid_spec, out_shape=jax.ShapeDtypeStruct((4, rs, cols), BF16), name=name,
                          compiler_params=_params(("parallel", "parallel")))(c_arr, g.reshape(N_DEV, rs, cols), recv)


def _sum_adamw(name, sums, recvs, q_arr, w, m, v, transposed, tile, job=None):
    rows, cols = w.shape
    nR = len(recvs)

    def body(q_ref, s_ref, *refs):
        r_refs = refs[:nR]
        w_ref, m_ref, v_ref, g_ref, d_ref, nm_ref, nv_ref = refs[nR:]
        g = s_ref[0].astype(F32)
        for r_ref in r_refs:
            for slot in range(r_ref.shape[0]):
                g = g + r_ref[slot].astype(F32)
        g = g.T if transposed else g
        g_ref[...] = g
        d_ref[...], nm_ref[...], nv_ref[...] = _adamw_math(w_ref[...], g, m_ref[...], v_ref[...])

    if transposed:
        slab = lambda n, first: pl.BlockSpec((n, cols, tile), lambda i, q: (first(q), 0, i))
    else:
        slab = lambda n, first: pl.BlockSpec((n, tile, cols), lambda i, q: (first(q), i, 0))
    spec = pl.BlockSpec((tile, cols), lambda i, q: (i, 0))
    shape = jax.ShapeDtypeStruct((rows, cols), F32)
    res, job_res = _pcall(
        body, grid=(rows // tile,),
        in_specs=[slab(1, lambda q: q[0])] + [slab(r.shape[0], lambda q: 0) for r in recvs] + [spec] * 3,
        out_specs=[spec] * 4, out_shape=[shape] * 4, scratch_shapes=[], name=name, semantics=("parallel",),
        args=[sums, *recvs, w, m, v], job=job, prefetch=[q_arr])
    return res if job is None else (res, job_res)


def _adamw(name, w, g, m, v, tr):
    rows, cols = w.shape

    def body(w_ref, g_ref, m_ref, v_ref, d_ref, nm_ref, nv_ref):
        d_ref[...], nm_ref[...], nv_ref[...] = _adamw_math(w_ref[...], g_ref[...], m_ref[...], v_ref[...])

    spec = pl.BlockSpec((tr, cols), lambda i: (i, 0))
    shape = jax.ShapeDtypeStruct((rows, cols), F32)
    return pl.pallas_call(
        body, grid=(rows // tr,), in_specs=[spec] * 4, out_specs=[spec] * 3, out_shape=[shape] * 3, name=name,
        compiler_params=_params(("parallel",)),
    )(w, g, m, v)


def _ada_update(sc_t, dmod_cols, w, m, v, tr):
    rows, cols = w.shape

    def body(s_ref, d_ref, w_ref, m_ref, v_ref, g_ref, dl_ref, nm_ref, nv_ref):
        g = jnp.dot(s_ref[...], d_ref[...], precision=lax.Precision.HIGHEST, preferred_element_type=F32)
        g_ref[...] = g
        dl_ref[...], nm_ref[...], nv_ref[...] = _adamw_math(w_ref[...], g, m_ref[...], v_ref[...])

    spec = pl.BlockSpec((tr, cols), lambda i: (i, 0))
    shape = jax.ShapeDtypeStruct((rows, cols), F32)
    return pl.pallas_call(
        body, grid=(rows // tr,),
        in_specs=[pl.BlockSpec((tr, N_DEV), lambda i: (i, 0)), pl.BlockSpec((N_DEV, cols), lambda i: (0, 0)), spec, spec, spec],
        out_specs=[spec] * 4, out_shape=[shape] * 4, name="ada_update", compiler_params=_params(("parallel",)),
    )(sc_t, dmod_cols, w, m, v)


BIG = ("w_in", "w_branch_a", "w_branch_b", "w_out", "w_mlp_in", "w_mlp_out")
COLUMN_SHARDED = ("w_in", "w_branch_a", "w_branch_b", "w_mlp_in")
AS_TRANSPOSE = ("w_in",)
WEIGHTS = ("w_ada", "b_ada", "norm1_gain", "w_in", "lb_logits", "hgrn_o_gain", "q_norm_gain", "k_norm_gain", "sinks",
           "w_branch_a", "w_branch_b", "w_out", "norm2_gain", "w_mlp_in", "w_mlp_out")


def _to_bf16(name, w, transposed, tile=256):
    rows, cols = w.shape

    def body(w_ref, o_ref):
        v = w_ref[...]
        o_ref[...] = (v.T if transposed else v).astype(BF16)

    out_spec = pl.BlockSpec((cols, tile), lambda i: (0, i)) if transposed else pl.BlockSpec((tile, cols), lambda i: (i, 0))
    return pl.pallas_call(
        body, grid=(rows // tile,), in_specs=[pl.BlockSpec((tile, cols), lambda i: (i, 0))], out_specs=out_spec,
        out_shape=jax.ShapeDtypeStruct((cols, rows) if transposed else (rows, cols), BF16), name=name,
        compiler_params=_params(("parallel",)))(w)


def _pack_small(p):
    lb = p["lb_logits"]
    src = dict(p, lb0=lb[0:1], lb1=lb[1:2])
    return jnp.concatenate([jnp.pad(src[nm], ((0, 0), (0, w - src[nm].shape[1]))) for nm, w in SMALL_SEGS], axis=1)


def _unpack_small(vec, shapes):
    so = _offsets(SMALL_SEGS)
    out = {}
    for nm, shp in shapes.items():
        if nm == "lb_logits":
            o = so["lb0"][0]
            out[nm] = vec[0, o:o + 2 * AW].reshape(2, AW)
        else:
            o = so[nm][0]
            out[nm] = vec[:, o:o + shp[1]]
    return out


def kernel(x, c, w_ada, b_ada, norm1_gain, w_in, lb_logits, hgrn_o_gain, q_norm_gain, k_norm_gain, sinks, w_branch_a, w_branch_b, w_out, norm2_gain, w_mlp_in, w_mlp_out, loss_target, m_w_ada, m_b_ada, m_norm1_gain, m_w_in, m_lb_logits, m_hgrn_o_gain, m_q_norm_gain, m_k_norm_gain, m_sinks, m_w_branch_a, m_w_branch_b, m_w_out, m_norm2_gain, m_w_mlp_in, m_w_mlp_out, v_w_ada, v_b_ada, v_norm1_gain, v_w_in, v_lb_logits, v_hgrn_o_gain, v_q_norm_gain, v_k_norm_gain, v_sinks, v_w_branch_a, v_w_branch_b, v_w_out, v_norm2_gain, v_w_mlp_in, v_w_mlp_out):
    w = dict(w_ada=w_ada, b_ada=b_ada, norm1_gain=norm1_gain, w_in=w_in, lb_logits=lb_logits, hgrn_o_gain=hgrn_o_gain,
             q_norm_gain=q_norm_gain, k_norm_gain=k_norm_gain, sinks=sinks, w_branch_a=w_branch_a, w_branch_b=w_branch_b,
             w_out=w_out, norm2_gain=norm2_gain, w_mlp_in=w_mlp_in, w_mlp_out=w_mlp_out)
    m = dict(w_ada=m_w_ada, b_ada=m_b_ada, norm1_gain=m_norm1_gain, w_in=m_w_in, lb_logits=m_lb_logits,
             hgrn_o_gain=m_hgrn_o_gain, q_norm_gain=m_q_norm_gain, k_norm_gain=m_k_norm_gain, sinks=m_sinks,
             w_branch_a=m_w_branch_a, w_branch_b=m_w_branch_b, w_out=m_w_out, norm2_gain=m_norm2_gain,
             w_mlp_in=m_w_mlp_in, w_mlp_out=m_w_mlp_out)
    v = dict(w_ada=v_w_ada, b_ada=v_b_ada, norm1_gain=v_norm1_gain, w_in=v_w_in, lb_logits=v_lb_logits,
             hgrn_o_gain=v_hgrn_o_gain, q_norm_gain=v_q_norm_gain, k_norm_gain=v_k_norm_gain, sinks=v_sinks,
             w_branch_a=v_w_branch_a, w_branch_b=v_w_branch_b, w_out=v_w_out, norm2_gain=v_norm2_gain,
             w_mlp_in=v_w_mlp_in, w_mlp_out=v_w_mlp_out)
    for d in (w, m, v):
        for nm in ("w_ada",) + BIG:
            d[nm] = d[nm][0]
    px, py, pc = _mesh_pos()
    me = _index((px, py, pc))
    c_arr = jnp.reshape(pc, (1,)).astype(jnp.int32)
    q_arr = jnp.reshape(2 * px + py, (1,)).astype(jnp.int32)

    shards = [_to_bf16("shard_" + nm, w[nm].T, False, w[nm].shape[1] // 4) if nm in AS_TRANSPOSE else
              _to_bf16("shard_" + nm, w[nm], nm in COLUMN_SHARDED) for nm in BIG]
    b_shard = lax.dynamic_slice(b_ada, (0, me * ADA_W), (1, ADA_W))
    mod, sc_all = _ada_mod(c, w["w_ada"], b_shard)

    grad, delta, new_m, new_v = {}, {}, {}, {}

    def update(nm, s, recvs, job=None):
        if nm in AS_TRANSPOSE:
            res = _sum_adamw("adamw_" + nm, s, recvs, q_arr, w[nm].T, m[nm].T, v[nm].T, False, w[nm].shape[1] // 4, job=job)
        else:
            res = _sum_adamw("adamw_" + nm, s, recvs, q_arr, w[nm], m[nm], v[nm], nm in COLUMN_SHARDED, 128, job=job)
        res, job_res = res if job is not None else (res, [])
        res = [t.T for t in res] if nm in AS_TRANSPOSE else res
        grad[nm], delta[nm], new_m[nm], new_v[nm] = res
        return job_res

    dx, sums, parts = _local_step(x[0], loss_target[0], mod, norm1_gain, norm2_gain, lb_logits, hgrn_o_gain,
                                  q_norm_gain, k_norm_gain, sinks, shards, c_arr, update)
    for nm in BIG:
        if nm not in grad:
            update(nm, *sums[nm])

    allx, g_small, loss = _small_reduce(parts, lb_logits)

    dmod_cols = lax.dynamic_slice(allx[:, 0, :], (0, me * ADA_W), (N_DEV, ADA_W))
    grad["w_ada"], delta["w_ada"], new_m["w_ada"], new_v["w_ada"] = _ada_update(
        sc_all.T, dmod_cols, w["w_ada"], m["w_ada"], v["w_ada"], 256)

    small_names = [nm for nm in WEIGHTS if nm not in BIG and nm != "w_ada"]
    shapes = {nm: w[nm].shape for nm in small_names}
    ds, ms, vs = _adamw("adamw_small", _pack_small(w), g_small, _pack_small(m), _pack_small(v), 1)
    for dst, vec in ((grad, g_small), (delta, ds), (new_m, ms), (new_v, vs)):
        dst.update(_unpack_small(vec, shapes))

    def full(d, nm):
        return d[nm][None] if nm in BIG or nm == "w_ada" else d[nm]

    return (loss[0, 0], dx[None], *[full(grad, nm) for nm in WEIGHTS], *[full(delta, nm) for nm in WEIGHTS],
            *[full(new_m, nm) for nm in WEIGHTS], *[full(new_v, nm) for nm in WEIGHTS])
```

```python
import functools

import jax
import jax.numpy as jnp
from jax import lax
from jax.experimental import pallas as pl
from jax.experimental.pallas import tpu as pltpu

F32 = jnp.float32
BF16 = jnp.bfloat16
MESH = pl.DeviceIdType.MESH

N_DEV = 8
D = 2048
A_HEADS, A_HD, CHUNK = 8, 128, 64
AW = A_HEADS * A_HD
Q_HEADS, KV_HEADS, GROUP, B_HD, BLK = 16, 4, 4, 64, 128
BW = Q_HEADS * B_HD
KVW = KV_HEADS * B_HD
HID = 4 * D
IN_W = 4 * AW + BW + 2 * KVW + 2 * D
OFF_QA, OFF_FA, OFF_IA, OFF_GA = 0, AW, 2 * AW, 3 * AW
OFF_QB = 4 * AW
OFF_KB = OFF_QB + BW
OFF_VB = OFF_KB + KVW
OFF_GTA = OFF_VB + KVW
OFF_GTB = OFF_GTA + D
N_MOD = 6
EPS = 1e-6
LR, B1, B2, ADAM_EPS, WD, STEP = 1e-3, 0.9, 0.999, 1e-8, 0.01, 10
NEG = -1e30

VMEM_LIMIT = 56 * 1024 * 1024
MI_CUT = 544

NN = (((1,), (0,)), ((), ()))
NT = (((1,), (1,)), ((), ()))
TN = (((0,), (0,)), ((), ()))
BNN = (((2,), (1,)), ((0,), (0,)))
BNT = (((2,), (2,)), ((0,), (0,)))
BTN = (((1,), (1,)), ((0,), (0,)))


def _dot(a, b, dims=NN):
    return lax.dot_general(a.astype(BF16), b.astype(BF16), dims, preferred_element_type=F32)


def _params(sem):
    return pltpu.CompilerParams(dimension_semantics=sem, vmem_limit_bytes=VMEM_LIMIT)


def _sigmoid(x):
    return jax.nn.sigmoid(x)


def _fold8(v):
    r, n = v.shape
    return jnp.sum(v.reshape(r // 8, 8, n), axis=0)


_VMEM = pl.BlockSpec(memory_space=pltpu.VMEM)
_ANY = pl.BlockSpec(memory_space=pl.ANY)
_SEMS = lambda n: pltpu.SemaphoreType.DMA((n,))


def _mesh_pos():
    return lax.axis_index("x"), lax.axis_index("y"), lax.axis_index("c")


def _flip(pos, k):
    return tuple(1 - p if (k >> s) & 1 else p for p, s in zip(pos, (2, 1, 0)))


def _index(pos):
    return 4 * pos[0] + 2 * pos[1] + pos[2]


class _Job:
    def __init__(self, ins, out_shape, sems, start, finish, aliases=None, middle=None):
        self.ins, self.out_shape, self.sems, self.start, self.finish = list(ins), list(out_shape), list(sems), start, finish
        self.aliases = dict(aliases or {})
        self.middle = middle


def _both(j1, j2):
    assert j1.middle is None and j2.middle is None
    n_in, n_out, n_sem = len(j1.ins), len(j1.out_shape), len(j1.sems)
    aliases = dict(j1.aliases, **{n_in + i: n_out + o for i, o in j2.aliases.items()})
    first = lambda ins, outs, sems: (ins[:n_in], outs[:n_out], sems[:n_sem])
    second = lambda ins, outs, sems: (ins[n_in:], outs[n_out:], sems[n_sem:])

    def start(*refs):
        j1.start(*first(*refs))
        j2.start(*second(*refs))

    def finish(*refs):
        j1.finish(*first(*refs))
        j2.finish(*second(*refs))

    return _Job(j1.ins + j2.ins, j1.out_shape + j2.out_shape, j1.sems + j2.sems, start, finish, aliases)


def _pcall(body, *, grid, in_specs, out_specs, out_shape, scratch_shapes, name, semantics, args, job=None, prefetch=()):
    n_pre = len(prefetch)

    def call(fn, in_specs_, out_specs_, out_shape_, scratch_, sem, operands, aliases):
        if n_pre:
            spec = pltpu.PrefetchScalarGridSpec(num_scalar_prefetch=n_pre, grid=grid, in_specs=in_specs_,
                                                out_specs=out_specs_, scratch_shapes=scratch_)
            return pl.pallas_call(fn, grid_spec=spec, out_shape=out_shape_, name=name, input_output_aliases=aliases,
                                  compiler_params=_params(sem))(*prefetch, *operands)
        return pl.pallas_call(fn, grid=grid, in_specs=in_specs_, out_specs=out_specs_, out_shape=out_shape_,
                              scratch_shapes=scratch_, name=name, input_output_aliases=aliases,
                              compiler_params=_params(sem))(*operands)

    if job is None:
        return list(call(body, in_specs, out_specs, out_shape, scratch_shapes, semantics, args, {})), []
    n_in, n_out, n_scr = len(in_specs), len(out_specs), len(scratch_shapes)
    j_in, j_out = len(job.ins), len(job.out_shape)
    steps = tuple(grid)

    def carrier(*refs):
        pre, refs = refs[:n_pre], refs[n_pre:]
        o = 0
        main_in, o = refs[o:o + n_in], o + n_in
        job_in, o = refs[o:o + j_in], o + j_in
        main_out, o = refs[o:o + n_out], o + n_out
        job_out, o = refs[o:o + j_out], o + j_out
        main_scr, job_sems = refs[o:o + n_scr], refs[o + n_scr:]
        ids = [pl.program_id(a) for a in range(len(steps))]
        first = functools.reduce(lambda p, q: p & q, [i == 0 for i in ids])
        last = functools.reduce(lambda p, q: p & q, [i == s - 1 for i, s in zip(ids, steps)])

        @pl.when(first)
        def _():
            job.start(job_in, job_out, job_sems)

        if job.middle is not None:
            flat, total = 0, 1
            for i, s in zip(ids, steps):
                flat, total = flat * s + i, total * s

            @pl.when(flat == total * 3 // 5)
            def _():
                job.middle(job_in, job_out, job_sems)

        body(*pre, *main_in, *main_out, *main_scr)

        @pl.when(last)
        def _():
            job.finish(job_in, job_out, job_sems)

    outs = call(carrier, list(in_specs) + [_ANY] * j_in, list(out_specs) + [_ANY] * j_out,
                list(out_shape) + job.out_shape, list(scratch_shapes) + job.sems, ("arbitrary",) * len(steps),
                list(args) + job.ins, {n_pre + n_in + i: n_out + o for i, o in job.aliases.items()})
    return list(outs[:n_out]), list(outs[n_out:])


def _gather_relay_job(shards, rows=None, into=None, alone=False):
    n = len(shards)
    rows = rows or [(0, s.shape[0]) for s in shards]
    into = into or [None] * n
    olds, aliases = [], {}
    for a, buf in enumerate(into):
        if buf is not None:
            aliases[n + len(olds)] = a
            olds.append(buf)

    def tools(ins, outs, sems):
        send_sems, recv_sems, local_sems = sems
        x, y, c = _mesh_pos()
        q = 2 * x + y
        chip_at = lambda rel: (1 - x if rel & 2 else x, 1 - y if rel & 1 else y)

        def part(a, chip, core):
            rs, (r0, r1) = shards[a].shape[0], rows[a]
            return outs[a].at[pl.ds((2 * chip + core) * rs + r0, r1 - r0), :]

        own = lambda a: ins[a].at[pl.ds(rows[a][0], rows[a][1] - rows[a][0]), :]

        def copy(a, slot, chip, core, to, src=None):
            blk = part(a, chip, core)
            return pltpu.make_async_remote_copy(src_ref=blk if src is None else src, dst_ref=blk,
                                                send_sem=send_sems.at[7 * a + slot], recv_sem=recv_sems.at[7 * a + slot],
                                                device_id=to, device_id_type=MESH)

        mine = [pltpu.make_async_copy(own(a), part(a, q, c), local_sems.at[a]) for a in range(n)]
        first = [copy(a, slot, q, c, (x, y, 1 - c) if slot == 0 else (*chip_at(slot), c), src=own(a))
                 for a in range(n) for slot in (0, 1, 2)]
        return x, y, c, q, chip_at, copy, mine, first

    def start(ins, outs, sems):
        *_, mine, first = tools(ins, outs, sems)
        for cp in mine + first:
            cp.start()

    def middle(ins, outs, sems):
        x, y, c, q, chip_at, copy, _, _ = tools(ins, outs, sems)
        me, sib = (x, y, c), (x, y, 1 - c)

        def relay(src, dst):
            for a in range(n):
                copy(a, src, q ^ src, c, me).wait_recv()
                copy(a, 3, q ^ src, c, (*chip_at(dst), c)).start()
                copy(a, 3 + src, q ^ src, c, sib).start()
            for a in range(n):
                copy(a, dst, q ^ dst, c, me).wait_recv()
                copy(a, 3 + dst, q ^ dst, c, sib).start()

        pl.when(c == 1)(lambda: relay(1, 2))
        pl.when(c == 0)(lambda: relay(2, 1))

    def finish(ins, outs, sems):
        if alone:
            middle(ins, outs, sems)
        x, y, c, q, chip_at, copy, mine, first = tools(ins, outs, sems)
        me, sib = (x, y, c), (x, y, 1 - c)
        for a in range(n):
            copy(a, 3, q ^ 3, c, me).wait_recv()
            copy(a, 6, q ^ 3, c, sib).start()
        for a in range(n):
            copy(a, 0, q, 1 - c, me).wait_recv()
            for rel in (1, 2, 3):
                copy(a, 3 + rel, q ^ rel, 1 - c, me).wait_recv()
        for a in range(n):
            for slot in range(3, 7):
                copy(a, slot, q, c, sib).wait_send()
        for cp in first:
            cp.wait_send()
        for cp in mine:
            cp.wait()

    return _Job(list(shards) + olds, [jax.ShapeDtypeStruct((N_DEV * s.shape[0], s.shape[1]), s.dtype) for s in shards],
                [_SEMS(7 * n), _SEMS(7 * n), _SEMS(n)], start, finish, aliases, middle=None if alone else middle)


def _pair_job(grads):
    n = len(grads)

    def copies(ins, outs, sems):
        send_sems, recv_sems = sems
        x, y, c = _mesh_pos()
        out = []
        for a in range(n):
            rs = grads[a].shape[0] // N_DEV
            for q in range(4):
                blk = ins[a].at[pl.ds((2 * q + 1 - c) * rs, rs), :]
                out.append(pltpu.make_async_remote_copy(
                    src_ref=blk, dst_ref=outs[a].at[q], send_sem=send_sems.at[4 * a + q], recv_sem=recv_sems.at[4 * a + q],
                    device_id=(x, y, 1 - c), device_id_type=MESH))
        return out

    def start(ins, outs, sems):
        for cp in copies(ins, outs, sems):
            cp.start()

    def finish(ins, outs, sems):
        for cp in copies(ins, outs, sems):
            cp.wait()

    return _Job(grads, [jax.ShapeDtypeStruct((4, g.shape[0] // N_DEV, g.shape[1]), g.dtype) for g in grads],
                [_SEMS(4 * n), _SEMS(4 * n)], start, finish)


def _chip_job(sums, rels=(1, 2, 3), rows=None, into=None):
    n, nr = len(sums), len(rels)
    r0, r1 = rows or (0, sums[0].shape[1])
    olds = list(into or [])
    aliases = {n + a: a for a in range(len(olds))}

    def copies(ins, outs, sems):
        send_sems, recv_sems = sems
        x, y, c = _mesh_pos()
        out = []
        for a in range(n):
            for slot, r in enumerate(rels):
                px, py = (1 - x if r & 2 else x), (1 - y if r & 1 else y)
                out.append(pltpu.make_async_remote_copy(
                    src_ref=ins[a].at[2 * px + py, pl.ds(r0, r1 - r0), :], dst_ref=outs[a].at[slot, pl.ds(r0, r1 - r0), :],
                    send_sem=send_sems.at[nr * a + slot], recv_sem=recv_sems.at[nr * a + slot],
                    device_id=(px, py, c), device_id_type=MESH))
        return out

    def start(ins, outs, sems):
        for cp in copies(ins, outs, sems):
            cp.start()

    def finish(ins, outs, sems):
        for cp in copies(ins, outs, sems):
            cp.wait()

    return _Job(list(sums) + olds, [jax.ShapeDtypeStruct((nr,) + s.shape[1:], s.dtype) for s in sums],
                [_SEMS(nr * n), _SEMS(nr * n)], start, finish, aliases)


def _mm(name, form, a_list, b, M, N, K, tm, tn, tk, extras, outs, epi, job=None):
    nI, nJ, nK = M // tm, N // tn, K // tk
    assert nI * tm == M and nJ * tn == N and nK * tk == K
    dims = {"nn": NN, "nt": NT, "tn": TN}[form]
    b_list = b if isinstance(b, list) else [(b, {"nn": N, "nt": K, "tn": N}[form])]
    nA, nB = len(a_list), len(b_list)
    assert nA == 1 or nB == 1
    assert nB == 1 or form in ("nn", "nt")
    AXIS = {"i": 0, "j": 1, "k": 2}
    a_axis, a_tile = ("i", tm) if form == "tn" else ("k", tk)
    b_axis, b_tile = ("k", tk) if form == "nt" else ("j", tn)

    def cut(pieces, tile, total):
        starts, s = [], 0
        for _, w in pieces:
            assert w % tile == 0
            starts.append(s // tile)
            s += w
        assert s == total
        return starts, [w // tile for _, w in pieces]

    a_st, a_cn = cut(a_list, a_tile, M if form == "tn" else K)
    b_st, b_cn = cut(b_list, b_tile, K if form == "nt" else N)

    def inside(idx, st, cn):
        return (idx >= st) & (idx < st + cn)

    def a_spec(p):
        st, cn = a_st[p], a_cn[p]
        if form == "tn":
            return pl.BlockSpec((tk, tm), lambda i, j, k: (jnp.where(inside(i, st, cn), k, 0), jnp.clip(i - st, 0, cn - 1)))
        return pl.BlockSpec((tm, tk), lambda i, j, k: (i, jnp.clip(k - st, 0, cn - 1)))

    def b_spec(p):
        st, cn = b_st[p], b_cn[p]
        if form == "nt":
            return pl.BlockSpec((tn, tk), lambda i, j, k: (j, jnp.clip(k - st, 0, cn - 1)))
        if nB == 1:
            return pl.BlockSpec((tk, tn), lambda i, j, k: (k, j))
        return pl.BlockSpec((tk, tn), lambda i, j, k: (jnp.where(inside(j, st, cn), k, 0), jnp.clip(j - st, 0, cn - 1)))

    in_specs = ([a_spec(p) for p in range(nA)] + [b_spec(p) for p in range(nB)]
                + [pl.BlockSpec(bs, im) for _, bs, im in extras])
    out_shape = [jax.ShapeDtypeStruct(s_, d_) for s_, d_, _, _ in outs]
    out_specs = [pl.BlockSpec(bs, im) for _, _, bs, im in outs]
    nE, nO = len(extras), len(outs)
    single = nA == 1 and nB == 1

    def body(*refs):
        a_refs, b_refs = refs[:nA], refs[nA:nA + nB]
        ex, ou = refs[nA + nB:nA + nB + nE], refs[nA + nB + nE:nA + nB + nE + nO]
        ids = [pl.program_id(a) for a in range(3)]

        def partial_of(p, q):
            return lax.dot_general(a_refs[p][...], b_refs[q][...], dims, preferred_element_type=F32)

        if nK == 1 and single:
            epi(partial_of(0, 0), ex, ou)
            return
        acc = refs[-1]
        k = ids[2]
        for p in range(nA):
            for q in range(nB):
                def first(p=p, q=q):
                    acc[...] = partial_of(p, q)

                def later(p=p, q=q):
                    acc[...] += partial_of(p, q)

                here = None
                if nA > 1:
                    here = inside(ids[AXIS[a_axis]], a_st[p], a_cn[p])
                if nB > 1:
                    here = inside(ids[AXIS[b_axis]], b_st[q], b_cn[q])
                pl.when(k == 0 if here is None else here & (k == 0))(first)
                pl.when(k > 0 if here is None else here & (k > 0))(later)

        @pl.when(k == nK - 1)
        def _():
            epi(acc[...], ex, ou)

    scratch = [] if (nK == 1 and single) else [pltpu.VMEM((tm, tn), F32)]
    res, job_res = _pcall(
        body, grid=(nI, nJ, nK), in_specs=in_specs, out_specs=out_specs, out_shape=out_shape, scratch_shapes=scratch,
        name=name, semantics=("parallel", "parallel", "arbitrary"),
        args=[a for a, _ in a_list] + [p for p, _ in b_list] + [e for e, _, _ in extras], job=job)
    return res if job is None else (res, job_res)


def _twin_mm(name, form, pairs, M, N, K, tm, tn, tk, out_dtype):
    nI, nJ, nK = M // tm, N // tn, K // tk
    dims = {"nn": NN, "tn": TN}[form]
    a_spec = (pl.BlockSpec((tm, tk), lambda i, j, k: (i, k)) if form == "nn" else pl.BlockSpec((tk, tm), lambda i, j, k: (k, i)))
    b_spec = pl.BlockSpec((tk, tn), lambda i, j, k: (k, j))
    o_spec = pl.BlockSpec((tm, tn), lambda i, j, k: (i, j))

    def body(a1, b1, a2, b2, o1, o2, *accs):
        k = pl.program_id(2)
        for a_ref, b_ref, o_ref, acc in ((a1, b1, o1, accs[0] if accs else None), (a2, b2, o2, accs[1] if accs else None)):
            part = lax.dot_general(a_ref[...], b_ref[...], dims, preferred_element_type=F32)
            if nK == 1:
                o_ref[...] = part.astype(out_dtype)
                continue

            @pl.when(k == 0)
            def _(acc=acc, part=part):
                acc[...] = part

            @pl.when(k > 0)
            def _(acc=acc, part=part):
                acc[...] += part

            @pl.when(k == nK - 1)
            def _(acc=acc, o_ref=o_ref):
                o_ref[...] = acc[...].astype(out_dtype)

    (a1, b1), (a2, b2) = pairs
    shape = jax.ShapeDtypeStruct((M, N), out_dtype)
    return pl.pallas_call(
        body, grid=(nI, nJ, nK), in_specs=[a_spec, b_spec, a_spec, b_spec], out_specs=[o_spec, o_spec],
        out_shape=[shape, shape], scratch_shapes=[] if nK == 1 else [pltpu.VMEM((tm, tn), F32)] * 2, name=name,
        compiler_params=_params(("parallel", "parallel", "arbitrary")))(a1, b1, a2, b2)


def _piece_tiles(pieces, tile):
    starts, s = [], 0
    for _, w in pieces:
        assert w % tile == 0
        starts.append(s // tile)
        s += w
    return starts, [w // tile for _, w in pieces], s


def _pieces_tn(name, pieces, b, tile, job=None):
    T, N = b.shape
    st, cn, M = _piece_tiles(pieces, tile)
    nP, nI = len(pieces), M // tile

    def body(*refs):
        p_refs, b_hbm, o_ref = refs[:nP], refs[nP], refs[nP + 1]
        bbuf, abuf, bsem, asem = refs[nP + 2:]
        i = pl.program_id(0)

        def fetch(step, slot):
            for p in range(nP):
                @pl.when((step >= st[p]) & (step < st[p] + cn[p]))
                def _():
                    col = pl.multiple_of((step - st[p]) * tile, tile)
                    pltpu.make_async_copy(p_refs[p].at[pl.ds(0, T), pl.ds(col, tile)], abuf.at[slot], asem.at[slot]).start()

        @pl.when(i == 0)
        def _():
            whole = pltpu.make_async_copy(b_hbm, bbuf, bsem)
            whole.start()
            fetch(0, 0)
            whole.wait()

        @pl.when(i + 1 < nI)
        def _():
            fetch(i + 1, (i + 1) % 2)

        pltpu.make_async_copy(p_refs[0].at[pl.ds(0, T), pl.ds(0, tile)], abuf.at[i % 2], asem.at[i % 2]).wait()
        o_ref[...] = lax.dot_general(abuf[i % 2], bbuf[...], TN, preferred_element_type=F32).astype(BF16)

    res, job_res = _pcall(
        body, grid=(nI,), in_specs=[_ANY] * (nP + 1), out_specs=[pl.BlockSpec((tile, N), lambda i: (i, 0))],
        out_shape=[jax.ShapeDtypeStruct((M, N), BF16)],
        scratch_shapes=[pltpu.VMEM((T, N), b.dtype), pltpu.VMEM((2, T, tile), b.dtype), pltpu.SemaphoreType.DMA, _SEMS(2)],
        name=name, semantics=("arbitrary",), args=[p for p, _ in pieces] + [b], job=job)
    return res if job is None else (res, job_res)


def _rows_mm(name, pieces, w, T, tm, tk, vecs, bufs, parts, epi, job=None):
    st, cn, K = _piece_tiles(pieces, tk)
    nP, nI, nK = len(pieces), T // tm, K // tk
    part_specs = [pl.BlockSpec(bs, lambda i, k, im=im: im(i, 0, k)) for _, _, bs, im in parts]
    n_vec, nB = len(vecs), len(bufs)
    load_ix = [n for n, (_, src, _) in enumerate(bufs) if src is not None]
    store_ix = [n for n, (_, _, store) in enumerate(bufs) if store]
    n_any_in, n_any_out = len(load_ix), len(store_ix)

    def body(*refs):
        o = nP
        p_refs, w_ref = refs[:nP], refs[o]
        vec_refs = refs[o + 1:o + 1 + n_vec]
        ins = refs[o + 1 + n_vec:o + 1 + n_vec + n_any_in]
        o = o + 1 + n_vec + n_any_in
        hbm_outs, p_outs = refs[o:o + n_any_out], refs[o + n_any_out:o + n_any_out + len(parts)]
        o = o + n_any_out + len(parts)
        acc, abuf = refs[o:o + 2]
        buf_refs = refs[o + 2:o + 2 + nB]
        asem, in_sems, out_sems = refs[-3:]
        i, k = pl.program_id(0), pl.program_id(1)
        g = i * nK + k
        rows_of = lambda ref, ii: ref.at[pl.ds(pl.multiple_of(ii * tm, tm), tm), :]
        bufs_in = [buf_refs[n] for n in load_ix]
        bufs_out = [buf_refs[n] for n in store_ix]

        def fetch(ii, kk, slot):
            for p in range(nP):
                @pl.when((kk >= st[p]) & (kk < st[p] + cn[p]))
                def _():
                    col = pl.multiple_of((kk - st[p]) * tk, tk)
                    src = p_refs[p].at[pl.ds(pl.multiple_of(ii * tm, tm), tm), pl.ds(col, tk)]
                    pltpu.make_async_copy(src, abuf.at[slot], asem.at[slot]).start()

        loads = lambda ii: [pltpu.make_async_copy(rows_of(src, ii), buf, in_sems.at[n])
                            for n, (src, buf) in enumerate(zip(ins, bufs_in))]
        stores = lambda ii: [pltpu.make_async_copy(buf, rows_of(dst, ii), out_sems.at[n])
                             for n, (buf, dst) in enumerate(zip(bufs_out, hbm_outs))]

        @pl.when(g == 0)
        def _():
            fetch(0, 0, 0)

        @pl.when(g + 1 < nI * nK)
        def _():
            last_k = k == nK - 1
            fetch(jnp.where(last_k, i + 1, i), jnp.where(last_k, 0, k + 1), (g + 1) % 2)

        @pl.when(k == 0)
        def _():
            @pl.when(i > 0)
            def _():
                for cp in stores(i - 1):
                    cp.wait()
            for cp in loads(i):
                cp.start()

        pltpu.make_async_copy(p_refs[0].at[pl.ds(0, tm), pl.ds(0, tk)], abuf.at[g % 2], asem.at[g % 2]).wait()

        def product(cols):
            return jnp.dot(abuf[g % 2], w_ref[:, cols], preferred_element_type=F32)

        col_blocks = [slice(c0, c0 + 512) for c0 in range(0, D, 512)]

        @pl.when(k == 0)
        def _():
            for cols in col_blocks:
                acc[:, cols] = product(cols)

        @pl.when(k > 0)
        def _():
            for cols in col_blocks:
                acc[:, cols] += product(cols)

        @pl.when(k == nK - 1)
        def _():
            for cp in loads(i):
                cp.wait()
            epi(acc, vec_refs, buf_refs, p_outs)
            for cp in stores(i):
                cp.start()

            @pl.when(i == nI - 1)
            def _():
                for cp in stores(i):
                    cp.wait()

    vec = pl.BlockSpec((1, D), lambda i, k: (0, 0))
    scratch = ([pltpu.VMEM((tm, D), F32), pltpu.VMEM((2, tm, tk), BF16)] + [pltpu.VMEM((tm, D), dt) for dt, _, _ in bufs]
               + [_SEMS(2), _SEMS(n_any_in), _SEMS(n_any_out)])
    res, job_res = _pcall(
        body, grid=(nI, nK),
        in_specs=[_ANY] * nP + [pl.BlockSpec((tk, D), lambda i, k: (k, 0))] + [vec] * n_vec + [_ANY] * n_any_in,
        out_specs=[_ANY] * n_any_out + part_specs,
        out_shape=([jax.ShapeDtypeStruct((T, D), bufs[n][0]) for n in store_ix]
                   + [jax.ShapeDtypeStruct(s, d) for s, d, _, _ in parts]),
        scratch_shapes=scratch, name=name, semantics=("arbitrary", "arbitrary"),
        args=[p for p, _ in pieces] + [w] + list(vecs) + [bufs[n][1] for n in load_ix], job=job)
    return res if job is None else (res, job_res)


def _pieces_nn_rms(name, pieces, w, x, gain, sc, dres, tm, tk, job=None):
    _, outs, epi = _rms_mod_bwd_epilogue(x, gain, sc, dres, tm)

    def on_rows(acc, vecs, bufs, parts):
        epi(acc, [bufs[0], vecs[0], vecs[1], bufs[1]], [bufs[1], *parts])

    return _rows_mm(name, pieces, w, x.shape[0], tm, tk, [gain, sc], [(F32, x, False), (F32, dres, True)],
                    outs[1:], on_rows, job=job)


def _rms_mod_fwd(name, x, gain, sc, sh, tr, job=None):
    T = x.shape[0]

    def body(x_ref, g_ref, sc_ref, sh_ref, h_ref):
        xv = x_ref[...]
        rstd = lax.rsqrt(jnp.mean(xv * xv, axis=-1, keepdims=True) + EPS)
        h_ref[...] = ((xv * rstd * g_ref[...]) * (1.0 + sc_ref[...]) + sh_ref[...]).astype(BF16)

    row = pl.BlockSpec((tr, D), lambda i: (i, 0))
    vec = pl.BlockSpec((1, D), lambda i: (0, 0))
    return _pcall(body, grid=(T // tr,), in_specs=[row, vec, vec, vec], out_specs=[row],
                  out_shape=[jax.ShapeDtypeStruct((T, D), BF16)], scratch_shapes=[], name=name, semantics=("parallel",),
                  args=[x, gain, sc, sh], job=job)


def _rms_mod_bwd_epilogue(x, gain, sc, dres, tm, gate=None, mo=None):
    T = x.shape[0]
    with_gate = gate is not None
    row = ((tm, D), lambda i, j, k: (i, 0))
    vec = ((1, D), lambda i, j, k: (0, 0))
    part = ((T // tm * 8, D), F32, (8, D), lambda i, j, k: (i, 0))
    extras = [(x, *row), (gain, *vec), (sc, *vec), (dres, *row)]
    outs = [((T, D), F32, *row), part, part, part]
    if with_gate:
        extras += [(gate, *vec), (mo, *row)]
        outs += [((T, D), BF16, *row), part]

    rows = min(64, tm)

    def epi(acc, ex, ou):
        g = ex[1][...]
        sums = [jnp.zeros((8, D), F32) for _ in range(4)]
        for r0 in range(0, tm, rows):
            rs = slice(r0, r0 + rows)
            dhv, xv = acc[rs, :], ex[0][rs, :]
            rstd = lax.rsqrt(jnp.mean(xv * xv, axis=-1, keepdims=True) + EPS)
            xhat = xv * rstd
            dn = dhv * (1.0 + ex[2][...])
            dxhat = dn * g
            dx = ex[3][rs, :] + rstd * (dxhat - xhat * jnp.mean(dxhat * xhat, axis=-1, keepdims=True))
            ou[0][rs, :] = dx
            terms = [dhv, dhv * (xhat * g), dn * xhat]
            if with_gate:
                terms.append(dx * ex[5][rs, :].astype(F32))
                ou[4][rs, :] = (ex[4][...] * dx).astype(BF16)
            sums = [s + _fold8(t) for s, t in zip(sums, terms)] + sums[len(terms):]
        ou[1][...], ou[2][...], ou[3][...] = sums[:3]
        if with_gate:
            ou[5][...] = sums[3]

    return extras, outs, epi


def _rms_mod_bwd(name, dh, x, gain, sc, dres, tr, gate=None, mo=None):
    T = x.shape[0]
    extras, outs, epi = _rms_mod_bwd_epilogue(x, gain, sc, dres, tr, gate, mo)
    rows_only = lambda im: (lambda i: im(i, 0, 0))
    nE = len(extras)

    def body(dh_ref, *refs):
        epi(dh_ref, refs[:nE], refs[nE:])

    return pl.pallas_call(
        body, grid=(T // tr,),
        in_specs=[pl.BlockSpec((tr, D), lambda i: (i, 0))] + [pl.BlockSpec(bs, rows_only(im)) for _, bs, im in extras],
        out_specs=[pl.BlockSpec(bs, rows_only(im)) for _, _, bs, im in outs],
        out_shape=[jax.ShapeDtypeStruct(s, d) for s, d, _, _ in outs], name=name, compiler_params=_params(("parallel",)),
    )(dh, *[e for e, _, _ in extras])


def _split3(v):
    h = v.astype(BF16)
    r1 = v - h.astype(F32)
    m = r1.astype(BF16)
    lo = (r1 - m.astype(F32)).astype(BF16)
    return h, m, lo


def _tri_mm(tri, v, dims=NN):
    h, m, lo = _split3(v)
    t = tri.astype(BF16)
    mm = lambda p: lax.dot_general(t, p, dims, preferred_element_type=F32)
    return (mm(lo) + mm(m)) + mm(h)


def _hgrn_chunk_terms(q, fl, lb):
    sig = _sigmoid(fl)
    f = lb + (1.0 - lb) * sig
    lf = jnp.log(f)
    kk = 1.0 - f
    sq = _sigmoid(q)
    qf = q * sq
    return sig, f, lf, kk, sq, qf


def _causal(n):
    r = lax.broadcasted_iota(jnp.int32, (n, n), 0)
    c = lax.broadcasted_iota(jnp.int32, (n, n), 1)
    return r >= c


def _hgrn_fwd(proj, lb_logits, o_gain, tt, job=None):
    T = proj.shape[0]
    nT, ncl = T // tt, tt // CHUNK
    C = CHUNK

    def body(q_ref, f_ref, i_ref, g_ref, lbl_ref, og_ref, y_ref, st_ref, S):
        @pl.when(pl.program_id(1) == 0)
        def _():
            S[...] = jnp.zeros_like(S)

        lbl = lbl_ref[...]
        lb = _sigmoid(lbl[0:1, :] - lbl[1:2, :])
        og = og_ref[...]
        shp = (ncl, C, A_HD)
        q, fl, v, g = (r[...].reshape(shp) for r in (q_ref, f_ref, i_ref, g_ref))
        tri = jnp.broadcast_to(_causal(C), (ncl, C, C))
        _, _, lf, kk, _, qf = _hgrn_chunk_terms(q, fl, lb)
        b = _tri_mm(tri, lf, BNN)
        bm, bl = b[:, C // 2 - 1:C // 2, :], b[:, C - 1:C, :]
        qd, kd = qf * jnp.exp(b - bm), kk * jnp.exp(bm - b)
        A = jnp.where(tri, _dot(qd, kd, BNT), 0.0)
        d_st = _dot(v, kk * jnp.exp(bl - b), BTN)
        dec = jnp.exp(bl)
        st = S[...]
        for ci in range(ncl):
            st_ref[0, ci] = st
            st = st * dec[ci] + d_st[ci]
        S[...] = st
        o = _dot(A, v, BNN) + _dot(qf * jnp.exp(b), st_ref[0], BNT)
        r = lax.rsqrt(jnp.mean(o * o, axis=-1, keepdims=True) + EPS)
        y_ref[...] = (o * r * og * (g * _sigmoid(g))).astype(BF16).reshape(tt, A_HD)

    def col(off):
        return pl.BlockSpec((tt, A_HD), lambda h, t: (t, off // A_HD + h))

    head_vec = lambda rows: pl.BlockSpec((rows, A_HD), lambda h, t: (0, h))
    return _pcall(
        body, grid=(A_HEADS, nT),
        in_specs=[col(OFF_QA), col(OFF_FA), col(OFF_IA), col(OFF_GA), head_vec(2), head_vec(1)],
        out_specs=[pl.BlockSpec((tt, A_HD), lambda h, t: (t, h)),
                   pl.BlockSpec((1, ncl, A_HD, A_HD), lambda h, t: (h, t, 0, 0))],
        out_shape=[jax.ShapeDtypeStruct((T, AW), BF16),
                   jax.ShapeDtypeStruct((A_HEADS, T // C, A_HD, A_HD), F32)],
        scratch_shapes=[pltpu.VMEM((A_HD, A_HD), F32)], name="hgrn_fwd", semantics=("parallel", "arbitrary"),
        args=[proj, proj, proj, proj, lb_logits, o_gain], job=job)


def _hgrn_bwd(proj, st, dy, lb_logits, o_gain, tt, job=None):
    T = proj.shape[0]
    nT, ncl = T // tt, tt // CHUNK
    C = CHUNK

    def body(q_ref, f_ref, i_ref, g_ref, st_ref, dy_ref, lbl_ref, og_ref,
             dq_ref, df_ref, di_ref, dg_ref, plb_ref, pog_ref, dS):
        @pl.when(pl.program_id(1) == 0)
        def _():
            dS[...] = jnp.zeros_like(dS)

        lbl = lbl_ref[...]
        lb = _sigmoid(lbl[0:1, :] - lbl[1:2, :])
        og = og_ref[...]
        shp = (ncl, C, A_HD)
        flat = lambda t: t.reshape(tt, A_HD)
        q, fl, v, g, dout = (r[...].reshape(shp) for r in (q_ref, f_ref, i_ref, g_ref, dy_ref))
        tri = jnp.broadcast_to(_causal(C), (ncl, C, C))
        rowi = lax.broadcasted_iota(jnp.int32, shp, 1)
        st0 = st_ref[0]
        sig, f, lf, kk, sq, qf = _hgrn_chunk_terms(q, fl, lb)
        b = _tri_mm(tri, lf, BNN)
        bm, bl = b[:, C // 2 - 1:C // 2, :], b[:, C - 1:C, :]
        e_qd, e_kd, e_ke, e_b = jnp.exp(b - bm), jnp.exp(bm - b), jnp.exp(bl - b), jnp.exp(b)
        qd, kd, ke, qe = qf * e_qd, kk * e_kd, kk * e_ke, qf * e_b
        dec = jnp.exp(bl)
        A = jnp.where(tri, _dot(qd, kd, BNT), 0.0)
        o = _dot(A, v, BNN) + _dot(qe, st0, BNT)
        r = lax.rsqrt(jnp.mean(o * o, axis=-1, keepdims=True) + EPS)
        sg = _sigmoid(g)
        on = o * r * og
        dg_ref[...] = flat((dout * on * (sg * (1.0 + g * (1.0 - sg)))).astype(BF16))
        don = dout * (g * sg)
        pog_ref[...] = _fold8(flat(don * o * r))
        dyh = don * og
        do = r * (dyh - o * (r * r) * jnp.mean(dyh * o, axis=-1, keepdims=True))
        g_st = _dot(do, qe, BTN)
        run = dS[...]
        after = [None] * ncl
        for ci in reversed(range(ncl)):
            after[ci] = run
            run = g_st[ci] + run * dec[ci]
        dS[...] = run
        d_after = jnp.stack(after, axis=0)
        ddec = jnp.sum(d_after * st0, axis=1, keepdims=True)
        dqe = _dot(do, st0, BNN)
        dke = _dot(v, d_after, BNN)
        dA = jnp.where(tri, _dot(do, v, BNT), 0.0)
        dv = _dot(ke, d_after, BNT) + _dot(A, do, BTN)
        dqd = _dot(dA, kd, BNN)
        dkd = _dot(dA, qd, BTN)
        di_ref[...] = flat(dv.astype(BF16))
        dqf = dqe * e_b + dqd * e_qd
        dkk = dkd * e_kd + dke * e_ke
        t_qd, t_kd, t_ke = dqd * qd, dkd * kd, dke * ke
        db = dqe * qe + t_qd - t_kd - t_ke
        dbm = jnp.sum(t_kd - t_qd, axis=1, keepdims=True)
        dbl = jnp.sum(t_ke, axis=1, keepdims=True) + ddec * dec
        db = db + jnp.where(rowi == C // 2 - 1, dbm, 0.0) + jnp.where(rowi == C - 1, dbl, 0.0)
        dlf = _tri_mm(tri, db, BTN)
        dfv = dlf / f - dkk
        df_ref[...] = flat((dfv * (1.0 - lb) * sig * (1.0 - sig)).astype(BF16))
        plb_ref[...] = _fold8(flat(dfv * (1.0 - sig)))
        dq_ref[...] = flat((dqf * (sq * (1.0 + q * (1.0 - sq)))).astype(BF16))

    def col(off):
        return pl.BlockSpec((tt, A_HD), lambda h, t: (nT - 1 - t, off // A_HD + h))

    head_vec = lambda rows: pl.BlockSpec((rows, A_HD), lambda h, t: (0, h))
    o_spec = pl.BlockSpec((tt, A_HD), lambda h, t: (nT - 1 - t, h))
    p_spec = pl.BlockSpec((8, A_HD), lambda h, t: (t, h))
    o_shape = jax.ShapeDtypeStruct((T, AW), BF16)
    p_shape = jax.ShapeDtypeStruct((nT * 8, AW), F32)
    return _pcall(
        body, grid=(A_HEADS, nT),
        in_specs=[col(OFF_QA), col(OFF_FA), col(OFF_IA), col(OFF_GA),
                  pl.BlockSpec((1, ncl, A_HD, A_HD), lambda h, t: (h, nT - 1 - t, 0, 0)),
                  pl.BlockSpec((tt, A_HD), lambda h, t: (nT - 1 - t, h)), head_vec(2), head_vec(1)],
        out_specs=[o_spec, o_spec, o_spec, o_spec, p_spec, p_spec],
        out_shape=[o_shape, o_shape, o_shape, o_shape, p_shape, p_shape],
        scratch_shapes=[pltpu.VMEM((A_HD, A_HD), F32)], name="hgrn_bwd", semantics=("parallel", "arbitrary"),
        args=[proj, proj, proj, proj, st, dy, lb_logits, o_gain], job=job)


LANES = 128
Q_COLS = BW // LANES


def _low_half():
    return lax.broadcasted_iota(jnp.int32, (1, LANES), 1) < B_HD


def _half_sum(t, low):
    lo = jnp.sum(jnp.where(low, t, 0.0), axis=-1, keepdims=True)
    hi = jnp.sum(jnp.where(low, 0.0, t), axis=-1, keepdims=True)
    return jnp.where(low, lo, hi)


def _half_rms(t, low):
    r = lax.rsqrt(_half_sum(t * t, low) * (1.0 / B_HD) + EPS)
    return t * r, r


def _fold_halves(p, low):
    return jnp.where(low, p + pltpu.roll(p, B_HD, 1), 0.0)


def _stack_cols(x):
    return jnp.stack([x[:, c * LANES:(c + 1) * LANES] for c in range(Q_COLS)], axis=0).reshape(KV_HEADS, 2 * BLK, LANES)


def _col_of(t, c):
    return t[c // 2, (c % 2) * BLK:(c % 2 + 1) * BLK]


def _split_halves(col, s, low):
    own = jnp.where(low if s == 0 else jnp.logical_not(low), col, 0.0)
    other = pltpu.roll(own, B_HD, 1)
    return (own, other) if s == 0 else (other, own)


def _swa_keys(kp_ref, kc_ref, vp_ref, vc_ref, kg, low):
    k_lo, k_hi, v_lo, v_hi, hats = [], [], [], [], []
    for j in range(KVW // LANES):
        cs = slice(j * LANES, (j + 1) * LANES)
        k_hat, k_r = _half_rms(jnp.concatenate([kp_ref[:, cs], kc_ref[:, cs]], axis=0), low)
        vcol = jnp.concatenate([vp_ref[:, cs], vc_ref[:, cs]], axis=0)
        hats.append((k_hat, k_r))
        for s in range(2):
            for dst_lo, dst_hi, col in ((k_lo, k_hi, k_hat * kg), (v_lo, v_hi, vcol)):
                lo, hi = _split_halves(col, s, low)
                dst_lo.append(lo)
                dst_hi.append(hi)
    st = lambda parts: jnp.stack(parts, axis=0)
    return st(k_lo), st(k_hi), st(v_lo), st(v_hi), hats


def _swa_mask(first_block):
    qi = lax.broadcasted_iota(jnp.int32, (BLK, 2 * BLK), 0) + BLK
    ki = lax.broadcasted_iota(jnp.int32, (BLK, 2 * BLK), 1)
    rel = qi - ki
    m = (rel >= 0) & (rel < BLK) & (jnp.logical_not(first_block) | (ki >= BLK))
    return jnp.concatenate([m, m], axis=0)


def _sink_cols(sk_ref, hi):
    top = lax.broadcasted_iota(jnp.int32, (2 * BLK, 1), 0) < BLK
    return jnp.stack([jnp.where(top, sk_ref[0, GROUP * hk + hi], sk_ref[0, GROUP * hk + 2 + hi])
                      for hk in range(KV_HEADS)], axis=0)


def _swa_probs(qn, k_half, sink, mask):
    s = jnp.where(mask, _dot(qn, k_half, BNT) * (B_HD ** -0.5), NEG)
    m = jnp.maximum(jnp.max(s, axis=-1, keepdims=True), sink)
    p = jnp.exp(s - m)
    ps = jnp.exp(sink - m)
    inv = 1.0 / (jnp.sum(p, axis=-1, keepdims=True) + ps)
    return p * inv, ps * inv


def _swa_fwd(proj, q_gain, k_gain, sinks, job=None):
    T = proj.shape[0]
    nb = T // BLK

    def body(q_ref, kc_ref, kp_ref, vc_ref, vp_ref, qg_ref, kg_ref, sk_ref, o_ref):
        low = _low_half()
        mask = _swa_mask(pl.program_id(0) == 0)
        qn = _half_rms(_stack_cols(q_ref[...]), low)[0] * qg_ref[...]
        k_lo, k_hi, v_lo, v_hi, _ = _swa_keys(kp_ref, kc_ref, vp_ref, vc_ref, kg_ref[...], low)
        p_lo, _ = _swa_probs(qn, k_lo, _sink_cols(sk_ref, 0), mask)
        p_hi, _ = _swa_probs(qn, k_hi, _sink_cols(sk_ref, 1), mask)
        o = (_dot(p_lo, v_lo, BNN) + _dot(p_hi, v_hi, BNN)).astype(BF16)
        for c in range(Q_COLS):
            o_ref[:, c * LANES:(c + 1) * LANES] = _col_of(o, c)

    q_gain, k_gain = jnp.tile(q_gain, (1, 2)), jnp.tile(k_gain, (1, 2))
    cur = lambda w, off: pl.BlockSpec((BLK, w), lambda i: (i, off // w))
    prev = lambda w, off: pl.BlockSpec((BLK, w), lambda i: (jnp.maximum(i - 1, 0), off // w))
    small = lambda n: pl.BlockSpec((1, 2 * n), lambda i: (0, 0))
    return _pcall(
        body, grid=(nb,),
        in_specs=[cur(BW, OFF_QB), cur(KVW, OFF_KB), prev(KVW, OFF_KB), cur(KVW, OFF_VB), prev(KVW, OFF_VB),
                  small(B_HD), small(B_HD), pl.BlockSpec(memory_space=pltpu.SMEM)],
        out_specs=[pl.BlockSpec((BLK, BW), lambda i: (i, 0))],
        out_shape=[jax.ShapeDtypeStruct((T, BW), BF16)], scratch_shapes=[], name="swa_fwd", semantics=("parallel",),
        args=[proj, proj, proj, proj, proj, q_gain, k_gain, sinks], job=job)


def _swa_bwd(proj, dout, q_gain, k_gain, sinks, job=None):
    T = proj.shape[0]
    nb = T // BLK
    W = BW + 2 * KVW

    def body(q_ref, kc_ref, kp_ref, vc_ref, vp_ref, do_ref, qg_ref, kg_ref, sk_ref,
             dq_ref, dkv_ref, pqg_ref, pkg_ref, psk_ref, dkn_c, dv_c):
        i = pl.program_id(0)
        live = i < nb
        low = _low_half()
        high = jnp.logical_not(low)
        qg, kg = qg_ref[...], kg_ref[...]
        mask = _swa_mask(i == 0)
        lane = lax.broadcasted_iota(jnp.int32, (1, LANES), 1)
        scale = B_HD ** -0.5

        @pl.when(i == 0)
        def _():
            dkn_c[...] = jnp.zeros_like(dkn_c)
            dv_c[...] = jnp.zeros_like(dv_c)

        q_hat, q_r = _half_rms(_stack_cols(q_ref[...]), low)
        qn = q_hat * qg
        k_lo, k_hi, v_lo, v_hi, hats = _swa_keys(kp_ref, kc_ref, vp_ref, vc_ref, kg, low)
        do = _stack_cols(do_ref[...])
        dqn = jnp.zeros((KV_HEADS, 2 * BLK, LANES), F32)
        acc_sk = jnp.zeros((1, LANES), F32)
        dk_parts, dv_parts = [], []
        for hi, (k_h, v_h) in enumerate(((k_lo, v_lo), (k_hi, v_hi))):
            p, ps = _swa_probs(qn, k_h, _sink_cols(sk_ref, hi), mask)
            dp = _dot(do, v_h, BNT)
            delta = jnp.sum(p * dp, axis=-1, keepdims=True)
            ds = p * (dp - delta) * scale
            dqn = dqn + _dot(ds, k_h, BNN)
            dk_parts.append(_dot(ds, qn, BTN))
            dv_parts.append(_dot(p, do, BTN))
            t = ps * delta
            for hk in range(KV_HEADS):
                for rows in range(2):
                    h = GROUP * hk + 2 * rows + hi
                    acc_sk = acc_sk + jnp.where(
                        lane == h, -jnp.sum(t[hk, rows * BLK:(rows + 1) * BLK], axis=0, keepdims=True), 0.0)
        dqh = dqn * qg
        dq = (q_r * (dqh - q_hat * (_half_sum(dqh * q_hat, low) * (1.0 / B_HD)))).astype(BF16)
        for c in range(Q_COLS):
            dq_ref[:, c * LANES:(c + 1) * LANES] = _col_of(dq, c)
        acc_qg = _fold_halves(_fold8((dqn * q_hat).reshape(KV_HEADS * 2 * BLK, LANES)), low)

        def native(parts, j):
            lo_arr, hi_arr = parts
            a, b = 2 * j, 2 * j + 1
            return (jnp.where(low, lo_arr[a], 0.0) + pltpu.roll(jnp.where(high, hi_arr[a], 0.0), B_HD, 1)
                    + jnp.where(high, hi_arr[b], 0.0) + pltpu.roll(jnp.where(low, lo_arr[b], 0.0), B_HD, 1))

        acc_kg = jnp.zeros((8, LANES), F32)
        for j in range(KVW // LANES):
            cs = slice(j * LANES, (j + 1) * LANES)
            dkn = jnp.where(live, native(dk_parts, j), 0.0)
            dvc = jnp.where(live, native(dv_parts, j), 0.0)
            kp_hat, kp_r = hats[j][0][:BLK], hats[j][1][:BLK]
            dkn_prev = dkn_c[:, cs] + dkn[:BLK]
            dv_prev = dv_c[:, cs] + dvc[:BLK]
            acc_kg = acc_kg + _fold8(dkn_prev * kp_hat)
            dkh = dkn_prev * kg
            dkv_ref[:, cs] = (kp_r * (dkh - kp_hat * (_half_sum(dkh * kp_hat, low) * (1.0 / B_HD)))).astype(BF16)
            dkv_ref[:, KVW + j * LANES:KVW + (j + 1) * LANES] = dv_prev.astype(BF16)
            dkn_c[:, cs] = dkn[BLK:]
            dv_c[:, cs] = dvc[BLK:]
        keep = jnp.where(i > 0, 1.0, 0.0)
        pqg_ref[...] = jnp.where(live, acc_qg, 0.0)
        pkg_ref[...] = _fold_halves(acc_kg, low) * keep
        psk_ref[...] = jnp.broadcast_to(jnp.where(live, acc_sk, 0.0), (8, LANES)) * (
            lax.broadcasted_iota(jnp.int32, (8, LANES), 0) == 0).astype(F32)

    q_gain, k_gain = jnp.tile(q_gain, (1, 2)), jnp.tile(k_gain, (1, 2))
    last = nb - 1
    cur = lambda w, off: pl.BlockSpec((BLK, w), lambda i: (jnp.minimum(i, last), off // w))
    prev = lambda w, off: pl.BlockSpec((BLK, w), lambda i: (jnp.maximum(i - 1, 0), off // w))
    small = lambda n: pl.BlockSpec((1, 2 * n), lambda i: (0, 0))
    part = pl.BlockSpec((8, 128), lambda i: (i, 0))
    p_shape = jax.ShapeDtypeStruct(((nb + 1) * 8, 128), F32)
    return _pcall(
        body, grid=(nb + 1,),
        in_specs=[cur(BW, OFF_QB), cur(KVW, OFF_KB), prev(KVW, OFF_KB), cur(KVW, OFF_VB), prev(KVW, OFF_VB),
                  pl.BlockSpec((BLK, BW), lambda i: (jnp.minimum(i, last), 0)), small(B_HD), small(B_HD),
                  pl.BlockSpec(memory_space=pltpu.SMEM)],
        out_specs=[pl.BlockSpec((BLK, BW), lambda i: (i, 0)),
                   pl.BlockSpec((BLK, 2 * KVW), lambda i: (jnp.maximum(i - 1, 0), 0)), part, part, part],
        out_shape=[jax.ShapeDtypeStruct((T + BLK, BW), BF16), jax.ShapeDtypeStruct((T, 2 * KVW), BF16),
                   p_shape, p_shape, p_shape],
        scratch_shapes=[pltpu.VMEM((BLK, KVW), F32), pltpu.VMEM((BLK, KVW), F32)], name="swa_bwd",
        semantics=("arbitrary",), args=[proj, proj, proj, proj, proj, dout, q_gain, k_gain, sinks], job=job)


def _branch_merge(ya_pre, attn, wa_t, wb_t, proj, tm, tn, job=None):
    T = ya_pre.shape[0]

    def body(a_ref, b_ref, wa_ref, wb_ref, ga_ref, gb_ref, ya_ref, yb_ref, mg_ref):
        ya = lax.dot_general(a_ref[...], wa_ref[...], NT, preferred_element_type=F32)
        yb = lax.dot_general(b_ref[...], wb_ref[...], NT, preferred_element_type=F32)
        ya_ref[...] = ya.astype(BF16)
        yb_ref[...] = yb.astype(BF16)
        mg_ref[...] = (_sigmoid(ga_ref[...]) * ya + _sigmoid(gb_ref[...]) * yb).astype(BF16)

    o_spec = pl.BlockSpec((tm, tn), lambda i, j: (i, j))
    o_shape = jax.ShapeDtypeStruct((T, D), BF16)
    return _pcall(
        body, grid=(T // tm, D // tn),
        in_specs=[pl.BlockSpec((tm, AW), lambda i, j: (i, 0)), pl.BlockSpec((tm, BW), lambda i, j: (i, 0)),
                  pl.BlockSpec((tn, AW), lambda i, j: (j, 0)), pl.BlockSpec((tn, BW), lambda i, j: (j, 0)),
                  pl.BlockSpec((tm, tn), lambda i, j: (i, OFF_GTA // tn + j)),
                  pl.BlockSpec((tm, tn), lambda i, j: (i, OFF_GTB // tn + j))],
        out_specs=[o_spec, o_spec, o_spec], out_shape=[o_shape, o_shape, o_shape], scratch_shapes=[], name="branch_merge",
        semantics=("parallel", "parallel"), args=[ya_pre, attn, wa_t, wb_t, proj, proj], job=job)


def _ij(i, j, k):
    return (i, j)


def _local_step(x, tgt, mod, g1, g2, lbl, og, qg, kg, sk, shards, c_arr, update):
    win_s, wa_s, wb_s, wout_s, wmi_s, wmo_s = shards
    T = x.shape[0]
    tm, tr, tt = min(1024, T), min(256, T), min(2048, T)
    tk_t = min(1024, T)
    tn = 512
    sh1, sc1, gt1, sh2, sc2, gt2 = (mod[:, i * D:(i + 1) * D] for i in range(N_MOD))
    nI = T // tm
    blk = (tm, tn)

    (h,), (win_t,) = _rms_mod_fwd("rms1_fwd", x, g1, sc1, sh1, tr, job=_gather_relay_job([win_s], alone=True))

    def epi_store(acc, ex, ou):
        ou[0][...] = acc.astype(ou[0].dtype)

    tm2 = min(2048, T)
    blk2 = (tm2, tn)

    full = lambda s: (0, s.shape[0])
    last = wmi_s.shape[0]
    gather = _gather_relay_job
    (proj,), (wa_t, wb_t, w_out, wmi_part) = _mm(
        "in_proj", "nt", [(h, D)], win_t, T, IN_W, D, tm2, tn, D, [], [((T, IN_W), F32, blk2, _ij)], epi_store,
        job=gather([wa_s, wb_s, wout_s, wmi_s], rows=[full(wa_s), full(wb_s), full(wout_s), (0, MI_CUT)]))
    (ya_pre, st), _ = _hgrn_fwd(proj, lbl, og, tt)
    (attn,), (wmi_t,) = _swa_fwd(proj, qg, kg, sk, job=gather([wmi_s], rows=[(MI_CUT, last)], into=[wmi_part]))
    (ya, yb, merged), _ = _branch_merge(ya_pre, attn, wa_t, wb_t, proj, tm, tn)

    def residual_rows(acc, vecs, bufs, parts):
        gt, gain, sc, sh = (v[...] for v in vecs)
        x_buf, mo_buf, h2_buf = bufs
        rows = min(64, tm)
        for r0 in range(0, tm, rows):
            rs = slice(r0, r0 + rows)
            z = acc[rs, :]
            mo_buf[rs, :] = z.astype(BF16)
            x1v = x_buf[rs, :] + gt * z
            x_buf[rs, :] = x1v
            rstd = lax.rsqrt(jnp.mean(x1v * x1v, axis=-1, keepdims=True) + EPS)
            h2_buf[rs, :] = ((x1v * rstd * gain) * (1.0 + sc) + sh).astype(BF16)

    x1, mo, h2 = _rows_mm("out_proj", [(merged, D)], w_out, T, tm, min(1024, D), [gt1, g2, sc2, sh2],
                          [(F32, x, True), (BF16, None, True), (BF16, None, True)], [], residual_rows)

    def epi_relu2(acc, ex, ou):
        r = jnp.maximum(acc, 0.0)
        ou[0][...] = r.astype(BF16)
        ou[1][...] = (r * r).astype(BF16)

    (r, a), (w_mo,) = _mm("mlp_in", "nt", [(h2, D)], wmi_t, T, HID, D, tm2, tn, D, [],
                          [((T, HID), BF16, blk2, _ij), ((T, HID), BF16, blk2, _ij)], epi_relu2,
                          job=gather([wmo_s]))

    def loss_rows(acc, vecs, bufs, parts):
        gt = vecs[0][...]
        x1_buf, t_buf, dz_buf = bufs
        rows = min(64, tm)
        loss_sum, gate_sum = jnp.zeros((8, D), F32), jnp.zeros((8, D), F32)
        for r0 in range(0, tm, rows):
            rs = slice(r0, r0 + rows)
            z = acc[rs, :]
            e = x1_buf[rs, :] + gt * z - t_buf[rs, :]
            dy = e * (1.0 / D)
            t_buf[rs, :] = dy
            dz_buf[rs, :] = (gt * dy).astype(BF16)
            loss_sum = loss_sum + _fold8(e * e)
            gate_sum = gate_sum + _fold8(dy * z)
        parts[0][...] = loss_sum * (0.5 / D)
        parts[1][...] = gate_sum

    part_rows = ((nI * 8, D), F32, (8, D), lambda i, j, k: (i, 0))
    dy, dz, p_loss, p_gt2 = _rows_mm(
        "mlp_out", [(a, HID)], w_mo, T, tm, 1024, [gt2], [(F32, x1, False), (F32, tgt, True), (BF16, None, True)],
        [part_rows, part_rows], loss_rows)

    def epi_du(acc, ex, ou):
        ou[0][...] = (acc * (2.0 * ex[0][...].astype(F32))).astype(BF16)

    (du,) = _mm("mlp_out_dx", "nt", [(dz, D)], w_mo, T, HID, D, tm2, tn, D, [(r, blk2, _ij)],
                [((T, HID), BF16, blk2, _ij)], epi_du)
    gblk = (1024, 1024)
    gwide = (1024, D)
    pair_sum = lambda nm, g, r1: _pair_sum("pair_sum_" + nm, g, r1, c_arr, _sum_rows(r1.shape[1]))
    (g_mo,) = _mm("mlp_out_dw", "tn", [(a, HID)], dz, HID, D, T, 1024, D, tk_t, [], [((HID, D), BF16, gwide, _ij)], epi_store)
    (dh2,), (r1_mo,) = _mm("mlp_in_dx", "nn", [(du, HID)], wmi_t, T, D, HID, tm, D, 1024, [],
                           [((T, D), F32, (tm, D), _ij)], epi_store, job=_pair_job([g_mo]))
    dx1, p_sh2, p_sc2, p_g2, dmo, p_gt1 = _rms_mod_bwd("rms2_bwd", dh2, x1, g2, sc2, dy, tr, gate=gt1, mo=mo)
    s_mo = pair_sum("mlp_out", g_mo, r1_mo)
    near, far = (1, 2), (3,)
    cut = s_mo.shape[1] // 2
    (g_mi,), (rn_mo, rf_mo) = _mm(
        "mlp_in_dw", "tn", [(du, HID)], h2, HID, D, T, 1024, D, tk_t, [], [((HID, D), BF16, gwide, _ij)], epi_store,
        job=_both(_chip_job([s_mo], near), _chip_job([s_mo], far, rows=(0, cut))))

    def epi_gates(acc, ex, ou):
        ya_ref, yb_ref, ga_ref, gb_ref = ex
        sa, sb = _sigmoid(ga_ref[...]), _sigmoid(gb_ref[...])
        ou[0][...] = (acc * sa).astype(BF16)
        ou[1][...] = (acc * sb).astype(BF16)
        ou[2][...] = (acc * ya_ref[...].astype(F32) * (sa * (1.0 - sa))).astype(BF16)
        ou[3][...] = (acc * yb_ref[...].astype(F32) * (sb * (1.0 - sb))).astype(BF16)

    o_bf = ((T, D), BF16, blk, _ij)
    (dya, dyb, dga, dgb), (rf_mo, r1_mi) = _mm(
        "out_proj_dx", "nt", [(dmo, D)], w_out, T, D, D, tm, tn, D,
        [(ya, blk, _ij), (yb, blk, _ij), (proj, blk, lambda i, j, k: (i, OFF_GTA // tn + j)),
         (proj, blk, lambda i, j, k: (i, OFF_GTB // tn + j))], [o_bf, o_bf, o_bf, o_bf], epi_gates,
        job=_both(_chip_job([s_mo], far, rows=(cut, 2 * cut), into=[rf_mo]), _pair_job([g_mi])))
    s_mi = pair_sum("mlp_in", g_mi, r1_mi)
    (g_out,) = _mm("out_proj_dw", "tn", [(merged, D)], dmo, D, D, T, 1024, 1024, tk_t, [], [((D, D), BF16, gblk, _ij)], epi_store)
    dya_pre, dattn = _twin_mm("branch_dx", "nn", [(dya, wa_t), (dyb, wb_t)], T, AW, D, tm, tn, D, F32)
    g_a, g_b = _twin_mm("branch_dw", "tn", [(dya, ya_pre), (dyb, attn)], D, AW, T, 1024, 1024, tk_t, BF16)
    (dqa, dfa, dia, dgg, p_lb, p_og), (rn_mi, r1_out, r1_a, r1_b) = _hgrn_bwd(
        proj, st, dya_pre, lbl, og, tt, job=_both(_chip_job([s_mi], near), _pair_job([g_out, g_a, g_b])))
    (dqb, dkv, p_qg, p_kg, p_sk), (rf_mi,) = _swa_bwd(proj, dattn, qg, kg, sk, job=_chip_job([s_mi], far))
    s_out, s_a, s_b = pair_sum("out", g_out, r1_out), pair_sum("branch_a", g_a, r1_a), pair_sum("branch_b", g_b, r1_b)
    pieces = [(dqa, AW), (dfa, AW), (dia, AW), (dgg, AW), (dqb, BW), (dkv, 2 * KVW), (dga, D), (dgb, D)]
    (g_in,), (r2_out, r2_a, r2_b) = _pieces_tn("in_proj_dw", pieces, h, 512, job=_chip_job([s_out, s_a, s_b]))
    (r1_in,) = update("w_mlp_in", s_mi, [rn_mi, rf_mi], job=_pair_job([g_in]))
    s_in = pair_sum("in", g_in, r1_in)
    (dx, p_sh1, p_sc1, p_g1), (r2_in,) = _pieces_nn_rms(
        "in_proj_dx", pieces, win_t, x, g1, sc1, dx1, tm, 512, job=_chip_job([s_in]))

    partials = dict(sh1=p_sh1, sc1=p_sc1, gt1=p_gt1, sh2=p_sh2, sc2=p_sc2, gt2=p_gt2, g1=p_g1, g2=p_g2,
                    lb=p_lb, og=p_og, qg=p_qg, kg=p_kg, sk=p_sk, loss=p_loss)
    sums = dict(w_in=(s_in, [r2_in]), w_branch_a=(s_a, [r2_a]), w_branch_b=(s_b, [r2_b]), w_out=(s_out, [r2_out]),
                w_mlp_in=(s_mi, [rn_mi, rf_mi]), w_mlp_out=(s_mo, [rn_mo, rf_mo]))
    return dx, sums, partials


def _exchange_slots(buf, send_sems, recv_sems):
    me = _mesh_pos()
    mine = buf.at[_index(me)]
    sends = []
    for k in range(1, N_DEV):
        cp = pltpu.make_async_remote_copy(src_ref=mine, dst_ref=mine, send_sem=send_sems.at[k - 1],
                                          recv_sem=recv_sems.at[k - 1], device_id=_flip(me, k), device_id_type=MESH)
        cp.start()
        sends.append(cp)
    for k in range(1, N_DEV):
        theirs = buf.at[_index(_flip(me, k))]
        pltpu.make_async_remote_copy(src_ref=theirs, dst_ref=theirs, send_sem=send_sems.at[k - 1],
                                     recv_sem=recv_sems.at[k - 1], device_id=_flip(me, k), device_id_type=MESH).wait_recv()
    for cp in sends:
        cp.wait_send()


ADA_W = N_MOD * D // N_DEV


def _ada_mod(c, w_ada, b_shard):
    def body(c_ref, w_ref, b_ref, mod_ref, sc_ref, cbuf, mbuf, s1, r1, s2, r2):
        me = _index(_mesh_pos())
        cbuf[me] = c_ref[...]
        _exchange_slots(cbuf, s1, r1)
        row = lax.broadcasted_iota(jnp.int32, (N_DEV, D), 0)
        call = jnp.zeros((N_DEV, D), F32)
        for d in range(N_DEV):
            call = jnp.where(row == d, cbuf[d], call)
        sc = call * _sigmoid(call)
        sc_ref[...] = sc
        mbuf[me] = _dot(sc, w_ref[...]) + b_ref[...]
        _exchange_slots(mbuf, s2, r2)
        for s in range(N_DEV):
            mod_ref[:, s * ADA_W:(s + 1) * ADA_W] = mbuf[s, pl.ds(me, 1), :]

    return pl.pallas_call(
        body, in_specs=[_VMEM, _VMEM, _VMEM], out_specs=[_VMEM, _VMEM],
        out_shape=[jax.ShapeDtypeStruct((1, N_MOD * D), F32), jax.ShapeDtypeStruct((N_DEV, D), F32)],
        scratch_shapes=[pltpu.VMEM((N_DEV, 1, D), F32), pltpu.VMEM((N_DEV, N_DEV, ADA_W), F32),
                        _SEMS(N_DEV - 1), _SEMS(N_DEV - 1), _SEMS(N_DEV - 1), _SEMS(N_DEV - 1)],
        name="ada_mod", compiler_params=pltpu.CompilerParams(vmem_limit_bytes=VMEM_LIMIT),
    )(c, w_ada, b_shard)


SMALL_SEGS = (("b_ada", N_MOD * D), ("norm1_gain", D), ("norm2_gain", D), ("lb0", AW), ("lb1", AW),
              ("hgrn_o_gain", AW), ("q_norm_gain", 128), ("k_norm_gain", 128), ("sinks", 128))
SMALL_W = sum(w for _, w in SMALL_SEGS)
X_SEGS = (("sh1", D), ("sc1", D), ("gt1", D), ("sh2", D), ("sc2", D), ("gt2", D), ("g1", D), ("g2", D),
          ("lb", AW), ("og", AW), ("qg", 128), ("kg", 128), ("sk", 128), ("loss", 128))
X_W = sum(w for _, w in X_SEGS)


def _offsets(segs):
    out, o = {}, 0
    for name, w in segs:
        out[name] = (o, w)
        o += w
    return out


def _small_reduce(parts, lb_logits):
    xo, so = _offsets(X_SEGS), _offsets(SMALL_SEGS)
    names = [nm for nm, _ in X_SEGS]

    def body(*refs):
        p_refs = dict(zip(names, refs[:len(names)]))
        lbl_ref, allx, gs_ref, loss_ref, send_sems, recv_sems = refs[len(names):]
        me = _index(_mesh_pos())
        for nm, (o, w) in xo.items():
            if nm == "loss":
                allx[me, :, o:o + w] = jnp.broadcast_to(jnp.sum(p_refs[nm][...]), (1, w))
            else:
                allx[me, :, o:o + w] = jnp.sum(p_refs[nm][...], axis=0, keepdims=True)
        _exchange_slots(allx, send_sems, recv_sems)
        tot = allx[0]
        for d in range(1, N_DEV):
            tot = tot + allx[d]
        seg = lambda nm: tot[:, xo[nm][0]:xo[nm][0] + xo[nm][1]]

        def put(nm, v):
            gs_ref[:, so[nm][0]:so[nm][0] + so[nm][1]] = v

        put("b_ada", tot[:, 0:N_MOD * D])
        put("norm1_gain", seg("g1"))
        put("norm2_gain", seg("g2"))
        lbl = lbl_ref[...]
        lb = _sigmoid(lbl[0:1, :] - lbl[1:2, :])
        dl0 = seg("lb") * lb * (1.0 - lb)
        put("lb0", dl0)
        put("lb1", -dl0)
        put("hgrn_o_gain", seg("og"))
        put("q_norm_gain", seg("qg"))
        put("k_norm_gain", seg("kg"))
        put("sinks", seg("sk"))
        loss_ref[...] = seg("loss")

    return pl.pallas_call(
        body, in_specs=[_VMEM] * (len(names) + 1), out_specs=[_VMEM, _VMEM, _VMEM],
        out_shape=[jax.ShapeDtypeStruct((N_DEV, 1, X_W), F32), jax.ShapeDtypeStruct((1, SMALL_W), F32),
                   jax.ShapeDtypeStruct((1, 128), F32)],
        scratch_shapes=[_SEMS(N_DEV - 1), _SEMS(N_DEV - 1)], name="small_reduce",
        compiler_params=pltpu.CompilerParams(vmem_limit_bytes=VMEM_LIMIT),
    )(*[parts[nm] for nm in names], lb_logits)


def _adamw_math(w, g, m, v):
    m = B1 * m + (1.0 - B1) * g
    v = B2 * v + (1.0 - B2) * (g * g)
    m_hat = m / (1.0 - B1 ** STEP)
    v_hat = v / (1.0 - B2 ** STEP)
    return -LR * (m_hat / (jnp.sqrt(v_hat) + ADAM_EPS) + WD * w), m, v


def _sum_rows(rs):
    return 256 if rs % 256 == 0 else rs // 2


def _pair_sum(name, g, recv, c_arr, tr):
    _, rs, cols = recv.shape
    blk = (1, tr, cols)

    def body(c_ref, g_ref, r_ref, o_ref):
        o_ref[...] = (g_ref[...].astype(F32) + r_ref[...].astype(F32)).astype(BF16)

    grid_spec = pltpu.PrefetchScalarGridSpec(
        num_scalar_prefetch=1, grid=(4, rs // tr),
        in_specs=[pl.BlockSpec(blk, lambda q, i, c: (2 * q + c[0], i, 0)), pl.BlockSpec(blk, lambda q, i, c: (q, i, 0))],
        out_specs=pl.BlockSpec(blk, lambda q, i, c: (q, i, 0)))
    return pl.pallas_call(body, grid_spec=grid_spec, out_shape=jax.ShapeDtypeStruct((4, rs, cols), BF16), name=name,
                          compiler_params=_params(("parallel", "parallel")))(c_arr, g.reshape(N_DEV, rs, cols), recv)


def _sum_adamw(name, sums, recvs, q_arr, w, m, v, transposed, tile, job=None):
    rows, cols = w.shape
    nR = len(recvs)

    def body(q_ref, s_ref, *refs):
        r_refs = refs[:nR]
        w_ref, m_ref, v_ref, g_ref, d_ref, nm_ref, nv_ref = refs[nR:]
        g = s_ref[0].astype(F32)
        for r_ref in r_refs:
            for slot in range(r_ref.shape[0]):
                g = g + r_ref[slot].astype(F32)
        g = g.T if transposed else g
        g_ref[...] = g
        d_ref[...], nm_ref[...], nv_ref[...] = _adamw_math(w_ref[...], g, m_ref[...], v_ref[...])

    if transposed:
        slab = lambda n, first: pl.BlockSpec((n, cols, tile), lambda i, q: (first(q), 0, i))
    else:
        slab = lambda n, first: pl.BlockSpec((n, tile, cols), lambda i, q: (first(q), i, 0))
    spec = pl.BlockSpec((tile, cols), lambda i, q: (i, 0))
    shape = jax.ShapeDtypeStruct((rows, cols), F32)
    res, job_res = _pcall(
        body, grid=(rows // tile,),
        in_specs=[slab(1, lambda q: q[0])] + [slab(r.shape[0], lambda q: 0) for r in recvs] + [spec] * 3,
        out_specs=[spec] * 4, out_shape=[shape] * 4, scratch_shapes=[], name=name, semantics=("parallel",),
        args=[sums, *recvs, w, m, v], job=job, prefetch=[q_arr])
    return res if job is None else (res, job_res)


def _adamw(name, w, g, m, v, tr):
    rows, cols = w.shape

    def body(w_ref, g_ref, m_ref, v_ref, d_ref, nm_ref, nv_ref):
        d_ref[...], nm_ref[...], nv_ref[...] = _adamw_math(w_ref[...], g_ref[...], m_ref[...], v_ref[...])

    spec = pl.BlockSpec((tr, cols), lambda i: (i, 0))
    shape = jax.ShapeDtypeStruct((rows, cols), F32)
    return pl.pallas_call(
        body, grid=(rows // tr,), in_specs=[spec] * 4, out_specs=[spec] * 3, out_shape=[shape] * 3, name=name,
        compiler_params=_params(("parallel",)),
    )(w, g, m, v)


def _ada_update(sc_t, dmod_cols, w, m, v, tr):
    rows, cols = w.shape

    def body(s_ref, d_ref, w_ref, m_ref, v_ref, g_ref, dl_ref, nm_ref, nv_ref):
        g = jnp.dot(s_ref[...], d_ref[...], precision=lax.Precision.HIGHEST, preferred_element_type=F32)
        g_ref[...] = g
        dl_ref[...], nm_ref[...], nv_ref[...] = _adamw_math(w_ref[...], g, m_ref[...], v_ref[...])

    spec = pl.BlockSpec((tr, cols), lambda i: (i, 0))
    shape = jax.ShapeDtypeStruct((rows, cols), F32)
    return pl.pallas_call(
        body, grid=(rows // tr,),
        in_specs=[pl.BlockSpec((tr, N_DEV), lambda i: (i, 0)), pl.BlockSpec((N_DEV, cols), lambda i: (0, 0)), spec, spec, spec],
        out_specs=[spec] * 4, out_shape=[shape] * 4, name="ada_update", compiler_params=_params(("parallel",)),
    )(sc_t, dmod_cols, w, m, v)


BIG = ("w_in", "w_branch_a", "w_branch_b", "w_out", "w_mlp_in", "w_mlp_out")
COLUMN_SHARDED = ("w_in", "w_branch_a", "w_branch_b", "w_mlp_in")
AS_TRANSPOSE = ("w_in",)
WEIGHTS = ("w_ada", "b_ada", "norm1_gain", "w_in", "lb_logits", "hgrn_o_gain", "q_norm_gain", "k_norm_gain", "sinks",
           "w_branch_a", "w_branch_b", "w_out", "norm2_gain", "w_mlp_in", "w_mlp_out")


def _to_bf16(name, w, transposed, tile=256):
    rows, cols = w.shape

    def body(w_ref, o_ref):
        v = w_ref[...]
        o_ref[...] = (v.T if transposed else v).astype(BF16)

    out_spec = pl.BlockSpec((cols, tile), lambda i: (0, i)) if transposed else pl.BlockSpec((tile, cols), lambda i: (i, 0))
    return pl.pallas_call(
        body, grid=(rows // tile,), in_specs=[pl.BlockSpec((tile, cols), lambda i: (i, 0))], out_specs=out_spec,
        out_shape=jax.ShapeDtypeStruct((cols, rows) if transposed else (rows, cols), BF16), name=name,
        compiler_params=_params(("parallel",)))(w)


def _pack_small(p):
    lb = p["lb_logits"]
    src = dict(p, lb0=lb[0:1], lb1=lb[1:2])
    return jnp.concatenate([jnp.pad(src[nm], ((0, 0), (0, w - src[nm].shape[1]))) for nm, w in SMALL_SEGS], axis=1)


def _unpack_small(vec, shapes):
    so = _offsets(SMALL_SEGS)
    out = {}
    for nm, shp in shapes.items():
        if nm == "lb_logits":
            o = so["lb0"][0]
            out[nm] = vec[0, o:o + 2 * AW].reshape(2, AW)
        else:
            o = so[nm][0]
            out[nm] = vec[:, o:o + shp[1]]
    return out


def kernel(x, c, w_ada, b_ada, norm1_gain, w_in, lb_logits, hgrn_o_gain, q_norm_gain, k_norm_gain, sinks, w_branch_a, w_branch_b, w_out, norm2_gain, w_mlp_in, w_mlp_out, loss_target, m_w_ada, m_b_ada, m_norm1_gain, m_w_in, m_lb_logits, m_hgrn_o_gain, m_q_norm_gain, m_k_norm_gain, m_sinks, m_w_branch_a, m_w_branch_b, m_w_out, m_norm2_gain, m_w_mlp_in, m_w_mlp_out, v_w_ada, v_b_ada, v_norm1_gain, v_w_in, v_lb_logits, v_hgrn_o_gain, v_q_norm_gain, v_k_norm_gain, v_sinks, v_w_branch_a, v_w_branch_b, v_w_out, v_norm2_gain, v_w_mlp_in, v_w_mlp_out):
    w = dict(w_ada=w_ada, b_ada=b_ada, norm1_gain=norm1_gain, w_in=w_in, lb_logits=lb_logits, hgrn_o_gain=hgrn_o_gain,
             q_norm_gain=q_norm_gain, k_norm_gain=k_norm_gain, sinks=sinks, w_branch_a=w_branch_a, w_branch_b=w_branch_b,
             w_out=w_out, norm2_gain=norm2_gain, w_mlp_in=w_mlp_in, w_mlp_out=w_mlp_out)
    m = dict(w_ada=m_w_ada, b_ada=m_b_ada, norm1_gain=m_norm1_gain, w_in=m_w_in, lb_logits=m_lb_logits,
             hgrn_o_gain=m_hgrn_o_gain, q_norm_gain=m_q_norm_gain, k_norm_gain=m_k_norm_gain, sinks=m_sinks,
             w_branch_a=m_w_branch_a, w_branch_b=m_w_branch_b, w_out=m_w_out, norm2_gain=m_norm2_gain,
             w_mlp_in=m_w_mlp_in, w_mlp_out=m_w_mlp_out)
    v = dict(w_ada=v_w_ada, b_ada=v_b_ada, norm1_gain=v_norm1_gain, w_in=v_w_in, lb_logits=v_lb_logits,
             hgrn_o_gain=v_hgrn_o_gain, q_norm_gain=v_q_norm_gain, k_norm_gain=v_k_norm_gain, sinks=v_sinks,
             w_branch_a=v_w_branch_a, w_branch_b=v_w_branch_b, w_out=v_w_out, norm2_gain=v_norm2_gain,
             w_mlp_in=v_w_mlp_in, w_mlp_out=v_w_mlp_out)
    for d in (w, m, v):
        for nm in ("w_ada",) + BIG:
            d[nm] = d[nm][0]
    px, py, pc = _mesh_pos()
    me = _index((px, py, pc))
    c_arr = jnp.reshape(pc, (1,)).astype(jnp.int32)
    q_arr = jnp.reshape(2 * px + py, (1,)).astype(jnp.int32)

    shards = [_to_bf16("shard_" + nm, w[nm].T, False, w[nm].shape[1] // 4) if nm in AS_TRANSPOSE else
              _to_bf16("shard_" + nm, w[nm], nm in COLUMN_SHARDED) for nm in BIG]
    b_shard = lax.dynamic_slice(b_ada, (0, me * ADA_W), (1, ADA_W))
    mod, sc_all = _ada_mod(c, w["w_ada"], b_shard)

    grad, delta, new_m, new_v = {}, {}, {}, {}

    def update(nm, s, recvs, job=None):
        if nm in AS_TRANSPOSE:
            res = _sum_adamw("adamw_" + nm, s, recvs, q_arr, w[nm].T, m[nm].T, v[nm].T, False, w[nm].shape[1] // 4, job=job)
        else:
            res = _sum_adamw("adamw_" + nm, s, recvs, q_arr, w[nm], m[nm], v[nm], nm in COLUMN_SHARDED, 128, job=job)
        res, job_res = res if job is not None else (res, [])
        res = [t.T for t in res] if nm in AS_TRANSPOSE else res
        grad[nm], delta[nm], new_m[nm], new_v[nm] = res
        return job_res

    dx, sums, parts = _local_step(x[0], loss_target[0], mod, norm1_gain, norm2_gain, lb_logits, hgrn_o_gain,
                                  q_norm_gain, k_norm_gain, sinks, shards, c_arr, update)
    for nm in BIG:
        if nm not in grad:
            update(nm, *sums[nm])

    allx, g_small, loss = _small_reduce(parts, lb_logits)

    dmod_cols = lax.dynamic_slice(allx[:, 0, :], (0, me * ADA_W), (N_DEV, ADA_W))
    grad["w_ada"], delta["w_ada"], new_m["w_ada"], new_v["w_ada"] = _ada_update(
        sc_all.T, dmod_cols, w["w_ada"], m["w_ada"], v["w_ada"], 256)

    small_names = [nm for nm in WEIGHTS if nm not in BIG and nm != "w_ada"]
    shapes = {nm: w[nm].shape for nm in small_names}
    ds, ms, vs = _adamw("adamw_small", _pack_small(w), g_small, _pack_small(m), _pack_small(v), 1)
    for dst, vec in ((grad, g_small), (delta, ds), (new_m, ms), (new_v, vs)):
        dst.update(_unpack_small(vec, shapes))

    def full(d, nm):
        return d[nm][None] if nm in BIG or nm == "w_ada" else d[nm]

    return (loss[0, 0], dx[None], *[full(grad, nm) for nm in WEIGHTS], *[full(delta, nm) for nm in WEIGHTS],
            *[full(new_m, nm) for nm in WEIGHTS], *[full(new_v, nm) for nm in WEIGHTS])
```

```python
import functools

import jax
import jax.numpy as jnp
from jax import lax
from jax.experimental import pallas as pl
from jax.experimental.pallas import tpu as pltpu

F32 = jnp.float32
BF16 = jnp.bfloat16
MESH = pl.DeviceIdType.MESH

N_DEV = 8
D = 2048
A_HEADS, A_HD, CHUNK = 8, 128, 64
AW = A_HEADS * A_HD
Q_HEADS, KV_HEADS, GROUP, B_HD, BLK = 16, 4, 4, 64, 128
BW = Q_HEADS * B_HD
KVW = KV_HEADS * B_HD
HID = 4 * D
IN_W = 4 * AW + BW + 2 * KVW + 2 * D
OFF_QA, OFF_FA, OFF_IA, OFF_GA = 0, AW, 2 * AW, 3 * AW
OFF_QB = 4 * AW
OFF_KB = OFF_QB + BW
OFF_VB = OFF_KB + KVW
OFF_GTA = OFF_VB + KVW
OFF_GTB = OFF_GTA + D
N_MOD = 6
EPS = 1e-6
LR, B1, B2, ADAM_EPS, WD, STEP = 1e-3, 0.9, 0.999, 1e-8, 0.01, 10
NEG = -1e30

VMEM_LIMIT = 56 * 1024 * 1024
MI_CUT = 544

NN = (((1,), (0,)), ((), ()))
NT = (((1,), (1,)), ((), ()))
TN = (((0,), (0,)), ((), ()))
BNN = (((2,), (1,)), ((0,), (0,)))
BNT = (((2,), (2,)), ((0,), (0,)))
BTN = (((1,), (1,)), ((0,), (0,)))


def _dot(a, b, dims=NN):
    return lax.dot_general(a.astype(BF16), b.astype(BF16), dims, preferred_element_type=F32)


def _params(sem):
    return pltpu.CompilerParams(dimension_semantics=sem, vmem_limit_bytes=VMEM_LIMIT)


def _sigmoid(x):
    return jax.nn.sigmoid(x)


def _fold8(v):
    r, n = v.shape
    return jnp.sum(v.reshape(r // 8, 8, n), axis=0)


_VMEM = pl.BlockSpec(memory_space=pltpu.VMEM)
_ANY = pl.BlockSpec(memory_space=pl.ANY)
_SEMS = lambda n: pltpu.SemaphoreType.DMA((n,))


def _mesh_pos():
    return lax.axis_index("x"), lax.axis_index("y"), lax.axis_index("c")


def _flip(pos, k):
    return tuple(1 - p if (k >> s) & 1 else p for p, s in zip(pos, (2, 1, 0)))


def _index(pos):
    return 4 * pos[0] + 2 * pos[1] + pos[2]


class _Job:
    def __init__(self, ins, out_shape, sems, start, finish, aliases=None, middle=None):
        self.ins, self.out_shape, self.sems, self.start, self.finish = list(ins), list(out_shape), list(sems), start, finish
        self.aliases = dict(aliases or {})
        self.middle = middle


def _both(j1, j2):
    assert j1.middle is None and j2.middle is None
    n_in, n_out, n_sem = len(j1.ins), len(j1.out_shape), len(j1.sems)
    aliases = dict(j1.aliases, **{n_in + i: n_out + o for i, o in j2.aliases.items()})
    first = lambda ins, outs, sems: (ins[:n_in], outs[:n_out], sems[:n_sem])
    second = lambda ins, outs, sems: (ins[n_in:], outs[n_out:], sems[n_sem:])

    def start(*refs):
        j1.start(*first(*refs))
        j2.start(*second(*refs))

    def finish(*refs):
        j1.finish(*first(*refs))
        j2.finish(*second(*refs))

    return _Job(j1.ins + j2.ins, j1.out_shape + j2.out_shape, j1.sems + j2.sems, start, finish, aliases)


def _pcall(body, *, grid, in_specs, out_specs, out_shape, scratch_shapes, name, semantics, args, job=None, prefetch=()):
    n_pre = len(prefetch)

    def call(fn, in_specs_, out_specs_, out_shape_, scratch_, sem, operands, aliases):
        if n_pre:
            spec = pltpu.PrefetchScalarGridSpec(num_scalar_prefetch=n_pre, grid=grid, in_specs=in_specs_,
                                                out_specs=out_specs_, scratch_shapes=scratch_)
            return pl.pallas_call(fn, grid_spec=spec, out_shape=out_shape_, name=name, input_output_aliases=aliases,
                                  compiler_params=_params(sem))(*prefetch, *operands)
        return pl.pallas_call(fn, grid=grid, in_specs=in_specs_, out_specs=out_specs_, out_shape=out_shape_,
                              scratch_shapes=scratch_, name=name, input_output_aliases=aliases,
                              compiler_params=_params(sem))(*operands)

    if job is None:
        return list(call(body, in_specs, out_specs, out_shape, scratch_shapes, semantics, args, {})), []
    n_in, n_out, n_scr = len(in_specs), len(out_specs), len(scratch_shapes)
    j_in, j_out = len(job.ins), len(job.out_shape)
    steps = tuple(grid)

    def carrier(*refs):
        pre, refs = refs[:n_pre], refs[n_pre:]
        o = 0
        main_in, o = refs[o:o + n_in], o + n_in
        job_in, o = refs[o:o + j_in], o + j_in
        main_out, o = refs[o:o + n_out], o + n_out
        job_out, o = refs[o:o + j_out], o + j_out
        main_scr, job_sems = refs[o:o + n_scr], refs[o + n_scr:]
        ids = [pl.program_id(a) for a in range(len(steps))]
        first = functools.reduce(lambda p, q: p & q, [i == 0 for i in ids])
        last = functools.reduce(lambda p, q: p & q, [i == s - 1 for i, s in zip(ids, steps)])

        @pl.when(first)
        def _():
            job.start(job_in, job_out, job_sems)

        if job.middle is not None:
            flat, total = 0, 1
            for i, s in zip(ids, steps):
                flat, total = flat * s + i, total * s

            @pl.when(flat == total * 3 // 5)
            def _():
                job.middle(job_in, job_out, job_sems)

        body(*pre, *main_in, *main_out, *main_scr)

        @pl.when(last)
        def _():
            job.finish(job_in, job_out, job_sems)

    outs = call(carrier, list(in_specs) + [_ANY] * j_in, list(out_specs) + [_ANY] * j_out,
                list(out_shape) + job.out_shape, list(scratch_shapes) + job.sems, ("arbitrary",) * len(steps),
                list(args) + job.ins, {n_pre + n_in + i: n_out + o for i, o in job.aliases.items()})
    return list(outs[:n_out]), list(outs[n_out:])


def _gather_relay_job(shards, rows=None, into=None, alone=False):
    n = len(shards)
    rows = rows or [(0, s.shape[0]) for s in shards]
    into = into or [None] * n
    olds, aliases = [], {}
    for a, buf in enumerate(into):
        if buf is not None:
            aliases[n + len(olds)] = a
            olds.append(buf)

    def tools(ins, outs, sems):
        send_sems, recv_sems, local_sems = sems
        x, y, c = _mesh_pos()
        q = 2 * x + y
        chip_at = lambda rel: (1 - x if rel & 2 else x, 1 - y if rel & 1 else y)

        def part(a, chip, core):
            rs, (r0, r1) = shards[a].shape[0], rows[a]
            return outs[a].at[pl.ds((2 * chip + core) * rs + r0, r1 - r0), :]

        own = lambda a: ins[a].at[pl.ds(rows[a][0], rows[a][1] - rows[a][0]), :]

        def copy(a, slot, chip, core, to, src=None):
            blk = part(a, chip, core)
            return pltpu.make_async_remote_copy(src_ref=blk if src is None else src, dst_ref=blk,
                                                send_sem=send_sems.at[7 * a + slot], recv_sem=recv_sems.at[7 * a + slot],
                                                device_id=to, device_id_type=MESH)

        mine = [pltpu.make_async_copy(own(a), part(a, q, c), local_sems.at[a]) for a in range(n)]
        first = [copy(a, slot, q, c, (x, y, 1 - c) if slot == 0 else (*chip_at(slot), c), src=own(a))
                 for a in range(n) for slot in (0, 1, 2)]
        return x, y, c, q, chip_at, copy, mine, first

    def start(ins, outs, sems):
        *_, mine, first = tools(ins, outs, sems)
        for cp in mine + first:
            cp.start()

    def middle(ins, outs, sems):
        x, y, c, q, chip_at, copy, _, _ = tools(ins, outs, sems)
        me, sib = (x, y, c), (x, y, 1 - c)

        def relay(src, dst):
            for a in range(n):
                copy(a, src, q ^ src, c, me).wait_recv()
                copy(a, 3, q ^ src, c, (*chip_at(dst), c)).start()
                copy(a, 3 + src, q ^ src, c, sib).start()
            for a in range(n):
                copy(a, dst, q ^ dst, c, me).wait_recv()
                copy(a, 3 + dst, q ^ dst, c, sib).start()

        pl.when(c == 1)(lambda: relay(1, 2))
        pl.when(c == 0)(lambda: relay(2, 1))

    def finish(ins, outs, sems):
        if alone:
            middle(ins, outs, sems)
        x, y, c, q, chip_at, copy, mine, first = tools(ins, outs, sems)
        me, sib = (x, y, c), (x, y, 1 - c)
        for a in range(n):
            copy(a, 3, q ^ 3, c, me).wait_recv()
            copy(a, 6, q ^ 3, c, sib).start()
        for a in range(n):
            copy(a, 0, q, 1 - c, me).wait_recv()
            for rel in (1, 2, 3):
                copy(a, 3 + rel, q ^ rel, 1 - c, me).wait_recv()
        for a in range(n):
            for slot in range(3, 7):
                copy(a, slot, q, c, sib).wait_send()
        for cp in first:
            cp.wait_send()
        for cp in mine:
            cp.wait()

    return _Job(list(shards) + olds, [jax.ShapeDtypeStruct((N_DEV * s.shape[0], s.shape[1]), s.dtype) for s in shards],
                [_SEMS(7 * n), _SEMS(7 * n), _SEMS(n)], start, finish, aliases, middle=None if alone else middle)


def _pair_job(grads):
    n = len(grads)

    def copies(ins, outs, sems):
        send_sems, recv_sems = sems
        x, y, c = _mesh_pos()
        out = []
        for a in range(n):
            rs = grads[a].shape[0] // N_DEV
            for q in range(4):
                blk = ins[a].at[pl.ds((2 * q + 1 - c) * rs, rs), :]
                out.append(pltpu.make_async_remote_copy(
                    src_ref=blk, dst_ref=outs[a].at[q], send_sem=send_sems.at[4 * a + q], recv_sem=recv_sems.at[4 * a + q],
                    device_id=(x, y, 1 - c), device_id_type=MESH))
        return out

    def start(ins, outs, sems):
        for cp in copies(ins, outs, sems):
            cp.start()

    def finish(ins, outs, sems):
        for cp in copies(ins, outs, sems):
            cp.wait()

    return _Job(grads, [jax.ShapeDtypeStruct((4, g.shape[0] // N_DEV, g.shape[1]), g.dtype) for g in grads],
                [_SEMS(4 * n), _SEMS(4 * n)], start, finish)


def _chip_job(sums, rels=(1, 2, 3), rows=None, into=None):
    n, nr = len(sums), len(rels)
    r0, r1 = rows or (0, sums[0].shape[1])
    olds = list(into or [])
    aliases = {n + a: a for a in range(len(olds))}

    def copies(ins, outs, sems):
        send_sems, recv_sems = sems
        x, y, c = _mesh_pos()
        out = []
        for a in range(n):
            for slot, r in enumerate(rels):
                px, py = (1 - x if r & 2 else x), (1 - y if r & 1 else y)
                out.append(pltpu.make_async_remote_copy(
                    src_ref=ins[a].at[2 * px + py, pl.ds(r0, r1 - r0), :], dst_ref=outs[a].at[slot, pl.ds(r0, r1 - r0), :],
                    send_sem=send_sems.at[nr * a + slot], recv_sem=recv_sems.at[nr * a + slot],
                    device_id=(px, py, c), device_id_type=MESH))
        return out

    def start(ins, outs, sems):
        for cp in copies(ins, outs, sems):
            cp.start()

    def finish(ins, outs, sems):
        for cp in copies(ins, outs, sems):
            cp.wait()

    return _Job(list(sums) + olds, [jax.ShapeDtypeStruct((nr,) + s.shape[1:], s.dtype) for s in sums],
                [_SEMS(nr * n), _SEMS(nr * n)], start, finish, aliases)


def _mm(name, form, a_list, b, M, N, K, tm, tn, tk, extras, outs, epi, job=None):
    nI, nJ, nK = M // tm, N // tn, K // tk
    assert nI * tm == M and nJ * tn == N and nK * tk == K
    dims = {"nn": NN, "nt": NT, "tn": TN}[form]
    b_list = b if isinstance(b, list) else [(b, {"nn": N, "nt": K, "tn": N}[form])]
    nA, nB = len(a_list), len(b_list)
    assert nA == 1 or nB == 1
    assert nB == 1 or form in ("nn", "nt")
    AXIS = {"i": 0, "j": 1, "k": 2}
    a_axis, a_tile = ("i", tm) if form == "tn" else ("k", tk)
    b_axis, b_tile = ("k", tk) if form == "nt" else ("j", tn)

    def cut(pieces, tile, total):
        starts, s = [], 0
        for _, w in pieces:
            assert w % tile == 0
            starts.append(s // tile)
            s += w
        assert s == total
        return starts, [w // tile for _, w in pieces]

    a_st, a_cn = cut(a_list, a_tile, M if form == "tn" else K)
    b_st, b_cn = cut(b_list, b_tile, K if form == "nt" else N)

    def inside(idx, st, cn):
        return (idx >= st) & (idx < st + cn)

    def a_spec(p):
        st, cn = a_st[p], a_cn[p]
        if form == "tn":
            return pl.BlockSpec((tk, tm), lambda i, j, k: (jnp.where(inside(i, st, cn), k, 0), jnp.clip(i - st, 0, cn - 1)))
        return pl.BlockSpec((tm, tk), lambda i, j, k: (i, jnp.clip(k - st, 0, cn - 1)))

    def b_spec(p):
        st, cn = b_st[p], b_cn[p]
        if form == "nt":
            return pl.BlockSpec((tn, tk), lambda i, j, k: (j, jnp.clip(k - st, 0, cn - 1)))
        if nB == 1:
            return pl.BlockSpec((tk, tn), lambda i, j, k: (k, j))
        return pl.BlockSpec((tk, tn), lambda i, j, k: (jnp.where(inside(j, st, cn), k, 0), jnp.clip(j - st, 0, cn - 1)))

    in_specs = ([a_spec(p) for p in range(nA)] + [b_spec(p) for p in range(nB)]
                + [pl.BlockSpec(bs, im) for _, bs, im in extras])
    out_shape = [jax.ShapeDtypeStruct(s_, d_) for s_, d_, _, _ in outs]
    out_specs = [pl.BlockSpec(bs, im) for _, _, bs, im in outs]
    nE, nO = len(extras), len(outs)
    single = nA == 1 and nB == 1

    def body(*refs):
        a_refs, b_refs = refs[:nA], refs[nA:nA + nB]
        ex, ou = refs[nA + nB:nA + nB + nE], refs[nA + nB + nE:nA + nB + nE + nO]
        ids = [pl.program_id(a) for a in range(3)]

        def partial_of(p, q):
            return lax.dot_general(a_refs[p][...], b_refs[q][...], dims, preferred_element_type=F32)

        if nK == 1 and single:
            epi(partial_of(0, 0), ex, ou)
            return
        acc = refs[-1]
        k = ids[2]
        for p in range(nA):
            for q in range(nB):
                def first(p=p, q=q):
                    acc[...] = partial_of(p, q)

                def later(p=p, q=q):
                    acc[...] += partial_of(p, q)

                here = None
                if nA > 1:
                    here = inside(ids[AXIS[a_axis]], a_st[p], a_cn[p])
                if nB > 1:
                    here = inside(ids[AXIS[b_axis]], b_st[q], b_cn[q])
                pl.when(k == 0 if here is None else here & (k == 0))(first)
                pl.when(k > 0 if here is None else here & (k > 0))(later)

        @pl.when(k == nK - 1)
        def _():
            epi(acc[...], ex, ou)

    scratch = [] if (nK == 1 and single) else [pltpu.VMEM((tm, tn), F32)]
    res, job_res = _pcall(
        body, grid=(nI, nJ, nK), in_specs=in_specs, out_specs=out_specs, out_shape=out_shape, scratch_shapes=scratch,
        name=name, semantics=("parallel", "parallel", "arbitrary"),
        args=[a for a, _ in a_list] + [p for p, _ in b_list] + [e for e, _, _ in extras], job=job)
    return res if job is None else (res, job_res)


def _twin_mm(name, form, pairs, M, N, K, tm, tn, tk, out_dtype):
    nI, nJ, nK = M // tm, N // tn, K // tk
    dims = {"nn": NN, "tn": TN}[form]
    a_spec = (pl.BlockSpec((tm, tk), lambda i, j, k: (i, k)) if form == "nn" else pl.BlockSpec((tk, tm), lambda i, j, k: (k, i)))
    b_spec = pl.BlockSpec((tk, tn), lambda i, j, k: (k, j))
    o_spec = pl.BlockSpec((tm, tn), lambda i, j, k: (i, j))

    def body(a1, b1, a2, b2, o1, o2, *accs):
        k = pl.program_id(2)
        for a_ref, b_ref, o_ref, acc in ((a1, b1, o1, accs[0] if accs else None), (a2, b2, o2, accs[1] if accs else None)):
            part = lax.dot_general(a_ref[...], b_ref[...], dims, preferred_element_type=F32)
            if nK == 1:
                o_ref[...] = part.astype(out_dtype)
                continue

            @pl.when(k == 0)
            def _(acc=acc, part=part):
                acc[...] = part

            @pl.when(k > 0)
            def _(acc=acc, part=part):
                acc[...] += part

            @pl.when(k == nK - 1)
            def _(acc=acc, o_ref=o_ref):
                o_ref[...] = acc[...].astype(out_dtype)

    (a1, b1), (a2, b2) = pairs
    shape = jax.ShapeDtypeStruct((M, N), out_dtype)
    return pl.pallas_call(
        body, grid=(nI, nJ, nK), in_specs=[a_spec, b_spec, a_spec, b_spec], out_specs=[o_spec, o_spec],
        out_shape=[shape, shape], scratch_shapes=[] if nK == 1 else [pltpu.VMEM((tm, tn), F32)] * 2, name=name,
        compiler_params=_params(("parallel", "parallel", "arbitrary")))(a1, b1, a2, b2)


def _piece_tiles(pieces, tile):
    starts, s = [], 0
    for _, w in pieces:
        assert w % tile == 0
        starts.append(s // tile)
        s += w
    return starts, [w // tile for _, w in pieces], s


def _pieces_tn(name, pieces, b, tile, job=None):
    T, N = b.shape
    st, cn, M = _piece_tiles(pieces, tile)
    nP, nI = len(pieces), M // tile

    def body(*refs):
        p_refs, b_hbm, o_ref = refs[:nP], refs[nP], refs[nP + 1]
        bbuf, abuf, bsem, asem = refs[nP + 2:]
        i = pl.program_id(0)

        def fetch(step, slot):
            for p in range(nP):
                @pl.when((step >= st[p]) & (step < st[p] + cn[p]))
                def _():
                    col = pl.multiple_of((step - st[p]) * tile, tile)
                    pltpu.make_async_copy(p_refs[p].at[pl.ds(0, T), pl.ds(col, tile)], abuf.at[slot], asem.at[slot]).start()

        @pl.when(i == 0)
        def _():
            whole = pltpu.make_async_copy(b_hbm, bbuf, bsem)
            whole.start()
            fetch(0, 0)
            whole.wait()

        @pl.when(i + 1 < nI)
        def _():
            fetch(i + 1, (i + 1) % 2)

        pltpu.make_async_copy(p_refs[0].at[pl.ds(0, T), pl.ds(0, tile)], abuf.at[i % 2], asem.at[i % 2]).wait()
        o_ref[...] = lax.dot_general(abuf[i % 2], bbuf[...], TN, preferred_element_type=F32).astype(BF16)

    res, job_res = _pcall(
        body, grid=(nI,), in_specs=[_ANY] * (nP + 1), out_specs=[pl.BlockSpec((tile, N), lambda i: (i, 0))],
        out_shape=[jax.ShapeDtypeStruct((M, N), BF16)],
        scratch_shapes=[pltpu.VMEM((T, N), b.dtype), pltpu.VMEM((2, T, tile), b.dtype), pltpu.SemaphoreType.DMA, _SEMS(2)],
        name=name, semantics=("arbitrary",), args=[p for p, _ in pieces] + [b], job=job)
    return res if job is None else (res, job_res)


def _rows_mm(name, pieces, w, T, tm, tk, vecs, bufs, parts, epi, job=None):
    st, cn, K = _piece_tiles(pieces, tk)
    nP, nI, nK = len(pieces), T // tm, K // tk
    part_specs = [pl.BlockSpec(bs, lambda i, k, im=im: im(i, 0, k)) for _, _, bs, im in parts]
    n_vec, nB = len(vecs), len(bufs)
    load_ix = [n for n, (_, src, _) in enumerate(bufs) if src is not None]
    store_ix = [n for n, (_, _, store) in enumerate(bufs) if store]
    n_any_in, n_any_out = len(load_ix), len(store_ix)

    def body(*refs):
        o = nP
        p_refs, w_ref = refs[:nP], refs[o]
        vec_refs = refs[o + 1:o + 1 + n_vec]
        ins = refs[o + 1 + n_vec:o + 1 + n_vec + n_any_in]
        o = o + 1 + n_vec + n_any_in
        hbm_outs, p_outs = refs[o:o + n_any_out], refs[o + n_any_out:o + n_any_out + len(parts)]
        o = o + n_any_out + len(parts)
        acc, abuf = refs[o:o + 2]
        buf_refs = refs[o + 2:o + 2 + nB]
        asem, in_sems, out_sems = refs[-3:]
        i, k = pl.program_id(0), pl.program_id(1)
        g = i * nK + k
        rows_of = lambda ref, ii: ref.at[pl.ds(pl.multiple_of(ii * tm, tm), tm), :]
        bufs_in = [buf_refs[n] for n in load_ix]
        bufs_out = [buf_refs[n] for n in store_ix]

        def fetch(ii, kk, slot):
            for p in range(nP):
                @pl.when((kk >= st[p]) & (kk < st[p] + cn[p]))
                def _():
                    col = pl.multiple_of((kk - st[p]) * tk, tk)
                    src = p_refs[p].at[pl.ds(pl.multiple_of(ii * tm, tm), tm), pl.ds(col, tk)]
                    pltpu.make_async_copy(src, abuf.at[slot], asem.at[slot]).start()

        loads = lambda ii: [pltpu.make_async_copy(rows_of(src, ii), buf, in_sems.at[n])
                            for n, (src, buf) in enumerate(zip(ins, bufs_in))]
        stores = lambda ii: [pltpu.make_async_copy(buf, rows_of(dst, ii), out_sems.at[n])
                             for n, (buf, dst) in enumerate(zip(bufs_out, hbm_outs))]

        @pl.when(g == 0)
        def _():
            fetch(0, 0, 0)

        @pl.when(g + 1 < nI * nK)
        def _():
            last_k = k == nK - 1
            fetch(jnp.where(last_k, i + 1, i), jnp.where(last_k, 0, k + 1), (g + 1) % 2)

        @pl.when(k == 0)
        def _():
            @pl.when(i > 0)
            def _():
                for cp in stores(i - 1):
                    cp.wait()
            for cp in loads(i):
                cp.start()

        pltpu.make_async_copy(p_refs[0].at[pl.ds(0, tm), pl.ds(0, tk)], abuf.at[g % 2], asem.at[g % 2]).wait()

        def product(cols):
            return jnp.dot(abuf[g % 2], w_ref[:, cols], preferred_element_type=F32)

        col_blocks = [slice(c0, c0 + 512) for c0 in range(0, D, 512)]

        @pl.when(k == 0)
        def _():
            for cols in col_blocks:
                acc[:, cols] = product(cols)

        @pl.when(k > 0)
        def _():
            for cols in col_blocks:
                acc[:, cols] += product(cols)

        @pl.when(k == nK - 1)
        def _():
            for cp in loads(i):
                cp.wait()
            epi(acc, vec_refs, buf_refs, p_outs)
            for cp in stores(i):
                cp.start()

            @pl.when(i == nI - 1)
            def _():
                for cp in stores(i):
                    cp.wait()

    vec = pl.BlockSpec((1, D), lambda i, k: (0, 0))
    scratch = ([pltpu.VMEM((tm, D), F32), pltpu.VMEM((2, tm, tk), BF16)] + [pltpu.VMEM((tm, D), dt) for dt, _, _ in bufs]
               + [_SEMS(2), _SEMS(n_any_in), _SEMS(n_any_out)])
    res, job_res = _pcall(
        body, grid=(nI, nK),
        in_specs=[_ANY] * nP + [pl.BlockSpec((tk, D), lambda i, k: (k, 0))] + [vec] * n_vec + [_ANY] * n_any_in,
        out_specs=[_ANY] * n_any_out + part_specs,
        out_shape=([jax.ShapeDtypeStruct((T, D), bufs[n][0]) for n in store_ix]
                   + [jax.ShapeDtypeStruct(s, d) for s, d, _, _ in parts]),
        scratch_shapes=scratch, name=name, semantics=("arbitrary", "arbitrary"),
        args=[p for p, _ in pieces] + [w] + list(vecs) + [bufs[n][1] for n in load_ix], job=job)
    return res if job is None else (res, job_res)


def _pieces_nn_rms(name, pieces, w, x, gain, sc, dres, tm, tk, job=None):
    _, outs, epi = _rms_mod_bwd_epilogue(x, gain, sc, dres, tm)

    def on_rows(acc, vecs, bufs, parts):
        epi(acc, [bufs[0], vecs[0], vecs[1], bufs[1]], [bufs[1], *parts])

    return _rows_mm(name, pieces, w, x.shape[0], tm, tk, [gain, sc], [(F32, x, False), (F32, dres, True)],
                    outs[1:], on_rows, job=job)


def _rms_mod_fwd(name, x, gain, sc, sh, tr, job=None):
    T = x.shape[0]

    def body(x_ref, g_ref, sc_ref, sh_ref, h_ref):
        xv = x_ref[...]
        rstd = lax.rsqrt(jnp.mean(xv * xv, axis=-1, keepdims=True) + EPS)
        h_ref[...] = ((xv * rstd * g_ref[...]) * (1.0 + sc_ref[...]) + sh_ref[...]).astype(BF16)

    row = pl.BlockSpec((tr, D), lambda i: (i, 0))
    vec = pl.BlockSpec((1, D), lambda i: (0, 0))
    return _pcall(body, grid=(T // tr,), in_specs=[row, vec, vec, vec], out_specs=[row],
                  out_shape=[jax.ShapeDtypeStruct((T, D), BF16)], scratch_shapes=[], name=name, semantics=("parallel",),
                  args=[x, gain, sc, sh], job=job)


def _rms_mod_bwd_epilogue(x, gain, sc, dres, tm, gate=None, mo=None):
    T = x.shape[0]
    with_gate = gate is not None
    row = ((tm, D), lambda i, j, k: (i, 0))
    vec = ((1, D), lambda i, j, k: (0, 0))
    part = ((T // tm * 8, D), F32, (8, D), lambda i, j, k: (i, 0))
    extras = [(x, *row), (gain, *vec), (sc, *vec), (dres, *row)]
    outs = [((T, D), F32, *row), part, part, part]
    if with_gate:
        extras += [(gate, *vec), (mo, *row)]
        outs += [((T, D), BF16, *row), part]

    rows = min(64, tm)

    def epi(acc, ex, ou):
        g = ex[1][...]
        sums = [jnp.zeros((8, D), F32) for _ in range(4)]
        for r0 in range(0, tm, rows):
            rs = slice(r0, r0 + rows)
            dhv, xv = acc[rs, :], ex[0][rs, :]
            rstd = lax.rsqrt(jnp.mean(xv * xv, axis=-1, keepdims=True) + EPS)
            xhat = xv * rstd
            dn = dhv * (1.0 + ex[2][...])
            dxhat = dn * g
            dx = ex[3][rs, :] + rstd * (dxhat - xhat * jnp.mean(dxhat * xhat, axis=-1, keepdims=True))
            ou[0][rs, :] = dx
            terms = [dhv, dhv * (xhat * g), dn * xhat]
            if with_gate:
                terms.append(dx * ex[5][rs, :].astype(F32))
                ou[4][rs, :] = (ex[4][...] * dx).astype(BF16)
            sums = [s + _fold8(t) for s, t in zip(sums, terms)] + sums[len(terms):]
        ou[1][...], ou[2][...], ou[3][...] = sums[:3]
        if with_gate:
            ou[5][...] = sums[3]

    return extras, outs, epi


def _rms_mod_bwd(name, dh, x, gain, sc, dres, tr, gate=None, mo=None, job=None):
    T = x.shape[0]
    extras, outs, epi = _rms_mod_bwd_epilogue(x, gain, sc, dres, tr, gate, mo)
    rows_only = lambda im: (lambda i: im(i, 0, 0))
    nE = len(extras)

    def body(dh_ref, *refs):
        epi(dh_ref, refs[:nE], refs[nE:])

    return _pcall(
        body, grid=(T // tr,),
        in_specs=[pl.BlockSpec((tr, D), lambda i: (i, 0))] + [pl.BlockSpec(bs, rows_only(im)) for _, bs, im in extras],
        out_specs=[pl.BlockSpec(bs, rows_only(im)) for _, _, bs, im in outs],
        out_shape=[jax.ShapeDtypeStruct(s, d) for s, d, _, _ in outs], scratch_shapes=[], name=name,
        semantics=("parallel",), args=[dh] + [e for e, _, _ in extras], job=job)


def _split3(v):
    h = v.astype(BF16)
    r1 = v - h.astype(F32)
    m = r1.astype(BF16)
    lo = (r1 - m.astype(F32)).astype(BF16)
    return h, m, lo


def _tri_mm(tri, v, dims=NN):
    h, m, lo = _split3(v)
    t = tri.astype(BF16)
    mm = lambda p: lax.dot_general(t, p, dims, preferred_element_type=F32)
    return (mm(lo) + mm(m)) + mm(h)


def _hgrn_chunk_terms(q, fl, lb):
    sig = _sigmoid(fl)
    f = lb + (1.0 - lb) * sig
    lf = jnp.log(f)
    kk = 1.0 - f
    sq = _sigmoid(q)
    qf = q * sq
    return sig, f, lf, kk, sq, qf


def _causal(n):
    r = lax.broadcasted_iota(jnp.int32, (n, n), 0)
    c = lax.broadcasted_iota(jnp.int32, (n, n), 1)
    return r >= c


def _hgrn_fwd(proj, lb_logits, o_gain, tt, job=None):
    T = proj.shape[0]
    nT, ncl = T // tt, tt // CHUNK
    C = CHUNK

    def body(q_ref, f_ref, i_ref, g_ref, lbl_ref, og_ref, y_ref, st_ref, S):
        @pl.when(pl.program_id(1) == 0)
        def _():
            S[...] = jnp.zeros_like(S)

        lbl = lbl_ref[...]
        lb = _sigmoid(lbl[0:1, :] - lbl[1:2, :])
        og = og_ref[...]
        shp = (ncl, C, A_HD)
        q, fl, v, g = (r[...].reshape(shp) for r in (q_ref, f_ref, i_ref, g_ref))
        tri = jnp.broadcast_to(_causal(C), (ncl, C, C))
        _, _, lf, kk, _, qf = _hgrn_chunk_terms(q, fl, lb)
        b = _tri_mm(tri, lf, BNN)
        bm, bl = b[:, C // 2 - 1:C // 2, :], b[:, C - 1:C, :]
        qd, kd = qf * jnp.exp(b - bm), kk * jnp.exp(bm - b)
        A = jnp.where(tri, _dot(qd, kd, BNT), 0.0)
        d_st = _dot(v, kk * jnp.exp(bl - b), BTN)
        dec = jnp.exp(bl)
        st = S[...]
        for ci in range(ncl):
            st_ref[0, ci] = st
            st = st * dec[ci] + d_st[ci]
        S[...] = st
        o = _dot(A, v, BNN) + _dot(qf * jnp.exp(b), st_ref[0], BNT)
        r = lax.rsqrt(jnp.mean(o * o, axis=-1, keepdims=True) + EPS)
        y_ref[...] = (o * r * og * (g * _sigmoid(g))).astype(BF16).reshape(tt, A_HD)

    def col(off):
        return pl.BlockSpec((tt, A_HD), lambda h, t: (t, off // A_HD + h))

    head_vec = lambda rows: pl.BlockSpec((rows, A_HD), lambda h, t: (0, h))
    return _pcall(
        body, grid=(A_HEADS, nT),
        in_specs=[col(OFF_QA), col(OFF_FA), col(OFF_IA), col(OFF_GA), head_vec(2), head_vec(1)],
        out_specs=[pl.BlockSpec((tt, A_HD), lambda h, t: (t, h)),
                   pl.BlockSpec((1, ncl, A_HD, A_HD), lambda h, t: (h, t, 0, 0))],
        out_shape=[jax.ShapeDtypeStruct((T, AW), BF16),
                   jax.ShapeDtypeStruct((A_HEADS, T // C, A_HD, A_HD), F32)],
        scratch_shapes=[pltpu.VMEM((A_HD, A_HD), F32)], name="hgrn_fwd", semantics=("parallel", "arbitrary"),
        args=[proj, proj, proj, proj, lb_logits, o_gain], job=job)


def _hgrn_bwd(proj, st, dy, lb_logits, o_gain, tt, job=None):
    T = proj.shape[0]
    nT, ncl = T // tt, tt // CHUNK
    C = CHUNK

    def body(q_ref, f_ref, i_ref, g_ref, st_ref, dy_ref, lbl_ref, og_ref,
             dq_ref, df_ref, di_ref, dg_ref, plb_ref, pog_ref, dS):
        @pl.when(pl.program_id(1) == 0)
        def _():
            dS[...] = jnp.zeros_like(dS)

        lbl = lbl_ref[...]
        lb = _sigmoid(lbl[0:1, :] - lbl[1:2, :])
        og = og_ref[...]
        shp = (ncl, C, A_HD)
        flat = lambda t: t.reshape(tt, A_HD)
        q, fl, v, g, dout = (r[...].reshape(shp) for r in (q_ref, f_ref, i_ref, g_ref, dy_ref))
        tri = jnp.broadcast_to(_causal(C), (ncl, C, C))
        rowi = lax.broadcasted_iota(jnp.int32, shp, 1)
        st0 = st_ref[0]
        sig, f, lf, kk, sq, qf = _hgrn_chunk_terms(q, fl, lb)
        b = _tri_mm(tri, lf, BNN)
        bm, bl = b[:, C // 2 - 1:C // 2, :], b[:, C - 1:C, :]
        e_qd, e_kd, e_ke, e_b = jnp.exp(b - bm), jnp.exp(bm - b), jnp.exp(bl - b), jnp.exp(b)
        qd, kd, ke, qe = qf * e_qd, kk * e_kd, kk * e_ke, qf * e_b
        dec = jnp.exp(bl)
        A = jnp.where(tri, _dot(qd, kd, BNT), 0.0)
        o = _dot(A, v, BNN) + _dot(qe, st0, BNT)
        r = lax.rsqrt(jnp.mean(o * o, axis=-1, keepdims=True) + EPS)
        sg = _sigmoid(g)
        on = o * r * og
        dg_ref[...] = flat((dout * on * (sg * (1.0 + g * (1.0 - sg)))).astype(BF16))
        don = dout * (g * sg)
        pog_ref[...] = _fold8(flat(don * o * r))
        dyh = don * og
        do = r * (dyh - o * (r * r) * jnp.mean(dyh * o, axis=-1, keepdims=True))
        g_st = _dot(do, qe, BTN)
        run = dS[...]
        after = [None] * ncl
        for ci in reversed(range(ncl)):
            after[ci] = run
            run = g_st[ci] + run * dec[ci]
        dS[...] = run
        d_after = jnp.stack(after, axis=0)
        ddec = jnp.sum(d_after * st0, axis=1, keepdims=True)
        dqe = _dot(do, st0, BNN)
        dke = _dot(v, d_after, BNN)
        dA = jnp.where(tri, _dot(do, v, BNT), 0.0)
        dv = _dot(ke, d_after, BNT) + _dot(A, do, BTN)
        dqd = _dot(dA, kd, BNN)
        dkd = _dot(dA, qd, BTN)
        di_ref[...] = flat(dv.astype(BF16))
        dqf = dqe * e_b + dqd * e_qd
        dkk = dkd * e_kd + dke * e_ke
        t_qd, t_kd, t_ke = dqd * qd, dkd * kd, dke * ke
        db = dqe * qe + t_qd - t_kd - t_ke
        dbm = jnp.sum(t_kd - t_qd, axis=1, keepdims=True)
        dbl = jnp.sum(t_ke, axis=1, keepdims=True) + ddec * dec
        db = db + jnp.where(rowi == C // 2 - 1, dbm, 0.0) + jnp.where(rowi == C - 1, dbl, 0.0)
        dlf = _tri_mm(tri, db, BTN)
        dfv = dlf / f - dkk
        df_ref[...] = flat((dfv * (1.0 - lb) * sig * (1.0 - sig)).astype(BF16))
        plb_ref[...] = _fold8(flat(dfv * (1.0 - sig)))
        dq_ref[...] = flat((dqf * (sq * (1.0 + q * (1.0 - sq)))).astype(BF16))

    def col(off):
        return pl.BlockSpec((tt, A_HD), lambda h, t: (nT - 1 - t, off // A_HD + h))

    head_vec = lambda rows: pl.BlockSpec((rows, A_HD), lambda h, t: (0, h))
    o_spec = pl.BlockSpec((tt, A_HD), lambda h, t: (nT - 1 - t, h))
    p_spec = pl.BlockSpec((8, A_HD), lambda h, t: (t, h))
    o_shape = jax.ShapeDtypeStruct((T, AW), BF16)
    p_shape = jax.ShapeDtypeStruct((nT * 8, AW), F32)
    return _pcall(
        body, grid=(A_HEADS, nT),
        in_specs=[col(OFF_QA), col(OFF_FA), col(OFF_IA), col(OFF_GA),
                  pl.BlockSpec((1, ncl, A_HD, A_HD), lambda h, t: (h, nT - 1 - t, 0, 0)),
                  pl.BlockSpec((tt, A_HD), lambda h, t: (nT - 1 - t, h)), head_vec(2), head_vec(1)],
        out_specs=[o_spec, o_spec, o_spec, o_spec, p_spec, p_spec],
        out_shape=[o_shape, o_shape, o_shape, o_shape, p_shape, p_shape],
        scratch_shapes=[pltpu.VMEM((A_HD, A_HD), F32)], name="hgrn_bwd", semantics=("parallel", "arbitrary"),
        args=[proj, proj, proj, proj, st, dy, lb_logits, o_gain], job=job)


LANES = 128
Q_COLS = BW // LANES


def _low_half():
    return lax.broadcasted_iota(jnp.int32, (1, LANES), 1) < B_HD


def _half_sum(t, low):
    lo = jnp.sum(jnp.where(low, t, 0.0), axis=-1, keepdims=True)
    hi = jnp.sum(jnp.where(low, 0.0, t), axis=-1, keepdims=True)
    return jnp.where(low, lo, hi)


def _half_rms(t, low):
    r = lax.rsqrt(_half_sum(t * t, low) * (1.0 / B_HD) + EPS)
    return t * r, r


def _fold_halves(p, low):
    return jnp.where(low, p + pltpu.roll(p, B_HD, 1), 0.0)


def _stack_cols(x):
    return jnp.stack([x[:, c * LANES:(c + 1) * LANES] for c in range(Q_COLS)], axis=0).reshape(KV_HEADS, 2 * BLK, LANES)


def _col_of(t, c):
    return t[c // 2, (c % 2) * BLK:(c % 2 + 1) * BLK]


def _split_halves(col, s, low):
    own = jnp.where(low if s == 0 else jnp.logical_not(low), col, 0.0)
    other = pltpu.roll(own, B_HD, 1)
    return (own, other) if s == 0 else (other, own)


def _swa_keys(kp_ref, kc_ref, vp_ref, vc_ref, kg, low):
    k_lo, k_hi, v_lo, v_hi, hats = [], [], [], [], []
    for j in range(KVW // LANES):
        cs = slice(j * LANES, (j + 1) * LANES)
        k_hat, k_r = _half_rms(jnp.concatenate([kp_ref[:, cs], kc_ref[:, cs]], axis=0), low)
        vcol = jnp.concatenate([vp_ref[:, cs], vc_ref[:, cs]], axis=0)
        hats.append((k_hat, k_r))
        for s in range(2):
            for dst_lo, dst_hi, col in ((k_lo, k_hi, k_hat * kg), (v_lo, v_hi, vcol)):
                lo, hi = _split_halves(col, s, low)
                dst_lo.append(lo)
                dst_hi.append(hi)
    st = lambda parts: jnp.stack(parts, axis=0)
    return st(k_lo), st(k_hi), st(v_lo), st(v_hi), hats


def _swa_mask(first_block):
    qi = lax.broadcasted_iota(jnp.int32, (BLK, 2 * BLK), 0) + BLK
    ki = lax.broadcasted_iota(jnp.int32, (BLK, 2 * BLK), 1)
    rel = qi - ki
    m = (rel >= 0) & (rel < BLK) & (jnp.logical_not(first_block) | (ki >= BLK))
    return jnp.concatenate([m, m], axis=0)


def _sink_cols(sk_ref, hi):
    top = lax.broadcasted_iota(jnp.int32, (2 * BLK, 1), 0) < BLK
    return jnp.stack([jnp.where(top, sk_ref[0, GROUP * hk + hi], sk_ref[0, GROUP * hk + 2 + hi])
                      for hk in range(KV_HEADS)], axis=0)


def _swa_probs(qn, k_half, sink, mask):
    s = jnp.where(mask, _dot(qn, k_half, BNT) * (B_HD ** -0.5), NEG)
    m = jnp.maximum(jnp.max(s, axis=-1, keepdims=True), sink)
    p = jnp.exp(s - m)
    ps = jnp.exp(sink - m)
    inv = 1.0 / (jnp.sum(p, axis=-1, keepdims=True) + ps)
    return p * inv, ps * inv


def _swa_fwd(proj, q_gain, k_gain, sinks, job=None):
    T = proj.shape[0]
    nb = T // BLK

    def body(q_ref, kc_ref, kp_ref, vc_ref, vp_ref, qg_ref, kg_ref, sk_ref, o_ref):
        low = _low_half()
        mask = _swa_mask(pl.program_id(0) == 0)
        qn = _half_rms(_stack_cols(q_ref[...]), low)[0] * qg_ref[...]
        k_lo, k_hi, v_lo, v_hi, _ = _swa_keys(kp_ref, kc_ref, vp_ref, vc_ref, kg_ref[...], low)
        p_lo, _ = _swa_probs(qn, k_lo, _sink_cols(sk_ref, 0), mask)
        p_hi, _ = _swa_probs(qn, k_hi, _sink_cols(sk_ref, 1), mask)
        o = (_dot(p_lo, v_lo, BNN) + _dot(p_hi, v_hi, BNN)).astype(BF16)
        for c in range(Q_COLS):
            o_ref[:, c * LANES:(c + 1) * LANES] = _col_of(o, c)

    q_gain, k_gain = jnp.tile(q_gain, (1, 2)), jnp.tile(k_gain, (1, 2))
    cur = lambda w, off: pl.BlockSpec((BLK, w), lambda i: (i, off // w))
    prev = lambda w, off: pl.BlockSpec((BLK, w), lambda i: (jnp.maximum(i - 1, 0), off // w))
    small = lambda n: pl.BlockSpec((1, 2 * n), lambda i: (0, 0))
    return _pcall(
        body, grid=(nb,),
        in_specs=[cur(BW, OFF_QB), cur(KVW, OFF_KB), prev(KVW, OFF_KB), cur(KVW, OFF_VB), prev(KVW, OFF_VB),
                  small(B_HD), small(B_HD), pl.BlockSpec(memory_space=pltpu.SMEM)],
        out_specs=[pl.BlockSpec((BLK, BW), lambda i: (i, 0))],
        out_shape=[jax.ShapeDtypeStruct((T, BW), BF16)], scratch_shapes=[], name="swa_fwd", semantics=("parallel",),
        args=[proj, proj, proj, proj, proj, q_gain, k_gain, sinks], job=job)


def _swa_bwd(proj, dout, q_gain, k_gain, sinks, job=None):
    T = proj.shape[0]
    nb = T // BLK
    W = BW + 2 * KVW

    def body(q_ref, kc_ref, kp_ref, vc_ref, vp_ref, do_ref, qg_ref, kg_ref, sk_ref,
             dq_ref, dkv_ref, pqg_ref, pkg_ref, psk_ref, dkn_c, dv_c):
        i = pl.program_id(0)
        live = i < nb
        low = _low_half()
        high = jnp.logical_not(low)
        qg, kg = qg_ref[...], kg_ref[...]
        mask = _swa_mask(i == 0)
        lane = lax.broadcasted_iota(jnp.int32, (1, LANES), 1)
        scale = B_HD ** -0.5

        @pl.when(i == 0)
        def _():
            dkn_c[...] = jnp.zeros_like(dkn_c)
            dv_c[...] = jnp.zeros_like(dv_c)

        q_hat, q_r = _half_rms(_stack_cols(q_ref[...]), low)
        qn = q_hat * qg
        k_lo, k_hi, v_lo, v_hi, hats = _swa_keys(kp_ref, kc_ref, vp_ref, vc_ref, kg, low)
        do = _stack_cols(do_ref[...])
        dqn = jnp.zeros((KV_HEADS, 2 * BLK, LANES), F32)
        acc_sk = jnp.zeros((1, LANES), F32)
        dk_parts, dv_parts = [], []
        for hi, (k_h, v_h) in enumerate(((k_lo, v_lo), (k_hi, v_hi))):
            p, ps = _swa_probs(qn, k_h, _sink_cols(sk_ref, hi), mask)
            dp = _dot(do, v_h, BNT)
            delta = jnp.sum(p * dp, axis=-1, keepdims=True)
            ds = p * (dp - delta) * scale
            dqn = dqn + _dot(ds, k_h, BNN)
            dk_parts.append(_dot(ds, qn, BTN))
            dv_parts.append(_dot(p, do, BTN))
            t = ps * delta
            for hk in range(KV_HEADS):
                for rows in range(2):
                    h = GROUP * hk + 2 * rows + hi
                    acc_sk = acc_sk + jnp.where(
                        lane == h, -jnp.sum(t[hk, rows * BLK:(rows + 1) * BLK], axis=0, keepdims=True), 0.0)
        dqh = dqn * qg
        dq = (q_r * (dqh - q_hat * (_half_sum(dqh * q_hat, low) * (1.0 / B_HD)))).astype(BF16)
        for c in range(Q_COLS):
            dq_ref[:, c * LANES:(c + 1) * LANES] = _col_of(dq, c)
        acc_qg = _fold_halves(_fold8((dqn * q_hat).reshape(KV_HEADS * 2 * BLK, LANES)), low)

        def native(parts, j):
            lo_arr, hi_arr = parts
            a, b = 2 * j, 2 * j + 1
            return (jnp.where(low, lo_arr[a], 0.0) + pltpu.roll(jnp.where(high, hi_arr[a], 0.0), B_HD, 1)
                    + jnp.where(high, hi_arr[b], 0.0) + pltpu.roll(jnp.where(low, lo_arr[b], 0.0), B_HD, 1))

        acc_kg = jnp.zeros((8, LANES), F32)
        for j in range(KVW // LANES):
            cs = slice(j * LANES, (j + 1) * LANES)
            dkn = jnp.where(live, native(dk_parts, j), 0.0)
            dvc = jnp.where(live, native(dv_parts, j), 0.0)
            kp_hat, kp_r = hats[j][0][:BLK], hats[j][1][:BLK]
            dkn_prev = dkn_c[:, cs] + dkn[:BLK]
            dv_prev = dv_c[:, cs] + dvc[:BLK]
            acc_kg = acc_kg + _fold8(dkn_prev * kp_hat)
            dkh = dkn_prev * kg
            dkv_ref[:, cs] = (kp_r * (dkh - kp_hat * (_half_sum(dkh * kp_hat, low) * (1.0 / B_HD)))).astype(BF16)
            dkv_ref[:, KVW + j * LANES:KVW + (j + 1) * LANES] = dv_prev.astype(BF16)
            dkn_c[:, cs] = dkn[BLK:]
            dv_c[:, cs] = dvc[BLK:]
        keep = jnp.where(i > 0, 1.0, 0.0)
        pqg_ref[...] = jnp.where(live, acc_qg, 0.0)
        pkg_ref[...] = _fold_halves(acc_kg, low) * keep
        psk_ref[...] = jnp.broadcast_to(jnp.where(live, acc_sk, 0.0), (8, LANES)) * (
            lax.broadcasted_iota(jnp.int32, (8, LANES), 0) == 0).astype(F32)

    q_gain, k_gain = jnp.tile(q_gain, (1, 2)), jnp.tile(k_gain, (1, 2))
    last = nb - 1
    cur = lambda w, off: pl.BlockSpec((BLK, w), lambda i: (jnp.minimum(i, last), off // w))
    prev = lambda w, off: pl.BlockSpec((BLK, w), lambda i: (jnp.maximum(i - 1, 0), off // w))
    small = lambda n: pl.BlockSpec((1, 2 * n), lambda i: (0, 0))
    part = pl.BlockSpec((8, 128), lambda i: (i, 0))
    p_shape = jax.ShapeDtypeStruct(((nb + 1) * 8, 128), F32)
    return _pcall(
        body, grid=(nb + 1,),
        in_specs=[cur(BW, OFF_QB), cur(KVW, OFF_KB), prev(KVW, OFF_KB), cur(KVW, OFF_VB), prev(KVW, OFF_VB),
                  pl.BlockSpec((BLK, BW), lambda i: (jnp.minimum(i, last), 0)), small(B_HD), small(B_HD),
                  pl.BlockSpec(memory_space=pltpu.SMEM)],
        out_specs=[pl.BlockSpec((BLK, BW), lambda i: (i, 0)),
                   pl.BlockSpec((BLK, 2 * KVW), lambda i: (jnp.maximum(i - 1, 0), 0)), part, part, part],
        out_shape=[jax.ShapeDtypeStruct((T + BLK, BW), BF16), jax.ShapeDtypeStruct((T, 2 * KVW), BF16),
                   p_shape, p_shape, p_shape],
        scratch_shapes=[pltpu.VMEM((BLK, KVW), F32), pltpu.VMEM((BLK, KVW), F32)], name="swa_bwd",
        semantics=("arbitrary",), args=[proj, proj, proj, proj, proj, dout, q_gain, k_gain, sinks], job=job)


def _branch_merge(ya_pre, attn, wa_t, wb_t, proj, tm, tn, job=None):
    T = ya_pre.shape[0]

    def body(a_ref, b_ref, wa_ref, wb_ref, ga_ref, gb_ref, ya_ref, yb_ref, mg_ref):
        ya = lax.dot_general(a_ref[...], wa_ref[...], NT, preferred_element_type=F32)
        yb = lax.dot_general(b_ref[...], wb_ref[...], NT, preferred_element_type=F32)
        ya_ref[...] = ya.astype(BF16)
        yb_ref[...] = yb.astype(BF16)
        mg_ref[...] = (_sigmoid(ga_ref[...]) * ya + _sigmoid(gb_ref[...]) * yb).astype(BF16)

    o_spec = pl.BlockSpec((tm, tn), lambda i, j: (i, j))
    o_shape = jax.ShapeDtypeStruct((T, D), BF16)
    return _pcall(
        body, grid=(T // tm, D // tn),
        in_specs=[pl.BlockSpec((tm, AW), lambda i, j: (i, 0)), pl.BlockSpec((tm, BW), lambda i, j: (i, 0)),
                  pl.BlockSpec((tn, AW), lambda i, j: (j, 0)), pl.BlockSpec((tn, BW), lambda i, j: (j, 0)),
                  pl.BlockSpec((tm, tn), lambda i, j: (i, OFF_GTA // tn + j)),
                  pl.BlockSpec((tm, tn), lambda i, j: (i, OFF_GTB // tn + j))],
        out_specs=[o_spec, o_spec, o_spec], out_shape=[o_shape, o_shape, o_shape], scratch_shapes=[], name="branch_merge",
        semantics=("parallel", "parallel"), args=[ya_pre, attn, wa_t, wb_t, proj, proj], job=job)


def _ij(i, j, k):
    return (i, j)


def _local_step(x, tgt, mod, g1, g2, lbl, og, qg, kg, sk, shards, c_arr, update):
    win_s, wa_s, wb_s, wout_s, wmi_s, wmo_s = shards
    T = x.shape[0]
    tm, tr, tt = min(1024, T), min(256, T), min(2048, T)
    tk_t = min(1024, T)
    tn = 512
    sh1, sc1, gt1, sh2, sc2, gt2 = (mod[:, i * D:(i + 1) * D] for i in range(N_MOD))
    nI = T // tm
    blk = (tm, tn)

    (h,), (win_t,) = _rms_mod_fwd("rms1_fwd", x, g1, sc1, sh1, tr, job=_gather_relay_job([win_s], alone=True))

    def epi_store(acc, ex, ou):
        ou[0][...] = acc.astype(ou[0].dtype)

    tm2 = min(2048, T)
    blk2 = (tm2, tn)

    full = lambda s: (0, s.shape[0])
    last = wmi_s.shape[0]
    gather = _gather_relay_job
    (proj,), (wa_t, wb_t, w_out, wmi_part) = _mm(
        "in_proj", "nt", [(h, D)], win_t, T, IN_W, D, tm2, tn, D, [], [((T, IN_W), F32, blk2, _ij)], epi_store,
        job=gather([wa_s, wb_s, wout_s, wmi_s], rows=[full(wa_s), full(wb_s), full(wout_s), (0, MI_CUT)]))
    (ya_pre, st), _ = _hgrn_fwd(proj, lbl, og, tt)
    (attn,), (wmi_t,) = _swa_fwd(proj, qg, kg, sk, job=gather([wmi_s], rows=[(MI_CUT, last)], into=[wmi_part]))
    (ya, yb, merged), _ = _branch_merge(ya_pre, attn, wa_t, wb_t, proj, tm, tn)

    def residual_rows(acc, vecs, bufs, parts):
        gt, gain, sc, sh = (v[...] for v in vecs)
        x_buf, mo_buf, h2_buf = bufs
        rows = min(64, tm)
        for r0 in range(0, tm, rows):
            rs = slice(r0, r0 + rows)
            z = acc[rs, :]
            mo_buf[rs, :] = z.astype(BF16)
            x1v = x_buf[rs, :] + gt * z
            x_buf[rs, :] = x1v
            rstd = lax.rsqrt(jnp.mean(x1v * x1v, axis=-1, keepdims=True) + EPS)
            h2_buf[rs, :] = ((x1v * rstd * gain) * (1.0 + sc) + sh).astype(BF16)

    x1, mo, h2 = _rows_mm("out_proj", [(merged, D)], w_out, T, tm, min(1024, D), [gt1, g2, sc2, sh2],
                          [(F32, x, True), (BF16, None, True), (BF16, None, True)], [], residual_rows)

    def epi_relu2(acc, ex, ou):
        r = jnp.maximum(acc, 0.0)
        ou[0][...] = r.astype(BF16)
        ou[1][...] = (r * r).astype(BF16)

    (r, a), (w_mo,) = _mm("mlp_in", "nt", [(h2, D)], wmi_t, T, HID, D, tm2, tn, D, [],
                          [((T, HID), BF16, blk2, _ij), ((T, HID), BF16, blk2, _ij)], epi_relu2,
                          job=gather([wmo_s]))

    def loss_rows(acc, vecs, bufs, parts):
        gt = vecs[0][...]
        x1_buf, t_buf, dz_buf = bufs
        rows = min(64, tm)
        loss_sum, gate_sum = jnp.zeros((8, D), F32), jnp.zeros((8, D), F32)
        for r0 in range(0, tm, rows):
            rs = slice(r0, r0 + rows)
            z = acc[rs, :]
            e = x1_buf[rs, :] + gt * z - t_buf[rs, :]
            dy = e * (1.0 / D)
            t_buf[rs, :] = dy
            dz_buf[rs, :] = (gt * dy).astype(BF16)
            loss_sum = loss_sum + _fold8(e * e)
            gate_sum = gate_sum + _fold8(dy * z)
        parts[0][...] = loss_sum * (0.5 / D)
        parts[1][...] = gate_sum

    part_rows = ((nI * 8, D), F32, (8, D), lambda i, j, k: (i, 0))
    dy, dz, p_loss, p_gt2 = _rows_mm(
        "mlp_out", [(a, HID)], w_mo, T, tm, 1024, [gt2], [(F32, x1, False), (F32, tgt, True), (BF16, None, True)],
        [part_rows, part_rows], loss_rows)

    def epi_du(acc, ex, ou):
        ou[0][...] = (acc * (2.0 * ex[0][...].astype(F32))).astype(BF16)

    (du,) = _mm("mlp_out_dx", "nt", [(dz, D)], w_mo, T, HID, D, tm2, tn, D, [(r, blk2, _ij)],
                [((T, HID), BF16, blk2, _ij)], epi_du)
    gblk = (1024, 1024)
    gwide = (1024, D)
    pair_sum = lambda nm, g, r1: _pair_sum("pair_sum_" + nm, g, r1, c_arr, _sum_rows(r1.shape[1]))
    (g_mo,) = _mm("mlp_out_dw", "tn", [(a, HID)], dz, HID, D, T, 1024, D, tk_t, [], [((HID, D), BF16, gwide, _ij)], epi_store)
    (dh2,), (r1_mo,) = _mm("mlp_in_dx", "nn", [(du, HID)], wmi_t, T, D, HID, tm, D, 1024, [],
                           [((T, D), F32, (tm, D), _ij)], epi_store, job=_pair_job([g_mo]))
    s_mo = pair_sum("mlp_out", g_mo, r1_mo)
    near, far = (1, 2), (3,)
    cut = s_mo.shape[1] // 2
    (g_mi,), (rn_mo, rf_mo) = _mm(
        "mlp_in_dw", "tn", [(du, HID)], h2, HID, D, T, 1024, D, tk_t, [], [((HID, D), BF16, gwide, _ij)], epi_store,
        job=_both(_chip_job([s_mo], near), _chip_job([s_mo], far, rows=(0, cut))))
    (dx1, p_sh2, p_sc2, p_g2, dmo, p_gt1), (r1_mi,) = _rms_mod_bwd(
        "rms2_bwd", dh2, x1, g2, sc2, dy, tr, gate=gt1, mo=mo, job=_pair_job([g_mi]))

    def epi_gates(acc, ex, ou):
        ya_ref, yb_ref, ga_ref, gb_ref = ex
        sa, sb = _sigmoid(ga_ref[...]), _sigmoid(gb_ref[...])
        ou[0][...] = (acc * sa).astype(BF16)
        ou[1][...] = (acc * sb).astype(BF16)
        ou[2][...] = (acc * ya_ref[...].astype(F32) * (sa * (1.0 - sa))).astype(BF16)
        ou[3][...] = (acc * yb_ref[...].astype(F32) * (sb * (1.0 - sb))).astype(BF16)

    o_bf = ((T, D), BF16, blk, _ij)
    (dya, dyb, dga, dgb), (rf_mo,) = _mm(
        "out_proj_dx", "nt", [(dmo, D)], w_out, T, D, D, tm, tn, D,
        [(ya, blk, _ij), (yb, blk, _ij), (proj, blk, lambda i, j, k: (i, OFF_GTA // tn + j)),
         (proj, blk, lambda i, j, k: (i, OFF_GTB // tn + j))], [o_bf, o_bf, o_bf, o_bf], epi_gates,
        job=_chip_job([s_mo], far, rows=(cut, 2 * cut), into=[rf_mo]))
    s_mi = pair_sum("mlp_in", g_mi, r1_mi)
    (g_out,) = _mm("out_proj_dw", "tn", [(merged, D)], dmo, D, D, T, 1024, 1024, tk_t, [], [((D, D), BF16, gblk, _ij)], epi_store)
    dya_pre, dattn = _twin_mm("branch_dx", "nn", [(dya, wa_t), (dyb, wb_t)], T, AW, D, tm, tn, D, F32)
    g_a, g_b = _twin_mm("branch_dw", "tn", [(dya, ya_pre), (dyb, attn)], D, AW, T, 1024, 1024, tk_t, BF16)
    (dqa, dfa, dia, dgg, p_lb, p_og), (rn_mi, r1_out, r1_a, r1_b) = _hgrn_bwd(
        proj, st, dya_pre, lbl, og, tt, job=_both(_chip_job([s_mi], near), _pair_job([g_out, g_a, g_b])))
    (dqb, dkv, p_qg, p_kg, p_sk), (rf_mi,) = _swa_bwd(proj, dattn, qg, kg, sk, job=_chip_job([s_mi], far))
    s_out, s_a, s_b = pair_sum("out", g_out, r1_out), pair_sum("branch_a", g_a, r1_a), pair_sum("branch_b", g_b, r1_b)
    pieces = [(dqa, AW), (dfa, AW), (dia, AW), (dgg, AW), (dqb, BW), (dkv, 2 * KVW), (dga, D), (dgb, D)]
    (g_in,), (r2_out, r2_a, r2_b) = _pieces_tn("in_proj_dw", pieces, h, 512, job=_chip_job([s_out, s_a, s_b]))
    (r1_in,) = update("w_mlp_in", s_mi, [rn_mi, rf_mi], job=_pair_job([g_in]))
    s_in = pair_sum("in", g_in, r1_in)
    (dx, p_sh1, p_sc1, p_g1), (r2_in,) = _pieces_nn_rms(
        "in_proj_dx", pieces, win_t, x, g1, sc1, dx1, tm, 512, job=_chip_job([s_in]))

    partials = dict(sh1=p_sh1, sc1=p_sc1, gt1=p_gt1, sh2=p_sh2, sc2=p_sc2, gt2=p_gt2, g1=p_g1, g2=p_g2,
                    lb=p_lb, og=p_og, qg=p_qg, kg=p_kg, sk=p_sk, loss=p_loss)
    sums = dict(w_in=(s_in, [r2_in]), w_branch_a=(s_a, [r2_a]), w_branch_b=(s_b, [r2_b]), w_out=(s_out, [r2_out]),
                w_mlp_in=(s_mi, [rn_mi, rf_mi]), w_mlp_out=(s_mo, [rn_mo, rf_mo]))
    return dx, sums, partials


def _exchange_slots(buf, send_sems, recv_sems):
    me = _mesh_pos()
    mine = buf.at[_index(me)]
    sends = []
    for k in range(1, N_DEV):
        cp = pltpu.make_async_remote_copy(src_ref=mine, dst_ref=mine, send_sem=send_sems.at[k - 1],
                                          recv_sem=recv_sems.at[k - 1], device_id=_flip(me, k), device_id_type=MESH)
        cp.start()
        sends.append(cp)
    for k in range(1, N_DEV):
        theirs = buf.at[_index(_flip(me, k))]
        pltpu.make_async_remote_copy(src_ref=theirs, dst_ref=theirs, send_sem=send_sems.at[k - 1],
                                     recv_sem=recv_sems.at[k - 1], device_id=_flip(me, k), device_id_type=MESH).wait_recv()
    for cp in sends:
        cp.wait_send()


ADA_W = N_MOD * D // N_DEV


def _ada_mod(c, w_ada, b_shard):
    def body(c_ref, w_ref, b_ref, mod_ref, sc_ref, cbuf, mbuf, s1, r1, s2, r2):
        me = _index(_mesh_pos())
        cbuf[me] = c_ref[...]
        _exchange_slots(cbuf, s1, r1)
        row = lax.broadcasted_iota(jnp.int32, (N_DEV, D), 0)
        call = jnp.zeros((N_DEV, D), F32)
        for d in range(N_DEV):
            call = jnp.where(row == d, cbuf[d], call)
        sc = call * _sigmoid(call)
        sc_ref[...] = sc
        mbuf[me] = _dot(sc, w_ref[...]) + b_ref[...]
        _exchange_slots(mbuf, s2, r2)
        for s in range(N_DEV):
            mod_ref[:, s * ADA_W:(s + 1) * ADA_W] = mbuf[s, pl.ds(me, 1), :]

    return pl.pallas_call(
        body, in_specs=[_VMEM, _VMEM, _VMEM], out_specs=[_VMEM, _VMEM],
        out_shape=[jax.ShapeDtypeStruct((1, N_MOD * D), F32), jax.ShapeDtypeStruct((N_DEV, D), F32)],
        scratch_shapes=[pltpu.VMEM((N_DEV, 1, D), F32), pltpu.VMEM((N_DEV, N_DEV, ADA_W), F32),
                        _SEMS(N_DEV - 1), _SEMS(N_DEV - 1), _SEMS(N_DEV - 1), _SEMS(N_DEV - 1)],
        name="ada_mod", compiler_params=pltpu.CompilerParams(vmem_limit_bytes=VMEM_LIMIT),
    )(c, w_ada, b_shard)


SMALL_SEGS = (("b_ada", N_MOD * D), ("norm1_gain", D), ("norm2_gain", D), ("lb0", AW), ("lb1", AW),
              ("hgrn_o_gain", AW), ("q_norm_gain", 128), ("k_norm_gain", 128), ("sinks", 128))
SMALL_W = sum(w for _, w in SMALL_SEGS)
X_SEGS = (("sh1", D), ("sc1", D), ("gt1", D), ("sh2", D), ("sc2", D), ("gt2", D), ("g1", D), ("g2", D),
          ("lb", AW), ("og", AW), ("qg", 128), ("kg", 128), ("sk", 128), ("loss", 128))
X_W = sum(w for _, w in X_SEGS)


def _offsets(segs):
    out, o = {}, 0
    for name, w in segs:
        out[name] = (o, w)
        o += w
    return out


def _small_reduce(parts, lb_logits):
    xo, so = _offsets(X_SEGS), _offsets(SMALL_SEGS)
    names = [nm for nm, _ in X_SEGS]

    def body(*refs):
        p_refs = dict(zip(names, refs[:len(names)]))
        lbl_ref, allx, gs_ref, loss_ref, send_sems, recv_sems = refs[len(names):]
        me = _index(_mesh_pos())
        for nm, (o, w) in xo.items():
            if nm == "loss":
                allx[me, :, o:o + w] = jnp.broadcast_to(jnp.sum(p_refs[nm][...]), (1, w))
            else:
                allx[me, :, o:o + w] = jnp.sum(p_refs[nm][...], axis=0, keepdims=True)
        _exchange_slots(allx, send_sems, recv_sems)
        tot = allx[0]
        for d in range(1, N_DEV):
            tot = tot + allx[d]
        seg = lambda nm: tot[:, xo[nm][0]:xo[nm][0] + xo[nm][1]]

        def put(nm, v):
            gs_ref[:, so[nm][0]:so[nm][0] + so[nm][1]] = v

        put("b_ada", tot[:, 0:N_MOD * D])
        put("norm1_gain", seg("g1"))
        put("norm2_gain", seg("g2"))
        lbl = lbl_ref[...]
        lb = _sigmoid(lbl[0:1, :] - lbl[1:2, :])
        dl0 = seg("lb") * lb * (1.0 - lb)
        put("lb0", dl0)
        put("lb1", -dl0)
        put("hgrn_o_gain", seg("og"))
        put("q_norm_gain", seg("qg"))
        put("k_norm_gain", seg("kg"))
        put("sinks", seg("sk"))
        loss_ref[...] = seg("loss")

    return pl.pallas_call(
        body, in_specs=[_VMEM] * (len(names) + 1), out_specs=[_VMEM, _VMEM, _VMEM],
        out_shape=[jax.ShapeDtypeStruct((N_DEV, 1, X_W), F32), jax.ShapeDtypeStruct((1, SMALL_W), F32),
                   jax.ShapeDtypeStruct((1, 128), F32)],
        scratch_shapes=[_SEMS(N_DEV - 1), _SEMS(N_DEV - 1)], name="small_reduce",
        compiler_params=pltpu.CompilerParams(vmem_limit_bytes=VMEM_LIMIT),
    )(*[parts[nm] for nm in names], lb_logits)


def _adamw_math(w, g, m, v):
    m = B1 * m + (1.0 - B1) * g
    v = B2 * v + (1.0 - B2) * (g * g)
    m_hat = m / (1.0 - B1 ** STEP)
    v_hat = v / (1.0 - B2 ** STEP)
    return -LR * (m_hat / (jnp.sqrt(v_hat) + ADAM_EPS) + WD * w), m, v


def _sum_rows(rs):
    return 256 if rs % 256 == 0 else rs // 2


def _pair_sum(name, g, recv, c_arr, tr):
    _, rs, cols = recv.shape
    blk = (1, tr, cols)

    def body(c_ref, g_ref, r_ref, o_ref):
        o_ref[...] = (g_ref[...].astype(F32) + r_ref[...].astype(F32)).astype(BF16)

    grid_spec = pltpu.PrefetchScalarGridSpec(
        num_scalar_prefetch=1, grid=(4, rs // tr),
        in_specs=[pl.BlockSpec(blk, lambda q, i, c: (2 * q + c[0], i, 0)), pl.BlockSpec(blk, lambda q, i, c: (q, i, 0))],
        out_specs=pl.BlockSpec(blk, lambda q, i, c: (q, i, 0)))
    return pl.pallas_call(body, grid_spec=grid_spec, out_shape=jax.ShapeDtypeStruct((4, rs, cols), BF16), name=name,
                          compiler_params=_params(("parallel", "parallel")))(c_arr, g.reshape(N_DEV, rs, cols), recv)


def _sum_adamw(name, sums, recvs, q_arr, w, m, v, transposed, tile, job=None):
    rows, cols = w.shape
    nR = len(recvs)

    def body(q_ref, s_ref, *refs):
        r_refs = refs[:nR]
        w_ref, m_ref, v_ref, g_ref, d_ref, nm_ref, nv_ref = refs[nR:]
        g = s_ref[0].astype(F32)
        for r_ref in r_refs:
            for slot in range(r_ref.shape[0]):
                g = g + r_ref[slot].astype(F32)
        g = g.T if transposed else g
        g_ref[...] = g
        d_ref[...], nm_ref[...], nv_ref[...] = _adamw_math(w_ref[...], g, m_ref[...], v_ref[...])

    if transposed:
        slab = lambda n, first: pl.BlockSpec((n, cols, tile), lambda i, q: (first(q), 0, i))
    else:
        slab = lambda n, first: pl.BlockSpec((n, tile, cols), lambda i, q: (first(q), i, 0))
    spec = pl.BlockSpec((tile, cols), lambda i, q: (i, 0))
    shape = jax.ShapeDtypeStruct((rows, cols), F32)
    res, job_res = _pcall(
        body, grid=(rows // tile,),
        in_specs=[slab(1, lambda q: q[0])] + [slab(r.shape[0], lambda q: 0) for r in recvs] + [spec] * 3,
        out_specs=[spec] * 4, out_shape=[shape] * 4, scratch_shapes=[], name=name, semantics=("parallel",),
        args=[sums, *recvs, w, m, v], job=job, prefetch=[q_arr])
    return res if job is None else (res, job_res)


def _adamw(name, w, g, m, v, tr):
    rows, cols = w.shape

    def body(w_ref, g_ref, m_ref, v_ref, d_ref, nm_ref, nv_ref):
        d_ref[...], nm_ref[...], nv_ref[...] = _adamw_math(w_ref[...], g_ref[...], m_ref[...], v_ref[...])

    spec = pl.BlockSpec((tr, cols), lambda i: (i, 0))
    shape = jax.ShapeDtypeStruct((rows, cols), F32)
    return pl.pallas_call(
        body, grid=(rows // tr,), in_specs=[spec] * 4, out_specs=[spec] * 3, out_shape=[shape] * 3, name=name,
        compiler_params=_params(("parallel",)),
    )(w, g, m, v)


def _ada_update(sc_t, dmod_cols, w, m, v, tr):
    rows, cols = w.shape

    def body(s_ref, d_ref, w_ref, m_ref, v_ref, g_ref, dl_ref, nm_ref, nv_ref):
        g = jnp.dot(s_ref[...], d_ref[...], precision=lax.Precision.HIGHEST, preferred_element_type=F32)
        g_ref[...] = g
        dl_ref[...], nm_ref[...], nv_ref[...] = _adamw_math(w_ref[...], g, m_ref[...], v_ref[...])

    spec = pl.BlockSpec((tr, cols), lambda i: (i, 0))
    shape = jax.ShapeDtypeStruct((rows, cols), F32)
    return pl.pallas_call(
        body, grid=(rows // tr,),
        in_specs=[pl.BlockSpec((tr, N_DEV), lambda i: (i, 0)), pl.BlockSpec((N_DEV, cols), lambda i: (0, 0)), spec, spec, spec],
        out_specs=[spec] * 4, out_shape=[shape] * 4, name="ada_update", compiler_params=_params(("parallel",)),
    )(sc_t, dmod_cols, w, m, v)


BIG = ("w_in", "w_branch_a", "w_branch_b", "w_out", "w_mlp_in", "w_mlp_out")
COLUMN_SHARDED = ("w_in", "w_branch_a", "w_branch_b", "w_mlp_in")
AS_TRANSPOSE = ("w_in",)
WEIGHTS = ("w_ada", "b_ada", "norm1_gain", "w_in", "lb_logits", "hgrn_o_gain", "q_norm_gain", "k_norm_gain", "sinks",
           "w_branch_a", "w_branch_b", "w_out", "norm2_gain", "w_mlp_in", "w_mlp_out")


def _to_bf16(name, w, transposed, tile=256):
    rows, cols = w.shape

    def body(w_ref, o_ref):
        v = w_ref[...]
        o_ref[...] = (v.T if transposed else v).astype(BF16)

    out_spec = pl.BlockSpec((cols, tile), lambda i: (0, i)) if transposed else pl.BlockSpec((tile, cols), lambda i: (i, 0))
    return pl.pallas_call(
        body, grid=(rows // tile,), in_specs=[pl.BlockSpec((tile, cols), lambda i: (i, 0))], out_specs=out_spec,
        out_shape=jax.ShapeDtypeStruct((cols, rows) if transposed else (rows, cols), BF16), name=name,
        compiler_params=_params(("parallel",)))(w)


def _pack_small(p):
    lb = p["lb_logits"]
    src = dict(p, lb0=lb[0:1], lb1=lb[1:2])
    return jnp.concatenate([jnp.pad(src[nm], ((0, 0), (0, w - src[nm].shape[1]))) for nm, w in SMALL_SEGS], axis=1)


def _unpack_small(vec, shapes):
    so = _offsets(SMALL_SEGS)
    out = {}
    for nm, shp in shapes.items():
        if nm == "lb_logits":
            o = so["lb0"][0]
            out[nm] = vec[0, o:o + 2 * AW].reshape(2, AW)
        else:
            o = so[nm][0]
            out[nm] = vec[:, o:o + shp[1]]
    return out


def kernel(x, c, w_ada, b_ada, norm1_gain, w_in, lb_logits, hgrn_o_gain, q_norm_gain, k_norm_gain, sinks, w_branch_a, w_branch_b, w_out, norm2_gain, w_mlp_in, w_mlp_out, loss_target, m_w_ada, m_b_ada, m_norm1_gain, m_w_in, m_lb_logits, m_hgrn_o_gain, m_q_norm_gain, m_k_norm_gain, m_sinks, m_w_branch_a, m_w_branch_b, m_w_out, m_norm2_gain, m_w_mlp_in, m_w_mlp_out, v_w_ada, v_b_ada, v_norm1_gain, v_w_in, v_lb_logits, v_hgrn_o_gain, v_q_norm_gain, v_k_norm_gain, v_sinks, v_w_branch_a, v_w_branch_b, v_w_out, v_norm2_gain, v_w_mlp_in, v_w_mlp_out):
    w = dict(w_ada=w_ada, b_ada=b_ada, norm1_gain=norm1_gain, w_in=w_in, lb_logits=lb_logits, hgrn_o_gain=hgrn_o_gain,
             q_norm_gain=q_norm_gain, k_norm_gain=k_norm_gain, sinks=sinks, w_branch_a=w_branch_a, w_branch_b=w_branch_b,
             w_out=w_out, norm2_gain=norm2_gain, w_mlp_in=w_mlp_in, w_mlp_out=w_mlp_out)
    m = dict(w_ada=m_w_ada, b_ada=m_b_ada, norm1_gain=m_norm1_gain, w_in=m_w_in, lb_logits=m_lb_logits,
             hgrn_o_gain=m_hgrn_o_gain, q_norm_gain=m_q_norm_gain, k_norm_gain=m_k_norm_gain, sinks=m_sinks,
             w_branch_a=m_w_branch_a, w_branch_b=m_w_branch_b, w_out=m_w_out, norm2_gain=m_norm2_gain,
             w_mlp_in=m_w_mlp_in, w_mlp_out=m_w_mlp_out)
    v = dict(w_ada=v_w_ada, b_ada=v_b_ada, norm1_gain=v_norm1_gain, w_in=v_w_in, lb_logits=v_lb_logits,
             hgrn_o_gain=v_hgrn_o_gain, q_norm_gain=v_q_norm_gain, k_norm_gain=v_k_norm_gain, sinks=v_sinks,
             w_branch_a=v_w_branch_a, w_branch_b=v_w_branch_b, w_out=v_w_out, norm2_gain=v_norm2_gain,
             w_mlp_in=v_w_mlp_in, w_mlp_out=v_w_mlp_out)
    for d in (w, m, v):
        for nm in ("w_ada",) + BIG:
            d[nm] = d[nm][0]
    px, py, pc = _mesh_pos()
    me = _index((px, py, pc))
    c_arr = jnp.reshape(pc, (1,)).astype(jnp.int32)
    q_arr = jnp.reshape(2 * px + py, (1,)).astype(jnp.int32)

    shards = [_to_bf16("shard_" + nm, w[nm].T, False, w[nm].shape[1] // 4) if nm in AS_TRANSPOSE else
              _to_bf16("shard_" + nm, w[nm], nm in COLUMN_SHARDED) for nm in BIG]
    b_shard = lax.dynamic_slice(b_ada, (0, me * ADA_W), (1, ADA_W))
    mod, sc_all = _ada_mod(c, w["w_ada"], b_shard)

    grad, delta, new_m, new_v = {}, {}, {}, {}

    def update(nm, s, recvs, job=None):
        if nm in AS_TRANSPOSE:
            res = _sum_adamw("adamw_" + nm, s, recvs, q_arr, w[nm].T, m[nm].T, v[nm].T, False, w[nm].shape[1] // 4, job=job)
        else:
            res = _sum_adamw("adamw_" + nm, s, recvs, q_arr, w[nm], m[nm], v[nm], nm in COLUMN_SHARDED, 128, job=job)
        res, job_res = res if job is not None else (res, [])
        res = [t.T for t in res] if nm in AS_TRANSPOSE else res
        grad[nm], delta[nm], new_m[nm], new_v[nm] = res
        return job_res

    dx, sums, parts = _local_step(x[0], loss_target[0], mod, norm1_gain, norm2_gain, lb_logits, hgrn_o_gain,
                                  q_norm_gain, k_norm_gain, sinks, shards, c_arr, update)
    for nm in BIG:
        if nm not in grad:
            update(nm, *sums[nm])

    allx, g_small, loss = _small_reduce(parts, lb_logits)

    dmod_cols = lax.dynamic_slice(allx[:, 0, :], (0, me * ADA_W), (N_DEV, ADA_W))
    grad["w_ada"], delta["w_ada"], new_m["w_ada"], new_v["w_ada"] = _ada_update(
        sc_all.T, dmod_cols, w["w_ada"], m["w_ada"], v["w_ada"], 256)

    small_names = [nm for nm in WEIGHTS if nm not in BIG and nm != "w_ada"]
    shapes = {nm: w[nm].shape for nm in small_names}
    ds, ms, vs = _adamw("adamw_small", _pack_small(w), g_small, _pack_small(m), _pack_small(v), 1)
    for dst, vec in ((grad, g_small), (delta, ds), (new_m, ms), (new_v, vs)):
        dst.update(_unpack_small(vec, shapes))

    def full(d, nm):
        return d[nm][None] if nm in BIG or nm == "w_ada" else d[nm]

    return (loss[0, 0], dx[None], *[full(grad, nm) for nm in WEIGHTS], *[full(delta, nm) for nm in WEIGHTS],
            *[full(new_m, nm) for nm in WEIGHTS], *[full(new_v, nm) for nm in WEIGHTS])
```

```python
import functools

import jax
import jax.numpy as jnp
from jax import lax
from jax.experimental import pallas as pl
from jax.experimental.pallas import tpu as pltpu

F32 = jnp.float32
BF16 = jnp.bfloat16
MESH = pl.DeviceIdType.MESH

N_DEV = 8
D = 2048
A_HEADS, A_HD, CHUNK = 8, 128, 64
AW = A_HEADS * A_HD
Q_HEADS, KV_HEADS, GROUP, B_HD, BLK = 16, 4, 4, 64, 128
BW = Q_HEADS * B_HD
KVW = KV_HEADS * B_HD
HID = 4 * D
IN_W = 4 * AW + BW + 2 * KVW + 2 * D
OFF_QA, OFF_FA, OFF_IA, OFF_GA = 0, AW, 2 * AW, 3 * AW
OFF_QB = 4 * AW
OFF_KB = OFF_QB + BW
OFF_VB = OFF_KB + KVW
OFF_GTA = OFF_VB + KVW
OFF_GTB = OFF_GTA + D
N_MOD = 6
EPS = 1e-6
LR, B1, B2, ADAM_EPS, WD, STEP = 1e-3, 0.9, 0.999, 1e-8, 0.01, 10
NEG = -1e30

VMEM_LIMIT = 56 * 1024 * 1024
MI_CUT = 544

NN = (((1,), (0,)), ((), ()))
NT = (((1,), (1,)), ((), ()))
TN = (((0,), (0,)), ((), ()))
BNN = (((2,), (1,)), ((0,), (0,)))
BNT = (((2,), (2,)), ((0,), (0,)))
BTN = (((1,), (1,)), ((0,), (0,)))


def _dot(a, b, dims=NN):
    return lax.dot_general(a.astype(BF16), b.astype(BF16), dims, preferred_element_type=F32)


def _params(sem):
    return pltpu.CompilerParams(dimension_semantics=sem, vmem_limit_bytes=VMEM_LIMIT)


def _sigmoid(x):
    return jax.nn.sigmoid(x)


def _fold8(v):
    r, n = v.shape
    return jnp.sum(v.reshape(r // 8, 8, n), axis=0)


_VMEM = pl.BlockSpec(memory_space=pltpu.VMEM)
_ANY = pl.BlockSpec(memory_space=pl.ANY)
_SEMS = lambda n: pltpu.SemaphoreType.DMA((n,))


def _mesh_pos():
    return lax.axis_index("x"), lax.axis_index("y"), lax.axis_index("c")


def _flip(pos, k):
    return tuple(1 - p if (k >> s) & 1 else p for p, s in zip(pos, (2, 1, 0)))


def _index(pos):
    return 4 * pos[0] + 2 * pos[1] + pos[2]


class _Job:
    def __init__(self, ins, out_shape, sems, start, finish, aliases=None, middle=None):
        self.ins, self.out_shape, self.sems, self.start, self.finish = list(ins), list(out_shape), list(sems), start, finish
        self.aliases = dict(aliases or {})
        self.middle = middle


def _both(j1, j2):
    assert j1.middle is None and j2.middle is None
    n_in, n_out, n_sem = len(j1.ins), len(j1.out_shape), len(j1.sems)
    aliases = dict(j1.aliases, **{n_in + i: n_out + o for i, o in j2.aliases.items()})
    first = lambda ins, outs, sems: (ins[:n_in], outs[:n_out], sems[:n_sem])
    second = lambda ins, outs, sems: (ins[n_in:], outs[n_out:], sems[n_sem:])

    def start(*refs):
        j1.start(*first(*refs))
        j2.start(*second(*refs))

    def finish(*refs):
        j1.finish(*first(*refs))
        j2.finish(*second(*refs))

    return _Job(j1.ins + j2.ins, j1.out_shape + j2.out_shape, j1.sems + j2.sems, start, finish, aliases)


def _pcall(body, *, grid, in_specs, out_specs, out_shape, scratch_shapes, name, semantics, args, job=None, prefetch=()):
    n_pre = len(prefetch)

    def call(fn, in_specs_, out_specs_, out_shape_, scratch_, sem, operands, aliases):
        if n_pre:
            spec = pltpu.PrefetchScalarGridSpec(num_scalar_prefetch=n_pre, grid=grid, in_specs=in_specs_,
                                                out_specs=out_specs_, scratch_shapes=scratch_)
            return pl.pallas_call(fn, grid_spec=spec, out_shape=out_shape_, name=name, input_output_aliases=aliases,
                                  compiler_params=_params(sem))(*prefetch, *operands)
        return pl.pallas_call(fn, grid=grid, in_specs=in_specs_, out_specs=out_specs_, out_shape=out_shape_,
                              scratch_shapes=scratch_, name=name, input_output_aliases=aliases,
                              compiler_params=_params(sem))(*operands)

    if job is None:
        return list(call(body, in_specs, out_specs, out_shape, scratch_shapes, semantics, args, {})), []
    n_in, n_out, n_scr = len(in_specs), len(out_specs), len(scratch_shapes)
    j_in, j_out = len(job.ins), len(job.out_shape)
    steps = tuple(grid)

    def carrier(*refs):
        pre, refs = refs[:n_pre], refs[n_pre:]
        o = 0
        main_in, o = refs[o:o + n_in], o + n_in
        job_in, o = refs[o:o + j_in], o + j_in
        main_out, o = refs[o:o + n_out], o + n_out
        job_out, o = refs[o:o + j_out], o + j_out
        main_scr, job_sems = refs[o:o + n_scr], refs[o + n_scr:]
        ids = [pl.program_id(a) for a in range(len(steps))]
        first = functools.reduce(lambda p, q: p & q, [i == 0 for i in ids])
        last = functools.reduce(lambda p, q: p & q, [i == s - 1 for i, s in zip(ids, steps)])

        @pl.when(first)
        def _():
            job.start(job_in, job_out, job_sems)

        if job.middle is not None:
            flat, total = 0, 1
            for i, s in zip(ids, steps):
                flat, total = flat * s + i, total * s

            @pl.when(flat == total * 3 // 5)
            def _():
                job.middle(job_in, job_out, job_sems)

        body(*pre, *main_in, *main_out, *main_scr)

        @pl.when(last)
        def _():
            job.finish(job_in, job_out, job_sems)

    outs = call(carrier, list(in_specs) + [_ANY] * j_in, list(out_specs) + [_ANY] * j_out,
                list(out_shape) + job.out_shape, list(scratch_shapes) + job.sems, ("arbitrary",) * len(steps),
                list(args) + job.ins, {n_pre + n_in + i: n_out + o for i, o in job.aliases.items()})
    return list(outs[:n_out]), list(outs[n_out:])


def _gather_relay_job(shards, rows=None, into=None, alone=False):
    n = len(shards)
    rows = rows or [(0, s.shape[0]) for s in shards]
    into = into or [None] * n
    olds, aliases = [], {}
    for a, buf in enumerate(into):
        if buf is not None:
            aliases[n + len(olds)] = a
            olds.append(buf)

    def tools(ins, outs, sems):
        send_sems, recv_sems, local_sems = sems
        x, y, c = _mesh_pos()
        q = 2 * x + y
        chip_at = lambda rel: (1 - x if rel & 2 else x, 1 - y if rel & 1 else y)

        def part(a, chip, core):
            rs, (r0, r1) = shards[a].shape[0], rows[a]
            return outs[a].at[pl.ds((2 * chip + core) * rs + r0, r1 - r0), :]

        own = lambda a: ins[a].at[pl.ds(rows[a][0], rows[a][1] - rows[a][0]), :]

        def copy(a, slot, chip, core, to, src=None):
            blk = part(a, chip, core)
            return pltpu.make_async_remote_copy(src_ref=blk if src is None else src, dst_ref=blk,
                                                send_sem=send_sems.at[7 * a + slot], recv_sem=recv_sems.at[7 * a + slot],
                                                device_id=to, device_id_type=MESH)

        mine = [pltpu.make_async_copy(own(a), part(a, q, c), local_sems.at[a]) for a in range(n)]
        first = [copy(a, slot, q, c, (x, y, 1 - c) if slot == 0 else (*chip_at(slot), c), src=own(a))
                 for a in range(n) for slot in (0, 1, 2)]
        return x, y, c, q, chip_at, copy, mine, first

    def start(ins, outs, sems):
        *_, mine, first = tools(ins, outs, sems)
        for cp in mine + first:
            cp.start()

    def middle(ins, outs, sems):
        x, y, c, q, chip_at, copy, _, _ = tools(ins, outs, sems)
        me, sib = (x, y, c), (x, y, 1 - c)

        def relay(src, dst):
            for a in range(n):
                copy(a, src, q ^ src, c, me).wait_recv()
                copy(a, 3, q ^ src, c, (*chip_at(dst), c)).start()
                copy(a, 3 + src, q ^ src, c, sib).start()
            for a in range(n):
                copy(a, dst, q ^ dst, c, me).wait_recv()
                copy(a, 3 + dst, q ^ dst, c, sib).start()

        pl.when(c == 1)(lambda: relay(1, 2))
        pl.when(c == 0)(lambda: relay(2, 1))

    def finish(ins, outs, sems):
        if alone:
            middle(ins, outs, sems)
        x, y, c, q, chip_at, copy, mine, first = tools(ins, outs, sems)
        me, sib = (x, y, c), (x, y, 1 - c)
        for a in range(n):
            copy(a, 3, q ^ 3, c, me).wait_recv()
            copy(a, 6, q ^ 3, c, sib).start()
        for a in range(n):
            copy(a, 0, q, 1 - c, me).wait_recv()
            for rel in (1, 2, 3):
                copy(a, 3 + rel, q ^ rel, 1 - c, me).wait_recv()
        for a in range(n):
            for slot in range(3, 7):
                copy(a, slot, q, c, sib).wait_send()
        for cp in first:
            cp.wait_send()
        for cp in mine:
            cp.wait()

    return _Job(list(shards) + olds, [jax.ShapeDtypeStruct((N_DEV * s.shape[0], s.shape[1]), s.dtype) for s in shards],
                [_SEMS(7 * n), _SEMS(7 * n), _SEMS(n)], start, finish, aliases, middle=None if alone else middle)


def _pair_job(grads):
    n = len(grads)

    def copies(ins, outs, sems):
        send_sems, recv_sems = sems
        x, y, c = _mesh_pos()
        out = []
        for a in range(n):
            rs = grads[a].shape[0] // N_DEV
            for q in range(4):
                blk = ins[a].at[pl.ds((2 * q + 1 - c) * rs, rs), :]
                out.append(pltpu.make_async_remote_copy(
                    src_ref=blk, dst_ref=outs[a].at[q], send_sem=send_sems.at[4 * a + q], recv_sem=recv_sems.at[4 * a + q],
                    device_id=(x, y, 1 - c), device_id_type=MESH))
        return out

    def start(ins, outs, sems):
        for cp in copies(ins, outs, sems):
            cp.start()

    def finish(ins, outs, sems):
        for cp in copies(ins, outs, sems):
            cp.wait()

    return _Job(grads, [jax.ShapeDtypeStruct((4, g.shape[0] // N_DEV, g.shape[1]), g.dtype) for g in grads],
                [_SEMS(4 * n), _SEMS(4 * n)], start, finish)


def _chip_job(sums, rels=(1, 2, 3), rows=None, into=None):
    n, nr = len(sums), len(rels)
    r0, r1 = rows or (0, sums[0].shape[1])
    olds = list(into or [])
    aliases = {n + a: a for a in range(len(olds))}

    def copies(ins, outs, sems):
        send_sems, recv_sems = sems
        x, y, c = _mesh_pos()
        out = []
        for a in range(n):
            for slot, r in enumerate(rels):
                px, py = (1 - x if r & 2 else x), (1 - y if r & 1 else y)
                out.append(pltpu.make_async_remote_copy(
                    src_ref=ins[a].at[2 * px + py, pl.ds(r0, r1 - r0), :], dst_ref=outs[a].at[slot, pl.ds(r0, r1 - r0), :],
                    send_sem=send_sems.at[nr * a + slot], recv_sem=recv_sems.at[nr * a + slot],
                    device_id=(px, py, c), device_id_type=MESH))
        return out

    def start(ins, outs, sems):
        for cp in copies(ins, outs, sems):
            cp.start()

    def finish(ins, outs, sems):
        for cp in copies(ins, outs, sems):
            cp.wait()

    return _Job(list(sums) + olds, [jax.ShapeDtypeStruct((nr,) + s.shape[1:], s.dtype) for s in sums],
                [_SEMS(nr * n), _SEMS(nr * n)], start, finish, aliases)


def _mm(name, form, a_list, b, M, N, K, tm, tn, tk, extras, outs, epi, job=None, a_first=0):
    nI, nJ, nK = M // tm, N // tn, K // tk
    assert nI * tm == M and nJ * tn == N and nK * tk == K
    dims = {"nn": NN, "nt": NT, "tn": TN}[form]
    b_list = b if isinstance(b, list) else [(b, {"nn": N, "nt": K, "tn": N}[form])]
    nA, nB = len(a_list), len(b_list)
    assert nA == 1 or nB == 1
    assert nB == 1 or form in ("nn", "nt")
    AXIS = {"i": 0, "j": 1, "k": 2}
    a_axis, a_tile = ("i", tm) if form == "tn" else ("k", tk)
    b_axis, b_tile = ("k", tk) if form == "nt" else ("j", tn)

    def cut(pieces, tile, total):
        starts, s = [], 0
        for _, w in pieces:
            assert w % tile == 0
            starts.append(s // tile)
            s += w
        assert s == total
        return starts, [w // tile for _, w in pieces]

    a_st, a_cn = cut(a_list, a_tile, M if form == "tn" else K)
    b_st, b_cn = cut(b_list, b_tile, K if form == "nt" else N)

    def inside(idx, st, cn):
        return (idx >= st) & (idx < st + cn)

    def a_spec(p):
        st, cn = a_st[p], a_cn[p]
        if form == "tn":
            return pl.BlockSpec((tk, tm), lambda i, j, k: (jnp.where(inside(i, st, cn), k, 0), jnp.clip(i - st, 0, cn - 1)))
        return pl.BlockSpec((tm, tk), lambda i, j, k: (i, a_first + jnp.clip(k - st, 0, cn - 1)))

    def b_spec(p):
        st, cn = b_st[p], b_cn[p]
        if form == "nt":
            return pl.BlockSpec((tn, tk), lambda i, j, k: (j, jnp.clip(k - st, 0, cn - 1)))
        if nB == 1:
            return pl.BlockSpec((tk, tn), lambda i, j, k: (k, j))
        return pl.BlockSpec((tk, tn), lambda i, j, k: (jnp.where(inside(j, st, cn), k, 0), jnp.clip(j - st, 0, cn - 1)))

    in_specs = ([a_spec(p) for p in range(nA)] + [b_spec(p) for p in range(nB)]
                + [pl.BlockSpec(bs, im) for _, bs, im in extras])
    out_shape = [jax.ShapeDtypeStruct(s_, d_) for s_, d_, _, _ in outs]
    out_specs = [pl.BlockSpec(bs, im) for _, _, bs, im in outs]
    nE, nO = len(extras), len(outs)
    single = nA == 1 and nB == 1

    def body(*refs):
        a_refs, b_refs = refs[:nA], refs[nA:nA + nB]
        ex, ou = refs[nA + nB:nA + nB + nE], refs[nA + nB + nE:nA + nB + nE + nO]
        ids = [pl.program_id(a) for a in range(3)]

        def partial_of(p, q):
            return lax.dot_general(a_refs[p][...], b_refs[q][...], dims, preferred_element_type=F32)

        if nK == 1 and single:
            epi(partial_of(0, 0), ex, ou)
            return
        acc = refs[-1]
        k = ids[2]
        for p in range(nA):
            for q in range(nB):
                def first(p=p, q=q):
                    acc[...] = partial_of(p, q)

                def later(p=p, q=q):
                    acc[...] += partial_of(p, q)

                here = None
                if nA > 1:
                    here = inside(ids[AXIS[a_axis]], a_st[p], a_cn[p])
                if nB > 1:
                    here = inside(ids[AXIS[b_axis]], b_st[q], b_cn[q])
                pl.when(k == 0 if here is None else here & (k == 0))(first)
                pl.when(k > 0 if here is None else here & (k > 0))(later)

        @pl.when(k == nK - 1)
        def _():
            epi(acc[...], ex, ou)

    scratch = [] if (nK == 1 and single) else [pltpu.VMEM((tm, tn), F32)]
    res, job_res = _pcall(
        body, grid=(nI, nJ, nK), in_specs=in_specs, out_specs=out_specs, out_shape=out_shape, scratch_shapes=scratch,
        name=name, semantics=("parallel", "parallel", "arbitrary"),
        args=[a for a, _ in a_list] + [p for p, _ in b_list] + [e for e, _, _ in extras], job=job)
    return res if job is None else (res, job_res)


def _twin_mm(name, form, pairs, M, N, K, tm, tn, tk, out_dtype):
    nI, nJ, nK = M // tm, N // tn, K // tk
    dims = {"nn": NN, "tn": TN}[form]
    a_spec = (pl.BlockSpec((tm, tk), lambda i, j, k: (i, k)) if form == "nn" else pl.BlockSpec((tk, tm), lambda i, j, k: (k, i)))
    b_spec = pl.BlockSpec((tk, tn), lambda i, j, k: (k, j))
    o_spec = pl.BlockSpec((tm, tn), lambda i, j, k: (i, j))

    def body(a1, b1, a2, b2, o1, o2, *accs):
        k = pl.program_id(2)
        for a_ref, b_ref, o_ref, acc in ((a1, b1, o1, accs[0] if accs else None), (a2, b2, o2, accs[1] if accs else None)):
            part = lax.dot_general(a_ref[...], b_ref[...], dims, preferred_element_type=F32)
            if nK == 1:
                o_ref[...] = part.astype(out_dtype)
                continue

            @pl.when(k == 0)
            def _(acc=acc, part=part):
                acc[...] = part

            @pl.when(k > 0)
            def _(acc=acc, part=part):
                acc[...] += part

            @pl.when(k == nK - 1)
            def _(acc=acc, o_ref=o_ref):
                o_ref[...] = acc[...].astype(out_dtype)

    (a1, b1), (a2, b2) = pairs
    shape = jax.ShapeDtypeStruct((M, N), out_dtype)
    return pl.pallas_call(
        body, grid=(nI, nJ, nK), in_specs=[a_spec, b_spec, a_spec, b_spec], out_specs=[o_spec, o_spec],
        out_shape=[shape, shape], scratch_shapes=[] if nK == 1 else [pltpu.VMEM((tm, tn), F32)] * 2, name=name,
        compiler_params=_params(("parallel", "parallel", "arbitrary")))(a1, b1, a2, b2)


def _piece_tiles(pieces, tile):
    starts, s = [], 0
    for _, w in pieces:
        assert w % tile == 0
        starts.append(s // tile)
        s += w
    return starts, [w // tile for _, w in pieces], s


def _pieces_tn(name, pieces, b, tile, job=None):
    T, N = b.shape
    st, cn, M = _piece_tiles(pieces, tile)
    nP, nI = len(pieces), M // tile

    def body(*refs):
        p_refs, b_hbm, o_ref = refs[:nP], refs[nP], refs[nP + 1]
        bbuf, abuf, bsem, asem = refs[nP + 2:]
        i = pl.program_id(0)

        def fetch(step, slot):
            for p in range(nP):
                @pl.when((step >= st[p]) & (step < st[p] + cn[p]))
                def _():
                    col = pl.multiple_of((step - st[p]) * tile, tile)
                    pltpu.make_async_copy(p_refs[p].at[pl.ds(0, T), pl.ds(col, tile)], abuf.at[slot], asem.at[slot]).start()

        @pl.when(i == 0)
        def _():
            whole = pltpu.make_async_copy(b_hbm, bbuf, bsem)
            whole.start()
            fetch(0, 0)
            whole.wait()

        @pl.when(i + 1 < nI)
        def _():
            fetch(i + 1, (i + 1) % 2)

        pltpu.make_async_copy(p_refs[0].at[pl.ds(0, T), pl.ds(0, tile)], abuf.at[i % 2], asem.at[i % 2]).wait()
        o_ref[...] = lax.dot_general(abuf[i % 2], bbuf[...], TN, preferred_element_type=F32).astype(BF16)

    res, job_res = _pcall(
        body, grid=(nI,), in_specs=[_ANY] * (nP + 1), out_specs=[pl.BlockSpec((tile, N), lambda i: (i, 0))],
        out_shape=[jax.ShapeDtypeStruct((M, N), BF16)],
        scratch_shapes=[pltpu.VMEM((T, N), b.dtype), pltpu.VMEM((2, T, tile), b.dtype), pltpu.SemaphoreType.DMA, _SEMS(2)],
        name=name, semantics=("arbitrary",), args=[p for p, _ in pieces] + [b], job=job)
    return res if job is None else (res, job_res)


def _rows_mm(name, pieces, w, T, tm, tk, vecs, bufs, parts, epi, job=None):
    st, cn, K = _piece_tiles(pieces, tk)
    nP, nI, nK = len(pieces), T // tm, K // tk
    ws = w if isinstance(w, list) else [w]
    nW = len(ws)
    wd = D // nW
    assert wd * nW == D and wd % 512 == 0
    part_specs = [pl.BlockSpec(bs, lambda i, k, im=im: im(i, 0, k)) for _, _, bs, im in parts]
    n_vec, nB = len(vecs), len(bufs)
    load_ix = [n for n, (_, src, _) in enumerate(bufs) if src is not None]
    store_ix = [n for n, (_, _, store) in enumerate(bufs) if store]
    n_any_in, n_any_out = len(load_ix), len(store_ix)

    def body(*refs):
        p_refs, w_refs = refs[:nP], refs[nP:nP + nW]
        o = nP + nW - 1
        vec_refs = refs[o + 1:o + 1 + n_vec]
        ins = refs[o + 1 + n_vec:o + 1 + n_vec + n_any_in]
        o = o + 1 + n_vec + n_any_in
        hbm_outs, p_outs = refs[o:o + n_any_out], refs[o + n_any_out:o + n_any_out + len(parts)]
        o = o + n_any_out + len(parts)
        acc, abuf = refs[o:o + 2]
        buf_refs = refs[o + 2:o + 2 + nB]
        asem, in_sems, out_sems = refs[-3:]
        i, k = pl.program_id(0), pl.program_id(1)
        g = i * nK + k
        rows_of = lambda ref, ii: ref.at[pl.ds(pl.multiple_of(ii * tm, tm), tm), :]
        bufs_in = [buf_refs[n] for n in load_ix]
        bufs_out = [buf_refs[n] for n in store_ix]

        def fetch(ii, kk, slot):
            for p in range(nP):
                @pl.when((kk >= st[p]) & (kk < st[p] + cn[p]))
                def _():
                    col = pl.multiple_of((kk - st[p]) * tk, tk)
                    src = p_refs[p].at[pl.ds(pl.multiple_of(ii * tm, tm), tm), pl.ds(col, tk)]
                    pltpu.make_async_copy(src, abuf.at[slot], asem.at[slot]).start()

        loads = lambda ii: [pltpu.make_async_copy(rows_of(src, ii), buf, in_sems.at[n])
                            for n, (src, buf) in enumerate(zip(ins, bufs_in))]
        stores = lambda ii: [pltpu.make_async_copy(buf, rows_of(dst, ii), out_sems.at[n])
                             for n, (buf, dst) in enumerate(zip(bufs_out, hbm_outs))]

        @pl.when(g == 0)
        def _():
            fetch(0, 0, 0)

        @pl.when(g + 1 < nI * nK)
        def _():
            last_k = k == nK - 1
            fetch(jnp.where(last_k, i + 1, i), jnp.where(last_k, 0, k + 1), (g + 1) % 2)

        @pl.when(k == 0)
        def _():
            @pl.when(i > 0)
            def _():
                for cp in stores(i - 1):
                    cp.wait()
            for cp in loads(i):
                cp.start()

        pltpu.make_async_copy(p_refs[0].at[pl.ds(0, tm), pl.ds(0, tk)], abuf.at[g % 2], asem.at[g % 2]).wait()

        def product(cols):
            c0 = cols.start % wd
            return jnp.dot(abuf[g % 2], w_refs[cols.start // wd][:, c0:c0 + 512], preferred_element_type=F32)

        col_blocks = [slice(c0, c0 + 512) for c0 in range(0, D, 512)]

        @pl.when(k == 0)
        def _():
            for cols in col_blocks:
                acc[:, cols] = product(cols)

        @pl.when(k > 0)
        def _():
            for cols in col_blocks:
                acc[:, cols] += product(cols)

        @pl.when(k == nK - 1)
        def _():
            for cp in loads(i):
                cp.wait()
            epi(acc, vec_refs, buf_refs, p_outs)
            for cp in stores(i):
                cp.start()

            @pl.when(i == nI - 1)
            def _():
                for cp in stores(i):
                    cp.wait()

    vec = pl.BlockSpec((1, D), lambda i, k: (0, 0))
    scratch = ([pltpu.VMEM((tm, D), F32), pltpu.VMEM((2, tm, tk), BF16)] + [pltpu.VMEM((tm, D), dt) for dt, _, _ in bufs]
               + [_SEMS(2), _SEMS(n_any_in), _SEMS(n_any_out)])
    res, job_res = _pcall(
        body, grid=(nI, nK),
        in_specs=[_ANY] * nP + [pl.BlockSpec((tk, wd), lambda i, k: (k, 0))] * nW + [vec] * n_vec + [_ANY] * n_any_in,
        out_specs=[_ANY] * n_any_out + part_specs,
        out_shape=([jax.ShapeDtypeStruct((T, D), bufs[n][0]) for n in store_ix]
                   + [jax.ShapeDtypeStruct(s, d) for s, d, _, _ in parts]),
        scratch_shapes=scratch, name=name, semantics=("arbitrary", "arbitrary"),
        args=[p for p, _ in pieces] + ws + list(vecs) + [bufs[n][1] for n in load_ix], job=job)
    return res if job is None else (res, job_res)


def _pieces_nn_rms(name, pieces, w, x, gain, sc, dres, tm, tk, job=None):
    _, outs, epi = _rms_mod_bwd_epilogue(x, gain, sc, dres, tm)

    def on_rows(acc, vecs, bufs, parts):
        epi(acc, [bufs[0], vecs[0], vecs[1], bufs[1]], [bufs[1], *parts])

    return _rows_mm(name, pieces, w, x.shape[0], tm, tk, [gain, sc], [(F32, x, False), (F32, dres, True)],
                    outs[1:], on_rows, job=job)


def _rms_mod_fwd(name, x, gain, sc, sh, tr, job=None):
    T = x.shape[0]

    def body(x_ref, g_ref, sc_ref, sh_ref, h_ref):
        xv = x_ref[...]
        rstd = lax.rsqrt(jnp.mean(xv * xv, axis=-1, keepdims=True) + EPS)
        h_ref[...] = ((xv * rstd * g_ref[...]) * (1.0 + sc_ref[...]) + sh_ref[...]).astype(BF16)

    row = pl.BlockSpec((tr, D), lambda i: (i, 0))
    vec = pl.BlockSpec((1, D), lambda i: (0, 0))
    return _pcall(body, grid=(T // tr,), in_specs=[row, vec, vec, vec], out_specs=[row],
                  out_shape=[jax.ShapeDtypeStruct((T, D), BF16)], scratch_shapes=[], name=name, semantics=("parallel",),
                  args=[x, gain, sc, sh], job=job)


def _rms_mod_bwd_epilogue(x, gain, sc, dres, tm, gate=None, mo=None):
    T = x.shape[0]
    with_gate = gate is not None
    row = ((tm, D), lambda i, j, k: (i, 0))
    vec = ((1, D), lambda i, j, k: (0, 0))
    part = ((T // tm * 8, D), F32, (8, D), lambda i, j, k: (i, 0))
    extras = [(x, *row), (gain, *vec), (sc, *vec), (dres, *row)]
    outs = [((T, D), F32, *row), part, part, part]
    if with_gate:
        extras += [(gate, *vec), (mo, *row)]
        outs += [((T, D), BF16, *row), part]

    rows = min(64, tm)

    def epi(acc, ex, ou):
        g = ex[1][...]
        sums = [jnp.zeros((8, D), F32) for _ in range(4)]
        for r0 in range(0, tm, rows):
            rs = slice(r0, r0 + rows)
            dhv, xv = acc[rs, :], ex[0][rs, :]
            rstd = lax.rsqrt(jnp.mean(xv * xv, axis=-1, keepdims=True) + EPS)
            xhat = xv * rstd
            dn = dhv * (1.0 + ex[2][...])
            dxhat = dn * g
            dx = ex[3][rs, :] + rstd * (dxhat - xhat * jnp.mean(dxhat * xhat, axis=-1, keepdims=True))
            ou[0][rs, :] = dx
            terms = [dhv, dhv * (xhat * g), dn * xhat]
            if with_gate:
                terms.append(dx * ex[5][rs, :].astype(F32))
                ou[4][rs, :] = (ex[4][...] * dx).astype(BF16)
            sums = [s + _fold8(t) for s, t in zip(sums, terms)] + sums[len(terms):]
        ou[1][...], ou[2][...], ou[3][...] = sums[:3]
        if with_gate:
            ou[5][...] = sums[3]

    return extras, outs, epi


def _rms_mod_bwd(name, dh, x, gain, sc, dres, tr, gate=None, mo=None):
    T = x.shape[0]
    extras, outs, epi = _rms_mod_bwd_epilogue(x, gain, sc, dres, tr, gate, mo)
    rows_only = lambda im: (lambda i: im(i, 0, 0))
    nE = len(extras)

    def body(dh_ref, *refs):
        epi(dh_ref, refs[:nE], refs[nE:])

    return pl.pallas_call(
        body, grid=(T // tr,),
        in_specs=[pl.BlockSpec((tr, D), lambda i: (i, 0))] + [pl.BlockSpec(bs, rows_only(im)) for _, bs, im in extras],
        out_specs=[pl.BlockSpec(bs, rows_only(im)) for _, _, bs, im in outs],
        out_shape=[jax.ShapeDtypeStruct(s, d) for s, d, _, _ in outs], name=name, compiler_params=_params(("parallel",)),
    )(dh, *[e for e, _, _ in extras])


def _split3(v):
    h = v.astype(BF16)
    r1 = v - h.astype(F32)
    m = r1.astype(BF16)
    lo = (r1 - m.astype(F32)).astype(BF16)
    return h, m, lo


def _tri_mm(tri, v, dims=NN):
    h, m, lo = _split3(v)
    t = tri.astype(BF16)
    mm = lambda p: lax.dot_general(t, p, dims, preferred_element_type=F32)
    return (mm(lo) + mm(m)) + mm(h)


def _hgrn_chunk_terms(q, fl, lb):
    sig = _sigmoid(fl)
    f = lb + (1.0 - lb) * sig
    lf = jnp.log(f)
    kk = 1.0 - f
    sq = _sigmoid(q)
    qf = q * sq
    return sig, f, lf, kk, sq, qf


def _causal(n):
    r = lax.broadcasted_iota(jnp.int32, (n, n), 0)
    c = lax.broadcasted_iota(jnp.int32, (n, n), 1)
    return r >= c


def _hgrn_fwd(proj, lb_logits, o_gain, tt, job=None):
    T = proj.shape[0]
    nT, ncl = T // tt, tt // CHUNK
    C = CHUNK

    def body(q_ref, f_ref, i_ref, g_ref, lbl_ref, og_ref, y_ref, st_ref, S):
        @pl.when(pl.program_id(1) == 0)
        def _():
            S[...] = jnp.zeros_like(S)

        lbl = lbl_ref[...]
        lb = _sigmoid(lbl[0:1, :] - lbl[1:2, :])
        og = og_ref[...]
        shp = (ncl, C, A_HD)
        q, fl, v, g = (r[...].reshape(shp) for r in (q_ref, f_ref, i_ref, g_ref))
        tri = jnp.broadcast_to(_causal(C), (ncl, C, C))
        _, _, lf, kk, _, qf = _hgrn_chunk_terms(q, fl, lb)
        b = _tri_mm(tri, lf, BNN)
        bm, bl = b[:, C // 2 - 1:C // 2, :], b[:, C - 1:C, :]
        qd, kd = qf * jnp.exp(b - bm), kk * jnp.exp(bm - b)
        A = jnp.where(tri, _dot(qd, kd, BNT), 0.0)
        d_st = _dot(v, kk * jnp.exp(bl - b), BTN)
        dec = jnp.exp(bl)
        st = S[...]
        for ci in range(ncl):
            st_ref[0, ci] = st
            st = st * dec[ci] + d_st[ci]
        S[...] = st
        o = _dot(A, v, BNN) + _dot(qf * jnp.exp(b), st_ref[0], BNT)
        r = lax.rsqrt(jnp.mean(o * o, axis=-1, keepdims=True) + EPS)
        y_ref[...] = (o * r * og * (g * _sigmoid(g))).astype(BF16).reshape(tt, A_HD)

    def col(off):
        return pl.BlockSpec((tt, A_HD), lambda h, t: (t, off // A_HD + h))

    head_vec = lambda rows: pl.BlockSpec((rows, A_HD), lambda h, t: (0, h))
    return _pcall(
        body, grid=(A_HEADS, nT),
        in_specs=[col(OFF_QA), col(OFF_FA), col(OFF_IA), col(OFF_GA), head_vec(2), head_vec(1)],
        out_specs=[pl.BlockSpec((tt, A_HD), lambda h, t: (t, h)),
                   pl.BlockSpec((1, ncl, A_HD, A_HD), lambda h, t: (h, t, 0, 0))],
        out_shape=[jax.ShapeDtypeStruct((T, AW), BF16),
                   jax.ShapeDtypeStruct((A_HEADS, T // C, A_HD, A_HD), F32)],
        scratch_shapes=[pltpu.VMEM((A_HD, A_HD), F32)], name="hgrn_fwd", semantics=("parallel", "arbitrary"),
        args=[proj, proj, proj, proj, lb_logits, o_gain], job=job)


def _hgrn_bwd(proj, st, dy, lb_logits, o_gain, tt, job=None):
    T = proj.shape[0]
    nT, ncl = T // tt, tt // CHUNK
    C = CHUNK

    def body(q_ref, f_ref, i_ref, g_ref, st_ref, dy_ref, lbl_ref, og_ref,
             dq_ref, df_ref, di_ref, dg_ref, plb_ref, pog_ref, dS):
        @pl.when(pl.program_id(1) == 0)
        def _():
            dS[...] = jnp.zeros_like(dS)

        lbl = lbl_ref[...]
        lb = _sigmoid(lbl[0:1, :] - lbl[1:2, :])
        og = og_ref[...]
        shp = (ncl, C, A_HD)
        flat = lambda t: t.reshape(tt, A_HD)
        q, fl, v, g, dout = (r[...].reshape(shp) for r in (q_ref, f_ref, i_ref, g_ref, dy_ref))
        tri = jnp.broadcast_to(_causal(C), (ncl, C, C))
        rowi = lax.broadcasted_iota(jnp.int32, shp, 1)
        st0 = st_ref[0]
        sig, f, lf, kk, sq, qf = _hgrn_chunk_terms(q, fl, lb)
        b = _tri_mm(tri, lf, BNN)
        bm, bl = b[:, C // 2 - 1:C // 2, :], b[:, C - 1:C, :]
        e_qd, e_kd, e_ke, e_b = jnp.exp(b - bm), jnp.exp(bm - b), jnp.exp(bl - b), jnp.exp(b)
        qd, kd, ke, qe = qf * e_qd, kk * e_kd, kk * e_ke, qf * e_b
        dec = jnp.exp(bl)
        A = jnp.where(tri, _dot(qd, kd, BNT), 0.0)
        o = _dot(A, v, BNN) + _dot(qe, st0, BNT)
        r = lax.rsqrt(jnp.mean(o * o, axis=-1, keepdims=True) + EPS)
        sg = _sigmoid(g)
        on = o * r * og
        dg_ref[...] = flat((dout * on * (sg * (1.0 + g * (1.0 - sg)))).astype(BF16))
        don = dout * (g * sg)
        pog_ref[...] = _fold8(flat(don * o * r))
        dyh = don * og
        do = r * (dyh - o * (r * r) * jnp.mean(dyh * o, axis=-1, keepdims=True))
        g_st = _dot(do, qe, BTN)
        run = dS[...]
        after = [None] * ncl
        for ci in reversed(range(ncl)):
            after[ci] = run
            run = g_st[ci] + run * dec[ci]
        dS[...] = run
        d_after = jnp.stack(after, axis=0)
        ddec = jnp.sum(d_after * st0, axis=1, keepdims=True)
        dqe = _dot(do, st0, BNN)
        dke = _dot(v, d_after, BNN)
        dA = jnp.where(tri, _dot(do, v, BNT), 0.0)
        dv = _dot(ke, d_after, BNT) + _dot(A, do, BTN)
        dqd = _dot(dA, kd, BNN)
        dkd = _dot(dA, qd, BTN)
        di_ref[...] = flat(dv.astype(BF16))
        dqf = dqe * e_b + dqd * e_qd
        dkk = dkd * e_kd + dke * e_ke
        t_qd, t_kd, t_ke = dqd * qd, dkd * kd, dke * ke
        db = dqe * qe + t_qd - t_kd - t_ke
        dbm = jnp.sum(t_kd - t_qd, axis=1, keepdims=True)
        dbl = jnp.sum(t_ke, axis=1, keepdims=True) + ddec * dec
        db = db + jnp.where(rowi == C // 2 - 1, dbm, 0.0) + jnp.where(rowi == C - 1, dbl, 0.0)
        dlf = _tri_mm(tri, db, BTN)
        dfv = dlf / f - dkk
        df_ref[...] = flat((dfv * (1.0 - lb) * sig * (1.0 - sig)).astype(BF16))
        plb_ref[...] = _fold8(flat(dfv * (1.0 - sig)))
        dq_ref[...] = flat((dqf * (sq * (1.0 + q * (1.0 - sq)))).astype(BF16))

    def col(off):
        return pl.BlockSpec((tt, A_HD), lambda h, t: (nT - 1 - t, off // A_HD + h))

    head_vec = lambda rows: pl.BlockSpec((rows, A_HD), lambda h, t: (0, h))
    o_spec = pl.BlockSpec((tt, A_HD), lambda h, t: (nT - 1 - t, h))
    p_spec = pl.BlockSpec((8, A_HD), lambda h, t: (t, h))
    o_shape = jax.ShapeDtypeStruct((T, AW), BF16)
    p_shape = jax.ShapeDtypeStruct((nT * 8, AW), F32)
    return _pcall(
        body, grid=(A_HEADS, nT),
        in_specs=[col(OFF_QA), col(OFF_FA), col(OFF_IA), col(OFF_GA),
                  pl.BlockSpec((1, ncl, A_HD, A_HD), lambda h, t: (h, nT - 1 - t, 0, 0)),
                  pl.BlockSpec((tt, A_HD), lambda h, t: (nT - 1 - t, h)), head_vec(2), head_vec(1)],
        out_specs=[o_spec, o_spec, o_spec, o_spec, p_spec, p_spec],
        out_shape=[o_shape, o_shape, o_shape, o_shape, p_shape, p_shape],
        scratch_shapes=[pltpu.VMEM((A_HD, A_HD), F32)], name="hgrn_bwd", semantics=("parallel", "arbitrary"),
        args=[proj, proj, proj, proj, st, dy, lb_logits, o_gain], job=job)


LANES = 128
Q_COLS = BW // LANES


def _low_half():
    return lax.broadcasted_iota(jnp.int32, (1, LANES), 1) < B_HD


def _half_sum(t, low):
    lo = jnp.sum(jnp.where(low, t, 0.0), axis=-1, keepdims=True)
    hi = jnp.sum(jnp.where(low, 0.0, t), axis=-1, keepdims=True)
    return jnp.where(low, lo, hi)


def _half_rms(t, low):
    r = lax.rsqrt(_half_sum(t * t, low) * (1.0 / B_HD) + EPS)
    return t * r, r


def _fold_halves(p, low):
    return jnp.where(low, p + pltpu.roll(p, B_HD, 1), 0.0)


def _stack_cols(x):
    return jnp.stack([x[:, c * LANES:(c + 1) * LANES] for c in range(Q_COLS)], axis=0).reshape(KV_HEADS, 2 * BLK, LANES)


def _col_of(t, c):
    return t[c // 2, (c % 2) * BLK:(c % 2 + 1) * BLK]


def _split_halves(col, s, low):
    own = jnp.where(low if s == 0 else jnp.logical_not(low), col, 0.0)
    other = pltpu.roll(own, B_HD, 1)
    return (own, other) if s == 0 else (other, own)


def _swa_keys(kp_ref, kc_ref, vp_ref, vc_ref, kg, low):
    k_lo, k_hi, v_lo, v_hi, hats = [], [], [], [], []
    for j in range(KVW // LANES):
        cs = slice(j * LANES, (j + 1) * LANES)
        k_hat, k_r = _half_rms(jnp.concatenate([kp_ref[:, cs], kc_ref[:, cs]], axis=0), low)
        vcol = jnp.concatenate([vp_ref[:, cs], vc_ref[:, cs]], axis=0)
        hats.append((k_hat, k_r))
        for s in range(2):
            for dst_lo, dst_hi, col in ((k_lo, k_hi, k_hat * kg), (v_lo, v_hi, vcol)):
                lo, hi = _split_halves(col, s, low)
                dst_lo.append(lo)
                dst_hi.append(hi)
    st = lambda parts: jnp.stack(parts, axis=0)
    return st(k_lo), st(k_hi), st(v_lo), st(v_hi), hats


def _swa_mask(first_block):
    qi = lax.broadcasted_iota(jnp.int32, (BLK, 2 * BLK), 0) + BLK
    ki = lax.broadcasted_iota(jnp.int32, (BLK, 2 * BLK), 1)
    rel = qi - ki
    m = (rel >= 0) & (rel < BLK) & (jnp.logical_not(first_block) | (ki >= BLK))
    return jnp.concatenate([m, m], axis=0)


def _sink_cols(sk_ref, hi):
    top = lax.broadcasted_iota(jnp.int32, (2 * BLK, 1), 0) < BLK
    return jnp.stack([jnp.where(top, sk_ref[0, GROUP * hk + hi], sk_ref[0, GROUP * hk + 2 + hi])
                      for hk in range(KV_HEADS)], axis=0)


def _swa_probs(qn, k_half, sink, mask):
    s = jnp.where(mask, _dot(qn, k_half, BNT) * (B_HD ** -0.5), NEG)
    m = jnp.maximum(jnp.max(s, axis=-1, keepdims=True), sink)
    p = jnp.exp(s - m)
    ps = jnp.exp(sink - m)
    inv = 1.0 / (jnp.sum(p, axis=-1, keepdims=True) + ps)
    return p * inv, ps * inv


def _swa_fwd(proj, q_gain, k_gain, sinks, job=None):
    T = proj.shape[0]
    nb = T // BLK

    def body(q_ref, kc_ref, kp_ref, vc_ref, vp_ref, qg_ref, kg_ref, sk_ref, o_ref):
        low = _low_half()
        mask = _swa_mask(pl.program_id(0) == 0)
        qn = _half_rms(_stack_cols(q_ref[...]), low)[0] * qg_ref[...]
        k_lo, k_hi, v_lo, v_hi, _ = _swa_keys(kp_ref, kc_ref, vp_ref, vc_ref, kg_ref[...], low)
        p_lo, _ = _swa_probs(qn, k_lo, _sink_cols(sk_ref, 0), mask)
        p_hi, _ = _swa_probs(qn, k_hi, _sink_cols(sk_ref, 1), mask)
        o = (_dot(p_lo, v_lo, BNN) + _dot(p_hi, v_hi, BNN)).astype(BF16)
        for c in range(Q_COLS):
            o_ref[:, c * LANES:(c + 1) * LANES] = _col_of(o, c)

    q_gain, k_gain = jnp.tile(q_gain, (1, 2)), jnp.tile(k_gain, (1, 2))
    cur = lambda w, off: pl.BlockSpec((BLK, w), lambda i: (i, off // w))
    prev = lambda w, off: pl.BlockSpec((BLK, w), lambda i: (jnp.maximum(i - 1, 0), off // w))
    small = lambda n: pl.BlockSpec((1, 2 * n), lambda i: (0, 0))
    return _pcall(
        body, grid=(nb,),
        in_specs=[cur(BW, OFF_QB), cur(KVW, OFF_KB), prev(KVW, OFF_KB), cur(KVW, OFF_VB), prev(KVW, OFF_VB),
                  small(B_HD), small(B_HD), pl.BlockSpec(memory_space=pltpu.SMEM)],
        out_specs=[pl.BlockSpec((BLK, BW), lambda i: (i, 0))],
        out_shape=[jax.ShapeDtypeStruct((T, BW), BF16)], scratch_shapes=[], name="swa_fwd", semantics=("parallel",),
        args=[proj, proj, proj, proj, proj, q_gain, k_gain, sinks], job=job)


def _swa_bwd(proj, dout, q_gain, k_gain, sinks, job=None):
    T = proj.shape[0]
    nb = T // BLK
    W = BW + 2 * KVW

    def body(q_ref, kc_ref, kp_ref, vc_ref, vp_ref, do_ref, qg_ref, kg_ref, sk_ref,
             dq_ref, dkv_ref, pqg_ref, pkg_ref, psk_ref, dkn_c, dv_c):
        i = pl.program_id(0)
        live = i < nb
        low = _low_half()
        high = jnp.logical_not(low)
        qg, kg = qg_ref[...], kg_ref[...]
        mask = _swa_mask(i == 0)
        lane = lax.broadcasted_iota(jnp.int32, (1, LANES), 1)
        scale = B_HD ** -0.5

        @pl.when(i == 0)
        def _():
            dkn_c[...] = jnp.zeros_like(dkn_c)
            dv_c[...] = jnp.zeros_like(dv_c)

        q_hat, q_r = _half_rms(_stack_cols(q_ref[...]), low)
        qn = q_hat * qg
        k_lo, k_hi, v_lo, v_hi, hats = _swa_keys(kp_ref, kc_ref, vp_ref, vc_ref, kg, low)
        do = _stack_cols(do_ref[...])
        dqn = jnp.zeros((KV_HEADS, 2 * BLK, LANES), F32)
        acc_sk = jnp.zeros((1, LANES), F32)
        dk_parts, dv_parts = [], []
        for hi, (k_h, v_h) in enumerate(((k_lo, v_lo), (k_hi, v_hi))):
            p, ps = _swa_probs(qn, k_h, _sink_cols(sk_ref, hi), mask)
            dp = _dot(do, v_h, BNT)
            delta = jnp.sum(p * dp, axis=-1, keepdims=True)
            ds = p * (dp - delta) * scale
            dqn = dqn + _dot(ds, k_h, BNN)
            dk_parts.append(_dot(ds, qn, BTN))
            dv_parts.append(_dot(p, do, BTN))
            t = ps * delta
            for hk in range(KV_HEADS):
                for rows in range(2):
                    h = GROUP * hk + 2 * rows + hi
                    acc_sk = acc_sk + jnp.where(
                        lane == h, -jnp.sum(t[hk, rows * BLK:(rows + 1) * BLK], axis=0, keepdims=True), 0.0)
        dqh = dqn * qg
        dq = (q_r * (dqh - q_hat * (_half_sum(dqh * q_hat, low) * (1.0 / B_HD)))).astype(BF16)
        for c in range(Q_COLS):
            dq_ref[:, c * LANES:(c + 1) * LANES] = _col_of(dq, c)
        acc_qg = _fold_halves(_fold8((dqn * q_hat).reshape(KV_HEADS * 2 * BLK, LANES)), low)

        def native(parts, j):
            lo_arr, hi_arr = parts
            a, b = 2 * j, 2 * j + 1
            return (jnp.where(low, lo_arr[a], 0.0) + pltpu.roll(jnp.where(high, hi_arr[a], 0.0), B_HD, 1)
                    + jnp.where(high, hi_arr[b], 0.0) + pltpu.roll(jnp.where(low, lo_arr[b], 0.0), B_HD, 1))

        acc_kg = jnp.zeros((8, LANES), F32)
        for j in range(KVW // LANES):
            cs = slice(j * LANES, (j + 1) * LANES)
            dkn = jnp.where(live, native(dk_parts, j), 0.0)
            dvc = jnp.where(live, native(dv_parts, j), 0.0)
            kp_hat, kp_r = hats[j][0][:BLK], hats[j][1][:BLK]
            dkn_prev = dkn_c[:, cs] + dkn[:BLK]
            dv_prev = dv_c[:, cs] + dvc[:BLK]
            acc_kg = acc_kg + _fold8(dkn_prev * kp_hat)
            dkh = dkn_prev * kg
            dkv_ref[:, cs] = (kp_r * (dkh - kp_hat * (_half_sum(dkh * kp_hat, low) * (1.0 / B_HD)))).astype(BF16)
            dkv_ref[:, KVW + j * LANES:KVW + (j + 1) * LANES] = dv_prev.astype(BF16)
            dkn_c[:, cs] = dkn[BLK:]
            dv_c[:, cs] = dvc[BLK:]
        keep = jnp.where(i > 0, 1.0, 0.0)
        pqg_ref[...] = jnp.where(live, acc_qg, 0.0)
        pkg_ref[...] = _fold_halves(acc_kg, low) * keep
        psk_ref[...] = jnp.broadcast_to(jnp.where(live, acc_sk, 0.0), (8, LANES)) * (
            lax.broadcasted_iota(jnp.int32, (8, LANES), 0) == 0).astype(F32)

    q_gain, k_gain = jnp.tile(q_gain, (1, 2)), jnp.tile(k_gain, (1, 2))
    last = nb - 1
    cur = lambda w, off: pl.BlockSpec((BLK, w), lambda i: (jnp.minimum(i, last), off // w))
    prev = lambda w, off: pl.BlockSpec((BLK, w), lambda i: (jnp.maximum(i - 1, 0), off // w))
    small = lambda n: pl.BlockSpec((1, 2 * n), lambda i: (0, 0))
    part = pl.BlockSpec((8, 128), lambda i: (i, 0))
    p_shape = jax.ShapeDtypeStruct(((nb + 1) * 8, 128), F32)
    return _pcall(
        body, grid=(nb + 1,),
        in_specs=[cur(BW, OFF_QB), cur(KVW, OFF_KB), prev(KVW, OFF_KB), cur(KVW, OFF_VB), prev(KVW, OFF_VB),
                  pl.BlockSpec((BLK, BW), lambda i: (jnp.minimum(i, last), 0)), small(B_HD), small(B_HD),
                  pl.BlockSpec(memory_space=pltpu.SMEM)],
        out_specs=[pl.BlockSpec((BLK, BW), lambda i: (i, 0)),
                   pl.BlockSpec((BLK, 2 * KVW), lambda i: (jnp.maximum(i - 1, 0), 0)), part, part, part],
        out_shape=[jax.ShapeDtypeStruct((T + BLK, BW), BF16), jax.ShapeDtypeStruct((T, 2 * KVW), BF16),
                   p_shape, p_shape, p_shape],
        scratch_shapes=[pltpu.VMEM((BLK, KVW), F32), pltpu.VMEM((BLK, KVW), F32)], name="swa_bwd",
        semantics=("arbitrary",), args=[proj, proj, proj, proj, proj, dout, q_gain, k_gain, sinks], job=job)


def _branch_merge(ya_pre, attn, wa_t, wb_t, proj, tm, tn, job=None):
    T = ya_pre.shape[0]

    def body(a_ref, b_ref, wa_ref, wb_ref, ga_ref, gb_ref, ya_ref, yb_ref, mg_ref):
        ya = lax.dot_general(a_ref[...], wa_ref[...], NT, preferred_element_type=F32)
        yb = lax.dot_general(b_ref[...], wb_ref[...], NT, preferred_element_type=F32)
        ya_ref[...] = ya.astype(BF16)
        yb_ref[...] = yb.astype(BF16)
        mg_ref[...] = (_sigmoid(ga_ref[...]) * ya + _sigmoid(gb_ref[...]) * yb).astype(BF16)

    o_spec = pl.BlockSpec((tm, tn), lambda i, j: (i, j))
    o_shape = jax.ShapeDtypeStruct((T, D), BF16)
    return _pcall(
        body, grid=(T // tm, D // tn),
        in_specs=[pl.BlockSpec((tm, AW), lambda i, j: (i, 0)), pl.BlockSpec((tm, BW), lambda i, j: (i, 0)),
                  pl.BlockSpec((tn, AW), lambda i, j: (j, 0)), pl.BlockSpec((tn, BW), lambda i, j: (j, 0)),
                  pl.BlockSpec((tm, tn), lambda i, j: (i, OFF_GTA // tn + j)),
                  pl.BlockSpec((tm, tn), lambda i, j: (i, OFF_GTB // tn + j))],
        out_specs=[o_spec, o_spec, o_spec], out_shape=[o_shape, o_shape, o_shape], scratch_shapes=[], name="branch_merge",
        semantics=("parallel", "parallel"), args=[ya_pre, attn, wa_t, wb_t, proj, proj], job=job)


def _ij(i, j, k):
    return (i, j)


def _local_step(x, tgt, mod, g1, g2, lbl, og, qg, kg, sk, shards, c_arr, update):
    win_s, wa_s, wb_s, wout_s, wmi_s, wmo_s = shards
    T = x.shape[0]
    tm, tr, tt = min(1024, T), min(256, T), min(2048, T)
    tk_t = min(1024, T)
    tn = 512
    sh1, sc1, gt1, sh2, sc2, gt2 = (mod[:, i * D:(i + 1) * D] for i in range(N_MOD))
    nI = T // tm
    blk = (tm, tn)

    win_lo_s, win_hi_s = win_s[:, :D // 2], win_s[:, D // 2:]
    (h,), (win_lo,) = _rms_mod_fwd("rms1_fwd", x, g1, sc1, sh1, tr, job=_gather_relay_job([win_lo_s], alone=True))

    def epi_store(acc, ex, ou):
        ou[0][...] = acc.astype(ou[0].dtype)

    tm2 = min(2048, T)
    blk2 = (tm2, tn)

    full = lambda s: (0, s.shape[0])
    last = wmi_s.shape[0]
    gather = _gather_relay_job
    def epi_add(acc, ex, ou):
        ou[0][...] = acc + ex[0][...]

    (proj_lo,), (win_hi,) = _mm(
        "in_proj_lo", "nt", [(h, D // 2)], win_lo, T, IN_W, D // 2, tm2, tn, D // 2, [], [((T, IN_W), F32, blk2, _ij)],
        epi_store, job=gather([win_hi_s]))
    (proj,), (wa_t, wb_t, w_out, wmi_part) = _mm(
        "in_proj_hi", "nt", [(h, D // 2)], win_hi, T, IN_W, D // 2, tm2, tn, D // 2, [(proj_lo, blk2, _ij)],
        [((T, IN_W), F32, blk2, _ij)], epi_add, a_first=1,
        job=gather([wa_s, wb_s, wout_s, wmi_s], rows=[full(wa_s), full(wb_s), full(wout_s), (0, MI_CUT)]))
    win_t = [win_lo, win_hi]
    (ya_pre, st), _ = _hgrn_fwd(proj, lbl, og, tt)
    (attn,), (wmi_t,) = _swa_fwd(proj, qg, kg, sk, job=gather([wmi_s], rows=[(MI_CUT, last)], into=[wmi_part]))
    (ya, yb, merged), _ = _branch_merge(ya_pre, attn, wa_t, wb_t, proj, tm, tn)

    def residual_rows(acc, vecs, bufs, parts):
        gt, gain, sc, sh = (v[...] for v in vecs)
        x_buf, mo_buf, h2_buf = bufs
        rows = min(64, tm)
        for r0 in range(0, tm, rows):
            rs = slice(r0, r0 + rows)
            z = acc[rs, :]
            mo_buf[rs, :] = z.astype(BF16)
            x1v = x_buf[rs, :] + gt * z
            x_buf[rs, :] = x1v
            rstd = lax.rsqrt(jnp.mean(x1v * x1v, axis=-1, keepdims=True) + EPS)
            h2_buf[rs, :] = ((x1v * rstd * gain) * (1.0 + sc) + sh).astype(BF16)

    x1, mo, h2 = _rows_mm("out_proj", [(merged, D)], w_out, T, tm, min(1024, D), [gt1, g2, sc2, sh2],
                          [(F32, x, True), (BF16, None, True), (BF16, None, True)], [], residual_rows)

    def epi_relu2(acc, ex, ou):
        r = jnp.maximum(acc, 0.0)
        ou[0][...] = r.astype(BF16)
        ou[1][...] = (r * r).astype(BF16)

    (r, a), (w_mo,) = _mm("mlp_in", "nt", [(h2, D)], wmi_t, T, HID, D, tm2, tn, D, [],
                          [((T, HID), BF16, blk2, _ij), ((T, HID), BF16, blk2, _ij)], epi_relu2,
                          job=gather([wmo_s]))

    def loss_rows(acc, vecs, bufs, parts):
        gt = vecs[0][...]
        x1_buf, t_buf, dz_buf = bufs
        rows = min(64, tm)
        loss_sum, gate_sum = jnp.zeros((8, D), F32), jnp.zeros((8, D), F32)
        for r0 in range(0, tm, rows):
            rs = slice(r0, r0 + rows)
            z = acc[rs, :]
            e = x1_buf[rs, :] + gt * z - t_buf[rs, :]
            dy = e * (1.0 / D)
            t_buf[rs, :] = dy
            dz_buf[rs, :] = (gt * dy).astype(BF16)
            loss_sum = loss_sum + _fold8(e * e)
            gate_sum = gate_sum + _fold8(dy * z)
        parts[0][...] = loss_sum * (0.5 / D)
        parts[1][...] = gate_sum

    part_rows = ((nI * 8, D), F32, (8, D), lambda i, j, k: (i, 0))
    dy, dz, p_loss, p_gt2 = _rows_mm(
        "mlp_out", [(a, HID)], w_mo, T, tm, 1024, [gt2], [(F32, x1, False), (F32, tgt, True), (BF16, None, True)],
        [part_rows, part_rows], loss_rows)

    def epi_du(acc, ex, ou):
        ou[0][...] = (acc * (2.0 * ex[0][...].astype(F32))).astype(BF16)

    (du,) = _mm("mlp_out_dx", "nt", [(dz, D)], w_mo, T, HID, D, tm2, tn, D, [(r, blk2, _ij)],
                [((T, HID), BF16, blk2, _ij)], epi_du)
    gblk = (1024, 1024)
    gwide = (1024, D)
    pair_sum = lambda nm, g, r1: _pair_sum("pair_sum_" + nm, g, r1, c_arr, _sum_rows(r1.shape[1]))
    (g_mo,) = _mm("mlp_out_dw", "tn", [(a, HID)], dz, HID, D, T, 1024, D, tk_t, [], [((HID, D), BF16, gwide, _ij)], epi_store)
    (dh2,), (r1_mo,) = _mm("mlp_in_dx", "nn", [(du, HID)], wmi_t, T, D, HID, tm, D, 1024, [],
                           [((T, D), F32, (tm, D), _ij)], epi_store, job=_pair_job([g_mo]))
    dx1, p_sh2, p_sc2, p_g2, dmo, p_gt1 = _rms_mod_bwd("rms2_bwd", dh2, x1, g2, sc2, dy, tr, gate=gt1, mo=mo)
    s_mo = pair_sum("mlp_out", g_mo, r1_mo)
    near, far = (1, 2), (3,)
    cut = s_mo.shape[1] // 2
    (g_mi,), (rn_mo, rf_mo) = _mm(
        "mlp_in_dw", "tn", [(du, HID)], h2, HID, D, T, 1024, D, tk_t, [], [((HID, D), BF16, gwide, _ij)], epi_store,
        job=_both(_chip_job([s_mo], near), _chip_job([s_mo], far, rows=(0, cut))))

    def epi_gates(acc, ex, ou):
        ya_ref, yb_ref, ga_ref, gb_ref = ex
        sa, sb = _sigmoid(ga_ref[...]), _sigmoid(gb_ref[...])
        ou[0][...] = (acc * sa).astype(BF16)
        ou[1][...] = (acc * sb).astype(BF16)
        ou[2][...] = (acc * ya_ref[...].astype(F32) * (sa * (1.0 - sa))).astype(BF16)
        ou[3][...] = (acc * yb_ref[...].astype(F32) * (sb * (1.0 - sb))).astype(BF16)

    o_bf = ((T, D), BF16, blk, _ij)
    (dya, dyb, dga, dgb), (rf_mo, r1_mi) = _mm(
        "out_proj_dx", "nt", [(dmo, D)], w_out, T, D, D, tm, tn, D,
        [(ya, blk, _ij), (yb, blk, _ij), (proj, blk, lambda i, j, k: (i, OFF_GTA // tn + j)),
         (proj, blk, lambda i, j, k: (i, OFF_GTB // tn + j))], [o_bf, o_bf, o_bf, o_bf], epi_gates,
        job=_both(_chip_job([s_mo], far, rows=(cut, 2 * cut), into=[rf_mo]), _pair_job([g_mi])))
    s_mi = pair_sum("mlp_in", g_mi, r1_mi)
    (g_out,) = _mm("out_proj_dw", "tn", [(merged, D)], dmo, D, D, T, 1024, 1024, tk_t, [], [((D, D), BF16, gblk, _ij)], epi_store)
    dya_pre, dattn = _twin_mm("branch_dx", "nn", [(dya, wa_t), (dyb, wb_t)], T, AW, D, tm, tn, D, F32)
    g_a, g_b = _twin_mm("branch_dw", "tn", [(dya, ya_pre), (dyb, attn)], D, AW, T, 1024, 1024, tk_t, BF16)
    (dqa, dfa, dia, dgg, p_lb, p_og), (rn_mi, r1_out, r1_a, r1_b) = _hgrn_bwd(
        proj, st, dya_pre, lbl, og, tt, job=_both(_chip_job([s_mi], near), _pair_job([g_out, g_a, g_b])))
    (dqb, dkv, p_qg, p_kg, p_sk), (rf_mi,) = _swa_bwd(proj, dattn, qg, kg, sk, job=_chip_job([s_mi], far))
    s_out, s_a, s_b = pair_sum("out", g_out, r1_out), pair_sum("branch_a", g_a, r1_a), pair_sum("branch_b", g_b, r1_b)
    pieces = [(dqa, AW), (dfa, AW), (dia, AW), (dgg, AW), (dqb, BW), (dkv, 2 * KVW), (dga, D), (dgb, D)]
    (g_in,), (r2_out, r2_a, r2_b) = _pieces_tn("in_proj_dw", pieces, h, 512, job=_chip_job([s_out, s_a, s_b]))
    (r1_in,) = update("w_mlp_in", s_mi, [rn_mi, rf_mi], job=_pair_job([g_in]))
    s_in = pair_sum("in", g_in, r1_in)
    (dx, p_sh1, p_sc1, p_g1), (r2_in,) = _pieces_nn_rms(
        "in_proj_dx", pieces, win_t, x, g1, sc1, dx1, tm, 512, job=_chip_job([s_in]))

    partials = dict(sh1=p_sh1, sc1=p_sc1, gt1=p_gt1, sh2=p_sh2, sc2=p_sc2, gt2=p_gt2, g1=p_g1, g2=p_g2,
                    lb=p_lb, og=p_og, qg=p_qg, kg=p_kg, sk=p_sk, loss=p_loss)
    sums = dict(w_in=(s_in, [r2_in]), w_branch_a=(s_a, [r2_a]), w_branch_b=(s_b, [r2_b]), w_out=(s_out, [r2_out]),
                w_mlp_in=(s_mi, [rn_mi, rf_mi]), w_mlp_out=(s_mo, [rn_mo, rf_mo]))
    return dx, sums, partials


def _exchange_slots(buf, send_sems, recv_sems):
    me = _mesh_pos()
    mine = buf.at[_index(me)]
    sends = []
    for k in range(1, N_DEV):
        cp = pltpu.make_async_remote_copy(src_ref=mine, dst_ref=mine, send_sem=send_sems.at[k - 1],
                                          recv_sem=recv_sems.at[k - 1], device_id=_flip(me, k), device_id_type=MESH)
        cp.start()
        sends.append(cp)
    for k in range(1, N_DEV):
        theirs = buf.at[_index(_flip(me, k))]
        pltpu.make_async_remote_copy(src_ref=theirs, dst_ref=theirs, send_sem=send_sems.at[k - 1],
                                     recv_sem=recv_sems.at[k - 1], device_id=_flip(me, k), device_id_type=MESH).wait_recv()
    for cp in sends:
        cp.wait_send()


ADA_W = N_MOD * D // N_DEV


def _ada_mod(c, w_ada, b_shard):
    def body(c_ref, w_ref, b_ref, mod_ref, sc_ref, cbuf, mbuf, s1, r1, s2, r2):
        me = _index(_mesh_pos())
        cbuf[me] = c_ref[...]
        _exchange_slots(cbuf, s1, r1)
        row = lax.broadcasted_iota(jnp.int32, (N_DEV, D), 0)
        call = jnp.zeros((N_DEV, D), F32)
        for d in range(N_DEV):
            call = jnp.where(row == d, cbuf[d], call)
        sc = call * _sigmoid(call)
        sc_ref[...] = sc
        mbuf[me] = _dot(sc, w_ref[...]) + b_ref[...]
        _exchange_slots(mbuf, s2, r2)
        for s in range(N_DEV):
            mod_ref[:, s * ADA_W:(s + 1) * ADA_W] = mbuf[s, pl.ds(me, 1), :]

    return pl.pallas_call(
        body, in_specs=[_VMEM, _VMEM, _VMEM], out_specs=[_VMEM, _VMEM],
        out_shape=[jax.ShapeDtypeStruct((1, N_MOD * D), F32), jax.ShapeDtypeStruct((N_DEV, D), F32)],
        scratch_shapes=[pltpu.VMEM((N_DEV, 1, D), F32), pltpu.VMEM((N_DEV, N_DEV, ADA_W), F32),
                        _SEMS(N_DEV - 1), _SEMS(N_DEV - 1), _SEMS(N_DEV - 1), _SEMS(N_DEV - 1)],
        name="ada_mod", compiler_params=pltpu.CompilerParams(vmem_limit_bytes=VMEM_LIMIT),
    )(c, w_ada, b_shard)


SMALL_SEGS = (("b_ada", N_MOD * D), ("norm1_gain", D), ("norm2_gain", D), ("lb0", AW), ("lb1", AW),
              ("hgrn_o_gain", AW), ("q_norm_gain", 128), ("k_norm_gain", 128), ("sinks", 128))
SMALL_W = sum(w for _, w in SMALL_SEGS)
X_SEGS = (("sh1", D), ("sc1", D), ("gt1", D), ("sh2", D), ("sc2", D), ("gt2", D), ("g1", D), ("g2", D),
          ("lb", AW), ("og", AW), ("qg", 128), ("kg", 128), ("sk", 128), ("loss", 128))
X_W = sum(w for _, w in X_SEGS)


def _offsets(segs):
    out, o = {}, 0
    for name, w in segs:
        out[name] = (o, w)
        o += w
    return out


def _small_reduce(parts, lb_logits):
    xo, so = _offsets(X_SEGS), _offsets(SMALL_SEGS)
    names = [nm for nm, _ in X_SEGS]

    def body(*refs):
        p_refs = dict(zip(names, refs[:len(names)]))
        lbl_ref, allx, gs_ref, loss_ref, send_sems, recv_sems = refs[len(names):]
        me = _index(_mesh_pos())
        for nm, (o, w) in xo.items():
            if nm == "loss":
                allx[me, :, o:o + w] = jnp.broadcast_to(jnp.sum(p_refs[nm][...]), (1, w))
            else:
                allx[me, :, o:o + w] = jnp.sum(p_refs[nm][...], axis=0, keepdims=True)
        _exchange_slots(allx, send_sems, recv_sems)
        tot = allx[0]
        for d in range(1, N_DEV):
            tot = tot + allx[d]
        seg = lambda nm: tot[:, xo[nm][0]:xo[nm][0] + xo[nm][1]]

        def put(nm, v):
            gs_ref[:, so[nm][0]:so[nm][0] + so[nm][1]] = v

        put("b_ada", tot[:, 0:N_MOD * D])
        put("norm1_gain", seg("g1"))
        put("norm2_gain", seg("g2"))
        lbl = lbl_ref[...]
        lb = _sigmoid(lbl[0:1, :] - lbl[1:2, :])
        dl0 = seg("lb") * lb * (1.0 - lb)
        put("lb0", dl0)
        put("lb1", -dl0)
        put("hgrn_o_gain", seg("og"))
        put("q_norm_gain", seg("qg"))
        put("k_norm_gain", seg("kg"))
        put("sinks", seg("sk"))
        loss_ref[...] = seg("loss")

    return pl.pallas_call(
        body, in_specs=[_VMEM] * (len(names) + 1), out_specs=[_VMEM, _VMEM, _VMEM],
        out_shape=[jax.ShapeDtypeStruct((N_DEV, 1, X_W), F32), jax.ShapeDtypeStruct((1, SMALL_W), F32),
                   jax.ShapeDtypeStruct((1, 128), F32)],
        scratch_shapes=[_SEMS(N_DEV - 1), _SEMS(N_DEV - 1)], name="small_reduce",
        compiler_params=pltpu.CompilerParams(vmem_limit_bytes=VMEM_LIMIT),
    )(*[parts[nm] for nm in names], lb_logits)


def _adamw_math(w, g, m, v):
    m = B1 * m + (1.0 - B1) * g
    v = B2 * v + (1.0 - B2) * (g * g)
    m_hat = m / (1.0 - B1 ** STEP)
    v_hat = v / (1.0 - B2 ** STEP)
    return -LR * (m_hat / (jnp.sqrt(v_hat) + ADAM_EPS) + WD * w), m, v


def _sum_rows(rs):
    return 256 if rs % 256 == 0 else rs // 2


def _pair_sum(name, g, recv, c_arr, tr):
    _, rs, cols = recv.shape
    blk = (1, tr, cols)

    def body(c_ref, g_ref, r_ref, o_ref):
        o_ref[...] = (g_ref[...].astype(F32) + r_ref[...].astype(F32)).astype(BF16)

    grid_spec = pltpu.PrefetchScalarGridSpec(
        num_scalar_prefetch=1, grid=(4, rs // tr),
        in_specs=[pl.BlockSpec(blk, lambda q, i, c: (2 * q + c[0], i, 0)), pl.BlockSpec(blk, lambda q, i, c: (q, i, 0))],
        out_specs=pl.BlockSpec(blk, lambda q, i, c: (q, i, 0)))
    return pl.pallas_call(body, grid_spec=grid_spec, out_shape=jax.ShapeDtypeStruct((4, rs, cols), BF16), name=name,
                          compiler_params=_params(("parallel", "parallel")))(c_arr, g.reshape(N_DEV, rs, cols), recv)


def _sum_adamw(name, sums, recvs, q_arr, w, m, v, transposed, tile, job=None):
    rows, cols = w.shape
    nR = len(recvs)

    def body(q_ref, s_ref, *refs):
        r_refs = refs[:nR]
        w_ref, m_ref, v_ref, g_ref, d_ref, nm_ref, nv_ref = refs[nR:]
        g = s_ref[0].astype(F32)
        for r_ref in r_refs:
            for slot in range(r_ref.shape[0]):
                g = g + r_ref[slot].astype(F32)
        g = g.T if transposed else g
        g_ref[...] = g
        d_ref[...], nm_ref[...], nv_ref[...] = _adamw_math(w_ref[...], g, m_ref[...], v_ref[...])

    if transposed:
        slab = lambda n, first: pl.BlockSpec((n, cols, tile), lambda i, q: (first(q), 0, i))
    else:
        slab = lambda n, first: pl.BlockSpec((n, tile, cols), lambda i, q: (first(q), i, 0))
    spec = pl.BlockSpec((tile, cols), lambda i, q: (i, 0))
    shape = jax.ShapeDtypeStruct((rows, cols), F32)
    res, job_res = _pcall(
        body, grid=(rows // tile,),
        in_specs=[slab(1, lambda q: q[0])] + [slab(r.shape[0], lambda q: 0) for r in recvs] + [spec] * 3,
        out_specs=[spec] * 4, out_shape=[shape] * 4, scratch_shapes=[], name=name, semantics=("parallel",),
        args=[sums, *recvs, w, m, v], job=job, prefetch=[q_arr])
    return res if job is None else (res, job_res)


def _adamw(name, w, g, m, v, tr):
    rows, cols = w.shape

    def body(w_ref, g_ref, m_ref, v_ref, d_ref, nm_ref, nv_ref):
        d_ref[...], nm_ref[...], nv_ref[...] = _adamw_math(w_ref[...], g_ref[...], m_ref[...], v_ref[...])

    spec = pl.BlockSpec((tr, cols), lambda i: (i, 0))
    shape = jax.ShapeDtypeStruct((rows, cols), F32)
    return pl.pallas_call(
        body, grid=(rows // tr,), in_specs=[spec] * 4, out_specs=[spec] * 3, out_shape=[shape] * 3, name=name,
        compiler_params=_params(("parallel",)),
    )(w, g, m, v)


def _ada_update(sc_t, dmod_cols, w, m, v, tr):
    rows, cols = w.shape

    def body(s_ref, d_ref, w_ref, m_ref, v_ref, g_ref, dl_ref, nm_ref, nv_ref):
        g = jnp.dot(s_ref[...], d_ref[...], precision=lax.Precision.HIGHEST, preferred_element_type=F32)
        g_ref[...] = g
        dl_ref[...], nm_ref[...], nv_ref[...] = _adamw_math(w_ref[...], g, m_ref[...], v_ref[...])

    spec = pl.BlockSpec((tr, cols), lambda i: (i, 0))
    shape = jax.ShapeDtypeStruct((rows, cols), F32)
    return pl.pallas_call(
        body, grid=(rows // tr,),
        in_specs=[pl.BlockSpec((tr, N_DEV), lambda i: (i, 0)), pl.BlockSpec((N_DEV, cols), lambda i: (0, 0)), spec, spec, spec],
        out_specs=[spec] * 4, out_shape=[shape] * 4, name="ada_update", compiler_params=_params(("parallel",)),
    )(sc_t, dmod_cols, w, m, v)


BIG = ("w_in", "w_branch_a", "w_branch_b", "w_out", "w_mlp_in", "w_mlp_out")
COLUMN_SHARDED = ("w_in", "w_branch_a", "w_branch_b", "w_mlp_in")
AS_TRANSPOSE = ("w_in",)
WEIGHTS = ("w_ada", "b_ada", "norm1_gain", "w_in", "lb_logits", "hgrn_o_gain", "q_norm_gain", "k_norm_gain", "sinks",
           "w_branch_a", "w_branch_b", "w_out", "norm2_gain", "w_mlp_in", "w_mlp_out")


def _to_bf16(name, w, transposed, tile=256):
    rows, cols = w.shape

    def body(w_ref, o_ref):
        v = w_ref[...]
        o_ref[...] = (v.T if transposed else v).astype(BF16)

    out_spec = pl.BlockSpec((cols, tile), lambda i: (0, i)) if transposed else pl.BlockSpec((tile, cols), lambda i: (i, 0))
    return pl.pallas_call(
        body, grid=(rows // tile,), in_specs=[pl.BlockSpec((tile, cols), lambda i: (i, 0))], out_specs=out_spec,
        out_shape=jax.ShapeDtypeStruct((cols, rows) if transposed else (rows, cols), BF16), name=name,
        compiler_params=_params(("parallel",)))(w)


def _pack_small(p):
    lb = p["lb_logits"]
    src = dict(p, lb0=lb[0:1], lb1=lb[1:2])
    return jnp.concatenate([jnp.pad(src[nm], ((0, 0), (0, w - src[nm].shape[1]))) for nm, w in SMALL_SEGS], axis=1)


def _unpack_small(vec, shapes):
    so = _offsets(SMALL_SEGS)
    out = {}
    for nm, shp in shapes.items():
        if nm == "lb_logits":
            o = so["lb0"][0]
            out[nm] = vec[0, o:o + 2 * AW].reshape(2, AW)
        else:
            o = so[nm][0]
            out[nm] = vec[:, o:o + shp[1]]
    return out


def kernel(x, c, w_ada, b_ada, norm1_gain, w_in, lb_logits, hgrn_o_gain, q_norm_gain, k_norm_gain, sinks, w_branch_a, w_branch_b, w_out, norm2_gain, w_mlp_in, w_mlp_out, loss_target, m_w_ada, m_b_ada, m_norm1_gain, m_w_in, m_lb_logits, m_hgrn_o_gain, m_q_norm_gain, m_k_norm_gain, m_sinks, m_w_branch_a, m_w_branch_b, m_w_out, m_norm2_gain, m_w_mlp_in, m_w_mlp_out, v_w_ada, v_b_ada, v_norm1_gain, v_w_in, v_lb_logits, v_hgrn_o_gain, v_q_norm_gain, v_k_norm_gain, v_sinks, v_w_branch_a, v_w_branch_b, v_w_out, v_norm2_gain, v_w_mlp_in, v_w_mlp_out):
    w = dict(w_ada=w_ada, b_ada=b_ada, norm1_gain=norm1_gain, w_in=w_in, lb_logits=lb_logits, hgrn_o_gain=hgrn_o_gain,
             q_norm_gain=q_norm_gain, k_norm_gain=k_norm_gain, sinks=sinks, w_branch_a=w_branch_a, w_branch_b=w_branch_b,
             w_out=w_out, norm2_gain=norm2_gain, w_mlp_in=w_mlp_in, w_mlp_out=w_mlp_out)
    m = dict(w_ada=m_w_ada, b_ada=m_b_ada, norm1_gain=m_norm1_gain, w_in=m_w_in, lb_logits=m_lb_logits,
             hgrn_o_gain=m_hgrn_o_gain, q_norm_gain=m_q_norm_gain, k_norm_gain=m_k_norm_gain, sinks=m_sinks,
             w_branch_a=m_w_branch_a, w_branch_b=m_w_branch_b, w_out=m_w_out, norm2_gain=m_norm2_gain,
             w_mlp_in=m_w_mlp_in, w_mlp_out=m_w_mlp_out)
    v = dict(w_ada=v_w_ada, b_ada=v_b_ada, norm1_gain=v_norm1_gain, w_in=v_w_in, lb_logits=v_lb_logits,
             hgrn_o_gain=v_hgrn_o_gain, q_norm_gain=v_q_norm_gain, k_norm_gain=v_k_norm_gain, sinks=v_sinks,
             w_branch_a=v_w_branch_a, w_branch_b=v_w_branch_b, w_out=v_w_out, norm2_gain=v_norm2_gain,
             w_mlp_in=v_w_mlp_in, w_mlp_out=v_w_mlp_out)
    for d in (w, m, v):
        for nm in ("w_ada",) + BIG:
            d[nm] = d[nm][0]
    px, py, pc = _mesh_pos()
    me = _index((px, py, pc))
    c_arr = jnp.reshape(pc, (1,)).astype(jnp.int32)
    q_arr = jnp.reshape(2 * px + py, (1,)).astype(jnp.int32)

    shards = [_to_bf16("shard_" + nm, w[nm].T, False, w[nm].shape[1] // 4) if nm in AS_TRANSPOSE else
              _to_bf16("shard_" + nm, w[nm], nm in COLUMN_SHARDED) for nm in BIG]
    b_shard = lax.dynamic_slice(b_ada, (0, me * ADA_W), (1, ADA_W))
    mod, sc_all = _ada_mod(c, w["w_ada"], b_shard)

    grad, delta, new_m, new_v = {}, {}, {}, {}

    def update(nm, s, recvs, job=None):
        if nm in AS_TRANSPOSE:
            res = _sum_adamw("adamw_" + nm, s, recvs, q_arr, w[nm].T, m[nm].T, v[nm].T, False, w[nm].shape[1] // 4, job=job)
        else:
            res = _sum_adamw("adamw_" + nm, s, recvs, q_arr, w[nm], m[nm], v[nm], nm in COLUMN_SHARDED, 128, job=job)
        res, job_res = res if job is not None else (res, [])
        res = [t.T for t in res] if nm in AS_TRANSPOSE else res
        grad[nm], delta[nm], new_m[nm], new_v[nm] = res
        return job_res

    dx, sums, parts = _local_step(x[0], loss_target[0], mod, norm1_gain, norm2_gain, lb_logits, hgrn_o_gain,
                                  q_norm_gain, k_norm_gain, sinks, shards, c_arr, update)
    for nm in BIG:
        if nm not in grad:
            update(nm, *sums[nm])

    allx, g_small, loss = _small_reduce(parts, lb_logits)

    dmod_cols = lax.dynamic_slice(allx[:, 0, :], (0, me * ADA_W), (N_DEV, ADA_W))
    grad["w_ada"], delta["w_ada"], new_m["w_ada"], new_v["w_ada"] = _ada_update(
        sc_all.T, dmod_cols, w["w_ada"], m["w_ada"], v["w_ada"], 256)

    small_names = [nm for nm in WEIGHTS if nm not in BIG and nm != "w_ada"]
    shapes = {nm: w[nm].shape for nm in small_names}
    ds, ms, vs = _adamw("adamw_small", _pack_small(w), g_small, _pack_small(m), _pack_small(v), 1)
    for dst, vec in ((grad, g_small), (delta, ds), (new_m, ms), (new_v, vs)):
        dst.update(_unpack_small(vec, shapes))

    def full(d, nm):
        return d[nm][None] if nm in BIG or nm == "w_ada" else d[nm]

    return (loss[0, 0], dx[None], *[full(grad, nm) for nm in WEIGHTS], *[full(delta, nm) for nm in WEIGHTS],
            *[full(new_m, nm) for nm in WEIGHTS], *[full(new_v, nm) for nm in WEIGHTS])
```

```python
import functools

import jax
import jax.numpy as jnp
from jax import lax
from jax.experimental import pallas as pl
from jax.experimental.pallas import tpu as pltpu

F32 = jnp.float32
BF16 = jnp.bfloat16
MESH = pl.DeviceIdType.MESH

N_DEV = 8
D = 2048
A_HEADS, A_HD, CHUNK = 8, 128, 64
AW = A_HEADS * A_HD
Q_HEADS, KV_HEADS, GROUP, B_HD, BLK = 16, 4, 4, 64, 128
BW = Q_HEADS * B_HD
KVW = KV_HEADS * B_HD
HID = 4 * D
IN_W = 4 * AW + BW + 2 * KVW + 2 * D
OFF_QA, OFF_FA, OFF_IA, OFF_GA = 0, AW, 2 * AW, 3 * AW
OFF_QB = 4 * AW
OFF_KB = OFF_QB + BW
OFF_VB = OFF_KB + KVW
OFF_GTA = OFF_VB + KVW
OFF_GTB = OFF_GTA + D
N_MOD = 6
EPS = 1e-6
LR, B1, B2, ADAM_EPS, WD, STEP = 1e-3, 0.9, 0.999, 1e-8, 0.01, 10
NEG = -1e30

VMEM_LIMIT = 56 * 1024 * 1024
MI_CUT = 352

NN = (((1,), (0,)), ((), ()))
NT = (((1,), (1,)), ((), ()))
TN = (((0,), (0,)), ((), ()))
BNN = (((2,), (1,)), ((0,), (0,)))
BNT = (((2,), (2,)), ((0,), (0,)))
BTN = (((1,), (1,)), ((0,), (0,)))


def _dot(a, b, dims=NN):
    return lax.dot_general(a.astype(BF16), b.astype(BF16), dims, preferred_element_type=F32)


def _params(sem):
    return pltpu.CompilerParams(dimension_semantics=sem, vmem_limit_bytes=VMEM_LIMIT)


def _sigmoid(x):
    return jax.nn.sigmoid(x)


def _fold8(v):
    r, n = v.shape
    return jnp.sum(v.reshape(r // 8, 8, n), axis=0)


_VMEM = pl.BlockSpec(memory_space=pltpu.VMEM)
_ANY = pl.BlockSpec(memory_space=pl.ANY)
_SEMS = lambda n: pltpu.SemaphoreType.DMA((n,))


def _mesh_pos():
    return lax.axis_index("x"), lax.axis_index("y"), lax.axis_index("c")


def _flip(pos, k):
    return tuple(1 - p if (k >> s) & 1 else p for p, s in zip(pos, (2, 1, 0)))


def _index(pos):
    return 4 * pos[0] + 2 * pos[1] + pos[2]


class _Job:
    def __init__(self, ins, out_shape, sems, start, finish, aliases=None, middle=None):
        self.ins, self.out_shape, self.sems, self.start, self.finish = list(ins), list(out_shape), list(sems), start, finish
        self.aliases = dict(aliases or {})
        self.middle = middle


def _both(j1, j2):
    assert j1.middle is None and j2.middle is None
    n_in, n_out, n_sem = len(j1.ins), len(j1.out_shape), len(j1.sems)
    aliases = dict(j1.aliases, **{n_in + i: n_out + o for i, o in j2.aliases.items()})
    first = lambda ins, outs, sems: (ins[:n_in], outs[:n_out], sems[:n_sem])
    second = lambda ins, outs, sems: (ins[n_in:], outs[n_out:], sems[n_sem:])

    def start(*refs):
        j1.start(*first(*refs))
        j2.start(*second(*refs))

    def finish(*refs):
        j1.finish(*first(*refs))
        j2.finish(*second(*refs))

    return _Job(j1.ins + j2.ins, j1.out_shape + j2.out_shape, j1.sems + j2.sems, start, finish, aliases)


def _pcall(body, *, grid, in_specs, out_specs, out_shape, scratch_shapes, name, semantics, args, job=None, prefetch=()):
    n_pre = len(prefetch)

    def call(fn, in_specs_, out_specs_, out_shape_, scratch_, sem, operands, aliases):
        if n_pre:
            spec = pltpu.PrefetchScalarGridSpec(num_scalar_prefetch=n_pre, grid=grid, in_specs=in_specs_,
                                                out_specs=out_specs_, scratch_shapes=scratch_)
            return pl.pallas_call(fn, grid_spec=spec, out_shape=out_shape_, name=name, input_output_aliases=aliases,
                                  compiler_params=_params(sem))(*prefetch, *operands)
        return pl.pallas_call(fn, grid=grid, in_specs=in_specs_, out_specs=out_specs_, out_shape=out_shape_,
                              scratch_shapes=scratch_, name=name, input_output_aliases=aliases,
                              compiler_params=_params(sem))(*operands)

    if job is None:
        return list(call(body, in_specs, out_specs, out_shape, scratch_shapes, semantics, args, {})), []
    n_in, n_out, n_scr = len(in_specs), len(out_specs), len(scratch_shapes)
    j_in, j_out = len(job.ins), len(job.out_shape)
    steps = tuple(grid)

    def carrier(*refs):
        pre, refs = refs[:n_pre], refs[n_pre:]
        o = 0
        main_in, o = refs[o:o + n_in], o + n_in
        job_in, o = refs[o:o + j_in], o + j_in
        main_out, o = refs[o:o + n_out], o + n_out
        job_out, o = refs[o:o + j_out], o + j_out
        main_scr, job_sems = refs[o:o + n_scr], refs[o + n_scr:]
        ids = [pl.program_id(a) for a in range(len(steps))]
        first = functools.reduce(lambda p, q: p & q, [i == 0 for i in ids])
        last = functools.reduce(lambda p, q: p & q, [i == s - 1 for i, s in zip(ids, steps)])

        @pl.when(first)
        def _():
            job.start(job_in, job_out, job_sems)

        if job.middle is not None:
            flat, total = 0, 1
            for i, s in zip(ids, steps):
                flat, total = flat * s + i, total * s

            @pl.when(flat == total * 3 // 5)
            def _():
                job.middle(job_in, job_out, job_sems)

        body(*pre, *main_in, *main_out, *main_scr)

        @pl.when(last)
        def _():
            job.finish(job_in, job_out, job_sems)

    outs = call(carrier, list(in_specs) + [_ANY] * j_in, list(out_specs) + [_ANY] * j_out,
                list(out_shape) + job.out_shape, list(scratch_shapes) + job.sems, ("arbitrary",) * len(steps),
                list(args) + job.ins, {n_pre + n_in + i: n_out + o for i, o in job.aliases.items()})
    return list(outs[:n_out]), list(outs[n_out:])


def _gather_relay_job(shards, rows=None, into=None, alone=False):
    n = len(shards)
    rows = rows or [(0, s.shape[0]) for s in shards]
    into = into or [None] * n
    olds, aliases = [], {}
    for a, buf in enumerate(into):
        if buf is not None:
            aliases[n + len(olds)] = a
            olds.append(buf)

    def tools(ins, outs, sems):
        send_sems, recv_sems, local_sems = sems
        x, y, c = _mesh_pos()
        q = 2 * x + y
        chip_at = lambda rel: (1 - x if rel & 2 else x, 1 - y if rel & 1 else y)

        def part(a, chip, core):
            rs, (r0, r1) = shards[a].shape[0], rows[a]
            return outs[a].at[pl.ds((2 * chip + core) * rs + r0, r1 - r0), :]

        own = lambda a: ins[a].at[pl.ds(rows[a][0], rows[a][1] - rows[a][0]), :]

        def copy(a, slot, chip, core, to, src=None):
            blk = part(a, chip, core)
            return pltpu.make_async_remote_copy(src_ref=blk if src is None else src, dst_ref=blk,
                                                send_sem=send_sems.at[7 * a + slot], recv_sem=recv_sems.at[7 * a + slot],
                                                device_id=to, device_id_type=MESH)

        mine = [pltpu.make_async_copy(own(a), part(a, q, c), local_sems.at[a]) for a in range(n)]
        first = [copy(a, slot, q, c, (x, y, 1 - c) if slot == 0 else (*chip_at(slot), c), src=own(a))
                 for a in range(n) for slot in (0, 1, 2)]
        return x, y, c, q, chip_at, copy, mine, first

    def start(ins, outs, sems):
        *_, mine, first = tools(ins, outs, sems)
        for cp in mine + first:
            cp.start()

    def middle(ins, outs, sems):
        x, y, c, q, chip_at, copy, _, _ = tools(ins, outs, sems)
        me, sib = (x, y, c), (x, y, 1 - c)

        def relay(src, dst):
            for a in range(n):
                copy(a, src, q ^ src, c, me).wait_recv()
                copy(a, 3, q ^ src, c, (*chip_at(dst), c)).start()
                copy(a, 3 + src, q ^ src, c, sib).start()
            for a in range(n):
                copy(a, dst, q ^ dst, c, me).wait_recv()
                copy(a, 3 + dst, q ^ dst, c, sib).start()

        pl.when(c == 1)(lambda: relay(1, 2))
        pl.when(c == 0)(lambda: relay(2, 1))

    def finish(ins, outs, sems):
        if alone:
            middle(ins, outs, sems)
        x, y, c, q, chip_at, copy, mine, first = tools(ins, outs, sems)
        me, sib = (x, y, c), (x, y, 1 - c)
        for a in range(n):
            copy(a, 3, q ^ 3, c, me).wait_recv()
            copy(a, 6, q ^ 3, c, sib).start()
        for a in range(n):
            copy(a, 0, q, 1 - c, me).wait_recv()
            for rel in (1, 2, 3):
                copy(a, 3 + rel, q ^ rel, 1 - c, me).wait_recv()
        for a in range(n):
            for slot in range(3, 7):
                copy(a, slot, q, c, sib).wait_send()
        for cp in first:
            cp.wait_send()
        for cp in mine:
            cp.wait()

    return _Job(list(shards) + olds, [jax.ShapeDtypeStruct((N_DEV * s.shape[0], s.shape[1]), s.dtype) for s in shards],
                [_SEMS(7 * n), _SEMS(7 * n), _SEMS(n)], start, finish, aliases, middle=None if alone else middle)


def _pair_job(grads):
    n = len(grads)

    def copies(ins, outs, sems):
        send_sems, recv_sems = sems
        x, y, c = _mesh_pos()
        out = []
        for a in range(n):
            rs = grads[a].shape[0] // N_DEV
            for q in range(4):
                blk = ins[a].at[pl.ds((2 * q + 1 - c) * rs, rs), :]
                out.append(pltpu.make_async_remote_copy(
                    src_ref=blk, dst_ref=outs[a].at[q], send_sem=send_sems.at[4 * a + q], recv_sem=recv_sems.at[4 * a + q],
                    device_id=(x, y, 1 - c), device_id_type=MESH))
        return out

    def start(ins, outs, sems):
        for cp in copies(ins, outs, sems):
            cp.start()

    def finish(ins, outs, sems):
        for cp in copies(ins, outs, sems):
            cp.wait()

    return _Job(grads, [jax.ShapeDtypeStruct((4, g.shape[0] // N_DEV, g.shape[1]), g.dtype) for g in grads],
                [_SEMS(4 * n), _SEMS(4 * n)], start, finish)


def _chip_job(sums, rels=(1, 2, 3), rows=None, into=None):
    n, nr = len(sums), len(rels)
    r0, r1 = rows or (0, sums[0].shape[1])
    olds = list(into or [])
    aliases = {n + a: a for a in range(len(olds))}

    def copies(ins, outs, sems):
        send_sems, recv_sems = sems
        x, y, c = _mesh_pos()
        out = []
        for a in range(n):
            for slot, r in enumerate(rels):
                px, py = (1 - x if r & 2 else x), (1 - y if r & 1 else y)
                out.append(pltpu.make_async_remote_copy(
                    src_ref=ins[a].at[2 * px + py, pl.ds(r0, r1 - r0), :], dst_ref=outs[a].at[slot, pl.ds(r0, r1 - r0), :],
                    send_sem=send_sems.at[nr * a + slot], recv_sem=recv_sems.at[nr * a + slot],
                    device_id=(px, py, c), device_id_type=MESH))
        return out

    def start(ins, outs, sems):
        for cp in copies(ins, outs, sems):
            cp.start()

    def finish(ins, outs, sems):
        for cp in copies(ins, outs, sems):
            cp.wait()

    return _Job(list(sums) + olds, [jax.ShapeDtypeStruct((nr,) + s.shape[1:], s.dtype) for s in sums],
                [_SEMS(nr * n), _SEMS(nr * n)], start, finish, aliases)


def _mm(name, form, a_list, b, M, N, K, tm, tn, tk, extras, outs, epi, job=None, a_first=0):
    nI, nJ, nK = M // tm, N // tn, K // tk
    assert nI * tm == M and nJ * tn == N and nK * tk == K
    dims = {"nn": NN, "nt": NT, "tn": TN}[form]
    b_list = b if isinstance(b, list) else [(b, {"nn": N, "nt": K, "tn": N}[form])]
    nA, nB = len(a_list), len(b_list)
    assert nA == 1 or nB == 1
    assert nB == 1 or form in ("nn", "nt")
    AXIS = {"i": 0, "j": 1, "k": 2}
    a_axis, a_tile = ("i", tm) if form == "tn" else ("k", tk)
    b_axis, b_tile = ("k", tk) if form == "nt" else ("j", tn)

    def cut(pieces, tile, total):
        starts, s = [], 0
        for _, w in pieces:
            assert w % tile == 0
            starts.append(s // tile)
            s += w
        assert s == total
        return starts, [w // tile for _, w in pieces]

    a_st, a_cn = cut(a_list, a_tile, M if form == "tn" else K)
    b_st, b_cn = cut(b_list, b_tile, K if form == "nt" else N)

    def inside(idx, st, cn):
        return (idx >= st) & (idx < st + cn)

    def a_spec(p):
        st, cn = a_st[p], a_cn[p]
        if form == "tn":
            return pl.BlockSpec((tk, tm), lambda i, j, k: (jnp.where(inside(i, st, cn), k, 0), jnp.clip(i - st, 0, cn - 1)))
        return pl.BlockSpec((tm, tk), lambda i, j, k: (i, a_first + jnp.clip(k - st, 0, cn - 1)))

    def b_spec(p):
        st, cn = b_st[p], b_cn[p]
        if form == "nt":
            return pl.BlockSpec((tn, tk), lambda i, j, k: (j, jnp.clip(k - st, 0, cn - 1)))
        if nB == 1:
            return pl.BlockSpec((tk, tn), lambda i, j, k: (k, j))
        return pl.BlockSpec((tk, tn), lambda i, j, k: (jnp.where(inside(j, st, cn), k, 0), jnp.clip(j - st, 0, cn - 1)))

    in_specs = ([a_spec(p) for p in range(nA)] + [b_spec(p) for p in range(nB)]
                + [pl.BlockSpec(bs, im) for _, bs, im in extras])
    out_shape = [jax.ShapeDtypeStruct(s_, d_) for s_, d_, _, _ in outs]
    out_specs = [pl.BlockSpec(bs, im) for _, _, bs, im in outs]
    nE, nO = len(extras), len(outs)
    single = nA == 1 and nB == 1

    def body(*refs):
        a_refs, b_refs = refs[:nA], refs[nA:nA + nB]
        ex, ou = refs[nA + nB:nA + nB + nE], refs[nA + nB + nE:nA + nB + nE + nO]
        ids = [pl.program_id(a) for a in range(3)]

        def partial_of(p, q):
            return lax.dot_general(a_refs[p][...], b_refs[q][...], dims, preferred_element_type=F32)

        if nK == 1 and single:
            epi(partial_of(0, 0), ex, ou)
            return
        acc = refs[-1]
        k = ids[2]
        for p in range(nA):
            for q in range(nB):
                def first(p=p, q=q):
                    acc[...] = partial_of(p, q)

                def later(p=p, q=q):
                    acc[...] += partial_of(p, q)

                here = None
                if nA > 1:
                    here = inside(ids[AXIS[a_axis]], a_st[p], a_cn[p])
                if nB > 1:
                    here = inside(ids[AXIS[b_axis]], b_st[q], b_cn[q])
                pl.when(k == 0 if here is None else here & (k == 0))(first)
                pl.when(k > 0 if here is None else here & (k > 0))(later)

        @pl.when(k == nK - 1)
        def _():
            epi(acc[...], ex, ou)

    scratch = [] if (nK == 1 and single) else [pltpu.VMEM((tm, tn), F32)]
    res, job_res = _pcall(
        body, grid=(nI, nJ, nK), in_specs=in_specs, out_specs=out_specs, out_shape=out_shape, scratch_shapes=scratch,
        name=name, semantics=("parallel", "parallel", "arbitrary"),
        args=[a for a, _ in a_list] + [p for p, _ in b_list] + [e for e, _, _ in extras], job=job)
    return res if job is None else (res, job_res)


def _twin_mm(name, form, pairs, M, N, K, tm, tn, tk, out_dtype):
    nI, nJ, nK = M // tm, N // tn, K // tk
    dims = {"nn": NN, "tn": TN}[form]
    a_spec = (pl.BlockSpec((tm, tk), lambda i, j, k: (i, k)) if form == "nn" else pl.BlockSpec((tk, tm), lambda i, j, k: (k, i)))
    b_spec = pl.BlockSpec((tk, tn), lambda i, j, k: (k, j))
    o_spec = pl.BlockSpec((tm, tn), lambda i, j, k: (i, j))

    def body(a1, b1, a2, b2, o1, o2, *accs):
        k = pl.program_id(2)
        for a_ref, b_ref, o_ref, acc in ((a1, b1, o1, accs[0] if accs else None), (a2, b2, o2, accs[1] if accs else None)):
            part = lax.dot_general(a_ref[...], b_ref[...], dims, preferred_element_type=F32)
            if nK == 1:
                o_ref[...] = part.astype(out_dtype)
                continue

            @pl.when(k == 0)
            def _(acc=acc, part=part):
                acc[...] = part

            @pl.when(k > 0)
            def _(acc=acc, part=part):
                acc[...] += part

            @pl.when(k == nK - 1)
            def _(acc=acc, o_ref=o_ref):
                o_ref[...] = acc[...].astype(out_dtype)

    (a1, b1), (a2, b2) = pairs
    shape = jax.ShapeDtypeStruct((M, N), out_dtype)
    return pl.pallas_call(
        body, grid=(nI, nJ, nK), in_specs=[a_spec, b_spec, a_spec, b_spec], out_specs=[o_spec, o_spec],
        out_shape=[shape, shape], scratch_shapes=[] if nK == 1 else [pltpu.VMEM((tm, tn), F32)] * 2, name=name,
        compiler_params=_params(("parallel", "parallel", "arbitrary")))(a1, b1, a2, b2)


def _piece_tiles(pieces, tile):
    starts, s = [], 0
    for _, w in pieces:
        assert w % tile == 0
        starts.append(s // tile)
        s += w
    return starts, [w // tile for _, w in pieces], s


def _pieces_tn(name, pieces, b, tile, job=None):
    T, N = b.shape
    st, cn, M = _piece_tiles(pieces, tile)
    nP, nI = len(pieces), M // tile

    def body(*refs):
        p_refs, b_hbm, o_ref = refs[:nP], refs[nP], refs[nP + 1]
        bbuf, abuf, bsem, asem = refs[nP + 2:]
        i = pl.program_id(0)

        def fetch(step, slot):
            for p in range(nP):
                @pl.when((step >= st[p]) & (step < st[p] + cn[p]))
                def _():
                    col = pl.multiple_of((step - st[p]) * tile, tile)
                    pltpu.make_async_copy(p_refs[p].at[pl.ds(0, T), pl.ds(col, tile)], abuf.at[slot], asem.at[slot]).start()

        @pl.when(i == 0)
        def _():
            whole = pltpu.make_async_copy(b_hbm, bbuf, bsem)
            whole.start()
            fetch(0, 0)
            whole.wait()

        @pl.when(i + 1 < nI)
        def _():
            fetch(i + 1, (i + 1) % 2)

        pltpu.make_async_copy(p_refs[0].at[pl.ds(0, T), pl.ds(0, tile)], abuf.at[i % 2], asem.at[i % 2]).wait()
        o_ref[...] = lax.dot_general(abuf[i % 2], bbuf[...], TN, preferred_element_type=F32).astype(BF16)

    res, job_res = _pcall(
        body, grid=(nI,), in_specs=[_ANY] * (nP + 1), out_specs=[pl.BlockSpec((tile, N), lambda i: (i, 0))],
        out_shape=[jax.ShapeDtypeStruct((M, N), BF16)],
        scratch_shapes=[pltpu.VMEM((T, N), b.dtype), pltpu.VMEM((2, T, tile), b.dtype), pltpu.SemaphoreType.DMA, _SEMS(2)],
        name=name, semantics=("arbitrary",), args=[p for p, _ in pieces] + [b], job=job)
    return res if job is None else (res, job_res)


def _rows_mm(name, pieces, w, T, tm, tk, vecs, bufs, parts, epi, job=None):
    st, cn, K = _piece_tiles(pieces, tk)
    nP, nI, nK = len(pieces), T // tm, K // tk
    ws = w if isinstance(w, list) else [w]
    nW = len(ws)
    wd = D // nW
    assert wd * nW == D and wd % 512 == 0
    part_specs = [pl.BlockSpec(bs, lambda i, k, im=im: im(i, 0, k)) for _, _, bs, im in parts]
    n_vec, nB = len(vecs), len(bufs)
    load_ix = [n for n, (_, src, _) in enumerate(bufs) if src is not None]
    store_ix = [n for n, (_, _, store) in enumerate(bufs) if store]
    n_any_in, n_any_out = len(load_ix), len(store_ix)

    def body(*refs):
        p_refs, w_refs = refs[:nP], refs[nP:nP + nW]
        o = nP + nW - 1
        vec_refs = refs[o + 1:o + 1 + n_vec]
        ins = refs[o + 1 + n_vec:o + 1 + n_vec + n_any_in]
        o = o + 1 + n_vec + n_any_in
        hbm_outs, p_outs = refs[o:o + n_any_out], refs[o + n_any_out:o + n_any_out + len(parts)]
        o = o + n_any_out + len(parts)
        acc, abuf = refs[o:o + 2]
        buf_refs = refs[o + 2:o + 2 + nB]
        asem, in_sems, out_sems = refs[-3:]
        i, k = pl.program_id(0), pl.program_id(1)
        g = i * nK + k
        rows_of = lambda ref, ii: ref.at[pl.ds(pl.multiple_of(ii * tm, tm), tm), :]
        bufs_in = [buf_refs[n] for n in load_ix]
        bufs_out = [buf_refs[n] for n in store_ix]

        def fetch(ii, kk, slot):
            for p in range(nP):
                @pl.when((kk >= st[p]) & (kk < st[p] + cn[p]))
                def _():
                    col = pl.multiple_of((kk - st[p]) * tk, tk)
                    src = p_refs[p].at[pl.ds(pl.multiple_of(ii * tm, tm), tm), pl.ds(col, tk)]
                    pltpu.make_async_copy(src, abuf.at[slot], asem.at[slot]).start()

        loads = lambda ii: [pltpu.make_async_copy(rows_of(src, ii), buf, in_sems.at[n])
                            for n, (src, buf) in enumerate(zip(ins, bufs_in))]
        stores = lambda ii: [pltpu.make_async_copy(buf, rows_of(dst, ii), out_sems.at[n])
                             for n, (buf, dst) in enumerate(zip(bufs_out, hbm_outs))]

        @pl.when(g == 0)
        def _():
            fetch(0, 0, 0)

        @pl.when(g + 1 < nI * nK)
        def _():
            last_k = k == nK - 1
            fetch(jnp.where(last_k, i + 1, i), jnp.where(last_k, 0, k + 1), (g + 1) % 2)

        @pl.when(k == 0)
        def _():
            @pl.when(i > 0)
            def _():
                for cp in stores(i - 1):
                    cp.wait()
            for cp in loads(i):
                cp.start()

        pltpu.make_async_copy(p_refs[0].at[pl.ds(0, tm), pl.ds(0, tk)], abuf.at[g % 2], asem.at[g % 2]).wait()

        def product(cols):
            c0 = cols.start % wd
            return jnp.dot(abuf[g % 2], w_refs[cols.start // wd][:, c0:c0 + 512], preferred_element_type=F32)

        col_blocks = [slice(c0, c0 + 512) for c0 in range(0, D, 512)]

        @pl.when(k == 0)
        def _():
            for cols in col_blocks:
                acc[:, cols] = product(cols)

        @pl.when(k > 0)
        def _():
            for cols in col_blocks:
                acc[:, cols] += product(cols)

        @pl.when(k == nK - 1)
        def _():
            for cp in loads(i):
                cp.wait()
            epi(acc, vec_refs, buf_refs, p_outs)
            for cp in stores(i):
                cp.start()

            @pl.when(i == nI - 1)
            def _():
                for cp in stores(i):
                    cp.wait()

    vec = pl.BlockSpec((1, D), lambda i, k: (0, 0))
    scratch = ([pltpu.VMEM((tm, D), F32), pltpu.VMEM((2, tm, tk), BF16)] + [pltpu.VMEM((tm, D), dt) for dt, _, _ in bufs]
               + [_SEMS(2), _SEMS(n_any_in), _SEMS(n_any_out)])
    res, job_res = _pcall(
        body, grid=(nI, nK),
        in_specs=[_ANY] * nP + [pl.BlockSpec((tk, wd), lambda i, k: (k, 0))] * nW + [vec] * n_vec + [_ANY] * n_any_in,
        out_specs=[_ANY] * n_any_out + part_specs,
        out_shape=([jax.ShapeDtypeStruct((T, D), bufs[n][0]) for n in store_ix]
                   + [jax.ShapeDtypeStruct(s, d) for s, d, _, _ in parts]),
        scratch_shapes=scratch, name=name, semantics=("arbitrary", "arbitrary"),
        args=[p for p, _ in pieces] + ws + list(vecs) + [bufs[n][1] for n in load_ix], job=job)
    return res if job is None else (res, job_res)


def _pieces_nn_rms(name, pieces, w, x, gain, sc, dres, tm, tk, job=None):
    _, outs, epi = _rms_mod_bwd_epilogue(x, gain, sc, dres, tm)

    def on_rows(acc, vecs, bufs, parts):
        epi(acc, [bufs[0], vecs[0], vecs[1], bufs[1]], [bufs[1], *parts])

    return _rows_mm(name, pieces, w, x.shape[0], tm, tk, [gain, sc], [(F32, x, False), (F32, dres, True)],
                    outs[1:], on_rows, job=job)


def _rms_mod_fwd(name, x, gain, sc, sh, tr, job=None):
    T = x.shape[0]

    def body(x_ref, g_ref, sc_ref, sh_ref, h_ref):
        xv = x_ref[...]
        rstd = lax.rsqrt(jnp.mean(xv * xv, axis=-1, keepdims=True) + EPS)
        h_ref[...] = ((xv * rstd * g_ref[...]) * (1.0 + sc_ref[...]) + sh_ref[...]).astype(BF16)

    row = pl.BlockSpec((tr, D), lambda i: (i, 0))
    vec = pl.BlockSpec((1, D), lambda i: (0, 0))
    return _pcall(body, grid=(T // tr,), in_specs=[row, vec, vec, vec], out_specs=[row],
                  out_shape=[jax.ShapeDtypeStruct((T, D), BF16)], scratch_shapes=[], name=name, semantics=("parallel",),
                  args=[x, gain, sc, sh], job=job)


def _rms_mod_bwd_epilogue(x, gain, sc, dres, tm, gate=None, mo=None):
    T = x.shape[0]
    with_gate = gate is not None
    row = ((tm, D), lambda i, j, k: (i, 0))
    vec = ((1, D), lambda i, j, k: (0, 0))
    part = ((T // tm * 8, D), F32, (8, D), lambda i, j, k: (i, 0))
    extras = [(x, *row), (gain, *vec), (sc, *vec), (dres, *row)]
    outs = [((T, D), F32, *row), part, part, part]
    if with_gate:
        extras += [(gate, *vec), (mo, *row)]
        outs += [((T, D), BF16, *row), part]

    rows = min(64, tm)

    def epi(acc, ex, ou):
        g = ex[1][...]
        sums = [jnp.zeros((8, D), F32) for _ in range(4)]
        for r0 in range(0, tm, rows):
            rs = slice(r0, r0 + rows)
            dhv, xv = acc[rs, :], ex[0][rs, :]
            rstd = lax.rsqrt(jnp.mean(xv * xv, axis=-1, keepdims=True) + EPS)
            xhat = xv * rstd
            dn = dhv * (1.0 + ex[2][...])
            dxhat = dn * g
            dx = ex[3][rs, :] + rstd * (dxhat - xhat * jnp.mean(dxhat * xhat, axis=-1, keepdims=True))
            ou[0][rs, :] = dx
            terms = [dhv, dhv * (xhat * g), dn * xhat]
            if with_gate:
                terms.append(dx * ex[5][rs, :].astype(F32))
                ou[4][rs, :] = (ex[4][...] * dx).astype(BF16)
            sums = [s + _fold8(t) for s, t in zip(sums, terms)] + sums[len(terms):]
        ou[1][...], ou[2][...], ou[3][...] = sums[:3]
        if with_gate:
            ou[5][...] = sums[3]

    return extras, outs, epi


def _rms_mod_bwd(name, dh, x, gain, sc, dres, tr, gate=None, mo=None):
    T = x.shape[0]
    extras, outs, epi = _rms_mod_bwd_epilogue(x, gain, sc, dres, tr, gate, mo)
    rows_only = lambda im: (lambda i: im(i, 0, 0))
    nE = len(extras)

    def body(dh_ref, *refs):
        epi(dh_ref, refs[:nE], refs[nE:])

    return pl.pallas_call(
        body, grid=(T // tr,),
        in_specs=[pl.BlockSpec((tr, D), lambda i: (i, 0))] + [pl.BlockSpec(bs, rows_only(im)) for _, bs, im in extras],
        out_specs=[pl.BlockSpec(bs, rows_only(im)) for _, _, bs, im in outs],
        out_shape=[jax.ShapeDtypeStruct(s, d) for s, d, _, _ in outs], name=name, compiler_params=_params(("parallel",)),
    )(dh, *[e for e, _, _ in extras])


def _split3(v):
    h = v.astype(BF16)
    r1 = v - h.astype(F32)
    m = r1.astype(BF16)
    lo = (r1 - m.astype(F32)).astype(BF16)
    return h, m, lo


def _tri_mm(tri, v, dims=NN):
    h, m, lo = _split3(v)
    t = tri.astype(BF16)
    mm = lambda p: lax.dot_general(t, p, dims, preferred_element_type=F32)
    return (mm(lo) + mm(m)) + mm(h)


def _hgrn_chunk_terms(q, fl, lb):
    sig = _sigmoid(fl)
    f = lb + (1.0 - lb) * sig
    lf = jnp.log(f)
    kk = 1.0 - f
    sq = _sigmoid(q)
    qf = q * sq
    return sig, f, lf, kk, sq, qf


def _causal(n):
    r = lax.broadcasted_iota(jnp.int32, (n, n), 0)
    c = lax.broadcasted_iota(jnp.int32, (n, n), 1)
    return r >= c


def _hgrn_fwd(proj, lb_logits, o_gain, tt, job=None):
    T = proj.shape[0]
    nT, ncl = T // tt, tt // CHUNK
    C = CHUNK

    def body(q_ref, f_ref, i_ref, g_ref, lbl_ref, og_ref, y_ref, st_ref, S):
        @pl.when(pl.program_id(1) == 0)
        def _():
            S[...] = jnp.zeros_like(S)

        lbl = lbl_ref[...]
        lb = _sigmoid(lbl[0:1, :] - lbl[1:2, :])
        og = og_ref[...]
        shp = (ncl, C, A_HD)
        q, fl, v, g = (r[...].reshape(shp) for r in (q_ref, f_ref, i_ref, g_ref))
        tri = jnp.broadcast_to(_causal(C), (ncl, C, C))
        _, _, lf, kk, _, qf = _hgrn_chunk_terms(q, fl, lb)
        b = _tri_mm(tri, lf, BNN)
        bm, bl = b[:, C // 2 - 1:C // 2, :], b[:, C - 1:C, :]
        qd, kd = qf * jnp.exp(b - bm), kk * jnp.exp(bm - b)
        A = jnp.where(tri, _dot(qd, kd, BNT), 0.0)
        d_st = _dot(v, kk * jnp.exp(bl - b), BTN)
        dec = jnp.exp(bl)
        st = S[...]
        for ci in range(ncl):
            st_ref[0, ci] = st
            st = st * dec[ci] + d_st[ci]
        S[...] = st
        o = _dot(A, v, BNN) + _dot(qf * jnp.exp(b), st_ref[0], BNT)
        r = lax.rsqrt(jnp.mean(o * o, axis=-1, keepdims=True) + EPS)
        y_ref[...] = (o * r * og * (g * _sigmoid(g))).astype(BF16).reshape(tt, A_HD)

    def col(off):
        return pl.BlockSpec((tt, A_HD), lambda h, t: (t, off // A_HD + h))

    head_vec = lambda rows: pl.BlockSpec((rows, A_HD), lambda h, t: (0, h))
    return _pcall(
        body, grid=(A_HEADS, nT),
        in_specs=[col(OFF_QA), col(OFF_FA), col(OFF_IA), col(OFF_GA), head_vec(2), head_vec(1)],
        out_specs=[pl.BlockSpec((tt, A_HD), lambda h, t: (t, h)),
                   pl.BlockSpec((1, ncl, A_HD, A_HD), lambda h, t: (h, t, 0, 0))],
        out_shape=[jax.ShapeDtypeStruct((T, AW), BF16),
                   jax.ShapeDtypeStruct((A_HEADS, T // C, A_HD, A_HD), F32)],
        scratch_shapes=[pltpu.VMEM((A_HD, A_HD), F32)], name="hgrn_fwd", semantics=("parallel", "arbitrary"),
        args=[proj, proj, proj, proj, lb_logits, o_gain], job=job)


def _hgrn_bwd(proj, st, dy, lb_logits, o_gain, tt, job=None):
    T = proj.shape[0]
    nT, ncl = T // tt, tt // CHUNK
    C = CHUNK

    def body(q_ref, f_ref, i_ref, g_ref, st_ref, dy_ref, lbl_ref, og_ref,
             dq_ref, df_ref, di_ref, dg_ref, plb_ref, pog_ref, dS):
        @pl.when(pl.program_id(1) == 0)
        def _():
            dS[...] = jnp.zeros_like(dS)

        lbl = lbl_ref[...]
        lb = _sigmoid(lbl[0:1, :] - lbl[1:2, :])
        og = og_ref[...]
        shp = (ncl, C, A_HD)
        flat = lambda t: t.reshape(tt, A_HD)
        q, fl, v, g, dout = (r[...].reshape(shp) for r in (q_ref, f_ref, i_ref, g_ref, dy_ref))
        tri = jnp.broadcast_to(_causal(C), (ncl, C, C))
        rowi = lax.broadcasted_iota(jnp.int32, shp, 1)
        st0 = st_ref[0]
        sig, f, lf, kk, sq, qf = _hgrn_chunk_terms(q, fl, lb)
        b = _tri_mm(tri, lf, BNN)
        bm, bl = b[:, C // 2 - 1:C // 2, :], b[:, C - 1:C, :]
        e_qd, e_kd, e_ke, e_b = jnp.exp(b - bm), jnp.exp(bm - b), jnp.exp(bl - b), jnp.exp(b)
        qd, kd, ke, qe = qf * e_qd, kk * e_kd, kk * e_ke, qf * e_b
        dec = jnp.exp(bl)
        A = jnp.where(tri, _dot(qd, kd, BNT), 0.0)
        o = _dot(A, v, BNN) + _dot(qe, st0, BNT)
        r = lax.rsqrt(jnp.mean(o * o, axis=-1, keepdims=True) + EPS)
        sg = _sigmoid(g)
        on = o * r * og
        dg_ref[...] = flat((dout * on * (sg * (1.0 + g * (1.0 - sg)))).astype(BF16))
        don = dout * (g * sg)
        pog_ref[...] = _fold8(flat(don * o * r))
        dyh = don * og
        do = r * (dyh - o * (r * r) * jnp.mean(dyh * o, axis=-1, keepdims=True))
        g_st = _dot(do, qe, BTN)
        run = dS[...]
        after = [None] * ncl
        for ci in reversed(range(ncl)):
            after[ci] = run
            run = g_st[ci] + run * dec[ci]
        dS[...] = run
        d_after = jnp.stack(after, axis=0)
        ddec = jnp.sum(d_after * st0, axis=1, keepdims=True)
        dqe = _dot(do, st0, BNN)
        dke = _dot(v, d_after, BNN)
        dA = jnp.where(tri, _dot(do, v, BNT), 0.0)
        dv = _dot(ke, d_after, BNT) + _dot(A, do, BTN)
        dqd = _dot(dA, kd, BNN)
        dkd = _dot(dA, qd, BTN)
        di_ref[...] = flat(dv.astype(BF16))
        dqf = dqe * e_b + dqd * e_qd
        dkk = dkd * e_kd + dke * e_ke
        t_qd, t_kd, t_ke = dqd * qd, dkd * kd, dke * ke
        db = dqe * qe + t_qd - t_kd - t_ke
        dbm = jnp.sum(t_kd - t_qd, axis=1, keepdims=True)
        dbl = jnp.sum(t_ke, axis=1, keepdims=True) + ddec * dec
        db = db + jnp.where(rowi == C // 2 - 1, dbm, 0.0) + jnp.where(rowi == C - 1, dbl, 0.0)
        dlf = _tri_mm(tri, db, BTN)
        dfv = dlf / f - dkk
        df_ref[...] = flat((dfv * (1.0 - lb) * sig * (1.0 - sig)).astype(BF16))
        plb_ref[...] = _fold8(flat(dfv * (1.0 - sig)))
        dq_ref[...] = flat((dqf * (sq * (1.0 + q * (1.0 - sq)))).astype(BF16))

    def col(off):
        return pl.BlockSpec((tt, A_HD), lambda h, t: (nT - 1 - t, off // A_HD + h))

    head_vec = lambda rows: pl.BlockSpec((rows, A_HD), lambda h, t: (0, h))
    o_spec = pl.BlockSpec((tt, A_HD), lambda h, t: (nT - 1 - t, h))
    p_spec = pl.BlockSpec((8, A_HD), lambda h, t: (t, h))
    o_shape = jax.ShapeDtypeStruct((T, AW), BF16)
    p_shape = jax.ShapeDtypeStruct((nT * 8, AW), F32)
    return _pcall(
        body, grid=(A_HEADS, nT),
        in_specs=[col(OFF_QA), col(OFF_FA), col(OFF_IA), col(OFF_GA),
                  pl.BlockSpec((1, ncl, A_HD, A_HD), lambda h, t: (h, nT - 1 - t, 0, 0)),
                  pl.BlockSpec((tt, A_HD), lambda h, t: (nT - 1 - t, h)), head_vec(2), head_vec(1)],
        out_specs=[o_spec, o_spec, o_spec, o_spec, p_spec, p_spec],
        out_shape=[o_shape, o_shape, o_shape, o_shape, p_shape, p_shape],
        scratch_shapes=[pltpu.VMEM((A_HD, A_HD), F32)], name="hgrn_bwd", semantics=("parallel", "arbitrary"),
        args=[proj, proj, proj, proj, st, dy, lb_logits, o_gain], job=job)


LANES = 128
Q_COLS = BW // LANES


def _low_half():
    return lax.broadcasted_iota(jnp.int32, (1, LANES), 1) < B_HD


def _half_sum(t, low):
    lo = jnp.sum(jnp.where(low, t, 0.0), axis=-1, keepdims=True)
    hi = jnp.sum(jnp.where(low, 0.0, t), axis=-1, keepdims=True)
    return jnp.where(low, lo, hi)


def _half_rms(t, low):
    r = lax.rsqrt(_half_sum(t * t, low) * (1.0 / B_HD) + EPS)
    return t * r, r


def _fold_halves(p, low):
    return jnp.where(low, p + pltpu.roll(p, B_HD, 1), 0.0)


def _stack_cols(x):
    return jnp.stack([x[:, c * LANES:(c + 1) * LANES] for c in range(Q_COLS)], axis=0).reshape(KV_HEADS, 2 * BLK, LANES)


def _col_of(t, c):
    return t[c // 2, (c % 2) * BLK:(c % 2 + 1) * BLK]


def _split_halves(col, s, low):
    own = jnp.where(low if s == 0 else jnp.logical_not(low), col, 0.0)
    other = pltpu.roll(own, B_HD, 1)
    return (own, other) if s == 0 else (other, own)


def _swa_keys(kp_ref, kc_ref, vp_ref, vc_ref, kg, low):
    k_lo, k_hi, v_lo, v_hi, hats = [], [], [], [], []
    for j in range(KVW // LANES):
        cs = slice(j * LANES, (j + 1) * LANES)
        k_hat, k_r = _half_rms(jnp.concatenate([kp_ref[:, cs], kc_ref[:, cs]], axis=0), low)
        vcol = jnp.concatenate([vp_ref[:, cs], vc_ref[:, cs]], axis=0)
        hats.append((k_hat, k_r))
        for s in range(2):
            for dst_lo, dst_hi, col in ((k_lo, k_hi, k_hat * kg), (v_lo, v_hi, vcol)):
                lo, hi = _split_halves(col, s, low)
                dst_lo.append(lo)
                dst_hi.append(hi)
    st = lambda parts: jnp.stack(parts, axis=0)
    return st(k_lo), st(k_hi), st(v_lo), st(v_hi), hats


def _swa_mask(first_block):
    qi = lax.broadcasted_iota(jnp.int32, (BLK, 2 * BLK), 0) + BLK
    ki = lax.broadcasted_iota(jnp.int32, (BLK, 2 * BLK), 1)
    rel = qi - ki
    m = (rel >= 0) & (rel < BLK) & (jnp.logical_not(first_block) | (ki >= BLK))
    return jnp.concatenate([m, m], axis=0)


def _sink_cols(sk_ref, hi):
    top = lax.broadcasted_iota(jnp.int32, (2 * BLK, 1), 0) < BLK
    return jnp.stack([jnp.where(top, sk_ref[0, GROUP * hk + hi], sk_ref[0, GROUP * hk + 2 + hi])
                      for hk in range(KV_HEADS)], axis=0)


def _swa_probs(qn, k_half, sink, mask):
    s = jnp.where(mask, _dot(qn, k_half, BNT) * (B_HD ** -0.5), NEG)
    m = jnp.maximum(jnp.max(s, axis=-1, keepdims=True), sink)
    p = jnp.exp(s - m)
    ps = jnp.exp(sink - m)
    inv = 1.0 / (jnp.sum(p, axis=-1, keepdims=True) + ps)
    return p * inv, ps * inv


def _swa_fwd(proj, q_gain, k_gain, sinks, job=None):
    T = proj.shape[0]
    nb = T // BLK

    def body(q_ref, kc_ref, kp_ref, vc_ref, vp_ref, qg_ref, kg_ref, sk_ref, o_ref):
        low = _low_half()
        mask = _swa_mask(pl.program_id(0) == 0)
        qn = _half_rms(_stack_cols(q_ref[...]), low)[0] * qg_ref[...]
        k_lo, k_hi, v_lo, v_hi, _ = _swa_keys(kp_ref, kc_ref, vp_ref, vc_ref, kg_ref[...], low)
        p_lo, _ = _swa_probs(qn, k_lo, _sink_cols(sk_ref, 0), mask)
        p_hi, _ = _swa_probs(qn, k_hi, _sink_cols(sk_ref, 1), mask)
        o = (_dot(p_lo, v_lo, BNN) + _dot(p_hi, v_hi, BNN)).astype(BF16)
        for c in range(Q_COLS):
            o_ref[:, c * LANES:(c + 1) * LANES] = _col_of(o, c)

    q_gain, k_gain = jnp.tile(q_gain, (1, 2)), jnp.tile(k_gain, (1, 2))
    cur = lambda w, off: pl.BlockSpec((BLK, w), lambda i: (i, off // w))
    prev = lambda w, off: pl.BlockSpec((BLK, w), lambda i: (jnp.maximum(i - 1, 0), off // w))
    small = lambda n: pl.BlockSpec((1, 2 * n), lambda i: (0, 0))
    return _pcall(
        body, grid=(nb,),
        in_specs=[cur(BW, OFF_QB), cur(KVW, OFF_KB), prev(KVW, OFF_KB), cur(KVW, OFF_VB), prev(KVW, OFF_VB),
                  small(B_HD), small(B_HD), pl.BlockSpec(memory_space=pltpu.SMEM)],
        out_specs=[pl.BlockSpec((BLK, BW), lambda i: (i, 0))],
        out_shape=[jax.ShapeDtypeStruct((T, BW), BF16)], scratch_shapes=[], name="swa_fwd", semantics=("parallel",),
        args=[proj, proj, proj, proj, proj, q_gain, k_gain, sinks], job=job)


def _swa_bwd(proj, dout, q_gain, k_gain, sinks, job=None):
    T = proj.shape[0]
    nb = T // BLK
    W = BW + 2 * KVW

    def body(q_ref, kc_ref, kp_ref, vc_ref, vp_ref, do_ref, qg_ref, kg_ref, sk_ref,
             dq_ref, dkv_ref, pqg_ref, pkg_ref, psk_ref, dkn_c, dv_c):
        i = pl.program_id(0)
        live = i < nb
        low = _low_half()
        high = jnp.logical_not(low)
        qg, kg = qg_ref[...], kg_ref[...]
        mask = _swa_mask(i == 0)
        lane = lax.broadcasted_iota(jnp.int32, (1, LANES), 1)
        scale = B_HD ** -0.5

        @pl.when(i == 0)
        def _():
            dkn_c[...] = jnp.zeros_like(dkn_c)
            dv_c[...] = jnp.zeros_like(dv_c)

        q_hat, q_r = _half_rms(_stack_cols(q_ref[...]), low)
        qn = q_hat * qg
        k_lo, k_hi, v_lo, v_hi, hats = _swa_keys(kp_ref, kc_ref, vp_ref, vc_ref, kg, low)
        do = _stack_cols(do_ref[...])
        dqn = jnp.zeros((KV_HEADS, 2 * BLK, LANES), F32)
        acc_sk = jnp.zeros((1, LANES), F32)
        dk_parts, dv_parts = [], []
        for hi, (k_h, v_h) in enumerate(((k_lo, v_lo), (k_hi, v_hi))):
            p, ps = _swa_probs(qn, k_h, _sink_cols(sk_ref, hi), mask)
            dp = _dot(do, v_h, BNT)
            delta = jnp.sum(p * dp, axis=-1, keepdims=True)
            ds = p * (dp - delta) * scale
            dqn = dqn + _dot(ds, k_h, BNN)
            dk_parts.append(_dot(ds, qn, BTN))
            dv_parts.append(_dot(p, do, BTN))
            t = ps * delta
            for hk in range(KV_HEADS):
                for rows in range(2):
                    h = GROUP * hk + 2 * rows + hi
                    acc_sk = acc_sk + jnp.where(
                        lane == h, -jnp.sum(t[hk, rows * BLK:(rows + 1) * BLK], axis=0, keepdims=True), 0.0)
        dqh = dqn * qg
        dq = (q_r * (dqh - q_hat * (_half_sum(dqh * q_hat, low) * (1.0 / B_HD)))).astype(BF16)
        for c in range(Q_COLS):
            dq_ref[:, c * LANES:(c + 1) * LANES] = _col_of(dq, c)
        acc_qg = _fold_halves(_fold8((dqn * q_hat).reshape(KV_HEADS * 2 * BLK, LANES)), low)

        def native(parts, j):
            lo_arr, hi_arr = parts
            a, b = 2 * j, 2 * j + 1
            return (jnp.where(low, lo_arr[a], 0.0) + pltpu.roll(jnp.where(high, hi_arr[a], 0.0), B_HD, 1)
                    + jnp.where(high, hi_arr[b], 0.0) + pltpu.roll(jnp.where(low, lo_arr[b], 0.0), B_HD, 1))

        acc_kg = jnp.zeros((8, LANES), F32)
        for j in range(KVW // LANES):
            cs = slice(j * LANES, (j + 1) * LANES)
            dkn = jnp.where(live, native(dk_parts, j), 0.0)
            dvc = jnp.where(live, native(dv_parts, j), 0.0)
            kp_hat, kp_r = hats[j][0][:BLK], hats[j][1][:BLK]
            dkn_prev = dkn_c[:, cs] + dkn[:BLK]
            dv_prev = dv_c[:, cs] + dvc[:BLK]
            acc_kg = acc_kg + _fold8(dkn_prev * kp_hat)
            dkh = dkn_prev * kg
            dkv_ref[:, cs] = (kp_r * (dkh - kp_hat * (_half_sum(dkh * kp_hat, low) * (1.0 / B_HD)))).astype(BF16)
            dkv_ref[:, KVW + j * LANES:KVW + (j + 1) * LANES] = dv_prev.astype(BF16)
            dkn_c[:, cs] = dkn[BLK:]
            dv_c[:, cs] = dvc[BLK:]
        keep = jnp.where(i > 0, 1.0, 0.0)
        pqg_ref[...] = jnp.where(live, acc_qg, 0.0)
        pkg_ref[...] = _fold_halves(acc_kg, low) * keep
        psk_ref[...] = jnp.broadcast_to(jnp.where(live, acc_sk, 0.0), (8, LANES)) * (
            lax.broadcasted_iota(jnp.int32, (8, LANES), 0) == 0).astype(F32)

    q_gain, k_gain = jnp.tile(q_gain, (1, 2)), jnp.tile(k_gain, (1, 2))
    last = nb - 1
    cur = lambda w, off: pl.BlockSpec((BLK, w), lambda i: (jnp.minimum(i, last), off // w))
    prev = lambda w, off: pl.BlockSpec((BLK, w), lambda i: (jnp.maximum(i - 1, 0), off // w))
    small = lambda n: pl.BlockSpec((1, 2 * n), lambda i: (0, 0))
    part = pl.BlockSpec((8, 128), lambda i: (i, 0))
    p_shape = jax.ShapeDtypeStruct(((nb + 1) * 8, 128), F32)
    return _pcall(
        body, grid=(nb + 1,),
        in_specs=[cur(BW, OFF_QB), cur(KVW, OFF_KB), prev(KVW, OFF_KB), cur(KVW, OFF_VB), prev(KVW, OFF_VB),
                  pl.BlockSpec((BLK, BW), lambda i: (jnp.minimum(i, last), 0)), small(B_HD), small(B_HD),
                  pl.BlockSpec(memory_space=pltpu.SMEM)],
        out_specs=[pl.BlockSpec((BLK, BW), lambda i: (i, 0)),
                   pl.BlockSpec((BLK, 2 * KVW), lambda i: (jnp.maximum(i - 1, 0), 0)), part, part, part],
        out_shape=[jax.ShapeDtypeStruct((T + BLK, BW), BF16), jax.ShapeDtypeStruct((T, 2 * KVW), BF16),
                   p_shape, p_shape, p_shape],
        scratch_shapes=[pltpu.VMEM((BLK, KVW), F32), pltpu.VMEM((BLK, KVW), F32)], name="swa_bwd",
        semantics=("arbitrary",), args=[proj, proj, proj, proj, proj, dout, q_gain, k_gain, sinks], job=job)


def _branch_merge(ya_pre, attn, wa_t, wb_t, proj, tm, tn, job=None):
    T = ya_pre.shape[0]

    def body(a_ref, b_ref, wa_ref, wb_ref, ga_ref, gb_ref, ya_ref, yb_ref, mg_ref):
        ya = lax.dot_general(a_ref[...], wa_ref[...], NT, preferred_element_type=F32)
        yb = lax.dot_general(b_ref[...], wb_ref[...], NT, preferred_element_type=F32)
        ya_ref[...] = ya.astype(BF16)
        yb_ref[...] = yb.astype(BF16)
        mg_ref[...] = (_sigmoid(ga_ref[...]) * ya + _sigmoid(gb_ref[...]) * yb).astype(BF16)

    o_spec = pl.BlockSpec((tm, tn), lambda i, j: (i, j))
    o_shape = jax.ShapeDtypeStruct((T, D), BF16)
    return _pcall(
        body, grid=(T // tm, D // tn),
        in_specs=[pl.BlockSpec((tm, AW), lambda i, j: (i, 0)), pl.BlockSpec((tm, BW), lambda i, j: (i, 0)),
                  pl.BlockSpec((tn, AW), lambda i, j: (j, 0)), pl.BlockSpec((tn, BW), lambda i, j: (j, 0)),
                  pl.BlockSpec((tm, tn), lambda i, j: (i, OFF_GTA // tn + j)),
                  pl.BlockSpec((tm, tn), lambda i, j: (i, OFF_GTB // tn + j))],
        out_specs=[o_spec, o_spec, o_spec], out_shape=[o_shape, o_shape, o_shape], scratch_shapes=[], name="branch_merge",
        semantics=("parallel", "parallel"), args=[ya_pre, attn, wa_t, wb_t, proj, proj], job=job)


def _ij(i, j, k):
    return (i, j)


def _local_step(x, tgt, mod, g1, g2, lbl, og, qg, kg, sk, shards, c_arr, update):
    win_s, wa_s, wb_s, wout_s, wmi_s, wmo_s = shards
    T = x.shape[0]
    tm, tr, tt = min(1024, T), min(256, T), min(2048, T)
    tk_t = min(1024, T)
    tn = 512
    sh1, sc1, gt1, sh2, sc2, gt2 = (mod[:, i * D:(i + 1) * D] for i in range(N_MOD))
    nI = T // tm
    blk = (tm, tn)

    win_lo_s, win_hi_s = win_s[:, :D // 2], win_s[:, D // 2:]
    (h,), (win_lo,) = _rms_mod_fwd("rms1_fwd", x, g1, sc1, sh1, tr, job=_gather_relay_job([win_lo_s], alone=True))

    def epi_store(acc, ex, ou):
        ou[0][...] = acc.astype(ou[0].dtype)

    tm2 = min(2048, T)
    blk2 = (tm2, tn)

    last = wmi_s.shape[0]
    gather = _gather_relay_job
    def epi_add(acc, ex, ou):
        ou[0][...] = acc + ex[0][...]

    (proj_lo,), (win_hi,) = _mm(
        "in_proj_lo", "nt", [(h, D // 2)], win_lo, T, IN_W, D // 2, tm2, tn, D // 2, [], [((T, IN_W), F32, blk2, _ij)],
        epi_store, job=gather([win_hi_s]))
    (proj,), (wa_t, wb_t, w_out) = _mm(
        "in_proj_hi", "nt", [(h, D // 2)], win_hi, T, IN_W, D // 2, tm2, tn, D // 2, [(proj_lo, blk2, _ij)],
        [((T, IN_W), F32, blk2, _ij)], epi_add, a_first=1, job=gather([wa_s, wb_s, wout_s]))
    win_t = [win_lo, win_hi]
    (ya_pre, st), (wmi_part,) = _hgrn_fwd(proj, lbl, og, tt, job=gather([wmi_s], rows=[(0, MI_CUT)]))
    (attn,), (wmi_t,) = _swa_fwd(proj, qg, kg, sk, job=gather([wmi_s], rows=[(MI_CUT, last)], into=[wmi_part]))
    (ya, yb, merged), _ = _branch_merge(ya_pre, attn, wa_t, wb_t, proj, tm, tn)

    def residual_rows(acc, vecs, bufs, parts):
        gt, gain, sc, sh = (v[...] for v in vecs)
        x_buf, mo_buf, h2_buf = bufs
        rows = min(64, tm)
        for r0 in range(0, tm, rows):
            rs = slice(r0, r0 + rows)
            z = acc[rs, :]
            mo_buf[rs, :] = z.astype(BF16)
            x1v = x_buf[rs, :] + gt * z
            x_buf[rs, :] = x1v
            rstd = lax.rsqrt(jnp.mean(x1v * x1v, axis=-1, keepdims=True) + EPS)
            h2_buf[rs, :] = ((x1v * rstd * gain) * (1.0 + sc) + sh).astype(BF16)

    x1, mo, h2 = _rows_mm("out_proj", [(merged, D)], w_out, T, tm, min(1024, D), [gt1, g2, sc2, sh2],
                          [(F32, x, True), (BF16, None, True), (BF16, None, True)], [], residual_rows)

    def epi_relu2(acc, ex, ou):
        r = jnp.maximum(acc, 0.0)
        ou[0][...] = r.astype(BF16)
        ou[1][...] = (r * r).astype(BF16)

    (r, a), (w_mo,) = _mm("mlp_in", "nt", [(h2, D)], wmi_t, T, HID, D, tm2, tn, D, [],
                          [((T, HID), BF16, blk2, _ij), ((T, HID), BF16, blk2, _ij)], epi_relu2,
                          job=gather([wmo_s]))

    def loss_rows(acc, vecs, bufs, parts):
        gt = vecs[0][...]
        x1_buf, t_buf, dz_buf = bufs
        rows = min(64, tm)
        loss_sum, gate_sum = jnp.zeros((8, D), F32), jnp.zeros((8, D), F32)
        for r0 in range(0, tm, rows):
            rs = slice(r0, r0 + rows)
            z = acc[rs, :]
            e = x1_buf[rs, :] + gt * z - t_buf[rs, :]
            dy = e * (1.0 / D)
            t_buf[rs, :] = dy
            dz_buf[rs, :] = (gt * dy).astype(BF16)
            loss_sum = loss_sum + _fold8(e * e)
            gate_sum = gate_sum + _fold8(dy * z)
        parts[0][...] = loss_sum * (0.5 / D)
        parts[1][...] = gate_sum

    part_rows = ((nI * 8, D), F32, (8, D), lambda i, j, k: (i, 0))
    dy, dz, p_loss, p_gt2 = _rows_mm(
        "mlp_out", [(a, HID)], w_mo, T, tm, 1024, [gt2], [(F32, x1, False), (F32, tgt, True), (BF16, None, True)],
        [part_rows, part_rows], loss_rows)

    def epi_du(acc, ex, ou):
        ou[0][...] = (acc * (2.0 * ex[0][...].astype(F32))).astype(BF16)

    (du,) = _mm("mlp_out_dx", "nt", [(dz, D)], w_mo, T, HID, D, tm2, tn, D, [(r, blk2, _ij)],
                [((T, HID), BF16, blk2, _ij)], epi_du)
    gblk = (1024, 1024)
    gwide = (1024, D)
    pair_sum = lambda nm, g, r1: _pair_sum("pair_sum_" + nm, g, r1, c_arr, _sum_rows(r1.shape[1]))
    (g_mo,) = _mm("mlp_out_dw", "tn", [(a, HID)], dz, HID, D, T, 1024, D, tk_t, [], [((HID, D), BF16, gwide, _ij)], epi_store)
    (dh2,), (r1_mo,) = _mm("mlp_in_dx", "nn", [(du, HID)], wmi_t, T, D, HID, tm, D, 1024, [],
                           [((T, D), F32, (tm, D), _ij)], epi_store, job=_pair_job([g_mo]))
    dx1, p_sh2, p_sc2, p_g2, dmo, p_gt1 = _rms_mod_bwd("rms2_bwd", dh2, x1, g2, sc2, dy, tr, gate=gt1, mo=mo)
    s_mo = pair_sum("mlp_out", g_mo, r1_mo)
    near, far = (1, 2), (3,)
    cut = s_mo.shape[1] // 2
    (g_mi,), (rn_mo, rf_mo) = _mm(
        "mlp_in_dw", "tn", [(du, HID)], h2, HID, D, T, 1024, D, tk_t, [], [((HID, D), BF16, gwide, _ij)], epi_store,
        job=_both(_chip_job([s_mo], near), _chip_job([s_mo], far, rows=(0, cut))))

    def epi_gates(acc, ex, ou):
        ya_ref, yb_ref, ga_ref, gb_ref = ex
        sa, sb = _sigmoid(ga_ref[...]), _sigmoid(gb_ref[...])
        ou[0][...] = (acc * sa).astype(BF16)
        ou[1][...] = (acc * sb).astype(BF16)
        ou[2][...] = (acc * ya_ref[...].astype(F32) * (sa * (1.0 - sa))).astype(BF16)
        ou[3][...] = (acc * yb_ref[...].astype(F32) * (sb * (1.0 - sb))).astype(BF16)

    o_bf = ((T, D), BF16, blk, _ij)
    (dya, dyb, dga, dgb), (rf_mo, r1_mi) = _mm(
        "out_proj_dx", "nt", [(dmo, D)], w_out, T, D, D, tm, tn, D,
        [(ya, blk, _ij), (yb, blk, _ij), (proj, blk, lambda i, j, k: (i, OFF_GTA // tn + j)),
         (proj, blk, lambda i, j, k: (i, OFF_GTB // tn + j))], [o_bf, o_bf, o_bf, o_bf], epi_gates,
        job=_both(_chip_job([s_mo], far, rows=(cut, 2 * cut), into=[rf_mo]), _pair_job([g_mi])))
    s_mi = pair_sum("mlp_in", g_mi, r1_mi)
    (g_out,) = _mm("out_proj_dw", "tn", [(merged, D)], dmo, D, D, T, 1024, 1024, tk_t, [], [((D, D), BF16, gblk, _ij)], epi_store)
    dya_pre, dattn = _twin_mm("branch_dx", "nn", [(dya, wa_t), (dyb, wb_t)], T, AW, D, tm, tn, D, F32)
    g_a, g_b = _twin_mm("branch_dw", "tn", [(dya, ya_pre), (dyb, attn)], D, AW, T, 1024, 1024, tk_t, BF16)
    (dqa, dfa, dia, dgg, p_lb, p_og), (rn_mi, r1_out, r1_a, r1_b) = _hgrn_bwd(
        proj, st, dya_pre, lbl, og, tt, job=_both(_chip_job([s_mi], near), _pair_job([g_out, g_a, g_b])))
    (dqb, dkv, p_qg, p_kg, p_sk), (rf_mi,) = _swa_bwd(proj, dattn, qg, kg, sk, job=_chip_job([s_mi], far))
    s_out, s_a, s_b = pair_sum("out", g_out, r1_out), pair_sum("branch_a", g_a, r1_a), pair_sum("branch_b", g_b, r1_b)
    pieces = [(dqa, AW), (dfa, AW), (dia, AW), (dgg, AW), (dqb, BW), (dkv, 2 * KVW), (dga, D), (dgb, D)]
    (g_in,), (r2_out, r2_a, r2_b) = _pieces_tn("in_proj_dw", pieces, h, 512, job=_chip_job([s_out, s_a, s_b]))
    (r1_in,) = update("w_mlp_in", s_mi, [rn_mi, rf_mi], job=_pair_job([g_in]))
    s_in = pair_sum("in", g_in, r1_in)
    (dx, p_sh1, p_sc1, p_g1), (r2_in,) = _pieces_nn_rms(
        "in_proj_dx", pieces, win_t, x, g1, sc1, dx1, tm, 512, job=_chip_job([s_in]))

    partials = dict(sh1=p_sh1, sc1=p_sc1, gt1=p_gt1, sh2=p_sh2, sc2=p_sc2, gt2=p_gt2, g1=p_g1, g2=p_g2,
                    lb=p_lb, og=p_og, qg=p_qg, kg=p_kg, sk=p_sk, loss=p_loss)
    sums = dict(w_in=(s_in, [r2_in]), w_branch_a=(s_a, [r2_a]), w_branch_b=(s_b, [r2_b]), w_out=(s_out, [r2_out]),
                w_mlp_in=(s_mi, [rn_mi, rf_mi]), w_mlp_out=(s_mo, [rn_mo, rf_mo]))
    return dx, sums, partials


def _exchange_slots(buf, send_sems, recv_sems):
    me = _mesh_pos()
    mine = buf.at[_index(me)]
    sends = []
    for k in range(1, N_DEV):
        cp = pltpu.make_async_remote_copy(src_ref=mine, dst_ref=mine, send_sem=send_sems.at[k - 1],
                                          recv_sem=recv_sems.at[k - 1], device_id=_flip(me, k), device_id_type=MESH)
        cp.start()
        sends.append(cp)
    for k in range(1, N_DEV):
        theirs = buf.at[_index(_flip(me, k))]
        pltpu.make_async_remote_copy(src_ref=theirs, dst_ref=theirs, send_sem=send_sems.at[k - 1],
                                     recv_sem=recv_sems.at[k - 1], device_id=_flip(me, k), device_id_type=MESH).wait_recv()
    for cp in sends:
        cp.wait_send()


ADA_W = N_MOD * D // N_DEV


def _ada_mod(c, w_ada, b_shard):
    def body(c_ref, w_ref, b_ref, mod_ref, sc_ref, cbuf, mbuf, s1, r1, s2, r2):
        me = _index(_mesh_pos())
        cbuf[me] = c_ref[...]
        _exchange_slots(cbuf, s1, r1)
        row = lax.broadcasted_iota(jnp.int32, (N_DEV, D), 0)
        call = jnp.zeros((N_DEV, D), F32)
        for d in range(N_DEV):
            call = jnp.where(row == d, cbuf[d], call)
        sc = call * _sigmoid(call)
        sc_ref[...] = sc
        mbuf[me] = _dot(sc, w_ref[...]) + b_ref[...]
        _exchange_slots(mbuf, s2, r2)
        for s in range(N_DEV):
            mod_ref[:, s * ADA_W:(s + 1) * ADA_W] = mbuf[s, pl.ds(me, 1), :]

    return pl.pallas_call(
        body, in_specs=[_VMEM, _VMEM, _VMEM], out_specs=[_VMEM, _VMEM],
        out_shape=[jax.ShapeDtypeStruct((1, N_MOD * D), F32), jax.ShapeDtypeStruct((N_DEV, D), F32)],
        scratch_shapes=[pltpu.VMEM((N_DEV, 1, D), F32), pltpu.VMEM((N_DEV, N_DEV, ADA_W), F32),
                        _SEMS(N_DEV - 1), _SEMS(N_DEV - 1), _SEMS(N_DEV - 1), _SEMS(N_DEV - 1)],
        name="ada_mod", compiler_params=pltpu.CompilerParams(vmem_limit_bytes=VMEM_LIMIT),
    )(c, w_ada, b_shard)


SMALL_SEGS = (("b_ada", N_MOD * D), ("norm1_gain", D), ("norm2_gain", D), ("lb0", AW), ("lb1", AW),
              ("hgrn_o_gain", AW), ("q_norm_gain", 128), ("k_norm_gain", 128), ("sinks", 128))
SMALL_W = sum(w for _, w in SMALL_SEGS)
X_SEGS = (("sh1", D), ("sc1", D), ("gt1", D), ("sh2", D), ("sc2", D), ("gt2", D), ("g1", D), ("g2", D),
          ("lb", AW), ("og", AW), ("qg", 128), ("kg", 128), ("sk", 128), ("loss", 128))
X_W = sum(w for _, w in X_SEGS)


def _offsets(segs):
    out, o = {}, 0
    for name, w in segs:
        out[name] = (o, w)
        o += w
    return out


def _small_reduce(parts, lb_logits):
    xo, so = _offsets(X_SEGS), _offsets(SMALL_SEGS)
    names = [nm for nm, _ in X_SEGS]

    def body(*refs):
        p_refs = dict(zip(names, refs[:len(names)]))
        lbl_ref, allx, gs_ref, loss_ref, send_sems, recv_sems = refs[len(names):]
        me = _index(_mesh_pos())
        for nm, (o, w) in xo.items():
            if nm == "loss":
                allx[me, :, o:o + w] = jnp.broadcast_to(jnp.sum(p_refs[nm][...]), (1, w))
            else:
                allx[me, :, o:o + w] = jnp.sum(p_refs[nm][...], axis=0, keepdims=True)
        _exchange_slots(allx, send_sems, recv_sems)
        tot = allx[0]
        for d in range(1, N_DEV):
            tot = tot + allx[d]
        seg = lambda nm: tot[:, xo[nm][0]:xo[nm][0] + xo[nm][1]]

        def put(nm, v):
            gs_ref[:, so[nm][0]:so[nm][0] + so[nm][1]] = v

        put("b_ada", tot[:, 0:N_MOD * D])
        put("norm1_gain", seg("g1"))
        put("norm2_gain", seg("g2"))
        lbl = lbl_ref[...]
        lb = _sigmoid(lbl[0:1, :] - lbl[1:2, :])
        dl0 = seg("lb") * lb * (1.0 - lb)
        put("lb0", dl0)
        put("lb1", -dl0)
        put("hgrn_o_gain", seg("og"))
        put("q_norm_gain", seg("qg"))
        put("k_norm_gain", seg("kg"))
        put("sinks", seg("sk"))
        loss_ref[...] = seg("loss")

    return pl.pallas_call(
        body, in_specs=[_VMEM] * (len(names) + 1), out_specs=[_VMEM, _VMEM, _VMEM],
        out_shape=[jax.ShapeDtypeStruct((N_DEV, 1, X_W), F32), jax.ShapeDtypeStruct((1, SMALL_W), F32),
                   jax.ShapeDtypeStruct((1, 128), F32)],
        scratch_shapes=[_SEMS(N_DEV - 1), _SEMS(N_DEV - 1)], name="small_reduce",
        compiler_params=pltpu.CompilerParams(vmem_limit_bytes=VMEM_LIMIT),
    )(*[parts[nm] for nm in names], lb_logits)


def _adamw_math(w, g, m, v):
    m = B1 * m + (1.0 - B1) * g
    v = B2 * v + (1.0 - B2) * (g * g)
    m_hat = m / (1.0 - B1 ** STEP)
    v_hat = v / (1.0 - B2 ** STEP)
    return -LR * (m_hat / (jnp.sqrt(v_hat) + ADAM_EPS) + WD * w), m, v


def _sum_rows(rs):
    return 256 if rs % 256 == 0 else rs // 2


def _pair_sum(name, g, recv, c_arr, tr):
    _, rs, cols = recv.shape
    blk = (1, tr, cols)

    def body(c_ref, g_ref, r_ref, o_ref):
        o_ref[...] = (g_ref[...].astype(F32) + r_ref[...].astype(F32)).astype(BF16)

    grid_spec = pltpu.PrefetchScalarGridSpec(
        num_scalar_prefetch=1, grid=(4, rs // tr),
        in_specs=[pl.BlockSpec(blk, lambda q, i, c: (2 * q + c[0], i, 0)), pl.BlockSpec(blk, lambda q, i, c: (q, i, 0))],
        out_specs=pl.BlockSpec(blk, lambda q, i, c: (q, i, 0)))
    return pl.pallas_call(body, grid_spec=grid_spec, out_shape=jax.ShapeDtypeStruct((4, rs, cols), BF16), name=name,
                          compiler_params=_params(("parallel", "parallel")))(c_arr, g.reshape(N_DEV, rs, cols), recv)


def _sum_adamw(name, sums, recvs, q_arr, w, m, v, transposed, tile, job=None):
    rows, cols = w.shape
    nR = len(recvs)

    def body(q_ref, s_ref, *refs):
        r_refs = refs[:nR]
        w_ref, m_ref, v_ref, g_ref, d_ref, nm_ref, nv_ref = refs[nR:]
        g = s_ref[0].astype(F32)
        for r_ref in r_refs:
            for slot in range(r_ref.shape[0]):
                g = g + r_ref[slot].astype(F32)
        g = g.T if transposed else g
        g_ref[...] = g
        d_ref[...], nm_ref[...], nv_ref[...] = _adamw_math(w_ref[...], g, m_ref[...], v_ref[...])

    if transposed:
        slab = lambda n, first: pl.BlockSpec((n, cols, tile), lambda i, q: (first(q), 0, i))
    else:
        slab = lambda n, first: pl.BlockSpec((n, tile, cols), lambda i, q: (first(q), i, 0))
    spec = pl.BlockSpec((tile, cols), lambda i, q: (i, 0))
    shape = jax.ShapeDtypeStruct((rows, cols), F32)
    res, job_res = _pcall(
        body, grid=(rows // tile,),
        in_specs=[slab(1, lambda q: q[0])] + [slab(r.shape[0], lambda q: 0) for r in recvs] + [spec] * 3,
        out_specs=[spec] * 4, out_shape=[shape] * 4, scratch_shapes=[], name=name, semantics=("parallel",),
        args=[sums, *recvs, w, m, v], job=job, prefetch=[q_arr])
    return res if job is None else (res, job_res)


def _adamw(name, w, g, m, v, tr):
    rows, cols = w.shape

    def body(w_ref, g_ref, m_ref, v_ref, d_ref, nm_ref, nv_ref):
        d_ref[...], nm_ref[...], nv_ref[...] = _adamw_math(w_ref[...], g_ref[...], m_ref[...], v_ref[...])

    spec = pl.BlockSpec((tr, cols), lambda i: (i, 0))
    shape = jax.ShapeDtypeStruct((rows, cols), F32)
    return pl.pallas_call(
        body, grid=(rows // tr,), in_specs=[spec] * 4, out_specs=[spec] * 3, out_shape=[shape] * 3, name=name,
        compiler_params=_params(("parallel",)),
    )(w, g, m, v)


def _ada_update(sc_t, dmod_cols, w, m, v, tr):
    rows, cols = w.shape

    def body(s_ref, d_ref, w_ref, m_ref, v_ref, g_ref, dl_ref, nm_ref, nv_ref):
        g = jnp.dot(s_ref[...], d_ref[...], precision=lax.Precision.HIGHEST, preferred_element_type=F32)
        g_ref[...] = g
        dl_ref[...], nm_ref[...], nv_ref[...] = _adamw_math(w_ref[...], g, m_ref[...], v_ref[...])

    spec = pl.BlockSpec((tr, cols), lambda i: (i, 0))
    shape = jax.ShapeDtypeStruct((rows, cols), F32)
    return pl.pallas_call(
        body, grid=(rows // tr,),
        in_specs=[pl.BlockSpec((tr, N_DEV), lambda i: (i, 0)), pl.BlockSpec((N_DEV, cols), lambda i: (0, 0)), spec, spec, spec],
        out_specs=[spec] * 4, out_shape=[shape] * 4, name="ada_update", compiler_params=_params(("parallel",)),
    )(sc_t, dmod_cols, w, m, v)


BIG = ("w_in", "w_branch_a", "w_branch_b", "w_out", "w_mlp_in", "w_mlp_out")
COLUMN_SHARDED = ("w_in", "w_branch_a", "w_branch_b", "w_mlp_in")
AS_TRANSPOSE = ("w_in",)
WEIGHTS = ("w_ada", "b_ada", "norm1_gain", "w_in", "lb_logits", "hgrn_o_gain", "q_norm_gain", "k_norm_gain", "sinks",
           "w_branch_a", "w_branch_b", "w_out", "norm2_gain", "w_mlp_in", "w_mlp_out")


def _to_bf16(name, w, transposed, tile=256):
    rows, cols = w.shape

    def body(w_ref, o_ref):
        v = w_ref[...]
        o_ref[...] = (v.T if transposed else v).astype(BF16)

    out_spec = pl.BlockSpec((cols, tile), lambda i: (0, i)) if transposed else pl.BlockSpec((tile, cols), lambda i: (i, 0))
    return pl.pallas_call(
        body, grid=(rows // tile,), in_specs=[pl.BlockSpec((tile, cols), lambda i: (i, 0))], out_specs=out_spec,
        out_shape=jax.ShapeDtypeStruct((cols, rows) if transposed else (rows, cols), BF16), name=name,
        compiler_params=_params(("parallel",)))(w)


def _pack_small(p):
    lb = p["lb_logits"]
    src = dict(p, lb0=lb[0:1], lb1=lb[1:2])
    return jnp.concatenate([jnp.pad(src[nm], ((0, 0), (0, w - src[nm].shape[1]))) for nm, w in SMALL_SEGS], axis=1)


def _unpack_small(vec, shapes):
    so = _offsets(SMALL_SEGS)
    out = {}
    for nm, shp in shapes.items():
        if nm == "lb_logits":
            o = so["lb0"][0]
            out[nm] = vec[0, o:o + 2 * AW].reshape(2, AW)
        else:
            o = so[nm][0]
            out[nm] = vec[:, o:o + shp[1]]
    return out


def kernel(x, c, w_ada, b_ada, norm1_gain, w_in, lb_logits, hgrn_o_gain, q_norm_gain, k_norm_gain, sinks, w_branch_a, w_branch_b, w_out, norm2_gain, w_mlp_in, w_mlp_out, loss_target, m_w_ada, m_b_ada, m_norm1_gain, m_w_in, m_lb_logits, m_hgrn_o_gain, m_q_norm_gain, m_k_norm_gain, m_sinks, m_w_branch_a, m_w_branch_b, m_w_out, m_norm2_gain, m_w_mlp_in, m_w_mlp_out, v_w_ada, v_b_ada, v_norm1_gain, v_w_in, v_lb_logits, v_hgrn_o_gain, v_q_norm_gain, v_k_norm_gain, v_sinks, v_w_branch_a, v_w_branch_b, v_w_out, v_norm2_gain, v_w_mlp_in, v_w_mlp_out):
    w = dict(w_ada=w_ada, b_ada=b_ada, norm1_gain=norm1_gain, w_in=w_in, lb_logits=lb_logits, hgrn_o_gain=hgrn_o_gain,
             q_norm_gain=q_norm_gain, k_norm_gain=k_norm_gain, sinks=sinks, w_branch_a=w_branch_a, w_branch_b=w_branch_b,
             w_out=w_out, norm2_gain=norm2_gain, w_mlp_in=w_mlp_in, w_mlp_out=w_mlp_out)
    m = dict(w_ada=m_w_ada, b_ada=m_b_ada, norm1_gain=m_norm1_gain, w_in=m_w_in, lb_logits=m_lb_logits,
             hgrn_o_gain=m_hgrn_o_gain, q_norm_gain=m_q_norm_gain, k_norm_gain=m_k_norm_gain, sinks=m_sinks,
             w_branch_a=m_w_branch_a, w_branch_b=m_w_branch_b, w_out=m_w_out, norm2_gain=m_norm2_gain,
             w_mlp_in=m_w_mlp_in, w_mlp_out=m_w_mlp_out)
    v = dict(w_ada=v_w_ada, b_ada=v_b_ada, norm1_gain=v_norm1_gain, w_in=v_w_in, lb_logits=v_lb_logits,
             hgrn_o_gain=v_hgrn_o_gain, q_norm_gain=v_q_norm_gain, k_norm_gain=v_k_norm_gain, sinks=v_sinks,
             w_branch_a=v_w_branch_a, w_branch_b=v_w_branch_b, w_out=v_w_out, norm2_gain=v_norm2_gain,
             w_mlp_in=v_w_mlp_in, w_mlp_out=v_w_mlp_out)
    for d in (w, m, v):
        for nm in ("w_ada",) + BIG:
            d[nm] = d[nm][0]
    px, py, pc = _mesh_pos()
    me = _index((px, py, pc))
    c_arr = jnp.reshape(pc, (1,)).astype(jnp.int32)
    q_arr = jnp.reshape(2 * px + py, (1,)).astype(jnp.int32)

    shards = [_to_bf16("shard_" + nm, w[nm].T, False, w[nm].shape[1] // 4) if nm in AS_TRANSPOSE else
              _to_bf16("shard_" + nm, w[nm], nm in COLUMN_SHARDED) for nm in BIG]
    b_shard = lax.dynamic_slice(b_ada, (0, me * ADA_W), (1, ADA_W))
    mod, sc_all = _ada_mod(c, w["w_ada"], b_shard)

    grad, delta, new_m, new_v = {}, {}, {}, {}

    def update(nm, s, recvs, job=None):
        if nm in AS_TRANSPOSE:
            res = _sum_adamw("adamw_" + nm, s, recvs, q_arr, w[nm].T, m[nm].T, v[nm].T, False, w[nm].shape[1] // 4, job=job)
        else:
            res = _sum_adamw("adamw_" + nm, s, recvs, q_arr, w[nm], m[nm], v[nm], nm in COLUMN_SHARDED, 128, job=job)
        res, job_res = res if job is not None else (res, [])
        res = [t.T for t in res] if nm in AS_TRANSPOSE else res
        grad[nm], delta[nm], new_m[nm], new_v[nm] = res
        return job_res

    dx, sums, parts = _local_step(x[0], loss_target[0], mod, norm1_gain, norm2_gain, lb_logits, hgrn_o_gain,
                                  q_norm_gain, k_norm_gain, sinks, shards, c_arr, update)
    for nm in BIG:
        if nm not in grad:
            update(nm, *sums[nm])

    allx, g_small, loss = _small_reduce(parts, lb_logits)

    dmod_cols = lax.dynamic_slice(allx[:, 0, :], (0, me * ADA_W), (N_DEV, ADA_W))
    grad["w_ada"], delta["w_ada"], new_m["w_ada"], new_v["w_ada"] = _ada_update(
        sc_all.T, dmod_cols, w["w_ada"], m["w_ada"], v["w_ada"], 256)

    small_names = [nm for nm in WEIGHTS if nm not in BIG and nm != "w_ada"]
    shapes = {nm: w[nm].shape for nm in small_names}
    ds, ms, vs = _adamw("adamw_small", _pack_small(w), g_small, _pack_small(m), _pack_small(v), 1)
    for dst, vec in ((grad, g_small), (delta, ds), (new_m, ms), (new_v, vs)):
        dst.update(_unpack_small(vec, shapes))

    def full(d, nm):
        return d[nm][None] if nm in BIG or nm == "w_ada" else d[nm]

    return (loss[0, 0], dx[None], *[full(grad, nm) for nm in WEIGHTS], *[full(delta, nm) for nm in WEIGHTS],
            *[full(new_m, nm) for nm in WEIGHTS], *[full(new_v, nm) for nm in WEIGHTS])
```
